```python
import jax, jax.numpy as jnp
from jax import lax
import numpy as np

D_MODEL = 1024
BATCH = 8
SEQ = 4096
DEPTH = 2

D_MIX = D_MODEL
A_WIDTH = D_MIX // 4
A_GROUPS = 4
A_GROUP_DIM = A_WIDTH // A_GROUPS
A_CHUNK = 128
B_WIDTH = D_MIX // 4
B_EXPAND = 64
B_HEADS = B_WIDTH // B_EXPAND
B_KDIM = B_EXPAND
B_VDIM = B_WIDTH // B_HEADS
B_FDIM = B_HEADS * B_KDIM
B_CHUNK = 128
C_WIDTH = D_MIX - A_WIDTH - B_WIDTH
C_HEAD_DIM = 64
C_HEADS = C_WIDTH // C_HEAD_DIM
C_BLOCK = 128
COL_WIDTHS = (A_WIDTH, A_WIDTH, A_WIDTH,
              B_FDIM, B_FDIM, B_WIDTH, B_WIDTH,
              C_WIDTH, C_WIDTH, C_WIDTH, C_WIDTH, C_HEADS)
D_IN = 3 * A_WIDTH + 2 * B_FDIM + 2 * B_WIDTH + 4 * C_WIDTH + C_HEADS
NORM_EPS = 1e-6
F_FLOOR = 1e-30

kernel_name = "hybrid_gmlp_hgrn2_fox_parallel_heads"


def _rmsnorm(x, g):
    xf = x.astype(jnp.float32)
    y = xf * lax.rsqrt(jnp.mean(xf * xf, axis=-1, keepdims=True) + NORM_EPS)
    return (y * g.astype(jnp.float32)).astype(x.dtype)


def _split_cols(proj):
    offsets = []
    acc = 0
    for w in COL_WIDTHS[:-1]:
        acc += w
        offsets.append(acc)
    return jnp.split(proj, offsets, axis=-1)


def _gmlp_mixer(u, v, ln_g, ln_b, w_s, b_s):
    bsz, seq, _ = u.shape
    nc = seq // A_CHUNK
    u = jax.nn.gelu(u)
    v = jax.nn.gelu(v).reshape(bsz, nc, A_CHUNK, A_GROUPS, A_GROUP_DIM)
    vf = v.astype(jnp.float32)
    mu = jnp.mean(vf, axis=-1, keepdims=True)
    var = jnp.mean(jnp.square(vf - mu), axis=-1, keepdims=True)
    vn = (vf - mu) * lax.rsqrt(var + NORM_EPS) * ln_g.astype(jnp.float32) + ln_b.astype(jnp.float32)
    causal = jnp.tril(jnp.ones((A_CHUNK, A_CHUNK), dtype=bool))
    w = jnp.where(causal[None], w_s.astype(jnp.float32), 0.0)
    mixed = jnp.einsum('gts,bnsgc->bntgc', w, vn)
    mixed = mixed + jnp.transpose(b_s.astype(jnp.float32))[None, None, :, :, None]
    return u * mixed.reshape(bsz, seq, A_WIDTH).astype(u.dtype)


def _hgrn2_mixer(q, f_logit, i, lb, onorm_g):
    bsz, seq, _ = q.shape
    nc = seq // B_CHUNK
    qf = jax.nn.silu(q.astype(jnp.float32)) * (B_KDIM ** -0.5)
    z = f_logit.astype(jnp.float32)
    f = lb + (1.0 - lb) * jax.nn.sigmoid(z)
    log_f = jnp.log(jnp.maximum(f, F_FLOOR))
    kf = (1.0 - lb) * jax.nn.sigmoid(-z)
    vf = i.astype(jnp.float32)

    def to_chunks(t, d):
        return t.reshape(bsz, nc, B_CHUNK, B_HEADS, d).transpose(1, 0, 3, 2, 4)

    qc, kc, gc = to_chunks(qf, B_KDIM), to_chunks(kf, B_KDIM), to_chunks(log_f, B_KDIM)
    vc = to_chunks(vf, B_VDIM)
    causal = jnp.tril(jnp.ones((B_CHUNK, B_CHUNK), dtype=bool))[None, None, :, :, None]

    def step(state, inp):
        qx, kx, vx, gx = inp
        b = jnp.cumsum(gx, axis=2)
        o_inter = jnp.einsum('bhtk,bhkv->bhtv', qx * jnp.exp(b), state)
        diff = b[:, :, :, None, :] - b[:, :, None, :, :]
        decay = jnp.exp(jnp.where(causal, diff, -jnp.inf))
        scores = jnp.einsum('bhtk,bhsk,bhtsk->bhts', qx, kx, decay)
        o_intra = jnp.einsum('bhts,bhsv->bhtv', scores, vx)
        b_last = b[:, :, -1:, :]
        new_state = (jnp.exp(b_last[:, :, 0, :])[..., None] * state
                     + jnp.einsum('bhsk,bhsv->bhkv', kx * jnp.exp(b_last - b), vx))
        return new_state, o_inter + o_intra

    state0 = jnp.zeros((bsz, B_HEADS, B_KDIM, B_VDIM), jnp.float32)
    _, ys = lax.scan(step, state0, (qc, kc, vc, gc))
    o = ys.transpose(1, 0, 3, 2, 4).reshape(bsz, seq, B_HEADS, B_VDIM)
    o = o * lax.rsqrt(jnp.mean(o * o, axis=-1, keepdims=True) + NORM_EPS) * onorm_g.astype(jnp.float32)
    return o.reshape(bsz, seq, B_WIDTH).astype(q.dtype)


def _fox_mixer(q, k, v, f_logit, b_f):
    bsz, seq, _ = q.shape

    def heads(t):
        return t.reshape(bsz, seq, C_HEADS, C_HEAD_DIM).transpose(0, 2, 1, 3)

    qh, kh, vh = heads(q), heads(k), heads(v)
    log_f = jax.nn.log_sigmoid(f_logit.astype(jnp.float32) + b_f.astype(jnp.float32))
    c = jnp.cumsum(jnp.transpose(log_f, (0, 2, 1)), axis=-1)
    scale = C_HEAD_DIM ** -0.5
    diag_mask = jnp.tril(jnp.ones((C_BLOCK, C_BLOCK), dtype=bool))
    outs = []
    for blk in range(seq // C_BLOCK):
        q0 = blk * C_BLOCK
        q1 = q0 + C_BLOCK
        s = jnp.einsum('bhqd,bhkd->bhqk', qh[:, :, q0:q1], kh[:, :, :q1]).astype(jnp.float32) * scale
        s = s + c[:, :, q0:q1, None] - c[:, :, None, :q1]
        mask = jnp.concatenate([jnp.ones((C_BLOCK, q0), dtype=bool), diag_mask], axis=1)
        s = jnp.where(mask[None, None], s, -jnp.inf)
        p = jax.nn.softmax(s, axis=-1)
        outs.append(jnp.einsum('bhqk,bhkd->bhqd', p.astype(vh.dtype), vh[:, :, :q1]))
    o = jnp.concatenate(outs, axis=2)
    return o.transpose(0, 2, 1, 3).reshape(bsz, seq, C_WIDTH)


def _fwd_setup_inputs(seed: int = 0) -> dict:
    key = jax.random.key(seed)
    ks = jax.random.split(key, 13)
    f32 = jnp.float32
    x = jax.random.normal(ks[0], (BATCH, SEQ, D_MODEL), f32)
    norm_g = 1.0 + 0.05 * jax.random.normal(ks[1], (DEPTH, D_MODEL), f32)
    w_in = jax.random.normal(ks[2], (DEPTH, D_MODEL, D_IN), f32) * D_MODEL ** -0.5
    w_out = jax.random.normal(ks[3], (DEPTH, D_MIX, D_MODEL), f32) * (D_MIX ** -0.5) * (2 * DEPTH) ** -0.5
    gmlp_ln_g = 1.0 + 0.05 * jax.random.normal(ks[4], (DEPTH, A_GROUPS, A_GROUP_DIM), f32)
    gmlp_ln_b = 0.02 * jax.random.normal(ks[5], (DEPTH, A_GROUPS, A_GROUP_DIM), f32)
    gmlp_w_s = jax.random.normal(ks[6], (DEPTH, A_GROUPS, A_CHUNK, A_CHUNK), f32) * A_CHUNK ** -0.5
    gmlp_b_s = 1.0 + 0.1 * jax.random.normal(ks[7], (DEPTH, A_GROUPS, A_CHUNK), f32)
    hgrn_lb = 0.1 * jax.random.normal(ks[8], (DEPTH, B_FDIM), f32)
    hgrn_onorm_g = 1.0 + 0.05 * jax.random.normal(ks[9], (DEPTH, B_VDIM), f32)
    fox_b_f = jax.random.uniform(ks[10], (DEPTH, C_HEADS), f32, 0.0, 3.0)
    final_norm_g = 1.0 + 0.05 * jax.random.normal(ks[11], (D_MODEL,), f32)
    return {"x": x, "norm_g": norm_g, "w_in": w_in, "w_out": w_out,
            "gmlp_ln_g": gmlp_ln_g, "gmlp_ln_b": gmlp_ln_b, "gmlp_w_s": gmlp_w_s,
            "gmlp_b_s": gmlp_b_s, "hgrn_lb": hgrn_lb, "hgrn_onorm_g": hgrn_onorm_g,
            "fox_b_f": fox_b_f, "final_norm_g": final_norm_g}


def _fwd_reference(x, norm_g, w_in, w_out, gmlp_ln_g, gmlp_ln_b, gmlp_w_s, gmlp_b_s,
              hgrn_lb, hgrn_onorm_g, fox_b_f, final_norm_g):
    p = jax.nn.softmax(hgrn_lb.astype(jnp.float32), axis=0)
    lb_all = jnp.clip(jnp.cumsum(p, axis=0) - p[0:1], 0.0, 1.0 - 1e-6)
    for layer in range(DEPTH):
        h = _rmsnorm(x, norm_g[layer])
        proj = jnp.einsum('bsd,de->bse', h, w_in[layer])
        (a_u, a_v, a_z, b_q, b_fl, b_i, b_z,
         c_q, c_k, c_v, c_z, c_fl) = _split_cols(proj)
        y_a = _gmlp_mixer(a_u, a_v, gmlp_ln_g[layer], gmlp_ln_b[layer],
                          gmlp_w_s[layer], gmlp_b_s[layer]) * jax.nn.silu(a_z)
        y_b = _hgrn2_mixer(b_q, b_fl, b_i, lb_all[layer], hgrn_onorm_g[layer]) * jax.nn.silu(b_z)
        y_c = _fox_mixer(c_q, c_k, c_v, c_fl, fox_b_f[layer]) * jax.nn.silu(c_z)
        y = jnp.concatenate([y_a, y_b, y_c], axis=-1)
        x = x + jnp.einsum('bse,ed->bsd', y, w_out[layer])
    return _rmsnorm(x, final_norm_g)


import jax as _jax
import jax.numpy as _jnp

TWIN_FORMAT = 'train_step'
FWD_PARAMS = ['x', 'norm_g', 'w_in', 'w_out', 'gmlp_ln_g', 'gmlp_ln_b', 'gmlp_w_s', 'gmlp_b_s', 'hgrn_lb', 'hgrn_onorm_g', 'fox_b_f', 'final_norm_g']
TWIN_WEIGHTS = ['norm_g', 'w_in', 'w_out', 'gmlp_ln_g', 'gmlp_ln_b', 'gmlp_w_s', 'gmlp_b_s', 'hgrn_lb', 'hgrn_onorm_g', 'fox_b_f', 'final_norm_g']
TWIN_DIFF_INPUT = 'x'
TWIN_INPUTS = ['x', 'norm_g', 'w_in', 'w_out', 'gmlp_ln_g', 'gmlp_ln_b', 'gmlp_w_s', 'gmlp_b_s', 'hgrn_lb', 'hgrn_onorm_g', 'fox_b_f', 'final_norm_g', 'loss_target', 'm_norm_g', 'm_w_in', 'm_w_out', 'm_gmlp_ln_g', 'm_gmlp_ln_b', 'm_gmlp_w_s', 'm_gmlp_b_s', 'm_hgrn_lb', 'm_hgrn_onorm_g', 'm_fox_b_f', 'm_final_norm_g', 'v_norm_g', 'v_w_in', 'v_w_out', 'v_gmlp_ln_g', 'v_gmlp_ln_b', 'v_gmlp_w_s', 'v_gmlp_b_s', 'v_hgrn_lb', 'v_hgrn_onorm_g', 'v_fox_b_f', 'v_final_norm_g']
TWIN_OUTPUTS = ['loss', 'grad_x', 'grad_norm_g', 'grad_w_in', 'grad_w_out', 'grad_gmlp_ln_g', 'grad_gmlp_ln_b', 'grad_gmlp_w_s', 'grad_gmlp_b_s', 'grad_hgrn_lb', 'grad_hgrn_onorm_g', 'grad_fox_b_f', 'grad_final_norm_g', 'delta_norm_g', 'delta_w_in', 'delta_w_out', 'delta_gmlp_ln_g', 'delta_gmlp_ln_b', 'delta_gmlp_w_s', 'delta_gmlp_b_s', 'delta_hgrn_lb', 'delta_hgrn_onorm_g', 'delta_fox_b_f', 'delta_final_norm_g', 'new_m_norm_g', 'new_m_w_in', 'new_m_w_out', 'new_m_gmlp_ln_g', 'new_m_gmlp_ln_b', 'new_m_gmlp_w_s', 'new_m_gmlp_b_s', 'new_m_hgrn_lb', 'new_m_hgrn_onorm_g', 'new_m_fox_b_f', 'new_m_final_norm_g', 'new_v_norm_g', 'new_v_w_in', 'new_v_w_out', 'new_v_gmlp_ln_g', 'new_v_gmlp_ln_b', 'new_v_gmlp_w_s', 'new_v_gmlp_b_s', 'new_v_hgrn_lb', 'new_v_hgrn_onorm_g', 'new_v_fox_b_f', 'new_v_final_norm_g']
TWIN_LEAF_KINDS = {'loss': 'loss', 'grad_x': 'grad_x', 'grad_norm_g': 'grad_w', 'grad_w_in': 'grad_w', 'grad_w_out': 'grad_w', 'grad_gmlp_ln_g': 'grad_w', 'grad_gmlp_ln_b': 'grad_w', 'grad_gmlp_w_s': 'grad_w', 'grad_gmlp_b_s': 'grad_w', 'grad_hgrn_lb': 'grad_w', 'grad_hgrn_onorm_g': 'grad_w', 'grad_fox_b_f': 'grad_w', 'grad_final_norm_g': 'grad_w', 'delta_norm_g': 'delta_w', 'delta_w_in': 'delta_w', 'delta_w_out': 'delta_w', 'delta_gmlp_ln_g': 'delta_w', 'delta_gmlp_ln_b': 'delta_w', 'delta_gmlp_w_s': 'delta_w', 'delta_gmlp_b_s': 'delta_w', 'delta_hgrn_lb': 'delta_w', 'delta_hgrn_onorm_g': 'delta_w', 'delta_fox_b_f': 'delta_w', 'delta_final_norm_g': 'delta_w', 'new_m_norm_g': 'new_m', 'new_m_w_in': 'new_m', 'new_m_w_out': 'new_m', 'new_m_gmlp_ln_g': 'new_m', 'new_m_gmlp_ln_b': 'new_m', 'new_m_gmlp_w_s': 'new_m', 'new_m_gmlp_b_s': 'new_m', 'new_m_hgrn_lb': 'new_m', 'new_m_hgrn_onorm_g': 'new_m', 'new_m_fox_b_f': 'new_m', 'new_m_final_norm_g': 'new_m', 'new_v_norm_g': 'new_v', 'new_v_w_in': 'new_v', 'new_v_w_out': 'new_v', 'new_v_gmlp_ln_g': 'new_v', 'new_v_gmlp_ln_b': 'new_v', 'new_v_gmlp_w_s': 'new_v', 'new_v_gmlp_b_s': 'new_v', 'new_v_hgrn_lb': 'new_v', 'new_v_hgrn_onorm_g': 'new_v', 'new_v_fox_b_f': 'new_v', 'new_v_final_norm_g': 'new_v'}


def _forward(args):
    return _fwd_reference(*[args[k] for k in FWD_PARAMS])


def _output_shape():
    def fwd():
        inp = _fwd_setup_inputs(0)
        return _fwd_reference(*[inp[k] for k in FWD_PARAMS])
    out = _jax.eval_shape(fwd)
    return out.shape, out.dtype

N_MICROBATCH = 1
ADAM_LR = 0.001
ADAM_B1 = 0.9
ADAM_B2 = 0.999
ADAM_EPS = 1e-08
ADAM_WD = 0.01
ADAM_STEP = 10
PER_EXAMPLE_BATCH_AXIS = {'x': 0, 'loss_target': 0}
SHARED_INPUTS = []
_WEIGHT_DTYPES = {'norm_g': _jnp.float32, 'w_in': _jnp.float32, 'w_out': _jnp.float32, 'gmlp_ln_g': _jnp.float32, 'gmlp_ln_b': _jnp.float32, 'gmlp_w_s': _jnp.float32, 'gmlp_b_s': _jnp.float32, 'hgrn_lb': _jnp.float32, 'hgrn_onorm_g': _jnp.float32, 'fox_b_f': _jnp.float32, 'final_norm_g': _jnp.float32}
MOMENT_SCALE = {'norm_g': 5.944983e-02, 'w_in': 3.106874e-02, 'w_out': 7.612770e-02, 'gmlp_ln_g': 2.375198e-02, 'gmlp_ln_b': 2.635840e-02, 'gmlp_w_s': 1.713072e-02, 'gmlp_b_s': 2.452079e-02, 'hgrn_lb': 7.425219e-03, 'hgrn_onorm_g': 1.084526e-01, 'fox_b_f': 1.398347e-01, 'final_norm_g': 3.206456e+01}


def _to_microbatches(a, axis):
    t = _jnp.moveaxis(a, axis, 0)
    t = t.reshape((N_MICROBATCH, t.shape[0] // N_MICROBATCH) + t.shape[1:])
    return _jnp.moveaxis(t, 1, axis + 1)


def setup_inputs(seed: int = 0) -> dict:
    inp = _fwd_setup_inputs(seed)
    key = _jax.random.fold_in(_jax.random.key(seed), 7919)
    shape, _ = _output_shape()
    out = dict(inp)
    out["loss_target"] = _jax.random.normal(_jax.random.fold_in(key, 0), shape, _jnp.float32)
    for i, name in enumerate(TWIN_WEIGHTS):
        w = inp[name].astype(_jnp.float32)
        if MOMENT_SCALE is None:
            s = _jnp.sqrt(_jnp.mean(_jnp.square(w)) + 1e-30)
        else:
            s = MOMENT_SCALE[name]
        km, kv = _jax.random.split(_jax.random.fold_in(key, i + 1))
        out[name] = w
        out["m_" + name] = s * _jax.random.normal(km, w.shape, _jnp.float32)
        out["v_" + name] = (s * s) * _jax.random.uniform(kv, w.shape, _jnp.float32, 0.5, 1.5)
    if N_MICROBATCH > 1:
        for name, axis in PER_EXAMPLE_BATCH_AXIS.items():
            out[name] = _to_microbatches(out[name], axis)
    return {'x': out['x'], 'norm_g': out['norm_g'], 'w_in': out['w_in'], 'w_out': out['w_out'], 'gmlp_ln_g': out['gmlp_ln_g'], 'gmlp_ln_b': out['gmlp_ln_b'], 'gmlp_w_s': out['gmlp_w_s'], 'gmlp_b_s': out['gmlp_b_s'], 'hgrn_lb': out['hgrn_lb'], 'hgrn_onorm_g': out['hgrn_onorm_g'], 'fox_b_f': out['fox_b_f'], 'final_norm_g': out['final_norm_g'], 'loss_target': out['loss_target'], 'm_norm_g': out['m_norm_g'], 'm_w_in': out['m_w_in'], 'm_w_out': out['m_w_out'], 'm_gmlp_ln_g': out['m_gmlp_ln_g'], 'm_gmlp_ln_b': out['m_gmlp_ln_b'], 'm_gmlp_w_s': out['m_gmlp_w_s'], 'm_gmlp_b_s': out['m_gmlp_b_s'], 'm_hgrn_lb': out['m_hgrn_lb'], 'm_hgrn_onorm_g': out['m_hgrn_onorm_g'], 'm_fox_b_f': out['m_fox_b_f'], 'm_final_norm_g': out['m_final_norm_g'], 'v_norm_g': out['v_norm_g'], 'v_w_in': out['v_w_in'], 'v_w_out': out['v_w_out'], 'v_gmlp_ln_g': out['v_gmlp_ln_g'], 'v_gmlp_ln_b': out['v_gmlp_ln_b'], 'v_gmlp_w_s': out['v_gmlp_w_s'], 'v_gmlp_b_s': out['v_gmlp_b_s'], 'v_hgrn_lb': out['v_hgrn_lb'], 'v_hgrn_onorm_g': out['v_hgrn_onorm_g'], 'v_fox_b_f': out['v_fox_b_f'], 'v_final_norm_g': out['v_final_norm_g']}


def _loss(weights, diff, rest, loss_target):
    with _jax.named_scope("forward"):
        args = {**rest, TWIN_DIFF_INPUT: diff, **{k: w.astype(_WEIGHT_DTYPES[k]) for k, w in weights.items()}}
        y = _forward(args)
    with _jax.named_scope("loss_head"):
        err = _jnp.square(y.astype(_jnp.float32) - loss_target)
        return 0.5 * _jnp.sum(_jnp.mean(err, axis=-1)) if err.ndim else 0.5 * err


def _adamw(w, g, m, v):
    m = ADAM_B1 * m + (1.0 - ADAM_B1) * g
    v = ADAM_B2 * v + (1.0 - ADAM_B2) * _jnp.square(g)
    m_hat = m / (1.0 - ADAM_B1 ** ADAM_STEP)
    v_hat = v / (1.0 - ADAM_B2 ** ADAM_STEP)
    delta = -ADAM_LR * (m_hat / (_jnp.sqrt(v_hat) + ADAM_EPS) + ADAM_WD * w)
    return delta, m, v


def reference(x, norm_g, w_in, w_out, gmlp_ln_g, gmlp_ln_b, gmlp_w_s, gmlp_b_s, hgrn_lb, hgrn_onorm_g, fox_b_f, final_norm_g, loss_target, m_norm_g, m_w_in, m_w_out, m_gmlp_ln_g, m_gmlp_ln_b, m_gmlp_w_s, m_gmlp_b_s, m_hgrn_lb, m_hgrn_onorm_g, m_fox_b_f, m_final_norm_g, v_norm_g, v_w_in, v_w_out, v_gmlp_ln_g, v_gmlp_ln_b, v_gmlp_w_s, v_gmlp_b_s, v_hgrn_lb, v_hgrn_onorm_g, v_fox_b_f, v_final_norm_g):
    given = dict(x=x, norm_g=norm_g, w_in=w_in, w_out=w_out, gmlp_ln_g=gmlp_ln_g, gmlp_ln_b=gmlp_ln_b, gmlp_w_s=gmlp_w_s, gmlp_b_s=gmlp_b_s, hgrn_lb=hgrn_lb, hgrn_onorm_g=hgrn_onorm_g, fox_b_f=fox_b_f, final_norm_g=final_norm_g, loss_target=loss_target, m_norm_g=m_norm_g, m_w_in=m_w_in, m_w_out=m_w_out, m_gmlp_ln_g=m_gmlp_ln_g, m_gmlp_ln_b=m_gmlp_ln_b, m_gmlp_w_s=m_gmlp_w_s, m_gmlp_b_s=m_gmlp_b_s, m_hgrn_lb=m_hgrn_lb, m_hgrn_onorm_g=m_hgrn_onorm_g, m_fox_b_f=m_fox_b_f, m_final_norm_g=m_final_norm_g, v_norm_g=v_norm_g, v_w_in=v_w_in, v_w_out=v_w_out, v_gmlp_ln_g=v_gmlp_ln_g, v_gmlp_ln_b=v_gmlp_ln_b, v_gmlp_w_s=v_gmlp_w_s, v_gmlp_b_s=v_gmlp_b_s, v_hgrn_lb=v_hgrn_lb, v_hgrn_onorm_g=v_hgrn_onorm_g, v_fox_b_f=v_fox_b_f, v_final_norm_g=v_final_norm_g)
    weights = {n: given[n] for n in TWIN_WEIGHTS}
    shared = {n: given[n] for n in SHARED_INPUTS}
    per_example = {n: given[n] for n in ['x']}
    grad_fn = _jax.value_and_grad(_loss, argnums=(0, 1))

    def one_microbatch(ex, loss_target):
        ex = dict(ex)
        diff = ex.pop(TWIN_DIFF_INPUT)
        return grad_fn(weights, diff, {**shared, **ex}, loss_target)

    if N_MICROBATCH == 1:
        loss, (grad_w, grad_x) = one_microbatch(per_example, given["loss_target"])
    else:
        def body(carry, xs):
            loss_sum, grad_sum = carry
            l_k, (gw_k, gx_k) = one_microbatch(xs[0], xs[1])
            with _jax.named_scope("update"):
                return (loss_sum + l_k, _jax.tree.map(_jnp.add, grad_sum, gw_k)), gx_k

        init = (_jnp.zeros((), _jnp.float32), _jax.tree.map(_jnp.zeros_like, weights))
        (loss, grad_w), grad_x = _jax.lax.scan(body, init, (per_example, given["loss_target"]))
    with _jax.named_scope("update"):
        delta_w, new_m, new_v = {}, {}, {}
        for n in TWIN_WEIGHTS:
            delta_w[n], new_m[n], new_v[n] = _adamw(weights[n], grad_w[n], given["m_" + n], given["v_" + n])
    return (loss, grad_x, *[grad_w[n] for n in TWIN_WEIGHTS], *[delta_w[n] for n in TWIN_WEIGHTS],
            *[new_m[n] for n in TWIN_WEIGHTS], *[new_v[n] for n in TWIN_WEIGHTS])
```

```python
import functools

import jax
import jax.numpy as jnp
import numpy as np
from jax import lax
from jax.experimental import pallas as pl
from jax.experimental.pallas import tpu as pltpu

F32 = jnp.float32
BF16 = jnp.bfloat16
HI = lax.Precision.HIGHEST

NORM_EPS = 1e-6
F_FLOOR = 1e-30
CHUNK = 128
LANES = 128
VMEM_LIMIT = 56 * 1024 * 1024


def _cparams(*sem):
    return pltpu.CompilerParams(dimension_semantics=sem, vmem_limit_bytes=VMEM_LIMIT)


def _dot(a, b, dims=(((1,), (0,)), ((), ())), precision=None):
    return lax.dot_general(a, b, dims, precision=precision, preferred_element_type=F32)


_NT = (((1,), (1,)), ((), ()))
_TN = (((0,), (0,)), ((), ()))


def _bd(a, b):
    return _dot(a.astype(BF16), b.astype(BF16))


def _group_mean_matrix(width, group):
    idx = np.arange(width) // group
    return jnp.asarray((idx[:, None] == idx[None, :]).astype(np.float32) / group)


def _group_ones_matrix(width, group):
    idx = np.arange(width) // group
    return jnp.asarray((idx[:, None] == idx[None, :]).astype(np.float32))


A_WIDTH = 256
A_GROUPS = 4
A_GDIM = 64


def _gmlp_chunk(x3, ln_g, ln_b, w_s, bs_t, mean_m, gind):
    u = jax.nn.gelu(x3[:, :A_WIDTH])
    v = jax.nn.gelu(x3[:, A_WIDTH:2 * A_WIDTH])
    z = x3[:, 2 * A_WIDTH:]
    mu = _dot(v, mean_m, precision=HI)
    d = v - mu
    var = _dot(d * d, mean_m, precision=HI)
    vn = d * lax.rsqrt(var + NORM_EPS) * ln_g + ln_b
    vnb = vn.astype(BF16)
    row = lax.broadcasted_iota(jnp.int32, (CHUNK, CHUNK), 0)
    col = lax.broadcasted_iota(jnp.int32, (CHUNK, CHUNK), 1)
    causal = row >= col
    lane_g = lax.shift_right_logical(lax.broadcasted_iota(jnp.int32, (CHUNK, A_WIDTH), 1), 6)
    mixed = _dot(bs_t, gind, precision=HI)
    for g in range(A_GROUPS):
        wc = jnp.where(causal, w_s[g], 0.0).astype(BF16)
        mixed = mixed + jnp.where(lane_g == g, _dot(wc, vnb), 0.0)
    return u * mixed * jax.nn.silu(z)


def _gmlp_consts():
    gind = np.zeros((LANES, A_WIDTH), np.float32)
    for g in range(A_GROUPS):
        gind[g, g * A_GDIM:(g + 1) * A_GDIM] = 1.0
    return _group_mean_matrix(A_WIDTH, A_GDIM), jnp.asarray(gind)


def _full(shape):
    return pl.BlockSpec(shape, lambda *_: (0,) * len(shape))


def gmlp_fwd(proj, ln_g, ln_b, w_s, bs_t):
    seq = proj.shape[0]
    mean_m, gind = _gmlp_consts()

    def body(x_ref, g_ref, b_ref, w_ref, bs_ref, m_ref, gi_ref, y_ref):
        y = _gmlp_chunk(x_ref[...], g_ref[...], b_ref[...], w_ref[...], bs_ref[...], m_ref[...], gi_ref[...])
        y_ref[...] = y.astype(BF16)

    return pl.pallas_call(
        body,
        name="gmlp_fwd",
        grid=(seq // CHUNK,),
        in_specs=[
            pl.BlockSpec((CHUNK, 3 * A_WIDTH), lambda n: (n, 0)),
            _full((1, A_WIDTH)), _full((1, A_WIDTH)), _full((A_GROUPS, CHUNK, CHUNK)), _full((CHUNK, LANES)),
            _full((A_WIDTH, A_WIDTH)), _full((LANES, A_WIDTH)),
        ],
        out_specs=pl.BlockSpec((CHUNK, A_WIDTH), lambda n: (n, 0)),
        out_shape=jax.ShapeDtypeStruct((seq, A_WIDTH), BF16),
        compiler_params=_cparams("parallel"),
    )(proj, ln_g, ln_b, w_s, bs_t, mean_m, gind)


def gmlp_bwd(proj, dy, ln_g, ln_b, w_s, bs_t):
    seq = proj.shape[0]
    mean_m, gind = _gmlp_consts()

    def body(x_ref, dy_ref, g_ref, b_ref, w_ref, bs_ref, m_ref, gi_ref, dx_ref, dg_ref, db_ref, dw_ref, dbs_ref):
        fn = functools.partial(_gmlp_chunk, mean_m=m_ref[...], gind=gi_ref[...])
        _, vjp = jax.vjp(fn, x_ref[...], g_ref[...], b_ref[...], w_ref[...], bs_ref[...])
        dx, dg, db, dw, dbs = vjp(dy_ref[...])
        dx_ref[...] = dx.astype(BF16)

        @pl.when(pl.program_id(0) == 0)
        def _():
            dg_ref[...] = jnp.zeros_like(dg_ref)
            db_ref[...] = jnp.zeros_like(db_ref)
            dw_ref[...] = jnp.zeros_like(dw_ref)
            dbs_ref[...] = jnp.zeros_like(dbs_ref)

        dg_ref[...] += dg
        db_ref[...] += db
        dw_ref[...] += dw
        dbs_ref[...] += dbs

    return pl.pallas_call(
        body,
        name="gmlp_bwd",
        grid=(seq // CHUNK,),
        in_specs=[
            pl.BlockSpec((CHUNK, 3 * A_WIDTH), lambda n: (n, 0)),
            pl.BlockSpec((CHUNK, A_WIDTH), lambda n: (n, 0)),
            _full((1, A_WIDTH)), _full((1, A_WIDTH)), _full((A_GROUPS, CHUNK, CHUNK)), _full((CHUNK, LANES)),
            _full((A_WIDTH, A_WIDTH)), _full((LANES, A_WIDTH)),
        ],
        out_specs=[
            pl.BlockSpec((CHUNK, 3 * A_WIDTH), lambda n: (n, 0)),
            _full((1, A_WIDTH)), _full((1, A_WIDTH)), _full((A_GROUPS, CHUNK, CHUNK)), _full((CHUNK, LANES)),
        ],
        out_shape=[
            jax.ShapeDtypeStruct((seq, 3 * A_WIDTH), BF16),
            jax.ShapeDtypeStruct((1, A_WIDTH), F32), jax.ShapeDtypeStruct((1, A_WIDTH), F32),
            jax.ShapeDtypeStruct((A_GROUPS, CHUNK, CHUNK), F32), jax.ShapeDtypeStruct((CHUNK, LANES), F32),
        ],
        compiler_params=_cparams("arbitrary"),
    )(proj, dy, ln_g, ln_b, w_s, bs_t, mean_m, gind)


B_WIDTH = 256
B_HEADS = 4
B_KDIM = 64
B_LEVELS = (64, 32, 16, 8, 4, 2, 1)


def _hgrn_consts():
    t = np.arange(CHUNK)
    u = t[None, :]
    mats = [np.tril(np.ones((CHUNK, CHUNK), np.float32))]
    for m in B_LEVELS:
        p = (t // (2 * m)) * (2 * m) + m - 1
        right = (t % (2 * m)) >= m
        sel = np.where(right[:, None], (u > p[:, None]) & (u <= t[:, None]), (u > t[:, None]) & (u <= p[:, None]))
        mats.append(sel.astype(np.float32))
    return jnp.asarray(np.concatenate(mats, 0)), _group_ones_matrix(B_WIDTH, B_KDIM)


def _hgrn_lower_bound(lb0, lb1, layer):
    mx = jnp.maximum(lb0, lb1)
    e0 = jnp.exp(lb0 - mx)
    e1 = jnp.exp(lb1 - mx)
    p0 = e0 / (e0 + e1)
    p1 = e1 / (e0 + e1)
    cs = p0 if layer == 0 else p0 + p1
    return jnp.clip(cs - p0, 0.0, 1.0 - 1e-6)


def _hgrn_chunk(x4, st, lb0, lb1, onorm, layer, tstack, ones_bd):
    q_raw, fl, v, zg = (x4[:, i * B_WIDTH:(i + 1) * B_WIDTH] for i in range(4))
    lb = _hgrn_lower_bound(lb0, lb1, layer)
    q = jax.nn.silu(q_raw) * (B_KDIM ** -0.5)
    f = lb + (1.0 - lb) * jax.nn.sigmoid(fl)
    logf = jnp.log(jnp.maximum(f, F_FLOOR))
    k = (1.0 - lb) * jax.nn.sigmoid(-fl)
    dall = _dot(tstack, logf, precision=HI)
    b = dall[:CHUNK]
    b_last = jnp.sum(logf, axis=0, keepdims=True)
    vb = v.astype(BF16)

    lane_h = lax.shift_right_logical(lax.broadcasted_iota(jnp.int32, (CHUNK, B_WIDTH), 1), 6)
    row = lax.broadcasted_iota(jnp.int32, (CHUNK, B_WIDTH), 0)
    srow = lax.broadcasted_iota(jnp.int32, (B_HEADS * CHUNK, CHUNK), 0) & (CHUNK - 1)
    scol = lax.broadcasted_iota(jnp.int32, (B_HEADS * CHUNK, CHUNK), 1)

    def heads_on_rows(a):
        return jnp.concatenate([jnp.where(lane_h == h, a, 0.0) for h in range(B_HEADS)], axis=0)

    def heads_from_rows(r):
        out = jnp.where(lane_h == 0, r[:CHUNK], 0.0)
        for h in range(1, B_HEADS):
            out = out + jnp.where(lane_h == h, r[h * CHUNK:(h + 1) * CHUNK], 0.0)
        return out

    o = lax.dot_general((q * jnp.exp(b)).astype(BF16), st.astype(BF16), _NT, preferred_element_type=F32)
    scores = jnp.zeros((B_HEADS * CHUNK, CHUNK), F32)
    for li, m in enumerate(B_LEVELS):
        e = jnp.exp(dall[(li + 1) * CHUNK:(li + 2) * CHUNK])
        right = (row & (2 * m - 1)) >= m
        qt = jnp.where(right, q * e, 0.0)
        kt = jnp.where(right, 0.0, k * e)
        sc = lax.dot_general(heads_on_rows(qt).astype(BF16), kt.astype(BF16), _NT, preferred_element_type=F32)
        sh = int(np.log2(2 * m))
        same = lax.shift_right_logical(srow, sh) == lax.shift_right_logical(scol, sh)
        scores = scores + jnp.where(same, sc, 0.0)
    o = o + heads_from_rows(_dot(scores.astype(BF16), vb))
    o = o + _dot(q * k, ones_bd, precision=HI) * v

    kv = lax.dot_general(vb, (k * jnp.exp(b_last - b)).astype(BF16), _TN, preferred_element_type=F32)
    st_new = st * jnp.exp(b_last) + jnp.where(ones_bd > 0.5, kv, 0.0)

    ms = _dot(o * o, ones_bd, precision=HI) * (1.0 / B_KDIM)
    y = o * lax.rsqrt(ms + NORM_EPS) * onorm * jax.nn.silu(zg)
    return y, st_new


def hgrn_fwd(proj, lb0, lb1, onorm, layer):
    seq = proj.shape[0]
    nc = seq // CHUNK
    tstack, ones_bd = _hgrn_consts()

    def body(x_ref, lb0_ref, lb1_ref, on_ref, t_ref, e_ref, y_ref, st_out_ref, st_ref):
        @pl.when(pl.program_id(0) == 0)
        def _():
            st_ref[...] = jnp.zeros_like(st_ref)

        st = st_ref[...]
        st_out_ref[0] = st
        y, st_new = _hgrn_chunk(x_ref[...], st, lb0_ref[...], lb1_ref[...], on_ref[...], layer, t_ref[...], e_ref[...])
        y_ref[...] = y.astype(BF16)
        st_ref[...] = st_new

    return pl.pallas_call(
        body,
        name=f"hgrn_fwd_{layer}",
        grid=(nc,),
        in_specs=[
            pl.BlockSpec((CHUNK, 4 * B_WIDTH), lambda n: (n, 1)),
            _full((1, B_WIDTH)), _full((1, B_WIDTH)), _full((1, B_WIDTH)),
            _full(((len(B_LEVELS) + 1) * CHUNK, CHUNK)), _full((B_WIDTH, B_WIDTH)),
        ],
        out_specs=[
            pl.BlockSpec((CHUNK, B_WIDTH), lambda n: (n, 0)),
            pl.BlockSpec((1, B_WIDTH, B_WIDTH), lambda n: (n, 0, 0)),
        ],
        out_shape=[jax.ShapeDtypeStruct((seq, B_WIDTH), BF16), jax.ShapeDtypeStruct((nc, B_WIDTH, B_WIDTH), F32)],
        scratch_shapes=[pltpu.VMEM((B_WIDTH, B_WIDTH), F32)],
        compiler_params=_cparams("arbitrary"),
    )(proj, lb0, lb1, onorm, tstack, ones_bd)


def hgrn_bwd(proj, states, dy, lb0, lb1, onorm, layer):
    seq = proj.shape[0]
    nc = seq // CHUNK
    tstack, ones_bd = _hgrn_consts()

    def body(x_ref, st_in_ref, dy_ref, lb0_ref, lb1_ref, on_ref, t_ref, e_ref, dx_ref, d0_ref, d1_ref, don_ref, dst_ref):
        @pl.when(pl.program_id(0) == 0)
        def _():
            dst_ref[...] = jnp.zeros_like(dst_ref)
            d0_ref[...] = jnp.zeros_like(d0_ref)
            d1_ref[...] = jnp.zeros_like(d1_ref)
            don_ref[...] = jnp.zeros_like(don_ref)

        fn = functools.partial(_hgrn_chunk, layer=layer, tstack=t_ref[...], ones_bd=e_ref[...])
        _, vjp = jax.vjp(fn, x_ref[...], st_in_ref[0], lb0_ref[...], lb1_ref[...], on_ref[...])
        dx, dst, d0, d1, don = vjp((dy_ref[...], dst_ref[...]))
        dx_ref[...] = dx.astype(BF16)
        dst_ref[...] = dst
        d0_ref[...] += d0
        d1_ref[...] += d1
        don_ref[...] += don

    rev = lambda n: nc - 1 - n
    return pl.pallas_call(
        body,
        name=f"hgrn_bwd_{layer}",
        grid=(nc,),
        in_specs=[
            pl.BlockSpec((CHUNK, 4 * B_WIDTH), lambda n: (rev(n), 1)),
            pl.BlockSpec((1, B_WIDTH, B_WIDTH), lambda n: (rev(n), 0, 0)),
            pl.BlockSpec((CHUNK, B_WIDTH), lambda n: (rev(n), 1)),
            _full((1, B_WIDTH)), _full((1, B_WIDTH)), _full((1, B_WIDTH)),
            _full(((len(B_LEVELS) + 1) * CHUNK, CHUNK)), _full((B_WIDTH, B_WIDTH)),
        ],
        out_specs=[
            pl.BlockSpec((CHUNK, 4 * B_WIDTH), lambda n: (rev(n), 0)),
            _full((1, B_WIDTH)), _full((1, B_WIDTH)), _full((1, B_WIDTH)),
        ],
        out_shape=[jax.ShapeDtypeStruct((seq, 4 * B_WIDTH), BF16)] + [jax.ShapeDtypeStruct((1, B_WIDTH), F32)] * 3,
        scratch_shapes=[pltpu.VMEM((B_WIDTH, B_WIDTH), F32)],
        compiler_params=_cparams("arbitrary"),
    )(proj, states, dy, lb0, lb1, onorm, tstack, ones_bd)


D_MODEL = 1024
D_INT = 4096


def _rms_stats(xf):
    r = lax.rsqrt(jnp.mean(xf * xf, axis=-1, keepdims=True) + NORM_EPS)
    return r, xf * r


def _rms_bwd(dy, g, r, xh):
    u = dy * g
    return r * (u - xh * jnp.mean(u * xh, axis=-1, keepdims=True))


def inproj(x, g, w):
    seq = x.shape[0]
    tm, tn = min(seq, 1024), 512

    def body(x_ref, g_ref, w_ref, p_ref, h_ref):
        @pl.when(pl.program_id(1) == 0)
        def _():
            _, xh = _rms_stats(x_ref[...])
            h_ref[...] = (xh * g_ref[...]).astype(BF16)

        p_ref[...] = _dot(h_ref[...], w_ref[...])

    return pl.pallas_call(
        body,
        name="inproj",
        grid=(seq // tm, D_INT // tn),
        in_specs=[
            pl.BlockSpec((tm, D_MODEL), lambda i, j: (i, 0)),
            _full((1, D_MODEL)),
            pl.BlockSpec((D_MODEL, tn), lambda i, j: (0, j)),
        ],
        out_specs=[pl.BlockSpec((tm, tn), lambda i, j: (i, j)), pl.BlockSpec((tm, D_MODEL), lambda i, j: (i, 0))],
        out_shape=[jax.ShapeDtypeStruct((seq, D_INT), F32), jax.ShapeDtypeStruct((seq, D_MODEL), BF16)],
        compiler_params=_cparams("parallel", "arbitrary"),
    )(x, g, w)


def outproj(x, ya, yb, o, proj, wo):
    seq = x.shape[0]
    tm = min(seq, 512)

    def body(x_ref, ya_ref, yb_ref, o_ref, z_ref, w_ref, xn_ref, y_ref):
        yc = (o_ref[...] * jax.nn.silu(z_ref[...])).astype(BF16)
        y = jnp.concatenate([ya_ref[...], yb_ref[...], yc], axis=1)
        y_ref[...] = y
        xn_ref[...] = x_ref[...] + _dot(y, w_ref[...])

    return pl.pallas_call(
        body,
        name="outproj",
        grid=(seq // tm,),
        in_specs=[
            pl.BlockSpec((tm, D_MODEL), lambda i: (i, 0)),
            pl.BlockSpec((tm, 256), lambda i: (i, 0)),
            pl.BlockSpec((tm, 256), lambda i: (i, 0)),
            pl.BlockSpec((tm, 512), lambda i: (i, 0)),
            pl.BlockSpec((tm, 512), lambda i: (i, 7)),
            _full((D_MODEL, D_MODEL)),
        ],
        out_specs=[pl.BlockSpec((tm, D_MODEL), lambda i: (i, 0)), pl.BlockSpec((tm, D_MODEL), lambda i: (i, 0))],
        out_shape=[jax.ShapeDtypeStruct((seq, D_MODEL), F32), jax.ShapeDtypeStruct((seq, D_MODEL), BF16)],
        compiler_params=_cparams("parallel"),
    )(x, ya, yb, o, proj, wo)


def outproj_bwd(dx, y, wo):
    seq = dx.shape[0]
    ts = min(seq, 512)

    def body(dx_ref, y_ref, w_ref, dy_ref, dw_ref):
        @pl.when(pl.program_id(0) == 0)
        def _():
            dw_ref[...] = jnp.zeros_like(dw_ref)

        dxb = dx_ref[...].astype(BF16)
        dy_ref[...] = lax.dot_general(dxb, w_ref[...], _NT, preferred_element_type=F32)
        dw_ref[...] += lax.dot_general(y_ref[...], dxb, _TN, preferred_element_type=F32)

    return pl.pallas_call(
        body,
        name="outproj_bwd",
        grid=(seq // ts,),
        in_specs=[
            pl.BlockSpec((ts, D_MODEL), lambda i: (i, 0)),
            pl.BlockSpec((ts, D_MODEL), lambda i: (i, 0)),
            _full((D_MODEL, D_MODEL)),
        ],
        out_specs=[pl.BlockSpec((ts, D_MODEL), lambda i: (i, 0)), _full((D_MODEL, D_MODEL))],
        out_shape=[jax.ShapeDtypeStruct((seq, D_MODEL), F32), jax.ShapeDtypeStruct((D_MODEL, D_MODEL), F32)],
        compiler_params=_cparams("arbitrary"),
    )(dx, y, wo)


def inproj_bwd_x(dproj, w, x, g, dx_in):
    seq = x.shape[0]
    tm, tk = min(seq, 512), 1024
    nk = D_INT // tk

    def body(dp_ref, w_ref, x_ref, g_ref, dxin_ref, dx_ref, dg_ref, acc_ref):
        k = pl.program_id(1)

        @pl.when(k == 0)
        def _():
            acc_ref[...] = jnp.zeros_like(acc_ref)

        acc_ref[...] += lax.dot_general(dp_ref[...], w_ref[...], _NT, preferred_element_type=F32)

        @pl.when(k == nk - 1)
        def _():
            @pl.when(pl.program_id(0) == 0)
            def _():
                dg_ref[...] = jnp.zeros_like(dg_ref)

            dh = acc_ref[...]
            g = g_ref[...]
            r, xh = _rms_stats(x_ref[...])
            dg_ref[...] += jnp.sum(dh * xh, axis=0, keepdims=True)
            dx_ref[...] = dxin_ref[...] + _rms_bwd(dh, g, r, xh)

    return pl.pallas_call(
        body,
        name="inproj_bwd_x",
        grid=(seq // tm, nk),
        in_specs=[
            pl.BlockSpec((tm, tk), lambda i, k: (i, k)),
            pl.BlockSpec((D_MODEL, tk), lambda i, k: (0, k)),
            pl.BlockSpec((tm, D_MODEL), lambda i, k: (i, 0)),
            _full((1, D_MODEL)),
            pl.BlockSpec((tm, D_MODEL), lambda i, k: (i, 0)),
        ],
        out_specs=[pl.BlockSpec((tm, D_MODEL), lambda i, k: (i, 0)), _full((1, D_MODEL))],
        out_shape=[jax.ShapeDtypeStruct((seq, D_MODEL), F32), jax.ShapeDtypeStruct((1, D_MODEL), F32)],
        scratch_shapes=[pltpu.VMEM((tm, D_MODEL), F32)],
        compiler_params=_cparams("arbitrary", "arbitrary"),
    )(dproj, w, x, g, dx_in)


def inproj_bwd_w(h, dproj):
    seq = h.shape[0]
    ts, tn = min(seq, 1024), 512

    def body(h_ref, dp_ref, dw_ref):
        @pl.when(pl.program_id(1) == 0)
        def _():
            dw_ref[...] = jnp.zeros_like(dw_ref)

        dw_ref[...] += lax.dot_general(h_ref[...], dp_ref[...], _TN, preferred_element_type=F32)

    return pl.pallas_call(
        body,
        name="inproj_bwd_w",
        grid=(D_INT // tn, seq // ts),
        in_specs=[pl.BlockSpec((ts, D_MODEL), lambda j, s: (s, 0)), pl.BlockSpec((ts, tn), lambda j, s: (s, j))],
        out_specs=pl.BlockSpec((D_MODEL, tn), lambda j, s: (0, j)),
        out_shape=jax.ShapeDtypeStruct((D_MODEL, D_INT), F32),
        compiler_params=_cparams("parallel", "arbitrary"),
    )(h, dproj)


def final_loss(x, g, tgt):
    seq = x.shape[0]
    tm = min(seq, 512)

    def body(x_ref, g_ref, t_ref, dx_ref, dg_ref, loss_ref):
        @pl.when(pl.program_id(0) == 0)
        def _():
            dg_ref[...] = jnp.zeros_like(dg_ref)
            loss_ref[...] = jnp.zeros_like(loss_ref)

        g = g_ref[...]
        r, xh = _rms_stats(x_ref[...])
        err = xh * g - t_ref[...]
        sq = jnp.sum(jnp.sum(err * err, axis=1, keepdims=True), axis=0, keepdims=True)
        loss_ref[...] += jnp.broadcast_to(sq * (0.5 / D_MODEL), loss_ref.shape)
        dout = err * (1.0 / D_MODEL)
        dg_ref[...] += jnp.sum(dout * xh, axis=0, keepdims=True)
        dx_ref[...] = _rms_bwd(dout, g, r, xh)

    return pl.pallas_call(
        body,
        name="final_loss",
        grid=(seq // tm,),
        in_specs=[pl.BlockSpec((tm, D_MODEL), lambda i: (i, 0)), _full((1, D_MODEL)), pl.BlockSpec((tm, D_MODEL), lambda i: (i, 0))],
        out_specs=[pl.BlockSpec((tm, D_MODEL), lambda i: (i, 0)), _full((1, D_MODEL)), _full((8, LANES))],
        out_shape=[jax.ShapeDtypeStruct((seq, D_MODEL), F32), jax.ShapeDtypeStruct((1, D_MODEL), F32), jax.ShapeDtypeStruct((8, LANES), F32)],
        compiler_params=_cparams("arbitrary"),
    )(x, g, tgt)


C_WIDTH = 512
C_HEADS = 8
C_HDIM = 64
C_PAIRS = C_HEADS // 2
C_BQ = 256


def fox_prep(proj, bf_row):
    seq = proj.shape[0]
    tril = jnp.asarray(np.tril(np.ones((CHUNK, CHUNK), np.float32)))

    def body(fl_ref, q_ref, k_ref, v_ref, bf_ref, l_ref, qb_ref, kb_ref, vb_ref, ct_ref, carry_ref):
        @pl.when(pl.program_id(0) == 0)
        def _():
            carry_ref[...] = jnp.zeros_like(carry_ref)

        lf = jax.nn.log_sigmoid(fl_ref[:, :LANES] + bf_ref[...])
        c = _dot(l_ref[...], lf, precision=HI) + carry_ref[...]
        carry_ref[...] += jnp.sum(lf, axis=0, keepdims=True)
        ct_ref[...] = c.T[:C_HEADS, :]
        qb_ref[...] = (q_ref[...] * (C_HDIM ** -0.5)).astype(BF16)
        kb_ref[...] = k_ref[...].astype(BF16)
        vb_ref[...] = v_ref[...].astype(BF16)

    wide = lambda j: pl.BlockSpec((CHUNK, C_WIDTH), lambda n: (n, j))
    return pl.pallas_call(
        body,
        name="fox_prep",
        grid=(seq // CHUNK,),
        in_specs=[pl.BlockSpec((CHUNK, 256), lambda n: (n, 3)), wide(4), wide(5), wide(6), _full((1, LANES)), _full((CHUNK, CHUNK))],
        out_specs=[wide(0), wide(0), wide(0), pl.BlockSpec((C_HEADS, CHUNK), lambda n: (0, n))],
        out_shape=[jax.ShapeDtypeStruct((seq, C_WIDTH), BF16)] * 3 + [jax.ShapeDtypeStruct((C_HEADS, seq), F32)],
        scratch_shapes=[pltpu.VMEM((1, LANES), F32)],
        compiler_params=_cparams("arbitrary"),
    )(proj, proj, proj, proj, bf_row, tril)


def _pair_blockdiag(blk):
    lo = lax.broadcasted_iota(jnp.int32, blk.shape, 1) < C_HDIM
    zero = jnp.zeros_like(blk)
    return jnp.concatenate([jnp.where(lo, blk, zero), jnp.where(lo, zero, blk)], axis=0)


def _pair_rows(c_ref, off):
    return jnp.concatenate([c_ref[0, 0:1, pl.ds(off, CHUNK)], c_ref[0, 1:2, pl.ds(off, CHUNK)]], axis=1)


def _causal(shape, row0, col0):
    row = row0 + lax.broadcasted_iota(jnp.int32, shape, 0)
    col = col0 + (lax.broadcasted_iota(jnp.int32, shape, 1) & (CHUNK - 1))
    return col <= row


def fox_fwd(qb, kb, vb, ct):
    seq = qb.shape[0]
    bq = min(C_BQ, seq)
    ndiag = bq // CHUNK

    def body(q_ref, k_ref, v_ref, c_ref, o_ref, lse_ref):
        i = pl.program_id(1)
        q = q_ref[...]
        lo_q = lax.broadcasted_iota(jnp.int32, (bq, LANES), 1) < C_HDIM

        def step(j, carry, masked):
            m_a, l_a, m_b, l_b, acc = carry
            off = pl.multiple_of(j * CHUNK, CHUNK)
            kbd = _pair_blockdiag(k_ref[pl.ds(off, CHUNK), :])
            vbd = _pair_blockdiag(v_ref[pl.ds(off, CHUNK), :])
            t = lax.dot_general(q, kbd, _NT, preferred_element_type=F32) - _pair_rows(c_ref, off)
            if masked:
                t = jnp.where(_causal(t.shape, i * bq, j * CHUNK), t, -jnp.inf)
            ta, tb = t[:, :CHUNK], t[:, CHUNK:]
            ma = jnp.maximum(m_a, jnp.max(ta, axis=1, keepdims=True))
            mb = jnp.maximum(m_b, jnp.max(tb, axis=1, keepdims=True))
            al_a, al_b = jnp.exp(m_a - ma), jnp.exp(m_b - mb)
            pa, pb = jnp.exp(ta - ma), jnp.exp(tb - mb)
            l_a = al_a * l_a + jnp.sum(pa, axis=1, keepdims=True)
            l_b = al_b * l_b + jnp.sum(pb, axis=1, keepdims=True)
            pv = _dot(jnp.concatenate([pa, pb], axis=1).astype(BF16), vbd)
            acc = acc * jnp.where(lo_q, al_a, al_b) + pv
            return ma, l_a, mb, l_b, acc

        col = lambda v: jnp.full((bq, 1), v, F32)
        carry = (col(-jnp.inf), col(0.0), col(-jnp.inf), col(0.0), jnp.zeros((bq, LANES), F32))
        carry = lax.fori_loop(0, i * ndiag, functools.partial(step, masked=False), carry)
        for d in range(ndiag):
            carry = step(i * ndiag + d, carry, True)
        m_a, l_a, m_b, l_b, acc = carry
        o_ref[...] = acc * jnp.where(lo_q, 1.0 / l_a, 1.0 / l_b)
        lse_ref[0] = jnp.broadcast_to(m_a + jnp.log(l_a), (bq, LANES))
        lse_ref[1] = jnp.broadcast_to(m_b + jnp.log(l_b), (bq, LANES))

    return pl.pallas_call(
        body,
        name="fox_fwd",
        grid=(C_PAIRS, seq // bq),
        in_specs=[
            pl.BlockSpec((bq, LANES), lambda p, i: (i, p)),
            pl.BlockSpec((seq, LANES), lambda p, i: (0, p)),
            pl.BlockSpec((seq, LANES), lambda p, i: (0, p)),
            pl.BlockSpec((1, 2, seq), lambda p, i: (p, 0, 0)),
        ],
        out_specs=[pl.BlockSpec((bq, LANES), lambda p, i: (i, p)), pl.BlockSpec((2, bq, LANES), lambda p, i: (p, i, 0))],
        out_shape=[jax.ShapeDtypeStruct((seq, C_WIDTH), F32), jax.ShapeDtypeStruct((C_HEADS, seq, LANES), F32)],
        compiler_params=_cparams("parallel", "parallel"),
    )(qb, kb, vb, ct)


def fox_bwd_prep(dy, o, proj):
    seq = o.shape[0]
    tm = min(seq, 256)
    ind = np.zeros((C_WIDTH, C_HEADS * LANES), np.float32)
    for h in range(C_HEADS):
        ind[h * C_HDIM:(h + 1) * C_HDIM, h * LANES:(h + 1) * LANES] = 1.0
    ind = jnp.asarray(ind, BF16)

    def body(dy_ref, o_ref, z_ref, ind_ref, do_ref, dz_ref, dl_ref):
        dy_c, o_v, z = dy_ref[...], o_ref[...], z_ref[...]
        sg = jax.nn.sigmoid(z)
        do = dy_c * (z * sg)
        do_ref[...] = do.astype(BF16)
        dz_ref[...] = (dy_c * o_v * (sg * (1.0 + z * (1.0 - sg)))).astype(BF16)
        prod = do * o_v
        hi = prod.astype(BF16)
        lo = (prod - hi.astype(F32)).astype(BF16)
        dl = _dot(hi, ind_ref[...]) + _dot(lo, ind_ref[...])
        for h in range(C_HEADS):
            dl_ref[h] = dl[:, h * LANES:(h + 1) * LANES]

    return pl.pallas_call(
        body,
        name="fox_bwd_prep",
        grid=(seq // tm,),
        in_specs=[
            pl.BlockSpec((tm, C_WIDTH), lambda i: (i, 1)),
            pl.BlockSpec((tm, C_WIDTH), lambda i: (i, 0)),
            pl.BlockSpec((tm, C_WIDTH), lambda i: (i, 7)),
            _full((C_WIDTH, C_HEADS * LANES)),
        ],
        out_specs=[
            pl.BlockSpec((tm, C_WIDTH), lambda i: (i, 0)),
            pl.BlockSpec((tm, C_WIDTH), lambda i: (i, 0)),
            pl.BlockSpec((C_HEADS, tm, LANES), lambda i: (0, i, 0)),
        ],
        out_shape=[jax.ShapeDtypeStruct((seq, C_WIDTH), BF16)] * 2 + [jax.ShapeDtypeStruct((C_HEADS, seq, LANES), F32)],
        compiler_params=_cparams("parallel"),
    )(dy, o, proj, ind)


def fox_bwd(qb, kb, vb, ct, dob, lse, dlt):
    seq = qb.shape[0]
    bq = min(C_BQ, seq)
    nq, nkv = seq // bq, seq // CHUNK

    def body(q_ref, k_ref, v_ref, c_ref, do_ref, lse_ref, dl_ref, dq_ref, dk_ref, dv_ref, dc_ref, dr_ref, dq_acc, dr_acc):
        j = pl.program_id(1)
        head_a = 2 * pl.program_id(0)
        lane_q = lax.broadcasted_iota(jnp.int32, (bq, LANES), 1)

        @pl.when(j == 0)
        def _():
            dq_acc[...] = jnp.zeros_like(dq_acc)
            dr_acc[...] = jnp.zeros_like(dr_acc)

        off = pl.multiple_of(j * CHUNK, CHUNK)
        kbd = _pair_blockdiag(k_ref[...])
        vbd = _pair_blockdiag(v_ref[...])
        crow = _pair_rows(c_ref, off)

        def step(i, carry, masked):
            dkbd, dvbd, dcr = carry
            rows = pl.ds(pl.multiple_of(i * bq, bq), bq)
            q, do = q_ref[rows, :], do_ref[rows, :]
            lse2 = jnp.concatenate([lse_ref[0, rows, :], lse_ref[1, rows, :]], axis=1)
            dl2 = jnp.concatenate([dl_ref[0, rows, :], dl_ref[1, rows, :]], axis=1)
            t = lax.dot_general(q, kbd, _NT, preferred_element_type=F32) - crow
            p2 = jnp.exp(t - lse2)
            if masked:
                p2 = jnp.where(_causal(p2.shape, i * bq, j * CHUNK), p2, 0.0)
            dp2 = lax.dot_general(do, vbd, _NT, preferred_element_type=F32)
            ds2 = p2 * (dp2 - dl2)
            p2b, ds2b = p2.astype(BF16), ds2.astype(BF16)
            dvbd = dvbd + lax.dot_general(p2b, do, _TN, preferred_element_type=F32)
            dkbd = dkbd + lax.dot_general(ds2b, q, _TN, preferred_element_type=F32)
            dq_acc[rows, :] += _dot(ds2b, kbd)
            dcr = dcr - jnp.sum(ds2, axis=0, keepdims=True)
            rs_a = jnp.sum(ds2[:, :CHUNK], axis=1, keepdims=True)
            rs_b = jnp.sum(ds2[:, CHUNK:], axis=1, keepdims=True)
            dr_acc[rows, :] += jnp.where(lane_q == head_a, rs_a, jnp.where(lane_q == head_a + 1, rs_b, 0.0))
            return dkbd, dvbd, dcr

        i0 = (j * CHUNK) // bq
        carry = (jnp.zeros((2 * CHUNK, LANES), F32), jnp.zeros((2 * CHUNK, LANES), F32), jnp.zeros((1, 2 * CHUNK), F32))
        carry = step(i0, carry, True)
        dkbd, dvbd, dcr = lax.fori_loop(i0 + 1, nq, functools.partial(step, masked=False), carry)
        lo = lax.broadcasted_iota(jnp.int32, (CHUNK, LANES), 1) < C_HDIM
        dk_ref[...] = jnp.where(lo, dkbd[:CHUNK], dkbd[CHUNK:]).astype(BF16)
        dv_ref[...] = jnp.where(lo, dvbd[:CHUNK], dvbd[CHUNK:]).astype(BF16)
        dc_ref[0, 0:1, :] = dcr[:, :CHUNK]
        dc_ref[0, 1:2, :] = dcr[:, CHUNK:]

        @pl.when(j == nkv - 1)
        def _():
            dq_ref[...] = (dq_acc[...] * (C_HDIM ** -0.5)).astype(BF16)
            dr_ref[0] = dr_acc[...]

    whole = lambda: pl.BlockSpec((seq, LANES), lambda p, j: (0, p))
    blk = lambda: pl.BlockSpec((CHUNK, LANES), lambda p, j: (j, p))
    rep = lambda: pl.BlockSpec((2, seq, LANES), lambda p, j: (p, 0, 0))
    return pl.pallas_call(
        body,
        name="fox_bwd",
        grid=(C_PAIRS, nkv),
        in_specs=[whole(), blk(), blk(), pl.BlockSpec((1, 2, seq), lambda p, j: (p, 0, 0)), whole(), rep(), rep()],
        out_specs=[whole(), blk(), blk(), pl.BlockSpec((1, 2, CHUNK), lambda p, j: (p, 0, j)),
                   pl.BlockSpec((1, seq, LANES), lambda p, j: (p, 0, 0))],
        out_shape=[jax.ShapeDtypeStruct((seq, C_WIDTH), BF16)] * 3
        + [jax.ShapeDtypeStruct((C_PAIRS, 2, seq), F32), jax.ShapeDtypeStruct((C_PAIRS, seq, LANES), F32)],
        scratch_shapes=[pltpu.VMEM((seq, LANES), F32), pltpu.VMEM((seq, LANES), F32)],
        compiler_params=_cparams("parallel", "arbitrary"),
    )(qb, kb, vb, ct, dob, lse, dlt)


def fox_post(dct, dcq, proj, bf_row):
    seq = proj.shape[0]
    nc = seq // CHUNK
    triu = jnp.asarray(np.triu(np.ones((CHUNK, CHUNK), np.float32)))

    def body(dc_ref, dr_ref, fl_ref, bf_ref, u_ref, dfl_ref, dbf_ref, carry_ref):
        @pl.when(pl.program_id(0) == 0)
        def _():
            carry_ref[...] = jnp.zeros_like(carry_ref)
            dbf_ref[...] = jnp.zeros_like(dbf_ref)

        dc = jnp.concatenate([dc_ref[...], jnp.zeros((CHUNK - C_HEADS, CHUNK), F32)], axis=0).T
        dc = dc + ((dr_ref[0] + dr_ref[1]) + (dr_ref[2] + dr_ref[3]))
        g = _dot(u_ref[...], dc, precision=HI) + carry_ref[...]
        carry_ref[...] += jnp.sum(dc, axis=0, keepdims=True)
        dfl = g * jax.nn.sigmoid(-(fl_ref[:, :LANES] + bf_ref[...]))
        dbf_ref[...] += jnp.sum(dfl, axis=0, keepdims=True)
        dfl_ref[...] = jnp.concatenate([dfl, jnp.zeros_like(dfl)], axis=1).astype(BF16)

    rev = lambda n: nc - 1 - n
    return pl.pallas_call(
        body,
        name="fox_post",
        grid=(nc,),
        in_specs=[
            pl.BlockSpec((C_HEADS, CHUNK), lambda n: (0, rev(n))),
            pl.BlockSpec((C_PAIRS, CHUNK, LANES), lambda n: (0, rev(n), 0)),
            pl.BlockSpec((CHUNK, 256), lambda n: (rev(n), 3)),
            _full((1, LANES)), _full((CHUNK, CHUNK)),
        ],
        out_specs=[pl.BlockSpec((CHUNK, 256), lambda n: (rev(n), 0)), _full((1, LANES))],
        out_shape=[jax.ShapeDtypeStruct((seq, 256), BF16), jax.ShapeDtypeStruct((1, LANES), F32)],
        scratch_shapes=[pltpu.VMEM((1, LANES), F32)],
        compiler_params=_cparams("arbitrary"),
    )(dct, dcq, proj, bf_row, triu)


N_DEV = 8
MESH = pl.DeviceIdType.MESH
_ANY = pl.BlockSpec(memory_space=pl.ANY)


def _mesh_pos():
    return lax.axis_index("x"), lax.axis_index("y"), lax.axis_index("c")


def _dev_index(px, py, pc):
    return 4 * px + 2 * py + pc


def allgather_weights(wi, wo):
    def body(wi_ref, wo_ref, wi_all, wo_all, send_sems, recv_sems, local_sems):
        x, y, c = _mesh_pos()
        me, sibling = (x, y, c), (x, y, 1 - c)
        chips = [(1 - x, y), (x, 1 - y), (1 - x, 1 - y)]
        arrays = ((wi_ref, wi_all), (wo_ref, wo_all))

        def copy(a, k, block, to, own=False):
            src, out = arrays[a]
            slot = out.at[_dev_index(*block)]
            return pltpu.make_async_remote_copy(
                src_ref=src if own else slot, dst_ref=slot, send_sem=send_sems.at[a, k], recv_sem=recv_sems.at[a, k],
                device_id=to, device_id_type=MESH)

        both = range(len(arrays))
        mine = [pltpu.make_async_copy(arrays[a][0], arrays[a][1].at[_dev_index(*me)], local_sems.at[a]) for a in both]
        for cp in mine:
            cp.start()
        first = [copy(a, 0, me, sibling, own=True) for a in both]
        first += [copy(a, 1 + j, me, (*chip, c), own=True) for j, chip in enumerate(chips) for a in both]
        for cp in first:
            cp.start()
        passed = [copy(a, 4 + j, (*chip, c), sibling) for j, chip in enumerate(chips) for a in both]
        for j, chip in enumerate(chips):
            for a in both:
                copy(a, 1 + j, (*chip, c), me).wait_recv()
            for a in both:
                passed[2 * j + a].start()
        for a in both:
            copy(a, 0, sibling, me).wait_recv()
        for j, chip in enumerate(chips):
            for a in both:
                copy(a, 4 + j, (*chip, 1 - c), me).wait_recv()
        for cp in first + passed:
            cp.wait_send()
        for cp in mine:
            cp.wait()

    return pl.pallas_call(
        body,
        name="allgather_weights",
        in_specs=[_ANY, _ANY],
        out_specs=[_ANY, _ANY],
        out_shape=[jax.ShapeDtypeStruct((N_DEV,) + wi.shape, wi.dtype), jax.ShapeDtypeStruct((N_DEV,) + wo.shape, wo.dtype)],
        scratch_shapes=[pltpu.SemaphoreType.DMA((2, 7)), pltpu.SemaphoreType.DMA((2, 7)), pltpu.SemaphoreType.DMA((2,))],
    )(wi, wo)


def exchange_grads(gwi, gwo, gsm):
    def body(gwi_ref, gwo_ref, gsm_ref, rwi, rwo, rsm, send_sems, recv_sems, local_sems):
        x, y, c = _mesh_pos()
        me = _dev_index(x, y, c)
        srcs, outs = (gwi_ref, gwo_ref, gsm_ref), (rwi, rwo, rsm)
        three = range(3)

        def for_dev(a, dev):
            return srcs[a] if a == 2 else srcs[a].at[dev]

        local = [pltpu.make_async_copy(for_dev(a, me), outs[a].at[me], local_sems.at[a]) for a in three]
        for cp in local:
            cp.start()

        def peer_of(k):
            return x ^ ((k >> 2) & 1), y ^ ((k >> 1) & 1), c ^ (k & 1)

        def copy(a, k, slot):
            p = peer_of(k)
            return pltpu.make_async_remote_copy(
                src_ref=for_dev(a, _dev_index(*p)), dst_ref=outs[a].at[slot], send_sem=send_sems.at[a, k - 1],
                recv_sem=recv_sems.at[a, k - 1], device_id=p, device_id_type=MESH)

        sends = [copy(a, k, me) for k in range(1, N_DEV) for a in three]
        for cp in sends:
            cp.start()
        for k in range(1, N_DEV):
            for a in three:
                copy(a, k, _dev_index(*peer_of(k))).wait_recv()
        for cp in sends:
            cp.wait_send()
        for cp in local:
            cp.wait()

    return pl.pallas_call(
        body,
        name="exchange_grads",
        in_specs=[_ANY, _ANY, _ANY],
        out_specs=[_ANY, _ANY, _ANY],
        out_shape=[jax.ShapeDtypeStruct(gwi.shape, gwi.dtype), jax.ShapeDtypeStruct(gwo.shape, gwo.dtype),
                   jax.ShapeDtypeStruct((N_DEV,) + gsm.shape, gsm.dtype)],
        scratch_shapes=[pltpu.SemaphoreType.DMA((3, 7)), pltpu.SemaphoreType.DMA((3, 7)), pltpu.SemaphoreType.DMA((3,))],
    )(gwi, gwo, gsm)


ADAM_LR = 0.001
ADAM_B1 = 0.9
ADAM_B2 = 0.999
ADAM_EPS = 1e-08
ADAM_WD = 0.01
ADAM_STEP = 10


def adam_reduce(parts, w, m, v, rows, name):
    n_l, n_r, n_c = w.shape

    def body(p_ref, w_ref, m_ref, v_ref, g_ref, d_ref, m2_ref, v2_ref):
        g = p_ref[0, 0]
        for d in range(1, N_DEV):
            g = g + p_ref[d, 0]
        m2 = ADAM_B1 * m_ref[0] + (1.0 - ADAM_B1) * g
        v2 = ADAM_B2 * v_ref[0] + (1.0 - ADAM_B2) * (g * g)
        m_hat = m2 / (1.0 - ADAM_B1 ** ADAM_STEP)
        v_hat = v2 / (1.0 - ADAM_B2 ** ADAM_STEP)
        g_ref[0] = g
        d_ref[0] = -ADAM_LR * (m_hat / (jnp.sqrt(v_hat) + ADAM_EPS) + ADAM_WD * w_ref[0])
        m2_ref[0] = m2
        v2_ref[0] = v2

    blk = lambda: pl.BlockSpec((1, rows, n_c), lambda l, r: (l, r, 0))
    return pl.pallas_call(
        body,
        name=name,
        grid=(n_l, n_r // rows),
        in_specs=[pl.BlockSpec((N_DEV, 1, rows, n_c), lambda l, r: (0, l, r, 0)), blk(), blk(), blk()],
        out_specs=[blk(), blk(), blk(), blk()],
        out_shape=[jax.ShapeDtypeStruct(w.shape, F32)] * 4,
        compiler_params=_cparams("parallel", "parallel"),
    )(parts, w, m, v)


_SMALL = (("norm_g", (2, 1024)), ("gmlp_ln_g", (2, 4, 64)), ("gmlp_ln_b", (2, 4, 64)), ("gmlp_w_s", (2, 4, 128, 128)),
          ("gmlp_b_s", (2, 4, 128)), ("hgrn_lb", (2, 256)), ("hgrn_onorm_g", (2, 64)), ("fox_b_f", (2, 8)),
          ("final_norm_g", (1024,)), ("loss", ()))


def _padded(n):
    return -(-n // LANES) * LANES


_SMALL_ROWS = -(-sum(_padded(int(np.prod(s))) for _, s in _SMALL) // LANES // 8) * 8


def _pack_small(vals):
    flat = []
    for (name, shape), a in zip(_SMALL, vals, strict=True):
        n = int(np.prod(shape))
        flat.append(jnp.pad(a.reshape(n).astype(F32), (0, _padded(n) - n)))
    flat = jnp.concatenate(flat)
    return jnp.pad(flat, (0, _SMALL_ROWS * LANES - flat.shape[0])).reshape(_SMALL_ROWS, LANES)


def _unpack_small(slab):
    flat, out, at = slab.reshape(-1), {}, 0
    for name, shape in _SMALL:
        n = int(np.prod(shape))
        out[name] = flat[at:at + n].reshape(shape)
        at += _padded(n)
    return out


def kernel(x, norm_g, w_in, w_out, gmlp_ln_g, gmlp_ln_b, gmlp_w_s, gmlp_b_s, hgrn_lb, hgrn_onorm_g, fox_b_f, final_norm_g, loss_target, m_norm_g, m_w_in, m_w_out, m_gmlp_ln_g, m_gmlp_ln_b, m_gmlp_w_s, m_gmlp_b_s, m_hgrn_lb, m_hgrn_onorm_g, m_fox_b_f, m_final_norm_g, v_norm_g, v_w_in, v_w_out, v_gmlp_ln_g, v_gmlp_ln_b, v_gmlp_w_s, v_gmlp_b_s, v_hgrn_lb, v_hgrn_onorm_g, v_fox_b_f, v_final_norm_g):
    depth = w_in.shape[0]
    seq = x.shape[1]
    n_in = w_in.shape[2] * N_DEV
    xs, tgt = x[0], loss_target[0]

    wi_all, wo_all = allgather_weights(w_in.astype(BF16), w_out.astype(BF16))
    wi_full = jnp.transpose(wi_all, (1, 2, 0, 3)).reshape(depth, D_MODEL, n_in)
    wi_int = jnp.concatenate(
        [wi_full[:, :, :768], wi_full[:, :, 3840:n_in], jnp.zeros((depth, D_MODEL, 1024 - 768 - (n_in - 3840)), BF16),
         wi_full[:, :, 768:3840]], axis=2)
    wo_full = jnp.transpose(wo_all, (1, 0, 2, 3)).reshape(depth, D_MODEL, D_MODEL)

    ln_g = gmlp_ln_g.reshape(depth, 1, A_WIDTH)
    ln_b = gmlp_ln_b.reshape(depth, 1, A_WIDTH)
    bs_t = jnp.pad(jnp.transpose(gmlp_b_s, (0, 2, 1)), ((0, 0), (0, 0), (0, LANES - A_GROUPS)))
    lb0, lb1 = hgrn_lb[0:1], hgrn_lb[1:2]
    onorm = jnp.tile(hgrn_onorm_g, (1, B_HEADS)).reshape(depth, 1, B_WIDTH)
    bf_row = jnp.pad(fox_b_f, ((0, 0), (0, LANES - C_HEADS))).reshape(depth, 1, LANES)

    saved = []
    xc = xs
    for l in range(depth):
        proj, h = inproj(xc, norm_g[l:l + 1], wi_int[l])
        ya = gmlp_fwd(proj, ln_g[l], ln_b[l], gmlp_w_s[l], bs_t[l])
        yb, states = hgrn_fwd(proj, lb0, lb1, onorm[l], l)
        qb, kb, vb, ct = fox_prep(proj, bf_row[l])
        ct = ct.reshape(C_PAIRS, 2, seq)
        o, lse = fox_fwd(qb, kb, vb, ct)
        xn, yfull = outproj(xc, ya, yb, o, proj, wo_full[l])
        saved.append((xc, proj, h, states, qb, kb, vb, ct, o, lse, yfull))
        xc = xn

    dx, d_final_g, loss_tile = final_loss(xc, final_norm_g[None], tgt)

    g_norm, g_wi, g_wo = [None] * depth, [None] * depth, [None] * depth
    g_ln_g, g_ln_b, g_ws, g_bs, g_on, g_bf = ([None] * depth for _ in range(6))
    g_lb0, g_lb1 = jnp.zeros_like(lb0), jnp.zeros_like(lb1)
    for l in reversed(range(depth)):
        x_in, proj, h, states, qb, kb, vb, ct, o, lse, yfull = saved[l]
        dy, g_wo[l] = outproj_bwd(dx, yfull, wo_full[l])
        d_a, g_ln_g[l], g_ln_b[l], g_ws[l], dbs_t = gmlp_bwd(proj, dy, ln_g[l], ln_b[l], gmlp_w_s[l], bs_t[l])
        g_bs[l] = dbs_t[:, :A_GROUPS].T
        d_b, d0, d1, don = hgrn_bwd(proj, states, dy, lb0, lb1, onorm[l], l)
        g_lb0, g_lb1 = g_lb0 + d0, g_lb1 + d1
        g_on[l] = don.reshape(B_HEADS, B_KDIM).sum(0)
        dob, d_z, dlt = fox_bwd_prep(dy, o, proj)
        d_q, d_k, d_v, dct, dcq = fox_bwd(qb, kb, vb, ct, dob, lse, dlt)
        d_fl, dbf = fox_post(dct.reshape(C_HEADS, seq), dcq, proj, bf_row[l])
        g_bf[l] = dbf[0, :C_HEADS]
        dproj = jnp.concatenate([d_a, d_fl, d_b, d_q, d_k, d_v, d_z], axis=1)
        dx, g_norm[l] = inproj_bwd_x(dproj, wi_int[l], x_in, norm_g[l:l + 1], dx)
        dwi = inproj_bwd_w(h, dproj)
        g_wi[l] = jnp.concatenate([dwi[:, :768], dwi[:, 1024:], dwi[:, 768:768 + n_in - 3840]], axis=1)

    gwi = jnp.transpose(jnp.stack(g_wi).reshape(depth, D_MODEL, N_DEV, n_in // N_DEV), (2, 0, 1, 3))
    gwo = jnp.transpose(jnp.stack(g_wo).reshape(depth, N_DEV, D_MODEL // N_DEV, D_MODEL), (1, 0, 2, 3))
    gsm = _pack_small([
        jnp.concatenate(g_norm), jnp.stack(g_ln_g), jnp.stack(g_ln_b), jnp.stack(g_ws), jnp.stack(g_bs),
        jnp.concatenate([g_lb0, g_lb1]), jnp.stack(g_on), jnp.stack(g_bf), d_final_g, loss_tile[0, 0]])
    rwi, rwo, rsm = exchange_grads(gwi, gwo, gsm)

    small_w = (norm_g, gmlp_ln_g, gmlp_ln_b, gmlp_w_s, gmlp_b_s, hgrn_lb, hgrn_onorm_g, fox_b_f, final_norm_g)
    small_m = (m_norm_g, m_gmlp_ln_g, m_gmlp_ln_b, m_gmlp_w_s, m_gmlp_b_s, m_hgrn_lb, m_hgrn_onorm_g, m_fox_b_f, m_final_norm_g)
    small_v = (v_norm_g, v_gmlp_ln_g, v_gmlp_ln_b, v_gmlp_w_s, v_gmlp_b_s, v_hgrn_lb, v_hgrn_onorm_g, v_fox_b_f, v_final_norm_g)
    zero = jnp.zeros((), F32)
    res_wi = adam_reduce(rwi, w_in, m_w_in, v_w_in, 256, "adam_w_in")
    res_wo = adam_reduce(rwo, w_out, m_w_out, v_w_out, w_out.shape[1], "adam_w_out")
    res_sm = adam_reduce(rsm[:, None], _pack_small(small_w + (zero,))[None], _pack_small(small_m + (zero,))[None],
                         _pack_small(small_v + (zero,))[None], _SMALL_ROWS, "adam_small")
    res_sm = [_unpack_small(r[0]) for r in res_sm]

    def group(i):
        s = res_sm[i]
        return [s["norm_g"], res_wi[i], res_wo[i], s["gmlp_ln_g"], s["gmlp_ln_b"], s["gmlp_w_s"], s["gmlp_b_s"],
                s["hgrn_lb"], s["hgrn_onorm_g"], s["fox_b_f"], s["final_norm_g"]]

    return (res_sm[0]["loss"], dx[None], *group(0), *group(1), *group(2), *group(3))
```

```python
import functools

import jax
import jax.numpy as jnp
import numpy as np
from jax import lax
from jax.experimental import pallas as pl
from jax.experimental.pallas import tpu as pltpu

F32 = jnp.float32
BF16 = jnp.bfloat16
HI = lax.Precision.HIGHEST

NORM_EPS = 1e-6
F_FLOOR = 1e-30
CHUNK = 128
LANES = 128
VMEM_LIMIT = 56 * 1024 * 1024


def _cparams(*sem):
    return pltpu.CompilerParams(dimension_semantics=sem, vmem_limit_bytes=VMEM_LIMIT)


def _dot(a, b, dims=(((1,), (0,)), ((), ())), precision=None):
    return lax.dot_general(a, b, dims, precision=precision, preferred_element_type=F32)


_NT = (((1,), (1,)), ((), ()))
_TN = (((0,), (0,)), ((), ()))


def _bd(a, b):
    return _dot(a.astype(BF16), b.astype(BF16))


def _group_mean_matrix(width, group):
    idx = np.arange(width) // group
    return jnp.asarray((idx[:, None] == idx[None, :]).astype(np.float32) / group)


def _group_ones_matrix(width, group):
    idx = np.arange(width) // group
    return jnp.asarray((idx[:, None] == idx[None, :]).astype(np.float32))


A_WIDTH = 256
A_GROUPS = 4
A_GDIM = 64


def _gmlp_chunk(x3, ln_g, ln_b, w_s, bs_t, mean_m, gind):
    u = jax.nn.gelu(x3[:, :A_WIDTH])
    v = jax.nn.gelu(x3[:, A_WIDTH:2 * A_WIDTH])
    z = x3[:, 2 * A_WIDTH:]
    mu = _dot(v, mean_m, precision=HI)
    d = v - mu
    var = _dot(d * d, mean_m, precision=HI)
    vn = d * lax.rsqrt(var + NORM_EPS) * ln_g + ln_b
    vnb = vn.astype(BF16)
    row = lax.broadcasted_iota(jnp.int32, (CHUNK, CHUNK), 0)
    col = lax.broadcasted_iota(jnp.int32, (CHUNK, CHUNK), 1)
    causal = row >= col
    lane_g = lax.shift_right_logical(lax.broadcasted_iota(jnp.int32, (CHUNK, A_WIDTH), 1), 6)
    mixed = _dot(bs_t, gind, precision=HI)
    for g in range(A_GROUPS):
        wc = jnp.where(causal, w_s[g], 0.0).astype(BF16)
        mixed = mixed + jnp.where(lane_g == g, _dot(wc, vnb), 0.0)
    return u * mixed * jax.nn.silu(z)


def _gmlp_consts():
    gind = np.zeros((LANES, A_WIDTH), np.float32)
    for g in range(A_GROUPS):
        gind[g, g * A_GDIM:(g + 1) * A_GDIM] = 1.0
    return _group_mean_matrix(A_WIDTH, A_GDIM), jnp.asarray(gind)


def _full(shape):
    return pl.BlockSpec(shape, lambda *_: (0,) * len(shape))


def gmlp_fwd(proj, ln_g, ln_b, w_s, bs_t):
    seq = proj.shape[0]
    mean_m, gind = _gmlp_consts()

    def body(x_ref, g_ref, b_ref, w_ref, bs_ref, m_ref, gi_ref, y_ref):
        y = _gmlp_chunk(x_ref[...], g_ref[...], b_ref[...], w_ref[...], bs_ref[...], m_ref[...], gi_ref[...])
        y_ref[...] = y.astype(BF16)

    return pl.pallas_call(
        body,
        name="gmlp_fwd",
        grid=(seq // CHUNK,),
        in_specs=[
            pl.BlockSpec((CHUNK, 3 * A_WIDTH), lambda n: (n, 0)),
            _full((1, A_WIDTH)), _full((1, A_WIDTH)), _full((A_GROUPS, CHUNK, CHUNK)), _full((CHUNK, LANES)),
            _full((A_WIDTH, A_WIDTH)), _full((LANES, A_WIDTH)),
        ],
        out_specs=pl.BlockSpec((CHUNK, A_WIDTH), lambda n: (n, 0)),
        out_shape=jax.ShapeDtypeStruct((seq, A_WIDTH), BF16),
        compiler_params=_cparams("parallel"),
    )(proj, ln_g, ln_b, w_s, bs_t, mean_m, gind)


def gmlp_bwd(proj, dy, ln_g, ln_b, w_s, bs_t):
    seq = proj.shape[0]
    mean_m, gind = _gmlp_consts()

    def body(x_ref, dy_ref, g_ref, b_ref, w_ref, bs_ref, m_ref, gi_ref, dx_ref, dg_ref, db_ref, dw_ref, dbs_ref):
        fn = functools.partial(_gmlp_chunk, mean_m=m_ref[...], gind=gi_ref[...])
        _, vjp = jax.vjp(fn, x_ref[...], g_ref[...], b_ref[...], w_ref[...], bs_ref[...])
        dx, dg, db, dw, dbs = vjp(dy_ref[...])
        dx_ref[...] = dx.astype(BF16)

        @pl.when(pl.program_id(0) == 0)
        def _():
            dg_ref[...] = jnp.zeros_like(dg_ref)
            db_ref[...] = jnp.zeros_like(db_ref)
            dw_ref[...] = jnp.zeros_like(dw_ref)
            dbs_ref[...] = jnp.zeros_like(dbs_ref)

        dg_ref[...] += dg
        db_ref[...] += db
        dw_ref[...] += dw
        dbs_ref[...] += dbs

    return pl.pallas_call(
        body,
        name="gmlp_bwd",
        grid=(seq // CHUNK,),
        in_specs=[
            pl.BlockSpec((CHUNK, 3 * A_WIDTH), lambda n: (n, 0)),
            pl.BlockSpec((CHUNK, A_WIDTH), lambda n: (n, 0)),
            _full((1, A_WIDTH)), _full((1, A_WIDTH)), _full((A_GROUPS, CHUNK, CHUNK)), _full((CHUNK, LANES)),
            _full((A_WIDTH, A_WIDTH)), _full((LANES, A_WIDTH)),
        ],
        out_specs=[
            pl.BlockSpec((CHUNK, 3 * A_WIDTH), lambda n: (n, 0)),
            _full((1, A_WIDTH)), _full((1, A_WIDTH)), _full((A_GROUPS, CHUNK, CHUNK)), _full((CHUNK, LANES)),
        ],
        out_shape=[
            jax.ShapeDtypeStruct((seq, 3 * A_WIDTH), BF16),
            jax.ShapeDtypeStruct((1, A_WIDTH), F32), jax.ShapeDtypeStruct((1, A_WIDTH), F32),
            jax.ShapeDtypeStruct((A_GROUPS, CHUNK, CHUNK), F32), jax.ShapeDtypeStruct((CHUNK, LANES), F32),
        ],
        compiler_params=_cparams("arbitrary"),
    )(proj, dy, ln_g, ln_b, w_s, bs_t, mean_m, gind)


B_WIDTH = 256
B_HEADS = 4
B_KDIM = 64
B_LEVELS = (64, 32, 16, 8, 4, 2, 1)


def _hgrn_consts():
    t = np.arange(CHUNK)
    u = t[None, :]
    mats = [np.tril(np.ones((CHUNK, CHUNK), np.float32))]
    for m in B_LEVELS:
        p = (t // (2 * m)) * (2 * m) + m - 1
        right = (t % (2 * m)) >= m
        sel = np.where(right[:, None], (u > p[:, None]) & (u <= t[:, None]), (u > t[:, None]) & (u <= p[:, None]))
        mats.append(sel.astype(np.float32))
    return jnp.asarray(np.concatenate(mats, 0)), _group_ones_matrix(B_WIDTH, B_KDIM)


def _hgrn_lower_bound(lb0, lb1, layer):
    mx = jnp.maximum(lb0, lb1)
    e0 = jnp.exp(lb0 - mx)
    e1 = jnp.exp(lb1 - mx)
    p0 = e0 / (e0 + e1)
    p1 = e1 / (e0 + e1)
    cs = p0 if layer == 0 else p0 + p1
    return jnp.clip(cs - p0, 0.0, 1.0 - 1e-6)


def _hgrn_chunk(x4, st, lb0, lb1, onorm, layer, tstack, ones_bd):
    q_raw, fl, v, zg = (x4[:, i * B_WIDTH:(i + 1) * B_WIDTH] for i in range(4))
    lb = _hgrn_lower_bound(lb0, lb1, layer)
    q = jax.nn.silu(q_raw) * (B_KDIM ** -0.5)
    f = lb + (1.0 - lb) * jax.nn.sigmoid(fl)
    logf = jnp.log(jnp.maximum(f, F_FLOOR))
    k = (1.0 - lb) * jax.nn.sigmoid(-fl)
    dall = _dot(tstack, logf, precision=HI)
    b = dall[:CHUNK]
    b_last = jnp.sum(logf, axis=0, keepdims=True)
    vb = v.astype(BF16)

    lane_h = lax.shift_right_logical(lax.broadcasted_iota(jnp.int32, (CHUNK, B_WIDTH), 1), 6)
    row = lax.broadcasted_iota(jnp.int32, (CHUNK, B_WIDTH), 0)
    srow = lax.broadcasted_iota(jnp.int32, (B_HEADS * CHUNK, CHUNK), 0) & (CHUNK - 1)
    scol = lax.broadcasted_iota(jnp.int32, (B_HEADS * CHUNK, CHUNK), 1)

    def heads_on_rows(a):
        return jnp.concatenate([jnp.where(lane_h == h, a, 0.0) for h in range(B_HEADS)], axis=0)

    def heads_from_rows(r):
        out = jnp.where(lane_h == 0, r[:CHUNK], 0.0)
        for h in range(1, B_HEADS):
            out = out + jnp.where(lane_h == h, r[h * CHUNK:(h + 1) * CHUNK], 0.0)
        return out

    o = lax.dot_general((q * jnp.exp(b)).astype(BF16), st.astype(BF16), _NT, preferred_element_type=F32)
    scores = jnp.zeros((B_HEADS * CHUNK, CHUNK), F32)
    for li, m in enumerate(B_LEVELS):
        e = jnp.exp(dall[(li + 1) * CHUNK:(li + 2) * CHUNK])
        right = (row & (2 * m - 1)) >= m
        qt = jnp.where(right, q * e, 0.0)
        kt = jnp.where(right, 0.0, k * e)
        sc = lax.dot_general(heads_on_rows(qt).astype(BF16), kt.astype(BF16), _NT, preferred_element_type=F32)
        sh = int(np.log2(2 * m))
        same = lax.shift_right_logical(srow, sh) == lax.shift_right_logical(scol, sh)
        scores = scores + jnp.where(same, sc, 0.0)
    o = o + heads_from_rows(_dot(scores.astype(BF16), vb))
    o = o + _dot(q * k, ones_bd, precision=HI) * v

    kv = lax.dot_general(vb, (k * jnp.exp(b_last - b)).astype(BF16), _TN, preferred_element_type=F32)
    st_new = st * jnp.exp(b_last) + jnp.where(ones_bd > 0.5, kv, 0.0)

    ms = _dot(o * o, ones_bd, precision=HI) * (1.0 / B_KDIM)
    y = o * lax.rsqrt(ms + NORM_EPS) * onorm * jax.nn.silu(zg)
    return y, st_new


def hgrn_fwd(proj, lb0, lb1, onorm, layer):
    seq = proj.shape[0]
    nc = seq // CHUNK
    tstack, ones_bd = _hgrn_consts()

    def body(x_ref, lb0_ref, lb1_ref, on_ref, t_ref, e_ref, y_ref, st_out_ref, st_ref):
        @pl.when(pl.program_id(0) == 0)
        def _():
            st_ref[...] = jnp.zeros_like(st_ref)

        st = st_ref[...]
        st_out_ref[0] = st
        y, st_new = _hgrn_chunk(x_ref[...], st, lb0_ref[...], lb1_ref[...], on_ref[...], layer, t_ref[...], e_ref[...])
        y_ref[...] = y.astype(BF16)
        st_ref[...] = st_new

    return pl.pallas_call(
        body,
        name=f"hgrn_fwd_{layer}",
        grid=(nc,),
        in_specs=[
            pl.BlockSpec((CHUNK, 4 * B_WIDTH), lambda n: (n, 1)),
            _full((1, B_WIDTH)), _full((1, B_WIDTH)), _full((1, B_WIDTH)),
            _full(((len(B_LEVELS) + 1) * CHUNK, CHUNK)), _full((B_WIDTH, B_WIDTH)),
        ],
        out_specs=[
            pl.BlockSpec((CHUNK, B_WIDTH), lambda n: (n, 0)),
            pl.BlockSpec((1, B_WIDTH, B_WIDTH), lambda n: (n, 0, 0)),
        ],
        out_shape=[jax.ShapeDtypeStruct((seq, B_WIDTH), BF16), jax.ShapeDtypeStruct((nc, B_WIDTH, B_WIDTH), F32)],
        scratch_shapes=[pltpu.VMEM((B_WIDTH, B_WIDTH), F32)],
        compiler_params=_cparams("arbitrary"),
    )(proj, lb0, lb1, onorm, tstack, ones_bd)


def hgrn_bwd(proj, states, dy, lb0, lb1, onorm, layer):
    seq = proj.shape[0]
    nc = seq // CHUNK
    tstack, ones_bd = _hgrn_consts()

    def body(x_ref, st_in_ref, dy_ref, lb0_ref, lb1_ref, on_ref, t_ref, e_ref, dx_ref, d0_ref, d1_ref, don_ref, dst_ref):
        @pl.when(pl.program_id(0) == 0)
        def _():
            dst_ref[...] = jnp.zeros_like(dst_ref)
            d0_ref[...] = jnp.zeros_like(d0_ref)
            d1_ref[...] = jnp.zeros_like(d1_ref)
            don_ref[...] = jnp.zeros_like(don_ref)

        fn = functools.partial(_hgrn_chunk, layer=layer, tstack=t_ref[...], ones_bd=e_ref[...])
        _, vjp = jax.vjp(fn, x_ref[...], st_in_ref[0], lb0_ref[...], lb1_ref[...], on_ref[...])
        dx, dst, d0, d1, don = vjp((dy_ref[...], dst_ref[...]))
        dx_ref[...] = dx.astype(BF16)
        dst_ref[...] = dst
        d0_ref[...] += d0
        d1_ref[...] += d1
        don_ref[...] += don

    rev = lambda n: nc - 1 - n
    return pl.pallas_call(
        body,
        name=f"hgrn_bwd_{layer}",
        grid=(nc,),
        in_specs=[
            pl.BlockSpec((CHUNK, 4 * B_WIDTH), lambda n: (rev(n), 1)),
            pl.BlockSpec((1, B_WIDTH, B_WIDTH), lambda n: (rev(n), 0, 0)),
            pl.BlockSpec((CHUNK, B_WIDTH), lambda n: (rev(n), 1)),
            _full((1, B_WIDTH)), _full((1, B_WIDTH)), _full((1, B_WIDTH)),
            _full(((len(B_LEVELS) + 1) * CHUNK, CHUNK)), _full((B_WIDTH, B_WIDTH)),
        ],
        out_specs=[
            pl.BlockSpec((CHUNK, 4 * B_WIDTH), lambda n: (rev(n), 0)),
            _full((1, B_WIDTH)), _full((1, B_WIDTH)), _full((1, B_WIDTH)),
        ],
        out_shape=[jax.ShapeDtypeStruct((seq, 4 * B_WIDTH), BF16)] + [jax.ShapeDtypeStruct((1, B_WIDTH), F32)] * 3,
        scratch_shapes=[pltpu.VMEM((B_WIDTH, B_WIDTH), F32)],
        compiler_params=_cparams("arbitrary"),
    )(proj, states, dy, lb0, lb1, onorm, tstack, ones_bd)


D_MODEL = 1024
D_INT = 4096


def _rms_stats(xf):
    r = lax.rsqrt(jnp.mean(xf * xf, axis=-1, keepdims=True) + NORM_EPS)
    return r, xf * r


def _rms_bwd(dy, g, r, xh):
    u = dy * g
    return r * (u - xh * jnp.mean(u * xh, axis=-1, keepdims=True))


def inproj(x, g, w):
    seq = x.shape[0]
    tm, tn = min(seq, 1024), 512

    def body(x_ref, g_ref, w_ref, p_ref, h_ref):
        @pl.when(pl.program_id(1) == 0)
        def _():
            _, xh = _rms_stats(x_ref[...])
            h_ref[...] = (xh * g_ref[...]).astype(BF16)

        p_ref[...] = _dot(h_ref[...], w_ref[...])

    return pl.pallas_call(
        body,
        name="inproj",
        grid=(seq // tm, D_INT // tn),
        in_specs=[
            pl.BlockSpec((tm, D_MODEL), lambda i, j: (i, 0)),
            _full((1, D_MODEL)),
            pl.BlockSpec((D_MODEL, tn), lambda i, j: (0, j)),
        ],
        out_specs=[pl.BlockSpec((tm, tn), lambda i, j: (i, j)), pl.BlockSpec((tm, D_MODEL), lambda i, j: (i, 0))],
        out_shape=[jax.ShapeDtypeStruct((seq, D_INT), F32), jax.ShapeDtypeStruct((seq, D_MODEL), BF16)],
        compiler_params=_cparams("parallel", "arbitrary"),
    )(x, g, w)


def outproj(x, ya, yb, o, proj, wo):
    seq = x.shape[0]
    tm = min(seq, 512)

    def body(x_ref, ya_ref, yb_ref, o_ref, z_ref, w_ref, xn_ref, y_ref):
        yc = (o_ref[...] * jax.nn.silu(z_ref[...])).astype(BF16)
        y = jnp.concatenate([ya_ref[...], yb_ref[...], yc], axis=1)
        y_ref[...] = y
        xn_ref[...] = x_ref[...] + _dot(y, w_ref[...])

    return pl.pallas_call(
        body,
        name="outproj",
        grid=(seq // tm,),
        in_specs=[
            pl.BlockSpec((tm, D_MODEL), lambda i: (i, 0)),
            pl.BlockSpec((tm, 256), lambda i: (i, 0)),
            pl.BlockSpec((tm, 256), lambda i: (i, 0)),
            pl.BlockSpec((tm, 512), lambda i: (i, 0)),
            pl.BlockSpec((tm, 512), lambda i: (i, 7)),
            _full((D_MODEL, D_MODEL)),
        ],
        out_specs=[pl.BlockSpec((tm, D_MODEL), lambda i: (i, 0)), pl.BlockSpec((tm, D_MODEL), lambda i: (i, 0))],
        out_shape=[jax.ShapeDtypeStruct((seq, D_MODEL), F32), jax.ShapeDtypeStruct((seq, D_MODEL), BF16)],
        compiler_params=_cparams("parallel"),
    )(x, ya, yb, o, proj, wo)


def outproj_bwd(dx, y, wo):
    seq = dx.shape[0]
    ts = min(seq, 512)

    def body(dx_ref, y_ref, w_ref, dy_ref, dw_ref):
        @pl.when(pl.program_id(0) == 0)
        def _():
            dw_ref[...] = jnp.zeros_like(dw_ref)

        dxb = dx_ref[...].astype(BF16)
        dy_ref[...] = lax.dot_general(dxb, w_ref[...], _NT, preferred_element_type=F32)
        dw_ref[...] += lax.dot_general(y_ref[...], dxb, _TN, preferred_element_type=F32)

    return pl.pallas_call(
        body,
        name="outproj_bwd",
        grid=(seq // ts,),
        in_specs=[
            pl.BlockSpec((ts, D_MODEL), lambda i: (i, 0)),
            pl.BlockSpec((ts, D_MODEL), lambda i: (i, 0)),
            _full((D_MODEL, D_MODEL)),
        ],
        out_specs=[pl.BlockSpec((ts, D_MODEL), lambda i: (i, 0)), _full((D_MODEL, D_MODEL))],
        out_shape=[jax.ShapeDtypeStruct((seq, D_MODEL), F32), jax.ShapeDtypeStruct((D_MODEL, D_MODEL), F32)],
        compiler_params=_cparams("arbitrary"),
    )(dx, y, wo)


def inproj_bwd_x(dproj, w, x, g, dx_in):
    seq = x.shape[0]
    tm, tk = min(seq, 512), 1024
    nk = D_INT // tk

    def body(dp_ref, w_ref, x_ref, g_ref, dxin_ref, dx_ref, dg_ref, acc_ref):
        k = pl.program_id(1)

        @pl.when(k == 0)
        def _():
            acc_ref[...] = jnp.zeros_like(acc_ref)

        acc_ref[...] += lax.dot_general(dp_ref[...], w_ref[...], _NT, preferred_element_type=F32)

        @pl.when(k == nk - 1)
        def _():
            @pl.when(pl.program_id(0) == 0)
            def _():
                dg_ref[...] = jnp.zeros_like(dg_ref)

            dh = acc_ref[...]
            g = g_ref[...]
            r, xh = _rms_stats(x_ref[...])
            dg_ref[...] += jnp.sum(dh * xh, axis=0, keepdims=True)
            dx_ref[...] = dxin_ref[...] + _rms_bwd(dh, g, r, xh)

    return pl.pallas_call(
        body,
        name="inproj_bwd_x",
        grid=(seq // tm, nk),
        in_specs=[
            pl.BlockSpec((tm, tk), lambda i, k: (i, k)),
            pl.BlockSpec((D_MODEL, tk), lambda i, k: (0, k)),
            pl.BlockSpec((tm, D_MODEL), lambda i, k: (i, 0)),
            _full((1, D_MODEL)),
            pl.BlockSpec((tm, D_MODEL), lambda i, k: (i, 0)),
        ],
        out_specs=[pl.BlockSpec((tm, D_MODEL), lambda i, k: (i, 0)), _full((1, D_MODEL))],
        out_shape=[jax.ShapeDtypeStruct((seq, D_MODEL), F32), jax.ShapeDtypeStruct((1, D_MODEL), F32)],
        scratch_shapes=[pltpu.VMEM((tm, D_MODEL), F32)],
        compiler_params=_cparams("arbitrary", "arbitrary"),
    )(dproj, w, x, g, dx_in)


def inproj_bwd_w(h, dproj):
    seq = h.shape[0]
    ts, tn = min(seq, 1024), 512

    def body(h_ref, dp_ref, dw_ref):
        @pl.when(pl.program_id(1) == 0)
        def _():
            dw_ref[...] = jnp.zeros_like(dw_ref)

        dw_ref[...] += lax.dot_general(h_ref[...], dp_ref[...], _TN, preferred_element_type=F32)

    return pl.pallas_call(
        body,
        name="inproj_bwd_w",
        grid=(D_INT // tn, seq // ts),
        in_specs=[pl.BlockSpec((ts, D_MODEL), lambda j, s: (s, 0)), pl.BlockSpec((ts, tn), lambda j, s: (s, j))],
        out_specs=pl.BlockSpec((D_MODEL, tn), lambda j, s: (0, j)),
        out_shape=jax.ShapeDtypeStruct((D_MODEL, D_INT), F32),
        compiler_params=_cparams("parallel", "arbitrary"),
    )(h, dproj)


def final_loss(x, g, tgt):
    seq = x.shape[0]
    tm = min(seq, 512)

    def body(x_ref, g_ref, t_ref, dx_ref, dg_ref, loss_ref):
        @pl.when(pl.program_id(0) == 0)
        def _():
            dg_ref[...] = jnp.zeros_like(dg_ref)
            loss_ref[...] = jnp.zeros_like(loss_ref)

        g = g_ref[...]
        r, xh = _rms_stats(x_ref[...])
        err = xh * g - t_ref[...]
        sq = jnp.sum(jnp.sum(err * err, axis=1, keepdims=True), axis=0, keepdims=True)
        loss_ref[...] += jnp.broadcast_to(sq * (0.5 / D_MODEL), loss_ref.shape)
        dout = err * (1.0 / D_MODEL)
        dg_ref[...] += jnp.sum(dout * xh, axis=0, keepdims=True)
        dx_ref[...] = _rms_bwd(dout, g, r, xh)

    return pl.pallas_call(
        body,
        name="final_loss",
        grid=(seq // tm,),
        in_specs=[pl.BlockSpec((tm, D_MODEL), lambda i: (i, 0)), _full((1, D_MODEL)), pl.BlockSpec((tm, D_MODEL), lambda i: (i, 0))],
        out_specs=[pl.BlockSpec((tm, D_MODEL), lambda i: (i, 0)), _full((1, D_MODEL)), _full((8, LANES))],
        out_shape=[jax.ShapeDtypeStruct((seq, D_MODEL), F32), jax.ShapeDtypeStruct((1, D_MODEL), F32), jax.ShapeDtypeStruct((8, LANES), F32)],
        compiler_params=_cparams("arbitrary"),
    )(x, g, tgt)


C_WIDTH = 512
C_HEADS = 8
C_HDIM = 64
C_PAIRS = C_HEADS // 2
C_BQ = 512
C_TAIL = 16
C_KG = 2


def _split3(x):
    hi = x.astype(BF16)
    r = x - hi.astype(F32)
    mid = r.astype(BF16)
    return hi, mid, (r - mid.astype(F32)).astype(BF16)


def _piece_selectors():
    sel = np.zeros((C_HEADS, 3 * LANES, LANES), np.float32)
    for p in range(C_PAIRS):
        for e in range(2):
            for t in range(3):
                sel[2 * p + e, t * LANES + 2 * p + e, 3 * e + t] = -1.0
    return sel


def fox_prep(proj, bf_row):
    seq = proj.shape[0]
    nblk = seq // CHUNK
    tril = jnp.asarray(np.tril(np.ones((CHUNK, CHUNK), np.float32)))
    sel = jnp.asarray(_piece_selectors(), BF16)
    rows_t = CHUNK + C_TAIL

    def body(fl_ref, q_ref, k_ref, v_ref, bf_ref, l_ref, sel_ref, ka_ref, va_ref, vt_ref, kt_ref, qt_ref, qa_ref, carry_ref):
        @pl.when(pl.program_id(0) == 0)
        def _():
            carry_ref[...] = jnp.zeros_like(carry_ref)

        lf = jax.nn.log_sigmoid(fl_ref[:, :LANES] + bf_ref[...])
        c = _dot(l_ref[...], lf, precision=HI) + carry_ref[...]
        carry_ref[...] += jnp.sum(lf, axis=0, keepdims=True)
        c3 = jnp.concatenate(_split3(c), axis=1)
        lane = lax.broadcasted_iota(jnp.int32, (CHUNK, LANES), 1)
        row = lax.broadcasted_iota(jnp.int32, (CHUNK, LANES), 0)
        r16 = lax.broadcasted_iota(jnp.int32, (C_TAIL, 2 * CHUNK), 0)
        l16 = lax.broadcasted_iota(jnp.int32, (C_TAIL, 2 * CHUNK), 1)
        zero = jnp.zeros((CHUNK, LANES), BF16)
        one = jnp.ones((CHUNK, LANES), BF16)

        def by_keys(x, right_a, right_b):
            xb = x.astype(BF16)
            top = jnp.concatenate([jnp.where(lane < C_HDIM, xb, zero), right_a], axis=1)
            return jnp.concatenate([top, jnp.concatenate([jnp.where(lane < C_HDIM, zero, xb), right_b], axis=1)], axis=0)

        def by_lanes(x, tail):
            xt = x.T.astype(BF16)
            main = jnp.concatenate([jnp.where(row < C_HDIM, xt, zero), jnp.where(row < C_HDIM, zero, xt)], axis=1)
            return jnp.concatenate([main, tail], axis=0)

        for p in range(C_PAIRS):
            cols = slice(p * LANES, (p + 1) * LANES)
            q2, k2, v2 = q_ref[:, cols] * (C_HDIM ** -0.5), k_ref[:, cols], v_ref[:, cols]
            negc = [_dot(c3, sel_ref[2 * p + e]).astype(BF16) for e in range(2)]
            ones3 = [jnp.where((lane >= 3 * e) & (lane < 3 * e + 3), one, zero) for e in range(2)]
            tail = jnp.where(((r16 == 2 * p) & (l16 < CHUNK)) | ((r16 == 2 * p + 1) & (l16 >= CHUNK)), 1.0, 0.0).astype(BF16)
            ka_ref[p, 0] = by_keys(k2, negc[0], negc[1])
            va_ref[p, 0] = by_keys(v2, ones3[0], ones3[1])
            kt_ref[p, 0] = by_lanes(k2, tail)
            vt_ref[p, 0] = by_lanes(v2, tail)
            qt_ref[p] = jnp.concatenate([q2.T.astype(BF16), jnp.where(row < 6, one, zero)], axis=0)
            qa_ref[p] = jnp.concatenate([q2.astype(BF16), jnp.where((lane == 2 * p) | (lane == 2 * p + 1), one, zero)], axis=1)

    wide = lambda j: pl.BlockSpec((CHUNK, C_WIDTH), lambda n: (n, j))
    sq = lambda r: pl.BlockSpec((C_PAIRS, 1, r, 2 * CHUNK), lambda n: (0, n, 0, 0))
    return pl.pallas_call(
        body,
        name="fox_prep",
        grid=(nblk,),
        in_specs=[pl.BlockSpec((CHUNK, 256), lambda n: (n, 3)), wide(4), wide(5), wide(6), _full((1, LANES)),
                  _full((CHUNK, CHUNK)), _full((C_HEADS, 3 * LANES, LANES))],
        out_specs=[sq(2 * CHUNK), sq(2 * CHUNK), sq(rows_t), sq(rows_t),
                   pl.BlockSpec((C_PAIRS, 2 * CHUNK, CHUNK), lambda n: (0, 0, n)),
                   pl.BlockSpec((C_PAIRS, CHUNK, 2 * CHUNK), lambda n: (0, n, 0))],
        out_shape=[jax.ShapeDtypeStruct((C_PAIRS, nblk, 2 * CHUNK, 2 * CHUNK), BF16)] * 2
        + [jax.ShapeDtypeStruct((C_PAIRS, nblk, rows_t, 2 * CHUNK), BF16)] * 2
        + [jax.ShapeDtypeStruct((C_PAIRS, 2 * CHUNK, seq), BF16), jax.ShapeDtypeStruct((C_PAIRS, seq, 2 * CHUNK), BF16)],
        scratch_shapes=[pltpu.VMEM((1, LANES), F32)],
        compiler_params=_cparams("arbitrary"),
    )(proj, proj, proj, proj, bf_row, tril, sel)


def _visible(shape, key0, query0):
    key = key0 + (lax.broadcasted_iota(jnp.int32, shape, 0) & (CHUNK - 1))
    return key <= query0 + lax.broadcasted_iota(jnp.int32, shape, 1)


def _rows_ab(a, b, n):
    return jnp.concatenate([jnp.broadcast_to(a, (C_HDIM, n)), jnp.broadcast_to(b, (C_HDIM, n))], axis=0)


def fox_fwd(qt, ka, vt):
    seq = qt.shape[2]
    nblk = seq // CHUNK
    bq = min(C_BQ, seq)
    grp = bq // CHUNK
    rows_t = CHUNK + C_TAIL

    def body(qt_ref, ka_ref, vt_ref, o_ref, lse_ref, acc_ref):
        p, i = pl.program_id(0), pl.program_id(1)
        qtile = qt_ref[0]
        r16 = lax.broadcasted_iota(jnp.int32, (C_TAIL, bq), 0)

        def group(j0, m, masked):
            ma, mb = m
            ss = []
            for g in range(grp):
                s = _dot(ka_ref[0, j0 + g], qtile)
                if masked:
                    s = jnp.where(_visible(s.shape, (j0 + g) * CHUNK, i * bq), s, -jnp.inf)
                ss.append(s)
            na, nb = ma, mb
            for s in ss:
                na = jnp.maximum(na, jnp.max(s[:CHUNK], axis=0, keepdims=True))
                nb = jnp.maximum(nb, jnp.max(s[CHUNK:], axis=0, keepdims=True))
            al_a, al_b = jnp.exp(ma - na), jnp.exp(mb - nb)
            pv = None
            for g, s in enumerate(ss):
                pt = jnp.concatenate([jnp.exp(s[:CHUNK] - na), jnp.exp(s[CHUNK:] - nb)], axis=0).astype(BF16)
                r = _dot(vt_ref[0, j0 + g], pt)
                pv = r if pv is None else pv + r
            tail = jnp.where(r16 == 2 * p, al_a, jnp.where(r16 == 2 * p + 1, al_b, 1.0))
            acc_ref[...] = acc_ref[...] * jnp.concatenate([_rows_ab(al_a, al_b, bq), tail], axis=0) + pv
            return na, nb

        acc_ref[...] = jnp.zeros_like(acc_ref)
        m = (jnp.full((1, bq), -jnp.inf, F32), jnp.full((1, bq), -jnp.inf, F32))
        m = lax.fori_loop(0, i, lambda t, m: group(t * grp, m, False), m)
        ma, mb = group(i * grp, m, True)
        tailv = acc_ref[CHUNK:rows_t, :]
        la = jnp.sum(jnp.where(r16 == 2 * p, tailv, 0.0), axis=0, keepdims=True)
        lb = jnp.sum(jnp.where(r16 == 2 * p + 1, tailv, 0.0), axis=0, keepdims=True)
        o_ref[...] = (acc_ref[0:CHUNK, :] * _rows_ab(1.0 / la, 1.0 / lb, bq)).T
        lse_ref[0, 0:1, :] = ma + jnp.log(la)
        lse_ref[0, 1:2, :] = mb + jnp.log(lb)

    return pl.pallas_call(
        body,
        name="fox_fwd",
        grid=(C_PAIRS, seq // bq),
        in_specs=[
            pl.BlockSpec((1, 2 * CHUNK, bq), lambda p, i: (p, 0, i)),
            pl.BlockSpec((1, nblk, 2 * CHUNK, 2 * CHUNK), lambda p, i: (p, 0, 0, 0)),
            pl.BlockSpec((1, nblk, rows_t, 2 * CHUNK), lambda p, i: (p, 0, 0, 0)),
        ],
        out_specs=[pl.BlockSpec((bq, LANES), lambda p, i: (i, p)), pl.BlockSpec((1, 2, bq), lambda p, i: (p, 0, i))],
        out_shape=[jax.ShapeDtypeStruct((seq, C_WIDTH), F32), jax.ShapeDtypeStruct((C_PAIRS, 2, seq), F32)],
        scratch_shapes=[pltpu.VMEM((rows_t, bq), F32)],
        compiler_params=_cparams("parallel", "arbitrary"),
    )(qt, ka, vt)


def fox_bwd_prep(dy, o, proj):
    seq = o.shape[0]
    ind = np.zeros((C_WIDTH, LANES), np.float32)
    for h in range(C_HEADS):
        ind[h * C_HDIM:(h + 1) * C_HDIM, h] = 1.0
    ind = jnp.asarray(ind, BF16)
    sel = _piece_selectors()
    sel = jnp.asarray(np.stack([sel[2 * p].T + sel[2 * p + 1].T for p in range(C_PAIRS)]), BF16)

    def body(dy_ref, o_ref, z_ref, ind_ref, sel_ref, do_ref, dz_ref, dot_ref):
        dy_c, o_v, z = dy_ref[...], o_ref[...], z_ref[...]
        sg = jax.nn.sigmoid(z)
        do = dy_c * (z * sg)
        do_ref[...] = do.astype(BF16)
        dz_ref[...] = (dy_c * o_v * (sg * (1.0 + z * (1.0 - sg)))).astype(BF16)
        prod = do * o_v
        hi = prod.astype(BF16)
        lo = (prod - hi.astype(F32)).astype(BF16)
        delta = _dot(hi, ind_ref[...]) + _dot(lo, ind_ref[...])
        d3 = jnp.concatenate(_split3(delta.T), axis=0)
        for p in range(C_PAIRS):
            tail = _dot(sel_ref[p], d3).astype(BF16)
            dot_ref[p] = jnp.concatenate([do[:, p * LANES:(p + 1) * LANES].T.astype(BF16), tail], axis=0)

    return pl.pallas_call(
        body,
        name="fox_bwd_prep",
        grid=(seq // CHUNK,),
        in_specs=[
            pl.BlockSpec((CHUNK, C_WIDTH), lambda i: (i, 1)),
            pl.BlockSpec((CHUNK, C_WIDTH), lambda i: (i, 0)),
            pl.BlockSpec((CHUNK, C_WIDTH), lambda i: (i, 7)),
            _full((C_WIDTH, LANES)), _full((C_PAIRS, LANES, 3 * LANES)),
        ],
        out_specs=[
            pl.BlockSpec((CHUNK, C_WIDTH), lambda i: (i, 0)),
            pl.BlockSpec((CHUNK, C_WIDTH), lambda i: (i, 0)),
            pl.BlockSpec((C_PAIRS, 2 * CHUNK, CHUNK), lambda i: (0, 0, i)),
        ],
        out_shape=[jax.ShapeDtypeStruct((seq, C_WIDTH), BF16)] * 2 + [jax.ShapeDtypeStruct((C_PAIRS, 2 * CHUNK, seq), BF16)],
        compiler_params=_cparams("parallel"),
    )(dy, o, proj, ind, sel)


def fox_bwd(ka, va, kt, qt, dot_t, qa, dob, lse):
    seq = qt.shape[2]
    nblk = seq // CHUNK
    bq = min(C_BQ, seq)
    nq = seq // bq
    kg = min(C_KG, nblk)
    ng = nblk // kg
    rows_t = CHUNK + C_TAIL

    def body(ka_ref, va_ref, kt_ref, qt_ref, dot_ref, qa_ref, do_ref, lse_ref,
             dq_ref, dk_ref, dv_ref, dck_ref, dcq_ref, dqt_acc, dv_acc, dka_acc):
        p, jg = pl.program_id(0), pl.program_id(1)

        @pl.when(jg == 0)
        def _():
            dqt_acc[...] = jnp.zeros_like(dqt_acc)

        dv_acc[...] = jnp.zeros_like(dv_acc)
        dka_acc[...] = jnp.zeros_like(dka_acc)

        def step(i, carry, masked):
            cols = pl.ds(pl.multiple_of(i * bq, bq), bq)
            qtile, dotile = qt_ref[0, :, cols], dot_ref[0, :, cols]
            do, qa_i = do_ref[cols, :], qa_ref[0, cols, :]
            lse2 = jnp.concatenate([jnp.broadcast_to(lse_ref[0, 0:1, cols], (CHUNK, bq)),
                                    jnp.broadcast_to(lse_ref[0, 1:2, cols], (CHUNK, bq))], axis=0)
            for kb in range(kg):
                pt = jnp.exp(_dot(ka_ref[0, kb], qtile) - lse2)
                if masked:
                    pt = jnp.where(_visible(pt.shape, (jg * kg + kb) * CHUNK, i * bq), pt, 0.0)
                ds = pt * _dot(va_ref[0, kb], dotile)
                ptb, dsb = pt.astype(BF16), ds.astype(BF16)
                dv_acc[kb] += _dot(ptb, do)
                dka_acc[kb] += _dot(dsb, qa_i)
                dqt_acc[:, cols] += _dot(kt_ref[0, kb], dsb)
            return carry

        i0 = (jg * kg * CHUNK) // bq
        step(i0, 0, True)
        lax.fori_loop(i0 + 1, nq, functools.partial(step, masked=False), 0)
        lane = lax.broadcasted_iota(jnp.int32, (CHUNK, LANES), 1)
        for kb in range(kg):
            rows = slice(kb * CHUNK, (kb + 1) * CHUNK)
            dk_ref[rows, :] = jnp.where(lane < C_HDIM, dka_acc[kb, 0:CHUNK, 0:LANES], dka_acc[kb, CHUNK:, 0:LANES]).astype(BF16)
            dv_ref[rows, :] = jnp.where(lane < C_HDIM, dv_acc[kb, 0:CHUNK, :], dv_acc[kb, CHUNK:, :]).astype(BF16)
            dck_ref[0, rows, :] = (jnp.where(lane == 2 * p, dka_acc[kb, 0:CHUNK, LANES:], 0.0)
                                   + jnp.where(lane == 2 * p + 1, dka_acc[kb, CHUNK:, LANES:], 0.0))

        @pl.when(jg == ng - 1)
        def _():
            for c in range(nq):
                dq_ref[c * bq:(c + 1) * bq, :] = (dqt_acc[0:CHUNK, c * bq:(c + 1) * bq].T * (C_HDIM ** -0.5)).astype(BF16)
            dcq_ref[0] = dqt_acc[CHUNK:rows_t, :]

    per_pair = lambda r, c: pl.BlockSpec((1, r, c), lambda p, j: (p, 0, 0))
    keys4 = lambda r: pl.BlockSpec((1, kg, r, 2 * CHUNK), lambda p, j: (p, j, 0, 0))
    return pl.pallas_call(
        body,
        name="fox_bwd",
        grid=(C_PAIRS, ng),
        in_specs=[keys4(2 * CHUNK), keys4(2 * CHUNK), keys4(rows_t), per_pair(2 * CHUNK, seq), per_pair(2 * CHUNK, seq),
                  per_pair(seq, 2 * CHUNK), pl.BlockSpec((seq, LANES), lambda p, j: (0, p)), per_pair(2, seq)],
        out_specs=[pl.BlockSpec((seq, LANES), lambda p, j: (0, p)),
                   pl.BlockSpec((kg * CHUNK, LANES), lambda p, j: (j, p)),
                   pl.BlockSpec((kg * CHUNK, LANES), lambda p, j: (j, p)),
                   pl.BlockSpec((1, kg * CHUNK, LANES), lambda p, j: (p, j, 0)),
                   per_pair(C_TAIL, seq)],
        out_shape=[jax.ShapeDtypeStruct((seq, C_WIDTH), BF16)] * 3
        + [jax.ShapeDtypeStruct((C_PAIRS, seq, LANES), F32), jax.ShapeDtypeStruct((C_PAIRS, C_TAIL, seq), F32)],
        scratch_shapes=[pltpu.VMEM((rows_t, seq), F32), pltpu.VMEM((kg, 2 * CHUNK, LANES), F32),
                        pltpu.VMEM((kg, 2 * CHUNK, 2 * CHUNK), F32)],
        compiler_params=_cparams("parallel", "arbitrary"),
    )(ka, va, kt, qt, dot_t, qa, dob, lse)


def fox_post(dcq, dck, proj, bf_row):
    seq = proj.shape[0]
    nc = seq // CHUNK
    triu = jnp.asarray(np.triu(np.ones((CHUNK, CHUNK), np.float32)))

    def body(dq_ref, dk_ref, fl_ref, bf_ref, u_ref, dfl_ref, dbf_ref, carry_ref):
        @pl.when(pl.program_id(0) == 0)
        def _():
            carry_ref[...] = jnp.zeros_like(carry_ref)
            dbf_ref[...] = jnp.zeros_like(dbf_ref)

        rows = (dq_ref[0] + dq_ref[1]) + (dq_ref[2] + dq_ref[3])
        dc = jnp.concatenate([rows, jnp.zeros((CHUNK - C_TAIL, CHUNK), F32)], axis=0).T
        dc = dc - ((dk_ref[0] + dk_ref[1]) + (dk_ref[2] + dk_ref[3]))
        g = _dot(u_ref[...], dc, precision=HI) + carry_ref[...]
        carry_ref[...] += jnp.sum(dc, axis=0, keepdims=True)
        dfl = g * jax.nn.sigmoid(-(fl_ref[:, :LANES] + bf_ref[...]))
        dbf_ref[...] += jnp.sum(dfl, axis=0, keepdims=True)
        dfl_ref[...] = jnp.concatenate([dfl, jnp.zeros_like(dfl)], axis=1).astype(BF16)

    rev = lambda n: nc - 1 - n
    return pl.pallas_call(
        body,
        name="fox_post",
        grid=(nc,),
        in_specs=[
            pl.BlockSpec((C_PAIRS, C_TAIL, CHUNK), lambda n: (0, 0, rev(n))),
            pl.BlockSpec((C_PAIRS, CHUNK, LANES), lambda n: (0, rev(n), 0)),
            pl.BlockSpec((CHUNK, 256), lambda n: (rev(n), 3)),
            _full((1, LANES)), _full((CHUNK, CHUNK)),
        ],
        out_specs=[pl.BlockSpec((CHUNK, 256), lambda n: (rev(n), 0)), _full((1, LANES))],
        out_shape=[jax.ShapeDtypeStruct((seq, 256), BF16), jax.ShapeDtypeStruct((1, LANES), F32)],
        scratch_shapes=[pltpu.VMEM((1, LANES), F32)],
        compiler_params=_cparams("arbitrary"),
    )(dcq, dck, proj, bf_row, triu)


N_DEV = 8
MESH = pl.DeviceIdType.MESH
_ANY = pl.BlockSpec(memory_space=pl.ANY)


def _mesh_pos():
    return lax.axis_index("x"), lax.axis_index("y"), lax.axis_index("c")


def _dev_index(px, py, pc):
    return 4 * px + 2 * py + pc


def allgather_weights(wi, wo):
    def body(wi_ref, wo_ref, wi_all, wo_all, send_sems, recv_sems, local_sems):
        x, y, c = _mesh_pos()
        me, sibling = (x, y, c), (x, y, 1 - c)
        chips = [(1 - x, y), (x, 1 - y), (1 - x, 1 - y)]
        arrays = ((wi_ref, wi_all), (wo_ref, wo_all))

        def copy(a, k, block, to, own=False):
            src, out = arrays[a]
            slot = out.at[_dev_index(*block)]
            return pltpu.make_async_remote_copy(
                src_ref=src if own else slot, dst_ref=slot, send_sem=send_sems.at[a, k], recv_sem=recv_sems.at[a, k],
                device_id=to, device_id_type=MESH)

        both = range(len(arrays))
        mine = [pltpu.make_async_copy(arrays[a][0], arrays[a][1].at[_dev_index(*me)], local_sems.at[a]) for a in both]
        for cp in mine:
            cp.start()
        first = [copy(a, 0, me, sibling, own=True) for a in both]
        first += [copy(a, 1 + j, me, (*chip, c), own=True) for j, chip in enumerate(chips) for a in both]
        for cp in first:
            cp.start()
        passed = [copy(a, 4 + j, (*chip, c), sibling) for j, chip in enumerate(chips) for a in both]
        for j, chip in enumerate(chips):
            for a in both:
                copy(a, 1 + j, (*chip, c), me).wait_recv()
            for a in both:
                passed[2 * j + a].start()
        for a in both:
            copy(a, 0, sibling, me).wait_recv()
        for j, chip in enumerate(chips):
            for a in both:
                copy(a, 4 + j, (*chip, 1 - c), me).wait_recv()
        for cp in first + passed:
            cp.wait_send()
        for cp in mine:
            cp.wait()

    return pl.pallas_call(
        body,
        name="allgather_weights",
        in_specs=[_ANY, _ANY],
        out_specs=[_ANY, _ANY],
        out_shape=[jax.ShapeDtypeStruct((N_DEV,) + wi.shape, wi.dtype), jax.ShapeDtypeStruct((N_DEV,) + wo.shape, wo.dtype)],
        scratch_shapes=[pltpu.SemaphoreType.DMA((2, 7)), pltpu.SemaphoreType.DMA((2, 7)), pltpu.SemaphoreType.DMA((2,))],
    )(wi, wo)


def exchange_grads(gwi, gwo, gsm):
    def body(gwi_ref, gwo_ref, gsm_ref, rwi, rwo, rsm, send_sems, recv_sems, local_sems):
        x, y, c = _mesh_pos()
        me = _dev_index(x, y, c)
        srcs, outs = (gwi_ref, gwo_ref, gsm_ref), (rwi, rwo, rsm)
        three = range(3)

        def for_dev(a, dev):
            return srcs[a] if a == 2 else srcs[a].at[dev]

        local = [pltpu.make_async_copy(for_dev(a, me), outs[a].at[me], local_sems.at[a]) for a in three]
        for cp in local:
            cp.start()

        def peer_of(k):
            return x ^ ((k >> 2) & 1), y ^ ((k >> 1) & 1), c ^ (k & 1)

        def copy(a, k, slot):
            p = peer_of(k)
            return pltpu.make_async_remote_copy(
                src_ref=for_dev(a, _dev_index(*p)), dst_ref=outs[a].at[slot], send_sem=send_sems.at[a, k - 1],
                recv_sem=recv_sems.at[a, k - 1], device_id=p, device_id_type=MESH)

        sends = [copy(a, k, me) for k in range(1, N_DEV) for a in three]
        for cp in sends:
            cp.start()
        for k in range(1, N_DEV):
            for a in three:
                copy(a, k, _dev_index(*peer_of(k))).wait_recv()
        for cp in sends:
            cp.wait_send()
        for cp in local:
            cp.wait()

    return pl.pallas_call(
        body,
        name="exchange_grads",
        in_specs=[_ANY, _ANY, _ANY],
        out_specs=[_ANY, _ANY, _ANY],
        out_shape=[jax.ShapeDtypeStruct(gwi.shape, gwi.dtype), jax.ShapeDtypeStruct(gwo.shape, gwo.dtype),
                   jax.ShapeDtypeStruct((N_DEV,) + gsm.shape, gsm.dtype)],
        scratch_shapes=[pltpu.SemaphoreType.DMA((3, 7)), pltpu.SemaphoreType.DMA((3, 7)), pltpu.SemaphoreType.DMA((3,))],
    )(gwi, gwo, gsm)


ADAM_LR = 0.001
ADAM_B1 = 0.9
ADAM_B2 = 0.999
ADAM_EPS = 1e-08
ADAM_WD = 0.01
ADAM_STEP = 10


def adam_reduce(parts, w, m, v, rows, name):
    n_l, n_r, n_c = w.shape

    def body(p_ref, w_ref, m_ref, v_ref, g_ref, d_ref, m2_ref, v2_ref):
        g = p_ref[0, 0]
        for d in range(1, N_DEV):
            g = g + p_ref[d, 0]
        m2 = ADAM_B1 * m_ref[0] + (1.0 - ADAM_B1) * g
        v2 = ADAM_B2 * v_ref[0] + (1.0 - ADAM_B2) * (g * g)
        m_hat = m2 / (1.0 - ADAM_B1 ** ADAM_STEP)
        v_hat = v2 / (1.0 - ADAM_B2 ** ADAM_STEP)
        g_ref[0] = g
        d_ref[0] = -ADAM_LR * (m_hat / (jnp.sqrt(v_hat) + ADAM_EPS) + ADAM_WD * w_ref[0])
        m2_ref[0] = m2
        v2_ref[0] = v2

    blk = lambda: pl.BlockSpec((1, rows, n_c), lambda l, r: (l, r, 0))
    return pl.pallas_call(
        body,
        name=name,
        grid=(n_l, n_r // rows),
        in_specs=[pl.BlockSpec((N_DEV, 1, rows, n_c), lambda l, r: (0, l, r, 0)), blk(), blk(), blk()],
        out_specs=[blk(), blk(), blk(), blk()],
        out_shape=[jax.ShapeDtypeStruct(w.shape, F32)] * 4,
        compiler_params=_cparams("parallel", "parallel"),
    )(parts, w, m, v)


_SMALL = (("norm_g", (2, 1024)), ("gmlp_ln_g", (2, 4, 64)), ("gmlp_ln_b", (2, 4, 64)), ("gmlp_w_s", (2, 4, 128, 128)),
          ("gmlp_b_s", (2, 4, 128)), ("hgrn_lb", (2, 256)), ("hgrn_onorm_g", (2, 64)), ("fox_b_f", (2, 8)),
          ("final_norm_g", (1024,)), ("loss", ()))


def _padded(n):
    return -(-n // LANES) * LANES


_SMALL_ROWS = -(-sum(_padded(int(np.prod(s))) for _, s in _SMALL) // LANES // 8) * 8


def _pack_small(vals):
    flat = []
    for (name, shape), a in zip(_SMALL, vals, strict=True):
        n = int(np.prod(shape))
        flat.append(jnp.pad(a.reshape(n).astype(F32), (0, _padded(n) - n)))
    flat = jnp.concatenate(flat)
    return jnp.pad(flat, (0, _SMALL_ROWS * LANES - flat.shape[0])).reshape(_SMALL_ROWS, LANES)


def _unpack_small(slab):
    flat, out, at = slab.reshape(-1), {}, 0
    for name, shape in _SMALL:
        n = int(np.prod(shape))
        out[name] = flat[at:at + n].reshape(shape)
        at += _padded(n)
    return out


def kernel(x, norm_g, w_in, w_out, gmlp_ln_g, gmlp_ln_b, gmlp_w_s, gmlp_b_s, hgrn_lb, hgrn_onorm_g, fox_b_f, final_norm_g, loss_target, m_norm_g, m_w_in, m_w_out, m_gmlp_ln_g, m_gmlp_ln_b, m_gmlp_w_s, m_gmlp_b_s, m_hgrn_lb, m_hgrn_onorm_g, m_fox_b_f, m_final_norm_g, v_norm_g, v_w_in, v_w_out, v_gmlp_ln_g, v_gmlp_ln_b, v_gmlp_w_s, v_gmlp_b_s, v_hgrn_lb, v_hgrn_onorm_g, v_fox_b_f, v_final_norm_g):
    depth = w_in.shape[0]
    seq = x.shape[1]
    n_in = w_in.shape[2] * N_DEV
    xs, tgt = x[0], loss_target[0]

    wi_all, wo_all = allgather_weights(w_in.astype(BF16), w_out.astype(BF16))
    wi_full = jnp.transpose(wi_all, (1, 2, 0, 3)).reshape(depth, D_MODEL, n_in)
    wi_int = jnp.concatenate(
        [wi_full[:, :, :768], wi_full[:, :, 3840:n_in], jnp.zeros((depth, D_MODEL, 1024 - 768 - (n_in - 3840)), BF16),
         wi_full[:, :, 768:3840]], axis=2)
    wo_full = jnp.transpose(wo_all, (1, 0, 2, 3)).reshape(depth, D_MODEL, D_MODEL)

    ln_g = gmlp_ln_g.reshape(depth, 1, A_WIDTH)
    ln_b = gmlp_ln_b.reshape(depth, 1, A_WIDTH)
    bs_t = jnp.pad(jnp.transpose(gmlp_b_s, (0, 2, 1)), ((0, 0), (0, 0), (0, LANES - A_GROUPS)))
    lb0, lb1 = hgrn_lb[0:1], hgrn_lb[1:2]
    onorm = jnp.tile(hgrn_onorm_g, (1, B_HEADS)).reshape(depth, 1, B_WIDTH)
    bf_row = jnp.pad(fox_b_f, ((0, 0), (0, LANES - C_HEADS))).reshape(depth, 1, LANES)

    saved = []
    xc = xs
    for l in range(depth):
        proj, h = inproj(xc, norm_g[l:l + 1], wi_int[l])
        ya = gmlp_fwd(proj, ln_g[l], ln_b[l], gmlp_w_s[l], bs_t[l])
        yb, states = hgrn_fwd(proj, lb0, lb1, onorm[l], l)
        ka, va, vt, kt, qt, qa = fox_prep(proj, bf_row[l])
        o, lse = fox_fwd(qt, ka, vt)
        xn, yfull = outproj(xc, ya, yb, o, proj, wo_full[l])
        saved.append((xc, proj, h, states, ka, va, kt, qt, qa, o, lse, yfull))
        xc = xn

    dx, d_final_g, loss_tile = final_loss(xc, final_norm_g[None], tgt)

    g_norm, g_wi, g_wo = [None] * depth, [None] * depth, [None] * depth
    g_ln_g, g_ln_b, g_ws, g_bs, g_on, g_bf = ([None] * depth for _ in range(6))
    g_lb0, g_lb1 = jnp.zeros_like(lb0), jnp.zeros_like(lb1)
    for l in reversed(range(depth)):
        x_in, proj, h, states, ka, va, kt, qt, qa, o, lse, yfull = saved[l]
        dy, g_wo[l] = outproj_bwd(dx, yfull, wo_full[l])
        d_a, g_ln_g[l], g_ln_b[l], g_ws[l], dbs_t = gmlp_bwd(proj, dy, ln_g[l], ln_b[l], gmlp_w_s[l], bs_t[l])
        g_bs[l] = dbs_t[:, :A_GROUPS].T
        d_b, d0, d1, don = hgrn_bwd(proj, states, dy, lb0, lb1, onorm[l], l)
        g_lb0, g_lb1 = g_lb0 + d0, g_lb1 + d1
        g_on[l] = don.reshape(B_HEADS, B_KDIM).sum(0)
        dob, d_z, dot_t = fox_bwd_prep(dy, o, proj)
        d_q, d_k, d_v, dck, dcq = fox_bwd(ka, va, kt, qt, dot_t, qa, dob, lse)
        d_fl, dbf = fox_post(dcq, dck, proj, bf_row[l])
        g_bf[l] = dbf[0, :C_HEADS]
        dproj = jnp.concatenate([d_a, d_fl, d_b, d_q, d_k, d_v, d_z], axis=1)
        dx, g_norm[l] = inproj_bwd_x(dproj, wi_int[l], x_in, norm_g[l:l + 1], dx)
        dwi = inproj_bwd_w(h, dproj)
        g_wi[l] = jnp.concatenate([dwi[:, :768], dwi[:, 1024:], dwi[:, 768:768 + n_in - 3840]], axis=1)

    gwi = jnp.transpose(jnp.stack(g_wi).reshape(depth, D_MODEL, N_DEV, n_in // N_DEV), (2, 0, 1, 3))
    gwo = jnp.transpose(jnp.stack(g_wo).reshape(depth, N_DEV, D_MODEL // N_DEV, D_MODEL), (1, 0, 2, 3))
    gsm = _pack_small([
        jnp.concatenate(g_norm), jnp.stack(g_ln_g), jnp.stack(g_ln_b), jnp.stack(g_ws), jnp.stack(g_bs),
        jnp.concatenate([g_lb0, g_lb1]), jnp.stack(g_on), jnp.stack(g_bf), d_final_g, loss_tile[0, 0]])
    rwi, rwo, rsm = exchange_grads(gwi, gwo, gsm)

    small_w = (norm_g, gmlp_ln_g, gmlp_ln_b, gmlp_w_s, gmlp_b_s, hgrn_lb, hgrn_onorm_g, fox_b_f, final_norm_g)
    small_m = (m_norm_g, m_gmlp_ln_g, m_gmlp_ln_b, m_gmlp_w_s, m_gmlp_b_s, m_hgrn_lb, m_hgrn_onorm_g, m_fox_b_f, m_final_norm_g)
    small_v = (v_norm_g, v_gmlp_ln_g, v_gmlp_ln_b, v_gmlp_w_s, v_gmlp_b_s, v_hgrn_lb, v_hgrn_onorm_g, v_fox_b_f, v_final_norm_g)
    zero = jnp.zeros((), F32)
    res_wi = adam_reduce(rwi, w_in, m_w_in, v_w_in, 256, "adam_w_in")
    res_wo = adam_reduce(rwo, w_out, m_w_out, v_w_out, w_out.shape[1], "adam_w_out")
    res_sm = adam_reduce(rsm[:, None], _pack_small(small_w + (zero,))[None], _pack_small(small_m + (zero,))[None],
                         _pack_small(small_v + (zero,))[None], _SMALL_ROWS, "adam_small")
    res_sm = [_unpack_small(r[0]) for r in res_sm]

    def group(i):
        s = res_sm[i]
        return [s["norm_g"], res_wi[i], res_wo[i], s["gmlp_ln_g"], s["gmlp_ln_b"], s["gmlp_w_s"], s["gmlp_b_s"],
                s["hgrn_lb"], s["hgrn_onorm_g"], s["fox_b_f"], s["final_norm_g"]]

    return (res_sm[0]["loss"], dx[None], *group(0), *group(1), *group(2), *group(3))
```

```python
import functools

import jax
import jax.numpy as jnp
import numpy as np
from jax import lax
from jax.experimental import pallas as pl
from jax.experimental.pallas import tpu as pltpu

F32 = jnp.float32
BF16 = jnp.bfloat16
HI = lax.Precision.HIGHEST

NORM_EPS = 1e-6
F_FLOOR = 1e-30
CHUNK = 128
LANES = 128
VMEM_LIMIT = 56 * 1024 * 1024


def _cparams(*sem):
    return pltpu.CompilerParams(dimension_semantics=sem, vmem_limit_bytes=VMEM_LIMIT)


def _dot(a, b, dims=(((1,), (0,)), ((), ())), precision=None):
    return lax.dot_general(a, b, dims, precision=precision, preferred_element_type=F32)


_NT = (((1,), (1,)), ((), ()))
_TN = (((0,), (0,)), ((), ()))


def _bd(a, b):
    return _dot(a.astype(BF16), b.astype(BF16))


def _group_mean_matrix(width, group):
    idx = np.arange(width) // group
    return jnp.asarray((idx[:, None] == idx[None, :]).astype(np.float32) / group)


def _group_ones_matrix(width, group):
    idx = np.arange(width) // group
    return jnp.asarray((idx[:, None] == idx[None, :]).astype(np.float32))


A_WIDTH = 256
A_GROUPS = 4
A_GDIM = 64


def _gmlp_chunk(x3, ln_g, ln_b, w_s, bs_t, mean_m, gind):
    u = jax.nn.gelu(x3[:, :A_WIDTH])
    v = jax.nn.gelu(x3[:, A_WIDTH:2 * A_WIDTH])
    z = x3[:, 2 * A_WIDTH:]
    mu = _dot(v, mean_m, precision=HI)
    d = v - mu
    var = _dot(d * d, mean_m, precision=HI)
    vn = d * lax.rsqrt(var + NORM_EPS) * ln_g + ln_b
    vnb = vn.astype(BF16)
    row = lax.broadcasted_iota(jnp.int32, (CHUNK, CHUNK), 0)
    col = lax.broadcasted_iota(jnp.int32, (CHUNK, CHUNK), 1)
    causal = row >= col
    lane_g = lax.shift_right_logical(lax.broadcasted_iota(jnp.int32, (CHUNK, A_WIDTH), 1), 6)
    mixed = _dot(bs_t, gind, precision=HI)
    for g in range(A_GROUPS):
        wc = jnp.where(causal, w_s[g], 0.0).astype(BF16)
        mixed = mixed + jnp.where(lane_g == g, _dot(wc, vnb), 0.0)
    return u * mixed * jax.nn.silu(z)


def _gmlp_consts():
    gind = np.zeros((LANES, A_WIDTH), np.float32)
    for g in range(A_GROUPS):
        gind[g, g * A_GDIM:(g + 1) * A_GDIM] = 1.0
    return _group_mean_matrix(A_WIDTH, A_GDIM), jnp.asarray(gind)


def _full(shape):
    return pl.BlockSpec(shape, lambda *_: (0,) * len(shape))


def gmlp_fwd(proj, ln_g, ln_b, w_s, bs_t):
    seq = proj.shape[0]
    mean_m, gind = _gmlp_consts()

    def body(x_ref, g_ref, b_ref, w_ref, bs_ref, m_ref, gi_ref, y_ref):
        y = _gmlp_chunk(x_ref[...], g_ref[...], b_ref[...], w_ref[...], bs_ref[...], m_ref[...], gi_ref[...])
        y_ref[...] = y.astype(BF16)

    return pl.pallas_call(
        body,
        name="gmlp_fwd",
        grid=(seq // CHUNK,),
        in_specs=[
            pl.BlockSpec((CHUNK, 3 * A_WIDTH), lambda n: (n, 0)),
            _full((1, A_WIDTH)), _full((1, A_WIDTH)), _full((A_GROUPS, CHUNK, CHUNK)), _full((CHUNK, LANES)),
            _full((A_WIDTH, A_WIDTH)), _full((LANES, A_WIDTH)),
        ],
        out_specs=pl.BlockSpec((CHUNK, A_WIDTH), lambda n: (n, 0)),
        out_shape=jax.ShapeDtypeStruct((seq, A_WIDTH), BF16),
        compiler_params=_cparams("parallel"),
    )(proj, ln_g, ln_b, w_s, bs_t, mean_m, gind)


def gmlp_bwd(proj, dy, ln_g, ln_b, w_s, bs_t):
    seq = proj.shape[0]
    mean_m, gind = _gmlp_consts()

    def body(x_ref, dy_ref, g_ref, b_ref, w_ref, bs_ref, m_ref, gi_ref, dx_ref, dg_ref, db_ref, dw_ref, dbs_ref):
        fn = functools.partial(_gmlp_chunk, mean_m=m_ref[...], gind=gi_ref[...])
        _, vjp = jax.vjp(fn, x_ref[...], g_ref[...], b_ref[...], w_ref[...], bs_ref[...])
        dx, dg, db, dw, dbs = vjp(dy_ref[...])
        dx_ref[...] = dx.astype(BF16)

        @pl.when(pl.program_id(0) == 0)
        def _():
            dg_ref[...] = jnp.zeros_like(dg_ref)
            db_ref[...] = jnp.zeros_like(db_ref)
            dw_ref[...] = jnp.zeros_like(dw_ref)
            dbs_ref[...] = jnp.zeros_like(dbs_ref)

        dg_ref[...] += dg
        db_ref[...] += db
        dw_ref[...] += dw
        dbs_ref[...] += dbs

    return pl.pallas_call(
        body,
        name="gmlp_bwd",
        grid=(seq // CHUNK,),
        in_specs=[
            pl.BlockSpec((CHUNK, 3 * A_WIDTH), lambda n: (n, 0)),
            pl.BlockSpec((CHUNK, A_WIDTH), lambda n: (n, 0)),
            _full((1, A_WIDTH)), _full((1, A_WIDTH)), _full((A_GROUPS, CHUNK, CHUNK)), _full((CHUNK, LANES)),
            _full((A_WIDTH, A_WIDTH)), _full((LANES, A_WIDTH)),
        ],
        out_specs=[
            pl.BlockSpec((CHUNK, 3 * A_WIDTH), lambda n: (n, 0)),
            _full((1, A_WIDTH)), _full((1, A_WIDTH)), _full((A_GROUPS, CHUNK, CHUNK)), _full((CHUNK, LANES)),
        ],
        out_shape=[
            jax.ShapeDtypeStruct((seq, 3 * A_WIDTH), BF16),
            jax.ShapeDtypeStruct((1, A_WIDTH), F32), jax.ShapeDtypeStruct((1, A_WIDTH), F32),
            jax.ShapeDtypeStruct((A_GROUPS, CHUNK, CHUNK), F32), jax.ShapeDtypeStruct((CHUNK, LANES), F32),
        ],
        compiler_params=_cparams("arbitrary"),
    )(proj, dy, ln_g, ln_b, w_s, bs_t, mean_m, gind)


B_WIDTH = 256
B_HEADS = 4
B_KDIM = 64
B_LEVELS = (64, 32, 16, 8, 4, 2, 1)


def _hgrn_consts():
    t = np.arange(CHUNK)
    u = t[None, :]
    mats = [np.tril(np.ones((CHUNK, CHUNK), np.float32))]
    for m in B_LEVELS:
        p = (t // (2 * m)) * (2 * m) + m - 1
        right = (t % (2 * m)) >= m
        sel = np.where(right[:, None], (u > p[:, None]) & (u <= t[:, None]), (u > t[:, None]) & (u <= p[:, None]))
        mats.append(sel.astype(np.float32))
    return jnp.asarray(np.concatenate(mats, 0)), _group_ones_matrix(B_WIDTH, B_KDIM)


def _hgrn_lower_bound(lb0, lb1, layer):
    mx = jnp.maximum(lb0, lb1)
    e0 = jnp.exp(lb0 - mx)
    e1 = jnp.exp(lb1 - mx)
    p0 = e0 / (e0 + e1)
    p1 = e1 / (e0 + e1)
    cs = p0 if layer == 0 else p0 + p1
    return jnp.clip(cs - p0, 0.0, 1.0 - 1e-6)


def _hgrn_chunk(x4, st, lb0, lb1, onorm, layer, tstack, ones_bd):
    q_raw, fl, v, zg = (x4[:, i * B_WIDTH:(i + 1) * B_WIDTH] for i in range(4))
    lb = _hgrn_lower_bound(lb0, lb1, layer)
    q = jax.nn.silu(q_raw) * (B_KDIM ** -0.5)
    f = lb + (1.0 - lb) * jax.nn.sigmoid(fl)
    logf = jnp.log(jnp.maximum(f, F_FLOOR))
    k = (1.0 - lb) * jax.nn.sigmoid(-fl)
    dall = _dot(tstack, logf, precision=HI)
    b = dall[:CHUNK]
    b_last = jnp.sum(logf, axis=0, keepdims=True)
    vb = v.astype(BF16)

    lane_h = lax.shift_right_logical(lax.broadcasted_iota(jnp.int32, (CHUNK, B_WIDTH), 1), 6)
    row = lax.broadcasted_iota(jnp.int32, (CHUNK, B_WIDTH), 0)
    srow = lax.broadcasted_iota(jnp.int32, (B_HEADS * CHUNK, CHUNK), 0) & (CHUNK - 1)
    scol = lax.broadcasted_iota(jnp.int32, (B_HEADS * CHUNK, CHUNK), 1)

    def heads_on_rows(a):
        return jnp.concatenate([jnp.where(lane_h == h, a, 0.0) for h in range(B_HEADS)], axis=0)

    def heads_from_rows(r):
        out = jnp.where(lane_h == 0, r[:CHUNK], 0.0)
        for h in range(1, B_HEADS):
            out = out + jnp.where(lane_h == h, r[h * CHUNK:(h + 1) * CHUNK], 0.0)
        return out

    o = lax.dot_general((q * jnp.exp(b)).astype(BF16), st.astype(BF16), _NT, preferred_element_type=F32)
    scores = jnp.zeros((B_HEADS * CHUNK, CHUNK), F32)
    for li, m in enumerate(B_LEVELS):
        e = jnp.exp(dall[(li + 1) * CHUNK:(li + 2) * CHUNK])
        right = (row & (2 * m - 1)) >= m
        qt = jnp.where(right, q * e, 0.0)
        kt = jnp.where(right, 0.0, k * e)
        sc = lax.dot_general(heads_on_rows(qt).astype(BF16), kt.astype(BF16), _NT, preferred_element_type=F32)
        sh = int(np.log2(2 * m))
        same = lax.shift_right_logical(srow, sh) == lax.shift_right_logical(scol, sh)
        scores = scores + jnp.where(same, sc, 0.0)
    o = o + heads_from_rows(_dot(scores.astype(BF16), vb))
    o = o + _dot(q * k, ones_bd, precision=HI) * v

    kv = lax.dot_general(vb, (k * jnp.exp(b_last - b)).astype(BF16), _TN, preferred_element_type=F32)
    st_new = st * jnp.exp(b_last) + jnp.where(ones_bd > 0.5, kv, 0.0)

    ms = _dot(o * o, ones_bd, precision=HI) * (1.0 / B_KDIM)
    y = o * lax.rsqrt(ms + NORM_EPS) * onorm * jax.nn.silu(zg)
    return y, st_new


def hgrn_fwd(proj, lb0, lb1, onorm, layer):
    seq = proj.shape[0]
    nc = seq // CHUNK
    tstack, ones_bd = _hgrn_consts()

    def body(x_ref, lb0_ref, lb1_ref, on_ref, t_ref, e_ref, y_ref, st_out_ref, st_ref):
        @pl.when(pl.program_id(0) == 0)
        def _():
            st_ref[...] = jnp.zeros_like(st_ref)

        st = st_ref[...]
        st_out_ref[0] = st
        y, st_new = _hgrn_chunk(x_ref[...], st, lb0_ref[...], lb1_ref[...], on_ref[...], layer, t_ref[...], e_ref[...])
        y_ref[...] = y.astype(BF16)
        st_ref[...] = st_new

    return pl.pallas_call(
        body,
        name=f"hgrn_fwd_{layer}",
        grid=(nc,),
        in_specs=[
            pl.BlockSpec((CHUNK, 4 * B_WIDTH), lambda n: (n, 1)),
            _full((1, B_WIDTH)), _full((1, B_WIDTH)), _full((1, B_WIDTH)),
            _full(((len(B_LEVELS) + 1) * CHUNK, CHUNK)), _full((B_WIDTH, B_WIDTH)),
        ],
        out_specs=[
            pl.BlockSpec((CHUNK, B_WIDTH), lambda n: (n, 0)),
            pl.BlockSpec((1, B_WIDTH, B_WIDTH), lambda n: (n, 0, 0)),
        ],
        out_shape=[jax.ShapeDtypeStruct((seq, B_WIDTH), BF16), jax.ShapeDtypeStruct((nc, B_WIDTH, B_WIDTH), F32)],
        scratch_shapes=[pltpu.VMEM((B_WIDTH, B_WIDTH), F32)],
        compiler_params=_cparams("arbitrary"),
    )(proj, lb0, lb1, onorm, tstack, ones_bd)


def hgrn_bwd(proj, states, dy, lb0, lb1, onorm, layer):
    seq = proj.shape[0]
    nc = seq // CHUNK
    tstack, ones_bd = _hgrn_consts()

    def body(x_ref, st_in_ref, dy_ref, lb0_ref, lb1_ref, on_ref, t_ref, e_ref, dx_ref, d0_ref, d1_ref, don_ref, dst_ref):
        @pl.when(pl.program_id(0) == 0)
        def _():
            dst_ref[...] = jnp.zeros_like(dst_ref)
            d0_ref[...] = jnp.zeros_like(d0_ref)
            d1_ref[...] = jnp.zeros_like(d1_ref)
            don_ref[...] = jnp.zeros_like(don_ref)

        fn = functools.partial(_hgrn_chunk, layer=layer, tstack=t_ref[...], ones_bd=e_ref[...])
        _, vjp = jax.vjp(fn, x_ref[...], st_in_ref[0], lb0_ref[...], lb1_ref[...], on_ref[...])
        dx, dst, d0, d1, don = vjp((dy_ref[...], dst_ref[...]))
        dx_ref[...] = dx.astype(BF16)
        dst_ref[...] = dst
        d0_ref[...] += d0
        d1_ref[...] += d1
        don_ref[...] += don

    rev = lambda n: nc - 1 - n
    return pl.pallas_call(
        body,
        name=f"hgrn_bwd_{layer}",
        grid=(nc,),
        in_specs=[
            pl.BlockSpec((CHUNK, 4 * B_WIDTH), lambda n: (rev(n), 1)),
            pl.BlockSpec((1, B_WIDTH, B_WIDTH), lambda n: (rev(n), 0, 0)),
            pl.BlockSpec((CHUNK, B_WIDTH), lambda n: (rev(n), 1)),
            _full((1, B_WIDTH)), _full((1, B_WIDTH)), _full((1, B_WIDTH)),
            _full(((len(B_LEVELS) + 1) * CHUNK, CHUNK)), _full((B_WIDTH, B_WIDTH)),
        ],
        out_specs=[
            pl.BlockSpec((CHUNK, 4 * B_WIDTH), lambda n: (rev(n), 0)),
            _full((1, B_WIDTH)), _full((1, B_WIDTH)), _full((1, B_WIDTH)),
        ],
        out_shape=[jax.ShapeDtypeStruct((seq, 4 * B_WIDTH), BF16)] + [jax.ShapeDtypeStruct((1, B_WIDTH), F32)] * 3,
        scratch_shapes=[pltpu.VMEM((B_WIDTH, B_WIDTH), F32)],
        compiler_params=_cparams("arbitrary"),
    )(proj, states, dy, lb0, lb1, onorm, tstack, ones_bd)


D_MODEL = 1024
D_INT = 4096


def _rms_stats(xf):
    r = lax.rsqrt(jnp.mean(xf * xf, axis=-1, keepdims=True) + NORM_EPS)
    return r, xf * r


def _rms_bwd(dy, g, r, xh):
    u = dy * g
    return r * (u - xh * jnp.mean(u * xh, axis=-1, keepdims=True))


def inproj(x, g, w, layer):
    seq = x.shape[0]
    tm, tn = min(seq, 1024), 512

    def body(x_ref, g_ref, w_ref, p_ref, h_ref):
        @pl.when(pl.program_id(1) == 0)
        def _():
            _, xh = _rms_stats(x_ref[...])
            h_ref[...] = (xh * g_ref[...]).astype(BF16)

        p_ref[...] = _dot(h_ref[...], w_ref[0])

    return pl.pallas_call(
        body,
        name="inproj",
        grid=(seq // tm, D_INT // tn),
        in_specs=[
            pl.BlockSpec((tm, D_MODEL), lambda i, j: (i, 0)),
            _full((1, D_MODEL)),
            pl.BlockSpec((1, D_MODEL, tn), lambda i, j: (layer, 0, j)),
        ],
        out_specs=[pl.BlockSpec((tm, tn), lambda i, j: (i, j)), pl.BlockSpec((tm, D_MODEL), lambda i, j: (i, 0))],
        out_shape=[jax.ShapeDtypeStruct((seq, D_INT), F32), jax.ShapeDtypeStruct((seq, D_MODEL), BF16)],
        compiler_params=_cparams("parallel", "arbitrary"),
    )(x, g, w)


def outproj(x, ya, yb, o, proj, wo, layer):
    seq = x.shape[0]
    tm = min(seq, 512)
    blk = wo.shape[2]

    def body(x_ref, ya_ref, yb_ref, o_ref, z_ref, w_ref, xn_ref, y_ref):
        yc = (o_ref[...] * jax.nn.silu(z_ref[...])).astype(BF16)
        y = jnp.concatenate([ya_ref[...], yb_ref[...], yc], axis=1)
        y_ref[...] = y
        w = jnp.concatenate([w_ref[d, 0] for d in range(N_DEV)], axis=0)
        xn_ref[...] = x_ref[...] + _dot(y, w)

    return pl.pallas_call(
        body,
        name="outproj",
        grid=(seq // tm,),
        in_specs=[
            pl.BlockSpec((tm, D_MODEL), lambda i: (i, 0)),
            pl.BlockSpec((tm, 256), lambda i: (i, 0)),
            pl.BlockSpec((tm, 256), lambda i: (i, 0)),
            pl.BlockSpec((tm, 512), lambda i: (i, 0)),
            pl.BlockSpec((tm, 512), lambda i: (i, 7)),
            pl.BlockSpec((N_DEV, 1, blk, D_MODEL), lambda i: (0, layer, 0, 0)),
        ],
        out_specs=[pl.BlockSpec((tm, D_MODEL), lambda i: (i, 0)), pl.BlockSpec((tm, D_MODEL), lambda i: (i, 0))],
        out_shape=[jax.ShapeDtypeStruct((seq, D_MODEL), F32), jax.ShapeDtypeStruct((seq, D_MODEL), BF16)],
        compiler_params=_cparams("parallel"),
    )(x, ya, yb, o, proj, wo)


def outproj_bwd(dx, y, wo, layer, stacked=None):
    seq = dx.shape[0]
    ts = min(seq, 512)
    _, depth, blk, _ = wo.shape

    def body(dx_ref, y_ref, w_ref, *refs):
        dy_ref, dw_ref = refs[-2:]

        @pl.when(pl.program_id(0) == 0)
        def _():
            dw_ref[...] = jnp.zeros_like(dw_ref)

        dxb = dx_ref[...].astype(BF16)
        w = jnp.concatenate([w_ref[d, 0] for d in range(N_DEV)], axis=0)
        dy_ref[...] = lax.dot_general(dxb, w, _NT, preferred_element_type=F32)
        dw = lax.dot_general(y_ref[...], dxb, _TN, preferred_element_type=F32)
        for d in range(N_DEV):
            dw_ref[d % 2, d // 2, 0] += dw[d * blk:(d + 1) * blk]

    carried = () if stacked is None else (stacked,)
    out_shape = [jax.ShapeDtypeStruct((seq, D_MODEL), F32), jax.ShapeDtypeStruct((2, N_CHIP, depth, blk, D_MODEL), F32)]
    return pl.pallas_call(
        body,
        name="outproj_bwd",
        grid=(seq // ts,),
        in_specs=[
            pl.BlockSpec((ts, D_MODEL), lambda i: (i, 0)),
            pl.BlockSpec((ts, D_MODEL), lambda i: (i, 0)),
            pl.BlockSpec((N_DEV, 1, blk, D_MODEL), lambda i: (0, layer, 0, 0)),
        ] + [_ANY] * len(carried),
        out_specs=[pl.BlockSpec((ts, D_MODEL), lambda i: (i, 0)),
                   pl.BlockSpec((2, N_CHIP, 1, blk, D_MODEL), lambda i: (0, 0, layer, 0, 0))],
        out_shape=out_shape,
        input_output_aliases={3: 1} if carried else {},
        compiler_params=_cparams("arbitrary"),
    )(dx, y, wo, *carried)


def inproj_bwd_x(dproj, w, x, g, dx_in, layer):
    seq = x.shape[0]
    tm, tk = min(seq, 512), 1024
    nk = D_INT // tk

    def body(dp_ref, w_ref, x_ref, g_ref, dxin_ref, dx_ref, dg_ref, acc_ref):
        k = pl.program_id(1)

        @pl.when(k == 0)
        def _():
            acc_ref[...] = jnp.zeros_like(acc_ref)

        acc_ref[...] += lax.dot_general(dp_ref[...], w_ref[0], _NT, preferred_element_type=F32)

        @pl.when(k == nk - 1)
        def _():
            @pl.when(pl.program_id(0) == 0)
            def _():
                dg_ref[...] = jnp.zeros_like(dg_ref)

            dh = acc_ref[...]
            g = g_ref[...]
            r, xh = _rms_stats(x_ref[...])
            dg_ref[...] += jnp.sum(dh * xh, axis=0, keepdims=True)
            dx_ref[...] = dxin_ref[...] + _rms_bwd(dh, g, r, xh)

    return pl.pallas_call(
        body,
        name="inproj_bwd_x",
        grid=(seq // tm, nk),
        in_specs=[
            pl.BlockSpec((tm, tk), lambda i, k: (i, k)),
            pl.BlockSpec((1, D_MODEL, tk), lambda i, k: (layer, 0, k)),
            pl.BlockSpec((tm, D_MODEL), lambda i, k: (i, 0)),
            _full((1, D_MODEL)),
            pl.BlockSpec((tm, D_MODEL), lambda i, k: (i, 0)),
        ],
        out_specs=[pl.BlockSpec((tm, D_MODEL), lambda i, k: (i, 0)), _full((1, D_MODEL))],
        out_shape=[jax.ShapeDtypeStruct((seq, D_MODEL), F32), jax.ShapeDtypeStruct((1, D_MODEL), F32)],
        scratch_shapes=[pltpu.VMEM((tm, D_MODEL), F32)],
        compiler_params=_cparams("arbitrary", "arbitrary"),
    )(dproj, w, x, g, dx_in)


def inproj_bwd_w(h, dproj, layer, depth, stacked=None):
    seq = h.shape[0]
    ts, tn = min(seq, 1024), 512

    def body(h_ref, dp_ref, *refs):
        dw_ref = refs[-1]

        @pl.when(pl.program_id(1) == 0)
        def _():
            dw_ref[...] = jnp.zeros_like(dw_ref)

        dw_ref[0] += lax.dot_general(h_ref[...], dp_ref[...], _TN, preferred_element_type=F32)

    carried = () if stacked is None else (stacked,)
    return pl.pallas_call(
        body,
        name="inproj_bwd_w",
        grid=(D_INT // tn, seq // ts),
        in_specs=[pl.BlockSpec((ts, D_MODEL), lambda j, s: (s, 0)), pl.BlockSpec((ts, tn), lambda j, s: (s, j))]
        + [_ANY] * len(carried),
        out_specs=pl.BlockSpec((1, D_MODEL, tn), lambda j, s: (layer, 0, j)),
        out_shape=jax.ShapeDtypeStruct((depth, D_MODEL, D_INT), F32),
        input_output_aliases={2: 0} if carried else {},
        compiler_params=_cparams("parallel", "arbitrary"),
    )(h, dproj, *carried)


N_IN = 3848


def _internal_of(col):
    return col if col < 768 else (col + 256 if col < 3840 else 768 + col - 3840)


def _column_runs(n_shard):
    runs = []
    for d in range(N_IN // n_shard):
        mine = []
        for j in range(n_shard):
            ci = _internal_of(d * n_shard + j)
            if mine and mine[-1][0] + mine[-1][1] == ci:
                mine[-1][1] += 1
            else:
                mine.append([ci, 1, j])
        runs.append(mine)
    return runs


def assemble_w_in(wi_all):
    n_dev, depth, _, n_shard = wi_all.shape
    tr = 256
    pieces = [[] for _ in range(D_INT // LANES)]
    for d, mine in enumerate(_column_runs(n_shard)):
        for ci, ln, off in mine:
            while ln > 0:
                blk, at = divmod(ci, LANES)
                take = min(ln, LANES - at)
                pieces[blk].append((at, take, d, off))
                ci, ln, off = ci + take, ln - take, off + take

    def body(x_ref, o_ref):
        for blk, parts in enumerate(pieces):
            vals, at = [], 0
            for start, ln, d, off in sorted(parts):
                if start > at:
                    vals.append(jnp.zeros((tr, start - at), BF16))
                vals.append(x_ref[d, 0, :, off:off + ln])
                at = start + ln
            if at < LANES:
                vals.append(jnp.zeros((tr, LANES - at), BF16))
            o_ref[0, :, blk * LANES:(blk + 1) * LANES] = vals[0] if len(vals) == 1 else jnp.concatenate(vals, axis=1)

    return pl.pallas_call(
        body,
        name="assemble_w_in",
        grid=(depth, D_MODEL // tr),
        in_specs=[pl.BlockSpec((n_dev, 1, tr, n_shard), lambda l, r: (0, l, r, 0))],
        out_specs=pl.BlockSpec((1, tr, D_INT), lambda l, r: (l, r, 0)),
        out_shape=jax.ShapeDtypeStruct((depth, D_MODEL, D_INT), BF16),
        compiler_params=_cparams("parallel", "parallel"),
    )(wi_all)


def split_w_in_grad(dwi, n_shard):
    depth = dwi.shape[0]
    tr = 256
    runs = _column_runs(n_shard)

    def body(x_ref, o_ref):
        for d, mine in enumerate(runs):
            for ci, ln, off in mine:
                o_ref[d % 2, d // 2, 0, :, off:off + ln] = x_ref[0, :, ci:ci + ln]

    return pl.pallas_call(
        body,
        name="split_w_in_grad",
        grid=(depth, D_MODEL // tr),
        in_specs=[pl.BlockSpec((1, tr, D_INT), lambda l, r: (l, r, 0))],
        out_specs=pl.BlockSpec((2, N_CHIP, 1, tr, n_shard), lambda l, r: (0, 0, l, r, 0)),
        out_shape=jax.ShapeDtypeStruct((2, N_CHIP, depth, D_MODEL, n_shard), F32),
        compiler_params=_cparams("parallel", "parallel"),
    )(dwi)


def final_loss(x, g, tgt):
    seq = x.shape[0]
    tm = min(seq, 512)

    def body(x_ref, g_ref, t_ref, dx_ref, dg_ref, loss_ref):
        @pl.when(pl.program_id(0) == 0)
        def _():
            dg_ref[...] = jnp.zeros_like(dg_ref)
            loss_ref[...] = jnp.zeros_like(loss_ref)

        g = g_ref[...]
        r, xh = _rms_stats(x_ref[...])
        err = xh * g - t_ref[...]
        sq = jnp.sum(jnp.sum(err * err, axis=1, keepdims=True), axis=0, keepdims=True)
        loss_ref[...] += jnp.broadcast_to(sq * (0.5 / D_MODEL), loss_ref.shape)
        dout = err * (1.0 / D_MODEL)
        dg_ref[...] += jnp.sum(dout * xh, axis=0, keepdims=True)
        dx_ref[...] = _rms_bwd(dout, g, r, xh)

    return pl.pallas_call(
        body,
        name="final_loss",
        grid=(seq // tm,),
        in_specs=[pl.BlockSpec((tm, D_MODEL), lambda i: (i, 0)), _full((1, D_MODEL)), pl.BlockSpec((tm, D_MODEL), lambda i: (i, 0))],
        out_specs=[pl.BlockSpec((tm, D_MODEL), lambda i: (i, 0)), _full((1, D_MODEL)), _full((8, LANES))],
        out_shape=[jax.ShapeDtypeStruct((seq, D_MODEL), F32), jax.ShapeDtypeStruct((1, D_MODEL), F32), jax.ShapeDtypeStruct((8, LANES), F32)],
        compiler_params=_cparams("arbitrary"),
    )(x, g, tgt)


C_WIDTH = 512
C_HEADS = 8
C_HDIM = 64
C_PAIRS = C_HEADS // 2
C_BQ = 512
C_TAIL = 16
C_KG = 2


def _split3(x):
    hi = x.astype(BF16)
    r = x - hi.astype(F32)
    mid = r.astype(BF16)
    return hi, mid, (r - mid.astype(F32)).astype(BF16)


def _piece_selectors():
    sel = np.zeros((C_HEADS, 3 * LANES, LANES), np.float32)
    for p in range(C_PAIRS):
        for e in range(2):
            for t in range(3):
                sel[2 * p + e, t * LANES + 2 * p + e, 3 * e + t] = -1.0
    return sel


def fox_prep(proj, bf_row):
    seq = proj.shape[0]
    nblk = seq // CHUNK
    tril = jnp.asarray(np.tril(np.ones((CHUNK, CHUNK), np.float32)))
    sel = jnp.asarray(_piece_selectors(), BF16)
    rows_t = CHUNK + C_TAIL

    def body(fl_ref, q_ref, k_ref, v_ref, bf_ref, l_ref, sel_ref, ka_ref, va_ref, vt_ref, kt_ref, qt_ref, qa_ref, carry_ref):
        @pl.when(pl.program_id(0) == 0)
        def _():
            carry_ref[...] = jnp.zeros_like(carry_ref)

        lf = jax.nn.log_sigmoid(fl_ref[:, :LANES] + bf_ref[...])
        c = _dot(l_ref[...], lf, precision=HI) + carry_ref[...]
        carry_ref[...] += jnp.sum(lf, axis=0, keepdims=True)
        c3 = jnp.concatenate(_split3(c), axis=1)
        lane = lax.broadcasted_iota(jnp.int32, (CHUNK, LANES), 1)
        row = lax.broadcasted_iota(jnp.int32, (CHUNK, LANES), 0)
        r16 = lax.broadcasted_iota(jnp.int32, (C_TAIL, 2 * CHUNK), 0)
        l16 = lax.broadcasted_iota(jnp.int32, (C_TAIL, 2 * CHUNK), 1)
        zero = jnp.zeros((CHUNK, LANES), BF16)
        one = jnp.ones((CHUNK, LANES), BF16)

        def by_keys(x, right_a, right_b):
            xb = x.astype(BF16)
            top = jnp.concatenate([jnp.where(lane < C_HDIM, xb, zero), right_a], axis=1)
            return jnp.concatenate([top, jnp.concatenate([jnp.where(lane < C_HDIM, zero, xb), right_b], axis=1)], axis=0)

        def by_lanes(x, tail):
            xt = x.T.astype(BF16)
            main = jnp.concatenate([jnp.where(row < C_HDIM, xt, zero), jnp.where(row < C_HDIM, zero, xt)], axis=1)
            return jnp.concatenate([main, tail], axis=0)

        for p in range(C_PAIRS):
            cols = slice(p * LANES, (p + 1) * LANES)
            q2, k2, v2 = q_ref[:, cols] * (C_HDIM ** -0.5), k_ref[:, cols], v_ref[:, cols]
            negc = [_dot(c3, sel_ref[2 * p + e]).astype(BF16) for e in range(2)]
            ones3 = [jnp.where((lane >= 3 * e) & (lane < 3 * e + 3), one, zero) for e in range(2)]
            tail = jnp.where(((r16 == 2 * p) & (l16 < CHUNK)) | ((r16 == 2 * p + 1) & (l16 >= CHUNK)), 1.0, 0.0).astype(BF16)
            ka_ref[p, 0] = by_keys(k2, negc[0], negc[1])
            va_ref[p, 0] = by_keys(v2, ones3[0], ones3[1])
            kt_ref[p, 0] = by_lanes(k2, tail)
            vt_ref[p, 0] = by_lanes(v2, tail)
            qt_ref[p] = jnp.concatenate([q2.T.astype(BF16), jnp.where(row < 6, one, zero)], axis=0)
            qa_ref[p] = jnp.concatenate([q2.astype(BF16), jnp.where((lane == 2 * p) | (lane == 2 * p + 1), one, zero)], axis=1)

    wide = lambda j: pl.BlockSpec((CHUNK, C_WIDTH), lambda n: (n, j))
    sq = lambda r: pl.BlockSpec((C_PAIRS, 1, r, 2 * CHUNK), lambda n: (0, n, 0, 0))
    return pl.pallas_call(
        body,
        name="fox_prep",
        grid=(nblk,),
        in_specs=[pl.BlockSpec((CHUNK, 256), lambda n: (n, 3)), wide(4), wide(5), wide(6), _full((1, LANES)),
                  _full((CHUNK, CHUNK)), _full((C_HEADS, 3 * LANES, LANES))],
        out_specs=[sq(2 * CHUNK), sq(2 * CHUNK), sq(rows_t), sq(rows_t),
                   pl.BlockSpec((C_PAIRS, 2 * CHUNK, CHUNK), lambda n: (0, 0, n)),
                   pl.BlockSpec((C_PAIRS, CHUNK, 2 * CHUNK), lambda n: (0, n, 0))],
        out_shape=[jax.ShapeDtypeStruct((C_PAIRS, nblk, 2 * CHUNK, 2 * CHUNK), BF16)] * 2
        + [jax.ShapeDtypeStruct((C_PAIRS, nblk, rows_t, 2 * CHUNK), BF16)] * 2
        + [jax.ShapeDtypeStruct((C_PAIRS, 2 * CHUNK, seq), BF16), jax.ShapeDtypeStruct((C_PAIRS, seq, 2 * CHUNK), BF16)],
        scratch_shapes=[pltpu.VMEM((1, LANES), F32)],
        compiler_params=_cparams("arbitrary"),
    )(proj, proj, proj, proj, bf_row, tril, sel)


def _visible(shape, key0, query0):
    key = key0 + (lax.broadcasted_iota(jnp.int32, shape, 0) & (CHUNK - 1))
    return key <= query0 + lax.broadcasted_iota(jnp.int32, shape, 1)


def _rows_ab(a, b, n):
    return jnp.concatenate([jnp.broadcast_to(a, (C_HDIM, n)), jnp.broadcast_to(b, (C_HDIM, n))], axis=0)


def fox_fwd(qt, ka, vt):
    seq = qt.shape[2]
    nblk = seq // CHUNK
    bq = min(C_BQ, seq)
    grp = bq // CHUNK
    rows_t = CHUNK + C_TAIL

    def body(qt_ref, ka_ref, vt_ref, o_ref, lse_ref, acc_ref):
        p, i = pl.program_id(0), pl.program_id(1)
        qtile = qt_ref[0]
        r16 = lax.broadcasted_iota(jnp.int32, (C_TAIL, bq), 0)

        def group(j0, m, masked):
            ma, mb = m
            ss = []
            for g in range(grp):
                s = _dot(ka_ref[0, j0 + g], qtile)
                if masked:
                    s = jnp.where(_visible(s.shape, (j0 + g) * CHUNK, i * bq), s, -jnp.inf)
                ss.append(s)
            na, nb = ma, mb
            for s in ss:
                na = jnp.maximum(na, jnp.max(s[:CHUNK], axis=0, keepdims=True))
                nb = jnp.maximum(nb, jnp.max(s[CHUNK:], axis=0, keepdims=True))
            al_a, al_b = jnp.exp(ma - na), jnp.exp(mb - nb)
            pv = None
            for g, s in enumerate(ss):
                pt = jnp.concatenate([jnp.exp(s[:CHUNK] - na), jnp.exp(s[CHUNK:] - nb)], axis=0).astype(BF16)
                r = _dot(vt_ref[0, j0 + g], pt)
                pv = r if pv is None else pv + r
            tail = jnp.where(r16 == 2 * p, al_a, jnp.where(r16 == 2 * p + 1, al_b, 1.0))
            acc_ref[...] = acc_ref[...] * jnp.concatenate([_rows_ab(al_a, al_b, bq), tail], axis=0) + pv
            return na, nb

        acc_ref[...] = jnp.zeros_like(acc_ref)
        m = (jnp.full((1, bq), -jnp.inf, F32), jnp.full((1, bq), -jnp.inf, F32))
        m = lax.fori_loop(0, i, lambda t, m: group(t * grp, m, False), m)
        ma, mb = group(i * grp, m, True)
        tailv = acc_ref[CHUNK:rows_t, :]
        la = jnp.sum(jnp.where(r16 == 2 * p, tailv, 0.0), axis=0, keepdims=True)
        lb = jnp.sum(jnp.where(r16 == 2 * p + 1, tailv, 0.0), axis=0, keepdims=True)
        o_ref[...] = (acc_ref[0:CHUNK, :] * _rows_ab(1.0 / la, 1.0 / lb, bq)).T
        lse_ref[0, 0:1, :] = ma + jnp.log(la)
        lse_ref[0, 1:2, :] = mb + jnp.log(lb)

    return pl.pallas_call(
        body,
        name="fox_fwd",
        grid=(C_PAIRS, seq // bq),
        in_specs=[
            pl.BlockSpec((1, 2 * CHUNK, bq), lambda p, i: (p, 0, i)),
            pl.BlockSpec((1, nblk, 2 * CHUNK, 2 * CHUNK), lambda p, i: (p, 0, 0, 0)),
            pl.BlockSpec((1, nblk, rows_t, 2 * CHUNK), lambda p, i: (p, 0, 0, 0)),
        ],
        out_specs=[pl.BlockSpec((bq, LANES), lambda p, i: (i, p)), pl.BlockSpec((1, 2, bq), lambda p, i: (p, 0, i))],
        out_shape=[jax.ShapeDtypeStruct((seq, C_WIDTH), F32), jax.ShapeDtypeStruct((C_PAIRS, 2, seq), F32)],
        scratch_shapes=[pltpu.VMEM((rows_t, bq), F32)],
        compiler_params=_cparams("parallel", "arbitrary"),
    )(qt, ka, vt)


def fox_bwd_prep(dy, o, proj):
    seq = o.shape[0]
    ind = np.zeros((C_WIDTH, LANES), np.float32)
    for h in range(C_HEADS):
        ind[h * C_HDIM:(h + 1) * C_HDIM, h] = 1.0
    ind = jnp.asarray(ind, BF16)
    sel = _piece_selectors()
    sel = jnp.asarray(np.stack([sel[2 * p].T + sel[2 * p + 1].T for p in range(C_PAIRS)]), BF16)

    def body(dy_ref, o_ref, z_ref, ind_ref, sel_ref, do_ref, dz_ref, dot_ref):
        dy_c, o_v, z = dy_ref[...], o_ref[...], z_ref[...]
        sg = jax.nn.sigmoid(z)
        do = dy_c * (z * sg)
        do_ref[...] = do.astype(BF16)
        dz_ref[...] = (dy_c * o_v * (sg * (1.0 + z * (1.0 - sg)))).astype(BF16)
        prod = do * o_v
        hi = prod.astype(BF16)
        lo = (prod - hi.astype(F32)).astype(BF16)
        delta = _dot(hi, ind_ref[...]) + _dot(lo, ind_ref[...])
        d3 = jnp.concatenate(_split3(delta.T), axis=0)
        for p in range(C_PAIRS):
            tail = _dot(sel_ref[p], d3).astype(BF16)
            dot_ref[p] = jnp.concatenate([do[:, p * LANES:(p + 1) * LANES].T.astype(BF16), tail], axis=0)

    return pl.pallas_call(
        body,
        name="fox_bwd_prep",
        grid=(seq // CHUNK,),
        in_specs=[
            pl.BlockSpec((CHUNK, C_WIDTH), lambda i: (i, 1)),
            pl.BlockSpec((CHUNK, C_WIDTH), lambda i: (i, 0)),
            pl.BlockSpec((CHUNK, C_WIDTH), lambda i: (i, 7)),
            _full((C_WIDTH, LANES)), _full((C_PAIRS, LANES, 3 * LANES)),
        ],
        out_specs=[
            pl.BlockSpec((CHUNK, C_WIDTH), lambda i: (i, 0)),
            pl.BlockSpec((CHUNK, C_WIDTH), lambda i: (i, 0)),
            pl.BlockSpec((C_PAIRS, 2 * CHUNK, CHUNK), lambda i: (0, 0, i)),
        ],
        out_shape=[jax.ShapeDtypeStruct((seq, C_WIDTH), BF16)] * 2 + [jax.ShapeDtypeStruct((C_PAIRS, 2 * CHUNK, seq), BF16)],
        compiler_params=_cparams("parallel"),
    )(dy, o, proj, ind, sel)


def fox_bwd(ka, va, kt, qt, dot_t, qa, dob, lse):
    seq = qt.shape[2]
    nblk = seq // CHUNK
    bq = min(C_BQ, seq)
    nq = seq // bq
    kg = min(C_KG, nblk)
    ng = nblk // kg
    rows_t = CHUNK + C_TAIL

    def body(ka_ref, va_ref, kt_ref, qt_ref, dot_ref, qa_ref, do_ref, lse_ref,
             dq_ref, dk_ref, dv_ref, dck_ref, dcq_ref, dqt_acc, dv_acc, dka_acc):
        p, jg = pl.program_id(0), pl.program_id(1)

        @pl.when(jg == 0)
        def _():
            dqt_acc[...] = jnp.zeros_like(dqt_acc)

        dv_acc[...] = jnp.zeros_like(dv_acc)
        dka_acc[...] = jnp.zeros_like(dka_acc)

        def step(i, carry, masked):
            cols = pl.ds(pl.multiple_of(i * bq, bq), bq)
            qtile, dotile = qt_ref[0, :, cols], dot_ref[0, :, cols]
            do, qa_i = do_ref[cols, :], qa_ref[0, cols, :]
            lse2 = jnp.concatenate([jnp.broadcast_to(lse_ref[0, 0:1, cols], (CHUNK, bq)),
                                    jnp.broadcast_to(lse_ref[0, 1:2, cols], (CHUNK, bq))], axis=0)
            for kb in range(kg):
                pt = jnp.exp(_dot(ka_ref[0, kb], qtile) - lse2)
                if masked:
                    pt = jnp.where(_visible(pt.shape, (jg * kg + kb) * CHUNK, i * bq), pt, 0.0)
                ds = pt * _dot(va_ref[0, kb], dotile)
                ptb, dsb = pt.astype(BF16), ds.astype(BF16)
                dv_acc[kb] += _dot(ptb, do)
                dka_acc[kb] += _dot(dsb, qa_i)
                dqt_acc[:, cols] += _dot(kt_ref[0, kb], dsb)
            return carry

        i0 = (jg * kg * CHUNK) // bq
        step(i0, 0, True)
        lax.fori_loop(i0 + 1, nq, functools.partial(step, masked=False), 0)
        lane = lax.broadcasted_iota(jnp.int32, (CHUNK, LANES), 1)
        for kb in range(kg):
            rows = slice(kb * CHUNK, (kb + 1) * CHUNK)
            dk_ref[rows, :] = jnp.where(lane < C_HDIM, dka_acc[kb, 0:CHUNK, 0:LANES], dka_acc[kb, CHUNK:, 0:LANES]).astype(BF16)
            dv_ref[rows, :] = jnp.where(lane < C_HDIM, dv_acc[kb, 0:CHUNK, :], dv_acc[kb, CHUNK:, :]).astype(BF16)
            dck_ref[0, rows, :] = (jnp.where(lane == 2 * p, dka_acc[kb, 0:CHUNK, LANES:], 0.0)
                                   + jnp.where(lane == 2 * p + 1, dka_acc[kb, CHUNK:, LANES:], 0.0))

        @pl.when(jg == ng - 1)
        def _():
            for c in range(nq):
                dq_ref[c * bq:(c + 1) * bq, :] = (dqt_acc[0:CHUNK, c * bq:(c + 1) * bq].T * (C_HDIM ** -0.5)).astype(BF16)
            dcq_ref[0] = dqt_acc[CHUNK:rows_t, :]

    per_pair = lambda r, c: pl.BlockSpec((1, r, c), lambda p, j: (p, 0, 0))
    keys4 = lambda r: pl.BlockSpec((1, kg, r, 2 * CHUNK), lambda p, j: (p, j, 0, 0))
    return pl.pallas_call(
        body,
        name="fox_bwd",
        grid=(C_PAIRS, ng),
        in_specs=[keys4(2 * CHUNK), keys4(2 * CHUNK), keys4(rows_t), per_pair(2 * CHUNK, seq), per_pair(2 * CHUNK, seq),
                  per_pair(seq, 2 * CHUNK), pl.BlockSpec((seq, LANES), lambda p, j: (0, p)), per_pair(2, seq)],
        out_specs=[pl.BlockSpec((seq, LANES), lambda p, j: (0, p)),
                   pl.BlockSpec((kg * CHUNK, LANES), lambda p, j: (j, p)),
                   pl.BlockSpec((kg * CHUNK, LANES), lambda p, j: (j, p)),
                   pl.BlockSpec((1, kg * CHUNK, LANES), lambda p, j: (p, j, 0)),
                   per_pair(C_TAIL, seq)],
        out_shape=[jax.ShapeDtypeStruct((seq, C_WIDTH), BF16)] * 3
        + [jax.ShapeDtypeStruct((C_PAIRS, seq, LANES), F32), jax.ShapeDtypeStruct((C_PAIRS, C_TAIL, seq), F32)],
        scratch_shapes=[pltpu.VMEM((rows_t, seq), F32), pltpu.VMEM((kg, 2 * CHUNK, LANES), F32),
                        pltpu.VMEM((kg, 2 * CHUNK, 2 * CHUNK), F32)],
        compiler_params=_cparams("parallel", "arbitrary"),
    )(ka, va, kt, qt, dot_t, qa, dob, lse)


def fox_post(dcq, dck, proj, bf_row):
    seq = proj.shape[0]
    nc = seq // CHUNK
    triu = jnp.asarray(np.triu(np.ones((CHUNK, CHUNK), np.float32)))

    def body(dq_ref, dk_ref, fl_ref, bf_ref, u_ref, dfl_ref, dbf_ref, carry_ref):
        @pl.when(pl.program_id(0) == 0)
        def _():
            carry_ref[...] = jnp.zeros_like(carry_ref)
            dbf_ref[...] = jnp.zeros_like(dbf_ref)

        rows = (dq_ref[0] + dq_ref[1]) + (dq_ref[2] + dq_ref[3])
        dc = jnp.concatenate([rows, jnp.zeros((CHUNK - C_TAIL, CHUNK), F32)], axis=0).T
        dc = dc - ((dk_ref[0] + dk_ref[1]) + (dk_ref[2] + dk_ref[3]))
        g = _dot(u_ref[...], dc, precision=HI) + carry_ref[...]
        carry_ref[...] += jnp.sum(dc, axis=0, keepdims=True)
        dfl = g * jax.nn.sigmoid(-(fl_ref[:, :LANES] + bf_ref[...]))
        dbf_ref[...] += jnp.sum(dfl, axis=0, keepdims=True)
        dfl_ref[...] = jnp.concatenate([dfl, jnp.zeros_like(dfl)], axis=1).astype(BF16)

    rev = lambda n: nc - 1 - n
    return pl.pallas_call(
        body,
        name="fox_post",
        grid=(nc,),
        in_specs=[
            pl.BlockSpec((C_PAIRS, C_TAIL, CHUNK), lambda n: (0, 0, rev(n))),
            pl.BlockSpec((C_PAIRS, CHUNK, LANES), lambda n: (0, rev(n), 0)),
            pl.BlockSpec((CHUNK, 256), lambda n: (rev(n), 3)),
            _full((1, LANES)), _full((CHUNK, CHUNK)),
        ],
        out_specs=[pl.BlockSpec((CHUNK, 256), lambda n: (rev(n), 0)), _full((1, LANES))],
        out_shape=[jax.ShapeDtypeStruct((seq, 256), BF16), jax.ShapeDtypeStruct((1, LANES), F32)],
        scratch_shapes=[pltpu.VMEM((1, LANES), F32)],
        compiler_params=_cparams("arbitrary"),
    )(dcq, dck, proj, bf_row, triu)


N_DEV = 8
MESH = pl.DeviceIdType.MESH
_ANY = pl.BlockSpec(memory_space=pl.ANY)


def _mesh_pos():
    return lax.axis_index("x"), lax.axis_index("y"), lax.axis_index("c")


def _dev_index(px, py, pc):
    return 4 * px + 2 * py + pc


def allgather_weights(wi, wo):
    def body(wi_ref, wo_ref, wi_all, wo_all, send_sems, recv_sems, local_sems):
        x, y, c = _mesh_pos()
        me, sibling = (x, y, c), (x, y, 1 - c)
        chips = [(1 - x, y), (x, 1 - y), (1 - x, 1 - y)]
        arrays = ((wi_ref, wi_all), (wo_ref, wo_all))

        def copy(a, k, block, to, own=False):
            src, out = arrays[a]
            slot = out.at[_dev_index(*block)]
            return pltpu.make_async_remote_copy(
                src_ref=src if own else slot, dst_ref=slot, send_sem=send_sems.at[a, k], recv_sem=recv_sems.at[a, k],
                device_id=to, device_id_type=MESH)

        both = range(len(arrays))
        mine = [pltpu.make_async_copy(arrays[a][0], arrays[a][1].at[_dev_index(*me)], local_sems.at[a]) for a in both]
        for cp in mine:
            cp.start()
        first = [copy(a, 0, me, sibling, own=True) for a in both]
        first += [copy(a, 1 + j, me, (*chip, c), own=True) for j, chip in enumerate(chips) for a in both]
        for cp in first:
            cp.start()
        passed = [copy(a, 4 + j, (*chip, c), sibling) for j, chip in enumerate(chips) for a in both]
        for j, chip in enumerate(chips):
            for a in both:
                copy(a, 1 + j, (*chip, c), me).wait_recv()
            for a in both:
                passed[2 * j + a].start()
        for a in both:
            copy(a, 0, sibling, me).wait_recv()
        for j, chip in enumerate(chips):
            for a in both:
                copy(a, 4 + j, (*chip, 1 - c), me).wait_recv()
        for cp in first + passed:
            cp.wait_send()
        for cp in mine:
            cp.wait()

    return pl.pallas_call(
        body,
        name="allgather_weights",
        in_specs=[_ANY, _ANY],
        out_specs=[_ANY, _ANY],
        out_shape=[jax.ShapeDtypeStruct((N_DEV,) + wi.shape, wi.dtype), jax.ShapeDtypeStruct((N_DEV,) + wo.shape, wo.dtype)],
        scratch_shapes=[pltpu.SemaphoreType.DMA((2, 7)), pltpu.SemaphoreType.DMA((2, 7)), pltpu.SemaphoreType.DMA((2,))],
    )(wi, wo)


N_CHIP = 4


def pair_exchange(gwi, gwo, gsm):
    def body(gwi_ref, gwo_ref, gsm_ref, pwi, pwo, psm, send_sems, recv_sems, local_sems):
        x, y, c = _mesh_pos()
        sibling = (x, y, 1 - c)
        srcs, outs = (gwi_ref, gwo_ref, gsm_ref), (pwi, pwo, psm)
        three = range(3)

        def part(a, core):
            return srcs[a] if a == 2 else srcs[a].at[core]

        local = [pltpu.make_async_copy(part(a, c), outs[a].at[c], local_sems.at[a]) for a in three]
        for cp in local:
            cp.start()

        def copy(a, slot):
            return pltpu.make_async_remote_copy(
                src_ref=part(a, 1 - c), dst_ref=outs[a].at[slot], send_sem=send_sems.at[a], recv_sem=recv_sems.at[a],
                device_id=sibling, device_id_type=MESH)

        sends = [copy(a, c) for a in three]
        for cp in sends:
            cp.start()
        for a in three:
            copy(a, 1 - c).wait_recv()
        for cp in sends:
            cp.wait_send()
        for cp in local:
            cp.wait()

    return pl.pallas_call(
        body,
        name="pair_exchange",
        in_specs=[_ANY, _ANY, _ANY],
        out_specs=[_ANY, _ANY, _ANY],
        out_shape=[jax.ShapeDtypeStruct((2,) + gwi.shape[1:], gwi.dtype), jax.ShapeDtypeStruct((2,) + gwo.shape[1:], gwo.dtype),
                   jax.ShapeDtypeStruct((2,) + gsm.shape, gsm.dtype)],
        scratch_shapes=[pltpu.SemaphoreType.DMA((3,)), pltpu.SemaphoreType.DMA((3,)), pltpu.SemaphoreType.DMA((3,))],
    )(gwi, gwo, gsm)


def pair_sum(pair, dtype, rows, name):
    _, n, n_r, n_c = pair.shape

    def body(p_ref, o_ref):
        o_ref[0] = (p_ref[0, 0] + p_ref[1, 0]).astype(dtype)

    return pl.pallas_call(
        body,
        name=name,
        grid=(n, n_r // rows),
        in_specs=[pl.BlockSpec((2, 1, rows, n_c), lambda i, r: (0, i, r, 0))],
        out_specs=pl.BlockSpec((1, rows, n_c), lambda i, r: (i, r, 0)),
        out_shape=jax.ShapeDtypeStruct((n, n_r, n_c), dtype),
        compiler_params=_cparams("parallel", "parallel"),
    )(pair)


def chip_exchange(swi, swo, ssm):
    def body(swi_ref, swo_ref, ssm_ref, rwi, rwo, rsm, send_sems, recv_sems, local_sems):
        x, y, c = _mesh_pos()
        chip = 2 * x + y
        srcs, outs = (swi_ref, swo_ref, ssm_ref), (rwi, rwo, rsm)
        three = range(3)

        def for_chip(a, j):
            return srcs[a] if a == 2 else srcs[a].at[j]

        local = [pltpu.make_async_copy(for_chip(a, chip), outs[a].at[chip], local_sems.at[a]) for a in three]
        for cp in local:
            cp.start()

        def peer_of(k):
            return x ^ ((k >> 1) & 1), y ^ (k & 1)

        def copy(a, k, slot):
            px, py = peer_of(k)
            return pltpu.make_async_remote_copy(
                src_ref=for_chip(a, 2 * px + py), dst_ref=outs[a].at[slot], send_sem=send_sems.at[a, k - 1],
                recv_sem=recv_sems.at[a, k - 1], device_id=(px, py, c), device_id_type=MESH)

        sends = [copy(a, k, chip) for k in range(1, N_CHIP) for a in three]
        for cp in sends:
            cp.start()
        for k in range(1, N_CHIP):
            px, py = peer_of(k)
            for a in three:
                copy(a, k, 2 * px + py).wait_recv()
        for cp in sends:
            cp.wait_send()
        for cp in local:
            cp.wait()

    return pl.pallas_call(
        body,
        name="chip_exchange",
        in_specs=[_ANY, _ANY, _ANY],
        out_specs=[_ANY, _ANY, _ANY],
        out_shape=[jax.ShapeDtypeStruct(swi.shape, swi.dtype), jax.ShapeDtypeStruct(swo.shape, swo.dtype),
                   jax.ShapeDtypeStruct((N_CHIP,) + ssm.shape, ssm.dtype)],
        scratch_shapes=[pltpu.SemaphoreType.DMA((3, 3)), pltpu.SemaphoreType.DMA((3, 3)), pltpu.SemaphoreType.DMA((3,))],
    )(swi, swo, ssm)


ADAM_LR = 0.001
ADAM_B1 = 0.9
ADAM_B2 = 0.999
ADAM_EPS = 1e-08
ADAM_WD = 0.01
ADAM_STEP = 10


def adam_reduce(parts, w, m, v, rows, name):
    n_l, n_r, n_c = w.shape

    def body(p_ref, w_ref, m_ref, v_ref, g_ref, d_ref, m2_ref, v2_ref):
        g = p_ref[0, 0].astype(F32)
        for d in range(1, N_CHIP):
            g = g + p_ref[d, 0].astype(F32)
        m2 = ADAM_B1 * m_ref[0] + (1.0 - ADAM_B1) * g
        v2 = ADAM_B2 * v_ref[0] + (1.0 - ADAM_B2) * (g * g)
        m_hat = m2 / (1.0 - ADAM_B1 ** ADAM_STEP)
        v_hat = v2 / (1.0 - ADAM_B2 ** ADAM_STEP)
        g_ref[0] = g
        d_ref[0] = -ADAM_LR * (m_hat / (jnp.sqrt(v_hat) + ADAM_EPS) + ADAM_WD * w_ref[0])
        m2_ref[0] = m2
        v2_ref[0] = v2

    blk = lambda: pl.BlockSpec((1, rows, n_c), lambda l, r: (l, r, 0))
    return pl.pallas_call(
        body,
        name=name,
        grid=(n_l, n_r // rows),
        in_specs=[pl.BlockSpec((N_CHIP, 1, rows, n_c), lambda l, r: (0, l, r, 0)), blk(), blk(), blk()],
        out_specs=[blk(), blk(), blk(), blk()],
        out_shape=[jax.ShapeDtypeStruct(w.shape, F32)] * 4,
        compiler_params=_cparams("parallel", "parallel"),
    )(parts, w, m, v)


_SMALL = (("norm_g", (2, 1024)), ("gmlp_ln_g", (2, 4, 64)), ("gmlp_ln_b", (2, 4, 64)), ("gmlp_w_s", (2, 4, 128, 128)),
          ("gmlp_b_s", (2, 4, 128)), ("hgrn_lb", (2, 256)), ("hgrn_onorm_g", (2, 64)), ("fox_b_f", (2, 8)),
          ("final_norm_g", (1024,)), ("loss", ()))


def _padded(n):
    return -(-n // LANES) * LANES


_SMALL_ROWS = -(-sum(_padded(int(np.prod(s))) for _, s in _SMALL) // LANES // 8) * 8


def _pack_small(vals):
    flat = []
    for (name, shape), a in zip(_SMALL, vals, strict=True):
        n = int(np.prod(shape))
        flat.append(jnp.pad(a.reshape(n).astype(F32), (0, _padded(n) - n)))
    flat = jnp.concatenate(flat)
    return jnp.pad(flat, (0, _SMALL_ROWS * LANES - flat.shape[0])).reshape(_SMALL_ROWS, LANES)


def _unpack_small(slab):
    flat, out, at = slab.reshape(-1), {}, 0
    for name, shape in _SMALL:
        n = int(np.prod(shape))
        out[name] = flat[at:at + n].reshape(shape)
        at += _padded(n)
    return out


def kernel(x, norm_g, w_in, w_out, gmlp_ln_g, gmlp_ln_b, gmlp_w_s, gmlp_b_s, hgrn_lb, hgrn_onorm_g, fox_b_f, final_norm_g, loss_target, m_norm_g, m_w_in, m_w_out, m_gmlp_ln_g, m_gmlp_ln_b, m_gmlp_w_s, m_gmlp_b_s, m_hgrn_lb, m_hgrn_onorm_g, m_fox_b_f, m_final_norm_g, v_norm_g, v_w_in, v_w_out, v_gmlp_ln_g, v_gmlp_ln_b, v_gmlp_w_s, v_gmlp_b_s, v_hgrn_lb, v_hgrn_onorm_g, v_fox_b_f, v_final_norm_g):
    depth = w_in.shape[0]
    seq = x.shape[1]
    assert w_in.shape[2] * N_DEV == N_IN
    xs, tgt = x[0], loss_target[0]

    wi_all, wo_all = allgather_weights(w_in.astype(BF16), w_out.astype(BF16))
    wi_int = assemble_w_in(wi_all)

    ln_g = gmlp_ln_g.reshape(depth, 1, A_WIDTH)
    ln_b = gmlp_ln_b.reshape(depth, 1, A_WIDTH)
    bs_t = jnp.pad(jnp.transpose(gmlp_b_s, (0, 2, 1)), ((0, 0), (0, 0), (0, LANES - A_GROUPS)))
    lb0, lb1 = hgrn_lb[0:1], hgrn_lb[1:2]
    onorm = jnp.tile(hgrn_onorm_g, (1, B_HEADS)).reshape(depth, 1, B_WIDTH)
    bf_row = jnp.pad(fox_b_f, ((0, 0), (0, LANES - C_HEADS))).reshape(depth, 1, LANES)

    saved = []
    xc = xs
    for l in range(depth):
        proj, h = inproj(xc, norm_g[l:l + 1], wi_int, l)
        ya = gmlp_fwd(proj, ln_g[l], ln_b[l], gmlp_w_s[l], bs_t[l])
        yb, states = hgrn_fwd(proj, lb0, lb1, onorm[l], l)
        ka, va, vt, kt, qt, qa = fox_prep(proj, bf_row[l])
        o, lse = fox_fwd(qt, ka, vt)
        xn, yfull = outproj(xc, ya, yb, o, proj, wo_all, l)
        saved.append((xc, proj, h, states, ka, va, kt, qt, qa, o, lse, yfull))
        xc = xn

    dx, d_final_g, loss_tile = final_loss(xc, final_norm_g[None], tgt)

    g_norm = [None] * depth
    g_ln_g, g_ln_b, g_ws, g_bs, g_on, g_bf = ([None] * depth for _ in range(6))
    g_lb0, g_lb1 = jnp.zeros_like(lb0), jnp.zeros_like(lb1)
    dwi = gwo = None
    for l in reversed(range(depth)):
        x_in, proj, h, states, ka, va, kt, qt, qa, o, lse, yfull = saved[l]
        dy, gwo = outproj_bwd(dx, yfull, wo_all, l, gwo)
        d_a, g_ln_g[l], g_ln_b[l], g_ws[l], dbs_t = gmlp_bwd(proj, dy, ln_g[l], ln_b[l], gmlp_w_s[l], bs_t[l])
        g_bs[l] = dbs_t[:, :A_GROUPS].T
        d_b, d0, d1, don = hgrn_bwd(proj, states, dy, lb0, lb1, onorm[l], l)
        g_lb0, g_lb1 = g_lb0 + d0, g_lb1 + d1
        g_on[l] = don.reshape(B_HEADS, B_KDIM).sum(0)
        dob, d_z, dot_t = fox_bwd_prep(dy, o, proj)
        d_q, d_k, d_v, dck, dcq = fox_bwd(ka, va, kt, qt, dot_t, qa, dob, lse)
        d_fl, dbf = fox_post(dcq, dck, proj, bf_row[l])
        g_bf[l] = dbf[0, :C_HEADS]
        dproj = jnp.concatenate([d_a, d_fl, d_b, d_q, d_k, d_v, d_z], axis=1)
        dx, g_norm[l] = inproj_bwd_x(dproj, wi_int, x_in, norm_g[l:l + 1], dx, l)
        dwi = inproj_bwd_w(h, dproj, l, depth, dwi)

    gwi = split_w_in_grad(dwi, w_in.shape[2])
    gsm = _pack_small([
        jnp.concatenate(g_norm), jnp.stack(g_ln_g), jnp.stack(g_ln_b), jnp.stack(g_ws), jnp.stack(g_bs),
        jnp.concatenate([g_lb0, g_lb1]), jnp.stack(g_on), jnp.stack(g_bf), d_final_g, loss_tile[0, 0]])
    pwi, pwo, psm = pair_exchange(gwi, gwo, gsm)
    swi = pair_sum(pwi.reshape(2, N_CHIP * depth, D_MODEL, -1), BF16, 256, "pair_sum_w_in").reshape(gwi.shape[1:])
    swo = pair_sum(pwo.reshape(2, N_CHIP * depth, -1, D_MODEL), BF16, gwo.shape[3], "pair_sum_w_out").reshape(gwo.shape[1:])
    ssm = pair_sum(psm[:, None], F32, _SMALL_ROWS, "pair_sum_small")[0]
    rwi, rwo, rsm = chip_exchange(swi, swo, ssm)

    small_w = (norm_g, gmlp_ln_g, gmlp_ln_b, gmlp_w_s, gmlp_b_s, hgrn_lb, hgrn_onorm_g, fox_b_f, final_norm_g)
    small_m = (m_norm_g, m_gmlp_ln_g, m_gmlp_ln_b, m_gmlp_w_s, m_gmlp_b_s, m_hgrn_lb, m_hgrn_onorm_g, m_fox_b_f, m_final_norm_g)
    small_v = (v_norm_g, v_gmlp_ln_g, v_gmlp_ln_b, v_gmlp_w_s, v_gmlp_b_s, v_hgrn_lb, v_hgrn_onorm_g, v_fox_b_f, v_final_norm_g)
    zero = jnp.zeros((), F32)
    res_wi = adam_reduce(rwi, w_in, m_w_in, v_w_in, 256, "adam_w_in")
    res_wo = adam_reduce(rwo, w_out, m_w_out, v_w_out, w_out.shape[1], "adam_w_out")
    res_sm = adam_reduce(rsm[:, None], _pack_small(small_w + (zero,))[None], _pack_small(small_m + (zero,))[None],
                         _pack_small(small_v + (zero,))[None], _SMALL_ROWS, "adam_small")
    res_sm = [_unpack_small(r[0]) for r in res_sm]

    def group(i):
        s = res_sm[i]
        return [s["norm_g"], res_wi[i], res_wo[i], s["gmlp_ln_g"], s["gmlp_ln_b"], s["gmlp_w_s"], s["gmlp_b_s"],
                s["hgrn_lb"], s["hgrn_onorm_g"], s["fox_b_f"], s["final_norm_g"]]

    return (res_sm[0]["loss"], dx[None], *group(0), *group(1), *group(2), *group(3))
```

```python
import functools

import jax
import jax.numpy as jnp
import numpy as np
from jax import lax
from jax.experimental import pallas as pl
from jax.experimental.pallas import tpu as pltpu

F32 = jnp.float32
BF16 = jnp.bfloat16
HI = lax.Precision.HIGHEST

NORM_EPS = 1e-6
F_FLOOR = 1e-30
CHUNK = 128
LANES = 128
VMEM_LIMIT = 56 * 1024 * 1024


def _cparams(*sem):
    return pltpu.CompilerParams(dimension_semantics=sem, vmem_limit_bytes=VMEM_LIMIT)


def _dot(a, b, dims=(((1,), (0,)), ((), ())), precision=None):
    return lax.dot_general(a, b, dims, precision=precision, preferred_element_type=F32)


_NT = (((1,), (1,)), ((), ()))
_TN = (((0,), (0,)), ((), ()))


def _bd(a, b):
    return _dot(a.astype(BF16), b.astype(BF16))


def _group_mean_matrix(width, group):
    idx = np.arange(width) // group
    return jnp.asarray((idx[:, None] == idx[None, :]).astype(np.float32) / group)


def _group_ones_matrix(width, group):
    idx = np.arange(width) // group
    return jnp.asarray((idx[:, None] == idx[None, :]).astype(np.float32))


A_WIDTH = 256
A_GROUPS = 4
A_GDIM = 64


def _gmlp_chunk(x3, ln_g, ln_b, w_s, bs_t, mean_m, gind):
    u = jax.nn.gelu(x3[:, :A_WIDTH])
    v = jax.nn.gelu(x3[:, A_WIDTH:2 * A_WIDTH])
    z = x3[:, 2 * A_WIDTH:]
    mu = _dot(v, mean_m, precision=HI)
    d = v - mu
    var = _dot(d * d, mean_m, precision=HI)
    vn = d * lax.rsqrt(var + NORM_EPS) * ln_g + ln_b
    vnb = vn.astype(BF16)
    row = lax.broadcasted_iota(jnp.int32, (CHUNK, CHUNK), 0)
    col = lax.broadcasted_iota(jnp.int32, (CHUNK, CHUNK), 1)
    causal = row >= col
    lane_g = lax.shift_right_logical(lax.broadcasted_iota(jnp.int32, (CHUNK, A_WIDTH), 1), 6)
    mixed = _dot(bs_t, gind, precision=HI)
    for g in range(A_GROUPS):
        wc = jnp.where(causal, w_s[g], 0.0).astype(BF16)
        mixed = mixed + jnp.where(lane_g == g, _dot(wc, vnb), 0.0)
    return u * mixed * jax.nn.silu(z)


def _gmlp_consts():
    gind = np.zeros((LANES, A_WIDTH), np.float32)
    for g in range(A_GROUPS):
        gind[g, g * A_GDIM:(g + 1) * A_GDIM] = 1.0
    return _group_mean_matrix(A_WIDTH, A_GDIM), jnp.asarray(gind)


def _full(shape):
    return pl.BlockSpec(shape, lambda *_: (0,) * len(shape))


def gmlp_fwd(proj, ln_g, ln_b, w_s, bs_t):
    seq = proj.shape[0]
    mean_m, gind = _gmlp_consts()

    def body(x_ref, g_ref, b_ref, w_ref, bs_ref, m_ref, gi_ref, y_ref):
        y = _gmlp_chunk(x_ref[...], g_ref[...], b_ref[...], w_ref[...], bs_ref[...], m_ref[...], gi_ref[...])
        y_ref[...] = y.astype(BF16)

    return pl.pallas_call(
        body,
        name="gmlp_fwd",
        grid=(seq // CHUNK,),
        in_specs=[
            pl.BlockSpec((CHUNK, 3 * A_WIDTH), lambda n: (n, 0)),
            _full((1, A_WIDTH)), _full((1, A_WIDTH)), _full((A_GROUPS, CHUNK, CHUNK)), _full((CHUNK, LANES)),
            _full((A_WIDTH, A_WIDTH)), _full((LANES, A_WIDTH)),
        ],
        out_specs=pl.BlockSpec((CHUNK, A_WIDTH), lambda n: (n, 0)),
        out_shape=jax.ShapeDtypeStruct((seq, A_WIDTH), BF16),
        compiler_params=_cparams("parallel"),
    )(proj, ln_g, ln_b, w_s, bs_t, mean_m, gind)


def gmlp_bwd(proj, dy, ln_g, ln_b, w_s, bs_t):
    seq = proj.shape[0]
    mean_m, gind = _gmlp_consts()

    def body(x_ref, dy_ref, g_ref, b_ref, w_ref, bs_ref, m_ref, gi_ref, dx_ref, dg_ref, db_ref, dw_ref, dbs_ref):
        fn = functools.partial(_gmlp_chunk, mean_m=m_ref[...], gind=gi_ref[...])
        _, vjp = jax.vjp(fn, x_ref[...], g_ref[...], b_ref[...], w_ref[...], bs_ref[...])
        dx, dg, db, dw, dbs = vjp(dy_ref[...])
        dx_ref[...] = dx.astype(BF16)

        @pl.when(pl.program_id(0) == 0)
        def _():
            dg_ref[...] = jnp.zeros_like(dg_ref)
            db_ref[...] = jnp.zeros_like(db_ref)
            dw_ref[...] = jnp.zeros_like(dw_ref)
            dbs_ref[...] = jnp.zeros_like(dbs_ref)

        dg_ref[...] += dg
        db_ref[...] += db
        dw_ref[...] += dw
        dbs_ref[...] += dbs

    return pl.pallas_call(
        body,
        name="gmlp_bwd",
        grid=(seq // CHUNK,),
        in_specs=[
            pl.BlockSpec((CHUNK, 3 * A_WIDTH), lambda n: (n, 0)),
            pl.BlockSpec((CHUNK, A_WIDTH), lambda n: (n, 0)),
            _full((1, A_WIDTH)), _full((1, A_WIDTH)), _full((A_GROUPS, CHUNK, CHUNK)), _full((CHUNK, LANES)),
            _full((A_WIDTH, A_WIDTH)), _full((LANES, A_WIDTH)),
        ],
        out_specs=[
            pl.BlockSpec((CHUNK, 3 * A_WIDTH), lambda n: (n, 0)),
            _full((1, A_WIDTH)), _full((1, A_WIDTH)), _full((A_GROUPS, CHUNK, CHUNK)), _full((CHUNK, LANES)),
        ],
        out_shape=[
            jax.ShapeDtypeStruct((seq, 3 * A_WIDTH), BF16),
            jax.ShapeDtypeStruct((1, A_WIDTH), F32), jax.ShapeDtypeStruct((1, A_WIDTH), F32),
            jax.ShapeDtypeStruct((A_GROUPS, CHUNK, CHUNK), F32), jax.ShapeDtypeStruct((CHUNK, LANES), F32),
        ],
        compiler_params=_cparams("arbitrary"),
    )(proj, dy, ln_g, ln_b, w_s, bs_t, mean_m, gind)


B_WIDTH = 256
B_HEADS = 4
B_KDIM = 64
B_LEVELS = (64, 32, 16, 8, 4, 2, 1)


def _hgrn_consts():
    t = np.arange(CHUNK)
    u = t[None, :]
    mats = [np.tril(np.ones((CHUNK, CHUNK), np.float32))]
    for m in B_LEVELS:
        p = (t // (2 * m)) * (2 * m) + m - 1
        right = (t % (2 * m)) >= m
        sel = np.where(right[:, None], (u > p[:, None]) & (u <= t[:, None]), (u > t[:, None]) & (u <= p[:, None]))
        mats.append(sel.astype(np.float32))
    return jnp.asarray(np.concatenate(mats, 0)), _group_ones_matrix(B_WIDTH, B_KDIM)


def _hgrn_lower_bound(lb0, lb1, layer):
    mx = jnp.maximum(lb0, lb1)
    e0 = jnp.exp(lb0 - mx)
    e1 = jnp.exp(lb1 - mx)
    p0 = e0 / (e0 + e1)
    p1 = e1 / (e0 + e1)
    cs = p0 if layer == 0 else p0 + p1
    return jnp.clip(cs - p0, 0.0, 1.0 - 1e-6)


def _hgrn_chunk(x4, st, lb0, lb1, onorm, layer, tstack, ones_bd):
    q_raw, fl, v, zg = (x4[:, i * B_WIDTH:(i + 1) * B_WIDTH] for i in range(4))
    lb = _hgrn_lower_bound(lb0, lb1, layer)
    q = jax.nn.silu(q_raw) * (B_KDIM ** -0.5)
    f = lb + (1.0 - lb) * jax.nn.sigmoid(fl)
    logf = jnp.log(jnp.maximum(f, F_FLOOR))
    k = (1.0 - lb) * jax.nn.sigmoid(-fl)
    dall = _dot(tstack, logf, precision=HI)
    b = dall[:CHUNK]
    b_last = jnp.sum(logf, axis=0, keepdims=True)
    vb = v.astype(BF16)

    lane_h = lax.shift_right_logical(lax.broadcasted_iota(jnp.int32, (CHUNK, B_WIDTH), 1), 6)
    row = lax.broadcasted_iota(jnp.int32, (CHUNK, B_WIDTH), 0)
    srow = lax.broadcasted_iota(jnp.int32, (B_HEADS * CHUNK, CHUNK), 0) & (CHUNK - 1)
    scol = lax.broadcasted_iota(jnp.int32, (B_HEADS * CHUNK, CHUNK), 1)

    def heads_on_rows(a):
        return jnp.concatenate([jnp.where(lane_h == h, a, 0.0) for h in range(B_HEADS)], axis=0)

    def heads_from_rows(r):
        out = jnp.where(lane_h == 0, r[:CHUNK], 0.0)
        for h in range(1, B_HEADS):
            out = out + jnp.where(lane_h == h, r[h * CHUNK:(h + 1) * CHUNK], 0.0)
        return out

    o = lax.dot_general((q * jnp.exp(b)).astype(BF16), st.astype(BF16), _NT, preferred_element_type=F32)
    scores = jnp.zeros((B_HEADS * CHUNK, CHUNK), F32)
    for li, m in enumerate(B_LEVELS):
        e = jnp.exp(dall[(li + 1) * CHUNK:(li + 2) * CHUNK])
        right = (row & (2 * m - 1)) >= m
        qt = jnp.where(right, q * e, 0.0)
        kt = jnp.where(right, 0.0, k * e)
        sc = lax.dot_general(heads_on_rows(qt).astype(BF16), kt.astype(BF16), _NT, preferred_element_type=F32)
        sh = int(np.log2(2 * m))
        same = lax.shift_right_logical(srow, sh) == lax.shift_right_logical(scol, sh)
        scores = scores + jnp.where(same, sc, 0.0)
    o = o + heads_from_rows(_dot(scores.astype(BF16), vb))
    o = o + _dot(q * k, ones_bd, precision=HI) * v

    kv = lax.dot_general(vb, (k * jnp.exp(b_last - b)).astype(BF16), _TN, preferred_element_type=F32)
    st_new = st * jnp.exp(b_last) + jnp.where(ones_bd > 0.5, kv, 0.0)

    ms = _dot(o * o, ones_bd, precision=HI) * (1.0 / B_KDIM)
    y = o * lax.rsqrt(ms + NORM_EPS) * onorm * jax.nn.silu(zg)
    return y, st_new


def hgrn_fwd(proj, lb0, lb1, onorm, layer):
    seq = proj.shape[0]
    nc = seq // CHUNK
    tstack, ones_bd = _hgrn_consts()

    def body(x_ref, lb0_ref, lb1_ref, on_ref, t_ref, e_ref, y_ref, st_out_ref, st_ref):
        @pl.when(pl.program_id(0) == 0)
        def _():
            st_ref[...] = jnp.zeros_like(st_ref)

        st = st_ref[...]
        st_out_ref[0] = st
        y, st_new = _hgrn_chunk(x_ref[...], st, lb0_ref[...], lb1_ref[...], on_ref[...], layer, t_ref[...], e_ref[...])
        y_ref[...] = y.astype(BF16)
        st_ref[...] = st_new

    return pl.pallas_call(
        body,
        name=f"hgrn_fwd_{layer}",
        grid=(nc,),
        in_specs=[
            pl.BlockSpec((CHUNK, 4 * B_WIDTH), lambda n: (n, 1)),
            _full((1, B_WIDTH)), _full((1, B_WIDTH)), _full((1, B_WIDTH)),
            _full(((len(B_LEVELS) + 1) * CHUNK, CHUNK)), _full((B_WIDTH, B_WIDTH)),
        ],
        out_specs=[
            pl.BlockSpec((CHUNK, B_WIDTH), lambda n: (n, 0)),
            pl.BlockSpec((1, B_WIDTH, B_WIDTH), lambda n: (n, 0, 0)),
        ],
        out_shape=[jax.ShapeDtypeStruct((seq, B_WIDTH), BF16), jax.ShapeDtypeStruct((nc, B_WIDTH, B_WIDTH), F32)],
        scratch_shapes=[pltpu.VMEM((B_WIDTH, B_WIDTH), F32)],
        compiler_params=_cparams("arbitrary"),
    )(proj, lb0, lb1, onorm, tstack, ones_bd)


def hgrn_bwd(proj, states, dy, lb0, lb1, onorm, layer):
    seq = proj.shape[0]
    nc = seq // CHUNK
    tstack, ones_bd = _hgrn_consts()

    def body(x_ref, st_in_ref, dy_ref, lb0_ref, lb1_ref, on_ref, t_ref, e_ref, dx_ref, d0_ref, d1_ref, don_ref, dst_ref):
        @pl.when(pl.program_id(0) == 0)
        def _():
            dst_ref[...] = jnp.zeros_like(dst_ref)
            d0_ref[...] = jnp.zeros_like(d0_ref)
            d1_ref[...] = jnp.zeros_like(d1_ref)
            don_ref[...] = jnp.zeros_like(don_ref)

        fn = functools.partial(_hgrn_chunk, layer=layer, tstack=t_ref[...], ones_bd=e_ref[...])
        _, vjp = jax.vjp(fn, x_ref[...], st_in_ref[0], lb0_ref[...], lb1_ref[...], on_ref[...])
        dx, dst, d0, d1, don = vjp((dy_ref[...], dst_ref[...]))
        dx_ref[...] = dx.astype(BF16)
        dst_ref[...] = dst
        d0_ref[...] += d0
        d1_ref[...] += d1
        don_ref[...] += don

    rev = lambda n: nc - 1 - n
    return pl.pallas_call(
        body,
        name=f"hgrn_bwd_{layer}",
        grid=(nc,),
        in_specs=[
            pl.BlockSpec((CHUNK, 4 * B_WIDTH), lambda n: (rev(n), 1)),
            pl.BlockSpec((1, B_WIDTH, B_WIDTH), lambda n: (rev(n), 0, 0)),
            pl.BlockSpec((CHUNK, B_WIDTH), lambda n: (rev(n), 1)),
            _full((1, B_WIDTH)), _full((1, B_WIDTH)), _full((1, B_WIDTH)),
            _full(((len(B_LEVELS) + 1) * CHUNK, CHUNK)), _full((B_WIDTH, B_WIDTH)),
        ],
        out_specs=[
            pl.BlockSpec((CHUNK, 4 * B_WIDTH), lambda n: (rev(n), 0)),
            _full((1, B_WIDTH)), _full((1, B_WIDTH)), _full((1, B_WIDTH)),
        ],
        out_shape=[jax.ShapeDtypeStruct((seq, 4 * B_WIDTH), BF16)] + [jax.ShapeDtypeStruct((1, B_WIDTH), F32)] * 3,
        scratch_shapes=[pltpu.VMEM((B_WIDTH, B_WIDTH), F32)],
        compiler_params=_cparams("arbitrary"),
    )(proj, states, dy, lb0, lb1, onorm, tstack, ones_bd)


D_MODEL = 1024
D_INT = 4096


def _rms_stats(xf):
    r = lax.rsqrt(jnp.mean(xf * xf, axis=-1, keepdims=True) + NORM_EPS)
    return r, xf * r


def _rms_bwd(dy, g, r, xh):
    u = dy * g
    return r * (u - xh * jnp.mean(u * xh, axis=-1, keepdims=True))


def inproj(x, g, w, layer):
    seq = x.shape[0]
    tm, tn = min(seq, 1024), 512

    def body(x_ref, g_ref, w_ref, p_ref, h_ref):
        @pl.when(pl.program_id(1) == 0)
        def _():
            _, xh = _rms_stats(x_ref[...])
            h_ref[...] = (xh * g_ref[...]).astype(BF16)

        p_ref[...] = _dot(h_ref[...], w_ref[0])

    return pl.pallas_call(
        body,
        name="inproj",
        grid=(seq // tm, D_INT // tn),
        in_specs=[
            pl.BlockSpec((tm, D_MODEL), lambda i, j: (i, 0)),
            _full((1, D_MODEL)),
            pl.BlockSpec((1, D_MODEL, tn), lambda i, j: (layer, 0, j)),
        ],
        out_specs=[pl.BlockSpec((tm, tn), lambda i, j: (i, j)), pl.BlockSpec((tm, D_MODEL), lambda i, j: (i, 0))],
        out_shape=[jax.ShapeDtypeStruct((seq, D_INT), F32), jax.ShapeDtypeStruct((seq, D_MODEL), BF16)],
        compiler_params=_cparams("parallel", "arbitrary"),
    )(x, g, w)


def outproj(x, ya, yb, o, proj, wo, layer):
    seq = x.shape[0]
    tm = min(seq, 512)
    blk = wo.shape[2]

    def body(x_ref, ya_ref, yb_ref, o_ref, z_ref, w_ref, xn_ref, y_ref):
        yc = (o_ref[...] * jax.nn.silu(z_ref[...])).astype(BF16)
        y = jnp.concatenate([ya_ref[...], yb_ref[...], yc], axis=1)
        y_ref[...] = y
        w = jnp.concatenate([w_ref[d, 0] for d in range(N_DEV)], axis=0)
        xn_ref[...] = x_ref[...] + _dot(y, w)

    return pl.pallas_call(
        body,
        name="outproj",
        grid=(seq // tm,),
        in_specs=[
            pl.BlockSpec((tm, D_MODEL), lambda i: (i, 0)),
            pl.BlockSpec((tm, 256), lambda i: (i, 0)),
            pl.BlockSpec((tm, 256), lambda i: (i, 0)),
            pl.BlockSpec((tm, 512), lambda i: (i, 0)),
            pl.BlockSpec((tm, 512), lambda i: (i, 7)),
            pl.BlockSpec((N_DEV, 1, blk, D_MODEL), lambda i: (0, layer, 0, 0)),
        ],
        out_specs=[pl.BlockSpec((tm, D_MODEL), lambda i: (i, 0)), pl.BlockSpec((tm, D_MODEL), lambda i: (i, 0))],
        out_shape=[jax.ShapeDtypeStruct((seq, D_MODEL), F32), jax.ShapeDtypeStruct((seq, D_MODEL), BF16)],
        compiler_params=_cparams("parallel"),
    )(x, ya, yb, o, proj, wo)


def outproj_bwd(dx, y, wo, layer, stacked=None):
    seq = dx.shape[0]
    ts = min(seq, 512)
    _, depth, blk, _ = wo.shape

    def body(dx_ref, y_ref, w_ref, *refs):
        dy_ref, dw_ref = refs[-2:]

        @pl.when(pl.program_id(0) == 0)
        def _():
            dw_ref[...] = jnp.zeros_like(dw_ref)

        dxb = dx_ref[...].astype(BF16)
        w = jnp.concatenate([w_ref[d, 0] for d in range(N_DEV)], axis=0)
        dy_ref[...] = lax.dot_general(dxb, w, _NT, preferred_element_type=F32)
        dw = lax.dot_general(y_ref[...], dxb, _TN, preferred_element_type=F32)
        for d in range(N_DEV):
            dw_ref[d % 2, d // 2, 0] += dw[d * blk:(d + 1) * blk]

    carried = () if stacked is None else (stacked,)
    out_shape = [jax.ShapeDtypeStruct((seq, D_MODEL), F32), jax.ShapeDtypeStruct((2, N_CHIP, depth, blk, D_MODEL), F32)]
    return pl.pallas_call(
        body,
        name="outproj_bwd",
        grid=(seq // ts,),
        in_specs=[
            pl.BlockSpec((ts, D_MODEL), lambda i: (i, 0)),
            pl.BlockSpec((ts, D_MODEL), lambda i: (i, 0)),
            pl.BlockSpec((N_DEV, 1, blk, D_MODEL), lambda i: (0, layer, 0, 0)),
        ] + [_ANY] * len(carried),
        out_specs=[pl.BlockSpec((ts, D_MODEL), lambda i: (i, 0)),
                   pl.BlockSpec((2, N_CHIP, 1, blk, D_MODEL), lambda i: (0, 0, layer, 0, 0))],
        out_shape=out_shape,
        input_output_aliases={3: 1} if carried else {},
        compiler_params=_cparams("arbitrary"),
    )(dx, y, wo, *carried)


def inproj_bwd_x(dproj, w, x, g, dx_in, layer):
    seq = x.shape[0]
    tm, tk = min(seq, 512), 1024
    nk = D_INT // tk

    def body(dp_ref, w_ref, x_ref, g_ref, dxin_ref, dx_ref, dg_ref, acc_ref):
        k = pl.program_id(1)

        @pl.when(k == 0)
        def _():
            acc_ref[...] = jnp.zeros_like(acc_ref)

        acc_ref[...] += lax.dot_general(dp_ref[...], w_ref[0], _NT, preferred_element_type=F32)

        @pl.when(k == nk - 1)
        def _():
            @pl.when(pl.program_id(0) == 0)
            def _():
                dg_ref[...] = jnp.zeros_like(dg_ref)

            dh = acc_ref[...]
            g = g_ref[...]
            r, xh = _rms_stats(x_ref[...])
            dg_ref[...] += jnp.sum(dh * xh, axis=0, keepdims=True)
            dx_ref[...] = dxin_ref[...] + _rms_bwd(dh, g, r, xh)

    return pl.pallas_call(
        body,
        name="inproj_bwd_x",
        grid=(seq // tm, nk),
        in_specs=[
            pl.BlockSpec((tm, tk), lambda i, k: (i, k)),
            pl.BlockSpec((1, D_MODEL, tk), lambda i, k: (layer, 0, k)),
            pl.BlockSpec((tm, D_MODEL), lambda i, k: (i, 0)),
            _full((1, D_MODEL)),
            pl.BlockSpec((tm, D_MODEL), lambda i, k: (i, 0)),
        ],
        out_specs=[pl.BlockSpec((tm, D_MODEL), lambda i, k: (i, 0)), _full((1, D_MODEL))],
        out_shape=[jax.ShapeDtypeStruct((seq, D_MODEL), F32), jax.ShapeDtypeStruct((1, D_MODEL), F32)],
        scratch_shapes=[pltpu.VMEM((tm, D_MODEL), F32)],
        compiler_params=_cparams("arbitrary", "arbitrary"),
    )(dproj, w, x, g, dx_in)


def inproj_bwd_w(h, dproj, layer, depth, stacked=None):
    seq = h.shape[0]
    ts, tn = min(seq, 1024), 512

    def body(h_ref, dp_ref, *refs):
        dw_ref = refs[-1]

        @pl.when(pl.program_id(1) == 0)
        def _():
            dw_ref[...] = jnp.zeros_like(dw_ref)

        dw_ref[0] += lax.dot_general(h_ref[...], dp_ref[...], _TN, preferred_element_type=F32)

    carried = () if stacked is None else (stacked,)
    return pl.pallas_call(
        body,
        name="inproj_bwd_w",
        grid=(D_INT // tn, seq // ts),
        in_specs=[pl.BlockSpec((ts, D_MODEL), lambda j, s: (s, 0)), pl.BlockSpec((ts, tn), lambda j, s: (s, j))]
        + [_ANY] * len(carried),
        out_specs=pl.BlockSpec((1, D_MODEL, tn), lambda j, s: (layer, 0, j)),
        out_shape=jax.ShapeDtypeStruct((depth, D_MODEL, D_INT), F32),
        input_output_aliases={2: 0} if carried else {},
        compiler_params=_cparams("parallel", "arbitrary"),
    )(h, dproj, *carried)


N_IN = 3848


def _internal_of(col):
    return col if col < 768 else (col + 256 if col < 3840 else 768 + col - 3840)


def _column_runs(n_shard):
    runs = []
    for d in range(N_IN // n_shard):
        mine = []
        for j in range(n_shard):
            ci = _internal_of(d * n_shard + j)
            if mine and mine[-1][0] + mine[-1][1] == ci:
                mine[-1][1] += 1
            else:
                mine.append([ci, 1, j])
        runs.append(mine)
    return runs


def assemble_w_in(wi_all):
    n_dev, depth, _, n_shard = wi_all.shape
    tr = 256
    pieces = [[] for _ in range(D_INT // LANES)]
    for d, mine in enumerate(_column_runs(n_shard)):
        for ci, ln, off in mine:
            while ln > 0:
                blk, at = divmod(ci, LANES)
                take = min(ln, LANES - at)
                pieces[blk].append((at, take, d, off))
                ci, ln, off = ci + take, ln - take, off + take

    def body(x_ref, o_ref):
        for blk, parts in enumerate(pieces):
            vals, at = [], 0
            for start, ln, d, off in sorted(parts):
                if start > at:
                    vals.append(jnp.zeros((tr, start - at), BF16))
                vals.append(x_ref[d, 0, :, off:off + ln])
                at = start + ln
            if at < LANES:
                vals.append(jnp.zeros((tr, LANES - at), BF16))
            o_ref[0, :, blk * LANES:(blk + 1) * LANES] = vals[0] if len(vals) == 1 else jnp.concatenate(vals, axis=1)

    return pl.pallas_call(
        body,
        name="assemble_w_in",
        grid=(depth, D_MODEL // tr),
        in_specs=[pl.BlockSpec((n_dev, 1, tr, n_shard), lambda l, r: (0, l, r, 0))],
        out_specs=pl.BlockSpec((1, tr, D_INT), lambda l, r: (l, r, 0)),
        out_shape=jax.ShapeDtypeStruct((depth, D_MODEL, D_INT), BF16),
        compiler_params=_cparams("parallel", "parallel"),
    )(wi_all)


def split_w_in_grad(dwi, n_shard):
    depth = dwi.shape[0]
    tr = 256
    runs = _column_runs(n_shard)

    def body(x_ref, o_ref):
        for d, mine in enumerate(runs):
            for ci, ln, off in mine:
                o_ref[d % 2, d // 2, 0, :, off:off + ln] = x_ref[0, :, ci:ci + ln]

    return pl.pallas_call(
        body,
        name="split_w_in_grad",
        grid=(depth, D_MODEL // tr),
        in_specs=[pl.BlockSpec((1, tr, D_INT), lambda l, r: (l, r, 0))],
        out_specs=pl.BlockSpec((2, N_CHIP, 1, tr, n_shard), lambda l, r: (0, 0, l, r, 0)),
        out_shape=jax.ShapeDtypeStruct((2, N_CHIP, depth, D_MODEL, n_shard), F32),
        compiler_params=_cparams("parallel", "parallel"),
    )(dwi)


def final_loss(x, g, tgt):
    seq = x.shape[0]
    tm = min(seq, 512)

    def body(x_ref, g_ref, t_ref, dx_ref, dg_ref, loss_ref):
        @pl.when(pl.program_id(0) == 0)
        def _():
            dg_ref[...] = jnp.zeros_like(dg_ref)
            loss_ref[...] = jnp.zeros_like(loss_ref)

        g = g_ref[...]
        r, xh = _rms_stats(x_ref[...])
        err = xh * g - t_ref[...]
        sq = jnp.sum(jnp.sum(err * err, axis=1, keepdims=True), axis=0, keepdims=True)
        loss_ref[...] += jnp.broadcast_to(sq * (0.5 / D_MODEL), loss_ref.shape)
        dout = err * (1.0 / D_MODEL)
        dg_ref[...] += jnp.sum(dout * xh, axis=0, keepdims=True)
        dx_ref[...] = _rms_bwd(dout, g, r, xh)

    return pl.pallas_call(
        body,
        name="final_loss",
        grid=(seq // tm,),
        in_specs=[pl.BlockSpec((tm, D_MODEL), lambda i: (i, 0)), _full((1, D_MODEL)), pl.BlockSpec((tm, D_MODEL), lambda i: (i, 0))],
        out_specs=[pl.BlockSpec((tm, D_MODEL), lambda i: (i, 0)), _full((1, D_MODEL)), _full((8, LANES))],
        out_shape=[jax.ShapeDtypeStruct((seq, D_MODEL), F32), jax.ShapeDtypeStruct((1, D_MODEL), F32), jax.ShapeDtypeStruct((8, LANES), F32)],
        compiler_params=_cparams("arbitrary"),
    )(x, g, tgt)


C_WIDTH = 512
C_HEADS = 8
C_HDIM = 64
C_PAIRS = C_HEADS // 2
C_BQ = 512
C_TAIL = 16
C_KG = 2


def _split3(x):
    hi = x.astype(BF16)
    r = x - hi.astype(F32)
    mid = r.astype(BF16)
    return hi, mid, (r - mid.astype(F32)).astype(BF16)


def _piece_selectors():
    sel = np.zeros((C_HEADS, 3 * LANES, LANES), np.float32)
    for p in range(C_PAIRS):
        for e in range(2):
            for t in range(3):
                sel[2 * p + e, t * LANES + 2 * p + e, 3 * e + t] = -1.0
    return sel


def fox_prep(proj, bf_row):
    seq = proj.shape[0]
    nblk = seq // CHUNK
    tril = jnp.asarray(np.tril(np.ones((CHUNK, CHUNK), np.float32)))
    sel = jnp.asarray(_piece_selectors(), BF16)
    rows_t = CHUNK + C_TAIL

    def body(fl_ref, q_ref, k_ref, v_ref, bf_ref, l_ref, sel_ref, ka_ref, va_ref, vt_ref, kt_ref, qt_ref, qa_ref, carry_ref):
        @pl.when(pl.program_id(0) == 0)
        def _():
            carry_ref[...] = jnp.zeros_like(carry_ref)

        lf = jax.nn.log_sigmoid(fl_ref[:, :LANES] + bf_ref[...])
        c = _dot(l_ref[...], lf, precision=HI) + carry_ref[...]
        carry_ref[...] += jnp.sum(lf, axis=0, keepdims=True)
        c3 = jnp.concatenate(_split3(c), axis=1)
        lane = lax.broadcasted_iota(jnp.int32, (CHUNK, LANES), 1)
        row = lax.broadcasted_iota(jnp.int32, (CHUNK, LANES), 0)
        r16 = lax.broadcasted_iota(jnp.int32, (C_TAIL, 2 * CHUNK), 0)
        l16 = lax.broadcasted_iota(jnp.int32, (C_TAIL, 2 * CHUNK), 1)
        zero = jnp.zeros((CHUNK, LANES), BF16)
        one = jnp.ones((CHUNK, LANES), BF16)

        def by_keys(x, right_a, right_b):
            xb = x.astype(BF16)
            top = jnp.concatenate([jnp.where(lane < C_HDIM, xb, zero), right_a], axis=1)
            return jnp.concatenate([top, jnp.concatenate([jnp.where(lane < C_HDIM, zero, xb), right_b], axis=1)], axis=0)

        def by_lanes(x, tail):
            xt = x.T.astype(BF16)
            main = jnp.concatenate([jnp.where(row < C_HDIM, xt, zero), jnp.where(row < C_HDIM, zero, xt)], axis=1)
            return jnp.concatenate([main, tail], axis=0)

        for p in range(C_PAIRS):
            cols = slice(p * LANES, (p + 1) * LANES)
            q2, k2, v2 = q_ref[:, cols] * (C_HDIM ** -0.5), k_ref[:, cols], v_ref[:, cols]
            negc = [_dot(c3, sel_ref[2 * p + e]).astype(BF16) for e in range(2)]
            ones3 = [jnp.where((lane >= 3 * e) & (lane < 3 * e + 3), one, zero) for e in range(2)]
            tail = jnp.where(((r16 == 2 * p) & (l16 < CHUNK)) | ((r16 == 2 * p + 1) & (l16 >= CHUNK)), 1.0, 0.0).astype(BF16)
            ka_ref[p, 0] = by_keys(k2, negc[0], negc[1])
            va_ref[p, 0] = by_keys(v2, ones3[0], ones3[1])
            kt_ref[p, 0] = by_lanes(k2, tail)
            vt_ref[p, 0] = by_lanes(v2, tail)
            qt_ref[p] = jnp.concatenate([q2.T.astype(BF16), jnp.where(row < 6, one, zero)], axis=0)
            qa_ref[p] = jnp.concatenate([q2.astype(BF16), jnp.where((lane == 2 * p) | (lane == 2 * p + 1), one, zero)], axis=1)

    wide = lambda j: pl.BlockSpec((CHUNK, C_WIDTH), lambda n: (n, j))
    sq = lambda r: pl.BlockSpec((C_PAIRS, 1, r, 2 * CHUNK), lambda n: (0, n, 0, 0))
    return pl.pallas_call(
        body,
        name="fox_prep",
        grid=(nblk,),
        in_specs=[pl.BlockSpec((CHUNK, 256), lambda n: (n, 3)), wide(4), wide(5), wide(6), _full((1, LANES)),
                  _full((CHUNK, CHUNK)), _full((C_HEADS, 3 * LANES, LANES))],
        out_specs=[sq(2 * CHUNK), sq(2 * CHUNK), sq(rows_t), sq(rows_t),
                   pl.BlockSpec((C_PAIRS, 2 * CHUNK, CHUNK), lambda n: (0, 0, n)),
                   pl.BlockSpec((C_PAIRS, CHUNK, 2 * CHUNK), lambda n: (0, n, 0))],
        out_shape=[jax.ShapeDtypeStruct((C_PAIRS, nblk, 2 * CHUNK, 2 * CHUNK), BF16)] * 2
        + [jax.ShapeDtypeStruct((C_PAIRS, nblk, rows_t, 2 * CHUNK), BF16)] * 2
        + [jax.ShapeDtypeStruct((C_PAIRS, 2 * CHUNK, seq), BF16), jax.ShapeDtypeStruct((C_PAIRS, seq, 2 * CHUNK), BF16)],
        scratch_shapes=[pltpu.VMEM((1, LANES), F32)],
        compiler_params=_cparams("arbitrary"),
    )(proj, proj, proj, proj, bf_row, tril, sel)


def _visible(shape, key0, query0):
    key = key0 + (lax.broadcasted_iota(jnp.int32, shape, 0) & (CHUNK - 1))
    return key <= query0 + lax.broadcasted_iota(jnp.int32, shape, 1)


def _rows_ab(a, b, n):
    return jnp.concatenate([jnp.broadcast_to(a, (C_HDIM, n)), jnp.broadcast_to(b, (C_HDIM, n))], axis=0)


def fox_fwd(qt, ka, vt):
    seq = qt.shape[2]
    nblk = seq // CHUNK
    bq = min(C_BQ, seq)
    grp = bq // CHUNK
    rows_t = CHUNK + C_TAIL

    def body(qt_ref, ka_ref, vt_ref, o_ref, lse_ref, acc_ref):
        p, i = pl.program_id(0), pl.program_id(1)
        qtile = qt_ref[0]
        r16 = lax.broadcasted_iota(jnp.int32, (C_TAIL, bq), 0)

        def group(j0, m, masked):
            ma, mb = m
            ss = []
            for g in range(grp):
                s = _dot(ka_ref[0, j0 + g], qtile)
                if masked:
                    s = jnp.where(_visible(s.shape, (j0 + g) * CHUNK, i * bq), s, -jnp.inf)
                ss.append(s)
            na, nb = ma, mb
            for s in ss:
                na = jnp.maximum(na, jnp.max(s[:CHUNK], axis=0, keepdims=True))
                nb = jnp.maximum(nb, jnp.max(s[CHUNK:], axis=0, keepdims=True))
            al_a, al_b = jnp.exp(ma - na), jnp.exp(mb - nb)
            pv = None
            for g, s in enumerate(ss):
                pt = jnp.concatenate([jnp.exp(s[:CHUNK] - na), jnp.exp(s[CHUNK:] - nb)], axis=0).astype(BF16)
                r = _dot(vt_ref[0, j0 + g], pt)
                pv = r if pv is None else pv + r
            tail = jnp.where(r16 == 2 * p, al_a, jnp.where(r16 == 2 * p + 1, al_b, 1.0))
            acc_ref[...] = acc_ref[...] * jnp.concatenate([_rows_ab(al_a, al_b, bq), tail], axis=0) + pv
            return na, nb

        acc_ref[...] = jnp.zeros_like(acc_ref)
        m = (jnp.full((1, bq), -jnp.inf, F32), jnp.full((1, bq), -jnp.inf, F32))
        m = lax.fori_loop(0, i, lambda t, m: group(t * grp, m, False), m)
        ma, mb = group(i * grp, m, True)
        tailv = acc_ref[CHUNK:rows_t, :]
        la = jnp.sum(jnp.where(r16 == 2 * p, tailv, 0.0), axis=0, keepdims=True)
        lb = jnp.sum(jnp.where(r16 == 2 * p + 1, tailv, 0.0), axis=0, keepdims=True)
        o_ref[...] = (acc_ref[0:CHUNK, :] * _rows_ab(1.0 / la, 1.0 / lb, bq)).T
        lse_ref[0, 0:1, :] = ma + jnp.log(la)
        lse_ref[0, 1:2, :] = mb + jnp.log(lb)

    return pl.pallas_call(
        body,
        name="fox_fwd",
        grid=(C_PAIRS, seq // bq),
        in_specs=[
            pl.BlockSpec((1, 2 * CHUNK, bq), lambda p, i: (p, 0, i)),
            pl.BlockSpec((1, nblk, 2 * CHUNK, 2 * CHUNK), lambda p, i: (p, 0, 0, 0)),
            pl.BlockSpec((1, nblk, rows_t, 2 * CHUNK), lambda p, i: (p, 0, 0, 0)),
        ],
        out_specs=[pl.BlockSpec((bq, LANES), lambda p, i: (i, p)), pl.BlockSpec((1, 2, bq), lambda p, i: (p, 0, i))],
        out_shape=[jax.ShapeDtypeStruct((seq, C_WIDTH), F32), jax.ShapeDtypeStruct((C_PAIRS, 2, seq), F32)],
        scratch_shapes=[pltpu.VMEM((rows_t, bq), F32)],
        compiler_params=_cparams("parallel", "arbitrary"),
    )(qt, ka, vt)


def fox_bwd_prep(dy, o, proj):
    seq = o.shape[0]
    ind = np.zeros((C_WIDTH, LANES), np.float32)
    for h in range(C_HEADS):
        ind[h * C_HDIM:(h + 1) * C_HDIM, h] = 1.0
    ind = jnp.asarray(ind, BF16)
    sel = _piece_selectors()
    sel = jnp.asarray(np.stack([sel[2 * p].T + sel[2 * p + 1].T for p in range(C_PAIRS)]), BF16)

    def body(dy_ref, o_ref, z_ref, ind_ref, sel_ref, do_ref, dz_ref, dot_ref):
        dy_c, o_v, z = dy_ref[...], o_ref[...], z_ref[...]
        sg = jax.nn.sigmoid(z)
        do = dy_c * (z * sg)
        do_ref[...] = do.astype(BF16)
        dz_ref[...] = (dy_c * o_v * (sg * (1.0 + z * (1.0 - sg)))).astype(BF16)
        prod = do * o_v
        hi = prod.astype(BF16)
        lo = (prod - hi.astype(F32)).astype(BF16)
        delta = _dot(hi, ind_ref[...]) + _dot(lo, ind_ref[...])
        d3 = jnp.concatenate(_split3(delta.T), axis=0)
        for p in range(C_PAIRS):
            tail = _dot(sel_ref[p], d3).astype(BF16)
            dot_ref[p] = jnp.concatenate([do[:, p * LANES:(p + 1) * LANES].T.astype(BF16), tail], axis=0)

    return pl.pallas_call(
        body,
        name="fox_bwd_prep",
        grid=(seq // CHUNK,),
        in_specs=[
            pl.BlockSpec((CHUNK, C_WIDTH), lambda i: (i, 1)),
            pl.BlockSpec((CHUNK, C_WIDTH), lambda i: (i, 0)),
            pl.BlockSpec((CHUNK, C_WIDTH), lambda i: (i, 7)),
            _full((C_WIDTH, LANES)), _full((C_PAIRS, LANES, 3 * LANES)),
        ],
        out_specs=[
            pl.BlockSpec((CHUNK, C_WIDTH), lambda i: (i, 0)),
            pl.BlockSpec((CHUNK, C_WIDTH), lambda i: (i, 0)),
            pl.BlockSpec((C_PAIRS, 2 * CHUNK, CHUNK), lambda i: (0, 0, i)),
        ],
        out_shape=[jax.ShapeDtypeStruct((seq, C_WIDTH), BF16)] * 2 + [jax.ShapeDtypeStruct((C_PAIRS, 2 * CHUNK, seq), BF16)],
        compiler_params=_cparams("parallel"),
    )(dy, o, proj, ind, sel)


def fox_bwd(ka, va, kt, qt, dot_t, qa, dob, lse):
    seq = qt.shape[2]
    nblk = seq // CHUNK
    bq = min(C_BQ, seq)
    nq = seq // bq
    kg = min(C_KG, nblk)
    ng = nblk // kg
    rows_t = CHUNK + C_TAIL

    def body(ka_ref, va_ref, kt_ref, qt_ref, dot_ref, qa_ref, do_ref, lse_ref,
             dq_ref, dk_ref, dv_ref, dck_ref, dcq_ref, dqt_acc, dv_acc, dka_acc):
        p, jg = pl.program_id(0), pl.program_id(1)

        @pl.when(jg == 0)
        def _():
            dqt_acc[...] = jnp.zeros_like(dqt_acc)

        dv_acc[...] = jnp.zeros_like(dv_acc)
        dka_acc[...] = jnp.zeros_like(dka_acc)

        def step(i, carry, masked):
            cols = pl.ds(pl.multiple_of(i * bq, bq), bq)
            qtile, dotile = qt_ref[0, :, cols], dot_ref[0, :, cols]
            do, qa_i = do_ref[cols, :], qa_ref[0, cols, :]
            lse2 = jnp.concatenate([jnp.broadcast_to(lse_ref[0, 0:1, cols], (CHUNK, bq)),
                                    jnp.broadcast_to(lse_ref[0, 1:2, cols], (CHUNK, bq))], axis=0)
            for kb in range(kg):
                pt = jnp.exp(_dot(ka_ref[0, kb], qtile) - lse2)
                if masked:
                    pt = jnp.where(_visible(pt.shape, (jg * kg + kb) * CHUNK, i * bq), pt, 0.0)
                ds = pt * _dot(va_ref[0, kb], dotile)
                ptb, dsb = pt.astype(BF16), ds.astype(BF16)
                dv_acc[kb] += _dot(ptb, do)
                dka_acc[kb] += _dot(dsb, qa_i)
                dqt_acc[:, cols] += _dot(kt_ref[0, kb], dsb)
            return carry

        i0 = (jg * kg * CHUNK) // bq
        step(i0, 0, True)
        lax.fori_loop(i0 + 1, nq, functools.partial(step, masked=False), 0)
        lane = lax.broadcasted_iota(jnp.int32, (CHUNK, LANES), 1)
        for kb in range(kg):
            rows = slice(kb * CHUNK, (kb + 1) * CHUNK)
            dk_ref[rows, :] = jnp.where(lane < C_HDIM, dka_acc[kb, 0:CHUNK, 0:LANES], dka_acc[kb, CHUNK:, 0:LANES]).astype(BF16)
            dv_ref[rows, :] = jnp.where(lane < C_HDIM, dv_acc[kb, 0:CHUNK, :], dv_acc[kb, CHUNK:, :]).astype(BF16)
            dck_ref[0, rows, :] = (jnp.where(lane == 2 * p, dka_acc[kb, 0:CHUNK, LANES:], 0.0)
                                   + jnp.where(lane == 2 * p + 1, dka_acc[kb, CHUNK:, LANES:], 0.0))

        @pl.when(jg == ng - 1)
        def _():
            for c in range(nq):
                dq_ref[c * bq:(c + 1) * bq, :] = (dqt_acc[0:CHUNK, c * bq:(c + 1) * bq].T * (C_HDIM ** -0.5)).astype(BF16)
            dcq_ref[0] = dqt_acc[CHUNK:rows_t, :]

    per_pair = lambda r, c: pl.BlockSpec((1, r, c), lambda p, j: (p, 0, 0))
    keys4 = lambda r: pl.BlockSpec((1, kg, r, 2 * CHUNK), lambda p, j: (p, j, 0, 0))
    return pl.pallas_call(
        body,
        name="fox_bwd",
        grid=(C_PAIRS, ng),
        in_specs=[keys4(2 * CHUNK), keys4(2 * CHUNK), keys4(rows_t), per_pair(2 * CHUNK, seq), per_pair(2 * CHUNK, seq),
                  per_pair(seq, 2 * CHUNK), pl.BlockSpec((seq, LANES), lambda p, j: (0, p)), per_pair(2, seq)],
        out_specs=[pl.BlockSpec((seq, LANES), lambda p, j: (0, p)),
                   pl.BlockSpec((kg * CHUNK, LANES), lambda p, j: (j, p)),
                   pl.BlockSpec((kg * CHUNK, LANES), lambda p, j: (j, p)),
                   pl.BlockSpec((1, kg * CHUNK, LANES), lambda p, j: (p, j, 0)),
                   per_pair(C_TAIL, seq)],
        out_shape=[jax.ShapeDtypeStruct((seq, C_WIDTH), BF16)] * 3
        + [jax.ShapeDtypeStruct((C_PAIRS, seq, LANES), F32), jax.ShapeDtypeStruct((C_PAIRS, C_TAIL, seq), F32)],
        scratch_shapes=[pltpu.VMEM((rows_t, seq), F32), pltpu.VMEM((kg, 2 * CHUNK, LANES), F32),
                        pltpu.VMEM((kg, 2 * CHUNK, 2 * CHUNK), F32)],
        compiler_params=_cparams("parallel", "arbitrary"),
    )(ka, va, kt, qt, dot_t, qa, dob, lse)


def fox_post(dcq, dck, proj, bf_row):
    seq = proj.shape[0]
    nc = seq // CHUNK
    triu = jnp.asarray(np.triu(np.ones((CHUNK, CHUNK), np.float32)))

    def body(dq_ref, dk_ref, fl_ref, bf_ref, u_ref, dfl_ref, dbf_ref, carry_ref):
        @pl.when(pl.program_id(0) == 0)
        def _():
            carry_ref[...] = jnp.zeros_like(carry_ref)
            dbf_ref[...] = jnp.zeros_like(dbf_ref)

        rows = (dq_ref[0] + dq_ref[1]) + (dq_ref[2] + dq_ref[3])
        dc = jnp.concatenate([rows, jnp.zeros((CHUNK - C_TAIL, CHUNK), F32)], axis=0).T
        dc = dc - ((dk_ref[0] + dk_ref[1]) + (dk_ref[2] + dk_ref[3]))
        g = _dot(u_ref[...], dc, precision=HI) + carry_ref[...]
        carry_ref[...] += jnp.sum(dc, axis=0, keepdims=True)
        dfl = g * jax.nn.sigmoid(-(fl_ref[:, :LANES] + bf_ref[...]))
        dbf_ref[...] += jnp.sum(dfl, axis=0, keepdims=True)
        dfl_ref[...] = jnp.concatenate([dfl, jnp.zeros_like(dfl)], axis=1).astype(BF16)

    rev = lambda n: nc - 1 - n
    return pl.pallas_call(
        body,
        name="fox_post",
        grid=(nc,),
        in_specs=[
            pl.BlockSpec((C_PAIRS, C_TAIL, CHUNK), lambda n: (0, 0, rev(n))),
            pl.BlockSpec((C_PAIRS, CHUNK, LANES), lambda n: (0, rev(n), 0)),
            pl.BlockSpec((CHUNK, 256), lambda n: (rev(n), 3)),
            _full((1, LANES)), _full((CHUNK, CHUNK)),
        ],
        out_specs=[pl.BlockSpec((CHUNK, 256), lambda n: (rev(n), 0)), _full((1, LANES))],
        out_shape=[jax.ShapeDtypeStruct((seq, 256), BF16), jax.ShapeDtypeStruct((1, LANES), F32)],
        scratch_shapes=[pltpu.VMEM((1, LANES), F32)],
        compiler_params=_cparams("arbitrary"),
    )(dcq, dck, proj, bf_row, triu)


N_DEV = 8
MESH = pl.DeviceIdType.MESH
_ANY = pl.BlockSpec(memory_space=pl.ANY)


def _mesh_pos():
    return lax.axis_index("x"), lax.axis_index("y"), lax.axis_index("c")


def _dev_index(px, py, pc):
    return 4 * px + 2 * py + pc


def _row_pieces(ref, rows):
    return [ref.at[idx + (pl.ds(r, rows),)] for idx in np.ndindex(*ref.shape[:-2]) for r in range(0, ref.shape[-2], rows)]


class _Transfer:
    def __init__(self, src, dst, rows, send_sem, recv_sem, to):
        self.src, self.dst, self.rows, self.sems, self.to = src, dst, rows, (send_sem, recv_sem), to

    def _copy(self, src, dst):
        return pltpu.make_async_remote_copy(src_ref=src, dst_ref=dst, send_sem=self.sems[0], recv_sem=self.sems[1],
                                            device_id=self.to, device_id_type=MESH)

    def start(self):
        for s, d in zip(_row_pieces(self.src, self.rows), _row_pieces(self.dst, self.rows), strict=True):
            self._copy(s, d).start()

    def wait_send(self):
        self._copy(self.src, self.dst).wait_send()

    def wait_recv(self):
        self._copy(self.src, self.dst).wait_recv()


def allgather_weights(wi, wo):
    piece_rows = (128, 64)

    def body(wi_ref, wo_ref, wi_all, wo_all, send_sems, recv_sems, local_sems):
        x, y, c = _mesh_pos()
        me, sibling = (x, y, c), (x, y, 1 - c)
        chips = [(1 - x, y), (x, 1 - y), (1 - x, 1 - y)]
        arrays = ((wi_ref, wi_all), (wo_ref, wo_all))

        def copy(a, k, block, to, own=False):
            src, out = arrays[a]
            slot = out.at[_dev_index(*block)]
            return _Transfer(src if own else slot, slot, piece_rows[a], send_sems.at[a, k], recv_sems.at[a, k], to)

        both = range(len(arrays))
        mine = [pltpu.make_async_copy(arrays[a][0], arrays[a][1].at[_dev_index(*me)], local_sems.at[a]) for a in both]
        for cp in mine:
            cp.start()
        first = [copy(a, 1 + j, me, (*chip, c), own=True) for j, chip in enumerate(chips) for a in both]
        first += [copy(a, 0, me, sibling, own=True) for a in both]
        for cp in first:
            cp.start()
        passed = [copy(a, 4 + j, (*chip, c), sibling) for j, chip in enumerate(chips) for a in both]
        for j, chip in enumerate(chips):
            for a in both:
                copy(a, 1 + j, (*chip, c), me).wait_recv()
            for a in both:
                passed[2 * j + a].start()
        for a in both:
            copy(a, 0, sibling, me).wait_recv()
        for j, chip in enumerate(chips):
            for a in both:
                copy(a, 4 + j, (*chip, 1 - c), me).wait_recv()
        for cp in first + passed:
            cp.wait_send()
        for cp in mine:
            cp.wait()

    return pl.pallas_call(
        body,
        name="allgather_weights",
        in_specs=[_ANY, _ANY],
        out_specs=[_ANY, _ANY],
        out_shape=[jax.ShapeDtypeStruct((N_DEV,) + wi.shape, wi.dtype), jax.ShapeDtypeStruct((N_DEV,) + wo.shape, wo.dtype)],
        scratch_shapes=[pltpu.SemaphoreType.DMA((2, 7)), pltpu.SemaphoreType.DMA((2, 7)), pltpu.SemaphoreType.DMA((2,))],
    )(wi, wo)


N_CHIP = 4


def pair_exchange(gwi, gwo, gsm):
    piece_rows = (256, gwo.shape[-2], gsm.shape[0] // 2)

    def body(gwi_ref, gwo_ref, gsm_ref, pwi, pwo, psm, send_sems, recv_sems, local_sems):
        x, y, c = _mesh_pos()
        sibling = (x, y, 1 - c)
        srcs, outs = (gwi_ref, gwo_ref, gsm_ref), (pwi, pwo, psm)
        three = range(3)

        def part(a, core):
            return srcs[a] if a == 2 else srcs[a].at[core]

        local = [pltpu.make_async_copy(part(a, c), outs[a].at[c], local_sems.at[a]) for a in three]
        for cp in local:
            cp.start()

        def copy(a, slot):
            return _Transfer(part(a, 1 - c), outs[a].at[slot], piece_rows[a], send_sems.at[a], recv_sems.at[a], sibling)

        sends = [copy(a, c) for a in three]
        for cp in sends:
            cp.start()
        for a in three:
            copy(a, 1 - c).wait_recv()
        for cp in sends:
            cp.wait_send()
        for cp in local:
            cp.wait()

    return pl.pallas_call(
        body,
        name="pair_exchange",
        in_specs=[_ANY, _ANY, _ANY],
        out_specs=[_ANY, _ANY, _ANY],
        out_shape=[jax.ShapeDtypeStruct((2,) + gwi.shape[1:], gwi.dtype), jax.ShapeDtypeStruct((2,) + gwo.shape[1:], gwo.dtype),
                   jax.ShapeDtypeStruct((2,) + gsm.shape, gsm.dtype)],
        scratch_shapes=[pltpu.SemaphoreType.DMA((3,)), pltpu.SemaphoreType.DMA((3,)), pltpu.SemaphoreType.DMA((3,))],
    )(gwi, gwo, gsm)


def pair_sum(pair, dtype, rows, name):
    _, n, n_r, n_c = pair.shape

    def body(p_ref, o_ref):
        o_ref[0] = (p_ref[0, 0] + p_ref[1, 0]).astype(dtype)

    return pl.pallas_call(
        body,
        name=name,
        grid=(n, n_r // rows),
        in_specs=[pl.BlockSpec((2, 1, rows, n_c), lambda i, r: (0, i, r, 0))],
        out_specs=pl.BlockSpec((1, rows, n_c), lambda i, r: (i, r, 0)),
        out_shape=jax.ShapeDtypeStruct((n, n_r, n_c), dtype),
        compiler_params=_cparams("parallel", "parallel"),
    )(pair)


def chip_exchange(swi, swo, ssm):
    piece_rows = (128, swo.shape[-2] // 2, ssm.shape[0] // 2)

    def body(swi_ref, swo_ref, ssm_ref, rwi, rwo, rsm, send_sems, recv_sems, local_sems):
        x, y, c = _mesh_pos()
        chip = 2 * x + y
        srcs, outs = (swi_ref, swo_ref, ssm_ref), (rwi, rwo, rsm)
        three = range(3)

        def for_chip(a, j):
            return srcs[a] if a == 2 else srcs[a].at[j]

        local = [pltpu.make_async_copy(for_chip(a, chip), outs[a].at[chip], local_sems.at[a]) for a in three]
        for cp in local:
            cp.start()

        def peer_of(k):
            return x ^ ((k >> 1) & 1), y ^ (k & 1)

        def copy(a, k, slot):
            px, py = peer_of(k)
            return _Transfer(for_chip(a, 2 * px + py), outs[a].at[slot], piece_rows[a], send_sems.at[a, k - 1],
                             recv_sems.at[a, k - 1], (px, py, c))

        sends = [copy(a, k, chip) for k in range(1, N_CHIP) for a in three]
        for cp in sends:
            cp.start()
        for k in range(1, N_CHIP):
            px, py = peer_of(k)
            for a in three:
                copy(a, k, 2 * px + py).wait_recv()
        for cp in sends:
            cp.wait_send()
        for cp in local:
            cp.wait()

    return pl.pallas_call(
        body,
        name="chip_exchange",
        in_specs=[_ANY, _ANY, _ANY],
        out_specs=[_ANY, _ANY, _ANY],
        out_shape=[jax.ShapeDtypeStruct(swi.shape, swi.dtype), jax.ShapeDtypeStruct(swo.shape, swo.dtype),
                   jax.ShapeDtypeStruct((N_CHIP,) + ssm.shape, ssm.dtype)],
        scratch_shapes=[pltpu.SemaphoreType.DMA((3, 3)), pltpu.SemaphoreType.DMA((3, 3)), pltpu.SemaphoreType.DMA((3,))],
    )(swi, swo, ssm)


ADAM_LR = 0.001
ADAM_B1 = 0.9
ADAM_B2 = 0.999
ADAM_EPS = 1e-08
ADAM_WD = 0.01
ADAM_STEP = 10


def adam_reduce(parts, w, m, v, rows, name):
    n_l, n_r, n_c = w.shape

    def body(p_ref, w_ref, m_ref, v_ref, g_ref, d_ref, m2_ref, v2_ref):
        g = p_ref[0, 0].astype(F32)
        for d in range(1, N_CHIP):
            g = g + p_ref[d, 0].astype(F32)
        m2 = ADAM_B1 * m_ref[0] + (1.0 - ADAM_B1) * g
        v2 = ADAM_B2 * v_ref[0] + (1.0 - ADAM_B2) * (g * g)
        m_hat = m2 / (1.0 - ADAM_B1 ** ADAM_STEP)
        v_hat = v2 / (1.0 - ADAM_B2 ** ADAM_STEP)
        g_ref[0] = g
        d_ref[0] = -ADAM_LR * (m_hat / (jnp.sqrt(v_hat) + ADAM_EPS) + ADAM_WD * w_ref[0])
        m2_ref[0] = m2
        v2_ref[0] = v2

    blk = lambda: pl.BlockSpec((1, rows, n_c), lambda l, r: (l, r, 0))
    return pl.pallas_call(
        body,
        name=name,
        grid=(n_l, n_r // rows),
        in_specs=[pl.BlockSpec((N_CHIP, 1, rows, n_c), lambda l, r: (0, l, r, 0)), blk(), blk(), blk()],
        out_specs=[blk(), blk(), blk(), blk()],
        out_shape=[jax.ShapeDtypeStruct(w.shape, F32)] * 4,
        compiler_params=_cparams("parallel", "parallel"),
    )(parts, w, m, v)


_SMALL = (("norm_g", (2, 1024)), ("gmlp_ln_g", (2, 4, 64)), ("gmlp_ln_b", (2, 4, 64)), ("gmlp_w_s", (2, 4, 128, 128)),
          ("gmlp_b_s", (2, 4, 128)), ("hgrn_lb", (2, 256)), ("hgrn_onorm_g", (2, 64)), ("fox_b_f", (2, 8)),
          ("final_norm_g", (1024,)), ("loss", ()))


def _padded(n):
    return -(-n // LANES) * LANES


_SMALL_ROWS = -(-sum(_padded(int(np.prod(s))) for _, s in _SMALL) // LANES // 8) * 8


def _pack_small(vals):
    flat = []
    for (name, shape), a in zip(_SMALL, vals, strict=True):
        n = int(np.prod(shape))
        flat.append(jnp.pad(a.reshape(n).astype(F32), (0, _padded(n) - n)))
    flat = jnp.concatenate(flat)
    return jnp.pad(flat, (0, _SMALL_ROWS * LANES - flat.shape[0])).reshape(_SMALL_ROWS, LANES)


def _unpack_small(slab):
    flat, out, at = slab.reshape(-1), {}, 0
    for name, shape in _SMALL:
        n = int(np.prod(shape))
        out[name] = flat[at:at + n].reshape(shape)
        at += _padded(n)
    return out


def kernel(x, norm_g, w_in, w_out, gmlp_ln_g, gmlp_ln_b, gmlp_w_s, gmlp_b_s, hgrn_lb, hgrn_onorm_g, fox_b_f, final_norm_g, loss_target, m_norm_g, m_w_in, m_w_out, m_gmlp_ln_g, m_gmlp_ln_b, m_gmlp_w_s, m_gmlp_b_s, m_hgrn_lb, m_hgrn_onorm_g, m_fox_b_f, m_final_norm_g, v_norm_g, v_w_in, v_w_out, v_gmlp_ln_g, v_gmlp_ln_b, v_gmlp_w_s, v_gmlp_b_s, v_hgrn_lb, v_hgrn_onorm_g, v_fox_b_f, v_final_norm_g):
    depth = w_in.shape[0]
    seq = x.shape[1]
    assert w_in.shape[2] * N_DEV == N_IN
    xs, tgt = x[0], loss_target[0]

    wi_all, wo_all = allgather_weights(w_in.astype(BF16), w_out.astype(BF16))
    wi_int = assemble_w_in(wi_all)

    ln_g = gmlp_ln_g.reshape(depth, 1, A_WIDTH)
    ln_b = gmlp_ln_b.reshape(depth, 1, A_WIDTH)
    bs_t = jnp.pad(jnp.transpose(gmlp_b_s, (0, 2, 1)), ((0, 0), (0, 0), (0, LANES - A_GROUPS)))
    lb0, lb1 = hgrn_lb[0:1], hgrn_lb[1:2]
    onorm = jnp.tile(hgrn_onorm_g, (1, B_HEADS)).reshape(depth, 1, B_WIDTH)
    bf_row = jnp.pad(fox_b_f, ((0, 0), (0, LANES - C_HEADS))).reshape(depth, 1, LANES)

    saved = []
    xc = xs
    for l in range(depth):
        proj, h = inproj(xc, norm_g[l:l + 1], wi_int, l)
        ya = gmlp_fwd(proj, ln_g[l], ln_b[l], gmlp_w_s[l], bs_t[l])
        yb, states = hgrn_fwd(proj, lb0, lb1, onorm[l], l)
        ka, va, vt, kt, qt, qa = fox_prep(proj, bf_row[l])
        o, lse = fox_fwd(qt, ka, vt)
        xn, yfull = outproj(xc, ya, yb, o, proj, wo_all, l)
        saved.append((xc, proj, h, states, ka, va, kt, qt, qa, o, lse, yfull))
        xc = xn

    dx, d_final_g, loss_tile = final_loss(xc, final_norm_g[None], tgt)

    g_norm = [None] * depth
    g_ln_g, g_ln_b, g_ws, g_bs, g_on, g_bf = ([None] * depth for _ in range(6))
    g_lb0, g_lb1 = jnp.zeros_like(lb0), jnp.zeros_like(lb1)
    dwi = gwo = None
    for l in reversed(range(depth)):
        x_in, proj, h, states, ka, va, kt, qt, qa, o, lse, yfull = saved[l]
        dy, gwo = outproj_bwd(dx, yfull, wo_all, l, gwo)
        d_a, g_ln_g[l], g_ln_b[l], g_ws[l], dbs_t = gmlp_bwd(proj, dy, ln_g[l], ln_b[l], gmlp_w_s[l], bs_t[l])
        g_bs[l] = dbs_t[:, :A_GROUPS].T
        d_b, d0, d1, don = hgrn_bwd(proj, states, dy, lb0, lb1, onorm[l], l)
        g_lb0, g_lb1 = g_lb0 + d0, g_lb1 + d1
        g_on[l] = don.reshape(B_HEADS, B_KDIM).sum(0)
        dob, d_z, dot_t = fox_bwd_prep(dy, o, proj)
        d_q, d_k, d_v, dck, dcq = fox_bwd(ka, va, kt, qt, dot_t, qa, dob, lse)
        d_fl, dbf = fox_post(dcq, dck, proj, bf_row[l])
        g_bf[l] = dbf[0, :C_HEADS]
        dproj = jnp.concatenate([d_a, d_fl, d_b, d_q, d_k, d_v, d_z], axis=1)
        dx, g_norm[l] = inproj_bwd_x(dproj, wi_int, x_in, norm_g[l:l + 1], dx, l)
        dwi = inproj_bwd_w(h, dproj, l, depth, dwi)

    gwi = split_w_in_grad(dwi, w_in.shape[2])
    gsm = _pack_small([
        jnp.concatenate(g_norm), jnp.stack(g_ln_g), jnp.stack(g_ln_b), jnp.stack(g_ws), jnp.stack(g_bs),
        jnp.concatenate([g_lb0, g_lb1]), jnp.stack(g_on), jnp.stack(g_bf), d_final_g, loss_tile[0, 0]])
    pwi, pwo, psm = pair_exchange(gwi, gwo, gsm)
    swi = pair_sum(pwi.reshape(2, N_CHIP * depth, D_MODEL, -1), BF16, 256, "pair_sum_w_in").reshape(gwi.shape[1:])
    swo = pair_sum(pwo.reshape(2, N_CHIP * depth, -1, D_MODEL), BF16, gwo.shape[3], "pair_sum_w_out").reshape(gwo.shape[1:])
    ssm = pair_sum(psm[:, None], F32, _SMALL_ROWS, "pair_sum_small")[0]
    rwi, rwo, rsm = chip_exchange(swi, swo, ssm)

    small_w = (norm_g, gmlp_ln_g, gmlp_ln_b, gmlp_w_s, gmlp_b_s, hgrn_lb, hgrn_onorm_g, fox_b_f, final_norm_g)
    small_m = (m_norm_g, m_gmlp_ln_g, m_gmlp_ln_b, m_gmlp_w_s, m_gmlp_b_s, m_hgrn_lb, m_hgrn_onorm_g, m_fox_b_f, m_final_norm_g)
    small_v = (v_norm_g, v_gmlp_ln_g, v_gmlp_ln_b, v_gmlp_w_s, v_gmlp_b_s, v_hgrn_lb, v_hgrn_onorm_g, v_fox_b_f, v_final_norm_g)
    zero = jnp.zeros((), F32)
    res_wi = adam_reduce(rwi, w_in, m_w_in, v_w_in, 256, "adam_w_in")
    res_wo = adam_reduce(rwo, w_out, m_w_out, v_w_out, w_out.shape[1], "adam_w_out")
    res_sm = adam_reduce(rsm[:, None], _pack_small(small_w + (zero,))[None], _pack_small(small_m + (zero,))[None],
                         _pack_small(small_v + (zero,))[None], _SMALL_ROWS, "adam_small")
    res_sm = [_unpack_small(r[0]) for r in res_sm]

    def group(i):
        s = res_sm[i]
        return [s["norm_g"], res_wi[i], res_wo[i], s["gmlp_ln_g"], s["gmlp_ln_b"], s["gmlp_w_s"], s["gmlp_b_s"],
                s["hgrn_lb"], s["hgrn_onorm_g"], s["fox_b_f"], s["final_norm_g"]]

    return (res_sm[0]["loss"], dx[None], *group(0), *group(1), *group(2), *group(3))
```

```python
import functools

import jax
import jax.numpy as jnp
import numpy as np
from jax import lax
from jax.experimental import pallas as pl
from jax.experimental.pallas import tpu as pltpu

F32 = jnp.float32
BF16 = jnp.bfloat16
HI = lax.Precision.HIGHEST

NORM_EPS = 1e-6
F_FLOOR = 1e-30
CHUNK = 128
LANES = 128
VMEM_LIMIT = 56 * 1024 * 1024


def _cparams(*sem):
    return pltpu.CompilerParams(dimension_semantics=sem, vmem_limit_bytes=VMEM_LIMIT)


def _dot(a, b, dims=(((1,), (0,)), ((), ())), precision=None):
    return lax.dot_general(a, b, dims, precision=precision, preferred_element_type=F32)


_NT = (((1,), (1,)), ((), ()))
_TN = (((0,), (0,)), ((), ()))


def _bd(a, b):
    return _dot(a.astype(BF16), b.astype(BF16))


def _group_mean_matrix(width, group):
    idx = np.arange(width) // group
    return jnp.asarray((idx[:, None] == idx[None, :]).astype(np.float32) / group)


def _group_ones_matrix(width, group):
    idx = np.arange(width) // group
    return jnp.asarray((idx[:, None] == idx[None, :]).astype(np.float32))


A_WIDTH = 256
A_GROUPS = 4
A_GDIM = 64


def _gmlp_chunk(x3, ln_g, ln_b, w_s, bs_t, mean_m, gind):
    u = jax.nn.gelu(x3[:, :A_WIDTH])
    v = jax.nn.gelu(x3[:, A_WIDTH:2 * A_WIDTH])
    z = x3[:, 2 * A_WIDTH:]
    mu = _dot(v, mean_m, precision=HI)
    d = v - mu
    var = _dot(d * d, mean_m, precision=HI)
    vn = d * lax.rsqrt(var + NORM_EPS) * ln_g + ln_b
    vnb = vn.astype(BF16)
    row = lax.broadcasted_iota(jnp.int32, (CHUNK, CHUNK), 0)
    col = lax.broadcasted_iota(jnp.int32, (CHUNK, CHUNK), 1)
    causal = row >= col
    lane_g = lax.shift_right_logical(lax.broadcasted_iota(jnp.int32, (CHUNK, A_WIDTH), 1), 6)
    mixed = _dot(bs_t, gind, precision=HI)
    for g in range(A_GROUPS):
        wc = jnp.where(causal, w_s[g], 0.0).astype(BF16)
        mixed = mixed + jnp.where(lane_g == g, _dot(wc, vnb), 0.0)
    return u * mixed * jax.nn.silu(z)


def _gmlp_consts():
    gind = np.zeros((LANES, A_WIDTH), np.float32)
    for g in range(A_GROUPS):
        gind[g, g * A_GDIM:(g + 1) * A_GDIM] = 1.0
    return _group_mean_matrix(A_WIDTH, A_GDIM), jnp.asarray(gind)


def _full(shape):
    return pl.BlockSpec(shape, lambda *_: (0,) * len(shape))


def gmlp_fwd(proj, ln_g, ln_b, w_s, bs_t):
    seq = proj.shape[0]
    mean_m, gind = _gmlp_consts()

    def body(x_ref, g_ref, b_ref, w_ref, bs_ref, m_ref, gi_ref, y_ref):
        y = _gmlp_chunk(x_ref[...], g_ref[...], b_ref[...], w_ref[...], bs_ref[...], m_ref[...], gi_ref[...])
        y_ref[...] = y.astype(BF16)

    return pl.pallas_call(
        body,
        name="gmlp_fwd",
        grid=(seq // CHUNK,),
        in_specs=[
            pl.BlockSpec((CHUNK, 3 * A_WIDTH), lambda n: (n, 0)),
            _full((1, A_WIDTH)), _full((1, A_WIDTH)), _full((A_GROUPS, CHUNK, CHUNK)), _full((CHUNK, LANES)),
            _full((A_WIDTH, A_WIDTH)), _full((LANES, A_WIDTH)),
        ],
        out_specs=pl.BlockSpec((CHUNK, A_WIDTH), lambda n: (n, 0)),
        out_shape=jax.ShapeDtypeStruct((seq, A_WIDTH), BF16),
        compiler_params=_cparams("parallel"),
    )(proj, ln_g, ln_b, w_s, bs_t, mean_m, gind)


def gmlp_bwd(proj, dy, ln_g, ln_b, w_s, bs_t):
    seq = proj.shape[0]
    mean_m, gind = _gmlp_consts()

    def body(x_ref, dy_ref, g_ref, b_ref, w_ref, bs_ref, m_ref, gi_ref, dx_ref, dg_ref, db_ref, dw_ref, dbs_ref):
        fn = functools.partial(_gmlp_chunk, mean_m=m_ref[...], gind=gi_ref[...])
        _, vjp = jax.vjp(fn, x_ref[...], g_ref[...], b_ref[...], w_ref[...], bs_ref[...])
        dx, dg, db, dw, dbs = vjp(dy_ref[...])
        dx_ref[...] = dx.astype(BF16)

        @pl.when(pl.program_id(0) == 0)
        def _():
            dg_ref[...] = jnp.zeros_like(dg_ref)
            db_ref[...] = jnp.zeros_like(db_ref)
            dw_ref[...] = jnp.zeros_like(dw_ref)
            dbs_ref[...] = jnp.zeros_like(dbs_ref)

        dg_ref[...] += dg
        db_ref[...] += db
        dw_ref[...] += dw
        dbs_ref[...] += dbs

    return pl.pallas_call(
        body,
        name="gmlp_bwd",
        grid=(seq // CHUNK,),
        in_specs=[
            pl.BlockSpec((CHUNK, 3 * A_WIDTH), lambda n: (n, 0)),
            pl.BlockSpec((CHUNK, A_WIDTH), lambda n: (n, 0)),
            _full((1, A_WIDTH)), _full((1, A_WIDTH)), _full((A_GROUPS, CHUNK, CHUNK)), _full((CHUNK, LANES)),
            _full((A_WIDTH, A_WIDTH)), _full((LANES, A_WIDTH)),
        ],
        out_specs=[
            pl.BlockSpec((CHUNK, 3 * A_WIDTH), lambda n: (n, 0)),
            _full((1, A_WIDTH)), _full((1, A_WIDTH)), _full((A_GROUPS, CHUNK, CHUNK)), _full((CHUNK, LANES)),
        ],
        out_shape=[
            jax.ShapeDtypeStruct((seq, 3 * A_WIDTH), BF16),
            jax.ShapeDtypeStruct((1, A_WIDTH), F32), jax.ShapeDtypeStruct((1, A_WIDTH), F32),
            jax.ShapeDtypeStruct((A_GROUPS, CHUNK, CHUNK), F32), jax.ShapeDtypeStruct((CHUNK, LANES), F32),
        ],
        compiler_params=_cparams("arbitrary"),
    )(proj, dy, ln_g, ln_b, w_s, bs_t, mean_m, gind)


B_WIDTH = 256
B_HEADS = 4
B_KDIM = 64
B_LEVELS = (64, 32, 16, 8, 4, 2, 1)


def _hgrn_consts():
    t = np.arange(CHUNK)
    u = t[None, :]
    mats = [np.tril(np.ones((CHUNK, CHUNK), np.float32))]
    for m in B_LEVELS:
        p = (t // (2 * m)) * (2 * m) + m - 1
        right = (t % (2 * m)) >= m
        sel = np.where(right[:, None], (u > p[:, None]) & (u <= t[:, None]), (u > t[:, None]) & (u <= p[:, None]))
        mats.append(sel.astype(np.float32))
    return jnp.asarray(np.concatenate(mats, 0)), _group_ones_matrix(B_WIDTH, B_KDIM)


def _hgrn_lower_bound(lb0, lb1, layer):
    mx = jnp.maximum(lb0, lb1)
    e0 = jnp.exp(lb0 - mx)
    e1 = jnp.exp(lb1 - mx)
    p0 = e0 / (e0 + e1)
    p1 = e1 / (e0 + e1)
    cs = p0 if layer == 0 else p0 + p1
    return jnp.clip(cs - p0, 0.0, 1.0 - 1e-6)


def _hgrn_chunk(x4, st, lb0, lb1, onorm, layer, tstack, ones_bd):
    q_raw, fl, v, zg = (x4[:, i * B_WIDTH:(i + 1) * B_WIDTH] for i in range(4))
    lb = _hgrn_lower_bound(lb0, lb1, layer)
    q = jax.nn.silu(q_raw) * (B_KDIM ** -0.5)
    f = lb + (1.0 - lb) * jax.nn.sigmoid(fl)
    logf = jnp.log(jnp.maximum(f, F_FLOOR))
    k = (1.0 - lb) * jax.nn.sigmoid(-fl)
    dall = _dot(tstack, logf, precision=HI)
    b = dall[:CHUNK]
    b_last = jnp.sum(logf, axis=0, keepdims=True)
    vb = v.astype(BF16)

    lane_h = lax.shift_right_logical(lax.broadcasted_iota(jnp.int32, (CHUNK, B_WIDTH), 1), 6)
    row = lax.broadcasted_iota(jnp.int32, (CHUNK, B_WIDTH), 0)
    srow = lax.broadcasted_iota(jnp.int32, (B_HEADS * CHUNK, CHUNK), 0) & (CHUNK - 1)
    scol = lax.broadcasted_iota(jnp.int32, (B_HEADS * CHUNK, CHUNK), 1)

    def heads_on_rows(a):
        return jnp.concatenate([jnp.where(lane_h == h, a, 0.0) for h in range(B_HEADS)], axis=0)

    def heads_from_rows(r):
        out = jnp.where(lane_h == 0, r[:CHUNK], 0.0)
        for h in range(1, B_HEADS):
            out = out + jnp.where(lane_h == h, r[h * CHUNK:(h + 1) * CHUNK], 0.0)
        return out

    o = lax.dot_general((q * jnp.exp(b)).astype(BF16), st.astype(BF16), _NT, preferred_element_type=F32)
    scores = jnp.zeros((B_HEADS * CHUNK, CHUNK), F32)
    for li, m in enumerate(B_LEVELS):
        e = jnp.exp(dall[(li + 1) * CHUNK:(li + 2) * CHUNK])
        right = (row & (2 * m - 1)) >= m
        qt = jnp.where(right, q * e, 0.0)
        kt = jnp.where(right, 0.0, k * e)
        sc = lax.dot_general(heads_on_rows(qt).astype(BF16), kt.astype(BF16), _NT, preferred_element_type=F32)
        sh = int(np.log2(2 * m))
        same = lax.shift_right_logical(srow, sh) == lax.shift_right_logical(scol, sh)
        scores = scores + jnp.where(same, sc, 0.0)
    o = o + heads_from_rows(_dot(scores.astype(BF16), vb))
    o = o + _dot(q * k, ones_bd, precision=HI) * v

    kv = lax.dot_general(vb, (k * jnp.exp(b_last - b)).astype(BF16), _TN, preferred_element_type=F32)
    st_new = st * jnp.exp(b_last) + jnp.where(ones_bd > 0.5, kv, 0.0)

    ms = _dot(o * o, ones_bd, precision=HI) * (1.0 / B_KDIM)
    y = o * lax.rsqrt(ms + NORM_EPS) * onorm * jax.nn.silu(zg)
    return y, st_new


def hgrn_fwd(proj, lb0, lb1, onorm, layer):
    seq = proj.shape[0]
    nc = seq // CHUNK
    tstack, ones_bd = _hgrn_consts()

    def body(x_ref, lb0_ref, lb1_ref, on_ref, t_ref, e_ref, y_ref, st_out_ref, st_ref):
        @pl.when(pl.program_id(0) == 0)
        def _():
            st_ref[...] = jnp.zeros_like(st_ref)

        st = st_ref[...]
        st_out_ref[0] = st
        y, st_new = _hgrn_chunk(x_ref[...], st, lb0_ref[...], lb1_ref[...], on_ref[...], layer, t_ref[...], e_ref[...])
        y_ref[...] = y.astype(BF16)
        st_ref[...] = st_new

    return pl.pallas_call(
        body,
        name=f"hgrn_fwd_{layer}",
        grid=(nc,),
        in_specs=[
            pl.BlockSpec((CHUNK, 4 * B_WIDTH), lambda n: (n, 1)),
            _full((1, B_WIDTH)), _full((1, B_WIDTH)), _full((1, B_WIDTH)),
            _full(((len(B_LEVELS) + 1) * CHUNK, CHUNK)), _full((B_WIDTH, B_WIDTH)),
        ],
        out_specs=[
            pl.BlockSpec((CHUNK, B_WIDTH), lambda n: (n, 0)),
            pl.BlockSpec((1, B_WIDTH, B_WIDTH), lambda n: (n, 0, 0)),
        ],
        out_shape=[jax.ShapeDtypeStruct((seq, B_WIDTH), BF16), jax.ShapeDtypeStruct((nc, B_WIDTH, B_WIDTH), F32)],
        scratch_shapes=[pltpu.VMEM((B_WIDTH, B_WIDTH), F32)],
        compiler_params=_cparams("arbitrary"),
    )(proj, lb0, lb1, onorm, tstack, ones_bd)


def hgrn_bwd(proj, states, dy, lb0, lb1, onorm, layer):
    seq = proj.shape[0]
    nc = seq // CHUNK
    tstack, ones_bd = _hgrn_consts()

    def body(x_ref, st_in_ref, dy_ref, lb0_ref, lb1_ref, on_ref, t_ref, e_ref, dx_ref, d0_ref, d1_ref, don_ref, dst_ref):
        @pl.when(pl.program_id(0) == 0)
        def _():
            dst_ref[...] = jnp.zeros_like(dst_ref)
            d0_ref[...] = jnp.zeros_like(d0_ref)
            d1_ref[...] = jnp.zeros_like(d1_ref)
            don_ref[...] = jnp.zeros_like(don_ref)

        fn = functools.partial(_hgrn_chunk, layer=layer, tstack=t_ref[...], ones_bd=e_ref[...])
        _, vjp = jax.vjp(fn, x_ref[...], st_in_ref[0], lb0_ref[...], lb1_ref[...], on_ref[...])
        dx, dst, d0, d1, don = vjp((dy_ref[...], dst_ref[...]))
        dx_ref[...] = dx.astype(BF16)
        dst_ref[...] = dst
        d0_ref[...] += d0
        d1_ref[...] += d1
        don_ref[...] += don

    rev = lambda n: nc - 1 - n
    return pl.pallas_call(
        body,
        name=f"hgrn_bwd_{layer}",
        grid=(nc,),
        in_specs=[
            pl.BlockSpec((CHUNK, 4 * B_WIDTH), lambda n: (rev(n), 1)),
            pl.BlockSpec((1, B_WIDTH, B_WIDTH), lambda n: (rev(n), 0, 0)),
            pl.BlockSpec((CHUNK, B_WIDTH), lambda n: (rev(n), 1)),
            _full((1, B_WIDTH)), _full((1, B_WIDTH)), _full((1, B_WIDTH)),
            _full(((len(B_LEVELS) + 1) * CHUNK, CHUNK)), _full((B_WIDTH, B_WIDTH)),
        ],
        out_specs=[
            pl.BlockSpec((CHUNK, 4 * B_WIDTH), lambda n: (rev(n), 0)),
            _full((1, B_WIDTH)), _full((1, B_WIDTH)), _full((1, B_WIDTH)),
        ],
        out_shape=[jax.ShapeDtypeStruct((seq, 4 * B_WIDTH), BF16)] + [jax.ShapeDtypeStruct((1, B_WIDTH), F32)] * 3,
        scratch_shapes=[pltpu.VMEM((B_WIDTH, B_WIDTH), F32)],
        compiler_params=_cparams("arbitrary"),
    )(proj, states, dy, lb0, lb1, onorm, tstack, ones_bd)


D_MODEL = 1024
D_INT = 4096


def _rms_stats(xf):
    r = lax.rsqrt(jnp.mean(xf * xf, axis=-1, keepdims=True) + NORM_EPS)
    return r, xf * r


def _rms_bwd(dy, g, r, xh):
    u = dy * g
    return r * (u - xh * jnp.mean(u * xh, axis=-1, keepdims=True))


def inproj(x, g, w, layer):
    seq = x.shape[0]
    tm, tn = min(seq, 1024), 512

    def body(x_ref, g_ref, w_ref, p_ref, h_ref):
        @pl.when(pl.program_id(1) == 0)
        def _():
            _, xh = _rms_stats(x_ref[...])
            h_ref[...] = (xh * g_ref[...]).astype(BF16)

        p_ref[...] = _dot(h_ref[...], w_ref[0])

    return pl.pallas_call(
        body,
        name="inproj",
        grid=(seq // tm, D_INT // tn),
        in_specs=[
            pl.BlockSpec((tm, D_MODEL), lambda i, j: (i, 0)),
            _full((1, D_MODEL)),
            pl.BlockSpec((1, D_MODEL, tn), lambda i, j: (layer, 0, j)),
        ],
        out_specs=[pl.BlockSpec((tm, tn), lambda i, j: (i, j)), pl.BlockSpec((tm, D_MODEL), lambda i, j: (i, 0))],
        out_shape=[jax.ShapeDtypeStruct((seq, D_INT), F32), jax.ShapeDtypeStruct((seq, D_MODEL), BF16)],
        compiler_params=_cparams("parallel", "arbitrary"),
    )(x, g, w)


def outproj(x, ya, yb, o, proj, wo, layer):
    seq = x.shape[0]
    tm = min(seq, 512)
    blk = wo.shape[2]

    def body(x_ref, ya_ref, yb_ref, o_ref, z_ref, w_ref, xn_ref, y_ref):
        yc = (o_ref[...] * jax.nn.silu(z_ref[...])).astype(BF16)
        y = jnp.concatenate([ya_ref[...], yb_ref[...], yc], axis=1)
        y_ref[...] = y
        w = jnp.concatenate([w_ref[d, 0] for d in range(N_DEV)], axis=0)
        xn_ref[...] = x_ref[...] + _dot(y, w)

    return pl.pallas_call(
        body,
        name="outproj",
        grid=(seq // tm,),
        in_specs=[
            pl.BlockSpec((tm, D_MODEL), lambda i: (i, 0)),
            pl.BlockSpec((tm, 256), lambda i: (i, 0)),
            pl.BlockSpec((tm, 256), lambda i: (i, 0)),
            pl.BlockSpec((tm, 512), lambda i: (i, 0)),
            pl.BlockSpec((tm, 512), lambda i: (i, 7)),
            pl.BlockSpec((N_DEV, 1, blk, D_MODEL), lambda i: (0, layer, 0, 0)),
        ],
        out_specs=[pl.BlockSpec((tm, D_MODEL), lambda i: (i, 0)), pl.BlockSpec((tm, D_MODEL), lambda i: (i, 0))],
        out_shape=[jax.ShapeDtypeStruct((seq, D_MODEL), F32), jax.ShapeDtypeStruct((seq, D_MODEL), BF16)],
        compiler_params=_cparams("parallel"),
    )(x, ya, yb, o, proj, wo)


def outproj_bwd(dx, y, wo, layer, stacked=None):
    seq = dx.shape[0]
    ts = min(seq, 512)
    _, depth, blk, _ = wo.shape

    def body(dx_ref, y_ref, w_ref, *refs):
        dy_ref, dw_ref = refs[-2:]

        @pl.when(pl.program_id(0) == 0)
        def _():
            dw_ref[...] = jnp.zeros_like(dw_ref)

        dxb = dx_ref[...].astype(BF16)
        w = jnp.concatenate([w_ref[d, 0] for d in range(N_DEV)], axis=0)
        dy_ref[...] = lax.dot_general(dxb, w, _NT, preferred_element_type=F32)
        dw = lax.dot_general(y_ref[...], dxb, _TN, preferred_element_type=F32)
        for d in range(N_DEV):
            dw_ref[d % 2, d // 2, 0] += dw[d * blk:(d + 1) * blk]

    carried = () if stacked is None else (stacked,)
    out_shape = [jax.ShapeDtypeStruct((seq, D_MODEL), F32), jax.ShapeDtypeStruct((2, N_CHIP, depth, blk, D_MODEL), F32)]
    return pl.pallas_call(
        body,
        name="outproj_bwd",
        grid=(seq // ts,),
        in_specs=[
            pl.BlockSpec((ts, D_MODEL), lambda i: (i, 0)),
            pl.BlockSpec((ts, D_MODEL), lambda i: (i, 0)),
            pl.BlockSpec((N_DEV, 1, blk, D_MODEL), lambda i: (0, layer, 0, 0)),
        ] + [_ANY] * len(carried),
        out_specs=[pl.BlockSpec((ts, D_MODEL), lambda i: (i, 0)),
                   pl.BlockSpec((2, N_CHIP, 1, blk, D_MODEL), lambda i: (0, 0, layer, 0, 0))],
        out_shape=out_shape,
        input_output_aliases={3: 1} if carried else {},
        compiler_params=_cparams("arbitrary"),
    )(dx, y, wo, *carried)


def inproj_bwd_x(dproj, w, x, g, dx_in, layer):
    seq = x.shape[0]
    tm, tk = min(seq, 512), 1024
    nk = D_INT // tk

    def body(dp_ref, w_ref, x_ref, g_ref, dxin_ref, dx_ref, dg_ref, acc_ref):
        k = pl.program_id(1)

        @pl.when(k == 0)
        def _():
            acc_ref[...] = jnp.zeros_like(acc_ref)

        acc_ref[...] += lax.dot_general(dp_ref[...], w_ref[0], _NT, preferred_element_type=F32)

        @pl.when(k == nk - 1)
        def _():
            @pl.when(pl.program_id(0) == 0)
            def _():
                dg_ref[...] = jnp.zeros_like(dg_ref)

            dh = acc_ref[...]
            g = g_ref[...]
            r, xh = _rms_stats(x_ref[...])
            dg_ref[...] += jnp.sum(dh * xh, axis=0, keepdims=True)
            dx_ref[...] = dxin_ref[...] + _rms_bwd(dh, g, r, xh)

    return pl.pallas_call(
        body,
        name="inproj_bwd_x",
        grid=(seq // tm, nk),
        in_specs=[
            pl.BlockSpec((tm, tk), lambda i, k: (i, k)),
            pl.BlockSpec((1, D_MODEL, tk), lambda i, k: (layer, 0, k)),
            pl.BlockSpec((tm, D_MODEL), lambda i, k: (i, 0)),
            _full((1, D_MODEL)),
            pl.BlockSpec((tm, D_MODEL), lambda i, k: (i, 0)),
        ],
        out_specs=[pl.BlockSpec((tm, D_MODEL), lambda i, k: (i, 0)), _full((1, D_MODEL))],
        out_shape=[jax.ShapeDtypeStruct((seq, D_MODEL), F32), jax.ShapeDtypeStruct((1, D_MODEL), F32)],
        scratch_shapes=[pltpu.VMEM((tm, D_MODEL), F32)],
        compiler_params=_cparams("arbitrary", "arbitrary"),
    )(dproj, w, x, g, dx_in)


def inproj_bwd_w(h, dproj, layer, depth, stacked=None):
    seq = h.shape[0]
    ts, tn = min(seq, 1024), 512

    def body(h_ref, dp_ref, *refs):
        dw_ref = refs[-1]

        @pl.when(pl.program_id(1) == 0)
        def _():
            dw_ref[...] = jnp.zeros_like(dw_ref)

        dw_ref[0] += lax.dot_general(h_ref[...], dp_ref[...], _TN, preferred_element_type=F32)

    carried = () if stacked is None else (stacked,)
    return pl.pallas_call(
        body,
        name="inproj_bwd_w",
        grid=(D_INT // tn, seq // ts),
        in_specs=[pl.BlockSpec((ts, D_MODEL), lambda j, s: (s, 0)), pl.BlockSpec((ts, tn), lambda j, s: (s, j))]
        + [_ANY] * len(carried),
        out_specs=pl.BlockSpec((1, D_MODEL, tn), lambda j, s: (layer, 0, j)),
        out_shape=jax.ShapeDtypeStruct((depth, D_MODEL, D_INT), F32),
        input_output_aliases={2: 0} if carried else {},
        compiler_params=_cparams("parallel", "arbitrary"),
    )(h, dproj, *carried)


N_IN = 3848


def _internal_of(col):
    return col if col < 768 else (col + 256 if col < 3840 else 768 + col - 3840)


def _column_runs(n_shard):
    runs = []
    for d in range(N_IN // n_shard):
        mine = []
        for j in range(n_shard):
            ci = _internal_of(d * n_shard + j)
            if mine and mine[-1][0] + mine[-1][1] == ci:
                mine[-1][1] += 1
            else:
                mine.append([ci, 1, j])
        runs.append(mine)
    return runs


def assemble_w_in(wi_all):
    n_dev, depth, _, n_shard = wi_all.shape
    tr = 256
    pieces = [[] for _ in range(D_INT // LANES)]
    for d, mine in enumerate(_column_runs(n_shard)):
        for ci, ln, off in mine:
            while ln > 0:
                blk, at = divmod(ci, LANES)
                take = min(ln, LANES - at)
                pieces[blk].append((at, take, d, off))
                ci, ln, off = ci + take, ln - take, off + take

    def body(x_ref, o_ref):
        for blk, parts in enumerate(pieces):
            vals, at = [], 0
            for start, ln, d, off in sorted(parts):
                if start > at:
                    vals.append(jnp.zeros((tr, start - at), BF16))
                vals.append(x_ref[d, 0, :, off:off + ln])
                at = start + ln
            if at < LANES:
                vals.append(jnp.zeros((tr, LANES - at), BF16))
            o_ref[0, :, blk * LANES:(blk + 1) * LANES] = vals[0] if len(vals) == 1 else jnp.concatenate(vals, axis=1)

    return pl.pallas_call(
        body,
        name="assemble_w_in",
        grid=(depth, D_MODEL // tr),
        in_specs=[pl.BlockSpec((n_dev, 1, tr, n_shard), lambda l, r: (0, l, r, 0))],
        out_specs=pl.BlockSpec((1, tr, D_INT), lambda l, r: (l, r, 0)),
        out_shape=jax.ShapeDtypeStruct((depth, D_MODEL, D_INT), BF16),
        compiler_params=_cparams("parallel", "parallel"),
    )(wi_all)


def split_w_in_grad(dwi, n_shard):
    depth = dwi.shape[0]
    tr = 256
    runs = _column_runs(n_shard)

    def body(x_ref, o_ref):
        for d, mine in enumerate(runs):
            for ci, ln, off in mine:
                o_ref[d % 2, d // 2, 0, :, off:off + ln] = x_ref[0, :, ci:ci + ln]

    return pl.pallas_call(
        body,
        name="split_w_in_grad",
        grid=(depth, D_MODEL // tr),
        in_specs=[pl.BlockSpec((1, tr, D_INT), lambda l, r: (l, r, 0))],
        out_specs=pl.BlockSpec((2, N_CHIP, 1, tr, n_shard), lambda l, r: (0, 0, l, r, 0)),
        out_shape=jax.ShapeDtypeStruct((2, N_CHIP, depth, D_MODEL, n_shard), F32),
        compiler_params=_cparams("parallel", "parallel"),
    )(dwi)


def final_loss(x, g, tgt):
    seq = x.shape[0]
    tm = min(seq, 512)

    def body(x_ref, g_ref, t_ref, dx_ref, dg_ref, loss_ref):
        @pl.when(pl.program_id(0) == 0)
        def _():
            dg_ref[...] = jnp.zeros_like(dg_ref)
            loss_ref[...] = jnp.zeros_like(loss_ref)

        g = g_ref[...]
        r, xh = _rms_stats(x_ref[...])
        err = xh * g - t_ref[...]
        sq = jnp.sum(jnp.sum(err * err, axis=1, keepdims=True), axis=0, keepdims=True)
        loss_ref[...] += jnp.broadcast_to(sq * (0.5 / D_MODEL), loss_ref.shape)
        dout = err * (1.0 / D_MODEL)
        dg_ref[...] += jnp.sum(dout * xh, axis=0, keepdims=True)
        dx_ref[...] = _rms_bwd(dout, g, r, xh)

    return pl.pallas_call(
        body,
        name="final_loss",
        grid=(seq // tm,),
        in_specs=[pl.BlockSpec((tm, D_MODEL), lambda i: (i, 0)), _full((1, D_MODEL)), pl.BlockSpec((tm, D_MODEL), lambda i: (i, 0))],
        out_specs=[pl.BlockSpec((tm, D_MODEL), lambda i: (i, 0)), _full((1, D_MODEL)), _full((8, LANES))],
        out_shape=[jax.ShapeDtypeStruct((seq, D_MODEL), F32), jax.ShapeDtypeStruct((1, D_MODEL), F32), jax.ShapeDtypeStruct((8, LANES), F32)],
        compiler_params=_cparams("arbitrary"),
    )(x, g, tgt)


C_WIDTH = 512
C_HEADS = 8
C_HDIM = 64
C_PAIRS = C_HEADS // 2
C_BQ = 512
C_TAIL = 16
C_KG = 2


def _split3(x):
    hi = x.astype(BF16)
    r = x - hi.astype(F32)
    mid = r.astype(BF16)
    return hi, mid, (r - mid.astype(F32)).astype(BF16)


def _piece_selectors():
    sel = np.zeros((C_HEADS, 3 * LANES, LANES), np.float32)
    for p in range(C_PAIRS):
        for e in range(2):
            for t in range(3):
                sel[2 * p + e, t * LANES + 2 * p + e, 3 * e + t] = -1.0
    return sel


def fox_prep(proj, bf_row):
    seq = proj.shape[0]
    nblk = seq // CHUNK
    tril = jnp.asarray(np.tril(np.ones((CHUNK, CHUNK), np.float32)))
    sel = jnp.asarray(_piece_selectors(), BF16)
    rows_t = CHUNK + C_TAIL

    def body(fl_ref, q_ref, k_ref, v_ref, bf_ref, l_ref, sel_ref, ka_ref, va_ref, vt_ref, kt_ref, qt_ref, qa_ref, carry_ref):
        @pl.when(pl.program_id(0) == 0)
        def _():
            carry_ref[...] = jnp.zeros_like(carry_ref)

        lf = jax.nn.log_sigmoid(fl_ref[:, :LANES] + bf_ref[...])
        c = _dot(l_ref[...], lf, precision=HI) + carry_ref[...]
        carry_ref[...] += jnp.sum(lf, axis=0, keepdims=True)
        c3 = jnp.concatenate(_split3(c), axis=1)
        lane = lax.broadcasted_iota(jnp.int32, (CHUNK, LANES), 1)
        row = lax.broadcasted_iota(jnp.int32, (CHUNK, LANES), 0)
        r16 = lax.broadcasted_iota(jnp.int32, (C_TAIL, 2 * CHUNK), 0)
        l16 = lax.broadcasted_iota(jnp.int32, (C_TAIL, 2 * CHUNK), 1)
        zero = jnp.zeros((CHUNK, LANES), BF16)
        one = jnp.ones((CHUNK, LANES), BF16)

        def by_keys(x, right_a, right_b):
            xb = x.astype(BF16)
            top = jnp.concatenate([jnp.where(lane < C_HDIM, xb, zero), right_a], axis=1)
            return jnp.concatenate([top, jnp.concatenate([jnp.where(lane < C_HDIM, zero, xb), right_b], axis=1)], axis=0)

        def by_lanes(x, tail):
            xt = x.T.astype(BF16)
            main = jnp.concatenate([jnp.where(row < C_HDIM, xt, zero), jnp.where(row < C_HDIM, zero, xt)], axis=1)
            return jnp.concatenate([main, tail], axis=0)

        for p in range(C_PAIRS):
            cols = slice(p * LANES, (p + 1) * LANES)
            q2, k2, v2 = q_ref[:, cols] * (C_HDIM ** -0.5), k_ref[:, cols], v_ref[:, cols]
            negc = [_dot(c3, sel_ref[2 * p + e]).astype(BF16) for e in range(2)]
            ones3 = [jnp.where((lane >= 3 * e) & (lane < 3 * e + 3), one, zero) for e in range(2)]
            tail = jnp.where(((r16 == 2 * p) & (l16 < CHUNK)) | ((r16 == 2 * p + 1) & (l16 >= CHUNK)), 1.0, 0.0).astype(BF16)
            ka_ref[p, 0] = by_keys(k2, negc[0], negc[1])
            va_ref[p, 0] = by_keys(v2, ones3[0], ones3[1])
            kt_ref[p, 0] = by_lanes(k2, tail)
            vt_ref[p, 0] = by_lanes(v2, tail)
            qt_ref[p] = jnp.concatenate([q2.T.astype(BF16), jnp.where(row < 6, one, zero)], axis=0)
            qa_ref[p] = jnp.concatenate([q2.astype(BF16), jnp.where((lane == 2 * p) | (lane == 2 * p + 1), one, zero)], axis=1)

    wide = lambda j: pl.BlockSpec((CHUNK, C_WIDTH), lambda n: (n, j))
    sq = lambda r: pl.BlockSpec((C_PAIRS, 1, r, 2 * CHUNK), lambda n: (0, n, 0, 0))
    return pl.pallas_call(
        body,
        name="fox_prep",
        grid=(nblk,),
        in_specs=[pl.BlockSpec((CHUNK, 256), lambda n: (n, 3)), wide(4), wide(5), wide(6), _full((1, LANES)),
                  _full((CHUNK, CHUNK)), _full((C_HEADS, 3 * LANES, LANES))],
        out_specs=[sq(2 * CHUNK), sq(2 * CHUNK), sq(rows_t), sq(rows_t),
                   pl.BlockSpec((C_PAIRS, 2 * CHUNK, CHUNK), lambda n: (0, 0, n)),
                   pl.BlockSpec((C_PAIRS, CHUNK, 2 * CHUNK), lambda n: (0, n, 0))],
        out_shape=[jax.ShapeDtypeStruct((C_PAIRS, nblk, 2 * CHUNK, 2 * CHUNK), BF16)] * 2
        + [jax.ShapeDtypeStruct((C_PAIRS, nblk, rows_t, 2 * CHUNK), BF16)] * 2
        + [jax.ShapeDtypeStruct((C_PAIRS, 2 * CHUNK, seq), BF16), jax.ShapeDtypeStruct((C_PAIRS, seq, 2 * CHUNK), BF16)],
        scratch_shapes=[pltpu.VMEM((1, LANES), F32)],
        compiler_params=_cparams("arbitrary"),
    )(proj, proj, proj, proj, bf_row, tril, sel)


def _visible(shape, key0, query0):
    key = key0 + (lax.broadcasted_iota(jnp.int32, shape, 0) & (CHUNK - 1))
    return key <= query0 + lax.broadcasted_iota(jnp.int32, shape, 1)


def _rows_ab(a, b, n):
    return jnp.concatenate([jnp.broadcast_to(a, (C_HDIM, n)), jnp.broadcast_to(b, (C_HDIM, n))], axis=0)


def fox_fwd(qt, ka, vt):
    seq = qt.shape[2]
    nblk = seq // CHUNK
    bq = min(C_BQ, seq)
    grp = bq // CHUNK
    rows_t = CHUNK + C_TAIL

    def body(qt_ref, ka_ref, vt_ref, o_ref, lse_ref, acc_ref):
        p, i = pl.program_id(0), pl.program_id(1)
        qtile = qt_ref[0]
        r16 = lax.broadcasted_iota(jnp.int32, (C_TAIL, bq), 0)

        def group(j0, m, masked):
            ma, mb = m
            ss = []
            for g in range(grp):
                s = _dot(ka_ref[0, j0 + g], qtile)
                if masked:
                    s = jnp.where(_visible(s.shape, (j0 + g) * CHUNK, i * bq), s, -jnp.inf)
                ss.append(s)
            na, nb = ma, mb
            for s in ss:
                na = jnp.maximum(na, jnp.max(s[:CHUNK], axis=0, keepdims=True))
                nb = jnp.maximum(nb, jnp.max(s[CHUNK:], axis=0, keepdims=True))
            al_a, al_b = jnp.exp(ma - na), jnp.exp(mb - nb)
            pv = None
            for g, s in enumerate(ss):
                pt = jnp.concatenate([jnp.exp(s[:CHUNK] - na), jnp.exp(s[CHUNK:] - nb)], axis=0).astype(BF16)
                r = _dot(vt_ref[0, j0 + g], pt)
                pv = r if pv is None else pv + r
            tail = jnp.where(r16 == 2 * p, al_a, jnp.where(r16 == 2 * p + 1, al_b, 1.0))
            acc_ref[...] = acc_ref[...] * jnp.concatenate([_rows_ab(al_a, al_b, bq), tail], axis=0) + pv
            return na, nb

        acc_ref[...] = jnp.zeros_like(acc_ref)
        m = (jnp.full((1, bq), -jnp.inf, F32), jnp.full((1, bq), -jnp.inf, F32))
        m = lax.fori_loop(0, i, lambda t, m: group(t * grp, m, False), m)
        ma, mb = group(i * grp, m, True)
        tailv = acc_ref[CHUNK:rows_t, :]
        la = jnp.sum(jnp.where(r16 == 2 * p, tailv, 0.0), axis=0, keepdims=True)
        lb = jnp.sum(jnp.where(r16 == 2 * p + 1, tailv, 0.0), axis=0, keepdims=True)
        o_ref[...] = (acc_ref[0:CHUNK, :] * _rows_ab(1.0 / la, 1.0 / lb, bq)).T
        lse_ref[0, 0:1, :] = ma + jnp.log(la)
        lse_ref[0, 1:2, :] = mb + jnp.log(lb)

    return pl.pallas_call(
        body,
        name="fox_fwd",
        grid=(C_PAIRS, seq // bq),
        in_specs=[
            pl.BlockSpec((1, 2 * CHUNK, bq), lambda p, i: (p, 0, i)),
            pl.BlockSpec((1, nblk, 2 * CHUNK, 2 * CHUNK), lambda p, i: (p, 0, 0, 0)),
            pl.BlockSpec((1, nblk, rows_t, 2 * CHUNK), lambda p, i: (p, 0, 0, 0)),
        ],
        out_specs=[pl.BlockSpec((bq, LANES), lambda p, i: (i, p)), pl.BlockSpec((1, 2, bq), lambda p, i: (p, 0, i))],
        out_shape=[jax.ShapeDtypeStruct((seq, C_WIDTH), F32), jax.ShapeDtypeStruct((C_PAIRS, 2, seq), F32)],
        scratch_shapes=[pltpu.VMEM((rows_t, bq), F32)],
        compiler_params=_cparams("parallel", "arbitrary"),
    )(qt, ka, vt)


def fox_bwd_prep(dy, o, proj):
    seq = o.shape[0]
    ind = np.zeros((C_WIDTH, LANES), np.float32)
    for h in range(C_HEADS):
        ind[h * C_HDIM:(h + 1) * C_HDIM, h] = 1.0
    ind = jnp.asarray(ind, BF16)
    sel = _piece_selectors()
    sel = jnp.asarray(np.stack([sel[2 * p].T + sel[2 * p + 1].T for p in range(C_PAIRS)]), BF16)

    def body(dy_ref, o_ref, z_ref, ind_ref, sel_ref, do_ref, dz_ref, dot_ref):
        dy_c, o_v, z = dy_ref[...], o_ref[...], z_ref[...]
        sg = jax.nn.sigmoid(z)
        do = dy_c * (z * sg)
        do_ref[...] = do.astype(BF16)
        dz_ref[...] = (dy_c * o_v * (sg * (1.0 + z * (1.0 - sg)))).astype(BF16)
        prod = do * o_v
        hi = prod.astype(BF16)
        lo = (prod - hi.astype(F32)).astype(BF16)
        delta = _dot(hi, ind_ref[...]) + _dot(lo, ind_ref[...])
        d3 = jnp.concatenate(_split3(delta.T), axis=0)
        for p in range(C_PAIRS):
            tail = _dot(sel_ref[p], d3).astype(BF16)
            dot_ref[p] = jnp.concatenate([do[:, p * LANES:(p + 1) * LANES].T.astype(BF16), tail], axis=0)

    return pl.pallas_call(
        body,
        name="fox_bwd_prep",
        grid=(seq // CHUNK,),
        in_specs=[
            pl.BlockSpec((CHUNK, C_WIDTH), lambda i: (i, 1)),
            pl.BlockSpec((CHUNK, C_WIDTH), lambda i: (i, 0)),
            pl.BlockSpec((CHUNK, C_WIDTH), lambda i: (i, 7)),
            _full((C_WIDTH, LANES)), _full((C_PAIRS, LANES, 3 * LANES)),
        ],
        out_specs=[
            pl.BlockSpec((CHUNK, C_WIDTH), lambda i: (i, 0)),
            pl.BlockSpec((CHUNK, C_WIDTH), lambda i: (i, 0)),
            pl.BlockSpec((C_PAIRS, 2 * CHUNK, CHUNK), lambda i: (0, 0, i)),
        ],
        out_shape=[jax.ShapeDtypeStruct((seq, C_WIDTH), BF16)] * 2 + [jax.ShapeDtypeStruct((C_PAIRS, 2 * CHUNK, seq), BF16)],
        compiler_params=_cparams("parallel"),
    )(dy, o, proj, ind, sel)


def fox_bwd(ka, va, kt, qt, dot_t, qa, dob, lse):
    seq = qt.shape[2]
    nblk = seq // CHUNK
    bq = min(C_BQ, seq)
    nq = seq // bq
    kg = min(C_KG, nblk)
    ng = nblk // kg
    rows_t = CHUNK + C_TAIL

    def body(ka_ref, va_ref, kt_ref, qt_ref, dot_ref, qa_ref, do_ref, lse_ref,
             dq_ref, dk_ref, dv_ref, dck_ref, dcq_ref, dqt_acc, dv_acc, dka_acc):
        p, jg = pl.program_id(0), pl.program_id(1)

        @pl.when(jg == 0)
        def _():
            dqt_acc[...] = jnp.zeros_like(dqt_acc)

        dv_acc[...] = jnp.zeros_like(dv_acc)
        dka_acc[...] = jnp.zeros_like(dka_acc)

        def step(i, carry, masked):
            cols = pl.ds(pl.multiple_of(i * bq, bq), bq)
            qtile, dotile = qt_ref[0, :, cols], dot_ref[0, :, cols]
            do, qa_i = do_ref[cols, :], qa_ref[0, cols, :]
            lse2 = jnp.concatenate([jnp.broadcast_to(lse_ref[0, 0:1, cols], (CHUNK, bq)),
                                    jnp.broadcast_to(lse_ref[0, 1:2, cols], (CHUNK, bq))], axis=0)
            for kb in range(kg):
                pt = jnp.exp(_dot(ka_ref[0, kb], qtile) - lse2)
                if masked:
                    pt = jnp.where(_visible(pt.shape, (jg * kg + kb) * CHUNK, i * bq), pt, 0.0)
                ds = pt * _dot(va_ref[0, kb], dotile)
                ptb, dsb = pt.astype(BF16), ds.astype(BF16)
                dv_acc[kb] += _dot(ptb, do)
                dka_acc[kb] += _dot(dsb, qa_i)
                dqt_acc[:, cols] += _dot(kt_ref[0, kb], dsb)
            return carry

        i0 = (jg * kg * CHUNK) // bq
        step(i0, 0, True)
        lax.fori_loop(i0 + 1, nq, functools.partial(step, masked=False), 0)
        lane = lax.broadcasted_iota(jnp.int32, (CHUNK, LANES), 1)
        for kb in range(kg):
            rows = slice(kb * CHUNK, (kb + 1) * CHUNK)
            dk_ref[rows, :] = jnp.where(lane < C_HDIM, dka_acc[kb, 0:CHUNK, 0:LANES], dka_acc[kb, CHUNK:, 0:LANES]).astype(BF16)
            dv_ref[rows, :] = jnp.where(lane < C_HDIM, dv_acc[kb, 0:CHUNK, :], dv_acc[kb, CHUNK:, :]).astype(BF16)
            dck_ref[0, rows, :] = (jnp.where(lane == 2 * p, dka_acc[kb, 0:CHUNK, LANES:], 0.0)
                                   + jnp.where(lane == 2 * p + 1, dka_acc[kb, CHUNK:, LANES:], 0.0))

        @pl.when(jg == ng - 1)
        def _():
            for c in range(nq):
                dq_ref[c * bq:(c + 1) * bq, :] = (dqt_acc[0:CHUNK, c * bq:(c + 1) * bq].T * (C_HDIM ** -0.5)).astype(BF16)
            dcq_ref[0] = dqt_acc[CHUNK:rows_t, :]

    per_pair = lambda r, c: pl.BlockSpec((1, r, c), lambda p, j: (p, 0, 0))
    keys4 = lambda r: pl.BlockSpec((1, kg, r, 2 * CHUNK), lambda p, j: (p, j, 0, 0))
    return pl.pallas_call(
        body,
        name="fox_bwd",
        grid=(C_PAIRS, ng),
        in_specs=[keys4(2 * CHUNK), keys4(2 * CHUNK), keys4(rows_t), per_pair(2 * CHUNK, seq), per_pair(2 * CHUNK, seq),
                  per_pair(seq, 2 * CHUNK), pl.BlockSpec((seq, LANES), lambda p, j: (0, p)), per_pair(2, seq)],
        out_specs=[pl.BlockSpec((seq, LANES), lambda p, j: (0, p)),
                   pl.BlockSpec((kg * CHUNK, LANES), lambda p, j: (j, p)),
                   pl.BlockSpec((kg * CHUNK, LANES), lambda p, j: (j, p)),
                   pl.BlockSpec((1, kg * CHUNK, LANES), lambda p, j: (p, j, 0)),
                   per_pair(C_TAIL, seq)],
        out_shape=[jax.ShapeDtypeStruct((seq, C_WIDTH), BF16)] * 3
        + [jax.ShapeDtypeStruct((C_PAIRS, seq, LANES), F32), jax.ShapeDtypeStruct((C_PAIRS, C_TAIL, seq), F32)],
        scratch_shapes=[pltpu.VMEM((rows_t, seq), F32), pltpu.VMEM((kg, 2 * CHUNK, LANES), F32),
                        pltpu.VMEM((kg, 2 * CHUNK, 2 * CHUNK), F32)],
        compiler_params=_cparams("parallel", "arbitrary"),
    )(ka, va, kt, qt, dot_t, qa, dob, lse)


def fox_post(dcq, dck, proj, bf_row):
    seq = proj.shape[0]
    nc = seq // CHUNK
    triu = jnp.asarray(np.triu(np.ones((CHUNK, CHUNK), np.float32)))

    def body(dq_ref, dk_ref, fl_ref, bf_ref, u_ref, dfl_ref, dbf_ref, carry_ref):
        @pl.when(pl.program_id(0) == 0)
        def _():
            carry_ref[...] = jnp.zeros_like(carry_ref)
            dbf_ref[...] = jnp.zeros_like(dbf_ref)

        rows = (dq_ref[0] + dq_ref[1]) + (dq_ref[2] + dq_ref[3])
        dc = jnp.concatenate([rows, jnp.zeros((CHUNK - C_TAIL, CHUNK), F32)], axis=0).T
        dc = dc - ((dk_ref[0] + dk_ref[1]) + (dk_ref[2] + dk_ref[3]))
        g = _dot(u_ref[...], dc, precision=HI) + carry_ref[...]
        carry_ref[...] += jnp.sum(dc, axis=0, keepdims=True)
        dfl = g * jax.nn.sigmoid(-(fl_ref[:, :LANES] + bf_ref[...]))
        dbf_ref[...] += jnp.sum(dfl, axis=0, keepdims=True)
        dfl_ref[...] = jnp.concatenate([dfl, jnp.zeros_like(dfl)], axis=1).astype(BF16)

    rev = lambda n: nc - 1 - n
    return pl.pallas_call(
        body,
        name="fox_post",
        grid=(nc,),
        in_specs=[
            pl.BlockSpec((C_PAIRS, C_TAIL, CHUNK), lambda n: (0, 0, rev(n))),
            pl.BlockSpec((C_PAIRS, CHUNK, LANES), lambda n: (0, rev(n), 0)),
            pl.BlockSpec((CHUNK, 256), lambda n: (rev(n), 3)),
            _full((1, LANES)), _full((CHUNK, CHUNK)),
        ],
        out_specs=[pl.BlockSpec((CHUNK, 256), lambda n: (rev(n), 0)), _full((1, LANES))],
        out_shape=[jax.ShapeDtypeStruct((seq, 256), BF16), jax.ShapeDtypeStruct((1, LANES), F32)],
        scratch_shapes=[pltpu.VMEM((1, LANES), F32)],
        compiler_params=_cparams("arbitrary"),
    )(dcq, dck, proj, bf_row, triu)


N_DEV = 8
MESH = pl.DeviceIdType.MESH
_ANY = pl.BlockSpec(memory_space=pl.ANY)


def _mesh_pos():
    return lax.axis_index("x"), lax.axis_index("y"), lax.axis_index("c")


def _dev_index(px, py, pc):
    return 4 * px + 2 * py + pc


def _row_pieces(ref, rows):
    return [ref.at[idx + (pl.ds(r, rows),)] for idx in np.ndindex(*ref.shape[:-2]) for r in range(0, ref.shape[-2], rows)]


class _Transfer:
    def __init__(self, src, dst, rows, send_sem, recv_sem, to):
        self.src, self.dst, self.rows, self.sems, self.to = src, dst, rows, (send_sem, recv_sem), to

    def _copy(self, src, dst):
        return pltpu.make_async_remote_copy(src_ref=src, dst_ref=dst, send_sem=self.sems[0], recv_sem=self.sems[1],
                                            device_id=self.to, device_id_type=MESH)

    def start(self):
        for s, d in zip(_row_pieces(self.src, self.rows), _row_pieces(self.dst, self.rows), strict=True):
            self._copy(s, d).start()

    def wait_send(self):
        self._copy(self.src, self.dst).wait_send()

    def wait_recv(self):
        self._copy(self.src, self.dst).wait_recv()


def allgather_weights(wi, wo):
    piece_rows = (128, 64)

    def body(wi_ref, wo_ref, wi_all, wo_all, send_sems, recv_sems, local_sems):
        x, y, c = _mesh_pos()
        me, sibling = (x, y, c), (x, y, 1 - c)
        chips = [(1 - x, y), (x, 1 - y), (1 - x, 1 - y)]
        arrays = ((wi_ref, wi_all), (wo_ref, wo_all))

        def copy(a, k, block, to, own=False):
            src, out = arrays[a]
            slot = out.at[_dev_index(*block)]
            return _Transfer(src if own else slot, slot, piece_rows[a], send_sems.at[a, k], recv_sems.at[a, k], to)

        both = range(len(arrays))
        mine = [pltpu.make_async_copy(arrays[a][0], arrays[a][1].at[_dev_index(*me)], local_sems.at[a]) for a in both]
        for cp in mine:
            cp.start()
        first = [copy(a, 1 + j, me, (*chip, c), own=True) for j, chip in enumerate(chips) for a in both]
        first += [copy(a, 0, me, sibling, own=True) for a in both]
        for cp in first:
            cp.start()
        passed = [copy(a, 4 + j, (*chip, c), sibling) for j, chip in enumerate(chips) for a in both]
        for j, chip in enumerate(chips):
            for a in both:
                copy(a, 1 + j, (*chip, c), me).wait_recv()
            for a in both:
                passed[2 * j + a].start()
        for a in both:
            copy(a, 0, sibling, me).wait_recv()
        for j, chip in enumerate(chips):
            for a in both:
                copy(a, 4 + j, (*chip, 1 - c), me).wait_recv()
        for cp in first + passed:
            cp.wait_send()
        for cp in mine:
            cp.wait()

    return pl.pallas_call(
        body,
        name="allgather_weights",
        in_specs=[_ANY, _ANY],
        out_specs=[_ANY, _ANY],
        out_shape=[jax.ShapeDtypeStruct((N_DEV,) + wi.shape, wi.dtype), jax.ShapeDtypeStruct((N_DEV,) + wo.shape, wo.dtype)],
        scratch_shapes=[pltpu.SemaphoreType.DMA((2, 7)), pltpu.SemaphoreType.DMA((2, 7)), pltpu.SemaphoreType.DMA((2,))],
    )(wi, wo)


N_CHIP = 4


def pair_exchange(gwi, gwo, gsm):
    piece_rows = (256, gwo.shape[-2], gsm.shape[0] // 2)

    def body(gwi_ref, gwo_ref, gsm_ref, qwi, qwo, qsm, send_sems, recv_sems):
        x, y, c = _mesh_pos()
        srcs, outs = (gwi_ref.at[1 - c], gwo_ref.at[1 - c], gsm_ref), (qwi, qwo, qsm)
        copies = [_Transfer(srcs[a], outs[a], piece_rows[a], send_sems.at[a], recv_sems.at[a], (x, y, 1 - c))
                  for a in range(3)]
        for cp in copies:
            cp.start()
        for cp in copies:
            cp.wait_recv()
        for cp in copies:
            cp.wait_send()

    return pl.pallas_call(
        body,
        name="pair_exchange",
        in_specs=[_ANY, _ANY, _ANY],
        out_specs=[_ANY, _ANY, _ANY],
        out_shape=[jax.ShapeDtypeStruct(gwi.shape[1:], gwi.dtype), jax.ShapeDtypeStruct(gwo.shape[1:], gwo.dtype),
                   jax.ShapeDtypeStruct(gsm.shape, gsm.dtype)],
        scratch_shapes=[pltpu.SemaphoreType.DMA((3,)), pltpu.SemaphoreType.DMA((3,))],
    )(gwi, gwo, gsm)


def _slab_spec(lead, rows, n_c, pick=None):
    if pick is None:
        return pl.BlockSpec((1, rows, n_c), lambda i, r, *_: (i, r, 0))
    return pl.BlockSpec((1, 1, rows, n_c), lambda i, r, s: (pick(s), i, r, 0))


def pair_sum(own, other, dtype, rows, name, core=None):
    n, n_r, n_c = other.shape

    def body(*refs):
        a_ref, b_ref, o_ref = refs[-3:]
        o_ref[0] = (a_ref[...].reshape(rows, n_c) + b_ref[0]).astype(dtype)

    grid_spec = pltpu.PrefetchScalarGridSpec(
        num_scalar_prefetch=0 if core is None else 1,
        grid=(n, n_r // rows),
        in_specs=[_slab_spec(1, rows, n_c, None if core is None else (lambda s: s[0])), _slab_spec(1, rows, n_c)],
        out_specs=_slab_spec(1, rows, n_c),
    )
    args = (own, other) if core is None else (core, own, other)
    return pl.pallas_call(
        body,
        name=name,
        grid_spec=grid_spec,
        out_shape=jax.ShapeDtypeStruct((n, n_r, n_c), dtype),
        compiler_params=_cparams("parallel", "parallel"),
    )(*args)


def chip_exchange(swi, swo, ssm):
    piece_rows = (128, swo.shape[-2] // 2, ssm.shape[0] // 2)

    def body(swi_ref, swo_ref, ssm_ref, rwi, rwo, rsm, send_sems, recv_sems, local_sem):
        x, y, c = _mesh_pos()
        chip = 2 * x + y
        srcs, outs = (swi_ref, swo_ref, ssm_ref), (rwi, rwo, rsm)
        three = range(3)
        local = pltpu.make_async_copy(ssm_ref, rsm.at[chip], local_sem)
        local.start()

        def peer_of(k):
            return x ^ ((k >> 1) & 1), y ^ (k & 1)

        def copy(a, k, sending):
            px, py = peer_of(k)
            src = srcs[a] if a == 2 else srcs[a].at[2 * px + py]
            slot = k - 1 if a < 2 else (chip if sending else 2 * px + py)
            return _Transfer(src, outs[a].at[slot], piece_rows[a], send_sems.at[a, k - 1], recv_sems.at[a, k - 1], (px, py, c))

        sends = [copy(a, k, True) for k in range(1, N_CHIP) for a in three]
        for cp in sends:
            cp.start()
        for k in range(1, N_CHIP):
            for a in three:
                copy(a, k, False).wait_recv()
        for cp in sends:
            cp.wait_send()
        local.wait()

    return pl.pallas_call(
        body,
        name="chip_exchange",
        in_specs=[_ANY, _ANY, _ANY],
        out_specs=[_ANY, _ANY, _ANY],
        out_shape=[jax.ShapeDtypeStruct((N_CHIP - 1,) + swi.shape[1:], swi.dtype),
                   jax.ShapeDtypeStruct((N_CHIP - 1,) + swo.shape[1:], swo.dtype),
                   jax.ShapeDtypeStruct((N_CHIP,) + ssm.shape, ssm.dtype)],
        scratch_shapes=[pltpu.SemaphoreType.DMA((3, 3)), pltpu.SemaphoreType.DMA((3, 3)), pltpu.SemaphoreType.DMA],
    )(swi, swo, ssm)


ADAM_LR = 0.001
ADAM_B1 = 0.9
ADAM_B2 = 0.999
ADAM_EPS = 1e-08
ADAM_WD = 0.01
ADAM_STEP = 10


def adam_reduce(parts, w, m, v, rows, name, own=None, chip=None):
    n_l, n_r, n_c = w.shape
    n_parts = parts.shape[0]

    def body(*refs):
        p_ref, w_ref, m_ref, v_ref, g_ref, d_ref, m2_ref, v2_ref = refs[-8:]
        g = p_ref[0, 0].astype(F32)
        if own is not None:
            g = refs[-9][...].reshape(rows, n_c).astype(F32) + g
        for d in range(1, n_parts):
            g = g + p_ref[d, 0].astype(F32)
        m2 = ADAM_B1 * m_ref[0] + (1.0 - ADAM_B1) * g
        v2 = ADAM_B2 * v_ref[0] + (1.0 - ADAM_B2) * (g * g)
        m_hat = m2 / (1.0 - ADAM_B1 ** ADAM_STEP)
        v_hat = v2 / (1.0 - ADAM_B2 ** ADAM_STEP)
        g_ref[0] = g
        d_ref[0] = -ADAM_LR * (m_hat / (jnp.sqrt(v_hat) + ADAM_EPS) + ADAM_WD * w_ref[0])
        m2_ref[0] = m2
        v2_ref[0] = v2

    blk = lambda: pl.BlockSpec((1, rows, n_c), lambda l, r, *_: (l, r, 0))
    in_specs = [pl.BlockSpec((n_parts, 1, rows, n_c), lambda l, r, *_: (0, l, r, 0)), blk(), blk(), blk()]
    args = (parts, w, m, v)
    if own is not None:
        in_specs = [pl.BlockSpec((1, 1, rows, n_c), lambda l, r, s: (s[0], l, r, 0))] + in_specs
        args = (chip, own) + args
    grid_spec = pltpu.PrefetchScalarGridSpec(
        num_scalar_prefetch=0 if own is None else 1, grid=(n_l, n_r // rows), in_specs=in_specs,
        out_specs=[blk(), blk(), blk(), blk()])
    return pl.pallas_call(
        body,
        name=name,
        grid_spec=grid_spec,
        out_shape=[jax.ShapeDtypeStruct(w.shape, F32)] * 4,
        compiler_params=_cparams("parallel", "parallel"),
    )(*args)


_SMALL = (("norm_g", (2, 1024)), ("gmlp_ln_g", (2, 4, 64)), ("gmlp_ln_b", (2, 4, 64)), ("gmlp_w_s", (2, 4, 128, 128)),
          ("gmlp_b_s", (2, 4, 128)), ("hgrn_lb", (2, 256)), ("hgrn_onorm_g", (2, 64)), ("fox_b_f", (2, 8)),
          ("final_norm_g", (1024,)), ("loss", ()))


def _padded(n):
    return -(-n // LANES) * LANES


_SMALL_ROWS = -(-sum(_padded(int(np.prod(s))) for _, s in _SMALL) // LANES // 8) * 8


def _pack_small(vals):
    flat = []
    for (name, shape), a in zip(_SMALL, vals, strict=True):
        n = int(np.prod(shape))
        flat.append(jnp.pad(a.reshape(n).astype(F32), (0, _padded(n) - n)))
    flat = jnp.concatenate(flat)
    return jnp.pad(flat, (0, _SMALL_ROWS * LANES - flat.shape[0])).reshape(_SMALL_ROWS, LANES)


def _unpack_small(slab):
    flat, out, at = slab.reshape(-1), {}, 0
    for name, shape in _SMALL:
        n = int(np.prod(shape))
        out[name] = flat[at:at + n].reshape(shape)
        at += _padded(n)
    return out


def kernel(x, norm_g, w_in, w_out, gmlp_ln_g, gmlp_ln_b, gmlp_w_s, gmlp_b_s, hgrn_lb, hgrn_onorm_g, fox_b_f, final_norm_g, loss_target, m_norm_g, m_w_in, m_w_out, m_gmlp_ln_g, m_gmlp_ln_b, m_gmlp_w_s, m_gmlp_b_s, m_hgrn_lb, m_hgrn_onorm_g, m_fox_b_f, m_final_norm_g, v_norm_g, v_w_in, v_w_out, v_gmlp_ln_g, v_gmlp_ln_b, v_gmlp_w_s, v_gmlp_b_s, v_hgrn_lb, v_hgrn_onorm_g, v_fox_b_f, v_final_norm_g):
    depth = w_in.shape[0]
    seq = x.shape[1]
    assert w_in.shape[2] * N_DEV == N_IN
    xs, tgt = x[0], loss_target[0]

    wi_all, wo_all = allgather_weights(w_in.astype(BF16), w_out.astype(BF16))
    wi_int = assemble_w_in(wi_all)

    ln_g = gmlp_ln_g.reshape(depth, 1, A_WIDTH)
    ln_b = gmlp_ln_b.reshape(depth, 1, A_WIDTH)
    bs_t = jnp.pad(jnp.transpose(gmlp_b_s, (0, 2, 1)), ((0, 0), (0, 0), (0, LANES - A_GROUPS)))
    lb0, lb1 = hgrn_lb[0:1], hgrn_lb[1:2]
    onorm = jnp.tile(hgrn_onorm_g, (1, B_HEADS)).reshape(depth, 1, B_WIDTH)
    bf_row = jnp.pad(fox_b_f, ((0, 0), (0, LANES - C_HEADS))).reshape(depth, 1, LANES)

    saved = []
    xc = xs
    for l in range(depth):
        proj, h = inproj(xc, norm_g[l:l + 1], wi_int, l)
        ya = gmlp_fwd(proj, ln_g[l], ln_b[l], gmlp_w_s[l], bs_t[l])
        yb, states = hgrn_fwd(proj, lb0, lb1, onorm[l], l)
        ka, va, vt, kt, qt, qa = fox_prep(proj, bf_row[l])
        o, lse = fox_fwd(qt, ka, vt)
        xn, yfull = outproj(xc, ya, yb, o, proj, wo_all, l)
        saved.append((xc, proj, h, states, ka, va, kt, qt, qa, o, lse, yfull))
        xc = xn

    dx, d_final_g, loss_tile = final_loss(xc, final_norm_g[None], tgt)

    g_norm = [None] * depth
    g_ln_g, g_ln_b, g_ws, g_bs, g_on, g_bf = ([None] * depth for _ in range(6))
    g_lb0, g_lb1 = jnp.zeros_like(lb0), jnp.zeros_like(lb1)
    dwi = gwo = None
    for l in reversed(range(depth)):
        x_in, proj, h, states, ka, va, kt, qt, qa, o, lse, yfull = saved[l]
        dy, gwo = outproj_bwd(dx, yfull, wo_all, l, gwo)
        d_a, g_ln_g[l], g_ln_b[l], g_ws[l], dbs_t = gmlp_bwd(proj, dy, ln_g[l], ln_b[l], gmlp_w_s[l], bs_t[l])
        g_bs[l] = dbs_t[:, :A_GROUPS].T
        d_b, d0, d1, don = hgrn_bwd(proj, states, dy, lb0, lb1, onorm[l], l)
        g_lb0, g_lb1 = g_lb0 + d0, g_lb1 + d1
        g_on[l] = don.reshape(B_HEADS, B_KDIM).sum(0)
        dob, d_z, dot_t = fox_bwd_prep(dy, o, proj)
        d_q, d_k, d_v, dck, dcq = fox_bwd(ka, va, kt, qt, dot_t, qa, dob, lse)
        d_fl, dbf = fox_post(dcq, dck, proj, bf_row[l])
        g_bf[l] = dbf[0, :C_HEADS]
        dproj = jnp.concatenate([d_a, d_fl, d_b, d_q, d_k, d_v, d_z], axis=1)
        dx, g_norm[l] = inproj_bwd_x(dproj, wi_int, x_in, norm_g[l:l + 1], dx, l)
        dwi = inproj_bwd_w(h, dproj, l, depth, dwi)

    gwi = split_w_in_grad(dwi, w_in.shape[2])
    gsm = _pack_small([
        jnp.concatenate(g_norm), jnp.stack(g_ln_g), jnp.stack(g_ln_b), jnp.stack(g_ws), jnp.stack(g_bs),
        jnp.concatenate([g_lb0, g_lb1]), jnp.stack(g_on), jnp.stack(g_bf), d_final_g, loss_tile[0, 0]])
    core = lax.axis_index("c").astype(jnp.int32).reshape(1)
    chip = (2 * lax.axis_index("x") + lax.axis_index("y")).astype(jnp.int32).reshape(1)
    qwi, qwo, qsm = pair_exchange(gwi, gwo, gsm)
    flat = lambda a: a.reshape(a.shape[:-4] + (N_CHIP * depth,) + a.shape[-2:])
    swi = pair_sum(flat(gwi), flat(qwi), BF16, 256, "pair_sum_w_in", core).reshape(qwi.shape)
    swo = pair_sum(flat(gwo), flat(qwo), BF16, gwo.shape[3], "pair_sum_w_out", core).reshape(qwo.shape)
    ssm = pair_sum(gsm[None], qsm[None], F32, _SMALL_ROWS, "pair_sum_small")[0]
    rwi, rwo, rsm = chip_exchange(swi, swo, ssm)

    small_w = (norm_g, gmlp_ln_g, gmlp_ln_b, gmlp_w_s, gmlp_b_s, hgrn_lb, hgrn_onorm_g, fox_b_f, final_norm_g)
    small_m = (m_norm_g, m_gmlp_ln_g, m_gmlp_ln_b, m_gmlp_w_s, m_gmlp_b_s, m_hgrn_lb, m_hgrn_onorm_g, m_fox_b_f, m_final_norm_g)
    small_v = (v_norm_g, v_gmlp_ln_g, v_gmlp_ln_b, v_gmlp_w_s, v_gmlp_b_s, v_hgrn_lb, v_hgrn_onorm_g, v_fox_b_f, v_final_norm_g)
    zero = jnp.zeros((), F32)
    res_wi = adam_reduce(rwi, w_in, m_w_in, v_w_in, 256, "adam_w_in", own=swi, chip=chip)
    res_wo = adam_reduce(rwo, w_out, m_w_out, v_w_out, w_out.shape[1], "adam_w_out", own=swo, chip=chip)
    res_sm = adam_reduce(rsm[:, None], _pack_small(small_w + (zero,))[None], _pack_small(small_m + (zero,))[None],
                         _pack_small(small_v + (zero,))[None], _SMALL_ROWS, "adam_small")
    res_sm = [_unpack_small(r[0]) for r in res_sm]

    def group(i):
        s = res_sm[i]
        return [s["norm_g"], res_wi[i], res_wo[i], s["gmlp_ln_g"], s["gmlp_ln_b"], s["gmlp_w_s"], s["gmlp_b_s"],
                s["hgrn_lb"], s["hgrn_onorm_g"], s["fox_b_f"], s["final_norm_g"]]

    return (res_sm[0]["loss"], dx[None], *group(0), *group(1), *group(2), *group(3))
```

```python
import functools

import jax
import jax.numpy as jnp
import numpy as np
from jax import lax
from jax.experimental import pallas as pl
from jax.experimental.pallas import tpu as pltpu

F32 = jnp.float32
BF16 = jnp.bfloat16

NORM_EPS = 1e-6
F_FLOOR = 1e-30
CHUNK = 128
LANES = 128
VMEM_LIMIT = 56 * 1024 * 1024


def _cparams(*sem):
    return pltpu.CompilerParams(dimension_semantics=sem, vmem_limit_bytes=VMEM_LIMIT)


def _dot(a, b, dims=(((1,), (0,)), ((), ())), precision=None):
    return lax.dot_general(a, b, dims, precision=precision, preferred_element_type=F32)


_NT = (((1,), (1,)), ((), ()))
_TN = (((0,), (0,)), ((), ()))


def _bf16_pieces(x, n):
    out, r = [], x
    for i in range(n):
        out.append(r.astype(BF16))
        if i + 1 < n:
            r = r - out[-1].astype(F32)
    return out


@functools.partial(jax.custom_vjp, nondiff_argnums=(2,))
def _times_exact(x, e, n):
    return functools.reduce(jnp.add, [_dot(p, e) for p in _bf16_pieces(x, n)])


def _times_exact_fwd(x, e, n):
    return _times_exact(x, e, n), e


def _times_exact_bwd(n, e, g):
    dx = functools.reduce(jnp.add, [lax.dot_general(p, e, _NT, preferred_element_type=F32) for p in _bf16_pieces(g, n)])
    return dx, jnp.zeros_like(e)


_times_exact.defvjp(_times_exact_fwd, _times_exact_bwd)


@functools.partial(jax.custom_vjp, nondiff_argnums=(2,))
def _exact_times(e, x, n):
    return functools.reduce(jnp.add, [_dot(e, p) for p in _bf16_pieces(x, n)])


def _exact_times_fwd(e, x, n):
    return _exact_times(e, x, n), e


def _exact_times_bwd(n, e, g):
    dx = functools.reduce(jnp.add, [lax.dot_general(e, p, _TN, preferred_element_type=F32) for p in _bf16_pieces(g, n)])
    return jnp.zeros_like(e), dx


_exact_times.defvjp(_exact_times_fwd, _exact_times_bwd)


def _group_mean_matrix(width, group):
    idx = np.arange(width) // group
    return jnp.asarray((idx[:, None] == idx[None, :]).astype(np.float32) / group, BF16)


def _group_ones_matrix(width, group):
    idx = np.arange(width) // group
    return jnp.asarray((idx[:, None] == idx[None, :]).astype(np.float32), BF16)


A_WIDTH = 256
A_GROUPS = 4
A_GDIM = 64


A_ROWS = 512


def _gmlp_chunk(x3, ln_g, ln_b, w_s, bs_t, mean_m, gind):
    n = x3.shape[0] // CHUNK
    u = jax.nn.gelu(x3[:, :A_WIDTH])
    v = jax.nn.gelu(x3[:, A_WIDTH:2 * A_WIDTH])
    z = x3[:, 2 * A_WIDTH:]
    mu = _times_exact(v, mean_m, 2)
    d = v - mu
    var = _times_exact(d * d, mean_m, 2)
    vn = d * lax.rsqrt(var + NORM_EPS) * ln_g + ln_b
    vnb = vn.astype(BF16)
    wide = jnp.concatenate([vnb[i * CHUNK:(i + 1) * CHUNK] for i in range(n)], axis=1)
    row = lax.broadcasted_iota(jnp.int32, (CHUNK, CHUNK), 0)
    col = lax.broadcasted_iota(jnp.int32, (CHUNK, CHUNK), 1)
    causal = row >= col
    lane_g = lax.shift_right_logical(lax.broadcasted_iota(jnp.int32, (CHUNK, n * A_WIDTH), 1), 6) & (A_GROUPS - 1)
    bias = _times_exact(bs_t, gind, 3)
    mixed = jnp.concatenate([bias] * n, axis=1)
    for g in range(A_GROUPS):
        wc = jnp.where(causal, w_s[g], 0.0).astype(BF16)
        mixed = mixed + jnp.where(lane_g == g, _dot(wc, wide), 0.0)
    mixed = jnp.concatenate([mixed[:, i * A_WIDTH:(i + 1) * A_WIDTH] for i in range(n)], axis=0)
    return u * mixed * jax.nn.silu(z)


def _gmlp_consts():
    gind = np.zeros((LANES, A_WIDTH), np.float32)
    for g in range(A_GROUPS):
        gind[g, g * A_GDIM:(g + 1) * A_GDIM] = 1.0
    return _group_mean_matrix(A_WIDTH, A_GDIM), jnp.asarray(gind, BF16)


def _full(shape):
    return pl.BlockSpec(shape, lambda *_: (0,) * len(shape))


def gmlp_fwd(proj, ln_g, ln_b, w_s, bs_t):
    seq = proj.shape[0]
    rows = min(A_ROWS, seq)
    mean_m, gind = _gmlp_consts()

    def body(x_ref, g_ref, b_ref, w_ref, bs_ref, m_ref, gi_ref, y_ref):
        y = _gmlp_chunk(x_ref[...], g_ref[...], b_ref[...], w_ref[...], bs_ref[...], m_ref[...], gi_ref[...])
        y_ref[...] = y.astype(BF16)

    return pl.pallas_call(
        body,
        name="gmlp_fwd",
        grid=(seq // rows,),
        in_specs=[
            pl.BlockSpec((rows, 3 * A_WIDTH), lambda n: (n, 0)),
            _full((1, A_WIDTH)), _full((1, A_WIDTH)), _full((A_GROUPS, CHUNK, CHUNK)), _full((CHUNK, LANES)),
            _full((A_WIDTH, A_WIDTH)), _full((LANES, A_WIDTH)),
        ],
        out_specs=pl.BlockSpec((rows, A_WIDTH), lambda n: (n, 0)),
        out_shape=jax.ShapeDtypeStruct((seq, A_WIDTH), BF16),
        compiler_params=_cparams("parallel"),
    )(proj, ln_g, ln_b, w_s, bs_t, mean_m, gind)


def gmlp_bwd(proj, dy, ln_g, ln_b, w_s, bs_t):
    seq = proj.shape[0]
    rows = min(A_ROWS, seq)
    mean_m, gind = _gmlp_consts()

    def body(x_ref, dy_ref, g_ref, b_ref, w_ref, bs_ref, m_ref, gi_ref, dx_ref, dg_ref, db_ref, dw_ref, dbs_ref):
        fn = functools.partial(_gmlp_chunk, mean_m=m_ref[...], gind=gi_ref[...])
        _, vjp = jax.vjp(fn, x_ref[...], g_ref[...], b_ref[...], w_ref[...], bs_ref[...])
        dx, dg, db, dw, dbs = vjp(dy_ref[...])
        dx_ref[...] = dx.astype(BF16)

        @pl.when(pl.program_id(0) == 0)
        def _():
            dg_ref[...] = jnp.zeros_like(dg_ref)
            db_ref[...] = jnp.zeros_like(db_ref)
            dw_ref[...] = jnp.zeros_like(dw_ref)
            dbs_ref[...] = jnp.zeros_like(dbs_ref)

        dg_ref[...] += dg
        db_ref[...] += db
        dw_ref[...] += dw
        dbs_ref[...] += dbs

    return pl.pallas_call(
        body,
        name="gmlp_bwd",
        grid=(seq // rows,),
        in_specs=[
            pl.BlockSpec((rows, 3 * A_WIDTH), lambda n: (n, 0)),
            pl.BlockSpec((rows, A_WIDTH), lambda n: (n, 0)),
            _full((1, A_WIDTH)), _full((1, A_WIDTH)), _full((A_GROUPS, CHUNK, CHUNK)), _full((CHUNK, LANES)),
            _full((A_WIDTH, A_WIDTH)), _full((LANES, A_WIDTH)),
        ],
        out_specs=[
            pl.BlockSpec((rows, 3 * A_WIDTH), lambda n: (n, 0)),
            _full((1, A_WIDTH)), _full((1, A_WIDTH)), _full((A_GROUPS, CHUNK, CHUNK)), _full((CHUNK, LANES)),
        ],
        out_shape=[
            jax.ShapeDtypeStruct((seq, 3 * A_WIDTH), BF16),
            jax.ShapeDtypeStruct((1, A_WIDTH), F32), jax.ShapeDtypeStruct((1, A_WIDTH), F32),
            jax.ShapeDtypeStruct((A_GROUPS, CHUNK, CHUNK), F32), jax.ShapeDtypeStruct((CHUNK, LANES), F32),
        ],
        compiler_params=_cparams("arbitrary"),
    )(proj, dy, ln_g, ln_b, w_s, bs_t, mean_m, gind)


B_WIDTH = 256
B_HEADS = 4
B_KDIM = 64
B_LEVELS = (64, 32, 16, 8, 4, 2, 1)


def _hgrn_consts():
    t = np.arange(CHUNK)
    u = t[None, :]
    mats = [np.tril(np.ones((CHUNK, CHUNK), np.float32))]
    for m in B_LEVELS:
        p = (t // (2 * m)) * (2 * m) + m - 1
        right = (t % (2 * m)) >= m
        sel = np.where(right[:, None], (u > p[:, None]) & (u <= t[:, None]), (u > t[:, None]) & (u <= p[:, None]))
        mats.append(sel.astype(np.float32))
    return jnp.asarray(np.concatenate(mats, 0), BF16), _group_ones_matrix(B_WIDTH, B_KDIM)


def _hgrn_lower_bound(lb0, lb1, layer):
    mx = jnp.maximum(lb0, lb1)
    e0 = jnp.exp(lb0 - mx)
    e1 = jnp.exp(lb1 - mx)
    p0 = e0 / (e0 + e1)
    p1 = e1 / (e0 + e1)
    cs = p0 if layer == 0 else p0 + p1
    return jnp.clip(cs - p0, 0.0, 1.0 - 1e-6)


def _hgrn_chunk(x4, st, lb0, lb1, onorm, layer, tstack, ones_bd):
    q_raw, fl, v, zg = (x4[:, i * B_WIDTH:(i + 1) * B_WIDTH] for i in range(4))
    lb = _hgrn_lower_bound(lb0, lb1, layer)
    q = jax.nn.silu(q_raw) * (B_KDIM ** -0.5)
    f = lb + (1.0 - lb) * jax.nn.sigmoid(fl)
    logf = jnp.log(jnp.maximum(f, F_FLOOR))
    k = (1.0 - lb) * jax.nn.sigmoid(-fl)
    dall = _exact_times(tstack, logf, 3)
    b = dall[:CHUNK]
    b_last = jnp.sum(logf, axis=0, keepdims=True)
    vb = v.astype(BF16)

    lane_h = lax.shift_right_logical(lax.broadcasted_iota(jnp.int32, (CHUNK, B_WIDTH), 1), 6)
    row = lax.broadcasted_iota(jnp.int32, (CHUNK, B_WIDTH), 0)
    srow = lax.broadcasted_iota(jnp.int32, (B_HEADS * CHUNK, CHUNK), 0) & (CHUNK - 1)
    scol = lax.broadcasted_iota(jnp.int32, (B_HEADS * CHUNK, CHUNK), 1)

    def heads_on_rows(a):
        return jnp.concatenate([jnp.where(lane_h == h, a, 0.0) for h in range(B_HEADS)], axis=0)

    def heads_from_rows(r):
        out = jnp.where(lane_h == 0, r[:CHUNK], 0.0)
        for h in range(1, B_HEADS):
            out = out + jnp.where(lane_h == h, r[h * CHUNK:(h + 1) * CHUNK], 0.0)
        return out

    o = lax.dot_general((q * jnp.exp(b)).astype(BF16), st.astype(BF16), _NT, preferred_element_type=F32)
    scores = jnp.zeros((B_HEADS * CHUNK, CHUNK), F32)
    for li, m in enumerate(B_LEVELS):
        e = jnp.exp(dall[(li + 1) * CHUNK:(li + 2) * CHUNK])
        right = (row & (2 * m - 1)) >= m
        qt = jnp.where(right, q * e, 0.0)
        kt = jnp.where(right, 0.0, k * e)
        sc = lax.dot_general(heads_on_rows(qt).astype(BF16), kt.astype(BF16), _NT, preferred_element_type=F32)
        sh = int(np.log2(2 * m))
        same = lax.shift_right_logical(srow, sh) == lax.shift_right_logical(scol, sh)
        scores = scores + jnp.where(same, sc, 0.0)
    o = o + heads_from_rows(_dot(scores.astype(BF16), vb))
    o = o + _times_exact(q * k, ones_bd, 2) * v

    kv = lax.dot_general(vb, (k * jnp.exp(b_last - b)).astype(BF16), _TN, preferred_element_type=F32)
    st_new = st * jnp.exp(b_last) + jnp.where(ones_bd > 0.5, kv, 0.0)

    ms = _times_exact(o * o, ones_bd, 2) * (1.0 / B_KDIM)
    y = o * lax.rsqrt(ms + NORM_EPS) * onorm * jax.nn.silu(zg)
    return y, st_new


B_ROWS = 256


def _hgrn_rows(x4, st, lb0, lb1, onorm, layer, tstack, ones_bd):
    ys = []
    for i in range(x4.shape[0] // CHUNK):
        y, st = _hgrn_chunk(x4[i * CHUNK:(i + 1) * CHUNK], st, lb0, lb1, onorm, layer, tstack, ones_bd)
        ys.append(y)
    return jnp.concatenate(ys, axis=0), st


def hgrn_fwd(proj, lb0, lb1, onorm, layer):
    seq = proj.shape[0]
    rows = min(B_ROWS, seq)
    nc = seq // rows
    tstack, ones_bd = _hgrn_consts()

    def body(x_ref, lb0_ref, lb1_ref, on_ref, t_ref, e_ref, y_ref, st_out_ref, st_ref):
        @pl.when(pl.program_id(0) == 0)
        def _():
            st_ref[...] = jnp.zeros_like(st_ref)

        st = st_ref[...]
        st_out_ref[0] = st
        y, st_new = _hgrn_rows(x_ref[...], st, lb0_ref[...], lb1_ref[...], on_ref[...], layer, t_ref[...], e_ref[...])
        y_ref[...] = y.astype(BF16)
        st_ref[...] = st_new

    return pl.pallas_call(
        body,
        name=f"hgrn_fwd_{layer}",
        grid=(nc,),
        in_specs=[
            pl.BlockSpec((rows, 4 * B_WIDTH), lambda n: (n, 1)),
            _full((1, B_WIDTH)), _full((1, B_WIDTH)), _full((1, B_WIDTH)),
            _full(((len(B_LEVELS) + 1) * CHUNK, CHUNK)), _full((B_WIDTH, B_WIDTH)),
        ],
        out_specs=[
            pl.BlockSpec((rows, B_WIDTH), lambda n: (n, 0)),
            pl.BlockSpec((1, B_WIDTH, B_WIDTH), lambda n: (n, 0, 0)),
        ],
        out_shape=[jax.ShapeDtypeStruct((seq, B_WIDTH), BF16), jax.ShapeDtypeStruct((nc, B_WIDTH, B_WIDTH), F32)],
        scratch_shapes=[pltpu.VMEM((B_WIDTH, B_WIDTH), F32)],
        compiler_params=_cparams("arbitrary"),
    )(proj, lb0, lb1, onorm, tstack, ones_bd)


def hgrn_bwd(proj, states, dy, lb0, lb1, onorm, layer):
    seq = proj.shape[0]
    rows = min(B_ROWS, seq)
    nc = seq // rows
    tstack, ones_bd = _hgrn_consts()

    def body(x_ref, st_in_ref, dy_ref, lb0_ref, lb1_ref, on_ref, t_ref, e_ref, dx_ref, d0_ref, d1_ref, don_ref, dst_ref):
        @pl.when(pl.program_id(0) == 0)
        def _():
            dst_ref[...] = jnp.zeros_like(dst_ref)
            d0_ref[...] = jnp.zeros_like(d0_ref)
            d1_ref[...] = jnp.zeros_like(d1_ref)
            don_ref[...] = jnp.zeros_like(don_ref)

        fn = functools.partial(_hgrn_rows, layer=layer, tstack=t_ref[...], ones_bd=e_ref[...])
        _, vjp = jax.vjp(fn, x_ref[...], st_in_ref[0], lb0_ref[...], lb1_ref[...], on_ref[...])
        dx, dst, d0, d1, don = vjp((dy_ref[...], dst_ref[...]))
        dx_ref[...] = dx.astype(BF16)
        dst_ref[...] = dst
        d0_ref[...] += d0
        d1_ref[...] += d1
        don_ref[...] += don

    rev = lambda n: nc - 1 - n
    return pl.pallas_call(
        body,
        name=f"hgrn_bwd_{layer}",
        grid=(nc,),
        in_specs=[
            pl.BlockSpec((rows, 4 * B_WIDTH), lambda n: (rev(n), 1)),
            pl.BlockSpec((1, B_WIDTH, B_WIDTH), lambda n: (rev(n), 0, 0)),
            pl.BlockSpec((rows, B_WIDTH), lambda n: (rev(n), 1)),
            _full((1, B_WIDTH)), _full((1, B_WIDTH)), _full((1, B_WIDTH)),
            _full(((len(B_LEVELS) + 1) * CHUNK, CHUNK)), _full((B_WIDTH, B_WIDTH)),
        ],
        out_specs=[
            pl.BlockSpec((rows, 4 * B_WIDTH), lambda n: (rev(n), 0)),
            _full((1, B_WIDTH)), _full((1, B_WIDTH)), _full((1, B_WIDTH)),
        ],
        out_shape=[jax.ShapeDtypeStruct((seq, 4 * B_WIDTH), BF16)] + [jax.ShapeDtypeStruct((1, B_WIDTH), F32)] * 3,
        scratch_shapes=[pltpu.VMEM((B_WIDTH, B_WIDTH), F32)],
        compiler_params=_cparams("arbitrary"),
    )(proj, states, dy, lb0, lb1, onorm, tstack, ones_bd)


D_MODEL = 1024
D_INT = 4096


def _rms_stats(xf):
    r = lax.rsqrt(jnp.mean(xf * xf, axis=-1, keepdims=True) + NORM_EPS)
    return r, xf * r


def _rms_bwd(dy, g, r, xh):
    u = dy * g
    return r * (u - xh * jnp.mean(u * xh, axis=-1, keepdims=True))


def inproj(x, g, w, layer):
    seq = x.shape[0]
    tm, tn = min(seq, 1024), 512

    def body(x_ref, g_ref, w_ref, p_ref, h_ref):
        @pl.when(pl.program_id(1) == 0)
        def _():
            _, xh = _rms_stats(x_ref[...])
            h_ref[...] = (xh * g_ref[...]).astype(BF16)

        p_ref[...] = _dot(h_ref[...], w_ref[0])

    return pl.pallas_call(
        body,
        name="inproj",
        grid=(seq // tm, D_INT // tn),
        in_specs=[
            pl.BlockSpec((tm, D_MODEL), lambda i, j: (i, 0)),
            _full((1, D_MODEL)),
            pl.BlockSpec((1, D_MODEL, tn), lambda i, j: (layer, 0, j)),
        ],
        out_specs=[pl.BlockSpec((tm, tn), lambda i, j: (i, j)), pl.BlockSpec((tm, D_MODEL), lambda i, j: (i, 0))],
        out_shape=[jax.ShapeDtypeStruct((seq, D_INT), F32), jax.ShapeDtypeStruct((seq, D_MODEL), BF16)],
        compiler_params=_cparams("parallel", "arbitrary"),
    )(x, g, w)


def outproj(x, ya, yb, o, proj, wo, layer):
    seq = x.shape[0]
    tm = min(seq, 512)
    blk = wo.shape[2]

    def body(x_ref, ya_ref, yb_ref, o_ref, z_ref, w_ref, xn_ref, y_ref):
        yc = (o_ref[...] * jax.nn.silu(z_ref[...])).astype(BF16)
        y = jnp.concatenate([ya_ref[...], yb_ref[...], yc], axis=1)
        y_ref[...] = y
        w = jnp.concatenate([w_ref[d, 0] for d in range(N_DEV)], axis=0)
        xn_ref[...] = x_ref[...] + _dot(y, w)

    return pl.pallas_call(
        body,
        name="outproj",
        grid=(seq // tm,),
        in_specs=[
            pl.BlockSpec((tm, D_MODEL), lambda i: (i, 0)),
            pl.BlockSpec((tm, 256), lambda i: (i, 0)),
            pl.BlockSpec((tm, 256), lambda i: (i, 0)),
            pl.BlockSpec((tm, 512), lambda i: (i, 0)),
            pl.BlockSpec((tm, 512), lambda i: (i, 7)),
            pl.BlockSpec((N_DEV, 1, blk, D_MODEL), lambda i: (0, layer, 0, 0)),
        ],
        out_specs=[pl.BlockSpec((tm, D_MODEL), lambda i: (i, 0)), pl.BlockSpec((tm, D_MODEL), lambda i: (i, 0))],
        out_shape=[jax.ShapeDtypeStruct((seq, D_MODEL), F32), jax.ShapeDtypeStruct((seq, D_MODEL), BF16)],
        compiler_params=_cparams("parallel"),
    )(x, ya, yb, o, proj, wo)


def outproj_bwd(dx, y, wo, layer, stacked=None):
    seq = dx.shape[0]
    ts = min(seq, 512)
    _, depth, blk, _ = wo.shape

    def body(dx_ref, y_ref, w_ref, *refs):
        dy_ref, dw_ref = refs[-2:]

        @pl.when(pl.program_id(0) == 0)
        def _():
            dw_ref[...] = jnp.zeros_like(dw_ref)

        dxb = dx_ref[...].astype(BF16)
        w = jnp.concatenate([w_ref[d, 0] for d in range(N_DEV)], axis=0)
        dy_ref[...] = lax.dot_general(dxb, w, _NT, preferred_element_type=F32)
        dw = lax.dot_general(y_ref[...], dxb, _TN, preferred_element_type=F32)
        for d in range(N_DEV):
            dw_ref[d % 2, d // 2, 0] += dw[d * blk:(d + 1) * blk]

    carried = () if stacked is None else (stacked,)
    out_shape = [jax.ShapeDtypeStruct((seq, D_MODEL), F32), jax.ShapeDtypeStruct((2, N_CHIP, depth, blk, D_MODEL), F32)]
    return pl.pallas_call(
        body,
        name="outproj_bwd",
        grid=(seq // ts,),
        in_specs=[
            pl.BlockSpec((ts, D_MODEL), lambda i: (i, 0)),
            pl.BlockSpec((ts, D_MODEL), lambda i: (i, 0)),
            pl.BlockSpec((N_DEV, 1, blk, D_MODEL), lambda i: (0, layer, 0, 0)),
        ] + [_ANY] * len(carried),
        out_specs=[pl.BlockSpec((ts, D_MODEL), lambda i: (i, 0)),
                   pl.BlockSpec((2, N_CHIP, 1, blk, D_MODEL), lambda i: (0, 0, layer, 0, 0))],
        out_shape=out_shape,
        input_output_aliases={3: 1} if carried else {},
        compiler_params=_cparams("arbitrary"),
    )(dx, y, wo, *carried)


def inproj_bwd_x(dproj, w, x, g, dx_in, layer):
    seq = x.shape[0]
    tm, tk = min(seq, 512), 1024
    nk = D_INT // tk

    def body(dp_ref, w_ref, x_ref, g_ref, dxin_ref, dx_ref, dg_ref, acc_ref):
        k = pl.program_id(1)

        @pl.when(k == 0)
        def _():
            acc_ref[...] = jnp.zeros_like(acc_ref)

        acc_ref[...] += lax.dot_general(dp_ref[...], w_ref[0], _NT, preferred_element_type=F32)

        @pl.when(k == nk - 1)
        def _():
            @pl.when(pl.program_id(0) == 0)
            def _():
                dg_ref[...] = jnp.zeros_like(dg_ref)

            dh = acc_ref[...]
            g = g_ref[...]
            r, xh = _rms_stats(x_ref[...])
            dg_ref[...] += jnp.sum(dh * xh, axis=0, keepdims=True)
            dx_ref[...] = dxin_ref[...] + _rms_bwd(dh, g, r, xh)

    return pl.pallas_call(
        body,
        name="inproj_bwd_x",
        grid=(seq // tm, nk),
        in_specs=[
            pl.BlockSpec((tm, tk), lambda i, k: (i, k)),
            pl.BlockSpec((1, D_MODEL, tk), lambda i, k: (layer, 0, k)),
            pl.BlockSpec((tm, D_MODEL), lambda i, k: (i, 0)),
            _full((1, D_MODEL)),
            pl.BlockSpec((tm, D_MODEL), lambda i, k: (i, 0)),
        ],
        out_specs=[pl.BlockSpec((tm, D_MODEL), lambda i, k: (i, 0)), _full((1, D_MODEL))],
        out_shape=[jax.ShapeDtypeStruct((seq, D_MODEL), F32), jax.ShapeDtypeStruct((1, D_MODEL), F32)],
        scratch_shapes=[pltpu.VMEM((tm, D_MODEL), F32)],
        compiler_params=_cparams("arbitrary", "arbitrary"),
    )(dproj, w, x, g, dx_in)


def inproj_bwd_w(h, dproj, layer, depth, stacked=None):
    seq = h.shape[0]
    ts, tn = min(seq, 1024), 512

    def body(h_ref, dp_ref, *refs):
        dw_ref = refs[-1]

        @pl.when(pl.program_id(1) == 0)
        def _():
            dw_ref[...] = jnp.zeros_like(dw_ref)

        dw_ref[0] += lax.dot_general(h_ref[...], dp_ref[...], _TN, preferred_element_type=F32)

    carried = () if stacked is None else (stacked,)
    return pl.pallas_call(
        body,
        name="inproj_bwd_w",
        grid=(D_INT // tn, seq // ts),
        in_specs=[pl.BlockSpec((ts, D_MODEL), lambda j, s: (s, 0)), pl.BlockSpec((ts, tn), lambda j, s: (s, j))]
        + [_ANY] * len(carried),
        out_specs=pl.BlockSpec((1, D_MODEL, tn), lambda j, s: (layer, 0, j)),
        out_shape=jax.ShapeDtypeStruct((depth, D_MODEL, D_INT), F32),
        input_output_aliases={2: 0} if carried else {},
        compiler_params=_cparams("parallel", "arbitrary"),
    )(h, dproj, *carried)


N_IN = 3848


def _internal_of(col):
    return col if col < 768 else (col + 256 if col < 3840 else 768 + col - 3840)


def _column_runs(n_shard):
    runs = []
    for d in range(N_IN // n_shard):
        mine = []
        for j in range(n_shard):
            ci = _internal_of(d * n_shard + j)
            if mine and mine[-1][0] + mine[-1][1] == ci:
                mine[-1][1] += 1
            else:
                mine.append([ci, 1, j])
        runs.append(mine)
    return runs


def assemble_w_in(wi_all):
    n_dev, depth, _, n_shard = wi_all.shape
    tr = 256
    pieces = [[] for _ in range(D_INT // LANES)]
    for d, mine in enumerate(_column_runs(n_shard)):
        for ci, ln, off in mine:
            while ln > 0:
                blk, at = divmod(ci, LANES)
                take = min(ln, LANES - at)
                pieces[blk].append((at, take, d, off))
                ci, ln, off = ci + take, ln - take, off + take

    def body(x_ref, o_ref):
        for blk, parts in enumerate(pieces):
            vals, at = [], 0
            for start, ln, d, off in sorted(parts):
                if start > at:
                    vals.append(jnp.zeros((tr, start - at), BF16))
                vals.append(x_ref[d, 0, :, off:off + ln])
                at = start + ln
            if at < LANES:
                vals.append(jnp.zeros((tr, LANES - at), BF16))
            o_ref[0, :, blk * LANES:(blk + 1) * LANES] = vals[0] if len(vals) == 1 else jnp.concatenate(vals, axis=1)

    return pl.pallas_call(
        body,
        name="assemble_w_in",
        grid=(depth, D_MODEL // tr),
        in_specs=[pl.BlockSpec((n_dev, 1, tr, n_shard), lambda l, r: (0, l, r, 0))],
        out_specs=pl.BlockSpec((1, tr, D_INT), lambda l, r: (l, r, 0)),
        out_shape=jax.ShapeDtypeStruct((depth, D_MODEL, D_INT), BF16),
        compiler_params=_cparams("parallel", "parallel"),
    )(wi_all)


def split_w_in_grad(dwi, n_shard):
    depth = dwi.shape[0]
    tr = 256
    runs = _column_runs(n_shard)

    def body(x_ref, o_ref):
        for d, mine in enumerate(runs):
            for ci, ln, off in mine:
                o_ref[d % 2, d // 2, 0, :, off:off + ln] = x_ref[0, :, ci:ci + ln]

    return pl.pallas_call(
        body,
        name="split_w_in_grad",
        grid=(depth, D_MODEL // tr),
        in_specs=[pl.BlockSpec((1, tr, D_INT), lambda l, r: (l, r, 0))],
        out_specs=pl.BlockSpec((2, N_CHIP, 1, tr, n_shard), lambda l, r: (0, 0, l, r, 0)),
        out_shape=jax.ShapeDtypeStruct((2, N_CHIP, depth, D_MODEL, n_shard), F32),
        compiler_params=_cparams("parallel", "parallel"),
    )(dwi)


def final_loss(x, g, tgt):
    seq = x.shape[0]
    tm = min(seq, 512)

    def body(x_ref, g_ref, t_ref, dx_ref, dg_ref, loss_ref):
        @pl.when(pl.program_id(0) == 0)
        def _():
            dg_ref[...] = jnp.zeros_like(dg_ref)
            loss_ref[...] = jnp.zeros_like(loss_ref)

        g = g_ref[...]
        r, xh = _rms_stats(x_ref[...])
        err = xh * g - t_ref[...]
        sq = jnp.sum(jnp.sum(err * err, axis=1, keepdims=True), axis=0, keepdims=True)
        loss_ref[...] += jnp.broadcast_to(sq * (0.5 / D_MODEL), loss_ref.shape)
        dout = err * (1.0 / D_MODEL)
        dg_ref[...] += jnp.sum(dout * xh, axis=0, keepdims=True)
        dx_ref[...] = _rms_bwd(dout, g, r, xh)

    return pl.pallas_call(
        body,
        name="final_loss",
        grid=(seq // tm,),
        in_specs=[pl.BlockSpec((tm, D_MODEL), lambda i: (i, 0)), _full((1, D_MODEL)), pl.BlockSpec((tm, D_MODEL), lambda i: (i, 0))],
        out_specs=[pl.BlockSpec((tm, D_MODEL), lambda i: (i, 0)), _full((1, D_MODEL)), _full((8, LANES))],
        out_shape=[jax.ShapeDtypeStruct((seq, D_MODEL), F32), jax.ShapeDtypeStruct((1, D_MODEL), F32), jax.ShapeDtypeStruct((8, LANES), F32)],
        compiler_params=_cparams("arbitrary"),
    )(x, g, tgt)


C_WIDTH = 512
C_HEADS = 8
C_HDIM = 64
C_PAIRS = C_HEADS // 2
C_BQ = 512
C_TAIL = 16
C_KG = 2


def _split3(x):
    hi = x.astype(BF16)
    r = x - hi.astype(F32)
    mid = r.astype(BF16)
    return hi, mid, (r - mid.astype(F32)).astype(BF16)


def _piece_selectors():
    sel = np.zeros((C_HEADS, 3 * LANES, LANES), np.float32)
    for p in range(C_PAIRS):
        for e in range(2):
            for t in range(3):
                sel[2 * p + e, t * LANES + 2 * p + e, 3 * e + t] = -1.0
    return sel


def fox_prep(proj, bf_row):
    seq = proj.shape[0]
    nblk = seq // CHUNK
    tril = jnp.asarray(np.tril(np.ones((CHUNK, CHUNK), np.float32)), BF16)
    sel = jnp.asarray(_piece_selectors(), BF16)
    rows_t = CHUNK + C_TAIL

    def body(fl_ref, q_ref, k_ref, v_ref, bf_ref, l_ref, sel_ref, ka_ref, va_ref, vt_ref, kt_ref, qt_ref, qa_ref, carry_ref):
        @pl.when(pl.program_id(0) == 0)
        def _():
            carry_ref[...] = jnp.zeros_like(carry_ref)

        lf = jax.nn.log_sigmoid(fl_ref[:, :LANES] + bf_ref[...])
        c = _exact_times(l_ref[...], lf, 3) + carry_ref[...]
        carry_ref[...] += jnp.sum(lf, axis=0, keepdims=True)
        c3 = jnp.concatenate(_split3(c), axis=1)
        lane = lax.broadcasted_iota(jnp.int32, (CHUNK, LANES), 1)
        row = lax.broadcasted_iota(jnp.int32, (CHUNK, LANES), 0)
        r16 = lax.broadcasted_iota(jnp.int32, (C_TAIL, 2 * CHUNK), 0)
        l16 = lax.broadcasted_iota(jnp.int32, (C_TAIL, 2 * CHUNK), 1)
        zero = jnp.zeros((CHUNK, LANES), BF16)
        one = jnp.ones((CHUNK, LANES), BF16)

        def by_keys(x, right_a, right_b):
            xb = x.astype(BF16)
            top = jnp.concatenate([jnp.where(lane < C_HDIM, xb, zero), right_a], axis=1)
            return jnp.concatenate([top, jnp.concatenate([jnp.where(lane < C_HDIM, zero, xb), right_b], axis=1)], axis=0)

        def by_lanes(x, tail):
            xt = x.T.astype(BF16)
            main = jnp.concatenate([jnp.where(row < C_HDIM, xt, zero), jnp.where(row < C_HDIM, zero, xt)], axis=1)
            return jnp.concatenate([main, tail], axis=0)

        for p in range(C_PAIRS):
            cols = slice(p * LANES, (p + 1) * LANES)
            q2, k2, v2 = q_ref[:, cols] * (C_HDIM ** -0.5), k_ref[:, cols], v_ref[:, cols]
            negc = [_dot(c3, sel_ref[2 * p + e]).astype(BF16) for e in range(2)]
            ones3 = [jnp.where((lane >= 3 * e) & (lane < 3 * e + 3), one, zero) for e in range(2)]
            tail = jnp.where(((r16 == 2 * p) & (l16 < CHUNK)) | ((r16 == 2 * p + 1) & (l16 >= CHUNK)), 1.0, 0.0).astype(BF16)
            ka_ref[p, 0] = by_keys(k2, negc[0], negc[1])
            va_ref[p, 0] = by_keys(v2, ones3[0], ones3[1])
            kt_ref[p, 0] = by_lanes(k2, tail)
            vt_ref[p, 0] = by_lanes(v2, tail)
            qt_ref[p] = jnp.concatenate([q2.T.astype(BF16), jnp.where(row < 6, one, zero)], axis=0)
            qa_ref[p] = jnp.concatenate([q2.astype(BF16), jnp.where((lane == 2 * p) | (lane == 2 * p + 1), one, zero)], axis=1)

    wide = lambda j: pl.BlockSpec((CHUNK, C_WIDTH), lambda n: (n, j))
    sq = lambda r: pl.BlockSpec((C_PAIRS, 1, r, 2 * CHUNK), lambda n: (0, n, 0, 0))
    return pl.pallas_call(
        body,
        name="fox_prep",
        grid=(nblk,),
        in_specs=[pl.BlockSpec((CHUNK, 256), lambda n: (n, 3)), wide(4), wide(5), wide(6), _full((1, LANES)),
                  _full((CHUNK, CHUNK)), _full((C_HEADS, 3 * LANES, LANES))],
        out_specs=[sq(2 * CHUNK), sq(2 * CHUNK), sq(rows_t), sq(rows_t),
                   pl.BlockSpec((C_PAIRS, 2 * CHUNK, CHUNK), lambda n: (0, 0, n)),
                   pl.BlockSpec((C_PAIRS, CHUNK, 2 * CHUNK), lambda n: (0, n, 0))],
        out_shape=[jax.ShapeDtypeStruct((C_PAIRS, nblk, 2 * CHUNK, 2 * CHUNK), BF16)] * 2
        + [jax.ShapeDtypeStruct((C_PAIRS, nblk, rows_t, 2 * CHUNK), BF16)] * 2
        + [jax.ShapeDtypeStruct((C_PAIRS, 2 * CHUNK, seq), BF16), jax.ShapeDtypeStruct((C_PAIRS, seq, 2 * CHUNK), BF16)],
        scratch_shapes=[pltpu.VMEM((1, LANES), F32)],
        compiler_params=_cparams("arbitrary"),
    )(proj, proj, proj, proj, bf_row, tril, sel)


def _visible(shape, key0, query0):
    key = key0 + (lax.broadcasted_iota(jnp.int32, shape, 0) & (CHUNK - 1))
    return key <= query0 + lax.broadcasted_iota(jnp.int32, shape, 1)


def _rows_ab(a, b, n):
    return jnp.concatenate([jnp.broadcast_to(a, (C_HDIM, n)), jnp.broadcast_to(b, (C_HDIM, n))], axis=0)


def fox_fwd(qt, ka, vt):
    seq = qt.shape[2]
    nblk = seq // CHUNK
    bq = min(C_BQ, seq)
    grp = bq // CHUNK
    rows_t = CHUNK + C_TAIL

    def body(qt_ref, ka_ref, vt_ref, o_ref, lse_ref, acc_ref):
        p, i = pl.program_id(0), pl.program_id(1)
        qtile = qt_ref[0]
        r16 = lax.broadcasted_iota(jnp.int32, (C_TAIL, bq), 0)

        def group(j0, m, masked):
            ma, mb = m
            ss = []
            for g in range(grp):
                s = _dot(ka_ref[0, j0 + g], qtile)
                if masked:
                    s = jnp.where(_visible(s.shape, (j0 + g) * CHUNK, i * bq), s, -jnp.inf)
                ss.append(s)
            na, nb = ma, mb
            for s in ss:
                na = jnp.maximum(na, jnp.max(s[:CHUNK], axis=0, keepdims=True))
                nb = jnp.maximum(nb, jnp.max(s[CHUNK:], axis=0, keepdims=True))
            al_a, al_b = jnp.exp(ma - na), jnp.exp(mb - nb)
            pv = None
            for g, s in enumerate(ss):
                pt = jnp.concatenate([jnp.exp(s[:CHUNK] - na), jnp.exp(s[CHUNK:] - nb)], axis=0).astype(BF16)
                r = _dot(vt_ref[0, j0 + g], pt)
                pv = r if pv is None else pv + r
            tail = jnp.where(r16 == 2 * p, al_a, jnp.where(r16 == 2 * p + 1, al_b, 1.0))
            acc_ref[...] = acc_ref[...] * jnp.concatenate([_rows_ab(al_a, al_b, bq), tail], axis=0) + pv
            return na, nb

        acc_ref[...] = jnp.zeros_like(acc_ref)
        m = (jnp.full((1, bq), -jnp.inf, F32), jnp.full((1, bq), -jnp.inf, F32))
        m = lax.fori_loop(0, i, lambda t, m: group(t * grp, m, False), m)
        ma, mb = group(i * grp, m, True)
        tailv = acc_ref[CHUNK:rows_t, :]
        la = jnp.sum(jnp.where(r16 == 2 * p, tailv, 0.0), axis=0, keepdims=True)
        lb = jnp.sum(jnp.where(r16 == 2 * p + 1, tailv, 0.0), axis=0, keepdims=True)
        o_ref[...] = (acc_ref[0:CHUNK, :] * _rows_ab(1.0 / la, 1.0 / lb, bq)).T
        lse_ref[0, 0:1, :] = ma + jnp.log(la)
        lse_ref[0, 1:2, :] = mb + jnp.log(lb)

    return pl.pallas_call(
        body,
        name="fox_fwd",
        grid=(C_PAIRS, seq // bq),
        in_specs=[
            pl.BlockSpec((1, 2 * CHUNK, bq), lambda p, i: (p, 0, i)),
            pl.BlockSpec((1, nblk, 2 * CHUNK, 2 * CHUNK), lambda p, i: (p, 0, 0, 0)),
            pl.BlockSpec((1, nblk, rows_t, 2 * CHUNK), lambda p, i: (p, 0, 0, 0)),
        ],
        out_specs=[pl.BlockSpec((bq, LANES), lambda p, i: (i, p)), pl.BlockSpec((1, 2, bq), lambda p, i: (p, 0, i))],
        out_shape=[jax.ShapeDtypeStruct((seq, C_WIDTH), F32), jax.ShapeDtypeStruct((C_PAIRS, 2, seq), F32)],
        scratch_shapes=[pltpu.VMEM((rows_t, bq), F32)],
        compiler_params=_cparams("parallel", "arbitrary"),
    )(qt, ka, vt)


def fox_bwd_prep(dy, o, proj):
    seq = o.shape[0]
    ind = np.zeros((C_WIDTH, LANES), np.float32)
    for h in range(C_HEADS):
        ind[h * C_HDIM:(h + 1) * C_HDIM, h] = 1.0
    ind = jnp.asarray(ind, BF16)
    sel = _piece_selectors()
    sel = jnp.asarray(np.stack([sel[2 * p].T + sel[2 * p + 1].T for p in range(C_PAIRS)]), BF16)

    def body(dy_ref, o_ref, z_ref, ind_ref, sel_ref, do_ref, dz_ref, dot_ref):
        dy_c, o_v, z = dy_ref[...], o_ref[...], z_ref[...]
        sg = jax.nn.sigmoid(z)
        do = dy_c * (z * sg)
        do_ref[...] = do.astype(BF16)
        dz_ref[...] = (dy_c * o_v * (sg * (1.0 + z * (1.0 - sg)))).astype(BF16)
        prod = do * o_v
        hi = prod.astype(BF16)
        lo = (prod - hi.astype(F32)).astype(BF16)
        delta = _dot(hi, ind_ref[...]) + _dot(lo, ind_ref[...])
        d3 = jnp.concatenate(_split3(delta.T), axis=0)
        for p in range(C_PAIRS):
            tail = _dot(sel_ref[p], d3).astype(BF16)
            dot_ref[p] = jnp.concatenate([do[:, p * LANES:(p + 1) * LANES].T.astype(BF16), tail], axis=0)

    return pl.pallas_call(
        body,
        name="fox_bwd_prep",
        grid=(seq // CHUNK,),
        in_specs=[
            pl.BlockSpec((CHUNK, C_WIDTH), lambda i: (i, 1)),
            pl.BlockSpec((CHUNK, C_WIDTH), lambda i: (i, 0)),
            pl.BlockSpec((CHUNK, C_WIDTH), lambda i: (i, 7)),
            _full((C_WIDTH, LANES)), _full((C_PAIRS, LANES, 3 * LANES)),
        ],
        out_specs=[
            pl.BlockSpec((CHUNK, C_WIDTH), lambda i: (i, 0)),
            pl.BlockSpec((CHUNK, C_WIDTH), lambda i: (i, 0)),
            pl.BlockSpec((C_PAIRS, 2 * CHUNK, CHUNK), lambda i: (0, 0, i)),
        ],
        out_shape=[jax.ShapeDtypeStruct((seq, C_WIDTH), BF16)] * 2 + [jax.ShapeDtypeStruct((C_PAIRS, 2 * CHUNK, seq), BF16)],
        compiler_params=_cparams("parallel"),
    )(dy, o, proj, ind, sel)


def fox_bwd(ka, va, kt, qt, dot_t, qa, dob, lse):
    seq = qt.shape[2]
    nblk = seq // CHUNK
    bq = min(C_BQ, seq)
    nq = seq // bq
    kg = min(C_KG, nblk)
    ng = nblk // kg
    rows_t = CHUNK + C_TAIL

    def body(ka_ref, va_ref, kt_ref, qt_ref, dot_ref, qa_ref, do_ref, lse_ref,
             dq_ref, dk_ref, dv_ref, dck_ref, dcq_ref, dqt_acc, dv_acc, dka_acc):
        p, jg = pl.program_id(0), pl.program_id(1)

        @pl.when(jg == 0)
        def _():
            dqt_acc[...] = jnp.zeros_like(dqt_acc)

        dv_acc[...] = jnp.zeros_like(dv_acc)
        dka_acc[...] = jnp.zeros_like(dka_acc)

        def step(i, carry, masked):
            cols = pl.ds(pl.multiple_of(i * bq, bq), bq)
            qtile, dotile = qt_ref[0, :, cols], dot_ref[0, :, cols]
            do, qa_i = do_ref[cols, :], qa_ref[0, cols, :]
            lse2 = jnp.concatenate([jnp.broadcast_to(lse_ref[0, 0:1, cols], (CHUNK, bq)),
                                    jnp.broadcast_to(lse_ref[0, 1:2, cols], (CHUNK, bq))], axis=0)
            for kb in range(kg):
                pt = jnp.exp(_dot(ka_ref[0, kb], qtile) - lse2)
                if masked:
                    pt = jnp.where(_visible(pt.shape, (jg * kg + kb) * CHUNK, i * bq), pt, 0.0)
                ds = pt * _dot(va_ref[0, kb], dotile)
                ptb, dsb = pt.astype(BF16), ds.astype(BF16)
                dv_acc[kb] += _dot(ptb, do)
                dka_acc[kb] += _dot(dsb, qa_i)
                dqt_acc[:, cols] += _dot(kt_ref[0, kb], dsb)
            return carry

        i0 = (jg * kg * CHUNK) // bq
        step(i0, 0, True)
        lax.fori_loop(i0 + 1, nq, functools.partial(step, masked=False), 0)
        lane = lax.broadcasted_iota(jnp.int32, (CHUNK, LANES), 1)
        for kb in range(kg):
            rows = slice(kb * CHUNK, (kb + 1) * CHUNK)
            dk_ref[rows, :] = jnp.where(lane < C_HDIM, dka_acc[kb, 0:CHUNK, 0:LANES], dka_acc[kb, CHUNK:, 0:LANES]).astype(BF16)
            dv_ref[rows, :] = jnp.where(lane < C_HDIM, dv_acc[kb, 0:CHUNK, :], dv_acc[kb, CHUNK:, :]).astype(BF16)
            dck_ref[0, rows, :] = (jnp.where(lane == 2 * p, dka_acc[kb, 0:CHUNK, LANES:], 0.0)
                                   + jnp.where(lane == 2 * p + 1, dka_acc[kb, CHUNK:, LANES:], 0.0))

        @pl.when(jg == ng - 1)
        def _():
            for c in range(nq):
                dq_ref[c * bq:(c + 1) * bq, :] = (dqt_acc[0:CHUNK, c * bq:(c + 1) * bq].T * (C_HDIM ** -0.5)).astype(BF16)
            dcq_ref[0] = dqt_acc[CHUNK:rows_t, :]

    per_pair = lambda r, c: pl.BlockSpec((1, r, c), lambda p, j: (p, 0, 0))
    keys4 = lambda r: pl.BlockSpec((1, kg, r, 2 * CHUNK), lambda p, j: (p, j, 0, 0))
    return pl.pallas_call(
        body,
        name="fox_bwd",
        grid=(C_PAIRS, ng),
        in_specs=[keys4(2 * CHUNK), keys4(2 * CHUNK), keys4(rows_t), per_pair(2 * CHUNK, seq), per_pair(2 * CHUNK, seq),
                  per_pair(seq, 2 * CHUNK), pl.BlockSpec((seq, LANES), lambda p, j: (0, p)), per_pair(2, seq)],
        out_specs=[pl.BlockSpec((seq, LANES), lambda p, j: (0, p)),
                   pl.BlockSpec((kg * CHUNK, LANES), lambda p, j: (j, p)),
                   pl.BlockSpec((kg * CHUNK, LANES), lambda p, j: (j, p)),
                   pl.BlockSpec((1, kg * CHUNK, LANES), lambda p, j: (p, j, 0)),
                   per_pair(C_TAIL, seq)],
        out_shape=[jax.ShapeDtypeStruct((seq, C_WIDTH), BF16)] * 3
        + [jax.ShapeDtypeStruct((C_PAIRS, seq, LANES), F32), jax.ShapeDtypeStruct((C_PAIRS, C_TAIL, seq), F32)],
        scratch_shapes=[pltpu.VMEM((rows_t, seq), F32), pltpu.VMEM((kg, 2 * CHUNK, LANES), F32),
                        pltpu.VMEM((kg, 2 * CHUNK, 2 * CHUNK), F32)],
        compiler_params=_cparams("parallel", "arbitrary"),
    )(ka, va, kt, qt, dot_t, qa, dob, lse)


def fox_post(dcq, dck, proj, bf_row):
    seq = proj.shape[0]
    nc = seq // CHUNK
    triu = jnp.asarray(np.triu(np.ones((CHUNK, CHUNK), np.float32)), BF16)

    def body(dq_ref, dk_ref, fl_ref, bf_ref, u_ref, dfl_ref, dbf_ref, carry_ref):
        @pl.when(pl.program_id(0) == 0)
        def _():
            carry_ref[...] = jnp.zeros_like(carry_ref)
            dbf_ref[...] = jnp.zeros_like(dbf_ref)

        rows = (dq_ref[0] + dq_ref[1]) + (dq_ref[2] + dq_ref[3])
        dc = jnp.concatenate([rows, jnp.zeros((CHUNK - C_TAIL, CHUNK), F32)], axis=0).T
        dc = dc - ((dk_ref[0] + dk_ref[1]) + (dk_ref[2] + dk_ref[3]))
        g = _exact_times(u_ref[...], dc, 3) + carry_ref[...]
        carry_ref[...] += jnp.sum(dc, axis=0, keepdims=True)
        dfl = g * jax.nn.sigmoid(-(fl_ref[:, :LANES] + bf_ref[...]))
        dbf_ref[...] += jnp.sum(dfl, axis=0, keepdims=True)
        dfl_ref[...] = jnp.concatenate([dfl, jnp.zeros_like(dfl)], axis=1).astype(BF16)

    rev = lambda n: nc - 1 - n
    return pl.pallas_call(
        body,
        name="fox_post",
        grid=(nc,),
        in_specs=[
            pl.BlockSpec((C_PAIRS, C_TAIL, CHUNK), lambda n: (0, 0, rev(n))),
            pl.BlockSpec((C_PAIRS, CHUNK, LANES), lambda n: (0, rev(n), 0)),
            pl.BlockSpec((CHUNK, 256), lambda n: (rev(n), 3)),
            _full((1, LANES)), _full((CHUNK, CHUNK)),
        ],
        out_specs=[pl.BlockSpec((CHUNK, 256), lambda n: (rev(n), 0)), _full((1, LANES))],
        out_shape=[jax.ShapeDtypeStruct((seq, 256), BF16), jax.ShapeDtypeStruct((1, LANES), F32)],
        scratch_shapes=[pltpu.VMEM((1, LANES), F32)],
        compiler_params=_cparams("arbitrary"),
    )(dcq, dck, proj, bf_row, triu)


N_DEV = 8
MESH = pl.DeviceIdType.MESH
_ANY = pl.BlockSpec(memory_space=pl.ANY)


def _mesh_pos():
    return lax.axis_index("x"), lax.axis_index("y"), lax.axis_index("c")


def _dev_index(px, py, pc):
    return 4 * px + 2 * py + pc


def _row_pieces(ref, rows):
    return [ref.at[idx + (pl.ds(r, rows),)] for idx in np.ndindex(*ref.shape[:-2]) for r in range(0, ref.shape[-2], rows)]


class _Transfer:
    def __init__(self, src, dst, rows, send_sem, recv_sem, to):
        self.src, self.dst, self.rows, self.sems, self.to = src, dst, rows, (send_sem, recv_sem), to

    def _copy(self, src, dst):
        return pltpu.make_async_remote_copy(src_ref=src, dst_ref=dst, send_sem=self.sems[0], recv_sem=self.sems[1],
                                            device_id=self.to, device_id_type=MESH)

    def start(self):
        for s, d in zip(_row_pieces(self.src, self.rows), _row_pieces(self.dst, self.rows), strict=True):
            self._copy(s, d).start()

    def wait_send(self):
        self._copy(self.src, self.dst).wait_send()

    def wait_recv(self):
        self._copy(self.src, self.dst).wait_recv()


def allgather_weights(wi, wo):
    piece_rows = (128, 64)

    def body(wi_ref, wo_ref, wi_all, wo_all, send_sems, recv_sems, local_sems):
        x, y, c = _mesh_pos()
        me, sibling = (x, y, c), (x, y, 1 - c)
        chips = [(1 - x, y), (x, 1 - y), (1 - x, 1 - y)]
        arrays = ((wi_ref, wi_all), (wo_ref, wo_all))

        def copy(a, k, block, to, own=False):
            src, out = arrays[a]
            slot = out.at[_dev_index(*block)]
            return _Transfer(src if own else slot, slot, piece_rows[a], send_sems.at[a, k], recv_sems.at[a, k], to)

        both = range(len(arrays))
        mine = [pltpu.make_async_copy(arrays[a][0], arrays[a][1].at[_dev_index(*me)], local_sems.at[a]) for a in both]
        for cp in mine:
            cp.start()
        first = [copy(a, 1 + j, me, (*chip, c), own=True) for j, chip in enumerate(chips) for a in both]
        first += [copy(a, 0, me, sibling, own=True) for a in both]
        for cp in first:
            cp.start()
        passed = [copy(a, 4 + j, (*chip, c), sibling) for j, chip in enumerate(chips) for a in both]
        for j, chip in enumerate(chips):
            for a in both:
                copy(a, 1 + j, (*chip, c), me).wait_recv()
            for a in both:
                passed[2 * j + a].start()
        for a in both:
            copy(a, 0, sibling, me).wait_recv()
        for j, chip in enumerate(chips):
            for a in both:
                copy(a, 4 + j, (*chip, 1 - c), me).wait_recv()
        for cp in first + passed:
            cp.wait_send()
        for cp in mine:
            cp.wait()

    return pl.pallas_call(
        body,
        name="allgather_weights",
        in_specs=[_ANY, _ANY],
        out_specs=[_ANY, _ANY],
        out_shape=[jax.ShapeDtypeStruct((N_DEV,) + wi.shape, wi.dtype), jax.ShapeDtypeStruct((N_DEV,) + wo.shape, wo.dtype)],
        scratch_shapes=[pltpu.SemaphoreType.DMA((2, 7)), pltpu.SemaphoreType.DMA((2, 7)), pltpu.SemaphoreType.DMA((2,))],
    )(wi, wo)


N_CHIP = 4


def pair_exchange(gwi, gwo, gsm):
    piece_rows = (256, gwo.shape[-2], gsm.shape[0] // 2)

    def body(gwi_ref, gwo_ref, gsm_ref, qwi, qwo, qsm, send_sems, recv_sems):
        x, y, c = _mesh_pos()
        srcs, outs = (gwi_ref.at[1 - c], gwo_ref.at[1 - c], gsm_ref), (qwi, qwo, qsm)
        copies = [_Transfer(srcs[a], outs[a], piece_rows[a], send_sems.at[a], recv_sems.at[a], (x, y, 1 - c))
                  for a in range(3)]
        for cp in copies:
            cp.start()
        for cp in copies:
            cp.wait_recv()
        for cp in copies:
            cp.wait_send()

    return pl.pallas_call(
        body,
        name="pair_exchange",
        in_specs=[_ANY, _ANY, _ANY],
        out_specs=[_ANY, _ANY, _ANY],
        out_shape=[jax.ShapeDtypeStruct(gwi.shape[1:], gwi.dtype), jax.ShapeDtypeStruct(gwo.shape[1:], gwo.dtype),
                   jax.ShapeDtypeStruct(gsm.shape, gsm.dtype)],
        scratch_shapes=[pltpu.SemaphoreType.DMA((3,)), pltpu.SemaphoreType.DMA((3,))],
    )(gwi, gwo, gsm)


def _slab_spec(lead, rows, n_c, pick=None):
    if pick is None:
        return pl.BlockSpec((1, rows, n_c), lambda i, r, *_: (i, r, 0))
    return pl.BlockSpec((1, 1, rows, n_c), lambda i, r, s: (pick(s), i, r, 0))


def pair_sum(own, other, dtype, rows, name, core=None):
    n, n_r, n_c = other.shape

    def body(*refs):
        a_ref, b_ref, o_ref = refs[-3:]
        o_ref[0] = (a_ref[...].reshape(rows, n_c) + b_ref[0]).astype(dtype)

    grid_spec = pltpu.PrefetchScalarGridSpec(
        num_scalar_prefetch=0 if core is None else 1,
        grid=(n, n_r // rows),
        in_specs=[_slab_spec(1, rows, n_c, None if core is None else (lambda s: s[0])), _slab_spec(1, rows, n_c)],
        out_specs=_slab_spec(1, rows, n_c),
    )
    args = (own, other) if core is None else (core, own, other)
    return pl.pallas_call(
        body,
        name=name,
        grid_spec=grid_spec,
        out_shape=jax.ShapeDtypeStruct((n, n_r, n_c), dtype),
        compiler_params=_cparams("parallel", "parallel"),
    )(*args)


def chip_exchange(swi, swo, ssm):
    piece_rows = (128, swo.shape[-2] // 2, ssm.shape[0] // 2)

    def body(swi_ref, swo_ref, ssm_ref, rwi, rwo, rsm, send_sems, recv_sems, local_sem):
        x, y, c = _mesh_pos()
        chip = 2 * x + y
        srcs, outs = (swi_ref, swo_ref, ssm_ref), (rwi, rwo, rsm)
        three = range(3)
        local = pltpu.make_async_copy(ssm_ref, rsm.at[chip], local_sem)
        local.start()

        def peer_of(k):
            return x ^ ((k >> 1) & 1), y ^ (k & 1)

        def copy(a, k, sending):
            px, py = peer_of(k)
            src = srcs[a] if a == 2 else srcs[a].at[2 * px + py]
            slot = k - 1 if a < 2 else (chip if sending else 2 * px + py)
            return _Transfer(src, outs[a].at[slot], piece_rows[a], send_sems.at[a, k - 1], recv_sems.at[a, k - 1], (px, py, c))

        sends = [copy(a, k, True) for k in range(1, N_CHIP) for a in three]
        for cp in sends:
            cp.start()
        for k in range(1, N_CHIP):
            for a in three:
                copy(a, k, False).wait_recv()
        for cp in sends:
            cp.wait_send()
        local.wait()

    return pl.pallas_call(
        body,
        name="chip_exchange",
        in_specs=[_ANY, _ANY, _ANY],
        out_specs=[_ANY, _ANY, _ANY],
        out_shape=[jax.ShapeDtypeStruct((N_CHIP - 1,) + swi.shape[1:], swi.dtype),
                   jax.ShapeDtypeStruct((N_CHIP - 1,) + swo.shape[1:], swo.dtype),
                   jax.ShapeDtypeStruct((N_CHIP,) + ssm.shape, ssm.dtype)],
        scratch_shapes=[pltpu.SemaphoreType.DMA((3, 3)), pltpu.SemaphoreType.DMA((3, 3)), pltpu.SemaphoreType.DMA],
    )(swi, swo, ssm)


ADAM_LR = 0.001
ADAM_B1 = 0.9
ADAM_B2 = 0.999
ADAM_EPS = 1e-08
ADAM_WD = 0.01
ADAM_STEP = 10


def adam_reduce(parts, w, m, v, rows, name, own=None, chip=None):
    n_l, n_r, n_c = w.shape
    n_parts = parts.shape[0]

    def body(*refs):
        p_ref, w_ref, m_ref, v_ref, g_ref, d_ref, m2_ref, v2_ref = refs[-8:]
        g = p_ref[0, 0].astype(F32)
        if own is not None:
            g = refs[-9][...].reshape(rows, n_c).astype(F32) + g
        for d in range(1, n_parts):
            g = g + p_ref[d, 0].astype(F32)
        m2 = ADAM_B1 * m_ref[0] + (1.0 - ADAM_B1) * g
        v2 = ADAM_B2 * v_ref[0] + (1.0 - ADAM_B2) * (g * g)
        m_hat = m2 / (1.0 - ADAM_B1 ** ADAM_STEP)
        v_hat = v2 / (1.0 - ADAM_B2 ** ADAM_STEP)
        g_ref[0] = g
        d_ref[0] = -ADAM_LR * (m_hat / (jnp.sqrt(v_hat) + ADAM_EPS) + ADAM_WD * w_ref[0])
        m2_ref[0] = m2
        v2_ref[0] = v2

    blk = lambda: pl.BlockSpec((1, rows, n_c), lambda l, r, *_: (l, r, 0))
    in_specs = [pl.BlockSpec((n_parts, 1, rows, n_c), lambda l, r, *_: (0, l, r, 0)), blk(), blk(), blk()]
    args = (parts, w, m, v)
    if own is not None:
        in_specs = [pl.BlockSpec((1, 1, rows, n_c), lambda l, r, s: (s[0], l, r, 0))] + in_specs
        args = (chip, own) + args
    grid_spec = pltpu.PrefetchScalarGridSpec(
        num_scalar_prefetch=0 if own is None else 1, grid=(n_l, n_r // rows), in_specs=in_specs,
        out_specs=[blk(), blk(), blk(), blk()])
    return pl.pallas_call(
        body,
        name=name,
        grid_spec=grid_spec,
        out_shape=[jax.ShapeDtypeStruct(w.shape, F32)] * 4,
        compiler_params=_cparams("parallel", "parallel"),
    )(*args)


_SMALL = (("norm_g", (2, 1024)), ("gmlp_ln_g", (2, 4, 64)), ("gmlp_ln_b", (2, 4, 64)), ("gmlp_w_s", (2, 4, 128, 128)),
          ("gmlp_b_s", (2, 4, 128)), ("hgrn_lb", (2, 256)), ("hgrn_onorm_g", (2, 64)), ("fox_b_f", (2, 8)),
          ("final_norm_g", (1024,)), ("loss", ()))


def _padded(n):
    return -(-n // LANES) * LANES


_SMALL_ROWS = -(-sum(_padded(int(np.prod(s))) for _, s in _SMALL) // LANES // 8) * 8


def _pack_small(vals):
    flat = []
    for (name, shape), a in zip(_SMALL, vals, strict=True):
        n = int(np.prod(shape))
        flat.append(jnp.pad(a.reshape(n).astype(F32), (0, _padded(n) - n)))
    flat = jnp.concatenate(flat)
    return jnp.pad(flat, (0, _SMALL_ROWS * LANES - flat.shape[0])).reshape(_SMALL_ROWS, LANES)


def _unpack_small(slab):
    flat, out, at = slab.reshape(-1), {}, 0
    for name, shape in _SMALL:
        n = int(np.prod(shape))
        out[name] = flat[at:at + n].reshape(shape)
        at += _padded(n)
    return out


def kernel(x, norm_g, w_in, w_out, gmlp_ln_g, gmlp_ln_b, gmlp_w_s, gmlp_b_s, hgrn_lb, hgrn_onorm_g, fox_b_f, final_norm_g, loss_target, m_norm_g, m_w_in, m_w_out, m_gmlp_ln_g, m_gmlp_ln_b, m_gmlp_w_s, m_gmlp_b_s, m_hgrn_lb, m_hgrn_onorm_g, m_fox_b_f, m_final_norm_g, v_norm_g, v_w_in, v_w_out, v_gmlp_ln_g, v_gmlp_ln_b, v_gmlp_w_s, v_gmlp_b_s, v_hgrn_lb, v_hgrn_onorm_g, v_fox_b_f, v_final_norm_g):
    depth = w_in.shape[0]
    seq = x.shape[1]
    assert w_in.shape[2] * N_DEV == N_IN
    xs, tgt = x[0], loss_target[0]

    wi_all, wo_all = allgather_weights(w_in.astype(BF16), w_out.astype(BF16))
    wi_int = assemble_w_in(wi_all)

    ln_g = gmlp_ln_g.reshape(depth, 1, A_WIDTH)
    ln_b = gmlp_ln_b.reshape(depth, 1, A_WIDTH)
    bs_t = jnp.pad(jnp.transpose(gmlp_b_s, (0, 2, 1)), ((0, 0), (0, 0), (0, LANES - A_GROUPS)))
    lb0, lb1 = hgrn_lb[0:1], hgrn_lb[1:2]
    onorm = jnp.tile(hgrn_onorm_g, (1, B_HEADS)).reshape(depth, 1, B_WIDTH)
    bf_row = jnp.pad(fox_b_f, ((0, 0), (0, LANES - C_HEADS))).reshape(depth, 1, LANES)

    saved = []
    xc = xs
    for l in range(depth):
        proj, h = inproj(xc, norm_g[l:l + 1], wi_int, l)
        ya = gmlp_fwd(proj, ln_g[l], ln_b[l], gmlp_w_s[l], bs_t[l])
        yb, states = hgrn_fwd(proj, lb0, lb1, onorm[l], l)
        ka, va, vt, kt, qt, qa = fox_prep(proj, bf_row[l])
        o, lse = fox_fwd(qt, ka, vt)
        xn, yfull = outproj(xc, ya, yb, o, proj, wo_all, l)
        saved.append((xc, proj, h, states, ka, va, kt, qt, qa, o, lse, yfull))
        xc = xn

    dx, d_final_g, loss_tile = final_loss(xc, final_norm_g[None], tgt)

    g_norm = [None] * depth
    g_ln_g, g_ln_b, g_ws, g_bs, g_on, g_bf = ([None] * depth for _ in range(6))
    g_lb0, g_lb1 = jnp.zeros_like(lb0), jnp.zeros_like(lb1)
    dwi = gwo = None
    for l in reversed(range(depth)):
        x_in, proj, h, states, ka, va, kt, qt, qa, o, lse, yfull = saved[l]
        dy, gwo = outproj_bwd(dx, yfull, wo_all, l, gwo)
        d_a, g_ln_g[l], g_ln_b[l], g_ws[l], dbs_t = gmlp_bwd(proj, dy, ln_g[l], ln_b[l], gmlp_w_s[l], bs_t[l])
        g_bs[l] = dbs_t[:, :A_GROUPS].T
        d_b, d0, d1, don = hgrn_bwd(proj, states, dy, lb0, lb1, onorm[l], l)
        g_lb0, g_lb1 = g_lb0 + d0, g_lb1 + d1
        g_on[l] = don.reshape(B_HEADS, B_KDIM).sum(0)
        dob, d_z, dot_t = fox_bwd_prep(dy, o, proj)
        d_q, d_k, d_v, dck, dcq = fox_bwd(ka, va, kt, qt, dot_t, qa, dob, lse)
        d_fl, dbf = fox_post(dcq, dck, proj, bf_row[l])
        g_bf[l] = dbf[0, :C_HEADS]
        dproj = jnp.concatenate([d_a, d_fl, d_b, d_q, d_k, d_v, d_z], axis=1)
        dx, g_norm[l] = inproj_bwd_x(dproj, wi_int, x_in, norm_g[l:l + 1], dx, l)
        dwi = inproj_bwd_w(h, dproj, l, depth, dwi)

    gwi = split_w_in_grad(dwi, w_in.shape[2])
    gsm = _pack_small([
        jnp.concatenate(g_norm), jnp.stack(g_ln_g), jnp.stack(g_ln_b), jnp.stack(g_ws), jnp.stack(g_bs),
        jnp.concatenate([g_lb0, g_lb1]), jnp.stack(g_on), jnp.stack(g_bf), d_final_g, loss_tile[0, 0]])
    core = lax.axis_index("c").astype(jnp.int32).reshape(1)
    chip = (2 * lax.axis_index("x") + lax.axis_index("y")).astype(jnp.int32).reshape(1)
    qwi, qwo, qsm = pair_exchange(gwi, gwo, gsm)
    flat = lambda a: a.reshape(a.shape[:-4] + (N_CHIP * depth,) + a.shape[-2:])
    swi = pair_sum(flat(gwi), flat(qwi), BF16, 256, "pair_sum_w_in", core).reshape(qwi.shape)
    swo = pair_sum(flat(gwo), flat(qwo), BF16, gwo.shape[3], "pair_sum_w_out", core).reshape(qwo.shape)
    ssm = pair_sum(gsm[None], qsm[None], F32, _SMALL_ROWS, "pair_sum_small")[0]
    rwi, rwo, rsm = chip_exchange(swi, swo, ssm)

    small_w = (norm_g, gmlp_ln_g, gmlp_ln_b, gmlp_w_s, gmlp_b_s, hgrn_lb, hgrn_onorm_g, fox_b_f, final_norm_g)
    small_m = (m_norm_g, m_gmlp_ln_g, m_gmlp_ln_b, m_gmlp_w_s, m_gmlp_b_s, m_hgrn_lb, m_hgrn_onorm_g, m_fox_b_f, m_final_norm_g)
    small_v = (v_norm_g, v_gmlp_ln_g, v_gmlp_ln_b, v_gmlp_w_s, v_gmlp_b_s, v_hgrn_lb, v_hgrn_onorm_g, v_fox_b_f, v_final_norm_g)
    zero = jnp.zeros((), F32)
    res_wi = adam_reduce(rwi, w_in, m_w_in, v_w_in, 256, "adam_w_in", own=swi, chip=chip)
    res_wo = adam_reduce(rwo, w_out, m_w_out, v_w_out, w_out.shape[1], "adam_w_out", own=swo, chip=chip)
    res_sm = adam_reduce(rsm[:, None], _pack_small(small_w + (zero,))[None], _pack_small(small_m + (zero,))[None],
                         _pack_small(small_v + (zero,))[None], _SMALL_ROWS, "adam_small")
    res_sm = [_unpack_small(r[0]) for r in res_sm]

    def group(i):
        s = res_sm[i]
        return [s["norm_g"], res_wi[i], res_wo[i], s["gmlp_ln_g"], s["gmlp_ln_b"], s["gmlp_w_s"], s["gmlp_b_s"],
                s["hgrn_lb"], s["hgrn_onorm_g"], s["fox_b_f"], s["final_norm_g"]]

    return (res_sm[0]["loss"], dx[None], *group(0), *group(1), *group(2), *group(3))
```

```python
import functools

import jax
import jax.numpy as jnp
import numpy as np
from jax import lax
from jax.experimental import pallas as pl
from jax.experimental.pallas import tpu as pltpu

F32 = jnp.float32
BF16 = jnp.bfloat16

NORM_EPS = 1e-6
F_FLOOR = 1e-30
CHUNK = 128
LANES = 128
VMEM_LIMIT = 56 * 1024 * 1024


def _cparams(*sem):
    return pltpu.CompilerParams(dimension_semantics=sem, vmem_limit_bytes=VMEM_LIMIT)


def _dot(a, b, dims=(((1,), (0,)), ((), ())), precision=None):
    return lax.dot_general(a, b, dims, precision=precision, preferred_element_type=F32)


_NT = (((1,), (1,)), ((), ()))
_TN = (((0,), (0,)), ((), ()))


def _bf16_pieces(x, n):
    out, r = [], x
    for i in range(n):
        out.append(r.astype(BF16))
        if i + 1 < n:
            r = r - out[-1].astype(F32)
    return out


@functools.partial(jax.custom_vjp, nondiff_argnums=(2,))
def _times_exact(x, e, n):
    return functools.reduce(jnp.add, [_dot(p, e) for p in _bf16_pieces(x, n)])


def _times_exact_fwd(x, e, n):
    return _times_exact(x, e, n), e


def _times_exact_bwd(n, e, g):
    dx = functools.reduce(jnp.add, [lax.dot_general(p, e, _NT, preferred_element_type=F32) for p in _bf16_pieces(g, n)])
    return dx, jnp.zeros_like(e)


_times_exact.defvjp(_times_exact_fwd, _times_exact_bwd)


@functools.partial(jax.custom_vjp, nondiff_argnums=(2,))
def _exact_times(e, x, n):
    return functools.reduce(jnp.add, [_dot(e, p) for p in _bf16_pieces(x, n)])


def _exact_times_fwd(e, x, n):
    return _exact_times(e, x, n), e


def _exact_times_bwd(n, e, g):
    dx = functools.reduce(jnp.add, [lax.dot_general(e, p, _TN, preferred_element_type=F32) for p in _bf16_pieces(g, n)])
    return jnp.zeros_like(e), dx


_exact_times.defvjp(_exact_times_fwd, _exact_times_bwd)


def _group_mean_matrix(width, group):
    idx = np.arange(width) // group
    return jnp.asarray((idx[:, None] == idx[None, :]).astype(np.float32) / group, BF16)


def _group_ones_matrix(width, group):
    idx = np.arange(width) // group
    return jnp.asarray((idx[:, None] == idx[None, :]).astype(np.float32), BF16)


A_WIDTH = 256
A_GROUPS = 4
A_GDIM = 64


A_ROWS = 512


def _gmlp_chunk(x3, ln_g, ln_b, w_s, bs_t, mean_m, gind):
    n = x3.shape[0] // CHUNK
    u = jax.nn.gelu(x3[:, :A_WIDTH])
    v = jax.nn.gelu(x3[:, A_WIDTH:2 * A_WIDTH])
    z = x3[:, 2 * A_WIDTH:]
    mu = _times_exact(v, mean_m, 2)
    d = v - mu
    var = _times_exact(d * d, mean_m, 2)
    vn = d * lax.rsqrt(var + NORM_EPS) * ln_g + ln_b
    vnb = vn.astype(BF16)
    wide = jnp.concatenate([vnb[i * CHUNK:(i + 1) * CHUNK] for i in range(n)], axis=1)
    row = lax.broadcasted_iota(jnp.int32, (CHUNK, CHUNK), 0)
    col = lax.broadcasted_iota(jnp.int32, (CHUNK, CHUNK), 1)
    causal = row >= col
    lane_g = lax.shift_right_logical(lax.broadcasted_iota(jnp.int32, (CHUNK, n * A_WIDTH), 1), 6) & (A_GROUPS - 1)
    bias = _times_exact(bs_t, gind, 3)
    mixed = jnp.concatenate([bias] * n, axis=1)
    for g in range(A_GROUPS):
        wc = jnp.where(causal, w_s[g], 0.0).astype(BF16)
        mixed = mixed + jnp.where(lane_g == g, _dot(wc, wide), 0.0)
    mixed = jnp.concatenate([mixed[:, i * A_WIDTH:(i + 1) * A_WIDTH] for i in range(n)], axis=0)
    return u * mixed * jax.nn.silu(z)


def _gmlp_consts():
    gind = np.zeros((LANES, A_WIDTH), np.float32)
    for g in range(A_GROUPS):
        gind[g, g * A_GDIM:(g + 1) * A_GDIM] = 1.0
    return _group_mean_matrix(A_WIDTH, A_GDIM), jnp.asarray(gind, BF16)


def _full(shape):
    return pl.BlockSpec(shape, lambda *_: (0,) * len(shape))


def gmlp_fwd(proj, ln_g, ln_b, w_s, bs_t):
    seq = proj.shape[0]
    rows = min(A_ROWS, seq)
    mean_m, gind = _gmlp_consts()

    def body(x_ref, g_ref, b_ref, w_ref, bs_ref, m_ref, gi_ref, y_ref):
        y = _gmlp_chunk(x_ref[...], g_ref[...], b_ref[...], w_ref[...], bs_ref[...], m_ref[...], gi_ref[...])
        y_ref[...] = y.astype(BF16)

    return pl.pallas_call(
        body,
        name="gmlp_fwd",
        grid=(seq // rows,),
        in_specs=[
            pl.BlockSpec((rows, 3 * A_WIDTH), lambda n: (n, 0)),
            _full((1, A_WIDTH)), _full((1, A_WIDTH)), _full((A_GROUPS, CHUNK, CHUNK)), _full((CHUNK, LANES)),
            _full((A_WIDTH, A_WIDTH)), _full((LANES, A_WIDTH)),
        ],
        out_specs=pl.BlockSpec((rows, A_WIDTH), lambda n: (n, 0)),
        out_shape=jax.ShapeDtypeStruct((seq, A_WIDTH), BF16),
        compiler_params=_cparams("parallel"),
    )(proj, ln_g, ln_b, w_s, bs_t, mean_m, gind)


def gmlp_bwd(proj, dy, ln_g, ln_b, w_s, bs_t):
    seq = proj.shape[0]
    rows = min(A_ROWS, seq)
    mean_m, gind = _gmlp_consts()

    def body(x_ref, dy_ref, g_ref, b_ref, w_ref, bs_ref, m_ref, gi_ref, dx_ref, dg_ref, db_ref, dw_ref, dbs_ref):
        fn = functools.partial(_gmlp_chunk, mean_m=m_ref[...], gind=gi_ref[...])
        _, vjp = jax.vjp(fn, x_ref[...], g_ref[...], b_ref[...], w_ref[...], bs_ref[...])
        dx, dg, db, dw, dbs = vjp(dy_ref[...])
        dx_ref[...] = dx.astype(BF16)

        @pl.when(pl.program_id(0) == 0)
        def _():
            dg_ref[...] = jnp.zeros_like(dg_ref)
            db_ref[...] = jnp.zeros_like(db_ref)
            dw_ref[...] = jnp.zeros_like(dw_ref)
            dbs_ref[...] = jnp.zeros_like(dbs_ref)

        dg_ref[...] += dg
        db_ref[...] += db
        dw_ref[...] += dw
        dbs_ref[...] += dbs

    return pl.pallas_call(
        body,
        name="gmlp_bwd",
        grid=(seq // rows,),
        in_specs=[
            pl.BlockSpec((rows, 3 * A_WIDTH), lambda n: (n, 0)),
            pl.BlockSpec((rows, A_WIDTH), lambda n: (n, 0)),
            _full((1, A_WIDTH)), _full((1, A_WIDTH)), _full((A_GROUPS, CHUNK, CHUNK)), _full((CHUNK, LANES)),
            _full((A_WIDTH, A_WIDTH)), _full((LANES, A_WIDTH)),
        ],
        out_specs=[
            pl.BlockSpec((rows, 3 * A_WIDTH), lambda n: (n, 0)),
            _full((1, A_WIDTH)), _full((1, A_WIDTH)), _full((A_GROUPS, CHUNK, CHUNK)), _full((CHUNK, LANES)),
        ],
        out_shape=[
            jax.ShapeDtypeStruct((seq, 3 * A_WIDTH), BF16),
            jax.ShapeDtypeStruct((1, A_WIDTH), F32), jax.ShapeDtypeStruct((1, A_WIDTH), F32),
            jax.ShapeDtypeStruct((A_GROUPS, CHUNK, CHUNK), F32), jax.ShapeDtypeStruct((CHUNK, LANES), F32),
        ],
        compiler_params=_cparams("arbitrary"),
    )(proj, dy, ln_g, ln_b, w_s, bs_t, mean_m, gind)


B_WIDTH = 256
B_HEADS = 4
B_KDIM = 64
B_LEVELS = (64, 32, 16, 8, 4, 2, 1)


def _hgrn_consts():
    t = np.arange(CHUNK)
    u = t[None, :]
    mats = [np.tril(np.ones((CHUNK, CHUNK), np.float32))]
    for m in B_LEVELS:
        p = (t // (2 * m)) * (2 * m) + m - 1
        right = (t % (2 * m)) >= m
        sel = np.where(right[:, None], (u > p[:, None]) & (u <= t[:, None]), (u > t[:, None]) & (u <= p[:, None]))
        mats.append(sel.astype(np.float32))
    return jnp.asarray(np.concatenate(mats, 0), BF16), _group_ones_matrix(B_WIDTH, B_KDIM)


def _hgrn_lower_bound(lb0, lb1, layer):
    mx = jnp.maximum(lb0, lb1)
    e0 = jnp.exp(lb0 - mx)
    e1 = jnp.exp(lb1 - mx)
    p0 = e0 / (e0 + e1)
    p1 = e1 / (e0 + e1)
    cs = p0 if layer == 0 else p0 + p1
    return jnp.clip(cs - p0, 0.0, 1.0 - 1e-6)


def _hgrn_chunk(x4, st, lb0, lb1, onorm, layer, tstack, ones_bd):
    q_raw, fl, v, zg = (x4[:, i * B_WIDTH:(i + 1) * B_WIDTH] for i in range(4))
    lb = _hgrn_lower_bound(lb0, lb1, layer)
    q = jax.nn.silu(q_raw) * (B_KDIM ** -0.5)
    f = lb + (1.0 - lb) * jax.nn.sigmoid(fl)
    logf = jnp.log(jnp.maximum(f, F_FLOOR))
    k = (1.0 - lb) * jax.nn.sigmoid(-fl)
    dall = _exact_times(tstack, logf, 3)
    b = dall[:CHUNK]
    b_last = jnp.sum(logf, axis=0, keepdims=True)
    vb = v.astype(BF16)

    lane_h = lax.shift_right_logical(lax.broadcasted_iota(jnp.int32, (CHUNK, B_WIDTH), 1), 6)
    row = lax.broadcasted_iota(jnp.int32, (CHUNK, B_WIDTH), 0)
    srow = lax.broadcasted_iota(jnp.int32, (B_HEADS * CHUNK, CHUNK), 0) & (CHUNK - 1)
    scol = lax.broadcasted_iota(jnp.int32, (B_HEADS * CHUNK, CHUNK), 1)

    def heads_on_rows(a):
        return jnp.concatenate([jnp.where(lane_h == h, a, 0.0) for h in range(B_HEADS)], axis=0)

    def heads_from_rows(r):
        out = jnp.where(lane_h == 0, r[:CHUNK], 0.0)
        for h in range(1, B_HEADS):
            out = out + jnp.where(lane_h == h, r[h * CHUNK:(h + 1) * CHUNK], 0.0)
        return out

    o = lax.dot_general((q * jnp.exp(b)).astype(BF16), st.astype(BF16), _NT, preferred_element_type=F32)
    scores = jnp.zeros((B_HEADS * CHUNK, CHUNK), F32)
    for li, m in enumerate(B_LEVELS):
        e = jnp.exp(dall[(li + 1) * CHUNK:(li + 2) * CHUNK])
        right = (row & (2 * m - 1)) >= m
        qt = jnp.where(right, q * e, 0.0)
        kt = jnp.where(right, 0.0, k * e)
        sc = lax.dot_general(heads_on_rows(qt).astype(BF16), kt.astype(BF16), _NT, preferred_element_type=F32)
        sh = int(np.log2(2 * m))
        same = lax.shift_right_logical(srow, sh) == lax.shift_right_logical(scol, sh)
        scores = scores + jnp.where(same, sc, 0.0)
    o = o + heads_from_rows(_dot(scores.astype(BF16), vb))
    o = o + _times_exact(q * k, ones_bd, 2) * v

    kv = lax.dot_general(vb, (k * jnp.exp(b_last - b)).astype(BF16), _TN, preferred_element_type=F32)
    st_new = st * jnp.exp(b_last) + jnp.where(ones_bd > 0.5, kv, 0.0)

    ms = _times_exact(o * o, ones_bd, 2) * (1.0 / B_KDIM)
    y = o * lax.rsqrt(ms + NORM_EPS) * onorm * jax.nn.silu(zg)
    return y, st_new


B_ROWS = 256


def _hgrn_rows(x4, st, lb0, lb1, onorm, layer, tstack, ones_bd):
    ys = []
    for i in range(x4.shape[0] // CHUNK):
        y, st = _hgrn_chunk(x4[i * CHUNK:(i + 1) * CHUNK], st, lb0, lb1, onorm, layer, tstack, ones_bd)
        ys.append(y)
    return jnp.concatenate(ys, axis=0), st


def hgrn_fwd(proj, lb0, lb1, onorm, layer):
    seq = proj.shape[0]
    rows = min(B_ROWS, seq)
    nc = seq // rows
    tstack, ones_bd = _hgrn_consts()

    def body(x_ref, lb0_ref, lb1_ref, on_ref, t_ref, e_ref, y_ref, st_out_ref, st_ref):
        @pl.when(pl.program_id(0) == 0)
        def _():
            st_ref[...] = jnp.zeros_like(st_ref)

        st = st_ref[...]
        st_out_ref[0] = st
        y, st_new = _hgrn_rows(x_ref[...], st, lb0_ref[...], lb1_ref[...], on_ref[...], layer, t_ref[...], e_ref[...])
        y_ref[...] = y.astype(BF16)
        st_ref[...] = st_new

    return pl.pallas_call(
        body,
        name=f"hgrn_fwd_{layer}",
        grid=(nc,),
        in_specs=[
            pl.BlockSpec((rows, 4 * B_WIDTH), lambda n: (n, 1)),
            _full((1, B_WIDTH)), _full((1, B_WIDTH)), _full((1, B_WIDTH)),
            _full(((len(B_LEVELS) + 1) * CHUNK, CHUNK)), _full((B_WIDTH, B_WIDTH)),
        ],
        out_specs=[
            pl.BlockSpec((rows, B_WIDTH), lambda n: (n, 0)),
            pl.BlockSpec((1, B_WIDTH, B_WIDTH), lambda n: (n, 0, 0)),
        ],
        out_shape=[jax.ShapeDtypeStruct((seq, B_WIDTH), BF16), jax.ShapeDtypeStruct((nc, B_WIDTH, B_WIDTH), F32)],
        scratch_shapes=[pltpu.VMEM((B_WIDTH, B_WIDTH), F32)],
        compiler_params=_cparams("arbitrary"),
    )(proj, lb0, lb1, onorm, tstack, ones_bd)


def hgrn_bwd(proj, states, dy, lb0, lb1, onorm, layer):
    seq = proj.shape[0]
    rows = min(B_ROWS, seq)
    nc = seq // rows
    tstack, ones_bd = _hgrn_consts()

    def body(x_ref, st_in_ref, dy_ref, lb0_ref, lb1_ref, on_ref, t_ref, e_ref, dx_ref, d0_ref, d1_ref, don_ref, dst_ref):
        @pl.when(pl.program_id(0) == 0)
        def _():
            dst_ref[...] = jnp.zeros_like(dst_ref)
            d0_ref[...] = jnp.zeros_like(d0_ref)
            d1_ref[...] = jnp.zeros_like(d1_ref)
            don_ref[...] = jnp.zeros_like(don_ref)

        fn = functools.partial(_hgrn_rows, layer=layer, tstack=t_ref[...], ones_bd=e_ref[...])
        _, vjp = jax.vjp(fn, x_ref[...], st_in_ref[0], lb0_ref[...], lb1_ref[...], on_ref[...])
        dx, dst, d0, d1, don = vjp((dy_ref[...], dst_ref[...]))
        dx_ref[...] = dx.astype(BF16)
        dst_ref[...] = dst
        d0_ref[...] += d0
        d1_ref[...] += d1
        don_ref[...] += don

    rev = lambda n: nc - 1 - n
    return pl.pallas_call(
        body,
        name=f"hgrn_bwd_{layer}",
        grid=(nc,),
        in_specs=[
            pl.BlockSpec((rows, 4 * B_WIDTH), lambda n: (rev(n), 1)),
            pl.BlockSpec((1, B_WIDTH, B_WIDTH), lambda n: (rev(n), 0, 0)),
            pl.BlockSpec((rows, B_WIDTH), lambda n: (rev(n), 1)),
            _full((1, B_WIDTH)), _full((1, B_WIDTH)), _full((1, B_WIDTH)),
            _full(((len(B_LEVELS) + 1) * CHUNK, CHUNK)), _full((B_WIDTH, B_WIDTH)),
        ],
        out_specs=[
            pl.BlockSpec((rows, 4 * B_WIDTH), lambda n: (rev(n), 0)),
            _full((1, B_WIDTH)), _full((1, B_WIDTH)), _full((1, B_WIDTH)),
        ],
        out_shape=[jax.ShapeDtypeStruct((seq, 4 * B_WIDTH), BF16)] + [jax.ShapeDtypeStruct((1, B_WIDTH), F32)] * 3,
        scratch_shapes=[pltpu.VMEM((B_WIDTH, B_WIDTH), F32)],
        compiler_params=_cparams("arbitrary"),
    )(proj, states, dy, lb0, lb1, onorm, tstack, ones_bd)


D_MODEL = 1024
D_INT = 4096


def _rms_stats(xf):
    r = lax.rsqrt(jnp.mean(xf * xf, axis=-1, keepdims=True) + NORM_EPS)
    return r, xf * r


def _rms_bwd(dy, g, r, xh):
    u = dy * g
    return r * (u - xh * jnp.mean(u * xh, axis=-1, keepdims=True))


def inproj(x, g, w, layer):
    seq = x.shape[0]
    tm, tn = min(seq, 1024), 512

    def body(x_ref, g_ref, w_ref, p_ref, h_ref):
        @pl.when(pl.program_id(1) == 0)
        def _():
            _, xh = _rms_stats(x_ref[...])
            h_ref[...] = (xh * g_ref[...]).astype(BF16)

        p_ref[...] = _dot(h_ref[...], w_ref[0])

    return pl.pallas_call(
        body,
        name="inproj",
        grid=(seq // tm, D_INT // tn),
        in_specs=[
            pl.BlockSpec((tm, D_MODEL), lambda i, j: (i, 0)),
            _full((1, D_MODEL)),
            pl.BlockSpec((1, D_MODEL, tn), lambda i, j: (layer, 0, j)),
        ],
        out_specs=[pl.BlockSpec((tm, tn), lambda i, j: (i, j)), pl.BlockSpec((tm, D_MODEL), lambda i, j: (i, 0))],
        out_shape=[jax.ShapeDtypeStruct((seq, D_INT), F32), jax.ShapeDtypeStruct((seq, D_MODEL), BF16)],
        compiler_params=_cparams("parallel", "arbitrary"),
    )(x, g, w)


def outproj(x, ya, yb, o, proj, wo, layer):
    seq = x.shape[0]
    tm = min(seq, 512)
    blk = wo.shape[2]

    def body(x_ref, ya_ref, yb_ref, o_ref, z_ref, w_ref, xn_ref, y_ref):
        yc = (o_ref[...] * jax.nn.silu(z_ref[...])).astype(BF16)
        y = jnp.concatenate([ya_ref[...], yb_ref[...], yc], axis=1)
        y_ref[...] = y
        w = jnp.concatenate([w_ref[d, 0] for d in range(N_DEV)], axis=0)
        xn_ref[...] = x_ref[...] + _dot(y, w)

    return pl.pallas_call(
        body,
        name="outproj",
        grid=(seq // tm,),
        in_specs=[
            pl.BlockSpec((tm, D_MODEL), lambda i: (i, 0)),
            pl.BlockSpec((tm, 256), lambda i: (i, 0)),
            pl.BlockSpec((tm, 256), lambda i: (i, 0)),
            pl.BlockSpec((tm, 512), lambda i: (i, 0)),
            pl.BlockSpec((tm, 512), lambda i: (i, 7)),
            pl.BlockSpec((N_DEV, 1, blk, D_MODEL), lambda i: (0, layer, 0, 0)),
        ],
        out_specs=[pl.BlockSpec((tm, D_MODEL), lambda i: (i, 0)), pl.BlockSpec((tm, D_MODEL), lambda i: (i, 0))],
        out_shape=[jax.ShapeDtypeStruct((seq, D_MODEL), F32), jax.ShapeDtypeStruct((seq, D_MODEL), BF16)],
        compiler_params=_cparams("parallel"),
    )(x, ya, yb, o, proj, wo)


def outproj_bwd(dx, y, wo, layer, stacked=None):
    seq = dx.shape[0]
    ts = min(seq, 512)
    _, depth, blk, _ = wo.shape

    def body(dx_ref, y_ref, w_ref, *refs):
        dy_ref, dw_ref = refs[-2:]

        @pl.when(pl.program_id(0) == 0)
        def _():
            dw_ref[...] = jnp.zeros_like(dw_ref)

        dxb = dx_ref[...].astype(BF16)
        w = jnp.concatenate([w_ref[d, 0] for d in range(N_DEV)], axis=0)
        dy_ref[...] = lax.dot_general(dxb, w, _NT, preferred_element_type=F32)
        dw = lax.dot_general(y_ref[...], dxb, _TN, preferred_element_type=F32)
        for d in range(N_DEV):
            dw_ref[d % 2, d // 2, 0] += dw[d * blk:(d + 1) * blk]

    carried = () if stacked is None else (stacked,)
    out_shape = [jax.ShapeDtypeStruct((seq, D_MODEL), F32), jax.ShapeDtypeStruct((2, N_CHIP, depth, blk, D_MODEL), F32)]
    return pl.pallas_call(
        body,
        name="outproj_bwd",
        grid=(seq // ts,),
        in_specs=[
            pl.BlockSpec((ts, D_MODEL), lambda i: (i, 0)),
            pl.BlockSpec((ts, D_MODEL), lambda i: (i, 0)),
            pl.BlockSpec((N_DEV, 1, blk, D_MODEL), lambda i: (0, layer, 0, 0)),
        ] + [_ANY] * len(carried),
        out_specs=[pl.BlockSpec((ts, D_MODEL), lambda i: (i, 0)),
                   pl.BlockSpec((2, N_CHIP, 1, blk, D_MODEL), lambda i: (0, 0, layer, 0, 0))],
        out_shape=out_shape,
        input_output_aliases={3: 1} if carried else {},
        compiler_params=_cparams("arbitrary"),
    )(dx, y, wo, *carried)


def inproj_bwd_x(dproj, w, x, g, dx_in, layer):
    seq = x.shape[0]
    tm, tk = min(seq, 512), 1024
    nk = D_INT // tk

    def body(dp_ref, w_ref, x_ref, g_ref, dxin_ref, dx_ref, dg_ref, acc_ref):
        k = pl.program_id(1)

        @pl.when(k == 0)
        def _():
            acc_ref[...] = jnp.zeros_like(acc_ref)

        acc_ref[...] += lax.dot_general(dp_ref[...], w_ref[0], _NT, preferred_element_type=F32)

        @pl.when(k == nk - 1)
        def _():
            @pl.when(pl.program_id(0) == 0)
            def _():
                dg_ref[...] = jnp.zeros_like(dg_ref)

            dh = acc_ref[...]
            g = g_ref[...]
            r, xh = _rms_stats(x_ref[...])
            dg_ref[...] += jnp.sum(dh * xh, axis=0, keepdims=True)
            dx_ref[...] = dxin_ref[...] + _rms_bwd(dh, g, r, xh)

    return pl.pallas_call(
        body,
        name="inproj_bwd_x",
        grid=(seq // tm, nk),
        in_specs=[
            pl.BlockSpec((tm, tk), lambda i, k: (i, k)),
            pl.BlockSpec((1, D_MODEL, tk), lambda i, k: (layer, 0, k)),
            pl.BlockSpec((tm, D_MODEL), lambda i, k: (i, 0)),
            _full((1, D_MODEL)),
            pl.BlockSpec((tm, D_MODEL), lambda i, k: (i, 0)),
        ],
        out_specs=[pl.BlockSpec((tm, D_MODEL), lambda i, k: (i, 0)), _full((1, D_MODEL))],
        out_shape=[jax.ShapeDtypeStruct((seq, D_MODEL), F32), jax.ShapeDtypeStruct((1, D_MODEL), F32)],
        scratch_shapes=[pltpu.VMEM((tm, D_MODEL), F32)],
        compiler_params=_cparams("arbitrary", "arbitrary"),
    )(dproj, w, x, g, dx_in)


def inproj_bwd_w(h, dproj, layer, depth, stacked=None):
    seq = h.shape[0]
    ts, tn = min(seq, 1024), 512

    def body(h_ref, dp_ref, *refs):
        dw_ref = refs[-1]

        @pl.when(pl.program_id(1) == 0)
        def _():
            dw_ref[...] = jnp.zeros_like(dw_ref)

        dw_ref[0] += lax.dot_general(h_ref[...], dp_ref[...], _TN, preferred_element_type=F32)

    carried = () if stacked is None else (stacked,)
    return pl.pallas_call(
        body,
        name="inproj_bwd_w",
        grid=(D_INT // tn, seq // ts),
        in_specs=[pl.BlockSpec((ts, D_MODEL), lambda j, s: (s, 0)), pl.BlockSpec((ts, tn), lambda j, s: (s, j))]
        + [_ANY] * len(carried),
        out_specs=pl.BlockSpec((1, D_MODEL, tn), lambda j, s: (layer, 0, j)),
        out_shape=jax.ShapeDtypeStruct((depth, D_MODEL, D_INT), F32),
        input_output_aliases={2: 0} if carried else {},
        compiler_params=_cparams("parallel", "arbitrary"),
    )(h, dproj, *carried)


N_IN = 3848


def _internal_of(col):
    return col if col < 768 else (col + 256 if col < 3840 else 768 + col - 3840)


def _column_runs(n_shard):
    runs = []
    for d in range(N_IN // n_shard):
        mine = []
        for j in range(n_shard):
            ci = _internal_of(d * n_shard + j)
            if mine and mine[-1][0] + mine[-1][1] == ci:
                mine[-1][1] += 1
            else:
                mine.append([ci, 1, j])
        runs.append(mine)
    return runs


def assemble_w_in(wi_all):
    n_dev, depth, _, n_shard = wi_all.shape
    tr = 256
    pieces = [[] for _ in range(D_INT // LANES)]
    for d, mine in enumerate(_column_runs(n_shard)):
        for ci, ln, off in mine:
            while ln > 0:
                blk, at = divmod(ci, LANES)
                take = min(ln, LANES - at)
                pieces[blk].append((at, take, d, off))
                ci, ln, off = ci + take, ln - take, off + take

    def body(x_ref, o_ref):
        for blk, parts in enumerate(pieces):
            vals, at = [], 0
            for start, ln, d, off in sorted(parts):
                if start > at:
                    vals.append(jnp.zeros((tr, start - at), BF16))
                vals.append(x_ref[d, 0, :, off:off + ln])
                at = start + ln
            if at < LANES:
                vals.append(jnp.zeros((tr, LANES - at), BF16))
            o_ref[0, :, blk * LANES:(blk + 1) * LANES] = vals[0] if len(vals) == 1 else jnp.concatenate(vals, axis=1)

    return pl.pallas_call(
        body,
        name="assemble_w_in",
        grid=(depth, D_MODEL // tr),
        in_specs=[pl.BlockSpec((n_dev, 1, tr, n_shard), lambda l, r: (0, l, r, 0))],
        out_specs=pl.BlockSpec((1, tr, D_INT), lambda l, r: (l, r, 0)),
        out_shape=jax.ShapeDtypeStruct((depth, D_MODEL, D_INT), BF16),
        compiler_params=_cparams("parallel", "parallel"),
    )(wi_all)


def split_w_in_grad(dwi, n_shard):
    depth = dwi.shape[0]
    tr = 256
    runs = _column_runs(n_shard)

    def body(x_ref, o_ref):
        for d, mine in enumerate(runs):
            for ci, ln, off in mine:
                o_ref[d % 2, d // 2, 0, :, off:off + ln] = x_ref[0, :, ci:ci + ln]

    return pl.pallas_call(
        body,
        name="split_w_in_grad",
        grid=(depth, D_MODEL // tr),
        in_specs=[pl.BlockSpec((1, tr, D_INT), lambda l, r: (l, r, 0))],
        out_specs=pl.BlockSpec((2, N_CHIP, 1, tr, n_shard), lambda l, r: (0, 0, l, r, 0)),
        out_shape=jax.ShapeDtypeStruct((2, N_CHIP, depth, D_MODEL, n_shard), F32),
        compiler_params=_cparams("parallel", "parallel"),
    )(dwi)


def final_loss(x, g, tgt):
    seq = x.shape[0]
    tm = min(seq, 512)

    def body(x_ref, g_ref, t_ref, dx_ref, dg_ref, loss_ref):
        @pl.when(pl.program_id(0) == 0)
        def _():
            dg_ref[...] = jnp.zeros_like(dg_ref)
            loss_ref[...] = jnp.zeros_like(loss_ref)

        g = g_ref[...]
        r, xh = _rms_stats(x_ref[...])
        err = xh * g - t_ref[...]
        sq = jnp.sum(jnp.sum(err * err, axis=1, keepdims=True), axis=0, keepdims=True)
        loss_ref[...] += jnp.broadcast_to(sq * (0.5 / D_MODEL), loss_ref.shape)
        dout = err * (1.0 / D_MODEL)
        dg_ref[...] += jnp.sum(dout * xh, axis=0, keepdims=True)
        dx_ref[...] = _rms_bwd(dout, g, r, xh)

    return pl.pallas_call(
        body,
        name="final_loss",
        grid=(seq // tm,),
        in_specs=[pl.BlockSpec((tm, D_MODEL), lambda i: (i, 0)), _full((1, D_MODEL)), pl.BlockSpec((tm, D_MODEL), lambda i: (i, 0))],
        out_specs=[pl.BlockSpec((tm, D_MODEL), lambda i: (i, 0)), _full((1, D_MODEL)), _full((8, LANES))],
        out_shape=[jax.ShapeDtypeStruct((seq, D_MODEL), F32), jax.ShapeDtypeStruct((1, D_MODEL), F32), jax.ShapeDtypeStruct((8, LANES), F32)],
        compiler_params=_cparams("arbitrary"),
    )(x, g, tgt)


C_WIDTH = 512
C_HEADS = 8
C_HDIM = 64
C_PAIRS = C_HEADS // 2
C_BQ = 512
C_TAIL = 16
C_KG = 4


def _split3(x):
    hi = x.astype(BF16)
    r = x - hi.astype(F32)
    mid = r.astype(BF16)
    return hi, mid, (r - mid.astype(F32)).astype(BF16)


def _piece_selectors():
    sel = np.zeros((C_HEADS, 3 * LANES, LANES), np.float32)
    for p in range(C_PAIRS):
        for e in range(2):
            for t in range(3):
                sel[2 * p + e, t * LANES + 2 * p + e, 3 * e + t] = -1.0
    return sel


def fox_prep(proj, bf_row):
    seq = proj.shape[0]
    nblk = seq // CHUNK
    tril = jnp.asarray(np.tril(np.ones((CHUNK, CHUNK), np.float32)), BF16)
    sel = jnp.asarray(_piece_selectors(), BF16)
    rows_t = CHUNK + C_TAIL

    def body(fl_ref, q_ref, k_ref, v_ref, bf_ref, l_ref, sel_ref, ka_ref, va_ref, vt_ref, kt_ref, qt_ref, qa_ref, carry_ref):
        @pl.when(pl.program_id(0) == 0)
        def _():
            carry_ref[...] = jnp.zeros_like(carry_ref)

        lf = jax.nn.log_sigmoid(fl_ref[:, :LANES] + bf_ref[...])
        c = _exact_times(l_ref[...], lf, 3) + carry_ref[...]
        carry_ref[...] += jnp.sum(lf, axis=0, keepdims=True)
        c3 = jnp.concatenate(_split3(c), axis=1)
        lane = lax.broadcasted_iota(jnp.int32, (CHUNK, LANES), 1)
        row = lax.broadcasted_iota(jnp.int32, (CHUNK, LANES), 0)
        r16 = lax.broadcasted_iota(jnp.int32, (C_TAIL, 2 * CHUNK), 0)
        l16 = lax.broadcasted_iota(jnp.int32, (C_TAIL, 2 * CHUNK), 1)
        zero = jnp.zeros((CHUNK, LANES), BF16)
        one = jnp.ones((CHUNK, LANES), BF16)

        def by_keys(x, right_a, right_b):
            xb = x.astype(BF16)
            top = jnp.concatenate([jnp.where(lane < C_HDIM, xb, zero), right_a], axis=1)
            return jnp.concatenate([top, jnp.concatenate([jnp.where(lane < C_HDIM, zero, xb), right_b], axis=1)], axis=0)

        def by_lanes(x, tail):
            xt = x.T.astype(BF16)
            main = jnp.concatenate([jnp.where(row < C_HDIM, xt, zero), jnp.where(row < C_HDIM, zero, xt)], axis=1)
            return jnp.concatenate([main, tail], axis=0)

        for p in range(C_PAIRS):
            cols = slice(p * LANES, (p + 1) * LANES)
            q2, k2, v2 = q_ref[:, cols] * (C_HDIM ** -0.5), k_ref[:, cols], v_ref[:, cols]
            negc = [_dot(c3, sel_ref[2 * p + e]).astype(BF16) for e in range(2)]
            ones3 = [jnp.where((lane >= 3 * e) & (lane < 3 * e + 3), one, zero) for e in range(2)]
            tail = jnp.where(((r16 == 2 * p) & (l16 < CHUNK)) | ((r16 == 2 * p + 1) & (l16 >= CHUNK)), 1.0, 0.0).astype(BF16)
            ka_ref[p] = by_keys(k2, negc[0], negc[1])
            va_ref[p] = by_keys(v2, ones3[0], ones3[1])
            kt_ref[p] = by_lanes(k2, tail)
            vt_ref[p] = by_lanes(v2, tail)
            qt_ref[p] = jnp.concatenate([q2.T.astype(BF16), jnp.where(row < 6, one, zero)], axis=0)
            qa_ref[p] = jnp.concatenate([q2.astype(BF16), jnp.where((lane == 2 * p) | (lane == 2 * p + 1), one, zero)], axis=1)

    wide = lambda j: pl.BlockSpec((CHUNK, C_WIDTH), lambda n: (n, j))
    by_rows = pl.BlockSpec((C_PAIRS, 2 * CHUNK, 2 * CHUNK), lambda n: (0, n, 0))
    by_cols = pl.BlockSpec((C_PAIRS, rows_t, 2 * CHUNK), lambda n: (0, 0, n))
    return pl.pallas_call(
        body,
        name="fox_prep",
        grid=(nblk,),
        in_specs=[pl.BlockSpec((CHUNK, 256), lambda n: (n, 3)), wide(4), wide(5), wide(6), _full((1, LANES)),
                  _full((CHUNK, CHUNK)), _full((C_HEADS, 3 * LANES, LANES))],
        out_specs=[by_rows, by_rows, by_cols, by_cols,
                   pl.BlockSpec((C_PAIRS, 2 * CHUNK, CHUNK), lambda n: (0, 0, n)),
                   pl.BlockSpec((C_PAIRS, CHUNK, 2 * CHUNK), lambda n: (0, n, 0))],
        out_shape=[jax.ShapeDtypeStruct((C_PAIRS, 2 * seq, 2 * CHUNK), BF16)] * 2
        + [jax.ShapeDtypeStruct((C_PAIRS, rows_t, 2 * seq), BF16)] * 2
        + [jax.ShapeDtypeStruct((C_PAIRS, 2 * CHUNK, seq), BF16), jax.ShapeDtypeStruct((C_PAIRS, seq, 2 * CHUNK), BF16)],
        scratch_shapes=[pltpu.VMEM((1, LANES), F32)],
        compiler_params=_cparams("arbitrary"),
    )(proj, proj, proj, proj, bf_row, tril, sel)


def _visible(shape, key0, query0):
    row = lax.broadcasted_iota(jnp.int32, shape, 0)
    key = key0 + lax.shift_left(lax.shift_right_logical(row, 8), 7) + (row & (CHUNK - 1))
    return key <= query0 + lax.broadcasted_iota(jnp.int32, shape, 1)


def _rows_ab(a, b, n):
    return jnp.concatenate([jnp.broadcast_to(a, (C_HDIM, n)), jnp.broadcast_to(b, (C_HDIM, n))], axis=0)


def fox_fwd(qt, ka, vt):
    seq = qt.shape[2]
    nblk = seq // CHUNK
    bq = min(C_BQ, seq)
    grp = bq // CHUNK
    rows_t = CHUNK + C_TAIL

    def body(qt_ref, ka_ref, vt_ref, o_ref, lse_ref, acc_ref, s_ref):
        p, i = pl.program_id(0), pl.program_id(1)
        qtile = qt_ref[0]
        r16 = lax.broadcasted_iota(jnp.int32, (C_TAIL, bq), 0)

        def scores(t):
            at = pl.multiple_of(t * grp * 2 * CHUNK, 2 * CHUNK)
            return _dot(ka_ref[0, pl.ds(at, grp * 2 * CHUNK), :], qtile)

        def group(t, m, masked):
            ma, mb = m
            at = pl.multiple_of(t * grp * 2 * CHUNK, 2 * CHUNK)
            s = s_ref[...]
            if masked:
                s = jnp.where(_visible(s.shape, t * bq, i * bq), s, -jnp.inf)
            sa = [s[g * 2 * CHUNK:g * 2 * CHUNK + CHUNK] for g in range(grp)]
            sb = [s[g * 2 * CHUNK + CHUNK:(g + 1) * 2 * CHUNK] for g in range(grp)]
            na, nb = ma, mb
            for g in range(grp):
                na = jnp.maximum(na, jnp.max(sa[g], axis=0, keepdims=True))
                nb = jnp.maximum(nb, jnp.max(sb[g], axis=0, keepdims=True))
            al_a, al_b = jnp.exp(ma - na), jnp.exp(mb - nb)
            pt = jnp.concatenate([jnp.exp(x - n) for g in range(grp) for x, n in ((sa[g], na), (sb[g], nb))], axis=0)
            pv = _dot(vt_ref[0, :, pl.ds(at, grp * 2 * CHUNK)], pt.astype(BF16))
            tail = jnp.where(r16 == 2 * p, al_a, jnp.where(r16 == 2 * p + 1, al_b, 1.0))
            acc_ref[...] = acc_ref[...] * jnp.concatenate([_rows_ab(al_a, al_b, bq), tail], axis=0) + pv
            return na, nb

        def step(t, m):
            s_next = scores(t + 1)
            m = group(t, m, False)
            s_ref[...] = s_next
            return m

        acc_ref[...] = jnp.zeros_like(acc_ref)
        s_ref[...] = scores(0)
        m = (jnp.full((1, bq), -jnp.inf, F32), jnp.full((1, bq), -jnp.inf, F32))
        m = lax.fori_loop(0, i, step, m)
        ma, mb = group(i, m, True)
        tailv = acc_ref[CHUNK:rows_t, :]
        la = jnp.sum(jnp.where(r16 == 2 * p, tailv, 0.0), axis=0, keepdims=True)
        lb = jnp.sum(jnp.where(r16 == 2 * p + 1, tailv, 0.0), axis=0, keepdims=True)
        o_ref[...] = (acc_ref[0:CHUNK, :] * _rows_ab(1.0 / la, 1.0 / lb, bq)).T
        lse_ref[0, 0:1, :] = ma + jnp.log(la)
        lse_ref[0, 1:2, :] = mb + jnp.log(lb)

    return pl.pallas_call(
        body,
        name="fox_fwd",
        grid=(C_PAIRS, seq // bq),
        in_specs=[
            pl.BlockSpec((1, 2 * CHUNK, bq), lambda p, i: (p, 0, i)),
            pl.BlockSpec((1, 2 * seq, 2 * CHUNK), lambda p, i: (p, 0, 0)),
            pl.BlockSpec((1, rows_t, 2 * seq), lambda p, i: (p, 0, 0)),
        ],
        out_specs=[pl.BlockSpec((bq, LANES), lambda p, i: (i, p)), pl.BlockSpec((1, 2, bq), lambda p, i: (p, 0, i))],
        out_shape=[jax.ShapeDtypeStruct((seq, C_WIDTH), F32), jax.ShapeDtypeStruct((C_PAIRS, 2, seq), F32)],
        scratch_shapes=[pltpu.VMEM((rows_t, bq), F32), pltpu.VMEM((grp * 2 * CHUNK, bq), F32)],
        compiler_params=_cparams("parallel", "arbitrary"),
    )(qt, ka, vt)


def fox_bwd_prep(dy, o, proj):
    seq = o.shape[0]
    ind = np.zeros((C_WIDTH, LANES), np.float32)
    for h in range(C_HEADS):
        ind[h * C_HDIM:(h + 1) * C_HDIM, h] = 1.0
    ind = jnp.asarray(ind, BF16)
    sel = _piece_selectors()
    sel = jnp.asarray(np.stack([sel[2 * p].T + sel[2 * p + 1].T for p in range(C_PAIRS)]), BF16)

    def body(dy_ref, o_ref, z_ref, ind_ref, sel_ref, do_ref, dz_ref, dot_ref):
        dy_c, o_v, z = dy_ref[...], o_ref[...], z_ref[...]
        sg = jax.nn.sigmoid(z)
        do = dy_c * (z * sg)
        do_ref[...] = do.astype(BF16)
        dz_ref[...] = (dy_c * o_v * (sg * (1.0 + z * (1.0 - sg)))).astype(BF16)
        prod = do * o_v
        hi = prod.astype(BF16)
        lo = (prod - hi.astype(F32)).astype(BF16)
        delta = _dot(hi, ind_ref[...]) + _dot(lo, ind_ref[...])
        d3 = jnp.concatenate(_split3(delta.T), axis=0)
        for p in range(C_PAIRS):
            tail = _dot(sel_ref[p], d3).astype(BF16)
            dot_ref[p] = jnp.concatenate([do[:, p * LANES:(p + 1) * LANES].T.astype(BF16), tail], axis=0)

    return pl.pallas_call(
        body,
        name="fox_bwd_prep",
        grid=(seq // CHUNK,),
        in_specs=[
            pl.BlockSpec((CHUNK, C_WIDTH), lambda i: (i, 1)),
            pl.BlockSpec((CHUNK, C_WIDTH), lambda i: (i, 0)),
            pl.BlockSpec((CHUNK, C_WIDTH), lambda i: (i, 7)),
            _full((C_WIDTH, LANES)), _full((C_PAIRS, LANES, 3 * LANES)),
        ],
        out_specs=[
            pl.BlockSpec((CHUNK, C_WIDTH), lambda i: (i, 0)),
            pl.BlockSpec((CHUNK, C_WIDTH), lambda i: (i, 0)),
            pl.BlockSpec((C_PAIRS, 2 * CHUNK, CHUNK), lambda i: (0, 0, i)),
        ],
        out_shape=[jax.ShapeDtypeStruct((seq, C_WIDTH), BF16)] * 2 + [jax.ShapeDtypeStruct((C_PAIRS, 2 * CHUNK, seq), BF16)],
        compiler_params=_cparams("parallel"),
    )(dy, o, proj, ind, sel)


def fox_bwd(ka, va, kt, qt, dot_t, qa, dob, lse):
    seq = qt.shape[2]
    nblk = seq // CHUNK
    bq = min(C_BQ, seq)
    nq = seq // bq
    kg = min(C_KG, nblk)
    ng = nblk // kg
    rows_t = CHUNK + C_TAIL

    def body(ka_ref, va_ref, kt_ref, qt_ref, dot_ref, qa_ref, do_ref, lse_ref,
             dq_ref, dk_ref, dv_ref, dck_ref, dcq_ref, dqt_acc, dv_acc, dka_acc):
        p, jg = pl.program_id(0), pl.program_id(1)

        @pl.when(jg == 0)
        def _():
            dqt_acc[...] = jnp.zeros_like(dqt_acc)

        dv_acc[...] = jnp.zeros_like(dv_acc)
        dka_acc[...] = jnp.zeros_like(dka_acc)

        def step(i, carry, masked):
            cols = pl.ds(pl.multiple_of(i * bq, bq), bq)
            qtile, dotile = qt_ref[0, :, cols], dot_ref[0, :, cols]
            do, qa_i = do_ref[cols, :], qa_ref[0, cols, :]
            lse2 = jnp.concatenate([jnp.broadcast_to(lse_ref[0, 0:1, cols], (CHUNK, bq)),
                                    jnp.broadcast_to(lse_ref[0, 1:2, cols], (CHUNK, bq))] * kg, axis=0)
            pt = jnp.exp(_dot(ka_ref[0], qtile) - lse2)
            if masked:
                pt = jnp.where(_visible(pt.shape, jg * kg * CHUNK, i * bq), pt, 0.0)
            ds = pt * _dot(va_ref[0], dotile)
            ptb, dsb = pt.astype(BF16), ds.astype(BF16)
            dv_acc[...] += _dot(ptb, do)
            dka_acc[...] += _dot(dsb, qa_i)
            dqt_acc[:, cols] += _dot(kt_ref[0], dsb)
            return carry

        i0 = (jg * kg * CHUNK) // bq
        step(i0, 0, True)
        lax.fori_loop(i0 + 1, nq, functools.partial(step, masked=False), 0)
        lane = lax.broadcasted_iota(jnp.int32, (CHUNK, LANES), 1)
        for kb in range(kg):
            rows = slice(kb * CHUNK, (kb + 1) * CHUNK)
            ra = slice(kb * 2 * CHUNK, kb * 2 * CHUNK + CHUNK)
            rb = slice(kb * 2 * CHUNK + CHUNK, (kb + 1) * 2 * CHUNK)
            dk_ref[rows, :] = jnp.where(lane < C_HDIM, dka_acc[ra, 0:LANES], dka_acc[rb, 0:LANES]).astype(BF16)
            dv_ref[rows, :] = jnp.where(lane < C_HDIM, dv_acc[ra, :], dv_acc[rb, :]).astype(BF16)
            dck_ref[0, rows, :] = (jnp.where(lane == 2 * p, dka_acc[ra, LANES:], 0.0)
                                   + jnp.where(lane == 2 * p + 1, dka_acc[rb, LANES:], 0.0))

        @pl.when(jg == ng - 1)
        def _():
            for c in range(nq):
                dq_ref[c * bq:(c + 1) * bq, :] = (dqt_acc[0:CHUNK, c * bq:(c + 1) * bq].T * (C_HDIM ** -0.5)).astype(BF16)
            dcq_ref[0] = dqt_acc[CHUNK:rows_t, :]

    per_pair = lambda r, c: pl.BlockSpec((1, r, c), lambda p, j: (p, 0, 0))
    by_rows = pl.BlockSpec((1, kg * 2 * CHUNK, 2 * CHUNK), lambda p, j: (p, j, 0))
    by_cols = pl.BlockSpec((1, rows_t, kg * 2 * CHUNK), lambda p, j: (p, 0, j))
    return pl.pallas_call(
        body,
        name="fox_bwd",
        grid=(C_PAIRS, ng),
        in_specs=[by_rows, by_rows, by_cols, per_pair(2 * CHUNK, seq), per_pair(2 * CHUNK, seq),
                  per_pair(seq, 2 * CHUNK), pl.BlockSpec((seq, LANES), lambda p, j: (0, p)), per_pair(2, seq)],
        out_specs=[pl.BlockSpec((seq, LANES), lambda p, j: (0, p)),
                   pl.BlockSpec((kg * CHUNK, LANES), lambda p, j: (j, p)),
                   pl.BlockSpec((kg * CHUNK, LANES), lambda p, j: (j, p)),
                   pl.BlockSpec((1, kg * CHUNK, LANES), lambda p, j: (p, j, 0)),
                   per_pair(C_TAIL, seq)],
        out_shape=[jax.ShapeDtypeStruct((seq, C_WIDTH), BF16)] * 3
        + [jax.ShapeDtypeStruct((C_PAIRS, seq, LANES), F32), jax.ShapeDtypeStruct((C_PAIRS, C_TAIL, seq), F32)],
        scratch_shapes=[pltpu.VMEM((rows_t, seq), F32), pltpu.VMEM((kg * 2 * CHUNK, LANES), F32),
                        pltpu.VMEM((kg * 2 * CHUNK, 2 * CHUNK), F32)],
        compiler_params=_cparams("parallel", "arbitrary"),
    )(ka, va, kt, qt, dot_t, qa, dob, lse)


def fox_post(dcq, dck, proj, bf_row):
    seq = proj.shape[0]
    nc = seq // CHUNK
    triu = jnp.asarray(np.triu(np.ones((CHUNK, CHUNK), np.float32)), BF16)

    def body(dq_ref, dk_ref, fl_ref, bf_ref, u_ref, dfl_ref, dbf_ref, carry_ref):
        @pl.when(pl.program_id(0) == 0)
        def _():
            carry_ref[...] = jnp.zeros_like(carry_ref)
            dbf_ref[...] = jnp.zeros_like(dbf_ref)

        rows = (dq_ref[0] + dq_ref[1]) + (dq_ref[2] + dq_ref[3])
        dc = jnp.concatenate([rows, jnp.zeros((CHUNK - C_TAIL, CHUNK), F32)], axis=0).T
        dc = dc - ((dk_ref[0] + dk_ref[1]) + (dk_ref[2] + dk_ref[3]))
        g = _exact_times(u_ref[...], dc, 3) + carry_ref[...]
        carry_ref[...] += jnp.sum(dc, axis=0, keepdims=True)
        dfl = g * jax.nn.sigmoid(-(fl_ref[:, :LANES] + bf_ref[...]))
        dbf_ref[...] += jnp.sum(dfl, axis=0, keepdims=True)
        dfl_ref[...] = jnp.concatenate([dfl, jnp.zeros_like(dfl)], axis=1).astype(BF16)

    rev = lambda n: nc - 1 - n
    return pl.pallas_call(
        body,
        name="fox_post",
        grid=(nc,),
        in_specs=[
            pl.BlockSpec((C_PAIRS, C_TAIL, CHUNK), lambda n: (0, 0, rev(n))),
            pl.BlockSpec((C_PAIRS, CHUNK, LANES), lambda n: (0, rev(n), 0)),
            pl.BlockSpec((CHUNK, 256), lambda n: (rev(n), 3)),
            _full((1, LANES)), _full((CHUNK, CHUNK)),
        ],
        out_specs=[pl.BlockSpec((CHUNK, 256), lambda n: (rev(n), 0)), _full((1, LANES))],
        out_shape=[jax.ShapeDtypeStruct((seq, 256), BF16), jax.ShapeDtypeStruct((1, LANES), F32)],
        scratch_shapes=[pltpu.VMEM((1, LANES), F32)],
        compiler_params=_cparams("arbitrary"),
    )(dcq, dck, proj, bf_row, triu)


N_DEV = 8
MESH = pl.DeviceIdType.MESH
_ANY = pl.BlockSpec(memory_space=pl.ANY)


def _mesh_pos():
    return lax.axis_index("x"), lax.axis_index("y"), lax.axis_index("c")


def _dev_index(px, py, pc):
    return 4 * px + 2 * py + pc


def _row_pieces(ref, rows):
    return [ref.at[idx + (pl.ds(r, rows),)] for idx in np.ndindex(*ref.shape[:-2]) for r in range(0, ref.shape[-2], rows)]


class _Transfer:
    def __init__(self, src, dst, rows, send_sem, recv_sem, to):
        self.src, self.dst, self.rows, self.sems, self.to = src, dst, rows, (send_sem, recv_sem), to

    def _copy(self, src, dst):
        return pltpu.make_async_remote_copy(src_ref=src, dst_ref=dst, send_sem=self.sems[0], recv_sem=self.sems[1],
                                            device_id=self.to, device_id_type=MESH)

    def start(self):
        for s, d in zip(_row_pieces(self.src, self.rows), _row_pieces(self.dst, self.rows), strict=True):
            self._copy(s, d).start()

    def wait_send(self):
        self._copy(self.src, self.dst).wait_send()

    def wait_recv(self):
        self._copy(self.src, self.dst).wait_recv()


def allgather_weights(wi, wo):
    piece_rows = (128, 64)

    def body(wi_ref, wo_ref, wi_all, wo_all, send_sems, recv_sems, local_sems):
        x, y, c = _mesh_pos()
        me, sibling = (x, y, c), (x, y, 1 - c)
        chips = [(1 - x, y), (x, 1 - y), (1 - x, 1 - y)]
        arrays = ((wi_ref, wi_all), (wo_ref, wo_all))

        def copy(a, k, block, to, own=False):
            src, out = arrays[a]
            slot = out.at[_dev_index(*block)]
            return _Transfer(src if own else slot, slot, piece_rows[a], send_sems.at[a, k], recv_sems.at[a, k], to)

        both = range(len(arrays))
        mine = [pltpu.make_async_copy(arrays[a][0], arrays[a][1].at[_dev_index(*me)], local_sems.at[a]) for a in both]
        for cp in mine:
            cp.start()
        first = [copy(a, 1 + j, me, (*chip, c), own=True) for j, chip in enumerate(chips) for a in both]
        first += [copy(a, 0, me, sibling, own=True) for a in both]
        for cp in first:
            cp.start()
        passed = [copy(a, 4 + j, (*chip, c), sibling) for j, chip in enumerate(chips) for a in both]
        for j, chip in enumerate(chips):
            for a in both:
                copy(a, 1 + j, (*chip, c), me).wait_recv()
            for a in both:
                passed[2 * j + a].start()
        for a in both:
            copy(a, 0, sibling, me).wait_recv()
        for j, chip in enumerate(chips):
            for a in both:
                copy(a, 4 + j, (*chip, 1 - c), me).wait_recv()
        for cp in first + passed:
            cp.wait_send()
        for cp in mine:
            cp.wait()

    return pl.pallas_call(
        body,
        name="allgather_weights",
        in_specs=[_ANY, _ANY],
        out_specs=[_ANY, _ANY],
        out_shape=[jax.ShapeDtypeStruct((N_DEV,) + wi.shape, wi.dtype), jax.ShapeDtypeStruct((N_DEV,) + wo.shape, wo.dtype)],
        scratch_shapes=[pltpu.SemaphoreType.DMA((2, 7)), pltpu.SemaphoreType.DMA((2, 7)), pltpu.SemaphoreType.DMA((2,))],
    )(wi, wo)


N_CHIP = 4


def pair_exchange(gwi, gwo, gsm):
    piece_rows = (256, gwo.shape[-2], gsm.shape[0] // 2)

    def body(gwi_ref, gwo_ref, gsm_ref, qwi, qwo, qsm, send_sems, recv_sems):
        x, y, c = _mesh_pos()
        srcs, outs = (gwi_ref.at[1 - c], gwo_ref.at[1 - c], gsm_ref), (qwi, qwo, qsm)
        copies = [_Transfer(srcs[a], outs[a], piece_rows[a], send_sems.at[a], recv_sems.at[a], (x, y, 1 - c))
                  for a in range(3)]
        for cp in copies:
            cp.start()
        for cp in copies:
            cp.wait_recv()
        for cp in copies:
            cp.wait_send()

    return pl.pallas_call(
        body,
        name="pair_exchange",
        in_specs=[_ANY, _ANY, _ANY],
        out_specs=[_ANY, _ANY, _ANY],
        out_shape=[jax.ShapeDtypeStruct(gwi.shape[1:], gwi.dtype), jax.ShapeDtypeStruct(gwo.shape[1:], gwo.dtype),
                   jax.ShapeDtypeStruct(gsm.shape, gsm.dtype)],
        scratch_shapes=[pltpu.SemaphoreType.DMA((3,)), pltpu.SemaphoreType.DMA((3,))],
    )(gwi, gwo, gsm)


def _slab_spec(lead, rows, n_c, pick=None):
    if pick is None:
        return pl.BlockSpec((1, rows, n_c), lambda i, r, *_: (i, r, 0))
    return pl.BlockSpec((1, 1, rows, n_c), lambda i, r, s: (pick(s), i, r, 0))


def pair_sum(own, other, dtype, rows, name, core=None):
    n, n_r, n_c = other.shape

    def body(*refs):
        a_ref, b_ref, o_ref = refs[-3:]
        o_ref[0] = (a_ref[...].reshape(rows, n_c) + b_ref[0]).astype(dtype)

    grid_spec = pltpu.PrefetchScalarGridSpec(
        num_scalar_prefetch=0 if core is None else 1,
        grid=(n, n_r // rows),
        in_specs=[_slab_spec(1, rows, n_c, None if core is None else (lambda s: s[0])), _slab_spec(1, rows, n_c)],
        out_specs=_slab_spec(1, rows, n_c),
    )
    args = (own, other) if core is None else (core, own, other)
    return pl.pallas_call(
        body,
        name=name,
        grid_spec=grid_spec,
        out_shape=jax.ShapeDtypeStruct((n, n_r, n_c), dtype),
        compiler_params=_cparams("parallel", "parallel"),
    )(*args)


def chip_exchange(swi, swo, ssm):
    piece_rows = (128, swo.shape[-2] // 2, ssm.shape[0] // 2)

    def body(swi_ref, swo_ref, ssm_ref, rwi, rwo, rsm, send_sems, recv_sems, local_sem):
        x, y, c = _mesh_pos()
        chip = 2 * x + y
        srcs, outs = (swi_ref, swo_ref, ssm_ref), (rwi, rwo, rsm)
        three = range(3)
        local = pltpu.make_async_copy(ssm_ref, rsm.at[chip], local_sem)
        local.start()

        def peer_of(k):
            return x ^ ((k >> 1) & 1), y ^ (k & 1)

        def copy(a, k, sending):
            px, py = peer_of(k)
            src = srcs[a] if a == 2 else srcs[a].at[2 * px + py]
            slot = k - 1 if a < 2 else (chip if sending else 2 * px + py)
            return _Transfer(src, outs[a].at[slot], piece_rows[a], send_sems.at[a, k - 1], recv_sems.at[a, k - 1], (px, py, c))

        sends = [copy(a, k, True) for k in range(1, N_CHIP) for a in three]
        for cp in sends:
            cp.start()
        for k in range(1, N_CHIP):
            for a in three:
                copy(a, k, False).wait_recv()
        for cp in sends:
            cp.wait_send()
        local.wait()

    return pl.pallas_call(
        body,
        name="chip_exchange",
        in_specs=[_ANY, _ANY, _ANY],
        out_specs=[_ANY, _ANY, _ANY],
        out_shape=[jax.ShapeDtypeStruct((N_CHIP - 1,) + swi.shape[1:], swi.dtype),
                   jax.ShapeDtypeStruct((N_CHIP - 1,) + swo.shape[1:], swo.dtype),
                   jax.ShapeDtypeStruct((N_CHIP,) + ssm.shape, ssm.dtype)],
        scratch_shapes=[pltpu.SemaphoreType.DMA((3, 3)), pltpu.SemaphoreType.DMA((3, 3)), pltpu.SemaphoreType.DMA],
    )(swi, swo, ssm)


ADAM_LR = 0.001
ADAM_B1 = 0.9
ADAM_B2 = 0.999
ADAM_EPS = 1e-08
ADAM_WD = 0.01
ADAM_STEP = 10


def adam_reduce(parts, w, m, v, rows, name, own=None, chip=None):
    n_l, n_r, n_c = w.shape
    n_parts = parts.shape[0]

    def body(*refs):
        p_ref, w_ref, m_ref, v_ref, g_ref, d_ref, m2_ref, v2_ref = refs[-8:]
        g = p_ref[0, 0].astype(F32)
        if own is not None:
            g = refs[-9][...].reshape(rows, n_c).astype(F32) + g
        for d in range(1, n_parts):
            g = g + p_ref[d, 0].astype(F32)
        m2 = ADAM_B1 * m_ref[0] + (1.0 - ADAM_B1) * g
        v2 = ADAM_B2 * v_ref[0] + (1.0 - ADAM_B2) * (g * g)
        m_hat = m2 / (1.0 - ADAM_B1 ** ADAM_STEP)
        v_hat = v2 / (1.0 - ADAM_B2 ** ADAM_STEP)
        g_ref[0] = g
        d_ref[0] = -ADAM_LR * (m_hat / (jnp.sqrt(v_hat) + ADAM_EPS) + ADAM_WD * w_ref[0])
        m2_ref[0] = m2
        v2_ref[0] = v2

    blk = lambda: pl.BlockSpec((1, rows, n_c), lambda l, r, *_: (l, r, 0))
    in_specs = [pl.BlockSpec((n_parts, 1, rows, n_c), lambda l, r, *_: (0, l, r, 0)), blk(), blk(), blk()]
    args = (parts, w, m, v)
    if own is not None:
        in_specs = [pl.BlockSpec((1, 1, rows, n_c), lambda l, r, s: (s[0], l, r, 0))] + in_specs
        args = (chip, own) + args
    grid_spec = pltpu.PrefetchScalarGridSpec(
        num_scalar_prefetch=0 if own is None else 1, grid=(n_l, n_r // rows), in_specs=in_specs,
        out_specs=[blk(), blk(), blk(), blk()])
    return pl.pallas_call(
        body,
        name=name,
        grid_spec=grid_spec,
        out_shape=[jax.ShapeDtypeStruct(w.shape, F32)] * 4,
        compiler_params=_cparams("parallel", "parallel"),
    )(*args)


_SMALL = (("norm_g", (2, 1024)), ("gmlp_ln_g", (2, 4, 64)), ("gmlp_ln_b", (2, 4, 64)), ("gmlp_w_s", (2, 4, 128, 128)),
          ("gmlp_b_s", (2, 4, 128)), ("hgrn_lb", (2, 256)), ("hgrn_onorm_g", (2, 64)), ("fox_b_f", (2, 8)),
          ("final_norm_g", (1024,)), ("loss", ()))


def _padded(n):
    return -(-n // LANES) * LANES


_SMALL_ROWS = -(-sum(_padded(int(np.prod(s))) for _, s in _SMALL) // LANES // 8) * 8


def _pack_small(vals):
    flat = []
    for (name, shape), a in zip(_SMALL, vals, strict=True):
        n = int(np.prod(shape))
        flat.append(jnp.pad(a.reshape(n).astype(F32), (0, _padded(n) - n)))
    flat = jnp.concatenate(flat)
    return jnp.pad(flat, (0, _SMALL_ROWS * LANES - flat.shape[0])).reshape(_SMALL_ROWS, LANES)


def _unpack_small(slab):
    flat, out, at = slab.reshape(-1), {}, 0
    for name, shape in _SMALL:
        n = int(np.prod(shape))
        out[name] = flat[at:at + n].reshape(shape)
        at += _padded(n)
    return out


def kernel(x, norm_g, w_in, w_out, gmlp_ln_g, gmlp_ln_b, gmlp_w_s, gmlp_b_s, hgrn_lb, hgrn_onorm_g, fox_b_f, final_norm_g, loss_target, m_norm_g, m_w_in, m_w_out, m_gmlp_ln_g, m_gmlp_ln_b, m_gmlp_w_s, m_gmlp_b_s, m_hgrn_lb, m_hgrn_onorm_g, m_fox_b_f, m_final_norm_g, v_norm_g, v_w_in, v_w_out, v_gmlp_ln_g, v_gmlp_ln_b, v_gmlp_w_s, v_gmlp_b_s, v_hgrn_lb, v_hgrn_onorm_g, v_fox_b_f, v_final_norm_g):
    depth = w_in.shape[0]
    seq = x.shape[1]
    assert w_in.shape[2] * N_DEV == N_IN
    xs, tgt = x[0], loss_target[0]

    wi_all, wo_all = allgather_weights(w_in.astype(BF16), w_out.astype(BF16))
    wi_int = assemble_w_in(wi_all)

    ln_g = gmlp_ln_g.reshape(depth, 1, A_WIDTH)
    ln_b = gmlp_ln_b.reshape(depth, 1, A_WIDTH)
    bs_t = jnp.pad(jnp.transpose(gmlp_b_s, (0, 2, 1)), ((0, 0), (0, 0), (0, LANES - A_GROUPS)))
    lb0, lb1 = hgrn_lb[0:1], hgrn_lb[1:2]
    onorm = jnp.tile(hgrn_onorm_g, (1, B_HEADS)).reshape(depth, 1, B_WIDTH)
    bf_row = jnp.pad(fox_b_f, ((0, 0), (0, LANES - C_HEADS))).reshape(depth, 1, LANES)

    saved = []
    xc = xs
    for l in range(depth):
        proj, h = inproj(xc, norm_g[l:l + 1], wi_int, l)
        ya = gmlp_fwd(proj, ln_g[l], ln_b[l], gmlp_w_s[l], bs_t[l])
        yb, states = hgrn_fwd(proj, lb0, lb1, onorm[l], l)
        ka, va, vt, kt, qt, qa = fox_prep(proj, bf_row[l])
        o, lse = fox_fwd(qt, ka, vt)
        xn, yfull = outproj(xc, ya, yb, o, proj, wo_all, l)
        saved.append((xc, proj, h, states, ka, va, kt, qt, qa, o, lse, yfull))
        xc = xn

    dx, d_final_g, loss_tile = final_loss(xc, final_norm_g[None], tgt)

    g_norm = [None] * depth
    g_ln_g, g_ln_b, g_ws, g_bs, g_on, g_bf = ([None] * depth for _ in range(6))
    g_lb0, g_lb1 = jnp.zeros_like(lb0), jnp.zeros_like(lb1)
    dwi = gwo = None
    for l in reversed(range(depth)):
        x_in, proj, h, states, ka, va, kt, qt, qa, o, lse, yfull = saved[l]
        dy, gwo = outproj_bwd(dx, yfull, wo_all, l, gwo)
        d_a, g_ln_g[l], g_ln_b[l], g_ws[l], dbs_t = gmlp_bwd(proj, dy, ln_g[l], ln_b[l], gmlp_w_s[l], bs_t[l])
        g_bs[l] = dbs_t[:, :A_GROUPS].T
        d_b, d0, d1, don = hgrn_bwd(proj, states, dy, lb0, lb1, onorm[l], l)
        g_lb0, g_lb1 = g_lb0 + d0, g_lb1 + d1
        g_on[l] = don.reshape(B_HEADS, B_KDIM).sum(0)
        dob, d_z, dot_t = fox_bwd_prep(dy, o, proj)
        d_q, d_k, d_v, dck, dcq = fox_bwd(ka, va, kt, qt, dot_t, qa, dob, lse)
        d_fl, dbf = fox_post(dcq, dck, proj, bf_row[l])
        g_bf[l] = dbf[0, :C_HEADS]
        dproj = jnp.concatenate([d_a, d_fl, d_b, d_q, d_k, d_v, d_z], axis=1)
        dx, g_norm[l] = inproj_bwd_x(dproj, wi_int, x_in, norm_g[l:l + 1], dx, l)
        dwi = inproj_bwd_w(h, dproj, l, depth, dwi)

    gwi = split_w_in_grad(dwi, w_in.shape[2])
    gsm = _pack_small([
        jnp.concatenate(g_norm), jnp.stack(g_ln_g), jnp.stack(g_ln_b), jnp.stack(g_ws), jnp.stack(g_bs),
        jnp.concatenate([g_lb0, g_lb1]), jnp.stack(g_on), jnp.stack(g_bf), d_final_g, loss_tile[0, 0]])
    core = lax.axis_index("c").astype(jnp.int32).reshape(1)
    chip = (2 * lax.axis_index("x") + lax.axis_index("y")).astype(jnp.int32).reshape(1)
    qwi, qwo, qsm = pair_exchange(gwi, gwo, gsm)
    flat = lambda a: a.reshape(a.shape[:-4] + (N_CHIP * depth,) + a.shape[-2:])
    swi = pair_sum(flat(gwi), flat(qwi), BF16, 256, "pair_sum_w_in", core).reshape(qwi.shape)
    swo = pair_sum(flat(gwo), flat(qwo), BF16, gwo.shape[3], "pair_sum_w_out", core).reshape(qwo.shape)
    ssm = pair_sum(gsm[None], qsm[None], F32, _SMALL_ROWS, "pair_sum_small")[0]
    rwi, rwo, rsm = chip_exchange(swi, swo, ssm)

    small_w = (norm_g, gmlp_ln_g, gmlp_ln_b, gmlp_w_s, gmlp_b_s, hgrn_lb, hgrn_onorm_g, fox_b_f, final_norm_g)
    small_m = (m_norm_g, m_gmlp_ln_g, m_gmlp_ln_b, m_gmlp_w_s, m_gmlp_b_s, m_hgrn_lb, m_hgrn_onorm_g, m_fox_b_f, m_final_norm_g)
    small_v = (v_norm_g, v_gmlp_ln_g, v_gmlp_ln_b, v_gmlp_w_s, v_gmlp_b_s, v_hgrn_lb, v_hgrn_onorm_g, v_fox_b_f, v_final_norm_g)
    zero = jnp.zeros((), F32)
    res_wi = adam_reduce(rwi, w_in, m_w_in, v_w_in, 256, "adam_w_in", own=swi, chip=chip)
    res_wo = adam_reduce(rwo, w_out, m_w_out, v_w_out, w_out.shape[1], "adam_w_out", own=swo, chip=chip)
    res_sm = adam_reduce(rsm[:, None], _pack_small(small_w + (zero,))[None], _pack_small(small_m + (zero,))[None],
                         _pack_small(small_v + (zero,))[None], _SMALL_ROWS, "adam_small")
    res_sm = [_unpack_small(r[0]) for r in res_sm]

    def group(i):
        s = res_sm[i]
        return [s["norm_g"], res_wi[i], res_wo[i], s["gmlp_ln_g"], s["gmlp_ln_b"], s["gmlp_w_s"], s["gmlp_b_s"],
                s["hgrn_lb"], s["hgrn_onorm_g"], s["fox_b_f"], s["final_norm_g"]]

    return (res_sm[0]["loss"], dx[None], *group(0), *group(1), *group(2), *group(3))
```

```python
import functools

import jax
import jax.numpy as jnp
import numpy as np
from jax import lax
from jax.experimental import pallas as pl
from jax.experimental.pallas import tpu as pltpu

F32 = jnp.float32
BF16 = jnp.bfloat16

NORM_EPS = 1e-6
F_FLOOR = 1e-30
CHUNK = 128
LANES = 128
VMEM_LIMIT = 56 * 1024 * 1024


def _cparams(*sem):
    return pltpu.CompilerParams(dimension_semantics=sem, vmem_limit_bytes=VMEM_LIMIT)


def _dot(a, b, dims=(((1,), (0,)), ((), ())), precision=None):
    return lax.dot_general(a, b, dims, precision=precision, preferred_element_type=F32)


_NT = (((1,), (1,)), ((), ()))
_TN = (((0,), (0,)), ((), ()))


def _bf16_pieces(x, n):
    out, r = [], x
    for i in range(n):
        out.append(r.astype(BF16))
        if i + 1 < n:
            r = r - out[-1].astype(F32)
    return out


@functools.partial(jax.custom_vjp, nondiff_argnums=(2,))
def _times_exact(x, e, n):
    return functools.reduce(jnp.add, [_dot(p, e) for p in _bf16_pieces(x, n)])


def _times_exact_fwd(x, e, n):
    return _times_exact(x, e, n), e


def _times_exact_bwd(n, e, g):
    dx = functools.reduce(jnp.add, [lax.dot_general(p, e, _NT, preferred_element_type=F32) for p in _bf16_pieces(g, n)])
    return dx, jnp.zeros_like(e)


_times_exact.defvjp(_times_exact_fwd, _times_exact_bwd)


@functools.partial(jax.custom_vjp, nondiff_argnums=(2,))
def _exact_times(e, x, n):
    return functools.reduce(jnp.add, [_dot(e, p) for p in _bf16_pieces(x, n)])


def _exact_times_fwd(e, x, n):
    return _exact_times(e, x, n), e


def _exact_times_bwd(n, e, g):
    dx = functools.reduce(jnp.add, [lax.dot_general(e, p, _TN, preferred_element_type=F32) for p in _bf16_pieces(g, n)])
    return jnp.zeros_like(e), dx


_exact_times.defvjp(_exact_times_fwd, _exact_times_bwd)


def _group_mean_matrix(width, group):
    idx = np.arange(width) // group
    return jnp.asarray((idx[:, None] == idx[None, :]).astype(np.float32) / group, BF16)


def _group_ones_matrix(width, group):
    idx = np.arange(width) // group
    return jnp.asarray((idx[:, None] == idx[None, :]).astype(np.float32), BF16)


A_WIDTH = 256
A_GROUPS = 4
A_GDIM = 64


A_ROWS = 512


def _gmlp_chunk(x3, ln_g, ln_b, w_s, bs_t, mean_m, gind):
    n = x3.shape[0] // CHUNK
    u = jax.nn.gelu(x3[:, :A_WIDTH])
    v = jax.nn.gelu(x3[:, A_WIDTH:2 * A_WIDTH])
    z = x3[:, 2 * A_WIDTH:]
    mu = _times_exact(v, mean_m, 2)
    d = v - mu
    var = _times_exact(d * d, mean_m, 2)
    vn = d * lax.rsqrt(var + NORM_EPS) * ln_g + ln_b
    vnb = vn.astype(BF16)
    wide = jnp.concatenate([vnb[i * CHUNK:(i + 1) * CHUNK] for i in range(n)], axis=1)
    row = lax.broadcasted_iota(jnp.int32, (CHUNK, CHUNK), 0)
    col = lax.broadcasted_iota(jnp.int32, (CHUNK, CHUNK), 1)
    causal = row >= col
    lane_g = lax.shift_right_logical(lax.broadcasted_iota(jnp.int32, (CHUNK, n * A_WIDTH), 1), 6) & (A_GROUPS - 1)
    bias = _times_exact(bs_t, gind, 3)
    mixed = jnp.concatenate([bias] * n, axis=1)
    for g in range(A_GROUPS):
        wc = jnp.where(causal, w_s[g], 0.0).astype(BF16)
        mixed = mixed + jnp.where(lane_g == g, _dot(wc, wide), 0.0)
    mixed = jnp.concatenate([mixed[:, i * A_WIDTH:(i + 1) * A_WIDTH] for i in range(n)], axis=0)
    return u * mixed * jax.nn.silu(z)


def _gmlp_consts():
    gind = np.zeros((LANES, A_WIDTH), np.float32)
    for g in range(A_GROUPS):
        gind[g, g * A_GDIM:(g + 1) * A_GDIM] = 1.0
    return _group_mean_matrix(A_WIDTH, A_GDIM), jnp.asarray(gind, BF16)


def _full(shape):
    return pl.BlockSpec(shape, lambda *_: (0,) * len(shape))


def gmlp_fwd(proj, ln_g, ln_b, w_s, bs_t):
    seq = proj.shape[0]
    rows = min(A_ROWS, seq)
    mean_m, gind = _gmlp_consts()

    def body(x_ref, g_ref, b_ref, w_ref, bs_ref, m_ref, gi_ref, y_ref):
        y = _gmlp_chunk(x_ref[...], g_ref[...], b_ref[...], w_ref[...], bs_ref[...], m_ref[...], gi_ref[...])
        y_ref[...] = y.astype(BF16)

    return pl.pallas_call(
        body,
        name="gmlp_fwd",
        grid=(seq // rows,),
        in_specs=[
            pl.BlockSpec((rows, 3 * A_WIDTH), lambda n: (n, 0)),
            _full((1, A_WIDTH)), _full((1, A_WIDTH)), _full((A_GROUPS, CHUNK, CHUNK)), _full((CHUNK, LANES)),
            _full((A_WIDTH, A_WIDTH)), _full((LANES, A_WIDTH)),
        ],
        out_specs=pl.BlockSpec((rows, A_WIDTH), lambda n: (n, 0)),
        out_shape=jax.ShapeDtypeStruct((seq, A_WIDTH), BF16),
        compiler_params=_cparams("parallel"),
    )(proj, ln_g, ln_b, w_s, bs_t, mean_m, gind)


def gmlp_bwd(proj, dy, ln_g, ln_b, w_s, bs_t):
    seq = proj.shape[0]
    rows = min(A_ROWS, seq)
    mean_m, gind = _gmlp_consts()

    def body(x_ref, dy_ref, g_ref, b_ref, w_ref, bs_ref, m_ref, gi_ref, dx_ref, dg_ref, db_ref, dw_ref, dbs_ref):
        fn = functools.partial(_gmlp_chunk, mean_m=m_ref[...], gind=gi_ref[...])
        _, vjp = jax.vjp(fn, x_ref[...], g_ref[...], b_ref[...], w_ref[...], bs_ref[...])
        dx, dg, db, dw, dbs = vjp(dy_ref[...])
        dx_ref[...] = dx.astype(BF16)

        @pl.when(pl.program_id(0) == 0)
        def _():
            dg_ref[...] = jnp.zeros_like(dg_ref)
            db_ref[...] = jnp.zeros_like(db_ref)
            dw_ref[...] = jnp.zeros_like(dw_ref)
            dbs_ref[...] = jnp.zeros_like(dbs_ref)

        dg_ref[...] += dg
        db_ref[...] += db
        dw_ref[...] += dw
        dbs_ref[...] += dbs

    return pl.pallas_call(
        body,
        name="gmlp_bwd",
        grid=(seq // rows,),
        in_specs=[
            pl.BlockSpec((rows, 3 * A_WIDTH), lambda n: (n, 0)),
            pl.BlockSpec((rows, A_WIDTH), lambda n: (n, 0)),
            _full((1, A_WIDTH)), _full((1, A_WIDTH)), _full((A_GROUPS, CHUNK, CHUNK)), _full((CHUNK, LANES)),
            _full((A_WIDTH, A_WIDTH)), _full((LANES, A_WIDTH)),
        ],
        out_specs=[
            pl.BlockSpec((rows, 3 * A_WIDTH), lambda n: (n, 0)),
            _full((1, A_WIDTH)), _full((1, A_WIDTH)), _full((A_GROUPS, CHUNK, CHUNK)), _full((CHUNK, LANES)),
        ],
        out_shape=[
            jax.ShapeDtypeStruct((seq, 3 * A_WIDTH), BF16),
            jax.ShapeDtypeStruct((1, A_WIDTH), F32), jax.ShapeDtypeStruct((1, A_WIDTH), F32),
            jax.ShapeDtypeStruct((A_GROUPS, CHUNK, CHUNK), F32), jax.ShapeDtypeStruct((CHUNK, LANES), F32),
        ],
        compiler_params=_cparams("arbitrary"),
    )(proj, dy, ln_g, ln_b, w_s, bs_t, mean_m, gind)


B_WIDTH = 256
B_HEADS = 4
B_KDIM = 64
B_LEVELS = (64, 32, 16, 8, 4, 2, 1)


def _hgrn_consts():
    t = np.arange(CHUNK)
    u = t[None, :]
    mats = [np.tril(np.ones((CHUNK, CHUNK), np.float32))]
    for m in B_LEVELS:
        p = (t // (2 * m)) * (2 * m) + m - 1
        right = (t % (2 * m)) >= m
        sel = np.where(right[:, None], (u > p[:, None]) & (u <= t[:, None]), (u > t[:, None]) & (u <= p[:, None]))
        mats.append(sel.astype(np.float32))
    return jnp.asarray(np.concatenate(mats, 0), BF16), _group_ones_matrix(B_WIDTH, B_KDIM)


def _hgrn_lower_bound(lb0, lb1, layer):
    mx = jnp.maximum(lb0, lb1)
    e0 = jnp.exp(lb0 - mx)
    e1 = jnp.exp(lb1 - mx)
    p0 = e0 / (e0 + e1)
    p1 = e1 / (e0 + e1)
    cs = p0 if layer == 0 else p0 + p1
    return jnp.clip(cs - p0, 0.0, 1.0 - 1e-6)


def _hgrn_chunk(x4, st, lb0, lb1, onorm, layer, tstack, ones_bd):
    q_raw, fl, v, zg = (x4[:, i * B_WIDTH:(i + 1) * B_WIDTH] for i in range(4))
    lb = _hgrn_lower_bound(lb0, lb1, layer)
    q = jax.nn.silu(q_raw) * (B_KDIM ** -0.5)
    f = lb + (1.0 - lb) * jax.nn.sigmoid(fl)
    logf = jnp.log(jnp.maximum(f, F_FLOOR))
    k = (1.0 - lb) * jax.nn.sigmoid(-fl)
    dall = _exact_times(tstack, logf, 3)
    b = dall[:CHUNK]
    b_last = jnp.sum(logf, axis=0, keepdims=True)
    vb = v.astype(BF16)

    lane_h = lax.shift_right_logical(lax.broadcasted_iota(jnp.int32, (CHUNK, B_WIDTH), 1), 6)
    row = lax.broadcasted_iota(jnp.int32, (CHUNK, B_WIDTH), 0)
    srow = lax.broadcasted_iota(jnp.int32, (B_HEADS * CHUNK, CHUNK), 0) & (CHUNK - 1)
    scol = lax.broadcasted_iota(jnp.int32, (B_HEADS * CHUNK, CHUNK), 1)

    def heads_on_rows(a):
        return jnp.concatenate([jnp.where(lane_h == h, a, 0.0) for h in range(B_HEADS)], axis=0)

    def heads_from_rows(r):
        out = jnp.where(lane_h == 0, r[:CHUNK], 0.0)
        for h in range(1, B_HEADS):
            out = out + jnp.where(lane_h == h, r[h * CHUNK:(h + 1) * CHUNK], 0.0)
        return out

    o = lax.dot_general((q * jnp.exp(b)).astype(BF16), st.astype(BF16), _NT, preferred_element_type=F32)
    scores = jnp.zeros((B_HEADS * CHUNK, CHUNK), F32)
    for li, m in enumerate(B_LEVELS):
        e = jnp.exp(dall[(li + 1) * CHUNK:(li + 2) * CHUNK])
        right = (row & (2 * m - 1)) >= m
        qt = jnp.where(right, q * e, 0.0)
        kt = jnp.where(right, 0.0, k * e)
        sc = lax.dot_general(heads_on_rows(qt).astype(BF16), kt.astype(BF16), _NT, preferred_element_type=F32)
        sh = int(np.log2(2 * m))
        same = lax.shift_right_logical(srow, sh) == lax.shift_right_logical(scol, sh)
        scores = scores + jnp.where(same, sc, 0.0)
    o = o + heads_from_rows(_dot(scores.astype(BF16), vb))
    o = o + _times_exact(q * k, ones_bd, 2) * v

    kv = lax.dot_general(vb, (k * jnp.exp(b_last - b)).astype(BF16), _TN, preferred_element_type=F32)
    st_new = st * jnp.exp(b_last) + jnp.where(ones_bd > 0.5, kv, 0.0)

    ms = _times_exact(o * o, ones_bd, 2) * (1.0 / B_KDIM)
    y = o * lax.rsqrt(ms + NORM_EPS) * onorm * jax.nn.silu(zg)
    return y, st_new


B_ROWS = 256


def _hgrn_rows(x4, st, lb0, lb1, onorm, layer, tstack, ones_bd):
    ys = []
    for i in range(x4.shape[0] // CHUNK):
        y, st = _hgrn_chunk(x4[i * CHUNK:(i + 1) * CHUNK], st, lb0, lb1, onorm, layer, tstack, ones_bd)
        ys.append(y)
    return jnp.concatenate(ys, axis=0), st


def hgrn_fwd(proj, lb0, lb1, onorm, layer):
    seq = proj.shape[0]
    rows = min(B_ROWS, seq)
    nc = seq // rows
    tstack, ones_bd = _hgrn_consts()

    def body(x_ref, lb0_ref, lb1_ref, on_ref, t_ref, e_ref, y_ref, st_out_ref, st_ref):
        @pl.when(pl.program_id(0) == 0)
        def _():
            st_ref[...] = jnp.zeros_like(st_ref)

        st = st_ref[...]
        st_out_ref[0] = st
        y, st_new = _hgrn_rows(x_ref[...], st, lb0_ref[...], lb1_ref[...], on_ref[...], layer, t_ref[...], e_ref[...])
        y_ref[...] = y.astype(BF16)
        st_ref[...] = st_new

    return pl.pallas_call(
        body,
        name=f"hgrn_fwd_{layer}",
        grid=(nc,),
        in_specs=[
            pl.BlockSpec((rows, 4 * B_WIDTH), lambda n: (n, 1)),
            _full((1, B_WIDTH)), _full((1, B_WIDTH)), _full((1, B_WIDTH)),
            _full(((len(B_LEVELS) + 1) * CHUNK, CHUNK)), _full((B_WIDTH, B_WIDTH)),
        ],
        out_specs=[
            pl.BlockSpec((rows, B_WIDTH), lambda n: (n, 0)),
            pl.BlockSpec((1, B_WIDTH, B_WIDTH), lambda n: (n, 0, 0)),
        ],
        out_shape=[jax.ShapeDtypeStruct((seq, B_WIDTH), BF16), jax.ShapeDtypeStruct((nc, B_WIDTH, B_WIDTH), F32)],
        scratch_shapes=[pltpu.VMEM((B_WIDTH, B_WIDTH), F32)],
        compiler_params=_cparams("arbitrary"),
    )(proj, lb0, lb1, onorm, tstack, ones_bd)


def hgrn_bwd(proj, states, dy, lb0, lb1, onorm, layer):
    seq = proj.shape[0]
    rows = min(B_ROWS, seq)
    nc = seq // rows
    tstack, ones_bd = _hgrn_consts()

    def body(x_ref, st_in_ref, dy_ref, lb0_ref, lb1_ref, on_ref, t_ref, e_ref, dx_ref, d0_ref, d1_ref, don_ref, dst_ref):
        @pl.when(pl.program_id(0) == 0)
        def _():
            dst_ref[...] = jnp.zeros_like(dst_ref)
            d0_ref[...] = jnp.zeros_like(d0_ref)
            d1_ref[...] = jnp.zeros_like(d1_ref)
            don_ref[...] = jnp.zeros_like(don_ref)

        fn = functools.partial(_hgrn_rows, layer=layer, tstack=t_ref[...], ones_bd=e_ref[...])
        _, vjp = jax.vjp(fn, x_ref[...], st_in_ref[0], lb0_ref[...], lb1_ref[...], on_ref[...])
        dx, dst, d0, d1, don = vjp((dy_ref[...], dst_ref[...]))
        dx_ref[...] = dx.astype(BF16)
        dst_ref[...] = dst
        d0_ref[...] += d0
        d1_ref[...] += d1
        don_ref[...] += don

    rev = lambda n: nc - 1 - n
    return pl.pallas_call(
        body,
        name=f"hgrn_bwd_{layer}",
        grid=(nc,),
        in_specs=[
            pl.BlockSpec((rows, 4 * B_WIDTH), lambda n: (rev(n), 1)),
            pl.BlockSpec((1, B_WIDTH, B_WIDTH), lambda n: (rev(n), 0, 0)),
            pl.BlockSpec((rows, B_WIDTH), lambda n: (rev(n), 1)),
            _full((1, B_WIDTH)), _full((1, B_WIDTH)), _full((1, B_WIDTH)),
            _full(((len(B_LEVELS) + 1) * CHUNK, CHUNK)), _full((B_WIDTH, B_WIDTH)),
        ],
        out_specs=[
            pl.BlockSpec((rows, 4 * B_WIDTH), lambda n: (rev(n), 0)),
            _full((1, B_WIDTH)), _full((1, B_WIDTH)), _full((1, B_WIDTH)),
        ],
        out_shape=[jax.ShapeDtypeStruct((seq, 4 * B_WIDTH), BF16)] + [jax.ShapeDtypeStruct((1, B_WIDTH), F32)] * 3,
        scratch_shapes=[pltpu.VMEM((B_WIDTH, B_WIDTH), F32)],
        compiler_params=_cparams("arbitrary"),
    )(proj, states, dy, lb0, lb1, onorm, tstack, ones_bd)


D_MODEL = 1024
D_INT = 4096


def _rms_stats(xf):
    r = lax.rsqrt(jnp.mean(xf * xf, axis=-1, keepdims=True) + NORM_EPS)
    return r, xf * r


def _rms_bwd(dy, g, r, xh):
    u = dy * g
    return r * (u - xh * jnp.mean(u * xh, axis=-1, keepdims=True))


def inproj(x, g, w, layer):
    seq = x.shape[0]
    tm, tn = min(seq, 1024), 512

    def body(x_ref, g_ref, w_ref, p_ref, h_ref):
        @pl.when(pl.program_id(1) == 0)
        def _():
            _, xh = _rms_stats(x_ref[...])
            h_ref[...] = (xh * g_ref[...]).astype(BF16)

        p_ref[...] = _dot(h_ref[...], w_ref[0])

    return pl.pallas_call(
        body,
        name="inproj",
        grid=(seq // tm, D_INT // tn),
        in_specs=[
            pl.BlockSpec((tm, D_MODEL), lambda i, j: (i, 0)),
            _full((1, D_MODEL)),
            pl.BlockSpec((1, D_MODEL, tn), lambda i, j: (layer, 0, j)),
        ],
        out_specs=[pl.BlockSpec((tm, tn), lambda i, j: (i, j)), pl.BlockSpec((tm, D_MODEL), lambda i, j: (i, 0))],
        out_shape=[jax.ShapeDtypeStruct((seq, D_INT), F32), jax.ShapeDtypeStruct((seq, D_MODEL), BF16)],
        compiler_params=_cparams("parallel", "arbitrary"),
    )(x, g, w)


def outproj(x, ya, yb, o, proj, wo, layer):
    seq = x.shape[0]
    tm = min(seq, 512)
    blk = wo.shape[2]

    def body(x_ref, ya_ref, yb_ref, o_ref, z_ref, w_ref, xn_ref, y_ref):
        yc = (o_ref[...] * jax.nn.silu(z_ref[...])).astype(BF16)
        y = jnp.concatenate([ya_ref[...], yb_ref[...], yc], axis=1)
        y_ref[...] = y
        w = jnp.concatenate([w_ref[d, 0] for d in range(N_DEV)], axis=0)
        xn_ref[...] = x_ref[...] + _dot(y, w)

    return pl.pallas_call(
        body,
        name="outproj",
        grid=(seq // tm,),
        in_specs=[
            pl.BlockSpec((tm, D_MODEL), lambda i: (i, 0)),
            pl.BlockSpec((tm, 256), lambda i: (i, 0)),
            pl.BlockSpec((tm, 256), lambda i: (i, 0)),
            pl.BlockSpec((tm, 512), lambda i: (i, 0)),
            pl.BlockSpec((tm, 512), lambda i: (i, 7)),
            pl.BlockSpec((N_DEV, 1, blk, D_MODEL), lambda i: (0, layer, 0, 0)),
        ],
        out_specs=[pl.BlockSpec((tm, D_MODEL), lambda i: (i, 0)), pl.BlockSpec((tm, D_MODEL), lambda i: (i, 0))],
        out_shape=[jax.ShapeDtypeStruct((seq, D_MODEL), F32), jax.ShapeDtypeStruct((seq, D_MODEL), BF16)],
        compiler_params=_cparams("parallel"),
    )(x, ya, yb, o, proj, wo)


def outproj_bwd(dx, y, wo, layer, stacked=None):
    seq = dx.shape[0]
    ts = min(seq, 512)
    _, depth, blk, _ = wo.shape

    def body(dx_ref, y_ref, w_ref, *refs):
        dy_ref, dw_ref = refs[-2:]

        @pl.when(pl.program_id(0) == 0)
        def _():
            dw_ref[...] = jnp.zeros_like(dw_ref)

        dxb = dx_ref[...].astype(BF16)
        w = jnp.concatenate([w_ref[d, 0] for d in range(N_DEV)], axis=0)
        dy_ref[...] = lax.dot_general(dxb, w, _NT, preferred_element_type=F32)
        dw = lax.dot_general(y_ref[...], dxb, _TN, preferred_element_type=F32)
        for d in range(N_DEV):
            dw_ref[d % 2, d // 2, 0] += dw[d * blk:(d + 1) * blk]

    carried = () if stacked is None else (stacked,)
    out_shape = [jax.ShapeDtypeStruct((seq, D_MODEL), F32), jax.ShapeDtypeStruct((2, N_CHIP, depth, blk, D_MODEL), F32)]
    return pl.pallas_call(
        body,
        name="outproj_bwd",
        grid=(seq // ts,),
        in_specs=[
            pl.BlockSpec((ts, D_MODEL), lambda i: (i, 0)),
            pl.BlockSpec((ts, D_MODEL), lambda i: (i, 0)),
            pl.BlockSpec((N_DEV, 1, blk, D_MODEL), lambda i: (0, layer, 0, 0)),
        ] + [_ANY] * len(carried),
        out_specs=[pl.BlockSpec((ts, D_MODEL), lambda i: (i, 0)),
                   pl.BlockSpec((2, N_CHIP, 1, blk, D_MODEL), lambda i: (0, 0, layer, 0, 0))],
        out_shape=out_shape,
        input_output_aliases={3: 1} if carried else {},
        compiler_params=_cparams("arbitrary"),
    )(dx, y, wo, *carried)


def inproj_bwd_x(dproj, w, x, g, dx_in, layer):
    seq = x.shape[0]
    tm, tk = min(seq, 512), 1024
    nk = D_INT // tk

    def body(dp_ref, w_ref, x_ref, g_ref, dxin_ref, dx_ref, dg_ref, acc_ref):
        k = pl.program_id(1)

        @pl.when(k == 0)
        def _():
            acc_ref[...] = jnp.zeros_like(acc_ref)

        acc_ref[...] += lax.dot_general(dp_ref[...], w_ref[0], _NT, preferred_element_type=F32)

        @pl.when(k == nk - 1)
        def _():
            @pl.when(pl.program_id(0) == 0)
            def _():
                dg_ref[...] = jnp.zeros_like(dg_ref)

            dh = acc_ref[...]
            g = g_ref[...]
            r, xh = _rms_stats(x_ref[...])
            dg_ref[...] += jnp.sum(dh * xh, axis=0, keepdims=True)
            dx_ref[...] = dxin_ref[...] + _rms_bwd(dh, g, r, xh)

    return pl.pallas_call(
        body,
        name="inproj_bwd_x",
        grid=(seq // tm, nk),
        in_specs=[
            pl.BlockSpec((tm, tk), lambda i, k: (i, k)),
            pl.BlockSpec((1, D_MODEL, tk), lambda i, k: (layer, 0, k)),
            pl.BlockSpec((tm, D_MODEL), lambda i, k: (i, 0)),
            _full((1, D_MODEL)),
            pl.BlockSpec((tm, D_MODEL), lambda i, k: (i, 0)),
        ],
        out_specs=[pl.BlockSpec((tm, D_MODEL), lambda i, k: (i, 0)), _full((1, D_MODEL))],
        out_shape=[jax.ShapeDtypeStruct((seq, D_MODEL), F32), jax.ShapeDtypeStruct((1, D_MODEL), F32)],
        scratch_shapes=[pltpu.VMEM((tm, D_MODEL), F32)],
        compiler_params=_cparams("arbitrary", "arbitrary"),
    )(dproj, w, x, g, dx_in)


def inproj_bwd_w(h, dproj, layer, depth, stacked=None):
    seq = h.shape[0]
    ts, tn = min(seq, 1024), 512

    def body(h_ref, dp_ref, *refs):
        dw_ref = refs[-1]

        @pl.when(pl.program_id(1) == 0)
        def _():
            dw_ref[...] = jnp.zeros_like(dw_ref)

        dw_ref[0] += lax.dot_general(h_ref[...], dp_ref[...], _TN, preferred_element_type=F32)

    carried = () if stacked is None else (stacked,)
    return pl.pallas_call(
        body,
        name="inproj_bwd_w",
        grid=(D_INT // tn, seq // ts),
        in_specs=[pl.BlockSpec((ts, D_MODEL), lambda j, s: (s, 0)), pl.BlockSpec((ts, tn), lambda j, s: (s, j))]
        + [_ANY] * len(carried),
        out_specs=pl.BlockSpec((1, D_MODEL, tn), lambda j, s: (layer, 0, j)),
        out_shape=jax.ShapeDtypeStruct((depth, D_MODEL, D_INT), F32),
        input_output_aliases={2: 0} if carried else {},
        compiler_params=_cparams("parallel", "arbitrary"),
    )(h, dproj, *carried)


N_IN = 3848


def _internal_of(col):
    return col if col < 768 else (col + 256 if col < 3840 else 768 + col - 3840)


def _column_runs(n_shard):
    runs = []
    for d in range(N_IN // n_shard):
        mine = []
        for j in range(n_shard):
            ci = _internal_of(d * n_shard + j)
            if mine and mine[-1][0] + mine[-1][1] == ci:
                mine[-1][1] += 1
            else:
                mine.append([ci, 1, j])
        runs.append(mine)
    return runs


def assemble_w_in(wi_all):
    n_dev, depth, _, n_shard = wi_all.shape
    tr = 256
    pieces = [[] for _ in range(D_INT // LANES)]
    for d, mine in enumerate(_column_runs(n_shard)):
        for ci, ln, off in mine:
            while ln > 0:
                blk, at = divmod(ci, LANES)
                take = min(ln, LANES - at)
                pieces[blk].append((at, take, d, off))
                ci, ln, off = ci + take, ln - take, off + take

    def body(x_ref, o_ref):
        for blk, parts in enumerate(pieces):
            vals, at = [], 0
            for start, ln, d, off in sorted(parts):
                if start > at:
                    vals.append(jnp.zeros((tr, start - at), BF16))
                vals.append(x_ref[d, 0, :, off:off + ln])
                at = start + ln
            if at < LANES:
                vals.append(jnp.zeros((tr, LANES - at), BF16))
            o_ref[0, :, blk * LANES:(blk + 1) * LANES] = vals[0] if len(vals) == 1 else jnp.concatenate(vals, axis=1)

    return pl.pallas_call(
        body,
        name="assemble_w_in",
        grid=(depth, D_MODEL // tr),
        in_specs=[pl.BlockSpec((n_dev, 1, tr, n_shard), lambda l, r: (0, l, r, 0))],
        out_specs=pl.BlockSpec((1, tr, D_INT), lambda l, r: (l, r, 0)),
        out_shape=jax.ShapeDtypeStruct((depth, D_MODEL, D_INT), BF16),
        compiler_params=_cparams("parallel", "parallel"),
    )(wi_all)


def split_w_in_grad(dwi, n_shard):
    depth = dwi.shape[0]
    tr = 256
    runs = _column_runs(n_shard)

    def body(x_ref, o_ref):
        for d, mine in enumerate(runs):
            for ci, ln, off in mine:
                o_ref[d % 2, d // 2, 0, :, off:off + ln] = x_ref[0, :, ci:ci + ln]

    return pl.pallas_call(
        body,
        name="split_w_in_grad",
        grid=(depth, D_MODEL // tr),
        in_specs=[pl.BlockSpec((1, tr, D_INT), lambda l, r: (l, r, 0))],
        out_specs=pl.BlockSpec((2, N_CHIP, 1, tr, n_shard), lambda l, r: (0, 0, l, r, 0)),
        out_shape=jax.ShapeDtypeStruct((2, N_CHIP, depth, D_MODEL, n_shard), F32),
        compiler_params=_cparams("parallel", "parallel"),
    )(dwi)


def final_loss(x, g, tgt):
    seq = x.shape[0]
    tm = min(seq, 512)

    def body(x_ref, g_ref, t_ref, dx_ref, dg_ref, loss_ref):
        @pl.when(pl.program_id(0) == 0)
        def _():
            dg_ref[...] = jnp.zeros_like(dg_ref)
            loss_ref[...] = jnp.zeros_like(loss_ref)

        g = g_ref[...]
        r, xh = _rms_stats(x_ref[...])
        err = xh * g - t_ref[...]
        sq = jnp.sum(jnp.sum(err * err, axis=1, keepdims=True), axis=0, keepdims=True)
        loss_ref[...] += jnp.broadcast_to(sq * (0.5 / D_MODEL), loss_ref.shape)
        dout = err * (1.0 / D_MODEL)
        dg_ref[...] += jnp.sum(dout * xh, axis=0, keepdims=True)
        dx_ref[...] = _rms_bwd(dout, g, r, xh)

    return pl.pallas_call(
        body,
        name="final_loss",
        grid=(seq // tm,),
        in_specs=[pl.BlockSpec((tm, D_MODEL), lambda i: (i, 0)), _full((1, D_MODEL)), pl.BlockSpec((tm, D_MODEL), lambda i: (i, 0))],
        out_specs=[pl.BlockSpec((tm, D_MODEL), lambda i: (i, 0)), _full((1, D_MODEL)), _full((8, LANES))],
        out_shape=[jax.ShapeDtypeStruct((seq, D_MODEL), F32), jax.ShapeDtypeStruct((1, D_MODEL), F32), jax.ShapeDtypeStruct((8, LANES), F32)],
        compiler_params=_cparams("arbitrary"),
    )(x, g, tgt)


C_WIDTH = 512
C_HEADS = 8
C_HDIM = 64
C_PAIRS = C_HEADS // 2
C_BQ = 512
C_TAIL = 16
C_KG = 4


def _split3(x):
    hi = x.astype(BF16)
    r = x - hi.astype(F32)
    mid = r.astype(BF16)
    return hi, mid, (r - mid.astype(F32)).astype(BF16)


def _piece_selectors():
    sel = np.zeros((C_HEADS, 3 * LANES, LANES), np.float32)
    for p in range(C_PAIRS):
        for e in range(2):
            for t in range(3):
                sel[2 * p + e, t * LANES + 2 * p + e, 3 * e + t] = -1.0
    return sel


def fox_prep(proj, bf_row):
    seq = proj.shape[0]
    nblk = seq // CHUNK
    tril = jnp.asarray(np.tril(np.ones((CHUNK, CHUNK), np.float32)), BF16)
    sel = jnp.asarray(_piece_selectors(), BF16)
    rows_t = CHUNK + C_TAIL

    def body(fl_ref, q_ref, k_ref, v_ref, bf_ref, l_ref, sel_ref, ka_ref, va_ref, vt_ref, kt_ref, qt_ref, qa_ref, carry_ref):
        @pl.when(pl.program_id(0) == 0)
        def _():
            carry_ref[...] = jnp.zeros_like(carry_ref)

        lf = jax.nn.log_sigmoid(fl_ref[:, :LANES] + bf_ref[...])
        c = _exact_times(l_ref[...], lf, 3) + carry_ref[...]
        carry_ref[...] += jnp.sum(lf, axis=0, keepdims=True)
        c3 = jnp.concatenate(_split3(c), axis=1)
        lane = lax.broadcasted_iota(jnp.int32, (CHUNK, LANES), 1)
        row = lax.broadcasted_iota(jnp.int32, (CHUNK, LANES), 0)
        r16 = lax.broadcasted_iota(jnp.int32, (C_TAIL, 2 * CHUNK), 0)
        l16 = lax.broadcasted_iota(jnp.int32, (C_TAIL, 2 * CHUNK), 1)
        zero = jnp.zeros((CHUNK, LANES), BF16)
        one = jnp.ones((CHUNK, LANES), BF16)

        def by_keys(x, right_a, right_b):
            xb = x.astype(BF16)
            top = jnp.concatenate([jnp.where(lane < C_HDIM, xb, zero), right_a], axis=1)
            return jnp.concatenate([top, jnp.concatenate([jnp.where(lane < C_HDIM, zero, xb), right_b], axis=1)], axis=0)

        def by_lanes(x, tail):
            xt = x.T.astype(BF16)
            main = jnp.concatenate([jnp.where(row < C_HDIM, xt, zero), jnp.where(row < C_HDIM, zero, xt)], axis=1)
            return jnp.concatenate([main, tail], axis=0)

        for p in range(C_PAIRS):
            cols = slice(p * LANES, (p + 1) * LANES)
            q2, k2, v2 = q_ref[:, cols] * (C_HDIM ** -0.5), k_ref[:, cols], v_ref[:, cols]
            negc = [_dot(c3, sel_ref[2 * p + e]).astype(BF16) for e in range(2)]
            ones3 = [jnp.where((lane >= 3 * e) & (lane < 3 * e + 3), one, zero) for e in range(2)]
            tail = jnp.where(((r16 == 2 * p) & (l16 < CHUNK)) | ((r16 == 2 * p + 1) & (l16 >= CHUNK)), 1.0, 0.0).astype(BF16)
            ka_ref[p] = by_keys(k2, negc[0], negc[1])
            va_ref[p] = by_keys(v2, ones3[0], ones3[1])
            kt_ref[p] = by_lanes(k2, tail)
            vt_ref[p] = by_lanes(v2, tail)
            qt_ref[p] = jnp.concatenate([q2.T.astype(BF16), jnp.where(row < 6, one, zero)], axis=0)
            qa_ref[p] = jnp.concatenate([q2.astype(BF16), jnp.where((lane == 2 * p) | (lane == 2 * p + 1), one, zero)], axis=1)

    wide = lambda j: pl.BlockSpec((CHUNK, C_WIDTH), lambda n: (n, j))
    by_rows = pl.BlockSpec((C_PAIRS, 2 * CHUNK, 2 * CHUNK), lambda n: (0, n, 0))
    by_cols = pl.BlockSpec((C_PAIRS, rows_t, 2 * CHUNK), lambda n: (0, 0, n))
    return pl.pallas_call(
        body,
        name="fox_prep",
        grid=(nblk,),
        in_specs=[pl.BlockSpec((CHUNK, 256), lambda n: (n, 3)), wide(4), wide(5), wide(6), _full((1, LANES)),
                  _full((CHUNK, CHUNK)), _full((C_HEADS, 3 * LANES, LANES))],
        out_specs=[by_rows, by_rows, by_cols, by_cols,
                   pl.BlockSpec((C_PAIRS, 2 * CHUNK, CHUNK), lambda n: (0, 0, n)),
                   pl.BlockSpec((C_PAIRS, CHUNK, 2 * CHUNK), lambda n: (0, n, 0))],
        out_shape=[jax.ShapeDtypeStruct((C_PAIRS, 2 * seq, 2 * CHUNK), BF16)] * 2
        + [jax.ShapeDtypeStruct((C_PAIRS, rows_t, 2 * seq), BF16)] * 2
        + [jax.ShapeDtypeStruct((C_PAIRS, 2 * CHUNK, seq), BF16), jax.ShapeDtypeStruct((C_PAIRS, seq, 2 * CHUNK), BF16)],
        scratch_shapes=[pltpu.VMEM((1, LANES), F32)],
        compiler_params=_cparams("arbitrary"),
    )(proj, proj, proj, proj, bf_row, tril, sel)


def _visible(shape, key0, query0):
    row = lax.broadcasted_iota(jnp.int32, shape, 0)
    key = key0 + lax.shift_left(lax.shift_right_logical(row, 8), 7) + (row & (CHUNK - 1))
    return key <= query0 + lax.broadcasted_iota(jnp.int32, shape, 1)


def _rows_ab(a, b, n):
    return jnp.concatenate([jnp.broadcast_to(a, (C_HDIM, n)), jnp.broadcast_to(b, (C_HDIM, n))], axis=0)


def _call_carrying(ex, body, operands, *, name, grid, in_specs, out_specs, out_shape, scratch_shapes):
    if ex is None:
        return pl.pallas_call(body, name=name, grid=grid, in_specs=in_specs, out_specs=out_specs, out_shape=out_shape,
                              scratch_shapes=scratch_shapes, compiler_params=_cparams("parallel", *["arbitrary"] * (len(grid) - 1)),
                              )(*operands)
    n_in, n_out = len(in_specs), len(out_specs)

    def wrapped(*refs):
        own, parts = _carried_refs(refs, n_in, n_out, ex)
        ids = [pl.program_id(a) for a in range(len(grid))]
        pl.when(functools.reduce(jnp.logical_and, [i == 0 for i in ids]))(lambda: ex.start(*parts))
        body(*own)
        pl.when(functools.reduce(jnp.logical_and, [i == g - 1 for i, g in zip(ids, grid)]))(lambda: ex.finish(*parts))

    return pl.pallas_call(
        wrapped, name=name, grid=grid,
        in_specs=list(in_specs) + [_ANY] * len(ex.inputs), out_specs=list(out_specs) + [_ANY] * len(ex.out_shape),
        out_shape=list(out_shape) + list(ex.out_shape), scratch_shapes=list(scratch_shapes) + list(ex.scratch),
        input_output_aliases={n_in + i: n_out + o for i, o in getattr(ex, "aliases", {}).items()},
        compiler_params=_cparams(*["arbitrary"] * len(grid)),
    )(*operands, *ex.inputs)


def fox_fwd(qt, ka, vt, carried=None):
    seq = qt.shape[2]
    nblk = seq // CHUNK
    bq = min(C_BQ, seq)
    grp = bq // CHUNK
    rows_t = CHUNK + C_TAIL

    def body(qt_ref, ka_ref, vt_ref, o_ref, lse_ref, acc_ref, s_ref):
        p, i = pl.program_id(0), pl.program_id(1)
        qtile = qt_ref[0]
        r16 = lax.broadcasted_iota(jnp.int32, (C_TAIL, bq), 0)

        def scores(t):
            at = pl.multiple_of(t * grp * 2 * CHUNK, 2 * CHUNK)
            return _dot(ka_ref[0, pl.ds(at, grp * 2 * CHUNK), :], qtile)

        def group(t, m, masked):
            ma, mb = m
            at = pl.multiple_of(t * grp * 2 * CHUNK, 2 * CHUNK)
            s = s_ref[...]
            if masked:
                s = jnp.where(_visible(s.shape, t * bq, i * bq), s, -jnp.inf)
            sa = [s[g * 2 * CHUNK:g * 2 * CHUNK + CHUNK] for g in range(grp)]
            sb = [s[g * 2 * CHUNK + CHUNK:(g + 1) * 2 * CHUNK] for g in range(grp)]
            na, nb = ma, mb
            for g in range(grp):
                na = jnp.maximum(na, jnp.max(sa[g], axis=0, keepdims=True))
                nb = jnp.maximum(nb, jnp.max(sb[g], axis=0, keepdims=True))
            al_a, al_b = jnp.exp(ma - na), jnp.exp(mb - nb)
            pt = jnp.concatenate([jnp.exp(x - n) for g in range(grp) for x, n in ((sa[g], na), (sb[g], nb))], axis=0)
            pv = _dot(vt_ref[0, :, pl.ds(at, grp * 2 * CHUNK)], pt.astype(BF16))
            tail = jnp.where(r16 == 2 * p, al_a, jnp.where(r16 == 2 * p + 1, al_b, 1.0))
            acc_ref[...] = acc_ref[...] * jnp.concatenate([_rows_ab(al_a, al_b, bq), tail], axis=0) + pv
            return na, nb

        def step(t, m):
            s_next = scores(t + 1)
            m = group(t, m, False)
            s_ref[...] = s_next
            return m

        acc_ref[...] = jnp.zeros_like(acc_ref)
        s_ref[...] = scores(0)
        m = (jnp.full((1, bq), -jnp.inf, F32), jnp.full((1, bq), -jnp.inf, F32))
        m = lax.fori_loop(0, i, step, m)
        ma, mb = group(i, m, True)
        tailv = acc_ref[CHUNK:rows_t, :]
        la = jnp.sum(jnp.where(r16 == 2 * p, tailv, 0.0), axis=0, keepdims=True)
        lb = jnp.sum(jnp.where(r16 == 2 * p + 1, tailv, 0.0), axis=0, keepdims=True)
        o_ref[...] = (acc_ref[0:CHUNK, :] * _rows_ab(1.0 / la, 1.0 / lb, bq)).T
        lse_ref[0, 0:1, :] = ma + jnp.log(la)
        lse_ref[0, 1:2, :] = mb + jnp.log(lb)

    return _call_carrying(
        carried, body, (qt, ka, vt),
        name="fox_fwd",
        grid=(C_PAIRS, seq // bq),
        in_specs=[
            pl.BlockSpec((1, 2 * CHUNK, bq), lambda p, i: (p, 0, i)),
            pl.BlockSpec((1, 2 * seq, 2 * CHUNK), lambda p, i: (p, 0, 0)),
            pl.BlockSpec((1, rows_t, 2 * seq), lambda p, i: (p, 0, 0)),
        ],
        out_specs=[pl.BlockSpec((bq, LANES), lambda p, i: (i, p)), pl.BlockSpec((1, 2, bq), lambda p, i: (p, 0, i))],
        out_shape=[jax.ShapeDtypeStruct((seq, C_WIDTH), F32), jax.ShapeDtypeStruct((C_PAIRS, 2, seq), F32)],
        scratch_shapes=[pltpu.VMEM((rows_t, bq), F32), pltpu.VMEM((grp * 2 * CHUNK, bq), F32)],
    )


def fox_bwd_prep(dy, o, proj):
    seq = o.shape[0]
    ind = np.zeros((C_WIDTH, LANES), np.float32)
    for h in range(C_HEADS):
        ind[h * C_HDIM:(h + 1) * C_HDIM, h] = 1.0
    ind = jnp.asarray(ind, BF16)
    sel = _piece_selectors()
    sel = jnp.asarray(np.stack([sel[2 * p].T + sel[2 * p + 1].T for p in range(C_PAIRS)]), BF16)

    def body(dy_ref, o_ref, z_ref, ind_ref, sel_ref, do_ref, dz_ref, dot_ref):
        dy_c, o_v, z = dy_ref[...], o_ref[...], z_ref[...]
        sg = jax.nn.sigmoid(z)
        do = dy_c * (z * sg)
        do_ref[...] = do.astype(BF16)
        dz_ref[...] = (dy_c * o_v * (sg * (1.0 + z * (1.0 - sg)))).astype(BF16)
        prod = do * o_v
        hi = prod.astype(BF16)
        lo = (prod - hi.astype(F32)).astype(BF16)
        delta = _dot(hi, ind_ref[...]) + _dot(lo, ind_ref[...])
        d3 = jnp.concatenate(_split3(delta.T), axis=0)
        for p in range(C_PAIRS):
            tail = _dot(sel_ref[p], d3).astype(BF16)
            dot_ref[p] = jnp.concatenate([do[:, p * LANES:(p + 1) * LANES].T.astype(BF16), tail], axis=0)

    return pl.pallas_call(
        body,
        name="fox_bwd_prep",
        grid=(seq // CHUNK,),
        in_specs=[
            pl.BlockSpec((CHUNK, C_WIDTH), lambda i: (i, 1)),
            pl.BlockSpec((CHUNK, C_WIDTH), lambda i: (i, 0)),
            pl.BlockSpec((CHUNK, C_WIDTH), lambda i: (i, 7)),
            _full((C_WIDTH, LANES)), _full((C_PAIRS, LANES, 3 * LANES)),
        ],
        out_specs=[
            pl.BlockSpec((CHUNK, C_WIDTH), lambda i: (i, 0)),
            pl.BlockSpec((CHUNK, C_WIDTH), lambda i: (i, 0)),
            pl.BlockSpec((C_PAIRS, 2 * CHUNK, CHUNK), lambda i: (0, 0, i)),
        ],
        out_shape=[jax.ShapeDtypeStruct((seq, C_WIDTH), BF16)] * 2 + [jax.ShapeDtypeStruct((C_PAIRS, 2 * CHUNK, seq), BF16)],
        compiler_params=_cparams("parallel"),
    )(dy, o, proj, ind, sel)


def fox_bwd(ka, va, kt, qt, dot_t, qa, dob, lse, carried=None):
    seq = qt.shape[2]
    nblk = seq // CHUNK
    bq = min(C_BQ, seq)
    nq = seq // bq
    kg = min(C_KG, nblk)
    ng = nblk // kg
    rows_t = CHUNK + C_TAIL

    def body(ka_ref, va_ref, kt_ref, qt_ref, dot_ref, qa_ref, do_ref, lse_ref,
             dq_ref, dk_ref, dv_ref, dck_ref, dcq_ref, dqt_acc, dv_acc, dka_acc):
        p, jg = pl.program_id(0), pl.program_id(1)

        @pl.when(jg == 0)
        def _():
            dqt_acc[...] = jnp.zeros_like(dqt_acc)

        dv_acc[...] = jnp.zeros_like(dv_acc)
        dka_acc[...] = jnp.zeros_like(dka_acc)

        def step(i, carry, masked):
            cols = pl.ds(pl.multiple_of(i * bq, bq), bq)
            qtile, dotile = qt_ref[0, :, cols], dot_ref[0, :, cols]
            do, qa_i = do_ref[cols, :], qa_ref[0, cols, :]
            lse2 = jnp.concatenate([jnp.broadcast_to(lse_ref[0, 0:1, cols], (CHUNK, bq)),
                                    jnp.broadcast_to(lse_ref[0, 1:2, cols], (CHUNK, bq))] * kg, axis=0)
            pt = jnp.exp(_dot(ka_ref[0], qtile) - lse2)
            if masked:
                pt = jnp.where(_visible(pt.shape, jg * kg * CHUNK, i * bq), pt, 0.0)
            ds = pt * _dot(va_ref[0], dotile)
            ptb, dsb = pt.astype(BF16), ds.astype(BF16)
            dv_acc[...] += _dot(ptb, do)
            dka_acc[...] += _dot(dsb, qa_i)
            dqt_acc[:, cols] += _dot(kt_ref[0], dsb)
            return carry

        i0 = (jg * kg * CHUNK) // bq
        step(i0, 0, True)
        lax.fori_loop(i0 + 1, nq, functools.partial(step, masked=False), 0)
        lane = lax.broadcasted_iota(jnp.int32, (CHUNK, LANES), 1)
        for kb in range(kg):
            rows = slice(kb * CHUNK, (kb + 1) * CHUNK)
            ra = slice(kb * 2 * CHUNK, kb * 2 * CHUNK + CHUNK)
            rb = slice(kb * 2 * CHUNK + CHUNK, (kb + 1) * 2 * CHUNK)
            dk_ref[rows, :] = jnp.where(lane < C_HDIM, dka_acc[ra, 0:LANES], dka_acc[rb, 0:LANES]).astype(BF16)
            dv_ref[rows, :] = jnp.where(lane < C_HDIM, dv_acc[ra, :], dv_acc[rb, :]).astype(BF16)
            dck_ref[0, rows, :] = (jnp.where(lane == 2 * p, dka_acc[ra, LANES:], 0.0)
                                   + jnp.where(lane == 2 * p + 1, dka_acc[rb, LANES:], 0.0))

        @pl.when(jg == ng - 1)
        def _():
            for c in range(nq):
                dq_ref[c * bq:(c + 1) * bq, :] = (dqt_acc[0:CHUNK, c * bq:(c + 1) * bq].T * (C_HDIM ** -0.5)).astype(BF16)
            dcq_ref[0] = dqt_acc[CHUNK:rows_t, :]

    per_pair = lambda r, c: pl.BlockSpec((1, r, c), lambda p, j: (p, 0, 0))
    by_rows = pl.BlockSpec((1, kg * 2 * CHUNK, 2 * CHUNK), lambda p, j: (p, j, 0))
    by_cols = pl.BlockSpec((1, rows_t, kg * 2 * CHUNK), lambda p, j: (p, 0, j))
    return _call_carrying(
        carried, body, (ka, va, kt, qt, dot_t, qa, dob, lse),
        name="fox_bwd",
        grid=(C_PAIRS, ng),
        in_specs=[by_rows, by_rows, by_cols, per_pair(2 * CHUNK, seq), per_pair(2 * CHUNK, seq),
                  per_pair(seq, 2 * CHUNK), pl.BlockSpec((seq, LANES), lambda p, j: (0, p)), per_pair(2, seq)],
        out_specs=[pl.BlockSpec((seq, LANES), lambda p, j: (0, p)),
                   pl.BlockSpec((kg * CHUNK, LANES), lambda p, j: (j, p)),
                   pl.BlockSpec((kg * CHUNK, LANES), lambda p, j: (j, p)),
                   pl.BlockSpec((1, kg * CHUNK, LANES), lambda p, j: (p, j, 0)),
                   per_pair(C_TAIL, seq)],
        out_shape=[jax.ShapeDtypeStruct((seq, C_WIDTH), BF16)] * 3
        + [jax.ShapeDtypeStruct((C_PAIRS, seq, LANES), F32), jax.ShapeDtypeStruct((C_PAIRS, C_TAIL, seq), F32)],
        scratch_shapes=[pltpu.VMEM((rows_t, seq), F32), pltpu.VMEM((kg * 2 * CHUNK, LANES), F32),
                        pltpu.VMEM((kg * 2 * CHUNK, 2 * CHUNK), F32)],
    )


def fox_post(dcq, dck, proj, bf_row):
    seq = proj.shape[0]
    nc = seq // CHUNK
    triu = jnp.asarray(np.triu(np.ones((CHUNK, CHUNK), np.float32)), BF16)

    def body(dq_ref, dk_ref, fl_ref, bf_ref, u_ref, dfl_ref, dbf_ref, carry_ref):
        @pl.when(pl.program_id(0) == 0)
        def _():
            carry_ref[...] = jnp.zeros_like(carry_ref)
            dbf_ref[...] = jnp.zeros_like(dbf_ref)

        rows = (dq_ref[0] + dq_ref[1]) + (dq_ref[2] + dq_ref[3])
        dc = jnp.concatenate([rows, jnp.zeros((CHUNK - C_TAIL, CHUNK), F32)], axis=0).T
        dc = dc - ((dk_ref[0] + dk_ref[1]) + (dk_ref[2] + dk_ref[3]))
        g = _exact_times(u_ref[...], dc, 3) + carry_ref[...]
        carry_ref[...] += jnp.sum(dc, axis=0, keepdims=True)
        dfl = g * jax.nn.sigmoid(-(fl_ref[:, :LANES] + bf_ref[...]))
        dbf_ref[...] += jnp.sum(dfl, axis=0, keepdims=True)
        dfl_ref[...] = jnp.concatenate([dfl, jnp.zeros_like(dfl)], axis=1).astype(BF16)

    rev = lambda n: nc - 1 - n
    return pl.pallas_call(
        body,
        name="fox_post",
        grid=(nc,),
        in_specs=[
            pl.BlockSpec((C_PAIRS, C_TAIL, CHUNK), lambda n: (0, 0, rev(n))),
            pl.BlockSpec((C_PAIRS, CHUNK, LANES), lambda n: (0, rev(n), 0)),
            pl.BlockSpec((CHUNK, 256), lambda n: (rev(n), 3)),
            _full((1, LANES)), _full((CHUNK, CHUNK)),
        ],
        out_specs=[pl.BlockSpec((CHUNK, 256), lambda n: (rev(n), 0)), _full((1, LANES))],
        out_shape=[jax.ShapeDtypeStruct((seq, 256), BF16), jax.ShapeDtypeStruct((1, LANES), F32)],
        scratch_shapes=[pltpu.VMEM((1, LANES), F32)],
        compiler_params=_cparams("arbitrary"),
    )(dcq, dck, proj, bf_row, triu)


N_DEV = 8
MESH = pl.DeviceIdType.MESH
_ANY = pl.BlockSpec(memory_space=pl.ANY)


def _mesh_pos():
    return lax.axis_index("x"), lax.axis_index("y"), lax.axis_index("c")


def _dev_index(px, py, pc):
    return 4 * px + 2 * py + pc


def _row_pieces(ref, rows):
    return [ref.at[idx + (pl.ds(r, rows),)] for idx in np.ndindex(*ref.shape[:-2]) for r in range(0, ref.shape[-2], rows)]


class _Transfer:
    def __init__(self, src, dst, rows, send_sem, recv_sem, to):
        self.src, self.dst, self.rows, self.sems, self.to = src, dst, rows, (send_sem, recv_sem), to

    def _copy(self, src, dst):
        return pltpu.make_async_remote_copy(src_ref=src, dst_ref=dst, send_sem=self.sems[0], recv_sem=self.sems[1],
                                            device_id=self.to, device_id_type=MESH)

    def start(self):
        for s, d in zip(_row_pieces(self.src, self.rows), _row_pieces(self.dst, self.rows), strict=True):
            self._copy(s, d).start()

    def wait_send(self):
        self._copy(self.src, self.dst).wait_send()

    def wait_recv(self):
        self._copy(self.src, self.dst).wait_recv()


def _exchange_call(ex, name):
    n_in, n_out = len(ex.inputs), len(ex.out_shape)

    def body(*refs):
        parts = refs[:n_in], refs[n_in:n_in + n_out], refs[n_in + n_out:]
        ex.start(*parts)
        ex.finish(*parts)

    return pl.pallas_call(body, name=name, in_specs=[_ANY] * n_in, out_specs=[_ANY] * n_out, out_shape=ex.out_shape,
                          scratch_shapes=ex.scratch, input_output_aliases=getattr(ex, "aliases", {}))(*ex.inputs)


def _carried_refs(refs, n_in, n_out, ex):
    k_in, k_out, k_sem = (len(ex.inputs), len(ex.out_shape), len(ex.scratch)) if ex else (0, 0, 0)
    a, b, c = n_in + k_in, n_in + k_in + n_out, n_in + k_in + n_out + k_out
    own = refs[:n_in] + refs[a:b] + refs[c:len(refs) - k_sem]
    return own, (refs[n_in:a], refs[b:c], refs[len(refs) - k_sem:])


class AllGatherWeights:
    piece_rows = (128, 64)

    def __init__(self, wi, wo):
        self.inputs = (wi, wo)
        self.out_shape = [jax.ShapeDtypeStruct((N_DEV,) + wi.shape, wi.dtype), jax.ShapeDtypeStruct((N_DEV,) + wo.shape, wo.dtype)]
        self.scratch = [pltpu.SemaphoreType.DMA((2, 7)), pltpu.SemaphoreType.DMA((2, 7)), pltpu.SemaphoreType.DMA((2,))]

    def _plan(self, ins, outs, sems):
        send_sems, recv_sems, local_sems = sems
        x, y, c = _mesh_pos()
        me, sibling = (x, y, c), (x, y, 1 - c)
        chips = [(1 - x, y), (x, 1 - y), (1 - x, 1 - y)]
        both = range(2)

        def copy(a, k, block, to, own=False):
            slot = outs[a].at[_dev_index(*block)]
            return _Transfer(ins[a] if own else slot, slot, self.piece_rows[a], send_sems.at[a, k], recv_sems.at[a, k], to)

        mine = [pltpu.make_async_copy(ins[a], outs[a].at[_dev_index(*me)], local_sems.at[a]) for a in both]
        first = [copy(a, 1 + j, me, (*chip, c), own=True) for j, chip in enumerate(chips) for a in both]
        first += [copy(a, 0, me, sibling, own=True) for a in both]
        passed = [copy(a, 4 + j, (*chip, c), sibling) for j, chip in enumerate(chips) for a in both]
        return me, sibling, chips, c, copy, mine, first, passed

    def start(self, ins, outs, sems):
        *_, mine, first, _ = self._plan(ins, outs, sems)
        for cp in mine + first:
            cp.start()

    def finish(self, ins, outs, sems):
        me, sibling, chips, c, copy, mine, first, passed = self._plan(ins, outs, sems)
        for j, chip in enumerate(chips):
            for a in range(2):
                copy(a, 1 + j, (*chip, c), me).wait_recv()
            for a in range(2):
                passed[2 * j + a].start()
        for a in range(2):
            copy(a, 0, sibling, me).wait_recv()
        for j, chip in enumerate(chips):
            for a in range(2):
                copy(a, 4 + j, (*chip, 1 - c), me).wait_recv()
        for cp in first + passed:
            cp.wait_send()
        for cp in mine:
            cp.wait()


N_CHIP = 4


class PairExchange:
    def __init__(self, by_core, whole=()):
        self.inputs = tuple(by_core) + tuple(whole)
        self.n_by_core = len(by_core)
        self.out_shape = ([jax.ShapeDtypeStruct(a.shape[1:], a.dtype) for a in by_core]
                          + [jax.ShapeDtypeStruct(a.shape, a.dtype) for a in whole])
        n = len(self.inputs)
        self.scratch = [pltpu.SemaphoreType.DMA((n,)), pltpu.SemaphoreType.DMA((n,))]

    def _copies(self, ins, outs, sems):
        x, y, c = _mesh_pos()
        srcs = [r.at[1 - c] if a < self.n_by_core else r for a, r in enumerate(ins)]
        return [_Transfer(srcs[a], outs[a], outs[a].shape[-2], sems[0].at[a], sems[1].at[a], (x, y, 1 - c))
                for a in range(len(ins))]

    def start(self, ins, outs, sems):
        for cp in self._copies(ins, outs, sems):
            cp.start()

    def finish(self, ins, outs, sems):
        copies = self._copies(ins, outs, sems)
        for cp in copies:
            cp.wait_recv()
        for cp in copies:
            cp.wait_send()


def pair_sum(own, other, dtype, rows, name, core, layer, depth, stacked=None):
    n, n_r, n_c = other.shape

    def body(core_ref, a_ref, b_ref, *refs):
        refs[-1][0, 0] = (a_ref[0, 0] + b_ref[0]).astype(dtype)

    carried = () if stacked is None else (stacked,)
    grid_spec = pltpu.PrefetchScalarGridSpec(
        num_scalar_prefetch=1,
        grid=(n, n_r // rows),
        in_specs=[pl.BlockSpec((1, 1, rows, n_c), lambda i, r, s: (s[0], i, r, 0)),
                  pl.BlockSpec((1, rows, n_c), lambda i, r, s: (i, r, 0))] + [_ANY] * len(carried),
        out_specs=pl.BlockSpec((1, 1, rows, n_c), lambda i, r, s: (i, layer, r, 0)),
    )
    return pl.pallas_call(
        body,
        name=name,
        grid_spec=grid_spec,
        out_shape=jax.ShapeDtypeStruct((n, depth, n_r, n_c), dtype),
        input_output_aliases={3: 0} if carried else {},
        compiler_params=_cparams("parallel", "parallel"),
    )(core, own, other, *carried)


def small_sum(a, b):
    def body(a_ref, b_ref, o_ref):
        o_ref[...] = a_ref[...] + b_ref[...]

    return pl.pallas_call(body, name="pair_sum_small", out_shape=jax.ShapeDtypeStruct(a.shape, a.dtype))(a, b)


class ChipExchange:
    def __init__(self, by_chip, layer, gathered=(), stacked=None):
        stacked = tuple(stacked or ())
        self.inputs = tuple(by_chip) + tuple(gathered) + stacked
        self.n_by_chip, self.n_gathered, self.layer = len(by_chip), len(gathered), layer
        self.out_shape = ([jax.ShapeDtypeStruct((N_CHIP - 1,) + a.shape[1:], a.dtype) for a in by_chip]
                          + [jax.ShapeDtypeStruct((N_CHIP,) + a.shape, a.dtype) for a in gathered])
        self.aliases = {self.n_by_chip + self.n_gathered + i: i for i in range(len(stacked))}
        n = self.n_by_chip + self.n_gathered
        self.scratch = [pltpu.SemaphoreType.DMA((n, 3)), pltpu.SemaphoreType.DMA((n, 3)),
                        pltpu.SemaphoreType.DMA((max(self.n_gathered, 1),))]

    def _plan(self, ins, outs, sems):
        x, y, c = _mesh_pos()
        chip = 2 * x + y
        n = self.n_by_chip + self.n_gathered

        def copy(a, k, sending):
            px, py = x ^ ((k >> 1) & 1), y ^ (k & 1)
            if a < self.n_by_chip:
                src, dst = ins[a].at[2 * px + py, self.layer], outs[a].at[k - 1, self.layer]
            else:
                src, dst = ins[a], outs[a].at[chip if sending else 2 * px + py]
            return _Transfer(src, dst, dst.shape[-2], sems[0].at[a, k - 1], sems[1].at[a, k - 1], (px, py, c))

        local = [pltpu.make_async_copy(ins[a], outs[a].at[chip], sems[2].at[a - self.n_by_chip])
                 for a in range(self.n_by_chip, n)]
        return n, copy, local

    def start(self, ins, outs, sems):
        n, copy, local = self._plan(ins, outs, sems)
        for cp in local:
            cp.start()
        for k in range(1, N_CHIP):
            for a in range(n):
                copy(a, k, True).start()

    def finish(self, ins, outs, sems):
        n, copy, local = self._plan(ins, outs, sems)
        for k in range(1, N_CHIP):
            for a in range(n):
                copy(a, k, False).wait_recv()
        for k in range(1, N_CHIP):
            for a in range(n):
                copy(a, k, True).wait_send()
        for cp in local:
            cp.wait()


ADAM_LR = 0.001
ADAM_B1 = 0.9
ADAM_B2 = 0.999
ADAM_EPS = 1e-08
ADAM_WD = 0.01
ADAM_STEP = 10


def adam_reduce(parts, w, m, v, rows, name, own=None, chip=None):
    n_l, n_r, n_c = w.shape
    n_parts = parts.shape[0]

    def body(*refs):
        p_ref, w_ref, m_ref, v_ref, g_ref, d_ref, m2_ref, v2_ref = refs[-8:]
        g = p_ref[0, 0].astype(F32)
        if own is not None:
            g = refs[-9][...].reshape(rows, n_c).astype(F32) + g
        for d in range(1, n_parts):
            g = g + p_ref[d, 0].astype(F32)
        m2 = ADAM_B1 * m_ref[0] + (1.0 - ADAM_B1) * g
        v2 = ADAM_B2 * v_ref[0] + (1.0 - ADAM_B2) * (g * g)
        m_hat = m2 / (1.0 - ADAM_B1 ** ADAM_STEP)
        v_hat = v2 / (1.0 - ADAM_B2 ** ADAM_STEP)
        g_ref[0] = g
        d_ref[0] = -ADAM_LR * (m_hat / (jnp.sqrt(v_hat) + ADAM_EPS) + ADAM_WD * w_ref[0])
        m2_ref[0] = m2
        v2_ref[0] = v2

    blk = lambda: pl.BlockSpec((1, rows, n_c), lambda l, r, *_: (l, r, 0))
    in_specs = [pl.BlockSpec((n_parts, 1, rows, n_c), lambda l, r, *_: (0, l, r, 0)), blk(), blk(), blk()]
    args = (parts, w, m, v)
    if own is not None:
        in_specs = [pl.BlockSpec((1, 1, rows, n_c), lambda l, r, s: (s[0], l, r, 0))] + in_specs
        args = (chip, own) + args
    grid_spec = pltpu.PrefetchScalarGridSpec(
        num_scalar_prefetch=0 if own is None else 1, grid=(n_l, n_r // rows), in_specs=in_specs,
        out_specs=[blk(), blk(), blk(), blk()])
    return pl.pallas_call(
        body,
        name=name,
        grid_spec=grid_spec,
        out_shape=[jax.ShapeDtypeStruct(w.shape, F32)] * 4,
        compiler_params=_cparams("parallel", "parallel"),
    )(*args)


_SMALL = (("norm_g", (2, 1024)), ("gmlp_ln_g", (2, 4, 64)), ("gmlp_ln_b", (2, 4, 64)), ("gmlp_w_s", (2, 4, 128, 128)),
          ("gmlp_b_s", (2, 4, 128)), ("hgrn_lb", (2, 256)), ("hgrn_onorm_g", (2, 64)), ("fox_b_f", (2, 8)),
          ("final_norm_g", (1024,)), ("loss", ()))


def _padded(n):
    return -(-n // LANES) * LANES


_SMALL_ROWS = -(-sum(_padded(int(np.prod(s))) for _, s in _SMALL) // LANES // 8) * 8


def _pack_small(vals):
    flat = []
    for (name, shape), a in zip(_SMALL, vals, strict=True):
        n = int(np.prod(shape))
        flat.append(jnp.pad(a.reshape(n).astype(F32), (0, _padded(n) - n)))
    flat = jnp.concatenate(flat)
    return jnp.pad(flat, (0, _SMALL_ROWS * LANES - flat.shape[0])).reshape(_SMALL_ROWS, LANES)


def _unpack_small(slab):
    flat, out, at = slab.reshape(-1), {}, 0
    for name, shape in _SMALL:
        n = int(np.prod(shape))
        out[name] = flat[at:at + n].reshape(shape)
        at += _padded(n)
    return out


def kernel(x, norm_g, w_in, w_out, gmlp_ln_g, gmlp_ln_b, gmlp_w_s, gmlp_b_s, hgrn_lb, hgrn_onorm_g, fox_b_f, final_norm_g, loss_target, m_norm_g, m_w_in, m_w_out, m_gmlp_ln_g, m_gmlp_ln_b, m_gmlp_w_s, m_gmlp_b_s, m_hgrn_lb, m_hgrn_onorm_g, m_fox_b_f, m_final_norm_g, v_norm_g, v_w_in, v_w_out, v_gmlp_ln_g, v_gmlp_ln_b, v_gmlp_w_s, v_gmlp_b_s, v_hgrn_lb, v_hgrn_onorm_g, v_fox_b_f, v_final_norm_g):
    depth = w_in.shape[0]
    seq = x.shape[1]
    assert w_in.shape[2] * N_DEV == N_IN
    xs, tgt = x[0], loss_target[0]

    wi_blk, wo_blk = w_in.astype(BF16), w_out.astype(BF16)
    wi_all, wo_all = _exchange_call(AllGatherWeights(wi_blk[0], wo_blk[0]), "allgather_weights_0")

    ln_g = gmlp_ln_g.reshape(depth, 1, A_WIDTH)
    ln_b = gmlp_ln_b.reshape(depth, 1, A_WIDTH)
    bs_t = jnp.pad(jnp.transpose(gmlp_b_s, (0, 2, 1)), ((0, 0), (0, 0), (0, LANES - A_GROUPS)))
    lb0, lb1 = hgrn_lb[0:1], hgrn_lb[1:2]
    onorm = jnp.tile(hgrn_onorm_g, (1, B_HEADS)).reshape(depth, 1, B_WIDTH)
    bf_row = jnp.pad(fox_b_f, ((0, 0), (0, LANES - C_HEADS))).reshape(depth, 1, LANES)

    core = lax.axis_index("c").astype(jnp.int32).reshape(1)
    chip = (2 * lax.axis_index("x") + lax.axis_index("y")).astype(jnp.int32).reshape(1)

    saved = []
    xc = xs
    for l in range(depth):
        wi_int = assemble_w_in(wi_all[:, None])
        proj, h = inproj(xc, norm_g[l:l + 1], wi_int, 0)
        ya = gmlp_fwd(proj, ln_g[l], ln_b[l], gmlp_w_s[l], bs_t[l])
        yb, states = hgrn_fwd(proj, lb0, lb1, onorm[l], l)
        ka, va, vt, kt, qt, qa = fox_prep(proj, bf_row[l])
        nxt = AllGatherWeights(wi_blk[l + 1], wo_blk[l + 1]) if l + 1 < depth else None
        o, lse, *gathered = fox_fwd(qt, ka, vt, nxt)
        xn, yfull = outproj(xc, ya, yb, o, proj, wo_all[:, None], 0)
        saved.append((xc, proj, h, states, ka, va, kt, qt, qa, o, lse, yfull, wi_int, wo_all))
        if gathered:
            wi_all, wo_all = gathered
        xc = xn

    dx, d_final_g, loss_tile = final_loss(xc, final_norm_g[None], tgt)

    n_shard = w_in.shape[2]
    g_norm = [None] * depth
    g_ln_g, g_ln_b, g_ws, g_bs, g_on, g_bf = ([None] * depth for _ in range(6))
    g_lb0, g_lb1 = jnp.zeros_like(lb0), jnp.zeros_like(lb1)
    swi = swo = rwi = rwo = None
    pending = None
    for l in reversed(range(depth)):
        x_in, proj, h, states, ka, va, kt, qt, qa, o, lse, yfull, wi_int, wo_l = saved[l]
        dy, gwo = outproj_bwd(dx, yfull, wo_l[:, None], 0)
        d_a, g_ln_g[l], g_ln_b[l], g_ws[l], dbs_t = gmlp_bwd(proj, dy, ln_g[l], ln_b[l], gmlp_w_s[l], bs_t[l])
        g_bs[l] = dbs_t[:, :A_GROUPS].T
        d_b, d0, d1, don = hgrn_bwd(proj, states, dy, lb0, lb1, onorm[l], l)
        g_lb0, g_lb1 = g_lb0 + d0, g_lb1 + d1
        g_on[l] = don.reshape(B_HEADS, B_KDIM).sum(0)
        dob, d_z, dot_t = fox_bwd_prep(dy, o, proj)
        d_q, d_k, d_v, dck, dcq, *arrived = fox_bwd(ka, va, kt, qt, dot_t, qa, dob, lse, pending)
        if arrived:
            rwi, rwo = arrived
        d_fl, dbf = fox_post(dcq, dck, proj, bf_row[l])
        g_bf[l] = dbf[0, :C_HEADS]
        dproj = jnp.concatenate([d_a, d_fl, d_b, d_q, d_k, d_v, d_z], axis=1)
        dx, g_norm[l] = inproj_bwd_x(dproj, wi_int, x_in, norm_g[l:l + 1], dx, 0)
        gwi = split_w_in_grad(inproj_bwd_w(h, dproj, 0, 1), n_shard)
        gwi, gwo = gwi[:, :, 0], gwo[:, :, 0]
        if l > 0:
            qwi, qwo = _exchange_call(PairExchange([gwi, gwo]), f"pair_exchange_{l}")
        else:
            gsm = _pack_small([
                jnp.concatenate(g_norm), jnp.stack(g_ln_g), jnp.stack(g_ln_b), jnp.stack(g_ws), jnp.stack(g_bs),
                jnp.concatenate([g_lb0, g_lb1]), jnp.stack(g_on), jnp.stack(g_bf), d_final_g, loss_tile[0, 0]])
            qwi, qwo, qsm = _exchange_call(PairExchange([gwi, gwo], [gsm]), f"pair_exchange_{l}")
        swi = pair_sum(gwi, qwi, BF16, 256, "pair_sum_w_in", core, l, depth, swi)
        swo = pair_sum(gwo, qwo, BF16, gwo.shape[2], "pair_sum_w_out", core, l, depth, swo)
        if l > 0:
            pending = ChipExchange([swi, swo], l, stacked=None if rwi is None else [rwi, rwo])
        else:
            ssm = small_sum(gsm, qsm)
            rwi, rwo, rsm = _exchange_call(
                ChipExchange([swi, swo], l, [ssm], None if rwi is None else [rwi, rwo]), "chip_exchange_0")

    small_w = (norm_g, gmlp_ln_g, gmlp_ln_b, gmlp_w_s, gmlp_b_s, hgrn_lb, hgrn_onorm_g, fox_b_f, final_norm_g)
    small_m = (m_norm_g, m_gmlp_ln_g, m_gmlp_ln_b, m_gmlp_w_s, m_gmlp_b_s, m_hgrn_lb, m_hgrn_onorm_g, m_fox_b_f, m_final_norm_g)
    small_v = (v_norm_g, v_gmlp_ln_g, v_gmlp_ln_b, v_gmlp_w_s, v_gmlp_b_s, v_hgrn_lb, v_hgrn_onorm_g, v_fox_b_f, v_final_norm_g)
    zero = jnp.zeros((), F32)
    res_wi = adam_reduce(rwi, w_in, m_w_in, v_w_in, 256, "adam_w_in", own=swi, chip=chip)
    res_wo = adam_reduce(rwo, w_out, m_w_out, v_w_out, w_out.shape[1], "adam_w_out", own=swo, chip=chip)
    res_sm = adam_reduce(rsm[:, None], _pack_small(small_w + (zero,))[None], _pack_small(small_m + (zero,))[None],
                         _pack_small(small_v + (zero,))[None], _SMALL_ROWS, "adam_small")
    res_sm = [_unpack_small(r[0]) for r in res_sm]

    def group(i):
        s = res_sm[i]
        return [s["norm_g"], res_wi[i], res_wo[i], s["gmlp_ln_g"], s["gmlp_ln_b"], s["gmlp_w_s"], s["gmlp_b_s"],
                s["hgrn_lb"], s["hgrn_onorm_g"], s["fox_b_f"], s["final_norm_g"]]

    return (res_sm[0]["loss"], dx[None], *group(0), *group(1), *group(2), *group(3))
```

```python
import functools

import jax
import jax.numpy as jnp
import numpy as np
from jax import lax
from jax.experimental import pallas as pl
from jax.experimental.pallas import tpu as pltpu

F32 = jnp.float32
BF16 = jnp.bfloat16

NORM_EPS = 1e-6
F_FLOOR = 1e-30
CHUNK = 128
LANES = 128
VMEM_LIMIT = 56 * 1024 * 1024


def _cparams(*sem):
    return pltpu.CompilerParams(dimension_semantics=sem, vmem_limit_bytes=VMEM_LIMIT)


def _dot(a, b, dims=(((1,), (0,)), ((), ())), precision=None):
    return lax.dot_general(a, b, dims, precision=precision, preferred_element_type=F32)


_NT = (((1,), (1,)), ((), ()))
_TN = (((0,), (0,)), ((), ()))


def _bf16_pieces(x, n):
    out, r = [], x
    for i in range(n):
        out.append(r.astype(BF16))
        if i + 1 < n:
            r = r - out[-1].astype(F32)
    return out


@functools.partial(jax.custom_vjp, nondiff_argnums=(2,))
def _times_exact(x, e, n):
    return functools.reduce(jnp.add, [_dot(p, e) for p in _bf16_pieces(x, n)])


def _times_exact_fwd(x, e, n):
    return _times_exact(x, e, n), e


def _times_exact_bwd(n, e, g):
    dx = functools.reduce(jnp.add, [lax.dot_general(p, e, _NT, preferred_element_type=F32) for p in _bf16_pieces(g, n)])
    return dx, jnp.zeros_like(e)


_times_exact.defvjp(_times_exact_fwd, _times_exact_bwd)


@functools.partial(jax.custom_vjp, nondiff_argnums=(2,))
def _exact_times(e, x, n):
    return functools.reduce(jnp.add, [_dot(e, p) for p in _bf16_pieces(x, n)])


def _exact_times_fwd(e, x, n):
    return _exact_times(e, x, n), e


def _exact_times_bwd(n, e, g):
    dx = functools.reduce(jnp.add, [lax.dot_general(e, p, _TN, preferred_element_type=F32) for p in _bf16_pieces(g, n)])
    return jnp.zeros_like(e), dx


_exact_times.defvjp(_exact_times_fwd, _exact_times_bwd)


def _group_mean_matrix(width, group):
    idx = np.arange(width) // group
    return jnp.asarray((idx[:, None] == idx[None, :]).astype(np.float32) / group, BF16)


def _group_ones_matrix(width, group):
    idx = np.arange(width) // group
    return jnp.asarray((idx[:, None] == idx[None, :]).astype(np.float32), BF16)


A_WIDTH = 256
A_GROUPS = 4
A_GDIM = 64


A_ROWS = 512


def _gmlp_chunk(x3, ln_g, ln_b, w_s, bs_t, mean_m, gind):
    n = x3.shape[0] // CHUNK
    u = jax.nn.gelu(x3[:, :A_WIDTH])
    v = jax.nn.gelu(x3[:, A_WIDTH:2 * A_WIDTH])
    z = x3[:, 2 * A_WIDTH:]
    mu = _times_exact(v, mean_m, 2)
    d = v - mu
    var = _times_exact(d * d, mean_m, 2)
    vn = d * lax.rsqrt(var + NORM_EPS) * ln_g + ln_b
    vnb = vn.astype(BF16)
    wide = jnp.concatenate([vnb[i * CHUNK:(i + 1) * CHUNK] for i in range(n)], axis=1)
    row = lax.broadcasted_iota(jnp.int32, (CHUNK, CHUNK), 0)
    col = lax.broadcasted_iota(jnp.int32, (CHUNK, CHUNK), 1)
    causal = row >= col
    lane_g = lax.shift_right_logical(lax.broadcasted_iota(jnp.int32, (CHUNK, n * A_WIDTH), 1), 6) & (A_GROUPS - 1)
    bias = _times_exact(bs_t, gind, 3)
    mixed = jnp.concatenate([bias] * n, axis=1)
    for g in range(A_GROUPS):
        wc = jnp.where(causal, w_s[g], 0.0).astype(BF16)
        mixed = mixed + jnp.where(lane_g == g, _dot(wc, wide), 0.0)
    mixed = jnp.concatenate([mixed[:, i * A_WIDTH:(i + 1) * A_WIDTH] for i in range(n)], axis=0)
    return u * mixed * jax.nn.silu(z)


def _gmlp_consts():
    gind = np.zeros((LANES, A_WIDTH), np.float32)
    for g in range(A_GROUPS):
        gind[g, g * A_GDIM:(g + 1) * A_GDIM] = 1.0
    return _group_mean_matrix(A_WIDTH, A_GDIM), jnp.asarray(gind, BF16)


def _full(shape):
    return pl.BlockSpec(shape, lambda *_: (0,) * len(shape))


def gmlp_fwd(proj, ln_g, ln_b, w_s, bs_t):
    seq = proj.shape[0]
    rows = min(A_ROWS, seq)
    mean_m, gind = _gmlp_consts()

    def body(x_ref, g_ref, b_ref, w_ref, bs_ref, m_ref, gi_ref, y_ref):
        y = _gmlp_chunk(x_ref[...], g_ref[...], b_ref[...], w_ref[...], bs_ref[...], m_ref[...], gi_ref[...])
        y_ref[...] = y.astype(BF16)

    return pl.pallas_call(
        body,
        name="gmlp_fwd",
        grid=(seq // rows,),
        in_specs=[
            pl.BlockSpec((rows, 3 * A_WIDTH), lambda n: (n, 0)),
            _full((1, A_WIDTH)), _full((1, A_WIDTH)), _full((A_GROUPS, CHUNK, CHUNK)), _full((CHUNK, LANES)),
            _full((A_WIDTH, A_WIDTH)), _full((LANES, A_WIDTH)),
        ],
        out_specs=pl.BlockSpec((rows, A_WIDTH), lambda n: (n, 0)),
        out_shape=jax.ShapeDtypeStruct((seq, A_WIDTH), BF16),
        compiler_params=_cparams("parallel"),
    )(proj, ln_g, ln_b, w_s, bs_t, mean_m, gind)


def gmlp_bwd(proj, dy, ln_g, ln_b, w_s, bs_t):
    seq = proj.shape[0]
    rows = min(A_ROWS, seq)
    mean_m, gind = _gmlp_consts()

    def body(x_ref, dy_ref, g_ref, b_ref, w_ref, bs_ref, m_ref, gi_ref, dx_ref, dg_ref, db_ref, dw_ref, dbs_ref):
        fn = functools.partial(_gmlp_chunk, mean_m=m_ref[...], gind=gi_ref[...])
        _, vjp = jax.vjp(fn, x_ref[...], g_ref[...], b_ref[...], w_ref[...], bs_ref[...])
        dx, dg, db, dw, dbs = vjp(dy_ref[...])
        dx_ref[...] = dx.astype(BF16)

        @pl.when(pl.program_id(0) == 0)
        def _():
            dg_ref[...] = jnp.zeros_like(dg_ref)
            db_ref[...] = jnp.zeros_like(db_ref)
            dw_ref[...] = jnp.zeros_like(dw_ref)
            dbs_ref[...] = jnp.zeros_like(dbs_ref)

        dg_ref[...] += dg
        db_ref[...] += db
        dw_ref[...] += dw
        dbs_ref[...] += dbs

    return pl.pallas_call(
        body,
        name="gmlp_bwd",
        grid=(seq // rows,),
        in_specs=[
            pl.BlockSpec((rows, 3 * A_WIDTH), lambda n: (n, 0)),
            pl.BlockSpec((rows, A_WIDTH), lambda n: (n, 0)),
            _full((1, A_WIDTH)), _full((1, A_WIDTH)), _full((A_GROUPS, CHUNK, CHUNK)), _full((CHUNK, LANES)),
            _full((A_WIDTH, A_WIDTH)), _full((LANES, A_WIDTH)),
        ],
        out_specs=[
            pl.BlockSpec((rows, 3 * A_WIDTH), lambda n: (n, 0)),
            _full((1, A_WIDTH)), _full((1, A_WIDTH)), _full((A_GROUPS, CHUNK, CHUNK)), _full((CHUNK, LANES)),
        ],
        out_shape=[
            jax.ShapeDtypeStruct((seq, 3 * A_WIDTH), BF16),
            jax.ShapeDtypeStruct((1, A_WIDTH), F32), jax.ShapeDtypeStruct((1, A_WIDTH), F32),
            jax.ShapeDtypeStruct((A_GROUPS, CHUNK, CHUNK), F32), jax.ShapeDtypeStruct((CHUNK, LANES), F32),
        ],
        compiler_params=_cparams("arbitrary"),
    )(proj, dy, ln_g, ln_b, w_s, bs_t, mean_m, gind)


B_WIDTH = 256
B_HEADS = 4
B_KDIM = 64
B_LEVELS = (64, 32, 16, 8, 4, 2, 1)


def _hgrn_consts():
    t = np.arange(CHUNK)
    u = t[None, :]
    mats = [np.tril(np.ones((CHUNK, CHUNK), np.float32))]
    for m in B_LEVELS:
        p = (t // (2 * m)) * (2 * m) + m - 1
        right = (t % (2 * m)) >= m
        sel = np.where(right[:, None], (u > p[:, None]) & (u <= t[:, None]), (u > t[:, None]) & (u <= p[:, None]))
        mats.append(sel.astype(np.float32))
    return jnp.asarray(np.concatenate(mats, 0), BF16), _group_ones_matrix(B_WIDTH, B_KDIM)


def _hgrn_lower_bound(lb0, lb1, layer):
    mx = jnp.maximum(lb0, lb1)
    e0 = jnp.exp(lb0 - mx)
    e1 = jnp.exp(lb1 - mx)
    p0 = e0 / (e0 + e1)
    p1 = e1 / (e0 + e1)
    cs = p0 if layer == 0 else p0 + p1
    return jnp.clip(cs - p0, 0.0, 1.0 - 1e-6)


def _hgrn_chunk(x4, st, lb0, lb1, onorm, layer, tstack, ones_bd):
    q_raw, fl, v, zg = (x4[:, i * B_WIDTH:(i + 1) * B_WIDTH] for i in range(4))
    lb = _hgrn_lower_bound(lb0, lb1, layer)
    q = jax.nn.silu(q_raw) * (B_KDIM ** -0.5)
    f = lb + (1.0 - lb) * jax.nn.sigmoid(fl)
    logf = jnp.log(jnp.maximum(f, F_FLOOR))
    k = (1.0 - lb) * jax.nn.sigmoid(-fl)
    dall = _exact_times(tstack, logf, 3)
    b = dall[:CHUNK]
    b_last = jnp.sum(logf, axis=0, keepdims=True)
    vb = v.astype(BF16)

    lane_h = lax.shift_right_logical(lax.broadcasted_iota(jnp.int32, (CHUNK, B_WIDTH), 1), 6)
    row = lax.broadcasted_iota(jnp.int32, (CHUNK, B_WIDTH), 0)
    srow = lax.broadcasted_iota(jnp.int32, (B_HEADS * CHUNK, CHUNK), 0) & (CHUNK - 1)
    scol = lax.broadcasted_iota(jnp.int32, (B_HEADS * CHUNK, CHUNK), 1)

    def heads_on_rows(a):
        return jnp.concatenate([jnp.where(lane_h == h, a, 0.0) for h in range(B_HEADS)], axis=0)

    def heads_from_rows(r):
        out = jnp.where(lane_h == 0, r[:CHUNK], 0.0)
        for h in range(1, B_HEADS):
            out = out + jnp.where(lane_h == h, r[h * CHUNK:(h + 1) * CHUNK], 0.0)
        return out

    o = lax.dot_general((q * jnp.exp(b)).astype(BF16), st.astype(BF16), _NT, preferred_element_type=F32)
    scores = jnp.zeros((B_HEADS * CHUNK, CHUNK), F32)
    for li, m in enumerate(B_LEVELS):
        e = jnp.exp(dall[(li + 1) * CHUNK:(li + 2) * CHUNK])
        right = (row & (2 * m - 1)) >= m
        qt = jnp.where(right, q * e, 0.0)
        kt = jnp.where(right, 0.0, k * e)
        sc = lax.dot_general(heads_on_rows(qt).astype(BF16), kt.astype(BF16), _NT, preferred_element_type=F32)
        sh = int(np.log2(2 * m))
        same = lax.shift_right_logical(srow, sh) == lax.shift_right_logical(scol, sh)
        scores = scores + jnp.where(same, sc, 0.0)
    o = o + heads_from_rows(_dot(scores.astype(BF16), vb))
    o = o + _times_exact(q * k, ones_bd, 2) * v

    kv = lax.dot_general(vb, (k * jnp.exp(b_last - b)).astype(BF16), _TN, preferred_element_type=F32)
    st_new = st * jnp.exp(b_last) + jnp.where(ones_bd > 0.5, kv, 0.0)

    ms = _times_exact(o * o, ones_bd, 2) * (1.0 / B_KDIM)
    y = o * lax.rsqrt(ms + NORM_EPS) * onorm * jax.nn.silu(zg)
    return y, st_new


B_ROWS = 256


def _hgrn_rows(x4, st, lb0, lb1, onorm, layer, tstack, ones_bd):
    ys = []
    for i in range(x4.shape[0] // CHUNK):
        y, st = _hgrn_chunk(x4[i * CHUNK:(i + 1) * CHUNK], st, lb0, lb1, onorm, layer, tstack, ones_bd)
        ys.append(y)
    return jnp.concatenate(ys, axis=0), st


def hgrn_fwd(proj, lb0, lb1, onorm, layer):
    seq = proj.shape[0]
    rows = min(B_ROWS, seq)
    nc = seq // rows
    tstack, ones_bd = _hgrn_consts()

    def body(x_ref, lb0_ref, lb1_ref, on_ref, t_ref, e_ref, y_ref, st_out_ref, st_ref):
        @pl.when(pl.program_id(0) == 0)
        def _():
            st_ref[...] = jnp.zeros_like(st_ref)

        st = st_ref[...]
        st_out_ref[0] = st
        y, st_new = _hgrn_rows(x_ref[...], st, lb0_ref[...], lb1_ref[...], on_ref[...], layer, t_ref[...], e_ref[...])
        y_ref[...] = y.astype(BF16)
        st_ref[...] = st_new

    return pl.pallas_call(
        body,
        name=f"hgrn_fwd_{layer}",
        grid=(nc,),
        in_specs=[
            pl.BlockSpec((rows, 4 * B_WIDTH), lambda n: (n, 1)),
            _full((1, B_WIDTH)), _full((1, B_WIDTH)), _full((1, B_WIDTH)),
            _full(((len(B_LEVELS) + 1) * CHUNK, CHUNK)), _full((B_WIDTH, B_WIDTH)),
        ],
        out_specs=[
            pl.BlockSpec((rows, B_WIDTH), lambda n: (n, 0)),
            pl.BlockSpec((1, B_WIDTH, B_WIDTH), lambda n: (n, 0, 0)),
        ],
        out_shape=[jax.ShapeDtypeStruct((seq, B_WIDTH), BF16), jax.ShapeDtypeStruct((nc, B_WIDTH, B_WIDTH), F32)],
        scratch_shapes=[pltpu.VMEM((B_WIDTH, B_WIDTH), F32)],
        compiler_params=_cparams("arbitrary"),
    )(proj, lb0, lb1, onorm, tstack, ones_bd)


def hgrn_bwd(proj, states, dy, lb0, lb1, onorm, layer):
    seq = proj.shape[0]
    rows = min(B_ROWS, seq)
    nc = seq // rows
    tstack, ones_bd = _hgrn_consts()

    def body(x_ref, st_in_ref, dy_ref, lb0_ref, lb1_ref, on_ref, t_ref, e_ref, dx_ref, d0_ref, d1_ref, don_ref, dst_ref):
        @pl.when(pl.program_id(0) == 0)
        def _():
            dst_ref[...] = jnp.zeros_like(dst_ref)
            d0_ref[...] = jnp.zeros_like(d0_ref)
            d1_ref[...] = jnp.zeros_like(d1_ref)
            don_ref[...] = jnp.zeros_like(don_ref)

        fn = functools.partial(_hgrn_rows, layer=layer, tstack=t_ref[...], ones_bd=e_ref[...])
        _, vjp = jax.vjp(fn, x_ref[...], st_in_ref[0], lb0_ref[...], lb1_ref[...], on_ref[...])
        dx, dst, d0, d1, don = vjp((dy_ref[...], dst_ref[...]))
        dx_ref[...] = dx.astype(BF16)
        dst_ref[...] = dst
        d0_ref[...] += d0
        d1_ref[...] += d1
        don_ref[...] += don

    rev = lambda n: nc - 1 - n
    return pl.pallas_call(
        body,
        name=f"hgrn_bwd_{layer}",
        grid=(nc,),
        in_specs=[
            pl.BlockSpec((rows, 4 * B_WIDTH), lambda n: (rev(n), 1)),
            pl.BlockSpec((1, B_WIDTH, B_WIDTH), lambda n: (rev(n), 0, 0)),
            pl.BlockSpec((rows, B_WIDTH), lambda n: (rev(n), 1)),
            _full((1, B_WIDTH)), _full((1, B_WIDTH)), _full((1, B_WIDTH)),
            _full(((len(B_LEVELS) + 1) * CHUNK, CHUNK)), _full((B_WIDTH, B_WIDTH)),
        ],
        out_specs=[
            pl.BlockSpec((rows, 4 * B_WIDTH), lambda n: (rev(n), 0)),
            _full((1, B_WIDTH)), _full((1, B_WIDTH)), _full((1, B_WIDTH)),
        ],
        out_shape=[jax.ShapeDtypeStruct((seq, 4 * B_WIDTH), BF16)] + [jax.ShapeDtypeStruct((1, B_WIDTH), F32)] * 3,
        scratch_shapes=[pltpu.VMEM((B_WIDTH, B_WIDTH), F32)],
        compiler_params=_cparams("arbitrary"),
    )(proj, states, dy, lb0, lb1, onorm, tstack, ones_bd)


D_MODEL = 1024
D_INT = 4096


def _rms_stats(xf):
    r = lax.rsqrt(jnp.mean(xf * xf, axis=-1, keepdims=True) + NORM_EPS)
    return r, xf * r


def _rms_bwd(dy, g, r, xh):
    u = dy * g
    return r * (u - xh * jnp.mean(u * xh, axis=-1, keepdims=True))


def inproj(x, g, w, layer):
    seq = x.shape[0]
    tm = min(seq, 512)

    def body(x_ref, g_ref, w_ref, p_ref, h_ref):
        _, xh = _rms_stats(x_ref[...])
        h = (xh * g_ref[...]).astype(BF16)
        h_ref[...] = h
        p_ref[...] = _dot(h, w_ref[0])

    return pl.pallas_call(
        body,
        name="inproj",
        grid=(seq // tm,),
        in_specs=[
            pl.BlockSpec((tm, D_MODEL), lambda i: (i, 0)),
            _full((1, D_MODEL)),
            pl.BlockSpec((1, D_MODEL, D_INT), lambda i: (layer, 0, 0)),
        ],
        out_specs=[pl.BlockSpec((tm, D_INT), lambda i: (i, 0)), pl.BlockSpec((tm, D_MODEL), lambda i: (i, 0))],
        out_shape=[jax.ShapeDtypeStruct((seq, D_INT), F32), jax.ShapeDtypeStruct((seq, D_MODEL), BF16)],
        compiler_params=_cparams("parallel"),
    )(x, g, w)


def outproj(x, ya, yb, o, proj, wo, layer):
    seq = x.shape[0]
    tm = min(seq, 512)
    blk = wo.shape[2]

    def body(x_ref, ya_ref, yb_ref, o_ref, z_ref, w_ref, xn_ref, y_ref):
        yc = (o_ref[...] * jax.nn.silu(z_ref[...])).astype(BF16)
        y = jnp.concatenate([ya_ref[...], yb_ref[...], yc], axis=1)
        y_ref[...] = y
        w = jnp.concatenate([w_ref[d, 0] for d in range(N_DEV)], axis=0)
        xn_ref[...] = x_ref[...] + _dot(y, w)

    return pl.pallas_call(
        body,
        name="outproj",
        grid=(seq // tm,),
        in_specs=[
            pl.BlockSpec((tm, D_MODEL), lambda i: (i, 0)),
            pl.BlockSpec((tm, 256), lambda i: (i, 0)),
            pl.BlockSpec((tm, 256), lambda i: (i, 0)),
            pl.BlockSpec((tm, 512), lambda i: (i, 0)),
            pl.BlockSpec((tm, 512), lambda i: (i, 7)),
            pl.BlockSpec((N_DEV, 1, blk, D_MODEL), lambda i: (0, layer, 0, 0)),
        ],
        out_specs=[pl.BlockSpec((tm, D_MODEL), lambda i: (i, 0)), pl.BlockSpec((tm, D_MODEL), lambda i: (i, 0))],
        out_shape=[jax.ShapeDtypeStruct((seq, D_MODEL), F32), jax.ShapeDtypeStruct((seq, D_MODEL), BF16)],
        compiler_params=_cparams("parallel"),
    )(x, ya, yb, o, proj, wo)


def outproj_bwd(dx, y, wo, layer, stacked=None):
    seq = dx.shape[0]
    ts = min(seq, 512)
    _, depth, blk, _ = wo.shape

    def body(dx_ref, y_ref, w_ref, *refs):
        dy_ref, dw_ref = refs[-2:]

        @pl.when(pl.program_id(0) == 0)
        def _():
            dw_ref[...] = jnp.zeros_like(dw_ref)

        dxb = dx_ref[...].astype(BF16)
        w = jnp.concatenate([w_ref[d, 0] for d in range(N_DEV)], axis=0)
        dy_ref[...] = lax.dot_general(dxb, w, _NT, preferred_element_type=F32)
        dw = lax.dot_general(y_ref[...], dxb, _TN, preferred_element_type=F32)
        for d in range(N_DEV):
            dw_ref[d % 2, d // 2, 0] += dw[d * blk:(d + 1) * blk]

    carried = () if stacked is None else (stacked,)
    out_shape = [jax.ShapeDtypeStruct((seq, D_MODEL), F32), jax.ShapeDtypeStruct((2, N_CHIP, depth, blk, D_MODEL), F32)]
    return pl.pallas_call(
        body,
        name="outproj_bwd",
        grid=(seq // ts,),
        in_specs=[
            pl.BlockSpec((ts, D_MODEL), lambda i: (i, 0)),
            pl.BlockSpec((ts, D_MODEL), lambda i: (i, 0)),
            pl.BlockSpec((N_DEV, 1, blk, D_MODEL), lambda i: (0, layer, 0, 0)),
        ] + [_ANY] * len(carried),
        out_specs=[pl.BlockSpec((ts, D_MODEL), lambda i: (i, 0)),
                   pl.BlockSpec((2, N_CHIP, 1, blk, D_MODEL), lambda i: (0, 0, layer, 0, 0))],
        out_shape=out_shape,
        input_output_aliases={3: 1} if carried else {},
        compiler_params=_cparams("arbitrary"),
    )(dx, y, wo, *carried)


def inproj_bwd_x(dproj, w, x, g, dx_in, layer):
    seq = x.shape[0]
    tm = min(seq, 512)

    def body(dp_ref, w_ref, x_ref, g_ref, dxin_ref, dx_ref, dg_ref):
        @pl.when(pl.program_id(0) == 0)
        def _():
            dg_ref[...] = jnp.zeros_like(dg_ref)

        dh = lax.dot_general(dp_ref[...], w_ref[0], _NT, preferred_element_type=F32)
        r, xh = _rms_stats(x_ref[...])
        dg_ref[...] += jnp.sum(dh * xh, axis=0, keepdims=True)
        dx_ref[...] = dxin_ref[...] + _rms_bwd(dh, g_ref[...], r, xh)

    return pl.pallas_call(
        body,
        name="inproj_bwd_x",
        grid=(seq // tm,),
        in_specs=[
            pl.BlockSpec((tm, D_INT), lambda i: (i, 0)),
            pl.BlockSpec((1, D_MODEL, D_INT), lambda i: (layer, 0, 0)),
            pl.BlockSpec((tm, D_MODEL), lambda i: (i, 0)),
            _full((1, D_MODEL)),
            pl.BlockSpec((tm, D_MODEL), lambda i: (i, 0)),
        ],
        out_specs=[pl.BlockSpec((tm, D_MODEL), lambda i: (i, 0)), _full((1, D_MODEL))],
        out_shape=[jax.ShapeDtypeStruct((seq, D_MODEL), F32), jax.ShapeDtypeStruct((1, D_MODEL), F32)],
        compiler_params=_cparams("arbitrary"),
    )(dproj, w, x, g, dx_in)


def inproj_bwd_w(h, dproj, layer, depth, stacked=None):
    seq = h.shape[0]
    ts, tn = min(seq, 512), 2048

    def body(h_ref, dp_ref, *refs):
        dw_ref = refs[-1]

        @pl.when(pl.program_id(1) == 0)
        def _():
            dw_ref[...] = jnp.zeros_like(dw_ref)

        dw_ref[0] += lax.dot_general(h_ref[...], dp_ref[...], _TN, preferred_element_type=F32)

    carried = () if stacked is None else (stacked,)
    return pl.pallas_call(
        body,
        name="inproj_bwd_w",
        grid=(D_INT // tn, seq // ts),
        in_specs=[pl.BlockSpec((ts, D_MODEL), lambda j, s: (s, 0)), pl.BlockSpec((ts, tn), lambda j, s: (s, j))]
        + [_ANY] * len(carried),
        out_specs=pl.BlockSpec((1, D_MODEL, tn), lambda j, s: (layer, 0, j)),
        out_shape=jax.ShapeDtypeStruct((depth, D_MODEL, D_INT), F32),
        input_output_aliases={2: 0} if carried else {},
        compiler_params=_cparams("parallel", "arbitrary"),
    )(h, dproj, *carried)


N_IN = 3848


def _internal_of(col):
    return col if col < 768 else (col + 256 if col < 3840 else 768 + col - 3840)


def _column_runs(n_shard):
    runs = []
    for d in range(N_IN // n_shard):
        mine = []
        for j in range(n_shard):
            ci = _internal_of(d * n_shard + j)
            if mine and mine[-1][0] + mine[-1][1] == ci:
                mine[-1][1] += 1
            else:
                mine.append([ci, 1, j])
        runs.append(mine)
    return runs


def assemble_w_in(wi_all):
    n_dev, depth, _, n_shard = wi_all.shape
    tr = 256
    pieces = [[] for _ in range(D_INT // LANES)]
    for d, mine in enumerate(_column_runs(n_shard)):
        for ci, ln, off in mine:
            while ln > 0:
                blk, at = divmod(ci, LANES)
                take = min(ln, LANES - at)
                pieces[blk].append((at, take, d, off))
                ci, ln, off = ci + take, ln - take, off + take

    def body(x_ref, o_ref):
        for blk, parts in enumerate(pieces):
            vals, at = [], 0
            for start, ln, d, off in sorted(parts):
                if start > at:
                    vals.append(jnp.zeros((tr, start - at), BF16))
                vals.append(x_ref[d, 0, :, off:off + ln])
                at = start + ln
            if at < LANES:
                vals.append(jnp.zeros((tr, LANES - at), BF16))
            o_ref[0, :, blk * LANES:(blk + 1) * LANES] = vals[0] if len(vals) == 1 else jnp.concatenate(vals, axis=1)

    return pl.pallas_call(
        body,
        name="assemble_w_in",
        grid=(depth, D_MODEL // tr),
        in_specs=[pl.BlockSpec((n_dev, 1, tr, n_shard), lambda l, r: (0, l, r, 0))],
        out_specs=pl.BlockSpec((1, tr, D_INT), lambda l, r: (l, r, 0)),
        out_shape=jax.ShapeDtypeStruct((depth, D_MODEL, D_INT), BF16),
        compiler_params=_cparams("parallel", "parallel"),
    )(wi_all)


def split_w_in_grad(dwi, n_shard):
    depth = dwi.shape[0]
    tr = 256
    runs = _column_runs(n_shard)

    def body(x_ref, o_ref):
        for d, mine in enumerate(runs):
            for ci, ln, off in mine:
                o_ref[d % 2, d // 2, 0, :, off:off + ln] = x_ref[0, :, ci:ci + ln]

    return pl.pallas_call(
        body,
        name="split_w_in_grad",
        grid=(depth, D_MODEL // tr),
        in_specs=[pl.BlockSpec((1, tr, D_INT), lambda l, r: (l, r, 0))],
        out_specs=pl.BlockSpec((2, N_CHIP, 1, tr, n_shard), lambda l, r: (0, 0, l, r, 0)),
        out_shape=jax.ShapeDtypeStruct((2, N_CHIP, depth, D_MODEL, n_shard), F32),
        compiler_params=_cparams("parallel", "parallel"),
    )(dwi)


def final_loss(x, g, tgt):
    seq = x.shape[0]
    tm = min(seq, 512)

    def body(x_ref, g_ref, t_ref, dx_ref, dg_ref, loss_ref):
        @pl.when(pl.program_id(0) == 0)
        def _():
            dg_ref[...] = jnp.zeros_like(dg_ref)
            loss_ref[...] = jnp.zeros_like(loss_ref)

        g = g_ref[...]
        r, xh = _rms_stats(x_ref[...])
        err = xh * g - t_ref[...]
        sq = jnp.sum(jnp.sum(err * err, axis=1, keepdims=True), axis=0, keepdims=True)
        loss_ref[...] += jnp.broadcast_to(sq * (0.5 / D_MODEL), loss_ref.shape)
        dout = err * (1.0 / D_MODEL)
        dg_ref[...] += jnp.sum(dout * xh, axis=0, keepdims=True)
        dx_ref[...] = _rms_bwd(dout, g, r, xh)

    return pl.pallas_call(
        body,
        name="final_loss",
        grid=(seq // tm,),
        in_specs=[pl.BlockSpec((tm, D_MODEL), lambda i: (i, 0)), _full((1, D_MODEL)), pl.BlockSpec((tm, D_MODEL), lambda i: (i, 0))],
        out_specs=[pl.BlockSpec((tm, D_MODEL), lambda i: (i, 0)), _full((1, D_MODEL)), _full((8, LANES))],
        out_shape=[jax.ShapeDtypeStruct((seq, D_MODEL), F32), jax.ShapeDtypeStruct((1, D_MODEL), F32), jax.ShapeDtypeStruct((8, LANES), F32)],
        compiler_params=_cparams("arbitrary"),
    )(x, g, tgt)


C_WIDTH = 512
C_HEADS = 8
C_HDIM = 64
C_PAIRS = C_HEADS // 2
C_BQ = 512
C_TAIL = 16
C_KG = 4


def _split3(x):
    hi = x.astype(BF16)
    r = x - hi.astype(F32)
    mid = r.astype(BF16)
    return hi, mid, (r - mid.astype(F32)).astype(BF16)


def _piece_selectors():
    sel = np.zeros((C_HEADS, 3 * LANES, LANES), np.float32)
    for p in range(C_PAIRS):
        for e in range(2):
            for t in range(3):
                sel[2 * p + e, t * LANES + 2 * p + e, 3 * e + t] = -1.0
    return sel


def fox_prep(proj, bf_row):
    seq = proj.shape[0]
    nblk = seq // CHUNK
    tril = jnp.asarray(np.tril(np.ones((CHUNK, CHUNK), np.float32)), BF16)
    sel = jnp.asarray(_piece_selectors(), BF16)
    rows_t = CHUNK + C_TAIL

    def body(fl_ref, q_ref, k_ref, v_ref, bf_ref, l_ref, sel_ref, ka_ref, va_ref, vt_ref, kt_ref, qt_ref, qa_ref, carry_ref):
        @pl.when(pl.program_id(0) == 0)
        def _():
            carry_ref[...] = jnp.zeros_like(carry_ref)

        lf = jax.nn.log_sigmoid(fl_ref[:, :LANES] + bf_ref[...])
        c = _exact_times(l_ref[...], lf, 3) + carry_ref[...]
        carry_ref[...] += jnp.sum(lf, axis=0, keepdims=True)
        c3 = jnp.concatenate(_split3(c), axis=1)
        lane = lax.broadcasted_iota(jnp.int32, (CHUNK, LANES), 1)
        row = lax.broadcasted_iota(jnp.int32, (CHUNK, LANES), 0)
        r16 = lax.broadcasted_iota(jnp.int32, (C_TAIL, 2 * CHUNK), 0)
        l16 = lax.broadcasted_iota(jnp.int32, (C_TAIL, 2 * CHUNK), 1)
        zero = jnp.zeros((CHUNK, LANES), BF16)
        one = jnp.ones((CHUNK, LANES), BF16)

        def by_keys(x, right_a, right_b):
            xb = x.astype(BF16)
            top = jnp.concatenate([jnp.where(lane < C_HDIM, xb, zero), right_a], axis=1)
            return jnp.concatenate([top, jnp.concatenate([jnp.where(lane < C_HDIM, zero, xb), right_b], axis=1)], axis=0)

        def by_lanes(x, tail):
            xt = x.T.astype(BF16)
            main = jnp.concatenate([jnp.where(row < C_HDIM, xt, zero), jnp.where(row < C_HDIM, zero, xt)], axis=1)
            return jnp.concatenate([main, tail], axis=0)

        for p in range(C_PAIRS):
            cols = slice(p * LANES, (p + 1) * LANES)
            q2, k2, v2 = q_ref[:, cols] * (C_HDIM ** -0.5), k_ref[:, cols], v_ref[:, cols]
            negc = [_dot(c3, sel_ref[2 * p + e]).astype(BF16) for e in range(2)]
            ones3 = [jnp.where((lane >= 3 * e) & (lane < 3 * e + 3), one, zero) for e in range(2)]
            tail = jnp.where(((r16 == 2 * p) & (l16 < CHUNK)) | ((r16 == 2 * p + 1) & (l16 >= CHUNK)), 1.0, 0.0).astype(BF16)
            ka_ref[p] = by_keys(k2, negc[0], negc[1])
            va_ref[p] = by_keys(v2, ones3[0], ones3[1])
            kt_ref[p] = by_lanes(k2, tail)
            vt_ref[p] = by_lanes(v2, tail)
            qt_ref[p] = jnp.concatenate([q2.T.astype(BF16), jnp.where(row < 6, one, zero)], axis=0)
            qa_ref[p] = jnp.concatenate([q2.astype(BF16), jnp.where((lane == 2 * p) | (lane == 2 * p + 1), one, zero)], axis=1)

    wide = lambda j: pl.BlockSpec((CHUNK, C_WIDTH), lambda n: (n, j))
    by_rows = pl.BlockSpec((C_PAIRS, 2 * CHUNK, 2 * CHUNK), lambda n: (0, n, 0))
    by_cols = pl.BlockSpec((C_PAIRS, rows_t, 2 * CHUNK), lambda n: (0, 0, n))
    return pl.pallas_call(
        body,
        name="fox_prep",
        grid=(nblk,),
        in_specs=[pl.BlockSpec((CHUNK, 256), lambda n: (n, 3)), wide(4), wide(5), wide(6), _full((1, LANES)),
                  _full((CHUNK, CHUNK)), _full((C_HEADS, 3 * LANES, LANES))],
        out_specs=[by_rows, by_rows, by_cols, by_cols,
                   pl.BlockSpec((C_PAIRS, 2 * CHUNK, CHUNK), lambda n: (0, 0, n)),
                   pl.BlockSpec((C_PAIRS, CHUNK, 2 * CHUNK), lambda n: (0, n, 0))],
        out_shape=[jax.ShapeDtypeStruct((C_PAIRS, 2 * seq, 2 * CHUNK), BF16)] * 2
        + [jax.ShapeDtypeStruct((C_PAIRS, rows_t, 2 * seq), BF16)] * 2
        + [jax.ShapeDtypeStruct((C_PAIRS, 2 * CHUNK, seq), BF16), jax.ShapeDtypeStruct((C_PAIRS, seq, 2 * CHUNK), BF16)],
        scratch_shapes=[pltpu.VMEM((1, LANES), F32)],
        compiler_params=_cparams("arbitrary"),
    )(proj, proj, proj, proj, bf_row, tril, sel)


def _visible(shape, key0, query0):
    row = lax.broadcasted_iota(jnp.int32, shape, 0)
    key = key0 + lax.shift_left(lax.shift_right_logical(row, 8), 7) + (row & (CHUNK - 1))
    return key <= query0 + lax.broadcasted_iota(jnp.int32, shape, 1)


def _rows_ab(a, b, n):
    return jnp.concatenate([jnp.broadcast_to(a, (C_HDIM, n)), jnp.broadcast_to(b, (C_HDIM, n))], axis=0)


def _call_carrying(ex, body, operands, *, name, grid, in_specs, out_specs, out_shape, scratch_shapes):
    if ex is None:
        return pl.pallas_call(body, name=name, grid=grid, in_specs=in_specs, out_specs=out_specs, out_shape=out_shape,
                              scratch_shapes=scratch_shapes, compiler_params=_cparams("parallel", *["arbitrary"] * (len(grid) - 1)),
                              )(*operands)
    n_in, n_out = len(in_specs), len(out_specs)

    def wrapped(*refs):
        own, parts = _carried_refs(refs, n_in, n_out, ex)
        ids = [pl.program_id(a) for a in range(len(grid))]
        pl.when(functools.reduce(jnp.logical_and, [i == 0 for i in ids]))(lambda: ex.start(*parts))
        body(*own)
        pl.when(functools.reduce(jnp.logical_and, [i == g - 1 for i, g in zip(ids, grid)]))(lambda: ex.finish(*parts))

    return pl.pallas_call(
        wrapped, name=name, grid=grid,
        in_specs=list(in_specs) + [_ANY] * len(ex.inputs), out_specs=list(out_specs) + [_ANY] * len(ex.out_shape),
        out_shape=list(out_shape) + list(ex.out_shape), scratch_shapes=list(scratch_shapes) + list(ex.scratch),
        input_output_aliases={n_in + i: n_out + o for i, o in getattr(ex, "aliases", {}).items()},
        compiler_params=_cparams(*["arbitrary"] * len(grid)),
    )(*operands, *ex.inputs)


def fox_fwd(qt, ka, vt, carried=None):
    seq = qt.shape[2]
    nblk = seq // CHUNK
    bq = min(C_BQ, seq)
    grp = bq // CHUNK
    rows_t = CHUNK + C_TAIL

    def body(qt_ref, ka_ref, vt_ref, o_ref, lse_ref, acc_ref, s_ref):
        p, i = pl.program_id(0), pl.program_id(1)
        qtile = qt_ref[0]
        r16 = lax.broadcasted_iota(jnp.int32, (C_TAIL, bq), 0)

        def scores(t):
            at = pl.multiple_of(t * grp * 2 * CHUNK, 2 * CHUNK)
            return _dot(ka_ref[0, pl.ds(at, grp * 2 * CHUNK), :], qtile)

        def group(t, m, masked):
            ma, mb = m
            at = pl.multiple_of(t * grp * 2 * CHUNK, 2 * CHUNK)
            s = s_ref[...]
            if masked:
                s = jnp.where(_visible(s.shape, t * bq, i * bq), s, -jnp.inf)
            sa = [s[g * 2 * CHUNK:g * 2 * CHUNK + CHUNK] for g in range(grp)]
            sb = [s[g * 2 * CHUNK + CHUNK:(g + 1) * 2 * CHUNK] for g in range(grp)]
            na, nb = ma, mb
            for g in range(grp):
                na = jnp.maximum(na, jnp.max(sa[g], axis=0, keepdims=True))
                nb = jnp.maximum(nb, jnp.max(sb[g], axis=0, keepdims=True))
            al_a, al_b = jnp.exp(ma - na), jnp.exp(mb - nb)
            pt = jnp.concatenate([jnp.exp(x - n) for g in range(grp) for x, n in ((sa[g], na), (sb[g], nb))], axis=0)
            pv = _dot(vt_ref[0, :, pl.ds(at, grp * 2 * CHUNK)], pt.astype(BF16))
            tail = jnp.where(r16 == 2 * p, al_a, jnp.where(r16 == 2 * p + 1, al_b, 1.0))
            acc_ref[...] = acc_ref[...] * jnp.concatenate([_rows_ab(al_a, al_b, bq), tail], axis=0) + pv
            return na, nb

        def step(t, m):
            s_next = scores(t + 1)
            m = group(t, m, False)
            s_ref[...] = s_next
            return m

        acc_ref[...] = jnp.zeros_like(acc_ref)
        s_ref[...] = scores(0)
        m = (jnp.full((1, bq), -jnp.inf, F32), jnp.full((1, bq), -jnp.inf, F32))
        m = lax.fori_loop(0, i, step, m)
        ma, mb = group(i, m, True)
        tailv = acc_ref[CHUNK:rows_t, :]
        la = jnp.sum(jnp.where(r16 == 2 * p, tailv, 0.0), axis=0, keepdims=True)
        lb = jnp.sum(jnp.where(r16 == 2 * p + 1, tailv, 0.0), axis=0, keepdims=True)
        o_ref[...] = (acc_ref[0:CHUNK, :] * _rows_ab(1.0 / la, 1.0 / lb, bq)).T
        lse_ref[0, 0:1, :] = ma + jnp.log(la)
        lse_ref[0, 1:2, :] = mb + jnp.log(lb)

    return _call_carrying(
        carried, body, (qt, ka, vt),
        name="fox_fwd",
        grid=(C_PAIRS, seq // bq),
        in_specs=[
            pl.BlockSpec((1, 2 * CHUNK, bq), lambda p, i: (p, 0, i)),
            pl.BlockSpec((1, 2 * seq, 2 * CHUNK), lambda p, i: (p, 0, 0)),
            pl.BlockSpec((1, rows_t, 2 * seq), lambda p, i: (p, 0, 0)),
        ],
        out_specs=[pl.BlockSpec((bq, LANES), lambda p, i: (i, p)), pl.BlockSpec((1, 2, bq), lambda p, i: (p, 0, i))],
        out_shape=[jax.ShapeDtypeStruct((seq, C_WIDTH), F32), jax.ShapeDtypeStruct((C_PAIRS, 2, seq), F32)],
        scratch_shapes=[pltpu.VMEM((rows_t, bq), F32), pltpu.VMEM((grp * 2 * CHUNK, bq), F32)],
    )


def fox_bwd_prep(dy, o, proj):
    seq = o.shape[0]
    ind = np.zeros((C_WIDTH, LANES), np.float32)
    for h in range(C_HEADS):
        ind[h * C_HDIM:(h + 1) * C_HDIM, h] = 1.0
    ind = jnp.asarray(ind, BF16)
    sel = _piece_selectors()
    sel = jnp.asarray(np.stack([sel[2 * p].T + sel[2 * p + 1].T for p in range(C_PAIRS)]), BF16)

    def body(dy_ref, o_ref, z_ref, ind_ref, sel_ref, do_ref, dz_ref, dot_ref):
        dy_c, o_v, z = dy_ref[...], o_ref[...], z_ref[...]
        sg = jax.nn.sigmoid(z)
        do = dy_c * (z * sg)
        do_ref[...] = do.astype(BF16)
        dz_ref[...] = (dy_c * o_v * (sg * (1.0 + z * (1.0 - sg)))).astype(BF16)
        prod = do * o_v
        hi = prod.astype(BF16)
        lo = (prod - hi.astype(F32)).astype(BF16)
        delta = _dot(hi, ind_ref[...]) + _dot(lo, ind_ref[...])
        d3 = jnp.concatenate(_split3(delta.T), axis=0)
        for p in range(C_PAIRS):
            tail = _dot(sel_ref[p], d3).astype(BF16)
            dot_ref[p] = jnp.concatenate([do[:, p * LANES:(p + 1) * LANES].T.astype(BF16), tail], axis=0)

    return pl.pallas_call(
        body,
        name="fox_bwd_prep",
        grid=(seq // CHUNK,),
        in_specs=[
            pl.BlockSpec((CHUNK, C_WIDTH), lambda i: (i, 1)),
            pl.BlockSpec((CHUNK, C_WIDTH), lambda i: (i, 0)),
            pl.BlockSpec((CHUNK, C_WIDTH), lambda i: (i, 7)),
            _full((C_WIDTH, LANES)), _full((C_PAIRS, LANES, 3 * LANES)),
        ],
        out_specs=[
            pl.BlockSpec((CHUNK, C_WIDTH), lambda i: (i, 0)),
            pl.BlockSpec((CHUNK, C_WIDTH), lambda i: (i, 0)),
            pl.BlockSpec((C_PAIRS, 2 * CHUNK, CHUNK), lambda i: (0, 0, i)),
        ],
        out_shape=[jax.ShapeDtypeStruct((seq, C_WIDTH), BF16)] * 2 + [jax.ShapeDtypeStruct((C_PAIRS, 2 * CHUNK, seq), BF16)],
        compiler_params=_cparams("parallel"),
    )(dy, o, proj, ind, sel)


def fox_bwd(ka, va, kt, qt, dot_t, qa, dob, lse, carried=None):
    seq = qt.shape[2]
    nblk = seq // CHUNK
    bq = min(C_BQ, seq)
    nq = seq // bq
    kg = min(C_KG, nblk)
    ng = nblk // kg
    rows_t = CHUNK + C_TAIL

    def body(ka_ref, va_ref, kt_ref, qt_ref, dot_ref, qa_ref, do_ref, lse_ref,
             dq_ref, dk_ref, dv_ref, dck_ref, dcq_ref, dqt_acc, dv_acc, dka_acc):
        p, jg = pl.program_id(0), pl.program_id(1)

        @pl.when(jg == 0)
        def _():
            dqt_acc[...] = jnp.zeros_like(dqt_acc)

        dv_acc[...] = jnp.zeros_like(dv_acc)
        dka_acc[...] = jnp.zeros_like(dka_acc)

        def step(i, carry, masked):
            cols = pl.ds(pl.multiple_of(i * bq, bq), bq)
            qtile, dotile = qt_ref[0, :, cols], dot_ref[0, :, cols]
            do, qa_i = do_ref[cols, :], qa_ref[0, cols, :]
            lse2 = jnp.concatenate([jnp.broadcast_to(lse_ref[0, 0:1, cols], (CHUNK, bq)),
                                    jnp.broadcast_to(lse_ref[0, 1:2, cols], (CHUNK, bq))] * kg, axis=0)
            pt = jnp.exp(_dot(ka_ref[0], qtile) - lse2)
            if masked:
                pt = jnp.where(_visible(pt.shape, jg * kg * CHUNK, i * bq), pt, 0.0)
            ds = pt * _dot(va_ref[0], dotile)
            ptb, dsb = pt.astype(BF16), ds.astype(BF16)
            dv_acc[...] += _dot(ptb, do)
            dka_acc[...] += _dot(dsb, qa_i)
            dqt_acc[:, cols] += _dot(kt_ref[0], dsb)
            return carry

        i0 = (jg * kg * CHUNK) // bq
        step(i0, 0, True)
        lax.fori_loop(i0 + 1, nq, functools.partial(step, masked=False), 0)
        lane = lax.broadcasted_iota(jnp.int32, (CHUNK, LANES), 1)
        for kb in range(kg):
            rows = slice(kb * CHUNK, (kb + 1) * CHUNK)
            ra = slice(kb * 2 * CHUNK, kb * 2 * CHUNK + CHUNK)
            rb = slice(kb * 2 * CHUNK + CHUNK, (kb + 1) * 2 * CHUNK)
            dk_ref[rows, :] = jnp.where(lane < C_HDIM, dka_acc[ra, 0:LANES], dka_acc[rb, 0:LANES]).astype(BF16)
            dv_ref[rows, :] = jnp.where(lane < C_HDIM, dv_acc[ra, :], dv_acc[rb, :]).astype(BF16)
            dck_ref[0, rows, :] = (jnp.where(lane == 2 * p, dka_acc[ra, LANES:], 0.0)
                                   + jnp.where(lane == 2 * p + 1, dka_acc[rb, LANES:], 0.0))

        @pl.when(jg == ng - 1)
        def _():
            for c in range(nq):
                dq_ref[c * bq:(c + 1) * bq, :] = (dqt_acc[0:CHUNK, c * bq:(c + 1) * bq].T * (C_HDIM ** -0.5)).astype(BF16)
            dcq_ref[0] = dqt_acc[CHUNK:rows_t, :]

    per_pair = lambda r, c: pl.BlockSpec((1, r, c), lambda p, j: (p, 0, 0))
    by_rows = pl.BlockSpec((1, kg * 2 * CHUNK, 2 * CHUNK), lambda p, j: (p, j, 0))
    by_cols = pl.BlockSpec((1, rows_t, kg * 2 * CHUNK), lambda p, j: (p, 0, j))
    return _call_carrying(
        carried, body, (ka, va, kt, qt, dot_t, qa, dob, lse),
        name="fox_bwd",
        grid=(C_PAIRS, ng),
        in_specs=[by_rows, by_rows, by_cols, per_pair(2 * CHUNK, seq), per_pair(2 * CHUNK, seq),
                  per_pair(seq, 2 * CHUNK), pl.BlockSpec((seq, LANES), lambda p, j: (0, p)), per_pair(2, seq)],
        out_specs=[pl.BlockSpec((seq, LANES), lambda p, j: (0, p)),
                   pl.BlockSpec((kg * CHUNK, LANES), lambda p, j: (j, p)),
                   pl.BlockSpec((kg * CHUNK, LANES), lambda p, j: (j, p)),
                   pl.BlockSpec((1, kg * CHUNK, LANES), lambda p, j: (p, j, 0)),
                   per_pair(C_TAIL, seq)],
        out_shape=[jax.ShapeDtypeStruct((seq, C_WIDTH), BF16)] * 3
        + [jax.ShapeDtypeStruct((C_PAIRS, seq, LANES), F32), jax.ShapeDtypeStruct((C_PAIRS, C_TAIL, seq), F32)],
        scratch_shapes=[pltpu.VMEM((rows_t, seq), F32), pltpu.VMEM((kg * 2 * CHUNK, LANES), F32),
                        pltpu.VMEM((kg * 2 * CHUNK, 2 * CHUNK), F32)],
    )


def fox_post(dcq, dck, proj, bf_row):
    seq = proj.shape[0]
    nc = seq // CHUNK
    triu = jnp.asarray(np.triu(np.ones((CHUNK, CHUNK), np.float32)), BF16)

    def body(dq_ref, dk_ref, fl_ref, bf_ref, u_ref, dfl_ref, dbf_ref, carry_ref):
        @pl.when(pl.program_id(0) == 0)
        def _():
            carry_ref[...] = jnp.zeros_like(carry_ref)
            dbf_ref[...] = jnp.zeros_like(dbf_ref)

        rows = (dq_ref[0] + dq_ref[1]) + (dq_ref[2] + dq_ref[3])
        dc = jnp.concatenate([rows, jnp.zeros((CHUNK - C_TAIL, CHUNK), F32)], axis=0).T
        dc = dc - ((dk_ref[0] + dk_ref[1]) + (dk_ref[2] + dk_ref[3]))
        g = _exact_times(u_ref[...], dc, 3) + carry_ref[...]
        carry_ref[...] += jnp.sum(dc, axis=0, keepdims=True)
        dfl = g * jax.nn.sigmoid(-(fl_ref[:, :LANES] + bf_ref[...]))
        dbf_ref[...] += jnp.sum(dfl, axis=0, keepdims=True)
        dfl_ref[...] = jnp.concatenate([dfl, jnp.zeros_like(dfl)], axis=1).astype(BF16)

    rev = lambda n: nc - 1 - n
    return pl.pallas_call(
        body,
        name="fox_post",
        grid=(nc,),
        in_specs=[
            pl.BlockSpec((C_PAIRS, C_TAIL, CHUNK), lambda n: (0, 0, rev(n))),
            pl.BlockSpec((C_PAIRS, CHUNK, LANES), lambda n: (0, rev(n), 0)),
            pl.BlockSpec((CHUNK, 256), lambda n: (rev(n), 3)),
            _full((1, LANES)), _full((CHUNK, CHUNK)),
        ],
        out_specs=[pl.BlockSpec((CHUNK, 256), lambda n: (rev(n), 0)), _full((1, LANES))],
        out_shape=[jax.ShapeDtypeStruct((seq, 256), BF16), jax.ShapeDtypeStruct((1, LANES), F32)],
        scratch_shapes=[pltpu.VMEM((1, LANES), F32)],
        compiler_params=_cparams("arbitrary"),
    )(dcq, dck, proj, bf_row, triu)


N_DEV = 8
MESH = pl.DeviceIdType.MESH
_ANY = pl.BlockSpec(memory_space=pl.ANY)


def _mesh_pos():
    return lax.axis_index("x"), lax.axis_index("y"), lax.axis_index("c")


def _dev_index(px, py, pc):
    return 4 * px + 2 * py + pc


def _row_pieces(ref, rows):
    return [ref.at[idx + (pl.ds(r, rows),)] for idx in np.ndindex(*ref.shape[:-2]) for r in range(0, ref.shape[-2], rows)]


class _Transfer:
    def __init__(self, src, dst, rows, send_sem, recv_sem, to):
        self.src, self.dst, self.rows, self.sems, self.to = src, dst, rows, (send_sem, recv_sem), to

    def _copy(self, src, dst):
        return pltpu.make_async_remote_copy(src_ref=src, dst_ref=dst, send_sem=self.sems[0], recv_sem=self.sems[1],
                                            device_id=self.to, device_id_type=MESH)

    def start(self):
        for s, d in zip(_row_pieces(self.src, self.rows), _row_pieces(self.dst, self.rows), strict=True):
            self._copy(s, d).start()

    def wait_send(self):
        self._copy(self.src, self.dst).wait_send()

    def wait_recv(self):
        self._copy(self.src, self.dst).wait_recv()


def _exchange_call(ex, name):
    n_in, n_out = len(ex.inputs), len(ex.out_shape)

    def body(*refs):
        parts = refs[:n_in], refs[n_in:n_in + n_out], refs[n_in + n_out:]
        ex.start(*parts)
        ex.finish(*parts)

    return pl.pallas_call(body, name=name, in_specs=[_ANY] * n_in, out_specs=[_ANY] * n_out, out_shape=ex.out_shape,
                          scratch_shapes=ex.scratch, input_output_aliases=getattr(ex, "aliases", {}))(*ex.inputs)


def _carried_refs(refs, n_in, n_out, ex):
    k_in, k_out, k_sem = (len(ex.inputs), len(ex.out_shape), len(ex.scratch)) if ex else (0, 0, 0)
    a, b, c = n_in + k_in, n_in + k_in + n_out, n_in + k_in + n_out + k_out
    own = refs[:n_in] + refs[a:b] + refs[c:len(refs) - k_sem]
    return own, (refs[n_in:a], refs[b:c], refs[len(refs) - k_sem:])


class AllGatherWeights:
    piece_rows = (128, 64)

    def __init__(self, wi, wo):
        self.inputs = (wi, wo)
        self.out_shape = [jax.ShapeDtypeStruct((N_DEV,) + wi.shape, wi.dtype), jax.ShapeDtypeStruct((N_DEV,) + wo.shape, wo.dtype)]
        self.scratch = [pltpu.SemaphoreType.DMA((2, 7)), pltpu.SemaphoreType.DMA((2, 7)), pltpu.SemaphoreType.DMA((2,))]

    def _plan(self, ins, outs, sems):
        send_sems, recv_sems, local_sems = sems
        x, y, c = _mesh_pos()
        me, sibling = (x, y, c), (x, y, 1 - c)
        chips = [(1 - x, y), (x, 1 - y), (1 - x, 1 - y)]
        both = range(2)

        def copy(a, k, block, to, own=False):
            slot = outs[a].at[_dev_index(*block)]
            return _Transfer(ins[a] if own else slot, slot, self.piece_rows[a], send_sems.at[a, k], recv_sems.at[a, k], to)

        mine = [pltpu.make_async_copy(ins[a], outs[a].at[_dev_index(*me)], local_sems.at[a]) for a in both]
        first = [copy(a, 1 + j, me, (*chip, c), own=True) for j, chip in enumerate(chips) for a in both]
        first += [copy(a, 0, me, sibling, own=True) for a in both]
        passed = [copy(a, 4 + j, (*chip, c), sibling) for j, chip in enumerate(chips) for a in both]
        return me, sibling, chips, c, copy, mine, first, passed

    def start(self, ins, outs, sems):
        *_, mine, first, _ = self._plan(ins, outs, sems)
        for cp in mine + first:
            cp.start()

    def finish(self, ins, outs, sems):
        me, sibling, chips, c, copy, mine, first, passed = self._plan(ins, outs, sems)
        for j, chip in enumerate(chips):
            for a in range(2):
                copy(a, 1 + j, (*chip, c), me).wait_recv()
            for a in range(2):
                passed[2 * j + a].start()
        for a in range(2):
            copy(a, 0, sibling, me).wait_recv()
        for j, chip in enumerate(chips):
            for a in range(2):
                copy(a, 4 + j, (*chip, 1 - c), me).wait_recv()
        for cp in first + passed:
            cp.wait_send()
        for cp in mine:
            cp.wait()


N_CHIP = 4


class PairExchange:
    def __init__(self, by_core, whole=()):
        self.inputs = tuple(by_core) + tuple(whole)
        self.n_by_core = len(by_core)
        self.out_shape = ([jax.ShapeDtypeStruct(a.shape[1:], a.dtype) for a in by_core]
                          + [jax.ShapeDtypeStruct(a.shape, a.dtype) for a in whole])
        n = len(self.inputs)
        self.scratch = [pltpu.SemaphoreType.DMA((n,)), pltpu.SemaphoreType.DMA((n,))]

    def _copies(self, ins, outs, sems):
        x, y, c = _mesh_pos()
        srcs = [r.at[1 - c] if a < self.n_by_core else r for a, r in enumerate(ins)]
        return [_Transfer(srcs[a], outs[a], outs[a].shape[-2], sems[0].at[a], sems[1].at[a], (x, y, 1 - c))
                for a in range(len(ins))]

    def start(self, ins, outs, sems):
        for cp in self._copies(ins, outs, sems):
            cp.start()

    def finish(self, ins, outs, sems):
        copies = self._copies(ins, outs, sems)
        for cp in copies:
            cp.wait_recv()
        for cp in copies:
            cp.wait_send()


def pair_sum(own, other, dtype, rows, name, core, layer, depth, stacked=None):
    n, n_r, n_c = other.shape

    def body(core_ref, a_ref, b_ref, *refs):
        refs[-1][0, 0] = (a_ref[0, 0] + b_ref[0]).astype(dtype)

    carried = () if stacked is None else (stacked,)
    grid_spec = pltpu.PrefetchScalarGridSpec(
        num_scalar_prefetch=1,
        grid=(n, n_r // rows),
        in_specs=[pl.BlockSpec((1, 1, rows, n_c), lambda i, r, s: (s[0], i, r, 0)),
                  pl.BlockSpec((1, rows, n_c), lambda i, r, s: (i, r, 0))] + [_ANY] * len(carried),
        out_specs=pl.BlockSpec((1, 1, rows, n_c), lambda i, r, s: (i, layer, r, 0)),
    )
    return pl.pallas_call(
        body,
        name=name,
        grid_spec=grid_spec,
        out_shape=jax.ShapeDtypeStruct((n, depth, n_r, n_c), dtype),
        input_output_aliases={3: 0} if carried else {},
        compiler_params=_cparams("parallel", "parallel"),
    )(core, own, other, *carried)


def small_sum(a, b):
    def body(a_ref, b_ref, o_ref):
        o_ref[...] = a_ref[...] + b_ref[...]

    return pl.pallas_call(body, name="pair_sum_small", out_shape=jax.ShapeDtypeStruct(a.shape, a.dtype))(a, b)


class ChipExchange:
    def __init__(self, by_chip, layer, gathered=(), stacked=None):
        stacked = tuple(stacked or ())
        self.inputs = tuple(by_chip) + tuple(gathered) + stacked
        self.n_by_chip, self.n_gathered, self.layer = len(by_chip), len(gathered), layer
        self.out_shape = ([jax.ShapeDtypeStruct((N_CHIP - 1,) + a.shape[1:], a.dtype) for a in by_chip]
                          + [jax.ShapeDtypeStruct((N_CHIP,) + a.shape, a.dtype) for a in gathered])
        self.aliases = {self.n_by_chip + self.n_gathered + i: i for i in range(len(stacked))}
        n = self.n_by_chip + self.n_gathered
        self.scratch = [pltpu.SemaphoreType.DMA((n, 3)), pltpu.SemaphoreType.DMA((n, 3)),
                        pltpu.SemaphoreType.DMA((max(self.n_gathered, 1),))]

    def _plan(self, ins, outs, sems):
        x, y, c = _mesh_pos()
        chip = 2 * x + y
        n = self.n_by_chip + self.n_gathered

        def copy(a, k, sending):
            px, py = x ^ ((k >> 1) & 1), y ^ (k & 1)
            if a < self.n_by_chip:
                src, dst = ins[a].at[2 * px + py, self.layer], outs[a].at[k - 1, self.layer]
            else:
                src, dst = ins[a], outs[a].at[chip if sending else 2 * px + py]
            return _Transfer(src, dst, dst.shape[-2], sems[0].at[a, k - 1], sems[1].at[a, k - 1], (px, py, c))

        local = [pltpu.make_async_copy(ins[a], outs[a].at[chip], sems[2].at[a - self.n_by_chip])
                 for a in range(self.n_by_chip, n)]
        return n, copy, local

    def start(self, ins, outs, sems):
        n, copy, local = self._plan(ins, outs, sems)
        for cp in local:
            cp.start()
        for k in range(1, N_CHIP):
            for a in range(n):
                copy(a, k, True).start()

    def finish(self, ins, outs, sems):
        n, copy, local = self._plan(ins, outs, sems)
        for k in range(1, N_CHIP):
            for a in range(n):
                copy(a, k, False).wait_recv()
        for k in range(1, N_CHIP):
            for a in range(n):
                copy(a, k, True).wait_send()
        for cp in local:
            cp.wait()


ADAM_LR = 0.001
ADAM_B1 = 0.9
ADAM_B2 = 0.999
ADAM_EPS = 1e-08
ADAM_WD = 0.01
ADAM_STEP = 10


def adam_reduce(parts, w, m, v, rows, name, own=None, chip=None):
    n_l, n_r, n_c = w.shape
    n_parts = parts.shape[0]

    def body(*refs):
        p_ref, w_ref, m_ref, v_ref, g_ref, d_ref, m2_ref, v2_ref = refs[-8:]
        g = p_ref[0, 0].astype(F32)
        if own is not None:
            g = refs[-9][...].reshape(rows, n_c).astype(F32) + g
        for d in range(1, n_parts):
            g = g + p_ref[d, 0].astype(F32)
        m2 = ADAM_B1 * m_ref[0] + (1.0 - ADAM_B1) * g
        v2 = ADAM_B2 * v_ref[0] + (1.0 - ADAM_B2) * (g * g)
        m_hat = m2 / (1.0 - ADAM_B1 ** ADAM_STEP)
        v_hat = v2 / (1.0 - ADAM_B2 ** ADAM_STEP)
        g_ref[0] = g
        d_ref[0] = -ADAM_LR * (m_hat / (jnp.sqrt(v_hat) + ADAM_EPS) + ADAM_WD * w_ref[0])
        m2_ref[0] = m2
        v2_ref[0] = v2

    blk = lambda: pl.BlockSpec((1, rows, n_c), lambda l, r, *_: (l, r, 0))
    in_specs = [pl.BlockSpec((n_parts, 1, rows, n_c), lambda l, r, *_: (0, l, r, 0)), blk(), blk(), blk()]
    args = (parts, w, m, v)
    if own is not None:
        in_specs = [pl.BlockSpec((1, 1, rows, n_c), lambda l, r, s: (s[0], l, r, 0))] + in_specs
        args = (chip, own) + args
    grid_spec = pltpu.PrefetchScalarGridSpec(
        num_scalar_prefetch=0 if own is None else 1, grid=(n_l, n_r // rows), in_specs=in_specs,
        out_specs=[blk(), blk(), blk(), blk()])
    return pl.pallas_call(
        body,
        name=name,
        grid_spec=grid_spec,
        out_shape=[jax.ShapeDtypeStruct(w.shape, F32)] * 4,
        compiler_params=_cparams("parallel", "parallel"),
    )(*args)


_SMALL = (("norm_g", (2, 1024)), ("gmlp_ln_g", (2, 4, 64)), ("gmlp_ln_b", (2, 4, 64)), ("gmlp_w_s", (2, 4, 128, 128)),
          ("gmlp_b_s", (2, 4, 128)), ("hgrn_lb", (2, 256)), ("hgrn_onorm_g", (2, 64)), ("fox_b_f", (2, 8)),
          ("final_norm_g", (1024,)), ("loss", ()))


def _padded(n):
    return -(-n // LANES) * LANES


_SMALL_ROWS = -(-sum(_padded(int(np.prod(s))) for _, s in _SMALL) // LANES // 8) * 8


def _pack_small(vals):
    flat = []
    for (name, shape), a in zip(_SMALL, vals, strict=True):
        n = int(np.prod(shape))
        flat.append(jnp.pad(a.reshape(n).astype(F32), (0, _padded(n) - n)))
    flat = jnp.concatenate(flat)
    return jnp.pad(flat, (0, _SMALL_ROWS * LANES - flat.shape[0])).reshape(_SMALL_ROWS, LANES)


def _unpack_small(slab):
    flat, out, at = slab.reshape(-1), {}, 0
    for name, shape in _SMALL:
        n = int(np.prod(shape))
        out[name] = flat[at:at + n].reshape(shape)
        at += _padded(n)
    return out


def kernel(x, norm_g, w_in, w_out, gmlp_ln_g, gmlp_ln_b, gmlp_w_s, gmlp_b_s, hgrn_lb, hgrn_onorm_g, fox_b_f, final_norm_g, loss_target, m_norm_g, m_w_in, m_w_out, m_gmlp_ln_g, m_gmlp_ln_b, m_gmlp_w_s, m_gmlp_b_s, m_hgrn_lb, m_hgrn_onorm_g, m_fox_b_f, m_final_norm_g, v_norm_g, v_w_in, v_w_out, v_gmlp_ln_g, v_gmlp_ln_b, v_gmlp_w_s, v_gmlp_b_s, v_hgrn_lb, v_hgrn_onorm_g, v_fox_b_f, v_final_norm_g):
    depth = w_in.shape[0]
    seq = x.shape[1]
    assert w_in.shape[2] * N_DEV == N_IN
    xs, tgt = x[0], loss_target[0]

    wi_blk, wo_blk = w_in.astype(BF16), w_out.astype(BF16)
    wi_all, wo_all = _exchange_call(AllGatherWeights(wi_blk[0], wo_blk[0]), "allgather_weights_0")

    ln_g = gmlp_ln_g.reshape(depth, 1, A_WIDTH)
    ln_b = gmlp_ln_b.reshape(depth, 1, A_WIDTH)
    bs_t = jnp.pad(jnp.transpose(gmlp_b_s, (0, 2, 1)), ((0, 0), (0, 0), (0, LANES - A_GROUPS)))
    lb0, lb1 = hgrn_lb[0:1], hgrn_lb[1:2]
    onorm = jnp.tile(hgrn_onorm_g, (1, B_HEADS)).reshape(depth, 1, B_WIDTH)
    bf_row = jnp.pad(fox_b_f, ((0, 0), (0, LANES - C_HEADS))).reshape(depth, 1, LANES)

    core = lax.axis_index("c").astype(jnp.int32).reshape(1)
    chip = (2 * lax.axis_index("x") + lax.axis_index("y")).astype(jnp.int32).reshape(1)

    saved = []
    xc = xs
    for l in range(depth):
        wi_int = assemble_w_in(wi_all[:, None])
        proj, h = inproj(xc, norm_g[l:l + 1], wi_int, 0)
        ya = gmlp_fwd(proj, ln_g[l], ln_b[l], gmlp_w_s[l], bs_t[l])
        yb, states = hgrn_fwd(proj, lb0, lb1, onorm[l], l)
        ka, va, vt, kt, qt, qa = fox_prep(proj, bf_row[l])
        nxt = AllGatherWeights(wi_blk[l + 1], wo_blk[l + 1]) if l + 1 < depth else None
        o, lse, *gathered = fox_fwd(qt, ka, vt, nxt)
        xn, yfull = outproj(xc, ya, yb, o, proj, wo_all[:, None], 0)
        saved.append((xc, proj, h, states, ka, va, kt, qt, qa, o, lse, yfull, wi_int, wo_all))
        if gathered:
            wi_all, wo_all = gathered
        xc = xn

    dx, d_final_g, loss_tile = final_loss(xc, final_norm_g[None], tgt)

    n_shard = w_in.shape[2]
    g_norm = [None] * depth
    g_ln_g, g_ln_b, g_ws, g_bs, g_on, g_bf = ([None] * depth for _ in range(6))
    g_lb0, g_lb1 = jnp.zeros_like(lb0), jnp.zeros_like(lb1)
    swi = swo = rwi = rwo = None
    pending = None
    for l in reversed(range(depth)):
        x_in, proj, h, states, ka, va, kt, qt, qa, o, lse, yfull, wi_int, wo_l = saved[l]
        dy, gwo = outproj_bwd(dx, yfull, wo_l[:, None], 0)
        d_a, g_ln_g[l], g_ln_b[l], g_ws[l], dbs_t = gmlp_bwd(proj, dy, ln_g[l], ln_b[l], gmlp_w_s[l], bs_t[l])
        g_bs[l] = dbs_t[:, :A_GROUPS].T
        d_b, d0, d1, don = hgrn_bwd(proj, states, dy, lb0, lb1, onorm[l], l)
        g_lb0, g_lb1 = g_lb0 + d0, g_lb1 + d1
        g_on[l] = don.reshape(B_HEADS, B_KDIM).sum(0)
        dob, d_z, dot_t = fox_bwd_prep(dy, o, proj)
        d_q, d_k, d_v, dck, dcq, *arrived = fox_bwd(ka, va, kt, qt, dot_t, qa, dob, lse, pending)
        if arrived:
            rwi, rwo = arrived
        d_fl, dbf = fox_post(dcq, dck, proj, bf_row[l])
        g_bf[l] = dbf[0, :C_HEADS]
        dproj = jnp.concatenate([d_a, d_fl, d_b, d_q, d_k, d_v, d_z], axis=1)
        dx, g_norm[l] = inproj_bwd_x(dproj, wi_int, x_in, norm_g[l:l + 1], dx, 0)
        gwi = split_w_in_grad(inproj_bwd_w(h, dproj, 0, 1), n_shard)
        gwi, gwo = gwi[:, :, 0], gwo[:, :, 0]
        if l > 0:
            qwi, qwo = _exchange_call(PairExchange([gwi, gwo]), f"pair_exchange_{l}")
        else:
            gsm = _pack_small([
                jnp.concatenate(g_norm), jnp.stack(g_ln_g), jnp.stack(g_ln_b), jnp.stack(g_ws), jnp.stack(g_bs),
                jnp.concatenate([g_lb0, g_lb1]), jnp.stack(g_on), jnp.stack(g_bf), d_final_g, loss_tile[0, 0]])
            qwi, qwo, qsm = _exchange_call(PairExchange([gwi, gwo], [gsm]), f"pair_exchange_{l}")
        swi = pair_sum(gwi, qwi, BF16, 256, "pair_sum_w_in", core, l, depth, swi)
        swo = pair_sum(gwo, qwo, BF16, gwo.shape[2], "pair_sum_w_out", core, l, depth, swo)
        if l > 0:
            pending = ChipExchange([swi, swo], l, stacked=None if rwi is None else [rwi, rwo])
        else:
            ssm = small_sum(gsm, qsm)
            rwi, rwo, rsm = _exchange_call(
                ChipExchange([swi, swo], l, [ssm], None if rwi is None else [rwi, rwo]), "chip_exchange_0")

    small_w = (norm_g, gmlp_ln_g, gmlp_ln_b, gmlp_w_s, gmlp_b_s, hgrn_lb, hgrn_onorm_g, fox_b_f, final_norm_g)
    small_m = (m_norm_g, m_gmlp_ln_g, m_gmlp_ln_b, m_gmlp_w_s, m_gmlp_b_s, m_hgrn_lb, m_hgrn_onorm_g, m_fox_b_f, m_final_norm_g)
    small_v = (v_norm_g, v_gmlp_ln_g, v_gmlp_ln_b, v_gmlp_w_s, v_gmlp_b_s, v_hgrn_lb, v_hgrn_onorm_g, v_fox_b_f, v_final_norm_g)
    zero = jnp.zeros((), F32)
    res_wi = adam_reduce(rwi, w_in, m_w_in, v_w_in, 256, "adam_w_in", own=swi, chip=chip)
    res_wo = adam_reduce(rwo, w_out, m_w_out, v_w_out, w_out.shape[1], "adam_w_out", own=swo, chip=chip)
    res_sm = adam_reduce(rsm[:, None], _pack_small(small_w + (zero,))[None], _pack_small(small_m + (zero,))[None],
                         _pack_small(small_v + (zero,))[None], _SMALL_ROWS, "adam_small")
    res_sm = [_unpack_small(r[0]) for r in res_sm]

    def group(i):
        s = res_sm[i]
        return [s["norm_g"], res_wi[i], res_wo[i], s["gmlp_ln_g"], s["gmlp_ln_b"], s["gmlp_w_s"], s["gmlp_b_s"],
                s["hgrn_lb"], s["hgrn_onorm_g"], s["fox_b_f"], s["final_norm_g"]]

    return (res_sm[0]["loss"], dx[None], *group(0), *group(1), *group(2), *group(3))
```

```python
import functools

import jax
import jax.numpy as jnp
import numpy as np
from jax import lax
from jax.experimental import pallas as pl
from jax.experimental.pallas import tpu as pltpu

F32 = jnp.float32
BF16 = jnp.bfloat16

NORM_EPS = 1e-6
F_FLOOR = 1e-30
CHUNK = 128
LANES = 128
VMEM_LIMIT = 56 * 1024 * 1024


def _cparams(*sem):
    return pltpu.CompilerParams(dimension_semantics=sem, vmem_limit_bytes=VMEM_LIMIT)


def _dot(a, b, dims=(((1,), (0,)), ((), ())), precision=None):
    return lax.dot_general(a, b, dims, precision=precision, preferred_element_type=F32)


_NT = (((1,), (1,)), ((), ()))
_TN = (((0,), (0,)), ((), ()))


def _bf16_pieces(x, n):
    out, r = [], x
    for i in range(n):
        out.append(r.astype(BF16))
        if i + 1 < n:
            r = r - out[-1].astype(F32)
    return out


@functools.partial(jax.custom_vjp, nondiff_argnums=(2,))
def _times_exact(x, e, n):
    return functools.reduce(jnp.add, [_dot(p, e) for p in _bf16_pieces(x, n)])


def _times_exact_fwd(x, e, n):
    return _times_exact(x, e, n), e


def _times_exact_bwd(n, e, g):
    dx = functools.reduce(jnp.add, [lax.dot_general(p, e, _NT, preferred_element_type=F32) for p in _bf16_pieces(g, n)])
    return dx, jnp.zeros_like(e)


_times_exact.defvjp(_times_exact_fwd, _times_exact_bwd)


@functools.partial(jax.custom_vjp, nondiff_argnums=(2,))
def _exact_times(e, x, n):
    return functools.reduce(jnp.add, [_dot(e, p) for p in _bf16_pieces(x, n)])


def _exact_times_fwd(e, x, n):
    return _exact_times(e, x, n), e


def _exact_times_bwd(n, e, g):
    dx = functools.reduce(jnp.add, [lax.dot_general(e, p, _TN, preferred_element_type=F32) for p in _bf16_pieces(g, n)])
    return jnp.zeros_like(e), dx


_exact_times.defvjp(_exact_times_fwd, _exact_times_bwd)


def _group_mean_matrix(width, group):
    idx = np.arange(width) // group
    return jnp.asarray((idx[:, None] == idx[None, :]).astype(np.float32) / group, BF16)


def _group_ones_matrix(width, group):
    idx = np.arange(width) // group
    return jnp.asarray((idx[:, None] == idx[None, :]).astype(np.float32), BF16)


A_WIDTH = 256
A_GROUPS = 4
A_GDIM = 64


A_ROWS = 512


def _gmlp_chunk(x3, ln_g, ln_b, w_s, bs_t, mean_m, gind):
    n = x3.shape[0] // CHUNK
    u = jax.nn.gelu(x3[:, :A_WIDTH])
    v = jax.nn.gelu(x3[:, A_WIDTH:2 * A_WIDTH])
    z = x3[:, 2 * A_WIDTH:]
    mu = _times_exact(v, mean_m, 2)
    d = v - mu
    var = _times_exact(d * d, mean_m, 2)
    vn = d * lax.rsqrt(var + NORM_EPS) * ln_g + ln_b
    vnb = vn.astype(BF16)
    wide = jnp.concatenate([vnb[i * CHUNK:(i + 1) * CHUNK] for i in range(n)], axis=1)
    row = lax.broadcasted_iota(jnp.int32, (CHUNK, CHUNK), 0)
    col = lax.broadcasted_iota(jnp.int32, (CHUNK, CHUNK), 1)
    causal = row >= col
    lane_g = lax.shift_right_logical(lax.broadcasted_iota(jnp.int32, (CHUNK, n * A_WIDTH), 1), 6) & (A_GROUPS - 1)
    bias = _times_exact(bs_t, gind, 3)
    mixed = jnp.concatenate([bias] * n, axis=1)
    for g in range(A_GROUPS):
        wc = jnp.where(causal, w_s[g], 0.0).astype(BF16)
        mixed = mixed + jnp.where(lane_g == g, _dot(wc, wide), 0.0)
    mixed = jnp.concatenate([mixed[:, i * A_WIDTH:(i + 1) * A_WIDTH] for i in range(n)], axis=0)
    return u * mixed * jax.nn.silu(z)


def _gmlp_consts():
    gind = np.zeros((LANES, A_WIDTH), np.float32)
    for g in range(A_GROUPS):
        gind[g, g * A_GDIM:(g + 1) * A_GDIM] = 1.0
    return _group_mean_matrix(A_WIDTH, A_GDIM), jnp.asarray(gind, BF16)


def _full(shape):
    return pl.BlockSpec(shape, lambda *_: (0,) * len(shape))


def gmlp_fwd(proj, ln_g, ln_b, w_s, bs_t):
    seq = proj.shape[0]
    rows = min(A_ROWS, seq)
    mean_m, gind = _gmlp_consts()

    def body(x_ref, g_ref, b_ref, w_ref, bs_ref, m_ref, gi_ref, y_ref):
        y = _gmlp_chunk(x_ref[...], g_ref[...], b_ref[...], w_ref[...], bs_ref[...], m_ref[...], gi_ref[...])
        y_ref[...] = y.astype(BF16)

    return pl.pallas_call(
        body,
        name="gmlp_fwd",
        grid=(seq // rows,),
        in_specs=[
            pl.BlockSpec((rows, 3 * A_WIDTH), lambda n: (n, 0)),
            _full((1, A_WIDTH)), _full((1, A_WIDTH)), _full((A_GROUPS, CHUNK, CHUNK)), _full((CHUNK, LANES)),
            _full((A_WIDTH, A_WIDTH)), _full((LANES, A_WIDTH)),
        ],
        out_specs=pl.BlockSpec((rows, A_WIDTH), lambda n: (n, 0)),
        out_shape=jax.ShapeDtypeStruct((seq, A_WIDTH), BF16),
        compiler_params=_cparams("parallel"),
    )(proj, ln_g, ln_b, w_s, bs_t, mean_m, gind)


def gmlp_bwd(proj, dy, ln_g, ln_b, w_s, bs_t):
    seq = proj.shape[0]
    rows = min(A_ROWS, seq)
    mean_m, gind = _gmlp_consts()

    def body(x_ref, dy_ref, g_ref, b_ref, w_ref, bs_ref, m_ref, gi_ref, dx_ref, dg_ref, db_ref, dw_ref, dbs_ref):
        fn = functools.partial(_gmlp_chunk, mean_m=m_ref[...], gind=gi_ref[...])
        _, vjp = jax.vjp(fn, x_ref[...], g_ref[...], b_ref[...], w_ref[...], bs_ref[...])
        dx, dg, db, dw, dbs = vjp(dy_ref[...])
        dx_ref[...] = dx.astype(BF16)

        @pl.when(pl.program_id(0) == 0)
        def _():
            dg_ref[...] = jnp.zeros_like(dg_ref)
            db_ref[...] = jnp.zeros_like(db_ref)
            dw_ref[...] = jnp.zeros_like(dw_ref)
            dbs_ref[...] = jnp.zeros_like(dbs_ref)

        dg_ref[...] += dg
        db_ref[...] += db
        dw_ref[...] += dw
        dbs_ref[...] += dbs

    return pl.pallas_call(
        body,
        name="gmlp_bwd",
        grid=(seq // rows,),
        in_specs=[
            pl.BlockSpec((rows, 3 * A_WIDTH), lambda n: (n, 0)),
            pl.BlockSpec((rows, A_WIDTH), lambda n: (n, 0)),
            _full((1, A_WIDTH)), _full((1, A_WIDTH)), _full((A_GROUPS, CHUNK, CHUNK)), _full((CHUNK, LANES)),
            _full((A_WIDTH, A_WIDTH)), _full((LANES, A_WIDTH)),
        ],
        out_specs=[
            pl.BlockSpec((rows, 3 * A_WIDTH), lambda n: (n, 0)),
            _full((1, A_WIDTH)), _full((1, A_WIDTH)), _full((A_GROUPS, CHUNK, CHUNK)), _full((CHUNK, LANES)),
        ],
        out_shape=[
            jax.ShapeDtypeStruct((seq, D_INT), BF16),
            jax.ShapeDtypeStruct((1, A_WIDTH), F32), jax.ShapeDtypeStruct((1, A_WIDTH), F32),
            jax.ShapeDtypeStruct((A_GROUPS, CHUNK, CHUNK), F32), jax.ShapeDtypeStruct((CHUNK, LANES), F32),
        ],
        compiler_params=_cparams("arbitrary"),
    )(proj, dy, ln_g, ln_b, w_s, bs_t, mean_m, gind)


B_WIDTH = 256
B_HEADS = 4
B_KDIM = 64
B_LEVELS = (64, 32, 16, 8, 4, 2, 1)


def _hgrn_consts():
    t = np.arange(CHUNK)
    u = t[None, :]
    mats = [np.tril(np.ones((CHUNK, CHUNK), np.float32))]
    for m in B_LEVELS:
        p = (t // (2 * m)) * (2 * m) + m - 1
        right = (t % (2 * m)) >= m
        sel = np.where(right[:, None], (u > p[:, None]) & (u <= t[:, None]), (u > t[:, None]) & (u <= p[:, None]))
        mats.append(sel.astype(np.float32))
    return jnp.asarray(np.concatenate(mats, 0), BF16), _group_ones_matrix(B_WIDTH, B_KDIM)


def _hgrn_lower_bound(lb0, lb1, layer):
    mx = jnp.maximum(lb0, lb1)
    e0 = jnp.exp(lb0 - mx)
    e1 = jnp.exp(lb1 - mx)
    p0 = e0 / (e0 + e1)
    p1 = e1 / (e0 + e1)
    cs = p0 if layer == 0 else p0 + p1
    return jnp.clip(cs - p0, 0.0, 1.0 - 1e-6)


def _hgrn_chunk(x4, st, lb0, lb1, onorm, layer, tstack, ones_bd):
    q_raw, fl, v, zg = (x4[:, i * B_WIDTH:(i + 1) * B_WIDTH] for i in range(4))
    lb = _hgrn_lower_bound(lb0, lb1, layer)
    q = jax.nn.silu(q_raw) * (B_KDIM ** -0.5)
    f = lb + (1.0 - lb) * jax.nn.sigmoid(fl)
    logf = jnp.log(jnp.maximum(f, F_FLOOR))
    k = (1.0 - lb) * jax.nn.sigmoid(-fl)
    b = _exact_times(tstack[:CHUNK], logf, 3)
    dall = jnp.concatenate([b, _exact_times(tstack[CHUNK:], logf, 2)], axis=0)
    b_last = jnp.sum(logf, axis=0, keepdims=True)
    vb = v.astype(BF16)

    lane_h = lax.shift_right_logical(lax.broadcasted_iota(jnp.int32, (CHUNK, B_WIDTH), 1), 6)
    row = lax.broadcasted_iota(jnp.int32, (CHUNK, B_WIDTH), 0)
    srow = lax.broadcasted_iota(jnp.int32, (B_HEADS * CHUNK, CHUNK), 0) & (CHUNK - 1)
    scol = lax.broadcasted_iota(jnp.int32, (B_HEADS * CHUNK, CHUNK), 1)

    def heads_on_rows(a):
        return jnp.concatenate([jnp.where(lane_h == h, a, 0.0) for h in range(B_HEADS)], axis=0)

    def heads_from_rows(r):
        out = jnp.where(lane_h == 0, r[:CHUNK], 0.0)
        for h in range(1, B_HEADS):
            out = out + jnp.where(lane_h == h, r[h * CHUNK:(h + 1) * CHUNK], 0.0)
        return out

    o = lax.dot_general((q * jnp.exp(b)).astype(BF16), st.astype(BF16), _NT, preferred_element_type=F32)
    scores = jnp.zeros((B_HEADS * CHUNK, CHUNK), F32)
    for li, m in enumerate(B_LEVELS):
        e = jnp.exp(dall[(li + 1) * CHUNK:(li + 2) * CHUNK])
        right = (row & (2 * m - 1)) >= m
        qt = jnp.where(right, q * e, 0.0)
        kt = jnp.where(right, 0.0, k * e)
        sc = lax.dot_general(heads_on_rows(qt).astype(BF16), kt.astype(BF16), _NT, preferred_element_type=F32)
        sh = int(np.log2(2 * m))
        same = lax.shift_right_logical(srow, sh) == lax.shift_right_logical(scol, sh)
        scores = scores + jnp.where(same, sc, 0.0)
    o = o + heads_from_rows(_dot(scores.astype(BF16), vb))
    o = o + _times_exact(q * k, ones_bd, 2) * v

    kv = lax.dot_general(vb, (k * jnp.exp(b_last - b)).astype(BF16), _TN, preferred_element_type=F32)
    st_new = st * jnp.exp(b_last) + jnp.where(ones_bd > 0.5, kv, 0.0)

    ms = _times_exact(o * o, ones_bd, 2) * (1.0 / B_KDIM)
    y = o * lax.rsqrt(ms + NORM_EPS) * onorm * jax.nn.silu(zg)
    return y, st_new


B_ROWS = 256


def _hgrn_rows(x4, st, lb0, lb1, onorm, layer, tstack, ones_bd):
    ys = []
    for i in range(x4.shape[0] // CHUNK):
        y, st = _hgrn_chunk(x4[i * CHUNK:(i + 1) * CHUNK], st, lb0, lb1, onorm, layer, tstack, ones_bd)
        ys.append(y)
    return jnp.concatenate(ys, axis=0), st


def hgrn_fwd(proj, lb0, lb1, onorm, layer):
    seq = proj.shape[0]
    rows = min(B_ROWS, seq)
    nc = seq // rows
    tstack, ones_bd = _hgrn_consts()

    def body(x_ref, lb0_ref, lb1_ref, on_ref, t_ref, e_ref, y_ref, st_out_ref, st_ref):
        @pl.when(pl.program_id(0) == 0)
        def _():
            st_ref[...] = jnp.zeros_like(st_ref)

        st = st_ref[...]
        st_out_ref[0] = st
        y, st_new = _hgrn_rows(x_ref[...], st, lb0_ref[...], lb1_ref[...], on_ref[...], layer, t_ref[...], e_ref[...])
        y_ref[...] = y.astype(BF16)
        st_ref[...] = st_new

    return pl.pallas_call(
        body,
        name=f"hgrn_fwd_{layer}",
        grid=(nc,),
        in_specs=[
            pl.BlockSpec((rows, 4 * B_WIDTH), lambda n: (n, 1)),
            _full((1, B_WIDTH)), _full((1, B_WIDTH)), _full((1, B_WIDTH)),
            _full(((len(B_LEVELS) + 1) * CHUNK, CHUNK)), _full((B_WIDTH, B_WIDTH)),
        ],
        out_specs=[
            pl.BlockSpec((rows, B_WIDTH), lambda n: (n, 0)),
            pl.BlockSpec((1, B_WIDTH, B_WIDTH), lambda n: (n, 0, 0)),
        ],
        out_shape=[jax.ShapeDtypeStruct((seq, B_WIDTH), BF16), jax.ShapeDtypeStruct((nc, B_WIDTH, B_WIDTH), F32)],
        scratch_shapes=[pltpu.VMEM((B_WIDTH, B_WIDTH), F32)],
        compiler_params=_cparams("arbitrary"),
    )(proj, lb0, lb1, onorm, tstack, ones_bd)


def hgrn_bwd(proj, states, dy, lb0, lb1, onorm, layer, dproj):
    seq = proj.shape[0]
    rows = min(B_ROWS, seq)
    nc = seq // rows
    tstack, ones_bd = _hgrn_consts()

    def body(x_ref, st_in_ref, dy_ref, lb0_ref, lb1_ref, on_ref, t_ref, e_ref, _, dx_ref, d0_ref, d1_ref, don_ref, dst_ref):
        @pl.when(pl.program_id(0) == 0)
        def _():
            dst_ref[...] = jnp.zeros_like(dst_ref)
            d0_ref[...] = jnp.zeros_like(d0_ref)
            d1_ref[...] = jnp.zeros_like(d1_ref)
            don_ref[...] = jnp.zeros_like(don_ref)

        fn = functools.partial(_hgrn_rows, layer=layer, tstack=t_ref[...], ones_bd=e_ref[...])
        _, vjp = jax.vjp(fn, x_ref[...], st_in_ref[0], lb0_ref[...], lb1_ref[...], on_ref[...])
        dx, dst, d0, d1, don = vjp((dy_ref[...], dst_ref[...]))
        dx_ref[...] = dx.astype(BF16)
        dst_ref[...] = dst
        d0_ref[...] += d0
        d1_ref[...] += d1
        don_ref[...] += don

    rev = lambda n: nc - 1 - n
    return pl.pallas_call(
        body,
        name=f"hgrn_bwd_{layer}",
        grid=(nc,),
        in_specs=[
            pl.BlockSpec((rows, 4 * B_WIDTH), lambda n: (rev(n), 1)),
            pl.BlockSpec((1, B_WIDTH, B_WIDTH), lambda n: (rev(n), 0, 0)),
            pl.BlockSpec((rows, B_WIDTH), lambda n: (rev(n), 1)),
            _full((1, B_WIDTH)), _full((1, B_WIDTH)), _full((1, B_WIDTH)),
            _full(((len(B_LEVELS) + 1) * CHUNK, CHUNK)), _full((B_WIDTH, B_WIDTH)), _ANY,
        ],
        out_specs=[
            pl.BlockSpec((rows, 4 * B_WIDTH), lambda n: (rev(n), 1)),
            _full((1, B_WIDTH)), _full((1, B_WIDTH)), _full((1, B_WIDTH)),
        ],
        out_shape=[jax.ShapeDtypeStruct(dproj.shape, BF16)] + [jax.ShapeDtypeStruct((1, B_WIDTH), F32)] * 3,
        input_output_aliases={8: 0},
        scratch_shapes=[pltpu.VMEM((B_WIDTH, B_WIDTH), F32)],
        compiler_params=_cparams("arbitrary"),
    )(proj, states, dy, lb0, lb1, onorm, tstack, ones_bd, dproj)


D_MODEL = 1024
D_INT = 4096


def _rms_stats(xf):
    r = lax.rsqrt(jnp.mean(xf * xf, axis=-1, keepdims=True) + NORM_EPS)
    return r, xf * r


def _rms_bwd(dy, g, r, xh):
    u = dy * g
    return r * (u - xh * jnp.mean(u * xh, axis=-1, keepdims=True))


def inproj(x, g, w, layer):
    seq = x.shape[0]
    tm = min(seq, 512)

    def body(x_ref, g_ref, w_ref, p_ref, h_ref):
        _, xh = _rms_stats(x_ref[...])
        h = (xh * g_ref[...]).astype(BF16)
        h_ref[...] = h
        p_ref[...] = _dot(h, w_ref[0])

    return pl.pallas_call(
        body,
        name="inproj",
        grid=(seq // tm,),
        in_specs=[
            pl.BlockSpec((tm, D_MODEL), lambda i: (i, 0)),
            _full((1, D_MODEL)),
            pl.BlockSpec((1, D_MODEL, D_INT), lambda i: (layer, 0, 0)),
        ],
        out_specs=[pl.BlockSpec((tm, D_INT), lambda i: (i, 0)), pl.BlockSpec((tm, D_MODEL), lambda i: (i, 0))],
        out_shape=[jax.ShapeDtypeStruct((seq, D_INT), F32), jax.ShapeDtypeStruct((seq, D_MODEL), BF16)],
        compiler_params=_cparams("parallel"),
    )(x, g, w)


def outproj(x, ya, yb, o, proj, wo, layer):
    seq = x.shape[0]
    tm = min(seq, 512)
    blk = wo.shape[2]

    def body(x_ref, ya_ref, yb_ref, o_ref, z_ref, w_ref, xn_ref, y_ref):
        yc = (o_ref[...] * jax.nn.silu(z_ref[...])).astype(BF16)
        y = jnp.concatenate([ya_ref[...], yb_ref[...], yc], axis=1)
        y_ref[...] = y
        w = jnp.concatenate([w_ref[d, 0] for d in range(N_DEV)], axis=0)
        xn_ref[...] = x_ref[...] + _dot(y, w)

    return pl.pallas_call(
        body,
        name="outproj",
        grid=(seq // tm,),
        in_specs=[
            pl.BlockSpec((tm, D_MODEL), lambda i: (i, 0)),
            pl.BlockSpec((tm, 256), lambda i: (i, 0)),
            pl.BlockSpec((tm, 256), lambda i: (i, 0)),
            pl.BlockSpec((tm, 512), lambda i: (i, 0)),
            pl.BlockSpec((tm, 512), lambda i: (i, 7)),
            pl.BlockSpec((N_DEV, 1, blk, D_MODEL), lambda i: (0, layer, 0, 0)),
        ],
        out_specs=[pl.BlockSpec((tm, D_MODEL), lambda i: (i, 0)), pl.BlockSpec((tm, D_MODEL), lambda i: (i, 0))],
        out_shape=[jax.ShapeDtypeStruct((seq, D_MODEL), F32), jax.ShapeDtypeStruct((seq, D_MODEL), BF16)],
        compiler_params=_cparams("parallel"),
    )(x, ya, yb, o, proj, wo)


def outproj_bwd(dx, y, wo, layer, stacked=None):
    seq = dx.shape[0]
    ts = min(seq, 512)
    _, depth, blk, _ = wo.shape

    def body(dx_ref, y_ref, w_ref, *refs):
        dy_ref, dw_ref = refs[-2:]

        @pl.when(pl.program_id(0) == 0)
        def _():
            dw_ref[...] = jnp.zeros_like(dw_ref)

        dxb = dx_ref[...].astype(BF16)
        w = jnp.concatenate([w_ref[d, 0] for d in range(N_DEV)], axis=0)
        dy_ref[...] = lax.dot_general(dxb, w, _NT, preferred_element_type=F32)
        dw = lax.dot_general(y_ref[...], dxb, _TN, preferred_element_type=F32)
        for d in range(N_DEV):
            dw_ref[d % 2, d // 2, 0] += dw[d * blk:(d + 1) * blk]

    carried = () if stacked is None else (stacked,)
    out_shape = [jax.ShapeDtypeStruct((seq, D_MODEL), F32), jax.ShapeDtypeStruct((2, N_CHIP, depth, blk, D_MODEL), F32)]
    return pl.pallas_call(
        body,
        name="outproj_bwd",
        grid=(seq // ts,),
        in_specs=[
            pl.BlockSpec((ts, D_MODEL), lambda i: (i, 0)),
            pl.BlockSpec((ts, D_MODEL), lambda i: (i, 0)),
            pl.BlockSpec((N_DEV, 1, blk, D_MODEL), lambda i: (0, layer, 0, 0)),
        ] + [_ANY] * len(carried),
        out_specs=[pl.BlockSpec((ts, D_MODEL), lambda i: (i, 0)),
                   pl.BlockSpec((2, N_CHIP, 1, blk, D_MODEL), lambda i: (0, 0, layer, 0, 0))],
        out_shape=out_shape,
        input_output_aliases={3: 1} if carried else {},
        compiler_params=_cparams("arbitrary"),
    )(dx, y, wo, *carried)


C_QKV = (2048, 3584)


def _dproj_parts(dp_ref, dqkv_refs, rows):
    lo, hi = C_QKV
    step = (hi - lo) // len(dqkv_refs)
    return ([(0, dp_ref.at[rows, 0:lo])] + [(lo + i * step, r.at[rows, :]) for i, r in enumerate(dqkv_refs)]
            + [(hi, dp_ref.at[rows, hi:D_INT])])


def inproj_bwd_x(dproj, dqkv, w, x, g, dx_in, layer):
    seq = x.shape[0]
    tm = min(seq, 512)

    def body(dp_ref, dq_ref, dk_ref, dv_ref, w_ref, x_ref, g_ref, dxin_ref, dx_ref, dg_ref):
        @pl.when(pl.program_id(0) == 0)
        def _():
            dg_ref[...] = jnp.zeros_like(dg_ref)

        dh = None
        for at, part in _dproj_parts(dp_ref, (dq_ref, dk_ref, dv_ref), slice(None)):
            term = lax.dot_general(part[...], w_ref[0, :, at:at + part.shape[1]], _NT, preferred_element_type=F32)
            dh = term if dh is None else dh + term
        r, xh = _rms_stats(x_ref[...])
        dg_ref[...] += jnp.sum(dh * xh, axis=0, keepdims=True)
        dx_ref[...] = dxin_ref[...] + _rms_bwd(dh, g_ref[...], r, xh)

    third = lambda: pl.BlockSpec((tm, C_WIDTH), lambda i: (i, 0))
    return pl.pallas_call(
        body,
        name="inproj_bwd_x",
        grid=(seq // tm,),
        in_specs=[
            pl.BlockSpec((tm, D_INT), lambda i: (i, 0)), third(), third(), third(),
            pl.BlockSpec((1, D_MODEL, D_INT), lambda i: (layer, 0, 0)),
            pl.BlockSpec((tm, D_MODEL), lambda i: (i, 0)),
            _full((1, D_MODEL)),
            pl.BlockSpec((tm, D_MODEL), lambda i: (i, 0)),
        ],
        out_specs=[pl.BlockSpec((tm, D_MODEL), lambda i: (i, 0)), _full((1, D_MODEL))],
        out_shape=[jax.ShapeDtypeStruct((seq, D_MODEL), F32), jax.ShapeDtypeStruct((1, D_MODEL), F32)],
        compiler_params=_cparams("arbitrary"),
    )(dproj, *dqkv, w, x, g, dx_in)


def inproj_bwd_w(h, dproj, dqkv):
    seq = h.shape[0]
    ts, tn = min(seq, 512), 512

    def body(h_ref, dp_ref, dq_ref, dk_ref, dv_ref, dw_ref):
        @pl.when(pl.program_id(0) == 0)
        def _():
            dw_ref[...] = jnp.zeros_like(dw_ref)

        ht = h_ref[...].T
        for at, part in _dproj_parts(dp_ref, (dq_ref, dk_ref, dv_ref), slice(None)):
            for c in range(0, part.shape[1], tn):
                dw_ref[0, :, at + c:at + c + tn] += _dot(ht, part[:, c:c + tn])

    third = lambda: pl.BlockSpec((ts, C_WIDTH), lambda s: (s, 0))
    return pl.pallas_call(
        body,
        name="inproj_bwd_w",
        grid=(seq // ts,),
        in_specs=[pl.BlockSpec((ts, D_MODEL), lambda s: (s, 0)), pl.BlockSpec((ts, D_INT), lambda s: (s, 0)),
                  third(), third(), third()],
        out_specs=_full((1, D_MODEL, D_INT)),
        out_shape=jax.ShapeDtypeStruct((1, D_MODEL, D_INT), F32),
        compiler_params=_cparams("arbitrary"),
    )(h, dproj, *dqkv)


N_IN = 3848


def _internal_of(col):
    return col if col < 768 else (col + 256 if col < 3840 else 768 + col - 3840)


def _column_runs(n_shard):
    runs = []
    for d in range(N_IN // n_shard):
        mine = []
        for j in range(n_shard):
            ci = _internal_of(d * n_shard + j)
            if mine and mine[-1][0] + mine[-1][1] == ci:
                mine[-1][1] += 1
            else:
                mine.append([ci, 1, j])
        runs.append(mine)
    return runs


def assemble_w_in(wi_all):
    n_dev, depth, _, n_shard = wi_all.shape
    tr = 256
    pieces = [[] for _ in range(D_INT // LANES)]
    for d, mine in enumerate(_column_runs(n_shard)):
        for ci, ln, off in mine:
            while ln > 0:
                blk, at = divmod(ci, LANES)
                take = min(ln, LANES - at)
                pieces[blk].append((at, take, d, off))
                ci, ln, off = ci + take, ln - take, off + take

    def body(x_ref, o_ref):
        for blk, parts in enumerate(pieces):
            vals, at = [], 0
            for start, ln, d, off in sorted(parts):
                if start > at:
                    vals.append(jnp.zeros((tr, start - at), BF16))
                vals.append(x_ref[d, 0, :, off:off + ln])
                at = start + ln
            if at < LANES:
                vals.append(jnp.zeros((tr, LANES - at), BF16))
            o_ref[0, :, blk * LANES:(blk + 1) * LANES] = vals[0] if len(vals) == 1 else jnp.concatenate(vals, axis=1)

    return pl.pallas_call(
        body,
        name="assemble_w_in",
        grid=(depth, D_MODEL // tr),
        in_specs=[pl.BlockSpec((n_dev, 1, tr, n_shard), lambda l, r: (0, l, r, 0))],
        out_specs=pl.BlockSpec((1, tr, D_INT), lambda l, r: (l, r, 0)),
        out_shape=jax.ShapeDtypeStruct((depth, D_MODEL, D_INT), BF16),
        compiler_params=_cparams("parallel", "parallel"),
    )(wi_all)


def split_w_in_grad(dwi, n_shard):
    depth = dwi.shape[0]
    tr = 256
    runs = _column_runs(n_shard)

    def body(x_ref, o_ref):
        for d, mine in enumerate(runs):
            for ci, ln, off in mine:
                o_ref[d % 2, d // 2, 0, :, off:off + ln] = x_ref[0, :, ci:ci + ln]

    return pl.pallas_call(
        body,
        name="split_w_in_grad",
        grid=(depth, D_MODEL // tr),
        in_specs=[pl.BlockSpec((1, tr, D_INT), lambda l, r: (l, r, 0))],
        out_specs=pl.BlockSpec((2, N_CHIP, 1, tr, n_shard), lambda l, r: (0, 0, l, r, 0)),
        out_shape=jax.ShapeDtypeStruct((2, N_CHIP, depth, D_MODEL, n_shard), F32),
        compiler_params=_cparams("parallel", "parallel"),
    )(dwi)


def final_loss(x, g, tgt):
    seq = x.shape[0]
    tm = min(seq, 512)

    def body(x_ref, g_ref, t_ref, dx_ref, dg_ref, loss_ref):
        @pl.when(pl.program_id(0) == 0)
        def _():
            dg_ref[...] = jnp.zeros_like(dg_ref)
            loss_ref[...] = jnp.zeros_like(loss_ref)

        g = g_ref[...]
        r, xh = _rms_stats(x_ref[...])
        err = xh * g - t_ref[...]
        sq = jnp.sum(jnp.sum(err * err, axis=1, keepdims=True), axis=0, keepdims=True)
        loss_ref[...] += jnp.broadcast_to(sq * (0.5 / D_MODEL), loss_ref.shape)
        dout = err * (1.0 / D_MODEL)
        dg_ref[...] += jnp.sum(dout * xh, axis=0, keepdims=True)
        dx_ref[...] = _rms_bwd(dout, g, r, xh)

    return pl.pallas_call(
        body,
        name="final_loss",
        grid=(seq // tm,),
        in_specs=[pl.BlockSpec((tm, D_MODEL), lambda i: (i, 0)), _full((1, D_MODEL)), pl.BlockSpec((tm, D_MODEL), lambda i: (i, 0))],
        out_specs=[pl.BlockSpec((tm, D_MODEL), lambda i: (i, 0)), _full((1, D_MODEL)), _full((8, LANES))],
        out_shape=[jax.ShapeDtypeStruct((seq, D_MODEL), F32), jax.ShapeDtypeStruct((1, D_MODEL), F32), jax.ShapeDtypeStruct((8, LANES), F32)],
        compiler_params=_cparams("arbitrary"),
    )(x, g, tgt)


C_WIDTH = 512
C_HEADS = 8
C_HDIM = 64
C_PAIRS = C_HEADS // 2
C_BQ = 512
C_TAIL = 16
C_KG = 4


def _split3(x):
    hi = x.astype(BF16)
    r = x - hi.astype(F32)
    mid = r.astype(BF16)
    return hi, mid, (r - mid.astype(F32)).astype(BF16)


def _piece_selectors():
    sel = np.zeros((C_HEADS, 3 * LANES, LANES), np.float32)
    for p in range(C_PAIRS):
        for e in range(2):
            for t in range(3):
                sel[2 * p + e, t * LANES + 2 * p + e, 3 * e + t] = -1.0
    return sel


def fox_prep(proj, bf_row):
    seq = proj.shape[0]
    nblk = seq // CHUNK
    tril = jnp.asarray(np.tril(np.ones((CHUNK, CHUNK), np.float32)), BF16)
    sel = jnp.asarray(_piece_selectors(), BF16)
    rows_t = CHUNK + C_TAIL

    def body(fl_ref, q_ref, k_ref, v_ref, bf_ref, l_ref, sel_ref, ka_ref, va_ref, vt_ref, kt_ref, qt_ref, qa_ref, carry_ref):
        @pl.when(pl.program_id(0) == 0)
        def _():
            carry_ref[...] = jnp.zeros_like(carry_ref)

        lf = jax.nn.log_sigmoid(fl_ref[:, :LANES] + bf_ref[...])
        c = _exact_times(l_ref[...], lf, 3) + carry_ref[...]
        carry_ref[...] += jnp.sum(lf, axis=0, keepdims=True)
        c3 = jnp.concatenate(_split3(c), axis=1)
        lane = lax.broadcasted_iota(jnp.int32, (CHUNK, LANES), 1)
        row = lax.broadcasted_iota(jnp.int32, (CHUNK, LANES), 0)
        r16 = lax.broadcasted_iota(jnp.int32, (C_TAIL, 2 * CHUNK), 0)
        l16 = lax.broadcasted_iota(jnp.int32, (C_TAIL, 2 * CHUNK), 1)
        zero = jnp.zeros((CHUNK, LANES), BF16)
        one = jnp.ones((CHUNK, LANES), BF16)

        def by_keys(x, right_a, right_b):
            xb = x.astype(BF16)
            top = jnp.concatenate([jnp.where(lane < C_HDIM, xb, zero), right_a], axis=1)
            return jnp.concatenate([top, jnp.concatenate([jnp.where(lane < C_HDIM, zero, xb), right_b], axis=1)], axis=0)

        def by_lanes(x, tail):
            xt = x.T.astype(BF16)
            main = jnp.concatenate([jnp.where(row < C_HDIM, xt, zero), jnp.where(row < C_HDIM, zero, xt)], axis=1)
            return jnp.concatenate([main, tail], axis=0)

        for p in range(C_PAIRS):
            cols = slice(p * LANES, (p + 1) * LANES)
            q2, k2, v2 = q_ref[:, cols] * (C_HDIM ** -0.5), k_ref[:, cols], v_ref[:, cols]
            negc = [_dot(c3, sel_ref[2 * p + e]).astype(BF16) for e in range(2)]
            ones3 = [jnp.where((lane >= 3 * e) & (lane < 3 * e + 3), one, zero) for e in range(2)]
            tail = jnp.where(((r16 == 2 * p) & (l16 < CHUNK)) | ((r16 == 2 * p + 1) & (l16 >= CHUNK)), 1.0, 0.0).astype(BF16)
            ka_ref[p] = by_keys(k2, negc[0], negc[1])
            va_ref[p] = by_keys(v2, ones3[0], ones3[1])
            kt_ref[p] = by_lanes(k2, tail)
            vt_ref[p] = by_lanes(v2, tail)
            qt_ref[p] = jnp.concatenate([q2.T.astype(BF16), jnp.where(row < 6, one, zero)], axis=0)
            qa_ref[p] = jnp.concatenate([q2.astype(BF16), jnp.where((lane == 2 * p) | (lane == 2 * p + 1), one, zero)], axis=1)

    wide = lambda j: pl.BlockSpec((CHUNK, C_WIDTH), lambda n: (n, j))
    by_rows = pl.BlockSpec((C_PAIRS, 2 * CHUNK, 2 * CHUNK), lambda n: (0, n, 0))
    by_cols = pl.BlockSpec((C_PAIRS, rows_t, 2 * CHUNK), lambda n: (0, 0, n))
    return pl.pallas_call(
        body,
        name="fox_prep",
        grid=(nblk,),
        in_specs=[pl.BlockSpec((CHUNK, 256), lambda n: (n, 3)), wide(4), wide(5), wide(6), _full((1, LANES)),
                  _full((CHUNK, CHUNK)), _full((C_HEADS, 3 * LANES, LANES))],
        out_specs=[by_rows, by_rows, by_cols, by_cols,
                   pl.BlockSpec((C_PAIRS, 2 * CHUNK, CHUNK), lambda n: (0, 0, n)),
                   pl.BlockSpec((C_PAIRS, CHUNK, 2 * CHUNK), lambda n: (0, n, 0))],
        out_shape=[jax.ShapeDtypeStruct((C_PAIRS, 2 * seq, 2 * CHUNK), BF16)] * 2
        + [jax.ShapeDtypeStruct((C_PAIRS, rows_t, 2 * seq), BF16)] * 2
        + [jax.ShapeDtypeStruct((C_PAIRS, 2 * CHUNK, seq), BF16), jax.ShapeDtypeStruct((C_PAIRS, seq, 2 * CHUNK), BF16)],
        scratch_shapes=[pltpu.VMEM((1, LANES), F32)],
        compiler_params=_cparams("arbitrary"),
    )(proj, proj, proj, proj, bf_row, tril, sel)


def _visible(shape, key0, query0):
    row = lax.broadcasted_iota(jnp.int32, shape, 0)
    key = key0 + lax.shift_left(lax.shift_right_logical(row, 8), 7) + (row & (CHUNK - 1))
    return key <= query0 + lax.broadcasted_iota(jnp.int32, shape, 1)


def _rows_ab(a, b, n):
    return jnp.concatenate([jnp.broadcast_to(a, (C_HDIM, n)), jnp.broadcast_to(b, (C_HDIM, n))], axis=0)


def _call_carrying(ex, body, operands, *, name, grid, in_specs, out_specs, out_shape, scratch_shapes):
    if ex is None:
        return pl.pallas_call(body, name=name, grid=grid, in_specs=in_specs, out_specs=out_specs, out_shape=out_shape,
                              scratch_shapes=scratch_shapes, compiler_params=_cparams("parallel", *["arbitrary"] * (len(grid) - 1)),
                              )(*operands)
    n_in, n_out = len(in_specs), len(out_specs)

    def wrapped(*refs):
        own, parts = _carried_refs(refs, n_in, n_out, ex)
        ids = [pl.program_id(a) for a in range(len(grid))]
        pl.when(functools.reduce(jnp.logical_and, [i == 0 for i in ids]))(lambda: ex.start(*parts))
        body(*own)
        pl.when(functools.reduce(jnp.logical_and, [i == g - 1 for i, g in zip(ids, grid)]))(lambda: ex.finish(*parts))

    return pl.pallas_call(
        wrapped, name=name, grid=grid,
        in_specs=list(in_specs) + [_ANY] * len(ex.inputs), out_specs=list(out_specs) + [_ANY] * len(ex.out_shape),
        out_shape=list(out_shape) + list(ex.out_shape), scratch_shapes=list(scratch_shapes) + list(ex.scratch),
        input_output_aliases={n_in + i: n_out + o for i, o in getattr(ex, "aliases", {}).items()},
        compiler_params=_cparams(*["arbitrary"] * len(grid)),
    )(*operands, *ex.inputs)


def fox_fwd(qt, ka, vt, carried=None):
    seq = qt.shape[2]
    nblk = seq // CHUNK
    bq = min(C_BQ, seq)
    grp = bq // CHUNK
    rows_t = CHUNK + C_TAIL

    def body(qt_ref, ka_ref, vt_ref, o_ref, lse_ref, acc_ref, s_ref):
        p, i = pl.program_id(0), pl.program_id(1)
        qtile = qt_ref[0]
        r16 = lax.broadcasted_iota(jnp.int32, (C_TAIL, bq), 0)

        def scores(t):
            at = pl.multiple_of(t * grp * 2 * CHUNK, 2 * CHUNK)
            return _dot(ka_ref[0, pl.ds(at, grp * 2 * CHUNK), :], qtile)

        def group(t, m, masked):
            ma, mb = m
            at = pl.multiple_of(t * grp * 2 * CHUNK, 2 * CHUNK)
            s = s_ref[...]
            if masked:
                s = jnp.where(_visible(s.shape, t * bq, i * bq), s, -jnp.inf)
            sa = [s[g * 2 * CHUNK:g * 2 * CHUNK + CHUNK] for g in range(grp)]
            sb = [s[g * 2 * CHUNK + CHUNK:(g + 1) * 2 * CHUNK] for g in range(grp)]
            na, nb = ma, mb
            for g in range(grp):
                na = jnp.maximum(na, jnp.max(sa[g], axis=0, keepdims=True))
                nb = jnp.maximum(nb, jnp.max(sb[g], axis=0, keepdims=True))
            al_a, al_b = jnp.exp(ma - na), jnp.exp(mb - nb)
            pt = jnp.concatenate([jnp.exp(x - n) for g in range(grp) for x, n in ((sa[g], na), (sb[g], nb))], axis=0)
            pv = _dot(vt_ref[0, :, pl.ds(at, grp * 2 * CHUNK)], pt.astype(BF16))
            tail = jnp.where(r16 == 2 * p, al_a, jnp.where(r16 == 2 * p + 1, al_b, 1.0))
            acc_ref[...] = acc_ref[...] * jnp.concatenate([_rows_ab(al_a, al_b, bq), tail], axis=0) + pv
            return na, nb

        def step(t, m):
            s_next = scores(t + 1)
            m = group(t, m, False)
            s_ref[...] = s_next
            return m

        acc_ref[...] = jnp.zeros_like(acc_ref)
        s_ref[...] = scores(0)
        m = (jnp.full((1, bq), -jnp.inf, F32), jnp.full((1, bq), -jnp.inf, F32))
        m = lax.fori_loop(0, i, step, m)
        ma, mb = group(i, m, True)
        tailv = acc_ref[CHUNK:rows_t, :]
        la = jnp.sum(jnp.where(r16 == 2 * p, tailv, 0.0), axis=0, keepdims=True)
        lb = jnp.sum(jnp.where(r16 == 2 * p + 1, tailv, 0.0), axis=0, keepdims=True)
        o_ref[...] = (acc_ref[0:CHUNK, :] * _rows_ab(1.0 / la, 1.0 / lb, bq)).T
        lse_ref[0, 0:1, :] = ma + jnp.log(la)
        lse_ref[0, 1:2, :] = mb + jnp.log(lb)

    return _call_carrying(
        carried, body, (qt, ka, vt),
        name="fox_fwd",
        grid=(C_PAIRS, seq // bq),
        in_specs=[
            pl.BlockSpec((1, 2 * CHUNK, bq), lambda p, i: (p, 0, i)),
            pl.BlockSpec((1, 2 * seq, 2 * CHUNK), lambda p, i: (p, 0, 0)),
            pl.BlockSpec((1, rows_t, 2 * seq), lambda p, i: (p, 0, 0)),
        ],
        out_specs=[pl.BlockSpec((bq, LANES), lambda p, i: (i, p)), pl.BlockSpec((1, 2, bq), lambda p, i: (p, 0, i))],
        out_shape=[jax.ShapeDtypeStruct((seq, C_WIDTH), F32), jax.ShapeDtypeStruct((C_PAIRS, 2, seq), F32)],
        scratch_shapes=[pltpu.VMEM((rows_t, bq), F32), pltpu.VMEM((grp * 2 * CHUNK, bq), F32)],
    )


def fox_bwd_prep(dy, o, proj, dproj):
    seq = o.shape[0]
    ind = np.zeros((C_WIDTH, LANES), np.float32)
    for h in range(C_HEADS):
        ind[h * C_HDIM:(h + 1) * C_HDIM, h] = 1.0
    ind = jnp.asarray(ind, BF16)
    sel = _piece_selectors()
    sel = jnp.asarray(np.stack([sel[2 * p].T + sel[2 * p + 1].T for p in range(C_PAIRS)]), BF16)

    def body(dy_ref, o_ref, z_ref, ind_ref, sel_ref, _, do_ref, dz_ref, dot_ref):
        dy_c, o_v, z = dy_ref[...], o_ref[...], z_ref[...]
        sg = jax.nn.sigmoid(z)
        do = dy_c * (z * sg)
        do_ref[...] = do.astype(BF16)
        dz_ref[...] = (dy_c * o_v * (sg * (1.0 + z * (1.0 - sg)))).astype(BF16)
        prod = do * o_v
        hi = prod.astype(BF16)
        lo = (prod - hi.astype(F32)).astype(BF16)
        delta = _dot(hi, ind_ref[...]) + _dot(lo, ind_ref[...])
        d3 = jnp.concatenate(_split3(delta.T), axis=0)
        for p in range(C_PAIRS):
            tail = _dot(sel_ref[p], d3).astype(BF16)
            dot_ref[p] = jnp.concatenate([do[:, p * LANES:(p + 1) * LANES].T.astype(BF16), tail], axis=0)

    return pl.pallas_call(
        body,
        name="fox_bwd_prep",
        grid=(seq // CHUNK,),
        in_specs=[
            pl.BlockSpec((CHUNK, C_WIDTH), lambda i: (i, 1)),
            pl.BlockSpec((CHUNK, C_WIDTH), lambda i: (i, 0)),
            pl.BlockSpec((CHUNK, C_WIDTH), lambda i: (i, 7)),
            _full((C_WIDTH, LANES)), _full((C_PAIRS, LANES, 3 * LANES)), _ANY,
        ],
        out_specs=[
            pl.BlockSpec((CHUNK, C_WIDTH), lambda i: (i, 0)),
            pl.BlockSpec((CHUNK, C_WIDTH), lambda i: (i, 7)),
            pl.BlockSpec((C_PAIRS, 2 * CHUNK, CHUNK), lambda i: (0, 0, i)),
        ],
        out_shape=[jax.ShapeDtypeStruct((seq, C_WIDTH), BF16), jax.ShapeDtypeStruct(dproj.shape, BF16),
                   jax.ShapeDtypeStruct((C_PAIRS, 2 * CHUNK, seq), BF16)],
        input_output_aliases={5: 1},
        compiler_params=_cparams("parallel"),
    )(dy, o, proj, ind, sel, dproj)


def fox_bwd(ka, va, kt, qt, dot_t, qa, dob, lse, carried=None):
    seq = qt.shape[2]
    nblk = seq // CHUNK
    bq = min(C_BQ, seq)
    nq = seq // bq
    kg = min(C_KG, nblk)
    ng = nblk // kg
    rows_t = CHUNK + C_TAIL

    def body(ka_ref, va_ref, kt_ref, qt_ref, dot_ref, qa_ref, do_ref, lse_ref,
             dq_ref, dk_ref, dv_ref, dck_ref, dcq_ref, dqt_acc, dv_acc, dka_acc):
        p, jg = pl.program_id(0), pl.program_id(1)

        @pl.when(jg == 0)
        def _():
            dqt_acc[...] = jnp.zeros_like(dqt_acc)

        dv_acc[...] = jnp.zeros_like(dv_acc)
        dka_acc[...] = jnp.zeros_like(dka_acc)

        def step(i, carry, masked):
            cols = pl.ds(pl.multiple_of(i * bq, bq), bq)
            qtile, dotile = qt_ref[0, :, cols], dot_ref[0, :, cols]
            do, qa_i = do_ref[cols, :], qa_ref[0, cols, :]
            lse2 = jnp.concatenate([jnp.broadcast_to(lse_ref[0, 0:1, cols], (CHUNK, bq)),
                                    jnp.broadcast_to(lse_ref[0, 1:2, cols], (CHUNK, bq))] * kg, axis=0)
            pt = jnp.exp(_dot(ka_ref[0], qtile) - lse2)
            if masked:
                pt = jnp.where(_visible(pt.shape, jg * kg * CHUNK, i * bq), pt, 0.0)
            ds = pt * _dot(va_ref[0], dotile)
            ptb, dsb = pt.astype(BF16), ds.astype(BF16)
            dv_acc[...] += _dot(ptb, do)
            dka_acc[...] += _dot(dsb, qa_i)
            dqt_acc[:, cols] += _dot(kt_ref[0], dsb)
            return carry

        i0 = (jg * kg * CHUNK) // bq
        step(i0, 0, True)
        lax.fori_loop(i0 + 1, nq, functools.partial(step, masked=False), 0)
        lane = lax.broadcasted_iota(jnp.int32, (CHUNK, LANES), 1)
        for kb in range(kg):
            rows = slice(kb * CHUNK, (kb + 1) * CHUNK)
            ra = slice(kb * 2 * CHUNK, kb * 2 * CHUNK + CHUNK)
            rb = slice(kb * 2 * CHUNK + CHUNK, (kb + 1) * 2 * CHUNK)
            dk_ref[rows, :] = jnp.where(lane < C_HDIM, dka_acc[ra, 0:LANES], dka_acc[rb, 0:LANES]).astype(BF16)
            dv_ref[rows, :] = jnp.where(lane < C_HDIM, dv_acc[ra, :], dv_acc[rb, :]).astype(BF16)
            dck_ref[0, rows, :] = (jnp.where(lane == 2 * p, dka_acc[ra, LANES:], 0.0)
                                   + jnp.where(lane == 2 * p + 1, dka_acc[rb, LANES:], 0.0))

        @pl.when(jg == ng - 1)
        def _():
            for c in range(nq):
                dq_ref[c * bq:(c + 1) * bq, :] = (dqt_acc[0:CHUNK, c * bq:(c + 1) * bq].T * (C_HDIM ** -0.5)).astype(BF16)
            dcq_ref[0] = dqt_acc[CHUNK:rows_t, :]

    per_pair = lambda r, c: pl.BlockSpec((1, r, c), lambda p, j: (p, 0, 0))
    by_rows = pl.BlockSpec((1, kg * 2 * CHUNK, 2 * CHUNK), lambda p, j: (p, j, 0))
    by_cols = pl.BlockSpec((1, rows_t, kg * 2 * CHUNK), lambda p, j: (p, 0, j))
    return _call_carrying(
        carried, body, (ka, va, kt, qt, dot_t, qa, dob, lse),
        name="fox_bwd",
        grid=(C_PAIRS, ng),
        in_specs=[by_rows, by_rows, by_cols, per_pair(2 * CHUNK, seq), per_pair(2 * CHUNK, seq),
                  per_pair(seq, 2 * CHUNK), pl.BlockSpec((seq, LANES), lambda p, j: (0, p)), per_pair(2, seq)],
        out_specs=[pl.BlockSpec((seq, LANES), lambda p, j: (0, p)),
                   pl.BlockSpec((kg * CHUNK, LANES), lambda p, j: (j, p)),
                   pl.BlockSpec((kg * CHUNK, LANES), lambda p, j: (j, p)),
                   pl.BlockSpec((1, kg * CHUNK, LANES), lambda p, j: (p, j, 0)),
                   per_pair(C_TAIL, seq)],
        out_shape=[jax.ShapeDtypeStruct((seq, C_WIDTH), BF16)] * 3
        + [jax.ShapeDtypeStruct((C_PAIRS, seq, LANES), F32), jax.ShapeDtypeStruct((C_PAIRS, C_TAIL, seq), F32)],
        scratch_shapes=[pltpu.VMEM((rows_t, seq), F32), pltpu.VMEM((kg * 2 * CHUNK, LANES), F32),
                        pltpu.VMEM((kg * 2 * CHUNK, 2 * CHUNK), F32)],
    )


def fox_post(dcq, dck, proj, bf_row, dproj):
    seq = proj.shape[0]
    nc = seq // CHUNK
    triu = jnp.asarray(np.triu(np.ones((CHUNK, CHUNK), np.float32)), BF16)

    def body(dq_ref, dk_ref, fl_ref, bf_ref, u_ref, _, dfl_ref, dbf_ref, carry_ref):
        @pl.when(pl.program_id(0) == 0)
        def _():
            carry_ref[...] = jnp.zeros_like(carry_ref)
            dbf_ref[...] = jnp.zeros_like(dbf_ref)

        rows = (dq_ref[0] + dq_ref[1]) + (dq_ref[2] + dq_ref[3])
        dc = jnp.concatenate([rows, jnp.zeros((CHUNK - C_TAIL, CHUNK), F32)], axis=0).T
        dc = dc - ((dk_ref[0] + dk_ref[1]) + (dk_ref[2] + dk_ref[3]))
        g = _exact_times(u_ref[...], dc, 3) + carry_ref[...]
        carry_ref[...] += jnp.sum(dc, axis=0, keepdims=True)
        dfl = g * jax.nn.sigmoid(-(fl_ref[:, :LANES] + bf_ref[...]))
        dbf_ref[...] += jnp.sum(dfl, axis=0, keepdims=True)
        dfl_ref[...] = jnp.concatenate([dfl, jnp.zeros_like(dfl)], axis=1).astype(BF16)

    rev = lambda n: nc - 1 - n
    return pl.pallas_call(
        body,
        name="fox_post",
        grid=(nc,),
        in_specs=[
            pl.BlockSpec((C_PAIRS, C_TAIL, CHUNK), lambda n: (0, 0, rev(n))),
            pl.BlockSpec((C_PAIRS, CHUNK, LANES), lambda n: (0, rev(n), 0)),
            pl.BlockSpec((CHUNK, 256), lambda n: (rev(n), 3)),
            _full((1, LANES)), _full((CHUNK, CHUNK)), _ANY,
        ],
        out_specs=[pl.BlockSpec((CHUNK, 256), lambda n: (rev(n), 3)), _full((1, LANES))],
        out_shape=[jax.ShapeDtypeStruct(dproj.shape, BF16), jax.ShapeDtypeStruct((1, LANES), F32)],
        input_output_aliases={5: 0},
        scratch_shapes=[pltpu.VMEM((1, LANES), F32)],
        compiler_params=_cparams("arbitrary"),
    )(dcq, dck, proj, bf_row, triu, dproj)


N_DEV = 8
MESH = pl.DeviceIdType.MESH
_ANY = pl.BlockSpec(memory_space=pl.ANY)


def _mesh_pos():
    return lax.axis_index("x"), lax.axis_index("y"), lax.axis_index("c")


def _dev_index(px, py, pc):
    return 4 * px + 2 * py + pc


def _row_pieces(ref, rows):
    return [ref.at[idx + (pl.ds(r, rows),)] for idx in np.ndindex(*ref.shape[:-2]) for r in range(0, ref.shape[-2], rows)]


class _Transfer:
    def __init__(self, src, dst, rows, send_sem, recv_sem, to):
        self.src, self.dst, self.rows, self.sems, self.to = src, dst, rows, (send_sem, recv_sem), to

    def _copy(self, src, dst):
        return pltpu.make_async_remote_copy(src_ref=src, dst_ref=dst, send_sem=self.sems[0], recv_sem=self.sems[1],
                                            device_id=self.to, device_id_type=MESH)

    def start(self):
        for s, d in zip(_row_pieces(self.src, self.rows), _row_pieces(self.dst, self.rows), strict=True):
            self._copy(s, d).start()

    def wait_send(self):
        self._copy(self.src, self.dst).wait_send()

    def wait_recv(self):
        self._copy(self.src, self.dst).wait_recv()


def _exchange_call(ex, name):
    n_in, n_out = len(ex.inputs), len(ex.out_shape)

    def body(*refs):
        parts = refs[:n_in], refs[n_in:n_in + n_out], refs[n_in + n_out:]
        ex.start(*parts)
        ex.finish(*parts)

    return pl.pallas_call(body, name=name, in_specs=[_ANY] * n_in, out_specs=[_ANY] * n_out, out_shape=ex.out_shape,
                          scratch_shapes=ex.scratch, input_output_aliases=getattr(ex, "aliases", {}))(*ex.inputs)


def _carried_refs(refs, n_in, n_out, ex):
    k_in, k_out, k_sem = (len(ex.inputs), len(ex.out_shape), len(ex.scratch)) if ex else (0, 0, 0)
    a, b, c = n_in + k_in, n_in + k_in + n_out, n_in + k_in + n_out + k_out
    own = refs[:n_in] + refs[a:b] + refs[c:len(refs) - k_sem]
    return own, (refs[n_in:a], refs[b:c], refs[len(refs) - k_sem:])


class AllGatherWeights:
    piece_rows = (128, 64)

    def __init__(self, wi, wo):
        self.inputs = (wi, wo)
        self.out_shape = [jax.ShapeDtypeStruct((N_DEV,) + wi.shape, wi.dtype), jax.ShapeDtypeStruct((N_DEV,) + wo.shape, wo.dtype)]
        self.scratch = [pltpu.SemaphoreType.DMA((2, 7)), pltpu.SemaphoreType.DMA((2, 7)), pltpu.SemaphoreType.DMA((2,))]

    def _plan(self, ins, outs, sems):
        send_sems, recv_sems, local_sems = sems
        x, y, c = _mesh_pos()
        me, sibling = (x, y, c), (x, y, 1 - c)
        chips = [(1 - x, y), (x, 1 - y), (1 - x, 1 - y)]
        both = range(2)

        def copy(a, k, block, to, own=False):
            slot = outs[a].at[_dev_index(*block)]
            return _Transfer(ins[a] if own else slot, slot, self.piece_rows[a], send_sems.at[a, k], recv_sems.at[a, k], to)

        mine = [pltpu.make_async_copy(ins[a], outs[a].at[_dev_index(*me)], local_sems.at[a]) for a in both]
        first = [copy(a, 1 + j, me, (*chip, c), own=True) for j, chip in enumerate(chips) for a in both]
        first += [copy(a, 0, me, sibling, own=True) for a in both]
        passed = [copy(a, 4 + j, (*chip, c), sibling) for j, chip in enumerate(chips) for a in both]
        return me, sibling, chips, c, copy, mine, first, passed

    def start(self, ins, outs, sems):
        *_, mine, first, _ = self._plan(ins, outs, sems)
        for cp in mine + first:
            cp.start()

    def finish(self, ins, outs, sems):
        me, sibling, chips, c, copy, mine, first, passed = self._plan(ins, outs, sems)
        for j, chip in enumerate(chips):
            for a in range(2):
                copy(a, 1 + j, (*chip, c), me).wait_recv()
            for a in range(2):
                passed[2 * j + a].start()
        for a in range(2):
            copy(a, 0, sibling, me).wait_recv()
        for j, chip in enumerate(chips):
            for a in range(2):
                copy(a, 4 + j, (*chip, 1 - c), me).wait_recv()
        for cp in first + passed:
            cp.wait_send()
        for cp in mine:
            cp.wait()


N_CHIP = 4


class PairExchange:
    def __init__(self, by_core, whole=()):
        self.inputs = tuple(by_core) + tuple(whole)
        self.n_by_core = len(by_core)
        self.out_shape = ([jax.ShapeDtypeStruct(a.shape[1:], a.dtype) for a in by_core]
                          + [jax.ShapeDtypeStruct(a.shape, a.dtype) for a in whole])
        n = len(self.inputs)
        self.scratch = [pltpu.SemaphoreType.DMA((n,)), pltpu.SemaphoreType.DMA((n,))]

    def _copies(self, ins, outs, sems):
        x, y, c = _mesh_pos()
        srcs = [r.at[1 - c] if a < self.n_by_core else r for a, r in enumerate(ins)]
        return [_Transfer(srcs[a], outs[a], outs[a].shape[-2], sems[0].at[a], sems[1].at[a], (x, y, 1 - c))
                for a in range(len(ins))]

    def start(self, ins, outs, sems):
        for cp in self._copies(ins, outs, sems):
            cp.start()

    def finish(self, ins, outs, sems):
        copies = self._copies(ins, outs, sems)
        for cp in copies:
            cp.wait_recv()
        for cp in copies:
            cp.wait_send()


def pair_sum(own, other, dtype, rows, name, core, layer, depth, stacked=None):
    n, n_r, n_c = other.shape

    def body(core_ref, a_ref, b_ref, *refs):
        refs[-1][0, 0] = (a_ref[0, 0] + b_ref[0]).astype(dtype)

    carried = () if stacked is None else (stacked,)
    grid_spec = pltpu.PrefetchScalarGridSpec(
        num_scalar_prefetch=1,
        grid=(n, n_r // rows),
        in_specs=[pl.BlockSpec((1, 1, rows, n_c), lambda i, r, s: (s[0], i, r, 0)),
                  pl.BlockSpec((1, rows, n_c), lambda i, r, s: (i, r, 0))] + [_ANY] * len(carried),
        out_specs=pl.BlockSpec((1, 1, rows, n_c), lambda i, r, s: (i, layer, r, 0)),
    )
    return pl.pallas_call(
        body,
        name=name,
        grid_spec=grid_spec,
        out_shape=jax.ShapeDtypeStruct((n, depth, n_r, n_c), dtype),
        input_output_aliases={3: 0} if carried else {},
        compiler_params=_cparams("parallel", "parallel"),
    )(core, own, other, *carried)


def small_sum(a, b, name):
    def body(a_ref, b_ref, o_ref):
        o_ref[...] = a_ref[...] + b_ref[...]

    return pl.pallas_call(body, name=name, out_shape=jax.ShapeDtypeStruct(a.shape, a.dtype))(a, b)


class ChipExchange:
    def __init__(self, by_chip, layer, gathered=(), stacked=None):
        stacked = tuple(stacked or ())
        self.inputs = tuple(by_chip) + tuple(gathered) + stacked
        self.n_by_chip, self.n_gathered, self.layer = len(by_chip), len(gathered), layer
        self.out_shape = ([jax.ShapeDtypeStruct((N_CHIP - 1,) + a.shape[1:], a.dtype) for a in by_chip]
                          + [jax.ShapeDtypeStruct((N_CHIP,) + a.shape, a.dtype) for a in gathered])
        self.aliases = {self.n_by_chip + self.n_gathered + i: i for i in range(len(stacked))}
        n = self.n_by_chip + self.n_gathered
        self.scratch = [pltpu.SemaphoreType.DMA((n, 3)), pltpu.SemaphoreType.DMA((n, 3)),
                        pltpu.SemaphoreType.DMA((max(self.n_gathered, 1),))]

    def _plan(self, ins, outs, sems):
        x, y, c = _mesh_pos()
        chip = 2 * x + y
        n = self.n_by_chip + self.n_gathered

        def copy(a, k, sending):
            px, py = x ^ ((k >> 1) & 1), y ^ (k & 1)
            if a < self.n_by_chip:
                src, dst = ins[a].at[2 * px + py, self.layer], outs[a].at[k - 1, self.layer]
            else:
                src, dst = ins[a], outs[a].at[chip if sending else 2 * px + py]
            return _Transfer(src, dst, dst.shape[-2], sems[0].at[a, k - 1], sems[1].at[a, k - 1], (px, py, c))

        local = [pltpu.make_async_copy(ins[a], outs[a].at[chip], sems[2].at[a - self.n_by_chip])
                 for a in range(self.n_by_chip, n)]
        return n, copy, local

    def start(self, ins, outs, sems):
        n, copy, local = self._plan(ins, outs, sems)
        for cp in local:
            cp.start()
        for k in range(1, N_CHIP):
            for a in range(n):
                copy(a, k, True).start()

    def finish(self, ins, outs, sems):
        n, copy, local = self._plan(ins, outs, sems)
        for k in range(1, N_CHIP):
            for a in range(n):
                copy(a, k, False).wait_recv()
        for k in range(1, N_CHIP):
            for a in range(n):
                copy(a, k, True).wait_send()
        for cp in local:
            cp.wait()


ADAM_LR = 0.001
ADAM_B1 = 0.9
ADAM_B2 = 0.999
ADAM_EPS = 1e-08
ADAM_WD = 0.01
ADAM_STEP = 10


def adam_reduce(parts, w, m, v, rows, name, own=None, chip=None):
    n_l, n_r, n_c = w.shape
    n_parts = parts.shape[0]

    def body(*refs):
        p_ref, w_ref, m_ref, v_ref, g_ref, d_ref, m2_ref, v2_ref = refs[-8:]
        g = p_ref[0, 0].astype(F32)
        if own is not None:
            g = refs[-9][...].reshape(rows, n_c).astype(F32) + g
        for d in range(1, n_parts):
            g = g + p_ref[d, 0].astype(F32)
        m2 = ADAM_B1 * m_ref[0] + (1.0 - ADAM_B1) * g
        v2 = ADAM_B2 * v_ref[0] + (1.0 - ADAM_B2) * (g * g)
        m_hat = m2 / (1.0 - ADAM_B1 ** ADAM_STEP)
        v_hat = v2 / (1.0 - ADAM_B2 ** ADAM_STEP)
        g_ref[0] = g
        d_ref[0] = -ADAM_LR * (m_hat / (jnp.sqrt(v_hat) + ADAM_EPS) + ADAM_WD * w_ref[0])
        m2_ref[0] = m2
        v2_ref[0] = v2

    blk = lambda: pl.BlockSpec((1, rows, n_c), lambda l, r, *_: (l, r, 0))
    in_specs = [pl.BlockSpec((n_parts, 1, rows, n_c), lambda l, r, *_: (0, l, r, 0)), blk(), blk(), blk()]
    args = (parts, w, m, v)
    if own is not None:
        in_specs = [pl.BlockSpec((1, 1, rows, n_c), lambda l, r, s: (s[0], l, r, 0))] + in_specs
        args = (chip, own) + args
    grid_spec = pltpu.PrefetchScalarGridSpec(
        num_scalar_prefetch=0 if own is None else 1, grid=(n_l, n_r // rows), in_specs=in_specs,
        out_specs=[blk(), blk(), blk(), blk()])
    return pl.pallas_call(
        body,
        name=name,
        grid_spec=grid_spec,
        out_shape=[jax.ShapeDtypeStruct(w.shape, F32)] * 4,
        compiler_params=_cparams("parallel", "parallel"),
    )(*args)


_SMALL = (("norm_g", (2, 1024)), ("gmlp_ln_g", (2, 4, 64)), ("gmlp_ln_b", (2, 4, 64)),
          ("gmlp_b_s", (2, 4, 128)), ("hgrn_lb", (2, 256)), ("hgrn_onorm_g", (2, 64)), ("fox_b_f", (2, 8)),
          ("final_norm_g", (1024,)), ("loss", ()))


def _padded(n):
    return -(-n // LANES) * LANES


_SMALL_ROWS = -(-sum(_padded(int(np.prod(s))) for _, s in _SMALL) // LANES // 8) * 8


def _pack_small(vals):
    flat = []
    for (name, shape), a in zip(_SMALL, vals, strict=True):
        n = int(np.prod(shape))
        flat.append(jnp.pad(a.reshape(n).astype(F32), (0, _padded(n) - n)))
    flat = jnp.concatenate(flat)
    return jnp.pad(flat, (0, _SMALL_ROWS * LANES - flat.shape[0])).reshape(_SMALL_ROWS, LANES)


def _unpack_small(slab):
    flat, out, at = slab.reshape(-1), {}, 0
    for name, shape in _SMALL:
        n = int(np.prod(shape))
        out[name] = flat[at:at + n].reshape(shape)
        at += _padded(n)
    return out


def kernel(x, norm_g, w_in, w_out, gmlp_ln_g, gmlp_ln_b, gmlp_w_s, gmlp_b_s, hgrn_lb, hgrn_onorm_g, fox_b_f, final_norm_g, loss_target, m_norm_g, m_w_in, m_w_out, m_gmlp_ln_g, m_gmlp_ln_b, m_gmlp_w_s, m_gmlp_b_s, m_hgrn_lb, m_hgrn_onorm_g, m_fox_b_f, m_final_norm_g, v_norm_g, v_w_in, v_w_out, v_gmlp_ln_g, v_gmlp_ln_b, v_gmlp_w_s, v_gmlp_b_s, v_hgrn_lb, v_hgrn_onorm_g, v_fox_b_f, v_final_norm_g):
    depth = w_in.shape[0]
    seq = x.shape[1]
    assert w_in.shape[2] * N_DEV == N_IN
    xs, tgt = x[0], loss_target[0]

    wi_blk, wo_blk = w_in.astype(BF16), w_out.astype(BF16)
    wi_all, wo_all = _exchange_call(AllGatherWeights(wi_blk[0], wo_blk[0]), "allgather_weights_0")

    ln_g = gmlp_ln_g.reshape(depth, 1, A_WIDTH)
    ln_b = gmlp_ln_b.reshape(depth, 1, A_WIDTH)
    bs_t = jnp.pad(jnp.transpose(gmlp_b_s, (0, 2, 1)), ((0, 0), (0, 0), (0, LANES - A_GROUPS)))
    lb0, lb1 = hgrn_lb[0:1], hgrn_lb[1:2]
    onorm = jnp.tile(hgrn_onorm_g, (1, B_HEADS)).reshape(depth, 1, B_WIDTH)
    bf_row = jnp.pad(fox_b_f, ((0, 0), (0, LANES - C_HEADS))).reshape(depth, 1, LANES)

    core = lax.axis_index("c").astype(jnp.int32).reshape(1)
    chip = (2 * lax.axis_index("x") + lax.axis_index("y")).astype(jnp.int32).reshape(1)

    saved = []
    xc = xs
    for l in range(depth):
        wi_int = assemble_w_in(wi_all[:, None])
        proj, h = inproj(xc, norm_g[l:l + 1], wi_int, 0)
        ya = gmlp_fwd(proj, ln_g[l], ln_b[l], gmlp_w_s[l], bs_t[l])
        yb, states = hgrn_fwd(proj, lb0, lb1, onorm[l], l)
        ka, va, vt, kt, qt, qa = fox_prep(proj, bf_row[l])
        nxt = AllGatherWeights(wi_blk[l + 1], wo_blk[l + 1]) if l + 1 < depth else None
        o, lse, *gathered = fox_fwd(qt, ka, vt, nxt)
        xn, yfull = outproj(xc, ya, yb, o, proj, wo_all[:, None], 0)
        saved.append((xc, proj, h, states, ka, va, kt, qt, qa, o, lse, yfull, wi_int, wo_all))
        if gathered:
            wi_all, wo_all = gathered
        xc = xn

    dx, d_final_g, loss_tile = final_loss(xc, final_norm_g[None], tgt)

    n_shard = w_in.shape[2]
    g_norm = [None] * depth
    g_ln_g, g_ln_b, g_ws, g_bs, g_on, g_bf = ([None] * depth for _ in range(6))
    g_lb0, g_lb1 = jnp.zeros_like(lb0), jnp.zeros_like(lb1)
    swi = swo = rwi = rwo = None
    pending = None
    for l in reversed(range(depth)):
        x_in, proj, h, states, ka, va, kt, qt, qa, o, lse, yfull, wi_int, wo_l = saved[l]
        dy, gwo = outproj_bwd(dx, yfull, wo_l[:, None], 0)
        dproj, g_ln_g[l], g_ln_b[l], g_ws[l], dbs_t = gmlp_bwd(proj, dy, ln_g[l], ln_b[l], gmlp_w_s[l], bs_t[l])
        g_bs[l] = dbs_t[:, :A_GROUPS].T
        dproj, d0, d1, don = hgrn_bwd(proj, states, dy, lb0, lb1, onorm[l], l, dproj)
        g_lb0, g_lb1 = g_lb0 + d0, g_lb1 + d1
        g_on[l] = don.reshape(B_HEADS, B_KDIM).sum(0)
        dob, dproj, dot_t = fox_bwd_prep(dy, o, proj, dproj)
        outs = fox_bwd(ka, va, kt, qt, dot_t, qa, dob, lse, pending)
        dqkv, (dck, dcq) = outs[:3], outs[3:5]
        if pending is not None:
            rwi, rwo = outs[5:]
        dproj, dbf = fox_post(dcq, dck, proj, bf_row[l], dproj)
        g_bf[l] = dbf[0, :C_HEADS]
        dx, g_norm[l] = inproj_bwd_x(dproj, dqkv, wi_int, x_in, norm_g[l:l + 1], dx, 0)
        gwi = split_w_in_grad(inproj_bwd_w(h, dproj, dqkv), n_shard)
        gwi, gwo = gwi[:, :, 0], gwo[:, :, 0]
        if l > 0:
            qwi, qwo = _exchange_call(PairExchange([gwi, gwo]), f"pair_exchange_{l}")
        else:
            gsm = _pack_small([
                jnp.concatenate(g_norm), jnp.stack(g_ln_g), jnp.stack(g_ln_b), jnp.stack(g_bs),
                jnp.concatenate([g_lb0, g_lb1]), jnp.stack(g_on), jnp.stack(g_bf), d_final_g, loss_tile[0, 0]])
            gws = jnp.stack(g_ws).reshape(-1, LANES)
            qwi, qwo, qsm, qws = _exchange_call(PairExchange([gwi, gwo], [gsm, gws]), f"pair_exchange_{l}")
        swi = pair_sum(gwi, qwi, BF16, 256, "pair_sum_w_in", core, l, depth, swi)
        swo = pair_sum(gwo, qwo, BF16, gwo.shape[2], "pair_sum_w_out", core, l, depth, swo)
        if l > 0:
            pending = ChipExchange([swi, swo], l, stacked=None if rwi is None else [rwi, rwo])
        else:
            ssm, sws = small_sum(gsm, qsm, "pair_sum_small"), small_sum(gws, qws, "pair_sum_w_s")
            rwi, rwo, rsm, rws = _exchange_call(
                ChipExchange([swi, swo], l, [ssm, sws], None if rwi is None else [rwi, rwo]), "chip_exchange_0")

    small_w = (norm_g, gmlp_ln_g, gmlp_ln_b, gmlp_b_s, hgrn_lb, hgrn_onorm_g, fox_b_f, final_norm_g)
    small_m = (m_norm_g, m_gmlp_ln_g, m_gmlp_ln_b, m_gmlp_b_s, m_hgrn_lb, m_hgrn_onorm_g, m_fox_b_f, m_final_norm_g)
    small_v = (v_norm_g, v_gmlp_ln_g, v_gmlp_ln_b, v_gmlp_b_s, v_hgrn_lb, v_hgrn_onorm_g, v_fox_b_f, v_final_norm_g)
    zero = jnp.zeros((), F32)
    res_wi = adam_reduce(rwi, w_in, m_w_in, v_w_in, 256, "adam_w_in", own=swi, chip=chip)
    res_wo = adam_reduce(rwo, w_out, m_w_out, v_w_out, w_out.shape[1], "adam_w_out", own=swo, chip=chip)
    res_sm = adam_reduce(rsm[:, None], _pack_small(small_w + (zero,))[None], _pack_small(small_m + (zero,))[None],
                         _pack_small(small_v + (zero,))[None], _SMALL_ROWS, "adam_small")
    res_sm = [_unpack_small(r[0]) for r in res_sm]
    as_rows = lambda a: a.reshape(1, -1, LANES)
    res_ws = adam_reduce(rws[:, None], as_rows(gmlp_w_s), as_rows(m_gmlp_w_s), as_rows(v_gmlp_w_s), rws.shape[1], "adam_w_s")
    for s, r in zip(res_sm, res_ws, strict=True):
        s["gmlp_w_s"] = r.reshape(gmlp_w_s.shape)

    def group(i):
        s = res_sm[i]
        return [s["norm_g"], res_wi[i], res_wo[i], s["gmlp_ln_g"], s["gmlp_ln_b"], s["gmlp_w_s"], s["gmlp_b_s"],
                s["hgrn_lb"], s["hgrn_onorm_g"], s["fox_b_f"], s["final_norm_g"]]

    return (res_sm[0]["loss"], dx[None], *group(0), *group(1), *group(2), *group(3))
```

```python
import functools

import jax
import jax.numpy as jnp
import numpy as np
from jax import lax
from jax.experimental import pallas as pl
from jax.experimental.pallas import tpu as pltpu

F32 = jnp.float32
BF16 = jnp.bfloat16

NORM_EPS = 1e-6
F_FLOOR = 1e-30
CHUNK = 128
LANES = 128
VMEM_LIMIT = 56 * 1024 * 1024


def _cparams(*sem):
    return pltpu.CompilerParams(dimension_semantics=sem, vmem_limit_bytes=VMEM_LIMIT)


def _dot(a, b, dims=(((1,), (0,)), ((), ())), precision=None):
    return lax.dot_general(a, b, dims, precision=precision, preferred_element_type=F32)


_NT = (((1,), (1,)), ((), ()))
_TN = (((0,), (0,)), ((), ()))


def _bf16_pieces(x, n):
    out, r = [], x
    for i in range(n):
        out.append(r.astype(BF16))
        if i + 1 < n:
            r = r - out[-1].astype(F32)
    return out


@functools.partial(jax.custom_vjp, nondiff_argnums=(2,))
def _times_exact(x, e, n):
    return functools.reduce(jnp.add, [_dot(p, e) for p in _bf16_pieces(x, n)])


def _times_exact_fwd(x, e, n):
    return _times_exact(x, e, n), e


def _times_exact_bwd(n, e, g):
    dx = functools.reduce(jnp.add, [lax.dot_general(p, e, _NT, preferred_element_type=F32) for p in _bf16_pieces(g, n)])
    return dx, jnp.zeros_like(e)


_times_exact.defvjp(_times_exact_fwd, _times_exact_bwd)


@functools.partial(jax.custom_vjp, nondiff_argnums=(2,))
def _exact_times(e, x, n):
    return functools.reduce(jnp.add, [_dot(e, p) for p in _bf16_pieces(x, n)])


def _exact_times_fwd(e, x, n):
    return _exact_times(e, x, n), e


def _exact_times_bwd(n, e, g):
    dx = functools.reduce(jnp.add, [lax.dot_general(e, p, _TN, preferred_element_type=F32) for p in _bf16_pieces(g, n)])
    return jnp.zeros_like(e), dx


_exact_times.defvjp(_exact_times_fwd, _exact_times_bwd)


def _group_mean_matrix(width, group):
    idx = np.arange(width) // group
    return jnp.asarray((idx[:, None] == idx[None, :]).astype(np.float32) / group, BF16)


def _group_ones_matrix(width, group):
    idx = np.arange(width) // group
    return jnp.asarray((idx[:, None] == idx[None, :]).astype(np.float32), BF16)


A_WIDTH = 256
A_GROUPS = 4
A_GDIM = 64


A_ROWS = 512


def _gmlp_chunk(x3, ln_g, ln_b, w_s, bs_t, mean_m, gind):
    n = x3.shape[0] // CHUNK
    u = jax.nn.gelu(x3[:, :A_WIDTH])
    v = jax.nn.gelu(x3[:, A_WIDTH:2 * A_WIDTH])
    z = x3[:, 2 * A_WIDTH:]
    mu = _times_exact(v, mean_m, 2)
    d = v - mu
    var = _times_exact(d * d, mean_m, 2)
    vn = d * lax.rsqrt(var + NORM_EPS) * ln_g + ln_b
    vnb = vn.astype(BF16)
    wide = jnp.concatenate([vnb[i * CHUNK:(i + 1) * CHUNK] for i in range(n)], axis=1)
    row = lax.broadcasted_iota(jnp.int32, (CHUNK, CHUNK), 0)
    col = lax.broadcasted_iota(jnp.int32, (CHUNK, CHUNK), 1)
    causal = row >= col
    lane_g = lax.shift_right_logical(lax.broadcasted_iota(jnp.int32, (CHUNK, n * A_WIDTH), 1), 6) & (A_GROUPS - 1)
    bias = _times_exact(bs_t, gind, 3)
    mixed = jnp.concatenate([bias] * n, axis=1)
    for g in range(A_GROUPS):
        wc = jnp.where(causal, w_s[g], 0.0).astype(BF16)
        mixed = mixed + jnp.where(lane_g == g, _dot(wc, wide), 0.0)
    mixed = jnp.concatenate([mixed[:, i * A_WIDTH:(i + 1) * A_WIDTH] for i in range(n)], axis=0)
    return u * mixed * jax.nn.silu(z)


def _gmlp_consts():
    gind = np.zeros((LANES, A_WIDTH), np.float32)
    for g in range(A_GROUPS):
        gind[g, g * A_GDIM:(g + 1) * A_GDIM] = 1.0
    return _group_mean_matrix(A_WIDTH, A_GDIM), jnp.asarray(gind, BF16)


def _full(shape):
    return pl.BlockSpec(shape, lambda *_: (0,) * len(shape))


def gmlp_fwd(proj, ln_g, ln_b, w_s, bs_t):
    seq = proj.shape[0]
    rows = min(A_ROWS, seq)
    mean_m, gind = _gmlp_consts()

    def body(x_ref, g_ref, b_ref, w_ref, bs_ref, m_ref, gi_ref, y_ref):
        y = _gmlp_chunk(x_ref[...], g_ref[...], b_ref[...], w_ref[...], bs_ref[...], m_ref[...], gi_ref[...])
        y_ref[...] = y.astype(BF16)

    return pl.pallas_call(
        body,
        name="gmlp_fwd",
        grid=(seq // rows,),
        in_specs=[
            pl.BlockSpec((rows, 3 * A_WIDTH), lambda n: (n, 0)),
            _full((1, A_WIDTH)), _full((1, A_WIDTH)), _full((A_GROUPS, CHUNK, CHUNK)), _full((CHUNK, LANES)),
            _full((A_WIDTH, A_WIDTH)), _full((LANES, A_WIDTH)),
        ],
        out_specs=pl.BlockSpec((rows, A_WIDTH), lambda n: (n, 0)),
        out_shape=jax.ShapeDtypeStruct((seq, A_WIDTH), BF16),
        compiler_params=_cparams("parallel"),
    )(proj, ln_g, ln_b, w_s, bs_t, mean_m, gind)


def gmlp_bwd(proj, dy, ln_g, ln_b, w_s, bs_t):
    seq = proj.shape[0]
    rows = min(A_ROWS, seq)
    mean_m, gind = _gmlp_consts()

    def body(x_ref, dy_ref, g_ref, b_ref, w_ref, bs_ref, m_ref, gi_ref, dx_ref, dg_ref, db_ref, dw_ref, dbs_ref):
        fn = functools.partial(_gmlp_chunk, mean_m=m_ref[...], gind=gi_ref[...])
        _, vjp = jax.vjp(fn, x_ref[...], g_ref[...], b_ref[...], w_ref[...], bs_ref[...])
        dx, dg, db, dw, dbs = vjp(dy_ref[...])
        dx_ref[...] = dx.astype(BF16)

        @pl.when(pl.program_id(0) == 0)
        def _():
            dg_ref[...] = jnp.zeros_like(dg_ref)
            db_ref[...] = jnp.zeros_like(db_ref)
            dw_ref[...] = jnp.zeros_like(dw_ref)
            dbs_ref[...] = jnp.zeros_like(dbs_ref)

        dg_ref[...] += dg
        db_ref[...] += db
        dw_ref[...] += dw
        dbs_ref[...] += dbs

    return pl.pallas_call(
        body,
        name="gmlp_bwd",
        grid=(seq // rows,),
        in_specs=[
            pl.BlockSpec((rows, 3 * A_WIDTH), lambda n: (n, 0)),
            pl.BlockSpec((rows, A_WIDTH), lambda n: (n, 0)),
            _full((1, A_WIDTH)), _full((1, A_WIDTH)), _full((A_GROUPS, CHUNK, CHUNK)), _full((CHUNK, LANES)),
            _full((A_WIDTH, A_WIDTH)), _full((LANES, A_WIDTH)),
        ],
        out_specs=[
            pl.BlockSpec((rows, 3 * A_WIDTH), lambda n: (n, 0)),
            _full((1, A_WIDTH)), _full((1, A_WIDTH)), _full((A_GROUPS, CHUNK, CHUNK)), _full((CHUNK, LANES)),
        ],
        out_shape=[
            jax.ShapeDtypeStruct((seq, D_INT), BF16),
            jax.ShapeDtypeStruct((1, A_WIDTH), F32), jax.ShapeDtypeStruct((1, A_WIDTH), F32),
            jax.ShapeDtypeStruct((A_GROUPS, CHUNK, CHUNK), F32), jax.ShapeDtypeStruct((CHUNK, LANES), F32),
        ],
        compiler_params=_cparams("arbitrary"),
    )(proj, dy, ln_g, ln_b, w_s, bs_t, mean_m, gind)


B_WIDTH = 256
B_HEADS = 4
B_KDIM = 64
B_LEVELS = (64, 32, 16, 8, 4, 2, 1)


def _hgrn_consts():
    t = np.arange(CHUNK)
    u = t[None, :]
    mats = [np.tril(np.ones((CHUNK, CHUNK), np.float32))]
    for m in B_LEVELS:
        p = (t // (2 * m)) * (2 * m) + m - 1
        right = (t % (2 * m)) >= m
        sel = np.where(right[:, None], (u > p[:, None]) & (u <= t[:, None]), (u > t[:, None]) & (u <= p[:, None]))
        mats.append(sel.astype(np.float32))
    return jnp.asarray(np.concatenate(mats, 0), BF16), _group_ones_matrix(B_WIDTH, B_KDIM)


def _hgrn_lower_bound(lb0, lb1, layer):
    mx = jnp.maximum(lb0, lb1)
    e0 = jnp.exp(lb0 - mx)
    e1 = jnp.exp(lb1 - mx)
    p0 = e0 / (e0 + e1)
    p1 = e1 / (e0 + e1)
    cs = p0 if layer == 0 else p0 + p1
    return jnp.clip(cs - p0, 0.0, 1.0 - 1e-6)


def _hgrn_chunk(x4, st, lb0, lb1, onorm, layer, tstack, ones_bd):
    q_raw, fl, v, zg = (x4[:, i * B_WIDTH:(i + 1) * B_WIDTH] for i in range(4))
    lb = _hgrn_lower_bound(lb0, lb1, layer)
    q = jax.nn.silu(q_raw) * (B_KDIM ** -0.5)
    f = lb + (1.0 - lb) * jax.nn.sigmoid(fl)
    logf = jnp.log(jnp.maximum(f, F_FLOOR))
    k = (1.0 - lb) * jax.nn.sigmoid(-fl)
    b = _exact_times(tstack[:CHUNK], logf, 3)
    dall = jnp.concatenate([b, _exact_times(tstack[CHUNK:], logf, 2)], axis=0)
    b_last = jnp.sum(logf, axis=0, keepdims=True)
    vb = v.astype(BF16)

    lane_h = lax.shift_right_logical(lax.broadcasted_iota(jnp.int32, (CHUNK, B_WIDTH), 1), 6)
    row = lax.broadcasted_iota(jnp.int32, (CHUNK, B_WIDTH), 0)
    srow = lax.broadcasted_iota(jnp.int32, (B_HEADS * CHUNK, CHUNK), 0) & (CHUNK - 1)
    scol = lax.broadcasted_iota(jnp.int32, (B_HEADS * CHUNK, CHUNK), 1)

    def heads_on_rows(a):
        return jnp.concatenate([jnp.where(lane_h == h, a, 0.0) for h in range(B_HEADS)], axis=0)

    def heads_from_rows(r):
        out = jnp.where(lane_h == 0, r[:CHUNK], 0.0)
        for h in range(1, B_HEADS):
            out = out + jnp.where(lane_h == h, r[h * CHUNK:(h + 1) * CHUNK], 0.0)
        return out

    o = lax.dot_general((q * jnp.exp(b)).astype(BF16), st.astype(BF16), _NT, preferred_element_type=F32)
    scores = jnp.zeros((B_HEADS * CHUNK, CHUNK), F32)
    for li, m in enumerate(B_LEVELS):
        e = jnp.exp(dall[(li + 1) * CHUNK:(li + 2) * CHUNK])
        right = (row & (2 * m - 1)) >= m
        qt = jnp.where(right, q * e, 0.0)
        kt = jnp.where(right, 0.0, k * e)
        sc = lax.dot_general(heads_on_rows(qt).astype(BF16), kt.astype(BF16), _NT, preferred_element_type=F32)
        sh = int(np.log2(2 * m))
        same = lax.shift_right_logical(srow, sh) == lax.shift_right_logical(scol, sh)
        scores = scores + jnp.where(same, sc, 0.0)
    o = o + heads_from_rows(_dot(scores.astype(BF16), vb))
    o = o + _times_exact(q * k, ones_bd, 2) * v

    kv = lax.dot_general(vb, (k * jnp.exp(b_last - b)).astype(BF16), _TN, preferred_element_type=F32)
    st_new = st * jnp.exp(b_last) + jnp.where(ones_bd > 0.5, kv, 0.0)

    ms = _times_exact(o * o, ones_bd, 2) * (1.0 / B_KDIM)
    y = o * lax.rsqrt(ms + NORM_EPS) * onorm * jax.nn.silu(zg)
    return y, st_new


B_ROWS = 256


def _hgrn_rows(x4, st, lb0, lb1, onorm, layer, tstack, ones_bd):
    ys = []
    for i in range(x4.shape[0] // CHUNK):
        y, st = _hgrn_chunk(x4[i * CHUNK:(i + 1) * CHUNK], st, lb0, lb1, onorm, layer, tstack, ones_bd)
        ys.append(y)
    return jnp.concatenate(ys, axis=0), st


def hgrn_fwd(proj, lb0, lb1, onorm, layer):
    seq = proj.shape[0]
    rows = min(B_ROWS, seq)
    nc = seq // rows
    tstack, ones_bd = _hgrn_consts()

    def body(x_ref, lb0_ref, lb1_ref, on_ref, t_ref, e_ref, y_ref, st_out_ref, st_ref):
        @pl.when(pl.program_id(0) == 0)
        def _():
            st_ref[...] = jnp.zeros_like(st_ref)

        st = st_ref[...]
        st_out_ref[0] = st
        y, st_new = _hgrn_rows(x_ref[...], st, lb0_ref[...], lb1_ref[...], on_ref[...], layer, t_ref[...], e_ref[...])
        y_ref[...] = y.astype(BF16)
        st_ref[...] = st_new

    return pl.pallas_call(
        body,
        name=f"hgrn_fwd_{layer}",
        grid=(nc,),
        in_specs=[
            pl.BlockSpec((rows, 4 * B_WIDTH), lambda n: (n, 1)),
            _full((1, B_WIDTH)), _full((1, B_WIDTH)), _full((1, B_WIDTH)),
            _full(((len(B_LEVELS) + 1) * CHUNK, CHUNK)), _full((B_WIDTH, B_WIDTH)),
        ],
        out_specs=[
            pl.BlockSpec((rows, B_WIDTH), lambda n: (n, 0)),
            pl.BlockSpec((1, B_WIDTH, B_WIDTH), lambda n: (n, 0, 0)),
        ],
        out_shape=[jax.ShapeDtypeStruct((seq, B_WIDTH), BF16), jax.ShapeDtypeStruct((nc, B_WIDTH, B_WIDTH), F32)],
        scratch_shapes=[pltpu.VMEM((B_WIDTH, B_WIDTH), F32)],
        compiler_params=_cparams("arbitrary"),
    )(proj, lb0, lb1, onorm, tstack, ones_bd)


def hgrn_bwd(proj, states, dy, lb0, lb1, onorm, layer, dproj):
    seq = proj.shape[0]
    rows = min(B_ROWS, seq)
    nc = seq // rows
    tstack, ones_bd = _hgrn_consts()

    def body(x_ref, st_in_ref, dy_ref, lb0_ref, lb1_ref, on_ref, t_ref, e_ref, _, dx_ref, d0_ref, d1_ref, don_ref, dst_ref):
        @pl.when(pl.program_id(0) == 0)
        def _():
            dst_ref[...] = jnp.zeros_like(dst_ref)
            d0_ref[...] = jnp.zeros_like(d0_ref)
            d1_ref[...] = jnp.zeros_like(d1_ref)
            don_ref[...] = jnp.zeros_like(don_ref)

        fn = functools.partial(_hgrn_rows, layer=layer, tstack=t_ref[...], ones_bd=e_ref[...])
        _, vjp = jax.vjp(fn, x_ref[...], st_in_ref[0], lb0_ref[...], lb1_ref[...], on_ref[...])
        dx, dst, d0, d1, don = vjp((dy_ref[...], dst_ref[...]))
        dx_ref[...] = dx.astype(BF16)
        dst_ref[...] = dst
        d0_ref[...] += d0
        d1_ref[...] += d1
        don_ref[...] += don

    rev = lambda n: nc - 1 - n
    return pl.pallas_call(
        body,
        name=f"hgrn_bwd_{layer}",
        grid=(nc,),
        in_specs=[
            pl.BlockSpec((rows, 4 * B_WIDTH), lambda n: (rev(n), 1)),
            pl.BlockSpec((1, B_WIDTH, B_WIDTH), lambda n: (rev(n), 0, 0)),
            pl.BlockSpec((rows, B_WIDTH), lambda n: (rev(n), 1)),
            _full((1, B_WIDTH)), _full((1, B_WIDTH)), _full((1, B_WIDTH)),
            _full(((len(B_LEVELS) + 1) * CHUNK, CHUNK)), _full((B_WIDTH, B_WIDTH)), _ANY,
        ],
        out_specs=[
            pl.BlockSpec((rows, 4 * B_WIDTH), lambda n: (rev(n), 1)),
            _full((1, B_WIDTH)), _full((1, B_WIDTH)), _full((1, B_WIDTH)),
        ],
        out_shape=[jax.ShapeDtypeStruct(dproj.shape, BF16)] + [jax.ShapeDtypeStruct((1, B_WIDTH), F32)] * 3,
        input_output_aliases={8: 0},
        scratch_shapes=[pltpu.VMEM((B_WIDTH, B_WIDTH), F32)],
        compiler_params=_cparams("arbitrary"),
    )(proj, states, dy, lb0, lb1, onorm, tstack, ones_bd, dproj)


D_MODEL = 1024
D_INT = 4096


def _rms_stats(xf):
    r = lax.rsqrt(jnp.mean(xf * xf, axis=-1, keepdims=True) + NORM_EPS)
    return r, xf * r


def _rms_bwd(dy, g, r, xh):
    u = dy * g
    return r * (u - xh * jnp.mean(u * xh, axis=-1, keepdims=True))


def inproj(x, g, w, layer):
    seq = x.shape[0]
    tm = min(seq, 512)

    def body(x_ref, g_ref, w_ref, p_ref, h_ref):
        _, xh = _rms_stats(x_ref[...])
        h = (xh * g_ref[...]).astype(BF16)
        h_ref[...] = h
        p_ref[...] = _dot(h, w_ref[0])

    return pl.pallas_call(
        body,
        name="inproj",
        grid=(seq // tm,),
        in_specs=[
            pl.BlockSpec((tm, D_MODEL), lambda i: (i, 0)),
            _full((1, D_MODEL)),
            pl.BlockSpec((1, D_MODEL, D_INT), lambda i: (layer, 0, 0)),
        ],
        out_specs=[pl.BlockSpec((tm, D_INT), lambda i: (i, 0)), pl.BlockSpec((tm, D_MODEL), lambda i: (i, 0))],
        out_shape=[jax.ShapeDtypeStruct((seq, D_INT), F32), jax.ShapeDtypeStruct((seq, D_MODEL), BF16)],
        compiler_params=_cparams("parallel"),
    )(x, g, w)


def outproj(x, ya, yb, o, proj, wo, layer):
    seq = x.shape[0]
    tm = min(seq, 512)
    blk = wo.shape[2]

    def body(x_ref, ya_ref, yb_ref, o_ref, z_ref, w_ref, xn_ref, y_ref):
        yc = (o_ref[...] * jax.nn.silu(z_ref[...])).astype(BF16)
        y = jnp.concatenate([ya_ref[...], yb_ref[...], yc], axis=1)
        y_ref[...] = y
        w = jnp.concatenate([w_ref[d, 0] for d in range(N_DEV)], axis=0)
        xn_ref[...] = x_ref[...] + _dot(y, w)

    return pl.pallas_call(
        body,
        name="outproj",
        grid=(seq // tm,),
        in_specs=[
            pl.BlockSpec((tm, D_MODEL), lambda i: (i, 0)),
            pl.BlockSpec((tm, 256), lambda i: (i, 0)),
            pl.BlockSpec((tm, 256), lambda i: (i, 0)),
            pl.BlockSpec((tm, 512), lambda i: (i, 0)),
            pl.BlockSpec((tm, 512), lambda i: (i, 7)),
            pl.BlockSpec((N_DEV, 1, blk, D_MODEL), lambda i: (0, layer, 0, 0)),
        ],
        out_specs=[pl.BlockSpec((tm, D_MODEL), lambda i: (i, 0)), pl.BlockSpec((tm, D_MODEL), lambda i: (i, 0))],
        out_shape=[jax.ShapeDtypeStruct((seq, D_MODEL), F32), jax.ShapeDtypeStruct((seq, D_MODEL), BF16)],
        compiler_params=_cparams("parallel"),
    )(x, ya, yb, o, proj, wo)


def outproj_bwd(dx, y, wo, layer, stacked=None):
    seq = dx.shape[0]
    ts = min(seq, 512)
    _, depth, blk, _ = wo.shape

    def body(dx_ref, y_ref, w_ref, *refs):
        dy_ref, dw_ref = refs[-2:]

        @pl.when(pl.program_id(0) == 0)
        def _():
            dw_ref[...] = jnp.zeros_like(dw_ref)

        dxb = dx_ref[...].astype(BF16)
        w = jnp.concatenate([w_ref[d, 0] for d in range(N_DEV)], axis=0)
        dy_ref[...] = lax.dot_general(dxb, w, _NT, preferred_element_type=F32)
        dw = lax.dot_general(y_ref[...], dxb, _TN, preferred_element_type=F32)
        for d in range(N_DEV):
            dw_ref[d % 2, d // 2, 0] += dw[d * blk:(d + 1) * blk]

    carried = () if stacked is None else (stacked,)
    out_shape = [jax.ShapeDtypeStruct((seq, D_MODEL), F32), jax.ShapeDtypeStruct((2, N_CHIP, depth, blk, D_MODEL), F32)]
    return pl.pallas_call(
        body,
        name="outproj_bwd",
        grid=(seq // ts,),
        in_specs=[
            pl.BlockSpec((ts, D_MODEL), lambda i: (i, 0)),
            pl.BlockSpec((ts, D_MODEL), lambda i: (i, 0)),
            pl.BlockSpec((N_DEV, 1, blk, D_MODEL), lambda i: (0, layer, 0, 0)),
        ] + [_ANY] * len(carried),
        out_specs=[pl.BlockSpec((ts, D_MODEL), lambda i: (i, 0)),
                   pl.BlockSpec((2, N_CHIP, 1, blk, D_MODEL), lambda i: (0, 0, layer, 0, 0))],
        out_shape=out_shape,
        input_output_aliases={3: 1} if carried else {},
        compiler_params=_cparams("arbitrary"),
    )(dx, y, wo, *carried)


C_QKV = (2048, 3584)


def _dproj_parts(dp_ref, dqkv_refs, rows):
    lo, hi = C_QKV
    step = (hi - lo) // len(dqkv_refs)
    return ([(0, dp_ref.at[rows, 0:lo])] + [(lo + i * step, r.at[rows, :]) for i, r in enumerate(dqkv_refs)]
            + [(hi, dp_ref.at[rows, hi:D_INT])])


def inproj_bwd_x(dproj, dqkv, w, x, g, dx_in, layer, carried=None):
    seq = x.shape[0]
    tm = min(seq, 512)

    def body(dp_ref, dq_ref, dk_ref, dv_ref, w_ref, x_ref, g_ref, dxin_ref, dx_ref, dg_ref):
        @pl.when(pl.program_id(0) == 0)
        def _():
            dg_ref[...] = jnp.zeros_like(dg_ref)

        dh = None
        for at, part in _dproj_parts(dp_ref, (dq_ref, dk_ref, dv_ref), slice(None)):
            term = lax.dot_general(part[...], w_ref[0, :, at:at + part.shape[1]], _NT, preferred_element_type=F32)
            dh = term if dh is None else dh + term
        r, xh = _rms_stats(x_ref[...])
        dg_ref[...] += jnp.sum(dh * xh, axis=0, keepdims=True)
        dx_ref[...] = dxin_ref[...] + _rms_bwd(dh, g_ref[...], r, xh)

    third = lambda: pl.BlockSpec((tm, C_WIDTH), lambda i: (i, 0))
    return _call_carrying(
        carried, body, (dproj, *dqkv, w, x, g, dx_in),
        name="inproj_bwd_x",
        grid=(seq // tm,),
        in_specs=[
            pl.BlockSpec((tm, D_INT), lambda i: (i, 0)), third(), third(), third(),
            pl.BlockSpec((1, D_MODEL, D_INT), lambda i: (layer, 0, 0)),
            pl.BlockSpec((tm, D_MODEL), lambda i: (i, 0)),
            _full((1, D_MODEL)),
            pl.BlockSpec((tm, D_MODEL), lambda i: (i, 0)),
        ],
        out_specs=[pl.BlockSpec((tm, D_MODEL), lambda i: (i, 0)), _full((1, D_MODEL))],
        out_shape=[jax.ShapeDtypeStruct((seq, D_MODEL), F32), jax.ShapeDtypeStruct((1, D_MODEL), F32)],
        scratch_shapes=[], semantics=("arbitrary",),
    )


def inproj_bwd_w(h, dproj, dqkv):
    seq = h.shape[0]
    ts, tn = min(seq, 512), 512

    def body(h_ref, dp_ref, dq_ref, dk_ref, dv_ref, dw_ref):
        @pl.when(pl.program_id(0) == 0)
        def _():
            dw_ref[...] = jnp.zeros_like(dw_ref)

        ht = h_ref[...].T
        for at, part in _dproj_parts(dp_ref, (dq_ref, dk_ref, dv_ref), slice(None)):
            for c in range(0, part.shape[1], tn):
                dw_ref[0, :, at + c:at + c + tn] += _dot(ht, part[:, c:c + tn])

    third = lambda: pl.BlockSpec((ts, C_WIDTH), lambda s: (s, 0))
    return pl.pallas_call(
        body,
        name="inproj_bwd_w",
        grid=(seq // ts,),
        in_specs=[pl.BlockSpec((ts, D_MODEL), lambda s: (s, 0)), pl.BlockSpec((ts, D_INT), lambda s: (s, 0)),
                  third(), third(), third()],
        out_specs=_full((1, D_MODEL, D_INT)),
        out_shape=jax.ShapeDtypeStruct((1, D_MODEL, D_INT), F32),
        compiler_params=_cparams("arbitrary"),
    )(h, dproj, *dqkv)


N_IN = 3848


def _internal_of(col):
    return col if col < 768 else (col + 256 if col < 3840 else 768 + col - 3840)


def _column_runs(n_shard):
    runs = []
    for d in range(N_IN // n_shard):
        mine = []
        for j in range(n_shard):
            ci = _internal_of(d * n_shard + j)
            if mine and mine[-1][0] + mine[-1][1] == ci:
                mine[-1][1] += 1
            else:
                mine.append([ci, 1, j])
        runs.append(mine)
    return runs


def assemble_w_in(wi_all):
    n_dev, depth, _, n_shard = wi_all.shape
    tr = 256
    pieces = [[] for _ in range(D_INT // LANES)]
    for d, mine in enumerate(_column_runs(n_shard)):
        for ci, ln, off in mine:
            while ln > 0:
                blk, at = divmod(ci, LANES)
                take = min(ln, LANES - at)
                pieces[blk].append((at, take, d, off))
                ci, ln, off = ci + take, ln - take, off + take

    def body(x_ref, o_ref):
        for blk, parts in enumerate(pieces):
            vals, at = [], 0
            for start, ln, d, off in sorted(parts):
                if start > at:
                    vals.append(jnp.zeros((tr, start - at), BF16))
                vals.append(x_ref[d, 0, :, off:off + ln])
                at = start + ln
            if at < LANES:
                vals.append(jnp.zeros((tr, LANES - at), BF16))
            o_ref[0, :, blk * LANES:(blk + 1) * LANES] = vals[0] if len(vals) == 1 else jnp.concatenate(vals, axis=1)

    return pl.pallas_call(
        body,
        name="assemble_w_in",
        grid=(depth, D_MODEL // tr),
        in_specs=[pl.BlockSpec((n_dev, 1, tr, n_shard), lambda l, r: (0, l, r, 0))],
        out_specs=pl.BlockSpec((1, tr, D_INT), lambda l, r: (l, r, 0)),
        out_shape=jax.ShapeDtypeStruct((depth, D_MODEL, D_INT), BF16),
        compiler_params=_cparams("parallel", "parallel"),
    )(wi_all)


def split_w_in_grad(dwi, n_shard):
    depth = dwi.shape[0]
    tr = 256
    runs = _column_runs(n_shard)

    def body(x_ref, o_ref):
        for d, mine in enumerate(runs):
            for ci, ln, off in mine:
                o_ref[d % 2, d // 2, 0, :, off:off + ln] = x_ref[0, :, ci:ci + ln]

    return pl.pallas_call(
        body,
        name="split_w_in_grad",
        grid=(depth, D_MODEL // tr),
        in_specs=[pl.BlockSpec((1, tr, D_INT), lambda l, r: (l, r, 0))],
        out_specs=pl.BlockSpec((2, N_CHIP, 1, tr, n_shard), lambda l, r: (0, 0, l, r, 0)),
        out_shape=jax.ShapeDtypeStruct((2, N_CHIP, depth, D_MODEL, n_shard), F32),
        compiler_params=_cparams("parallel", "parallel"),
    )(dwi)


def final_loss(x, g, tgt):
    seq = x.shape[0]
    tm = min(seq, 512)

    def body(x_ref, g_ref, t_ref, dx_ref, dg_ref, loss_ref):
        @pl.when(pl.program_id(0) == 0)
        def _():
            dg_ref[...] = jnp.zeros_like(dg_ref)
            loss_ref[...] = jnp.zeros_like(loss_ref)

        g = g_ref[...]
        r, xh = _rms_stats(x_ref[...])
        err = xh * g - t_ref[...]
        sq = jnp.sum(jnp.sum(err * err, axis=1, keepdims=True), axis=0, keepdims=True)
        loss_ref[...] += jnp.broadcast_to(sq * (0.5 / D_MODEL), loss_ref.shape)
        dout = err * (1.0 / D_MODEL)
        dg_ref[...] += jnp.sum(dout * xh, axis=0, keepdims=True)
        dx_ref[...] = _rms_bwd(dout, g, r, xh)

    return pl.pallas_call(
        body,
        name="final_loss",
        grid=(seq // tm,),
        in_specs=[pl.BlockSpec((tm, D_MODEL), lambda i: (i, 0)), _full((1, D_MODEL)), pl.BlockSpec((tm, D_MODEL), lambda i: (i, 0))],
        out_specs=[pl.BlockSpec((tm, D_MODEL), lambda i: (i, 0)), _full((1, D_MODEL)), _full((8, LANES))],
        out_shape=[jax.ShapeDtypeStruct((seq, D_MODEL), F32), jax.ShapeDtypeStruct((1, D_MODEL), F32), jax.ShapeDtypeStruct((8, LANES), F32)],
        compiler_params=_cparams("arbitrary"),
    )(x, g, tgt)


C_WIDTH = 512
C_HEADS = 8
C_HDIM = 64
C_PAIRS = C_HEADS // 2
C_BQ = 512
C_TAIL = 16
C_KG = 4


def _split3(x):
    hi = x.astype(BF16)
    r = x - hi.astype(F32)
    mid = r.astype(BF16)
    return hi, mid, (r - mid.astype(F32)).astype(BF16)


def _piece_selectors():
    sel = np.zeros((C_HEADS, 3 * LANES, LANES), np.float32)
    for p in range(C_PAIRS):
        for e in range(2):
            for t in range(3):
                sel[2 * p + e, t * LANES + 2 * p + e, 3 * e + t] = -1.0
    return sel


def fox_prep(proj, bf_row):
    seq = proj.shape[0]
    nblk = seq // CHUNK
    tril = jnp.asarray(np.tril(np.ones((CHUNK, CHUNK), np.float32)), BF16)
    sel = jnp.asarray(_piece_selectors(), BF16)
    rows_t = CHUNK + C_TAIL

    def body(fl_ref, q_ref, k_ref, v_ref, bf_ref, l_ref, sel_ref, ka_ref, va_ref, vt_ref, kt_ref, qt_ref, qa_ref, carry_ref):
        @pl.when(pl.program_id(0) == 0)
        def _():
            carry_ref[...] = jnp.zeros_like(carry_ref)

        lf = jax.nn.log_sigmoid(fl_ref[:, :LANES] + bf_ref[...])
        c = _exact_times(l_ref[...], lf, 3) + carry_ref[...]
        carry_ref[...] += jnp.sum(lf, axis=0, keepdims=True)
        c3 = jnp.concatenate(_split3(c), axis=1)
        lane = lax.broadcasted_iota(jnp.int32, (CHUNK, LANES), 1)
        row = lax.broadcasted_iota(jnp.int32, (CHUNK, LANES), 0)
        r16 = lax.broadcasted_iota(jnp.int32, (C_TAIL, 2 * CHUNK), 0)
        l16 = lax.broadcasted_iota(jnp.int32, (C_TAIL, 2 * CHUNK), 1)
        zero = jnp.zeros((CHUNK, LANES), BF16)
        one = jnp.ones((CHUNK, LANES), BF16)

        def by_keys(x, right_a, right_b):
            xb = x.astype(BF16)
            top = jnp.concatenate([jnp.where(lane < C_HDIM, xb, zero), right_a], axis=1)
            return jnp.concatenate([top, jnp.concatenate([jnp.where(lane < C_HDIM, zero, xb), right_b], axis=1)], axis=0)

        def by_lanes(x, tail):
            xt = x.T.astype(BF16)
            main = jnp.concatenate([jnp.where(row < C_HDIM, xt, zero), jnp.where(row < C_HDIM, zero, xt)], axis=1)
            return jnp.concatenate([main, tail], axis=0)

        for p in range(C_PAIRS):
            cols = slice(p * LANES, (p + 1) * LANES)
            q2, k2, v2 = q_ref[:, cols] * (C_HDIM ** -0.5), k_ref[:, cols], v_ref[:, cols]
            negc = [_dot(c3, sel_ref[2 * p + e]).astype(BF16) for e in range(2)]
            ones3 = [jnp.where((lane >= 3 * e) & (lane < 3 * e + 3), one, zero) for e in range(2)]
            tail = jnp.where(((r16 == 2 * p) & (l16 < CHUNK)) | ((r16 == 2 * p + 1) & (l16 >= CHUNK)), 1.0, 0.0).astype(BF16)
            ka_ref[p] = by_keys(k2, negc[0], negc[1])
            va_ref[p] = by_keys(v2, ones3[0], ones3[1])
            kt_ref[p] = by_lanes(k2, tail)
            vt_ref[p] = by_lanes(v2, tail)
            qt_ref[p] = jnp.concatenate([q2.T.astype(BF16), jnp.where(row < 6, one, zero)], axis=0)
            qa_ref[p] = jnp.concatenate([q2.astype(BF16), jnp.where((lane == 2 * p) | (lane == 2 * p + 1), one, zero)], axis=1)

    wide = lambda j: pl.BlockSpec((CHUNK, C_WIDTH), lambda n: (n, j))
    by_rows = pl.BlockSpec((C_PAIRS, 2 * CHUNK, 2 * CHUNK), lambda n: (0, n, 0))
    by_cols = pl.BlockSpec((C_PAIRS, rows_t, 2 * CHUNK), lambda n: (0, 0, n))
    return pl.pallas_call(
        body,
        name="fox_prep",
        grid=(nblk,),
        in_specs=[pl.BlockSpec((CHUNK, 256), lambda n: (n, 3)), wide(4), wide(5), wide(6), _full((1, LANES)),
                  _full((CHUNK, CHUNK)), _full((C_HEADS, 3 * LANES, LANES))],
        out_specs=[by_rows, by_rows, by_cols, by_cols,
                   pl.BlockSpec((C_PAIRS, 2 * CHUNK, CHUNK), lambda n: (0, 0, n)),
                   pl.BlockSpec((C_PAIRS, CHUNK, 2 * CHUNK), lambda n: (0, n, 0))],
        out_shape=[jax.ShapeDtypeStruct((C_PAIRS, 2 * seq, 2 * CHUNK), BF16)] * 2
        + [jax.ShapeDtypeStruct((C_PAIRS, rows_t, 2 * seq), BF16)] * 2
        + [jax.ShapeDtypeStruct((C_PAIRS, 2 * CHUNK, seq), BF16), jax.ShapeDtypeStruct((C_PAIRS, seq, 2 * CHUNK), BF16)],
        scratch_shapes=[pltpu.VMEM((1, LANES), F32)],
        compiler_params=_cparams("arbitrary"),
    )(proj, proj, proj, proj, bf_row, tril, sel)


def _visible(shape, key0, query0):
    row = lax.broadcasted_iota(jnp.int32, shape, 0)
    key = key0 + lax.shift_left(lax.shift_right_logical(row, 8), 7) + (row & (CHUNK - 1))
    return key <= query0 + lax.broadcasted_iota(jnp.int32, shape, 1)


def _rows_ab(a, b, n):
    return jnp.concatenate([jnp.broadcast_to(a, (C_HDIM, n)), jnp.broadcast_to(b, (C_HDIM, n))], axis=0)


def _call_carrying(ex, body, operands, *, name, grid, in_specs, out_specs, out_shape, scratch_shapes, semantics=None):
    if ex is None:
        semantics = semantics or ("parallel", *["arbitrary"] * (len(grid) - 1))
        return pl.pallas_call(body, name=name, grid=grid, in_specs=in_specs, out_specs=out_specs, out_shape=out_shape,
                              scratch_shapes=scratch_shapes, compiler_params=_cparams(*semantics))(*operands)
    n_in, n_out = len(in_specs), len(out_specs)

    def wrapped(*refs):
        own, parts = _carried_refs(refs, n_in, n_out, ex)
        ids = [pl.program_id(a) for a in range(len(grid))]
        pl.when(functools.reduce(jnp.logical_and, [i == 0 for i in ids]))(lambda: ex.start(*parts))
        body(*own)
        pl.when(functools.reduce(jnp.logical_and, [i == g - 1 for i, g in zip(ids, grid)]))(lambda: ex.finish(*parts))

    return pl.pallas_call(
        wrapped, name=name, grid=grid,
        in_specs=list(in_specs) + [_ANY] * len(ex.inputs), out_specs=list(out_specs) + [_ANY] * len(ex.out_shape),
        out_shape=list(out_shape) + list(ex.out_shape), scratch_shapes=list(scratch_shapes) + list(ex.scratch),
        input_output_aliases={n_in + i: n_out + o for i, o in getattr(ex, "aliases", {}).items()},
        compiler_params=_cparams(*["arbitrary"] * len(grid)),
    )(*operands, *ex.inputs)


def fox_fwd(qt, ka, vt, carried=None):
    seq = qt.shape[2]
    nblk = seq // CHUNK
    bq = min(C_BQ, seq)
    grp = bq // CHUNK
    rows_t = CHUNK + C_TAIL

    def body(qt_ref, ka_ref, vt_ref, o_ref, lse_ref, acc_ref, s_ref):
        p, i = pl.program_id(0), pl.program_id(1)
        qtile = qt_ref[0]
        r16 = lax.broadcasted_iota(jnp.int32, (C_TAIL, bq), 0)

        def scores(t):
            at = pl.multiple_of(t * grp * 2 * CHUNK, 2 * CHUNK)
            return _dot(ka_ref[0, pl.ds(at, grp * 2 * CHUNK), :], qtile)

        def group(t, m, masked):
            ma, mb = m
            at = pl.multiple_of(t * grp * 2 * CHUNK, 2 * CHUNK)
            s = s_ref[...]
            if masked:
                s = jnp.where(_visible(s.shape, t * bq, i * bq), s, -jnp.inf)
            sa = [s[g * 2 * CHUNK:g * 2 * CHUNK + CHUNK] for g in range(grp)]
            sb = [s[g * 2 * CHUNK + CHUNK:(g + 1) * 2 * CHUNK] for g in range(grp)]
            na, nb = ma, mb
            for g in range(grp):
                na = jnp.maximum(na, jnp.max(sa[g], axis=0, keepdims=True))
                nb = jnp.maximum(nb, jnp.max(sb[g], axis=0, keepdims=True))
            al_a, al_b = jnp.exp(ma - na), jnp.exp(mb - nb)
            pt = jnp.concatenate([jnp.exp(x - n) for g in range(grp) for x, n in ((sa[g], na), (sb[g], nb))], axis=0)
            pv = _dot(vt_ref[0, :, pl.ds(at, grp * 2 * CHUNK)], pt.astype(BF16))
            tail = jnp.where(r16 == 2 * p, al_a, jnp.where(r16 == 2 * p + 1, al_b, 1.0))
            acc_ref[...] = acc_ref[...] * jnp.concatenate([_rows_ab(al_a, al_b, bq), tail], axis=0) + pv
            return na, nb

        def step(t, m):
            s_next = scores(t + 1)
            m = group(t, m, False)
            s_ref[...] = s_next
            return m

        acc_ref[...] = jnp.zeros_like(acc_ref)
        s_ref[...] = scores(0)
        m = (jnp.full((1, bq), -jnp.inf, F32), jnp.full((1, bq), -jnp.inf, F32))
        m = lax.fori_loop(0, i, step, m)
        ma, mb = group(i, m, True)
        tailv = acc_ref[CHUNK:rows_t, :]
        la = jnp.sum(jnp.where(r16 == 2 * p, tailv, 0.0), axis=0, keepdims=True)
        lb = jnp.sum(jnp.where(r16 == 2 * p + 1, tailv, 0.0), axis=0, keepdims=True)
        o_ref[...] = (acc_ref[0:CHUNK, :] * _rows_ab(1.0 / la, 1.0 / lb, bq)).T
        lse_ref[0, 0:1, :] = ma + jnp.log(la)
        lse_ref[0, 1:2, :] = mb + jnp.log(lb)

    return _call_carrying(
        carried, body, (qt, ka, vt),
        name="fox_fwd",
        grid=(C_PAIRS, seq // bq),
        in_specs=[
            pl.BlockSpec((1, 2 * CHUNK, bq), lambda p, i: (p, 0, i)),
            pl.BlockSpec((1, 2 * seq, 2 * CHUNK), lambda p, i: (p, 0, 0)),
            pl.BlockSpec((1, rows_t, 2 * seq), lambda p, i: (p, 0, 0)),
        ],
        out_specs=[pl.BlockSpec((bq, LANES), lambda p, i: (i, p)), pl.BlockSpec((1, 2, bq), lambda p, i: (p, 0, i))],
        out_shape=[jax.ShapeDtypeStruct((seq, C_WIDTH), F32), jax.ShapeDtypeStruct((C_PAIRS, 2, seq), F32)],
        scratch_shapes=[pltpu.VMEM((rows_t, bq), F32), pltpu.VMEM((grp * 2 * CHUNK, bq), F32)],
    )


def fox_bwd_prep(dy, o, proj, dproj):
    seq = o.shape[0]
    ind = np.zeros((C_WIDTH, LANES), np.float32)
    for h in range(C_HEADS):
        ind[h * C_HDIM:(h + 1) * C_HDIM, h] = 1.0
    ind = jnp.asarray(ind, BF16)
    sel = _piece_selectors()
    sel = jnp.asarray(np.stack([sel[2 * p].T + sel[2 * p + 1].T for p in range(C_PAIRS)]), BF16)

    def body(dy_ref, o_ref, z_ref, ind_ref, sel_ref, _, do_ref, dz_ref, dot_ref):
        dy_c, o_v, z = dy_ref[...], o_ref[...], z_ref[...]
        sg = jax.nn.sigmoid(z)
        do = dy_c * (z * sg)
        do_ref[...] = do.astype(BF16)
        dz_ref[...] = (dy_c * o_v * (sg * (1.0 + z * (1.0 - sg)))).astype(BF16)
        prod = do * o_v
        hi = prod.astype(BF16)
        lo = (prod - hi.astype(F32)).astype(BF16)
        delta = _dot(hi, ind_ref[...]) + _dot(lo, ind_ref[...])
        d3 = jnp.concatenate(_split3(delta.T), axis=0)
        for p in range(C_PAIRS):
            tail = _dot(sel_ref[p], d3).astype(BF16)
            dot_ref[p] = jnp.concatenate([do[:, p * LANES:(p + 1) * LANES].T.astype(BF16), tail], axis=0)

    return pl.pallas_call(
        body,
        name="fox_bwd_prep",
        grid=(seq // CHUNK,),
        in_specs=[
            pl.BlockSpec((CHUNK, C_WIDTH), lambda i: (i, 1)),
            pl.BlockSpec((CHUNK, C_WIDTH), lambda i: (i, 0)),
            pl.BlockSpec((CHUNK, C_WIDTH), lambda i: (i, 7)),
            _full((C_WIDTH, LANES)), _full((C_PAIRS, LANES, 3 * LANES)), _ANY,
        ],
        out_specs=[
            pl.BlockSpec((CHUNK, C_WIDTH), lambda i: (i, 0)),
            pl.BlockSpec((CHUNK, C_WIDTH), lambda i: (i, 7)),
            pl.BlockSpec((C_PAIRS, 2 * CHUNK, CHUNK), lambda i: (0, 0, i)),
        ],
        out_shape=[jax.ShapeDtypeStruct((seq, C_WIDTH), BF16), jax.ShapeDtypeStruct(dproj.shape, BF16),
                   jax.ShapeDtypeStruct((C_PAIRS, 2 * CHUNK, seq), BF16)],
        input_output_aliases={5: 1},
        compiler_params=_cparams("parallel"),
    )(dy, o, proj, ind, sel, dproj)


def fox_bwd(ka, va, kt, qt, dot_t, qa, dob, lse, carried=None):
    seq = qt.shape[2]
    nblk = seq // CHUNK
    bq = min(C_BQ, seq)
    nq = seq // bq
    kg = min(C_KG, nblk)
    ng = nblk // kg
    rows_t = CHUNK + C_TAIL

    def body(ka_ref, va_ref, kt_ref, qt_ref, dot_ref, qa_ref, do_ref, lse_ref,
             dq_ref, dk_ref, dv_ref, dck_ref, dcq_ref, dqt_acc, dv_acc, dka_acc):
        p, jg = pl.program_id(0), pl.program_id(1)

        @pl.when(jg == 0)
        def _():
            dqt_acc[...] = jnp.zeros_like(dqt_acc)

        dv_acc[...] = jnp.zeros_like(dv_acc)
        dka_acc[...] = jnp.zeros_like(dka_acc)

        def step(i, carry, masked):
            cols = pl.ds(pl.multiple_of(i * bq, bq), bq)
            qtile, dotile = qt_ref[0, :, cols], dot_ref[0, :, cols]
            do, qa_i = do_ref[cols, :], qa_ref[0, cols, :]
            lse2 = jnp.concatenate([jnp.broadcast_to(lse_ref[0, 0:1, cols], (CHUNK, bq)),
                                    jnp.broadcast_to(lse_ref[0, 1:2, cols], (CHUNK, bq))] * kg, axis=0)
            pt = jnp.exp(_dot(ka_ref[0], qtile) - lse2)
            if masked:
                pt = jnp.where(_visible(pt.shape, jg * kg * CHUNK, i * bq), pt, 0.0)
            ds = pt * _dot(va_ref[0], dotile)
            ptb, dsb = pt.astype(BF16), ds.astype(BF16)
            dv_acc[...] += _dot(ptb, do)
            dka_acc[...] += _dot(dsb, qa_i)
            dqt_acc[:, cols] += _dot(kt_ref[0], dsb)
            return carry

        i0 = (jg * kg * CHUNK) // bq
        step(i0, 0, True)
        lax.fori_loop(i0 + 1, nq, functools.partial(step, masked=False), 0)
        lane = lax.broadcasted_iota(jnp.int32, (CHUNK, LANES), 1)
        for kb in range(kg):
            rows = slice(kb * CHUNK, (kb + 1) * CHUNK)
            ra = slice(kb * 2 * CHUNK, kb * 2 * CHUNK + CHUNK)
            rb = slice(kb * 2 * CHUNK + CHUNK, (kb + 1) * 2 * CHUNK)
            dk_ref[rows, :] = jnp.where(lane < C_HDIM, dka_acc[ra, 0:LANES], dka_acc[rb, 0:LANES]).astype(BF16)
            dv_ref[rows, :] = jnp.where(lane < C_HDIM, dv_acc[ra, :], dv_acc[rb, :]).astype(BF16)
            dck_ref[0, rows, :] = (jnp.where(lane == 2 * p, dka_acc[ra, LANES:], 0.0)
                                   + jnp.where(lane == 2 * p + 1, dka_acc[rb, LANES:], 0.0))

        @pl.when(jg == ng - 1)
        def _():
            for c in range(nq):
                dq_ref[c * bq:(c + 1) * bq, :] = (dqt_acc[0:CHUNK, c * bq:(c + 1) * bq].T * (C_HDIM ** -0.5)).astype(BF16)
            dcq_ref[0] = dqt_acc[CHUNK:rows_t, :]

    per_pair = lambda r, c: pl.BlockSpec((1, r, c), lambda p, j: (p, 0, 0))
    by_rows = pl.BlockSpec((1, kg * 2 * CHUNK, 2 * CHUNK), lambda p, j: (p, j, 0))
    by_cols = pl.BlockSpec((1, rows_t, kg * 2 * CHUNK), lambda p, j: (p, 0, j))
    return _call_carrying(
        carried, body, (ka, va, kt, qt, dot_t, qa, dob, lse),
        name="fox_bwd",
        grid=(C_PAIRS, ng),
        in_specs=[by_rows, by_rows, by_cols, per_pair(2 * CHUNK, seq), per_pair(2 * CHUNK, seq),
                  per_pair(seq, 2 * CHUNK), pl.BlockSpec((seq, LANES), lambda p, j: (0, p)), per_pair(2, seq)],
        out_specs=[pl.BlockSpec((seq, LANES), lambda p, j: (0, p)),
                   pl.BlockSpec((kg * CHUNK, LANES), lambda p, j: (j, p)),
                   pl.BlockSpec((kg * CHUNK, LANES), lambda p, j: (j, p)),
                   pl.BlockSpec((1, kg * CHUNK, LANES), lambda p, j: (p, j, 0)),
                   per_pair(C_TAIL, seq)],
        out_shape=[jax.ShapeDtypeStruct((seq, C_WIDTH), BF16)] * 3
        + [jax.ShapeDtypeStruct((C_PAIRS, seq, LANES), F32), jax.ShapeDtypeStruct((C_PAIRS, C_TAIL, seq), F32)],
        scratch_shapes=[pltpu.VMEM((rows_t, seq), F32), pltpu.VMEM((kg * 2 * CHUNK, LANES), F32),
                        pltpu.VMEM((kg * 2 * CHUNK, 2 * CHUNK), F32)],
    )


def fox_post(dcq, dck, proj, bf_row, dproj):
    seq = proj.shape[0]
    nc = seq // CHUNK
    triu = jnp.asarray(np.triu(np.ones((CHUNK, CHUNK), np.float32)), BF16)

    def body(dq_ref, dk_ref, fl_ref, bf_ref, u_ref, _, dfl_ref, dbf_ref, carry_ref):
        @pl.when(pl.program_id(0) == 0)
        def _():
            carry_ref[...] = jnp.zeros_like(carry_ref)
            dbf_ref[...] = jnp.zeros_like(dbf_ref)

        rows = (dq_ref[0] + dq_ref[1]) + (dq_ref[2] + dq_ref[3])
        dc = jnp.concatenate([rows, jnp.zeros((CHUNK - C_TAIL, CHUNK), F32)], axis=0).T
        dc = dc - ((dk_ref[0] + dk_ref[1]) + (dk_ref[2] + dk_ref[3]))
        g = _exact_times(u_ref[...], dc, 3) + carry_ref[...]
        carry_ref[...] += jnp.sum(dc, axis=0, keepdims=True)
        dfl = g * jax.nn.sigmoid(-(fl_ref[:, :LANES] + bf_ref[...]))
        dbf_ref[...] += jnp.sum(dfl, axis=0, keepdims=True)
        dfl_ref[...] = jnp.concatenate([dfl, jnp.zeros_like(dfl)], axis=1).astype(BF16)

    rev = lambda n: nc - 1 - n
    return pl.pallas_call(
        body,
        name="fox_post",
        grid=(nc,),
        in_specs=[
            pl.BlockSpec((C_PAIRS, C_TAIL, CHUNK), lambda n: (0, 0, rev(n))),
            pl.BlockSpec((C_PAIRS, CHUNK, LANES), lambda n: (0, rev(n), 0)),
            pl.BlockSpec((CHUNK, 256), lambda n: (rev(n), 3)),
            _full((1, LANES)), _full((CHUNK, CHUNK)), _ANY,
        ],
        out_specs=[pl.BlockSpec((CHUNK, 256), lambda n: (rev(n), 3)), _full((1, LANES))],
        out_shape=[jax.ShapeDtypeStruct(dproj.shape, BF16), jax.ShapeDtypeStruct((1, LANES), F32)],
        input_output_aliases={5: 0},
        scratch_shapes=[pltpu.VMEM((1, LANES), F32)],
        compiler_params=_cparams("arbitrary"),
    )(dcq, dck, proj, bf_row, triu, dproj)


N_DEV = 8
MESH = pl.DeviceIdType.MESH
_ANY = pl.BlockSpec(memory_space=pl.ANY)


def _mesh_pos():
    return lax.axis_index("x"), lax.axis_index("y"), lax.axis_index("c")


def _dev_index(px, py, pc):
    return 4 * px + 2 * py + pc


def _row_pieces(ref, rows):
    return [ref.at[idx + (pl.ds(r, rows),)] for idx in np.ndindex(*ref.shape[:-2]) for r in range(0, ref.shape[-2], rows)]


class _Transfer:
    def __init__(self, src, dst, rows, send_sem, recv_sem, to):
        self.src, self.dst, self.rows, self.sems, self.to = src, dst, rows, (send_sem, recv_sem), to

    def _copy(self, src, dst):
        return pltpu.make_async_remote_copy(src_ref=src, dst_ref=dst, send_sem=self.sems[0], recv_sem=self.sems[1],
                                            device_id=self.to, device_id_type=MESH)

    def start(self):
        for s, d in zip(_row_pieces(self.src, self.rows), _row_pieces(self.dst, self.rows), strict=True):
            self._copy(s, d).start()

    def wait_send(self):
        self._copy(self.src, self.dst).wait_send()

    def wait_recv(self):
        self._copy(self.src, self.dst).wait_recv()


def _exchange_call(ex, name):
    n_in, n_out = len(ex.inputs), len(ex.out_shape)

    def body(*refs):
        parts = refs[:n_in], refs[n_in:n_in + n_out], refs[n_in + n_out:]
        ex.start(*parts)
        ex.finish(*parts)

    return pl.pallas_call(body, name=name, in_specs=[_ANY] * n_in, out_specs=[_ANY] * n_out, out_shape=ex.out_shape,
                          scratch_shapes=ex.scratch, input_output_aliases=getattr(ex, "aliases", {}))(*ex.inputs)


def _carried_refs(refs, n_in, n_out, ex):
    k_in, k_out, k_sem = (len(ex.inputs), len(ex.out_shape), len(ex.scratch)) if ex else (0, 0, 0)
    a, b, c = n_in + k_in, n_in + k_in + n_out, n_in + k_in + n_out + k_out
    own = refs[:n_in] + refs[a:b] + refs[c:len(refs) - k_sem]
    return own, (refs[n_in:a], refs[b:c], refs[len(refs) - k_sem:])


class AllGatherWeights:
    piece_rows = (128, 64)

    def __init__(self, wi, wo):
        self.inputs = (wi, wo)
        self.out_shape = [jax.ShapeDtypeStruct((N_DEV,) + wi.shape, wi.dtype), jax.ShapeDtypeStruct((N_DEV,) + wo.shape, wo.dtype)]
        self.scratch = [pltpu.SemaphoreType.DMA((2, 7)), pltpu.SemaphoreType.DMA((2, 7)), pltpu.SemaphoreType.DMA((2,))]

    def _plan(self, ins, outs, sems):
        send_sems, recv_sems, local_sems = sems
        x, y, c = _mesh_pos()
        me, sibling = (x, y, c), (x, y, 1 - c)
        chips = [(1 - x, y), (x, 1 - y), (1 - x, 1 - y)]
        both = range(2)

        def copy(a, k, block, to, own=False):
            slot = outs[a].at[_dev_index(*block)]
            return _Transfer(ins[a] if own else slot, slot, self.piece_rows[a], send_sems.at[a, k], recv_sems.at[a, k], to)

        mine = [pltpu.make_async_copy(ins[a], outs[a].at[_dev_index(*me)], local_sems.at[a]) for a in both]
        first = [copy(a, 1 + j, me, (*chip, c), own=True) for j, chip in enumerate(chips) for a in both]
        first += [copy(a, 0, me, sibling, own=True) for a in both]
        passed = [copy(a, 4 + j, (*chip, c), sibling) for j, chip in enumerate(chips) for a in both]
        return me, sibling, chips, c, copy, mine, first, passed

    def start(self, ins, outs, sems):
        *_, mine, first, _ = self._plan(ins, outs, sems)
        for cp in mine + first:
            cp.start()

    def finish(self, ins, outs, sems):
        me, sibling, chips, c, copy, mine, first, passed = self._plan(ins, outs, sems)
        for j, chip in enumerate(chips):
            for a in range(2):
                copy(a, 1 + j, (*chip, c), me).wait_recv()
            for a in range(2):
                passed[2 * j + a].start()
        for a in range(2):
            copy(a, 0, sibling, me).wait_recv()
        for j, chip in enumerate(chips):
            for a in range(2):
                copy(a, 4 + j, (*chip, 1 - c), me).wait_recv()
        for cp in first + passed:
            cp.wait_send()
        for cp in mine:
            cp.wait()


N_CHIP = 4


class PairExchange:
    def __init__(self, by_core, whole=()):
        self.inputs = tuple(by_core) + tuple(whole)
        self.n_by_core = len(by_core)
        self.out_shape = ([jax.ShapeDtypeStruct(a.shape[1:], a.dtype) for a in by_core]
                          + [jax.ShapeDtypeStruct(a.shape, a.dtype) for a in whole])
        n = len(self.inputs)
        self.scratch = [pltpu.SemaphoreType.DMA((n,)), pltpu.SemaphoreType.DMA((n,))]

    def _copies(self, ins, outs, sems):
        x, y, c = _mesh_pos()
        srcs = [r.at[1 - c] if a < self.n_by_core else r for a, r in enumerate(ins)]
        return [_Transfer(srcs[a], outs[a], outs[a].shape[-2], sems[0].at[a], sems[1].at[a], (x, y, 1 - c))
                for a in range(len(ins))]

    def start(self, ins, outs, sems):
        for cp in self._copies(ins, outs, sems):
            cp.start()

    def finish(self, ins, outs, sems):
        copies = self._copies(ins, outs, sems)
        for cp in copies:
            cp.wait_recv()
        for cp in copies:
            cp.wait_send()


def pair_sum(own, other, dtype, rows, name, core, layer, depth, stacked=None):
    n, n_r, n_c = other.shape

    def body(core_ref, a_ref, b_ref, *refs):
        refs[-1][0, 0] = (a_ref[0, 0] + b_ref[0]).astype(dtype)

    carried = () if stacked is None else (stacked,)
    grid_spec = pltpu.PrefetchScalarGridSpec(
        num_scalar_prefetch=1,
        grid=(n, n_r // rows),
        in_specs=[pl.BlockSpec((1, 1, rows, n_c), lambda i, r, s: (s[0], i, r, 0)),
                  pl.BlockSpec((1, rows, n_c), lambda i, r, s: (i, r, 0))] + [_ANY] * len(carried),
        out_specs=pl.BlockSpec((1, 1, rows, n_c), lambda i, r, s: (i, layer, r, 0)),
    )
    return pl.pallas_call(
        body,
        name=name,
        grid_spec=grid_spec,
        out_shape=jax.ShapeDtypeStruct((n, depth, n_r, n_c), dtype),
        input_output_aliases={3: 0} if carried else {},
        compiler_params=_cparams("parallel", "parallel"),
    )(core, own, other, *carried)


def small_sum(a, b, name):
    def body(a_ref, b_ref, o_ref):
        o_ref[...] = a_ref[...] + b_ref[...]

    return pl.pallas_call(body, name=name, out_shape=jax.ShapeDtypeStruct(a.shape, a.dtype))(a, b)


class ChipExchange:
    def __init__(self, by_chip=(), layers=(), gathered=(), stacked=()):
        stacked = tuple(stacked) or (None,) * len(by_chip)
        kept = [s for s in stacked if s is not None]
        self.inputs = tuple(by_chip) + tuple(gathered) + tuple(kept)
        self.n_by_chip, self.n_gathered = len(by_chip), len(gathered)
        self.items = [(a, l) for a in range(len(by_chip)) for l in layers[a]] + [(self.n_by_chip + g, None) for g in range(len(gathered))]
        self.out_shape = ([jax.ShapeDtypeStruct((N_CHIP - 1,) + a.shape[1:], a.dtype) for a in by_chip]
                          + [jax.ShapeDtypeStruct((N_CHIP,) + a.shape, a.dtype) for a in gathered])
        at = iter(range(self.n_by_chip + self.n_gathered, len(self.inputs)))
        self.aliases = {next(at): a for a, s in enumerate(stacked) if s is not None}
        n = len(self.items)
        self.scratch = [pltpu.SemaphoreType.DMA((n, 3)), pltpu.SemaphoreType.DMA((n, 3)),
                        pltpu.SemaphoreType.DMA((max(self.n_gathered, 1),))]

    def _plan(self, ins, outs, sems):
        x, y, c = _mesh_pos()
        chip = 2 * x + y
        n = len(self.items)

        def copy(i, k, sending):
            a, layer = self.items[i]
            px, py = x ^ ((k >> 1) & 1), y ^ (k & 1)
            if layer is not None:
                src, dst = ins[a].at[2 * px + py, layer], outs[a].at[k - 1, layer]
            else:
                src, dst = ins[a], outs[a].at[chip if sending else 2 * px + py]
            return _Transfer(src, dst, dst.shape[-2], sems[0].at[i, k - 1], sems[1].at[i, k - 1], (px, py, c))

        local = [pltpu.make_async_copy(ins[a], outs[a].at[chip], sems[2].at[a - self.n_by_chip])
                 for a in range(self.n_by_chip, self.n_by_chip + self.n_gathered)]
        return n, copy, local

    def start(self, ins, outs, sems):
        n, copy, local = self._plan(ins, outs, sems)
        for cp in local:
            cp.start()
        for k in range(1, N_CHIP):
            for a in range(n):
                copy(a, k, True).start()

    def finish(self, ins, outs, sems):
        n, copy, local = self._plan(ins, outs, sems)
        for k in range(1, N_CHIP):
            for a in range(n):
                copy(a, k, False).wait_recv()
        for k in range(1, N_CHIP):
            for a in range(n):
                copy(a, k, True).wait_send()
        for cp in local:
            cp.wait()


ADAM_LR = 0.001
ADAM_B1 = 0.9
ADAM_B2 = 0.999
ADAM_EPS = 1e-08
ADAM_WD = 0.01
ADAM_STEP = 10


def adam_reduce(parts, w, m, v, rows, name, own=None, chip=None):
    n_l, n_r, n_c = w.shape
    n_parts = parts.shape[0]

    def body(*refs):
        p_ref, w_ref, m_ref, v_ref, g_ref, d_ref, m2_ref, v2_ref = refs[-8:]
        g = p_ref[0, 0].astype(F32)
        if own is not None:
            g = refs[-9][...].reshape(rows, n_c).astype(F32) + g
        for d in range(1, n_parts):
            g = g + p_ref[d, 0].astype(F32)
        m2 = ADAM_B1 * m_ref[0] + (1.0 - ADAM_B1) * g
        v2 = ADAM_B2 * v_ref[0] + (1.0 - ADAM_B2) * (g * g)
        m_hat = m2 / (1.0 - ADAM_B1 ** ADAM_STEP)
        v_hat = v2 / (1.0 - ADAM_B2 ** ADAM_STEP)
        g_ref[0] = g
        d_ref[0] = -ADAM_LR * (m_hat / (jnp.sqrt(v_hat) + ADAM_EPS) + ADAM_WD * w_ref[0])
        m2_ref[0] = m2
        v2_ref[0] = v2

    blk = lambda: pl.BlockSpec((1, rows, n_c), lambda l, r, *_: (l, r, 0))
    in_specs = [pl.BlockSpec((n_parts, 1, rows, n_c), lambda l, r, *_: (0, l, r, 0)), blk(), blk(), blk()]
    args = (parts, w, m, v)
    if own is not None:
        in_specs = [pl.BlockSpec((1, 1, rows, n_c), lambda l, r, s: (s[0], l, r, 0))] + in_specs
        args = (chip, own) + args
    grid_spec = pltpu.PrefetchScalarGridSpec(
        num_scalar_prefetch=0 if own is None else 1, grid=(n_l, n_r // rows), in_specs=in_specs,
        out_specs=[blk(), blk(), blk(), blk()])
    return pl.pallas_call(
        body,
        name=name,
        grid_spec=grid_spec,
        out_shape=[jax.ShapeDtypeStruct(w.shape, F32)] * 4,
        compiler_params=_cparams("parallel", "parallel"),
    )(*args)


_SMALL = (("norm_g", (2, 1024)), ("gmlp_ln_g", (2, 4, 64)), ("gmlp_ln_b", (2, 4, 64)),
          ("gmlp_b_s", (2, 4, 128)), ("hgrn_lb", (2, 256)), ("hgrn_onorm_g", (2, 64)), ("fox_b_f", (2, 8)),
          ("final_norm_g", (1024,)), ("loss", ()))


def _padded(n):
    return -(-n // LANES) * LANES


_SMALL_ROWS = -(-sum(_padded(int(np.prod(s))) for _, s in _SMALL) // LANES // 8) * 8


def _pack_small(vals):
    flat = []
    for (name, shape), a in zip(_SMALL, vals, strict=True):
        n = int(np.prod(shape))
        flat.append(jnp.pad(a.reshape(n).astype(F32), (0, _padded(n) - n)))
    flat = jnp.concatenate(flat)
    return jnp.pad(flat, (0, _SMALL_ROWS * LANES - flat.shape[0])).reshape(_SMALL_ROWS, LANES)


def _unpack_small(slab):
    flat, out, at = slab.reshape(-1), {}, 0
    for name, shape in _SMALL:
        n = int(np.prod(shape))
        out[name] = flat[at:at + n].reshape(shape)
        at += _padded(n)
    return out


def kernel(x, norm_g, w_in, w_out, gmlp_ln_g, gmlp_ln_b, gmlp_w_s, gmlp_b_s, hgrn_lb, hgrn_onorm_g, fox_b_f, final_norm_g, loss_target, m_norm_g, m_w_in, m_w_out, m_gmlp_ln_g, m_gmlp_ln_b, m_gmlp_w_s, m_gmlp_b_s, m_hgrn_lb, m_hgrn_onorm_g, m_fox_b_f, m_final_norm_g, v_norm_g, v_w_in, v_w_out, v_gmlp_ln_g, v_gmlp_ln_b, v_gmlp_w_s, v_gmlp_b_s, v_hgrn_lb, v_hgrn_onorm_g, v_fox_b_f, v_final_norm_g):
    depth = w_in.shape[0]
    seq = x.shape[1]
    assert w_in.shape[2] * N_DEV == N_IN
    xs, tgt = x[0], loss_target[0]

    wi_blk, wo_blk = w_in.astype(BF16), w_out.astype(BF16)
    wi_all, wo_all = _exchange_call(AllGatherWeights(wi_blk[0], wo_blk[0]), "allgather_weights_0")

    ln_g = gmlp_ln_g.reshape(depth, 1, A_WIDTH)
    ln_b = gmlp_ln_b.reshape(depth, 1, A_WIDTH)
    bs_t = jnp.pad(jnp.transpose(gmlp_b_s, (0, 2, 1)), ((0, 0), (0, 0), (0, LANES - A_GROUPS)))
    lb0, lb1 = hgrn_lb[0:1], hgrn_lb[1:2]
    onorm = jnp.tile(hgrn_onorm_g, (1, B_HEADS)).reshape(depth, 1, B_WIDTH)
    bf_row = jnp.pad(fox_b_f, ((0, 0), (0, LANES - C_HEADS))).reshape(depth, 1, LANES)

    core = lax.axis_index("c").astype(jnp.int32).reshape(1)
    chip = (2 * lax.axis_index("x") + lax.axis_index("y")).astype(jnp.int32).reshape(1)

    saved = []
    xc = xs
    for l in range(depth):
        wi_int = assemble_w_in(wi_all[:, None])
        proj, h = inproj(xc, norm_g[l:l + 1], wi_int, 0)
        ya = gmlp_fwd(proj, ln_g[l], ln_b[l], gmlp_w_s[l], bs_t[l])
        yb, states = hgrn_fwd(proj, lb0, lb1, onorm[l], l)
        ka, va, vt, kt, qt, qa = fox_prep(proj, bf_row[l])
        nxt = AllGatherWeights(wi_blk[l + 1], wo_blk[l + 1]) if l + 1 < depth else None
        o, lse, *gathered = fox_fwd(qt, ka, vt, nxt)
        xn, yfull = outproj(xc, ya, yb, o, proj, wo_all[:, None], 0)
        saved.append((xc, proj, h, states, ka, va, kt, qt, qa, o, lse, yfull, wi_int, wo_all))
        if gathered:
            wi_all, wo_all = gathered
        xc = xn

    dx, d_final_g, loss_tile = final_loss(xc, final_norm_g[None], tgt)

    n_shard = w_in.shape[2]
    g_norm = [None] * depth
    g_ln_g, g_ln_b, g_ws, g_bs, g_on, g_bf = ([None] * depth for _ in range(6))
    g_lb0, g_lb1 = jnp.zeros_like(lb0), jnp.zeros_like(lb1)
    swi = swo = rwi = rwo = None
    for l in reversed(range(depth)):
        x_in, proj, h, states, ka, va, kt, qt, qa, o, lse, yfull, wi_int, wo_l = saved[l]
        dy, gwo = outproj_bwd(dx, yfull, wo_l[:, None], 0)
        gwo = gwo[:, :, 0]
        (qwo,) = _exchange_call(PairExchange([gwo]), f"pair_exchange_w_out_{l}")
        swo = pair_sum(gwo, qwo, BF16, gwo.shape[2], "pair_sum_w_out", core, l, depth, swo)
        dproj, g_ln_g[l], g_ln_b[l], g_ws[l], dbs_t = gmlp_bwd(proj, dy, ln_g[l], ln_b[l], gmlp_w_s[l], bs_t[l])
        g_bs[l] = dbs_t[:, :A_GROUPS].T
        dproj, d0, d1, don = hgrn_bwd(proj, states, dy, lb0, lb1, onorm[l], l, dproj)
        g_lb0, g_lb1 = g_lb0 + d0, g_lb1 + d1
        g_on[l] = don.reshape(B_HEADS, B_KDIM).sum(0)
        dob, dproj, dot_t = fox_bwd_prep(dy, o, proj, dproj)
        top = l == depth - 1
        ride = (ChipExchange([swo], [(l,)], stacked=[rwo]) if top else
                ChipExchange([swi, swo], [(l + 1,), (l,)], stacked=[rwi, rwo]))
        outs = fox_bwd(ka, va, kt, qt, dot_t, qa, dob, lse, ride)
        dqkv, (dck, dcq) = outs[:3], outs[3:5]
        if top:
            (rwo,) = outs[5:]
        else:
            rwi, rwo = outs[5:]
        dproj, dbf = fox_post(dcq, dck, proj, bf_row[l], dproj)
        g_bf[l] = dbf[0, :C_HEADS]
        gwi = split_w_in_grad(inproj_bwd_w(h, dproj, dqkv), n_shard)[:, :, 0]
        (qwi,) = _exchange_call(PairExchange([gwi]), f"pair_exchange_w_in_{l}")
        swi = pair_sum(gwi, qwi, BF16, 256, "pair_sum_w_in", core, l, depth, swi)
        ride = ChipExchange([swi], [(l,)], stacked=[rwi]) if l == 0 else None
        outs = inproj_bwd_x(dproj, dqkv, wi_int, x_in, norm_g[l:l + 1], dx, 0, ride)
        dx, g_norm[l] = outs[:2]
        if ride is not None:
            (rwi,) = outs[2:]

    gsm = _pack_small([
        jnp.concatenate(g_norm), jnp.stack(g_ln_g), jnp.stack(g_ln_b), jnp.stack(g_bs),
        jnp.concatenate([g_lb0, g_lb1]), jnp.stack(g_on), jnp.stack(g_bf), d_final_g, loss_tile[0, 0]])
    gws = jnp.stack(g_ws).reshape(-1, LANES)
    qsm, qws = _exchange_call(PairExchange([], [gsm, gws]), "pair_exchange_small")
    ssm, sws = small_sum(gsm, qsm, "pair_sum_small"), small_sum(gws, qws, "pair_sum_w_s")
    rsm, rws = _exchange_call(ChipExchange(gathered=[ssm, sws]), "chip_exchange_small")

    small_w = (norm_g, gmlp_ln_g, gmlp_ln_b, gmlp_b_s, hgrn_lb, hgrn_onorm_g, fox_b_f, final_norm_g)
    small_m = (m_norm_g, m_gmlp_ln_g, m_gmlp_ln_b, m_gmlp_b_s, m_hgrn_lb, m_hgrn_onorm_g, m_fox_b_f, m_final_norm_g)
    small_v = (v_norm_g, v_gmlp_ln_g, v_gmlp_ln_b, v_gmlp_b_s, v_hgrn_lb, v_hgrn_onorm_g, v_fox_b_f, v_final_norm_g)
    zero = jnp.zeros((), F32)
    res_wi = adam_reduce(rwi, w_in, m_w_in, v_w_in, 256, "adam_w_in", own=swi, chip=chip)
    res_wo = adam_reduce(rwo, w_out, m_w_out, v_w_out, w_out.shape[1], "adam_w_out", own=swo, chip=chip)
    res_sm = adam_reduce(rsm[:, None], _pack_small(small_w + (zero,))[None], _pack_small(small_m + (zero,))[None],
                         _pack_small(small_v + (zero,))[None], _SMALL_ROWS, "adam_small")
    res_sm = [_unpack_small(r[0]) for r in res_sm]
    as_rows = lambda a: a.reshape(1, -1, LANES)
    res_ws = adam_reduce(rws[:, None], as_rows(gmlp_w_s), as_rows(m_gmlp_w_s), as_rows(v_gmlp_w_s), rws.shape[1], "adam_w_s")
    for s, r in zip(res_sm, res_ws, strict=True):
        s["gmlp_w_s"] = r.reshape(gmlp_w_s.shape)

    def group(i):
        s = res_sm[i]
        return [s["norm_g"], res_wi[i], res_wo[i], s["gmlp_ln_g"], s["gmlp_ln_b"], s["gmlp_w_s"], s["gmlp_b_s"],
                s["hgrn_lb"], s["hgrn_onorm_g"], s["fox_b_f"], s["final_norm_g"]]

    return (res_sm[0]["loss"], dx[None], *group(0), *group(1), *group(2), *group(3))
```

```python
import functools

import jax
import jax.numpy as jnp
import numpy as np
from jax import lax
from jax.experimental import pallas as pl
from jax.experimental.pallas import tpu as pltpu

F32 = jnp.float32
BF16 = jnp.bfloat16

NORM_EPS = 1e-6
F_FLOOR = 1e-30
CHUNK = 128
LANES = 128
VMEM_LIMIT = 56 * 1024 * 1024


def _cparams(*sem):
    return pltpu.CompilerParams(dimension_semantics=sem, vmem_limit_bytes=VMEM_LIMIT)


def _dot(a, b, dims=(((1,), (0,)), ((), ())), precision=None):
    return lax.dot_general(a, b, dims, precision=precision, preferred_element_type=F32)


_NT = (((1,), (1,)), ((), ()))
_TN = (((0,), (0,)), ((), ()))


def _bf16_pieces(x, n):
    out, r = [], x
    for i in range(n):
        out.append(r.astype(BF16))
        if i + 1 < n:
            r = r - out[-1].astype(F32)
    return out


@functools.partial(jax.custom_vjp, nondiff_argnums=(2,))
def _times_exact(x, e, n):
    return functools.reduce(jnp.add, [_dot(p, e) for p in _bf16_pieces(x, n)])


def _times_exact_fwd(x, e, n):
    return _times_exact(x, e, n), e


def _times_exact_bwd(n, e, g):
    dx = functools.reduce(jnp.add, [lax.dot_general(p, e, _NT, preferred_element_type=F32) for p in _bf16_pieces(g, n)])
    return dx, jnp.zeros_like(e)


_times_exact.defvjp(_times_exact_fwd, _times_exact_bwd)


@functools.partial(jax.custom_vjp, nondiff_argnums=(2,))
def _exact_times(e, x, n):
    return functools.reduce(jnp.add, [_dot(e, p) for p in _bf16_pieces(x, n)])


def _exact_times_fwd(e, x, n):
    return _exact_times(e, x, n), e


def _exact_times_bwd(n, e, g):
    dx = functools.reduce(jnp.add, [lax.dot_general(e, p, _TN, preferred_element_type=F32) for p in _bf16_pieces(g, n)])
    return jnp.zeros_like(e), dx


_exact_times.defvjp(_exact_times_fwd, _exact_times_bwd)


def _group_mean_matrix(width, group):
    idx = np.arange(width) // group
    return jnp.asarray((idx[:, None] == idx[None, :]).astype(np.float32) / group, BF16)


def _group_ones_matrix(width, group):
    idx = np.arange(width) // group
    return jnp.asarray((idx[:, None] == idx[None, :]).astype(np.float32), BF16)


A_WIDTH = 256
A_GROUPS = 4
A_GDIM = 64


A_ROWS = 512


def _gmlp_chunk(x3, ln_g, ln_b, w_s, bs_t, mean_m, gind):
    n = x3.shape[0] // CHUNK
    u = jax.nn.gelu(x3[:, :A_WIDTH])
    v = jax.nn.gelu(x3[:, A_WIDTH:2 * A_WIDTH])
    z = x3[:, 2 * A_WIDTH:]
    mu = _times_exact(v, mean_m, 2)
    d = v - mu
    var = _times_exact(d * d, mean_m, 2)
    vn = d * lax.rsqrt(var + NORM_EPS) * ln_g + ln_b
    vnb = vn.astype(BF16)
    wide = jnp.concatenate([vnb[i * CHUNK:(i + 1) * CHUNK] for i in range(n)], axis=1)
    row = lax.broadcasted_iota(jnp.int32, (CHUNK, CHUNK), 0)
    col = lax.broadcasted_iota(jnp.int32, (CHUNK, CHUNK), 1)
    causal = row >= col
    lane_g = lax.shift_right_logical(lax.broadcasted_iota(jnp.int32, (CHUNK, n * A_WIDTH), 1), 6) & (A_GROUPS - 1)
    bias = _times_exact(bs_t, gind, 3)
    mixed = jnp.concatenate([bias] * n, axis=1)
    for g in range(A_GROUPS):
        wc = jnp.where(causal, w_s[g], 0.0).astype(BF16)
        mixed = mixed + jnp.where(lane_g == g, _dot(wc, wide), 0.0)
    mixed = jnp.concatenate([mixed[:, i * A_WIDTH:(i + 1) * A_WIDTH] for i in range(n)], axis=0)
    return u * mixed * jax.nn.silu(z)


def _gmlp_consts():
    gind = np.zeros((LANES, A_WIDTH), np.float32)
    for g in range(A_GROUPS):
        gind[g, g * A_GDIM:(g + 1) * A_GDIM] = 1.0
    return _group_mean_matrix(A_WIDTH, A_GDIM), jnp.asarray(gind, BF16)


def _full(shape):
    return pl.BlockSpec(shape, lambda *_: (0,) * len(shape))


def gmlp_fwd(proj, ln_g, ln_b, w_s, bs_t):
    seq = proj.shape[0]
    rows = min(A_ROWS, seq)
    mean_m, gind = _gmlp_consts()

    def body(x_ref, g_ref, b_ref, w_ref, bs_ref, m_ref, gi_ref, y_ref):
        y = _gmlp_chunk(x_ref[...], g_ref[...], b_ref[...], w_ref[...], bs_ref[...], m_ref[...], gi_ref[...])
        y_ref[...] = y.astype(BF16)

    return pl.pallas_call(
        body,
        name="gmlp_fwd",
        grid=(seq // rows,),
        in_specs=[
            pl.BlockSpec((rows, 3 * A_WIDTH), lambda n: (n, 0)),
            _full((1, A_WIDTH)), _full((1, A_WIDTH)), _full((A_GROUPS, CHUNK, CHUNK)), _full((CHUNK, LANES)),
            _full((A_WIDTH, A_WIDTH)), _full((LANES, A_WIDTH)),
        ],
        out_specs=pl.BlockSpec((rows, A_WIDTH), lambda n: (n, 0)),
        out_shape=jax.ShapeDtypeStruct((seq, A_WIDTH), BF16),
        compiler_params=_cparams("parallel"),
    )(proj, ln_g, ln_b, w_s, bs_t, mean_m, gind)


def gmlp_bwd(proj, dy, ln_g, ln_b, w_s, bs_t):
    seq = proj.shape[0]
    rows = min(A_ROWS, seq)
    mean_m, gind = _gmlp_consts()

    def body(x_ref, dy_ref, g_ref, b_ref, w_ref, bs_ref, m_ref, gi_ref, dx_ref, dg_ref, db_ref, dw_ref, dbs_ref):
        fn = functools.partial(_gmlp_chunk, mean_m=m_ref[...], gind=gi_ref[...])
        _, vjp = jax.vjp(fn, x_ref[...], g_ref[...], b_ref[...], w_ref[...], bs_ref[...])
        dx, dg, db, dw, dbs = vjp(dy_ref[...])
        dx_ref[...] = dx.astype(BF16)

        @pl.when(pl.program_id(0) == 0)
        def _():
            dg_ref[...] = jnp.zeros_like(dg_ref)
            db_ref[...] = jnp.zeros_like(db_ref)
            dw_ref[...] = jnp.zeros_like(dw_ref)
            dbs_ref[...] = jnp.zeros_like(dbs_ref)

        dg_ref[...] += dg
        db_ref[...] += db
        dw_ref[...] += dw
        dbs_ref[...] += dbs

    return pl.pallas_call(
        body,
        name="gmlp_bwd",
        grid=(seq // rows,),
        in_specs=[
            pl.BlockSpec((rows, 3 * A_WIDTH), lambda n: (n, 0)),
            pl.BlockSpec((rows, A_WIDTH), lambda n: (n, 0)),
            _full((1, A_WIDTH)), _full((1, A_WIDTH)), _full((A_GROUPS, CHUNK, CHUNK)), _full((CHUNK, LANES)),
            _full((A_WIDTH, A_WIDTH)), _full((LANES, A_WIDTH)),
        ],
        out_specs=[
            pl.BlockSpec((rows, 3 * A_WIDTH), lambda n: (n, 0)),
            _full((1, A_WIDTH)), _full((1, A_WIDTH)), _full((A_GROUPS, CHUNK, CHUNK)), _full((CHUNK, LANES)),
        ],
        out_shape=[
            jax.ShapeDtypeStruct((seq, D_INT), BF16),
            jax.ShapeDtypeStruct((1, A_WIDTH), F32), jax.ShapeDtypeStruct((1, A_WIDTH), F32),
            jax.ShapeDtypeStruct((A_GROUPS, CHUNK, CHUNK), F32), jax.ShapeDtypeStruct((CHUNK, LANES), F32),
        ],
        compiler_params=_cparams("arbitrary"),
    )(proj, dy, ln_g, ln_b, w_s, bs_t, mean_m, gind)


B_WIDTH = 256
B_HEADS = 4
B_KDIM = 64
B_LEVELS = (64, 32, 16, 8, 4, 2, 1)


def _hgrn_consts():
    t = np.arange(CHUNK)
    u = t[None, :]
    mats = [np.tril(np.ones((CHUNK, CHUNK), np.float32))]
    for m in B_LEVELS:
        p = (t // (2 * m)) * (2 * m) + m - 1
        right = (t % (2 * m)) >= m
        sel = np.where(right[:, None], (u > p[:, None]) & (u <= t[:, None]), (u > t[:, None]) & (u <= p[:, None]))
        mats.append(sel.astype(np.float32))
    return jnp.asarray(np.concatenate(mats, 0), BF16), _group_ones_matrix(B_WIDTH, B_KDIM)


def _hgrn_lower_bound(lb0, lb1, layer):
    mx = jnp.maximum(lb0, lb1)
    e0 = jnp.exp(lb0 - mx)
    e1 = jnp.exp(lb1 - mx)
    p0 = e0 / (e0 + e1)
    p1 = e1 / (e0 + e1)
    cs = p0 if layer == 0 else p0 + p1
    return jnp.clip(cs - p0, 0.0, 1.0 - 1e-6)


def _hgrn_chunk(x4, st, lb0, lb1, onorm, layer, tstack, ones_bd):
    q_raw, fl, v, zg = (x4[:, i * B_WIDTH:(i + 1) * B_WIDTH] for i in range(4))
    lb = _hgrn_lower_bound(lb0, lb1, layer)
    q = jax.nn.silu(q_raw) * (B_KDIM ** -0.5)
    f = lb + (1.0 - lb) * jax.nn.sigmoid(fl)
    logf = jnp.log(jnp.maximum(f, F_FLOOR))
    k = (1.0 - lb) * jax.nn.sigmoid(-fl)
    b = _exact_times(tstack[:CHUNK], logf, 3)
    dall = jnp.concatenate([b, _exact_times(tstack[CHUNK:], logf, 2)], axis=0)
    b_last = jnp.sum(logf, axis=0, keepdims=True)
    vb = v.astype(BF16)

    lane_h = lax.shift_right_logical(lax.broadcasted_iota(jnp.int32, (CHUNK, B_WIDTH), 1), 6)
    row = lax.broadcasted_iota(jnp.int32, (CHUNK, B_WIDTH), 0)
    srow = lax.broadcasted_iota(jnp.int32, (B_HEADS * CHUNK, CHUNK), 0) & (CHUNK - 1)
    scol = lax.broadcasted_iota(jnp.int32, (B_HEADS * CHUNK, CHUNK), 1)

    def heads_on_rows(a):
        return jnp.concatenate([jnp.where(lane_h == h, a, 0.0) for h in range(B_HEADS)], axis=0)

    def heads_from_rows(r):
        out = jnp.where(lane_h == 0, r[:CHUNK], 0.0)
        for h in range(1, B_HEADS):
            out = out + jnp.where(lane_h == h, r[h * CHUNK:(h + 1) * CHUNK], 0.0)
        return out

    o = lax.dot_general((q * jnp.exp(b)).astype(BF16), st.astype(BF16), _NT, preferred_element_type=F32)
    scores = jnp.zeros((B_HEADS * CHUNK, CHUNK), F32)
    for li, m in enumerate(B_LEVELS):
        e = jnp.exp(dall[(li + 1) * CHUNK:(li + 2) * CHUNK])
        right = (row & (2 * m - 1)) >= m
        qt = jnp.where(right, q * e, 0.0)
        kt = jnp.where(right, 0.0, k * e)
        sc = lax.dot_general(heads_on_rows(qt).astype(BF16), kt.astype(BF16), _NT, preferred_element_type=F32)
        sh = int(np.log2(2 * m))
        same = lax.shift_right_logical(srow, sh) == lax.shift_right_logical(scol, sh)
        scores = scores + jnp.where(same, sc, 0.0)
    o = o + heads_from_rows(_dot(scores.astype(BF16), vb))
    o = o + _times_exact(q * k, ones_bd, 2) * v

    kv = lax.dot_general(vb, (k * jnp.exp(b_last - b)).astype(BF16), _TN, preferred_element_type=F32)
    st_new = st * jnp.exp(b_last) + jnp.where(ones_bd > 0.5, kv, 0.0)

    ms = _times_exact(o * o, ones_bd, 2) * (1.0 / B_KDIM)
    y = o * lax.rsqrt(ms + NORM_EPS) * onorm * jax.nn.silu(zg)
    return y, st_new


B_ROWS = 256


def _hgrn_rows(x4, st, lb0, lb1, onorm, layer, tstack, ones_bd):
    ys = []
    for i in range(x4.shape[0] // CHUNK):
        y, st = _hgrn_chunk(x4[i * CHUNK:(i + 1) * CHUNK], st, lb0, lb1, onorm, layer, tstack, ones_bd)
        ys.append(y)
    return jnp.concatenate(ys, axis=0), st


def hgrn_fwd(proj, lb0, lb1, onorm, layer):
    seq = proj.shape[0]
    rows = min(B_ROWS, seq)
    nc = seq // rows
    tstack, ones_bd = _hgrn_consts()

    def body(x_ref, lb0_ref, lb1_ref, on_ref, t_ref, e_ref, y_ref, st_out_ref, st_ref):
        @pl.when(pl.program_id(0) == 0)
        def _():
            st_ref[...] = jnp.zeros_like(st_ref)

        st = st_ref[...]
        st_out_ref[0] = st
        y, st_new = _hgrn_rows(x_ref[...], st, lb0_ref[...], lb1_ref[...], on_ref[...], layer, t_ref[...], e_ref[...])
        y_ref[...] = y.astype(BF16)
        st_ref[...] = st_new

    return pl.pallas_call(
        body,
        name=f"hgrn_fwd_{layer}",
        grid=(nc,),
        in_specs=[
            pl.BlockSpec((rows, 4 * B_WIDTH), lambda n: (n, 1)),
            _full((1, B_WIDTH)), _full((1, B_WIDTH)), _full((1, B_WIDTH)),
            _full(((len(B_LEVELS) + 1) * CHUNK, CHUNK)), _full((B_WIDTH, B_WIDTH)),
        ],
        out_specs=[
            pl.BlockSpec((rows, B_WIDTH), lambda n: (n, 0)),
            pl.BlockSpec((1, B_WIDTH, B_WIDTH), lambda n: (n, 0, 0)),
        ],
        out_shape=[jax.ShapeDtypeStruct((seq, B_WIDTH), BF16), jax.ShapeDtypeStruct((nc, B_WIDTH, B_WIDTH), F32)],
        scratch_shapes=[pltpu.VMEM((B_WIDTH, B_WIDTH), F32)],
        compiler_params=_cparams("arbitrary"),
    )(proj, lb0, lb1, onorm, tstack, ones_bd)


def hgrn_bwd(proj, states, dy, lb0, lb1, onorm, layer, dproj):
    seq = proj.shape[0]
    rows = min(B_ROWS, seq)
    nc = seq // rows
    tstack, ones_bd = _hgrn_consts()

    def body(x_ref, st_in_ref, dy_ref, lb0_ref, lb1_ref, on_ref, t_ref, e_ref, _, dx_ref, d0_ref, d1_ref, don_ref, dst_ref):
        @pl.when(pl.program_id(0) == 0)
        def _():
            dst_ref[...] = jnp.zeros_like(dst_ref)
            d0_ref[...] = jnp.zeros_like(d0_ref)
            d1_ref[...] = jnp.zeros_like(d1_ref)
            don_ref[...] = jnp.zeros_like(don_ref)

        fn = functools.partial(_hgrn_rows, layer=layer, tstack=t_ref[...], ones_bd=e_ref[...])
        _, vjp = jax.vjp(fn, x_ref[...], st_in_ref[0], lb0_ref[...], lb1_ref[...], on_ref[...])
        dx, dst, d0, d1, don = vjp((dy_ref[...], dst_ref[...]))
        dx_ref[...] = dx.astype(BF16)
        dst_ref[...] = dst
        d0_ref[...] += d0
        d1_ref[...] += d1
        don_ref[...] += don

    rev = lambda n: nc - 1 - n
    return pl.pallas_call(
        body,
        name=f"hgrn_bwd_{layer}",
        grid=(nc,),
        in_specs=[
            pl.BlockSpec((rows, 4 * B_WIDTH), lambda n: (rev(n), 1)),
            pl.BlockSpec((1, B_WIDTH, B_WIDTH), lambda n: (rev(n), 0, 0)),
            pl.BlockSpec((rows, B_WIDTH), lambda n: (rev(n), 1)),
            _full((1, B_WIDTH)), _full((1, B_WIDTH)), _full((1, B_WIDTH)),
            _full(((len(B_LEVELS) + 1) * CHUNK, CHUNK)), _full((B_WIDTH, B_WIDTH)), _ANY,
        ],
        out_specs=[
            pl.BlockSpec((rows, 4 * B_WIDTH), lambda n: (rev(n), 1)),
            _full((1, B_WIDTH)), _full((1, B_WIDTH)), _full((1, B_WIDTH)),
        ],
        out_shape=[jax.ShapeDtypeStruct(dproj.shape, BF16)] + [jax.ShapeDtypeStruct((1, B_WIDTH), F32)] * 3,
        input_output_aliases={8: 0},
        scratch_shapes=[pltpu.VMEM((B_WIDTH, B_WIDTH), F32)],
        compiler_params=_cparams("arbitrary"),
    )(proj, states, dy, lb0, lb1, onorm, tstack, ones_bd, dproj)


D_MODEL = 1024
D_INT = 4096


def _rms_stats(xf):
    r = lax.rsqrt(jnp.mean(xf * xf, axis=-1, keepdims=True) + NORM_EPS)
    return r, xf * r


def _rms_bwd(dy, g, r, xh):
    u = dy * g
    return r * (u - xh * jnp.mean(u * xh, axis=-1, keepdims=True))


def inproj(x, g, w, layer):
    seq = x.shape[0]
    tm = min(seq, 512)

    def body(x_ref, g_ref, w_ref, p_ref, h_ref):
        _, xh = _rms_stats(x_ref[...])
        h = (xh * g_ref[...]).astype(BF16)
        h_ref[...] = h
        p_ref[...] = _dot(h, w_ref[0])

    return pl.pallas_call(
        body,
        name="inproj",
        grid=(seq // tm,),
        in_specs=[
            pl.BlockSpec((tm, D_MODEL), lambda i: (i, 0)),
            _full((1, D_MODEL)),
            pl.BlockSpec((1, D_MODEL, D_INT), lambda i: (layer, 0, 0)),
        ],
        out_specs=[pl.BlockSpec((tm, D_INT), lambda i: (i, 0)), pl.BlockSpec((tm, D_MODEL), lambda i: (i, 0))],
        out_shape=[jax.ShapeDtypeStruct((seq, D_INT), F32), jax.ShapeDtypeStruct((seq, D_MODEL), BF16)],
        compiler_params=_cparams("parallel"),
    )(x, g, w)


def outproj(x, ya, yb, o, proj, wo, layer):
    seq = x.shape[0]
    tm = min(seq, 512)
    blk = wo.shape[2]

    def body(x_ref, ya_ref, yb_ref, o_ref, z_ref, w_ref, xn_ref, y_ref):
        yc = (o_ref[...] * jax.nn.silu(z_ref[...])).astype(BF16)
        y = jnp.concatenate([ya_ref[...], yb_ref[...], yc], axis=1)
        y_ref[...] = y
        w = jnp.concatenate([w_ref[d, 0] for d in range(N_DEV)], axis=0)
        xn_ref[...] = x_ref[...] + _dot(y, w)

    return pl.pallas_call(
        body,
        name="outproj",
        grid=(seq // tm,),
        in_specs=[
            pl.BlockSpec((tm, D_MODEL), lambda i: (i, 0)),
            pl.BlockSpec((tm, 256), lambda i: (i, 0)),
            pl.BlockSpec((tm, 256), lambda i: (i, 0)),
            pl.BlockSpec((tm, 512), lambda i: (i, 0)),
            pl.BlockSpec((tm, 512), lambda i: (i, 7)),
            pl.BlockSpec((N_DEV, 1, blk, D_MODEL), lambda i: (0, layer, 0, 0)),
        ],
        out_specs=[pl.BlockSpec((tm, D_MODEL), lambda i: (i, 0)), pl.BlockSpec((tm, D_MODEL), lambda i: (i, 0))],
        out_shape=[jax.ShapeDtypeStruct((seq, D_MODEL), F32), jax.ShapeDtypeStruct((seq, D_MODEL), BF16)],
        compiler_params=_cparams("parallel"),
    )(x, ya, yb, o, proj, wo)


def outproj_bwd(dx, y, wo, layer, stacked=None):
    seq = dx.shape[0]
    ts = min(seq, 512)
    _, depth, blk, _ = wo.shape

    def body(dx_ref, y_ref, w_ref, *refs):
        dy_ref, dw_ref = refs[-2:]

        @pl.when(pl.program_id(0) == 0)
        def _():
            dw_ref[...] = jnp.zeros_like(dw_ref)

        dxb = dx_ref[...].astype(BF16)
        w = jnp.concatenate([w_ref[d, 0] for d in range(N_DEV)], axis=0)
        dy_ref[...] = lax.dot_general(dxb, w, _NT, preferred_element_type=F32)
        dw = lax.dot_general(y_ref[...], dxb, _TN, preferred_element_type=F32)
        for d in range(N_DEV):
            dw_ref[d % 2, d // 2, 0] += dw[d * blk:(d + 1) * blk]

    carried = () if stacked is None else (stacked,)
    out_shape = [jax.ShapeDtypeStruct((seq, D_MODEL), F32), jax.ShapeDtypeStruct((2, N_CHIP, depth, blk, D_MODEL), F32)]
    return pl.pallas_call(
        body,
        name="outproj_bwd",
        grid=(seq // ts,),
        in_specs=[
            pl.BlockSpec((ts, D_MODEL), lambda i: (i, 0)),
            pl.BlockSpec((ts, D_MODEL), lambda i: (i, 0)),
            pl.BlockSpec((N_DEV, 1, blk, D_MODEL), lambda i: (0, layer, 0, 0)),
        ] + [_ANY] * len(carried),
        out_specs=[pl.BlockSpec((ts, D_MODEL), lambda i: (i, 0)),
                   pl.BlockSpec((2, N_CHIP, 1, blk, D_MODEL), lambda i: (0, 0, layer, 0, 0))],
        out_shape=out_shape,
        input_output_aliases={3: 1} if carried else {},
        compiler_params=_cparams("arbitrary"),
    )(dx, y, wo, *carried)


C_QKV = (2048, 3584)


def _dproj_parts(dp_ref, dqkv_refs, rows):
    lo, hi = C_QKV
    step = (hi - lo) // len(dqkv_refs)
    return ([(0, dp_ref.at[rows, 0:lo])] + [(lo + i * step, r.at[rows, :]) for i, r in enumerate(dqkv_refs)]
            + [(hi, dp_ref.at[rows, hi:D_INT])])


def inproj_bwd_x(dproj, dqkv, w, x, g, dx_in, layer, carried=None):
    seq = x.shape[0]
    tm = min(seq, 512)

    def body(dp_ref, dq_ref, dk_ref, dv_ref, w_ref, x_ref, g_ref, dxin_ref, dx_ref, dg_ref):
        @pl.when(pl.program_id(0) == 0)
        def _():
            dg_ref[...] = jnp.zeros_like(dg_ref)

        dh = None
        for at, part in _dproj_parts(dp_ref, (dq_ref, dk_ref, dv_ref), slice(None)):
            term = lax.dot_general(part[...], w_ref[0, :, at:at + part.shape[1]], _NT, preferred_element_type=F32)
            dh = term if dh is None else dh + term
        r, xh = _rms_stats(x_ref[...])
        dg_ref[...] += jnp.sum(dh * xh, axis=0, keepdims=True)
        dx_ref[...] = dxin_ref[...] + _rms_bwd(dh, g_ref[...], r, xh)

    third = lambda: pl.BlockSpec((tm, C_WIDTH), lambda i: (i, 0))
    return _call_carrying(
        carried, body, (dproj, *dqkv, w, x, g, dx_in),
        name="inproj_bwd_x",
        grid=(seq // tm,),
        in_specs=[
            pl.BlockSpec((tm, D_INT), lambda i: (i, 0)), third(), third(), third(),
            pl.BlockSpec((1, D_MODEL, D_INT), lambda i: (layer, 0, 0)),
            pl.BlockSpec((tm, D_MODEL), lambda i: (i, 0)),
            _full((1, D_MODEL)),
            pl.BlockSpec((tm, D_MODEL), lambda i: (i, 0)),
        ],
        out_specs=[pl.BlockSpec((tm, D_MODEL), lambda i: (i, 0)), _full((1, D_MODEL))],
        out_shape=[jax.ShapeDtypeStruct((seq, D_MODEL), F32), jax.ShapeDtypeStruct((1, D_MODEL), F32)],
        scratch_shapes=[], semantics=("arbitrary",),
    )


def inproj_bwd_w(h, dproj, dqkv):
    seq = h.shape[0]
    ts, tn = min(seq, 512), 512

    def body(h_ref, dp_ref, dq_ref, dk_ref, dv_ref, dw_ref):
        @pl.when(pl.program_id(0) == 0)
        def _():
            dw_ref[...] = jnp.zeros_like(dw_ref)

        ht = h_ref[...].T
        for at, part in _dproj_parts(dp_ref, (dq_ref, dk_ref, dv_ref), slice(None)):
            for c in range(0, part.shape[1], tn):
                dw_ref[0, :, at + c:at + c + tn] += _dot(ht, part[:, c:c + tn])

    third = lambda: pl.BlockSpec((ts, C_WIDTH), lambda s: (s, 0))
    return pl.pallas_call(
        body,
        name="inproj_bwd_w",
        grid=(seq // ts,),
        in_specs=[pl.BlockSpec((ts, D_MODEL), lambda s: (s, 0)), pl.BlockSpec((ts, D_INT), lambda s: (s, 0)),
                  third(), third(), third()],
        out_specs=_full((1, D_MODEL, D_INT)),
        out_shape=jax.ShapeDtypeStruct((1, D_MODEL, D_INT), F32),
        compiler_params=_cparams("arbitrary"),
    )(h, dproj, *dqkv)


N_IN = 3848


def _internal_of(col):
    return col if col < 768 else (col + 256 if col < 3840 else 768 + col - 3840)


def _column_runs(n_shard):
    runs = []
    for d in range(N_IN // n_shard):
        mine = []
        for j in range(n_shard):
            ci = _internal_of(d * n_shard + j)
            if mine and mine[-1][0] + mine[-1][1] == ci:
                mine[-1][1] += 1
            else:
                mine.append([ci, 1, j])
        runs.append(mine)
    return runs


def assemble_w_in(wi_all):
    n_dev, depth, _, n_shard = wi_all.shape
    tr = 256
    pieces = [[] for _ in range(D_INT // LANES)]
    for d, mine in enumerate(_column_runs(n_shard)):
        for ci, ln, off in mine:
            while ln > 0:
                blk, at = divmod(ci, LANES)
                take = min(ln, LANES - at)
                pieces[blk].append((at, take, d, off))
                ci, ln, off = ci + take, ln - take, off + take

    def body(x_ref, o_ref):
        for blk, parts in enumerate(pieces):
            vals, at = [], 0
            for start, ln, d, off in sorted(parts):
                if start > at:
                    vals.append(jnp.zeros((tr, start - at), BF16))
                vals.append(x_ref[d, 0, :, off:off + ln])
                at = start + ln
            if at < LANES:
                vals.append(jnp.zeros((tr, LANES - at), BF16))
            o_ref[0, :, blk * LANES:(blk + 1) * LANES] = vals[0] if len(vals) == 1 else jnp.concatenate(vals, axis=1)

    return pl.pallas_call(
        body,
        name="assemble_w_in",
        grid=(depth, D_MODEL // tr),
        in_specs=[pl.BlockSpec((n_dev, 1, tr, n_shard), lambda l, r: (0, l, r, 0))],
        out_specs=pl.BlockSpec((1, tr, D_INT), lambda l, r: (l, r, 0)),
        out_shape=jax.ShapeDtypeStruct((depth, D_MODEL, D_INT), BF16),
        compiler_params=_cparams("parallel", "parallel"),
    )(wi_all)


def split_w_in_grad(dwi, n_shard):
    depth = dwi.shape[0]
    tr = 256
    runs = _column_runs(n_shard)

    def body(x_ref, o_ref):
        for d, mine in enumerate(runs):
            for ci, ln, off in mine:
                o_ref[d % 2, d // 2, 0, :, off:off + ln] = x_ref[0, :, ci:ci + ln]

    return pl.pallas_call(
        body,
        name="split_w_in_grad",
        grid=(depth, D_MODEL // tr),
        in_specs=[pl.BlockSpec((1, tr, D_INT), lambda l, r: (l, r, 0))],
        out_specs=pl.BlockSpec((2, N_CHIP, 1, tr, n_shard), lambda l, r: (0, 0, l, r, 0)),
        out_shape=jax.ShapeDtypeStruct((2, N_CHIP, depth, D_MODEL, n_shard), F32),
        compiler_params=_cparams("parallel", "parallel"),
    )(dwi)


def final_loss(x, g, tgt):
    seq = x.shape[0]
    tm = min(seq, 512)

    def body(x_ref, g_ref, t_ref, dx_ref, dg_ref, loss_ref):
        @pl.when(pl.program_id(0) == 0)
        def _():
            dg_ref[...] = jnp.zeros_like(dg_ref)
            loss_ref[...] = jnp.zeros_like(loss_ref)

        g = g_ref[...]
        r, xh = _rms_stats(x_ref[...])
        err = xh * g - t_ref[...]
        sq = jnp.sum(jnp.sum(err * err, axis=1, keepdims=True), axis=0, keepdims=True)
        loss_ref[...] += jnp.broadcast_to(sq * (0.5 / D_MODEL), loss_ref.shape)
        dout = err * (1.0 / D_MODEL)
        dg_ref[...] += jnp.sum(dout * xh, axis=0, keepdims=True)
        dx_ref[...] = _rms_bwd(dout, g, r, xh)

    return pl.pallas_call(
        body,
        name="final_loss",
        grid=(seq // tm,),
        in_specs=[pl.BlockSpec((tm, D_MODEL), lambda i: (i, 0)), _full((1, D_MODEL)), pl.BlockSpec((tm, D_MODEL), lambda i: (i, 0))],
        out_specs=[pl.BlockSpec((tm, D_MODEL), lambda i: (i, 0)), _full((1, D_MODEL)), _full((8, LANES))],
        out_shape=[jax.ShapeDtypeStruct((seq, D_MODEL), F32), jax.ShapeDtypeStruct((1, D_MODEL), F32), jax.ShapeDtypeStruct((8, LANES), F32)],
        compiler_params=_cparams("arbitrary"),
    )(x, g, tgt)


C_WIDTH = 512
C_HEADS = 8
C_HDIM = 64
C_PAIRS = C_HEADS // 2
C_BQ = 512
C_TAIL = 16
C_KG = 4


def _split3(x):
    hi = x.astype(BF16)
    r = x - hi.astype(F32)
    mid = r.astype(BF16)
    return hi, mid, (r - mid.astype(F32)).astype(BF16)


def _piece_selectors():
    sel = np.zeros((C_HEADS, 3 * LANES, LANES), np.float32)
    for p in range(C_PAIRS):
        for e in range(2):
            for t in range(3):
                sel[2 * p + e, t * LANES + 2 * p + e, 3 * e + t] = -1.0
    return sel


def fox_prep(proj, bf_row):
    seq = proj.shape[0]
    nblk = seq // CHUNK
    tril = jnp.asarray(np.tril(np.ones((CHUNK, CHUNK), np.float32)), BF16)
    sel = jnp.asarray(_piece_selectors(), BF16)
    rows_t = CHUNK + C_TAIL

    def body(fl_ref, q_ref, k_ref, v_ref, bf_ref, l_ref, sel_ref, ka_ref, va_ref, vt_ref, kt_ref, qt_ref, qa_ref, carry_ref):
        @pl.when(pl.program_id(0) == 0)
        def _():
            carry_ref[...] = jnp.zeros_like(carry_ref)

        lf = jax.nn.log_sigmoid(fl_ref[:, :LANES] + bf_ref[...])
        c = _exact_times(l_ref[...], lf, 3) + carry_ref[...]
        carry_ref[...] += jnp.sum(lf, axis=0, keepdims=True)
        c3 = jnp.concatenate(_split3(c), axis=1)
        lane = lax.broadcasted_iota(jnp.int32, (CHUNK, LANES), 1)
        row = lax.broadcasted_iota(jnp.int32, (CHUNK, LANES), 0)
        r16 = lax.broadcasted_iota(jnp.int32, (C_TAIL, 2 * CHUNK), 0)
        l16 = lax.broadcasted_iota(jnp.int32, (C_TAIL, 2 * CHUNK), 1)
        zero = jnp.zeros((CHUNK, LANES), BF16)
        one = jnp.ones((CHUNK, LANES), BF16)

        def by_keys(x, right_a, right_b):
            xb = x.astype(BF16)
            top = jnp.concatenate([jnp.where(lane < C_HDIM, xb, zero), right_a], axis=1)
            return jnp.concatenate([top, jnp.concatenate([jnp.where(lane < C_HDIM, zero, xb), right_b], axis=1)], axis=0)

        def by_lanes(x, tail):
            xt = x.T.astype(BF16)
            main = jnp.concatenate([jnp.where(row < C_HDIM, xt, zero), jnp.where(row < C_HDIM, zero, xt)], axis=1)
            return jnp.concatenate([main, tail], axis=0)

        for p in range(C_PAIRS):
            cols = slice(p * LANES, (p + 1) * LANES)
            q2, k2, v2 = q_ref[:, cols] * (C_HDIM ** -0.5), k_ref[:, cols], v_ref[:, cols]
            negc = [_dot(c3, sel_ref[2 * p + e]).astype(BF16) for e in range(2)]
            ones3 = [jnp.where((lane >= 3 * e) & (lane < 3 * e + 3), one, zero) for e in range(2)]
            tail = jnp.where(((r16 == 2 * p) & (l16 < CHUNK)) | ((r16 == 2 * p + 1) & (l16 >= CHUNK)), 1.0, 0.0).astype(BF16)
            ka_ref[p] = by_keys(k2, negc[0], negc[1])
            va_ref[p] = by_keys(v2, ones3[0], ones3[1])
            kt_ref[p] = by_lanes(k2, tail)
            vt_ref[p] = by_lanes(v2, tail)
            qt_ref[p] = jnp.concatenate([q2.T.astype(BF16), jnp.where(row < 6, one, zero)], axis=0)
            qa_ref[p] = jnp.concatenate([q2.astype(BF16), jnp.where((lane == 2 * p) | (lane == 2 * p + 1), one, zero)], axis=1)

    wide = lambda j: pl.BlockSpec((CHUNK, C_WIDTH), lambda n: (n, j))
    by_rows = pl.BlockSpec((C_PAIRS, 2 * CHUNK, 2 * CHUNK), lambda n: (0, n, 0))
    by_cols = pl.BlockSpec((C_PAIRS, rows_t, 2 * CHUNK), lambda n: (0, 0, n))
    return pl.pallas_call(
        body,
        name="fox_prep",
        grid=(nblk,),
        in_specs=[pl.BlockSpec((CHUNK, 256), lambda n: (n, 3)), wide(4), wide(5), wide(6), _full((1, LANES)),
                  _full((CHUNK, CHUNK)), _full((C_HEADS, 3 * LANES, LANES))],
        out_specs=[by_rows, by_rows, by_cols, by_cols,
                   pl.BlockSpec((C_PAIRS, 2 * CHUNK, CHUNK), lambda n: (0, 0, n)),
                   pl.BlockSpec((C_PAIRS, CHUNK, 2 * CHUNK), lambda n: (0, n, 0))],
        out_shape=[jax.ShapeDtypeStruct((C_PAIRS, 2 * seq, 2 * CHUNK), BF16)] * 2
        + [jax.ShapeDtypeStruct((C_PAIRS, rows_t, 2 * seq), BF16)] * 2
        + [jax.ShapeDtypeStruct((C_PAIRS, 2 * CHUNK, seq), BF16), jax.ShapeDtypeStruct((C_PAIRS, seq, 2 * CHUNK), BF16)],
        scratch_shapes=[pltpu.VMEM((1, LANES), F32)],
        compiler_params=_cparams("arbitrary"),
    )(proj, proj, proj, proj, bf_row, tril, sel)


def _visible(shape, key0, query0):
    row = lax.broadcasted_iota(jnp.int32, shape, 0)
    key = key0 + lax.shift_left(lax.shift_right_logical(row, 8), 7) + (row & (CHUNK - 1))
    return key <= query0 + lax.broadcasted_iota(jnp.int32, shape, 1)


def _rows_ab(a, b, n):
    return jnp.concatenate([jnp.broadcast_to(a, (C_HDIM, n)), jnp.broadcast_to(b, (C_HDIM, n))], axis=0)


def _call_carrying(ex, body, operands, *, name, grid, in_specs, out_specs, out_shape, scratch_shapes, semantics=None):
    if ex is None:
        semantics = semantics or ("parallel", *["arbitrary"] * (len(grid) - 1))
        return pl.pallas_call(body, name=name, grid=grid, in_specs=in_specs, out_specs=out_specs, out_shape=out_shape,
                              scratch_shapes=scratch_shapes, compiler_params=_cparams(*semantics))(*operands)
    n_in, n_out = len(in_specs), len(out_specs)

    def wrapped(*refs):
        own, parts = _carried_refs(refs, n_in, n_out, ex)
        ids = [pl.program_id(a) for a in range(len(grid))]
        pl.when(functools.reduce(jnp.logical_and, [i == 0 for i in ids]))(lambda: ex.start(*parts))
        body(*own)
        pl.when(functools.reduce(jnp.logical_and, [i == g - 1 for i, g in zip(ids, grid)]))(lambda: ex.finish(*parts))

    return pl.pallas_call(
        wrapped, name=name, grid=grid,
        in_specs=list(in_specs) + [_ANY] * len(ex.inputs), out_specs=list(out_specs) + [_ANY] * len(ex.out_shape),
        out_shape=list(out_shape) + list(ex.out_shape), scratch_shapes=list(scratch_shapes) + list(ex.scratch),
        input_output_aliases={n_in + i: n_out + o for i, o in getattr(ex, "aliases", {}).items()},
        compiler_params=_cparams(*["arbitrary"] * len(grid)),
    )(*operands, *ex.inputs)


def fox_fwd(qt, ka, vt, carried=None):
    seq = qt.shape[2]
    nblk = seq // CHUNK
    bq = min(C_BQ, seq)
    grp = bq // CHUNK
    rows_t = CHUNK + C_TAIL

    def body(qt_ref, ka_ref, vt_ref, o_ref, lse_ref, acc_ref, s_ref):
        p, i = pl.program_id(0), pl.program_id(1)
        qtile = qt_ref[0]
        r16 = lax.broadcasted_iota(jnp.int32, (C_TAIL, bq), 0)

        def scores(t):
            at = pl.multiple_of(t * grp * 2 * CHUNK, 2 * CHUNK)
            return _dot(ka_ref[0, pl.ds(at, grp * 2 * CHUNK), :], qtile)

        def group(t, m, masked):
            ma, mb = m
            at = pl.multiple_of(t * grp * 2 * CHUNK, 2 * CHUNK)
            s = s_ref[...]
            if masked:
                s = jnp.where(_visible(s.shape, t * bq, i * bq), s, -jnp.inf)
            sa = [s[g * 2 * CHUNK:g * 2 * CHUNK + CHUNK] for g in range(grp)]
            sb = [s[g * 2 * CHUNK + CHUNK:(g + 1) * 2 * CHUNK] for g in range(grp)]
            na, nb = ma, mb
            for g in range(grp):
                na = jnp.maximum(na, jnp.max(sa[g], axis=0, keepdims=True))
                nb = jnp.maximum(nb, jnp.max(sb[g], axis=0, keepdims=True))
            al_a, al_b = jnp.exp(ma - na), jnp.exp(mb - nb)
            pt = jnp.concatenate([jnp.exp(x - n) for g in range(grp) for x, n in ((sa[g], na), (sb[g], nb))], axis=0)
            pv = _dot(vt_ref[0, :, pl.ds(at, grp * 2 * CHUNK)], pt.astype(BF16))
            tail = jnp.where(r16 == 2 * p, al_a, jnp.where(r16 == 2 * p + 1, al_b, 1.0))
            acc_ref[...] = acc_ref[...] * jnp.concatenate([_rows_ab(al_a, al_b, bq), tail], axis=0) + pv
            return na, nb

        def step(t, m):
            s_next = scores(t + 1)
            m = group(t, m, False)
            s_ref[...] = s_next
            return m

        acc_ref[...] = jnp.zeros_like(acc_ref)
        s_ref[...] = scores(0)
        m = (jnp.full((1, bq), -jnp.inf, F32), jnp.full((1, bq), -jnp.inf, F32))
        m = lax.fori_loop(0, i, step, m)
        ma, mb = group(i, m, True)
        tailv = acc_ref[CHUNK:rows_t, :]
        la = jnp.sum(jnp.where(r16 == 2 * p, tailv, 0.0), axis=0, keepdims=True)
        lb = jnp.sum(jnp.where(r16 == 2 * p + 1, tailv, 0.0), axis=0, keepdims=True)
        o_ref[...] = (acc_ref[0:CHUNK, :] * _rows_ab(1.0 / la, 1.0 / lb, bq)).T
        lse_ref[0, 0:1, :] = ma + jnp.log(la)
        lse_ref[0, 1:2, :] = mb + jnp.log(lb)

    return _call_carrying(
        carried, body, (qt, ka, vt),
        name="fox_fwd",
        grid=(C_PAIRS, seq // bq),
        in_specs=[
            pl.BlockSpec((1, 2 * CHUNK, bq), lambda p, i: (p, 0, i)),
            pl.BlockSpec((1, 2 * seq, 2 * CHUNK), lambda p, i: (p, 0, 0)),
            pl.BlockSpec((1, rows_t, 2 * seq), lambda p, i: (p, 0, 0)),
        ],
        out_specs=[pl.BlockSpec((bq, LANES), lambda p, i: (i, p)), pl.BlockSpec((1, 2, bq), lambda p, i: (p, 0, i))],
        out_shape=[jax.ShapeDtypeStruct((seq, C_WIDTH), F32), jax.ShapeDtypeStruct((C_PAIRS, 2, seq), F32)],
        scratch_shapes=[pltpu.VMEM((rows_t, bq), F32), pltpu.VMEM((grp * 2 * CHUNK, bq), F32)],
    )


def fox_bwd_prep(dy, o, proj, dproj):
    seq = o.shape[0]
    ind = np.zeros((C_WIDTH, LANES), np.float32)
    for h in range(C_HEADS):
        ind[h * C_HDIM:(h + 1) * C_HDIM, h] = 1.0
    ind = jnp.asarray(ind, BF16)
    sel = _piece_selectors()
    sel = jnp.asarray(np.stack([sel[2 * p].T + sel[2 * p + 1].T for p in range(C_PAIRS)]), BF16)

    def body(dy_ref, o_ref, z_ref, ind_ref, sel_ref, _, do_ref, dz_ref, dot_ref):
        dy_c, o_v, z = dy_ref[...], o_ref[...], z_ref[...]
        sg = jax.nn.sigmoid(z)
        do = dy_c * (z * sg)
        do_ref[...] = do.astype(BF16)
        dz_ref[...] = (dy_c * o_v * (sg * (1.0 + z * (1.0 - sg)))).astype(BF16)
        prod = do * o_v
        hi = prod.astype(BF16)
        lo = (prod - hi.astype(F32)).astype(BF16)
        delta = _dot(hi, ind_ref[...]) + _dot(lo, ind_ref[...])
        d3 = jnp.concatenate(_split3(delta.T), axis=0)
        for p in range(C_PAIRS):
            tail = _dot(sel_ref[p], d3).astype(BF16)
            dot_ref[p] = jnp.concatenate([do[:, p * LANES:(p + 1) * LANES].T.astype(BF16), tail], axis=0)

    return pl.pallas_call(
        body,
        name="fox_bwd_prep",
        grid=(seq // CHUNK,),
        in_specs=[
            pl.BlockSpec((CHUNK, C_WIDTH), lambda i: (i, 1)),
            pl.BlockSpec((CHUNK, C_WIDTH), lambda i: (i, 0)),
            pl.BlockSpec((CHUNK, C_WIDTH), lambda i: (i, 7)),
            _full((C_WIDTH, LANES)), _full((C_PAIRS, LANES, 3 * LANES)), _ANY,
        ],
        out_specs=[
            pl.BlockSpec((CHUNK, C_WIDTH), lambda i: (i, 0)),
            pl.BlockSpec((CHUNK, C_WIDTH), lambda i: (i, 7)),
            pl.BlockSpec((C_PAIRS, 2 * CHUNK, CHUNK), lambda i: (0, 0, i)),
        ],
        out_shape=[jax.ShapeDtypeStruct((seq, C_WIDTH), BF16), jax.ShapeDtypeStruct(dproj.shape, BF16),
                   jax.ShapeDtypeStruct((C_PAIRS, 2 * CHUNK, seq), BF16)],
        input_output_aliases={5: 1},
        compiler_params=_cparams("parallel"),
    )(dy, o, proj, ind, sel, dproj)


def fox_bwd(ka, va, kt, qt, dot_t, qa, dob, lse, carried=None):
    seq = qt.shape[2]
    nblk = seq // CHUNK
    bq = min(C_BQ, seq)
    nq = seq // bq
    kg = min(C_KG, nblk)
    ng = nblk // kg
    rows_t = CHUNK + C_TAIL

    def body(ka_ref, va_ref, kt_ref, qt_ref, dot_ref, qa_ref, do_ref, lse_ref,
             dq_ref, dk_ref, dv_ref, dck_ref, dcq_ref, dqt_acc, dv_acc, dka_acc):
        p, jg = pl.program_id(0), pl.program_id(1)

        @pl.when(jg == 0)
        def _():
            dqt_acc[...] = jnp.zeros_like(dqt_acc)

        dv_acc[...] = jnp.zeros_like(dv_acc)
        dka_acc[...] = jnp.zeros_like(dka_acc)

        def step(i, carry, masked):
            cols = pl.ds(pl.multiple_of(i * bq, bq), bq)
            qtile, dotile = qt_ref[0, :, cols], dot_ref[0, :, cols]
            do, qa_i = do_ref[cols, :], qa_ref[0, cols, :]
            lse2 = jnp.concatenate([jnp.broadcast_to(lse_ref[0, 0:1, cols], (CHUNK, bq)),
                                    jnp.broadcast_to(lse_ref[0, 1:2, cols], (CHUNK, bq))] * kg, axis=0)
            pt = jnp.exp(_dot(ka_ref[0], qtile) - lse2)
            if masked:
                pt = jnp.where(_visible(pt.shape, jg * kg * CHUNK, i * bq), pt, 0.0)
            ds = pt * _dot(va_ref[0], dotile)
            ptb, dsb = pt.astype(BF16), ds.astype(BF16)
            dv_acc[...] += _dot(ptb, do)
            dka_acc[...] += _dot(dsb, qa_i)
            dqt_acc[:, cols] += _dot(kt_ref[0], dsb)
            return carry

        i0 = (jg * kg * CHUNK) // bq
        step(i0, 0, True)
        lax.fori_loop(i0 + 1, nq, functools.partial(step, masked=False), 0)
        lane = lax.broadcasted_iota(jnp.int32, (CHUNK, LANES), 1)
        for kb in range(kg):
            rows = slice(kb * CHUNK, (kb + 1) * CHUNK)
            ra = slice(kb * 2 * CHUNK, kb * 2 * CHUNK + CHUNK)
            rb = slice(kb * 2 * CHUNK + CHUNK, (kb + 1) * 2 * CHUNK)
            dk_ref[rows, :] = jnp.where(lane < C_HDIM, dka_acc[ra, 0:LANES], dka_acc[rb, 0:LANES]).astype(BF16)
            dv_ref[rows, :] = jnp.where(lane < C_HDIM, dv_acc[ra, :], dv_acc[rb, :]).astype(BF16)
            dck_ref[0, rows, :] = (jnp.where(lane == 2 * p, dka_acc[ra, LANES:], 0.0)
                                   + jnp.where(lane == 2 * p + 1, dka_acc[rb, LANES:], 0.0))

        @pl.when(jg == ng - 1)
        def _():
            for c in range(nq):
                dq_ref[c * bq:(c + 1) * bq, :] = (dqt_acc[0:CHUNK, c * bq:(c + 1) * bq].T * (C_HDIM ** -0.5)).astype(BF16)
            dcq_ref[0] = dqt_acc[CHUNK:rows_t, :]

    per_pair = lambda r, c: pl.BlockSpec((1, r, c), lambda p, j: (p, 0, 0))
    by_rows = pl.BlockSpec((1, kg * 2 * CHUNK, 2 * CHUNK), lambda p, j: (p, j, 0))
    by_cols = pl.BlockSpec((1, rows_t, kg * 2 * CHUNK), lambda p, j: (p, 0, j))
    return _call_carrying(
        carried, body, (ka, va, kt, qt, dot_t, qa, dob, lse),
        name="fox_bwd",
        grid=(C_PAIRS, ng),
        in_specs=[by_rows, by_rows, by_cols, per_pair(2 * CHUNK, seq), per_pair(2 * CHUNK, seq),
                  per_pair(seq, 2 * CHUNK), pl.BlockSpec((seq, LANES), lambda p, j: (0, p)), per_pair(2, seq)],
        out_specs=[pl.BlockSpec((seq, LANES), lambda p, j: (0, p)),
                   pl.BlockSpec((kg * CHUNK, LANES), lambda p, j: (j, p)),
                   pl.BlockSpec((kg * CHUNK, LANES), lambda p, j: (j, p)),
                   pl.BlockSpec((1, kg * CHUNK, LANES), lambda p, j: (p, j, 0)),
                   per_pair(C_TAIL, seq)],
        out_shape=[jax.ShapeDtypeStruct((seq, C_WIDTH), BF16)] * 3
        + [jax.ShapeDtypeStruct((C_PAIRS, seq, LANES), F32), jax.ShapeDtypeStruct((C_PAIRS, C_TAIL, seq), F32)],
        scratch_shapes=[pltpu.VMEM((rows_t, seq), F32), pltpu.VMEM((kg * 2 * CHUNK, LANES), F32),
                        pltpu.VMEM((kg * 2 * CHUNK, 2 * CHUNK), F32)],
    )


def fox_post(dcq, dck, proj, bf_row, dproj):
    seq = proj.shape[0]
    nc = seq // CHUNK
    triu = jnp.asarray(np.triu(np.ones((CHUNK, CHUNK), np.float32)), BF16)

    def body(dq_ref, dk_ref, fl_ref, bf_ref, u_ref, _, dfl_ref, dbf_ref, carry_ref):
        @pl.when(pl.program_id(0) == 0)
        def _():
            carry_ref[...] = jnp.zeros_like(carry_ref)
            dbf_ref[...] = jnp.zeros_like(dbf_ref)

        rows = (dq_ref[0] + dq_ref[1]) + (dq_ref[2] + dq_ref[3])
        dc = jnp.concatenate([rows, jnp.zeros((CHUNK - C_TAIL, CHUNK), F32)], axis=0).T
        dc = dc - ((dk_ref[0] + dk_ref[1]) + (dk_ref[2] + dk_ref[3]))
        g = _exact_times(u_ref[...], dc, 3) + carry_ref[...]
        carry_ref[...] += jnp.sum(dc, axis=0, keepdims=True)
        dfl = g * jax.nn.sigmoid(-(fl_ref[:, :LANES] + bf_ref[...]))
        dbf_ref[...] += jnp.sum(dfl, axis=0, keepdims=True)
        dfl_ref[...] = jnp.concatenate([dfl, jnp.zeros_like(dfl)], axis=1).astype(BF16)

    rev = lambda n: nc - 1 - n
    return pl.pallas_call(
        body,
        name="fox_post",
        grid=(nc,),
        in_specs=[
            pl.BlockSpec((C_PAIRS, C_TAIL, CHUNK), lambda n: (0, 0, rev(n))),
            pl.BlockSpec((C_PAIRS, CHUNK, LANES), lambda n: (0, rev(n), 0)),
            pl.BlockSpec((CHUNK, 256), lambda n: (rev(n), 3)),
            _full((1, LANES)), _full((CHUNK, CHUNK)), _ANY,
        ],
        out_specs=[pl.BlockSpec((CHUNK, 256), lambda n: (rev(n), 3)), _full((1, LANES))],
        out_shape=[jax.ShapeDtypeStruct(dproj.shape, BF16), jax.ShapeDtypeStruct((1, LANES), F32)],
        input_output_aliases={5: 0},
        scratch_shapes=[pltpu.VMEM((1, LANES), F32)],
        compiler_params=_cparams("arbitrary"),
    )(dcq, dck, proj, bf_row, triu, dproj)


N_DEV = 8
MESH = pl.DeviceIdType.MESH
_ANY = pl.BlockSpec(memory_space=pl.ANY)


def _mesh_pos():
    return lax.axis_index("x"), lax.axis_index("y"), lax.axis_index("c")


def _dev_index(px, py, pc):
    return 4 * px + 2 * py + pc


def _row_pieces(ref, rows):
    return [ref.at[idx + (pl.ds(r, rows),)] for idx in np.ndindex(*ref.shape[:-2]) for r in range(0, ref.shape[-2], rows)]


class _Transfer:
    def __init__(self, src, dst, rows, send_sem, recv_sem, to):
        self.src, self.dst, self.rows, self.sems, self.to = src, dst, rows, (send_sem, recv_sem), to

    def _copy(self, src, dst):
        return pltpu.make_async_remote_copy(src_ref=src, dst_ref=dst, send_sem=self.sems[0], recv_sem=self.sems[1],
                                            device_id=self.to, device_id_type=MESH)

    def start(self):
        for s, d in zip(_row_pieces(self.src, self.rows), _row_pieces(self.dst, self.rows), strict=True):
            self._copy(s, d).start()

    def wait_send(self):
        self._copy(self.src, self.dst).wait_send()

    def wait_recv(self):
        self._copy(self.src, self.dst).wait_recv()


def _exchange_call(ex, name):
    n_in, n_out = len(ex.inputs), len(ex.out_shape)

    def body(*refs):
        parts = refs[:n_in], refs[n_in:n_in + n_out], refs[n_in + n_out:]
        ex.start(*parts)
        ex.finish(*parts)

    return pl.pallas_call(body, name=name, in_specs=[_ANY] * n_in, out_specs=[_ANY] * n_out, out_shape=ex.out_shape,
                          scratch_shapes=ex.scratch, input_output_aliases=getattr(ex, "aliases", {}))(*ex.inputs)


def _carried_refs(refs, n_in, n_out, ex):
    k_in, k_out, k_sem = (len(ex.inputs), len(ex.out_shape), len(ex.scratch)) if ex else (0, 0, 0)
    a, b, c = n_in + k_in, n_in + k_in + n_out, n_in + k_in + n_out + k_out
    own = refs[:n_in] + refs[a:b] + refs[c:len(refs) - k_sem]
    return own, (refs[n_in:a], refs[b:c], refs[len(refs) - k_sem:])


class AllGatherWeights:
    piece_rows = (128, 64)

    def __init__(self, wi, wo):
        self.inputs = (wi, wo)
        self.out_shape = [jax.ShapeDtypeStruct((N_DEV,) + wi.shape, wi.dtype), jax.ShapeDtypeStruct((N_DEV,) + wo.shape, wo.dtype)]
        self.scratch = [pltpu.SemaphoreType.DMA((2, 7)), pltpu.SemaphoreType.DMA((2, 7)), pltpu.SemaphoreType.DMA((2,))]

    def _plan(self, ins, outs, sems):
        send_sems, recv_sems, local_sems = sems
        x, y, c = _mesh_pos()
        me, sibling = (x, y, c), (x, y, 1 - c)
        chips = [(1 - x, y), (x, 1 - y), (1 - x, 1 - y)]
        both = range(2)

        def copy(a, k, block, to, own=False):
            slot = outs[a].at[_dev_index(*block)]
            return _Transfer(ins[a] if own else slot, slot, self.piece_rows[a], send_sems.at[a, k], recv_sems.at[a, k], to)

        mine = [pltpu.make_async_copy(ins[a], outs[a].at[_dev_index(*me)], local_sems.at[a]) for a in both]
        first = [copy(a, 1 + j, me, (*chip, c), own=True) for j, chip in enumerate(chips) for a in both]
        first += [copy(a, 0, me, sibling, own=True) for a in both]
        passed = [copy(a, 4 + j, (*chip, c), sibling) for j, chip in enumerate(chips) for a in both]
        return me, sibling, chips, c, copy, mine, first, passed

    def start(self, ins, outs, sems):
        *_, mine, first, _ = self._plan(ins, outs, sems)
        for cp in mine + first:
            cp.start()

    def finish(self, ins, outs, sems):
        me, sibling, chips, c, copy, mine, first, passed = self._plan(ins, outs, sems)
        for j, chip in enumerate(chips):
            for a in range(2):
                copy(a, 1 + j, (*chip, c), me).wait_recv()
            for a in range(2):
                passed[2 * j + a].start()
        for a in range(2):
            copy(a, 0, sibling, me).wait_recv()
        for j, chip in enumerate(chips):
            for a in range(2):
                copy(a, 4 + j, (*chip, 1 - c), me).wait_recv()
        for cp in first + passed:
            cp.wait_send()
        for cp in mine:
            cp.wait()


N_CHIP = 4


class PairExchange:
    def __init__(self, by_core, whole=()):
        self.inputs = tuple(by_core) + tuple(whole)
        self.n_by_core = len(by_core)
        self.out_shape = ([jax.ShapeDtypeStruct(a.shape[1:], a.dtype) for a in by_core]
                          + [jax.ShapeDtypeStruct(a.shape, a.dtype) for a in whole])
        n = len(self.inputs)
        self.scratch = [pltpu.SemaphoreType.DMA((n,)), pltpu.SemaphoreType.DMA((n,))]

    def _copies(self, ins, outs, sems):
        x, y, c = _mesh_pos()
        srcs = [r.at[1 - c] if a < self.n_by_core else r for a, r in enumerate(ins)]
        return [_Transfer(srcs[a], outs[a], outs[a].shape[-2], sems[0].at[a], sems[1].at[a], (x, y, 1 - c))
                for a in range(len(ins))]

    def start(self, ins, outs, sems):
        for cp in self._copies(ins, outs, sems):
            cp.start()

    def finish(self, ins, outs, sems):
        copies = self._copies(ins, outs, sems)
        for cp in copies:
            cp.wait_recv()
        for cp in copies:
            cp.wait_send()


def pair_sum(own, other, dtype, rows, name, core, layer, depth, stacked=None):
    n, n_r, n_c = other.shape

    def body(core_ref, a_ref, b_ref, *refs):
        refs[-1][0, 0] = (a_ref[0, 0] + b_ref[0]).astype(dtype)

    carried = () if stacked is None else (stacked,)
    grid_spec = pltpu.PrefetchScalarGridSpec(
        num_scalar_prefetch=1,
        grid=(n, n_r // rows),
        in_specs=[pl.BlockSpec((1, 1, rows, n_c), lambda i, r, s: (s[0], i, r, 0)),
                  pl.BlockSpec((1, rows, n_c), lambda i, r, s: (i, r, 0))] + [_ANY] * len(carried),
        out_specs=pl.BlockSpec((1, 1, rows, n_c), lambda i, r, s: (i, layer, r, 0)),
    )
    return pl.pallas_call(
        body,
        name=name,
        grid_spec=grid_spec,
        out_shape=jax.ShapeDtypeStruct((n, depth, n_r, n_c), dtype),
        input_output_aliases={3: 0} if carried else {},
        compiler_params=_cparams("parallel", "parallel"),
    )(core, own, other, *carried)


def small_sum(a, b, name):
    def body(a_ref, b_ref, o_ref):
        o_ref[...] = a_ref[...] + b_ref[...]

    return pl.pallas_call(body, name=name, out_shape=jax.ShapeDtypeStruct(a.shape, a.dtype))(a, b)


class ChipExchange:
    def __init__(self, by_chip=(), layers=(), gathered=(), stacked=()):
        stacked = tuple(stacked) or (None,) * len(by_chip)
        kept = [s for s in stacked if s is not None]
        self.inputs = tuple(by_chip) + tuple(gathered) + tuple(kept)
        self.n_by_chip, self.n_gathered = len(by_chip), len(gathered)
        self.items = [(a, l) for a in range(len(by_chip)) for l in layers[a]] + [(self.n_by_chip + g, None) for g in range(len(gathered))]
        self.out_shape = ([jax.ShapeDtypeStruct((N_CHIP - 1,) + a.shape[1:], a.dtype) for a in by_chip]
                          + [jax.ShapeDtypeStruct((N_CHIP,) + a.shape, a.dtype) for a in gathered])
        at = iter(range(self.n_by_chip + self.n_gathered, len(self.inputs)))
        self.aliases = {next(at): a for a, s in enumerate(stacked) if s is not None}
        n = len(self.items)
        self.scratch = [pltpu.SemaphoreType.DMA((n, 3)), pltpu.SemaphoreType.DMA((n, 3)),
                        pltpu.SemaphoreType.DMA((max(self.n_gathered, 1),))]

    def _plan(self, ins, outs, sems):
        x, y, c = _mesh_pos()
        chip = 2 * x + y
        n = len(self.items)

        def copy(i, k, sending):
            a, layer = self.items[i]
            px, py = x ^ ((k >> 1) & 1), y ^ (k & 1)
            if layer is not None:
                src, dst = ins[a].at[2 * px + py, layer], outs[a].at[k - 1, layer]
            else:
                src, dst = ins[a], outs[a].at[chip if sending else 2 * px + py]
            return _Transfer(src, dst, dst.shape[-2], sems[0].at[i, k - 1], sems[1].at[i, k - 1], (px, py, c))

        local = [pltpu.make_async_copy(ins[a], outs[a].at[chip], sems[2].at[a - self.n_by_chip])
                 for a in range(self.n_by_chip, self.n_by_chip + self.n_gathered)]
        return n, copy, local

    def start(self, ins, outs, sems):
        n, copy, local = self._plan(ins, outs, sems)
        for cp in local:
            cp.start()
        for k in range(1, N_CHIP):
            for a in range(n):
                copy(a, k, True).start()

    def finish(self, ins, outs, sems):
        n, copy, local = self._plan(ins, outs, sems)
        for k in range(1, N_CHIP):
            for a in range(n):
                copy(a, k, False).wait_recv()
        for k in range(1, N_CHIP):
            for a in range(n):
                copy(a, k, True).wait_send()
        for cp in local:
            cp.wait()


ADAM_LR = 0.001
ADAM_B1 = 0.9
ADAM_B2 = 0.999
ADAM_EPS = 1e-08
ADAM_WD = 0.01
ADAM_STEP = 10


def adam_reduce(parts, w, m, v, rows, name, own=None, chip=None):
    n_l, n_r, n_c = w.shape
    n_parts = parts.shape[0]

    def body(*refs):
        p_ref, w_ref, m_ref, v_ref, g_ref, d_ref, m2_ref, v2_ref = refs[-8:]
        g = p_ref[0, 0].astype(F32)
        if own is not None:
            g = refs[-9][...].reshape(rows, n_c).astype(F32) + g
        for d in range(1, n_parts):
            g = g + p_ref[d, 0].astype(F32)
        m2 = ADAM_B1 * m_ref[0] + (1.0 - ADAM_B1) * g
        v2 = ADAM_B2 * v_ref[0] + (1.0 - ADAM_B2) * (g * g)
        m_hat = m2 / (1.0 - ADAM_B1 ** ADAM_STEP)
        v_hat = v2 / (1.0 - ADAM_B2 ** ADAM_STEP)
        g_ref[0] = g
        d_ref[0] = -ADAM_LR * (m_hat / (jnp.sqrt(v_hat) + ADAM_EPS) + ADAM_WD * w_ref[0])
        m2_ref[0] = m2
        v2_ref[0] = v2

    blk = lambda: pl.BlockSpec((1, rows, n_c), lambda l, r, *_: (l, r, 0))
    in_specs = [pl.BlockSpec((n_parts, 1, rows, n_c), lambda l, r, *_: (0, l, r, 0)), blk(), blk(), blk()]
    args = (parts, w, m, v)
    if own is not None:
        in_specs = [pl.BlockSpec((1, 1, rows, n_c), lambda l, r, s: (s[0], l, r, 0))] + in_specs
        args = (chip, own) + args
    grid_spec = pltpu.PrefetchScalarGridSpec(
        num_scalar_prefetch=0 if own is None else 1, grid=(n_l, n_r // rows), in_specs=in_specs,
        out_specs=[blk(), blk(), blk(), blk()])
    return pl.pallas_call(
        body,
        name=name,
        grid_spec=grid_spec,
        out_shape=[jax.ShapeDtypeStruct(w.shape, F32)] * 4,
        compiler_params=_cparams("parallel", "parallel"),
    )(*args)


_SMALL = (("norm_g", (2, 1024)), ("gmlp_ln_g", (2, 4, 64)), ("gmlp_ln_b", (2, 4, 64)),
          ("gmlp_b_s", (2, 4, 128)), ("hgrn_lb", (2, 256)), ("hgrn_onorm_g", (2, 64)), ("fox_b_f", (2, 8)),
          ("final_norm_g", (1024,)), ("loss", ()))


def _padded(n):
    return -(-n // LANES) * LANES


_SMALL_ROWS = -(-sum(_padded(int(np.prod(s))) for _, s in _SMALL) // LANES // 8) * 8


def _pack_small(vals):
    flat = []
    for (name, shape), a in zip(_SMALL, vals, strict=True):
        n = int(np.prod(shape))
        flat.append(jnp.pad(a.reshape(n).astype(F32), (0, _padded(n) - n)))
    flat = jnp.concatenate(flat)
    return jnp.pad(flat, (0, _SMALL_ROWS * LANES - flat.shape[0])).reshape(_SMALL_ROWS, LANES)


def _unpack_small(slab):
    flat, out, at = slab.reshape(-1), {}, 0
    for name, shape in _SMALL:
        n = int(np.prod(shape))
        out[name] = flat[at:at + n].reshape(shape)
        at += _padded(n)
    return out


def kernel(x, norm_g, w_in, w_out, gmlp_ln_g, gmlp_ln_b, gmlp_w_s, gmlp_b_s, hgrn_lb, hgrn_onorm_g, fox_b_f, final_norm_g, loss_target, m_norm_g, m_w_in, m_w_out, m_gmlp_ln_g, m_gmlp_ln_b, m_gmlp_w_s, m_gmlp_b_s, m_hgrn_lb, m_hgrn_onorm_g, m_fox_b_f, m_final_norm_g, v_norm_g, v_w_in, v_w_out, v_gmlp_ln_g, v_gmlp_ln_b, v_gmlp_w_s, v_gmlp_b_s, v_hgrn_lb, v_hgrn_onorm_g, v_fox_b_f, v_final_norm_g):
    depth = w_in.shape[0]
    seq = x.shape[1]
    assert w_in.shape[2] * N_DEV == N_IN
    xs, tgt = x[0], loss_target[0]

    wi_blk, wo_blk = w_in.astype(BF16), w_out.astype(BF16)
    wi_all, wo_all = _exchange_call(AllGatherWeights(wi_blk[0], wo_blk[0]), "allgather_weights_0")

    ln_g = gmlp_ln_g.reshape(depth, 1, A_WIDTH)
    ln_b = gmlp_ln_b.reshape(depth, 1, A_WIDTH)
    bs_t = jnp.pad(jnp.transpose(gmlp_b_s, (0, 2, 1)), ((0, 0), (0, 0), (0, LANES - A_GROUPS)))
    lb0, lb1 = hgrn_lb[0:1], hgrn_lb[1:2]
    onorm = jnp.tile(hgrn_onorm_g, (1, B_HEADS)).reshape(depth, 1, B_WIDTH)
    bf_row = jnp.pad(fox_b_f, ((0, 0), (0, LANES - C_HEADS))).reshape(depth, 1, LANES)

    core = lax.axis_index("c").astype(jnp.int32).reshape(1)
    chip = (2 * lax.axis_index("x") + lax.axis_index("y")).astype(jnp.int32).reshape(1)

    saved = []
    xc = xs
    for l in range(depth):
        wi_int = assemble_w_in(wi_all[:, None])
        proj, h = inproj(xc, norm_g[l:l + 1], wi_int, 0)
        ya = gmlp_fwd(proj, ln_g[l], ln_b[l], gmlp_w_s[l], bs_t[l])
        yb, states = hgrn_fwd(proj, lb0, lb1, onorm[l], l)
        ka, va, vt, kt, qt, qa = fox_prep(proj, bf_row[l])
        nxt = AllGatherWeights(wi_blk[l + 1], wo_blk[l + 1]) if l + 1 < depth else None
        o, lse, *gathered = fox_fwd(qt, ka, vt, nxt)
        xn, yfull = outproj(xc, ya, yb, o, proj, wo_all[:, None], 0)
        saved.append((xc, proj, h, states, ka, va, kt, qt, qa, o, lse, yfull, wi_int, wo_all))
        if gathered:
            wi_all, wo_all = gathered
        xc = xn

    dx, d_final_g, loss_tile = final_loss(xc, final_norm_g[None], tgt)

    n_shard = w_in.shape[2]
    g_norm = [None] * depth
    g_ln_g, g_ln_b, g_ws, g_bs, g_on, g_bf = ([None] * depth for _ in range(6))
    g_lb0, g_lb1 = jnp.zeros_like(lb0), jnp.zeros_like(lb1)
    swi = swo = rwi = rwo = None
    for l in reversed(range(depth)):
        x_in, proj, h, states, ka, va, kt, qt, qa, o, lse, yfull, wi_int, wo_l = saved[l]
        dy, gwo = outproj_bwd(dx, yfull, wo_l[:, None], 0)
        gwo = gwo[:, :, 0]
        dproj, g_ln_g[l], g_ln_b[l], g_ws[l], dbs_t = gmlp_bwd(proj, dy, ln_g[l], ln_b[l], gmlp_w_s[l], bs_t[l])
        g_bs[l] = dbs_t[:, :A_GROUPS].T
        if l > 0:
            (qwo,) = _exchange_call(PairExchange([gwo]), f"pair_exchange_w_out_{l}")
        else:
            gws = jnp.stack(g_ws).reshape(-1, LANES)
            qwo, qws = _exchange_call(PairExchange([gwo], [gws]), f"pair_exchange_w_out_{l}")
            sws = small_sum(gws, qws, "pair_sum_w_s")
        swo = pair_sum(gwo, qwo, BF16, gwo.shape[2], "pair_sum_w_out", core, l, depth, swo)
        dproj, d0, d1, don = hgrn_bwd(proj, states, dy, lb0, lb1, onorm[l], l, dproj)
        g_lb0, g_lb1 = g_lb0 + d0, g_lb1 + d1
        g_on[l] = don.reshape(B_HEADS, B_KDIM).sum(0)
        dob, dproj, dot_t = fox_bwd_prep(dy, o, proj, dproj)
        top = l == depth - 1
        ride = ChipExchange([swo] if top else [swi, swo], [(l,)] if top else [(l + 1,), (l,)],
                            [sws] if l == 0 else [], [rwo] if top else [rwi, rwo])
        outs = fox_bwd(ka, va, kt, qt, dot_t, qa, dob, lse, ride)
        dqkv, (dck, dcq), got = outs[:3], outs[3:5], list(outs[5:])
        if not top:
            rwi = got.pop(0)
        rwo = got.pop(0)
        if l == 0:
            (rws,) = got
        dproj, dbf = fox_post(dcq, dck, proj, bf_row[l], dproj)
        g_bf[l] = dbf[0, :C_HEADS]
        gwi = split_w_in_grad(inproj_bwd_w(h, dproj, dqkv), n_shard)[:, :, 0]
        (qwi,) = _exchange_call(PairExchange([gwi]), f"pair_exchange_w_in_{l}")
        swi = pair_sum(gwi, qwi, BF16, 256, "pair_sum_w_in", core, l, depth, swi)
        ride = ChipExchange([swi], [(l,)], stacked=[rwi]) if l == 0 else None
        outs = inproj_bwd_x(dproj, dqkv, wi_int, x_in, norm_g[l:l + 1], dx, 0, ride)
        dx, g_norm[l] = outs[:2]
        if ride is not None:
            (rwi,) = outs[2:]

    gsm = _pack_small([
        jnp.concatenate(g_norm), jnp.stack(g_ln_g), jnp.stack(g_ln_b), jnp.stack(g_bs),
        jnp.concatenate([g_lb0, g_lb1]), jnp.stack(g_on), jnp.stack(g_bf), d_final_g, loss_tile[0, 0]])
    (qsm,) = _exchange_call(PairExchange([], [gsm]), "pair_exchange_small")
    ssm = small_sum(gsm, qsm, "pair_sum_small")
    (rsm,) = _exchange_call(ChipExchange(gathered=[ssm]), "chip_exchange_small")

    small_w = (norm_g, gmlp_ln_g, gmlp_ln_b, gmlp_b_s, hgrn_lb, hgrn_onorm_g, fox_b_f, final_norm_g)
    small_m = (m_norm_g, m_gmlp_ln_g, m_gmlp_ln_b, m_gmlp_b_s, m_hgrn_lb, m_hgrn_onorm_g, m_fox_b_f, m_final_norm_g)
    small_v = (v_norm_g, v_gmlp_ln_g, v_gmlp_ln_b, v_gmlp_b_s, v_hgrn_lb, v_hgrn_onorm_g, v_fox_b_f, v_final_norm_g)
    zero = jnp.zeros((), F32)
    res_wi = adam_reduce(rwi, w_in, m_w_in, v_w_in, 256, "adam_w_in", own=swi, chip=chip)
    res_wo = adam_reduce(rwo, w_out, m_w_out, v_w_out, w_out.shape[1], "adam_w_out", own=swo, chip=chip)
    res_sm = adam_reduce(rsm[:, None], _pack_small(small_w + (zero,))[None], _pack_small(small_m + (zero,))[None],
                         _pack_small(small_v + (zero,))[None], _SMALL_ROWS, "adam_small")
    res_sm = [_unpack_small(r[0]) for r in res_sm]
    as_rows = lambda a: a.reshape(1, -1, LANES)
    res_ws = adam_reduce(rws[:, None], as_rows(gmlp_w_s), as_rows(m_gmlp_w_s), as_rows(v_gmlp_w_s), rws.shape[1], "adam_w_s")
    for s, r in zip(res_sm, res_ws, strict=True):
        s["gmlp_w_s"] = r.reshape(gmlp_w_s.shape)

    def group(i):
        s = res_sm[i]
        return [s["norm_g"], res_wi[i], res_wo[i], s["gmlp_ln_g"], s["gmlp_ln_b"], s["gmlp_w_s"], s["gmlp_b_s"],
                s["hgrn_lb"], s["hgrn_onorm_g"], s["fox_b_f"], s["final_norm_g"]]

    return (res_sm[0]["loss"], dx[None], *group(0), *group(1), *group(2), *group(3))
```

```python
import functools

import jax
import jax.numpy as jnp
import numpy as np
from jax import lax
from jax.experimental import pallas as pl
from jax.experimental.pallas import tpu as pltpu

F32 = jnp.float32
BF16 = jnp.bfloat16

NORM_EPS = 1e-6
F_FLOOR = 1e-30
CHUNK = 128
LANES = 128
VMEM_LIMIT = 56 * 1024 * 1024


def _cparams(*sem):
    return pltpu.CompilerParams(dimension_semantics=sem, vmem_limit_bytes=VMEM_LIMIT)


def _dot(a, b, dims=(((1,), (0,)), ((), ())), precision=None):
    return lax.dot_general(a, b, dims, precision=precision, preferred_element_type=F32)


_NT = (((1,), (1,)), ((), ()))
_TN = (((0,), (0,)), ((), ()))


def _bf16_pieces(x, n):
    out, r = [], x
    for i in range(n):
        out.append(r.astype(BF16))
        if i + 1 < n:
            r = r - out[-1].astype(F32)
    return out


@functools.partial(jax.custom_vjp, nondiff_argnums=(2,))
def _times_exact(x, e, n):
    return functools.reduce(jnp.add, [_dot(p, e) for p in _bf16_pieces(x, n)])


def _times_exact_fwd(x, e, n):
    return _times_exact(x, e, n), e


def _times_exact_bwd(n, e, g):
    dx = functools.reduce(jnp.add, [lax.dot_general(p, e, _NT, preferred_element_type=F32) for p in _bf16_pieces(g, n)])
    return dx, jnp.zeros_like(e)


_times_exact.defvjp(_times_exact_fwd, _times_exact_bwd)


@functools.partial(jax.custom_vjp, nondiff_argnums=(2,))
def _exact_times(e, x, n):
    return functools.reduce(jnp.add, [_dot(e, p) for p in _bf16_pieces(x, n)])


def _exact_times_fwd(e, x, n):
    return _exact_times(e, x, n), e


def _exact_times_bwd(n, e, g):
    dx = functools.reduce(jnp.add, [lax.dot_general(e, p, _TN, preferred_element_type=F32) for p in _bf16_pieces(g, n)])
    return jnp.zeros_like(e), dx


_exact_times.defvjp(_exact_times_fwd, _exact_times_bwd)


def _group_mean_matrix(width, group):
    idx = np.arange(width) // group
    return jnp.asarray((idx[:, None] == idx[None, :]).astype(np.float32) / group, BF16)


def _group_ones_matrix(width, group):
    idx = np.arange(width) // group
    return jnp.asarray((idx[:, None] == idx[None, :]).astype(np.float32), BF16)


A_WIDTH = 256
A_GROUPS = 4
A_GDIM = 64


A_ROWS = 512


def _gmlp_chunk(x3, ln_g, ln_b, w_s, bs_t, mean_m, gind):
    n = x3.shape[0] // CHUNK
    u = jax.nn.gelu(x3[:, :A_WIDTH])
    v = jax.nn.gelu(x3[:, A_WIDTH:2 * A_WIDTH])
    z = x3[:, 2 * A_WIDTH:]
    mu = _times_exact(v, mean_m, 2)
    d = v - mu
    var = _times_exact(d * d, mean_m, 2)
    vn = d * lax.rsqrt(var + NORM_EPS) * ln_g + ln_b
    vnb = vn.astype(BF16)
    wide = jnp.concatenate([vnb[i * CHUNK:(i + 1) * CHUNK] for i in range(n)], axis=1)
    row = lax.broadcasted_iota(jnp.int32, (CHUNK, CHUNK), 0)
    col = lax.broadcasted_iota(jnp.int32, (CHUNK, CHUNK), 1)
    causal = row >= col
    lane_g = lax.shift_right_logical(lax.broadcasted_iota(jnp.int32, (CHUNK, n * A_WIDTH), 1), 6) & (A_GROUPS - 1)
    bias = _times_exact(bs_t, gind, 3)
    mixed = jnp.concatenate([bias] * n, axis=1)
    for g in range(A_GROUPS):
        wc = jnp.where(causal, w_s[g], 0.0).astype(BF16)
        mixed = mixed + jnp.where(lane_g == g, _dot(wc, wide), 0.0)
    mixed = jnp.concatenate([mixed[:, i * A_WIDTH:(i + 1) * A_WIDTH] for i in range(n)], axis=0)
    return u * mixed * jax.nn.silu(z)


def _gmlp_consts():
    gind = np.zeros((LANES, A_WIDTH), np.float32)
    for g in range(A_GROUPS):
        gind[g, g * A_GDIM:(g + 1) * A_GDIM] = 1.0
    return _group_mean_matrix(A_WIDTH, A_GDIM), jnp.asarray(gind, BF16)


def _full(shape):
    return pl.BlockSpec(shape, lambda *_: (0,) * len(shape))


def gmlp_fwd(proj, ln_g, ln_b, w_s, bs_t):
    seq = proj.shape[0]
    rows = min(A_ROWS, seq)
    mean_m, gind = _gmlp_consts()

    def body(x_ref, g_ref, b_ref, w_ref, bs_ref, m_ref, gi_ref, y_ref):
        y = _gmlp_chunk(x_ref[...], g_ref[...], b_ref[...], w_ref[...], bs_ref[...], m_ref[...], gi_ref[...])
        y_ref[...] = y.astype(BF16)

    return pl.pallas_call(
        body,
        name="gmlp_fwd",
        grid=(seq // rows,),
        in_specs=[
            pl.BlockSpec((rows, 3 * A_WIDTH), lambda n: (n, 0)),
            _full((1, A_WIDTH)), _full((1, A_WIDTH)), _full((A_GROUPS, CHUNK, CHUNK)), _full((CHUNK, LANES)),
            _full((A_WIDTH, A_WIDTH)), _full((LANES, A_WIDTH)),
        ],
        out_specs=pl.BlockSpec((rows, A_WIDTH), lambda n: (n, 0)),
        out_shape=jax.ShapeDtypeStruct((seq, A_WIDTH), BF16),
        compiler_params=_cparams("parallel"),
    )(proj, ln_g, ln_b, w_s, bs_t, mean_m, gind)


def gmlp_bwd(proj, dy, ln_g, ln_b, w_s, bs_t):
    seq = proj.shape[0]
    rows = min(A_ROWS, seq)
    mean_m, gind = _gmlp_consts()

    def body(x_ref, dy_ref, g_ref, b_ref, w_ref, bs_ref, m_ref, gi_ref, dx_ref, dg_ref, db_ref, dw_ref, dbs_ref):
        fn = functools.partial(_gmlp_chunk, mean_m=m_ref[...], gind=gi_ref[...])
        _, vjp = jax.vjp(fn, x_ref[...], g_ref[...], b_ref[...], w_ref[...], bs_ref[...])
        dx, dg, db, dw, dbs = vjp(dy_ref[...])
        dx_ref[...] = dx.astype(BF16)

        @pl.when(pl.program_id(0) == 0)
        def _():
            dg_ref[...] = jnp.zeros_like(dg_ref)
            db_ref[...] = jnp.zeros_like(db_ref)
            dw_ref[...] = jnp.zeros_like(dw_ref)
            dbs_ref[...] = jnp.zeros_like(dbs_ref)

        dg_ref[...] += dg
        db_ref[...] += db
        dw_ref[...] += dw
        dbs_ref[...] += dbs

    return pl.pallas_call(
        body,
        name="gmlp_bwd",
        grid=(seq // rows,),
        in_specs=[
            pl.BlockSpec((rows, 3 * A_WIDTH), lambda n: (n, 0)),
            pl.BlockSpec((rows, A_WIDTH), lambda n: (n, 0)),
            _full((1, A_WIDTH)), _full((1, A_WIDTH)), _full((A_GROUPS, CHUNK, CHUNK)), _full((CHUNK, LANES)),
            _full((A_WIDTH, A_WIDTH)), _full((LANES, A_WIDTH)),
        ],
        out_specs=[
            pl.BlockSpec((rows, 3 * A_WIDTH), lambda n: (n, 0)),
            _full((1, A_WIDTH)), _full((1, A_WIDTH)), _full((A_GROUPS, CHUNK, CHUNK)), _full((CHUNK, LANES)),
        ],
        out_shape=[
            jax.ShapeDtypeStruct((seq, D_INT), BF16),
            jax.ShapeDtypeStruct((1, A_WIDTH), F32), jax.ShapeDtypeStruct((1, A_WIDTH), F32),
            jax.ShapeDtypeStruct((A_GROUPS, CHUNK, CHUNK), F32), jax.ShapeDtypeStruct((CHUNK, LANES), F32),
        ],
        compiler_params=_cparams("arbitrary"),
    )(proj, dy, ln_g, ln_b, w_s, bs_t, mean_m, gind)


B_WIDTH = 256
B_HEADS = 4
B_KDIM = 64
B_LEVELS = (64, 32, 16, 8, 4, 2, 1)


def _hgrn_consts():
    t = np.arange(CHUNK)
    u = t[None, :]
    mats = [np.tril(np.ones((CHUNK, CHUNK), np.float32))]
    for m in B_LEVELS:
        p = (t // (2 * m)) * (2 * m) + m - 1
        right = (t % (2 * m)) >= m
        sel = np.where(right[:, None], (u > p[:, None]) & (u <= t[:, None]), (u > t[:, None]) & (u <= p[:, None]))
        mats.append(sel.astype(np.float32))
    return jnp.asarray(np.concatenate(mats, 0), BF16), _group_ones_matrix(B_WIDTH, B_KDIM)


def _hgrn_lower_bound(lb0, lb1, layer):
    mx = jnp.maximum(lb0, lb1)
    e0 = jnp.exp(lb0 - mx)
    e1 = jnp.exp(lb1 - mx)
    p0 = e0 / (e0 + e1)
    p1 = e1 / (e0 + e1)
    cs = p0 if layer == 0 else p0 + p1
    return jnp.clip(cs - p0, 0.0, 1.0 - 1e-6)


def _hgrn_chunk(x4, st, lb0, lb1, onorm, layer, tstack, ones_bd):
    q_raw, fl, v, zg = (x4[:, i * B_WIDTH:(i + 1) * B_WIDTH] for i in range(4))
    lb = _hgrn_lower_bound(lb0, lb1, layer)
    q = jax.nn.silu(q_raw) * (B_KDIM ** -0.5)
    f = lb + (1.0 - lb) * jax.nn.sigmoid(fl)
    logf = jnp.log(jnp.maximum(f, F_FLOOR))
    k = (1.0 - lb) * jax.nn.sigmoid(-fl)
    b = _exact_times(tstack[:CHUNK], logf, 3)
    dall = jnp.concatenate([b, _exact_times(tstack[CHUNK:], logf, 2)], axis=0)
    b_last = jnp.sum(logf, axis=0, keepdims=True)
    vb = v.astype(BF16)

    lane_h = lax.shift_right_logical(lax.broadcasted_iota(jnp.int32, (CHUNK, B_WIDTH), 1), 6)
    row = lax.broadcasted_iota(jnp.int32, (CHUNK, B_WIDTH), 0)
    srow = lax.broadcasted_iota(jnp.int32, (B_HEADS * CHUNK, CHUNK), 0) & (CHUNK - 1)
    scol = lax.broadcasted_iota(jnp.int32, (B_HEADS * CHUNK, CHUNK), 1)

    def heads_on_rows(a):
        return jnp.concatenate([jnp.where(lane_h == h, a, 0.0) for h in range(B_HEADS)], axis=0)

    def heads_from_rows(r):
        out = jnp.where(lane_h == 0, r[:CHUNK], 0.0)
        for h in range(1, B_HEADS):
            out = out + jnp.where(lane_h == h, r[h * CHUNK:(h + 1) * CHUNK], 0.0)
        return out

    o = lax.dot_general((q * jnp.exp(b)).astype(BF16), st.astype(BF16), _NT, preferred_element_type=F32)
    scores = jnp.zeros((B_HEADS * CHUNK, CHUNK), F32)
    for li, m in enumerate(B_LEVELS):
        e = jnp.exp(dall[(li + 1) * CHUNK:(li + 2) * CHUNK])
        right = (row & (2 * m - 1)) >= m
        qt = jnp.where(right, q * e, 0.0)
        kt = jnp.where(right, 0.0, k * e)
        sc = lax.dot_general(heads_on_rows(qt).astype(BF16), kt.astype(BF16), _NT, preferred_element_type=F32)
        sh = int(np.log2(2 * m))
        same = lax.shift_right_logical(srow, sh) == lax.shift_right_logical(scol, sh)
        scores = scores + jnp.where(same, sc, 0.0)
    o = o + heads_from_rows(_dot(scores.astype(BF16), vb))
    o = o + _times_exact(q * k, ones_bd, 2) * v

    kv = lax.dot_general(vb, (k * jnp.exp(b_last - b)).astype(BF16), _TN, preferred_element_type=F32)
    st_new = st * jnp.exp(b_last) + jnp.where(ones_bd > 0.5, kv, 0.0)

    ms = _times_exact(o * o, ones_bd, 2) * (1.0 / B_KDIM)
    y = o * lax.rsqrt(ms + NORM_EPS) * onorm * jax.nn.silu(zg)
    return y, st_new


B_ROWS = 256


def _hgrn_rows(x4, st, lb0, lb1, onorm, layer, tstack, ones_bd):
    ys = []
    for i in range(x4.shape[0] // CHUNK):
        y, st = _hgrn_chunk(x4[i * CHUNK:(i + 1) * CHUNK], st, lb0, lb1, onorm, layer, tstack, ones_bd)
        ys.append(y)
    return jnp.concatenate(ys, axis=0), st


def hgrn_fwd(proj, lb0, lb1, onorm, layer):
    seq = proj.shape[0]
    rows = min(B_ROWS, seq)
    nc = seq // rows
    tstack, ones_bd = _hgrn_consts()

    def body(x_ref, lb0_ref, lb1_ref, on_ref, t_ref, e_ref, y_ref, st_out_ref, st_ref):
        @pl.when(pl.program_id(0) == 0)
        def _():
            st_ref[...] = jnp.zeros_like(st_ref)

        st = st_ref[...]
        st_out_ref[0] = st
        y, st_new = _hgrn_rows(x_ref[...], st, lb0_ref[...], lb1_ref[...], on_ref[...], layer, t_ref[...], e_ref[...])
        y_ref[...] = y.astype(BF16)
        st_ref[...] = st_new

    return pl.pallas_call(
        body,
        name=f"hgrn_fwd_{layer}",
        grid=(nc,),
        in_specs=[
            pl.BlockSpec((rows, 4 * B_WIDTH), lambda n: (n, 1)),
            _full((1, B_WIDTH)), _full((1, B_WIDTH)), _full((1, B_WIDTH)),
            _full(((len(B_LEVELS) + 1) * CHUNK, CHUNK)), _full((B_WIDTH, B_WIDTH)),
        ],
        out_specs=[
            pl.BlockSpec((rows, B_WIDTH), lambda n: (n, 0)),
            pl.BlockSpec((1, B_WIDTH, B_WIDTH), lambda n: (n, 0, 0)),
        ],
        out_shape=[jax.ShapeDtypeStruct((seq, B_WIDTH), BF16), jax.ShapeDtypeStruct((nc, B_WIDTH, B_WIDTH), F32)],
        scratch_shapes=[pltpu.VMEM((B_WIDTH, B_WIDTH), F32)],
        compiler_params=_cparams("arbitrary"),
    )(proj, lb0, lb1, onorm, tstack, ones_bd)


def hgrn_bwd(proj, states, dy, lb0, lb1, onorm, layer, dproj):
    seq = proj.shape[0]
    rows = min(B_ROWS, seq)
    nc = seq // rows
    tstack, ones_bd = _hgrn_consts()

    def body(x_ref, st_in_ref, dy_ref, lb0_ref, lb1_ref, on_ref, t_ref, e_ref, _, dx_ref, d0_ref, d1_ref, don_ref, dst_ref):
        @pl.when(pl.program_id(0) == 0)
        def _():
            dst_ref[...] = jnp.zeros_like(dst_ref)
            d0_ref[...] = jnp.zeros_like(d0_ref)
            d1_ref[...] = jnp.zeros_like(d1_ref)
            don_ref[...] = jnp.zeros_like(don_ref)

        fn = functools.partial(_hgrn_rows, layer=layer, tstack=t_ref[...], ones_bd=e_ref[...])
        _, vjp = jax.vjp(fn, x_ref[...], st_in_ref[0], lb0_ref[...], lb1_ref[...], on_ref[...])
        dx, dst, d0, d1, don = vjp((dy_ref[...], dst_ref[...]))
        dx_ref[...] = dx.astype(BF16)
        dst_ref[...] = dst
        d0_ref[...] += d0
        d1_ref[...] += d1
        don_ref[...] += don

    rev = lambda n: nc - 1 - n
    return pl.pallas_call(
        body,
        name=f"hgrn_bwd_{layer}",
        grid=(nc,),
        in_specs=[
            pl.BlockSpec((rows, 4 * B_WIDTH), lambda n: (rev(n), 1)),
            pl.BlockSpec((1, B_WIDTH, B_WIDTH), lambda n: (rev(n), 0, 0)),
            pl.BlockSpec((rows, B_WIDTH), lambda n: (rev(n), 1)),
            _full((1, B_WIDTH)), _full((1, B_WIDTH)), _full((1, B_WIDTH)),
            _full(((len(B_LEVELS) + 1) * CHUNK, CHUNK)), _full((B_WIDTH, B_WIDTH)), _ANY,
        ],
        out_specs=[
            pl.BlockSpec((rows, 4 * B_WIDTH), lambda n: (rev(n), 1)),
            _full((1, B_WIDTH)), _full((1, B_WIDTH)), _full((1, B_WIDTH)),
        ],
        out_shape=[jax.ShapeDtypeStruct(dproj.shape, BF16)] + [jax.ShapeDtypeStruct((1, B_WIDTH), F32)] * 3,
        input_output_aliases={8: 0},
        scratch_shapes=[pltpu.VMEM((B_WIDTH, B_WIDTH), F32)],
        compiler_params=_cparams("arbitrary"),
    )(proj, states, dy, lb0, lb1, onorm, tstack, ones_bd, dproj)


D_MODEL = 1024
D_INT = 4096


def _rms_stats(xf):
    r = lax.rsqrt(jnp.mean(xf * xf, axis=-1, keepdims=True) + NORM_EPS)
    return r, xf * r


def _rms_bwd(dy, g, r, xh):
    u = dy * g
    return r * (u - xh * jnp.mean(u * xh, axis=-1, keepdims=True))


def inproj(x, g, w, layer):
    seq = x.shape[0]
    tm = min(seq, 512)

    def body(x_ref, g_ref, w_ref, p_ref, h_ref):
        _, xh = _rms_stats(x_ref[...])
        h = (xh * g_ref[...]).astype(BF16)
        h_ref[...] = h
        p_ref[...] = _dot(h, w_ref[0])

    return pl.pallas_call(
        body,
        name="inproj",
        grid=(seq // tm,),
        in_specs=[
            pl.BlockSpec((tm, D_MODEL), lambda i: (i, 0)),
            _full((1, D_MODEL)),
            pl.BlockSpec((1, D_MODEL, D_INT), lambda i: (layer, 0, 0)),
        ],
        out_specs=[pl.BlockSpec((tm, D_INT), lambda i: (i, 0)), pl.BlockSpec((tm, D_MODEL), lambda i: (i, 0))],
        out_shape=[jax.ShapeDtypeStruct((seq, D_INT), F32), jax.ShapeDtypeStruct((seq, D_MODEL), BF16)],
        compiler_params=_cparams("parallel"),
    )(x, g, w)


def outproj(x, ya, yb, o, proj, wo, layer):
    seq = x.shape[0]
    tm = min(seq, 512)
    blk = wo.shape[2]

    def body(x_ref, ya_ref, yb_ref, o_ref, z_ref, w_ref, xn_ref, y_ref):
        yc = (o_ref[...] * jax.nn.silu(z_ref[...])).astype(BF16)
        y = jnp.concatenate([ya_ref[...], yb_ref[...], yc], axis=1)
        y_ref[...] = y
        w = jnp.concatenate([w_ref[d, 0] for d in range(N_DEV)], axis=0)
        xn_ref[...] = x_ref[...] + _dot(y, w)

    return pl.pallas_call(
        body,
        name="outproj",
        grid=(seq // tm,),
        in_specs=[
            pl.BlockSpec((tm, D_MODEL), lambda i: (i, 0)),
            pl.BlockSpec((tm, 256), lambda i: (i, 0)),
            pl.BlockSpec((tm, 256), lambda i: (i, 0)),
            pl.BlockSpec((tm, 512), lambda i: (i, 0)),
            pl.BlockSpec((tm, 512), lambda i: (i, 7)),
            pl.BlockSpec((N_DEV, 1, blk, D_MODEL), lambda i: (0, layer, 0, 0)),
        ],
        out_specs=[pl.BlockSpec((tm, D_MODEL), lambda i: (i, 0)), pl.BlockSpec((tm, D_MODEL), lambda i: (i, 0))],
        out_shape=[jax.ShapeDtypeStruct((seq, D_MODEL), F32), jax.ShapeDtypeStruct((seq, D_MODEL), BF16)],
        compiler_params=_cparams("parallel"),
    )(x, ya, yb, o, proj, wo)


def outproj_bwd(dx, y, wo, layer, stacked=None):
    seq = dx.shape[0]
    ts = min(seq, 512)
    _, depth, blk, _ = wo.shape

    def body(dx_ref, y_ref, w_ref, *refs):
        dy_ref, dw_ref = refs[-2:]

        @pl.when(pl.program_id(0) == 0)
        def _():
            dw_ref[...] = jnp.zeros_like(dw_ref)

        dxb = dx_ref[...].astype(BF16)
        w = jnp.concatenate([w_ref[d, 0] for d in range(N_DEV)], axis=0)
        dy_ref[...] = lax.dot_general(dxb, w, _NT, preferred_element_type=F32)
        dw = lax.dot_general(y_ref[...], dxb, _TN, preferred_element_type=F32)
        for d in range(N_DEV):
            dw_ref[d % 2, d // 2, 0] += dw[d * blk:(d + 1) * blk]

    carried = () if stacked is None else (stacked,)
    out_shape = [jax.ShapeDtypeStruct((seq, D_MODEL), F32), jax.ShapeDtypeStruct((2, N_CHIP, depth, blk, D_MODEL), F32)]
    return pl.pallas_call(
        body,
        name="outproj_bwd",
        grid=(seq // ts,),
        in_specs=[
            pl.BlockSpec((ts, D_MODEL), lambda i: (i, 0)),
            pl.BlockSpec((ts, D_MODEL), lambda i: (i, 0)),
            pl.BlockSpec((N_DEV, 1, blk, D_MODEL), lambda i: (0, layer, 0, 0)),
        ] + [_ANY] * len(carried),
        out_specs=[pl.BlockSpec((ts, D_MODEL), lambda i: (i, 0)),
                   pl.BlockSpec((2, N_CHIP, 1, blk, D_MODEL), lambda i: (0, 0, layer, 0, 0))],
        out_shape=out_shape,
        input_output_aliases={3: 1} if carried else {},
        compiler_params=_cparams("arbitrary"),
    )(dx, y, wo, *carried)


C_QKV = (2048, 3584)


def _dproj_parts(dp_ref, dqkv_refs, rows):
    lo, hi = C_QKV
    step = (hi - lo) // len(dqkv_refs)
    return ([(0, dp_ref.at[rows, 0:lo])] + [(lo + i * step, r.at[rows, :]) for i, r in enumerate(dqkv_refs)]
            + [(hi, dp_ref.at[rows, hi:D_INT])])


def inproj_bwd_x(dproj, dqkv, w, x, g, dx_in, layer, carried=None):
    seq = x.shape[0]
    tm = min(seq, 512)

    def body(dp_ref, dq_ref, dk_ref, dv_ref, w_ref, x_ref, g_ref, dxin_ref, dx_ref, dg_ref):
        @pl.when(pl.program_id(0) == 0)
        def _():
            dg_ref[...] = jnp.zeros_like(dg_ref)

        dh = None
        for at, part in _dproj_parts(dp_ref, (dq_ref, dk_ref, dv_ref), slice(None)):
            term = lax.dot_general(part[...], w_ref[0, :, at:at + part.shape[1]], _NT, preferred_element_type=F32)
            dh = term if dh is None else dh + term
        r, xh = _rms_stats(x_ref[...])
        dg_ref[...] += jnp.sum(dh * xh, axis=0, keepdims=True)
        dx_ref[...] = dxin_ref[...] + _rms_bwd(dh, g_ref[...], r, xh)

    third = lambda: pl.BlockSpec((tm, C_WIDTH), lambda i: (i, 0))
    return _call_carrying(
        carried, body, (dproj, *dqkv, w, x, g, dx_in),
        name="inproj_bwd_x",
        grid=(seq // tm,),
        in_specs=[
            pl.BlockSpec((tm, D_INT), lambda i: (i, 0)), third(), third(), third(),
            pl.BlockSpec((1, D_MODEL, D_INT), lambda i: (layer, 0, 0)),
            pl.BlockSpec((tm, D_MODEL), lambda i: (i, 0)),
            _full((1, D_MODEL)),
            pl.BlockSpec((tm, D_MODEL), lambda i: (i, 0)),
        ],
        out_specs=[pl.BlockSpec((tm, D_MODEL), lambda i: (i, 0)), _full((1, D_MODEL))],
        out_shape=[jax.ShapeDtypeStruct((seq, D_MODEL), F32), jax.ShapeDtypeStruct((1, D_MODEL), F32)],
        scratch_shapes=[], semantics=("arbitrary",),
    )


def inproj_bwd_w(h, dproj, dqkv):
    seq = h.shape[0]
    ts, tn = min(seq, 512), 512

    def body(h_ref, dp_ref, dq_ref, dk_ref, dv_ref, dw_ref):
        @pl.when(pl.program_id(0) == 0)
        def _():
            dw_ref[...] = jnp.zeros_like(dw_ref)

        ht = h_ref[...].T
        for at, part in _dproj_parts(dp_ref, (dq_ref, dk_ref, dv_ref), slice(None)):
            for c in range(0, part.shape[1], tn):
                dw_ref[0, :, at + c:at + c + tn] += _dot(ht, part[:, c:c + tn])

    third = lambda: pl.BlockSpec((ts, C_WIDTH), lambda s: (s, 0))
    return pl.pallas_call(
        body,
        name="inproj_bwd_w",
        grid=(seq // ts,),
        in_specs=[pl.BlockSpec((ts, D_MODEL), lambda s: (s, 0)), pl.BlockSpec((ts, D_INT), lambda s: (s, 0)),
                  third(), third(), third()],
        out_specs=_full((1, D_MODEL, D_INT)),
        out_shape=jax.ShapeDtypeStruct((1, D_MODEL, D_INT), F32),
        compiler_params=_cparams("arbitrary"),
    )(h, dproj, *dqkv)


N_IN = 3848


def _internal_of(col):
    return col if col < 768 else (col + 256 if col < 3840 else 768 + col - 3840)


def _column_runs(n_shard):
    runs = []
    for d in range(N_IN // n_shard):
        mine = []
        for j in range(n_shard):
            ci = _internal_of(d * n_shard + j)
            if mine and mine[-1][0] + mine[-1][1] == ci:
                mine[-1][1] += 1
            else:
                mine.append([ci, 1, j])
        runs.append(mine)
    return runs


def assemble_w_in(wi_all):
    n_dev, depth, _, n_shard = wi_all.shape
    tr = 256
    pieces = [[] for _ in range(D_INT // LANES)]
    for d, mine in enumerate(_column_runs(n_shard)):
        for ci, ln, off in mine:
            while ln > 0:
                blk, at = divmod(ci, LANES)
                take = min(ln, LANES - at)
                pieces[blk].append((at, take, d, off))
                ci, ln, off = ci + take, ln - take, off + take

    def body(x_ref, o_ref):
        for blk, parts in enumerate(pieces):
            vals, at = [], 0
            for start, ln, d, off in sorted(parts):
                if start > at:
                    vals.append(jnp.zeros((tr, start - at), BF16))
                vals.append(x_ref[d, 0, :, off:off + ln])
                at = start + ln
            if at < LANES:
                vals.append(jnp.zeros((tr, LANES - at), BF16))
            o_ref[0, :, blk * LANES:(blk + 1) * LANES] = vals[0] if len(vals) == 1 else jnp.concatenate(vals, axis=1)

    return pl.pallas_call(
        body,
        name="assemble_w_in",
        grid=(depth, D_MODEL // tr),
        in_specs=[pl.BlockSpec((n_dev, 1, tr, n_shard), lambda l, r: (0, l, r, 0))],
        out_specs=pl.BlockSpec((1, tr, D_INT), lambda l, r: (l, r, 0)),
        out_shape=jax.ShapeDtypeStruct((depth, D_MODEL, D_INT), BF16),
        compiler_params=_cparams("parallel", "parallel"),
    )(wi_all)


def split_w_in_grad(dwi, n_shard):
    depth = dwi.shape[0]
    tr = 256
    runs = _column_runs(n_shard)

    def body(x_ref, o_ref):
        for d, mine in enumerate(runs):
            for ci, ln, off in mine:
                o_ref[d % 2, d // 2, 0, :, off:off + ln] = x_ref[0, :, ci:ci + ln]

    return pl.pallas_call(
        body,
        name="split_w_in_grad",
        grid=(depth, D_MODEL // tr),
        in_specs=[pl.BlockSpec((1, tr, D_INT), lambda l, r: (l, r, 0))],
        out_specs=pl.BlockSpec((2, N_CHIP, 1, tr, n_shard), lambda l, r: (0, 0, l, r, 0)),
        out_shape=jax.ShapeDtypeStruct((2, N_CHIP, depth, D_MODEL, n_shard), F32),
        compiler_params=_cparams("parallel", "parallel"),
    )(dwi)


def final_loss(x, g, tgt):
    seq = x.shape[0]
    tm = min(seq, 512)

    def body(x_ref, g_ref, t_ref, dx_ref, dg_ref, loss_ref):
        @pl.when(pl.program_id(0) == 0)
        def _():
            dg_ref[...] = jnp.zeros_like(dg_ref)
            loss_ref[...] = jnp.zeros_like(loss_ref)

        g = g_ref[...]
        r, xh = _rms_stats(x_ref[...])
        err = xh * g - t_ref[...]
        sq = jnp.sum(jnp.sum(err * err, axis=1, keepdims=True), axis=0, keepdims=True)
        loss_ref[...] += jnp.broadcast_to(sq * (0.5 / D_MODEL), loss_ref.shape)
        dout = err * (1.0 / D_MODEL)
        dg_ref[...] += jnp.sum(dout * xh, axis=0, keepdims=True)
        dx_ref[...] = _rms_bwd(dout, g, r, xh)

    return pl.pallas_call(
        body,
        name="final_loss",
        grid=(seq // tm,),
        in_specs=[pl.BlockSpec((tm, D_MODEL), lambda i: (i, 0)), _full((1, D_MODEL)), pl.BlockSpec((tm, D_MODEL), lambda i: (i, 0))],
        out_specs=[pl.BlockSpec((tm, D_MODEL), lambda i: (i, 0)), _full((1, D_MODEL)), _full((8, LANES))],
        out_shape=[jax.ShapeDtypeStruct((seq, D_MODEL), F32), jax.ShapeDtypeStruct((1, D_MODEL), F32), jax.ShapeDtypeStruct((8, LANES), F32)],
        compiler_params=_cparams("arbitrary"),
    )(x, g, tgt)


C_WIDTH = 512
C_HEADS = 8
C_HDIM = 64
C_PAIRS = C_HEADS // 2
C_BQ = 512
C_TAIL = 16
C_KG = 4


def _split3(x):
    hi = x.astype(BF16)
    r = x - hi.astype(F32)
    mid = r.astype(BF16)
    return hi, mid, (r - mid.astype(F32)).astype(BF16)


def _piece_selectors():
    sel = np.zeros((C_HEADS, 3 * LANES, LANES), np.float32)
    for p in range(C_PAIRS):
        for e in range(2):
            for t in range(3):
                sel[2 * p + e, t * LANES + 2 * p + e, 3 * e + t] = -1.0
    return sel


def fox_prep(proj, bf_row):
    seq = proj.shape[0]
    nblk = seq // CHUNK
    tril = jnp.asarray(np.tril(np.ones((CHUNK, CHUNK), np.float32)), BF16)
    sel = jnp.asarray(_piece_selectors(), BF16)
    rows_t = CHUNK + C_TAIL

    def body(fl_ref, q_ref, k_ref, v_ref, bf_ref, l_ref, sel_ref, ka_ref, va_ref, vt_ref, kt_ref, qt_ref, qa_ref, carry_ref):
        @pl.when(pl.program_id(0) == 0)
        def _():
            carry_ref[...] = jnp.zeros_like(carry_ref)

        lf = jax.nn.log_sigmoid(fl_ref[:, :LANES] + bf_ref[...])
        c = _exact_times(l_ref[...], lf, 3) + carry_ref[...]
        carry_ref[...] += jnp.sum(lf, axis=0, keepdims=True)
        c3 = jnp.concatenate(_split3(c), axis=1)
        lane = lax.broadcasted_iota(jnp.int32, (CHUNK, LANES), 1)
        row = lax.broadcasted_iota(jnp.int32, (CHUNK, LANES), 0)
        r16 = lax.broadcasted_iota(jnp.int32, (C_TAIL, 2 * CHUNK), 0)
        l16 = lax.broadcasted_iota(jnp.int32, (C_TAIL, 2 * CHUNK), 1)
        zero = jnp.zeros((CHUNK, LANES), BF16)
        one = jnp.ones((CHUNK, LANES), BF16)

        def by_keys(x, right_a, right_b):
            xb = x.astype(BF16)
            top = jnp.concatenate([jnp.where(lane < C_HDIM, xb, zero), right_a], axis=1)
            return jnp.concatenate([top, jnp.concatenate([jnp.where(lane < C_HDIM, zero, xb), right_b], axis=1)], axis=0)

        def by_lanes(x, tail):
            xt = x.T.astype(BF16)
            main = jnp.concatenate([jnp.where(row < C_HDIM, xt, zero), jnp.where(row < C_HDIM, zero, xt)], axis=1)
            return jnp.concatenate([main, tail], axis=0)

        for p in range(C_PAIRS):
            cols = slice(p * LANES, (p + 1) * LANES)
            q2, k2, v2 = q_ref[:, cols] * (C_HDIM ** -0.5), k_ref[:, cols], v_ref[:, cols]
            negc = [_dot(c3, sel_ref[2 * p + e]).astype(BF16) for e in range(2)]
            ones3 = [jnp.where((lane >= 3 * e) & (lane < 3 * e + 3), one, zero) for e in range(2)]
            tail = jnp.where(((r16 == 2 * p) & (l16 < CHUNK)) | ((r16 == 2 * p + 1) & (l16 >= CHUNK)), 1.0, 0.0).astype(BF16)
            ka_ref[p] = by_keys(k2, negc[0], negc[1])
            va_ref[p] = by_keys(v2, ones3[0], ones3[1])
            kt_ref[p] = by_lanes(k2, tail)
            vt_ref[p] = by_lanes(v2, tail)
            qt_ref[p] = jnp.concatenate([q2.T.astype(BF16), jnp.where(row < 6, one, zero)], axis=0)
            qa_ref[p] = jnp.concatenate([q2.astype(BF16), jnp.where((lane == 2 * p) | (lane == 2 * p + 1), one, zero)], axis=1)

    wide = lambda j: pl.BlockSpec((CHUNK, C_WIDTH), lambda n: (n, j))
    by_rows = pl.BlockSpec((C_PAIRS, 2 * CHUNK, 2 * CHUNK), lambda n: (0, n, 0))
    by_cols = pl.BlockSpec((C_PAIRS, rows_t, 2 * CHUNK), lambda n: (0, 0, n))
    return pl.pallas_call(
        body,
        name="fox_prep",
        grid=(nblk,),
        in_specs=[pl.BlockSpec((CHUNK, 256), lambda n: (n, 3)), wide(4), wide(5), wide(6), _full((1, LANES)),
                  _full((CHUNK, CHUNK)), _full((C_HEADS, 3 * LANES, LANES))],
        out_specs=[by_rows, by_rows, by_cols, by_cols,
                   pl.BlockSpec((C_PAIRS, 2 * CHUNK, CHUNK), lambda n: (0, 0, n)),
                   pl.BlockSpec((C_PAIRS, CHUNK, 2 * CHUNK), lambda n: (0, n, 0))],
        out_shape=[jax.ShapeDtypeStruct((C_PAIRS, 2 * seq, 2 * CHUNK), BF16)] * 2
        + [jax.ShapeDtypeStruct((C_PAIRS, rows_t, 2 * seq), BF16)] * 2
        + [jax.ShapeDtypeStruct((C_PAIRS, 2 * CHUNK, seq), BF16), jax.ShapeDtypeStruct((C_PAIRS, seq, 2 * CHUNK), BF16)],
        scratch_shapes=[pltpu.VMEM((1, LANES), F32)],
        compiler_params=_cparams("arbitrary"),
    )(proj, proj, proj, proj, bf_row, tril, sel)


def _visible(shape, key0, query0):
    row = lax.broadcasted_iota(jnp.int32, shape, 0)
    key = key0 + lax.shift_left(lax.shift_right_logical(row, 8), 7) + (row & (CHUNK - 1))
    return key <= query0 + lax.broadcasted_iota(jnp.int32, shape, 1)


def _rows_ab(a, b, n):
    return jnp.concatenate([jnp.broadcast_to(a, (C_HDIM, n)), jnp.broadcast_to(b, (C_HDIM, n))], axis=0)


def _call_carrying(ex, body, operands, *, name, grid, in_specs, out_specs, out_shape, scratch_shapes, semantics=None):
    if ex is None:
        semantics = semantics or ("parallel", *["arbitrary"] * (len(grid) - 1))
        return pl.pallas_call(body, name=name, grid=grid, in_specs=in_specs, out_specs=out_specs, out_shape=out_shape,
                              scratch_shapes=scratch_shapes, compiler_params=_cparams(*semantics))(*operands)
    n_in, n_out = len(in_specs), len(out_specs)

    def wrapped(*refs):
        own, parts = _carried_refs(refs, n_in, n_out, ex)
        ids = [pl.program_id(a) for a in range(len(grid))]
        pl.when(functools.reduce(jnp.logical_and, [i == 0 for i in ids]))(lambda: ex.start(*parts))
        body(*own)
        pl.when(functools.reduce(jnp.logical_and, [i == g - 1 for i, g in zip(ids, grid)]))(lambda: ex.finish(*parts))

    return pl.pallas_call(
        wrapped, name=name, grid=grid,
        in_specs=list(in_specs) + [_ANY] * len(ex.inputs), out_specs=list(out_specs) + [_ANY] * len(ex.out_shape),
        out_shape=list(out_shape) + list(ex.out_shape), scratch_shapes=list(scratch_shapes) + list(ex.scratch),
        input_output_aliases={n_in + i: n_out + o for i, o in getattr(ex, "aliases", {}).items()},
        compiler_params=_cparams(*["arbitrary"] * len(grid)),
    )(*operands, *ex.inputs)


def fox_fwd(qt, ka, vt, carried=None):
    seq = qt.shape[2]
    nblk = seq // CHUNK
    bq = min(C_BQ, seq)
    grp = bq // CHUNK
    rows_t = CHUNK + C_TAIL

    def body(qt_ref, ka_ref, vt_ref, o_ref, lse_ref, acc_ref, s_ref):
        p, i = pl.program_id(0), pl.program_id(1)
        qtile = qt_ref[0]
        r16 = lax.broadcasted_iota(jnp.int32, (C_TAIL, bq), 0)

        def scores(t):
            at = pl.multiple_of(t * grp * 2 * CHUNK, 2 * CHUNK)
            return _dot(ka_ref[0, pl.ds(at, grp * 2 * CHUNK), :], qtile)

        def group(t, m, masked):
            ma, mb = m
            at = pl.multiple_of(t * grp * 2 * CHUNK, 2 * CHUNK)
            s = s_ref[...]
            if masked:
                s = jnp.where(_visible(s.shape, t * bq, i * bq), s, -jnp.inf)
            sa = [s[g * 2 * CHUNK:g * 2 * CHUNK + CHUNK] for g in range(grp)]
            sb = [s[g * 2 * CHUNK + CHUNK:(g + 1) * 2 * CHUNK] for g in range(grp)]
            na, nb = ma, mb
            for g in range(grp):
                na = jnp.maximum(na, jnp.max(sa[g], axis=0, keepdims=True))
                nb = jnp.maximum(nb, jnp.max(sb[g], axis=0, keepdims=True))
            al_a, al_b = jnp.exp(ma - na), jnp.exp(mb - nb)
            pt = jnp.concatenate([jnp.exp(x - n) for g in range(grp) for x, n in ((sa[g], na), (sb[g], nb))], axis=0)
            pv = _dot(vt_ref[0, :, pl.ds(at, grp * 2 * CHUNK)], pt.astype(BF16))
            tail = jnp.where(r16 == 2 * p, al_a, jnp.where(r16 == 2 * p + 1, al_b, 1.0))
            acc_ref[...] = acc_ref[...] * jnp.concatenate([_rows_ab(al_a, al_b, bq), tail], axis=0) + pv
            return na, nb

        def step(t, m):
            s_next = scores(t + 1)
            m = group(t, m, False)
            s_ref[...] = s_next
            return m

        acc_ref[...] = jnp.zeros_like(acc_ref)
        s_ref[...] = scores(0)
        m = (jnp.full((1, bq), -jnp.inf, F32), jnp.full((1, bq), -jnp.inf, F32))
        m = lax.fori_loop(0, i, step, m)
        ma, mb = group(i, m, True)
        tailv = acc_ref[CHUNK:rows_t, :]
        la = jnp.sum(jnp.where(r16 == 2 * p, tailv, 0.0), axis=0, keepdims=True)
        lb = jnp.sum(jnp.where(r16 == 2 * p + 1, tailv, 0.0), axis=0, keepdims=True)
        o_ref[...] = (acc_ref[0:CHUNK, :] * _rows_ab(1.0 / la, 1.0 / lb, bq)).T
        lse_ref[0, 0:1, :] = ma + jnp.log(la)
        lse_ref[0, 1:2, :] = mb + jnp.log(lb)

    return _call_carrying(
        carried, body, (qt, ka, vt),
        name="fox_fwd",
        grid=(C_PAIRS, seq // bq),
        in_specs=[
            pl.BlockSpec((1, 2 * CHUNK, bq), lambda p, i: (p, 0, i)),
            pl.BlockSpec((1, 2 * seq, 2 * CHUNK), lambda p, i: (p, 0, 0)),
            pl.BlockSpec((1, rows_t, 2 * seq), lambda p, i: (p, 0, 0)),
        ],
        out_specs=[pl.BlockSpec((bq, LANES), lambda p, i: (i, p)), pl.BlockSpec((1, 2, bq), lambda p, i: (p, 0, i))],
        out_shape=[jax.ShapeDtypeStruct((seq, C_WIDTH), F32), jax.ShapeDtypeStruct((C_PAIRS, 2, seq), F32)],
        scratch_shapes=[pltpu.VMEM((rows_t, bq), F32), pltpu.VMEM((grp * 2 * CHUNK, bq), F32)],
    )


def fox_bwd_prep(dy, o, proj, dproj):
    seq = o.shape[0]
    ind = np.zeros((C_WIDTH, LANES), np.float32)
    for h in range(C_HEADS):
        ind[h * C_HDIM:(h + 1) * C_HDIM, h] = 1.0
    ind = jnp.asarray(ind, BF16)
    sel = _piece_selectors()
    sel = jnp.asarray(np.stack([sel[2 * p].T + sel[2 * p + 1].T for p in range(C_PAIRS)]), BF16)

    def body(dy_ref, o_ref, z_ref, ind_ref, sel_ref, _, do_ref, dz_ref, dot_ref):
        dy_c, o_v, z = dy_ref[...], o_ref[...], z_ref[...]
        sg = jax.nn.sigmoid(z)
        do = dy_c * (z * sg)
        do_ref[...] = do.astype(BF16)
        dz_ref[...] = (dy_c * o_v * (sg * (1.0 + z * (1.0 - sg)))).astype(BF16)
        prod = do * o_v
        hi = prod.astype(BF16)
        lo = (prod - hi.astype(F32)).astype(BF16)
        delta = _dot(hi, ind_ref[...]) + _dot(lo, ind_ref[...])
        d3 = jnp.concatenate(_split3(delta.T), axis=0)
        for p in range(C_PAIRS):
            tail = _dot(sel_ref[p], d3).astype(BF16)
            dot_ref[p] = jnp.concatenate([do[:, p * LANES:(p + 1) * LANES].T.astype(BF16), tail], axis=0)

    return pl.pallas_call(
        body,
        name="fox_bwd_prep",
        grid=(seq // CHUNK,),
        in_specs=[
            pl.BlockSpec((CHUNK, C_WIDTH), lambda i: (i, 1)),
            pl.BlockSpec((CHUNK, C_WIDTH), lambda i: (i, 0)),
            pl.BlockSpec((CHUNK, C_WIDTH), lambda i: (i, 7)),
            _full((C_WIDTH, LANES)), _full((C_PAIRS, LANES, 3 * LANES)), _ANY,
        ],
        out_specs=[
            pl.BlockSpec((CHUNK, C_WIDTH), lambda i: (i, 0)),
            pl.BlockSpec((CHUNK, C_WIDTH), lambda i: (i, 7)),
            pl.BlockSpec((C_PAIRS, 2 * CHUNK, CHUNK), lambda i: (0, 0, i)),
        ],
        out_shape=[jax.ShapeDtypeStruct((seq, C_WIDTH), BF16), jax.ShapeDtypeStruct(dproj.shape, BF16),
                   jax.ShapeDtypeStruct((C_PAIRS, 2 * CHUNK, seq), BF16)],
        input_output_aliases={5: 1},
        compiler_params=_cparams("parallel"),
    )(dy, o, proj, ind, sel, dproj)


def fox_bwd(ka, va, kt, qt, dot_t, qa, dob, lse, carried=None):
    seq = qt.shape[2]
    nblk = seq // CHUNK
    bq = min(C_BQ, seq)
    nq = seq // bq
    kg = min(C_KG, nblk)
    ng = nblk // kg
    rows_t = CHUNK + C_TAIL

    def body(ka_ref, va_ref, kt_ref, qt_ref, dot_ref, qa_ref, do_ref, lse_ref,
             dq_ref, dk_ref, dv_ref, dck_ref, dcq_ref, dqt_acc, dv_acc, dka_acc):
        p, jg = pl.program_id(0), pl.program_id(1)

        @pl.when(jg == 0)
        def _():
            dqt_acc[...] = jnp.zeros_like(dqt_acc)

        dv_acc[...] = jnp.zeros_like(dv_acc)
        dka_acc[...] = jnp.zeros_like(dka_acc)

        def step(i, carry, masked):
            cols = pl.ds(pl.multiple_of(i * bq, bq), bq)
            qtile, dotile = qt_ref[0, :, cols], dot_ref[0, :, cols]
            do, qa_i = do_ref[cols, :], qa_ref[0, cols, :]
            lse2 = jnp.concatenate([jnp.broadcast_to(lse_ref[0, 0:1, cols], (CHUNK, bq)),
                                    jnp.broadcast_to(lse_ref[0, 1:2, cols], (CHUNK, bq))] * kg, axis=0)
            pt = jnp.exp(_dot(ka_ref[0], qtile) - lse2)
            if masked:
                pt = jnp.where(_visible(pt.shape, jg * kg * CHUNK, i * bq), pt, 0.0)
            ds = pt * _dot(va_ref[0], dotile)
            ptb, dsb = pt.astype(BF16), ds.astype(BF16)
            dv_acc[...] += _dot(ptb, do)
            dka_acc[...] += _dot(dsb, qa_i)
            dqt_acc[:, cols] += _dot(kt_ref[0], dsb)
            return carry

        i0 = (jg * kg * CHUNK) // bq
        step(i0, 0, True)
        lax.fori_loop(i0 + 1, nq, functools.partial(step, masked=False), 0)
        lane = lax.broadcasted_iota(jnp.int32, (CHUNK, LANES), 1)
        for kb in range(kg):
            rows = slice(kb * CHUNK, (kb + 1) * CHUNK)
            ra = slice(kb * 2 * CHUNK, kb * 2 * CHUNK + CHUNK)
            rb = slice(kb * 2 * CHUNK + CHUNK, (kb + 1) * 2 * CHUNK)
            dk_ref[rows, :] = jnp.where(lane < C_HDIM, dka_acc[ra, 0:LANES], dka_acc[rb, 0:LANES]).astype(BF16)
            dv_ref[rows, :] = jnp.where(lane < C_HDIM, dv_acc[ra, :], dv_acc[rb, :]).astype(BF16)
            dck_ref[0, rows, :] = (jnp.where(lane == 2 * p, dka_acc[ra, LANES:], 0.0)
                                   + jnp.where(lane == 2 * p + 1, dka_acc[rb, LANES:], 0.0))

        @pl.when(jg == ng - 1)
        def _():
            for c in range(nq):
                dq_ref[c * bq:(c + 1) * bq, :] = (dqt_acc[0:CHUNK, c * bq:(c + 1) * bq].T * (C_HDIM ** -0.5)).astype(BF16)
            dcq_ref[0] = dqt_acc[CHUNK:rows_t, :]

    per_pair = lambda r, c: pl.BlockSpec((1, r, c), lambda p, j: (p, 0, 0))
    by_rows = pl.BlockSpec((1, kg * 2 * CHUNK, 2 * CHUNK), lambda p, j: (p, j, 0))
    by_cols = pl.BlockSpec((1, rows_t, kg * 2 * CHUNK), lambda p, j: (p, 0, j))
    return _call_carrying(
        carried, body, (ka, va, kt, qt, dot_t, qa, dob, lse),
        name="fox_bwd",
        grid=(C_PAIRS, ng),
        in_specs=[by_rows, by_rows, by_cols, per_pair(2 * CHUNK, seq), per_pair(2 * CHUNK, seq),
                  per_pair(seq, 2 * CHUNK), pl.BlockSpec((seq, LANES), lambda p, j: (0, p)), per_pair(2, seq)],
        out_specs=[pl.BlockSpec((seq, LANES), lambda p, j: (0, p)),
                   pl.BlockSpec((kg * CHUNK, LANES), lambda p, j: (j, p)),
                   pl.BlockSpec((kg * CHUNK, LANES), lambda p, j: (j, p)),
                   pl.BlockSpec((1, kg * CHUNK, LANES), lambda p, j: (p, j, 0)),
                   per_pair(C_TAIL, seq)],
        out_shape=[jax.ShapeDtypeStruct((seq, C_WIDTH), BF16)] * 3
        + [jax.ShapeDtypeStruct((C_PAIRS, seq, LANES), F32), jax.ShapeDtypeStruct((C_PAIRS, C_TAIL, seq), F32)],
        scratch_shapes=[pltpu.VMEM((rows_t, seq), F32), pltpu.VMEM((kg * 2 * CHUNK, LANES), F32),
                        pltpu.VMEM((kg * 2 * CHUNK, 2 * CHUNK), F32)],
    )


def fox_post(dcq, dck, proj, bf_row, dproj):
    seq = proj.shape[0]
    nc = seq // CHUNK
    triu = jnp.asarray(np.triu(np.ones((CHUNK, CHUNK), np.float32)), BF16)

    def body(dq_ref, dk_ref, fl_ref, bf_ref, u_ref, _, dfl_ref, dbf_ref, carry_ref):
        @pl.when(pl.program_id(0) == 0)
        def _():
            carry_ref[...] = jnp.zeros_like(carry_ref)
            dbf_ref[...] = jnp.zeros_like(dbf_ref)

        rows = (dq_ref[0] + dq_ref[1]) + (dq_ref[2] + dq_ref[3])
        dc = jnp.concatenate([rows, jnp.zeros((CHUNK - C_TAIL, CHUNK), F32)], axis=0).T
        dc = dc - ((dk_ref[0] + dk_ref[1]) + (dk_ref[2] + dk_ref[3]))
        g = _exact_times(u_ref[...], dc, 3) + carry_ref[...]
        carry_ref[...] += jnp.sum(dc, axis=0, keepdims=True)
        dfl = g * jax.nn.sigmoid(-(fl_ref[:, :LANES] + bf_ref[...]))
        dbf_ref[...] += jnp.sum(dfl, axis=0, keepdims=True)
        dfl_ref[...] = jnp.concatenate([dfl, jnp.zeros_like(dfl)], axis=1).astype(BF16)

    rev = lambda n: nc - 1 - n
    return pl.pallas_call(
        body,
        name="fox_post",
        grid=(nc,),
        in_specs=[
            pl.BlockSpec((C_PAIRS, C_TAIL, CHUNK), lambda n: (0, 0, rev(n))),
            pl.BlockSpec((C_PAIRS, CHUNK, LANES), lambda n: (0, rev(n), 0)),
            pl.BlockSpec((CHUNK, 256), lambda n: (rev(n), 3)),
            _full((1, LANES)), _full((CHUNK, CHUNK)), _ANY,
        ],
        out_specs=[pl.BlockSpec((CHUNK, 256), lambda n: (rev(n), 3)), _full((1, LANES))],
        out_shape=[jax.ShapeDtypeStruct(dproj.shape, BF16), jax.ShapeDtypeStruct((1, LANES), F32)],
        input_output_aliases={5: 0},
        scratch_shapes=[pltpu.VMEM((1, LANES), F32)],
        compiler_params=_cparams("arbitrary"),
    )(dcq, dck, proj, bf_row, triu, dproj)


N_DEV = 8
MESH = pl.DeviceIdType.MESH
_ANY = pl.BlockSpec(memory_space=pl.ANY)


def _mesh_pos():
    return lax.axis_index("x"), lax.axis_index("y"), lax.axis_index("c")


def _dev_index(px, py, pc):
    return 4 * px + 2 * py + pc


def _row_pieces(ref, rows):
    return [ref.at[idx + (pl.ds(r, rows),)] for idx in np.ndindex(*ref.shape[:-2]) for r in range(0, ref.shape[-2], rows)]


class _Transfer:
    def __init__(self, src, dst, rows, send_sem, recv_sem, to):
        self.src, self.dst, self.rows, self.sems, self.to = src, dst, rows, (send_sem, recv_sem), to

    def _copy(self, src, dst):
        return pltpu.make_async_remote_copy(src_ref=src, dst_ref=dst, send_sem=self.sems[0], recv_sem=self.sems[1],
                                            device_id=self.to, device_id_type=MESH)

    def start(self):
        for s, d in zip(_row_pieces(self.src, self.rows), _row_pieces(self.dst, self.rows), strict=True):
            self._copy(s, d).start()

    def wait_send(self):
        self._copy(self.src, self.dst).wait_send()

    def wait_recv(self):
        self._copy(self.src, self.dst).wait_recv()


def _exchange_call(ex, name):
    n_in, n_out = len(ex.inputs), len(ex.out_shape)

    def body(*refs):
        parts = refs[:n_in], refs[n_in:n_in + n_out], refs[n_in + n_out:]
        ex.start(*parts)
        ex.finish(*parts)

    return pl.pallas_call(body, name=name, in_specs=[_ANY] * n_in, out_specs=[_ANY] * n_out, out_shape=ex.out_shape,
                          scratch_shapes=ex.scratch, input_output_aliases=getattr(ex, "aliases", {}))(*ex.inputs)


def _carried_refs(refs, n_in, n_out, ex):
    k_in, k_out, k_sem = (len(ex.inputs), len(ex.out_shape), len(ex.scratch)) if ex else (0, 0, 0)
    a, b, c = n_in + k_in, n_in + k_in + n_out, n_in + k_in + n_out + k_out
    own = refs[:n_in] + refs[a:b] + refs[c:len(refs) - k_sem]
    return own, (refs[n_in:a], refs[b:c], refs[len(refs) - k_sem:])


class AllGatherWeights:
    piece_rows = (128, 64)

    def __init__(self, wi, wo):
        self.inputs = (wi, wo)
        self.out_shape = [jax.ShapeDtypeStruct((N_DEV,) + wi.shape, wi.dtype), jax.ShapeDtypeStruct((N_DEV,) + wo.shape, wo.dtype)]
        self.scratch = [pltpu.SemaphoreType.DMA((2, 7)), pltpu.SemaphoreType.DMA((2, 7)), pltpu.SemaphoreType.DMA((2,))]

    def _plan(self, ins, outs, sems):
        send_sems, recv_sems, local_sems = sems
        x, y, c = _mesh_pos()
        me, sibling = (x, y, c), (x, y, 1 - c)
        chips = [(1 - x, y), (x, 1 - y), (1 - x, 1 - y)]
        both = range(2)

        def copy(a, k, block, to, own=False):
            slot = outs[a].at[_dev_index(*block)]
            return _Transfer(ins[a] if own else slot, slot, self.piece_rows[a], send_sems.at[a, k], recv_sems.at[a, k], to)

        mine = [pltpu.make_async_copy(ins[a], outs[a].at[_dev_index(*me)], local_sems.at[a]) for a in both]
        first = [copy(a, 1 + j, me, (*chip, c), own=True) for j, chip in enumerate(chips) for a in both]
        first += [copy(a, 0, me, sibling, own=True) for a in both]
        passed = [copy(a, 4 + j, (*chip, c), sibling) for j, chip in enumerate(chips) for a in both]
        return me, sibling, chips, c, copy, mine, first, passed

    def start(self, ins, outs, sems):
        *_, mine, first, _ = self._plan(ins, outs, sems)
        for cp in mine + first:
            cp.start()

    def finish(self, ins, outs, sems):
        me, sibling, chips, c, copy, mine, first, passed = self._plan(ins, outs, sems)
        for j, chip in enumerate(chips):
            for a in range(2):
                copy(a, 1 + j, (*chip, c), me).wait_recv()
            for a in range(2):
                passed[2 * j + a].start()
        for a in range(2):
            copy(a, 0, sibling, me).wait_recv()
        for j, chip in enumerate(chips):
            for a in range(2):
                copy(a, 4 + j, (*chip, 1 - c), me).wait_recv()
        for cp in first + passed:
            cp.wait_send()
        for cp in mine:
            cp.wait()


N_CHIP = 4


class PairExchange:
    def __init__(self, by_core, whole=()):
        self.inputs = tuple(by_core) + tuple(whole)
        self.n_by_core = len(by_core)
        self.out_shape = ([jax.ShapeDtypeStruct(a.shape[1:], a.dtype) for a in by_core]
                          + [jax.ShapeDtypeStruct(a.shape, a.dtype) for a in whole])
        n = len(self.inputs)
        self.scratch = [pltpu.SemaphoreType.DMA((n,)), pltpu.SemaphoreType.DMA((n,))]

    def _copies(self, ins, outs, sems):
        x, y, c = _mesh_pos()
        srcs = [r.at[1 - c] if a < self.n_by_core else r for a, r in enumerate(ins)]
        return [_Transfer(srcs[a], outs[a], outs[a].shape[-2], sems[0].at[a], sems[1].at[a], (x, y, 1 - c))
                for a in range(len(ins))]

    def start(self, ins, outs, sems):
        for cp in self._copies(ins, outs, sems):
            cp.start()

    def finish(self, ins, outs, sems):
        copies = self._copies(ins, outs, sems)
        for cp in copies:
            cp.wait_recv()
        for cp in copies:
            cp.wait_send()


def pair_sum(own, other, dtype, rows, name, core, layer, depth, stacked=None):
    n, n_r, n_c = other.shape

    def body(core_ref, a_ref, b_ref, *refs):
        refs[-1][0, 0] = (a_ref[0, 0] + b_ref[0]).astype(dtype)

    carried = () if stacked is None else (stacked,)
    grid_spec = pltpu.PrefetchScalarGridSpec(
        num_scalar_prefetch=1,
        grid=(n, n_r // rows),
        in_specs=[pl.BlockSpec((1, 1, rows, n_c), lambda i, r, s: (s[0], i, r, 0)),
                  pl.BlockSpec((1, rows, n_c), lambda i, r, s: (i, r, 0))] + [_ANY] * len(carried),
        out_specs=pl.BlockSpec((1, 1, rows, n_c), lambda i, r, s: (i, layer, r, 0)),
    )
    return pl.pallas_call(
        body,
        name=name,
        grid_spec=grid_spec,
        out_shape=jax.ShapeDtypeStruct((n, depth, n_r, n_c), dtype),
        input_output_aliases={3: 0} if carried else {},
        compiler_params=_cparams("parallel", "parallel"),
    )(core, own, other, *carried)


def small_sum(a, b, name):
    def body(a_ref, b_ref, o_ref):
        o_ref[...] = a_ref[...] + b_ref[...]

    return pl.pallas_call(body, name=name, out_shape=jax.ShapeDtypeStruct(a.shape, a.dtype))(a, b)


class ChipExchange:
    def __init__(self, by_chip=(), layers=(), gathered=(), stacked=()):
        stacked = tuple(stacked) or (None,) * len(by_chip)
        kept = [s for s in stacked if s is not None]
        self.inputs = tuple(by_chip) + tuple(gathered) + tuple(kept)
        self.n_by_chip, self.n_gathered = len(by_chip), len(gathered)
        self.items = [(a, l) for a in range(len(by_chip)) for l in layers[a]] + [(self.n_by_chip + g, None) for g in range(len(gathered))]
        self.out_shape = ([jax.ShapeDtypeStruct((N_CHIP - 1,) + a.shape[1:], a.dtype) for a in by_chip]
                          + [jax.ShapeDtypeStruct((N_CHIP,) + a.shape, a.dtype) for a in gathered])
        at = iter(range(self.n_by_chip + self.n_gathered, len(self.inputs)))
        self.aliases = {next(at): a for a, s in enumerate(stacked) if s is not None}
        n = len(self.items)
        self.scratch = [pltpu.SemaphoreType.DMA((n, 3)), pltpu.SemaphoreType.DMA((n, 3)),
                        pltpu.SemaphoreType.DMA((max(self.n_gathered, 1),))]

    def _plan(self, ins, outs, sems):
        x, y, c = _mesh_pos()
        chip = 2 * x + y
        n = len(self.items)

        def copy(i, k, sending):
            a, layer = self.items[i]
            px, py = x ^ ((k >> 1) & 1), y ^ (k & 1)
            if layer is not None:
                src, dst = ins[a].at[2 * px + py, layer], outs[a].at[k - 1, layer]
            else:
                src, dst = ins[a], outs[a].at[chip if sending else 2 * px + py]
            return _Transfer(src, dst, dst.shape[-2], sems[0].at[i, k - 1], sems[1].at[i, k - 1], (px, py, c))

        local = [pltpu.make_async_copy(ins[a], outs[a].at[chip], sems[2].at[a - self.n_by_chip])
                 for a in range(self.n_by_chip, self.n_by_chip + self.n_gathered)]
        return n, copy, local

    def start(self, ins, outs, sems):
        n, copy, local = self._plan(ins, outs, sems)
        for cp in local:
            cp.start()
        for k in range(1, N_CHIP):
            for a in range(n):
                copy(a, k, True).start()

    def finish(self, ins, outs, sems):
        n, copy, local = self._plan(ins, outs, sems)
        for k in range(1, N_CHIP):
            for a in range(n):
                copy(a, k, False).wait_recv()
        for k in range(1, N_CHIP):
            for a in range(n):
                copy(a, k, True).wait_send()
        for cp in local:
            cp.wait()


ADAM_LR = 0.001
ADAM_B1 = 0.9
ADAM_B2 = 0.999
ADAM_EPS = 1e-08
ADAM_WD = 0.01
ADAM_STEP = 10


def adam_reduce(parts, w, m, v, rows, name, own=None, chip=None):
    n_l, n_r, n_c = w.shape
    n_parts = parts.shape[0]

    def body(*refs):
        p_ref, w_ref, m_ref, v_ref, g_ref, d_ref, m2_ref, v2_ref = refs[-8:]
        g = p_ref[0, 0].astype(F32)
        if own is not None:
            g = refs[-9][...].reshape(rows, n_c).astype(F32) + g
        for d in range(1, n_parts):
            g = g + p_ref[d, 0].astype(F32)
        m2 = ADAM_B1 * m_ref[0] + (1.0 - ADAM_B1) * g
        v2 = ADAM_B2 * v_ref[0] + (1.0 - ADAM_B2) * (g * g)
        m_hat = m2 / (1.0 - ADAM_B1 ** ADAM_STEP)
        v_hat = v2 / (1.0 - ADAM_B2 ** ADAM_STEP)
        g_ref[0] = g
        d_ref[0] = -ADAM_LR * (m_hat / (jnp.sqrt(v_hat) + ADAM_EPS) + ADAM_WD * w_ref[0])
        m2_ref[0] = m2
        v2_ref[0] = v2

    blk = lambda: pl.BlockSpec((1, rows, n_c), lambda l, r, *_: (l, r, 0))
    in_specs = [pl.BlockSpec((n_parts, 1, rows, n_c), lambda l, r, *_: (0, l, r, 0)), blk(), blk(), blk()]
    args = (parts, w, m, v)
    if own is not None:
        in_specs = [pl.BlockSpec((1, 1, rows, n_c), lambda l, r, s: (s[0], l, r, 0))] + in_specs
        args = (chip, own) + args
    grid_spec = pltpu.PrefetchScalarGridSpec(
        num_scalar_prefetch=0 if own is None else 1, grid=(n_l, n_r // rows), in_specs=in_specs,
        out_specs=[blk(), blk(), blk(), blk()])
    return pl.pallas_call(
        body,
        name=name,
        grid_spec=grid_spec,
        out_shape=[jax.ShapeDtypeStruct(w.shape, F32)] * 4,
        compiler_params=_cparams("parallel", "parallel"),
    )(*args)


def adam_reduce_columns(parts, w, m, v, name, own, chip):
    n_l, n_r, n_c = w.shape
    n_parts = parts.shape[0]
    view = lambda a: jnp.transpose(a, (2, 0, 1))

    def body(_, own_ref, p_ref, w_ref, m_ref, v_ref, g_ref, d_ref, m2_ref, v2_ref):
        for l in range(n_l):
            g = own_ref[0, l].astype(F32) + p_ref[0, l].astype(F32)
            for d in range(1, n_parts):
                g = g + p_ref[d, l].astype(F32)
            g = g.T
            w_l, m_l, v_l = w_ref[:, l, :], m_ref[:, l, :], v_ref[:, l, :]
            m2 = ADAM_B1 * m_l + (1.0 - ADAM_B1) * g
            v2 = ADAM_B2 * v_l + (1.0 - ADAM_B2) * (g * g)
            m_hat = m2 / (1.0 - ADAM_B1 ** ADAM_STEP)
            v_hat = v2 / (1.0 - ADAM_B2 ** ADAM_STEP)
            g_ref[:, l, :] = g
            d_ref[:, l, :] = -ADAM_LR * (m_hat / (jnp.sqrt(v_hat) + ADAM_EPS) + ADAM_WD * w_l)
            m2_ref[:, l, :] = m2
            v2_ref[:, l, :] = v2

    blk = lambda: pl.BlockSpec((LANES, n_l, n_r), lambda c, s: (c, 0, 0))
    grid_spec = pltpu.PrefetchScalarGridSpec(
        num_scalar_prefetch=1, grid=(pl.cdiv(n_c, LANES),),
        in_specs=[pl.BlockSpec((1, n_l, n_r, LANES), lambda c, s: (s[0], 0, 0, c)),
                  pl.BlockSpec((n_parts, n_l, n_r, LANES), lambda c, s: (0, 0, 0, c)), blk(), blk(), blk()],
        out_specs=[blk(), blk(), blk(), blk()])
    outs = pl.pallas_call(
        body,
        name=name,
        grid_spec=grid_spec,
        out_shape=[jax.ShapeDtypeStruct((n_c, n_l, n_r), F32)] * 4,
        compiler_params=_cparams("parallel"),
    )(chip, own, parts, view(w), view(m), view(v))
    return [jnp.transpose(o, (1, 2, 0)) for o in outs]


_SMALL = (("norm_g", (2, 1024)), ("gmlp_ln_g", (2, 4, 64)), ("gmlp_ln_b", (2, 4, 64)),
          ("gmlp_b_s", (2, 4, 128)), ("hgrn_lb", (2, 256)), ("hgrn_onorm_g", (2, 64)), ("fox_b_f", (2, 8)),
          ("final_norm_g", (1024,)), ("loss", ()))


def _padded(n):
    return -(-n // LANES) * LANES


_SMALL_ROWS = -(-sum(_padded(int(np.prod(s))) for _, s in _SMALL) // LANES // 8) * 8


def _pack_small(vals):
    flat = []
    for (name, shape), a in zip(_SMALL, vals, strict=True):
        n = int(np.prod(shape))
        flat.append(jnp.pad(a.reshape(n).astype(F32), (0, _padded(n) - n)))
    flat = jnp.concatenate(flat)
    return jnp.pad(flat, (0, _SMALL_ROWS * LANES - flat.shape[0])).reshape(_SMALL_ROWS, LANES)


def _unpack_small(slab):
    flat, out, at = slab.reshape(-1), {}, 0
    for name, shape in _SMALL:
        n = int(np.prod(shape))
        out[name] = flat[at:at + n].reshape(shape)
        at += _padded(n)
    return out


def kernel(x, norm_g, w_in, w_out, gmlp_ln_g, gmlp_ln_b, gmlp_w_s, gmlp_b_s, hgrn_lb, hgrn_onorm_g, fox_b_f, final_norm_g, loss_target, m_norm_g, m_w_in, m_w_out, m_gmlp_ln_g, m_gmlp_ln_b, m_gmlp_w_s, m_gmlp_b_s, m_hgrn_lb, m_hgrn_onorm_g, m_fox_b_f, m_final_norm_g, v_norm_g, v_w_in, v_w_out, v_gmlp_ln_g, v_gmlp_ln_b, v_gmlp_w_s, v_gmlp_b_s, v_hgrn_lb, v_hgrn_onorm_g, v_fox_b_f, v_final_norm_g):
    depth = w_in.shape[0]
    seq = x.shape[1]
    assert w_in.shape[2] * N_DEV == N_IN
    xs, tgt = x[0], loss_target[0]

    wi_blk, wo_blk = w_in.astype(BF16), w_out.astype(BF16)
    wi_all, wo_all = _exchange_call(AllGatherWeights(wi_blk[0], wo_blk[0]), "allgather_weights_0")

    ln_g = gmlp_ln_g.reshape(depth, 1, A_WIDTH)
    ln_b = gmlp_ln_b.reshape(depth, 1, A_WIDTH)
    bs_t = jnp.pad(jnp.transpose(gmlp_b_s, (0, 2, 1)), ((0, 0), (0, 0), (0, LANES - A_GROUPS)))
    lb0, lb1 = hgrn_lb[0:1], hgrn_lb[1:2]
    onorm = jnp.tile(hgrn_onorm_g, (1, B_HEADS)).reshape(depth, 1, B_WIDTH)
    bf_row = jnp.pad(fox_b_f, ((0, 0), (0, LANES - C_HEADS))).reshape(depth, 1, LANES)

    core = lax.axis_index("c").astype(jnp.int32).reshape(1)
    chip = (2 * lax.axis_index("x") + lax.axis_index("y")).astype(jnp.int32).reshape(1)

    saved = []
    xc = xs
    for l in range(depth):
        wi_int = assemble_w_in(wi_all[:, None])
        proj, h = inproj(xc, norm_g[l:l + 1], wi_int, 0)
        ya = gmlp_fwd(proj, ln_g[l], ln_b[l], gmlp_w_s[l], bs_t[l])
        yb, states = hgrn_fwd(proj, lb0, lb1, onorm[l], l)
        ka, va, vt, kt, qt, qa = fox_prep(proj, bf_row[l])
        nxt = AllGatherWeights(wi_blk[l + 1], wo_blk[l + 1]) if l + 1 < depth else None
        o, lse, *gathered = fox_fwd(qt, ka, vt, nxt)
        xn, yfull = outproj(xc, ya, yb, o, proj, wo_all[:, None], 0)
        saved.append((xc, proj, h, states, ka, va, kt, qt, qa, o, lse, yfull, wi_int, wo_all))
        if gathered:
            wi_all, wo_all = gathered
        xc = xn

    dx, d_final_g, loss_tile = final_loss(xc, final_norm_g[None], tgt)

    n_shard = w_in.shape[2]
    g_norm = [None] * depth
    g_ln_g, g_ln_b, g_ws, g_bs, g_on, g_bf = ([None] * depth for _ in range(6))
    g_lb0, g_lb1 = jnp.zeros_like(lb0), jnp.zeros_like(lb1)
    swi = swo = rwi = rwo = None
    for l in reversed(range(depth)):
        x_in, proj, h, states, ka, va, kt, qt, qa, o, lse, yfull, wi_int, wo_l = saved[l]
        dy, gwo = outproj_bwd(dx, yfull, wo_l[:, None], 0)
        gwo = gwo[:, :, 0]
        dproj, g_ln_g[l], g_ln_b[l], g_ws[l], dbs_t = gmlp_bwd(proj, dy, ln_g[l], ln_b[l], gmlp_w_s[l], bs_t[l])
        g_bs[l] = dbs_t[:, :A_GROUPS].T
        if l > 0:
            (qwo,) = _exchange_call(PairExchange([gwo]), f"pair_exchange_w_out_{l}")
        else:
            gws = jnp.stack(g_ws).reshape(-1, LANES)
            qwo, qws = _exchange_call(PairExchange([gwo], [gws]), f"pair_exchange_w_out_{l}")
            sws = small_sum(gws, qws, "pair_sum_w_s")
        swo = pair_sum(gwo, qwo, BF16, gwo.shape[2], "pair_sum_w_out", core, l, depth, swo)
        dproj, d0, d1, don = hgrn_bwd(proj, states, dy, lb0, lb1, onorm[l], l, dproj)
        g_lb0, g_lb1 = g_lb0 + d0, g_lb1 + d1
        g_on[l] = don.reshape(B_HEADS, B_KDIM).sum(0)
        dob, dproj, dot_t = fox_bwd_prep(dy, o, proj, dproj)
        top = l == depth - 1
        ride = ChipExchange([swo] if top else [swi, swo], [(l,)] if top else [(l + 1,), (l,)],
                            [sws] if l == 0 else [], [rwo] if top else [rwi, rwo])
        outs = fox_bwd(ka, va, kt, qt, dot_t, qa, dob, lse, ride)
        dqkv, (dck, dcq), got = outs[:3], outs[3:5], list(outs[5:])
        if not top:
            rwi = got.pop(0)
        rwo = got.pop(0)
        if l == 0:
            (rws,) = got
        dproj, dbf = fox_post(dcq, dck, proj, bf_row[l], dproj)
        g_bf[l] = dbf[0, :C_HEADS]
        gwi = split_w_in_grad(inproj_bwd_w(h, dproj, dqkv), n_shard)[:, :, 0]
        (qwi,) = _exchange_call(PairExchange([gwi]), f"pair_exchange_w_in_{l}")
        swi = pair_sum(gwi, qwi, BF16, 256, "pair_sum_w_in", core, l, depth, swi)
        ride = ChipExchange([swi], [(l,)], stacked=[rwi]) if l == 0 else None
        outs = inproj_bwd_x(dproj, dqkv, wi_int, x_in, norm_g[l:l + 1], dx, 0, ride)
        dx, g_norm[l] = outs[:2]
        if ride is not None:
            (rwi,) = outs[2:]

    gsm = _pack_small([
        jnp.concatenate(g_norm), jnp.stack(g_ln_g), jnp.stack(g_ln_b), jnp.stack(g_bs),
        jnp.concatenate([g_lb0, g_lb1]), jnp.stack(g_on), jnp.stack(g_bf), d_final_g, loss_tile[0, 0]])
    (qsm,) = _exchange_call(PairExchange([], [gsm]), "pair_exchange_small")
    ssm = small_sum(gsm, qsm, "pair_sum_small")
    (rsm,) = _exchange_call(ChipExchange(gathered=[ssm]), "chip_exchange_small")

    small_w = (norm_g, gmlp_ln_g, gmlp_ln_b, gmlp_b_s, hgrn_lb, hgrn_onorm_g, fox_b_f, final_norm_g)
    small_m = (m_norm_g, m_gmlp_ln_g, m_gmlp_ln_b, m_gmlp_b_s, m_hgrn_lb, m_hgrn_onorm_g, m_fox_b_f, m_final_norm_g)
    small_v = (v_norm_g, v_gmlp_ln_g, v_gmlp_ln_b, v_gmlp_b_s, v_hgrn_lb, v_hgrn_onorm_g, v_fox_b_f, v_final_norm_g)
    zero = jnp.zeros((), F32)
    res_wi = adam_reduce_columns(rwi, w_in, m_w_in, v_w_in, "adam_w_in", swi, chip)
    res_wo = adam_reduce(rwo, w_out, m_w_out, v_w_out, w_out.shape[1], "adam_w_out", own=swo, chip=chip)
    res_sm = adam_reduce(rsm[:, None], _pack_small(small_w + (zero,))[None], _pack_small(small_m + (zero,))[None],
                         _pack_small(small_v + (zero,))[None], _SMALL_ROWS, "adam_small")
    res_sm = [_unpack_small(r[0]) for r in res_sm]
    as_rows = lambda a: a.reshape(1, -1, LANES)
    res_ws = adam_reduce(rws[:, None], as_rows(gmlp_w_s), as_rows(m_gmlp_w_s), as_rows(v_gmlp_w_s), rws.shape[1], "adam_w_s")
    for s, r in zip(res_sm, res_ws, strict=True):
        s["gmlp_w_s"] = r.reshape(gmlp_w_s.shape)

    def group(i):
        s = res_sm[i]
        return [s["norm_g"], res_wi[i], res_wo[i], s["gmlp_ln_g"], s["gmlp_ln_b"], s["gmlp_w_s"], s["gmlp_b_s"],
                s["hgrn_lb"], s["hgrn_onorm_g"], s["fox_b_f"], s["final_norm_g"]]

    return (res_sm[0]["loss"], dx[None], *group(0), *group(1), *group(2), *group(3))
```

```python
import functools

import jax
import jax.numpy as jnp
import numpy as np
from jax import lax
from jax.experimental import pallas as pl
from jax.experimental.pallas import tpu as pltpu

F32 = jnp.float32
BF16 = jnp.bfloat16

NORM_EPS = 1e-6
F_FLOOR = 1e-30
CHUNK = 128
LANES = 128
VMEM_LIMIT = 56 * 1024 * 1024


def _cparams(*sem):
    return pltpu.CompilerParams(dimension_semantics=sem, vmem_limit_bytes=VMEM_LIMIT)


def _dot(a, b, dims=(((1,), (0,)), ((), ())), precision=None):
    return lax.dot_general(a, b, dims, precision=precision, preferred_element_type=F32)


_NT = (((1,), (1,)), ((), ()))
_TN = (((0,), (0,)), ((), ()))


def _bf16_pieces(x, n):
    out, r = [], x
    for i in range(n):
        out.append(r.astype(BF16))
        if i + 1 < n:
            r = r - out[-1].astype(F32)
    return out


@functools.partial(jax.custom_vjp, nondiff_argnums=(2,))
def _times_exact(x, e, n):
    return functools.reduce(jnp.add, [_dot(p, e) for p in _bf16_pieces(x, n)])


def _times_exact_fwd(x, e, n):
    return _times_exact(x, e, n), e


def _times_exact_bwd(n, e, g):
    dx = functools.reduce(jnp.add, [lax.dot_general(p, e, _NT, preferred_element_type=F32) for p in _bf16_pieces(g, n)])
    return dx, jnp.zeros_like(e)


_times_exact.defvjp(_times_exact_fwd, _times_exact_bwd)


@functools.partial(jax.custom_vjp, nondiff_argnums=(2,))
def _exact_times(e, x, n):
    return functools.reduce(jnp.add, [_dot(e, p) for p in _bf16_pieces(x, n)])


def _exact_times_fwd(e, x, n):
    return _exact_times(e, x, n), e


def _exact_times_bwd(n, e, g):
    dx = functools.reduce(jnp.add, [lax.dot_general(e, p, _TN, preferred_element_type=F32) for p in _bf16_pieces(g, n)])
    return jnp.zeros_like(e), dx


_exact_times.defvjp(_exact_times_fwd, _exact_times_bwd)


def _group_mean_matrix(width, group):
    idx = np.arange(width) // group
    return jnp.asarray((idx[:, None] == idx[None, :]).astype(np.float32) / group, BF16)


def _group_ones_matrix(width, group):
    idx = np.arange(width) // group
    return jnp.asarray((idx[:, None] == idx[None, :]).astype(np.float32), BF16)


A_WIDTH = 256
A_GROUPS = 4
A_GDIM = 64


A_ROWS = 512


def _gmlp_chunk(x3, ln_g, ln_b, w_s, bs_t, mean_m, gind):
    n = x3.shape[0] // CHUNK
    u = jax.nn.gelu(x3[:, :A_WIDTH])
    v = jax.nn.gelu(x3[:, A_WIDTH:2 * A_WIDTH])
    z = x3[:, 2 * A_WIDTH:]
    mu = _times_exact(v, mean_m, 2)
    d = v - mu
    var = _times_exact(d * d, mean_m, 2)
    vn = d * lax.rsqrt(var + NORM_EPS) * ln_g + ln_b
    vnb = vn.astype(BF16)
    wide = jnp.concatenate([vnb[i * CHUNK:(i + 1) * CHUNK] for i in range(n)], axis=1)
    row = lax.broadcasted_iota(jnp.int32, (CHUNK, CHUNK), 0)
    col = lax.broadcasted_iota(jnp.int32, (CHUNK, CHUNK), 1)
    causal = row >= col
    lane_g = lax.shift_right_logical(lax.broadcasted_iota(jnp.int32, (CHUNK, n * A_WIDTH), 1), 6) & (A_GROUPS - 1)
    bias = _times_exact(bs_t, gind, 3)
    mixed = jnp.concatenate([bias] * n, axis=1)
    for g in range(A_GROUPS):
        wc = jnp.where(causal, w_s[g], 0.0).astype(BF16)
        mixed = mixed + jnp.where(lane_g == g, _dot(wc, wide), 0.0)
    mixed = jnp.concatenate([mixed[:, i * A_WIDTH:(i + 1) * A_WIDTH] for i in range(n)], axis=0)
    return u * mixed * jax.nn.silu(z)


def _gmlp_consts():
    gind = np.zeros((LANES, A_WIDTH), np.float32)
    for g in range(A_GROUPS):
        gind[g, g * A_GDIM:(g + 1) * A_GDIM] = 1.0
    return _group_mean_matrix(A_WIDTH, A_GDIM), jnp.asarray(gind, BF16)


def _full(shape):
    return pl.BlockSpec(shape, lambda *_: (0,) * len(shape))


def gmlp_fwd(proj, ln_g, ln_b, w_s, bs_t):
    seq = proj.shape[0]
    rows = min(A_ROWS, seq)
    mean_m, gind = _gmlp_consts()

    def body(x_ref, g_ref, b_ref, w_ref, bs_ref, m_ref, gi_ref, y_ref):
        y = _gmlp_chunk(x_ref[...], g_ref[...], b_ref[...], w_ref[...], bs_ref[...], m_ref[...], gi_ref[...])
        y_ref[...] = y.astype(BF16)

    return pl.pallas_call(
        body,
        name="gmlp_fwd",
        grid=(seq // rows,),
        in_specs=[
            pl.BlockSpec((rows, 3 * A_WIDTH), lambda n: (n, 0)),
            _full((1, A_WIDTH)), _full((1, A_WIDTH)), _full((A_GROUPS, CHUNK, CHUNK)), _full((CHUNK, LANES)),
            _full((A_WIDTH, A_WIDTH)), _full((LANES, A_WIDTH)),
        ],
        out_specs=pl.BlockSpec((rows, A_WIDTH), lambda n: (n, 0)),
        out_shape=jax.ShapeDtypeStruct((seq, A_WIDTH), BF16),
        compiler_params=_cparams("parallel"),
    )(proj, ln_g, ln_b, w_s, bs_t, mean_m, gind)


def gmlp_bwd(proj, dy, ln_g, ln_b, w_s, bs_t):
    seq = proj.shape[0]
    rows = min(A_ROWS, seq)
    mean_m, gind = _gmlp_consts()

    def body(x_ref, dy_ref, g_ref, b_ref, w_ref, bs_ref, m_ref, gi_ref, dx_ref, dg_ref, db_ref, dw_ref, dbs_ref):
        fn = functools.partial(_gmlp_chunk, mean_m=m_ref[...], gind=gi_ref[...])
        _, vjp = jax.vjp(fn, x_ref[...], g_ref[...], b_ref[...], w_ref[...], bs_ref[...])
        dx, dg, db, dw, dbs = vjp(dy_ref[...])
        dx_ref[...] = dx.astype(BF16)

        @pl.when(pl.program_id(0) == 0)
        def _():
            dg_ref[...] = jnp.zeros_like(dg_ref)
            db_ref[...] = jnp.zeros_like(db_ref)
            dw_ref[...] = jnp.zeros_like(dw_ref)
            dbs_ref[...] = jnp.zeros_like(dbs_ref)

        dg_ref[...] += dg
        db_ref[...] += db
        dw_ref[...] += dw
        dbs_ref[...] += dbs

    return pl.pallas_call(
        body,
        name="gmlp_bwd",
        grid=(seq // rows,),
        in_specs=[
            pl.BlockSpec((rows, 3 * A_WIDTH), lambda n: (n, 0)),
            pl.BlockSpec((rows, A_WIDTH), lambda n: (n, 0)),
            _full((1, A_WIDTH)), _full((1, A_WIDTH)), _full((A_GROUPS, CHUNK, CHUNK)), _full((CHUNK, LANES)),
            _full((A_WIDTH, A_WIDTH)), _full((LANES, A_WIDTH)),
        ],
        out_specs=[
            pl.BlockSpec((rows, 3 * A_WIDTH), lambda n: (n, 0)),
            _full((1, A_WIDTH)), _full((1, A_WIDTH)), _full((A_GROUPS, CHUNK, CHUNK)), _full((CHUNK, LANES)),
        ],
        out_shape=[
            jax.ShapeDtypeStruct((seq, D_INT), BF16),
            jax.ShapeDtypeStruct((1, A_WIDTH), F32), jax.ShapeDtypeStruct((1, A_WIDTH), F32),
            jax.ShapeDtypeStruct((A_GROUPS, CHUNK, CHUNK), F32), jax.ShapeDtypeStruct((CHUNK, LANES), F32),
        ],
        compiler_params=_cparams("arbitrary"),
    )(proj, dy, ln_g, ln_b, w_s, bs_t, mean_m, gind)


B_WIDTH = 256
B_HEADS = 4
B_KDIM = 64
B_LEVELS = (64, 32, 16, 8, 4, 2, 1)


def _hgrn_consts():
    t = np.arange(CHUNK)
    u = t[None, :]
    mats = [np.tril(np.ones((CHUNK, CHUNK), np.float32))]
    for m in B_LEVELS:
        p = (t // (2 * m)) * (2 * m) + m - 1
        right = (t % (2 * m)) >= m
        sel = np.where(right[:, None], (u > p[:, None]) & (u <= t[:, None]), (u > t[:, None]) & (u <= p[:, None]))
        mats.append(sel.astype(np.float32))
    return jnp.asarray(np.concatenate(mats, 0), BF16), _group_ones_matrix(B_WIDTH, B_KDIM)


def _hgrn_lower_bound(lb0, lb1, layer):
    mx = jnp.maximum(lb0, lb1)
    e0 = jnp.exp(lb0 - mx)
    e1 = jnp.exp(lb1 - mx)
    p0 = e0 / (e0 + e1)
    p1 = e1 / (e0 + e1)
    cs = p0 if layer == 0 else p0 + p1
    return jnp.clip(cs - p0, 0.0, 1.0 - 1e-6)


def _hgrn_chunk(x4, st, lb0, lb1, onorm, layer, tstack, ones_bd):
    q_raw, fl, v, zg = (x4[:, i * B_WIDTH:(i + 1) * B_WIDTH] for i in range(4))
    lb = _hgrn_lower_bound(lb0, lb1, layer)
    q = jax.nn.silu(q_raw) * (B_KDIM ** -0.5)
    f = lb + (1.0 - lb) * jax.nn.sigmoid(fl)
    logf = jnp.log(jnp.maximum(f, F_FLOOR))
    k = (1.0 - lb) * jax.nn.sigmoid(-fl)
    b = _exact_times(tstack[:CHUNK], logf, 3)
    dall = jnp.concatenate([b, _exact_times(tstack[CHUNK:], logf, 2)], axis=0)
    b_last = jnp.sum(logf, axis=0, keepdims=True)
    vb = v.astype(BF16)

    lane_h = lax.shift_right_logical(lax.broadcasted_iota(jnp.int32, (CHUNK, B_WIDTH), 1), 6)
    row = lax.broadcasted_iota(jnp.int32, (CHUNK, B_WIDTH), 0)
    srow = lax.broadcasted_iota(jnp.int32, (B_HEADS * CHUNK, CHUNK), 0) & (CHUNK - 1)
    scol = lax.broadcasted_iota(jnp.int32, (B_HEADS * CHUNK, CHUNK), 1)

    def heads_on_rows(a):
        return jnp.concatenate([jnp.where(lane_h == h, a, 0.0) for h in range(B_HEADS)], axis=0)

    def heads_from_rows(r):
        out = jnp.where(lane_h == 0, r[:CHUNK], 0.0)
        for h in range(1, B_HEADS):
            out = out + jnp.where(lane_h == h, r[h * CHUNK:(h + 1) * CHUNK], 0.0)
        return out

    o = lax.dot_general((q * jnp.exp(b)).astype(BF16), st.astype(BF16), _NT, preferred_element_type=F32)
    scores = jnp.zeros((B_HEADS * CHUNK, CHUNK), F32)
    for li, m in enumerate(B_LEVELS):
        e = jnp.exp(dall[(li + 1) * CHUNK:(li + 2) * CHUNK])
        right = (row & (2 * m - 1)) >= m
        qt = jnp.where(right, q * e, 0.0)
        kt = jnp.where(right, 0.0, k * e)
        sc = lax.dot_general(heads_on_rows(qt).astype(BF16), kt.astype(BF16), _NT, preferred_element_type=F32)
        sh = int(np.log2(2 * m))
        same = lax.shift_right_logical(srow, sh) == lax.shift_right_logical(scol, sh)
        scores = scores + jnp.where(same, sc, 0.0)
    o = o + heads_from_rows(_dot(scores.astype(BF16), vb))
    o = o + _times_exact(q * k, ones_bd, 2) * v

    kv = lax.dot_general(vb, (k * jnp.exp(b_last - b)).astype(BF16), _TN, preferred_element_type=F32)
    st_new = st * jnp.exp(b_last) + jnp.where(ones_bd > 0.5, kv, 0.0)

    ms = _times_exact(o * o, ones_bd, 2) * (1.0 / B_KDIM)
    y = o * lax.rsqrt(ms + NORM_EPS) * onorm * jax.nn.silu(zg)
    return y, st_new


B_ROWS = 256


def _hgrn_rows(x4, st, lb0, lb1, onorm, layer, tstack, ones_bd):
    ys = []
    for i in range(x4.shape[0] // CHUNK):
        y, st = _hgrn_chunk(x4[i * CHUNK:(i + 1) * CHUNK], st, lb0, lb1, onorm, layer, tstack, ones_bd)
        ys.append(y)
    return jnp.concatenate(ys, axis=0), st


def hgrn_fwd(proj, lb0, lb1, onorm, layer):
    seq = proj.shape[0]
    rows = min(B_ROWS, seq)
    nc = seq // rows
    tstack, ones_bd = _hgrn_consts()

    def body(x_ref, lb0_ref, lb1_ref, on_ref, t_ref, e_ref, y_ref, st_out_ref, st_ref):
        @pl.when(pl.program_id(0) == 0)
        def _():
            st_ref[...] = jnp.zeros_like(st_ref)

        st = st_ref[...]
        st_out_ref[0] = st
        y, st_new = _hgrn_rows(x_ref[...], st, lb0_ref[...], lb1_ref[...], on_ref[...], layer, t_ref[...], e_ref[...])
        y_ref[...] = y.astype(BF16)
        st_ref[...] = st_new

    return pl.pallas_call(
        body,
        name=f"hgrn_fwd_{layer}",
        grid=(nc,),
        in_specs=[
            pl.BlockSpec((rows, 4 * B_WIDTH), lambda n: (n, 1)),
            _full((1, B_WIDTH)), _full((1, B_WIDTH)), _full((1, B_WIDTH)),
            _full(((len(B_LEVELS) + 1) * CHUNK, CHUNK)), _full((B_WIDTH, B_WIDTH)),
        ],
        out_specs=[
            pl.BlockSpec((rows, B_WIDTH), lambda n: (n, 0)),
            pl.BlockSpec((1, B_WIDTH, B_WIDTH), lambda n: (n, 0, 0)),
        ],
        out_shape=[jax.ShapeDtypeStruct((seq, B_WIDTH), BF16), jax.ShapeDtypeStruct((nc, B_WIDTH, B_WIDTH), F32)],
        scratch_shapes=[pltpu.VMEM((B_WIDTH, B_WIDTH), F32)],
        compiler_params=_cparams("arbitrary"),
    )(proj, lb0, lb1, onorm, tstack, ones_bd)


def hgrn_bwd(proj, states, dy, lb0, lb1, onorm, layer, dproj):
    seq = proj.shape[0]
    rows = min(B_ROWS, seq)
    nc = seq // rows
    tstack, ones_bd = _hgrn_consts()

    def body(x_ref, st_in_ref, dy_ref, lb0_ref, lb1_ref, on_ref, t_ref, e_ref, _, dx_ref, d0_ref, d1_ref, don_ref, dst_ref):
        @pl.when(pl.program_id(0) == 0)
        def _():
            dst_ref[...] = jnp.zeros_like(dst_ref)
            d0_ref[...] = jnp.zeros_like(d0_ref)
            d1_ref[...] = jnp.zeros_like(d1_ref)
            don_ref[...] = jnp.zeros_like(don_ref)

        fn = functools.partial(_hgrn_rows, layer=layer, tstack=t_ref[...], ones_bd=e_ref[...])
        _, vjp = jax.vjp(fn, x_ref[...], st_in_ref[0], lb0_ref[...], lb1_ref[...], on_ref[...])
        dx, dst, d0, d1, don = vjp((dy_ref[...], dst_ref[...]))
        dx_ref[...] = dx.astype(BF16)
        dst_ref[...] = dst
        d0_ref[...] += d0
        d1_ref[...] += d1
        don_ref[...] += don

    rev = lambda n: nc - 1 - n
    return pl.pallas_call(
        body,
        name=f"hgrn_bwd_{layer}",
        grid=(nc,),
        in_specs=[
            pl.BlockSpec((rows, 4 * B_WIDTH), lambda n: (rev(n), 1)),
            pl.BlockSpec((1, B_WIDTH, B_WIDTH), lambda n: (rev(n), 0, 0)),
            pl.BlockSpec((rows, B_WIDTH), lambda n: (rev(n), 1)),
            _full((1, B_WIDTH)), _full((1, B_WIDTH)), _full((1, B_WIDTH)),
            _full(((len(B_LEVELS) + 1) * CHUNK, CHUNK)), _full((B_WIDTH, B_WIDTH)), _ANY,
        ],
        out_specs=[
            pl.BlockSpec((rows, 4 * B_WIDTH), lambda n: (rev(n), 1)),
            _full((1, B_WIDTH)), _full((1, B_WIDTH)), _full((1, B_WIDTH)),
        ],
        out_shape=[jax.ShapeDtypeStruct(dproj.shape, BF16)] + [jax.ShapeDtypeStruct((1, B_WIDTH), F32)] * 3,
        input_output_aliases={8: 0},
        scratch_shapes=[pltpu.VMEM((B_WIDTH, B_WIDTH), F32)],
        compiler_params=_cparams("arbitrary"),
    )(proj, states, dy, lb0, lb1, onorm, tstack, ones_bd, dproj)


D_MODEL = 1024
D_INT = 4096


def _rms_stats(xf):
    r = lax.rsqrt(jnp.mean(xf * xf, axis=-1, keepdims=True) + NORM_EPS)
    return r, xf * r


def _rms_bwd(dy, g, r, xh):
    u = dy * g
    return r * (u - xh * jnp.mean(u * xh, axis=-1, keepdims=True))


def inproj(x, g, w, layer):
    seq = x.shape[0]
    tm = min(seq, 512)

    def body(x_ref, g_ref, w_ref, p_ref, h_ref):
        _, xh = _rms_stats(x_ref[...])
        h = (xh * g_ref[...]).astype(BF16)
        h_ref[...] = h
        p_ref[...] = _dot(h, w_ref[0])

    return pl.pallas_call(
        body,
        name="inproj",
        grid=(seq // tm,),
        in_specs=[
            pl.BlockSpec((tm, D_MODEL), lambda i: (i, 0)),
            _full((1, D_MODEL)),
            pl.BlockSpec((1, D_MODEL, D_INT), lambda i: (layer, 0, 0)),
        ],
        out_specs=[pl.BlockSpec((tm, D_INT), lambda i: (i, 0)), pl.BlockSpec((tm, D_MODEL), lambda i: (i, 0))],
        out_shape=[jax.ShapeDtypeStruct((seq, D_INT), F32), jax.ShapeDtypeStruct((seq, D_MODEL), BF16)],
        compiler_params=_cparams("parallel"),
    )(x, g, w)


def outproj(x, ya, yb, o, proj, wo, layer):
    seq = x.shape[0]
    tm = min(seq, 512)
    blk = wo.shape[2]

    def body(x_ref, ya_ref, yb_ref, o_ref, z_ref, w_ref, xn_ref, y_ref):
        yc = (o_ref[...] * jax.nn.silu(z_ref[...])).astype(BF16)
        y = jnp.concatenate([ya_ref[...], yb_ref[...], yc], axis=1)
        y_ref[...] = y
        w = jnp.concatenate([w_ref[d, 0] for d in range(N_DEV)], axis=0)
        xn_ref[...] = x_ref[...] + _dot(y, w)

    return pl.pallas_call(
        body,
        name="outproj",
        grid=(seq // tm,),
        in_specs=[
            pl.BlockSpec((tm, D_MODEL), lambda i: (i, 0)),
            pl.BlockSpec((tm, 256), lambda i: (i, 0)),
            pl.BlockSpec((tm, 256), lambda i: (i, 0)),
            pl.BlockSpec((tm, 512), lambda i: (i, 0)),
            pl.BlockSpec((tm, 512), lambda i: (i, 7)),
            pl.BlockSpec((N_DEV, 1, blk, D_MODEL), lambda i: (0, layer, 0, 0)),
        ],
        out_specs=[pl.BlockSpec((tm, D_MODEL), lambda i: (i, 0)), pl.BlockSpec((tm, D_MODEL), lambda i: (i, 0))],
        out_shape=[jax.ShapeDtypeStruct((seq, D_MODEL), F32), jax.ShapeDtypeStruct((seq, D_MODEL), BF16)],
        compiler_params=_cparams("parallel"),
    )(x, ya, yb, o, proj, wo)


def outproj_bwd(dx, y, wo, layer, stacked=None):
    seq = dx.shape[0]
    ts = min(seq, 512)
    _, depth, blk, _ = wo.shape

    def body(dx_ref, y_ref, w_ref, *refs):
        dy_ref, dw_ref = refs[-2:]

        @pl.when(pl.program_id(0) == 0)
        def _():
            dw_ref[...] = jnp.zeros_like(dw_ref)

        dxb = dx_ref[...].astype(BF16)
        w = jnp.concatenate([w_ref[d, 0] for d in range(N_DEV)], axis=0)
        dy_ref[...] = lax.dot_general(dxb, w, _NT, preferred_element_type=F32)
        dw = lax.dot_general(y_ref[...], dxb, _TN, preferred_element_type=F32)
        for d in range(N_DEV):
            dw_ref[d % 2, d // 2, 0] += dw[d * blk:(d + 1) * blk]

    carried = () if stacked is None else (stacked,)
    out_shape = [jax.ShapeDtypeStruct((seq, D_MODEL), F32), jax.ShapeDtypeStruct((2, N_CHIP, depth, blk, D_MODEL), F32)]
    return pl.pallas_call(
        body,
        name="outproj_bwd",
        grid=(seq // ts,),
        in_specs=[
            pl.BlockSpec((ts, D_MODEL), lambda i: (i, 0)),
            pl.BlockSpec((ts, D_MODEL), lambda i: (i, 0)),
            pl.BlockSpec((N_DEV, 1, blk, D_MODEL), lambda i: (0, layer, 0, 0)),
        ] + [_ANY] * len(carried),
        out_specs=[pl.BlockSpec((ts, D_MODEL), lambda i: (i, 0)),
                   pl.BlockSpec((2, N_CHIP, 1, blk, D_MODEL), lambda i: (0, 0, layer, 0, 0))],
        out_shape=out_shape,
        input_output_aliases={3: 1} if carried else {},
        compiler_params=_cparams("arbitrary"),
    )(dx, y, wo, *carried)


C_QKV = (2048, 3584)


def _dproj_parts(dp_ref, dqkv_refs, rows):
    lo, hi = C_QKV
    step = (hi - lo) // len(dqkv_refs)
    return ([(0, dp_ref.at[rows, 0:lo])] + [(lo + i * step, r.at[rows, :]) for i, r in enumerate(dqkv_refs)]
            + [(hi, dp_ref.at[rows, hi:D_INT])])


def inproj_bwd_x(dproj, dqkv, w, x, g, dx_in, layer, carried=None):
    seq = x.shape[0]
    tm = min(seq, 512)

    def body(dp_ref, dq_ref, dk_ref, dv_ref, w_ref, x_ref, g_ref, dxin_ref, dx_ref, dg_ref):
        @pl.when(pl.program_id(0) == 0)
        def _():
            dg_ref[...] = jnp.zeros_like(dg_ref)

        dh = None
        for at, part in _dproj_parts(dp_ref, (dq_ref, dk_ref, dv_ref), slice(None)):
            term = lax.dot_general(part[...], w_ref[0, :, at:at + part.shape[1]], _NT, preferred_element_type=F32)
            dh = term if dh is None else dh + term
        r, xh = _rms_stats(x_ref[...])
        dg_ref[...] += jnp.sum(dh * xh, axis=0, keepdims=True)
        dx_ref[...] = dxin_ref[...] + _rms_bwd(dh, g_ref[...], r, xh)

    third = lambda: pl.BlockSpec((tm, C_WIDTH), lambda i: (i, 0))
    return _call_carrying(
        carried, body, (dproj, *dqkv, w, x, g, dx_in),
        name="inproj_bwd_x",
        grid=(seq // tm,),
        in_specs=[
            pl.BlockSpec((tm, D_INT), lambda i: (i, 0)), third(), third(), third(),
            pl.BlockSpec((1, D_MODEL, D_INT), lambda i: (layer, 0, 0)),
            pl.BlockSpec((tm, D_MODEL), lambda i: (i, 0)),
            _full((1, D_MODEL)),
            pl.BlockSpec((tm, D_MODEL), lambda i: (i, 0)),
        ],
        out_specs=[pl.BlockSpec((tm, D_MODEL), lambda i: (i, 0)), _full((1, D_MODEL))],
        out_shape=[jax.ShapeDtypeStruct((seq, D_MODEL), F32), jax.ShapeDtypeStruct((1, D_MODEL), F32)],
        scratch_shapes=[], semantics=("arbitrary",),
    )


def inproj_bwd_w(h, dproj, dqkv):
    seq = h.shape[0]
    ts, tn = min(seq, 512), 512

    def body(h_ref, dp_ref, dq_ref, dk_ref, dv_ref, dw_ref):
        @pl.when(pl.program_id(0) == 0)
        def _():
            dw_ref[...] = jnp.zeros_like(dw_ref)

        ht = h_ref[...].T
        for at, part in _dproj_parts(dp_ref, (dq_ref, dk_ref, dv_ref), slice(None)):
            for c in range(0, part.shape[1], tn):
                dw_ref[0, :, at + c:at + c + tn] += _dot(ht, part[:, c:c + tn])

    third = lambda: pl.BlockSpec((ts, C_WIDTH), lambda s: (s, 0))
    return pl.pallas_call(
        body,
        name="inproj_bwd_w",
        grid=(seq // ts,),
        in_specs=[pl.BlockSpec((ts, D_MODEL), lambda s: (s, 0)), pl.BlockSpec((ts, D_INT), lambda s: (s, 0)),
                  third(), third(), third()],
        out_specs=_full((1, D_MODEL, D_INT)),
        out_shape=jax.ShapeDtypeStruct((1, D_MODEL, D_INT), F32),
        compiler_params=_cparams("arbitrary"),
    )(h, dproj, *dqkv)


N_IN = 3848


def _internal_of(col):
    return col if col < 768 else (col + 256 if col < 3840 else 768 + col - 3840)


def _column_runs(n_shard):
    runs = []
    for d in range(N_IN // n_shard):
        mine = []
        for j in range(n_shard):
            ci = _internal_of(d * n_shard + j)
            if mine and mine[-1][0] + mine[-1][1] == ci:
                mine[-1][1] += 1
            else:
                mine.append([ci, 1, j])
        runs.append(mine)
    return runs


def assemble_w_in(wi_all):
    n_dev, depth, _, n_shard = wi_all.shape
    tr = 256
    pieces = [[] for _ in range(D_INT // LANES)]
    for d, mine in enumerate(_column_runs(n_shard)):
        for ci, ln, off in mine:
            while ln > 0:
                blk, at = divmod(ci, LANES)
                take = min(ln, LANES - at)
                pieces[blk].append((at, take, d, off))
                ci, ln, off = ci + take, ln - take, off + take

    def body(x_ref, o_ref):
        for blk, parts in enumerate(pieces):
            vals, at = [], 0
            for start, ln, d, off in sorted(parts):
                if start > at:
                    vals.append(jnp.zeros((tr, start - at), BF16))
                vals.append(x_ref[d, 0, :, off:off + ln])
                at = start + ln
            if at < LANES:
                vals.append(jnp.zeros((tr, LANES - at), BF16))
            o_ref[0, :, blk * LANES:(blk + 1) * LANES] = vals[0] if len(vals) == 1 else jnp.concatenate(vals, axis=1)

    return pl.pallas_call(
        body,
        name="assemble_w_in",
        grid=(depth, D_MODEL // tr),
        in_specs=[pl.BlockSpec((n_dev, 1, tr, n_shard), lambda l, r: (0, l, r, 0))],
        out_specs=pl.BlockSpec((1, tr, D_INT), lambda l, r: (l, r, 0)),
        out_shape=jax.ShapeDtypeStruct((depth, D_MODEL, D_INT), BF16),
        compiler_params=_cparams("parallel", "parallel"),
    )(wi_all)


def split_w_in_grad(dwi, n_shard):
    depth = dwi.shape[0]
    tr = 256
    runs = _column_runs(n_shard)

    def body(x_ref, o_ref):
        for d, mine in enumerate(runs):
            for ci, ln, off in mine:
                o_ref[d % 2, d // 2, 0, :, off:off + ln] = x_ref[0, :, ci:ci + ln]

    return pl.pallas_call(
        body,
        name="split_w_in_grad",
        grid=(depth, D_MODEL // tr),
        in_specs=[pl.BlockSpec((1, tr, D_INT), lambda l, r: (l, r, 0))],
        out_specs=pl.BlockSpec((2, N_CHIP, 1, tr, n_shard), lambda l, r: (0, 0, l, r, 0)),
        out_shape=jax.ShapeDtypeStruct((2, N_CHIP, depth, D_MODEL, n_shard), F32),
        compiler_params=_cparams("parallel", "parallel"),
    )(dwi)


def final_loss(x, g, tgt):
    seq = x.shape[0]
    tm = min(seq, 512)

    def body(x_ref, g_ref, t_ref, dx_ref, dg_ref, loss_ref):
        @pl.when(pl.program_id(0) == 0)
        def _():
            dg_ref[...] = jnp.zeros_like(dg_ref)
            loss_ref[...] = jnp.zeros_like(loss_ref)

        g = g_ref[...]
        r, xh = _rms_stats(x_ref[...])
        err = xh * g - t_ref[...]
        sq = jnp.sum(jnp.sum(err * err, axis=1, keepdims=True), axis=0, keepdims=True)
        loss_ref[...] += jnp.broadcast_to(sq * (0.5 / D_MODEL), loss_ref.shape)
        dout = err * (1.0 / D_MODEL)
        dg_ref[...] += jnp.sum(dout * xh, axis=0, keepdims=True)
        dx_ref[...] = _rms_bwd(dout, g, r, xh)

    return pl.pallas_call(
        body,
        name="final_loss",
        grid=(seq // tm,),
        in_specs=[pl.BlockSpec((tm, D_MODEL), lambda i: (i, 0)), _full((1, D_MODEL)), pl.BlockSpec((tm, D_MODEL), lambda i: (i, 0))],
        out_specs=[pl.BlockSpec((tm, D_MODEL), lambda i: (i, 0)), _full((1, D_MODEL)), _full((8, LANES))],
        out_shape=[jax.ShapeDtypeStruct((seq, D_MODEL), F32), jax.ShapeDtypeStruct((1, D_MODEL), F32), jax.ShapeDtypeStruct((8, LANES), F32)],
        compiler_params=_cparams("arbitrary"),
    )(x, g, tgt)


C_WIDTH = 512
C_HEADS = 8
C_HDIM = 64
C_PAIRS = C_HEADS // 2
C_BQ = 512
C_TAIL = 16
C_KG = 4


def _split3(x):
    hi = x.astype(BF16)
    r = x - hi.astype(F32)
    mid = r.astype(BF16)
    return hi, mid, (r - mid.astype(F32)).astype(BF16)


def _piece_selectors():
    sel = np.zeros((C_HEADS, 3 * LANES, LANES), np.float32)
    for p in range(C_PAIRS):
        for e in range(2):
            for t in range(3):
                sel[2 * p + e, t * LANES + 2 * p + e, 3 * e + t] = -1.0
    return sel


def fox_prep(proj, bf_row):
    seq = proj.shape[0]
    nblk = seq // CHUNK
    tril = jnp.asarray(np.tril(np.ones((CHUNK, CHUNK), np.float32)), BF16)
    sel = jnp.asarray(_piece_selectors(), BF16)
    rows_t = CHUNK + C_TAIL

    def body(fl_ref, q_ref, k_ref, v_ref, bf_ref, l_ref, sel_ref, ka_ref, va_ref, vt_ref, kt_ref, qt_ref, qa_ref, carry_ref):
        @pl.when(pl.program_id(0) == 0)
        def _():
            carry_ref[...] = jnp.zeros_like(carry_ref)

        lf = jax.nn.log_sigmoid(fl_ref[:, :LANES] + bf_ref[...])
        c = _exact_times(l_ref[...], lf, 3) + carry_ref[...]
        carry_ref[...] += jnp.sum(lf, axis=0, keepdims=True)
        c3 = jnp.concatenate(_split3(c), axis=1)
        lane = lax.broadcasted_iota(jnp.int32, (CHUNK, LANES), 1)
        row = lax.broadcasted_iota(jnp.int32, (CHUNK, LANES), 0)
        r16 = lax.broadcasted_iota(jnp.int32, (C_TAIL, 2 * CHUNK), 0)
        l16 = lax.broadcasted_iota(jnp.int32, (C_TAIL, 2 * CHUNK), 1)
        zero = jnp.zeros((CHUNK, LANES), BF16)
        one = jnp.ones((CHUNK, LANES), BF16)

        def by_keys(x, right_a, right_b):
            xb = x.astype(BF16)
            top = jnp.concatenate([jnp.where(lane < C_HDIM, xb, zero), right_a], axis=1)
            return jnp.concatenate([top, jnp.concatenate([jnp.where(lane < C_HDIM, zero, xb), right_b], axis=1)], axis=0)

        def by_lanes(x, tail):
            xt = x.T.astype(BF16)
            main = jnp.concatenate([jnp.where(row < C_HDIM, xt, zero), jnp.where(row < C_HDIM, zero, xt)], axis=1)
            return jnp.concatenate([main, tail], axis=0)

        for p in range(C_PAIRS):
            cols = slice(p * LANES, (p + 1) * LANES)
            q2, k2, v2 = q_ref[:, cols] * (C_HDIM ** -0.5), k_ref[:, cols], v_ref[:, cols]
            negc = [_dot(c3, sel_ref[2 * p + e]).astype(BF16) for e in range(2)]
            ones3 = [jnp.where((lane >= 3 * e) & (lane < 3 * e + 3), one, zero) for e in range(2)]
            tail = jnp.where(((r16 == 2 * p) & (l16 < CHUNK)) | ((r16 == 2 * p + 1) & (l16 >= CHUNK)), 1.0, 0.0).astype(BF16)
            ka_ref[p] = by_keys(k2, negc[0], negc[1])
            va_ref[p] = by_keys(v2, ones3[0], ones3[1])
            kt_ref[p] = by_lanes(k2, tail)
            vt_ref[p] = by_lanes(v2, tail)
            qt_ref[p] = jnp.concatenate([q2.T.astype(BF16), jnp.where(row < 6, one, zero)], axis=0)
            qa_ref[p] = jnp.concatenate([q2.astype(BF16), jnp.where((lane == 2 * p) | (lane == 2 * p + 1), one, zero)], axis=1)

    wide = lambda j: pl.BlockSpec((CHUNK, C_WIDTH), lambda n: (n, j))
    by_rows = pl.BlockSpec((C_PAIRS, 2 * CHUNK, 2 * CHUNK), lambda n: (0, n, 0))
    by_cols = pl.BlockSpec((C_PAIRS, rows_t, 2 * CHUNK), lambda n: (0, 0, n))
    return pl.pallas_call(
        body,
        name="fox_prep",
        grid=(nblk,),
        in_specs=[pl.BlockSpec((CHUNK, 256), lambda n: (n, 3)), wide(4), wide(5), wide(6), _full((1, LANES)),
                  _full((CHUNK, CHUNK)), _full((C_HEADS, 3 * LANES, LANES))],
        out_specs=[by_rows, by_rows, by_cols, by_cols,
                   pl.BlockSpec((C_PAIRS, 2 * CHUNK, CHUNK), lambda n: (0, 0, n)),
                   pl.BlockSpec((C_PAIRS, CHUNK, 2 * CHUNK), lambda n: (0, n, 0))],
        out_shape=[jax.ShapeDtypeStruct((C_PAIRS, 2 * seq, 2 * CHUNK), BF16)] * 2
        + [jax.ShapeDtypeStruct((C_PAIRS, rows_t, 2 * seq), BF16)] * 2
        + [jax.ShapeDtypeStruct((C_PAIRS, 2 * CHUNK, seq), BF16), jax.ShapeDtypeStruct((C_PAIRS, seq, 2 * CHUNK), BF16)],
        scratch_shapes=[pltpu.VMEM((1, LANES), F32)],
        compiler_params=_cparams("arbitrary"),
    )(proj, proj, proj, proj, bf_row, tril, sel)


def _visible(shape, key0, query0):
    row = lax.broadcasted_iota(jnp.int32, shape, 0)
    key = key0 + lax.shift_left(lax.shift_right_logical(row, 8), 7) + (row & (CHUNK - 1))
    return key <= query0 + lax.broadcasted_iota(jnp.int32, shape, 1)


def _rows_ab(a, b, n):
    return jnp.concatenate([jnp.broadcast_to(a, (C_HDIM, n)), jnp.broadcast_to(b, (C_HDIM, n))], axis=0)


def _call_carrying(ex, body, operands, *, name, grid, in_specs, out_specs, out_shape, scratch_shapes, semantics=None):
    if ex is None:
        semantics = semantics or ("parallel", *["arbitrary"] * (len(grid) - 1))
        return pl.pallas_call(body, name=name, grid=grid, in_specs=in_specs, out_specs=out_specs, out_shape=out_shape,
                              scratch_shapes=scratch_shapes, compiler_params=_cparams(*semantics))(*operands)
    n_in, n_out = len(in_specs), len(out_specs)

    def wrapped(*refs):
        own, parts = _carried_refs(refs, n_in, n_out, ex)
        ids = [pl.program_id(a) for a in range(len(grid))]
        pl.when(functools.reduce(jnp.logical_and, [i == 0 for i in ids]))(lambda: ex.start(*parts))
        body(*own)
        pl.when(functools.reduce(jnp.logical_and, [i == g - 1 for i, g in zip(ids, grid)]))(lambda: ex.finish(*parts))

    return pl.pallas_call(
        wrapped, name=name, grid=grid,
        in_specs=list(in_specs) + [_ANY] * len(ex.inputs), out_specs=list(out_specs) + [_ANY] * len(ex.out_shape),
        out_shape=list(out_shape) + list(ex.out_shape), scratch_shapes=list(scratch_shapes) + list(ex.scratch),
        input_output_aliases={n_in + i: n_out + o for i, o in getattr(ex, "aliases", {}).items()},
        compiler_params=_cparams(*["arbitrary"] * len(grid)),
    )(*operands, *ex.inputs)


def fox_fwd(qt, ka, vt, carried=None):
    seq = qt.shape[2]
    nblk = seq // CHUNK
    bq = min(C_BQ, seq)
    grp = bq // CHUNK
    rows_t = CHUNK + C_TAIL

    def body(qt_ref, ka_ref, vt_ref, o_ref, lse_ref, acc_ref, s_ref):
        p, i = pl.program_id(0), pl.program_id(1)
        qtile = qt_ref[0]
        r16 = lax.broadcasted_iota(jnp.int32, (C_TAIL, bq), 0)

        def scores(t):
            at = pl.multiple_of(t * grp * 2 * CHUNK, 2 * CHUNK)
            return _dot(ka_ref[0, pl.ds(at, grp * 2 * CHUNK), :], qtile)

        def group(t, m, masked):
            ma, mb = m
            at = pl.multiple_of(t * grp * 2 * CHUNK, 2 * CHUNK)
            s = s_ref[...]
            if masked:
                s = jnp.where(_visible(s.shape, t * bq, i * bq), s, -jnp.inf)
            sa = [s[g * 2 * CHUNK:g * 2 * CHUNK + CHUNK] for g in range(grp)]
            sb = [s[g * 2 * CHUNK + CHUNK:(g + 1) * 2 * CHUNK] for g in range(grp)]
            na, nb = ma, mb
            for g in range(grp):
                na = jnp.maximum(na, jnp.max(sa[g], axis=0, keepdims=True))
                nb = jnp.maximum(nb, jnp.max(sb[g], axis=0, keepdims=True))
            al_a, al_b = jnp.exp(ma - na), jnp.exp(mb - nb)
            pt = jnp.concatenate([jnp.exp(x - n) for g in range(grp) for x, n in ((sa[g], na), (sb[g], nb))], axis=0)
            pv = _dot(vt_ref[0, :, pl.ds(at, grp * 2 * CHUNK)], pt.astype(BF16))
            tail = jnp.where(r16 == 2 * p, al_a, jnp.where(r16 == 2 * p + 1, al_b, 1.0))
            acc_ref[...] = acc_ref[...] * jnp.concatenate([_rows_ab(al_a, al_b, bq), tail], axis=0) + pv
            return na, nb

        def step(t, m):
            s_next = scores(t + 1)
            m = group(t, m, False)
            s_ref[...] = s_next
            return m

        acc_ref[...] = jnp.zeros_like(acc_ref)
        s_ref[...] = scores(0)
        m = (jnp.full((1, bq), -jnp.inf, F32), jnp.full((1, bq), -jnp.inf, F32))
        m = lax.fori_loop(0, i, step, m)
        ma, mb = group(i, m, True)
        tailv = acc_ref[CHUNK:rows_t, :]
        la = jnp.sum(jnp.where(r16 == 2 * p, tailv, 0.0), axis=0, keepdims=True)
        lb = jnp.sum(jnp.where(r16 == 2 * p + 1, tailv, 0.0), axis=0, keepdims=True)
        o_ref[...] = (acc_ref[0:CHUNK, :] * _rows_ab(1.0 / la, 1.0 / lb, bq)).T
        lse_ref[0, 0:1, :] = ma + jnp.log(la)
        lse_ref[0, 1:2, :] = mb + jnp.log(lb)

    return _call_carrying(
        carried, body, (qt, ka, vt),
        name="fox_fwd",
        grid=(C_PAIRS, seq // bq),
        in_specs=[
            pl.BlockSpec((1, 2 * CHUNK, bq), lambda p, i: (p, 0, i)),
            pl.BlockSpec((1, 2 * seq, 2 * CHUNK), lambda p, i: (p, 0, 0)),
            pl.BlockSpec((1, rows_t, 2 * seq), lambda p, i: (p, 0, 0)),
        ],
        out_specs=[pl.BlockSpec((bq, LANES), lambda p, i: (i, p)), pl.BlockSpec((1, 2, bq), lambda p, i: (p, 0, i))],
        out_shape=[jax.ShapeDtypeStruct((seq, C_WIDTH), F32), jax.ShapeDtypeStruct((C_PAIRS, 2, seq), F32)],
        scratch_shapes=[pltpu.VMEM((rows_t, bq), F32), pltpu.VMEM((grp * 2 * CHUNK, bq), F32)],
    )


def fox_bwd_prep(dy, o, proj, dproj):
    seq = o.shape[0]
    ind = np.zeros((C_WIDTH, LANES), np.float32)
    for h in range(C_HEADS):
        ind[h * C_HDIM:(h + 1) * C_HDIM, h] = 1.0
    ind = jnp.asarray(ind, BF16)
    sel = _piece_selectors()
    sel = jnp.asarray(np.stack([sel[2 * p].T + sel[2 * p + 1].T for p in range(C_PAIRS)]), BF16)

    def body(dy_ref, o_ref, z_ref, ind_ref, sel_ref, _, do_ref, dz_ref, dot_ref):
        dy_c, o_v, z = dy_ref[...], o_ref[...], z_ref[...]
        sg = jax.nn.sigmoid(z)
        do = dy_c * (z * sg)
        do_ref[...] = do.astype(BF16)
        dz_ref[...] = (dy_c * o_v * (sg * (1.0 + z * (1.0 - sg)))).astype(BF16)
        prod = do * o_v
        hi = prod.astype(BF16)
        lo = (prod - hi.astype(F32)).astype(BF16)
        delta = _dot(hi, ind_ref[...]) + _dot(lo, ind_ref[...])
        d3 = jnp.concatenate(_split3(delta.T), axis=0)
        for p in range(C_PAIRS):
            tail = _dot(sel_ref[p], d3).astype(BF16)
            dot_ref[p] = jnp.concatenate([do[:, p * LANES:(p + 1) * LANES].T.astype(BF16), tail], axis=0)

    return pl.pallas_call(
        body,
        name="fox_bwd_prep",
        grid=(seq // CHUNK,),
        in_specs=[
            pl.BlockSpec((CHUNK, C_WIDTH), lambda i: (i, 1)),
            pl.BlockSpec((CHUNK, C_WIDTH), lambda i: (i, 0)),
            pl.BlockSpec((CHUNK, C_WIDTH), lambda i: (i, 7)),
            _full((C_WIDTH, LANES)), _full((C_PAIRS, LANES, 3 * LANES)), _ANY,
        ],
        out_specs=[
            pl.BlockSpec((CHUNK, C_WIDTH), lambda i: (i, 0)),
            pl.BlockSpec((CHUNK, C_WIDTH), lambda i: (i, 7)),
            pl.BlockSpec((C_PAIRS, 2 * CHUNK, CHUNK), lambda i: (0, 0, i)),
        ],
        out_shape=[jax.ShapeDtypeStruct((seq, C_WIDTH), BF16), jax.ShapeDtypeStruct(dproj.shape, BF16),
                   jax.ShapeDtypeStruct((C_PAIRS, 2 * CHUNK, seq), BF16)],
        input_output_aliases={5: 1},
        compiler_params=_cparams("parallel"),
    )(dy, o, proj, ind, sel, dproj)


def fox_bwd(ka, va, kt, qt, dot_t, qa, dob, lse, carried=None):
    seq = qt.shape[2]
    nblk = seq // CHUNK
    bq = min(C_BQ, seq)
    nq = seq // bq
    kg = min(C_KG, nblk)
    ng = nblk // kg
    rows_t = CHUNK + C_TAIL

    def body(ka_ref, va_ref, kt_ref, qt_ref, dot_ref, qa_ref, do_ref, lse_ref,
             dq_ref, dk_ref, dv_ref, dck_ref, dcq_ref, dqt_acc, dv_acc, dka_acc):
        p, jg = pl.program_id(0), pl.program_id(1)

        @pl.when(jg == 0)
        def _():
            dqt_acc[...] = jnp.zeros_like(dqt_acc)

        dv_acc[...] = jnp.zeros_like(dv_acc)
        dka_acc[...] = jnp.zeros_like(dka_acc)

        def step(i, carry, masked):
            cols = pl.ds(pl.multiple_of(i * bq, bq), bq)
            qtile, dotile = qt_ref[0, :, cols], dot_ref[0, :, cols]
            do, qa_i = do_ref[cols, :], qa_ref[0, cols, :]
            lse2 = jnp.concatenate([jnp.broadcast_to(lse_ref[0, 0:1, cols], (CHUNK, bq)),
                                    jnp.broadcast_to(lse_ref[0, 1:2, cols], (CHUNK, bq))] * kg, axis=0)
            pt = jnp.exp(_dot(ka_ref[0], qtile) - lse2)
            if masked:
                pt = jnp.where(_visible(pt.shape, jg * kg * CHUNK, i * bq), pt, 0.0)
            ds = pt * _dot(va_ref[0], dotile)
            ptb, dsb = pt.astype(BF16), ds.astype(BF16)
            dv_acc[...] += _dot(ptb, do)
            dka_acc[...] += _dot(dsb, qa_i)
            dqt_acc[:, cols] += _dot(kt_ref[0], dsb)
            return carry

        i0 = (jg * kg * CHUNK) // bq
        step(i0, 0, True)
        lax.fori_loop(i0 + 1, nq, functools.partial(step, masked=False), 0)
        lane = lax.broadcasted_iota(jnp.int32, (CHUNK, LANES), 1)
        for kb in range(kg):
            rows = slice(kb * CHUNK, (kb + 1) * CHUNK)
            ra = slice(kb * 2 * CHUNK, kb * 2 * CHUNK + CHUNK)
            rb = slice(kb * 2 * CHUNK + CHUNK, (kb + 1) * 2 * CHUNK)
            dk_ref[rows, :] = jnp.where(lane < C_HDIM, dka_acc[ra, 0:LANES], dka_acc[rb, 0:LANES]).astype(BF16)
            dv_ref[rows, :] = jnp.where(lane < C_HDIM, dv_acc[ra, :], dv_acc[rb, :]).astype(BF16)
            dck_ref[0, rows, :] = (jnp.where(lane == 2 * p, dka_acc[ra, LANES:], 0.0)
                                   + jnp.where(lane == 2 * p + 1, dka_acc[rb, LANES:], 0.0))

        @pl.when(jg == ng - 1)
        def _():
            for c in range(nq):
                dq_ref[c * bq:(c + 1) * bq, :] = (dqt_acc[0:CHUNK, c * bq:(c + 1) * bq].T * (C_HDIM ** -0.5)).astype(BF16)
            dcq_ref[0] = dqt_acc[CHUNK:rows_t, :]

    per_pair = lambda r, c: pl.BlockSpec((1, r, c), lambda p, j: (p, 0, 0))
    by_rows = pl.BlockSpec((1, kg * 2 * CHUNK, 2 * CHUNK), lambda p, j: (p, j, 0))
    by_cols = pl.BlockSpec((1, rows_t, kg * 2 * CHUNK), lambda p, j: (p, 0, j))
    return _call_carrying(
        carried, body, (ka, va, kt, qt, dot_t, qa, dob, lse),
        name="fox_bwd",
        grid=(C_PAIRS, ng),
        in_specs=[by_rows, by_rows, by_cols, per_pair(2 * CHUNK, seq), per_pair(2 * CHUNK, seq),
                  per_pair(seq, 2 * CHUNK), pl.BlockSpec((seq, LANES), lambda p, j: (0, p)), per_pair(2, seq)],
        out_specs=[pl.BlockSpec((seq, LANES), lambda p, j: (0, p)),
                   pl.BlockSpec((kg * CHUNK, LANES), lambda p, j: (j, p)),
                   pl.BlockSpec((kg * CHUNK, LANES), lambda p, j: (j, p)),
                   pl.BlockSpec((1, kg * CHUNK, LANES), lambda p, j: (p, j, 0)),
                   per_pair(C_TAIL, seq)],
        out_shape=[jax.ShapeDtypeStruct((seq, C_WIDTH), BF16)] * 3
        + [jax.ShapeDtypeStruct((C_PAIRS, seq, LANES), F32), jax.ShapeDtypeStruct((C_PAIRS, C_TAIL, seq), F32)],
        scratch_shapes=[pltpu.VMEM((rows_t, seq), F32), pltpu.VMEM((kg * 2 * CHUNK, LANES), F32),
                        pltpu.VMEM((kg * 2 * CHUNK, 2 * CHUNK), F32)],
    )


def fox_post(dcq, dck, proj, bf_row, dproj):
    seq = proj.shape[0]
    nc = seq // CHUNK
    triu = jnp.asarray(np.triu(np.ones((CHUNK, CHUNK), np.float32)), BF16)

    def body(dq_ref, dk_ref, fl_ref, bf_ref, u_ref, _, dfl_ref, dbf_ref, carry_ref):
        @pl.when(pl.program_id(0) == 0)
        def _():
            carry_ref[...] = jnp.zeros_like(carry_ref)
            dbf_ref[...] = jnp.zeros_like(dbf_ref)

        rows = (dq_ref[0] + dq_ref[1]) + (dq_ref[2] + dq_ref[3])
        dc = jnp.concatenate([rows, jnp.zeros((CHUNK - C_TAIL, CHUNK), F32)], axis=0).T
        dc = dc - ((dk_ref[0] + dk_ref[1]) + (dk_ref[2] + dk_ref[3]))
        g = _exact_times(u_ref[...], dc, 3) + carry_ref[...]
        carry_ref[...] += jnp.sum(dc, axis=0, keepdims=True)
        dfl = g * jax.nn.sigmoid(-(fl_ref[:, :LANES] + bf_ref[...]))
        dbf_ref[...] += jnp.sum(dfl, axis=0, keepdims=True)
        dfl_ref[...] = jnp.concatenate([dfl, jnp.zeros_like(dfl)], axis=1).astype(BF16)

    rev = lambda n: nc - 1 - n
    return pl.pallas_call(
        body,
        name="fox_post",
        grid=(nc,),
        in_specs=[
            pl.BlockSpec((C_PAIRS, C_TAIL, CHUNK), lambda n: (0, 0, rev(n))),
            pl.BlockSpec((C_PAIRS, CHUNK, LANES), lambda n: (0, rev(n), 0)),
            pl.BlockSpec((CHUNK, 256), lambda n: (rev(n), 3)),
            _full((1, LANES)), _full((CHUNK, CHUNK)), _ANY,
        ],
        out_specs=[pl.BlockSpec((CHUNK, 256), lambda n: (rev(n), 3)), _full((1, LANES))],
        out_shape=[jax.ShapeDtypeStruct(dproj.shape, BF16), jax.ShapeDtypeStruct((1, LANES), F32)],
        input_output_aliases={5: 0},
        scratch_shapes=[pltpu.VMEM((1, LANES), F32)],
        compiler_params=_cparams("arbitrary"),
    )(dcq, dck, proj, bf_row, triu, dproj)


N_DEV = 8
MESH = pl.DeviceIdType.MESH
_ANY = pl.BlockSpec(memory_space=pl.ANY)


def _mesh_pos():
    return lax.axis_index("x"), lax.axis_index("y"), lax.axis_index("c")


def _dev_index(px, py, pc):
    return 4 * px + 2 * py + pc


def _row_pieces(ref, rows):
    return [ref.at[idx + (pl.ds(r, rows),)] for idx in np.ndindex(*ref.shape[:-2]) for r in range(0, ref.shape[-2], rows)]


class _Transfer:
    def __init__(self, src, dst, rows, send_sem, recv_sem, to):
        self.src, self.dst, self.rows, self.sems, self.to = src, dst, rows, (send_sem, recv_sem), to

    def _copy(self, src, dst):
        return pltpu.make_async_remote_copy(src_ref=src, dst_ref=dst, send_sem=self.sems[0], recv_sem=self.sems[1],
                                            device_id=self.to, device_id_type=MESH)

    def start(self):
        for s, d in zip(_row_pieces(self.src, self.rows), _row_pieces(self.dst, self.rows), strict=True):
            self._copy(s, d).start()

    def wait_send(self):
        self._copy(self.src, self.dst).wait_send()

    def wait_recv(self):
        self._copy(self.src, self.dst).wait_recv()


def _exchange_call(ex, name):
    n_in, n_out = len(ex.inputs), len(ex.out_shape)

    def body(*refs):
        parts = refs[:n_in], refs[n_in:n_in + n_out], refs[n_in + n_out:]
        ex.start(*parts)
        ex.finish(*parts)

    return pl.pallas_call(body, name=name, in_specs=[_ANY] * n_in, out_specs=[_ANY] * n_out, out_shape=ex.out_shape,
                          scratch_shapes=ex.scratch, input_output_aliases=getattr(ex, "aliases", {}))(*ex.inputs)


def _carried_refs(refs, n_in, n_out, ex):
    k_in, k_out, k_sem = (len(ex.inputs), len(ex.out_shape), len(ex.scratch)) if ex else (0, 0, 0)
    a, b, c = n_in + k_in, n_in + k_in + n_out, n_in + k_in + n_out + k_out
    own = refs[:n_in] + refs[a:b] + refs[c:len(refs) - k_sem]
    return own, (refs[n_in:a], refs[b:c], refs[len(refs) - k_sem:])


class AllGatherWeights:
    piece_rows = (128, 64)

    def __init__(self, wi, wo):
        self.inputs = (wi, wo)
        self.out_shape = [jax.ShapeDtypeStruct((N_DEV,) + wi.shape, wi.dtype), jax.ShapeDtypeStruct((N_DEV,) + wo.shape, wo.dtype)]
        self.scratch = [pltpu.SemaphoreType.DMA((2, 7)), pltpu.SemaphoreType.DMA((2, 7)), pltpu.SemaphoreType.DMA((2,))]

    def _plan(self, ins, outs, sems):
        send_sems, recv_sems, local_sems = sems
        x, y, c = _mesh_pos()
        me, sibling = (x, y, c), (x, y, 1 - c)
        chips = [(1 - x, y), (x, 1 - y), (1 - x, 1 - y)]
        both = range(2)

        def copy(a, k, block, to, own=False):
            slot = outs[a].at[_dev_index(*block)]
            return _Transfer(ins[a] if own else slot, slot, self.piece_rows[a], send_sems.at[a, k], recv_sems.at[a, k], to)

        mine = [pltpu.make_async_copy(ins[a], outs[a].at[_dev_index(*me)], local_sems.at[a]) for a in both]
        first = [copy(a, 1 + j, me, (*chip, c), own=True) for j, chip in enumerate(chips) for a in both]
        first += [copy(a, 0, me, sibling, own=True) for a in both]
        passed = [copy(a, 4 + j, (*chip, c), sibling) for j, chip in enumerate(chips) for a in both]
        return me, sibling, chips, c, copy, mine, first, passed

    def start(self, ins, outs, sems):
        *_, mine, first, _ = self._plan(ins, outs, sems)
        for cp in mine + first:
            cp.start()

    def finish(self, ins, outs, sems):
        me, sibling, chips, c, copy, mine, first, passed = self._plan(ins, outs, sems)
        for j, chip in enumerate(chips):
            for a in range(2):
                copy(a, 1 + j, (*chip, c), me).wait_recv()
            for a in range(2):
                passed[2 * j + a].start()
        for a in range(2):
            copy(a, 0, sibling, me).wait_recv()
        for j, chip in enumerate(chips):
            for a in range(2):
                copy(a, 4 + j, (*chip, 1 - c), me).wait_recv()
        for cp in first + passed:
            cp.wait_send()
        for cp in mine:
            cp.wait()


N_CHIP = 4


class PairExchange:
    def __init__(self, by_core, whole=()):
        self.inputs = tuple(by_core) + tuple(whole)
        self.n_by_core = len(by_core)
        self.out_shape = ([jax.ShapeDtypeStruct(a.shape[1:], a.dtype) for a in by_core]
                          + [jax.ShapeDtypeStruct(a.shape, a.dtype) for a in whole])
        n = len(self.inputs)
        self.scratch = [pltpu.SemaphoreType.DMA((n,)), pltpu.SemaphoreType.DMA((n,))]

    def _copies(self, ins, outs, sems):
        x, y, c = _mesh_pos()
        srcs = [r.at[1 - c] if a < self.n_by_core else r for a, r in enumerate(ins)]
        return [_Transfer(srcs[a], outs[a], outs[a].shape[-2], sems[0].at[a], sems[1].at[a], (x, y, 1 - c))
                for a in range(len(ins))]

    def start(self, ins, outs, sems):
        for cp in self._copies(ins, outs, sems):
            cp.start()

    def finish(self, ins, outs, sems):
        copies = self._copies(ins, outs, sems)
        for cp in copies:
            cp.wait_recv()
        for cp in copies:
            cp.wait_send()


def pair_sum(own, other, dtype, rows, name, core, layer, depth, stacked=None):
    n, n_r, n_c = other.shape

    def body(core_ref, a_ref, b_ref, *refs):
        refs[-1][0, 0] = (a_ref[0, 0] + b_ref[0]).astype(dtype)

    carried = () if stacked is None else (stacked,)
    grid_spec = pltpu.PrefetchScalarGridSpec(
        num_scalar_prefetch=1,
        grid=(n, n_r // rows),
        in_specs=[pl.BlockSpec((1, 1, rows, n_c), lambda i, r, s: (s[0], i, r, 0)),
                  pl.BlockSpec((1, rows, n_c), lambda i, r, s: (i, r, 0))] + [_ANY] * len(carried),
        out_specs=pl.BlockSpec((1, 1, rows, n_c), lambda i, r, s: (i, layer, r, 0)),
    )
    return pl.pallas_call(
        body,
        name=name,
        grid_spec=grid_spec,
        out_shape=jax.ShapeDtypeStruct((n, depth, n_r, n_c), dtype),
        input_output_aliases={3: 0} if carried else {},
        compiler_params=_cparams("parallel", "parallel"),
    )(core, own, other, *carried)


def small_sum(a, b, name):
    def body(a_ref, b_ref, o_ref):
        o_ref[...] = a_ref[...] + b_ref[...]

    return pl.pallas_call(body, name=name, out_shape=jax.ShapeDtypeStruct(a.shape, a.dtype))(a, b)


class ChipExchange:
    def __init__(self, by_chip=(), layers=(), gathered=(), stacked=()):
        stacked = tuple(stacked) or (None,) * len(by_chip)
        kept = [s for s in stacked if s is not None]
        self.inputs = tuple(by_chip) + tuple(gathered) + tuple(kept)
        self.n_by_chip, self.n_gathered = len(by_chip), len(gathered)
        self.items = [(a, l) for a in range(len(by_chip)) for l in layers[a]] + [(self.n_by_chip + g, None) for g in range(len(gathered))]
        self.out_shape = ([jax.ShapeDtypeStruct((N_CHIP - 1,) + a.shape[1:], a.dtype) for a in by_chip]
                          + [jax.ShapeDtypeStruct((N_CHIP,) + a.shape, a.dtype) for a in gathered])
        at = iter(range(self.n_by_chip + self.n_gathered, len(self.inputs)))
        self.aliases = {next(at): a for a, s in enumerate(stacked) if s is not None}
        n = len(self.items)
        self.scratch = [pltpu.SemaphoreType.DMA((n, 3)), pltpu.SemaphoreType.DMA((n, 3)),
                        pltpu.SemaphoreType.DMA((max(self.n_gathered, 1),))]

    def _plan(self, ins, outs, sems):
        x, y, c = _mesh_pos()
        chip = 2 * x + y
        n = len(self.items)

        def copy(i, k, sending):
            a, layer = self.items[i]
            px, py = x ^ ((k >> 1) & 1), y ^ (k & 1)
            if layer is not None:
                src, dst = ins[a].at[2 * px + py, layer], outs[a].at[k - 1, layer]
            else:
                src, dst = ins[a], outs[a].at[chip if sending else 2 * px + py]
            return _Transfer(src, dst, dst.shape[-2], sems[0].at[i, k - 1], sems[1].at[i, k - 1], (px, py, c))

        local = [pltpu.make_async_copy(ins[a], outs[a].at[chip], sems[2].at[a - self.n_by_chip])
                 for a in range(self.n_by_chip, self.n_by_chip + self.n_gathered)]
        return n, copy, local

    def start(self, ins, outs, sems):
        n, copy, local = self._plan(ins, outs, sems)
        for cp in local:
            cp.start()
        for k in range(1, N_CHIP):
            for a in range(n):
                copy(a, k, True).start()

    def finish(self, ins, outs, sems):
        n, copy, local = self._plan(ins, outs, sems)
        for k in range(1, N_CHIP):
            for a in range(n):
                copy(a, k, False).wait_recv()
        for k in range(1, N_CHIP):
            for a in range(n):
                copy(a, k, True).wait_send()
        for cp in local:
            cp.wait()


ADAM_LR = 0.001
ADAM_B1 = 0.9
ADAM_B2 = 0.999
ADAM_EPS = 1e-08
ADAM_WD = 0.01
ADAM_STEP = 10


def adam_reduce(parts, w, m, v, rows, name, own=None, chip=None):
    n_l, n_r, n_c = w.shape
    n_parts = parts.shape[0]

    def body(*refs):
        p_ref, w_ref, m_ref, v_ref, g_ref, d_ref, m2_ref, v2_ref = refs[-8:]
        g = p_ref[0, 0].astype(F32)
        if own is not None:
            g = refs[-9][...].reshape(rows, n_c).astype(F32) + g
        for d in range(1, n_parts):
            g = g + p_ref[d, 0].astype(F32)
        m2 = ADAM_B1 * m_ref[0] + (1.0 - ADAM_B1) * g
        v2 = ADAM_B2 * v_ref[0] + (1.0 - ADAM_B2) * (g * g)
        m_hat = m2 / (1.0 - ADAM_B1 ** ADAM_STEP)
        v_hat = v2 / (1.0 - ADAM_B2 ** ADAM_STEP)
        g_ref[0] = g
        d_ref[0] = -ADAM_LR * (m_hat / (jnp.sqrt(v_hat) + ADAM_EPS) + ADAM_WD * w_ref[0])
        m2_ref[0] = m2
        v2_ref[0] = v2

    blk = lambda: pl.BlockSpec((1, rows, n_c), lambda l, r, *_: (l, r, 0))
    in_specs = [pl.BlockSpec((n_parts, 1, rows, n_c), lambda l, r, *_: (0, l, r, 0)), blk(), blk(), blk()]
    args = (parts, w, m, v)
    if own is not None:
        in_specs = [pl.BlockSpec((1, 1, rows, n_c), lambda l, r, s: (s[0], l, r, 0))] + in_specs
        args = (chip, own) + args
    grid_spec = pltpu.PrefetchScalarGridSpec(
        num_scalar_prefetch=0 if own is None else 1, grid=(n_l, n_r // rows), in_specs=in_specs,
        out_specs=[blk(), blk(), blk(), blk()])
    return pl.pallas_call(
        body,
        name=name,
        grid_spec=grid_spec,
        out_shape=[jax.ShapeDtypeStruct(w.shape, F32)] * 4,
        compiler_params=_cparams("parallel", "parallel"),
    )(*args)


def adam_reduce_columns(parts, w, m, v, name, own, chip):
    n_l, n_r, n_c = w.shape
    n_parts = parts.shape[0]
    view = lambda a: jnp.transpose(a, (2, 0, 1))

    def body(_, own_ref, p_ref, w_ref, m_ref, v_ref, g_ref, d_ref, m2_ref, v2_ref):
        for l in range(n_l):
            g = own_ref[0, l].astype(F32) + p_ref[0, l].astype(F32)
            for d in range(1, n_parts):
                g = g + p_ref[d, l].astype(F32)
            g = g.T
            w_l, m_l, v_l = w_ref[:, l, :], m_ref[:, l, :], v_ref[:, l, :]
            m2 = ADAM_B1 * m_l + (1.0 - ADAM_B1) * g
            v2 = ADAM_B2 * v_l + (1.0 - ADAM_B2) * (g * g)
            m_hat = m2 / (1.0 - ADAM_B1 ** ADAM_STEP)
            v_hat = v2 / (1.0 - ADAM_B2 ** ADAM_STEP)
            g_ref[:, l, :] = g
            d_ref[:, l, :] = -ADAM_LR * (m_hat / (jnp.sqrt(v_hat) + ADAM_EPS) + ADAM_WD * w_l)
            m2_ref[:, l, :] = m2
            v2_ref[:, l, :] = v2

    blk = lambda: pl.BlockSpec((LANES, n_l, n_r), lambda c, s: (c, 0, 0))
    grid_spec = pltpu.PrefetchScalarGridSpec(
        num_scalar_prefetch=1, grid=(pl.cdiv(n_c, LANES),),
        in_specs=[pl.BlockSpec((1, n_l, n_r, LANES), lambda c, s: (s[0], 0, 0, c)),
                  pl.BlockSpec((n_parts, n_l, n_r, LANES), lambda c, s: (0, 0, 0, c)), blk(), blk(), blk()],
        out_specs=[blk(), blk(), blk(), blk()])
    outs = pl.pallas_call(
        body,
        name=name,
        grid_spec=grid_spec,
        out_shape=[jax.ShapeDtypeStruct((n_c, n_l, n_r), F32)] * 4,
        compiler_params=_cparams("parallel"),
    )(chip, own, parts, view(w), view(m), view(v))
    return [jnp.transpose(o, (1, 2, 0)) for o in outs]


_SMALL = (("norm_g", (2, 1024)), ("gmlp_ln_g", (2, 4, 64)), ("gmlp_ln_b", (2, 4, 64)),
          ("gmlp_b_s", (2, 4, 128)), ("hgrn_lb", (2, 256)), ("hgrn_onorm_g", (2, 64)), ("fox_b_f", (2, 8)),
          ("final_norm_g", (1024,)), ("loss", ()))


def _padded(n):
    return -(-n // LANES) * LANES


_SMALL_ROWS = -(-sum(_padded(int(np.prod(s))) for _, s in _SMALL) // LANES // 8) * 8


def _pack_small(vals):
    flat = []
    for (name, shape), a in zip(_SMALL, vals, strict=True):
        n = int(np.prod(shape))
        flat.append(jnp.pad(a.reshape(n).astype(F32), (0, _padded(n) - n)))
    flat = jnp.concatenate(flat)
    return jnp.pad(flat, (0, _SMALL_ROWS * LANES - flat.shape[0])).reshape(_SMALL_ROWS, LANES)


def _unpack_small(slab):
    flat, out, at = slab.reshape(-1), {}, 0
    for name, shape in _SMALL:
        n = int(np.prod(shape))
        out[name] = flat[at:at + n].reshape(shape)
        at += _padded(n)
    return out


def sum_parts(parts, name):
    def body(p_ref, o_ref):
        g = p_ref[0]
        for d in range(1, parts.shape[0]):
            g = g + p_ref[d]
        o_ref[...] = g

    return pl.pallas_call(body, name=name, out_shape=jax.ShapeDtypeStruct(parts.shape[1:], F32))(parts)


def adam_small(gs, ws, ms, vs):
    n = len(gs)

    def body(*refs):
        for k in range(n):
            g, w, m, v = (refs[j * n + k][...] for j in range(4))
            m2 = ADAM_B1 * m + (1.0 - ADAM_B1) * g
            v2 = ADAM_B2 * v + (1.0 - ADAM_B2) * (g * g)
            m_hat = m2 / (1.0 - ADAM_B1 ** ADAM_STEP)
            v_hat = v2 / (1.0 - ADAM_B2 ** ADAM_STEP)
            refs[4 * n + k][...] = -ADAM_LR * (m_hat / (jnp.sqrt(v_hat) + ADAM_EPS) + ADAM_WD * w)
            refs[5 * n + k][...] = m2
            refs[6 * n + k][...] = v2

    outs = pl.pallas_call(body, name="adam_small",
                          out_shape=[jax.ShapeDtypeStruct(w.shape, F32) for _ in range(3) for w in ws])(*gs, *ws, *ms, *vs)
    return outs[:n], outs[n:2 * n], outs[2 * n:]


def kernel(x, norm_g, w_in, w_out, gmlp_ln_g, gmlp_ln_b, gmlp_w_s, gmlp_b_s, hgrn_lb, hgrn_onorm_g, fox_b_f, final_norm_g, loss_target, m_norm_g, m_w_in, m_w_out, m_gmlp_ln_g, m_gmlp_ln_b, m_gmlp_w_s, m_gmlp_b_s, m_hgrn_lb, m_hgrn_onorm_g, m_fox_b_f, m_final_norm_g, v_norm_g, v_w_in, v_w_out, v_gmlp_ln_g, v_gmlp_ln_b, v_gmlp_w_s, v_gmlp_b_s, v_hgrn_lb, v_hgrn_onorm_g, v_fox_b_f, v_final_norm_g):
    depth = w_in.shape[0]
    seq = x.shape[1]
    assert w_in.shape[2] * N_DEV == N_IN
    xs, tgt = x[0], loss_target[0]

    wi_blk, wo_blk = w_in.astype(BF16), w_out.astype(BF16)
    wi_all, wo_all = _exchange_call(AllGatherWeights(wi_blk[0], wo_blk[0]), "allgather_weights_0")

    ln_g = gmlp_ln_g.reshape(depth, 1, A_WIDTH)
    ln_b = gmlp_ln_b.reshape(depth, 1, A_WIDTH)
    bs_t = jnp.pad(jnp.transpose(gmlp_b_s, (0, 2, 1)), ((0, 0), (0, 0), (0, LANES - A_GROUPS)))
    lb0, lb1 = hgrn_lb[0:1], hgrn_lb[1:2]
    onorm = jnp.tile(hgrn_onorm_g, (1, B_HEADS)).reshape(depth, 1, B_WIDTH)
    bf_row = jnp.pad(fox_b_f, ((0, 0), (0, LANES - C_HEADS))).reshape(depth, 1, LANES)

    core = lax.axis_index("c").astype(jnp.int32).reshape(1)
    chip = (2 * lax.axis_index("x") + lax.axis_index("y")).astype(jnp.int32).reshape(1)

    saved = []
    xc = xs
    for l in range(depth):
        wi_int = assemble_w_in(wi_all[:, None])
        proj, h = inproj(xc, norm_g[l:l + 1], wi_int, 0)
        ya = gmlp_fwd(proj, ln_g[l], ln_b[l], gmlp_w_s[l], bs_t[l])
        yb, states = hgrn_fwd(proj, lb0, lb1, onorm[l], l)
        ka, va, vt, kt, qt, qa = fox_prep(proj, bf_row[l])
        nxt = AllGatherWeights(wi_blk[l + 1], wo_blk[l + 1]) if l + 1 < depth else None
        o, lse, *gathered = fox_fwd(qt, ka, vt, nxt)
        xn, yfull = outproj(xc, ya, yb, o, proj, wo_all[:, None], 0)
        saved.append((xc, proj, h, states, ka, va, kt, qt, qa, o, lse, yfull, wi_int, wo_all))
        if gathered:
            wi_all, wo_all = gathered
        xc = xn

    dx, d_final_g, loss_tile = final_loss(xc, final_norm_g[None], tgt)

    n_shard = w_in.shape[2]
    g_norm = [None] * depth
    g_ln_g, g_ln_b, g_ws, g_bs, g_on, g_bf = ([None] * depth for _ in range(6))
    g_lb0, g_lb1 = jnp.zeros_like(lb0), jnp.zeros_like(lb1)
    swi = swo = rwi = rwo = None
    for l in reversed(range(depth)):
        x_in, proj, h, states, ka, va, kt, qt, qa, o, lse, yfull, wi_int, wo_l = saved[l]
        dy, gwo = outproj_bwd(dx, yfull, wo_l[:, None], 0)
        gwo = gwo[:, :, 0]
        dproj, g_ln_g[l], g_ln_b[l], g_ws[l], dbs_t = gmlp_bwd(proj, dy, ln_g[l], ln_b[l], gmlp_w_s[l], bs_t[l])
        g_bs[l] = dbs_t[:, :A_GROUPS].T
        if l > 0:
            (qwo,) = _exchange_call(PairExchange([gwo]), f"pair_exchange_w_out_{l}")
        else:
            gws = jnp.stack(g_ws).reshape(-1, LANES)
            qwo, qws = _exchange_call(PairExchange([gwo], [gws]), f"pair_exchange_w_out_{l}")
            sws = small_sum(gws, qws, "pair_sum_w_s")
        swo = pair_sum(gwo, qwo, BF16, gwo.shape[2], "pair_sum_w_out", core, l, depth, swo)
        dproj, d0, d1, don = hgrn_bwd(proj, states, dy, lb0, lb1, onorm[l], l, dproj)
        g_lb0, g_lb1 = g_lb0 + d0, g_lb1 + d1
        g_on[l] = don.reshape(B_HEADS, B_KDIM).sum(0)
        dob, dproj, dot_t = fox_bwd_prep(dy, o, proj, dproj)
        top = l == depth - 1
        ride = ChipExchange([swo] if top else [swi, swo], [(l,)] if top else [(l + 1,), (l,)],
                            [sws] if l == 0 else [], [rwo] if top else [rwi, rwo])
        outs = fox_bwd(ka, va, kt, qt, dot_t, qa, dob, lse, ride)
        dqkv, (dck, dcq), got = outs[:3], outs[3:5], list(outs[5:])
        if not top:
            rwi = got.pop(0)
        rwo = got.pop(0)
        if l == 0:
            (rws,) = got
        dproj, dbf = fox_post(dcq, dck, proj, bf_row[l], dproj)
        g_bf[l] = dbf[0, :C_HEADS]
        gwi = split_w_in_grad(inproj_bwd_w(h, dproj, dqkv), n_shard)[:, :, 0]
        (qwi,) = _exchange_call(PairExchange([gwi]), f"pair_exchange_w_in_{l}")
        swi = pair_sum(gwi, qwi, BF16, 256, "pair_sum_w_in", core, l, depth, swi)
        ride = ChipExchange([swi], [(l,)], stacked=[rwi]) if l == 0 else None
        outs = inproj_bwd_x(dproj, dqkv, wi_int, x_in, norm_g[l:l + 1], dx, 0, ride)
        dx, g_norm[l] = outs[:2]
        if ride is not None:
            (rwi,) = outs[2:]

    gsm = _pack_small([
        jnp.concatenate(g_norm), jnp.stack(g_ln_g), jnp.stack(g_ln_b), jnp.stack(g_bs),
        jnp.concatenate([g_lb0, g_lb1]), jnp.stack(g_on), jnp.stack(g_bf), d_final_g, loss_tile[0, 0]])
    (qsm,) = _exchange_call(PairExchange([], [gsm]), "pair_exchange_small")
    ssm = small_sum(gsm, qsm, "pair_sum_small")
    (rsm,) = _exchange_call(ChipExchange(gathered=[ssm]), "chip_exchange_small")

    small_w = (norm_g, gmlp_ln_g, gmlp_ln_b, gmlp_b_s, hgrn_lb, hgrn_onorm_g, fox_b_f, final_norm_g)
    small_m = (m_norm_g, m_gmlp_ln_g, m_gmlp_ln_b, m_gmlp_b_s, m_hgrn_lb, m_hgrn_onorm_g, m_fox_b_f, m_final_norm_g)
    small_v = (v_norm_g, v_gmlp_ln_g, v_gmlp_ln_b, v_gmlp_b_s, v_hgrn_lb, v_hgrn_onorm_g, v_fox_b_f, v_final_norm_g)
    res_wi = adam_reduce_columns(rwi, w_in, m_w_in, v_w_in, "adam_w_in", swi, chip)
    res_wo = adam_reduce(rwo, w_out, m_w_out, v_w_out, w_out.shape[1], "adam_w_out", own=swo, chip=chip)
    grads = _unpack_small(sum_parts(rsm, "sum_small"))
    names = [name for name, _ in _SMALL if name != "loss"]
    rows = lambda a: a.reshape(1, -1) if a.ndim == 1 else a
    res_sm = adam_small([rows(grads[k]) for k in names], *([rows(a) for a in wmv] for wmv in (small_w, small_m, small_v)))
    res_sm = [grads] + [{k: a.reshape(grads[k].shape) for k, a in zip(names, r, strict=True)} for r in res_sm]
    as_rows = lambda a: a.reshape(1, -1, LANES)
    res_ws = adam_reduce(rws[:, None], as_rows(gmlp_w_s), as_rows(m_gmlp_w_s), as_rows(v_gmlp_w_s), rws.shape[1], "adam_w_s")
    for s, r in zip(res_sm, res_ws, strict=True):
        s["gmlp_w_s"] = r.reshape(gmlp_w_s.shape)

    def group(i):
        s = res_sm[i]
        return [s["norm_g"], res_wi[i], res_wo[i], s["gmlp_ln_g"], s["gmlp_ln_b"], s["gmlp_w_s"], s["gmlp_b_s"],
                s["hgrn_lb"], s["hgrn_onorm_g"], s["fox_b_f"], s["final_norm_g"]]

    return (res_sm[0]["loss"], dx[None], *group(0), *group(1), *group(2), *group(3))
```

```python
import functools

import jax
import jax.numpy as jnp
import numpy as np
from jax import lax
from jax.experimental import pallas as pl
from jax.experimental.pallas import tpu as pltpu

F32 = jnp.float32
BF16 = jnp.bfloat16

NORM_EPS = 1e-6
F_FLOOR = 1e-30
CHUNK = 128
LANES = 128
VMEM_LIMIT = 56 * 1024 * 1024


def _cparams(*sem):
    return pltpu.CompilerParams(dimension_semantics=sem, vmem_limit_bytes=VMEM_LIMIT)


def _dot(a, b, dims=(((1,), (0,)), ((), ())), precision=None):
    return lax.dot_general(a, b, dims, precision=precision, preferred_element_type=F32)


_NT = (((1,), (1,)), ((), ()))
_TN = (((0,), (0,)), ((), ()))


def _bf16_pieces(x, n):
    out, r = [], x
    for i in range(n):
        out.append(r.astype(BF16))
        if i + 1 < n:
            r = r - out[-1].astype(F32)
    return out


@functools.partial(jax.custom_vjp, nondiff_argnums=(2,))
def _times_exact(x, e, n):
    return functools.reduce(jnp.add, [_dot(p, e) for p in _bf16_pieces(x, n)])


def _times_exact_fwd(x, e, n):
    return _times_exact(x, e, n), e


def _times_exact_bwd(n, e, g):
    dx = functools.reduce(jnp.add, [lax.dot_general(p, e, _NT, preferred_element_type=F32) for p in _bf16_pieces(g, n)])
    return dx, jnp.zeros_like(e)


_times_exact.defvjp(_times_exact_fwd, _times_exact_bwd)


@functools.partial(jax.custom_vjp, nondiff_argnums=(2,))
def _exact_times(e, x, n):
    return functools.reduce(jnp.add, [_dot(e, p) for p in _bf16_pieces(x, n)])


def _exact_times_fwd(e, x, n):
    return _exact_times(e, x, n), e


def _exact_times_bwd(n, e, g):
    dx = functools.reduce(jnp.add, [lax.dot_general(e, p, _TN, preferred_element_type=F32) for p in _bf16_pieces(g, n)])
    return jnp.zeros_like(e), dx


_exact_times.defvjp(_exact_times_fwd, _exact_times_bwd)


def _group_mean_matrix(width, group):
    idx = np.arange(width) // group
    return jnp.asarray((idx[:, None] == idx[None, :]).astype(np.float32) / group, BF16)


def _group_ones_matrix(width, group):
    idx = np.arange(width) // group
    return jnp.asarray((idx[:, None] == idx[None, :]).astype(np.float32), BF16)


A_WIDTH = 256
A_GROUPS = 4
A_GDIM = 64


A_ROWS = 512


def _gmlp_chunk(x3, ln_g, ln_b, w_s, bs_t, mean_m, gind):
    n = x3.shape[0] // CHUNK
    u = jax.nn.gelu(x3[:, :A_WIDTH])
    v = jax.nn.gelu(x3[:, A_WIDTH:2 * A_WIDTH])
    z = x3[:, 2 * A_WIDTH:]
    mu = _times_exact(v, mean_m, 2)
    d = v - mu
    var = _times_exact(d * d, mean_m, 2)
    vn = d * lax.rsqrt(var + NORM_EPS) * ln_g + ln_b
    vnb = vn.astype(BF16)
    wide = jnp.concatenate([vnb[i * CHUNK:(i + 1) * CHUNK] for i in range(n)], axis=1)
    row = lax.broadcasted_iota(jnp.int32, (CHUNK, CHUNK), 0)
    col = lax.broadcasted_iota(jnp.int32, (CHUNK, CHUNK), 1)
    causal = row >= col
    lane_g = lax.shift_right_logical(lax.broadcasted_iota(jnp.int32, (CHUNK, n * A_WIDTH), 1), 6) & (A_GROUPS - 1)
    bias = _times_exact(bs_t, gind, 3)
    mixed = jnp.concatenate([bias] * n, axis=1)
    for g in range(A_GROUPS):
        wc = jnp.where(causal, w_s[g], 0.0).astype(BF16)
        mixed = mixed + jnp.where(lane_g == g, _dot(wc, wide), 0.0)
    mixed = jnp.concatenate([mixed[:, i * A_WIDTH:(i + 1) * A_WIDTH] for i in range(n)], axis=0)
    return u * mixed * jax.nn.silu(z)


def _gmlp_consts():
    gind = np.zeros((LANES, A_WIDTH), np.float32)
    for g in range(A_GROUPS):
        gind[g, g * A_GDIM:(g + 1) * A_GDIM] = 1.0
    return _group_mean_matrix(A_WIDTH, A_GDIM), jnp.asarray(gind, BF16)


def _full(shape):
    return pl.BlockSpec(shape, lambda *_: (0,) * len(shape))


def gmlp_fwd(proj, ln_g, ln_b, w_s, bs_t):
    seq = proj.shape[0]
    rows = min(A_ROWS, seq)
    mean_m, gind = _gmlp_consts()

    def body(x_ref, g_ref, b_ref, w_ref, bs_ref, m_ref, gi_ref, y_ref):
        y = _gmlp_chunk(x_ref[...], g_ref[...], b_ref[...], w_ref[...], bs_ref[...], m_ref[...], gi_ref[...])
        y_ref[...] = y.astype(BF16)

    return pl.pallas_call(
        body,
        name="gmlp_fwd",
        grid=(seq // rows,),
        in_specs=[
            pl.BlockSpec((rows, 3 * A_WIDTH), lambda n: (n, 0)),
            _full((1, A_WIDTH)), _full((1, A_WIDTH)), _full((A_GROUPS, CHUNK, CHUNK)), _full((CHUNK, LANES)),
            _full((A_WIDTH, A_WIDTH)), _full((LANES, A_WIDTH)),
        ],
        out_specs=pl.BlockSpec((rows, A_WIDTH), lambda n: (n, 0)),
        out_shape=jax.ShapeDtypeStruct((seq, A_WIDTH), BF16),
        compiler_params=_cparams("parallel"),
    )(proj, ln_g, ln_b, w_s, bs_t, mean_m, gind)


def gmlp_bwd(proj, dy, ln_g, ln_b, w_s, bs_t):
    seq = proj.shape[0]
    rows = min(A_ROWS, seq)
    mean_m, gind = _gmlp_consts()

    def body(x_ref, dy_ref, g_ref, b_ref, w_ref, bs_ref, m_ref, gi_ref, dx_ref, dg_ref, db_ref, dw_ref, dbs_ref):
        fn = functools.partial(_gmlp_chunk, mean_m=m_ref[...], gind=gi_ref[...])
        _, vjp = jax.vjp(fn, x_ref[...], g_ref[...], b_ref[...], w_ref[...], bs_ref[...])
        dx, dg, db, dw, dbs = vjp(dy_ref[...])
        dx_ref[...] = dx.astype(BF16)

        @pl.when(pl.program_id(0) == 0)
        def _():
            dg_ref[...] = jnp.zeros_like(dg_ref)
            db_ref[...] = jnp.zeros_like(db_ref)
            dw_ref[...] = jnp.zeros_like(dw_ref)
            dbs_ref[...] = jnp.zeros_like(dbs_ref)

        dg_ref[...] += dg
        db_ref[...] += db
        dw_ref[...] += dw
        dbs_ref[...] += dbs

    return pl.pallas_call(
        body,
        name="gmlp_bwd",
        grid=(seq // rows,),
        in_specs=[
            pl.BlockSpec((rows, 3 * A_WIDTH), lambda n: (n, 0)),
            pl.BlockSpec((rows, A_WIDTH), lambda n: (n, 0)),
            _full((1, A_WIDTH)), _full((1, A_WIDTH)), _full((A_GROUPS, CHUNK, CHUNK)), _full((CHUNK, LANES)),
            _full((A_WIDTH, A_WIDTH)), _full((LANES, A_WIDTH)),
        ],
        out_specs=[
            pl.BlockSpec((rows, 3 * A_WIDTH), lambda n: (n, 0)),
            _full((1, A_WIDTH)), _full((1, A_WIDTH)), _full((A_GROUPS, CHUNK, CHUNK)), _full((CHUNK, LANES)),
        ],
        out_shape=[
            jax.ShapeDtypeStruct((seq, D_INT), BF16),
            jax.ShapeDtypeStruct((1, A_WIDTH), F32), jax.ShapeDtypeStruct((1, A_WIDTH), F32),
            jax.ShapeDtypeStruct((A_GROUPS, CHUNK, CHUNK), F32), jax.ShapeDtypeStruct((CHUNK, LANES), F32),
        ],
        compiler_params=_cparams("arbitrary"),
    )(proj, dy, ln_g, ln_b, w_s, bs_t, mean_m, gind)


B_WIDTH = 256
B_HEADS = 4
B_KDIM = 64
B_LEVELS = (64, 32, 16, 8, 4, 2, 1)


def _hgrn_consts():
    t = np.arange(CHUNK)
    u = t[None, :]
    mats = [np.tril(np.ones((CHUNK, CHUNK), np.float32))]
    for m in B_LEVELS:
        p = (t // (2 * m)) * (2 * m) + m - 1
        right = (t % (2 * m)) >= m
        sel = np.where(right[:, None], (u > p[:, None]) & (u <= t[:, None]), (u > t[:, None]) & (u <= p[:, None]))
        mats.append(sel.astype(np.float32))
    return jnp.asarray(np.concatenate(mats, 0), BF16), _group_ones_matrix(B_WIDTH, B_KDIM)


def _hgrn_lower_bound(lb0, lb1, layer):
    mx = jnp.maximum(lb0, lb1)
    e0 = jnp.exp(lb0 - mx)
    e1 = jnp.exp(lb1 - mx)
    p0 = e0 / (e0 + e1)
    p1 = e1 / (e0 + e1)
    cs = p0 if layer == 0 else p0 + p1
    return jnp.clip(cs - p0, 0.0, 1.0 - 1e-6)


def _hgrn_chunk(x4, st, lb0, lb1, onorm, layer, tstack, ones_bd):
    q_raw, fl, v, zg = (x4[:, i * B_WIDTH:(i + 1) * B_WIDTH] for i in range(4))
    lb = _hgrn_lower_bound(lb0, lb1, layer)
    q = jax.nn.silu(q_raw) * (B_KDIM ** -0.5)
    f = lb + (1.0 - lb) * jax.nn.sigmoid(fl)
    logf = jnp.log(jnp.maximum(f, F_FLOOR))
    k = (1.0 - lb) * jax.nn.sigmoid(-fl)
    b = _exact_times(tstack[:CHUNK], logf, 3)
    dall = jnp.concatenate([b, _exact_times(tstack[CHUNK:], logf, 2)], axis=0)
    b_last = jnp.sum(logf, axis=0, keepdims=True)
    vb = v.astype(BF16)

    lane_h = lax.shift_right_logical(lax.broadcasted_iota(jnp.int32, (CHUNK, B_WIDTH), 1), 6)
    row = lax.broadcasted_iota(jnp.int32, (CHUNK, B_WIDTH), 0)
    srow = lax.broadcasted_iota(jnp.int32, (B_HEADS * CHUNK, CHUNK), 0) & (CHUNK - 1)
    scol = lax.broadcasted_iota(jnp.int32, (B_HEADS * CHUNK, CHUNK), 1)

    def heads_on_rows(a):
        return jnp.concatenate([jnp.where(lane_h == h, a, 0.0) for h in range(B_HEADS)], axis=0)

    def heads_from_rows(r):
        out = jnp.where(lane_h == 0, r[:CHUNK], 0.0)
        for h in range(1, B_HEADS):
            out = out + jnp.where(lane_h == h, r[h * CHUNK:(h + 1) * CHUNK], 0.0)
        return out

    o = lax.dot_general((q * jnp.exp(b)).astype(BF16), st.astype(BF16), _NT, preferred_element_type=F32)
    scores = jnp.zeros((B_HEADS * CHUNK, CHUNK), F32)
    for li, m in enumerate(B_LEVELS):
        e = jnp.exp(dall[(li + 1) * CHUNK:(li + 2) * CHUNK])
        right = (row & (2 * m - 1)) >= m
        qt = jnp.where(right, q * e, 0.0)
        kt = jnp.where(right, 0.0, k * e)
        sc = lax.dot_general(heads_on_rows(qt).astype(BF16), kt.astype(BF16), _NT, preferred_element_type=F32)
        sh = int(np.log2(2 * m))
        same = lax.shift_right_logical(srow, sh) == lax.shift_right_logical(scol, sh)
        scores = scores + jnp.where(same, sc, 0.0)
    o = o + heads_from_rows(_dot(scores.astype(BF16), vb))
    o = o + _times_exact(q * k, ones_bd, 2) * v

    kv = lax.dot_general(vb, (k * jnp.exp(b_last - b)).astype(BF16), _TN, preferred_element_type=F32)
    st_new = st * jnp.exp(b_last) + jnp.where(ones_bd > 0.5, kv, 0.0)

    ms = _times_exact(o * o, ones_bd, 2) * (1.0 / B_KDIM)
    y = o * lax.rsqrt(ms + NORM_EPS) * onorm * jax.nn.silu(zg)
    return y, st_new


B_ROWS = 256


def _hgrn_rows(x4, st, lb0, lb1, onorm, layer, tstack, ones_bd):
    ys = []
    for i in range(x4.shape[0] // CHUNK):
        y, st = _hgrn_chunk(x4[i * CHUNK:(i + 1) * CHUNK], st, lb0, lb1, onorm, layer, tstack, ones_bd)
        ys.append(y)
    return jnp.concatenate(ys, axis=0), st


def hgrn_fwd(proj, lb0, lb1, onorm, layer):
    seq = proj.shape[0]
    rows = min(B_ROWS, seq)
    nc = seq // rows
    tstack, ones_bd = _hgrn_consts()

    def body(x_ref, lb0_ref, lb1_ref, on_ref, t_ref, e_ref, y_ref, st_out_ref, st_ref):
        @pl.when(pl.program_id(0) == 0)
        def _():
            st_ref[...] = jnp.zeros_like(st_ref)

        st = st_ref[...]
        st_out_ref[0] = st
        y, st_new = _hgrn_rows(x_ref[...], st, lb0_ref[...], lb1_ref[...], on_ref[...], layer, t_ref[...], e_ref[...])
        y_ref[...] = y.astype(BF16)
        st_ref[...] = st_new

    return pl.pallas_call(
        body,
        name=f"hgrn_fwd_{layer}",
        grid=(nc,),
        in_specs=[
            pl.BlockSpec((rows, 4 * B_WIDTH), lambda n: (n, 1)),
            _full((1, B_WIDTH)), _full((1, B_WIDTH)), _full((1, B_WIDTH)),
            _full(((len(B_LEVELS) + 1) * CHUNK, CHUNK)), _full((B_WIDTH, B_WIDTH)),
        ],
        out_specs=[
            pl.BlockSpec((rows, B_WIDTH), lambda n: (n, 0)),
            pl.BlockSpec((1, B_WIDTH, B_WIDTH), lambda n: (n, 0, 0)),
        ],
        out_shape=[jax.ShapeDtypeStruct((seq, B_WIDTH), BF16), jax.ShapeDtypeStruct((nc, B_WIDTH, B_WIDTH), F32)],
        scratch_shapes=[pltpu.VMEM((B_WIDTH, B_WIDTH), F32)],
        compiler_params=_cparams("arbitrary"),
    )(proj, lb0, lb1, onorm, tstack, ones_bd)


def hgrn_bwd(proj, states, dy, lb0, lb1, onorm, layer, dproj):
    seq = proj.shape[0]
    rows = min(B_ROWS, seq)
    nc = seq // rows
    tstack, ones_bd = _hgrn_consts()

    def body(x_ref, st_in_ref, dy_ref, lb0_ref, lb1_ref, on_ref, t_ref, e_ref, _, dx_ref, d0_ref, d1_ref, don_ref, dst_ref):
        @pl.when(pl.program_id(0) == 0)
        def _():
            dst_ref[...] = jnp.zeros_like(dst_ref)
            d0_ref[...] = jnp.zeros_like(d0_ref)
            d1_ref[...] = jnp.zeros_like(d1_ref)
            don_ref[...] = jnp.zeros_like(don_ref)

        fn = functools.partial(_hgrn_rows, layer=layer, tstack=t_ref[...], ones_bd=e_ref[...])
        _, vjp = jax.vjp(fn, x_ref[...], st_in_ref[0], lb0_ref[...], lb1_ref[...], on_ref[...])
        dx, dst, d0, d1, don = vjp((dy_ref[...], dst_ref[...]))
        dx_ref[...] = dx.astype(BF16)
        dst_ref[...] = dst
        d0_ref[...] += d0
        d1_ref[...] += d1
        don_ref[...] += don

    rev = lambda n: nc - 1 - n
    return pl.pallas_call(
        body,
        name=f"hgrn_bwd_{layer}",
        grid=(nc,),
        in_specs=[
            pl.BlockSpec((rows, 4 * B_WIDTH), lambda n: (rev(n), 1)),
            pl.BlockSpec((1, B_WIDTH, B_WIDTH), lambda n: (rev(n), 0, 0)),
            pl.BlockSpec((rows, B_WIDTH), lambda n: (rev(n), 1)),
            _full((1, B_WIDTH)), _full((1, B_WIDTH)), _full((1, B_WIDTH)),
            _full(((len(B_LEVELS) + 1) * CHUNK, CHUNK)), _full((B_WIDTH, B_WIDTH)), _ANY,
        ],
        out_specs=[
            pl.BlockSpec((rows, 4 * B_WIDTH), lambda n: (rev(n), 1)),
            _full((1, B_WIDTH)), _full((1, B_WIDTH)), _full((1, B_WIDTH)),
        ],
        out_shape=[jax.ShapeDtypeStruct(dproj.shape, BF16)] + [jax.ShapeDtypeStruct((1, B_WIDTH), F32)] * 3,
        input_output_aliases={8: 0},
        scratch_shapes=[pltpu.VMEM((B_WIDTH, B_WIDTH), F32)],
        compiler_params=_cparams("arbitrary"),
    )(proj, states, dy, lb0, lb1, onorm, tstack, ones_bd, dproj)


D_MODEL = 1024
D_INT = 4096


def _rms_stats(xf):
    r = lax.rsqrt(jnp.mean(xf * xf, axis=-1, keepdims=True) + NORM_EPS)
    return r, xf * r


def _rms_bwd(dy, g, r, xh):
    u = dy * g
    return r * (u - xh * jnp.mean(u * xh, axis=-1, keepdims=True))


def inproj(x, g, w, layer):
    seq = x.shape[0]
    tm = min(seq, 512)

    def body(x_ref, g_ref, w_ref, p_ref, h_ref):
        _, xh = _rms_stats(x_ref[...])
        h = (xh * g_ref[...]).astype(BF16)
        h_ref[...] = h
        p_ref[...] = _dot(h, w_ref[0])

    return pl.pallas_call(
        body,
        name="inproj",
        grid=(seq // tm,),
        in_specs=[
            pl.BlockSpec((tm, D_MODEL), lambda i: (i, 0)),
            _full((1, D_MODEL)),
            pl.BlockSpec((1, D_MODEL, D_INT), lambda i: (layer, 0, 0)),
        ],
        out_specs=[pl.BlockSpec((tm, D_INT), lambda i: (i, 0)), pl.BlockSpec((tm, D_MODEL), lambda i: (i, 0))],
        out_shape=[jax.ShapeDtypeStruct((seq, D_INT), F32), jax.ShapeDtypeStruct((seq, D_MODEL), BF16)],
        compiler_params=_cparams("parallel"),
    )(x, g, w)


def outproj(x, ya, yb, o, proj, wo, layer):
    seq = x.shape[0]
    tm = min(seq, 512)
    blk = wo.shape[2]

    def body(x_ref, ya_ref, yb_ref, o_ref, z_ref, w_ref, xn_ref, y_ref):
        yc = (o_ref[...] * jax.nn.silu(z_ref[...])).astype(BF16)
        y = jnp.concatenate([ya_ref[...], yb_ref[...], yc], axis=1)
        y_ref[...] = y
        w = jnp.concatenate([w_ref[d, 0] for d in range(N_DEV)], axis=0)
        xn_ref[...] = x_ref[...] + _dot(y, w)

    return pl.pallas_call(
        body,
        name="outproj",
        grid=(seq // tm,),
        in_specs=[
            pl.BlockSpec((tm, D_MODEL), lambda i: (i, 0)),
            pl.BlockSpec((tm, 256), lambda i: (i, 0)),
            pl.BlockSpec((tm, 256), lambda i: (i, 0)),
            pl.BlockSpec((tm, 512), lambda i: (i, 0)),
            pl.BlockSpec((tm, 512), lambda i: (i, 7)),
            pl.BlockSpec((N_DEV, 1, blk, D_MODEL), lambda i: (0, layer, 0, 0)),
        ],
        out_specs=[pl.BlockSpec((tm, D_MODEL), lambda i: (i, 0)), pl.BlockSpec((tm, D_MODEL), lambda i: (i, 0))],
        out_shape=[jax.ShapeDtypeStruct((seq, D_MODEL), F32), jax.ShapeDtypeStruct((seq, D_MODEL), BF16)],
        compiler_params=_cparams("parallel"),
    )(x, ya, yb, o, proj, wo)


def outproj_bwd(dx, y, wo, layer):
    seq = dx.shape[0]
    ts = min(seq, 512)
    blk = wo.shape[2]

    def body(dx_ref, y_ref, w_ref, dy_ref, dw_ref):
        @pl.when(pl.program_id(0) == 0)
        def _():
            dw_ref[...] = jnp.zeros_like(dw_ref)

        dxb = dx_ref[...].astype(BF16)
        w = jnp.concatenate([w_ref[d, 0] for d in range(N_DEV)], axis=0)
        dy_ref[...] = lax.dot_general(dxb, w, _NT, preferred_element_type=F32)
        dw = lax.dot_general(y_ref[...], dxb, _TN, preferred_element_type=F32)
        for d in range(N_DEV):
            dw_ref[d % 2, d // 2] += dw[d * blk:(d + 1) * blk]

    return pl.pallas_call(
        body,
        name="outproj_bwd",
        grid=(seq // ts,),
        in_specs=[
            pl.BlockSpec((ts, D_MODEL), lambda i: (i, 0)),
            pl.BlockSpec((ts, D_MODEL), lambda i: (i, 0)),
            pl.BlockSpec((N_DEV, 1, blk, D_MODEL), lambda i: (0, layer, 0, 0)),
        ],
        out_specs=[pl.BlockSpec((ts, D_MODEL), lambda i: (i, 0)),
                   pl.BlockSpec((2, N_CHIP, blk, D_MODEL), lambda i: (0, 0, 0, 0))],
        out_shape=[jax.ShapeDtypeStruct((seq, D_MODEL), F32), jax.ShapeDtypeStruct((2, N_CHIP, blk, D_MODEL), F32)],
        compiler_params=_cparams("arbitrary"),
    )(dx, y, wo)


C_QKV = (2048, 3584)


def _dproj_parts(dp_ref, dqkv_refs, rows):
    lo, hi = C_QKV
    step = (hi - lo) // len(dqkv_refs)
    return ([(0, dp_ref.at[rows, 0:lo])] + [(lo + i * step, r.at[rows, :]) for i, r in enumerate(dqkv_refs)]
            + [(hi, dp_ref.at[rows, hi:D_INT])])


def inproj_bwd_x(dproj, dqkv, w, x, g, dx_in, layer, carried=None):
    seq = x.shape[0]
    tm = min(seq, 512)

    def body(dp_ref, dq_ref, dk_ref, dv_ref, w_ref, x_ref, g_ref, dxin_ref, dx_ref, dg_ref):
        @pl.when(pl.program_id(0) == 0)
        def _():
            dg_ref[...] = jnp.zeros_like(dg_ref)

        dh = None
        for at, part in _dproj_parts(dp_ref, (dq_ref, dk_ref, dv_ref), slice(None)):
            term = lax.dot_general(part[...], w_ref[0, :, at:at + part.shape[1]], _NT, preferred_element_type=F32)
            dh = term if dh is None else dh + term
        r, xh = _rms_stats(x_ref[...])
        dg_ref[...] += jnp.sum(dh * xh, axis=0, keepdims=True)
        dx_ref[...] = dxin_ref[...] + _rms_bwd(dh, g_ref[...], r, xh)

    third = lambda: pl.BlockSpec((tm, C_WIDTH), lambda i: (i, 0))
    return _call_carrying(
        carried, body, (dproj, *dqkv, w, x, g, dx_in),
        name="inproj_bwd_x",
        grid=(seq // tm,),
        in_specs=[
            pl.BlockSpec((tm, D_INT), lambda i: (i, 0)), third(), third(), third(),
            pl.BlockSpec((1, D_MODEL, D_INT), lambda i: (layer, 0, 0)),
            pl.BlockSpec((tm, D_MODEL), lambda i: (i, 0)),
            _full((1, D_MODEL)),
            pl.BlockSpec((tm, D_MODEL), lambda i: (i, 0)),
        ],
        out_specs=[pl.BlockSpec((tm, D_MODEL), lambda i: (i, 0)), _full((1, D_MODEL))],
        out_shape=[jax.ShapeDtypeStruct((seq, D_MODEL), F32), jax.ShapeDtypeStruct((1, D_MODEL), F32)],
        scratch_shapes=[], semantics=("arbitrary",),
    )


def inproj_bwd_w(h, dproj, dqkv):
    seq = h.shape[0]
    ts, tn = min(seq, 512), 512

    def body(h_ref, dp_ref, dq_ref, dk_ref, dv_ref, dw_ref):
        @pl.when(pl.program_id(0) == 0)
        def _():
            dw_ref[...] = jnp.zeros_like(dw_ref)

        ht = h_ref[...].T
        for at, part in _dproj_parts(dp_ref, (dq_ref, dk_ref, dv_ref), slice(None)):
            for c in range(0, part.shape[1], tn):
                dw_ref[0, :, at + c:at + c + tn] += _dot(ht, part[:, c:c + tn])

    third = lambda: pl.BlockSpec((ts, C_WIDTH), lambda s: (s, 0))
    return pl.pallas_call(
        body,
        name="inproj_bwd_w",
        grid=(seq // ts,),
        in_specs=[pl.BlockSpec((ts, D_MODEL), lambda s: (s, 0)), pl.BlockSpec((ts, D_INT), lambda s: (s, 0)),
                  third(), third(), third()],
        out_specs=_full((1, D_MODEL, D_INT)),
        out_shape=jax.ShapeDtypeStruct((1, D_MODEL, D_INT), F32),
        compiler_params=_cparams("arbitrary"),
    )(h, dproj, *dqkv)


N_IN = 3848


def _internal_of(col):
    return col if col < 768 else (col + 256 if col < 3840 else 768 + col - 3840)


def _column_runs(n_shard):
    runs = []
    for d in range(N_IN // n_shard):
        mine = []
        for j in range(n_shard):
            ci = _internal_of(d * n_shard + j)
            if mine and mine[-1][0] + mine[-1][1] == ci:
                mine[-1][1] += 1
            else:
                mine.append([ci, 1, j])
        runs.append(mine)
    return runs


def assemble_w_in(wi_all):
    n_dev, depth, _, n_shard = wi_all.shape
    tr = 256
    pieces = [[] for _ in range(D_INT // LANES)]
    for d, mine in enumerate(_column_runs(n_shard)):
        for ci, ln, off in mine:
            while ln > 0:
                blk, at = divmod(ci, LANES)
                take = min(ln, LANES - at)
                pieces[blk].append((at, take, d, off))
                ci, ln, off = ci + take, ln - take, off + take

    def body(x_ref, o_ref):
        for blk, parts in enumerate(pieces):
            vals, at = [], 0
            for start, ln, d, off in sorted(parts):
                if start > at:
                    vals.append(jnp.zeros((tr, start - at), BF16))
                vals.append(x_ref[d, 0, :, off:off + ln])
                at = start + ln
            if at < LANES:
                vals.append(jnp.zeros((tr, LANES - at), BF16))
            o_ref[0, :, blk * LANES:(blk + 1) * LANES] = vals[0] if len(vals) == 1 else jnp.concatenate(vals, axis=1)

    return pl.pallas_call(
        body,
        name="assemble_w_in",
        grid=(depth, D_MODEL // tr),
        in_specs=[pl.BlockSpec((n_dev, 1, tr, n_shard), lambda l, r: (0, l, r, 0))],
        out_specs=pl.BlockSpec((1, tr, D_INT), lambda l, r: (l, r, 0)),
        out_shape=jax.ShapeDtypeStruct((depth, D_MODEL, D_INT), BF16),
        compiler_params=_cparams("parallel", "parallel"),
    )(wi_all)


def split_w_in_grad(dwi, n_shard):
    depth = dwi.shape[0]
    tr = 256
    runs = _column_runs(n_shard)

    def body(x_ref, o_ref):
        for d, mine in enumerate(runs):
            for ci, ln, off in mine:
                o_ref[d % 2, d // 2, 0, :, off:off + ln] = x_ref[0, :, ci:ci + ln]

    return pl.pallas_call(
        body,
        name="split_w_in_grad",
        grid=(depth, D_MODEL // tr),
        in_specs=[pl.BlockSpec((1, tr, D_INT), lambda l, r: (l, r, 0))],
        out_specs=pl.BlockSpec((2, N_CHIP, 1, tr, n_shard), lambda l, r: (0, 0, l, r, 0)),
        out_shape=jax.ShapeDtypeStruct((2, N_CHIP, depth, D_MODEL, n_shard), F32),
        compiler_params=_cparams("parallel", "parallel"),
    )(dwi)


def final_loss(x, g, tgt):
    seq = x.shape[0]
    tm = min(seq, 512)

    def body(x_ref, g_ref, t_ref, dx_ref, dg_ref, loss_ref):
        @pl.when(pl.program_id(0) == 0)
        def _():
            dg_ref[...] = jnp.zeros_like(dg_ref)
            loss_ref[...] = jnp.zeros_like(loss_ref)

        g = g_ref[...]
        r, xh = _rms_stats(x_ref[...])
        err = xh * g - t_ref[...]
        sq = jnp.sum(jnp.sum(err * err, axis=1, keepdims=True), axis=0, keepdims=True)
        loss_ref[...] += jnp.broadcast_to(sq * (0.5 / D_MODEL), loss_ref.shape)
        dout = err * (1.0 / D_MODEL)
        dg_ref[...] += jnp.sum(dout * xh, axis=0, keepdims=True)
        dx_ref[...] = _rms_bwd(dout, g, r, xh)

    return pl.pallas_call(
        body,
        name="final_loss",
        grid=(seq // tm,),
        in_specs=[pl.BlockSpec((tm, D_MODEL), lambda i: (i, 0)), _full((1, D_MODEL)), pl.BlockSpec((tm, D_MODEL), lambda i: (i, 0))],
        out_specs=[pl.BlockSpec((tm, D_MODEL), lambda i: (i, 0)), _full((1, D_MODEL)), _full((8, LANES))],
        out_shape=[jax.ShapeDtypeStruct((seq, D_MODEL), F32), jax.ShapeDtypeStruct((1, D_MODEL), F32), jax.ShapeDtypeStruct((8, LANES), F32)],
        compiler_params=_cparams("arbitrary"),
    )(x, g, tgt)


C_WIDTH = 512
C_HEADS = 8
C_HDIM = 64
C_PAIRS = C_HEADS // 2
C_BQ = 512
C_TAIL = 16
C_KG = 4


def _split3(x):
    hi = x.astype(BF16)
    r = x - hi.astype(F32)
    mid = r.astype(BF16)
    return hi, mid, (r - mid.astype(F32)).astype(BF16)


def _piece_selectors():
    sel = np.zeros((C_HEADS, 3 * LANES, LANES), np.float32)
    for p in range(C_PAIRS):
        for e in range(2):
            for t in range(3):
                sel[2 * p + e, t * LANES + 2 * p + e, 3 * e + t] = -1.0
    return sel


def fox_prep(proj, bf_row):
    seq = proj.shape[0]
    nblk = seq // CHUNK
    tril = jnp.asarray(np.tril(np.ones((CHUNK, CHUNK), np.float32)), BF16)
    sel = jnp.asarray(_piece_selectors(), BF16)
    rows_t = CHUNK + C_TAIL

    def body(fl_ref, q_ref, k_ref, v_ref, bf_ref, l_ref, sel_ref, ka_ref, va_ref, vt_ref, kt_ref, qt_ref, qa_ref, carry_ref):
        @pl.when(pl.program_id(0) == 0)
        def _():
            carry_ref[...] = jnp.zeros_like(carry_ref)

        lf = jax.nn.log_sigmoid(fl_ref[:, :LANES] + bf_ref[...])
        c = _exact_times(l_ref[...], lf, 3) + carry_ref[...]
        carry_ref[...] += jnp.sum(lf, axis=0, keepdims=True)
        c3 = jnp.concatenate(_split3(c), axis=1)
        lane = lax.broadcasted_iota(jnp.int32, (CHUNK, LANES), 1)
        row = lax.broadcasted_iota(jnp.int32, (CHUNK, LANES), 0)
        r16 = lax.broadcasted_iota(jnp.int32, (C_TAIL, 2 * CHUNK), 0)
        l16 = lax.broadcasted_iota(jnp.int32, (C_TAIL, 2 * CHUNK), 1)
        zero = jnp.zeros((CHUNK, LANES), BF16)
        one = jnp.ones((CHUNK, LANES), BF16)

        def by_keys(x, right_a, right_b):
            xb = x.astype(BF16)
            top = jnp.concatenate([jnp.where(lane < C_HDIM, xb, zero), right_a], axis=1)
            return jnp.concatenate([top, jnp.concatenate([jnp.where(lane < C_HDIM, zero, xb), right_b], axis=1)], axis=0)

        def by_lanes(x, tail):
            xt = x.T.astype(BF16)
            main = jnp.concatenate([jnp.where(row < C_HDIM, xt, zero), jnp.where(row < C_HDIM, zero, xt)], axis=1)
            return jnp.concatenate([main, tail], axis=0)

        for p in range(C_PAIRS):
            cols = slice(p * LANES, (p + 1) * LANES)
            q2, k2, v2 = q_ref[:, cols] * (C_HDIM ** -0.5), k_ref[:, cols], v_ref[:, cols]
            negc = [_dot(c3, sel_ref[2 * p + e]).astype(BF16) for e in range(2)]
            ones3 = [jnp.where((lane >= 3 * e) & (lane < 3 * e + 3), one, zero) for e in range(2)]
            tail = jnp.where(((r16 == 2 * p) & (l16 < CHUNK)) | ((r16 == 2 * p + 1) & (l16 >= CHUNK)), 1.0, 0.0).astype(BF16)
            ka_ref[p] = by_keys(k2, negc[0], negc[1])
            va_ref[p] = by_keys(v2, ones3[0], ones3[1])
            kt_ref[p] = by_lanes(k2, tail)
            vt_ref[p] = by_lanes(v2, tail)
            qt_ref[p] = jnp.concatenate([q2.T.astype(BF16), jnp.where(row < 6, one, zero)], axis=0)
            qa_ref[p] = jnp.concatenate([q2.astype(BF16), jnp.where((lane == 2 * p) | (lane == 2 * p + 1), one, zero)], axis=1)

    wide = lambda j: pl.BlockSpec((CHUNK, C_WIDTH), lambda n: (n, j))
    by_rows = pl.BlockSpec((C_PAIRS, 2 * CHUNK, 2 * CHUNK), lambda n: (0, n, 0))
    by_cols = pl.BlockSpec((C_PAIRS, rows_t, 2 * CHUNK), lambda n: (0, 0, n))
    return pl.pallas_call(
        body,
        name="fox_prep",
        grid=(nblk,),
        in_specs=[pl.BlockSpec((CHUNK, 256), lambda n: (n, 3)), wide(4), wide(5), wide(6), _full((1, LANES)),
                  _full((CHUNK, CHUNK)), _full((C_HEADS, 3 * LANES, LANES))],
        out_specs=[by_rows, by_rows, by_cols, by_cols,
                   pl.BlockSpec((C_PAIRS, 2 * CHUNK, CHUNK), lambda n: (0, 0, n)),
                   pl.BlockSpec((C_PAIRS, CHUNK, 2 * CHUNK), lambda n: (0, n, 0))],
        out_shape=[jax.ShapeDtypeStruct((C_PAIRS, 2 * seq, 2 * CHUNK), BF16)] * 2
        + [jax.ShapeDtypeStruct((C_PAIRS, rows_t, 2 * seq), BF16)] * 2
        + [jax.ShapeDtypeStruct((C_PAIRS, 2 * CHUNK, seq), BF16), jax.ShapeDtypeStruct((C_PAIRS, seq, 2 * CHUNK), BF16)],
        scratch_shapes=[pltpu.VMEM((1, LANES), F32)],
        compiler_params=_cparams("arbitrary"),
    )(proj, proj, proj, proj, bf_row, tril, sel)


def _visible(shape, key0, query0):
    row = lax.broadcasted_iota(jnp.int32, shape, 0)
    key = key0 + lax.shift_left(lax.shift_right_logical(row, 8), 7) + (row & (CHUNK - 1))
    return key <= query0 + lax.broadcasted_iota(jnp.int32, shape, 1)


def _rows_ab(a, b, n):
    return jnp.concatenate([jnp.broadcast_to(a, (C_HDIM, n)), jnp.broadcast_to(b, (C_HDIM, n))], axis=0)


def _call_carrying(ex, body, operands, *, name, grid, in_specs, out_specs, out_shape, scratch_shapes, semantics=None):
    if ex is None:
        semantics = semantics or ("parallel", *["arbitrary"] * (len(grid) - 1))
        return pl.pallas_call(body, name=name, grid=grid, in_specs=in_specs, out_specs=out_specs, out_shape=out_shape,
                              scratch_shapes=scratch_shapes, compiler_params=_cparams(*semantics))(*operands)
    n_in, n_out = len(in_specs), len(out_specs)

    def wrapped(*refs):
        own, parts = _carried_refs(refs, n_in, n_out, ex)
        ids = [pl.program_id(a) for a in range(len(grid))]
        pl.when(functools.reduce(jnp.logical_and, [i == 0 for i in ids]))(lambda: ex.start(*parts))
        body(*own)
        pl.when(functools.reduce(jnp.logical_and, [i == g - 1 for i, g in zip(ids, grid)]))(lambda: ex.finish(*parts))

    return pl.pallas_call(
        wrapped, name=name, grid=grid,
        in_specs=list(in_specs) + [_ANY] * len(ex.inputs), out_specs=list(out_specs) + [_ANY] * len(ex.out_shape),
        out_shape=list(out_shape) + list(ex.out_shape), scratch_shapes=list(scratch_shapes) + list(ex.scratch),
        input_output_aliases={n_in + i: n_out + o for i, o in getattr(ex, "aliases", {}).items()},
        compiler_params=_cparams(*["arbitrary"] * len(grid)),
    )(*operands, *ex.inputs)


def fox_fwd(qt, ka, vt, carried=None):
    seq = qt.shape[2]
    nblk = seq // CHUNK
    bq = min(C_BQ, seq)
    grp = bq // CHUNK
    rows_t = CHUNK + C_TAIL

    def body(qt_ref, ka_ref, vt_ref, o_ref, lse_ref, acc_ref, s_ref):
        p, i = pl.program_id(0), pl.program_id(1)
        qtile = qt_ref[0]
        r16 = lax.broadcasted_iota(jnp.int32, (C_TAIL, bq), 0)

        def scores(t):
            at = pl.multiple_of(t * grp * 2 * CHUNK, 2 * CHUNK)
            return _dot(ka_ref[0, pl.ds(at, grp * 2 * CHUNK), :], qtile)

        def group(t, m, masked):
            ma, mb = m
            at = pl.multiple_of(t * grp * 2 * CHUNK, 2 * CHUNK)
            s = s_ref[...]
            if masked:
                s = jnp.where(_visible(s.shape, t * bq, i * bq), s, -jnp.inf)
            sa = [s[g * 2 * CHUNK:g * 2 * CHUNK + CHUNK] for g in range(grp)]
            sb = [s[g * 2 * CHUNK + CHUNK:(g + 1) * 2 * CHUNK] for g in range(grp)]
            na, nb = ma, mb
            for g in range(grp):
                na = jnp.maximum(na, jnp.max(sa[g], axis=0, keepdims=True))
                nb = jnp.maximum(nb, jnp.max(sb[g], axis=0, keepdims=True))
            al_a, al_b = jnp.exp(ma - na), jnp.exp(mb - nb)
            pt = jnp.concatenate([jnp.exp(x - n) for g in range(grp) for x, n in ((sa[g], na), (sb[g], nb))], axis=0)
            pv = _dot(vt_ref[0, :, pl.ds(at, grp * 2 * CHUNK)], pt.astype(BF16))
            tail = jnp.where(r16 == 2 * p, al_a, jnp.where(r16 == 2 * p + 1, al_b, 1.0))
            acc_ref[...] = acc_ref[...] * jnp.concatenate([_rows_ab(al_a, al_b, bq), tail], axis=0) + pv
            return na, nb

        def step(t, m):
            s_next = scores(t + 1)
            m = group(t, m, False)
            s_ref[...] = s_next
            return m

        acc_ref[...] = jnp.zeros_like(acc_ref)
        s_ref[...] = scores(0)
        m = (jnp.full((1, bq), -jnp.inf, F32), jnp.full((1, bq), -jnp.inf, F32))
        m = lax.fori_loop(0, i, step, m)
        ma, mb = group(i, m, True)
        tailv = acc_ref[CHUNK:rows_t, :]
        la = jnp.sum(jnp.where(r16 == 2 * p, tailv, 0.0), axis=0, keepdims=True)
        lb = jnp.sum(jnp.where(r16 == 2 * p + 1, tailv, 0.0), axis=0, keepdims=True)
        o_ref[...] = (acc_ref[0:CHUNK, :] * _rows_ab(1.0 / la, 1.0 / lb, bq)).T
        lse_ref[0, 0:1, :] = ma + jnp.log(la)
        lse_ref[0, 1:2, :] = mb + jnp.log(lb)

    return _call_carrying(
        carried, body, (qt, ka, vt),
        name="fox_fwd",
        grid=(C_PAIRS, seq // bq),
        in_specs=[
            pl.BlockSpec((1, 2 * CHUNK, bq), lambda p, i: (p, 0, i)),
            pl.BlockSpec((1, 2 * seq, 2 * CHUNK), lambda p, i: (p, 0, 0)),
            pl.BlockSpec((1, rows_t, 2 * seq), lambda p, i: (p, 0, 0)),
        ],
        out_specs=[pl.BlockSpec((bq, LANES), lambda p, i: (i, p)), pl.BlockSpec((1, 2, bq), lambda p, i: (p, 0, i))],
        out_shape=[jax.ShapeDtypeStruct((seq, C_WIDTH), F32), jax.ShapeDtypeStruct((C_PAIRS, 2, seq), F32)],
        scratch_shapes=[pltpu.VMEM((rows_t, bq), F32), pltpu.VMEM((grp * 2 * CHUNK, bq), F32)],
    )


def fox_bwd_prep(dy, o, proj, dproj):
    seq = o.shape[0]
    ind = np.zeros((C_WIDTH, LANES), np.float32)
    for h in range(C_HEADS):
        ind[h * C_HDIM:(h + 1) * C_HDIM, h] = 1.0
    ind = jnp.asarray(ind, BF16)
    sel = _piece_selectors()
    sel = jnp.asarray(np.stack([sel[2 * p].T + sel[2 * p + 1].T for p in range(C_PAIRS)]), BF16)

    def body(dy_ref, o_ref, z_ref, ind_ref, sel_ref, _, do_ref, dz_ref, dot_ref):
        dy_c, o_v, z = dy_ref[...], o_ref[...], z_ref[...]
        sg = jax.nn.sigmoid(z)
        do = dy_c * (z * sg)
        do_ref[...] = do.astype(BF16)
        dz_ref[...] = (dy_c * o_v * (sg * (1.0 + z * (1.0 - sg)))).astype(BF16)
        prod = do * o_v
        hi = prod.astype(BF16)
        lo = (prod - hi.astype(F32)).astype(BF16)
        delta = _dot(hi, ind_ref[...]) + _dot(lo, ind_ref[...])
        d3 = jnp.concatenate(_split3(delta.T), axis=0)
        for p in range(C_PAIRS):
            tail = _dot(sel_ref[p], d3).astype(BF16)
            dot_ref[p] = jnp.concatenate([do[:, p * LANES:(p + 1) * LANES].T.astype(BF16), tail], axis=0)

    return pl.pallas_call(
        body,
        name="fox_bwd_prep",
        grid=(seq // CHUNK,),
        in_specs=[
            pl.BlockSpec((CHUNK, C_WIDTH), lambda i: (i, 1)),
            pl.BlockSpec((CHUNK, C_WIDTH), lambda i: (i, 0)),
            pl.BlockSpec((CHUNK, C_WIDTH), lambda i: (i, 7)),
            _full((C_WIDTH, LANES)), _full((C_PAIRS, LANES, 3 * LANES)), _ANY,
        ],
        out_specs=[
            pl.BlockSpec((CHUNK, C_WIDTH), lambda i: (i, 0)),
            pl.BlockSpec((CHUNK, C_WIDTH), lambda i: (i, 7)),
            pl.BlockSpec((C_PAIRS, 2 * CHUNK, CHUNK), lambda i: (0, 0, i)),
        ],
        out_shape=[jax.ShapeDtypeStruct((seq, C_WIDTH), BF16), jax.ShapeDtypeStruct(dproj.shape, BF16),
                   jax.ShapeDtypeStruct((C_PAIRS, 2 * CHUNK, seq), BF16)],
        input_output_aliases={5: 1},
        compiler_params=_cparams("parallel"),
    )(dy, o, proj, ind, sel, dproj)


def fox_bwd(ka, va, kt, qt, dot_t, qa, dob, lse, carried=None):
    seq = qt.shape[2]
    nblk = seq // CHUNK
    bq = min(C_BQ, seq)
    nq = seq // bq
    kg = min(C_KG, nblk)
    ng = nblk // kg
    rows_t = CHUNK + C_TAIL

    def body(ka_ref, va_ref, kt_ref, qt_ref, dot_ref, qa_ref, do_ref, lse_ref,
             dq_ref, dk_ref, dv_ref, dck_ref, dcq_ref, dqt_acc, dv_acc, dka_acc):
        p, jg = pl.program_id(0), pl.program_id(1)

        @pl.when(jg == 0)
        def _():
            dqt_acc[...] = jnp.zeros_like(dqt_acc)

        dv_acc[...] = jnp.zeros_like(dv_acc)
        dka_acc[...] = jnp.zeros_like(dka_acc)

        def step(i, carry, masked):
            cols = pl.ds(pl.multiple_of(i * bq, bq), bq)
            qtile, dotile = qt_ref[0, :, cols], dot_ref[0, :, cols]
            do, qa_i = do_ref[cols, :], qa_ref[0, cols, :]
            lse2 = jnp.concatenate([jnp.broadcast_to(lse_ref[0, 0:1, cols], (CHUNK, bq)),
                                    jnp.broadcast_to(lse_ref[0, 1:2, cols], (CHUNK, bq))] * kg, axis=0)
            pt = jnp.exp(_dot(ka_ref[0], qtile) - lse2)
            if masked:
                pt = jnp.where(_visible(pt.shape, jg * kg * CHUNK, i * bq), pt, 0.0)
            ds = pt * _dot(va_ref[0], dotile)
            ptb, dsb = pt.astype(BF16), ds.astype(BF16)
            dv_acc[...] += _dot(ptb, do)
            dka_acc[...] += _dot(dsb, qa_i)
            dqt_acc[:, cols] += _dot(kt_ref[0], dsb)
            return carry

        i0 = (jg * kg * CHUNK) // bq
        step(i0, 0, True)
        lax.fori_loop(i0 + 1, nq, functools.partial(step, masked=False), 0)
        lane = lax.broadcasted_iota(jnp.int32, (CHUNK, LANES), 1)
        for kb in range(kg):
            rows = slice(kb * CHUNK, (kb + 1) * CHUNK)
            ra = slice(kb * 2 * CHUNK, kb * 2 * CHUNK + CHUNK)
            rb = slice(kb * 2 * CHUNK + CHUNK, (kb + 1) * 2 * CHUNK)
            dk_ref[rows, :] = jnp.where(lane < C_HDIM, dka_acc[ra, 0:LANES], dka_acc[rb, 0:LANES]).astype(BF16)
            dv_ref[rows, :] = jnp.where(lane < C_HDIM, dv_acc[ra, :], dv_acc[rb, :]).astype(BF16)
            dck_ref[0, rows, :] = (jnp.where(lane == 2 * p, dka_acc[ra, LANES:], 0.0)
                                   + jnp.where(lane == 2 * p + 1, dka_acc[rb, LANES:], 0.0))

        @pl.when(jg == ng - 1)
        def _():
            for c in range(nq):
                dq_ref[c * bq:(c + 1) * bq, :] = (dqt_acc[0:CHUNK, c * bq:(c + 1) * bq].T * (C_HDIM ** -0.5)).astype(BF16)
            dcq_ref[0] = dqt_acc[CHUNK:rows_t, :]

    per_pair = lambda r, c: pl.BlockSpec((1, r, c), lambda p, j: (p, 0, 0))
    by_rows = pl.BlockSpec((1, kg * 2 * CHUNK, 2 * CHUNK), lambda p, j: (p, j, 0))
    by_cols = pl.BlockSpec((1, rows_t, kg * 2 * CHUNK), lambda p, j: (p, 0, j))
    return _call_carrying(
        carried, body, (ka, va, kt, qt, dot_t, qa, dob, lse),
        name="fox_bwd",
        grid=(C_PAIRS, ng),
        in_specs=[by_rows, by_rows, by_cols, per_pair(2 * CHUNK, seq), per_pair(2 * CHUNK, seq),
                  per_pair(seq, 2 * CHUNK), pl.BlockSpec((seq, LANES), lambda p, j: (0, p)), per_pair(2, seq)],
        out_specs=[pl.BlockSpec((seq, LANES), lambda p, j: (0, p)),
                   pl.BlockSpec((kg * CHUNK, LANES), lambda p, j: (j, p)),
                   pl.BlockSpec((kg * CHUNK, LANES), lambda p, j: (j, p)),
                   pl.BlockSpec((1, kg * CHUNK, LANES), lambda p, j: (p, j, 0)),
                   per_pair(C_TAIL, seq)],
        out_shape=[jax.ShapeDtypeStruct((seq, C_WIDTH), BF16)] * 3
        + [jax.ShapeDtypeStruct((C_PAIRS, seq, LANES), F32), jax.ShapeDtypeStruct((C_PAIRS, C_TAIL, seq), F32)],
        scratch_shapes=[pltpu.VMEM((rows_t, seq), F32), pltpu.VMEM((kg * 2 * CHUNK, LANES), F32),
                        pltpu.VMEM((kg * 2 * CHUNK, 2 * CHUNK), F32)],
    )


def fox_post(dcq, dck, proj, bf_row, dproj):
    seq = proj.shape[0]
    nc = seq // CHUNK
    triu = jnp.asarray(np.triu(np.ones((CHUNK, CHUNK), np.float32)), BF16)

    def body(dq_ref, dk_ref, fl_ref, bf_ref, u_ref, _, dfl_ref, dbf_ref, carry_ref):
        @pl.when(pl.program_id(0) == 0)
        def _():
            carry_ref[...] = jnp.zeros_like(carry_ref)
            dbf_ref[...] = jnp.zeros_like(dbf_ref)

        rows = (dq_ref[0] + dq_ref[1]) + (dq_ref[2] + dq_ref[3])
        dc = jnp.concatenate([rows, jnp.zeros((CHUNK - C_TAIL, CHUNK), F32)], axis=0).T
        dc = dc - ((dk_ref[0] + dk_ref[1]) + (dk_ref[2] + dk_ref[3]))
        g = _exact_times(u_ref[...], dc, 3) + carry_ref[...]
        carry_ref[...] += jnp.sum(dc, axis=0, keepdims=True)
        dfl = g * jax.nn.sigmoid(-(fl_ref[:, :LANES] + bf_ref[...]))
        dbf_ref[...] += jnp.sum(dfl, axis=0, keepdims=True)
        dfl_ref[...] = jnp.concatenate([dfl, jnp.zeros_like(dfl)], axis=1).astype(BF16)

    rev = lambda n: nc - 1 - n
    return pl.pallas_call(
        body,
        name="fox_post",
        grid=(nc,),
        in_specs=[
            pl.BlockSpec((C_PAIRS, C_TAIL, CHUNK), lambda n: (0, 0, rev(n))),
            pl.BlockSpec((C_PAIRS, CHUNK, LANES), lambda n: (0, rev(n), 0)),
            pl.BlockSpec((CHUNK, 256), lambda n: (rev(n), 3)),
            _full((1, LANES)), _full((CHUNK, CHUNK)), _ANY,
        ],
        out_specs=[pl.BlockSpec((CHUNK, 256), lambda n: (rev(n), 3)), _full((1, LANES))],
        out_shape=[jax.ShapeDtypeStruct(dproj.shape, BF16), jax.ShapeDtypeStruct((1, LANES), F32)],
        input_output_aliases={5: 0},
        scratch_shapes=[pltpu.VMEM((1, LANES), F32)],
        compiler_params=_cparams("arbitrary"),
    )(dcq, dck, proj, bf_row, triu, dproj)


N_DEV = 8
MESH = pl.DeviceIdType.MESH
_ANY = pl.BlockSpec(memory_space=pl.ANY)


def _mesh_pos():
    return lax.axis_index("x"), lax.axis_index("y"), lax.axis_index("c")


def _dev_index(px, py, pc):
    return 4 * px + 2 * py + pc


def _row_pieces(ref, rows):
    return [ref.at[idx + (pl.ds(r, rows),)] for idx in np.ndindex(*ref.shape[:-2]) for r in range(0, ref.shape[-2], rows)]


class _Transfer:
    def __init__(self, src, dst, rows, send_sem, recv_sem, to):
        self.src, self.dst, self.rows, self.sems, self.to = src, dst, rows, (send_sem, recv_sem), to

    def _copy(self, src, dst):
        return pltpu.make_async_remote_copy(src_ref=src, dst_ref=dst, send_sem=self.sems[0], recv_sem=self.sems[1],
                                            device_id=self.to, device_id_type=MESH)

    def start(self):
        for s, d in zip(_row_pieces(self.src, self.rows), _row_pieces(self.dst, self.rows), strict=True):
            self._copy(s, d).start()

    def wait_send(self):
        self._copy(self.src, self.dst).wait_send()

    def wait_recv(self):
        self._copy(self.src, self.dst).wait_recv()


def _exchange_call(ex, name):
    n_in, n_out = len(ex.inputs), len(ex.out_shape)

    def body(*refs):
        parts = refs[:n_in], refs[n_in:n_in + n_out], refs[n_in + n_out:]
        ex.start(*parts)
        ex.finish(*parts)

    return pl.pallas_call(body, name=name, in_specs=[_ANY] * n_in, out_specs=[_ANY] * n_out, out_shape=ex.out_shape,
                          scratch_shapes=ex.scratch, input_output_aliases=getattr(ex, "aliases", {}))(*ex.inputs)


def _carried_refs(refs, n_in, n_out, ex):
    k_in, k_out, k_sem = (len(ex.inputs), len(ex.out_shape), len(ex.scratch)) if ex else (0, 0, 0)
    a, b, c = n_in + k_in, n_in + k_in + n_out, n_in + k_in + n_out + k_out
    own = refs[:n_in] + refs[a:b] + refs[c:len(refs) - k_sem]
    return own, (refs[n_in:a], refs[b:c], refs[len(refs) - k_sem:])


class AllGatherWeights:
    def __init__(self, blocks):
        n = len(blocks)
        self.inputs = tuple(blocks)
        self.out_shape = [jax.ShapeDtypeStruct((N_DEV,) + b.shape, b.dtype) for b in blocks]
        self.scratch = ([pltpu.SemaphoreType.DMA((n, 7)), pltpu.SemaphoreType.DMA((n, 7)), pltpu.SemaphoreType.DMA((n, 2))]
                        + [pltpu.VMEM(b.shape, b.dtype) for b in blocks])

    def _plan(self, ins, outs, scratch):
        send_sems, recv_sems, local_sems, *staged = scratch
        x, y, c = _mesh_pos()
        me, sibling = (x, y, c), (x, y, 1 - c)
        chips = [(1 - x, y), (x, 1 - y), (1 - x, 1 - y)]
        every = range(len(ins))

        def copy(a, k, block, to, own=False):
            slot = outs[a].at[_dev_index(*block)]
            return _Transfer(ins[a] if own else slot, slot, ins[a].shape[-2] // 8, send_sems.at[a, k], recv_sems.at[a, k], to)

        mine = [(pltpu.make_async_copy(ins[a], staged[a], local_sems.at[a, 0]),
                 pltpu.make_async_copy(staged[a], outs[a].at[_dev_index(*me)], local_sems.at[a, 1])) for a in every]
        first = [copy(a, 1 + j, me, (*chip, c), own=True) for j, chip in enumerate(chips) for a in every]
        first += [copy(a, 0, me, sibling, own=True) for a in every]
        passed = [[copy(a, 4 + j, (*chip, c), sibling) for a in every] for j, chip in enumerate(chips)]
        return me, sibling, chips, c, every, copy, mine, first, passed

    def start(self, ins, outs, scratch):
        *_, mine, first, _ = self._plan(ins, outs, scratch)
        for to_vmem, _ in mine:
            to_vmem.start()
        for cp in first:
            cp.start()

    def finish(self, ins, outs, scratch):
        me, sibling, chips, c, every, copy, mine, first, passed = self._plan(ins, outs, scratch)
        for to_vmem, to_slot in mine:
            to_vmem.wait()
            to_slot.start()
        for j, chip in enumerate(chips):
            for a in every:
                copy(a, 1 + j, (*chip, c), me).wait_recv()
            for cp in passed[j]:
                cp.start()
        for a in every:
            copy(a, 0, sibling, me).wait_recv()
        for j, chip in enumerate(chips):
            for a in every:
                copy(a, 4 + j, (*chip, 1 - c), me).wait_recv()
        for cp in first + [cp for group in passed for cp in group]:
            cp.wait_send()
        for _, to_slot in mine:
            to_slot.wait()


N_CHIP = 4


class PairExchange:
    def __init__(self, by_core, whole=()):
        self.inputs = tuple(by_core) + tuple(whole)
        self.n_by_core = len(by_core)
        self.out_shape = ([jax.ShapeDtypeStruct(a.shape[1:], a.dtype) for a in by_core]
                          + [jax.ShapeDtypeStruct(a.shape, a.dtype) for a in whole])
        n = len(self.inputs)
        self.scratch = [pltpu.SemaphoreType.DMA((n,)), pltpu.SemaphoreType.DMA((n,))]

    def _copies(self, ins, outs, sems):
        x, y, c = _mesh_pos()
        srcs = [r.at[1 - c] if a < self.n_by_core else r for a, r in enumerate(ins)]
        return [_Transfer(srcs[a], outs[a], outs[a].shape[-2], sems[0].at[a], sems[1].at[a], (x, y, 1 - c))
                for a in range(len(ins))]

    def start(self, ins, outs, sems):
        for cp in self._copies(ins, outs, sems):
            cp.start()

    def finish(self, ins, outs, sems):
        copies = self._copies(ins, outs, sems)
        for cp in copies:
            cp.wait_recv()
        for cp in copies:
            cp.wait_send()


def pair_sum(own, other, dtype, rows, name, core, layer, depth, stacked=None):
    n, n_r, n_c = other.shape

    def body(core_ref, a_ref, b_ref, *refs):
        refs[-1][0, 0] = (a_ref[0, 0] + b_ref[0]).astype(dtype)

    carried = () if stacked is None else (stacked,)
    grid_spec = pltpu.PrefetchScalarGridSpec(
        num_scalar_prefetch=1,
        grid=(n, n_r // rows),
        in_specs=[pl.BlockSpec((1, 1, rows, n_c), lambda i, r, s: (s[0], i, r, 0)),
                  pl.BlockSpec((1, rows, n_c), lambda i, r, s: (i, r, 0))] + [_ANY] * len(carried),
        out_specs=pl.BlockSpec((1, 1, rows, n_c), lambda i, r, s: (i, layer, r, 0)),
    )
    return pl.pallas_call(
        body,
        name=name,
        grid_spec=grid_spec,
        out_shape=jax.ShapeDtypeStruct((n, depth, n_r, n_c), dtype),
        input_output_aliases={3: 0} if carried else {},
        compiler_params=_cparams("parallel", "parallel"),
    )(core, own, other, *carried)


def small_sum(a, b, name):
    def body(a_ref, b_ref, o_ref):
        o_ref[...] = a_ref[...] + b_ref[...]

    return pl.pallas_call(body, name=name, out_shape=jax.ShapeDtypeStruct(a.shape, a.dtype))(a, b)


class ChipExchange:
    def __init__(self, by_chip=(), layers=(), gathered=(), stacked=()):
        stacked = tuple(stacked) or (None,) * len(by_chip)
        kept = [s for s in stacked if s is not None]
        self.inputs = tuple(by_chip) + tuple(gathered) + tuple(kept)
        self.n_by_chip, self.n_gathered = len(by_chip), len(gathered)
        self.items = [(a, l) for a in range(len(by_chip)) for l in layers[a]] + [(self.n_by_chip + g, None) for g in range(len(gathered))]
        self.out_shape = ([jax.ShapeDtypeStruct((N_CHIP - 1,) + a.shape[1:], a.dtype) for a in by_chip]
                          + [jax.ShapeDtypeStruct((N_CHIP,) + a.shape, a.dtype) for a in gathered])
        at = iter(range(self.n_by_chip + self.n_gathered, len(self.inputs)))
        self.aliases = {next(at): a for a, s in enumerate(stacked) if s is not None}
        n = len(self.items)
        self.scratch = [pltpu.SemaphoreType.DMA((n, 3)), pltpu.SemaphoreType.DMA((n, 3)),
                        pltpu.SemaphoreType.DMA((max(self.n_gathered, 1),))]

    def _plan(self, ins, outs, sems):
        x, y, c = _mesh_pos()
        chip = 2 * x + y
        n = len(self.items)

        def copy(i, k, sending):
            a, layer = self.items[i]
            px, py = x ^ ((k >> 1) & 1), y ^ (k & 1)
            if layer is not None:
                src, dst = ins[a].at[2 * px + py, layer], outs[a].at[k - 1, layer]
            else:
                src, dst = ins[a], outs[a].at[chip if sending else 2 * px + py]
            return _Transfer(src, dst, dst.shape[-2], sems[0].at[i, k - 1], sems[1].at[i, k - 1], (px, py, c))

        local = [pltpu.make_async_copy(ins[a], outs[a].at[chip], sems[2].at[a - self.n_by_chip])
                 for a in range(self.n_by_chip, self.n_by_chip + self.n_gathered)]
        return n, copy, local

    def start(self, ins, outs, sems):
        n, copy, local = self._plan(ins, outs, sems)
        for cp in local:
            cp.start()
        for k in range(1, N_CHIP):
            for a in range(n):
                copy(a, k, True).start()

    def finish(self, ins, outs, sems):
        n, copy, local = self._plan(ins, outs, sems)
        for k in range(1, N_CHIP):
            for a in range(n):
                copy(a, k, False).wait_recv()
        for k in range(1, N_CHIP):
            for a in range(n):
                copy(a, k, True).wait_send()
        for cp in local:
            cp.wait()


ADAM_LR = 0.001
ADAM_B1 = 0.9
ADAM_B2 = 0.999
ADAM_EPS = 1e-08
ADAM_WD = 0.01
ADAM_STEP = 10


def adam_reduce(parts, w, m, v, rows, name, own=None, chip=None):
    n_l, n_r, n_c = w.shape
    n_parts = parts.shape[0]

    def body(*refs):
        p_ref, w_ref, m_ref, v_ref, g_ref, d_ref, m2_ref, v2_ref = refs[-8:]
        g = p_ref[0, 0].astype(F32)
        if own is not None:
            g = refs[-9][...].reshape(rows, n_c).astype(F32) + g
        for d in range(1, n_parts):
            g = g + p_ref[d, 0].astype(F32)
        m2 = ADAM_B1 * m_ref[0] + (1.0 - ADAM_B1) * g
        v2 = ADAM_B2 * v_ref[0] + (1.0 - ADAM_B2) * (g * g)
        m_hat = m2 / (1.0 - ADAM_B1 ** ADAM_STEP)
        v_hat = v2 / (1.0 - ADAM_B2 ** ADAM_STEP)
        g_ref[0] = g
        d_ref[0] = -ADAM_LR * (m_hat / (jnp.sqrt(v_hat) + ADAM_EPS) + ADAM_WD * w_ref[0])
        m2_ref[0] = m2
        v2_ref[0] = v2

    blk = lambda: pl.BlockSpec((1, rows, n_c), lambda l, r, *_: (l, r, 0))
    in_specs = [pl.BlockSpec((n_parts, 1, rows, n_c), lambda l, r, *_: (0, l, r, 0)), blk(), blk(), blk()]
    args = (parts, w, m, v)
    if own is not None:
        in_specs = [pl.BlockSpec((1, 1, rows, n_c), lambda l, r, s: (s[0], l, r, 0))] + in_specs
        args = (chip, own) + args
    grid_spec = pltpu.PrefetchScalarGridSpec(
        num_scalar_prefetch=0 if own is None else 1, grid=(n_l, n_r // rows), in_specs=in_specs,
        out_specs=[blk(), blk(), blk(), blk()])
    return pl.pallas_call(
        body,
        name=name,
        grid_spec=grid_spec,
        out_shape=[jax.ShapeDtypeStruct(w.shape, F32)] * 4,
        compiler_params=_cparams("parallel", "parallel"),
    )(*args)


def adam_reduce_columns(parts, w, m, v, name, own, chip):
    n_l, n_r, n_c = w.shape
    n_parts = parts.shape[0]
    view = lambda a: jnp.transpose(a, (2, 0, 1))

    def body(_, own_ref, p_ref, w_ref, m_ref, v_ref, g_ref, d_ref, m2_ref, v2_ref):
        for l in range(n_l):
            g = own_ref[0, l].astype(F32) + p_ref[0, l].astype(F32)
            for d in range(1, n_parts):
                g = g + p_ref[d, l].astype(F32)
            g = g.T
            w_l, m_l, v_l = w_ref[:, l, :], m_ref[:, l, :], v_ref[:, l, :]
            m2 = ADAM_B1 * m_l + (1.0 - ADAM_B1) * g
            v2 = ADAM_B2 * v_l + (1.0 - ADAM_B2) * (g * g)
            m_hat = m2 / (1.0 - ADAM_B1 ** ADAM_STEP)
            v_hat = v2 / (1.0 - ADAM_B2 ** ADAM_STEP)
            g_ref[:, l, :] = g
            d_ref[:, l, :] = -ADAM_LR * (m_hat / (jnp.sqrt(v_hat) + ADAM_EPS) + ADAM_WD * w_l)
            m2_ref[:, l, :] = m2
            v2_ref[:, l, :] = v2

    blk = lambda: pl.BlockSpec((LANES, n_l, n_r), lambda c, s: (c, 0, 0))
    grid_spec = pltpu.PrefetchScalarGridSpec(
        num_scalar_prefetch=1, grid=(pl.cdiv(n_c, LANES),),
        in_specs=[pl.BlockSpec((1, n_l, n_r, LANES), lambda c, s: (s[0], 0, 0, c)),
                  pl.BlockSpec((n_parts, n_l, n_r, LANES), lambda c, s: (0, 0, 0, c)), blk(), blk(), blk()],
        out_specs=[blk(), blk(), blk(), blk()])
    outs = pl.pallas_call(
        body,
        name=name,
        grid_spec=grid_spec,
        out_shape=[jax.ShapeDtypeStruct((n_c, n_l, n_r), F32)] * 4,
        compiler_params=_cparams("parallel"),
    )(chip, own, parts, view(w), view(m), view(v))
    return [jnp.transpose(o, (1, 2, 0)) for o in outs]


_SMALL = (("norm_g", (2, 1024)), ("gmlp_ln_g", (2, 4, 64)), ("gmlp_ln_b", (2, 4, 64)),
          ("gmlp_b_s", (2, 4, 128)), ("hgrn_lb", (2, 256)), ("hgrn_onorm_g", (2, 64)), ("fox_b_f", (2, 8)),
          ("final_norm_g", (1024,)), ("loss", ()))


def _padded(n):
    return -(-n // LANES) * LANES


_SMALL_ROWS = -(-sum(_padded(int(np.prod(s))) for _, s in _SMALL) // LANES // 8) * 8


def _pack_small(vals):
    flat = []
    for (name, shape), a in zip(_SMALL, vals, strict=True):
        n = int(np.prod(shape))
        flat.append(jnp.pad(a.reshape(n).astype(F32), (0, _padded(n) - n)))
    flat = jnp.concatenate(flat)
    return jnp.pad(flat, (0, _SMALL_ROWS * LANES - flat.shape[0])).reshape(_SMALL_ROWS, LANES)


def _unpack_small(slab):
    flat, out, at = slab.reshape(-1), {}, 0
    for name, shape in _SMALL:
        n = int(np.prod(shape))
        out[name] = flat[at:at + n].reshape(shape)
        at += _padded(n)
    return out


def sum_parts(parts, name):
    def body(p_ref, o_ref):
        g = p_ref[0]
        for d in range(1, parts.shape[0]):
            g = g + p_ref[d]
        o_ref[...] = g

    return pl.pallas_call(body, name=name, out_shape=jax.ShapeDtypeStruct(parts.shape[1:], F32))(parts)


def adam_small(gs, ws, ms, vs):
    n = len(gs)

    def body(*refs):
        for k in range(n):
            g, w, m, v = (refs[j * n + k][...] for j in range(4))
            m2 = ADAM_B1 * m + (1.0 - ADAM_B1) * g
            v2 = ADAM_B2 * v + (1.0 - ADAM_B2) * (g * g)
            m_hat = m2 / (1.0 - ADAM_B1 ** ADAM_STEP)
            v_hat = v2 / (1.0 - ADAM_B2 ** ADAM_STEP)
            refs[4 * n + k][...] = -ADAM_LR * (m_hat / (jnp.sqrt(v_hat) + ADAM_EPS) + ADAM_WD * w)
            refs[5 * n + k][...] = m2
            refs[6 * n + k][...] = v2

    outs = pl.pallas_call(body, name="adam_small",
                          out_shape=[jax.ShapeDtypeStruct(w.shape, F32) for _ in range(3) for w in ws])(*gs, *ws, *ms, *vs)
    return outs[:n], outs[n:2 * n], outs[2 * n:]


def kernel(x, norm_g, w_in, w_out, gmlp_ln_g, gmlp_ln_b, gmlp_w_s, gmlp_b_s, hgrn_lb, hgrn_onorm_g, fox_b_f, final_norm_g, loss_target, m_norm_g, m_w_in, m_w_out, m_gmlp_ln_g, m_gmlp_ln_b, m_gmlp_w_s, m_gmlp_b_s, m_hgrn_lb, m_hgrn_onorm_g, m_fox_b_f, m_final_norm_g, v_norm_g, v_w_in, v_w_out, v_gmlp_ln_g, v_gmlp_ln_b, v_gmlp_w_s, v_gmlp_b_s, v_hgrn_lb, v_hgrn_onorm_g, v_fox_b_f, v_final_norm_g):
    depth = w_in.shape[0]
    seq = x.shape[1]
    assert w_in.shape[2] * N_DEV == N_IN
    xs, tgt = x[0], loss_target[0]

    wi_blk, wo_blk = w_in.astype(BF16), w_out.astype(BF16)
    (wi_all,) = _exchange_call(AllGatherWeights([wi_blk[0]]), "allgather_weights_0")

    ln_g = gmlp_ln_g.reshape(depth, 1, A_WIDTH)
    ln_b = gmlp_ln_b.reshape(depth, 1, A_WIDTH)
    bs_t = jnp.pad(jnp.transpose(gmlp_b_s, (0, 2, 1)), ((0, 0), (0, 0), (0, LANES - A_GROUPS)))
    lb0, lb1 = hgrn_lb[0:1], hgrn_lb[1:2]
    onorm = jnp.tile(hgrn_onorm_g, (1, B_HEADS)).reshape(depth, 1, B_WIDTH)
    bf_row = jnp.pad(fox_b_f, ((0, 0), (0, LANES - C_HEADS))).reshape(depth, 1, LANES)

    core = lax.axis_index("c").astype(jnp.int32).reshape(1)
    chip = (2 * lax.axis_index("x") + lax.axis_index("y")).astype(jnp.int32).reshape(1)

    saved = []
    xc = xs
    for l in range(depth):
        wi_int = assemble_w_in(wi_all[:, None])
        proj, h = inproj(xc, norm_g[l:l + 1], wi_int, 0)
        ya = gmlp_fwd(proj, ln_g[l], ln_b[l], gmlp_w_s[l], bs_t[l])
        yb, states = hgrn_fwd(proj, lb0, lb1, onorm[l], l)
        ka, va, vt, kt, qt, qa = fox_prep(proj, bf_row[l])
        ride = ([wo_blk] if l == 0 else []) + ([wi_blk[l + 1]] if l + 1 < depth else [])
        o, lse, *gathered = fox_fwd(qt, ka, vt, AllGatherWeights(ride) if ride else None)
        if l == 0:
            wo_all = gathered.pop(0)
        if gathered:
            (wi_all,) = gathered
        xn, yfull = outproj(xc, ya, yb, o, proj, wo_all, l)
        saved.append((xc, proj, h, states, ka, va, kt, qt, qa, o, lse, yfull, wi_int))
        xc = xn

    dx, d_final_g, loss_tile = final_loss(xc, final_norm_g[None], tgt)

    n_shard = w_in.shape[2]
    g_norm = [None] * depth
    g_ln_g, g_ln_b, g_ws, g_bs, g_on, g_bf = ([None] * depth for _ in range(6))
    g_lb0, g_lb1 = jnp.zeros_like(lb0), jnp.zeros_like(lb1)
    swi = swo = rwi = rwo = None
    for l in reversed(range(depth)):
        x_in, proj, h, states, ka, va, kt, qt, qa, o, lse, yfull, wi_int = saved[l]
        dy, gwo = outproj_bwd(dx, yfull, wo_all, l)
        dproj, g_ln_g[l], g_ln_b[l], g_ws[l], dbs_t = gmlp_bwd(proj, dy, ln_g[l], ln_b[l], gmlp_w_s[l], bs_t[l])
        g_bs[l] = dbs_t[:, :A_GROUPS].T
        if l > 0:
            (qwo,) = _exchange_call(PairExchange([gwo]), f"pair_exchange_w_out_{l}")
        else:
            gws = jnp.stack(g_ws).reshape(-1, LANES)
            qwo, qws = _exchange_call(PairExchange([gwo], [gws]), f"pair_exchange_w_out_{l}")
            sws = small_sum(gws, qws, "pair_sum_w_s")
        swo = pair_sum(gwo, qwo, BF16, gwo.shape[2], "pair_sum_w_out", core, l, depth, swo)
        dproj, d0, d1, don = hgrn_bwd(proj, states, dy, lb0, lb1, onorm[l], l, dproj)
        g_lb0, g_lb1 = g_lb0 + d0, g_lb1 + d1
        g_on[l] = don.reshape(B_HEADS, B_KDIM).sum(0)
        dob, dproj, dot_t = fox_bwd_prep(dy, o, proj, dproj)
        top = l == depth - 1
        ride = ChipExchange([swo] if top else [swi, swo], [(l,)] if top else [(l + 1,), (l,)],
                            [sws] if l == 0 else [], [rwo] if top else [rwi, rwo])
        outs = fox_bwd(ka, va, kt, qt, dot_t, qa, dob, lse, ride)
        dqkv, (dck, dcq), got = outs[:3], outs[3:5], list(outs[5:])
        if not top:
            rwi = got.pop(0)
        rwo = got.pop(0)
        if l == 0:
            (rws,) = got
        dproj, dbf = fox_post(dcq, dck, proj, bf_row[l], dproj)
        g_bf[l] = dbf[0, :C_HEADS]
        gwi = split_w_in_grad(inproj_bwd_w(h, dproj, dqkv), n_shard)[:, :, 0]
        (qwi,) = _exchange_call(PairExchange([gwi]), f"pair_exchange_w_in_{l}")
        swi = pair_sum(gwi, qwi, BF16, 256, "pair_sum_w_in", core, l, depth, swi)
        ride = ChipExchange([swi], [(l,)], stacked=[rwi]) if l == 0 else None
        outs = inproj_bwd_x(dproj, dqkv, wi_int, x_in, norm_g[l:l + 1], dx, 0, ride)
        dx, g_norm[l] = outs[:2]
        if ride is not None:
            (rwi,) = outs[2:]

    gsm = _pack_small([
        jnp.concatenate(g_norm), jnp.stack(g_ln_g), jnp.stack(g_ln_b), jnp.stack(g_bs),
        jnp.concatenate([g_lb0, g_lb1]), jnp.stack(g_on), jnp.stack(g_bf), d_final_g, loss_tile[0, 0]])
    (qsm,) = _exchange_call(PairExchange([], [gsm]), "pair_exchange_small")
    ssm = small_sum(gsm, qsm, "pair_sum_small")
    (rsm,) = _exchange_call(ChipExchange(gathered=[ssm]), "chip_exchange_small")

    small_w = (norm_g, gmlp_ln_g, gmlp_ln_b, gmlp_b_s, hgrn_lb, hgrn_onorm_g, fox_b_f, final_norm_g)
    small_m = (m_norm_g, m_gmlp_ln_g, m_gmlp_ln_b, m_gmlp_b_s, m_hgrn_lb, m_hgrn_onorm_g, m_fox_b_f, m_final_norm_g)
    small_v = (v_norm_g, v_gmlp_ln_g, v_gmlp_ln_b, v_gmlp_b_s, v_hgrn_lb, v_hgrn_onorm_g, v_fox_b_f, v_final_norm_g)
    res_wi = adam_reduce_columns(rwi, w_in, m_w_in, v_w_in, "adam_w_in", swi, chip)
    res_wo = adam_reduce(rwo, w_out, m_w_out, v_w_out, w_out.shape[1], "adam_w_out", own=swo, chip=chip)
    grads = _unpack_small(sum_parts(rsm, "sum_small"))
    names = [name for name, _ in _SMALL if name != "loss"]
    rows = lambda a: a.reshape(1, -1) if a.ndim == 1 else a
    res_sm = adam_small([rows(grads[k]) for k in names], *([rows(a) for a in wmv] for wmv in (small_w, small_m, small_v)))
    res_sm = [grads] + [{k: a.reshape(grads[k].shape) for k, a in zip(names, r, strict=True)} for r in res_sm]
    as_rows = lambda a: a.reshape(1, -1, LANES)
    res_ws = adam_reduce(rws[:, None], as_rows(gmlp_w_s), as_rows(m_gmlp_w_s), as_rows(v_gmlp_w_s), rws.shape[1], "adam_w_s")
    for s, r in zip(res_sm, res_ws, strict=True):
        s["gmlp_w_s"] = r.reshape(gmlp_w_s.shape)

    def group(i):
        s = res_sm[i]
        return [s["norm_g"], res_wi[i], res_wo[i], s["gmlp_ln_g"], s["gmlp_ln_b"], s["gmlp_w_s"], s["gmlp_b_s"],
                s["hgrn_lb"], s["hgrn_onorm_g"], s["fox_b_f"], s["final_norm_g"]]

    return (res_sm[0]["loss"], dx[None], *group(0), *group(1), *group(2), *group(3))
```

```python
import functools

import jax
import jax.numpy as jnp
import numpy as np
from jax import lax
from jax.experimental import pallas as pl
from jax.experimental.pallas import tpu as pltpu

F32 = jnp.float32
BF16 = jnp.bfloat16

NORM_EPS = 1e-6
F_FLOOR = 1e-30
CHUNK = 128
LANES = 128
VMEM_LIMIT = 56 * 1024 * 1024


def _cparams(*sem):
    return pltpu.CompilerParams(dimension_semantics=sem, vmem_limit_bytes=VMEM_LIMIT)


def _dot(a, b, dims=(((1,), (0,)), ((), ())), precision=None):
    return lax.dot_general(a, b, dims, precision=precision, preferred_element_type=F32)


_NT = (((1,), (1,)), ((), ()))
_TN = (((0,), (0,)), ((), ()))


def _bf16_pieces(x, n):
    out, r = [], x
    for i in range(n):
        out.append(r.astype(BF16))
        if i + 1 < n:
            r = r - out[-1].astype(F32)
    return out


@functools.partial(jax.custom_vjp, nondiff_argnums=(2,))
def _times_exact(x, e, n):
    return functools.reduce(jnp.add, [_dot(p, e) for p in _bf16_pieces(x, n)])


def _times_exact_fwd(x, e, n):
    return _times_exact(x, e, n), e


def _times_exact_bwd(n, e, g):
    dx = functools.reduce(jnp.add, [lax.dot_general(p, e, _NT, preferred_element_type=F32) for p in _bf16_pieces(g, n)])
    return dx, jnp.zeros_like(e)


_times_exact.defvjp(_times_exact_fwd, _times_exact_bwd)


@functools.partial(jax.custom_vjp, nondiff_argnums=(2,))
def _exact_times(e, x, n):
    return functools.reduce(jnp.add, [_dot(e, p) for p in _bf16_pieces(x, n)])


def _exact_times_fwd(e, x, n):
    return _exact_times(e, x, n), e


def _exact_times_bwd(n, e, g):
    dx = functools.reduce(jnp.add, [lax.dot_general(e, p, _TN, preferred_element_type=F32) for p in _bf16_pieces(g, n)])
    return jnp.zeros_like(e), dx


_exact_times.defvjp(_exact_times_fwd, _exact_times_bwd)


def _group_mean_matrix(width, group):
    idx = np.arange(width) // group
    return jnp.asarray((idx[:, None] == idx[None, :]).astype(np.float32) / group, BF16)


def _group_ones_matrix(width, group):
    idx = np.arange(width) // group
    return jnp.asarray((idx[:, None] == idx[None, :]).astype(np.float32), BF16)


A_WIDTH = 256
A_GROUPS = 4
A_GDIM = 64


A_ROWS = 512


def _gmlp_chunk(x3, ln_g, ln_b, w_s, bs_t, mean_m, gind):
    n = x3.shape[0] // CHUNK
    u = jax.nn.gelu(x3[:, :A_WIDTH])
    v = jax.nn.gelu(x3[:, A_WIDTH:2 * A_WIDTH])
    z = x3[:, 2 * A_WIDTH:]
    mu = _times_exact(v, mean_m, 2)
    d = v - mu
    var = _times_exact(d * d, mean_m, 2)
    vn = d * lax.rsqrt(var + NORM_EPS) * ln_g + ln_b
    vnb = vn.astype(BF16)
    wide = jnp.concatenate([vnb[i * CHUNK:(i + 1) * CHUNK] for i in range(n)], axis=1)
    row = lax.broadcasted_iota(jnp.int32, (CHUNK, CHUNK), 0)
    col = lax.broadcasted_iota(jnp.int32, (CHUNK, CHUNK), 1)
    causal = row >= col
    lane_g = lax.shift_right_logical(lax.broadcasted_iota(jnp.int32, (CHUNK, n * A_WIDTH), 1), 6) & (A_GROUPS - 1)
    bias = _times_exact(bs_t, gind, 3)
    mixed = jnp.concatenate([bias] * n, axis=1)
    for g in range(A_GROUPS):
        wc = jnp.where(causal, w_s[g], 0.0).astype(BF16)
        mixed = mixed + jnp.where(lane_g == g, _dot(wc, wide), 0.0)
    mixed = jnp.concatenate([mixed[:, i * A_WIDTH:(i + 1) * A_WIDTH] for i in range(n)], axis=0)
    return u * mixed * jax.nn.silu(z)


def _gmlp_consts():
    gind = np.zeros((LANES, A_WIDTH), np.float32)
    for g in range(A_GROUPS):
        gind[g, g * A_GDIM:(g + 1) * A_GDIM] = 1.0
    return _group_mean_matrix(A_WIDTH, A_GDIM), jnp.asarray(gind, BF16)


def _full(shape):
    return pl.BlockSpec(shape, lambda *_: (0,) * len(shape))


def gmlp_fwd(proj, ln_g, ln_b, w_s, bs_t):
    seq = proj.shape[0]
    rows = min(A_ROWS, seq)
    mean_m, gind = _gmlp_consts()

    def body(x_ref, g_ref, b_ref, w_ref, bs_ref, m_ref, gi_ref, y_ref):
        y = _gmlp_chunk(x_ref[...], g_ref[...], b_ref[...], w_ref[...], bs_ref[...], m_ref[...], gi_ref[...])
        y_ref[...] = y.astype(BF16)

    return pl.pallas_call(
        body,
        name="gmlp_fwd",
        grid=(seq // rows,),
        in_specs=[
            pl.BlockSpec((rows, 3 * A_WIDTH), lambda n: (n, 0)),
            _full((1, A_WIDTH)), _full((1, A_WIDTH)), _full((A_GROUPS, CHUNK, CHUNK)), _full((CHUNK, LANES)),
            _full((A_WIDTH, A_WIDTH)), _full((LANES, A_WIDTH)),
        ],
        out_specs=pl.BlockSpec((rows, A_WIDTH), lambda n: (n, 0)),
        out_shape=jax.ShapeDtypeStruct((seq, A_WIDTH), BF16),
        compiler_params=_cparams("parallel"),
    )(proj, ln_g, ln_b, w_s, bs_t, mean_m, gind)


def gmlp_bwd(proj, dy, ln_g, ln_b, w_s, bs_t):
    seq = proj.shape[0]
    rows = min(A_ROWS, seq)
    mean_m, gind = _gmlp_consts()

    def body(x_ref, dy_ref, g_ref, b_ref, w_ref, bs_ref, m_ref, gi_ref, dx_ref, dg_ref, db_ref, dw_ref, dbs_ref):
        fn = functools.partial(_gmlp_chunk, mean_m=m_ref[...], gind=gi_ref[...])
        _, vjp = jax.vjp(fn, x_ref[...], g_ref[...], b_ref[...], w_ref[...], bs_ref[...])
        dx, dg, db, dw, dbs = vjp(dy_ref[...])
        dx_ref[...] = dx.astype(BF16)

        @pl.when(pl.program_id(0) == 0)
        def _():
            dg_ref[...] = jnp.zeros_like(dg_ref)
            db_ref[...] = jnp.zeros_like(db_ref)
            dw_ref[...] = jnp.zeros_like(dw_ref)
            dbs_ref[...] = jnp.zeros_like(dbs_ref)

        dg_ref[...] += dg
        db_ref[...] += db
        dw_ref[...] += dw
        dbs_ref[...] += dbs

    return pl.pallas_call(
        body,
        name="gmlp_bwd",
        grid=(seq // rows,),
        in_specs=[
            pl.BlockSpec((rows, 3 * A_WIDTH), lambda n: (n, 0)),
            pl.BlockSpec((rows, A_WIDTH), lambda n: (n, 0)),
            _full((1, A_WIDTH)), _full((1, A_WIDTH)), _full((A_GROUPS, CHUNK, CHUNK)), _full((CHUNK, LANES)),
            _full((A_WIDTH, A_WIDTH)), _full((LANES, A_WIDTH)),
        ],
        out_specs=[
            pl.BlockSpec((rows, 3 * A_WIDTH), lambda n: (n, 0)),
            _full((1, A_WIDTH)), _full((1, A_WIDTH)), _full((A_GROUPS, CHUNK, CHUNK)), _full((CHUNK, LANES)),
        ],
        out_shape=[
            jax.ShapeDtypeStruct((seq, D_INT), BF16),
            jax.ShapeDtypeStruct((1, A_WIDTH), F32), jax.ShapeDtypeStruct((1, A_WIDTH), F32),
            jax.ShapeDtypeStruct((A_GROUPS, CHUNK, CHUNK), F32), jax.ShapeDtypeStruct((CHUNK, LANES), F32),
        ],
        compiler_params=_cparams("arbitrary"),
    )(proj, dy, ln_g, ln_b, w_s, bs_t, mean_m, gind)


B_WIDTH = 256
B_HEADS = 4
B_KDIM = 64
B_LEVELS = (64, 32, 16, 8, 4, 2, 1)


def _hgrn_consts():
    t = np.arange(CHUNK)
    u = t[None, :]
    mats = [np.tril(np.ones((CHUNK, CHUNK), np.float32))]
    for m in B_LEVELS:
        p = (t // (2 * m)) * (2 * m) + m - 1
        right = (t % (2 * m)) >= m
        sel = np.where(right[:, None], (u > p[:, None]) & (u <= t[:, None]), (u > t[:, None]) & (u <= p[:, None]))
        mats.append(sel.astype(np.float32))
    return jnp.asarray(np.concatenate(mats, 0), BF16), _group_ones_matrix(B_WIDTH, B_KDIM)


def _hgrn_lower_bound(lb0, lb1, layer):
    mx = jnp.maximum(lb0, lb1)
    e0 = jnp.exp(lb0 - mx)
    e1 = jnp.exp(lb1 - mx)
    p0 = e0 / (e0 + e1)
    p1 = e1 / (e0 + e1)
    cs = p0 if layer == 0 else p0 + p1
    return jnp.clip(cs - p0, 0.0, 1.0 - 1e-6)


def _hgrn_chunk(x4, st, lb0, lb1, onorm, layer, tstack, ones_bd):
    q_raw, fl, v, zg = (x4[:, i * B_WIDTH:(i + 1) * B_WIDTH] for i in range(4))
    lb = _hgrn_lower_bound(lb0, lb1, layer)
    q = jax.nn.silu(q_raw) * (B_KDIM ** -0.5)
    f = lb + (1.0 - lb) * jax.nn.sigmoid(fl)
    logf = jnp.log(jnp.maximum(f, F_FLOOR))
    k = (1.0 - lb) * jax.nn.sigmoid(-fl)
    b = _exact_times(tstack[:CHUNK], logf, 3)
    dall = jnp.concatenate([b, _exact_times(tstack[CHUNK:], logf, 2)], axis=0)
    b_last = jnp.sum(logf, axis=0, keepdims=True)
    vb = v.astype(BF16)

    lane_h = lax.shift_right_logical(lax.broadcasted_iota(jnp.int32, (CHUNK, B_WIDTH), 1), 6)
    row = lax.broadcasted_iota(jnp.int32, (CHUNK, B_WIDTH), 0)
    srow = lax.broadcasted_iota(jnp.int32, (B_HEADS * CHUNK, CHUNK), 0) & (CHUNK - 1)
    scol = lax.broadcasted_iota(jnp.int32, (B_HEADS * CHUNK, CHUNK), 1)

    def heads_on_rows(a):
        return jnp.concatenate([jnp.where(lane_h == h, a, 0.0) for h in range(B_HEADS)], axis=0)

    def heads_from_rows(r):
        out = jnp.where(lane_h == 0, r[:CHUNK], 0.0)
        for h in range(1, B_HEADS):
            out = out + jnp.where(lane_h == h, r[h * CHUNK:(h + 1) * CHUNK], 0.0)
        return out

    o = lax.dot_general((q * jnp.exp(b)).astype(BF16), st.astype(BF16), _NT, preferred_element_type=F32)
    scores = jnp.zeros((B_HEADS * CHUNK, CHUNK), F32)
    for li, m in enumerate(B_LEVELS):
        e = jnp.exp(dall[(li + 1) * CHUNK:(li + 2) * CHUNK])
        right = (row & (2 * m - 1)) >= m
        qt = jnp.where(right, q * e, 0.0)
        kt = jnp.where(right, 0.0, k * e)
        sc = lax.dot_general(heads_on_rows(qt).astype(BF16), kt.astype(BF16), _NT, preferred_element_type=F32)
        sh = int(np.log2(2 * m))
        same = lax.shift_right_logical(srow, sh) == lax.shift_right_logical(scol, sh)
        scores = scores + jnp.where(same, sc, 0.0)
    o = o + heads_from_rows(_dot(scores.astype(BF16), vb))
    o = o + _times_exact(q * k, ones_bd, 2) * v

    kv = lax.dot_general(vb, (k * jnp.exp(b_last - b)).astype(BF16), _TN, preferred_element_type=F32)
    st_new = st * jnp.exp(b_last) + jnp.where(ones_bd > 0.5, kv, 0.0)

    ms = _times_exact(o * o, ones_bd, 2) * (1.0 / B_KDIM)
    y = o * lax.rsqrt(ms + NORM_EPS) * onorm * jax.nn.silu(zg)
    return y, st_new


B_ROWS = 256


def _hgrn_rows(x4, st, lb0, lb1, onorm, layer, tstack, ones_bd):
    ys = []
    for i in range(x4.shape[0] // CHUNK):
        y, st = _hgrn_chunk(x4[i * CHUNK:(i + 1) * CHUNK], st, lb0, lb1, onorm, layer, tstack, ones_bd)
        ys.append(y)
    return jnp.concatenate(ys, axis=0), st


def hgrn_fwd(proj, lb0, lb1, onorm, layer):
    seq = proj.shape[0]
    rows = min(B_ROWS, seq)
    nc = seq // rows
    tstack, ones_bd = _hgrn_consts()

    def body(x_ref, lb0_ref, lb1_ref, on_ref, t_ref, e_ref, y_ref, st_out_ref, st_ref):
        @pl.when(pl.program_id(0) == 0)
        def _():
            st_ref[...] = jnp.zeros_like(st_ref)

        st = st_ref[...]
        st_out_ref[0] = st
        y, st_new = _hgrn_rows(x_ref[...], st, lb0_ref[...], lb1_ref[...], on_ref[...], layer, t_ref[...], e_ref[...])
        y_ref[...] = y.astype(BF16)
        st_ref[...] = st_new

    return pl.pallas_call(
        body,
        name=f"hgrn_fwd_{layer}",
        grid=(nc,),
        in_specs=[
            pl.BlockSpec((rows, 4 * B_WIDTH), lambda n: (n, 1)),
            _full((1, B_WIDTH)), _full((1, B_WIDTH)), _full((1, B_WIDTH)),
            _full(((len(B_LEVELS) + 1) * CHUNK, CHUNK)), _full((B_WIDTH, B_WIDTH)),
        ],
        out_specs=[
            pl.BlockSpec((rows, B_WIDTH), lambda n: (n, 0)),
            pl.BlockSpec((1, B_WIDTH, B_WIDTH), lambda n: (n, 0, 0)),
        ],
        out_shape=[jax.ShapeDtypeStruct((seq, B_WIDTH), BF16), jax.ShapeDtypeStruct((nc, B_WIDTH, B_WIDTH), F32)],
        scratch_shapes=[pltpu.VMEM((B_WIDTH, B_WIDTH), F32)],
        compiler_params=_cparams("arbitrary"),
    )(proj, lb0, lb1, onorm, tstack, ones_bd)


def hgrn_bwd(proj, states, dy, lb0, lb1, onorm, layer, dproj):
    seq = proj.shape[0]
    rows = min(B_ROWS, seq)
    nc = seq // rows
    tstack, ones_bd = _hgrn_consts()

    def body(x_ref, st_in_ref, dy_ref, lb0_ref, lb1_ref, on_ref, t_ref, e_ref, _, dx_ref, d0_ref, d1_ref, don_ref, dst_ref):
        @pl.when(pl.program_id(0) == 0)
        def _():
            dst_ref[...] = jnp.zeros_like(dst_ref)
            d0_ref[...] = jnp.zeros_like(d0_ref)
            d1_ref[...] = jnp.zeros_like(d1_ref)
            don_ref[...] = jnp.zeros_like(don_ref)

        fn = functools.partial(_hgrn_rows, layer=layer, tstack=t_ref[...], ones_bd=e_ref[...])
        _, vjp = jax.vjp(fn, x_ref[...], st_in_ref[0], lb0_ref[...], lb1_ref[...], on_ref[...])
        dx, dst, d0, d1, don = vjp((dy_ref[...], dst_ref[...]))
        dx_ref[...] = dx.astype(BF16)
        dst_ref[...] = dst
        d0_ref[...] += d0
        d1_ref[...] += d1
        don_ref[...] += don

    rev = lambda n: nc - 1 - n
    return pl.pallas_call(
        body,
        name=f"hgrn_bwd_{layer}",
        grid=(nc,),
        in_specs=[
            pl.BlockSpec((rows, 4 * B_WIDTH), lambda n: (rev(n), 1)),
            pl.BlockSpec((1, B_WIDTH, B_WIDTH), lambda n: (rev(n), 0, 0)),
            pl.BlockSpec((rows, B_WIDTH), lambda n: (rev(n), 1)),
            _full((1, B_WIDTH)), _full((1, B_WIDTH)), _full((1, B_WIDTH)),
            _full(((len(B_LEVELS) + 1) * CHUNK, CHUNK)), _full((B_WIDTH, B_WIDTH)), _ANY,
        ],
        out_specs=[
            pl.BlockSpec((rows, 4 * B_WIDTH), lambda n: (rev(n), 1)),
            _full((1, B_WIDTH)), _full((1, B_WIDTH)), _full((1, B_WIDTH)),
        ],
        out_shape=[jax.ShapeDtypeStruct(dproj.shape, BF16)] + [jax.ShapeDtypeStruct((1, B_WIDTH), F32)] * 3,
        input_output_aliases={8: 0},
        scratch_shapes=[pltpu.VMEM((B_WIDTH, B_WIDTH), F32)],
        compiler_params=_cparams("arbitrary"),
    )(proj, states, dy, lb0, lb1, onorm, tstack, ones_bd, dproj)


D_MODEL = 1024
D_INT = 4096


def _rms_stats(xf):
    r = lax.rsqrt(jnp.mean(xf * xf, axis=-1, keepdims=True) + NORM_EPS)
    return r, xf * r


def _rms_bwd(dy, g, r, xh):
    u = dy * g
    return r * (u - xh * jnp.mean(u * xh, axis=-1, keepdims=True))


def inproj(x, g, w, layer):
    seq = x.shape[0]
    tm = min(seq, 512)

    def body(x_ref, g_ref, w_ref, p_ref, h_ref):
        _, xh = _rms_stats(x_ref[...])
        h = (xh * g_ref[...]).astype(BF16)
        h_ref[...] = h
        p_ref[...] = _dot(h, w_ref[0])

    return pl.pallas_call(
        body,
        name="inproj",
        grid=(seq // tm,),
        in_specs=[
            pl.BlockSpec((tm, D_MODEL), lambda i: (i, 0)),
            _full((1, D_MODEL)),
            pl.BlockSpec((1, D_MODEL, D_INT), lambda i: (layer, 0, 0)),
        ],
        out_specs=[pl.BlockSpec((tm, D_INT), lambda i: (i, 0)), pl.BlockSpec((tm, D_MODEL), lambda i: (i, 0))],
        out_shape=[jax.ShapeDtypeStruct((seq, D_INT), F32), jax.ShapeDtypeStruct((seq, D_MODEL), BF16)],
        compiler_params=_cparams("parallel"),
    )(x, g, w)


def outproj(x, ya, yb, o, proj, wo, layer):
    seq = x.shape[0]
    tm = min(seq, 512)
    blk = wo.shape[2]

    def body(x_ref, ya_ref, yb_ref, o_ref, z_ref, w_ref, xn_ref, y_ref):
        yc = (o_ref[...] * jax.nn.silu(z_ref[...])).astype(BF16)
        y = jnp.concatenate([ya_ref[...], yb_ref[...], yc], axis=1)
        y_ref[...] = y
        w = jnp.concatenate([w_ref[d, 0] for d in range(N_DEV)], axis=0)
        xn_ref[...] = x_ref[...] + _dot(y, w)

    return pl.pallas_call(
        body,
        name="outproj",
        grid=(seq // tm,),
        in_specs=[
            pl.BlockSpec((tm, D_MODEL), lambda i: (i, 0)),
            pl.BlockSpec((tm, 256), lambda i: (i, 0)),
            pl.BlockSpec((tm, 256), lambda i: (i, 0)),
            pl.BlockSpec((tm, 512), lambda i: (i, 0)),
            pl.BlockSpec((tm, 512), lambda i: (i, 7)),
            pl.BlockSpec((N_DEV, 1, blk, D_MODEL), lambda i: (0, layer, 0, 0)),
        ],
        out_specs=[pl.BlockSpec((tm, D_MODEL), lambda i: (i, 0)), pl.BlockSpec((tm, D_MODEL), lambda i: (i, 0))],
        out_shape=[jax.ShapeDtypeStruct((seq, D_MODEL), F32), jax.ShapeDtypeStruct((seq, D_MODEL), BF16)],
        compiler_params=_cparams("parallel"),
    )(x, ya, yb, o, proj, wo)


def outproj_bwd(dx, y, wo, layer):
    seq = dx.shape[0]
    ts = min(seq, 512)
    blk = wo.shape[2]

    def body(dx_ref, y_ref, w_ref, dy_ref, dw_ref):
        @pl.when(pl.program_id(0) == 0)
        def _():
            dw_ref[...] = jnp.zeros_like(dw_ref)

        dxb = dx_ref[...].astype(BF16)
        w = jnp.concatenate([w_ref[d, 0] for d in range(N_DEV)], axis=0)
        dy_ref[...] = lax.dot_general(dxb, w, _NT, preferred_element_type=F32)
        dw = lax.dot_general(y_ref[...], dxb, _TN, preferred_element_type=F32)
        for d in range(N_DEV):
            dw_ref[d % 2, d // 2] += dw[d * blk:(d + 1) * blk]

    return pl.pallas_call(
        body,
        name="outproj_bwd",
        grid=(seq // ts,),
        in_specs=[
            pl.BlockSpec((ts, D_MODEL), lambda i: (i, 0)),
            pl.BlockSpec((ts, D_MODEL), lambda i: (i, 0)),
            pl.BlockSpec((N_DEV, 1, blk, D_MODEL), lambda i: (0, layer, 0, 0)),
        ],
        out_specs=[pl.BlockSpec((ts, D_MODEL), lambda i: (i, 0)),
                   pl.BlockSpec((2, N_CHIP, blk, D_MODEL), lambda i: (0, 0, 0, 0))],
        out_shape=[jax.ShapeDtypeStruct((seq, D_MODEL), F32), jax.ShapeDtypeStruct((2, N_CHIP, blk, D_MODEL), F32)],
        compiler_params=_cparams("arbitrary"),
    )(dx, y, wo)


C_QKV = (2048, 3584)


def _dproj_parts(dp_ref, dqkv_refs, rows):
    lo, hi = C_QKV
    step = (hi - lo) // len(dqkv_refs)
    return ([(0, dp_ref.at[rows, 0:lo])] + [(lo + i * step, r.at[rows, :]) for i, r in enumerate(dqkv_refs)]
            + [(hi, dp_ref.at[rows, hi:D_INT])])


def inproj_bwd_x(dproj, dqkv, w, x, g, dx_in, layer, carried=None):
    seq = x.shape[0]
    tm = min(seq, 512)

    def body(dp_ref, dq_ref, dk_ref, dv_ref, w_ref, x_ref, g_ref, dxin_ref, dx_ref, dg_ref):
        @pl.when(pl.program_id(0) == 0)
        def _():
            dg_ref[...] = jnp.zeros_like(dg_ref)

        dh = None
        for at, part in _dproj_parts(dp_ref, (dq_ref, dk_ref, dv_ref), slice(None)):
            term = lax.dot_general(part[...], w_ref[0, :, at:at + part.shape[1]], _NT, preferred_element_type=F32)
            dh = term if dh is None else dh + term
        r, xh = _rms_stats(x_ref[...])
        dg_ref[...] += jnp.sum(dh * xh, axis=0, keepdims=True)
        dx_ref[...] = dxin_ref[...] + _rms_bwd(dh, g_ref[...], r, xh)

    third = lambda: pl.BlockSpec((tm, C_WIDTH), lambda i: (i, 0))
    return _call_carrying(
        carried, body, (dproj, *dqkv, w, x, g, dx_in),
        name="inproj_bwd_x",
        grid=(seq // tm,),
        in_specs=[
            pl.BlockSpec((tm, D_INT), lambda i: (i, 0)), third(), third(), third(),
            pl.BlockSpec((1, D_MODEL, D_INT), lambda i: (layer, 0, 0)),
            pl.BlockSpec((tm, D_MODEL), lambda i: (i, 0)),
            _full((1, D_MODEL)),
            pl.BlockSpec((tm, D_MODEL), lambda i: (i, 0)),
        ],
        out_specs=[pl.BlockSpec((tm, D_MODEL), lambda i: (i, 0)), _full((1, D_MODEL))],
        out_shape=[jax.ShapeDtypeStruct((seq, D_MODEL), F32), jax.ShapeDtypeStruct((1, D_MODEL), F32)],
        scratch_shapes=[], semantics=("arbitrary",),
    )


def inproj_bwd_w(h, dproj, dqkv):
    seq = h.shape[0]
    ts, tn = min(seq, 512), 512

    def body(h_ref, dp_ref, dq_ref, dk_ref, dv_ref, dw_ref):
        @pl.when(pl.program_id(0) == 0)
        def _():
            dw_ref[...] = jnp.zeros_like(dw_ref)

        ht = h_ref[...].T
        for at, part in _dproj_parts(dp_ref, (dq_ref, dk_ref, dv_ref), slice(None)):
            for c in range(0, part.shape[1], tn):
                dw_ref[0, :, at + c:at + c + tn] += _dot(ht, part[:, c:c + tn])

    third = lambda: pl.BlockSpec((ts, C_WIDTH), lambda s: (s, 0))
    return pl.pallas_call(
        body,
        name="inproj_bwd_w",
        grid=(seq // ts,),
        in_specs=[pl.BlockSpec((ts, D_MODEL), lambda s: (s, 0)), pl.BlockSpec((ts, D_INT), lambda s: (s, 0)),
                  third(), third(), third()],
        out_specs=_full((1, D_MODEL, D_INT)),
        out_shape=jax.ShapeDtypeStruct((1, D_MODEL, D_INT), F32),
        compiler_params=_cparams("arbitrary"),
    )(h, dproj, *dqkv)


N_IN = 3848


def _internal_of(col):
    return col if col < 768 else (col + 256 if col < 3840 else 768 + col - 3840)


def _column_runs(n_shard):
    runs = []
    for d in range(N_IN // n_shard):
        mine = []
        for j in range(n_shard):
            ci = _internal_of(d * n_shard + j)
            if mine and mine[-1][0] + mine[-1][1] == ci:
                mine[-1][1] += 1
            else:
                mine.append([ci, 1, j])
        runs.append(mine)
    return runs


def assemble_w_in(wi_all):
    n_dev, depth, _, n_shard = wi_all.shape
    tr = 256
    pieces = [[] for _ in range(D_INT // LANES)]
    for d, mine in enumerate(_column_runs(n_shard)):
        for ci, ln, off in mine:
            while ln > 0:
                blk, at = divmod(ci, LANES)
                take = min(ln, LANES - at)
                pieces[blk].append((at, take, d, off))
                ci, ln, off = ci + take, ln - take, off + take

    def body(x_ref, o_ref):
        for blk, parts in enumerate(pieces):
            vals, at = [], 0
            for start, ln, d, off in sorted(parts):
                if start > at:
                    vals.append(jnp.zeros((tr, start - at), BF16))
                vals.append(x_ref[d, 0, :, off:off + ln])
                at = start + ln
            if at < LANES:
                vals.append(jnp.zeros((tr, LANES - at), BF16))
            o_ref[0, :, blk * LANES:(blk + 1) * LANES] = vals[0] if len(vals) == 1 else jnp.concatenate(vals, axis=1)

    return pl.pallas_call(
        body,
        name="assemble_w_in",
        grid=(depth, D_MODEL // tr),
        in_specs=[pl.BlockSpec((n_dev, 1, tr, n_shard), lambda l, r: (0, l, r, 0))],
        out_specs=pl.BlockSpec((1, tr, D_INT), lambda l, r: (l, r, 0)),
        out_shape=jax.ShapeDtypeStruct((depth, D_MODEL, D_INT), BF16),
        compiler_params=_cparams("parallel", "parallel"),
    )(wi_all)


def split_w_in_grad(dwi, n_shard):
    depth = dwi.shape[0]
    tr = 256
    runs = _column_runs(n_shard)

    def body(x_ref, o_ref):
        for d, mine in enumerate(runs):
            for ci, ln, off in mine:
                o_ref[d % 2, d // 2, 0, :, off:off + ln] = x_ref[0, :, ci:ci + ln]

    return pl.pallas_call(
        body,
        name="split_w_in_grad",
        grid=(depth, D_MODEL // tr),
        in_specs=[pl.BlockSpec((1, tr, D_INT), lambda l, r: (l, r, 0))],
        out_specs=pl.BlockSpec((2, N_CHIP, 1, tr, n_shard), lambda l, r: (0, 0, l, r, 0)),
        out_shape=jax.ShapeDtypeStruct((2, N_CHIP, depth, D_MODEL, n_shard), F32),
        compiler_params=_cparams("parallel", "parallel"),
    )(dwi)


def final_loss(x, g, tgt):
    seq = x.shape[0]
    tm = min(seq, 512)

    def body(x_ref, g_ref, t_ref, dx_ref, dg_ref, loss_ref):
        @pl.when(pl.program_id(0) == 0)
        def _():
            dg_ref[...] = jnp.zeros_like(dg_ref)
            loss_ref[...] = jnp.zeros_like(loss_ref)

        g = g_ref[...]
        r, xh = _rms_stats(x_ref[...])
        err = xh * g - t_ref[...]
        sq = jnp.sum(jnp.sum(err * err, axis=1, keepdims=True), axis=0, keepdims=True)
        loss_ref[...] += jnp.broadcast_to(sq * (0.5 / D_MODEL), loss_ref.shape)
        dout = err * (1.0 / D_MODEL)
        dg_ref[...] += jnp.sum(dout * xh, axis=0, keepdims=True)
        dx_ref[...] = _rms_bwd(dout, g, r, xh)

    return pl.pallas_call(
        body,
        name="final_loss",
        grid=(seq // tm,),
        in_specs=[pl.BlockSpec((tm, D_MODEL), lambda i: (i, 0)), _full((1, D_MODEL)), pl.BlockSpec((tm, D_MODEL), lambda i: (i, 0))],
        out_specs=[pl.BlockSpec((tm, D_MODEL), lambda i: (i, 0)), _full((1, D_MODEL)), _full((8, LANES))],
        out_shape=[jax.ShapeDtypeStruct((seq, D_MODEL), F32), jax.ShapeDtypeStruct((1, D_MODEL), F32), jax.ShapeDtypeStruct((8, LANES), F32)],
        compiler_params=_cparams("arbitrary"),
    )(x, g, tgt)


C_WIDTH = 512
C_HEADS = 8
C_HDIM = 64
C_PAIRS = C_HEADS // 2
C_BQ = 512
C_TAIL = 16
C_KG = 4


def _split3(x):
    hi = x.astype(BF16)
    r = x - hi.astype(F32)
    mid = r.astype(BF16)
    return hi, mid, (r - mid.astype(F32)).astype(BF16)


def _piece_selectors():
    sel = np.zeros((C_HEADS, 3 * LANES, LANES), np.float32)
    for p in range(C_PAIRS):
        for e in range(2):
            for t in range(3):
                sel[2 * p + e, t * LANES + 2 * p + e, 3 * e + t] = -1.0
    return sel


def fox_prep(proj, bf_row):
    seq = proj.shape[0]
    nblk = seq // CHUNK
    tril = jnp.asarray(np.tril(np.ones((CHUNK, CHUNK), np.float32)), BF16)
    sel = jnp.asarray(_piece_selectors(), BF16)
    rows_t = CHUNK + C_TAIL

    def body(fl_ref, q_ref, k_ref, v_ref, bf_ref, l_ref, sel_ref, ka_ref, va_ref, vt_ref, kt_ref, qt_ref, qa_ref, carry_ref):
        @pl.when(pl.program_id(0) == 0)
        def _():
            carry_ref[...] = jnp.zeros_like(carry_ref)

        lf = jax.nn.log_sigmoid(fl_ref[:, :LANES] + bf_ref[...])
        c = _exact_times(l_ref[...], lf, 3) + carry_ref[...]
        carry_ref[...] += jnp.sum(lf, axis=0, keepdims=True)
        c3 = jnp.concatenate(_split3(c), axis=1)
        lane = lax.broadcasted_iota(jnp.int32, (CHUNK, LANES), 1)
        row = lax.broadcasted_iota(jnp.int32, (CHUNK, LANES), 0)
        r16 = lax.broadcasted_iota(jnp.int32, (C_TAIL, 2 * CHUNK), 0)
        l16 = lax.broadcasted_iota(jnp.int32, (C_TAIL, 2 * CHUNK), 1)
        zero = jnp.zeros((CHUNK, LANES), BF16)
        one = jnp.ones((CHUNK, LANES), BF16)

        def by_keys(x, right_a, right_b):
            xb = x.astype(BF16)
            top = jnp.concatenate([jnp.where(lane < C_HDIM, xb, zero), right_a], axis=1)
            return jnp.concatenate([top, jnp.concatenate([jnp.where(lane < C_HDIM, zero, xb), right_b], axis=1)], axis=0)

        def by_lanes(x, tail):
            xt = x.T.astype(BF16)
            main = jnp.concatenate([jnp.where(row < C_HDIM, xt, zero), jnp.where(row < C_HDIM, zero, xt)], axis=1)
            return jnp.concatenate([main, tail], axis=0)

        for p in range(C_PAIRS):
            cols = slice(p * LANES, (p + 1) * LANES)
            q2, k2, v2 = q_ref[:, cols] * (C_HDIM ** -0.5), k_ref[:, cols], v_ref[:, cols]
            negc = [_dot(c3, sel_ref[2 * p + e]).astype(BF16) for e in range(2)]
            ones3 = [jnp.where((lane >= 3 * e) & (lane < 3 * e + 3), one, zero) for e in range(2)]
            tail = jnp.where(((r16 == 2 * p) & (l16 < CHUNK)) | ((r16 == 2 * p + 1) & (l16 >= CHUNK)), 1.0, 0.0).astype(BF16)
            ka_ref[p] = by_keys(k2, negc[0], negc[1])
            va_ref[p] = by_keys(v2, ones3[0], ones3[1])
            kt_ref[p] = by_lanes(k2, tail)
            vt_ref[p] = by_lanes(v2, tail)
            qt_ref[p] = jnp.concatenate([q2.T.astype(BF16), jnp.where(row < 6, one, zero)], axis=0)
            qa_ref[p] = jnp.concatenate([q2.astype(BF16), jnp.where((lane == 2 * p) | (lane == 2 * p + 1), one, zero)], axis=1)

    wide = lambda j: pl.BlockSpec((CHUNK, C_WIDTH), lambda n: (n, j))
    by_rows = pl.BlockSpec((C_PAIRS, 2 * CHUNK, 2 * CHUNK), lambda n: (0, n, 0))
    by_cols = pl.BlockSpec((C_PAIRS, rows_t, 2 * CHUNK), lambda n: (0, 0, n))
    return pl.pallas_call(
        body,
        name="fox_prep",
        grid=(nblk,),
        in_specs=[pl.BlockSpec((CHUNK, 256), lambda n: (n, 3)), wide(4), wide(5), wide(6), _full((1, LANES)),
                  _full((CHUNK, CHUNK)), _full((C_HEADS, 3 * LANES, LANES))],
        out_specs=[by_rows, by_rows, by_cols, by_cols,
                   pl.BlockSpec((C_PAIRS, 2 * CHUNK, CHUNK), lambda n: (0, 0, n)),
                   pl.BlockSpec((C_PAIRS, CHUNK, 2 * CHUNK), lambda n: (0, n, 0))],
        out_shape=[jax.ShapeDtypeStruct((C_PAIRS, 2 * seq, 2 * CHUNK), BF16)] * 2
        + [jax.ShapeDtypeStruct((C_PAIRS, rows_t, 2 * seq), BF16)] * 2
        + [jax.ShapeDtypeStruct((C_PAIRS, 2 * CHUNK, seq), BF16), jax.ShapeDtypeStruct((C_PAIRS, seq, 2 * CHUNK), BF16)],
        scratch_shapes=[pltpu.VMEM((1, LANES), F32)],
        compiler_params=_cparams("arbitrary"),
    )(proj, proj, proj, proj, bf_row, tril, sel)


def _visible(shape, key0, query0):
    row = lax.broadcasted_iota(jnp.int32, shape, 0)
    key = key0 + lax.shift_left(lax.shift_right_logical(row, 8), 7) + (row & (CHUNK - 1))
    return key <= query0 + lax.broadcasted_iota(jnp.int32, shape, 1)


def _rows_ab(a, b, n):
    return jnp.concatenate([jnp.broadcast_to(a, (C_HDIM, n)), jnp.broadcast_to(b, (C_HDIM, n))], axis=0)


def _call_carrying(ex, body, operands, *, name, grid, in_specs, out_specs, out_shape, scratch_shapes, semantics=None):
    if ex is None:
        semantics = semantics or ("parallel", *["arbitrary"] * (len(grid) - 1))
        return pl.pallas_call(body, name=name, grid=grid, in_specs=in_specs, out_specs=out_specs, out_shape=out_shape,
                              scratch_shapes=scratch_shapes, compiler_params=_cparams(*semantics))(*operands)
    n_in, n_out = len(in_specs), len(out_specs)

    def wrapped(*refs):
        own, parts = _carried_refs(refs, n_in, n_out, ex)
        ids = [pl.program_id(a) for a in range(len(grid))]
        pl.when(functools.reduce(jnp.logical_and, [i == 0 for i in ids]))(lambda: ex.start(*parts))
        if hasattr(ex, "relay"):
            linear = functools.reduce(lambda at, ig: at * ig[1] + ig[0], zip(ids, grid), 0)
            pl.when(linear == int(np.prod(grid)) // 2)(lambda: ex.relay(*parts))
        body(*own)
        pl.when(functools.reduce(jnp.logical_and, [i == g - 1 for i, g in zip(ids, grid)]))(lambda: ex.finish(*parts))

    return pl.pallas_call(
        wrapped, name=name, grid=grid,
        in_specs=list(in_specs) + [_ANY] * len(ex.inputs), out_specs=list(out_specs) + [_ANY] * len(ex.out_shape),
        out_shape=list(out_shape) + list(ex.out_shape), scratch_shapes=list(scratch_shapes) + list(ex.scratch),
        input_output_aliases={n_in + i: n_out + o for i, o in getattr(ex, "aliases", {}).items()},
        compiler_params=_cparams(*["arbitrary"] * len(grid)),
    )(*operands, *ex.inputs)


def fox_fwd(qt, ka, vt, carried=None):
    seq = qt.shape[2]
    nblk = seq // CHUNK
    bq = min(C_BQ, seq)
    grp = bq // CHUNK
    rows_t = CHUNK + C_TAIL

    def body(qt_ref, ka_ref, vt_ref, o_ref, lse_ref, acc_ref, s_ref):
        p, i = pl.program_id(0), pl.program_id(1)
        qtile = qt_ref[0]
        r16 = lax.broadcasted_iota(jnp.int32, (C_TAIL, bq), 0)

        def scores(t):
            at = pl.multiple_of(t * grp * 2 * CHUNK, 2 * CHUNK)
            return _dot(ka_ref[0, pl.ds(at, grp * 2 * CHUNK), :], qtile)

        def group(t, m, masked):
            ma, mb = m
            at = pl.multiple_of(t * grp * 2 * CHUNK, 2 * CHUNK)
            s = s_ref[...]
            if masked:
                s = jnp.where(_visible(s.shape, t * bq, i * bq), s, -jnp.inf)
            sa = [s[g * 2 * CHUNK:g * 2 * CHUNK + CHUNK] for g in range(grp)]
            sb = [s[g * 2 * CHUNK + CHUNK:(g + 1) * 2 * CHUNK] for g in range(grp)]
            na, nb = ma, mb
            for g in range(grp):
                na = jnp.maximum(na, jnp.max(sa[g], axis=0, keepdims=True))
                nb = jnp.maximum(nb, jnp.max(sb[g], axis=0, keepdims=True))
            al_a, al_b = jnp.exp(ma - na), jnp.exp(mb - nb)
            pt = jnp.concatenate([jnp.exp(x - n) for g in range(grp) for x, n in ((sa[g], na), (sb[g], nb))], axis=0)
            pv = _dot(vt_ref[0, :, pl.ds(at, grp * 2 * CHUNK)], pt.astype(BF16))
            tail = jnp.where(r16 == 2 * p, al_a, jnp.where(r16 == 2 * p + 1, al_b, 1.0))
            acc_ref[...] = acc_ref[...] * jnp.concatenate([_rows_ab(al_a, al_b, bq), tail], axis=0) + pv
            return na, nb

        def step(t, m):
            s_next = scores(t + 1)
            m = group(t, m, False)
            s_ref[...] = s_next
            return m

        acc_ref[...] = jnp.zeros_like(acc_ref)
        s_ref[...] = scores(0)
        m = (jnp.full((1, bq), -jnp.inf, F32), jnp.full((1, bq), -jnp.inf, F32))
        m = lax.fori_loop(0, i, step, m)
        ma, mb = group(i, m, True)
        tailv = acc_ref[CHUNK:rows_t, :]
        la = jnp.sum(jnp.where(r16 == 2 * p, tailv, 0.0), axis=0, keepdims=True)
        lb = jnp.sum(jnp.where(r16 == 2 * p + 1, tailv, 0.0), axis=0, keepdims=True)
        o_ref[...] = (acc_ref[0:CHUNK, :] * _rows_ab(1.0 / la, 1.0 / lb, bq)).T
        lse_ref[0, 0:1, :] = ma + jnp.log(la)
        lse_ref[0, 1:2, :] = mb + jnp.log(lb)

    return _call_carrying(
        carried, body, (qt, ka, vt),
        name="fox_fwd",
        grid=(C_PAIRS, seq // bq),
        in_specs=[
            pl.BlockSpec((1, 2 * CHUNK, bq), lambda p, i: (p, 0, i)),
            pl.BlockSpec((1, 2 * seq, 2 * CHUNK), lambda p, i: (p, 0, 0)),
            pl.BlockSpec((1, rows_t, 2 * seq), lambda p, i: (p, 0, 0)),
        ],
        out_specs=[pl.BlockSpec((bq, LANES), lambda p, i: (i, p)), pl.BlockSpec((1, 2, bq), lambda p, i: (p, 0, i))],
        out_shape=[jax.ShapeDtypeStruct((seq, C_WIDTH), F32), jax.ShapeDtypeStruct((C_PAIRS, 2, seq), F32)],
        scratch_shapes=[pltpu.VMEM((rows_t, bq), F32), pltpu.VMEM((grp * 2 * CHUNK, bq), F32)],
    )


def fox_bwd_prep(dy, o, proj, dproj):
    seq = o.shape[0]
    ind = np.zeros((C_WIDTH, LANES), np.float32)
    for h in range(C_HEADS):
        ind[h * C_HDIM:(h + 1) * C_HDIM, h] = 1.0
    ind = jnp.asarray(ind, BF16)
    sel = _piece_selectors()
    sel = jnp.asarray(np.stack([sel[2 * p].T + sel[2 * p + 1].T for p in range(C_PAIRS)]), BF16)

    def body(dy_ref, o_ref, z_ref, ind_ref, sel_ref, _, do_ref, dz_ref, dot_ref):
        dy_c, o_v, z = dy_ref[...], o_ref[...], z_ref[...]
        sg = jax.nn.sigmoid(z)
        do = dy_c * (z * sg)
        do_ref[...] = do.astype(BF16)
        dz_ref[...] = (dy_c * o_v * (sg * (1.0 + z * (1.0 - sg)))).astype(BF16)
        prod = do * o_v
        hi = prod.astype(BF16)
        lo = (prod - hi.astype(F32)).astype(BF16)
        delta = _dot(hi, ind_ref[...]) + _dot(lo, ind_ref[...])
        d3 = jnp.concatenate(_split3(delta.T), axis=0)
        for p in range(C_PAIRS):
            tail = _dot(sel_ref[p], d3).astype(BF16)
            dot_ref[p] = jnp.concatenate([do[:, p * LANES:(p + 1) * LANES].T.astype(BF16), tail], axis=0)

    return pl.pallas_call(
        body,
        name="fox_bwd_prep",
        grid=(seq // CHUNK,),
        in_specs=[
            pl.BlockSpec((CHUNK, C_WIDTH), lambda i: (i, 1)),
            pl.BlockSpec((CHUNK, C_WIDTH), lambda i: (i, 0)),
            pl.BlockSpec((CHUNK, C_WIDTH), lambda i: (i, 7)),
            _full((C_WIDTH, LANES)), _full((C_PAIRS, LANES, 3 * LANES)), _ANY,
        ],
        out_specs=[
            pl.BlockSpec((CHUNK, C_WIDTH), lambda i: (i, 0)),
            pl.BlockSpec((CHUNK, C_WIDTH), lambda i: (i, 7)),
            pl.BlockSpec((C_PAIRS, 2 * CHUNK, CHUNK), lambda i: (0, 0, i)),
        ],
        out_shape=[jax.ShapeDtypeStruct((seq, C_WIDTH), BF16), jax.ShapeDtypeStruct(dproj.shape, BF16),
                   jax.ShapeDtypeStruct((C_PAIRS, 2 * CHUNK, seq), BF16)],
        input_output_aliases={5: 1},
        compiler_params=_cparams("parallel"),
    )(dy, o, proj, ind, sel, dproj)


def fox_bwd(ka, va, kt, qt, dot_t, qa, dob, lse, carried=None):
    seq = qt.shape[2]
    nblk = seq // CHUNK
    bq = min(C_BQ, seq)
    nq = seq // bq
    kg = min(C_KG, nblk)
    ng = nblk // kg
    rows_t = CHUNK + C_TAIL

    def body(ka_ref, va_ref, kt_ref, qt_ref, dot_ref, qa_ref, do_ref, lse_ref,
             dq_ref, dk_ref, dv_ref, dck_ref, dcq_ref, dqt_acc, dv_acc, dka_acc):
        p, jg = pl.program_id(0), pl.program_id(1)

        @pl.when(jg == 0)
        def _():
            dqt_acc[...] = jnp.zeros_like(dqt_acc)

        dv_acc[...] = jnp.zeros_like(dv_acc)
        dka_acc[...] = jnp.zeros_like(dka_acc)

        def step(i, carry, masked):
            cols = pl.ds(pl.multiple_of(i * bq, bq), bq)
            qtile, dotile = qt_ref[0, :, cols], dot_ref[0, :, cols]
            do, qa_i = do_ref[cols, :], qa_ref[0, cols, :]
            lse2 = jnp.concatenate([jnp.broadcast_to(lse_ref[0, 0:1, cols], (CHUNK, bq)),
                                    jnp.broadcast_to(lse_ref[0, 1:2, cols], (CHUNK, bq))] * kg, axis=0)
            pt = jnp.exp(_dot(ka_ref[0], qtile) - lse2)
            if masked:
                pt = jnp.where(_visible(pt.shape, jg * kg * CHUNK, i * bq), pt, 0.0)
            ds = pt * _dot(va_ref[0], dotile)
            ptb, dsb = pt.astype(BF16), ds.astype(BF16)
            dv_acc[...] += _dot(ptb, do)
            dka_acc[...] += _dot(dsb, qa_i)
            dqt_acc[:, cols] += _dot(kt_ref[0], dsb)
            return carry

        i0 = (jg * kg * CHUNK) // bq
        step(i0, 0, True)
        lax.fori_loop(i0 + 1, nq, functools.partial(step, masked=False), 0)
        lane = lax.broadcasted_iota(jnp.int32, (CHUNK, LANES), 1)
        for kb in range(kg):
            rows = slice(kb * CHUNK, (kb + 1) * CHUNK)
            ra = slice(kb * 2 * CHUNK, kb * 2 * CHUNK + CHUNK)
            rb = slice(kb * 2 * CHUNK + CHUNK, (kb + 1) * 2 * CHUNK)
            dk_ref[rows, :] = jnp.where(lane < C_HDIM, dka_acc[ra, 0:LANES], dka_acc[rb, 0:LANES]).astype(BF16)
            dv_ref[rows, :] = jnp.where(lane < C_HDIM, dv_acc[ra, :], dv_acc[rb, :]).astype(BF16)
            dck_ref[0, rows, :] = (jnp.where(lane == 2 * p, dka_acc[ra, LANES:], 0.0)
                                   + jnp.where(lane == 2 * p + 1, dka_acc[rb, LANES:], 0.0))

        @pl.when(jg == ng - 1)
        def _():
            for c in range(nq):
                dq_ref[c * bq:(c + 1) * bq, :] = (dqt_acc[0:CHUNK, c * bq:(c + 1) * bq].T * (C_HDIM ** -0.5)).astype(BF16)
            dcq_ref[0] = dqt_acc[CHUNK:rows_t, :]

    per_pair = lambda r, c: pl.BlockSpec((1, r, c), lambda p, j: (p, 0, 0))
    by_rows = pl.BlockSpec((1, kg * 2 * CHUNK, 2 * CHUNK), lambda p, j: (p, j, 0))
    by_cols = pl.BlockSpec((1, rows_t, kg * 2 * CHUNK), lambda p, j: (p, 0, j))
    return _call_carrying(
        carried, body, (ka, va, kt, qt, dot_t, qa, dob, lse),
        name="fox_bwd",
        grid=(C_PAIRS, ng),
        in_specs=[by_rows, by_rows, by_cols, per_pair(2 * CHUNK, seq), per_pair(2 * CHUNK, seq),
                  per_pair(seq, 2 * CHUNK), pl.BlockSpec((seq, LANES), lambda p, j: (0, p)), per_pair(2, seq)],
        out_specs=[pl.BlockSpec((seq, LANES), lambda p, j: (0, p)),
                   pl.BlockSpec((kg * CHUNK, LANES), lambda p, j: (j, p)),
                   pl.BlockSpec((kg * CHUNK, LANES), lambda p, j: (j, p)),
                   pl.BlockSpec((1, kg * CHUNK, LANES), lambda p, j: (p, j, 0)),
                   per_pair(C_TAIL, seq)],
        out_shape=[jax.ShapeDtypeStruct((seq, C_WIDTH), BF16)] * 3
        + [jax.ShapeDtypeStruct((C_PAIRS, seq, LANES), F32), jax.ShapeDtypeStruct((C_PAIRS, C_TAIL, seq), F32)],
        scratch_shapes=[pltpu.VMEM((rows_t, seq), F32), pltpu.VMEM((kg * 2 * CHUNK, LANES), F32),
                        pltpu.VMEM((kg * 2 * CHUNK, 2 * CHUNK), F32)],
    )


def fox_post(dcq, dck, proj, bf_row, dproj):
    seq = proj.shape[0]
    nc = seq // CHUNK
    triu = jnp.asarray(np.triu(np.ones((CHUNK, CHUNK), np.float32)), BF16)

    def body(dq_ref, dk_ref, fl_ref, bf_ref, u_ref, _, dfl_ref, dbf_ref, carry_ref):
        @pl.when(pl.program_id(0) == 0)
        def _():
            carry_ref[...] = jnp.zeros_like(carry_ref)
            dbf_ref[...] = jnp.zeros_like(dbf_ref)

        rows = (dq_ref[0] + dq_ref[1]) + (dq_ref[2] + dq_ref[3])
        dc = jnp.concatenate([rows, jnp.zeros((CHUNK - C_TAIL, CHUNK), F32)], axis=0).T
        dc = dc - ((dk_ref[0] + dk_ref[1]) + (dk_ref[2] + dk_ref[3]))
        g = _exact_times(u_ref[...], dc, 3) + carry_ref[...]
        carry_ref[...] += jnp.sum(dc, axis=0, keepdims=True)
        dfl = g * jax.nn.sigmoid(-(fl_ref[:, :LANES] + bf_ref[...]))
        dbf_ref[...] += jnp.sum(dfl, axis=0, keepdims=True)
        dfl_ref[...] = jnp.concatenate([dfl, jnp.zeros_like(dfl)], axis=1).astype(BF16)

    rev = lambda n: nc - 1 - n
    return pl.pallas_call(
        body,
        name="fox_post",
        grid=(nc,),
        in_specs=[
            pl.BlockSpec((C_PAIRS, C_TAIL, CHUNK), lambda n: (0, 0, rev(n))),
            pl.BlockSpec((C_PAIRS, CHUNK, LANES), lambda n: (0, rev(n), 0)),
            pl.BlockSpec((CHUNK, 256), lambda n: (rev(n), 3)),
            _full((1, LANES)), _full((CHUNK, CHUNK)), _ANY,
        ],
        out_specs=[pl.BlockSpec((CHUNK, 256), lambda n: (rev(n), 3)), _full((1, LANES))],
        out_shape=[jax.ShapeDtypeStruct(dproj.shape, BF16), jax.ShapeDtypeStruct((1, LANES), F32)],
        input_output_aliases={5: 0},
        scratch_shapes=[pltpu.VMEM((1, LANES), F32)],
        compiler_params=_cparams("arbitrary"),
    )(dcq, dck, proj, bf_row, triu, dproj)


N_DEV = 8
MESH = pl.DeviceIdType.MESH
_ANY = pl.BlockSpec(memory_space=pl.ANY)


def _mesh_pos():
    return lax.axis_index("x"), lax.axis_index("y"), lax.axis_index("c")


def _dev_index(px, py, pc):
    return 4 * px + 2 * py + pc


def _row_pieces(ref, rows):
    return [ref.at[idx + (pl.ds(r, rows),)] for idx in np.ndindex(*ref.shape[:-2]) for r in range(0, ref.shape[-2], rows)]


class _Transfer:
    def __init__(self, src, dst, rows, send_sem, recv_sem, to):
        self.src, self.dst, self.rows, self.sems, self.to = src, dst, rows, (send_sem, recv_sem), to

    def _copy(self, src, dst):
        return pltpu.make_async_remote_copy(src_ref=src, dst_ref=dst, send_sem=self.sems[0], recv_sem=self.sems[1],
                                            device_id=self.to, device_id_type=MESH)

    def start(self):
        for s, d in zip(_row_pieces(self.src, self.rows), _row_pieces(self.dst, self.rows), strict=True):
            self._copy(s, d).start()

    def wait_send(self):
        self._copy(self.src, self.dst).wait_send()

    def wait_recv(self):
        self._copy(self.src, self.dst).wait_recv()


def _exchange_call(ex, name):
    n_in, n_out = len(ex.inputs), len(ex.out_shape)

    def body(*refs):
        parts = refs[:n_in], refs[n_in:n_in + n_out], refs[n_in + n_out:]
        ex.start(*parts)
        getattr(ex, "relay", lambda *_: None)(*parts)
        ex.finish(*parts)

    return pl.pallas_call(body, name=name, in_specs=[_ANY] * n_in, out_specs=[_ANY] * n_out, out_shape=ex.out_shape,
                          scratch_shapes=ex.scratch, input_output_aliases=getattr(ex, "aliases", {}))(*ex.inputs)


def _carried_refs(refs, n_in, n_out, ex):
    k_in, k_out, k_sem = (len(ex.inputs), len(ex.out_shape), len(ex.scratch)) if ex else (0, 0, 0)
    a, b, c = n_in + k_in, n_in + k_in + n_out, n_in + k_in + n_out + k_out
    own = refs[:n_in] + refs[a:b] + refs[c:len(refs) - k_sem]
    return own, (refs[n_in:a], refs[b:c], refs[len(refs) - k_sem:])


class AllGatherWeights:
    def __init__(self, blocks):
        n = len(blocks)
        self.inputs = tuple(blocks)
        self.out_shape = [jax.ShapeDtypeStruct((N_DEV,) + b.shape, b.dtype) for b in blocks]
        self.scratch = ([pltpu.SemaphoreType.DMA((n, 7)), pltpu.SemaphoreType.DMA((n, 7)), pltpu.SemaphoreType.DMA((n, 2))]
                        + [pltpu.VMEM(b.shape, b.dtype) for b in blocks])

    def _plan(self, ins, outs, scratch):
        send_sems, recv_sems, local_sems, *staged = scratch
        x, y, c = _mesh_pos()
        me, sibling = (x, y, c), (x, y, 1 - c)
        chips = [(1 - x, y), (x, 1 - y), (1 - x, 1 - y)]
        every = range(len(ins))

        def copy(a, k, block, to, own=False):
            slot = outs[a].at[_dev_index(*block)]
            return _Transfer(ins[a] if own else slot, slot, ins[a].shape[-2] // 8, send_sems.at[a, k], recv_sems.at[a, k], to)

        mine = [(pltpu.make_async_copy(ins[a], staged[a], local_sems.at[a, 0]),
                 pltpu.make_async_copy(staged[a], outs[a].at[_dev_index(*me)], local_sems.at[a, 1])) for a in every]
        first = [copy(a, 1 + j, me, (*chip, c), own=True) for j, chip in enumerate(chips) for a in every]
        first += [copy(a, 0, me, sibling, own=True) for a in every]
        passed = [[copy(a, 4 + j, (*chip, c), sibling) for a in every] for j, chip in enumerate(chips)]
        return me, sibling, chips, c, every, copy, mine, first, passed

    def start(self, ins, outs, scratch):
        *_, mine, first, _ = self._plan(ins, outs, scratch)
        for to_vmem, _ in mine:
            to_vmem.start()
        for cp in first:
            cp.start()

    def relay(self, ins, outs, scratch):
        me, sibling, chips, c, every, copy, mine, first, passed = self._plan(ins, outs, scratch)
        for to_vmem, to_slot in mine:
            to_vmem.wait()
            to_slot.start()
        for j, chip in enumerate(chips):
            for a in every:
                copy(a, 1 + j, (*chip, c), me).wait_recv()
            for cp in passed[j]:
                cp.start()

    def finish(self, ins, outs, scratch):
        me, sibling, chips, c, every, copy, mine, first, passed = self._plan(ins, outs, scratch)
        for a in every:
            copy(a, 0, sibling, me).wait_recv()
        for j, chip in enumerate(chips):
            for a in every:
                copy(a, 4 + j, (*chip, 1 - c), me).wait_recv()
        for cp in first + [cp for group in passed for cp in group]:
            cp.wait_send()
        for _, to_slot in mine:
            to_slot.wait()


N_CHIP = 4


class PairExchange:
    def __init__(self, by_core, whole=()):
        self.inputs = tuple(by_core) + tuple(whole)
        self.n_by_core = len(by_core)
        self.out_shape = ([jax.ShapeDtypeStruct(a.shape[1:], a.dtype) for a in by_core]
                          + [jax.ShapeDtypeStruct(a.shape, a.dtype) for a in whole])
        n = len(self.inputs)
        self.scratch = [pltpu.SemaphoreType.DMA((n,)), pltpu.SemaphoreType.DMA((n,))]

    def _copies(self, ins, outs, sems):
        x, y, c = _mesh_pos()
        srcs = [r.at[1 - c] if a < self.n_by_core else r for a, r in enumerate(ins)]
        return [_Transfer(srcs[a], outs[a], outs[a].shape[-2], sems[0].at[a], sems[1].at[a], (x, y, 1 - c))
                for a in range(len(ins))]

    def start(self, ins, outs, sems):
        for cp in self._copies(ins, outs, sems):
            cp.start()

    def finish(self, ins, outs, sems):
        copies = self._copies(ins, outs, sems)
        for cp in copies:
            cp.wait_recv()
        for cp in copies:
            cp.wait_send()


def pair_sum(own, other, dtype, rows, name, core, layer, depth, stacked=None):
    n, n_r, n_c = other.shape

    def body(core_ref, a_ref, b_ref, *refs):
        refs[-1][0, 0] = (a_ref[0, 0] + b_ref[0]).astype(dtype)

    carried = () if stacked is None else (stacked,)
    grid_spec = pltpu.PrefetchScalarGridSpec(
        num_scalar_prefetch=1,
        grid=(n, n_r // rows),
        in_specs=[pl.BlockSpec((1, 1, rows, n_c), lambda i, r, s: (s[0], i, r, 0)),
                  pl.BlockSpec((1, rows, n_c), lambda i, r, s: (i, r, 0))] + [_ANY] * len(carried),
        out_specs=pl.BlockSpec((1, 1, rows, n_c), lambda i, r, s: (i, layer, r, 0)),
    )
    return pl.pallas_call(
        body,
        name=name,
        grid_spec=grid_spec,
        out_shape=jax.ShapeDtypeStruct((n, depth, n_r, n_c), dtype),
        input_output_aliases={3: 0} if carried else {},
        compiler_params=_cparams("parallel", "parallel"),
    )(core, own, other, *carried)


def small_sum(a, b, name):
    def body(a_ref, b_ref, o_ref):
        o_ref[...] = a_ref[...] + b_ref[...]

    return pl.pallas_call(body, name=name, out_shape=jax.ShapeDtypeStruct(a.shape, a.dtype))(a, b)


class ChipExchange:
    def __init__(self, by_chip=(), layers=(), gathered=(), stacked=()):
        stacked = tuple(stacked) or (None,) * len(by_chip)
        kept = [s for s in stacked if s is not None]
        self.inputs = tuple(by_chip) + tuple(gathered) + tuple(kept)
        self.n_by_chip, self.n_gathered = len(by_chip), len(gathered)
        self.items = [(a, l) for a in range(len(by_chip)) for l in layers[a]] + [(self.n_by_chip + g, None) for g in range(len(gathered))]
        self.out_shape = ([jax.ShapeDtypeStruct((N_CHIP - 1,) + a.shape[1:], a.dtype) for a in by_chip]
                          + [jax.ShapeDtypeStruct((N_CHIP,) + a.shape, a.dtype) for a in gathered])
        at = iter(range(self.n_by_chip + self.n_gathered, len(self.inputs)))
        self.aliases = {next(at): a for a, s in enumerate(stacked) if s is not None}
        n = len(self.items)
        self.scratch = [pltpu.SemaphoreType.DMA((n, 3)), pltpu.SemaphoreType.DMA((n, 3)),
                        pltpu.SemaphoreType.DMA((max(self.n_gathered, 1),))]

    def _plan(self, ins, outs, sems):
        x, y, c = _mesh_pos()
        chip = 2 * x + y
        n = len(self.items)

        def copy(i, k, sending):
            a, layer = self.items[i]
            px, py = x ^ ((k >> 1) & 1), y ^ (k & 1)
            if layer is not None:
                src, dst = ins[a].at[2 * px + py, layer], outs[a].at[k - 1, layer]
            else:
                src, dst = ins[a], outs[a].at[chip if sending else 2 * px + py]
            return _Transfer(src, dst, dst.shape[-2], sems[0].at[i, k - 1], sems[1].at[i, k - 1], (px, py, c))

        local = [pltpu.make_async_copy(ins[a], outs[a].at[chip], sems[2].at[a - self.n_by_chip])
                 for a in range(self.n_by_chip, self.n_by_chip + self.n_gathered)]
        return n, copy, local

    def start(self, ins, outs, sems):
        n, copy, local = self._plan(ins, outs, sems)
        for cp in local:
            cp.start()
        for k in range(1, N_CHIP):
            for a in range(n):
                copy(a, k, True).start()

    def finish(self, ins, outs, sems):
        n, copy, local = self._plan(ins, outs, sems)
        for k in range(1, N_CHIP):
            for a in range(n):
                copy(a, k, False).wait_recv()
        for k in range(1, N_CHIP):
            for a in range(n):
                copy(a, k, True).wait_send()
        for cp in local:
            cp.wait()


ADAM_LR = 0.001
ADAM_B1 = 0.9
ADAM_B2 = 0.999
ADAM_EPS = 1e-08
ADAM_WD = 0.01
ADAM_STEP = 10


def adam_reduce(parts, w, m, v, rows, name, own=None, chip=None):
    n_l, n_r, n_c = w.shape
    n_parts = parts.shape[0]

    def body(*refs):
        p_ref, w_ref, m_ref, v_ref, g_ref, d_ref, m2_ref, v2_ref = refs[-8:]
        g = p_ref[0, 0].astype(F32)
        if own is not None:
            g = refs[-9][...].reshape(rows, n_c).astype(F32) + g
        for d in range(1, n_parts):
            g = g + p_ref[d, 0].astype(F32)
        m2 = ADAM_B1 * m_ref[0] + (1.0 - ADAM_B1) * g
        v2 = ADAM_B2 * v_ref[0] + (1.0 - ADAM_B2) * (g * g)
        m_hat = m2 / (1.0 - ADAM_B1 ** ADAM_STEP)
        v_hat = v2 / (1.0 - ADAM_B2 ** ADAM_STEP)
        g_ref[0] = g
        d_ref[0] = -ADAM_LR * (m_hat / (jnp.sqrt(v_hat) + ADAM_EPS) + ADAM_WD * w_ref[0])
        m2_ref[0] = m2
        v2_ref[0] = v2

    blk = lambda: pl.BlockSpec((1, rows, n_c), lambda l, r, *_: (l, r, 0))
    in_specs = [pl.BlockSpec((n_parts, 1, rows, n_c), lambda l, r, *_: (0, l, r, 0)), blk(), blk(), blk()]
    args = (parts, w, m, v)
    if own is not None:
        in_specs = [pl.BlockSpec((1, 1, rows, n_c), lambda l, r, s: (s[0], l, r, 0))] + in_specs
        args = (chip, own) + args
    grid_spec = pltpu.PrefetchScalarGridSpec(
        num_scalar_prefetch=0 if own is None else 1, grid=(n_l, n_r // rows), in_specs=in_specs,
        out_specs=[blk(), blk(), blk(), blk()])
    return pl.pallas_call(
        body,
        name=name,
        grid_spec=grid_spec,
        out_shape=[jax.ShapeDtypeStruct(w.shape, F32)] * 4,
        compiler_params=_cparams("parallel", "parallel"),
    )(*args)


def adam_reduce_columns(parts, w, m, v, name, own, chip):
    n_l, n_r, n_c = w.shape
    n_parts = parts.shape[0]
    view = lambda a: jnp.transpose(a, (2, 0, 1))

    def body(_, own_ref, p_ref, w_ref, m_ref, v_ref, g_ref, d_ref, m2_ref, v2_ref):
        for l in range(n_l):
            g = own_ref[0, l].astype(F32) + p_ref[0, l].astype(F32)
            for d in range(1, n_parts):
                g = g + p_ref[d, l].astype(F32)
            g = g.T
            w_l, m_l, v_l = w_ref[:, l, :], m_ref[:, l, :], v_ref[:, l, :]
            m2 = ADAM_B1 * m_l + (1.0 - ADAM_B1) * g
            v2 = ADAM_B2 * v_l + (1.0 - ADAM_B2) * (g * g)
            m_hat = m2 / (1.0 - ADAM_B1 ** ADAM_STEP)
            v_hat = v2 / (1.0 - ADAM_B2 ** ADAM_STEP)
            g_ref[:, l, :] = g
            d_ref[:, l, :] = -ADAM_LR * (m_hat / (jnp.sqrt(v_hat) + ADAM_EPS) + ADAM_WD * w_l)
            m2_ref[:, l, :] = m2
            v2_ref[:, l, :] = v2

    blk = lambda: pl.BlockSpec((LANES, n_l, n_r), lambda c, s: (c, 0, 0))
    grid_spec = pltpu.PrefetchScalarGridSpec(
        num_scalar_prefetch=1, grid=(pl.cdiv(n_c, LANES),),
        in_specs=[pl.BlockSpec((1, n_l, n_r, LANES), lambda c, s: (s[0], 0, 0, c)),
                  pl.BlockSpec((n_parts, n_l, n_r, LANES), lambda c, s: (0, 0, 0, c)), blk(), blk(), blk()],
        out_specs=[blk(), blk(), blk(), blk()])
    outs = pl.pallas_call(
        body,
        name=name,
        grid_spec=grid_spec,
        out_shape=[jax.ShapeDtypeStruct((n_c, n_l, n_r), F32)] * 4,
        compiler_params=_cparams("parallel"),
    )(chip, own, parts, view(w), view(m), view(v))
    return [jnp.transpose(o, (1, 2, 0)) for o in outs]


_SMALL = (("norm_g", (2, 1024)), ("gmlp_ln_g", (2, 4, 64)), ("gmlp_ln_b", (2, 4, 64)),
          ("gmlp_b_s", (2, 4, 128)), ("hgrn_lb", (2, 256)), ("hgrn_onorm_g", (2, 64)), ("fox_b_f", (2, 8)),
          ("final_norm_g", (1024,)), ("loss", ()))


def _padded(n):
    return -(-n // LANES) * LANES


_SMALL_ROWS = -(-sum(_padded(int(np.prod(s))) for _, s in _SMALL) // LANES // 8) * 8


def _pack_small(vals):
    flat = []
    for (name, shape), a in zip(_SMALL, vals, strict=True):
        n = int(np.prod(shape))
        flat.append(jnp.pad(a.reshape(n).astype(F32), (0, _padded(n) - n)))
    flat = jnp.concatenate(flat)
    return jnp.pad(flat, (0, _SMALL_ROWS * LANES - flat.shape[0])).reshape(_SMALL_ROWS, LANES)


def _unpack_small(slab):
    flat, out, at = slab.reshape(-1), {}, 0
    for name, shape in _SMALL:
        n = int(np.prod(shape))
        out[name] = flat[at:at + n].reshape(shape)
        at += _padded(n)
    return out


def sum_parts(parts, name):
    def body(p_ref, o_ref):
        g = p_ref[0]
        for d in range(1, parts.shape[0]):
            g = g + p_ref[d]
        o_ref[...] = g

    return pl.pallas_call(body, name=name, out_shape=jax.ShapeDtypeStruct(parts.shape[1:], F32))(parts)


def adam_small(gs, ws, ms, vs):
    n = len(gs)

    def body(*refs):
        for k in range(n):
            g, w, m, v = (refs[j * n + k][...] for j in range(4))
            m2 = ADAM_B1 * m + (1.0 - ADAM_B1) * g
            v2 = ADAM_B2 * v + (1.0 - ADAM_B2) * (g * g)
            m_hat = m2 / (1.0 - ADAM_B1 ** ADAM_STEP)
            v_hat = v2 / (1.0 - ADAM_B2 ** ADAM_STEP)
            refs[4 * n + k][...] = -ADAM_LR * (m_hat / (jnp.sqrt(v_hat) + ADAM_EPS) + ADAM_WD * w)
            refs[5 * n + k][...] = m2
            refs[6 * n + k][...] = v2

    outs = pl.pallas_call(body, name="adam_small",
                          out_shape=[jax.ShapeDtypeStruct(w.shape, F32) for _ in range(3) for w in ws])(*gs, *ws, *ms, *vs)
    return outs[:n], outs[n:2 * n], outs[2 * n:]


def kernel(x, norm_g, w_in, w_out, gmlp_ln_g, gmlp_ln_b, gmlp_w_s, gmlp_b_s, hgrn_lb, hgrn_onorm_g, fox_b_f, final_norm_g, loss_target, m_norm_g, m_w_in, m_w_out, m_gmlp_ln_g, m_gmlp_ln_b, m_gmlp_w_s, m_gmlp_b_s, m_hgrn_lb, m_hgrn_onorm_g, m_fox_b_f, m_final_norm_g, v_norm_g, v_w_in, v_w_out, v_gmlp_ln_g, v_gmlp_ln_b, v_gmlp_w_s, v_gmlp_b_s, v_hgrn_lb, v_hgrn_onorm_g, v_fox_b_f, v_final_norm_g):
    depth = w_in.shape[0]
    seq = x.shape[1]
    assert w_in.shape[2] * N_DEV == N_IN
    xs, tgt = x[0], loss_target[0]

    wi_blk, wo_blk = w_in.astype(BF16), w_out.astype(BF16)
    (wi_all,) = _exchange_call(AllGatherWeights([wi_blk[0]]), "allgather_weights_0")

    ln_g = gmlp_ln_g.reshape(depth, 1, A_WIDTH)
    ln_b = gmlp_ln_b.reshape(depth, 1, A_WIDTH)
    bs_t = jnp.pad(jnp.transpose(gmlp_b_s, (0, 2, 1)), ((0, 0), (0, 0), (0, LANES - A_GROUPS)))
    lb0, lb1 = hgrn_lb[0:1], hgrn_lb[1:2]
    onorm = jnp.tile(hgrn_onorm_g, (1, B_HEADS)).reshape(depth, 1, B_WIDTH)
    bf_row = jnp.pad(fox_b_f, ((0, 0), (0, LANES - C_HEADS))).reshape(depth, 1, LANES)

    core = lax.axis_index("c").astype(jnp.int32).reshape(1)
    chip = (2 * lax.axis_index("x") + lax.axis_index("y")).astype(jnp.int32).reshape(1)

    saved = []
    xc = xs
    for l in range(depth):
        wi_int = assemble_w_in(wi_all[:, None])
        proj, h = inproj(xc, norm_g[l:l + 1], wi_int, 0)
        ya = gmlp_fwd(proj, ln_g[l], ln_b[l], gmlp_w_s[l], bs_t[l])
        yb, states = hgrn_fwd(proj, lb0, lb1, onorm[l], l)
        ka, va, vt, kt, qt, qa = fox_prep(proj, bf_row[l])
        ride = ([wo_blk] if l == 0 else []) + ([wi_blk[l + 1]] if l + 1 < depth else [])
        o, lse, *gathered = fox_fwd(qt, ka, vt, AllGatherWeights(ride) if ride else None)
        if l == 0:
            wo_all = gathered.pop(0)
        if gathered:
            (wi_all,) = gathered
        xn, yfull = outproj(xc, ya, yb, o, proj, wo_all, l)
        saved.append((xc, proj, h, states, ka, va, kt, qt, qa, o, lse, yfull, wi_int))
        xc = xn

    dx, d_final_g, loss_tile = final_loss(xc, final_norm_g[None], tgt)

    n_shard = w_in.shape[2]
    g_norm = [None] * depth
    g_ln_g, g_ln_b, g_ws, g_bs, g_on, g_bf = ([None] * depth for _ in range(6))
    g_lb0, g_lb1 = jnp.zeros_like(lb0), jnp.zeros_like(lb1)
    swi = swo = rwi = rwo = None
    for l in reversed(range(depth)):
        x_in, proj, h, states, ka, va, kt, qt, qa, o, lse, yfull, wi_int = saved[l]
        dy, gwo = outproj_bwd(dx, yfull, wo_all, l)
        dproj, g_ln_g[l], g_ln_b[l], g_ws[l], dbs_t = gmlp_bwd(proj, dy, ln_g[l], ln_b[l], gmlp_w_s[l], bs_t[l])
        g_bs[l] = dbs_t[:, :A_GROUPS].T
        if l > 0:
            (qwo,) = _exchange_call(PairExchange([gwo]), f"pair_exchange_w_out_{l}")
        else:
            gws = jnp.stack(g_ws).reshape(-1, LANES)
            qwo, qws = _exchange_call(PairExchange([gwo], [gws]), f"pair_exchange_w_out_{l}")
            sws = small_sum(gws, qws, "pair_sum_w_s")
        swo = pair_sum(gwo, qwo, BF16, gwo.shape[2], "pair_sum_w_out", core, l, depth, swo)
        dproj, d0, d1, don = hgrn_bwd(proj, states, dy, lb0, lb1, onorm[l], l, dproj)
        g_lb0, g_lb1 = g_lb0 + d0, g_lb1 + d1
        g_on[l] = don.reshape(B_HEADS, B_KDIM).sum(0)
        dob, dproj, dot_t = fox_bwd_prep(dy, o, proj, dproj)
        top = l == depth - 1
        ride = ChipExchange([swo] if top else [swi, swo], [(l,)] if top else [(l + 1,), (l,)],
                            [sws] if l == 0 else [], [rwo] if top else [rwi, rwo])
        outs = fox_bwd(ka, va, kt, qt, dot_t, qa, dob, lse, ride)
        dqkv, (dck, dcq), got = outs[:3], outs[3:5], list(outs[5:])
        if not top:
            rwi = got.pop(0)
        rwo = got.pop(0)
        if l == 0:
            (rws,) = got
        dproj, dbf = fox_post(dcq, dck, proj, bf_row[l], dproj)
        g_bf[l] = dbf[0, :C_HEADS]
        gwi = split_w_in_grad(inproj_bwd_w(h, dproj, dqkv), n_shard)[:, :, 0]
        (qwi,) = _exchange_call(PairExchange([gwi]), f"pair_exchange_w_in_{l}")
        swi = pair_sum(gwi, qwi, BF16, 256, "pair_sum_w_in", core, l, depth, swi)
        ride = ChipExchange([swi], [(l,)], stacked=[rwi]) if l == 0 else None
        outs = inproj_bwd_x(dproj, dqkv, wi_int, x_in, norm_g[l:l + 1], dx, 0, ride)
        dx, g_norm[l] = outs[:2]
        if ride is not None:
            (rwi,) = outs[2:]

    gsm = _pack_small([
        jnp.concatenate(g_norm), jnp.stack(g_ln_g), jnp.stack(g_ln_b), jnp.stack(g_bs),
        jnp.concatenate([g_lb0, g_lb1]), jnp.stack(g_on), jnp.stack(g_bf), d_final_g, loss_tile[0, 0]])
    (qsm,) = _exchange_call(PairExchange([], [gsm]), "pair_exchange_small")
    ssm = small_sum(gsm, qsm, "pair_sum_small")
    (rsm,) = _exchange_call(ChipExchange(gathered=[ssm]), "chip_exchange_small")

    small_w = (norm_g, gmlp_ln_g, gmlp_ln_b, gmlp_b_s, hgrn_lb, hgrn_onorm_g, fox_b_f, final_norm_g)
    small_m = (m_norm_g, m_gmlp_ln_g, m_gmlp_ln_b, m_gmlp_b_s, m_hgrn_lb, m_hgrn_onorm_g, m_fox_b_f, m_final_norm_g)
    small_v = (v_norm_g, v_gmlp_ln_g, v_gmlp_ln_b, v_gmlp_b_s, v_hgrn_lb, v_hgrn_onorm_g, v_fox_b_f, v_final_norm_g)
    res_wi = adam_reduce_columns(rwi, w_in, m_w_in, v_w_in, "adam_w_in", swi, chip)
    res_wo = adam_reduce(rwo, w_out, m_w_out, v_w_out, w_out.shape[1], "adam_w_out", own=swo, chip=chip)
    grads = _unpack_small(sum_parts(rsm, "sum_small"))
    names = [name for name, _ in _SMALL if name != "loss"]
    rows = lambda a: a.reshape(1, -1) if a.ndim == 1 else a
    res_sm = adam_small([rows(grads[k]) for k in names], *([rows(a) for a in wmv] for wmv in (small_w, small_m, small_v)))
    res_sm = [grads] + [{k: a.reshape(grads[k].shape) for k, a in zip(names, r, strict=True)} for r in res_sm]
    as_rows = lambda a: a.reshape(1, -1, LANES)
    res_ws = adam_reduce(rws[:, None], as_rows(gmlp_w_s), as_rows(m_gmlp_w_s), as_rows(v_gmlp_w_s), rws.shape[1], "adam_w_s")
    for s, r in zip(res_sm, res_ws, strict=True):
        s["gmlp_w_s"] = r.reshape(gmlp_w_s.shape)

    def group(i):
        s = res_sm[i]
        return [s["norm_g"], res_wi[i], res_wo[i], s["gmlp_ln_g"], s["gmlp_ln_b"], s["gmlp_w_s"], s["gmlp_b_s"],
                s["hgrn_lb"], s["hgrn_onorm_g"], s["fox_b_f"], s["final_norm_g"]]

    return (res_sm[0]["loss"], dx[None], *group(0), *group(1), *group(2), *group(3))
```

```python
import functools

import jax
import jax.numpy as jnp
import numpy as np
from jax import lax
from jax.experimental import pallas as pl
from jax.experimental.pallas import tpu as pltpu

F32 = jnp.float32
BF16 = jnp.bfloat16

NORM_EPS = 1e-6
F_FLOOR = 1e-30
CHUNK = 128
LANES = 128
VMEM_LIMIT = 56 * 1024 * 1024


def _cparams(*sem):
    return pltpu.CompilerParams(dimension_semantics=sem, vmem_limit_bytes=VMEM_LIMIT)


def _dot(a, b, dims=(((1,), (0,)), ((), ())), precision=None):
    return lax.dot_general(a, b, dims, precision=precision, preferred_element_type=F32)


_NT = (((1,), (1,)), ((), ()))
_TN = (((0,), (0,)), ((), ()))


def _bf16_pieces(x, n):
    out, r = [], x
    for i in range(n):
        out.append(r.astype(BF16))
        if i + 1 < n:
            r = r - out[-1].astype(F32)
    return out


@functools.partial(jax.custom_vjp, nondiff_argnums=(2,))
def _times_exact(x, e, n):
    return functools.reduce(jnp.add, [_dot(p, e) for p in _bf16_pieces(x, n)])


def _times_exact_fwd(x, e, n):
    return _times_exact(x, e, n), e


def _times_exact_bwd(n, e, g):
    dx = functools.reduce(jnp.add, [lax.dot_general(p, e, _NT, preferred_element_type=F32) for p in _bf16_pieces(g, n)])
    return dx, jnp.zeros_like(e)


_times_exact.defvjp(_times_exact_fwd, _times_exact_bwd)


@functools.partial(jax.custom_vjp, nondiff_argnums=(2,))
def _exact_times(e, x, n):
    return functools.reduce(jnp.add, [_dot(e, p) for p in _bf16_pieces(x, n)])


def _exact_times_fwd(e, x, n):
    return _exact_times(e, x, n), e


def _exact_times_bwd(n, e, g):
    dx = functools.reduce(jnp.add, [lax.dot_general(e, p, _TN, preferred_element_type=F32) for p in _bf16_pieces(g, n)])
    return jnp.zeros_like(e), dx


_exact_times.defvjp(_exact_times_fwd, _exact_times_bwd)


def _group_mean_matrix(width, group):
    idx = np.arange(width) // group
    return jnp.asarray((idx[:, None] == idx[None, :]).astype(np.float32) / group, BF16)


def _group_ones_matrix(width, group):
    idx = np.arange(width) // group
    return jnp.asarray((idx[:, None] == idx[None, :]).astype(np.float32), BF16)


A_WIDTH = 256
A_GROUPS = 4
A_GDIM = 64


A_ROWS = 512


def _gmlp_chunk(x3, ln_g, ln_b, w_s, bs_t, mean_m, gind):
    n = x3.shape[0] // CHUNK
    u = jax.nn.gelu(x3[:, :A_WIDTH])
    v = jax.nn.gelu(x3[:, A_WIDTH:2 * A_WIDTH])
    z = x3[:, 2 * A_WIDTH:]
    mu = _times_exact(v, mean_m, 2)
    d = v - mu
    var = _times_exact(d * d, mean_m, 2)
    vn = d * lax.rsqrt(var + NORM_EPS) * ln_g + ln_b
    vnb = vn.astype(BF16)
    wide = jnp.concatenate([vnb[i * CHUNK:(i + 1) * CHUNK] for i in range(n)], axis=1)
    row = lax.broadcasted_iota(jnp.int32, (CHUNK, CHUNK), 0)
    col = lax.broadcasted_iota(jnp.int32, (CHUNK, CHUNK), 1)
    causal = row >= col
    lane_g = lax.shift_right_logical(lax.broadcasted_iota(jnp.int32, (CHUNK, n * A_WIDTH), 1), 6) & (A_GROUPS - 1)
    bias = _times_exact(bs_t, gind, 3)
    mixed = jnp.concatenate([bias] * n, axis=1)
    for g in range(A_GROUPS):
        wc = jnp.where(causal, w_s[g], 0.0).astype(BF16)
        mixed = mixed + jnp.where(lane_g == g, _dot(wc, wide), 0.0)
    mixed = jnp.concatenate([mixed[:, i * A_WIDTH:(i + 1) * A_WIDTH] for i in range(n)], axis=0)
    return u * mixed * jax.nn.silu(z)


def _gmlp_consts():
    gind = np.zeros((LANES, A_WIDTH), np.float32)
    for g in range(A_GROUPS):
        gind[g, g * A_GDIM:(g + 1) * A_GDIM] = 1.0
    return _group_mean_matrix(A_WIDTH, A_GDIM), jnp.asarray(gind, BF16)


def _full(shape):
    return pl.BlockSpec(shape, lambda *_: (0,) * len(shape))


def gmlp_fwd(proj, ln_g, ln_b, w_s, bs_t):
    seq = proj.shape[0]
    rows = min(A_ROWS, seq)
    mean_m, gind = _gmlp_consts()

    def body(x_ref, g_ref, b_ref, w_ref, bs_ref, m_ref, gi_ref, y_ref):
        y = _gmlp_chunk(x_ref[...], g_ref[...], b_ref[...], w_ref[...], bs_ref[...], m_ref[...], gi_ref[...])
        y_ref[...] = y.astype(BF16)

    return pl.pallas_call(
        body,
        name="gmlp_fwd",
        grid=(seq // rows,),
        in_specs=[
            pl.BlockSpec((rows, 3 * A_WIDTH), lambda n: (n, 0)),
            _full((1, A_WIDTH)), _full((1, A_WIDTH)), _full((A_GROUPS, CHUNK, CHUNK)), _full((CHUNK, LANES)),
            _full((A_WIDTH, A_WIDTH)), _full((LANES, A_WIDTH)),
        ],
        out_specs=pl.BlockSpec((rows, A_WIDTH), lambda n: (n, 0)),
        out_shape=jax.ShapeDtypeStruct((seq, A_WIDTH), BF16),
        compiler_params=_cparams("parallel"),
    )(proj, ln_g, ln_b, w_s, bs_t, mean_m, gind)


def gmlp_bwd(proj, dy, ln_g, ln_b, w_s, bs_t):
    seq = proj.shape[0]
    rows = min(A_ROWS, seq)
    mean_m, gind = _gmlp_consts()

    def body(x_ref, dy_ref, g_ref, b_ref, w_ref, bs_ref, m_ref, gi_ref, dx_ref, dg_ref, db_ref, dw_ref, dbs_ref):
        fn = functools.partial(_gmlp_chunk, mean_m=m_ref[...], gind=gi_ref[...])
        _, vjp = jax.vjp(fn, x_ref[...], g_ref[...], b_ref[...], w_ref[...], bs_ref[...])
        dx, dg, db, dw, dbs = vjp(dy_ref[...])
        dx_ref[...] = dx.astype(BF16)

        @pl.when(pl.program_id(0) == 0)
        def _():
            dg_ref[...] = jnp.zeros_like(dg_ref)
            db_ref[...] = jnp.zeros_like(db_ref)
            dw_ref[...] = jnp.zeros_like(dw_ref)
            dbs_ref[...] = jnp.zeros_like(dbs_ref)

        dg_ref[...] += dg
        db_ref[...] += db
        dw_ref[...] += dw
        dbs_ref[...] += dbs

    return pl.pallas_call(
        body,
        name="gmlp_bwd",
        grid=(seq // rows,),
        in_specs=[
            pl.BlockSpec((rows, 3 * A_WIDTH), lambda n: (n, 0)),
            pl.BlockSpec((rows, A_WIDTH), lambda n: (n, 0)),
            _full((1, A_WIDTH)), _full((1, A_WIDTH)), _full((A_GROUPS, CHUNK, CHUNK)), _full((CHUNK, LANES)),
            _full((A_WIDTH, A_WIDTH)), _full((LANES, A_WIDTH)),
        ],
        out_specs=[
            pl.BlockSpec((rows, 3 * A_WIDTH), lambda n: (n, 0)),
            _full((1, A_WIDTH)), _full((1, A_WIDTH)), _full((A_GROUPS, CHUNK, CHUNK)), _full((CHUNK, LANES)),
        ],
        out_shape=[
            jax.ShapeDtypeStruct((seq, D_INT), BF16),
            jax.ShapeDtypeStruct((1, A_WIDTH), F32), jax.ShapeDtypeStruct((1, A_WIDTH), F32),
            jax.ShapeDtypeStruct((A_GROUPS, CHUNK, CHUNK), F32), jax.ShapeDtypeStruct((CHUNK, LANES), F32),
        ],
        compiler_params=_cparams("arbitrary"),
    )(proj, dy, ln_g, ln_b, w_s, bs_t, mean_m, gind)


B_WIDTH = 256
B_HEADS = 4
B_KDIM = 64
B_LEVELS = (64, 32, 16, 8, 4, 2, 1)


def _hgrn_consts():
    t = np.arange(CHUNK)
    u = t[None, :]
    mats = [np.tril(np.ones((CHUNK, CHUNK), np.float32))]
    for m in B_LEVELS:
        p = (t // (2 * m)) * (2 * m) + m - 1
        right = (t % (2 * m)) >= m
        sel = np.where(right[:, None], (u > p[:, None]) & (u <= t[:, None]), (u > t[:, None]) & (u <= p[:, None]))
        mats.append(sel.astype(np.float32))
    return jnp.asarray(np.concatenate(mats, 0), BF16), _group_ones_matrix(B_WIDTH, B_KDIM)


def _hgrn_lower_bound(lb0, lb1, layer):
    mx = jnp.maximum(lb0, lb1)
    e0 = jnp.exp(lb0 - mx)
    e1 = jnp.exp(lb1 - mx)
    p0 = e0 / (e0 + e1)
    p1 = e1 / (e0 + e1)
    cs = p0 if layer == 0 else p0 + p1
    return jnp.clip(cs - p0, 0.0, 1.0 - 1e-6)


def _hgrn_chunk(x4, st, lb0, lb1, onorm, layer, tstack, ones_bd):
    q_raw, fl, v, zg = (x4[:, i * B_WIDTH:(i + 1) * B_WIDTH] for i in range(4))
    lb = _hgrn_lower_bound(lb0, lb1, layer)
    q = jax.nn.silu(q_raw) * (B_KDIM ** -0.5)
    f = lb + (1.0 - lb) * jax.nn.sigmoid(fl)
    logf = jnp.log(jnp.maximum(f, F_FLOOR))
    k = (1.0 - lb) * jax.nn.sigmoid(-fl)
    b = _exact_times(tstack[:CHUNK], logf, 3)
    dall = jnp.concatenate([b, _exact_times(tstack[CHUNK:], logf, 2)], axis=0)
    b_last = jnp.sum(logf, axis=0, keepdims=True)
    vb = v.astype(BF16)

    lane_h = lax.shift_right_logical(lax.broadcasted_iota(jnp.int32, (CHUNK, B_WIDTH), 1), 6)
    row = lax.broadcasted_iota(jnp.int32, (CHUNK, B_WIDTH), 0)
    srow = lax.broadcasted_iota(jnp.int32, (B_HEADS * CHUNK, CHUNK), 0) & (CHUNK - 1)
    scol = lax.broadcasted_iota(jnp.int32, (B_HEADS * CHUNK, CHUNK), 1)

    def heads_on_rows(a):
        return jnp.concatenate([jnp.where(lane_h == h, a, 0.0) for h in range(B_HEADS)], axis=0)

    def heads_from_rows(r):
        out = jnp.where(lane_h == 0, r[:CHUNK], 0.0)
        for h in range(1, B_HEADS):
            out = out + jnp.where(lane_h == h, r[h * CHUNK:(h + 1) * CHUNK], 0.0)
        return out

    o = lax.dot_general((q * jnp.exp(b)).astype(BF16), st.astype(BF16), _NT, preferred_element_type=F32)
    scores = jnp.zeros((B_HEADS * CHUNK, CHUNK), F32)
    for li, m in enumerate(B_LEVELS):
        e = jnp.exp(dall[(li + 1) * CHUNK:(li + 2) * CHUNK])
        right = (row & (2 * m - 1)) >= m
        qt = jnp.where(right, q * e, 0.0)
        kt = jnp.where(right, 0.0, k * e)
        sc = lax.dot_general(heads_on_rows(qt).astype(BF16), kt.astype(BF16), _NT, preferred_element_type=F32)
        sh = int(np.log2(2 * m))
        same = lax.shift_right_logical(srow, sh) == lax.shift_right_logical(scol, sh)
        scores = scores + jnp.where(same, sc, 0.0)
    o = o + heads_from_rows(_dot(scores.astype(BF16), vb))
    o = o + _times_exact(q * k, ones_bd, 2) * v

    kv = lax.dot_general(vb, (k * jnp.exp(b_last - b)).astype(BF16), _TN, preferred_element_type=F32)
    st_new = st * jnp.exp(b_last) + jnp.where(ones_bd > 0.5, kv, 0.0)

    ms = _times_exact(o * o, ones_bd, 2) * (1.0 / B_KDIM)
    y = o * lax.rsqrt(ms + NORM_EPS) * onorm * jax.nn.silu(zg)
    return y, st_new


B_ROWS = 256


def _hgrn_rows(x4, st, lb0, lb1, onorm, layer, tstack, ones_bd):
    ys = []
    for i in range(x4.shape[0] // CHUNK):
        y, st = _hgrn_chunk(x4[i * CHUNK:(i + 1) * CHUNK], st, lb0, lb1, onorm, layer, tstack, ones_bd)
        ys.append(y)
    return jnp.concatenate(ys, axis=0), st


def hgrn_fwd(proj, lb0, lb1, onorm, layer):
    seq = proj.shape[0]
    rows = min(B_ROWS, seq)
    nc = seq // rows
    tstack, ones_bd = _hgrn_consts()

    def body(x_ref, lb0_ref, lb1_ref, on_ref, t_ref, e_ref, y_ref, st_out_ref, st_ref):
        @pl.when(pl.program_id(0) == 0)
        def _():
            st_ref[...] = jnp.zeros_like(st_ref)

        st = st_ref[...]
        st_out_ref[0] = st
        y, st_new = _hgrn_rows(x_ref[...], st, lb0_ref[...], lb1_ref[...], on_ref[...], layer, t_ref[...], e_ref[...])
        y_ref[...] = y.astype(BF16)
        st_ref[...] = st_new

    return pl.pallas_call(
        body,
        name=f"hgrn_fwd_{layer}",
        grid=(nc,),
        in_specs=[
            pl.BlockSpec((rows, 4 * B_WIDTH), lambda n: (n, 1)),
            _full((1, B_WIDTH)), _full((1, B_WIDTH)), _full((1, B_WIDTH)),
            _full(((len(B_LEVELS) + 1) * CHUNK, CHUNK)), _full((B_WIDTH, B_WIDTH)),
        ],
        out_specs=[
            pl.BlockSpec((rows, B_WIDTH), lambda n: (n, 0)),
            pl.BlockSpec((1, B_WIDTH, B_WIDTH), lambda n: (n, 0, 0)),
        ],
        out_shape=[jax.ShapeDtypeStruct((seq, B_WIDTH), BF16), jax.ShapeDtypeStruct((nc, B_WIDTH, B_WIDTH), F32)],
        scratch_shapes=[pltpu.VMEM((B_WIDTH, B_WIDTH), F32)],
        compiler_params=_cparams("arbitrary"),
    )(proj, lb0, lb1, onorm, tstack, ones_bd)


def hgrn_bwd(proj, states, dy, lb0, lb1, onorm, layer, dproj):
    seq = proj.shape[0]
    rows = min(B_ROWS, seq)
    nc = seq // rows
    tstack, ones_bd = _hgrn_consts()

    def body(x_ref, st_in_ref, dy_ref, lb0_ref, lb1_ref, on_ref, t_ref, e_ref, _, dx_ref, d0_ref, d1_ref, don_ref, dst_ref):
        @pl.when(pl.program_id(0) == 0)
        def _():
            dst_ref[...] = jnp.zeros_like(dst_ref)
            d0_ref[...] = jnp.zeros_like(d0_ref)
            d1_ref[...] = jnp.zeros_like(d1_ref)
            don_ref[...] = jnp.zeros_like(don_ref)

        fn = functools.partial(_hgrn_rows, layer=layer, tstack=t_ref[...], ones_bd=e_ref[...])
        _, vjp = jax.vjp(fn, x_ref[...], st_in_ref[0], lb0_ref[...], lb1_ref[...], on_ref[...])
        dx, dst, d0, d1, don = vjp((dy_ref[...], dst_ref[...]))
        dx_ref[...] = dx.astype(BF16)
        dst_ref[...] = dst
        d0_ref[...] += d0
        d1_ref[...] += d1
        don_ref[...] += don

    rev = lambda n: nc - 1 - n
    return pl.pallas_call(
        body,
        name=f"hgrn_bwd_{layer}",
        grid=(nc,),
        in_specs=[
            pl.BlockSpec((rows, 4 * B_WIDTH), lambda n: (rev(n), 1)),
            pl.BlockSpec((1, B_WIDTH, B_WIDTH), lambda n: (rev(n), 0, 0)),
            pl.BlockSpec((rows, B_WIDTH), lambda n: (rev(n), 1)),
            _full((1, B_WIDTH)), _full((1, B_WIDTH)), _full((1, B_WIDTH)),
            _full(((len(B_LEVELS) + 1) * CHUNK, CHUNK)), _full((B_WIDTH, B_WIDTH)), _ANY,
        ],
        out_specs=[
            pl.BlockSpec((rows, 4 * B_WIDTH), lambda n: (rev(n), 1)),
            _full((1, B_WIDTH)), _full((1, B_WIDTH)), _full((1, B_WIDTH)),
        ],
        out_shape=[jax.ShapeDtypeStruct(dproj.shape, BF16)] + [jax.ShapeDtypeStruct((1, B_WIDTH), F32)] * 3,
        input_output_aliases={8: 0},
        scratch_shapes=[pltpu.VMEM((B_WIDTH, B_WIDTH), F32)],
        compiler_params=_cparams("arbitrary"),
    )(proj, states, dy, lb0, lb1, onorm, tstack, ones_bd, dproj)


D_MODEL = 1024
D_INT = 4096


def _rms_stats(xf):
    r = lax.rsqrt(jnp.mean(xf * xf, axis=-1, keepdims=True) + NORM_EPS)
    return r, xf * r


def _rms_bwd(dy, g, r, xh):
    u = dy * g
    return r * (u - xh * jnp.mean(u * xh, axis=-1, keepdims=True))


def inproj(x, g, w, layer):
    seq = x.shape[0]
    tm = min(seq, 512)

    def body(x_ref, g_ref, w_ref, p_ref, h_ref):
        _, xh = _rms_stats(x_ref[...])
        h = (xh * g_ref[...]).astype(BF16)
        h_ref[...] = h
        p_ref[...] = _dot(h, w_ref[0])

    return pl.pallas_call(
        body,
        name="inproj",
        grid=(seq // tm,),
        in_specs=[
            pl.BlockSpec((tm, D_MODEL), lambda i: (i, 0)),
            _full((1, D_MODEL)),
            pl.BlockSpec((1, D_MODEL, D_INT), lambda i: (layer, 0, 0)),
        ],
        out_specs=[pl.BlockSpec((tm, D_INT), lambda i: (i, 0)), pl.BlockSpec((tm, D_MODEL), lambda i: (i, 0))],
        out_shape=[jax.ShapeDtypeStruct((seq, D_INT), F32), jax.ShapeDtypeStruct((seq, D_MODEL), BF16)],
        compiler_params=_cparams("parallel"),
    )(x, g, w)


def outproj(x, ya, yb, o, proj, wo, layer, head=None):
    seq = x.shape[0]
    tm = min(seq, 512)
    blk = wo.shape[2]

    def body(x_ref, ya_ref, yb_ref, o_ref, z_ref, w_ref, *refs):
        yc = (o_ref[...] * jax.nn.silu(z_ref[...])).astype(BF16)
        y = jnp.concatenate([ya_ref[...], yb_ref[...], yc], axis=1)
        w = jnp.concatenate([w_ref[d, 0] for d in range(N_DEV)], axis=0)
        xn = x_ref[...] + _dot(y, w)
        if head is None:
            xn_ref, y_ref = refs
            xn_ref[...] = xn
        else:
            g_ref, t_ref, dx_ref, y_ref, dg_ref, loss_ref = refs

            @pl.when(pl.program_id(0) == 0)
            def _():
                dg_ref[...] = jnp.zeros_like(dg_ref)
                loss_ref[...] = jnp.zeros_like(loss_ref)

            g = g_ref[...]
            r, xh = _rms_stats(xn)
            err = xh * g - t_ref[...]
            sq = jnp.sum(jnp.sum(err * err, axis=1, keepdims=True), axis=0, keepdims=True)
            loss_ref[...] += jnp.broadcast_to(sq * (0.5 / D_MODEL), loss_ref.shape)
            dout = err * (1.0 / D_MODEL)
            dg_ref[...] += jnp.sum(dout * xh, axis=0, keepdims=True)
            dx_ref[...] = _rms_bwd(dout, g, r, xh)
        y_ref[...] = y

    rows = lambda: pl.BlockSpec((tm, D_MODEL), lambda i: (i, 0))
    tail = (() if head is None else (_full((1, D_MODEL)), rows()),
            () if head is None else (_full((1, D_MODEL)), _full((8, LANES))),
            () if head is None else (jax.ShapeDtypeStruct((1, D_MODEL), F32), jax.ShapeDtypeStruct((8, LANES), F32)))
    return pl.pallas_call(
        body,
        name="outproj" if head is None else "outproj_loss",
        grid=(seq // tm,),
        in_specs=[
            rows(),
            pl.BlockSpec((tm, 256), lambda i: (i, 0)),
            pl.BlockSpec((tm, 256), lambda i: (i, 0)),
            pl.BlockSpec((tm, 512), lambda i: (i, 0)),
            pl.BlockSpec((tm, 512), lambda i: (i, 7)),
            pl.BlockSpec((N_DEV, 1, blk, D_MODEL), lambda i: (0, layer, 0, 0)),
            *tail[0],
        ],
        out_specs=[rows(), rows(), *tail[1]],
        out_shape=[jax.ShapeDtypeStruct((seq, D_MODEL), F32), jax.ShapeDtypeStruct((seq, D_MODEL), BF16), *tail[2]],
        compiler_params=_cparams("parallel" if head is None else "arbitrary"),
    )(x, ya, yb, o, proj, wo, *(head or ()))


def outproj_bwd(dx, y, wo, layer):
    seq = dx.shape[0]
    ts = min(seq, 512)
    blk = wo.shape[2]

    def body(dx_ref, y_ref, w_ref, dy_ref, dw_ref):
        @pl.when(pl.program_id(0) == 0)
        def _():
            dw_ref[...] = jnp.zeros_like(dw_ref)

        dxb = dx_ref[...].astype(BF16)
        w = jnp.concatenate([w_ref[d, 0] for d in range(N_DEV)], axis=0)
        dy_ref[...] = lax.dot_general(dxb, w, _NT, preferred_element_type=F32)
        dw = lax.dot_general(y_ref[...], dxb, _TN, preferred_element_type=F32)
        for d in range(N_DEV):
            dw_ref[d % 2, d // 2] += dw[d * blk:(d + 1) * blk]

    return pl.pallas_call(
        body,
        name="outproj_bwd",
        grid=(seq // ts,),
        in_specs=[
            pl.BlockSpec((ts, D_MODEL), lambda i: (i, 0)),
            pl.BlockSpec((ts, D_MODEL), lambda i: (i, 0)),
            pl.BlockSpec((N_DEV, 1, blk, D_MODEL), lambda i: (0, layer, 0, 0)),
        ],
        out_specs=[pl.BlockSpec((ts, D_MODEL), lambda i: (i, 0)),
                   pl.BlockSpec((2, N_CHIP, blk, D_MODEL), lambda i: (0, 0, 0, 0))],
        out_shape=[jax.ShapeDtypeStruct((seq, D_MODEL), F32), jax.ShapeDtypeStruct((2, N_CHIP, blk, D_MODEL), F32)],
        compiler_params=_cparams("arbitrary"),
    )(dx, y, wo)


C_QKV = (2048, 3584)


def _dproj_parts(dp_ref, dqkv_refs, rows):
    lo, hi = C_QKV
    step = (hi - lo) // len(dqkv_refs)
    return ([(0, dp_ref.at[rows, 0:lo])] + [(lo + i * step, r.at[rows, :]) for i, r in enumerate(dqkv_refs)]
            + [(hi, dp_ref.at[rows, hi:D_INT])])


def inproj_bwd_x(dproj, dqkv, w, x, g, dx_in, layer, carried=None):
    seq = x.shape[0]
    tm = min(seq, 512)

    def body(dp_ref, dq_ref, dk_ref, dv_ref, w_ref, x_ref, g_ref, dxin_ref, dx_ref, dg_ref):
        @pl.when(pl.program_id(0) == 0)
        def _():
            dg_ref[...] = jnp.zeros_like(dg_ref)

        dh = None
        for at, part in _dproj_parts(dp_ref, (dq_ref, dk_ref, dv_ref), slice(None)):
            term = lax.dot_general(part[...], w_ref[0, :, at:at + part.shape[1]], _NT, preferred_element_type=F32)
            dh = term if dh is None else dh + term
        r, xh = _rms_stats(x_ref[...])
        dg_ref[...] += jnp.sum(dh * xh, axis=0, keepdims=True)
        dx_ref[...] = dxin_ref[...] + _rms_bwd(dh, g_ref[...], r, xh)

    third = lambda: pl.BlockSpec((tm, C_WIDTH), lambda i: (i, 0))
    return _call_carrying(
        carried, body, (dproj, *dqkv, w, x, g, dx_in),
        name="inproj_bwd_x",
        grid=(seq // tm,),
        in_specs=[
            pl.BlockSpec((tm, D_INT), lambda i: (i, 0)), third(), third(), third(),
            pl.BlockSpec((1, D_MODEL, D_INT), lambda i: (layer, 0, 0)),
            pl.BlockSpec((tm, D_MODEL), lambda i: (i, 0)),
            _full((1, D_MODEL)),
            pl.BlockSpec((tm, D_MODEL), lambda i: (i, 0)),
        ],
        out_specs=[pl.BlockSpec((tm, D_MODEL), lambda i: (i, 0)), _full((1, D_MODEL))],
        out_shape=[jax.ShapeDtypeStruct((seq, D_MODEL), F32), jax.ShapeDtypeStruct((1, D_MODEL), F32)],
        scratch_shapes=[], semantics=("arbitrary",),
    )


def inproj_bwd_w(h, dproj, dqkv):
    seq = h.shape[0]
    ts, tn = min(seq, 512), 512

    def body(h_ref, dp_ref, dq_ref, dk_ref, dv_ref, dw_ref):
        @pl.when(pl.program_id(0) == 0)
        def _():
            dw_ref[...] = jnp.zeros_like(dw_ref)

        ht = h_ref[...].T
        for at, part in _dproj_parts(dp_ref, (dq_ref, dk_ref, dv_ref), slice(None)):
            for c in range(0, part.shape[1], tn):
                dw_ref[0, :, at + c:at + c + tn] += _dot(ht, part[:, c:c + tn])

    third = lambda: pl.BlockSpec((ts, C_WIDTH), lambda s: (s, 0))
    return pl.pallas_call(
        body,
        name="inproj_bwd_w",
        grid=(seq // ts,),
        in_specs=[pl.BlockSpec((ts, D_MODEL), lambda s: (s, 0)), pl.BlockSpec((ts, D_INT), lambda s: (s, 0)),
                  third(), third(), third()],
        out_specs=_full((1, D_MODEL, D_INT)),
        out_shape=jax.ShapeDtypeStruct((1, D_MODEL, D_INT), F32),
        compiler_params=_cparams("arbitrary"),
    )(h, dproj, *dqkv)


N_IN = 3848


def _internal_of(col):
    return col if col < 768 else (col + 256 if col < 3840 else 768 + col - 3840)


def _column_runs(n_shard):
    runs = []
    for d in range(N_IN // n_shard):
        mine = []
        for j in range(n_shard):
            ci = _internal_of(d * n_shard + j)
            if mine and mine[-1][0] + mine[-1][1] == ci:
                mine[-1][1] += 1
            else:
                mine.append([ci, 1, j])
        runs.append(mine)
    return runs


def assemble_w_in(wi_all):
    n_dev, depth, _, n_shard = wi_all.shape
    tr = 256
    pieces = [[] for _ in range(D_INT // LANES)]
    for d, mine in enumerate(_column_runs(n_shard)):
        for ci, ln, off in mine:
            while ln > 0:
                blk, at = divmod(ci, LANES)
                take = min(ln, LANES - at)
                pieces[blk].append((at, take, d, off))
                ci, ln, off = ci + take, ln - take, off + take

    def body(x_ref, o_ref):
        for blk, parts in enumerate(pieces):
            vals, at = [], 0
            for start, ln, d, off in sorted(parts):
                if start > at:
                    vals.append(jnp.zeros((tr, start - at), BF16))
                vals.append(x_ref[d, 0, :, off:off + ln])
                at = start + ln
            if at < LANES:
                vals.append(jnp.zeros((tr, LANES - at), BF16))
            o_ref[0, :, blk * LANES:(blk + 1) * LANES] = vals[0] if len(vals) == 1 else jnp.concatenate(vals, axis=1)

    return pl.pallas_call(
        body,
        name="assemble_w_in",
        grid=(depth, D_MODEL // tr),
        in_specs=[pl.BlockSpec((n_dev, 1, tr, n_shard), lambda l, r: (0, l, r, 0))],
        out_specs=pl.BlockSpec((1, tr, D_INT), lambda l, r: (l, r, 0)),
        out_shape=jax.ShapeDtypeStruct((depth, D_MODEL, D_INT), BF16),
        compiler_params=_cparams("parallel", "parallel"),
    )(wi_all)


def split_w_in_grad(dwi, n_shard):
    depth = dwi.shape[0]
    tr = 256
    runs = _column_runs(n_shard)

    def body(x_ref, o_ref):
        for d, mine in enumerate(runs):
            for ci, ln, off in mine:
                o_ref[d % 2, d // 2, 0, :, off:off + ln] = x_ref[0, :, ci:ci + ln]

    return pl.pallas_call(
        body,
        name="split_w_in_grad",
        grid=(depth, D_MODEL // tr),
        in_specs=[pl.BlockSpec((1, tr, D_INT), lambda l, r: (l, r, 0))],
        out_specs=pl.BlockSpec((2, N_CHIP, 1, tr, n_shard), lambda l, r: (0, 0, l, r, 0)),
        out_shape=jax.ShapeDtypeStruct((2, N_CHIP, depth, D_MODEL, n_shard), F32),
        compiler_params=_cparams("parallel", "parallel"),
    )(dwi)


C_WIDTH = 512
C_HEADS = 8
C_HDIM = 64
C_PAIRS = C_HEADS // 2
C_BQ = 512
C_TAIL = 16
C_KG = 4


def _split3(x):
    hi = x.astype(BF16)
    r = x - hi.astype(F32)
    mid = r.astype(BF16)
    return hi, mid, (r - mid.astype(F32)).astype(BF16)


def _piece_selectors():
    sel = np.zeros((C_HEADS, 3 * LANES, LANES), np.float32)
    for p in range(C_PAIRS):
        for e in range(2):
            for t in range(3):
                sel[2 * p + e, t * LANES + 2 * p + e, 3 * e + t] = -1.0
    return sel


def fox_prep(proj, bf_row):
    seq = proj.shape[0]
    nblk = seq // CHUNK
    tril = jnp.asarray(np.tril(np.ones((CHUNK, CHUNK), np.float32)), BF16)
    sel = jnp.asarray(_piece_selectors(), BF16)
    rows_t = CHUNK + C_TAIL

    def body(fl_ref, q_ref, k_ref, v_ref, bf_ref, l_ref, sel_ref, ka_ref, va_ref, vt_ref, kt_ref, qt_ref, qa_ref, carry_ref):
        @pl.when(pl.program_id(0) == 0)
        def _():
            carry_ref[...] = jnp.zeros_like(carry_ref)

        lf = jax.nn.log_sigmoid(fl_ref[:, :LANES] + bf_ref[...])
        c = _exact_times(l_ref[...], lf, 3) + carry_ref[...]
        carry_ref[...] += jnp.sum(lf, axis=0, keepdims=True)
        c3 = jnp.concatenate(_split3(c), axis=1)
        lane = lax.broadcasted_iota(jnp.int32, (CHUNK, LANES), 1)
        row = lax.broadcasted_iota(jnp.int32, (CHUNK, LANES), 0)
        r16 = lax.broadcasted_iota(jnp.int32, (C_TAIL, 2 * CHUNK), 0)
        l16 = lax.broadcasted_iota(jnp.int32, (C_TAIL, 2 * CHUNK), 1)
        zero = jnp.zeros((CHUNK, LANES), BF16)
        one = jnp.ones((CHUNK, LANES), BF16)

        def by_keys(x, right_a, right_b):
            xb = x.astype(BF16)
            top = jnp.concatenate([jnp.where(lane < C_HDIM, xb, zero), right_a], axis=1)
            return jnp.concatenate([top, jnp.concatenate([jnp.where(lane < C_HDIM, zero, xb), right_b], axis=1)], axis=0)

        def by_lanes(x, tail):
            xt = x.T.astype(BF16)
            main = jnp.concatenate([jnp.where(row < C_HDIM, xt, zero), jnp.where(row < C_HDIM, zero, xt)], axis=1)
            return jnp.concatenate([main, tail], axis=0)

        for p in range(C_PAIRS):
            cols = slice(p * LANES, (p + 1) * LANES)
            q2, k2, v2 = q_ref[:, cols] * (C_HDIM ** -0.5), k_ref[:, cols], v_ref[:, cols]
            negc = [_dot(c3, sel_ref[2 * p + e]).astype(BF16) for e in range(2)]
            ones3 = [jnp.where((lane >= 3 * e) & (lane < 3 * e + 3), one, zero) for e in range(2)]
            tail = jnp.where(((r16 == 2 * p) & (l16 < CHUNK)) | ((r16 == 2 * p + 1) & (l16 >= CHUNK)), 1.0, 0.0).astype(BF16)
            ka_ref[p] = by_keys(k2, negc[0], negc[1])
            va_ref[p] = by_keys(v2, ones3[0], ones3[1])
            kt_ref[p] = by_lanes(k2, tail)
            vt_ref[p] = by_lanes(v2, tail)
            qt_ref[p] = jnp.concatenate([q2.T.astype(BF16), jnp.where(row < 6, one, zero)], axis=0)
            qa_ref[p] = jnp.concatenate([q2.astype(BF16), jnp.where((lane == 2 * p) | (lane == 2 * p + 1), one, zero)], axis=1)

    wide = lambda j: pl.BlockSpec((CHUNK, C_WIDTH), lambda n: (n, j))
    by_rows = pl.BlockSpec((C_PAIRS, 2 * CHUNK, 2 * CHUNK), lambda n: (0, n, 0))
    by_cols = pl.BlockSpec((C_PAIRS, rows_t, 2 * CHUNK), lambda n: (0, 0, n))
    return pl.pallas_call(
        body,
        name="fox_prep",
        grid=(nblk,),
        in_specs=[pl.BlockSpec((CHUNK, 256), lambda n: (n, 3)), wide(4), wide(5), wide(6), _full((1, LANES)),
                  _full((CHUNK, CHUNK)), _full((C_HEADS, 3 * LANES, LANES))],
        out_specs=[by_rows, by_rows, by_cols, by_cols,
                   pl.BlockSpec((C_PAIRS, 2 * CHUNK, CHUNK), lambda n: (0, 0, n)),
                   pl.BlockSpec((C_PAIRS, CHUNK, 2 * CHUNK), lambda n: (0, n, 0))],
        out_shape=[jax.ShapeDtypeStruct((C_PAIRS, 2 * seq, 2 * CHUNK), BF16)] * 2
        + [jax.ShapeDtypeStruct((C_PAIRS, rows_t, 2 * seq), BF16)] * 2
        + [jax.ShapeDtypeStruct((C_PAIRS, 2 * CHUNK, seq), BF16), jax.ShapeDtypeStruct((C_PAIRS, seq, 2 * CHUNK), BF16)],
        scratch_shapes=[pltpu.VMEM((1, LANES), F32)],
        compiler_params=_cparams("arbitrary"),
    )(proj, proj, proj, proj, bf_row, tril, sel)


def _visible(shape, key0, query0):
    row = lax.broadcasted_iota(jnp.int32, shape, 0)
    key = key0 + lax.shift_left(lax.shift_right_logical(row, 8), 7) + (row & (CHUNK - 1))
    return key <= query0 + lax.broadcasted_iota(jnp.int32, shape, 1)


def _rows_ab(a, b, n):
    return jnp.concatenate([jnp.broadcast_to(a, (C_HDIM, n)), jnp.broadcast_to(b, (C_HDIM, n))], axis=0)


def _call_carrying(ex, body, operands, *, name, grid, in_specs, out_specs, out_shape, scratch_shapes, semantics=None):
    if ex is None:
        semantics = semantics or ("parallel", *["arbitrary"] * (len(grid) - 1))
        return pl.pallas_call(body, name=name, grid=grid, in_specs=in_specs, out_specs=out_specs, out_shape=out_shape,
                              scratch_shapes=scratch_shapes, compiler_params=_cparams(*semantics))(*operands)
    n_in, n_out = len(in_specs), len(out_specs)

    def wrapped(*refs):
        own, parts = _carried_refs(refs, n_in, n_out, ex)
        ids = [pl.program_id(a) for a in range(len(grid))]
        pl.when(functools.reduce(jnp.logical_and, [i == 0 for i in ids]))(lambda: ex.start(*parts))
        if hasattr(ex, "relay"):
            linear = functools.reduce(lambda at, ig: at * ig[1] + ig[0], zip(ids, grid), 0)
            pl.when(linear == int(np.prod(grid)) // 2)(lambda: ex.relay(*parts))
        body(*own)
        pl.when(functools.reduce(jnp.logical_and, [i == g - 1 for i, g in zip(ids, grid)]))(lambda: ex.finish(*parts))

    return pl.pallas_call(
        wrapped, name=name, grid=grid,
        in_specs=list(in_specs) + [_ANY] * len(ex.inputs), out_specs=list(out_specs) + [_ANY] * len(ex.out_shape),
        out_shape=list(out_shape) + list(ex.out_shape), scratch_shapes=list(scratch_shapes) + list(ex.scratch),
        input_output_aliases={n_in + i: n_out + o for i, o in getattr(ex, "aliases", {}).items()},
        compiler_params=_cparams(*["arbitrary"] * len(grid)),
    )(*operands, *ex.inputs)


def fox_fwd(qt, ka, vt, carried=None):
    seq = qt.shape[2]
    nblk = seq // CHUNK
    bq = min(C_BQ, seq)
    grp = bq // CHUNK
    rows_t = CHUNK + C_TAIL

    def body(qt_ref, ka_ref, vt_ref, o_ref, lse_ref, acc_ref, s_ref):
        p, i = pl.program_id(0), pl.program_id(1)
        qtile = qt_ref[0]
        r16 = lax.broadcasted_iota(jnp.int32, (C_TAIL, bq), 0)

        def scores(t):
            at = pl.multiple_of(t * grp * 2 * CHUNK, 2 * CHUNK)
            return _dot(ka_ref[0, pl.ds(at, grp * 2 * CHUNK), :], qtile)

        def group(t, m, masked):
            ma, mb = m
            at = pl.multiple_of(t * grp * 2 * CHUNK, 2 * CHUNK)
            s = s_ref[...]
            if masked:
                s = jnp.where(_visible(s.shape, t * bq, i * bq), s, -jnp.inf)
            sa = [s[g * 2 * CHUNK:g * 2 * CHUNK + CHUNK] for g in range(grp)]
            sb = [s[g * 2 * CHUNK + CHUNK:(g + 1) * 2 * CHUNK] for g in range(grp)]
            na, nb = ma, mb
            for g in range(grp):
                na = jnp.maximum(na, jnp.max(sa[g], axis=0, keepdims=True))
                nb = jnp.maximum(nb, jnp.max(sb[g], axis=0, keepdims=True))
            al_a, al_b = jnp.exp(ma - na), jnp.exp(mb - nb)
            pt = jnp.concatenate([jnp.exp(x - n) for g in range(grp) for x, n in ((sa[g], na), (sb[g], nb))], axis=0)
            pv = _dot(vt_ref[0, :, pl.ds(at, grp * 2 * CHUNK)], pt.astype(BF16))
            tail = jnp.where(r16 == 2 * p, al_a, jnp.where(r16 == 2 * p + 1, al_b, 1.0))
            acc_ref[...] = acc_ref[...] * jnp.concatenate([_rows_ab(al_a, al_b, bq), tail], axis=0) + pv
            return na, nb

        def step(t, m):
            s_next = scores(t + 1)
            m = group(t, m, False)
            s_ref[...] = s_next
            return m

        acc_ref[...] = jnp.zeros_like(acc_ref)
        s_ref[...] = scores(0)
        m = (jnp.full((1, bq), -jnp.inf, F32), jnp.full((1, bq), -jnp.inf, F32))
        m = lax.fori_loop(0, i, step, m)
        ma, mb = group(i, m, True)
        tailv = acc_ref[CHUNK:rows_t, :]
        la = jnp.sum(jnp.where(r16 == 2 * p, tailv, 0.0), axis=0, keepdims=True)
        lb = jnp.sum(jnp.where(r16 == 2 * p + 1, tailv, 0.0), axis=0, keepdims=True)
        o_ref[...] = (acc_ref[0:CHUNK, :] * _rows_ab(1.0 / la, 1.0 / lb, bq)).T
        lse_ref[0, 0:1, :] = ma + jnp.log(la)
        lse_ref[0, 1:2, :] = mb + jnp.log(lb)

    return _call_carrying(
        carried, body, (qt, ka, vt),
        name="fox_fwd",
        grid=(C_PAIRS, seq // bq),
        in_specs=[
            pl.BlockSpec((1, 2 * CHUNK, bq), lambda p, i: (p, 0, i)),
            pl.BlockSpec((1, 2 * seq, 2 * CHUNK), lambda p, i: (p, 0, 0)),
            pl.BlockSpec((1, rows_t, 2 * seq), lambda p, i: (p, 0, 0)),
        ],
        out_specs=[pl.BlockSpec((bq, LANES), lambda p, i: (i, p)), pl.BlockSpec((1, 2, bq), lambda p, i: (p, 0, i))],
        out_shape=[jax.ShapeDtypeStruct((seq, C_WIDTH), F32), jax.ShapeDtypeStruct((C_PAIRS, 2, seq), F32)],
        scratch_shapes=[pltpu.VMEM((rows_t, bq), F32), pltpu.VMEM((grp * 2 * CHUNK, bq), F32)],
    )


def fox_bwd_prep(dy, o, proj, dproj):
    seq = o.shape[0]
    ind = np.zeros((C_WIDTH, LANES), np.float32)
    for h in range(C_HEADS):
        ind[h * C_HDIM:(h + 1) * C_HDIM, h] = 1.0
    ind = jnp.asarray(ind, BF16)
    sel = _piece_selectors()
    sel = jnp.asarray(np.stack([sel[2 * p].T + sel[2 * p + 1].T for p in range(C_PAIRS)]), BF16)

    def body(dy_ref, o_ref, z_ref, ind_ref, sel_ref, _, do_ref, dz_ref, dot_ref):
        dy_c, o_v, z = dy_ref[...], o_ref[...], z_ref[...]
        sg = jax.nn.sigmoid(z)
        do = dy_c * (z * sg)
        do_ref[...] = do.astype(BF16)
        dz_ref[...] = (dy_c * o_v * (sg * (1.0 + z * (1.0 - sg)))).astype(BF16)
        prod = do * o_v
        hi = prod.astype(BF16)
        lo = (prod - hi.astype(F32)).astype(BF16)
        delta = _dot(hi, ind_ref[...]) + _dot(lo, ind_ref[...])
        d3 = jnp.concatenate(_split3(delta.T), axis=0)
        for p in range(C_PAIRS):
            tail = _dot(sel_ref[p], d3).astype(BF16)
            dot_ref[p] = jnp.concatenate([do[:, p * LANES:(p + 1) * LANES].T.astype(BF16), tail], axis=0)

    return pl.pallas_call(
        body,
        name="fox_bwd_prep",
        grid=(seq // CHUNK,),
        in_specs=[
            pl.BlockSpec((CHUNK, C_WIDTH), lambda i: (i, 1)),
            pl.BlockSpec((CHUNK, C_WIDTH), lambda i: (i, 0)),
            pl.BlockSpec((CHUNK, C_WIDTH), lambda i: (i, 7)),
            _full((C_WIDTH, LANES)), _full((C_PAIRS, LANES, 3 * LANES)), _ANY,
        ],
        out_specs=[
            pl.BlockSpec((CHUNK, C_WIDTH), lambda i: (i, 0)),
            pl.BlockSpec((CHUNK, C_WIDTH), lambda i: (i, 7)),
            pl.BlockSpec((C_PAIRS, 2 * CHUNK, CHUNK), lambda i: (0, 0, i)),
        ],
        out_shape=[jax.ShapeDtypeStruct((seq, C_WIDTH), BF16), jax.ShapeDtypeStruct(dproj.shape, BF16),
                   jax.ShapeDtypeStruct((C_PAIRS, 2 * CHUNK, seq), BF16)],
        input_output_aliases={5: 1},
        compiler_params=_cparams("parallel"),
    )(dy, o, proj, ind, sel, dproj)


def fox_bwd(ka, va, kt, qt, dot_t, qa, dob, lse, carried=None):
    seq = qt.shape[2]
    nblk = seq // CHUNK
    bq = min(C_BQ, seq)
    nq = seq // bq
    kg = min(C_KG, nblk)
    ng = nblk // kg
    rows_t = CHUNK + C_TAIL

    def body(ka_ref, va_ref, kt_ref, qt_ref, dot_ref, qa_ref, do_ref, lse_ref,
             dq_ref, dk_ref, dv_ref, dck_ref, dcq_ref, dqt_acc, dv_acc, dka_acc):
        p, jg = pl.program_id(0), pl.program_id(1)

        @pl.when(jg == 0)
        def _():
            dqt_acc[...] = jnp.zeros_like(dqt_acc)

        dv_acc[...] = jnp.zeros_like(dv_acc)
        dka_acc[...] = jnp.zeros_like(dka_acc)

        def step(i, carry, masked):
            cols = pl.ds(pl.multiple_of(i * bq, bq), bq)
            qtile, dotile = qt_ref[0, :, cols], dot_ref[0, :, cols]
            do, qa_i = do_ref[cols, :], qa_ref[0, cols, :]
            lse2 = jnp.concatenate([jnp.broadcast_to(lse_ref[0, 0:1, cols], (CHUNK, bq)),
                                    jnp.broadcast_to(lse_ref[0, 1:2, cols], (CHUNK, bq))] * kg, axis=0)
            pt = jnp.exp(_dot(ka_ref[0], qtile) - lse2)
            if masked:
                pt = jnp.where(_visible(pt.shape, jg * kg * CHUNK, i * bq), pt, 0.0)
            ds = pt * _dot(va_ref[0], dotile)
            ptb, dsb = pt.astype(BF16), ds.astype(BF16)
            dv_acc[...] += _dot(ptb, do)
            dka_acc[...] += _dot(dsb, qa_i)
            dqt_acc[:, cols] += _dot(kt_ref[0], dsb)
            return carry

        i0 = (jg * kg * CHUNK) // bq
        step(i0, 0, True)
        lax.fori_loop(i0 + 1, nq, functools.partial(step, masked=False), 0)
        lane = lax.broadcasted_iota(jnp.int32, (CHUNK, LANES), 1)
        for kb in range(kg):
            rows = slice(kb * CHUNK, (kb + 1) * CHUNK)
            ra = slice(kb * 2 * CHUNK, kb * 2 * CHUNK + CHUNK)
            rb = slice(kb * 2 * CHUNK + CHUNK, (kb + 1) * 2 * CHUNK)
            dk_ref[rows, :] = jnp.where(lane < C_HDIM, dka_acc[ra, 0:LANES], dka_acc[rb, 0:LANES]).astype(BF16)
            dv_ref[rows, :] = jnp.where(lane < C_HDIM, dv_acc[ra, :], dv_acc[rb, :]).astype(BF16)
            dck_ref[0, rows, :] = (jnp.where(lane == 2 * p, dka_acc[ra, LANES:], 0.0)
                                   + jnp.where(lane == 2 * p + 1, dka_acc[rb, LANES:], 0.0))

        @pl.when(jg == ng - 1)
        def _():
            for c in range(nq):
                dq_ref[c * bq:(c + 1) * bq, :] = (dqt_acc[0:CHUNK, c * bq:(c + 1) * bq].T * (C_HDIM ** -0.5)).astype(BF16)
            dcq_ref[0] = dqt_acc[CHUNK:rows_t, :]

    per_pair = lambda r, c: pl.BlockSpec((1, r, c), lambda p, j: (p, 0, 0))
    by_rows = pl.BlockSpec((1, kg * 2 * CHUNK, 2 * CHUNK), lambda p, j: (p, j, 0))
    by_cols = pl.BlockSpec((1, rows_t, kg * 2 * CHUNK), lambda p, j: (p, 0, j))
    return _call_carrying(
        carried, body, (ka, va, kt, qt, dot_t, qa, dob, lse),
        name="fox_bwd",
        grid=(C_PAIRS, ng),
        in_specs=[by_rows, by_rows, by_cols, per_pair(2 * CHUNK, seq), per_pair(2 * CHUNK, seq),
                  per_pair(seq, 2 * CHUNK), pl.BlockSpec((seq, LANES), lambda p, j: (0, p)), per_pair(2, seq)],
        out_specs=[pl.BlockSpec((seq, LANES), lambda p, j: (0, p)),
                   pl.BlockSpec((kg * CHUNK, LANES), lambda p, j: (j, p)),
                   pl.BlockSpec((kg * CHUNK, LANES), lambda p, j: (j, p)),
                   pl.BlockSpec((1, kg * CHUNK, LANES), lambda p, j: (p, j, 0)),
                   per_pair(C_TAIL, seq)],
        out_shape=[jax.ShapeDtypeStruct((seq, C_WIDTH), BF16)] * 3
        + [jax.ShapeDtypeStruct((C_PAIRS, seq, LANES), F32), jax.ShapeDtypeStruct((C_PAIRS, C_TAIL, seq), F32)],
        scratch_shapes=[pltpu.VMEM((rows_t, seq), F32), pltpu.VMEM((kg * 2 * CHUNK, LANES), F32),
                        pltpu.VMEM((kg * 2 * CHUNK, 2 * CHUNK), F32)],
    )


def fox_post(dcq, dck, proj, bf_row, dproj):
    seq = proj.shape[0]
    nc = seq // CHUNK
    triu = jnp.asarray(np.triu(np.ones((CHUNK, CHUNK), np.float32)), BF16)

    def body(dq_ref, dk_ref, fl_ref, bf_ref, u_ref, _, dfl_ref, dbf_ref, carry_ref):
        @pl.when(pl.program_id(0) == 0)
        def _():
            carry_ref[...] = jnp.zeros_like(carry_ref)
            dbf_ref[...] = jnp.zeros_like(dbf_ref)

        rows = (dq_ref[0] + dq_ref[1]) + (dq_ref[2] + dq_ref[3])
        dc = jnp.concatenate([rows, jnp.zeros((CHUNK - C_TAIL, CHUNK), F32)], axis=0).T
        dc = dc - ((dk_ref[0] + dk_ref[1]) + (dk_ref[2] + dk_ref[3]))
        g = _exact_times(u_ref[...], dc, 3) + carry_ref[...]
        carry_ref[...] += jnp.sum(dc, axis=0, keepdims=True)
        dfl = g * jax.nn.sigmoid(-(fl_ref[:, :LANES] + bf_ref[...]))
        dbf_ref[...] += jnp.sum(dfl, axis=0, keepdims=True)
        dfl_ref[...] = jnp.concatenate([dfl, jnp.zeros_like(dfl)], axis=1).astype(BF16)

    rev = lambda n: nc - 1 - n
    return pl.pallas_call(
        body,
        name="fox_post",
        grid=(nc,),
        in_specs=[
            pl.BlockSpec((C_PAIRS, C_TAIL, CHUNK), lambda n: (0, 0, rev(n))),
            pl.BlockSpec((C_PAIRS, CHUNK, LANES), lambda n: (0, rev(n), 0)),
            pl.BlockSpec((CHUNK, 256), lambda n: (rev(n), 3)),
            _full((1, LANES)), _full((CHUNK, CHUNK)), _ANY,
        ],
        out_specs=[pl.BlockSpec((CHUNK, 256), lambda n: (rev(n), 3)), _full((1, LANES))],
        out_shape=[jax.ShapeDtypeStruct(dproj.shape, BF16), jax.ShapeDtypeStruct((1, LANES), F32)],
        input_output_aliases={5: 0},
        scratch_shapes=[pltpu.VMEM((1, LANES), F32)],
        compiler_params=_cparams("arbitrary"),
    )(dcq, dck, proj, bf_row, triu, dproj)


N_DEV = 8
MESH = pl.DeviceIdType.MESH
_ANY = pl.BlockSpec(memory_space=pl.ANY)


def _mesh_pos():
    return lax.axis_index("x"), lax.axis_index("y"), lax.axis_index("c")


def _dev_index(px, py, pc):
    return 4 * px + 2 * py + pc


def _row_pieces(ref, rows):
    return [ref.at[idx + (pl.ds(r, rows),)] for idx in np.ndindex(*ref.shape[:-2]) for r in range(0, ref.shape[-2], rows)]


class _Transfer:
    def __init__(self, src, dst, rows, send_sem, recv_sem, to):
        self.src, self.dst, self.rows, self.sems, self.to = src, dst, rows, (send_sem, recv_sem), to

    def _copy(self, src, dst):
        return pltpu.make_async_remote_copy(src_ref=src, dst_ref=dst, send_sem=self.sems[0], recv_sem=self.sems[1],
                                            device_id=self.to, device_id_type=MESH)

    def start(self):
        for s, d in zip(_row_pieces(self.src, self.rows), _row_pieces(self.dst, self.rows), strict=True):
            self._copy(s, d).start()

    def wait_send(self):
        self._copy(self.src, self.dst).wait_send()

    def wait_recv(self):
        self._copy(self.src, self.dst).wait_recv()


def _exchange_call(ex, name):
    n_in, n_out = len(ex.inputs), len(ex.out_shape)

    def body(*refs):
        parts = refs[:n_in], refs[n_in:n_in + n_out], refs[n_in + n_out:]
        ex.start(*parts)
        getattr(ex, "relay", lambda *_: None)(*parts)
        ex.finish(*parts)

    return pl.pallas_call(body, name=name, in_specs=[_ANY] * n_in, out_specs=[_ANY] * n_out, out_shape=ex.out_shape,
                          scratch_shapes=ex.scratch, input_output_aliases=getattr(ex, "aliases", {}))(*ex.inputs)


def _carried_refs(refs, n_in, n_out, ex):
    k_in, k_out, k_sem = (len(ex.inputs), len(ex.out_shape), len(ex.scratch)) if ex else (0, 0, 0)
    a, b, c = n_in + k_in, n_in + k_in + n_out, n_in + k_in + n_out + k_out
    own = refs[:n_in] + refs[a:b] + refs[c:len(refs) - k_sem]
    return own, (refs[n_in:a], refs[b:c], refs[len(refs) - k_sem:])


class AllGatherWeights:
    def __init__(self, blocks):
        n = len(blocks)
        self.inputs = tuple(blocks)
        self.out_shape = [jax.ShapeDtypeStruct((N_DEV,) + b.shape, b.dtype) for b in blocks]
        self.scratch = ([pltpu.SemaphoreType.DMA((n, 7)), pltpu.SemaphoreType.DMA((n, 7)), pltpu.SemaphoreType.DMA((n, 2))]
                        + [pltpu.VMEM(b.shape, b.dtype) for b in blocks])

    def _plan(self, ins, outs, scratch):
        send_sems, recv_sems, local_sems, *staged = scratch
        x, y, c = _mesh_pos()
        me, sibling = (x, y, c), (x, y, 1 - c)
        chips = [(1 - x, y), (x, 1 - y), (1 - x, 1 - y)]
        every = range(len(ins))

        def copy(a, k, block, to, own=False):
            slot = outs[a].at[_dev_index(*block)]
            return _Transfer(ins[a] if own else slot, slot, ins[a].shape[-2], send_sems.at[a, k], recv_sems.at[a, k], to)

        mine = [(pltpu.make_async_copy(ins[a], staged[a], local_sems.at[a, 0]),
                 pltpu.make_async_copy(staged[a], outs[a].at[_dev_index(*me)], local_sems.at[a, 1])) for a in every]
        first = [copy(a, 1 + j, me, (*chip, c), own=True) for j, chip in enumerate(chips) for a in every]
        first += [copy(a, 0, me, sibling, own=True) for a in every]
        passed = [[copy(a, 4 + j, (*chip, c), sibling) for a in every] for j, chip in enumerate(chips)]
        return me, sibling, chips, c, every, copy, mine, first, passed

    def start(self, ins, outs, scratch):
        *_, mine, first, _ = self._plan(ins, outs, scratch)
        for to_vmem, _ in mine:
            to_vmem.start()
        for cp in first:
            cp.start()

    def relay(self, ins, outs, scratch):
        me, sibling, chips, c, every, copy, mine, first, passed = self._plan(ins, outs, scratch)
        for to_vmem, to_slot in mine:
            to_vmem.wait()
            to_slot.start()
        for j, chip in enumerate(chips):
            for a in every:
                copy(a, 1 + j, (*chip, c), me).wait_recv()
            for cp in passed[j]:
                cp.start()

    def finish(self, ins, outs, scratch):
        me, sibling, chips, c, every, copy, mine, first, passed = self._plan(ins, outs, scratch)
        for a in every:
            copy(a, 0, sibling, me).wait_recv()
        for j, chip in enumerate(chips):
            for a in every:
                copy(a, 4 + j, (*chip, 1 - c), me).wait_recv()
        for cp in first + [cp for group in passed for cp in group]:
            cp.wait_send()
        for _, to_slot in mine:
            to_slot.wait()


N_CHIP = 4


class PairExchange:
    def __init__(self, by_core, whole=()):
        self.inputs = tuple(by_core) + tuple(whole)
        self.n_by_core = len(by_core)
        self.out_shape = ([jax.ShapeDtypeStruct(a.shape[1:], a.dtype) for a in by_core]
                          + [jax.ShapeDtypeStruct(a.shape, a.dtype) for a in whole])
        n = len(self.inputs)
        self.scratch = [pltpu.SemaphoreType.DMA((n,)), pltpu.SemaphoreType.DMA((n,))]

    def _copies(self, ins, outs, sems):
        x, y, c = _mesh_pos()
        srcs = [r.at[1 - c] if a < self.n_by_core else r for a, r in enumerate(ins)]
        return [_Transfer(srcs[a], outs[a], outs[a].shape[-2], sems[0].at[a], sems[1].at[a], (x, y, 1 - c))
                for a in range(len(ins))]

    def start(self, ins, outs, sems):
        for cp in self._copies(ins, outs, sems):
            cp.start()

    def finish(self, ins, outs, sems):
        copies = self._copies(ins, outs, sems)
        for cp in copies:
            cp.wait_recv()
        for cp in copies:
            cp.wait_send()


def pair_sum(own, other, dtype, rows, name, core, layer, depth, stacked=None):
    n, n_r, n_c = other.shape

    def body(core_ref, a_ref, b_ref, *refs):
        refs[-1][0, 0] = (a_ref[0, 0] + b_ref[0]).astype(dtype)

    carried = () if stacked is None else (stacked,)
    grid_spec = pltpu.PrefetchScalarGridSpec(
        num_scalar_prefetch=1,
        grid=(n, n_r // rows),
        in_specs=[pl.BlockSpec((1, 1, rows, n_c), lambda i, r, s: (s[0], i, r, 0)),
                  pl.BlockSpec((1, rows, n_c), lambda i, r, s: (i, r, 0))] + [_ANY] * len(carried),
        out_specs=pl.BlockSpec((1, 1, rows, n_c), lambda i, r, s: (i, layer, r, 0)),
    )
    return pl.pallas_call(
        body,
        name=name,
        grid_spec=grid_spec,
        out_shape=jax.ShapeDtypeStruct((n, depth, n_r, n_c), dtype),
        input_output_aliases={3: 0} if carried else {},
        compiler_params=_cparams("parallel", "parallel"),
    )(core, own, other, *carried)


def small_sum(a, b, name):
    def body(a_ref, b_ref, o_ref):
        o_ref[...] = a_ref[...] + b_ref[...]

    return pl.pallas_call(body, name=name, out_shape=jax.ShapeDtypeStruct(a.shape, a.dtype))(a, b)


class ChipExchange:
    def __init__(self, by_chip=(), layers=(), gathered=(), stacked=()):
        stacked = tuple(stacked) or (None,) * len(by_chip)
        kept = [s for s in stacked if s is not None]
        self.inputs = tuple(by_chip) + tuple(gathered) + tuple(kept)
        self.n_by_chip, self.n_gathered = len(by_chip), len(gathered)
        self.items = [(a, l) for a in range(len(by_chip)) for l in layers[a]] + [(self.n_by_chip + g, None) for g in range(len(gathered))]
        self.out_shape = ([jax.ShapeDtypeStruct((N_CHIP - 1,) + a.shape[1:], a.dtype) for a in by_chip]
                          + [jax.ShapeDtypeStruct((N_CHIP,) + a.shape, a.dtype) for a in gathered])
        at = iter(range(self.n_by_chip + self.n_gathered, len(self.inputs)))
        self.aliases = {next(at): a for a, s in enumerate(stacked) if s is not None}
        n = len(self.items)
        self.scratch = [pltpu.SemaphoreType.DMA((n, 3)), pltpu.SemaphoreType.DMA((n, 3)),
                        pltpu.SemaphoreType.DMA((max(self.n_gathered, 1),))]

    def _plan(self, ins, outs, sems):
        x, y, c = _mesh_pos()
        chip = 2 * x + y
        n = len(self.items)

        def copy(i, k, sending):
            a, layer = self.items[i]
            px, py = x ^ ((k >> 1) & 1), y ^ (k & 1)
            if layer is not None:
                src, dst = ins[a].at[2 * px + py, layer], outs[a].at[k - 1, layer]
            else:
                src, dst = ins[a], outs[a].at[chip if sending else 2 * px + py]
            return _Transfer(src, dst, dst.shape[-2], sems[0].at[i, k - 1], sems[1].at[i, k - 1], (px, py, c))

        local = [pltpu.make_async_copy(ins[a], outs[a].at[chip], sems[2].at[a - self.n_by_chip])
                 for a in range(self.n_by_chip, self.n_by_chip + self.n_gathered)]
        return n, copy, local

    def start(self, ins, outs, sems):
        n, copy, local = self._plan(ins, outs, sems)
        for cp in local:
            cp.start()
        for k in range(1, N_CHIP):
            for a in range(n):
                copy(a, k, True).start()

    def finish(self, ins, outs, sems):
        n, copy, local = self._plan(ins, outs, sems)
        for k in range(1, N_CHIP):
            for a in range(n):
                copy(a, k, False).wait_recv()
        for k in range(1, N_CHIP):
            for a in range(n):
                copy(a, k, True).wait_send()
        for cp in local:
            cp.wait()


ADAM_LR = 0.001
ADAM_B1 = 0.9
ADAM_B2 = 0.999
ADAM_EPS = 1e-08
ADAM_WD = 0.01
ADAM_STEP = 10


def adam_reduce(parts, w, m, v, rows, name, own=None, chip=None):
    n_l, n_r, n_c = w.shape
    n_parts = parts.shape[0]

    def body(*refs):
        p_ref, w_ref, m_ref, v_ref, g_ref, d_ref, m2_ref, v2_ref = refs[-8:]
        g = p_ref[0, 0].astype(F32)
        if own is not None:
            g = refs[-9][...].reshape(rows, n_c).astype(F32) + g
        for d in range(1, n_parts):
            g = g + p_ref[d, 0].astype(F32)
        m2 = ADAM_B1 * m_ref[0] + (1.0 - ADAM_B1) * g
        v2 = ADAM_B2 * v_ref[0] + (1.0 - ADAM_B2) * (g * g)
        m_hat = m2 / (1.0 - ADAM_B1 ** ADAM_STEP)
        v_hat = v2 / (1.0 - ADAM_B2 ** ADAM_STEP)
        g_ref[0] = g
        d_ref[0] = -ADAM_LR * (m_hat / (jnp.sqrt(v_hat) + ADAM_EPS) + ADAM_WD * w_ref[0])
        m2_ref[0] = m2
        v2_ref[0] = v2

    blk = lambda: pl.BlockSpec((1, rows, n_c), lambda l, r, *_: (l, r, 0))
    in_specs = [pl.BlockSpec((n_parts, 1, rows, n_c), lambda l, r, *_: (0, l, r, 0)), blk(), blk(), blk()]
    args = (parts, w, m, v)
    if own is not None:
        in_specs = [pl.BlockSpec((1, 1, rows, n_c), lambda l, r, s: (s[0], l, r, 0))] + in_specs
        args = (chip, own) + args
    grid_spec = pltpu.PrefetchScalarGridSpec(
        num_scalar_prefetch=0 if own is None else 1, grid=(n_l, n_r // rows), in_specs=in_specs,
        out_specs=[blk(), blk(), blk(), blk()])
    return pl.pallas_call(
        body,
        name=name,
        grid_spec=grid_spec,
        out_shape=[jax.ShapeDtypeStruct(w.shape, F32)] * 4,
        compiler_params=_cparams("parallel", "parallel"),
    )(*args)


def adam_reduce_columns(parts, w, m, v, name, own, chip):
    n_l, n_r, n_c = w.shape
    n_parts = parts.shape[0]
    view = lambda a: jnp.transpose(a, (2, 0, 1))

    def body(_, own_ref, p_ref, w_ref, m_ref, v_ref, g_ref, d_ref, m2_ref, v2_ref):
        for l in range(n_l):
            g = own_ref[0, l].astype(F32) + p_ref[0, l].astype(F32)
            for d in range(1, n_parts):
                g = g + p_ref[d, l].astype(F32)
            g = g.T
            w_l, m_l, v_l = w_ref[:, l, :], m_ref[:, l, :], v_ref[:, l, :]
            m2 = ADAM_B1 * m_l + (1.0 - ADAM_B1) * g
            v2 = ADAM_B2 * v_l + (1.0 - ADAM_B2) * (g * g)
            m_hat = m2 / (1.0 - ADAM_B1 ** ADAM_STEP)
            v_hat = v2 / (1.0 - ADAM_B2 ** ADAM_STEP)
            g_ref[:, l, :] = g
            d_ref[:, l, :] = -ADAM_LR * (m_hat / (jnp.sqrt(v_hat) + ADAM_EPS) + ADAM_WD * w_l)
            m2_ref[:, l, :] = m2
            v2_ref[:, l, :] = v2

    blk = lambda: pl.BlockSpec((LANES, n_l, n_r), lambda c, s: (c, 0, 0))
    grid_spec = pltpu.PrefetchScalarGridSpec(
        num_scalar_prefetch=1, grid=(pl.cdiv(n_c, LANES),),
        in_specs=[pl.BlockSpec((1, n_l, n_r, LANES), lambda c, s: (s[0], 0, 0, c)),
                  pl.BlockSpec((n_parts, n_l, n_r, LANES), lambda c, s: (0, 0, 0, c)), blk(), blk(), blk()],
        out_specs=[blk(), blk(), blk(), blk()])
    outs = pl.pallas_call(
        body,
        name=name,
        grid_spec=grid_spec,
        out_shape=[jax.ShapeDtypeStruct((n_c, n_l, n_r), F32)] * 4,
        compiler_params=_cparams("parallel"),
    )(chip, own, parts, view(w), view(m), view(v))
    return [jnp.transpose(o, (1, 2, 0)) for o in outs]


_SMALL = (("norm_g", (2, 1024)), ("gmlp_ln_g", (2, 4, 64)), ("gmlp_ln_b", (2, 4, 64)),
          ("gmlp_b_s", (2, 4, 128)), ("hgrn_lb", (2, 256)), ("hgrn_onorm_g", (2, 64)), ("fox_b_f", (2, 8)),
          ("final_norm_g", (1024,)), ("loss", ()))


def _padded(n):
    return -(-n // LANES) * LANES


_SMALL_ROWS = -(-sum(_padded(int(np.prod(s))) for _, s in _SMALL) // LANES // 8) * 8


def _pack_small(vals):
    flat = []
    for (name, shape), a in zip(_SMALL, vals, strict=True):
        n = int(np.prod(shape))
        flat.append(jnp.pad(a.reshape(n).astype(F32), (0, _padded(n) - n)))
    flat = jnp.concatenate(flat)
    return jnp.pad(flat, (0, _SMALL_ROWS * LANES - flat.shape[0])).reshape(_SMALL_ROWS, LANES)


def _unpack_small(slab):
    flat, out, at = slab.reshape(-1), {}, 0
    for name, shape in _SMALL:
        n = int(np.prod(shape))
        out[name] = flat[at:at + n].reshape(shape)
        at += _padded(n)
    return out


def sum_parts(parts, name):
    def body(p_ref, o_ref):
        g = p_ref[0]
        for d in range(1, parts.shape[0]):
            g = g + p_ref[d]
        o_ref[...] = g

    return pl.pallas_call(body, name=name, out_shape=jax.ShapeDtypeStruct(parts.shape[1:], F32))(parts)


def adam_small(gs, ws, ms, vs):
    n = len(gs)

    def body(*refs):
        for k in range(n):
            g, w, m, v = (refs[j * n + k][...] for j in range(4))
            m2 = ADAM_B1 * m + (1.0 - ADAM_B1) * g
            v2 = ADAM_B2 * v + (1.0 - ADAM_B2) * (g * g)
            m_hat = m2 / (1.0 - ADAM_B1 ** ADAM_STEP)
            v_hat = v2 / (1.0 - ADAM_B2 ** ADAM_STEP)
            refs[4 * n + k][...] = -ADAM_LR * (m_hat / (jnp.sqrt(v_hat) + ADAM_EPS) + ADAM_WD * w)
            refs[5 * n + k][...] = m2
            refs[6 * n + k][...] = v2

    outs = pl.pallas_call(body, name="adam_small",
                          out_shape=[jax.ShapeDtypeStruct(w.shape, F32) for _ in range(3) for w in ws])(*gs, *ws, *ms, *vs)
    return outs[:n], outs[n:2 * n], outs[2 * n:]


def kernel(x, norm_g, w_in, w_out, gmlp_ln_g, gmlp_ln_b, gmlp_w_s, gmlp_b_s, hgrn_lb, hgrn_onorm_g, fox_b_f, final_norm_g, loss_target, m_norm_g, m_w_in, m_w_out, m_gmlp_ln_g, m_gmlp_ln_b, m_gmlp_w_s, m_gmlp_b_s, m_hgrn_lb, m_hgrn_onorm_g, m_fox_b_f, m_final_norm_g, v_norm_g, v_w_in, v_w_out, v_gmlp_ln_g, v_gmlp_ln_b, v_gmlp_w_s, v_gmlp_b_s, v_hgrn_lb, v_hgrn_onorm_g, v_fox_b_f, v_final_norm_g):
    depth = w_in.shape[0]
    seq = x.shape[1]
    assert w_in.shape[2] * N_DEV == N_IN
    xs, tgt = x[0], loss_target[0]

    wi_blk, wo_blk = w_in.astype(BF16), w_out.astype(BF16)
    (wi_all,) = _exchange_call(AllGatherWeights([wi_blk[0]]), "allgather_weights_0")

    ln_g = gmlp_ln_g.reshape(depth, 1, A_WIDTH)
    ln_b = gmlp_ln_b.reshape(depth, 1, A_WIDTH)
    bs_t = jnp.pad(jnp.transpose(gmlp_b_s, (0, 2, 1)), ((0, 0), (0, 0), (0, LANES - A_GROUPS)))
    lb0, lb1 = hgrn_lb[0:1], hgrn_lb[1:2]
    onorm = jnp.tile(hgrn_onorm_g, (1, B_HEADS)).reshape(depth, 1, B_WIDTH)
    bf_row = jnp.pad(fox_b_f, ((0, 0), (0, LANES - C_HEADS))).reshape(depth, 1, LANES)

    core = lax.axis_index("c").astype(jnp.int32).reshape(1)
    chip = (2 * lax.axis_index("x") + lax.axis_index("y")).astype(jnp.int32).reshape(1)

    saved = []
    xc = xs
    for l in range(depth):
        wi_int = assemble_w_in(wi_all[:, None])
        proj, h = inproj(xc, norm_g[l:l + 1], wi_int, 0)
        ya = gmlp_fwd(proj, ln_g[l], ln_b[l], gmlp_w_s[l], bs_t[l])
        yb, states = hgrn_fwd(proj, lb0, lb1, onorm[l], l)
        ka, va, vt, kt, qt, qa = fox_prep(proj, bf_row[l])
        ride = ([wo_blk] if l == 0 else []) + ([wi_blk[l + 1]] if l + 1 < depth else [])
        o, lse, *gathered = fox_fwd(qt, ka, vt, AllGatherWeights(ride) if ride else None)
        if l == 0:
            wo_all = gathered.pop(0)
        if gathered:
            (wi_all,) = gathered
        x_in = xc
        if l + 1 < depth:
            xc, yfull = outproj(x_in, ya, yb, o, proj, wo_all, l)
        else:
            dx, yfull, d_final_g, loss_tile = outproj(x_in, ya, yb, o, proj, wo_all, l, (final_norm_g[None], tgt))
        saved.append((x_in, proj, h, states, ka, va, kt, qt, qa, o, lse, yfull, wi_int))

    n_shard = w_in.shape[2]
    g_norm = [None] * depth
    g_ln_g, g_ln_b, g_ws, g_bs, g_on, g_bf = ([None] * depth for _ in range(6))
    g_lb0, g_lb1 = jnp.zeros_like(lb0), jnp.zeros_like(lb1)
    swi = swo = rwi = rwo = None
    for l in reversed(range(depth)):
        x_in, proj, h, states, ka, va, kt, qt, qa, o, lse, yfull, wi_int = saved[l]
        dy, gwo = outproj_bwd(dx, yfull, wo_all, l)
        dproj, g_ln_g[l], g_ln_b[l], g_ws[l], dbs_t = gmlp_bwd(proj, dy, ln_g[l], ln_b[l], gmlp_w_s[l], bs_t[l])
        g_bs[l] = dbs_t[:, :A_GROUPS].T
        if l > 0:
            (qwo,) = _exchange_call(PairExchange([gwo]), f"pair_exchange_w_out_{l}")
        else:
            gws = jnp.stack(g_ws).reshape(-1, LANES)
            qwo, qws = _exchange_call(PairExchange([gwo], [gws]), f"pair_exchange_w_out_{l}")
            sws = small_sum(gws, qws, "pair_sum_w_s")
        swo = pair_sum(gwo, qwo, BF16, gwo.shape[2], "pair_sum_w_out", core, l, depth, swo)
        dproj, d0, d1, don = hgrn_bwd(proj, states, dy, lb0, lb1, onorm[l], l, dproj)
        g_lb0, g_lb1 = g_lb0 + d0, g_lb1 + d1
        g_on[l] = don.reshape(B_HEADS, B_KDIM).sum(0)
        dob, dproj, dot_t = fox_bwd_prep(dy, o, proj, dproj)
        top = l == depth - 1
        ride = ChipExchange([swo] if top else [swi, swo], [(l,)] if top else [(l + 1,), (l,)],
                            [sws] if l == 0 else [], [rwo] if top else [rwi, rwo])
        outs = fox_bwd(ka, va, kt, qt, dot_t, qa, dob, lse, ride)
        dqkv, (dck, dcq), got = outs[:3], outs[3:5], list(outs[5:])
        if not top:
            rwi = got.pop(0)
        rwo = got.pop(0)
        if l == 0:
            (rws,) = got
        dproj, dbf = fox_post(dcq, dck, proj, bf_row[l], dproj)
        g_bf[l] = dbf[0, :C_HEADS]
        gwi = split_w_in_grad(inproj_bwd_w(h, dproj, dqkv), n_shard)[:, :, 0]
        (qwi,) = _exchange_call(PairExchange([gwi]), f"pair_exchange_w_in_{l}")
        swi = pair_sum(gwi, qwi, BF16, 256, "pair_sum_w_in", core, l, depth, swi)
        ride = ChipExchange([swi], [(l,)], stacked=[rwi]) if l == 0 else None
        outs = inproj_bwd_x(dproj, dqkv, wi_int, x_in, norm_g[l:l + 1], dx, 0, ride)
        dx, g_norm[l] = outs[:2]
        if ride is not None:
            (rwi,) = outs[2:]

    gsm = _pack_small([
        jnp.concatenate(g_norm), jnp.stack(g_ln_g), jnp.stack(g_ln_b), jnp.stack(g_bs),
        jnp.concatenate([g_lb0, g_lb1]), jnp.stack(g_on), jnp.stack(g_bf), d_final_g, loss_tile[0, 0]])
    (qsm,) = _exchange_call(PairExchange([], [gsm]), "pair_exchange_small")
    ssm = small_sum(gsm, qsm, "pair_sum_small")
    (rsm,) = _exchange_call(ChipExchange(gathered=[ssm]), "chip_exchange_small")

    small_w = (norm_g, gmlp_ln_g, gmlp_ln_b, gmlp_b_s, hgrn_lb, hgrn_onorm_g, fox_b_f, final_norm_g)
    small_m = (m_norm_g, m_gmlp_ln_g, m_gmlp_ln_b, m_gmlp_b_s, m_hgrn_lb, m_hgrn_onorm_g, m_fox_b_f, m_final_norm_g)
    small_v = (v_norm_g, v_gmlp_ln_g, v_gmlp_ln_b, v_gmlp_b_s, v_hgrn_lb, v_hgrn_onorm_g, v_fox_b_f, v_final_norm_g)
    res_wi = adam_reduce_columns(rwi, w_in, m_w_in, v_w_in, "adam_w_in", swi, chip)
    res_wo = adam_reduce(rwo, w_out, m_w_out, v_w_out, w_out.shape[1], "adam_w_out", own=swo, chip=chip)
    grads = _unpack_small(sum_parts(rsm, "sum_small"))
    names = [name for name, _ in _SMALL if name != "loss"]
    rows = lambda a: a.reshape(1, -1) if a.ndim == 1 else a
    res_sm = adam_small([rows(grads[k]) for k in names], *([rows(a) for a in wmv] for wmv in (small_w, small_m, small_v)))
    res_sm = [grads] + [{k: a.reshape(grads[k].shape) for k, a in zip(names, r, strict=True)} for r in res_sm]
    as_rows = lambda a: a.reshape(1, -1, LANES)
    res_ws = adam_reduce(rws[:, None], as_rows(gmlp_w_s), as_rows(m_gmlp_w_s), as_rows(v_gmlp_w_s), rws.shape[1], "adam_w_s")
    for s, r in zip(res_sm, res_ws, strict=True):
        s["gmlp_w_s"] = r.reshape(gmlp_w_s.shape)

    def group(i):
        s = res_sm[i]
        return [s["norm_g"], res_wi[i], res_wo[i], s["gmlp_ln_g"], s["gmlp_ln_b"], s["gmlp_w_s"], s["gmlp_b_s"],
                s["hgrn_lb"], s["hgrn_onorm_g"], s["fox_b_f"], s["final_norm_g"]]

    return (res_sm[0]["loss"], dx[None], *group(0), *group(1), *group(2), *group(3))
```

```python
import functools

import jax
import jax.numpy as jnp
import numpy as np
from jax import lax
from jax.experimental import pallas as pl
from jax.experimental.pallas import tpu as pltpu

F32 = jnp.float32
BF16 = jnp.bfloat16

NORM_EPS = 1e-6
F_FLOOR = 1e-30
CHUNK = 128
LANES = 128
VMEM_LIMIT = 56 * 1024 * 1024


def _cparams(*sem):
    return pltpu.CompilerParams(dimension_semantics=sem, vmem_limit_bytes=VMEM_LIMIT)


def _dot(a, b, dims=(((1,), (0,)), ((), ())), precision=None):
    return lax.dot_general(a, b, dims, precision=precision, preferred_element_type=F32)


_NT = (((1,), (1,)), ((), ()))
_TN = (((0,), (0,)), ((), ()))


def _bf16_pieces(x, n):
    out, r = [], x
    for i in range(n):
        out.append(r.astype(BF16))
        if i + 1 < n:
            r = r - out[-1].astype(F32)
    return out


@functools.partial(jax.custom_vjp, nondiff_argnums=(2,))
def _times_exact(x, e, n):
    return functools.reduce(jnp.add, [_dot(p, e) for p in _bf16_pieces(x, n)])


def _times_exact_fwd(x, e, n):
    return _times_exact(x, e, n), e


def _times_exact_bwd(n, e, g):
    dx = functools.reduce(jnp.add, [lax.dot_general(p, e, _NT, preferred_element_type=F32) for p in _bf16_pieces(g, n)])
    return dx, jnp.zeros_like(e)


_times_exact.defvjp(_times_exact_fwd, _times_exact_bwd)


@functools.partial(jax.custom_vjp, nondiff_argnums=(2,))
def _exact_times(e, x, n):
    return functools.reduce(jnp.add, [_dot(e, p) for p in _bf16_pieces(x, n)])


def _exact_times_fwd(e, x, n):
    return _exact_times(e, x, n), e


def _exact_times_bwd(n, e, g):
    dx = functools.reduce(jnp.add, [lax.dot_general(e, p, _TN, preferred_element_type=F32) for p in _bf16_pieces(g, n)])
    return jnp.zeros_like(e), dx


_exact_times.defvjp(_exact_times_fwd, _exact_times_bwd)


def _group_mean_matrix(width, group):
    idx = np.arange(width) // group
    return jnp.asarray((idx[:, None] == idx[None, :]).astype(np.float32) / group, BF16)


def _group_ones_matrix(width, group):
    idx = np.arange(width) // group
    return jnp.asarray((idx[:, None] == idx[None, :]).astype(np.float32), BF16)


A_WIDTH = 256
A_GROUPS = 4
A_GDIM = 64


A_ROWS = 512


def _gmlp_chunk(x3, ln_g, ln_b, w_s, bs_t, mean_m, gind):
    n = x3.shape[0] // CHUNK
    u = jax.nn.gelu(x3[:, :A_WIDTH])
    v = jax.nn.gelu(x3[:, A_WIDTH:2 * A_WIDTH])
    z = x3[:, 2 * A_WIDTH:]
    mu = _times_exact(v, mean_m, 2)
    d = v - mu
    var = _times_exact(d * d, mean_m, 2)
    vn = d * lax.rsqrt(var + NORM_EPS) * ln_g + ln_b
    vnb = vn.astype(BF16)
    wide = jnp.concatenate([vnb[i * CHUNK:(i + 1) * CHUNK] for i in range(n)], axis=1)
    row = lax.broadcasted_iota(jnp.int32, (CHUNK, CHUNK), 0)
    col = lax.broadcasted_iota(jnp.int32, (CHUNK, CHUNK), 1)
    causal = row >= col
    lane_g = lax.shift_right_logical(lax.broadcasted_iota(jnp.int32, (CHUNK, n * A_WIDTH), 1), 6) & (A_GROUPS - 1)
    bias = _times_exact(bs_t, gind, 3)
    mixed = jnp.concatenate([bias] * n, axis=1)
    for g in range(A_GROUPS):
        wc = jnp.where(causal, w_s[g], 0.0).astype(BF16)
        mixed = mixed + jnp.where(lane_g == g, _dot(wc, wide), 0.0)
    mixed = jnp.concatenate([mixed[:, i * A_WIDTH:(i + 1) * A_WIDTH] for i in range(n)], axis=0)
    return u * mixed * jax.nn.silu(z)


def _gmlp_consts():
    gind = np.zeros((LANES, A_WIDTH), np.float32)
    for g in range(A_GROUPS):
        gind[g, g * A_GDIM:(g + 1) * A_GDIM] = 1.0
    return _group_mean_matrix(A_WIDTH, A_GDIM), jnp.asarray(gind, BF16)


def _full(shape):
    return pl.BlockSpec(shape, lambda *_: (0,) * len(shape))


def gmlp_fwd(proj, ln_g, ln_b, w_s, bs_t):
    seq = proj.shape[0]
    rows = min(A_ROWS, seq)
    mean_m, gind = _gmlp_consts()

    def body(x_ref, g_ref, b_ref, w_ref, bs_ref, m_ref, gi_ref, y_ref):
        y = _gmlp_chunk(x_ref[...], g_ref[...], b_ref[...], w_ref[...], bs_ref[...], m_ref[...], gi_ref[...])
        y_ref[...] = y.astype(BF16)

    return pl.pallas_call(
        body,
        name="gmlp_fwd",
        grid=(seq // rows,),
        in_specs=[
            pl.BlockSpec((rows, 3 * A_WIDTH), lambda n: (n, 0)),
            _full((1, A_WIDTH)), _full((1, A_WIDTH)), _full((A_GROUPS, CHUNK, CHUNK)), _full((CHUNK, LANES)),
            _full((A_WIDTH, A_WIDTH)), _full((LANES, A_WIDTH)),
        ],
        out_specs=pl.BlockSpec((rows, A_WIDTH), lambda n: (n, 0)),
        out_shape=jax.ShapeDtypeStruct((seq, A_WIDTH), BF16),
        compiler_params=_cparams("parallel"),
    )(proj, ln_g, ln_b, w_s, bs_t, mean_m, gind)


def gmlp_bwd(proj, dy, ln_g, ln_b, w_s, bs_t):
    seq = proj.shape[0]
    rows = min(A_ROWS, seq)
    mean_m, gind = _gmlp_consts()

    def body(x_ref, dy_ref, g_ref, b_ref, w_ref, bs_ref, m_ref, gi_ref, dx_ref, dg_ref, db_ref, dw_ref, dbs_ref):
        fn = functools.partial(_gmlp_chunk, mean_m=m_ref[...], gind=gi_ref[...])
        _, vjp = jax.vjp(fn, x_ref[...], g_ref[...], b_ref[...], w_ref[...], bs_ref[...])
        dx, dg, db, dw, dbs = vjp(dy_ref[...])
        dx_ref[...] = dx.astype(BF16)

        @pl.when(pl.program_id(0) == 0)
        def _():
            dg_ref[...] = jnp.zeros_like(dg_ref)
            db_ref[...] = jnp.zeros_like(db_ref)
            dw_ref[...] = jnp.zeros_like(dw_ref)
            dbs_ref[...] = jnp.zeros_like(dbs_ref)

        dg_ref[...] += dg
        db_ref[...] += db
        dw_ref[...] += dw
        dbs_ref[...] += dbs

    return pl.pallas_call(
        body,
        name="gmlp_bwd",
        grid=(seq // rows,),
        in_specs=[
            pl.BlockSpec((rows, 3 * A_WIDTH), lambda n: (n, 0)),
            pl.BlockSpec((rows, A_WIDTH), lambda n: (n, 0)),
            _full((1, A_WIDTH)), _full((1, A_WIDTH)), _full((A_GROUPS, CHUNK, CHUNK)), _full((CHUNK, LANES)),
            _full((A_WIDTH, A_WIDTH)), _full((LANES, A_WIDTH)),
        ],
        out_specs=[
            pl.BlockSpec((rows, 3 * A_WIDTH), lambda n: (n, 0)),
            _full((1, A_WIDTH)), _full((1, A_WIDTH)), _full((A_GROUPS, CHUNK, CHUNK)), _full((CHUNK, LANES)),
        ],
        out_shape=[
            jax.ShapeDtypeStruct((seq, D_INT), BF16),
            jax.ShapeDtypeStruct((1, A_WIDTH), F32), jax.ShapeDtypeStruct((1, A_WIDTH), F32),
            jax.ShapeDtypeStruct((A_GROUPS, CHUNK, CHUNK), F32), jax.ShapeDtypeStruct((CHUNK, LANES), F32),
        ],
        compiler_params=_cparams("arbitrary"),
    )(proj, dy, ln_g, ln_b, w_s, bs_t, mean_m, gind)


B_WIDTH = 256
B_HEADS = 4
B_KDIM = 64
B_LEVELS = (64, 32, 16, 8, 4, 2, 1)


def _hgrn_consts():
    t = np.arange(CHUNK)
    u = t[None, :]
    mats = [np.tril(np.ones((CHUNK, CHUNK), np.float32))]
    for m in B_LEVELS:
        p = (t // (2 * m)) * (2 * m) + m - 1
        right = (t % (2 * m)) >= m
        sel = np.where(right[:, None], (u > p[:, None]) & (u <= t[:, None]), (u > t[:, None]) & (u <= p[:, None]))
        mats.append(sel.astype(np.float32))
    return jnp.asarray(np.concatenate(mats, 0), BF16), _group_ones_matrix(B_WIDTH, B_KDIM)


def _hgrn_lower_bound(lb0, lb1, layer):
    mx = jnp.maximum(lb0, lb1)
    e0 = jnp.exp(lb0 - mx)
    e1 = jnp.exp(lb1 - mx)
    p0 = e0 / (e0 + e1)
    p1 = e1 / (e0 + e1)
    cs = p0 if layer == 0 else p0 + p1
    return jnp.clip(cs - p0, 0.0, 1.0 - 1e-6)


def _hgrn_chunk(x4, st, lb0, lb1, onorm, layer, tstack, ones_bd):
    q_raw, fl, v, zg = (x4[:, i * B_WIDTH:(i + 1) * B_WIDTH] for i in range(4))
    lb = _hgrn_lower_bound(lb0, lb1, layer)
    q = jax.nn.silu(q_raw) * (B_KDIM ** -0.5)
    f = lb + (1.0 - lb) * jax.nn.sigmoid(fl)
    logf = jnp.log(jnp.maximum(f, F_FLOOR))
    k = (1.0 - lb) * jax.nn.sigmoid(-fl)
    b = _exact_times(tstack[:CHUNK], logf, 3)
    dall = jnp.concatenate([b, _exact_times(tstack[CHUNK:], logf, 2)], axis=0)
    b_last = jnp.sum(logf, axis=0, keepdims=True)
    vb = v.astype(BF16)

    lane_h = lax.shift_right_logical(lax.broadcasted_iota(jnp.int32, (CHUNK, B_WIDTH), 1), 6)
    row = lax.broadcasted_iota(jnp.int32, (CHUNK, B_WIDTH), 0)
    srow = lax.broadcasted_iota(jnp.int32, (B_HEADS * CHUNK, CHUNK), 0) & (CHUNK - 1)
    scol = lax.broadcasted_iota(jnp.int32, (B_HEADS * CHUNK, CHUNK), 1)

    def heads_on_rows(a):
        return jnp.concatenate([jnp.where(lane_h == h, a, 0.0) for h in range(B_HEADS)], axis=0)

    def heads_from_rows(r):
        out = jnp.where(lane_h == 0, r[:CHUNK], 0.0)
        for h in range(1, B_HEADS):
            out = out + jnp.where(lane_h == h, r[h * CHUNK:(h + 1) * CHUNK], 0.0)
        return out

    o = lax.dot_general((q * jnp.exp(b)).astype(BF16), st.astype(BF16), _NT, preferred_element_type=F32)
    scores = jnp.zeros((B_HEADS * CHUNK, CHUNK), F32)
    for li, m in enumerate(B_LEVELS):
        e = jnp.exp(dall[(li + 1) * CHUNK:(li + 2) * CHUNK])
        right = (row & (2 * m - 1)) >= m
        qt = jnp.where(right, q * e, 0.0)
        kt = jnp.where(right, 0.0, k * e)
        sc = lax.dot_general(heads_on_rows(qt).astype(BF16), kt.astype(BF16), _NT, preferred_element_type=F32)
        sh = int(np.log2(2 * m))
        same = lax.shift_right_logical(srow, sh) == lax.shift_right_logical(scol, sh)
        scores = scores + jnp.where(same, sc, 0.0)
    o = o + heads_from_rows(_dot(scores.astype(BF16), vb))
    o = o + _times_exact(q * k, ones_bd, 2) * v

    kv = lax.dot_general(vb, (k * jnp.exp(b_last - b)).astype(BF16), _TN, preferred_element_type=F32)
    st_new = st * jnp.exp(b_last) + jnp.where(ones_bd > 0.5, kv, 0.0)

    ms = _times_exact(o * o, ones_bd, 2) * (1.0 / B_KDIM)
    y = o * lax.rsqrt(ms + NORM_EPS) * onorm * jax.nn.silu(zg)
    return y, st_new


B_ROWS = 256


def _hgrn_rows(x4, st, lb0, lb1, onorm, layer, tstack, ones_bd):
    ys = []
    for i in range(x4.shape[0] // CHUNK):
        y, st = _hgrn_chunk(x4[i * CHUNK:(i + 1) * CHUNK], st, lb0, lb1, onorm, layer, tstack, ones_bd)
        ys.append(y)
    return jnp.concatenate(ys, axis=0), st


def hgrn_fwd(proj, lb0, lb1, onorm, layer):
    seq = proj.shape[0]
    rows = min(B_ROWS, seq)
    nc = seq // rows
    tstack, ones_bd = _hgrn_consts()

    def body(x_ref, lb0_ref, lb1_ref, on_ref, t_ref, e_ref, y_ref, st_out_ref, st_ref):
        @pl.when(pl.program_id(0) == 0)
        def _():
            st_ref[...] = jnp.zeros_like(st_ref)

        st = st_ref[...]
        st_out_ref[0] = st
        y, st_new = _hgrn_rows(x_ref[...], st, lb0_ref[...], lb1_ref[...], on_ref[...], layer, t_ref[...], e_ref[...])
        y_ref[...] = y.astype(BF16)
        st_ref[...] = st_new

    return pl.pallas_call(
        body,
        name=f"hgrn_fwd_{layer}",
        grid=(nc,),
        in_specs=[
            pl.BlockSpec((rows, 4 * B_WIDTH), lambda n: (n, 1)),
            _full((1, B_WIDTH)), _full((1, B_WIDTH)), _full((1, B_WIDTH)),
            _full(((len(B_LEVELS) + 1) * CHUNK, CHUNK)), _full((B_WIDTH, B_WIDTH)),
        ],
        out_specs=[
            pl.BlockSpec((rows, B_WIDTH), lambda n: (n, 0)),
            pl.BlockSpec((1, B_WIDTH, B_WIDTH), lambda n: (n, 0, 0)),
        ],
        out_shape=[jax.ShapeDtypeStruct((seq, B_WIDTH), BF16), jax.ShapeDtypeStruct((nc, B_WIDTH, B_WIDTH), F32)],
        scratch_shapes=[pltpu.VMEM((B_WIDTH, B_WIDTH), F32)],
        compiler_params=_cparams("arbitrary"),
    )(proj, lb0, lb1, onorm, tstack, ones_bd)


def hgrn_bwd(proj, states, dy, lb0, lb1, onorm, layer, dproj):
    seq = proj.shape[0]
    rows = min(B_ROWS, seq)
    nc = seq // rows
    tstack, ones_bd = _hgrn_consts()

    def body(x_ref, st_in_ref, dy_ref, lb0_ref, lb1_ref, on_ref, t_ref, e_ref, _, dx_ref, d0_ref, d1_ref, don_ref, dst_ref):
        @pl.when(pl.program_id(0) == 0)
        def _():
            dst_ref[...] = jnp.zeros_like(dst_ref)
            d0_ref[...] = jnp.zeros_like(d0_ref)
            d1_ref[...] = jnp.zeros_like(d1_ref)
            don_ref[...] = jnp.zeros_like(don_ref)

        fn = functools.partial(_hgrn_rows, layer=layer, tstack=t_ref[...], ones_bd=e_ref[...])
        _, vjp = jax.vjp(fn, x_ref[...], st_in_ref[0], lb0_ref[...], lb1_ref[...], on_ref[...])
        dx, dst, d0, d1, don = vjp((dy_ref[...], dst_ref[...]))
        dx_ref[...] = dx.astype(BF16)
        dst_ref[...] = dst
        d0_ref[...] += d0
        d1_ref[...] += d1
        don_ref[...] += don

    rev = lambda n: nc - 1 - n
    return pl.pallas_call(
        body,
        name=f"hgrn_bwd_{layer}",
        grid=(nc,),
        in_specs=[
            pl.BlockSpec((rows, 4 * B_WIDTH), lambda n: (rev(n), 1)),
            pl.BlockSpec((1, B_WIDTH, B_WIDTH), lambda n: (rev(n), 0, 0)),
            pl.BlockSpec((rows, B_WIDTH), lambda n: (rev(n), 1)),
            _full((1, B_WIDTH)), _full((1, B_WIDTH)), _full((1, B_WIDTH)),
            _full(((len(B_LEVELS) + 1) * CHUNK, CHUNK)), _full((B_WIDTH, B_WIDTH)), _ANY,
        ],
        out_specs=[
            pl.BlockSpec((rows, 4 * B_WIDTH), lambda n: (rev(n), 1)),
            _full((1, B_WIDTH)), _full((1, B_WIDTH)), _full((1, B_WIDTH)),
        ],
        out_shape=[jax.ShapeDtypeStruct(dproj.shape, BF16)] + [jax.ShapeDtypeStruct((1, B_WIDTH), F32)] * 3,
        input_output_aliases={8: 0},
        scratch_shapes=[pltpu.VMEM((B_WIDTH, B_WIDTH), F32)],
        compiler_params=_cparams("arbitrary"),
    )(proj, states, dy, lb0, lb1, onorm, tstack, ones_bd, dproj)


D_MODEL = 1024
D_INT = 4096


def _rms_stats(xf):
    r = lax.rsqrt(jnp.mean(xf * xf, axis=-1, keepdims=True) + NORM_EPS)
    return r, xf * r


def _rms_bwd(dy, g, r, xh):
    u = dy * g
    return r * (u - xh * jnp.mean(u * xh, axis=-1, keepdims=True))


def inproj(x, g, w, layer):
    seq = x.shape[0]
    tm = min(seq, 512)

    def body(x_ref, g_ref, w_ref, p_ref, h_ref):
        _, xh = _rms_stats(x_ref[...])
        h = (xh * g_ref[...]).astype(BF16)
        h_ref[...] = h
        p_ref[...] = _dot(h, w_ref[0])

    return pl.pallas_call(
        body,
        name="inproj",
        grid=(seq // tm,),
        in_specs=[
            pl.BlockSpec((tm, D_MODEL), lambda i: (i, 0)),
            _full((1, D_MODEL)),
            pl.BlockSpec((1, D_MODEL, D_INT), lambda i: (layer, 0, 0)),
        ],
        out_specs=[pl.BlockSpec((tm, D_INT), lambda i: (i, 0)), pl.BlockSpec((tm, D_MODEL), lambda i: (i, 0))],
        out_shape=[jax.ShapeDtypeStruct((seq, D_INT), F32), jax.ShapeDtypeStruct((seq, D_MODEL), BF16)],
        compiler_params=_cparams("parallel"),
    )(x, g, w)


def outproj(x, ya, yb, o, proj, wo, layer, head=None):
    seq = x.shape[0]
    tm = min(seq, 512)
    blk = wo.shape[2]

    def body(x_ref, ya_ref, yb_ref, o_ref, z_ref, w_ref, *refs):
        yc = (o_ref[...] * jax.nn.silu(z_ref[...])).astype(BF16)
        y = jnp.concatenate([ya_ref[...], yb_ref[...], yc], axis=1)
        w = jnp.concatenate([w_ref[d, 0] for d in range(N_DEV)], axis=0)
        xn = x_ref[...] + _dot(y, w)
        if head is None:
            xn_ref, y_ref = refs
            xn_ref[...] = xn
        else:
            g_ref, t_ref, dx_ref, y_ref, dg_ref, loss_ref = refs

            @pl.when(pl.program_id(0) == 0)
            def _():
                dg_ref[...] = jnp.zeros_like(dg_ref)
                loss_ref[...] = jnp.zeros_like(loss_ref)

            g = g_ref[...]
            r, xh = _rms_stats(xn)
            err = xh * g - t_ref[...]
            sq = jnp.sum(jnp.sum(err * err, axis=1, keepdims=True), axis=0, keepdims=True)
            loss_ref[...] += jnp.broadcast_to(sq * (0.5 / D_MODEL), loss_ref.shape)
            dout = err * (1.0 / D_MODEL)
            dg_ref[...] += jnp.sum(dout * xh, axis=0, keepdims=True)
            dx_ref[...] = _rms_bwd(dout, g, r, xh)
        y_ref[...] = y

    rows = lambda: pl.BlockSpec((tm, D_MODEL), lambda i: (i, 0))
    tail = (() if head is None else (_full((1, D_MODEL)), rows()),
            () if head is None else (_full((1, D_MODEL)), _full((8, LANES))),
            () if head is None else (jax.ShapeDtypeStruct((1, D_MODEL), F32), jax.ShapeDtypeStruct((8, LANES), F32)))
    return pl.pallas_call(
        body,
        name="outproj" if head is None else "outproj_loss",
        grid=(seq // tm,),
        in_specs=[
            rows(),
            pl.BlockSpec((tm, 256), lambda i: (i, 0)),
            pl.BlockSpec((tm, 256), lambda i: (i, 0)),
            pl.BlockSpec((tm, 512), lambda i: (i, 0)),
            pl.BlockSpec((tm, 512), lambda i: (i, 7)),
            pl.BlockSpec((N_DEV, 1, blk, D_MODEL), lambda i: (0, layer, 0, 0)),
            *tail[0],
        ],
        out_specs=[rows(), rows(), *tail[1]],
        out_shape=[jax.ShapeDtypeStruct((seq, D_MODEL), F32), jax.ShapeDtypeStruct((seq, D_MODEL), BF16), *tail[2]],
        compiler_params=_cparams("parallel" if head is None else "arbitrary"),
    )(x, ya, yb, o, proj, wo, *(head or ()))


def outproj_bwd(dx, y, wo, layer):
    seq = dx.shape[0]
    ts = min(seq, 512)
    blk = wo.shape[2]

    def body(dx_ref, y_ref, w_ref, dy_ref, dw_ref):
        @pl.when(pl.program_id(0) == 0)
        def _():
            dw_ref[...] = jnp.zeros_like(dw_ref)

        dxb = dx_ref[...].astype(BF16)
        w = jnp.concatenate([w_ref[d, 0] for d in range(N_DEV)], axis=0)
        dy_ref[...] = lax.dot_general(dxb, w, _NT, preferred_element_type=F32)
        dw = lax.dot_general(y_ref[...], dxb, _TN, preferred_element_type=F32)
        for d in range(N_DEV):
            dw_ref[d % 2, d // 2] += dw[d * blk:(d + 1) * blk]

    return pl.pallas_call(
        body,
        name="outproj_bwd",
        grid=(seq // ts,),
        in_specs=[
            pl.BlockSpec((ts, D_MODEL), lambda i: (i, 0)),
            pl.BlockSpec((ts, D_MODEL), lambda i: (i, 0)),
            pl.BlockSpec((N_DEV, 1, blk, D_MODEL), lambda i: (0, layer, 0, 0)),
        ],
        out_specs=[pl.BlockSpec((ts, D_MODEL), lambda i: (i, 0)),
                   pl.BlockSpec((2, N_CHIP, blk, D_MODEL), lambda i: (0, 0, 0, 0))],
        out_shape=[jax.ShapeDtypeStruct((seq, D_MODEL), F32), jax.ShapeDtypeStruct((2, N_CHIP, blk, D_MODEL), F32)],
        compiler_params=_cparams("arbitrary"),
    )(dx, y, wo)


C_QKV = (2048, 3584)


def _dproj_parts(dp_ref, dqkv_refs, rows):
    lo, hi = C_QKV
    step = (hi - lo) // len(dqkv_refs)
    return ([(0, dp_ref.at[rows, 0:lo])] + [(lo + i * step, r.at[rows, :]) for i, r in enumerate(dqkv_refs)]
            + [(hi, dp_ref.at[rows, hi:D_INT])])


def inproj_bwd_x(dproj, dqkv, w, x, g, dx_in, layer, carried=None):
    seq = x.shape[0]
    tm = min(seq, 512)

    def body(dp_ref, dq_ref, dk_ref, dv_ref, w_ref, x_ref, g_ref, dxin_ref, dx_ref, dg_ref):
        @pl.when(pl.program_id(0) == 0)
        def _():
            dg_ref[...] = jnp.zeros_like(dg_ref)

        dh = None
        for at, part in _dproj_parts(dp_ref, (dq_ref, dk_ref, dv_ref), slice(None)):
            term = lax.dot_general(part[...], w_ref[0, :, at:at + part.shape[1]], _NT, preferred_element_type=F32)
            dh = term if dh is None else dh + term
        r, xh = _rms_stats(x_ref[...])
        dg_ref[...] += jnp.sum(dh * xh, axis=0, keepdims=True)
        dx_ref[...] = dxin_ref[...] + _rms_bwd(dh, g_ref[...], r, xh)

    third = lambda: pl.BlockSpec((tm, C_WIDTH), lambda i: (i, 0))
    return _call_carrying(
        carried, body, (dproj, *dqkv, w, x, g, dx_in),
        name="inproj_bwd_x",
        grid=(seq // tm,),
        in_specs=[
            pl.BlockSpec((tm, D_INT), lambda i: (i, 0)), third(), third(), third(),
            pl.BlockSpec((1, D_MODEL, D_INT), lambda i: (layer, 0, 0)),
            pl.BlockSpec((tm, D_MODEL), lambda i: (i, 0)),
            _full((1, D_MODEL)),
            pl.BlockSpec((tm, D_MODEL), lambda i: (i, 0)),
        ],
        out_specs=[pl.BlockSpec((tm, D_MODEL), lambda i: (i, 0)), _full((1, D_MODEL))],
        out_shape=[jax.ShapeDtypeStruct((seq, D_MODEL), F32), jax.ShapeDtypeStruct((1, D_MODEL), F32)],
        scratch_shapes=[], semantics=("arbitrary",),
    )


def inproj_bwd_w(h, dproj, dqkv):
    seq = h.shape[0]
    ts, tn = min(seq, 512), 512

    def body(h_ref, dp_ref, dq_ref, dk_ref, dv_ref, dw_ref):
        @pl.when(pl.program_id(0) == 0)
        def _():
            dw_ref[...] = jnp.zeros_like(dw_ref)

        ht = h_ref[...].T
        for at, part in _dproj_parts(dp_ref, (dq_ref, dk_ref, dv_ref), slice(None)):
            for c in range(0, part.shape[1], tn):
                dw_ref[0, :, at + c:at + c + tn] += _dot(ht, part[:, c:c + tn])

    third = lambda: pl.BlockSpec((ts, C_WIDTH), lambda s: (s, 0))
    return pl.pallas_call(
        body,
        name="inproj_bwd_w",
        grid=(seq // ts,),
        in_specs=[pl.BlockSpec((ts, D_MODEL), lambda s: (s, 0)), pl.BlockSpec((ts, D_INT), lambda s: (s, 0)),
                  third(), third(), third()],
        out_specs=_full((1, D_MODEL, D_INT)),
        out_shape=jax.ShapeDtypeStruct((1, D_MODEL, D_INT), F32),
        compiler_params=_cparams("arbitrary"),
    )(h, dproj, *dqkv)


N_IN = 3848


def _internal_of(col):
    return col if col < 768 else (col + 256 if col < 3840 else 768 + col - 3840)


def _column_runs(n_shard):
    runs = []
    for d in range(N_IN // n_shard):
        mine = []
        for j in range(n_shard):
            ci = _internal_of(d * n_shard + j)
            if mine and mine[-1][0] + mine[-1][1] == ci:
                mine[-1][1] += 1
            else:
                mine.append([ci, 1, j])
        runs.append(mine)
    return runs


def assemble_w_in(wi_all):
    n_dev, depth, _, n_shard = wi_all.shape
    tr = 256
    pieces = [[] for _ in range(D_INT // LANES)]
    for d, mine in enumerate(_column_runs(n_shard)):
        for ci, ln, off in mine:
            while ln > 0:
                blk, at = divmod(ci, LANES)
                take = min(ln, LANES - at)
                pieces[blk].append((at, take, d, off))
                ci, ln, off = ci + take, ln - take, off + take

    def body(x_ref, o_ref):
        for blk, parts in enumerate(pieces):
            vals, at = [], 0
            for start, ln, d, off in sorted(parts):
                if start > at:
                    vals.append(jnp.zeros((tr, start - at), BF16))
                vals.append(x_ref[d, 0, :, off:off + ln])
                at = start + ln
            if at < LANES:
                vals.append(jnp.zeros((tr, LANES - at), BF16))
            o_ref[0, :, blk * LANES:(blk + 1) * LANES] = vals[0] if len(vals) == 1 else jnp.concatenate(vals, axis=1)

    return pl.pallas_call(
        body,
        name="assemble_w_in",
        grid=(depth, D_MODEL // tr),
        in_specs=[pl.BlockSpec((n_dev, 1, tr, n_shard), lambda l, r: (0, l, r, 0))],
        out_specs=pl.BlockSpec((1, tr, D_INT), lambda l, r: (l, r, 0)),
        out_shape=jax.ShapeDtypeStruct((depth, D_MODEL, D_INT), BF16),
        compiler_params=_cparams("parallel", "parallel"),
    )(wi_all)


def split_w_in_grad(dwi, n_shard, core):
    tr = 256
    runs = _column_runs(n_shard)

    def body(core_ref, x_ref, keep_ref, send_ref):
        for d, mine in enumerate(runs):
            @pl.when(core_ref[0] == d % 2)
            def _():
                for ci, ln, off in mine:
                    keep_ref[d // 2, :, off:off + ln] = x_ref[0, :, ci:ci + ln]

            @pl.when(core_ref[0] != d % 2)
            def _():
                for ci, ln, off in mine:
                    send_ref[d // 2, :, off:off + ln] = x_ref[0, :, ci:ci + ln].astype(BF16)

    shards = lambda: pl.BlockSpec((N_CHIP, tr, n_shard), lambda r, s: (0, r, 0))
    grid_spec = pltpu.PrefetchScalarGridSpec(
        num_scalar_prefetch=1, grid=(D_MODEL // tr,),
        in_specs=[pl.BlockSpec((1, tr, D_INT), lambda r, s: (0, r, 0))], out_specs=[shards(), shards()])
    return pl.pallas_call(
        body,
        name="split_w_in_grad",
        grid_spec=grid_spec,
        out_shape=[jax.ShapeDtypeStruct((N_CHIP, D_MODEL, n_shard), F32), jax.ShapeDtypeStruct((N_CHIP, D_MODEL, n_shard), BF16)],
        compiler_params=_cparams("parallel"),
    )(core, dwi)


C_WIDTH = 512
C_HEADS = 8
C_HDIM = 64
C_PAIRS = C_HEADS // 2
C_BQ = 512
C_TAIL = 16
C_KG = 4


def _split3(x):
    hi = x.astype(BF16)
    r = x - hi.astype(F32)
    mid = r.astype(BF16)
    return hi, mid, (r - mid.astype(F32)).astype(BF16)


def _piece_selectors():
    sel = np.zeros((C_HEADS, 3 * LANES, LANES), np.float32)
    for p in range(C_PAIRS):
        for e in range(2):
            for t in range(3):
                sel[2 * p + e, t * LANES + 2 * p + e, 3 * e + t] = -1.0
    return sel


def fox_prep(proj, bf_row):
    seq = proj.shape[0]
    nblk = seq // CHUNK
    tril = jnp.asarray(np.tril(np.ones((CHUNK, CHUNK), np.float32)), BF16)
    sel = jnp.asarray(_piece_selectors(), BF16)
    rows_t = CHUNK + C_TAIL

    def body(fl_ref, q_ref, k_ref, v_ref, bf_ref, l_ref, sel_ref, ka_ref, va_ref, vt_ref, kt_ref, qt_ref, qa_ref, carry_ref):
        @pl.when(pl.program_id(0) == 0)
        def _():
            carry_ref[...] = jnp.zeros_like(carry_ref)

        lf = jax.nn.log_sigmoid(fl_ref[:, :LANES] + bf_ref[...])
        c = _exact_times(l_ref[...], lf, 3) + carry_ref[...]
        carry_ref[...] += jnp.sum(lf, axis=0, keepdims=True)
        c3 = jnp.concatenate(_split3(c), axis=1)
        lane = lax.broadcasted_iota(jnp.int32, (CHUNK, LANES), 1)
        row = lax.broadcasted_iota(jnp.int32, (CHUNK, LANES), 0)
        r16 = lax.broadcasted_iota(jnp.int32, (C_TAIL, 2 * CHUNK), 0)
        l16 = lax.broadcasted_iota(jnp.int32, (C_TAIL, 2 * CHUNK), 1)
        zero = jnp.zeros((CHUNK, LANES), BF16)
        one = jnp.ones((CHUNK, LANES), BF16)

        def by_keys(x, right_a, right_b):
            xb = x.astype(BF16)
            top = jnp.concatenate([jnp.where(lane < C_HDIM, xb, zero), right_a], axis=1)
            return jnp.concatenate([top, jnp.concatenate([jnp.where(lane < C_HDIM, zero, xb), right_b], axis=1)], axis=0)

        def by_lanes(x, tail):
            xt = x.T.astype(BF16)
            main = jnp.concatenate([jnp.where(row < C_HDIM, xt, zero), jnp.where(row < C_HDIM, zero, xt)], axis=1)
            return jnp.concatenate([main, tail], axis=0)

        for p in range(C_PAIRS):
            cols = slice(p * LANES, (p + 1) * LANES)
            q2, k2, v2 = q_ref[:, cols] * (C_HDIM ** -0.5), k_ref[:, cols], v_ref[:, cols]
            negc = [_dot(c3, sel_ref[2 * p + e]).astype(BF16) for e in range(2)]
            ones3 = [jnp.where((lane >= 3 * e) & (lane < 3 * e + 3), one, zero) for e in range(2)]
            tail = jnp.where(((r16 == 2 * p) & (l16 < CHUNK)) | ((r16 == 2 * p + 1) & (l16 >= CHUNK)), 1.0, 0.0).astype(BF16)
            ka_ref[p] = by_keys(k2, negc[0], negc[1])
            va_ref[p] = by_keys(v2, ones3[0], ones3[1])
            kt_ref[p] = by_lanes(k2, tail)
            vt_ref[p] = by_lanes(v2, tail)
            qt_ref[p] = jnp.concatenate([q2.T.astype(BF16), jnp.where(row < 6, one, zero)], axis=0)
            qa_ref[p] = jnp.concatenate([q2.astype(BF16), jnp.where((lane == 2 * p) | (lane == 2 * p + 1), one, zero)], axis=1)

    wide = lambda j: pl.BlockSpec((CHUNK, C_WIDTH), lambda n: (n, j))
    by_rows = pl.BlockSpec((C_PAIRS, 2 * CHUNK, 2 * CHUNK), lambda n: (0, n, 0))
    by_cols = pl.BlockSpec((C_PAIRS, rows_t, 2 * CHUNK), lambda n: (0, 0, n))
    return pl.pallas_call(
        body,
        name="fox_prep",
        grid=(nblk,),
        in_specs=[pl.BlockSpec((CHUNK, 256), lambda n: (n, 3)), wide(4), wide(5), wide(6), _full((1, LANES)),
                  _full((CHUNK, CHUNK)), _full((C_HEADS, 3 * LANES, LANES))],
        out_specs=[by_rows, by_rows, by_cols, by_cols,
                   pl.BlockSpec((C_PAIRS, 2 * CHUNK, CHUNK), lambda n: (0, 0, n)),
                   pl.BlockSpec((C_PAIRS, CHUNK, 2 * CHUNK), lambda n: (0, n, 0))],
        out_shape=[jax.ShapeDtypeStruct((C_PAIRS, 2 * seq, 2 * CHUNK), BF16)] * 2
        + [jax.ShapeDtypeStruct((C_PAIRS, rows_t, 2 * seq), BF16)] * 2
        + [jax.ShapeDtypeStruct((C_PAIRS, 2 * CHUNK, seq), BF16), jax.ShapeDtypeStruct((C_PAIRS, seq, 2 * CHUNK), BF16)],
        scratch_shapes=[pltpu.VMEM((1, LANES), F32)],
        compiler_params=_cparams("arbitrary"),
    )(proj, proj, proj, proj, bf_row, tril, sel)


def _visible(shape, key0, query0):
    row = lax.broadcasted_iota(jnp.int32, shape, 0)
    key = key0 + lax.shift_left(lax.shift_right_logical(row, 8), 7) + (row & (CHUNK - 1))
    return key <= query0 + lax.broadcasted_iota(jnp.int32, shape, 1)


def _rows_ab(a, b, n):
    return jnp.concatenate([jnp.broadcast_to(a, (C_HDIM, n)), jnp.broadcast_to(b, (C_HDIM, n))], axis=0)


def _call_carrying(ex, body, operands, *, name, grid, in_specs, out_specs, out_shape, scratch_shapes, semantics=None):
    if ex is None:
        semantics = semantics or ("parallel", *["arbitrary"] * (len(grid) - 1))
        return pl.pallas_call(body, name=name, grid=grid, in_specs=in_specs, out_specs=out_specs, out_shape=out_shape,
                              scratch_shapes=scratch_shapes, compiler_params=_cparams(*semantics))(*operands)
    n_in, n_out = len(in_specs), len(out_specs)

    def wrapped(*refs):
        own, parts = _carried_refs(refs, n_in, n_out, ex)
        ids = [pl.program_id(a) for a in range(len(grid))]
        pl.when(functools.reduce(jnp.logical_and, [i == 0 for i in ids]))(lambda: ex.start(*parts))
        if hasattr(ex, "relay"):
            linear = functools.reduce(lambda at, ig: at * ig[1] + ig[0], zip(ids, grid), 0)
            pl.when(linear == int(np.prod(grid)) // 2)(lambda: ex.relay(*parts))
        body(*own)
        pl.when(functools.reduce(jnp.logical_and, [i == g - 1 for i, g in zip(ids, grid)]))(lambda: ex.finish(*parts))

    return pl.pallas_call(
        wrapped, name=name, grid=grid,
        in_specs=list(in_specs) + [_ANY] * len(ex.inputs), out_specs=list(out_specs) + [_ANY] * len(ex.out_shape),
        out_shape=list(out_shape) + list(ex.out_shape), scratch_shapes=list(scratch_shapes) + list(ex.scratch),
        input_output_aliases={n_in + i: n_out + o for i, o in getattr(ex, "aliases", {}).items()},
        compiler_params=_cparams(*["arbitrary"] * len(grid)),
    )(*operands, *ex.inputs)


def fox_fwd(qt, ka, vt, carried=None):
    seq = qt.shape[2]
    nblk = seq // CHUNK
    bq = min(C_BQ, seq)
    grp = bq // CHUNK
    rows_t = CHUNK + C_TAIL

    def body(qt_ref, ka_ref, vt_ref, o_ref, lse_ref, acc_ref, s_ref):
        p, i = pl.program_id(0), pl.program_id(1)
        qtile = qt_ref[0]
        r16 = lax.broadcasted_iota(jnp.int32, (C_TAIL, bq), 0)

        def scores(t):
            at = pl.multiple_of(t * grp * 2 * CHUNK, 2 * CHUNK)
            return _dot(ka_ref[0, pl.ds(at, grp * 2 * CHUNK), :], qtile)

        def group(t, m, masked):
            ma, mb = m
            at = pl.multiple_of(t * grp * 2 * CHUNK, 2 * CHUNK)
            s = s_ref[...]
            if masked:
                s = jnp.where(_visible(s.shape, t * bq, i * bq), s, -jnp.inf)
            sa = [s[g * 2 * CHUNK:g * 2 * CHUNK + CHUNK] for g in range(grp)]
            sb = [s[g * 2 * CHUNK + CHUNK:(g + 1) * 2 * CHUNK] for g in range(grp)]
            na, nb = ma, mb
            for g in range(grp):
                na = jnp.maximum(na, jnp.max(sa[g], axis=0, keepdims=True))
                nb = jnp.maximum(nb, jnp.max(sb[g], axis=0, keepdims=True))
            al_a, al_b = jnp.exp(ma - na), jnp.exp(mb - nb)
            pt = jnp.concatenate([jnp.exp(x - n) for g in range(grp) for x, n in ((sa[g], na), (sb[g], nb))], axis=0)
            pv = _dot(vt_ref[0, :, pl.ds(at, grp * 2 * CHUNK)], pt.astype(BF16))
            tail = jnp.where(r16 == 2 * p, al_a, jnp.where(r16 == 2 * p + 1, al_b, 1.0))
            acc_ref[...] = acc_ref[...] * jnp.concatenate([_rows_ab(al_a, al_b, bq), tail], axis=0) + pv
            return na, nb

        def step(t, m):
            s_next = scores(t + 1)
            m = group(t, m, False)
            s_ref[...] = s_next
            return m

        acc_ref[...] = jnp.zeros_like(acc_ref)
        s_ref[...] = scores(0)
        m = (jnp.full((1, bq), -jnp.inf, F32), jnp.full((1, bq), -jnp.inf, F32))
        m = lax.fori_loop(0, i, step, m)
        ma, mb = group(i, m, True)
        tailv = acc_ref[CHUNK:rows_t, :]
        la = jnp.sum(jnp.where(r16 == 2 * p, tailv, 0.0), axis=0, keepdims=True)
        lb = jnp.sum(jnp.where(r16 == 2 * p + 1, tailv, 0.0), axis=0, keepdims=True)
        o_ref[...] = (acc_ref[0:CHUNK, :] * _rows_ab(1.0 / la, 1.0 / lb, bq)).T
        lse_ref[0, 0:1, :] = ma + jnp.log(la)
        lse_ref[0, 1:2, :] = mb + jnp.log(lb)

    return _call_carrying(
        carried, body, (qt, ka, vt),
        name="fox_fwd",
        grid=(C_PAIRS, seq // bq),
        in_specs=[
            pl.BlockSpec((1, 2 * CHUNK, bq), lambda p, i: (p, 0, i)),
            pl.BlockSpec((1, 2 * seq, 2 * CHUNK), lambda p, i: (p, 0, 0)),
            pl.BlockSpec((1, rows_t, 2 * seq), lambda p, i: (p, 0, 0)),
        ],
        out_specs=[pl.BlockSpec((bq, LANES), lambda p, i: (i, p)), pl.BlockSpec((1, 2, bq), lambda p, i: (p, 0, i))],
        out_shape=[jax.ShapeDtypeStruct((seq, C_WIDTH), F32), jax.ShapeDtypeStruct((C_PAIRS, 2, seq), F32)],
        scratch_shapes=[pltpu.VMEM((rows_t, bq), F32), pltpu.VMEM((grp * 2 * CHUNK, bq), F32)],
    )


def fox_bwd_prep(dy, o, proj, dproj):
    seq = o.shape[0]
    ind = np.zeros((C_WIDTH, LANES), np.float32)
    for h in range(C_HEADS):
        ind[h * C_HDIM:(h + 1) * C_HDIM, h] = 1.0
    ind = jnp.asarray(ind, BF16)
    sel = _piece_selectors()
    sel = jnp.asarray(np.stack([sel[2 * p].T + sel[2 * p + 1].T for p in range(C_PAIRS)]), BF16)

    def body(dy_ref, o_ref, z_ref, ind_ref, sel_ref, _, do_ref, dz_ref, dot_ref):
        dy_c, o_v, z = dy_ref[...], o_ref[...], z_ref[...]
        sg = jax.nn.sigmoid(z)
        do = dy_c * (z * sg)
        do_ref[...] = do.astype(BF16)
        dz_ref[...] = (dy_c * o_v * (sg * (1.0 + z * (1.0 - sg)))).astype(BF16)
        prod = do * o_v
        hi = prod.astype(BF16)
        lo = (prod - hi.astype(F32)).astype(BF16)
        delta = _dot(hi, ind_ref[...]) + _dot(lo, ind_ref[...])
        d3 = jnp.concatenate(_split3(delta.T), axis=0)
        for p in range(C_PAIRS):
            tail = _dot(sel_ref[p], d3).astype(BF16)
            dot_ref[p] = jnp.concatenate([do[:, p * LANES:(p + 1) * LANES].T.astype(BF16), tail], axis=0)

    return pl.pallas_call(
        body,
        name="fox_bwd_prep",
        grid=(seq // CHUNK,),
        in_specs=[
            pl.BlockSpec((CHUNK, C_WIDTH), lambda i: (i, 1)),
            pl.BlockSpec((CHUNK, C_WIDTH), lambda i: (i, 0)),
            pl.BlockSpec((CHUNK, C_WIDTH), lambda i: (i, 7)),
            _full((C_WIDTH, LANES)), _full((C_PAIRS, LANES, 3 * LANES)), _ANY,
        ],
        out_specs=[
            pl.BlockSpec((CHUNK, C_WIDTH), lambda i: (i, 0)),
            pl.BlockSpec((CHUNK, C_WIDTH), lambda i: (i, 7)),
            pl.BlockSpec((C_PAIRS, 2 * CHUNK, CHUNK), lambda i: (0, 0, i)),
        ],
        out_shape=[jax.ShapeDtypeStruct((seq, C_WIDTH), BF16), jax.ShapeDtypeStruct(dproj.shape, BF16),
                   jax.ShapeDtypeStruct((C_PAIRS, 2 * CHUNK, seq), BF16)],
        input_output_aliases={5: 1},
        compiler_params=_cparams("parallel"),
    )(dy, o, proj, ind, sel, dproj)


def fox_bwd(ka, va, kt, qt, dot_t, qa, dob, lse, carried=None):
    seq = qt.shape[2]
    nblk = seq // CHUNK
    bq = min(C_BQ, seq)
    nq = seq // bq
    kg = min(C_KG, nblk)
    ng = nblk // kg
    rows_t = CHUNK + C_TAIL

    def body(ka_ref, va_ref, kt_ref, qt_ref, dot_ref, qa_ref, do_ref, lse_ref,
             dq_ref, dk_ref, dv_ref, dck_ref, dcq_ref, dqt_acc, dv_acc, dka_acc):
        p, jg = pl.program_id(0), pl.program_id(1)

        @pl.when(jg == 0)
        def _():
            dqt_acc[...] = jnp.zeros_like(dqt_acc)

        dv_acc[...] = jnp.zeros_like(dv_acc)
        dka_acc[...] = jnp.zeros_like(dka_acc)

        def step(i, carry, masked):
            cols = pl.ds(pl.multiple_of(i * bq, bq), bq)
            qtile, dotile = qt_ref[0, :, cols], dot_ref[0, :, cols]
            do, qa_i = do_ref[cols, :], qa_ref[0, cols, :]
            lse2 = jnp.concatenate([jnp.broadcast_to(lse_ref[0, 0:1, cols], (CHUNK, bq)),
                                    jnp.broadcast_to(lse_ref[0, 1:2, cols], (CHUNK, bq))] * kg, axis=0)
            pt = jnp.exp(_dot(ka_ref[0], qtile) - lse2)
            if masked:
                pt = jnp.where(_visible(pt.shape, jg * kg * CHUNK, i * bq), pt, 0.0)
            ds = pt * _dot(va_ref[0], dotile)
            ptb, dsb = pt.astype(BF16), ds.astype(BF16)
            dv_acc[...] += _dot(ptb, do)
            dka_acc[...] += _dot(dsb, qa_i)
            dqt_acc[:, cols] += _dot(kt_ref[0], dsb)
            return carry

        i0 = (jg * kg * CHUNK) // bq
        step(i0, 0, True)
        lax.fori_loop(i0 + 1, nq, functools.partial(step, masked=False), 0)
        lane = lax.broadcasted_iota(jnp.int32, (CHUNK, LANES), 1)
        for kb in range(kg):
            rows = slice(kb * CHUNK, (kb + 1) * CHUNK)
            ra = slice(kb * 2 * CHUNK, kb * 2 * CHUNK + CHUNK)
            rb = slice(kb * 2 * CHUNK + CHUNK, (kb + 1) * 2 * CHUNK)
            dk_ref[rows, :] = jnp.where(lane < C_HDIM, dka_acc[ra, 0:LANES], dka_acc[rb, 0:LANES]).astype(BF16)
            dv_ref[rows, :] = jnp.where(lane < C_HDIM, dv_acc[ra, :], dv_acc[rb, :]).astype(BF16)
            dck_ref[0, rows, :] = (jnp.where(lane == 2 * p, dka_acc[ra, LANES:], 0.0)
                                   + jnp.where(lane == 2 * p + 1, dka_acc[rb, LANES:], 0.0))

        @pl.when(jg == ng - 1)
        def _():
            for c in range(nq):
                dq_ref[c * bq:(c + 1) * bq, :] = (dqt_acc[0:CHUNK, c * bq:(c + 1) * bq].T * (C_HDIM ** -0.5)).astype(BF16)
            dcq_ref[0] = dqt_acc[CHUNK:rows_t, :]

    per_pair = lambda r, c: pl.BlockSpec((1, r, c), lambda p, j: (p, 0, 0))
    by_rows = pl.BlockSpec((1, kg * 2 * CHUNK, 2 * CHUNK), lambda p, j: (p, j, 0))
    by_cols = pl.BlockSpec((1, rows_t, kg * 2 * CHUNK), lambda p, j: (p, 0, j))
    return _call_carrying(
        carried, body, (ka, va, kt, qt, dot_t, qa, dob, lse),
        name="fox_bwd",
        grid=(C_PAIRS, ng),
        in_specs=[by_rows, by_rows, by_cols, per_pair(2 * CHUNK, seq), per_pair(2 * CHUNK, seq),
                  per_pair(seq, 2 * CHUNK), pl.BlockSpec((seq, LANES), lambda p, j: (0, p)), per_pair(2, seq)],
        out_specs=[pl.BlockSpec((seq, LANES), lambda p, j: (0, p)),
                   pl.BlockSpec((kg * CHUNK, LANES), lambda p, j: (j, p)),
                   pl.BlockSpec((kg * CHUNK, LANES), lambda p, j: (j, p)),
                   pl.BlockSpec((1, kg * CHUNK, LANES), lambda p, j: (p, j, 0)),
                   per_pair(C_TAIL, seq)],
        out_shape=[jax.ShapeDtypeStruct((seq, C_WIDTH), BF16)] * 3
        + [jax.ShapeDtypeStruct((C_PAIRS, seq, LANES), F32), jax.ShapeDtypeStruct((C_PAIRS, C_TAIL, seq), F32)],
        scratch_shapes=[pltpu.VMEM((rows_t, seq), F32), pltpu.VMEM((kg * 2 * CHUNK, LANES), F32),
                        pltpu.VMEM((kg * 2 * CHUNK, 2 * CHUNK), F32)],
    )


def fox_post(dcq, dck, proj, bf_row, dproj):
    seq = proj.shape[0]
    nc = seq // CHUNK
    triu = jnp.asarray(np.triu(np.ones((CHUNK, CHUNK), np.float32)), BF16)

    def body(dq_ref, dk_ref, fl_ref, bf_ref, u_ref, _, dfl_ref, dbf_ref, carry_ref):
        @pl.when(pl.program_id(0) == 0)
        def _():
            carry_ref[...] = jnp.zeros_like(carry_ref)
            dbf_ref[...] = jnp.zeros_like(dbf_ref)

        rows = (dq_ref[0] + dq_ref[1]) + (dq_ref[2] + dq_ref[3])
        dc = jnp.concatenate([rows, jnp.zeros((CHUNK - C_TAIL, CHUNK), F32)], axis=0).T
        dc = dc - ((dk_ref[0] + dk_ref[1]) + (dk_ref[2] + dk_ref[3]))
        g = _exact_times(u_ref[...], dc, 3) + carry_ref[...]
        carry_ref[...] += jnp.sum(dc, axis=0, keepdims=True)
        dfl = g * jax.nn.sigmoid(-(fl_ref[:, :LANES] + bf_ref[...]))
        dbf_ref[...] += jnp.sum(dfl, axis=0, keepdims=True)
        dfl_ref[...] = jnp.concatenate([dfl, jnp.zeros_like(dfl)], axis=1).astype(BF16)

    rev = lambda n: nc - 1 - n
    return pl.pallas_call(
        body,
        name="fox_post",
        grid=(nc,),
        in_specs=[
            pl.BlockSpec((C_PAIRS, C_TAIL, CHUNK), lambda n: (0, 0, rev(n))),
            pl.BlockSpec((C_PAIRS, CHUNK, LANES), lambda n: (0, rev(n), 0)),
            pl.BlockSpec((CHUNK, 256), lambda n: (rev(n), 3)),
            _full((1, LANES)), _full((CHUNK, CHUNK)), _ANY,
        ],
        out_specs=[pl.BlockSpec((CHUNK, 256), lambda n: (rev(n), 3)), _full((1, LANES))],
        out_shape=[jax.ShapeDtypeStruct(dproj.shape, BF16), jax.ShapeDtypeStruct((1, LANES), F32)],
        input_output_aliases={5: 0},
        scratch_shapes=[pltpu.VMEM((1, LANES), F32)],
        compiler_params=_cparams("arbitrary"),
    )(dcq, dck, proj, bf_row, triu, dproj)


N_DEV = 8
MESH = pl.DeviceIdType.MESH
_ANY = pl.BlockSpec(memory_space=pl.ANY)


def _mesh_pos():
    return lax.axis_index("x"), lax.axis_index("y"), lax.axis_index("c")


def _dev_index(px, py, pc):
    return 4 * px + 2 * py + pc


def _row_pieces(ref, rows):
    return [ref.at[idx + (pl.ds(r, rows),)] for idx in np.ndindex(*ref.shape[:-2]) for r in range(0, ref.shape[-2], rows)]


class _Transfer:
    def __init__(self, src, dst, rows, send_sem, recv_sem, to):
        self.src, self.dst, self.rows, self.sems, self.to = src, dst, rows, (send_sem, recv_sem), to

    def _copy(self, src, dst):
        return pltpu.make_async_remote_copy(src_ref=src, dst_ref=dst, send_sem=self.sems[0], recv_sem=self.sems[1],
                                            device_id=self.to, device_id_type=MESH)

    def start(self):
        for s, d in zip(_row_pieces(self.src, self.rows), _row_pieces(self.dst, self.rows), strict=True):
            self._copy(s, d).start()

    def wait_send(self):
        self._copy(self.src, self.dst).wait_send()

    def wait_recv(self):
        self._copy(self.src, self.dst).wait_recv()


def _exchange_call(ex, name):
    n_in, n_out = len(ex.inputs), len(ex.out_shape)

    def body(*refs):
        parts = refs[:n_in], refs[n_in:n_in + n_out], refs[n_in + n_out:]
        ex.start(*parts)
        getattr(ex, "relay", lambda *_: None)(*parts)
        ex.finish(*parts)

    return pl.pallas_call(body, name=name, in_specs=[_ANY] * n_in, out_specs=[_ANY] * n_out, out_shape=ex.out_shape,
                          scratch_shapes=ex.scratch, input_output_aliases=getattr(ex, "aliases", {}))(*ex.inputs)


def _carried_refs(refs, n_in, n_out, ex):
    k_in, k_out, k_sem = (len(ex.inputs), len(ex.out_shape), len(ex.scratch)) if ex else (0, 0, 0)
    a, b, c = n_in + k_in, n_in + k_in + n_out, n_in + k_in + n_out + k_out
    own = refs[:n_in] + refs[a:b] + refs[c:len(refs) - k_sem]
    return own, (refs[n_in:a], refs[b:c], refs[len(refs) - k_sem:])


class AllGatherWeights:
    def __init__(self, blocks):
        n = len(blocks)
        self.inputs = tuple(blocks)
        self.out_shape = [jax.ShapeDtypeStruct((N_DEV,) + b.shape, b.dtype) for b in blocks]
        self.scratch = ([pltpu.SemaphoreType.DMA((n, 7)), pltpu.SemaphoreType.DMA((n, 7)), pltpu.SemaphoreType.DMA((n, 2))]
                        + [pltpu.VMEM(b.shape, b.dtype) for b in blocks])

    def _plan(self, ins, outs, scratch):
        send_sems, recv_sems, local_sems, *staged = scratch
        x, y, c = _mesh_pos()
        me, sibling = (x, y, c), (x, y, 1 - c)
        chips = [(1 - x, y), (x, 1 - y), (1 - x, 1 - y)]
        every = range(len(ins))

        def copy(a, k, block, to, own=False):
            slot = outs[a].at[_dev_index(*block)]
            return _Transfer(ins[a] if own else slot, slot, ins[a].shape[-2], send_sems.at[a, k], recv_sems.at[a, k], to)

        mine = [(pltpu.make_async_copy(ins[a], staged[a], local_sems.at[a, 0]),
                 pltpu.make_async_copy(staged[a], outs[a].at[_dev_index(*me)], local_sems.at[a, 1])) for a in every]
        first = [copy(a, 1 + j, me, (*chip, c), own=True) for j, chip in enumerate(chips) for a in every]
        first += [copy(a, 0, me, sibling, own=True) for a in every]
        passed = [[copy(a, 4 + j, (*chip, c), sibling) for a in every] for j, chip in enumerate(chips)]
        return me, sibling, chips, c, every, copy, mine, first, passed

    def start(self, ins, outs, scratch):
        *_, mine, first, _ = self._plan(ins, outs, scratch)
        for to_vmem, _ in mine:
            to_vmem.start()
        for cp in first:
            cp.start()

    def relay(self, ins, outs, scratch):
        me, sibling, chips, c, every, copy, mine, first, passed = self._plan(ins, outs, scratch)
        for to_vmem, to_slot in mine:
            to_vmem.wait()
            to_slot.start()
        for j, chip in enumerate(chips):
            for a in every:
                copy(a, 1 + j, (*chip, c), me).wait_recv()
            for cp in passed[j]:
                cp.start()

    def finish(self, ins, outs, scratch):
        me, sibling, chips, c, every, copy, mine, first, passed = self._plan(ins, outs, scratch)
        for a in every:
            copy(a, 0, sibling, me).wait_recv()
        for j, chip in enumerate(chips):
            for a in every:
                copy(a, 4 + j, (*chip, 1 - c), me).wait_recv()
        for cp in first + [cp for group in passed for cp in group]:
            cp.wait_send()
        for _, to_slot in mine:
            to_slot.wait()


N_CHIP = 4


class PairExchange:
    def __init__(self, by_core, whole=()):
        self.inputs = tuple(by_core) + tuple(whole)
        self.n_by_core = len(by_core)
        self.out_shape = ([jax.ShapeDtypeStruct(a.shape[1:], a.dtype) for a in by_core]
                          + [jax.ShapeDtypeStruct(a.shape, a.dtype) for a in whole])
        n = len(self.inputs)
        self.scratch = [pltpu.SemaphoreType.DMA((n,)), pltpu.SemaphoreType.DMA((n,))]

    def _copies(self, ins, outs, sems):
        x, y, c = _mesh_pos()
        srcs = [r.at[1 - c] if a < self.n_by_core else r for a, r in enumerate(ins)]
        return [_Transfer(srcs[a], outs[a], outs[a].shape[-2], sems[0].at[a], sems[1].at[a], (x, y, 1 - c))
                for a in range(len(ins))]

    def start(self, ins, outs, sems):
        for cp in self._copies(ins, outs, sems):
            cp.start()

    def finish(self, ins, outs, sems):
        copies = self._copies(ins, outs, sems)
        for cp in copies:
            cp.wait_recv()
        for cp in copies:
            cp.wait_send()


def pair_sum(own, other, dtype, rows, name, core, layer, depth, stacked=None):
    n, n_r, n_c = other.shape
    by_core = own.ndim == 4
    own = own if by_core else own[None]

    def body(core_ref, a_ref, b_ref, *refs):
        refs[-1][0, 0] = (a_ref[0, 0] + b_ref[0].astype(F32)).astype(dtype)

    carried = () if stacked is None else (stacked,)
    grid_spec = pltpu.PrefetchScalarGridSpec(
        num_scalar_prefetch=1,
        grid=(n, n_r // rows),
        in_specs=[pl.BlockSpec((1, 1, rows, n_c), lambda i, r, s: (s[0] if by_core else 0, i, r, 0)),
                  pl.BlockSpec((1, rows, n_c), lambda i, r, s: (i, r, 0))] + [_ANY] * len(carried),
        out_specs=pl.BlockSpec((1, 1, rows, n_c), lambda i, r, s: (i, layer, r, 0)),
    )
    return pl.pallas_call(
        body,
        name=name,
        grid_spec=grid_spec,
        out_shape=jax.ShapeDtypeStruct((n, depth, n_r, n_c), dtype),
        input_output_aliases={3: 0} if carried else {},
        compiler_params=_cparams("parallel", "parallel"),
    )(core, own, other, *carried)


def small_sum(a, b, name):
    def body(a_ref, b_ref, o_ref):
        o_ref[...] = a_ref[...] + b_ref[...]

    return pl.pallas_call(body, name=name, out_shape=jax.ShapeDtypeStruct(a.shape, a.dtype))(a, b)


class ChipExchange:
    def __init__(self, by_chip=(), layers=(), gathered=(), stacked=()):
        stacked = tuple(stacked) or (None,) * len(by_chip)
        kept = [s for s in stacked if s is not None]
        self.inputs = tuple(by_chip) + tuple(gathered) + tuple(kept)
        self.n_by_chip, self.n_gathered = len(by_chip), len(gathered)
        self.items = [(a, l) for a in range(len(by_chip)) for l in layers[a]] + [(self.n_by_chip + g, None) for g in range(len(gathered))]
        self.out_shape = ([jax.ShapeDtypeStruct((N_CHIP - 1,) + a.shape[1:], a.dtype) for a in by_chip]
                          + [jax.ShapeDtypeStruct((N_CHIP,) + a.shape, a.dtype) for a in gathered])
        at = iter(range(self.n_by_chip + self.n_gathered, len(self.inputs)))
        self.aliases = {next(at): a for a, s in enumerate(stacked) if s is not None}
        n = len(self.items)
        self.scratch = [pltpu.SemaphoreType.DMA((n, 3)), pltpu.SemaphoreType.DMA((n, 3)),
                        pltpu.SemaphoreType.DMA((max(self.n_gathered, 1),))]

    def _plan(self, ins, outs, sems):
        x, y, c = _mesh_pos()
        chip = 2 * x + y
        n = len(self.items)

        def copy(i, k, sending):
            a, layer = self.items[i]
            px, py = x ^ ((k >> 1) & 1), y ^ (k & 1)
            if layer is not None:
                src, dst = ins[a].at[2 * px + py, layer], outs[a].at[k - 1, layer]
            else:
                src, dst = ins[a], outs[a].at[chip if sending else 2 * px + py]
            return _Transfer(src, dst, dst.shape[-2], sems[0].at[i, k - 1], sems[1].at[i, k - 1], (px, py, c))

        local = [pltpu.make_async_copy(ins[a], outs[a].at[chip], sems[2].at[a - self.n_by_chip])
                 for a in range(self.n_by_chip, self.n_by_chip + self.n_gathered)]
        return n, copy, local

    def start(self, ins, outs, sems):
        n, copy, local = self._plan(ins, outs, sems)
        for cp in local:
            cp.start()
        for k in range(1, N_CHIP):
            for a in range(n):
                copy(a, k, True).start()

    def finish(self, ins, outs, sems):
        n, copy, local = self._plan(ins, outs, sems)
        for k in range(1, N_CHIP):
            for a in range(n):
                copy(a, k, False).wait_recv()
        for k in range(1, N_CHIP):
            for a in range(n):
                copy(a, k, True).wait_send()
        for cp in local:
            cp.wait()


ADAM_LR = 0.001
ADAM_B1 = 0.9
ADAM_B2 = 0.999
ADAM_EPS = 1e-08
ADAM_WD = 0.01
ADAM_STEP = 10


def adam_reduce(parts, w, m, v, rows, name, own=None, chip=None):
    n_l, n_r, n_c = w.shape
    n_parts = parts.shape[0]

    def body(*refs):
        p_ref, w_ref, m_ref, v_ref, g_ref, d_ref, m2_ref, v2_ref = refs[-8:]
        g = p_ref[0, 0].astype(F32)
        if own is not None:
            g = refs[-9][...].reshape(rows, n_c).astype(F32) + g
        for d in range(1, n_parts):
            g = g + p_ref[d, 0].astype(F32)
        m2 = ADAM_B1 * m_ref[0] + (1.0 - ADAM_B1) * g
        v2 = ADAM_B2 * v_ref[0] + (1.0 - ADAM_B2) * (g * g)
        m_hat = m2 / (1.0 - ADAM_B1 ** ADAM_STEP)
        v_hat = v2 / (1.0 - ADAM_B2 ** ADAM_STEP)
        g_ref[0] = g
        d_ref[0] = -ADAM_LR * (m_hat / (jnp.sqrt(v_hat) + ADAM_EPS) + ADAM_WD * w_ref[0])
        m2_ref[0] = m2
        v2_ref[0] = v2

    blk = lambda: pl.BlockSpec((1, rows, n_c), lambda l, r, *_: (l, r, 0))
    in_specs = [pl.BlockSpec((n_parts, 1, rows, n_c), lambda l, r, *_: (0, l, r, 0)), blk(), blk(), blk()]
    args = (parts, w, m, v)
    if own is not None:
        in_specs = [pl.BlockSpec((1, 1, rows, n_c), lambda l, r, s: (s[0], l, r, 0))] + in_specs
        args = (chip, own) + args
    grid_spec = pltpu.PrefetchScalarGridSpec(
        num_scalar_prefetch=0 if own is None else 1, grid=(n_l, n_r // rows), in_specs=in_specs,
        out_specs=[blk(), blk(), blk(), blk()])
    return pl.pallas_call(
        body,
        name=name,
        grid_spec=grid_spec,
        out_shape=[jax.ShapeDtypeStruct(w.shape, F32)] * 4,
        compiler_params=_cparams("parallel", "parallel"),
    )(*args)


def adam_reduce_columns(parts, w, m, v, name, own, chip):
    n_l, n_r, n_c = w.shape
    n_parts = parts.shape[0]
    view = lambda a: jnp.transpose(a, (2, 0, 1))

    def body(_, own_ref, p_ref, w_ref, m_ref, v_ref, g_ref, d_ref, m2_ref, v2_ref):
        for l in range(n_l):
            g = own_ref[0, l].astype(F32) + p_ref[0, l].astype(F32)
            for d in range(1, n_parts):
                g = g + p_ref[d, l].astype(F32)
            g = g.T
            w_l, m_l, v_l = w_ref[:, l, :], m_ref[:, l, :], v_ref[:, l, :]
            m2 = ADAM_B1 * m_l + (1.0 - ADAM_B1) * g
            v2 = ADAM_B2 * v_l + (1.0 - ADAM_B2) * (g * g)
            m_hat = m2 / (1.0 - ADAM_B1 ** ADAM_STEP)
            v_hat = v2 / (1.0 - ADAM_B2 ** ADAM_STEP)
            g_ref[:, l, :] = g
            d_ref[:, l, :] = -ADAM_LR * (m_hat / (jnp.sqrt(v_hat) + ADAM_EPS) + ADAM_WD * w_l)
            m2_ref[:, l, :] = m2
            v2_ref[:, l, :] = v2

    blk = lambda: pl.BlockSpec((LANES, n_l, n_r), lambda c, s: (c, 0, 0))
    grid_spec = pltpu.PrefetchScalarGridSpec(
        num_scalar_prefetch=1, grid=(pl.cdiv(n_c, LANES),),
        in_specs=[pl.BlockSpec((1, n_l, n_r, LANES), lambda c, s: (s[0], 0, 0, c)),
                  pl.BlockSpec((n_parts, n_l, n_r, LANES), lambda c, s: (0, 0, 0, c)), blk(), blk(), blk()],
        out_specs=[blk(), blk(), blk(), blk()])
    outs = pl.pallas_call(
        body,
        name=name,
        grid_spec=grid_spec,
        out_shape=[jax.ShapeDtypeStruct((n_c, n_l, n_r), F32)] * 4,
        compiler_params=_cparams("parallel"),
    )(chip, own, parts, view(w), view(m), view(v))
    return [jnp.transpose(o, (1, 2, 0)) for o in outs]


_SMALL = (("norm_g", (2, 1024)), ("gmlp_ln_g", (2, 4, 64)), ("gmlp_ln_b", (2, 4, 64)),
          ("gmlp_b_s", (2, 4, 128)), ("hgrn_lb", (2, 256)), ("hgrn_onorm_g", (2, 64)), ("fox_b_f", (2, 8)),
          ("final_norm_g", (1024,)), ("loss", ()))


def _padded(n):
    return -(-n // LANES) * LANES


_SMALL_ROWS = -(-sum(_padded(int(np.prod(s))) for _, s in _SMALL) // LANES // 8) * 8


def _pack_small(vals):
    flat = []
    for (name, shape), a in zip(_SMALL, vals, strict=True):
        n = int(np.prod(shape))
        flat.append(jnp.pad(a.reshape(n).astype(F32), (0, _padded(n) - n)))
    flat = jnp.concatenate(flat)
    return jnp.pad(flat, (0, _SMALL_ROWS * LANES - flat.shape[0])).reshape(_SMALL_ROWS, LANES)


def _unpack_small(slab):
    flat, out, at = slab.reshape(-1), {}, 0
    for name, shape in _SMALL:
        n = int(np.prod(shape))
        out[name] = flat[at:at + n].reshape(shape)
        at += _padded(n)
    return out


def sum_parts(parts, name):
    def body(p_ref, o_ref):
        g = p_ref[0]
        for d in range(1, parts.shape[0]):
            g = g + p_ref[d]
        o_ref[...] = g

    return pl.pallas_call(body, name=name, out_shape=jax.ShapeDtypeStruct(parts.shape[1:], F32))(parts)


def adam_small(gs, ws, ms, vs):
    n = len(gs)

    def body(*refs):
        for k in range(n):
            g, w, m, v = (refs[j * n + k][...] for j in range(4))
            m2 = ADAM_B1 * m + (1.0 - ADAM_B1) * g
            v2 = ADAM_B2 * v + (1.0 - ADAM_B2) * (g * g)
            m_hat = m2 / (1.0 - ADAM_B1 ** ADAM_STEP)
            v_hat = v2 / (1.0 - ADAM_B2 ** ADAM_STEP)
            refs[4 * n + k][...] = -ADAM_LR * (m_hat / (jnp.sqrt(v_hat) + ADAM_EPS) + ADAM_WD * w)
            refs[5 * n + k][...] = m2
            refs[6 * n + k][...] = v2

    outs = pl.pallas_call(body, name="adam_small",
                          out_shape=[jax.ShapeDtypeStruct(w.shape, F32) for _ in range(3) for w in ws])(*gs, *ws, *ms, *vs)
    return outs[:n], outs[n:2 * n], outs[2 * n:]


def kernel(x, norm_g, w_in, w_out, gmlp_ln_g, gmlp_ln_b, gmlp_w_s, gmlp_b_s, hgrn_lb, hgrn_onorm_g, fox_b_f, final_norm_g, loss_target, m_norm_g, m_w_in, m_w_out, m_gmlp_ln_g, m_gmlp_ln_b, m_gmlp_w_s, m_gmlp_b_s, m_hgrn_lb, m_hgrn_onorm_g, m_fox_b_f, m_final_norm_g, v_norm_g, v_w_in, v_w_out, v_gmlp_ln_g, v_gmlp_ln_b, v_gmlp_w_s, v_gmlp_b_s, v_hgrn_lb, v_hgrn_onorm_g, v_fox_b_f, v_final_norm_g):
    depth = w_in.shape[0]
    seq = x.shape[1]
    assert w_in.shape[2] * N_DEV == N_IN
    xs, tgt = x[0], loss_target[0]

    wi_blk, wo_blk = w_in.astype(BF16), w_out.astype(BF16)
    (wi_all,) = _exchange_call(AllGatherWeights([wi_blk[0]]), "allgather_weights_0")

    ln_g = gmlp_ln_g.reshape(depth, 1, A_WIDTH)
    ln_b = gmlp_ln_b.reshape(depth, 1, A_WIDTH)
    bs_t = jnp.pad(jnp.transpose(gmlp_b_s, (0, 2, 1)), ((0, 0), (0, 0), (0, LANES - A_GROUPS)))
    lb0, lb1 = hgrn_lb[0:1], hgrn_lb[1:2]
    onorm = jnp.tile(hgrn_onorm_g, (1, B_HEADS)).reshape(depth, 1, B_WIDTH)
    bf_row = jnp.pad(fox_b_f, ((0, 0), (0, LANES - C_HEADS))).reshape(depth, 1, LANES)

    core = lax.axis_index("c").astype(jnp.int32).reshape(1)
    chip = (2 * lax.axis_index("x") + lax.axis_index("y")).astype(jnp.int32).reshape(1)

    saved = []
    xc = xs
    for l in range(depth):
        wi_int = assemble_w_in(wi_all[:, None])
        proj, h = inproj(xc, norm_g[l:l + 1], wi_int, 0)
        ya = gmlp_fwd(proj, ln_g[l], ln_b[l], gmlp_w_s[l], bs_t[l])
        yb, states = hgrn_fwd(proj, lb0, lb1, onorm[l], l)
        ka, va, vt, kt, qt, qa = fox_prep(proj, bf_row[l])
        ride = ([wo_blk] if l == 0 else []) + ([wi_blk[l + 1]] if l + 1 < depth else [])
        o, lse, *gathered = fox_fwd(qt, ka, vt, AllGatherWeights(ride) if ride else None)
        if l == 0:
            wo_all = gathered.pop(0)
        if gathered:
            (wi_all,) = gathered
        x_in = xc
        if l + 1 < depth:
            xc, yfull = outproj(x_in, ya, yb, o, proj, wo_all, l)
        else:
            dx, yfull, d_final_g, loss_tile = outproj(x_in, ya, yb, o, proj, wo_all, l, (final_norm_g[None], tgt))
        saved.append((x_in, proj, h, states, ka, va, kt, qt, qa, o, lse, yfull, wi_int))

    n_shard = w_in.shape[2]
    g_norm = [None] * depth
    g_ln_g, g_ln_b, g_ws, g_bs, g_on, g_bf = ([None] * depth for _ in range(6))
    g_lb0, g_lb1 = jnp.zeros_like(lb0), jnp.zeros_like(lb1)
    swi = swo = rwi = rwo = None
    for l in reversed(range(depth)):
        x_in, proj, h, states, ka, va, kt, qt, qa, o, lse, yfull, wi_int = saved[l]
        dy, gwo = outproj_bwd(dx, yfull, wo_all, l)
        dproj, g_ln_g[l], g_ln_b[l], g_ws[l], dbs_t = gmlp_bwd(proj, dy, ln_g[l], ln_b[l], gmlp_w_s[l], bs_t[l])
        g_bs[l] = dbs_t[:, :A_GROUPS].T
        if l > 0:
            (qwo,) = _exchange_call(PairExchange([gwo]), f"pair_exchange_w_out_{l}")
        else:
            gws = jnp.stack(g_ws).reshape(-1, LANES)
            qwo, qws = _exchange_call(PairExchange([gwo], [gws]), f"pair_exchange_w_out_{l}")
            sws = small_sum(gws, qws, "pair_sum_w_s")
        swo = pair_sum(gwo, qwo, BF16, gwo.shape[2], "pair_sum_w_out", core, l, depth, swo)
        dproj, d0, d1, don = hgrn_bwd(proj, states, dy, lb0, lb1, onorm[l], l, dproj)
        g_lb0, g_lb1 = g_lb0 + d0, g_lb1 + d1
        g_on[l] = don.reshape(B_HEADS, B_KDIM).sum(0)
        dob, dproj, dot_t = fox_bwd_prep(dy, o, proj, dproj)
        top = l == depth - 1
        ride = ChipExchange([swo] if top else [swi, swo], [(l,)] if top else [(l + 1,), (l,)],
                            [sws] if l == 0 else [], [rwo] if top else [rwi, rwo])
        outs = fox_bwd(ka, va, kt, qt, dot_t, qa, dob, lse, ride)
        dqkv, (dck, dcq), got = outs[:3], outs[3:5], list(outs[5:])
        if not top:
            rwi = got.pop(0)
        rwo = got.pop(0)
        if l == 0:
            (rws,) = got
        dproj, dbf = fox_post(dcq, dck, proj, bf_row[l], dproj)
        g_bf[l] = dbf[0, :C_HEADS]
        gwi, for_sibling = split_w_in_grad(inproj_bwd_w(h, dproj, dqkv), n_shard, core)
        (qwi,) = _exchange_call(PairExchange([], [for_sibling]), f"pair_exchange_w_in_{l}")
        swi = pair_sum(gwi, qwi, BF16, 256, "pair_sum_w_in", core, l, depth, swi)
        ride = ChipExchange([swi], [(l,)], stacked=[rwi]) if l == 0 else None
        outs = inproj_bwd_x(dproj, dqkv, wi_int, x_in, norm_g[l:l + 1], dx, 0, ride)
        dx, g_norm[l] = outs[:2]
        if ride is not None:
            (rwi,) = outs[2:]

    gsm = _pack_small([
        jnp.concatenate(g_norm), jnp.stack(g_ln_g), jnp.stack(g_ln_b), jnp.stack(g_bs),
        jnp.concatenate([g_lb0, g_lb1]), jnp.stack(g_on), jnp.stack(g_bf), d_final_g, loss_tile[0, 0]])
    (qsm,) = _exchange_call(PairExchange([], [gsm]), "pair_exchange_small")
    ssm = small_sum(gsm, qsm, "pair_sum_small")
    (rsm,) = _exchange_call(ChipExchange(gathered=[ssm]), "chip_exchange_small")

    small_w = (norm_g, gmlp_ln_g, gmlp_ln_b, gmlp_b_s, hgrn_lb, hgrn_onorm_g, fox_b_f, final_norm_g)
    small_m = (m_norm_g, m_gmlp_ln_g, m_gmlp_ln_b, m_gmlp_b_s, m_hgrn_lb, m_hgrn_onorm_g, m_fox_b_f, m_final_norm_g)
    small_v = (v_norm_g, v_gmlp_ln_g, v_gmlp_ln_b, v_gmlp_b_s, v_hgrn_lb, v_hgrn_onorm_g, v_fox_b_f, v_final_norm_g)
    res_wi = adam_reduce_columns(rwi, w_in, m_w_in, v_w_in, "adam_w_in", swi, chip)
    res_wo = adam_reduce(rwo, w_out, m_w_out, v_w_out, w_out.shape[1], "adam_w_out", own=swo, chip=chip)
    grads = _unpack_small(sum_parts(rsm, "sum_small"))
    names = [name for name, _ in _SMALL if name != "loss"]
    rows = lambda a: a.reshape(1, -1) if a.ndim == 1 else a
    res_sm = adam_small([rows(grads[k]) for k in names], *([rows(a) for a in wmv] for wmv in (small_w, small_m, small_v)))
    res_sm = [grads] + [{k: a.reshape(grads[k].shape) for k, a in zip(names, r, strict=True)} for r in res_sm]
    as_rows = lambda a: a.reshape(1, -1, LANES)
    res_ws = adam_reduce(rws[:, None], as_rows(gmlp_w_s), as_rows(m_gmlp_w_s), as_rows(v_gmlp_w_s), rws.shape[1], "adam_w_s")
    for s, r in zip(res_sm, res_ws, strict=True):
        s["gmlp_w_s"] = r.reshape(gmlp_w_s.shape)

    def group(i):
        s = res_sm[i]
        return [s["norm_g"], res_wi[i], res_wo[i], s["gmlp_ln_g"], s["gmlp_ln_b"], s["gmlp_w_s"], s["gmlp_b_s"],
                s["hgrn_lb"], s["hgrn_onorm_g"], s["fox_b_f"], s["final_norm_g"]]

    return (res_sm[0]["loss"], dx[None], *group(0), *group(1), *group(2), *group(3))
```

```python
import functools

import jax
import jax.numpy as jnp
import numpy as np
from jax import lax
from jax.experimental import pallas as pl
from jax.experimental.pallas import tpu as pltpu

F32 = jnp.float32
BF16 = jnp.bfloat16

NORM_EPS = 1e-6
F_FLOOR = 1e-30
CHUNK = 128
LANES = 128
VMEM_LIMIT = 56 * 1024 * 1024


def _cparams(*sem):
    return pltpu.CompilerParams(dimension_semantics=sem, vmem_limit_bytes=VMEM_LIMIT)


def _dot(a, b, dims=(((1,), (0,)), ((), ())), precision=None):
    return lax.dot_general(a, b, dims, precision=precision, preferred_element_type=F32)


_NT = (((1,), (1,)), ((), ()))
_TN = (((0,), (0,)), ((), ()))


def _bf16_pieces(x, n):
    out, r = [], x
    for i in range(n):
        out.append(r.astype(BF16))
        if i + 1 < n:
            r = r - out[-1].astype(F32)
    return out


@functools.partial(jax.custom_vjp, nondiff_argnums=(2,))
def _times_exact(x, e, n):
    return functools.reduce(jnp.add, [_dot(p, e) for p in _bf16_pieces(x, n)])


def _times_exact_fwd(x, e, n):
    return _times_exact(x, e, n), e


def _times_exact_bwd(n, e, g):
    dx = functools.reduce(jnp.add, [lax.dot_general(p, e, _NT, preferred_element_type=F32) for p in _bf16_pieces(g, n)])
    return dx, jnp.zeros_like(e)


_times_exact.defvjp(_times_exact_fwd, _times_exact_bwd)


@functools.partial(jax.custom_vjp, nondiff_argnums=(2,))
def _exact_times(e, x, n):
    return functools.reduce(jnp.add, [_dot(e, p) for p in _bf16_pieces(x, n)])


def _exact_times_fwd(e, x, n):
    return _exact_times(e, x, n), e


def _exact_times_bwd(n, e, g):
    dx = functools.reduce(jnp.add, [lax.dot_general(e, p, _TN, preferred_element_type=F32) for p in _bf16_pieces(g, n)])
    return jnp.zeros_like(e), dx


_exact_times.defvjp(_exact_times_fwd, _exact_times_bwd)


def _group_mean_matrix(width, group):
    idx = np.arange(width) // group
    return jnp.asarray((idx[:, None] == idx[None, :]).astype(np.float32) / group, BF16)


def _group_ones_matrix(width, group):
    idx = np.arange(width) // group
    return jnp.asarray((idx[:, None] == idx[None, :]).astype(np.float32), BF16)


A_WIDTH = 256
A_GROUPS = 4
A_GDIM = 64


A_ROWS = 512


def _gmlp_chunk(x3, ln_g, ln_b, w_s, bs_t, mean_m, gind):
    n = x3.shape[0] // CHUNK
    u = jax.nn.gelu(x3[:, :A_WIDTH])
    v = jax.nn.gelu(x3[:, A_WIDTH:2 * A_WIDTH])
    z = x3[:, 2 * A_WIDTH:]
    mu = _times_exact(v, mean_m, 2)
    d = v - mu
    var = _times_exact(d * d, mean_m, 2)
    vn = d * lax.rsqrt(var + NORM_EPS) * ln_g + ln_b
    vnb = vn.astype(BF16)
    wide = jnp.concatenate([vnb[i * CHUNK:(i + 1) * CHUNK] for i in range(n)], axis=1)
    row = lax.broadcasted_iota(jnp.int32, (CHUNK, CHUNK), 0)
    col = lax.broadcasted_iota(jnp.int32, (CHUNK, CHUNK), 1)
    causal = row >= col
    lane_g = lax.shift_right_logical(lax.broadcasted_iota(jnp.int32, (CHUNK, n * A_WIDTH), 1), 6) & (A_GROUPS - 1)
    bias = _times_exact(bs_t, gind, 3)
    mixed = jnp.concatenate([bias] * n, axis=1)
    for g in range(A_GROUPS):
        wc = jnp.where(causal, w_s[g], 0.0).astype(BF16)
        mixed = mixed + jnp.where(lane_g == g, _dot(wc, wide), 0.0)
    mixed = jnp.concatenate([mixed[:, i * A_WIDTH:(i + 1) * A_WIDTH] for i in range(n)], axis=0)
    return u * mixed * jax.nn.silu(z)


def _gmlp_consts():
    gind = np.zeros((LANES, A_WIDTH), np.float32)
    for g in range(A_GROUPS):
        gind[g, g * A_GDIM:(g + 1) * A_GDIM] = 1.0
    return _group_mean_matrix(A_WIDTH, A_GDIM), jnp.asarray(gind, BF16)


def _full(shape):
    return pl.BlockSpec(shape, lambda *_: (0,) * len(shape))


def gmlp_fwd(proj, ln_g, ln_b, w_s, bs_t):
    seq = proj.shape[0]
    rows = min(A_ROWS, seq)
    mean_m, gind = _gmlp_consts()

    def body(x_ref, g_ref, b_ref, w_ref, bs_ref, m_ref, gi_ref, y_ref):
        y = _gmlp_chunk(x_ref[...], g_ref[...], b_ref[...], w_ref[...], bs_ref[...], m_ref[...], gi_ref[...])
        y_ref[...] = y.astype(BF16)

    return pl.pallas_call(
        body,
        name="gmlp_fwd",
        grid=(seq // rows,),
        in_specs=[
            pl.BlockSpec((rows, 3 * A_WIDTH), lambda n: (n, 0)),
            _full((1, A_WIDTH)), _full((1, A_WIDTH)), _full((A_GROUPS, CHUNK, CHUNK)), _full((CHUNK, LANES)),
            _full((A_WIDTH, A_WIDTH)), _full((LANES, A_WIDTH)),
        ],
        out_specs=pl.BlockSpec((rows, A_WIDTH), lambda n: (n, 0)),
        out_shape=jax.ShapeDtypeStruct((seq, A_WIDTH), BF16),
        compiler_params=_cparams("parallel"),
    )(proj, ln_g, ln_b, w_s, bs_t, mean_m, gind)


def gmlp_bwd(proj, dy, ln_g, ln_b, w_s, bs_t):
    seq = proj.shape[0]
    rows = min(A_ROWS, seq)
    mean_m, gind = _gmlp_consts()

    def body(x_ref, dy_ref, g_ref, b_ref, w_ref, bs_ref, m_ref, gi_ref, dx_ref, dg_ref, db_ref, dw_ref, dbs_ref):
        fn = functools.partial(_gmlp_chunk, mean_m=m_ref[...], gind=gi_ref[...])
        _, vjp = jax.vjp(fn, x_ref[...], g_ref[...], b_ref[...], w_ref[...], bs_ref[...])
        dx, dg, db, dw, dbs = vjp(dy_ref[...])
        dx_ref[...] = dx.astype(BF16)

        @pl.when(pl.program_id(0) == 0)
        def _():
            dg_ref[...] = jnp.zeros_like(dg_ref)
            db_ref[...] = jnp.zeros_like(db_ref)
            dw_ref[...] = jnp.zeros_like(dw_ref)
            dbs_ref[...] = jnp.zeros_like(dbs_ref)

        dg_ref[...] += dg
        db_ref[...] += db
        dw_ref[...] += dw
        dbs_ref[...] += dbs

    return pl.pallas_call(
        body,
        name="gmlp_bwd",
        grid=(seq // rows,),
        in_specs=[
            pl.BlockSpec((rows, 3 * A_WIDTH), lambda n: (n, 0)),
            pl.BlockSpec((rows, A_WIDTH), lambda n: (n, 0)),
            _full((1, A_WIDTH)), _full((1, A_WIDTH)), _full((A_GROUPS, CHUNK, CHUNK)), _full((CHUNK, LANES)),
            _full((A_WIDTH, A_WIDTH)), _full((LANES, A_WIDTH)),
        ],
        out_specs=[
            pl.BlockSpec((rows, 3 * A_WIDTH), lambda n: (n, 0)),
            _full((1, A_WIDTH)), _full((1, A_WIDTH)), _full((A_GROUPS, CHUNK, CHUNK)), _full((CHUNK, LANES)),
        ],
        out_shape=[
            jax.ShapeDtypeStruct((seq, D_INT), BF16),
            jax.ShapeDtypeStruct((1, A_WIDTH), F32), jax.ShapeDtypeStruct((1, A_WIDTH), F32),
            jax.ShapeDtypeStruct((A_GROUPS, CHUNK, CHUNK), F32), jax.ShapeDtypeStruct((CHUNK, LANES), F32),
        ],
        compiler_params=_cparams("arbitrary"),
    )(proj, dy, ln_g, ln_b, w_s, bs_t, mean_m, gind)


B_WIDTH = 256
B_HEADS = 4
B_KDIM = 64
B_LEVELS = (64, 32, 16, 8, 4, 2, 1)


def _hgrn_consts():
    t = np.arange(CHUNK)
    u = t[None, :]
    mats = [np.tril(np.ones((CHUNK, CHUNK), np.float32))]
    for m in B_LEVELS:
        p = (t // (2 * m)) * (2 * m) + m - 1
        right = (t % (2 * m)) >= m
        sel = np.where(right[:, None], (u > p[:, None]) & (u <= t[:, None]), (u > t[:, None]) & (u <= p[:, None]))
        mats.append(sel.astype(np.float32))
    return jnp.asarray(np.concatenate(mats, 0), BF16), _group_ones_matrix(B_WIDTH, B_KDIM)


def _hgrn_lower_bound(lb0, lb1, layer):
    mx = jnp.maximum(lb0, lb1)
    e0 = jnp.exp(lb0 - mx)
    e1 = jnp.exp(lb1 - mx)
    p0 = e0 / (e0 + e1)
    p1 = e1 / (e0 + e1)
    cs = p0 if layer == 0 else p0 + p1
    return jnp.clip(cs - p0, 0.0, 1.0 - 1e-6)


def _hgrn_chunk(x4, st, lb0, lb1, onorm, layer, tstack, ones_bd):
    q_raw, fl, v, zg = (x4[:, i * B_WIDTH:(i + 1) * B_WIDTH] for i in range(4))
    lb = _hgrn_lower_bound(lb0, lb1, layer)
    q = jax.nn.silu(q_raw) * (B_KDIM ** -0.5)
    f = lb + (1.0 - lb) * jax.nn.sigmoid(fl)
    logf = jnp.log(jnp.maximum(f, F_FLOOR))
    k = (1.0 - lb) * jax.nn.sigmoid(-fl)
    b = _exact_times(tstack[:CHUNK], logf, 3)
    dall = jnp.concatenate([b, _exact_times(tstack[CHUNK:], logf, 2)], axis=0)
    b_last = jnp.sum(logf, axis=0, keepdims=True)
    vb = v.astype(BF16)

    lane_h = lax.shift_right_logical(lax.broadcasted_iota(jnp.int32, (CHUNK, B_WIDTH), 1), 6)
    row = lax.broadcasted_iota(jnp.int32, (CHUNK, B_WIDTH), 0)
    srow = lax.broadcasted_iota(jnp.int32, (B_HEADS * CHUNK, CHUNK), 0) & (CHUNK - 1)
    scol = lax.broadcasted_iota(jnp.int32, (B_HEADS * CHUNK, CHUNK), 1)

    def heads_on_rows(a):
        return jnp.concatenate([jnp.where(lane_h == h, a, 0.0) for h in range(B_HEADS)], axis=0)

    def heads_from_rows(r):
        out = jnp.where(lane_h == 0, r[:CHUNK], 0.0)
        for h in range(1, B_HEADS):
            out = out + jnp.where(lane_h == h, r[h * CHUNK:(h + 1) * CHUNK], 0.0)
        return out

    o = lax.dot_general((q * jnp.exp(b)).astype(BF16), st.astype(BF16), _NT, preferred_element_type=F32)
    scores = jnp.zeros((B_HEADS * CHUNK, CHUNK), F32)
    for li, m in enumerate(B_LEVELS):
        e = jnp.exp(dall[(li + 1) * CHUNK:(li + 2) * CHUNK])
        right = (row & (2 * m - 1)) >= m
        qt = jnp.where(right, q * e, 0.0)
        kt = jnp.where(right, 0.0, k * e)
        sc = lax.dot_general(heads_on_rows(qt).astype(BF16), kt.astype(BF16), _NT, preferred_element_type=F32)
        sh = int(np.log2(2 * m))
        same = lax.shift_right_logical(srow, sh) == lax.shift_right_logical(scol, sh)
        scores = scores + jnp.where(same, sc, 0.0)
    o = o + heads_from_rows(_dot(scores.astype(BF16), vb))
    o = o + _times_exact(q * k, ones_bd, 2) * v

    kv = lax.dot_general(vb, (k * jnp.exp(b_last - b)).astype(BF16), _TN, preferred_element_type=F32)
    st_new = st * jnp.exp(b_last) + jnp.where(ones_bd > 0.5, kv, 0.0)

    ms = _times_exact(o * o, ones_bd, 2) * (1.0 / B_KDIM)
    y = o * lax.rsqrt(ms + NORM_EPS) * onorm * jax.nn.silu(zg)
    return y, st_new


B_ROWS = 256


def _hgrn_rows(x4, st, lb0, lb1, onorm, layer, tstack, ones_bd):
    ys = []
    for i in range(x4.shape[0] // CHUNK):
        y, st = _hgrn_chunk(x4[i * CHUNK:(i + 1) * CHUNK], st, lb0, lb1, onorm, layer, tstack, ones_bd)
        ys.append(y)
    return jnp.concatenate(ys, axis=0), st


def hgrn_fwd(proj, lb0, lb1, onorm, layer):
    seq = proj.shape[0]
    rows = min(B_ROWS, seq)
    nc = seq // rows
    tstack, ones_bd = _hgrn_consts()

    def body(x_ref, lb0_ref, lb1_ref, on_ref, t_ref, e_ref, y_ref, st_out_ref, st_ref):
        @pl.when(pl.program_id(0) == 0)
        def _():
            st_ref[...] = jnp.zeros_like(st_ref)

        st = st_ref[...]
        st_out_ref[0] = st
        y, st_new = _hgrn_rows(x_ref[...], st, lb0_ref[...], lb1_ref[...], on_ref[...], layer, t_ref[...], e_ref[...])
        y_ref[...] = y.astype(BF16)
        st_ref[...] = st_new

    return pl.pallas_call(
        body,
        name=f"hgrn_fwd_{layer}",
        grid=(nc,),
        in_specs=[
            pl.BlockSpec((rows, 4 * B_WIDTH), lambda n: (n, 1)),
            _full((1, B_WIDTH)), _full((1, B_WIDTH)), _full((1, B_WIDTH)),
            _full(((len(B_LEVELS) + 1) * CHUNK, CHUNK)), _full((B_WIDTH, B_WIDTH)),
        ],
        out_specs=[
            pl.BlockSpec((rows, B_WIDTH), lambda n: (n, 0)),
            pl.BlockSpec((1, B_WIDTH, B_WIDTH), lambda n: (n, 0, 0)),
        ],
        out_shape=[jax.ShapeDtypeStruct((seq, B_WIDTH), BF16), jax.ShapeDtypeStruct((nc, B_WIDTH, B_WIDTH), F32)],
        scratch_shapes=[pltpu.VMEM((B_WIDTH, B_WIDTH), F32)],
        compiler_params=_cparams("arbitrary"),
    )(proj, lb0, lb1, onorm, tstack, ones_bd)


def hgrn_bwd(proj, states, dy, lb0, lb1, onorm, layer, dproj):
    seq = proj.shape[0]
    rows = min(B_ROWS, seq)
    nc = seq // rows
    tstack, ones_bd = _hgrn_consts()

    def body(x_ref, st_in_ref, dy_ref, lb0_ref, lb1_ref, on_ref, t_ref, e_ref, _, dx_ref, d0_ref, d1_ref, don_ref, dst_ref):
        @pl.when(pl.program_id(0) == 0)
        def _():
            dst_ref[...] = jnp.zeros_like(dst_ref)
            d0_ref[...] = jnp.zeros_like(d0_ref)
            d1_ref[...] = jnp.zeros_like(d1_ref)
            don_ref[...] = jnp.zeros_like(don_ref)

        fn = functools.partial(_hgrn_rows, layer=layer, tstack=t_ref[...], ones_bd=e_ref[...])
        _, vjp = jax.vjp(fn, x_ref[...], st_in_ref[0], lb0_ref[...], lb1_ref[...], on_ref[...])
        dx, dst, d0, d1, don = vjp((dy_ref[...], dst_ref[...]))
        dx_ref[...] = dx.astype(BF16)
        dst_ref[...] = dst
        d0_ref[...] += d0
        d1_ref[...] += d1
        don_ref[...] += don

    rev = lambda n: nc - 1 - n
    return pl.pallas_call(
        body,
        name=f"hgrn_bwd_{layer}",
        grid=(nc,),
        in_specs=[
            pl.BlockSpec((rows, 4 * B_WIDTH), lambda n: (rev(n), 1)),
            pl.BlockSpec((1, B_WIDTH, B_WIDTH), lambda n: (rev(n), 0, 0)),
            pl.BlockSpec((rows, B_WIDTH), lambda n: (rev(n), 1)),
            _full((1, B_WIDTH)), _full((1, B_WIDTH)), _full((1, B_WIDTH)),
            _full(((len(B_LEVELS) + 1) * CHUNK, CHUNK)), _full((B_WIDTH, B_WIDTH)), _ANY,
        ],
        out_specs=[
            pl.BlockSpec((rows, 4 * B_WIDTH), lambda n: (rev(n), 1)),
            _full((1, B_WIDTH)), _full((1, B_WIDTH)), _full((1, B_WIDTH)),
        ],
        out_shape=[jax.ShapeDtypeStruct(dproj.shape, BF16)] + [jax.ShapeDtypeStruct((1, B_WIDTH), F32)] * 3,
        input_output_aliases={8: 0},
        scratch_shapes=[pltpu.VMEM((B_WIDTH, B_WIDTH), F32)],
        compiler_params=_cparams("arbitrary"),
    )(proj, states, dy, lb0, lb1, onorm, tstack, ones_bd, dproj)


D_MODEL = 1024
D_INT = 4096


def _rms_stats(xf):
    r = lax.rsqrt(jnp.mean(xf * xf, axis=-1, keepdims=True) + NORM_EPS)
    return r, xf * r


def _rms_bwd(dy, g, r, xh):
    u = dy * g
    return r * (u - xh * jnp.mean(u * xh, axis=-1, keepdims=True))


def inproj(x, g, w, layer):
    seq = x.shape[0]
    tm = min(seq, 512)

    def body(x_ref, g_ref, w_ref, p_ref, h_ref):
        _, xh = _rms_stats(x_ref[...])
        h = (xh * g_ref[...]).astype(BF16)
        h_ref[...] = h
        p_ref[...] = _dot(h, w_ref[0])

    return pl.pallas_call(
        body,
        name="inproj",
        grid=(seq // tm,),
        in_specs=[
            pl.BlockSpec((tm, D_MODEL), lambda i: (i, 0)),
            _full((1, D_MODEL)),
            pl.BlockSpec((1, D_MODEL, D_INT), lambda i: (layer, 0, 0)),
        ],
        out_specs=[pl.BlockSpec((tm, D_INT), lambda i: (i, 0)), pl.BlockSpec((tm, D_MODEL), lambda i: (i, 0))],
        out_shape=[jax.ShapeDtypeStruct((seq, D_INT), F32), jax.ShapeDtypeStruct((seq, D_MODEL), BF16)],
        compiler_params=_cparams("parallel"),
    )(x, g, w)


def outproj(x, ya, yb, o, proj, wo, layer, head=None):
    seq = x.shape[0]
    tm = min(seq, 512)
    blk = wo.shape[2]

    def body(x_ref, ya_ref, yb_ref, o_ref, z_ref, w_ref, *refs):
        yc = (o_ref[...] * jax.nn.silu(z_ref[...])).astype(BF16)
        y = jnp.concatenate([ya_ref[...], yb_ref[...], yc], axis=1)
        w = jnp.concatenate([w_ref[d, 0] for d in range(N_DEV)], axis=0)
        xn = x_ref[...] + _dot(y, w)
        if head is None:
            xn_ref, y_ref = refs
            xn_ref[...] = xn
        else:
            g_ref, t_ref, dx_ref, y_ref, dg_ref, loss_ref = refs

            @pl.when(pl.program_id(0) == 0)
            def _():
                dg_ref[...] = jnp.zeros_like(dg_ref)
                loss_ref[...] = jnp.zeros_like(loss_ref)

            g = g_ref[...]
            r, xh = _rms_stats(xn)
            err = xh * g - t_ref[...]
            sq = jnp.sum(jnp.sum(err * err, axis=1, keepdims=True), axis=0, keepdims=True)
            loss_ref[...] += jnp.broadcast_to(sq * (0.5 / D_MODEL), loss_ref.shape)
            dout = err * (1.0 / D_MODEL)
            dg_ref[...] += jnp.sum(dout * xh, axis=0, keepdims=True)
            dx_ref[...] = _rms_bwd(dout, g, r, xh)
        y_ref[...] = y

    rows = lambda: pl.BlockSpec((tm, D_MODEL), lambda i: (i, 0))
    tail = (() if head is None else (_full((1, D_MODEL)), rows()),
            () if head is None else (_full((1, D_MODEL)), _full((8, LANES))),
            () if head is None else (jax.ShapeDtypeStruct((1, D_MODEL), F32), jax.ShapeDtypeStruct((8, LANES), F32)))
    return pl.pallas_call(
        body,
        name="outproj" if head is None else "outproj_loss",
        grid=(seq // tm,),
        in_specs=[
            rows(),
            pl.BlockSpec((tm, 256), lambda i: (i, 0)),
            pl.BlockSpec((tm, 256), lambda i: (i, 0)),
            pl.BlockSpec((tm, 512), lambda i: (i, 0)),
            pl.BlockSpec((tm, 512), lambda i: (i, 7)),
            pl.BlockSpec((N_DEV, 1, blk, D_MODEL), lambda i: (0, layer, 0, 0)),
            *tail[0],
        ],
        out_specs=[rows(), rows(), *tail[1]],
        out_shape=[jax.ShapeDtypeStruct((seq, D_MODEL), F32), jax.ShapeDtypeStruct((seq, D_MODEL), BF16), *tail[2]],
        compiler_params=_cparams("parallel" if head is None else "arbitrary"),
    )(x, ya, yb, o, proj, wo, *(head or ()))


def outproj_bwd(dx, y, wo, layer):
    seq = dx.shape[0]
    ts = min(seq, 512)
    blk = wo.shape[2]

    def body(dx_ref, y_ref, w_ref, dy_ref, dw_ref):
        @pl.when(pl.program_id(0) == 0)
        def _():
            dw_ref[...] = jnp.zeros_like(dw_ref)

        dxb = dx_ref[...].astype(BF16)
        w = jnp.concatenate([w_ref[d, 0] for d in range(N_DEV)], axis=0)
        dy_ref[...] = lax.dot_general(dxb, w, _NT, preferred_element_type=F32)
        dw = lax.dot_general(y_ref[...], dxb, _TN, preferred_element_type=F32)
        for d in range(N_DEV):
            dw_ref[d % 2, d // 2] += dw[d * blk:(d + 1) * blk]

    return pl.pallas_call(
        body,
        name="outproj_bwd",
        grid=(seq // ts,),
        in_specs=[
            pl.BlockSpec((ts, D_MODEL), lambda i: (i, 0)),
            pl.BlockSpec((ts, D_MODEL), lambda i: (i, 0)),
            pl.BlockSpec((N_DEV, 1, blk, D_MODEL), lambda i: (0, layer, 0, 0)),
        ],
        out_specs=[pl.BlockSpec((ts, D_MODEL), lambda i: (i, 0)),
                   pl.BlockSpec((2, N_CHIP, blk, D_MODEL), lambda i: (0, 0, 0, 0))],
        out_shape=[jax.ShapeDtypeStruct((seq, D_MODEL), F32), jax.ShapeDtypeStruct((2, N_CHIP, blk, D_MODEL), F32)],
        compiler_params=_cparams("arbitrary"),
    )(dx, y, wo)


C_QKV = (2048, 3584)


def _dproj_parts(dp_ref, dqkv_refs, rows):
    lo, hi = C_QKV
    step = (hi - lo) // len(dqkv_refs)
    return ([(0, dp_ref.at[rows, 0:lo])] + [(lo + i * step, r.at[rows, :]) for i, r in enumerate(dqkv_refs)]
            + [(hi, dp_ref.at[rows, hi:D_INT])])


def inproj_bwd_x(dproj, dqkv, w, x, g, dx_in, layer, carried=None):
    seq = x.shape[0]
    tm = min(seq, 512)

    def body(dp_ref, dq_ref, dk_ref, dv_ref, w_ref, x_ref, g_ref, dxin_ref, dx_ref, dg_ref):
        @pl.when(pl.program_id(0) == 0)
        def _():
            dg_ref[...] = jnp.zeros_like(dg_ref)

        dh = None
        for at, part in _dproj_parts(dp_ref, (dq_ref, dk_ref, dv_ref), slice(None)):
            term = lax.dot_general(part[...], w_ref[0, :, at:at + part.shape[1]], _NT, preferred_element_type=F32)
            dh = term if dh is None else dh + term
        r, xh = _rms_stats(x_ref[...])
        dg_ref[...] += jnp.sum(dh * xh, axis=0, keepdims=True)
        dx_ref[...] = dxin_ref[...] + _rms_bwd(dh, g_ref[...], r, xh)

    third = lambda: pl.BlockSpec((tm, C_WIDTH), lambda i: (i, 0))
    return _call_carrying(
        carried, body, (dproj, *dqkv, w, x, g, dx_in),
        name="inproj_bwd_x",
        grid=(seq // tm,),
        in_specs=[
            pl.BlockSpec((tm, D_INT), lambda i: (i, 0)), third(), third(), third(),
            pl.BlockSpec((1, D_MODEL, D_INT), lambda i: (layer, 0, 0)),
            pl.BlockSpec((tm, D_MODEL), lambda i: (i, 0)),
            _full((1, D_MODEL)),
            pl.BlockSpec((tm, D_MODEL), lambda i: (i, 0)),
        ],
        out_specs=[pl.BlockSpec((tm, D_MODEL), lambda i: (i, 0)), _full((1, D_MODEL))],
        out_shape=[jax.ShapeDtypeStruct((seq, D_MODEL), F32), jax.ShapeDtypeStruct((1, D_MODEL), F32)],
        scratch_shapes=[], semantics=("arbitrary",),
    )


def inproj_bwd_w(h, dproj, dqkv):
    seq = h.shape[0]
    ts, tn = min(seq, 512), 512

    def body(h_ref, dp_ref, dq_ref, dk_ref, dv_ref, dw_ref):
        @pl.when(pl.program_id(0) == 0)
        def _():
            dw_ref[...] = jnp.zeros_like(dw_ref)

        ht = h_ref[...].T
        for at, part in _dproj_parts(dp_ref, (dq_ref, dk_ref, dv_ref), slice(None)):
            for c in range(0, part.shape[1], tn):
                dw_ref[0, :, at + c:at + c + tn] += _dot(ht, part[:, c:c + tn])

    third = lambda: pl.BlockSpec((ts, C_WIDTH), lambda s: (s, 0))
    return pl.pallas_call(
        body,
        name="inproj_bwd_w",
        grid=(seq // ts,),
        in_specs=[pl.BlockSpec((ts, D_MODEL), lambda s: (s, 0)), pl.BlockSpec((ts, D_INT), lambda s: (s, 0)),
                  third(), third(), third()],
        out_specs=_full((1, D_MODEL, D_INT)),
        out_shape=jax.ShapeDtypeStruct((1, D_MODEL, D_INT), F32),
        compiler_params=_cparams("arbitrary"),
    )(h, dproj, *dqkv)


N_IN = 3848


def _internal_of(col):
    return col if col < 768 else (col + 256 if col < 3840 else 768 + col - 3840)


def _column_runs(n_shard):
    runs = []
    for d in range(N_IN // n_shard):
        mine = []
        for j in range(n_shard):
            ci = _internal_of(d * n_shard + j)
            if mine and mine[-1][0] + mine[-1][1] == ci:
                mine[-1][1] += 1
            else:
                mine.append([ci, 1, j])
        runs.append(mine)
    return runs


def assemble_w_in(wi_all):
    n_dev, depth, _, n_shard = wi_all.shape
    tr = 256
    pieces = [[] for _ in range(D_INT // LANES)]
    for d, mine in enumerate(_column_runs(n_shard)):
        for ci, ln, off in mine:
            while ln > 0:
                blk, at = divmod(ci, LANES)
                take = min(ln, LANES - at)
                pieces[blk].append((at, take, d, off))
                ci, ln, off = ci + take, ln - take, off + take

    def body(x_ref, o_ref):
        for blk, parts in enumerate(pieces):
            vals, at = [], 0
            for start, ln, d, off in sorted(parts):
                if start > at:
                    vals.append(jnp.zeros((tr, start - at), BF16))
                vals.append(x_ref[d, 0, :, off:off + ln])
                at = start + ln
            if at < LANES:
                vals.append(jnp.zeros((tr, LANES - at), BF16))
            o_ref[0, :, blk * LANES:(blk + 1) * LANES] = vals[0] if len(vals) == 1 else jnp.concatenate(vals, axis=1)

    return pl.pallas_call(
        body,
        name="assemble_w_in",
        grid=(depth, D_MODEL // tr),
        in_specs=[pl.BlockSpec((n_dev, 1, tr, n_shard), lambda l, r: (0, l, r, 0))],
        out_specs=pl.BlockSpec((1, tr, D_INT), lambda l, r: (l, r, 0)),
        out_shape=jax.ShapeDtypeStruct((depth, D_MODEL, D_INT), BF16),
        compiler_params=_cparams("parallel", "parallel"),
    )(wi_all)


def split_w_in_grad(dwi, n_shard, core):
    tr = 256
    runs = _column_runs(n_shard)

    def body(core_ref, x_ref, keep_ref, send_ref):
        for d, mine in enumerate(runs):
            @pl.when(core_ref[0] == d % 2)
            def _():
                for ci, ln, off in mine:
                    keep_ref[d // 2, :, off:off + ln] = x_ref[0, :, ci:ci + ln]

            @pl.when(core_ref[0] != d % 2)
            def _():
                for ci, ln, off in mine:
                    send_ref[d // 2, :, off:off + ln] = x_ref[0, :, ci:ci + ln].astype(BF16)

    shards = lambda: pl.BlockSpec((N_CHIP, tr, n_shard), lambda r, s: (0, r, 0))
    grid_spec = pltpu.PrefetchScalarGridSpec(
        num_scalar_prefetch=1, grid=(D_MODEL // tr,),
        in_specs=[pl.BlockSpec((1, tr, D_INT), lambda r, s: (0, r, 0))], out_specs=[shards(), shards()])
    return pl.pallas_call(
        body,
        name="split_w_in_grad",
        grid_spec=grid_spec,
        out_shape=[jax.ShapeDtypeStruct((N_CHIP, D_MODEL, n_shard), F32), jax.ShapeDtypeStruct((N_CHIP, D_MODEL, n_shard), BF16)],
        compiler_params=_cparams("parallel"),
    )(core, dwi)


C_WIDTH = 512
C_HEADS = 8
C_HDIM = 64
C_PAIRS = C_HEADS // 2
C_BQ = 512
C_TAIL = 16
C_KG = 4


def _split3(x):
    hi = x.astype(BF16)
    r = x - hi.astype(F32)
    mid = r.astype(BF16)
    return hi, mid, (r - mid.astype(F32)).astype(BF16)


def _piece_selectors():
    sel = np.zeros((C_HEADS, 3 * LANES, LANES), np.float32)
    for p in range(C_PAIRS):
        for e in range(2):
            for t in range(3):
                sel[2 * p + e, t * LANES + 2 * p + e, 3 * e + t] = -1.0
    return sel


def fox_prep(proj, bf_row):
    seq = proj.shape[0]
    nblk = seq // CHUNK
    tril = jnp.asarray(np.tril(np.ones((CHUNK, CHUNK), np.float32)), BF16)
    sel = jnp.asarray(_piece_selectors(), BF16)
    rows_t = CHUNK + C_TAIL

    def body(fl_ref, q_ref, k_ref, v_ref, bf_ref, l_ref, sel_ref, ka_ref, va_ref, vt_ref, kt_ref, qt_ref, qa_ref, carry_ref):
        @pl.when(pl.program_id(0) == 0)
        def _():
            carry_ref[...] = jnp.zeros_like(carry_ref)

        lf = jax.nn.log_sigmoid(fl_ref[:, :LANES] + bf_ref[...])
        c = _exact_times(l_ref[...], lf, 3) + carry_ref[...]
        carry_ref[...] += jnp.sum(lf, axis=0, keepdims=True)
        c3 = jnp.concatenate(_split3(c), axis=1)
        lane = lax.broadcasted_iota(jnp.int32, (CHUNK, LANES), 1)
        row = lax.broadcasted_iota(jnp.int32, (CHUNK, LANES), 0)
        r16 = lax.broadcasted_iota(jnp.int32, (C_TAIL, 2 * CHUNK), 0)
        l16 = lax.broadcasted_iota(jnp.int32, (C_TAIL, 2 * CHUNK), 1)
        zero = jnp.zeros((CHUNK, LANES), BF16)
        one = jnp.ones((CHUNK, LANES), BF16)

        def by_keys(x, right_a, right_b):
            xb = x.astype(BF16)
            top = jnp.concatenate([jnp.where(lane < C_HDIM, xb, zero), right_a], axis=1)
            return jnp.concatenate([top, jnp.concatenate([jnp.where(lane < C_HDIM, zero, xb), right_b], axis=1)], axis=0)

        def by_lanes(x, tail):
            xt = x.T.astype(BF16)
            main = jnp.concatenate([jnp.where(row < C_HDIM, xt, zero), jnp.where(row < C_HDIM, zero, xt)], axis=1)
            return jnp.concatenate([main, tail], axis=0)

        for p in range(C_PAIRS):
            cols = slice(p * LANES, (p + 1) * LANES)
            q2, k2, v2 = q_ref[:, cols] * (C_HDIM ** -0.5), k_ref[:, cols], v_ref[:, cols]
            negc = [_dot(c3, sel_ref[2 * p + e]).astype(BF16) for e in range(2)]
            ones3 = [jnp.where((lane >= 3 * e) & (lane < 3 * e + 3), one, zero) for e in range(2)]
            tail = jnp.where(((r16 == 2 * p) & (l16 < CHUNK)) | ((r16 == 2 * p + 1) & (l16 >= CHUNK)), 1.0, 0.0).astype(BF16)
            ka_ref[p] = by_keys(k2, negc[0], negc[1])
            va_ref[p] = by_keys(v2, ones3[0], ones3[1])
            kt_ref[p] = by_lanes(k2, tail)
            vt_ref[p] = by_lanes(v2, tail)
            qt_ref[p] = jnp.concatenate([q2.T.astype(BF16), jnp.where(row < 6, one, zero)], axis=0)
            qa_ref[p] = jnp.concatenate([q2.astype(BF16), jnp.where((lane == 2 * p) | (lane == 2 * p + 1), one, zero)], axis=1)

    wide = lambda j: pl.BlockSpec((CHUNK, C_WIDTH), lambda n: (n, j))
    by_rows = pl.BlockSpec((C_PAIRS, 2 * CHUNK, 2 * CHUNK), lambda n: (0, n, 0))
    by_cols = pl.BlockSpec((C_PAIRS, rows_t, 2 * CHUNK), lambda n: (0, 0, n))
    return pl.pallas_call(
        body,
        name="fox_prep",
        grid=(nblk,),
        in_specs=[pl.BlockSpec((CHUNK, 256), lambda n: (n, 3)), wide(4), wide(5), wide(6), _full((1, LANES)),
                  _full((CHUNK, CHUNK)), _full((C_HEADS, 3 * LANES, LANES))],
        out_specs=[by_rows, by_rows, by_cols, by_cols,
                   pl.BlockSpec((C_PAIRS, 2 * CHUNK, CHUNK), lambda n: (0, 0, n)),
                   pl.BlockSpec((C_PAIRS, CHUNK, 2 * CHUNK), lambda n: (0, n, 0))],
        out_shape=[jax.ShapeDtypeStruct((C_PAIRS, 2 * seq, 2 * CHUNK), BF16)] * 2
        + [jax.ShapeDtypeStruct((C_PAIRS, rows_t, 2 * seq), BF16)] * 2
        + [jax.ShapeDtypeStruct((C_PAIRS, 2 * CHUNK, seq), BF16), jax.ShapeDtypeStruct((C_PAIRS, seq, 2 * CHUNK), BF16)],
        scratch_shapes=[pltpu.VMEM((1, LANES), F32)],
        compiler_params=_cparams("arbitrary"),
    )(proj, proj, proj, proj, bf_row, tril, sel)


def _visible(shape, key0, query0):
    row = lax.broadcasted_iota(jnp.int32, shape, 0)
    key = key0 + lax.shift_left(lax.shift_right_logical(row, 8), 7) + (row & (CHUNK - 1))
    return key <= query0 + lax.broadcasted_iota(jnp.int32, shape, 1)


def _rows_ab(a, b, n):
    return jnp.concatenate([jnp.broadcast_to(a, (C_HDIM, n)), jnp.broadcast_to(b, (C_HDIM, n))], axis=0)


def _call_carrying(ex, body, operands, *, name, grid, in_specs, out_specs, out_shape, scratch_shapes, semantics=None):
    if ex is None:
        semantics = semantics or ("parallel", *["arbitrary"] * (len(grid) - 1))
        return pl.pallas_call(body, name=name, grid=grid, in_specs=in_specs, out_specs=out_specs, out_shape=out_shape,
                              scratch_shapes=scratch_shapes, compiler_params=_cparams(*semantics))(*operands)
    n_in, n_out = len(in_specs), len(out_specs)

    def wrapped(*refs):
        own, parts = _carried_refs(refs, n_in, n_out, ex)
        ids = [pl.program_id(a) for a in range(len(grid))]
        pl.when(functools.reduce(jnp.logical_and, [i == 0 for i in ids]))(lambda: ex.start(*parts))
        if hasattr(ex, "relay"):
            linear = functools.reduce(lambda at, ig: at * ig[1] + ig[0], zip(ids, grid), 0)
            pl.when(linear == int(np.prod(grid)) // 2)(lambda: ex.relay(*parts))
        body(*own)
        pl.when(functools.reduce(jnp.logical_and, [i == g - 1 for i, g in zip(ids, grid)]))(lambda: ex.finish(*parts))

    return pl.pallas_call(
        wrapped, name=name, grid=grid,
        in_specs=list(in_specs) + [_ANY] * len(ex.inputs), out_specs=list(out_specs) + [_ANY] * len(ex.out_shape),
        out_shape=list(out_shape) + list(ex.out_shape), scratch_shapes=list(scratch_shapes) + list(ex.scratch),
        input_output_aliases={n_in + i: n_out + o for i, o in getattr(ex, "aliases", {}).items()},
        compiler_params=_cparams(*["arbitrary"] * len(grid)),
    )(*operands, *ex.inputs)


def fox_fwd(qt, ka, vt, carried=None):
    seq = qt.shape[2]
    nblk = seq // CHUNK
    bq = min(C_BQ, seq)
    grp = bq // CHUNK
    rows_t = CHUNK + C_TAIL

    def body(qt_ref, ka_ref, vt_ref, o_ref, lse_ref, acc_ref, s_ref):
        p, i = pl.program_id(0), pl.program_id(1)
        qtile = qt_ref[0]
        r16 = lax.broadcasted_iota(jnp.int32, (C_TAIL, bq), 0)

        def scores(t):
            at = pl.multiple_of(t * grp * 2 * CHUNK, 2 * CHUNK)
            return _dot(ka_ref[0, pl.ds(at, grp * 2 * CHUNK), :], qtile)

        def group(t, m, masked):
            ma, mb = m
            at = pl.multiple_of(t * grp * 2 * CHUNK, 2 * CHUNK)
            s = s_ref[...]
            if masked:
                s = jnp.where(_visible(s.shape, t * bq, i * bq), s, -jnp.inf)
            sa = [s[g * 2 * CHUNK:g * 2 * CHUNK + CHUNK] for g in range(grp)]
            sb = [s[g * 2 * CHUNK + CHUNK:(g + 1) * 2 * CHUNK] for g in range(grp)]
            na, nb = ma, mb
            for g in range(grp):
                na = jnp.maximum(na, jnp.max(sa[g], axis=0, keepdims=True))
                nb = jnp.maximum(nb, jnp.max(sb[g], axis=0, keepdims=True))
            al_a, al_b = jnp.exp(ma - na), jnp.exp(mb - nb)
            pt = jnp.concatenate([jnp.exp(x - n) for g in range(grp) for x, n in ((sa[g], na), (sb[g], nb))], axis=0)
            pv = _dot(vt_ref[0, :, pl.ds(at, grp * 2 * CHUNK)], pt.astype(BF16))
            tail = jnp.where(r16 == 2 * p, al_a, jnp.where(r16 == 2 * p + 1, al_b, 1.0))
            acc_ref[...] = acc_ref[...] * jnp.concatenate([_rows_ab(al_a, al_b, bq), tail], axis=0) + pv
            return na, nb

        def step(t, m):
            s_next = scores(t + 1)
            m = group(t, m, False)
            s_ref[...] = s_next
            return m

        acc_ref[...] = jnp.zeros_like(acc_ref)
        s_ref[...] = scores(0)
        m = (jnp.full((1, bq), -jnp.inf, F32), jnp.full((1, bq), -jnp.inf, F32))
        m = lax.fori_loop(0, i, step, m)
        ma, mb = group(i, m, True)
        tailv = acc_ref[CHUNK:rows_t, :]
        la = jnp.sum(jnp.where(r16 == 2 * p, tailv, 0.0), axis=0, keepdims=True)
        lb = jnp.sum(jnp.where(r16 == 2 * p + 1, tailv, 0.0), axis=0, keepdims=True)
        o_ref[...] = (acc_ref[0:CHUNK, :] * _rows_ab(1.0 / la, 1.0 / lb, bq)).T
        lse_ref[0, 0:1, :] = ma + jnp.log(la)
        lse_ref[0, 1:2, :] = mb + jnp.log(lb)

    return _call_carrying(
        carried, body, (qt, ka, vt),
        name="fox_fwd",
        grid=(C_PAIRS, seq // bq),
        in_specs=[
            pl.BlockSpec((1, 2 * CHUNK, bq), lambda p, i: (p, 0, i)),
            pl.BlockSpec((1, 2 * seq, 2 * CHUNK), lambda p, i: (p, 0, 0)),
            pl.BlockSpec((1, rows_t, 2 * seq), lambda p, i: (p, 0, 0)),
        ],
        out_specs=[pl.BlockSpec((bq, LANES), lambda p, i: (i, p)), pl.BlockSpec((1, 2, bq), lambda p, i: (p, 0, i))],
        out_shape=[jax.ShapeDtypeStruct((seq, C_WIDTH), F32), jax.ShapeDtypeStruct((C_PAIRS, 2, seq), F32)],
        scratch_shapes=[pltpu.VMEM((rows_t, bq), F32), pltpu.VMEM((grp * 2 * CHUNK, bq), F32)],
    )


def fox_bwd_prep(dy, o, proj, dproj):
    seq = o.shape[0]
    ind = np.zeros((C_WIDTH, LANES), np.float32)
    for h in range(C_HEADS):
        ind[h * C_HDIM:(h + 1) * C_HDIM, h] = 1.0
    ind = jnp.asarray(ind, BF16)
    sel = _piece_selectors()
    sel = jnp.asarray(np.stack([sel[2 * p].T + sel[2 * p + 1].T for p in range(C_PAIRS)]), BF16)

    def body(dy_ref, o_ref, z_ref, ind_ref, sel_ref, _, do_ref, dz_ref, dot_ref):
        dy_c, o_v, z = dy_ref[...], o_ref[...], z_ref[...]
        sg = jax.nn.sigmoid(z)
        do = dy_c * (z * sg)
        do_ref[...] = do.astype(BF16)
        dz_ref[...] = (dy_c * o_v * (sg * (1.0 + z * (1.0 - sg)))).astype(BF16)
        prod = do * o_v
        hi = prod.astype(BF16)
        lo = (prod - hi.astype(F32)).astype(BF16)
        delta = _dot(hi, ind_ref[...]) + _dot(lo, ind_ref[...])
        d3 = jnp.concatenate(_split3(delta.T), axis=0)
        for p in range(C_PAIRS):
            tail = _dot(sel_ref[p], d3).astype(BF16)
            dot_ref[p] = jnp.concatenate([do[:, p * LANES:(p + 1) * LANES].T.astype(BF16), tail], axis=0)

    return pl.pallas_call(
        body,
        name="fox_bwd_prep",
        grid=(seq // CHUNK,),
        in_specs=[
            pl.BlockSpec((CHUNK, C_WIDTH), lambda i: (i, 1)),
            pl.BlockSpec((CHUNK, C_WIDTH), lambda i: (i, 0)),
            pl.BlockSpec((CHUNK, C_WIDTH), lambda i: (i, 7)),
            _full((C_WIDTH, LANES)), _full((C_PAIRS, LANES, 3 * LANES)), _ANY,
        ],
        out_specs=[
            pl.BlockSpec((CHUNK, C_WIDTH), lambda i: (i, 0)),
            pl.BlockSpec((CHUNK, C_WIDTH), lambda i: (i, 7)),
            pl.BlockSpec((C_PAIRS, 2 * CHUNK, CHUNK), lambda i: (0, 0, i)),
        ],
        out_shape=[jax.ShapeDtypeStruct((seq, C_WIDTH), BF16), jax.ShapeDtypeStruct(dproj.shape, BF16),
                   jax.ShapeDtypeStruct((C_PAIRS, 2 * CHUNK, seq), BF16)],
        input_output_aliases={5: 1},
        compiler_params=_cparams("parallel"),
    )(dy, o, proj, ind, sel, dproj)


def fox_bwd(ka, va, kt, qt, dot_t, qa, dob, lse, carried=None):
    seq = qt.shape[2]
    nblk = seq // CHUNK
    bq = min(C_BQ, seq)
    nq = seq // bq
    kg = min(C_KG, nblk)
    ng = nblk // kg
    rows_t = CHUNK + C_TAIL

    def body(ka_ref, va_ref, kt_ref, qt_ref, dot_ref, qa_ref, do_ref, lse_ref,
             dq_ref, dk_ref, dv_ref, dck_ref, dcq_ref, dqt_acc, dv_acc, dka_acc):
        p, jg = pl.program_id(0), pl.program_id(1)

        @pl.when(jg == 0)
        def _():
            dqt_acc[...] = jnp.zeros_like(dqt_acc)

        dv_acc[...] = jnp.zeros_like(dv_acc)
        dka_acc[...] = jnp.zeros_like(dka_acc)

        def step(i, carry):
            cols = pl.ds(pl.multiple_of(i * bq, bq), bq)
            qtile, dotile = qt_ref[0, :, cols], dot_ref[0, :, cols]
            do, qa_i = do_ref[cols, :], qa_ref[0, cols, :]
            lse2 = jnp.concatenate([jnp.broadcast_to(lse_ref[0, 0:1, cols], (CHUNK, bq)),
                                    jnp.broadcast_to(lse_ref[0, 1:2, cols], (CHUNK, bq))] * kg, axis=0)
            pt = jnp.exp(_dot(ka_ref[0], qtile) - lse2)
            ds = pt * _dot(va_ref[0], dotile)
            ptb, dsb = pt.astype(BF16), ds.astype(BF16)
            dv_acc[...] += _dot(ptb, do)
            dka_acc[...] += _dot(dsb, qa_i)
            dqt_acc[:, cols] += _dot(kt_ref[0], dsb)
            return carry

        def diagonal(i):
            cols = [pl.ds(pl.multiple_of(i * bq + kb * CHUNK, CHUNK), bq - kb * CHUNK) for kb in range(kg)]
            rows = [slice(kb * 2 * CHUNK, (kb + 1) * 2 * CHUNK) for kb in range(kg)]
            s = [_dot(ka_ref[0, rows[kb], :], qt_ref[0, :, cols[kb]]) for kb in range(kg)]
            dp = [_dot(va_ref[0, rows[kb], :], dot_ref[0, :, cols[kb]]) for kb in range(kg)]
            ptb, dsb = [], []
            for kb in range(kg):
                n = bq - kb * CHUNK
                lse2 = jnp.concatenate([jnp.broadcast_to(lse_ref[0, 0:1, cols[kb]], (CHUNK, n)),
                                        jnp.broadcast_to(lse_ref[0, 1:2, cols[kb]], (CHUNK, n))], axis=0)
                pt = jnp.exp(s[kb] - lse2)
                pt = jnp.where(_visible(pt.shape, (jg * kg + kb) * CHUNK, i * bq + kb * CHUNK), pt, 0.0)
                ptb.append(pt.astype(BF16))
                dsb.append((pt * dp[kb]).astype(BF16))
            for kb in range(kg):
                dv_acc[rows[kb], :] += _dot(ptb[kb], do_ref[cols[kb], :])
                dka_acc[rows[kb], :] += _dot(dsb[kb], qa_ref[0, cols[kb], :])
                dqt_acc[:, cols[kb]] += _dot(kt_ref[0, :, rows[kb]], dsb[kb])

        assert kg * CHUNK == bq
        diagonal(jg)
        lax.fori_loop(jg + 1, nq, step, 0)
        lane = lax.broadcasted_iota(jnp.int32, (CHUNK, LANES), 1)
        for kb in range(kg):
            rows = slice(kb * CHUNK, (kb + 1) * CHUNK)
            ra = slice(kb * 2 * CHUNK, kb * 2 * CHUNK + CHUNK)
            rb = slice(kb * 2 * CHUNK + CHUNK, (kb + 1) * 2 * CHUNK)
            dk_ref[rows, :] = jnp.where(lane < C_HDIM, dka_acc[ra, 0:LANES], dka_acc[rb, 0:LANES]).astype(BF16)
            dv_ref[rows, :] = jnp.where(lane < C_HDIM, dv_acc[ra, :], dv_acc[rb, :]).astype(BF16)
            dck_ref[0, rows, :] = (jnp.where(lane == 2 * p, dka_acc[ra, LANES:], 0.0)
                                   + jnp.where(lane == 2 * p + 1, dka_acc[rb, LANES:], 0.0))

        @pl.when(jg == ng - 1)
        def _():
            for c in range(nq):
                dq_ref[c * bq:(c + 1) * bq, :] = (dqt_acc[0:CHUNK, c * bq:(c + 1) * bq].T * (C_HDIM ** -0.5)).astype(BF16)
            dcq_ref[0] = dqt_acc[CHUNK:rows_t, :]

    per_pair = lambda r, c: pl.BlockSpec((1, r, c), lambda p, j: (p, 0, 0))
    by_rows = pl.BlockSpec((1, kg * 2 * CHUNK, 2 * CHUNK), lambda p, j: (p, j, 0))
    by_cols = pl.BlockSpec((1, rows_t, kg * 2 * CHUNK), lambda p, j: (p, 0, j))
    return _call_carrying(
        carried, body, (ka, va, kt, qt, dot_t, qa, dob, lse),
        name="fox_bwd",
        grid=(C_PAIRS, ng),
        in_specs=[by_rows, by_rows, by_cols, per_pair(2 * CHUNK, seq), per_pair(2 * CHUNK, seq),
                  per_pair(seq, 2 * CHUNK), pl.BlockSpec((seq, LANES), lambda p, j: (0, p)), per_pair(2, seq)],
        out_specs=[pl.BlockSpec((seq, LANES), lambda p, j: (0, p)),
                   pl.BlockSpec((kg * CHUNK, LANES), lambda p, j: (j, p)),
                   pl.BlockSpec((kg * CHUNK, LANES), lambda p, j: (j, p)),
                   pl.BlockSpec((1, kg * CHUNK, LANES), lambda p, j: (p, j, 0)),
                   per_pair(C_TAIL, seq)],
        out_shape=[jax.ShapeDtypeStruct((seq, C_WIDTH), BF16)] * 3
        + [jax.ShapeDtypeStruct((C_PAIRS, seq, LANES), F32), jax.ShapeDtypeStruct((C_PAIRS, C_TAIL, seq), F32)],
        scratch_shapes=[pltpu.VMEM((rows_t, seq), F32), pltpu.VMEM((kg * 2 * CHUNK, LANES), F32),
                        pltpu.VMEM((kg * 2 * CHUNK, 2 * CHUNK), F32)],
    )


def fox_post(dcq, dck, proj, bf_row, dproj):
    seq = proj.shape[0]
    nc = seq // CHUNK
    triu = jnp.asarray(np.triu(np.ones((CHUNK, CHUNK), np.float32)), BF16)

    def body(dq_ref, dk_ref, fl_ref, bf_ref, u_ref, _, dfl_ref, dbf_ref, carry_ref):
        @pl.when(pl.program_id(0) == 0)
        def _():
            carry_ref[...] = jnp.zeros_like(carry_ref)
            dbf_ref[...] = jnp.zeros_like(dbf_ref)

        rows = (dq_ref[0] + dq_ref[1]) + (dq_ref[2] + dq_ref[3])
        dc = jnp.concatenate([rows, jnp.zeros((CHUNK - C_TAIL, CHUNK), F32)], axis=0).T
        dc = dc - ((dk_ref[0] + dk_ref[1]) + (dk_ref[2] + dk_ref[3]))
        g = _exact_times(u_ref[...], dc, 3) + carry_ref[...]
        carry_ref[...] += jnp.sum(dc, axis=0, keepdims=True)
        dfl = g * jax.nn.sigmoid(-(fl_ref[:, :LANES] + bf_ref[...]))
        dbf_ref[...] += jnp.sum(dfl, axis=0, keepdims=True)
        dfl_ref[...] = jnp.concatenate([dfl, jnp.zeros_like(dfl)], axis=1).astype(BF16)

    rev = lambda n: nc - 1 - n
    return pl.pallas_call(
        body,
        name="fox_post",
        grid=(nc,),
        in_specs=[
            pl.BlockSpec((C_PAIRS, C_TAIL, CHUNK), lambda n: (0, 0, rev(n))),
            pl.BlockSpec((C_PAIRS, CHUNK, LANES), lambda n: (0, rev(n), 0)),
            pl.BlockSpec((CHUNK, 256), lambda n: (rev(n), 3)),
            _full((1, LANES)), _full((CHUNK, CHUNK)), _ANY,
        ],
        out_specs=[pl.BlockSpec((CHUNK, 256), lambda n: (rev(n), 3)), _full((1, LANES))],
        out_shape=[jax.ShapeDtypeStruct(dproj.shape, BF16), jax.ShapeDtypeStruct((1, LANES), F32)],
        input_output_aliases={5: 0},
        scratch_shapes=[pltpu.VMEM((1, LANES), F32)],
        compiler_params=_cparams("arbitrary"),
    )(dcq, dck, proj, bf_row, triu, dproj)


N_DEV = 8
MESH = pl.DeviceIdType.MESH
_ANY = pl.BlockSpec(memory_space=pl.ANY)


def _mesh_pos():
    return lax.axis_index("x"), lax.axis_index("y"), lax.axis_index("c")


def _dev_index(px, py, pc):
    return 4 * px + 2 * py + pc


def _row_pieces(ref, rows):
    return [ref.at[idx + (pl.ds(r, rows),)] for idx in np.ndindex(*ref.shape[:-2]) for r in range(0, ref.shape[-2], rows)]


class _Transfer:
    def __init__(self, src, dst, rows, send_sem, recv_sem, to):
        self.src, self.dst, self.rows, self.sems, self.to = src, dst, rows, (send_sem, recv_sem), to

    def _copy(self, src, dst):
        return pltpu.make_async_remote_copy(src_ref=src, dst_ref=dst, send_sem=self.sems[0], recv_sem=self.sems[1],
                                            device_id=self.to, device_id_type=MESH)

    def start(self):
        for s, d in zip(_row_pieces(self.src, self.rows), _row_pieces(self.dst, self.rows), strict=True):
            self._copy(s, d).start()

    def wait_send(self):
        self._copy(self.src, self.dst).wait_send()

    def wait_recv(self):
        self._copy(self.src, self.dst).wait_recv()


def _exchange_call(ex, name):
    n_in, n_out = len(ex.inputs), len(ex.out_shape)

    def body(*refs):
        parts = refs[:n_in], refs[n_in:n_in + n_out], refs[n_in + n_out:]
        ex.start(*parts)
        getattr(ex, "relay", lambda *_: None)(*parts)
        ex.finish(*parts)

    return pl.pallas_call(body, name=name, in_specs=[_ANY] * n_in, out_specs=[_ANY] * n_out, out_shape=ex.out_shape,
                          scratch_shapes=ex.scratch, input_output_aliases=getattr(ex, "aliases", {}))(*ex.inputs)


def _carried_refs(refs, n_in, n_out, ex):
    k_in, k_out, k_sem = (len(ex.inputs), len(ex.out_shape), len(ex.scratch)) if ex else (0, 0, 0)
    a, b, c = n_in + k_in, n_in + k_in + n_out, n_in + k_in + n_out + k_out
    own = refs[:n_in] + refs[a:b] + refs[c:len(refs) - k_sem]
    return own, (refs[n_in:a], refs[b:c], refs[len(refs) - k_sem:])


class AllGatherWeights:
    def __init__(self, blocks):
        n = len(blocks)
        self.inputs = tuple(blocks)
        self.out_shape = [jax.ShapeDtypeStruct((N_DEV,) + b.shape, b.dtype) for b in blocks]
        self.scratch = ([pltpu.SemaphoreType.DMA((n, 7)), pltpu.SemaphoreType.DMA((n, 7)), pltpu.SemaphoreType.DMA((n, 2))]
                        + [pltpu.VMEM(b.shape, b.dtype) for b in blocks])

    def _plan(self, ins, outs, scratch):
        send_sems, recv_sems, local_sems, *staged = scratch
        x, y, c = _mesh_pos()
        me, sibling = (x, y, c), (x, y, 1 - c)
        chips = [(1 - x, y), (x, 1 - y), (1 - x, 1 - y)]
        every = range(len(ins))

        def copy(a, k, block, to, own=False):
            slot = outs[a].at[_dev_index(*block)]
            return _Transfer(ins[a] if own else slot, slot, ins[a].shape[-2], send_sems.at[a, k], recv_sems.at[a, k], to)

        mine = [(pltpu.make_async_copy(ins[a], staged[a], local_sems.at[a, 0]),
                 pltpu.make_async_copy(staged[a], outs[a].at[_dev_index(*me)], local_sems.at[a, 1])) for a in every]
        first = [copy(a, 1 + j, me, (*chip, c), own=True) for j, chip in enumerate(chips) for a in every]
        first += [copy(a, 0, me, sibling, own=True) for a in every]
        passed = [[copy(a, 4 + j, (*chip, c), sibling) for a in every] for j, chip in enumerate(chips)]
        return me, sibling, chips, c, every, copy, mine, first, passed

    def start(self, ins, outs, scratch):
        *_, mine, first, _ = self._plan(ins, outs, scratch)
        for to_vmem, _ in mine:
            to_vmem.start()
        for cp in first:
            cp.start()

    def relay(self, ins, outs, scratch):
        me, sibling, chips, c, every, copy, mine, first, passed = self._plan(ins, outs, scratch)
        for to_vmem, to_slot in mine:
            to_vmem.wait()
            to_slot.start()
        for j, chip in enumerate(chips):
            for a in every:
                copy(a, 1 + j, (*chip, c), me).wait_recv()
            for cp in passed[j]:
                cp.start()

    def finish(self, ins, outs, scratch):
        me, sibling, chips, c, every, copy, mine, first, passed = self._plan(ins, outs, scratch)
        for a in every:
            copy(a, 0, sibling, me).wait_recv()
        for j, chip in enumerate(chips):
            for a in every:
                copy(a, 4 + j, (*chip, 1 - c), me).wait_recv()
        for cp in first + [cp for group in passed for cp in group]:
            cp.wait_send()
        for _, to_slot in mine:
            to_slot.wait()


N_CHIP = 4


class PairExchange:
    def __init__(self, by_core, whole=()):
        self.inputs = tuple(by_core) + tuple(whole)
        self.n_by_core = len(by_core)
        self.out_shape = ([jax.ShapeDtypeStruct(a.shape[1:], a.dtype) for a in by_core]
                          + [jax.ShapeDtypeStruct(a.shape, a.dtype) for a in whole])
        n = len(self.inputs)
        self.scratch = [pltpu.SemaphoreType.DMA((n,)), pltpu.SemaphoreType.DMA((n,))]

    def _copies(self, ins, outs, sems):
        x, y, c = _mesh_pos()
        srcs = [r.at[1 - c] if a < self.n_by_core else r for a, r in enumerate(ins)]
        return [_Transfer(srcs[a], outs[a], outs[a].shape[-2], sems[0].at[a], sems[1].at[a], (x, y, 1 - c))
                for a in range(len(ins))]

    def start(self, ins, outs, sems):
        for cp in self._copies(ins, outs, sems):
            cp.start()

    def finish(self, ins, outs, sems):
        copies = self._copies(ins, outs, sems)
        for cp in copies:
            cp.wait_recv()
        for cp in copies:
            cp.wait_send()


def pair_sum(own, other, dtype, rows, name, core, layer, depth, stacked=None):
    n, n_r, n_c = other.shape
    by_core = own.ndim == 4
    own = own if by_core else own[None]

    def body(core_ref, a_ref, b_ref, *refs):
        refs[-1][0, 0] = (a_ref[0, 0] + b_ref[0].astype(F32)).astype(dtype)

    carried = () if stacked is None else (stacked,)
    grid_spec = pltpu.PrefetchScalarGridSpec(
        num_scalar_prefetch=1,
        grid=(n, n_r // rows),
        in_specs=[pl.BlockSpec((1, 1, rows, n_c), lambda i, r, s: (s[0] if by_core else 0, i, r, 0)),
                  pl.BlockSpec((1, rows, n_c), lambda i, r, s: (i, r, 0))] + [_ANY] * len(carried),
        out_specs=pl.BlockSpec((1, 1, rows, n_c), lambda i, r, s: (i, layer, r, 0)),
    )
    return pl.pallas_call(
        body,
        name=name,
        grid_spec=grid_spec,
        out_shape=jax.ShapeDtypeStruct((n, depth, n_r, n_c), dtype),
        input_output_aliases={3: 0} if carried else {},
        compiler_params=_cparams("parallel", "parallel"),
    )(core, own, other, *carried)


def small_sum(a, b, name):
    def body(a_ref, b_ref, o_ref):
        o_ref[...] = a_ref[...] + b_ref[...]

    return pl.pallas_call(body, name=name, out_shape=jax.ShapeDtypeStruct(a.shape, a.dtype))(a, b)


class ChipExchange:
    def __init__(self, by_chip=(), layers=(), gathered=(), stacked=()):
        stacked = tuple(stacked) or (None,) * len(by_chip)
        kept = [s for s in stacked if s is not None]
        self.inputs = tuple(by_chip) + tuple(gathered) + tuple(kept)
        self.n_by_chip, self.n_gathered = len(by_chip), len(gathered)
        self.items = [(a, l) for a in range(len(by_chip)) for l in layers[a]] + [(self.n_by_chip + g, None) for g in range(len(gathered))]
        self.out_shape = ([jax.ShapeDtypeStruct((N_CHIP - 1,) + a.shape[1:], a.dtype) for a in by_chip]
                          + [jax.ShapeDtypeStruct((N_CHIP,) + a.shape, a.dtype) for a in gathered])
        at = iter(range(self.n_by_chip + self.n_gathered, len(self.inputs)))
        self.aliases = {next(at): a for a, s in enumerate(stacked) if s is not None}
        n = len(self.items)
        self.scratch = [pltpu.SemaphoreType.DMA((n, 3)), pltpu.SemaphoreType.DMA((n, 3)),
                        pltpu.SemaphoreType.DMA((max(self.n_gathered, 1),))]

    def _plan(self, ins, outs, sems):
        x, y, c = _mesh_pos()
        chip = 2 * x + y
        n = len(self.items)

        def copy(i, k, sending):
            a, layer = self.items[i]
            px, py = x ^ ((k >> 1) & 1), y ^ (k & 1)
            if layer is not None:
                src, dst = ins[a].at[2 * px + py, layer], outs[a].at[k - 1, layer]
            else:
                src, dst = ins[a], outs[a].at[chip if sending else 2 * px + py]
            return _Transfer(src, dst, dst.shape[-2], sems[0].at[i, k - 1], sems[1].at[i, k - 1], (px, py, c))

        local = [pltpu.make_async_copy(ins[a], outs[a].at[chip], sems[2].at[a - self.n_by_chip])
                 for a in range(self.n_by_chip, self.n_by_chip + self.n_gathered)]
        return n, copy, local

    def start(self, ins, outs, sems):
        n, copy, local = self._plan(ins, outs, sems)
        for cp in local:
            cp.start()
        for k in range(1, N_CHIP):
            for a in range(n):
                copy(a, k, True).start()

    def finish(self, ins, outs, sems):
        n, copy, local = self._plan(ins, outs, sems)
        for k in range(1, N_CHIP):
            for a in range(n):
                copy(a, k, False).wait_recv()
        for k in range(1, N_CHIP):
            for a in range(n):
                copy(a, k, True).wait_send()
        for cp in local:
            cp.wait()


ADAM_LR = 0.001
ADAM_B1 = 0.9
ADAM_B2 = 0.999
ADAM_EPS = 1e-08
ADAM_WD = 0.01
ADAM_STEP = 10


def adam_reduce(parts, w, m, v, rows, name, own=None, chip=None):
    n_l, n_r, n_c = w.shape
    n_parts = parts.shape[0]

    def body(*refs):
        p_ref, w_ref, m_ref, v_ref, g_ref, d_ref, m2_ref, v2_ref = refs[-8:]
        g = p_ref[0, 0].astype(F32)
        if own is not None:
            g = refs[-9][...].reshape(rows, n_c).astype(F32) + g
        for d in range(1, n_parts):
            g = g + p_ref[d, 0].astype(F32)
        m2 = ADAM_B1 * m_ref[0] + (1.0 - ADAM_B1) * g
        v2 = ADAM_B2 * v_ref[0] + (1.0 - ADAM_B2) * (g * g)
        m_hat = m2 / (1.0 - ADAM_B1 ** ADAM_STEP)
        v_hat = v2 / (1.0 - ADAM_B2 ** ADAM_STEP)
        g_ref[0] = g
        d_ref[0] = -ADAM_LR * (m_hat / (jnp.sqrt(v_hat) + ADAM_EPS) + ADAM_WD * w_ref[0])
        m2_ref[0] = m2
        v2_ref[0] = v2

    blk = lambda: pl.BlockSpec((1, rows, n_c), lambda l, r, *_: (l, r, 0))
    in_specs = [pl.BlockSpec((n_parts, 1, rows, n_c), lambda l, r, *_: (0, l, r, 0)), blk(), blk(), blk()]
    args = (parts, w, m, v)
    if own is not None:
        in_specs = [pl.BlockSpec((1, 1, rows, n_c), lambda l, r, s: (s[0], l, r, 0))] + in_specs
        args = (chip, own) + args
    grid_spec = pltpu.PrefetchScalarGridSpec(
        num_scalar_prefetch=0 if own is None else 1, grid=(n_l, n_r // rows), in_specs=in_specs,
        out_specs=[blk(), blk(), blk(), blk()])
    return pl.pallas_call(
        body,
        name=name,
        grid_spec=grid_spec,
        out_shape=[jax.ShapeDtypeStruct(w.shape, F32)] * 4,
        compiler_params=_cparams("parallel", "parallel"),
    )(*args)


def adam_reduce_columns(parts, w, m, v, name, own, chip):
    n_l, n_r, n_c = w.shape
    n_parts = parts.shape[0]
    view = lambda a: jnp.transpose(a, (2, 0, 1))

    def body(_, own_ref, p_ref, w_ref, m_ref, v_ref, g_ref, d_ref, m2_ref, v2_ref):
        for l in range(n_l):
            g = own_ref[0, l].astype(F32) + p_ref[0, l].astype(F32)
            for d in range(1, n_parts):
                g = g + p_ref[d, l].astype(F32)
            g = g.T
            w_l, m_l, v_l = w_ref[:, l, :], m_ref[:, l, :], v_ref[:, l, :]
            m2 = ADAM_B1 * m_l + (1.0 - ADAM_B1) * g
            v2 = ADAM_B2 * v_l + (1.0 - ADAM_B2) * (g * g)
            m_hat = m2 / (1.0 - ADAM_B1 ** ADAM_STEP)
            v_hat = v2 / (1.0 - ADAM_B2 ** ADAM_STEP)
            g_ref[:, l, :] = g
            d_ref[:, l, :] = -ADAM_LR * (m_hat / (jnp.sqrt(v_hat) + ADAM_EPS) + ADAM_WD * w_l)
            m2_ref[:, l, :] = m2
            v2_ref[:, l, :] = v2

    blk = lambda: pl.BlockSpec((LANES, n_l, n_r), lambda c, s: (c, 0, 0))
    grid_spec = pltpu.PrefetchScalarGridSpec(
        num_scalar_prefetch=1, grid=(pl.cdiv(n_c, LANES),),
        in_specs=[pl.BlockSpec((1, n_l, n_r, LANES), lambda c, s: (s[0], 0, 0, c)),
                  pl.BlockSpec((n_parts, n_l, n_r, LANES), lambda c, s: (0, 0, 0, c)), blk(), blk(), blk()],
        out_specs=[blk(), blk(), blk(), blk()])
    outs = pl.pallas_call(
        body,
        name=name,
        grid_spec=grid_spec,
        out_shape=[jax.ShapeDtypeStruct((n_c, n_l, n_r), F32)] * 4,
        compiler_params=_cparams("parallel"),
    )(chip, own, parts, view(w), view(m), view(v))
    return [jnp.transpose(o, (1, 2, 0)) for o in outs]


_SMALL = (("norm_g", (2, 1024)), ("gmlp_ln_g", (2, 4, 64)), ("gmlp_ln_b", (2, 4, 64)),
          ("gmlp_b_s", (2, 4, 128)), ("hgrn_lb", (2, 256)), ("hgrn_onorm_g", (2, 64)), ("fox_b_f", (2, 8)),
          ("final_norm_g", (1024,)), ("loss", ()))


def _padded(n):
    return -(-n // LANES) * LANES


_SMALL_ROWS = -(-sum(_padded(int(np.prod(s))) for _, s in _SMALL) // LANES // 8) * 8


def _pack_small(vals):
    flat = []
    for (name, shape), a in zip(_SMALL, vals, strict=True):
        n = int(np.prod(shape))
        flat.append(jnp.pad(a.reshape(n).astype(F32), (0, _padded(n) - n)))
    flat = jnp.concatenate(flat)
    return jnp.pad(flat, (0, _SMALL_ROWS * LANES - flat.shape[0])).reshape(_SMALL_ROWS, LANES)


def _unpack_small(slab):
    flat, out, at = slab.reshape(-1), {}, 0
    for name, shape in _SMALL:
        n = int(np.prod(shape))
        out[name] = flat[at:at + n].reshape(shape)
        at += _padded(n)
    return out


def sum_parts(parts, name):
    def body(p_ref, o_ref):
        g = p_ref[0]
        for d in range(1, parts.shape[0]):
            g = g + p_ref[d]
        o_ref[...] = g

    return pl.pallas_call(body, name=name, out_shape=jax.ShapeDtypeStruct(parts.shape[1:], F32))(parts)


def adam_small(gs, ws, ms, vs):
    n = len(gs)

    def body(*refs):
        for k in range(n):
            g, w, m, v = (refs[j * n + k][...] for j in range(4))
            m2 = ADAM_B1 * m + (1.0 - ADAM_B1) * g
            v2 = ADAM_B2 * v + (1.0 - ADAM_B2) * (g * g)
            m_hat = m2 / (1.0 - ADAM_B1 ** ADAM_STEP)
            v_hat = v2 / (1.0 - ADAM_B2 ** ADAM_STEP)
            refs[4 * n + k][...] = -ADAM_LR * (m_hat / (jnp.sqrt(v_hat) + ADAM_EPS) + ADAM_WD * w)
            refs[5 * n + k][...] = m2
            refs[6 * n + k][...] = v2

    outs = pl.pallas_call(body, name="adam_small",
                          out_shape=[jax.ShapeDtypeStruct(w.shape, F32) for _ in range(3) for w in ws])(*gs, *ws, *ms, *vs)
    return outs[:n], outs[n:2 * n], outs[2 * n:]


def kernel(x, norm_g, w_in, w_out, gmlp_ln_g, gmlp_ln_b, gmlp_w_s, gmlp_b_s, hgrn_lb, hgrn_onorm_g, fox_b_f, final_norm_g, loss_target, m_norm_g, m_w_in, m_w_out, m_gmlp_ln_g, m_gmlp_ln_b, m_gmlp_w_s, m_gmlp_b_s, m_hgrn_lb, m_hgrn_onorm_g, m_fox_b_f, m_final_norm_g, v_norm_g, v_w_in, v_w_out, v_gmlp_ln_g, v_gmlp_ln_b, v_gmlp_w_s, v_gmlp_b_s, v_hgrn_lb, v_hgrn_onorm_g, v_fox_b_f, v_final_norm_g):
    depth = w_in.shape[0]
    seq = x.shape[1]
    assert w_in.shape[2] * N_DEV == N_IN
    xs, tgt = x[0], loss_target[0]

    wi_blk, wo_blk = w_in.astype(BF16), w_out.astype(BF16)
    (wi_all,) = _exchange_call(AllGatherWeights([wi_blk[0]]), "allgather_weights_0")

    ln_g = gmlp_ln_g.reshape(depth, 1, A_WIDTH)
    ln_b = gmlp_ln_b.reshape(depth, 1, A_WIDTH)
    bs_t = jnp.pad(jnp.transpose(gmlp_b_s, (0, 2, 1)), ((0, 0), (0, 0), (0, LANES - A_GROUPS)))
    lb0, lb1 = hgrn_lb[0:1], hgrn_lb[1:2]
    onorm = jnp.tile(hgrn_onorm_g, (1, B_HEADS)).reshape(depth, 1, B_WIDTH)
    bf_row = jnp.pad(fox_b_f, ((0, 0), (0, LANES - C_HEADS))).reshape(depth, 1, LANES)

    core = lax.axis_index("c").astype(jnp.int32).reshape(1)
    chip = (2 * lax.axis_index("x") + lax.axis_index("y")).astype(jnp.int32).reshape(1)

    saved = []
    xc = xs
    for l in range(depth):
        wi_int = assemble_w_in(wi_all[:, None])
        proj, h = inproj(xc, norm_g[l:l + 1], wi_int, 0)
        ya = gmlp_fwd(proj, ln_g[l], ln_b[l], gmlp_w_s[l], bs_t[l])
        yb, states = hgrn_fwd(proj, lb0, lb1, onorm[l], l)
        ka, va, vt, kt, qt, qa = fox_prep(proj, bf_row[l])
        ride = ([wo_blk] if l == 0 else []) + ([wi_blk[l + 1]] if l + 1 < depth else [])
        o, lse, *gathered = fox_fwd(qt, ka, vt, AllGatherWeights(ride) if ride else None)
        if l == 0:
            wo_all = gathered.pop(0)
        if gathered:
            (wi_all,) = gathered
        x_in = xc
        if l + 1 < depth:
            xc, yfull = outproj(x_in, ya, yb, o, proj, wo_all, l)
        else:
            dx, yfull, d_final_g, loss_tile = outproj(x_in, ya, yb, o, proj, wo_all, l, (final_norm_g[None], tgt))
        saved.append((x_in, proj, h, states, ka, va, kt, qt, qa, o, lse, yfull, wi_int))

    n_shard = w_in.shape[2]
    g_norm = [None] * depth
    g_ln_g, g_ln_b, g_ws, g_bs, g_on, g_bf = ([None] * depth for _ in range(6))
    g_lb0, g_lb1 = jnp.zeros_like(lb0), jnp.zeros_like(lb1)
    swi = swo = rwi = rwo = None
    for l in reversed(range(depth)):
        x_in, proj, h, states, ka, va, kt, qt, qa, o, lse, yfull, wi_int = saved[l]
        dy, gwo = outproj_bwd(dx, yfull, wo_all, l)
        dproj, g_ln_g[l], g_ln_b[l], g_ws[l], dbs_t = gmlp_bwd(proj, dy, ln_g[l], ln_b[l], gmlp_w_s[l], bs_t[l])
        g_bs[l] = dbs_t[:, :A_GROUPS].T
        if l > 0:
            (qwo,) = _exchange_call(PairExchange([gwo]), f"pair_exchange_w_out_{l}")
        else:
            gws = jnp.stack(g_ws).reshape(-1, LANES)
            qwo, qws = _exchange_call(PairExchange([gwo], [gws]), f"pair_exchange_w_out_{l}")
            sws = small_sum(gws, qws, "pair_sum_w_s")
        swo = pair_sum(gwo, qwo, BF16, gwo.shape[2], "pair_sum_w_out", core, l, depth, swo)
        dproj, d0, d1, don = hgrn_bwd(proj, states, dy, lb0, lb1, onorm[l], l, dproj)
        g_lb0, g_lb1 = g_lb0 + d0, g_lb1 + d1
        g_on[l] = don.reshape(B_HEADS, B_KDIM).sum(0)
        dob, dproj, dot_t = fox_bwd_prep(dy, o, proj, dproj)
        top = l == depth - 1
        ride = ChipExchange([swo] if top else [swi, swo], [(l,)] if top else [(l + 1,), (l,)],
                            [sws] if l == 0 else [], [rwo] if top else [rwi, rwo])
        outs = fox_bwd(ka, va, kt, qt, dot_t, qa, dob, lse, ride)
        dqkv, (dck, dcq), got = outs[:3], outs[3:5], list(outs[5:])
        if not top:
            rwi = got.pop(0)
        rwo = got.pop(0)
        if l == 0:
            (rws,) = got
        dproj, dbf = fox_post(dcq, dck, proj, bf_row[l], dproj)
        g_bf[l] = dbf[0, :C_HEADS]
        gwi, for_sibling = split_w_in_grad(inproj_bwd_w(h, dproj, dqkv), n_shard, core)
        (qwi,) = _exchange_call(PairExchange([], [for_sibling]), f"pair_exchange_w_in_{l}")
        swi = pair_sum(gwi, qwi, BF16, 256, "pair_sum_w_in", core, l, depth, swi)
        ride = ChipExchange([swi], [(l,)], stacked=[rwi]) if l == 0 else None
        outs = inproj_bwd_x(dproj, dqkv, wi_int, x_in, norm_g[l:l + 1], dx, 0, ride)
        dx, g_norm[l] = outs[:2]
        if ride is not None:
            (rwi,) = outs[2:]

    gsm = _pack_small([
        jnp.concatenate(g_norm), jnp.stack(g_ln_g), jnp.stack(g_ln_b), jnp.stack(g_bs),
        jnp.concatenate([g_lb0, g_lb1]), jnp.stack(g_on), jnp.stack(g_bf), d_final_g, loss_tile[0, 0]])
    (qsm,) = _exchange_call(PairExchange([], [gsm]), "pair_exchange_small")
    ssm = small_sum(gsm, qsm, "pair_sum_small")
    (rsm,) = _exchange_call(ChipExchange(gathered=[ssm]), "chip_exchange_small")

    small_w = (norm_g, gmlp_ln_g, gmlp_ln_b, gmlp_b_s, hgrn_lb, hgrn_onorm_g, fox_b_f, final_norm_g)
    small_m = (m_norm_g, m_gmlp_ln_g, m_gmlp_ln_b, m_gmlp_b_s, m_hgrn_lb, m_hgrn_onorm_g, m_fox_b_f, m_final_norm_g)
    small_v = (v_norm_g, v_gmlp_ln_g, v_gmlp_ln_b, v_gmlp_b_s, v_hgrn_lb, v_hgrn_onorm_g, v_fox_b_f, v_final_norm_g)
    res_wi = adam_reduce_columns(rwi, w_in, m_w_in, v_w_in, "adam_w_in", swi, chip)
    res_wo = adam_reduce(rwo, w_out, m_w_out, v_w_out, w_out.shape[1], "adam_w_out", own=swo, chip=chip)
    grads = _unpack_small(sum_parts(rsm, "sum_small"))
    names = [name for name, _ in _SMALL if name != "loss"]
    rows = lambda a: a.reshape(1, -1) if a.ndim == 1 else a
    res_sm = adam_small([rows(grads[k]) for k in names], *([rows(a) for a in wmv] for wmv in (small_w, small_m, small_v)))
    res_sm = [grads] + [{k: a.reshape(grads[k].shape) for k, a in zip(names, r, strict=True)} for r in res_sm]
    as_rows = lambda a: a.reshape(1, -1, LANES)
    res_ws = adam_reduce(rws[:, None], as_rows(gmlp_w_s), as_rows(m_gmlp_w_s), as_rows(v_gmlp_w_s), rws.shape[1], "adam_w_s")
    for s, r in zip(res_sm, res_ws, strict=True):
        s["gmlp_w_s"] = r.reshape(gmlp_w_s.shape)

    def group(i):
        s = res_sm[i]
        return [s["norm_g"], res_wi[i], res_wo[i], s["gmlp_ln_g"], s["gmlp_ln_b"], s["gmlp_w_s"], s["gmlp_b_s"],
                s["hgrn_lb"], s["hgrn_onorm_g"], s["fox_b_f"], s["final_norm_g"]]

    return (res_sm[0]["loss"], dx[None], *group(0), *group(1), *group(2), *group(3))
```

```python
import functools

import jax
import jax.numpy as jnp
import numpy as np
from jax import lax
from jax.experimental import pallas as pl
from jax.experimental.pallas import tpu as pltpu

F32 = jnp.float32
BF16 = jnp.bfloat16

NORM_EPS = 1e-6
F_FLOOR = 1e-30
CHUNK = 128
LANES = 128
VMEM_LIMIT = 56 * 1024 * 1024


def _cparams(*sem):
    return pltpu.CompilerParams(dimension_semantics=sem, vmem_limit_bytes=VMEM_LIMIT)


def _dot(a, b, dims=(((1,), (0,)), ((), ())), precision=None):
    return lax.dot_general(a, b, dims, precision=precision, preferred_element_type=F32)


_NT = (((1,), (1,)), ((), ()))
_TN = (((0,), (0,)), ((), ()))


def _bf16_pieces(x, n):
    out, r = [], x
    for i in range(n):
        out.append(r.astype(BF16))
        if i + 1 < n:
            r = r - out[-1].astype(F32)
    return out


@functools.partial(jax.custom_vjp, nondiff_argnums=(2,))
def _times_exact(x, e, n):
    return functools.reduce(jnp.add, [_dot(p, e) for p in _bf16_pieces(x, n)])


def _times_exact_fwd(x, e, n):
    return _times_exact(x, e, n), e


def _times_exact_bwd(n, e, g):
    dx = functools.reduce(jnp.add, [lax.dot_general(p, e, _NT, preferred_element_type=F32) for p in _bf16_pieces(g, n)])
    return dx, jnp.zeros_like(e)


_times_exact.defvjp(_times_exact_fwd, _times_exact_bwd)


@functools.partial(jax.custom_vjp, nondiff_argnums=(2,))
def _exact_times(e, x, n):
    return functools.reduce(jnp.add, [_dot(e, p) for p in _bf16_pieces(x, n)])


def _exact_times_fwd(e, x, n):
    return _exact_times(e, x, n), e


def _exact_times_bwd(n, e, g):
    dx = functools.reduce(jnp.add, [lax.dot_general(e, p, _TN, preferred_element_type=F32) for p in _bf16_pieces(g, n)])
    return jnp.zeros_like(e), dx


_exact_times.defvjp(_exact_times_fwd, _exact_times_bwd)


def _group_mean_matrix(width, group):
    idx = np.arange(width) // group
    return jnp.asarray((idx[:, None] == idx[None, :]).astype(np.float32) / group, BF16)


def _group_ones_matrix(width, group):
    idx = np.arange(width) // group
    return jnp.asarray((idx[:, None] == idx[None, :]).astype(np.float32), BF16)


A_WIDTH = 256
A_GROUPS = 4
A_GDIM = 64


A_ROWS = 512


def _gmlp_chunk(x3, ln_g, ln_b, w_s, bs_t, mean_m, gind):
    n = x3.shape[0] // CHUNK
    u = jax.nn.gelu(x3[:, :A_WIDTH])
    v = jax.nn.gelu(x3[:, A_WIDTH:2 * A_WIDTH])
    z = x3[:, 2 * A_WIDTH:]
    mu = _times_exact(v, mean_m, 2)
    d = v - mu
    var = _times_exact(d * d, mean_m, 2)
    vn = d * lax.rsqrt(var + NORM_EPS) * ln_g + ln_b
    vnb = vn.astype(BF16)
    wide = jnp.concatenate([vnb[i * CHUNK:(i + 1) * CHUNK] for i in range(n)], axis=1)
    row = lax.broadcasted_iota(jnp.int32, (CHUNK, CHUNK), 0)
    col = lax.broadcasted_iota(jnp.int32, (CHUNK, CHUNK), 1)
    causal = row >= col
    lane_g = lax.shift_right_logical(lax.broadcasted_iota(jnp.int32, (CHUNK, n * A_WIDTH), 1), 6) & (A_GROUPS - 1)
    bias = _times_exact(bs_t, gind, 3)
    mixed = jnp.concatenate([bias] * n, axis=1)
    for g in range(A_GROUPS):
        wc = jnp.where(causal, w_s[g], 0.0).astype(BF16)
        mixed = mixed + jnp.where(lane_g == g, _dot(wc, wide), 0.0)
    mixed = jnp.concatenate([mixed[:, i * A_WIDTH:(i + 1) * A_WIDTH] for i in range(n)], axis=0)
    return u * mixed * jax.nn.silu(z)


def _gmlp_consts():
    gind = np.zeros((LANES, A_WIDTH), np.float32)
    for g in range(A_GROUPS):
        gind[g, g * A_GDIM:(g + 1) * A_GDIM] = 1.0
    return _group_mean_matrix(A_WIDTH, A_GDIM), jnp.asarray(gind, BF16)


def _full(shape):
    return pl.BlockSpec(shape, lambda *_: (0,) * len(shape))


def gmlp_fwd(proj, ln_g, ln_b, w_s, bs_t):
    seq = proj.shape[0]
    rows = min(A_ROWS, seq)
    mean_m, gind = _gmlp_consts()

    def body(x_ref, g_ref, b_ref, w_ref, bs_ref, m_ref, gi_ref, y_ref):
        y = _gmlp_chunk(x_ref[...], g_ref[...], b_ref[...], w_ref[...], bs_ref[...], m_ref[...], gi_ref[...])
        y_ref[...] = y.astype(BF16)

    return pl.pallas_call(
        body,
        name="gmlp_fwd",
        grid=(seq // rows,),
        in_specs=[
            pl.BlockSpec((rows, 3 * A_WIDTH), lambda n: (n, 0)),
            _full((1, A_WIDTH)), _full((1, A_WIDTH)), _full((A_GROUPS, CHUNK, CHUNK)), _full((CHUNK, LANES)),
            _full((A_WIDTH, A_WIDTH)), _full((LANES, A_WIDTH)),
        ],
        out_specs=pl.BlockSpec((rows, A_WIDTH), lambda n: (n, 0)),
        out_shape=jax.ShapeDtypeStruct((seq, A_WIDTH), BF16),
        compiler_params=_cparams("parallel"),
    )(proj, ln_g, ln_b, w_s, bs_t, mean_m, gind)


def gmlp_bwd(proj, dy, ln_g, ln_b, w_s, bs_t):
    seq = proj.shape[0]
    rows = min(A_ROWS, seq)
    mean_m, gind = _gmlp_consts()

    def body(x_ref, dy_ref, g_ref, b_ref, w_ref, bs_ref, m_ref, gi_ref, dx_ref, dg_ref, db_ref, dw_ref, dbs_ref):
        fn = functools.partial(_gmlp_chunk, mean_m=m_ref[...], gind=gi_ref[...])
        _, vjp = jax.vjp(fn, x_ref[...], g_ref[...], b_ref[...], w_ref[...], bs_ref[...])
        dx, dg, db, dw, dbs = vjp(dy_ref[...])
        dx_ref[...] = dx.astype(BF16)

        @pl.when(pl.program_id(0) == 0)
        def _():
            dg_ref[...] = jnp.zeros_like(dg_ref)
            db_ref[...] = jnp.zeros_like(db_ref)
            dw_ref[...] = jnp.zeros_like(dw_ref)
            dbs_ref[...] = jnp.zeros_like(dbs_ref)

        dg_ref[...] += dg
        db_ref[...] += db
        dw_ref[...] += dw
        dbs_ref[...] += dbs

    return pl.pallas_call(
        body,
        name="gmlp_bwd",
        grid=(seq // rows,),
        in_specs=[
            pl.BlockSpec((rows, 3 * A_WIDTH), lambda n: (n, 0)),
            pl.BlockSpec((rows, A_WIDTH), lambda n: (n, 0)),
            _full((1, A_WIDTH)), _full((1, A_WIDTH)), _full((A_GROUPS, CHUNK, CHUNK)), _full((CHUNK, LANES)),
            _full((A_WIDTH, A_WIDTH)), _full((LANES, A_WIDTH)),
        ],
        out_specs=[
            pl.BlockSpec((rows, 3 * A_WIDTH), lambda n: (n, 0)),
            _full((1, A_WIDTH)), _full((1, A_WIDTH)), _full((A_GROUPS, CHUNK, CHUNK)), _full((CHUNK, LANES)),
        ],
        out_shape=[
            jax.ShapeDtypeStruct((seq, D_INT), BF16),
            jax.ShapeDtypeStruct((1, A_WIDTH), F32), jax.ShapeDtypeStruct((1, A_WIDTH), F32),
            jax.ShapeDtypeStruct((A_GROUPS, CHUNK, CHUNK), F32), jax.ShapeDtypeStruct((CHUNK, LANES), F32),
        ],
        compiler_params=_cparams("arbitrary"),
    )(proj, dy, ln_g, ln_b, w_s, bs_t, mean_m, gind)


B_WIDTH = 256
B_HEADS = 4
B_KDIM = 64
B_LEVELS = (64, 32, 16, 8, 4, 2, 1)


def _hgrn_consts():
    t = np.arange(CHUNK)
    u = t[None, :]
    mats = [np.tril(np.ones((CHUNK, CHUNK), np.float32))]
    for m in B_LEVELS:
        p = (t // (2 * m)) * (2 * m) + m - 1
        right = (t % (2 * m)) >= m
        sel = np.where(right[:, None], (u > p[:, None]) & (u <= t[:, None]), (u > t[:, None]) & (u <= p[:, None]))
        mats.append(sel.astype(np.float32))
    return jnp.asarray(np.concatenate(mats, 0), BF16), _group_ones_matrix(B_WIDTH, B_KDIM)


def _hgrn_lower_bound(lb0, lb1, layer):
    mx = jnp.maximum(lb0, lb1)
    e0 = jnp.exp(lb0 - mx)
    e1 = jnp.exp(lb1 - mx)
    p0 = e0 / (e0 + e1)
    p1 = e1 / (e0 + e1)
    cs = p0 if layer == 0 else p0 + p1
    return jnp.clip(cs - p0, 0.0, 1.0 - 1e-6)


def _hgrn_chunk(x4, st, lb0, lb1, onorm, layer, tstack, ones_bd):
    q_raw, fl, v, zg = (x4[:, i * B_WIDTH:(i + 1) * B_WIDTH] for i in range(4))
    lb = _hgrn_lower_bound(lb0, lb1, layer)
    q = jax.nn.silu(q_raw) * (B_KDIM ** -0.5)
    f = lb + (1.0 - lb) * jax.nn.sigmoid(fl)
    logf = jnp.log(jnp.maximum(f, F_FLOOR))
    k = (1.0 - lb) * jax.nn.sigmoid(-fl)
    b = _exact_times(tstack[:CHUNK], logf, 3)
    dall = jnp.concatenate([b, _exact_times(tstack[CHUNK:], logf, 2)], axis=0)
    b_last = jnp.sum(logf, axis=0, keepdims=True)
    vb = v.astype(BF16)

    lane_h = lax.shift_right_logical(lax.broadcasted_iota(jnp.int32, (CHUNK, B_WIDTH), 1), 6)
    row = lax.broadcasted_iota(jnp.int32, (CHUNK, B_WIDTH), 0)
    srow = lax.broadcasted_iota(jnp.int32, (B_HEADS * CHUNK, CHUNK), 0) & (CHUNK - 1)
    scol = lax.broadcasted_iota(jnp.int32, (B_HEADS * CHUNK, CHUNK), 1)

    def heads_on_rows(a):
        return jnp.concatenate([jnp.where(lane_h == h, a, 0.0) for h in range(B_HEADS)], axis=0)

    def heads_from_rows(r):
        out = jnp.where(lane_h == 0, r[:CHUNK], 0.0)
        for h in range(1, B_HEADS):
            out = out + jnp.where(lane_h == h, r[h * CHUNK:(h + 1) * CHUNK], 0.0)
        return out

    o = lax.dot_general((q * jnp.exp(b)).astype(BF16), st.astype(BF16), _NT, preferred_element_type=F32)
    scores = jnp.zeros((B_HEADS * CHUNK, CHUNK), F32)
    for li, m in enumerate(B_LEVELS):
        e = jnp.exp(dall[(li + 1) * CHUNK:(li + 2) * CHUNK])
        right = (row & (2 * m - 1)) >= m
        qt = jnp.where(right, q * e, 0.0)
        kt = jnp.where(right, 0.0, k * e)
        sc = lax.dot_general(heads_on_rows(qt).astype(BF16), kt.astype(BF16), _NT, preferred_element_type=F32)
        sh = int(np.log2(2 * m))
        same = lax.shift_right_logical(srow, sh) == lax.shift_right_logical(scol, sh)
        scores = scores + jnp.where(same, sc, 0.0)
    o = o + heads_from_rows(_dot(scores.astype(BF16), vb))
    o = o + _times_exact(q * k, ones_bd, 2) * v

    kv = lax.dot_general(vb, (k * jnp.exp(b_last - b)).astype(BF16), _TN, preferred_element_type=F32)
    st_new = st * jnp.exp(b_last) + jnp.where(ones_bd > 0.5, kv, 0.0)

    ms = _times_exact(o * o, ones_bd, 2) * (1.0 / B_KDIM)
    y = o * lax.rsqrt(ms + NORM_EPS) * onorm * jax.nn.silu(zg)
    return y, st_new


B_ROWS = 256


def _hgrn_rows(x4, st, lb0, lb1, onorm, layer, tstack, ones_bd):
    ys = []
    for i in range(x4.shape[0] // CHUNK):
        y, st = _hgrn_chunk(x4[i * CHUNK:(i + 1) * CHUNK], st, lb0, lb1, onorm, layer, tstack, ones_bd)
        ys.append(y)
    return jnp.concatenate(ys, axis=0), st


def hgrn_fwd(proj, lb0, lb1, onorm, layer):
    seq = proj.shape[0]
    rows = min(B_ROWS, seq)
    nc = seq // rows
    tstack, ones_bd = _hgrn_consts()

    def body(x_ref, lb0_ref, lb1_ref, on_ref, t_ref, e_ref, y_ref, st_out_ref, st_ref):
        @pl.when(pl.program_id(0) == 0)
        def _():
            st_ref[...] = jnp.zeros_like(st_ref)

        st = st_ref[...]
        st_out_ref[0] = st
        y, st_new = _hgrn_rows(x_ref[...], st, lb0_ref[...], lb1_ref[...], on_ref[...], layer, t_ref[...], e_ref[...])
        y_ref[...] = y.astype(BF16)
        st_ref[...] = st_new

    return pl.pallas_call(
        body,
        name=f"hgrn_fwd_{layer}",
        grid=(nc,),
        in_specs=[
            pl.BlockSpec((rows, 4 * B_WIDTH), lambda n: (n, 1)),
            _full((1, B_WIDTH)), _full((1, B_WIDTH)), _full((1, B_WIDTH)),
            _full(((len(B_LEVELS) + 1) * CHUNK, CHUNK)), _full((B_WIDTH, B_WIDTH)),
        ],
        out_specs=[
            pl.BlockSpec((rows, B_WIDTH), lambda n: (n, 0)),
            pl.BlockSpec((1, B_WIDTH, B_WIDTH), lambda n: (n, 0, 0)),
        ],
        out_shape=[jax.ShapeDtypeStruct((seq, B_WIDTH), BF16), jax.ShapeDtypeStruct((nc, B_WIDTH, B_WIDTH), F32)],
        scratch_shapes=[pltpu.VMEM((B_WIDTH, B_WIDTH), F32)],
        compiler_params=_cparams("arbitrary"),
    )(proj, lb0, lb1, onorm, tstack, ones_bd)


def hgrn_bwd(proj, states, dy, lb0, lb1, onorm, layer, dproj):
    seq = proj.shape[0]
    rows = min(B_ROWS, seq)
    nc = seq // rows
    tstack, ones_bd = _hgrn_consts()

    def body(x_ref, st_in_ref, dy_ref, lb0_ref, lb1_ref, on_ref, t_ref, e_ref, _, dx_ref, d0_ref, d1_ref, don_ref, dst_ref):
        @pl.when(pl.program_id(0) == 0)
        def _():
            dst_ref[...] = jnp.zeros_like(dst_ref)
            d0_ref[...] = jnp.zeros_like(d0_ref)
            d1_ref[...] = jnp.zeros_like(d1_ref)
            don_ref[...] = jnp.zeros_like(don_ref)

        fn = functools.partial(_hgrn_rows, layer=layer, tstack=t_ref[...], ones_bd=e_ref[...])
        _, vjp = jax.vjp(fn, x_ref[...], st_in_ref[0], lb0_ref[...], lb1_ref[...], on_ref[...])
        dx, dst, d0, d1, don = vjp((dy_ref[...], dst_ref[...]))
        dx_ref[...] = dx.astype(BF16)
        dst_ref[...] = dst
        d0_ref[...] += d0
        d1_ref[...] += d1
        don_ref[...] += don

    rev = lambda n: nc - 1 - n
    return pl.pallas_call(
        body,
        name=f"hgrn_bwd_{layer}",
        grid=(nc,),
        in_specs=[
            pl.BlockSpec((rows, 4 * B_WIDTH), lambda n: (rev(n), 1)),
            pl.BlockSpec((1, B_WIDTH, B_WIDTH), lambda n: (rev(n), 0, 0)),
            pl.BlockSpec((rows, B_WIDTH), lambda n: (rev(n), 1)),
            _full((1, B_WIDTH)), _full((1, B_WIDTH)), _full((1, B_WIDTH)),
            _full(((len(B_LEVELS) + 1) * CHUNK, CHUNK)), _full((B_WIDTH, B_WIDTH)), _ANY,
        ],
        out_specs=[
            pl.BlockSpec((rows, 4 * B_WIDTH), lambda n: (rev(n), 1)),
            _full((1, B_WIDTH)), _full((1, B_WIDTH)), _full((1, B_WIDTH)),
        ],
        out_shape=[jax.ShapeDtypeStruct(dproj.shape, BF16)] + [jax.ShapeDtypeStruct((1, B_WIDTH), F32)] * 3,
        input_output_aliases={8: 0},
        scratch_shapes=[pltpu.VMEM((B_WIDTH, B_WIDTH), F32)],
        compiler_params=_cparams("arbitrary"),
    )(proj, states, dy, lb0, lb1, onorm, tstack, ones_bd, dproj)


D_MODEL = 1024
D_INT = 4096


def _rms_stats(xf):
    r = lax.rsqrt(jnp.mean(xf * xf, axis=-1, keepdims=True) + NORM_EPS)
    return r, xf * r


def _rms_bwd(dy, g, r, xh):
    u = dy * g
    return r * (u - xh * jnp.mean(u * xh, axis=-1, keepdims=True))


def inproj(x, g, w, layer):
    seq = x.shape[0]
    tm = min(seq, 512)

    def body(x_ref, g_ref, w_ref, p_ref, h_ref):
        _, xh = _rms_stats(x_ref[...])
        h = (xh * g_ref[...]).astype(BF16)
        h_ref[...] = h
        p_ref[...] = _dot(h, w_ref[0])

    return pl.pallas_call(
        body,
        name="inproj",
        grid=(seq // tm,),
        in_specs=[
            pl.BlockSpec((tm, D_MODEL), lambda i: (i, 0)),
            _full((1, D_MODEL)),
            pl.BlockSpec((1, D_MODEL, D_INT), lambda i: (layer, 0, 0)),
        ],
        out_specs=[pl.BlockSpec((tm, D_INT), lambda i: (i, 0)), pl.BlockSpec((tm, D_MODEL), lambda i: (i, 0))],
        out_shape=[jax.ShapeDtypeStruct((seq, D_INT), F32), jax.ShapeDtypeStruct((seq, D_MODEL), BF16)],
        compiler_params=_cparams("parallel"),
    )(x, g, w)


def outproj(x, ya, yb, o, proj, wo, layer, head=None):
    seq = x.shape[0]
    tm = min(seq, 512)
    blk = wo.shape[2]

    def body(x_ref, ya_ref, yb_ref, o_ref, z_ref, w_ref, *refs):
        yc = (o_ref[...] * jax.nn.silu(z_ref[...])).astype(BF16)
        y = jnp.concatenate([ya_ref[...], yb_ref[...], yc], axis=1)
        w = jnp.concatenate([w_ref[d, 0] for d in range(N_DEV)], axis=0)
        xn = x_ref[...] + _dot(y, w)
        if head is None:
            xn_ref, y_ref = refs
            xn_ref[...] = xn
        else:
            g_ref, t_ref, dx_ref, y_ref, dg_ref, loss_ref = refs

            @pl.when(pl.program_id(0) == 0)
            def _():
                dg_ref[...] = jnp.zeros_like(dg_ref)
                loss_ref[...] = jnp.zeros_like(loss_ref)

            g = g_ref[...]
            r, xh = _rms_stats(xn)
            err = xh * g - t_ref[...]
            sq = jnp.sum(jnp.sum(err * err, axis=1, keepdims=True), axis=0, keepdims=True)
            loss_ref[...] += jnp.broadcast_to(sq * (0.5 / D_MODEL), loss_ref.shape)
            dout = err * (1.0 / D_MODEL)
            dg_ref[...] += jnp.sum(dout * xh, axis=0, keepdims=True)
            dx_ref[...] = _rms_bwd(dout, g, r, xh)
        y_ref[...] = y

    rows = lambda: pl.BlockSpec((tm, D_MODEL), lambda i: (i, 0))
    tail = (() if head is None else (_full((1, D_MODEL)), rows()),
            () if head is None else (_full((1, D_MODEL)), _full((8, LANES))),
            () if head is None else (jax.ShapeDtypeStruct((1, D_MODEL), F32), jax.ShapeDtypeStruct((8, LANES), F32)))
    return pl.pallas_call(
        body,
        name="outproj" if head is None else "outproj_loss",
        grid=(seq // tm,),
        in_specs=[
            rows(),
            pl.BlockSpec((tm, 256), lambda i: (i, 0)),
            pl.BlockSpec((tm, 256), lambda i: (i, 0)),
            pl.BlockSpec((tm, 512), lambda i: (i, 0)),
            pl.BlockSpec((tm, 512), lambda i: (i, 7)),
            pl.BlockSpec((N_DEV, 1, blk, D_MODEL), lambda i: (0, layer, 0, 0)),
            *tail[0],
        ],
        out_specs=[rows(), rows(), *tail[1]],
        out_shape=[jax.ShapeDtypeStruct((seq, D_MODEL), F32), jax.ShapeDtypeStruct((seq, D_MODEL), BF16), *tail[2]],
        compiler_params=_cparams("parallel" if head is None else "arbitrary"),
    )(x, ya, yb, o, proj, wo, *(head or ()))


def outproj_bwd(dx, y, wo, layer):
    seq = dx.shape[0]
    ts = min(seq, 512)
    blk = wo.shape[2]

    def body(dx_ref, y_ref, w_ref, dy_ref, dw_ref):
        @pl.when(pl.program_id(0) == 0)
        def _():
            dw_ref[...] = jnp.zeros_like(dw_ref)

        dxb = dx_ref[...].astype(BF16)
        w = jnp.concatenate([w_ref[d, 0] for d in range(N_DEV)], axis=0)
        dy_ref[...] = lax.dot_general(dxb, w, _NT, preferred_element_type=F32)
        dw = lax.dot_general(y_ref[...], dxb, _TN, preferred_element_type=F32)
        for d in range(N_DEV):
            dw_ref[d % 2, d // 2] += dw[d * blk:(d + 1) * blk]

    return pl.pallas_call(
        body,
        name="outproj_bwd",
        grid=(seq // ts,),
        in_specs=[
            pl.BlockSpec((ts, D_MODEL), lambda i: (i, 0)),
            pl.BlockSpec((ts, D_MODEL), lambda i: (i, 0)),
            pl.BlockSpec((N_DEV, 1, blk, D_MODEL), lambda i: (0, layer, 0, 0)),
        ],
        out_specs=[pl.BlockSpec((ts, D_MODEL), lambda i: (i, 0)),
                   pl.BlockSpec((2, N_CHIP, blk, D_MODEL), lambda i: (0, 0, 0, 0))],
        out_shape=[jax.ShapeDtypeStruct((seq, D_MODEL), F32), jax.ShapeDtypeStruct((2, N_CHIP, blk, D_MODEL), F32)],
        compiler_params=_cparams("arbitrary"),
    )(dx, y, wo)


C_QKV = (2048, 3584)


def _dproj_parts(dp_ref, dqkv_refs, rows):
    lo, hi = C_QKV
    step = (hi - lo) // len(dqkv_refs)
    return ([(0, dp_ref.at[rows, 0:lo])] + [(lo + i * step, r.at[rows, :]) for i, r in enumerate(dqkv_refs)]
            + [(hi, dp_ref.at[rows, hi:D_INT])])


def inproj_bwd_x(dproj, dqkv, w, x, g, dx_in, layer, carried=None):
    seq = x.shape[0]
    tm = min(seq, 512)

    def body(dp_ref, dq_ref, dk_ref, dv_ref, w_ref, x_ref, g_ref, dxin_ref, dx_ref, dg_ref):
        @pl.when(pl.program_id(0) == 0)
        def _():
            dg_ref[...] = jnp.zeros_like(dg_ref)

        dh = None
        for at, part in _dproj_parts(dp_ref, (dq_ref, dk_ref, dv_ref), slice(None)):
            term = lax.dot_general(part[...], w_ref[0, :, at:at + part.shape[1]], _NT, preferred_element_type=F32)
            dh = term if dh is None else dh + term
        r, xh = _rms_stats(x_ref[...])
        dg_ref[...] += jnp.sum(dh * xh, axis=0, keepdims=True)
        dx_ref[...] = dxin_ref[...] + _rms_bwd(dh, g_ref[...], r, xh)

    third = lambda: pl.BlockSpec((tm, C_WIDTH), lambda i: (i, 0))
    return _call_carrying(
        carried, body, (dproj, *dqkv, w, x, g, dx_in),
        name="inproj_bwd_x",
        grid=(seq // tm,),
        in_specs=[
            pl.BlockSpec((tm, D_INT), lambda i: (i, 0)), third(), third(), third(),
            pl.BlockSpec((1, D_MODEL, D_INT), lambda i: (layer, 0, 0)),
            pl.BlockSpec((tm, D_MODEL), lambda i: (i, 0)),
            _full((1, D_MODEL)),
            pl.BlockSpec((tm, D_MODEL), lambda i: (i, 0)),
        ],
        out_specs=[pl.BlockSpec((tm, D_MODEL), lambda i: (i, 0)), _full((1, D_MODEL))],
        out_shape=[jax.ShapeDtypeStruct((seq, D_MODEL), F32), jax.ShapeDtypeStruct((1, D_MODEL), F32)],
        scratch_shapes=[], semantics=("arbitrary",),
    )


def inproj_bwd_w(h, dproj, dqkv):
    seq = h.shape[0]
    ts, tn = min(seq, 512), 512

    def body(h_ref, dp_ref, dq_ref, dk_ref, dv_ref, dw_ref):
        @pl.when(pl.program_id(0) == 0)
        def _():
            dw_ref[...] = jnp.zeros_like(dw_ref)

        ht = h_ref[...].T
        for at, part in _dproj_parts(dp_ref, (dq_ref, dk_ref, dv_ref), slice(None)):
            for c in range(0, part.shape[1], tn):
                dw_ref[0, :, at + c:at + c + tn] += _dot(ht, part[:, c:c + tn])

    third = lambda: pl.BlockSpec((ts, C_WIDTH), lambda s: (s, 0))
    return pl.pallas_call(
        body,
        name="inproj_bwd_w",
        grid=(seq // ts,),
        in_specs=[pl.BlockSpec((ts, D_MODEL), lambda s: (s, 0)), pl.BlockSpec((ts, D_INT), lambda s: (s, 0)),
                  third(), third(), third()],
        out_specs=_full((1, D_MODEL, D_INT)),
        out_shape=jax.ShapeDtypeStruct((1, D_MODEL, D_INT), F32),
        compiler_params=_cparams("arbitrary"),
    )(h, dproj, *dqkv)


N_IN = 3848


def _internal_of(col):
    return col if col < 768 else (col + 256 if col < 3840 else 768 + col - 3840)


def _column_runs(n_shard):
    runs = []
    for d in range(N_IN // n_shard):
        mine = []
        for j in range(n_shard):
            ci = _internal_of(d * n_shard + j)
            if mine and mine[-1][0] + mine[-1][1] == ci:
                mine[-1][1] += 1
            else:
                mine.append([ci, 1, j])
        runs.append(mine)
    return runs


def assemble_w_in(wi_all):
    n_dev, depth, _, n_shard = wi_all.shape
    tr = 256
    pieces = [[] for _ in range(D_INT // LANES)]
    for d, mine in enumerate(_column_runs(n_shard)):
        for ci, ln, off in mine:
            while ln > 0:
                blk, at = divmod(ci, LANES)
                take = min(ln, LANES - at)
                pieces[blk].append((at, take, d, off))
                ci, ln, off = ci + take, ln - take, off + take

    def body(x_ref, o_ref):
        for blk, parts in enumerate(pieces):
            vals, at = [], 0
            for start, ln, d, off in sorted(parts):
                if start > at:
                    vals.append(jnp.zeros((tr, start - at), BF16))
                vals.append(x_ref[d, 0, :, off:off + ln])
                at = start + ln
            if at < LANES:
                vals.append(jnp.zeros((tr, LANES - at), BF16))
            o_ref[0, :, blk * LANES:(blk + 1) * LANES] = vals[0] if len(vals) == 1 else jnp.concatenate(vals, axis=1)

    return pl.pallas_call(
        body,
        name="assemble_w_in",
        grid=(depth, D_MODEL // tr),
        in_specs=[pl.BlockSpec((n_dev, 1, tr, n_shard), lambda l, r: (0, l, r, 0))],
        out_specs=pl.BlockSpec((1, tr, D_INT), lambda l, r: (l, r, 0)),
        out_shape=jax.ShapeDtypeStruct((depth, D_MODEL, D_INT), BF16),
        compiler_params=_cparams("parallel", "parallel"),
    )(wi_all)


def split_w_in_grad(dwi, n_shard, core):
    tr = 256
    runs = _column_runs(n_shard)

    def body(core_ref, x_ref, keep_ref, send_ref):
        for d, mine in enumerate(runs):
            @pl.when(core_ref[0] == d % 2)
            def _():
                for ci, ln, off in mine:
                    keep_ref[d // 2, :, off:off + ln] = x_ref[0, :, ci:ci + ln]

            @pl.when(core_ref[0] != d % 2)
            def _():
                for ci, ln, off in mine:
                    send_ref[d // 2, :, off:off + ln] = x_ref[0, :, ci:ci + ln].astype(BF16)

    shards = lambda: pl.BlockSpec((N_CHIP, tr, n_shard), lambda r, s: (0, r, 0))
    grid_spec = pltpu.PrefetchScalarGridSpec(
        num_scalar_prefetch=1, grid=(D_MODEL // tr,),
        in_specs=[pl.BlockSpec((1, tr, D_INT), lambda r, s: (0, r, 0))], out_specs=[shards(), shards()])
    return pl.pallas_call(
        body,
        name="split_w_in_grad",
        grid_spec=grid_spec,
        out_shape=[jax.ShapeDtypeStruct((N_CHIP, D_MODEL, n_shard), F32), jax.ShapeDtypeStruct((N_CHIP, D_MODEL, n_shard), BF16)],
        compiler_params=_cparams("parallel"),
    )(core, dwi)


C_WIDTH = 512
C_HEADS = 8
C_HDIM = 64
C_PAIRS = C_HEADS // 2
C_BQ = 512
C_TAIL = 16
C_KG = 4


def _split3(x):
    hi = x.astype(BF16)
    r = x - hi.astype(F32)
    mid = r.astype(BF16)
    return hi, mid, (r - mid.astype(F32)).astype(BF16)


def _piece_selectors():
    sel = np.zeros((C_HEADS, 3 * LANES, LANES), np.float32)
    for p in range(C_PAIRS):
        for e in range(2):
            for t in range(3):
                sel[2 * p + e, t * LANES + 2 * p + e, 3 * e + t] = -1.0
    return sel


def fox_prep(proj, bf_row):
    seq = proj.shape[0]
    nblk = seq // CHUNK
    tril = jnp.asarray(np.tril(np.ones((CHUNK, CHUNK), np.float32)), BF16)
    sel = jnp.asarray(_piece_selectors(), BF16)
    rows_t = CHUNK + C_TAIL

    def body(fl_ref, q_ref, k_ref, v_ref, bf_ref, l_ref, sel_ref, ka_ref, va_ref, vt_ref, kt_ref, qt_ref, qa_ref, carry_ref):
        @pl.when(pl.program_id(0) == 0)
        def _():
            carry_ref[...] = jnp.zeros_like(carry_ref)

        lf = jax.nn.log_sigmoid(fl_ref[:, :LANES] + bf_ref[...])
        c = _exact_times(l_ref[...], lf, 3) + carry_ref[...]
        carry_ref[...] += jnp.sum(lf, axis=0, keepdims=True)
        c3 = jnp.concatenate(_split3(c), axis=1)
        lane = lax.broadcasted_iota(jnp.int32, (CHUNK, LANES), 1)
        row = lax.broadcasted_iota(jnp.int32, (CHUNK, LANES), 0)
        r16 = lax.broadcasted_iota(jnp.int32, (C_TAIL, 2 * CHUNK), 0)
        l16 = lax.broadcasted_iota(jnp.int32, (C_TAIL, 2 * CHUNK), 1)
        zero = jnp.zeros((CHUNK, LANES), BF16)
        one = jnp.ones((CHUNK, LANES), BF16)

        def by_keys(x, right_a, right_b):
            xb = x.astype(BF16)
            top = jnp.concatenate([jnp.where(lane < C_HDIM, xb, zero), right_a], axis=1)
            return jnp.concatenate([top, jnp.concatenate([jnp.where(lane < C_HDIM, zero, xb), right_b], axis=1)], axis=0)

        def by_lanes(x, tail):
            xt = x.T.astype(BF16)
            main = jnp.concatenate([jnp.where(row < C_HDIM, xt, zero), jnp.where(row < C_HDIM, zero, xt)], axis=1)
            return jnp.concatenate([main, tail], axis=0)

        for p in range(C_PAIRS):
            cols = slice(p * LANES, (p + 1) * LANES)
            q2, k2, v2 = q_ref[:, cols] * (C_HDIM ** -0.5), k_ref[:, cols], v_ref[:, cols]
            negc = [_dot(c3, sel_ref[2 * p + e]).astype(BF16) for e in range(2)]
            ones3 = [jnp.where((lane >= 3 * e) & (lane < 3 * e + 3), one, zero) for e in range(2)]
            tail = jnp.where(((r16 == 2 * p) & (l16 < CHUNK)) | ((r16 == 2 * p + 1) & (l16 >= CHUNK)), 1.0, 0.0).astype(BF16)
            ka_ref[p] = by_keys(k2, negc[0], negc[1])
            va_ref[p] = by_keys(v2, ones3[0], ones3[1])
            kt_ref[p] = by_lanes(k2, tail)
            vt_ref[p] = by_lanes(v2, tail)
            qt_ref[p] = jnp.concatenate([q2.T.astype(BF16), jnp.where(row < 6, one, zero)], axis=0)
            qa_ref[p] = jnp.concatenate([q2.astype(BF16), jnp.where((lane == 2 * p) | (lane == 2 * p + 1), one, zero)], axis=1)

    wide = lambda j: pl.BlockSpec((CHUNK, C_WIDTH), lambda n: (n, j))
    by_rows = pl.BlockSpec((C_PAIRS, 2 * CHUNK, 2 * CHUNK), lambda n: (0, n, 0))
    by_cols = pl.BlockSpec((C_PAIRS, rows_t, 2 * CHUNK), lambda n: (0, 0, n))
    return pl.pallas_call(
        body,
        name="fox_prep",
        grid=(nblk,),
        in_specs=[pl.BlockSpec((CHUNK, 256), lambda n: (n, 3)), wide(4), wide(5), wide(6), _full((1, LANES)),
                  _full((CHUNK, CHUNK)), _full((C_HEADS, 3 * LANES, LANES))],
        out_specs=[by_rows, by_rows, by_cols, by_cols,
                   pl.BlockSpec((C_PAIRS, 2 * CHUNK, CHUNK), lambda n: (0, 0, n)),
                   pl.BlockSpec((C_PAIRS, CHUNK, 2 * CHUNK), lambda n: (0, n, 0))],
        out_shape=[jax.ShapeDtypeStruct((C_PAIRS, 2 * seq, 2 * CHUNK), BF16)] * 2
        + [jax.ShapeDtypeStruct((C_PAIRS, rows_t, 2 * seq), BF16)] * 2
        + [jax.ShapeDtypeStruct((C_PAIRS, 2 * CHUNK, seq), BF16), jax.ShapeDtypeStruct((C_PAIRS, seq, 2 * CHUNK), BF16)],
        scratch_shapes=[pltpu.VMEM((1, LANES), F32)],
        compiler_params=_cparams("arbitrary"),
    )(proj, proj, proj, proj, bf_row, tril, sel)


def _visible(shape, key0, query0):
    row = lax.broadcasted_iota(jnp.int32, shape, 0)
    key = key0 + lax.shift_left(lax.shift_right_logical(row, 8), 7) + (row & (CHUNK - 1))
    return key <= query0 + lax.broadcasted_iota(jnp.int32, shape, 1)


def _rows_ab(a, b, n):
    return jnp.concatenate([jnp.broadcast_to(a, (C_HDIM, n)), jnp.broadcast_to(b, (C_HDIM, n))], axis=0)


def _call_carrying(ex, body, operands, *, name, grid, in_specs, out_specs, out_shape, scratch_shapes, semantics=None):
    if ex is None:
        semantics = semantics or ("parallel", *["arbitrary"] * (len(grid) - 1))
        return pl.pallas_call(body, name=name, grid=grid, in_specs=in_specs, out_specs=out_specs, out_shape=out_shape,
                              scratch_shapes=scratch_shapes, compiler_params=_cparams(*semantics))(*operands)
    n_in, n_out = len(in_specs), len(out_specs)

    def wrapped(*refs):
        own, parts = _carried_refs(refs, n_in, n_out, ex)
        ids = [pl.program_id(a) for a in range(len(grid))]
        pl.when(functools.reduce(jnp.logical_and, [i == 0 for i in ids]))(lambda: ex.start(*parts))
        if hasattr(ex, "relay"):
            linear = functools.reduce(lambda at, ig: at * ig[1] + ig[0], zip(ids, grid), 0)
            pl.when(linear == int(np.prod(grid)) // 2)(lambda: ex.relay(*parts))
        body(*own)
        pl.when(functools.reduce(jnp.logical_and, [i == g - 1 for i, g in zip(ids, grid)]))(lambda: ex.finish(*parts))

    return pl.pallas_call(
        wrapped, name=name, grid=grid,
        in_specs=list(in_specs) + [_ANY] * len(ex.inputs), out_specs=list(out_specs) + [_ANY] * len(ex.out_shape),
        out_shape=list(out_shape) + list(ex.out_shape), scratch_shapes=list(scratch_shapes) + list(ex.scratch),
        input_output_aliases={n_in + i: n_out + o for i, o in getattr(ex, "aliases", {}).items()},
        compiler_params=_cparams(*["arbitrary"] * len(grid)),
    )(*operands, *ex.inputs)


def fox_fwd(qt, ka, vt, carried=None):
    seq = qt.shape[2]
    nblk = seq // CHUNK
    bq = min(C_BQ, seq)
    grp = bq // CHUNK
    rows_t = CHUNK + C_TAIL

    def body(qt_ref, ka_ref, vt_ref, o_ref, lse_ref, acc_ref, s_ref):
        p, i = pl.program_id(0), pl.program_id(1)
        qtile = qt_ref[0]
        r16 = lax.broadcasted_iota(jnp.int32, (C_TAIL, bq), 0)

        def scores(t):
            at = pl.multiple_of(t * grp * 2 * CHUNK, 2 * CHUNK)
            return _dot(ka_ref[0, pl.ds(at, grp * 2 * CHUNK), :], qtile)

        def rescale(al_a, al_b):
            tail = jnp.where(r16 == 2 * p, al_a, jnp.where(r16 == 2 * p + 1, al_b, 1.0))
            return jnp.concatenate([_rows_ab(al_a, al_b, bq), tail], axis=0)

        def diagonal(m):
            ma, mb = m
            na, nb = ma, mb
            blocks = []
            for g in range(grp):
                s = s_ref[g * 2 * CHUNK:(g + 1) * 2 * CHUNK, g * CHUNK:]
                s = jnp.where(_visible(s.shape, i * bq + g * CHUNK, i * bq + g * CHUNK), s, -jnp.inf)
                blocks.append(s)
                unseen = [jnp.full((1, g * CHUNK), -jnp.inf, F32)] if g else []
                na = jnp.maximum(na, jnp.concatenate(unseen + [jnp.max(s[:CHUNK], axis=0, keepdims=True)], axis=1))
                nb = jnp.maximum(nb, jnp.concatenate(unseen + [jnp.max(s[CHUNK:], axis=0, keepdims=True)], axis=1))
            acc_ref[...] = acc_ref[...] * rescale(jnp.exp(ma - na), jnp.exp(mb - nb))
            for g in range(grp):
                n = bq - g * CHUNK
                n2 = jnp.concatenate([jnp.broadcast_to(na[:, g * CHUNK:], (CHUNK, n)),
                                      jnp.broadcast_to(nb[:, g * CHUNK:], (CHUNK, n))], axis=0)
                at = pl.multiple_of((i * grp + g) * 2 * CHUNK, 2 * CHUNK)
                pt = jnp.exp(blocks[g] - n2).astype(BF16)
                acc_ref[:, g * CHUNK:] += _dot(vt_ref[0, :, pl.ds(at, 2 * CHUNK)], pt)
            return na, nb

        def group(t, m):
            ma, mb = m
            at = pl.multiple_of(t * grp * 2 * CHUNK, 2 * CHUNK)
            s = s_ref[...]
            sa = [s[g * 2 * CHUNK:g * 2 * CHUNK + CHUNK] for g in range(grp)]
            sb = [s[g * 2 * CHUNK + CHUNK:(g + 1) * 2 * CHUNK] for g in range(grp)]
            na, nb = ma, mb
            for g in range(grp):
                na = jnp.maximum(na, jnp.max(sa[g], axis=0, keepdims=True))
                nb = jnp.maximum(nb, jnp.max(sb[g], axis=0, keepdims=True))
            al_a, al_b = jnp.exp(ma - na), jnp.exp(mb - nb)
            pt = jnp.concatenate([jnp.exp(x - n) for g in range(grp) for x, n in ((sa[g], na), (sb[g], nb))], axis=0)
            pv = _dot(vt_ref[0, :, pl.ds(at, grp * 2 * CHUNK)], pt.astype(BF16))
            acc_ref[...] = acc_ref[...] * rescale(al_a, al_b) + pv
            return na, nb

        def step(t, m):
            s_next = scores(t + 1)
            m = group(t, m)
            s_ref[...] = s_next
            return m

        acc_ref[...] = jnp.zeros_like(acc_ref)
        s_ref[...] = scores(0)
        m = (jnp.full((1, bq), -jnp.inf, F32), jnp.full((1, bq), -jnp.inf, F32))
        m = lax.fori_loop(0, i, step, m)
        ma, mb = diagonal(m)
        tailv = acc_ref[CHUNK:rows_t, :]
        la = jnp.sum(jnp.where(r16 == 2 * p, tailv, 0.0), axis=0, keepdims=True)
        lb = jnp.sum(jnp.where(r16 == 2 * p + 1, tailv, 0.0), axis=0, keepdims=True)
        o_ref[...] = (acc_ref[0:CHUNK, :] * _rows_ab(1.0 / la, 1.0 / lb, bq)).T
        lse_ref[0, 0:1, :] = ma + jnp.log(la)
        lse_ref[0, 1:2, :] = mb + jnp.log(lb)

    return _call_carrying(
        carried, body, (qt, ka, vt),
        name="fox_fwd",
        grid=(C_PAIRS, seq // bq),
        in_specs=[
            pl.BlockSpec((1, 2 * CHUNK, bq), lambda p, i: (p, 0, i)),
            pl.BlockSpec((1, 2 * seq, 2 * CHUNK), lambda p, i: (p, 0, 0)),
            pl.BlockSpec((1, rows_t, 2 * seq), lambda p, i: (p, 0, 0)),
        ],
        out_specs=[pl.BlockSpec((bq, LANES), lambda p, i: (i, p)), pl.BlockSpec((1, 2, bq), lambda p, i: (p, 0, i))],
        out_shape=[jax.ShapeDtypeStruct((seq, C_WIDTH), F32), jax.ShapeDtypeStruct((C_PAIRS, 2, seq), F32)],
        scratch_shapes=[pltpu.VMEM((rows_t, bq), F32), pltpu.VMEM((grp * 2 * CHUNK, bq), F32)],
    )


def fox_bwd_prep(dy, o, proj, dproj):
    seq = o.shape[0]
    ind = np.zeros((C_WIDTH, LANES), np.float32)
    for h in range(C_HEADS):
        ind[h * C_HDIM:(h + 1) * C_HDIM, h] = 1.0
    ind = jnp.asarray(ind, BF16)
    sel = _piece_selectors()
    sel = jnp.asarray(np.stack([sel[2 * p].T + sel[2 * p + 1].T for p in range(C_PAIRS)]), BF16)

    def body(dy_ref, o_ref, z_ref, ind_ref, sel_ref, _, do_ref, dz_ref, dot_ref):
        dy_c, o_v, z = dy_ref[...], o_ref[...], z_ref[...]
        sg = jax.nn.sigmoid(z)
        do = dy_c * (z * sg)
        do_ref[...] = do.astype(BF16)
        dz_ref[...] = (dy_c * o_v * (sg * (1.0 + z * (1.0 - sg)))).astype(BF16)
        prod = do * o_v
        hi = prod.astype(BF16)
        lo = (prod - hi.astype(F32)).astype(BF16)
        delta = _dot(hi, ind_ref[...]) + _dot(lo, ind_ref[...])
        d3 = jnp.concatenate(_split3(delta.T), axis=0)
        for p in range(C_PAIRS):
            tail = _dot(sel_ref[p], d3).astype(BF16)
            dot_ref[p] = jnp.concatenate([do[:, p * LANES:(p + 1) * LANES].T.astype(BF16), tail], axis=0)

    return pl.pallas_call(
        body,
        name="fox_bwd_prep",
        grid=(seq // CHUNK,),
        in_specs=[
            pl.BlockSpec((CHUNK, C_WIDTH), lambda i: (i, 1)),
            pl.BlockSpec((CHUNK, C_WIDTH), lambda i: (i, 0)),
            pl.BlockSpec((CHUNK, C_WIDTH), lambda i: (i, 7)),
            _full((C_WIDTH, LANES)), _full((C_PAIRS, LANES, 3 * LANES)), _ANY,
        ],
        out_specs=[
            pl.BlockSpec((CHUNK, C_WIDTH), lambda i: (i, 0)),
            pl.BlockSpec((CHUNK, C_WIDTH), lambda i: (i, 7)),
            pl.BlockSpec((C_PAIRS, 2 * CHUNK, CHUNK), lambda i: (0, 0, i)),
        ],
        out_shape=[jax.ShapeDtypeStruct((seq, C_WIDTH), BF16), jax.ShapeDtypeStruct(dproj.shape, BF16),
                   jax.ShapeDtypeStruct((C_PAIRS, 2 * CHUNK, seq), BF16)],
        input_output_aliases={5: 1},
        compiler_params=_cparams("parallel"),
    )(dy, o, proj, ind, sel, dproj)


def fox_bwd(ka, va, kt, qt, dot_t, qa, dob, lse, carried=None):
    seq = qt.shape[2]
    nblk = seq // CHUNK
    bq = min(C_BQ, seq)
    nq = seq // bq
    kg = min(C_KG, nblk)
    ng = nblk // kg
    rows_t = CHUNK + C_TAIL

    def body(ka_ref, va_ref, kt_ref, qt_ref, dot_ref, qa_ref, do_ref, lse_ref,
             dq_ref, dk_ref, dv_ref, dck_ref, dcq_ref, dqt_acc, dv_acc, dka_acc):
        p, jg = pl.program_id(0), pl.program_id(1)

        @pl.when(jg == 0)
        def _():
            dqt_acc[...] = jnp.zeros_like(dqt_acc)

        dv_acc[...] = jnp.zeros_like(dv_acc)
        dka_acc[...] = jnp.zeros_like(dka_acc)

        def step(i, carry):
            cols = pl.ds(pl.multiple_of(i * bq, bq), bq)
            qtile, dotile = qt_ref[0, :, cols], dot_ref[0, :, cols]
            do, qa_i = do_ref[cols, :], qa_ref[0, cols, :]
            lse2 = jnp.concatenate([jnp.broadcast_to(lse_ref[0, 0:1, cols], (CHUNK, bq)),
                                    jnp.broadcast_to(lse_ref[0, 1:2, cols], (CHUNK, bq))] * kg, axis=0)
            pt = jnp.exp(_dot(ka_ref[0], qtile) - lse2)
            ds = pt * _dot(va_ref[0], dotile)
            ptb, dsb = pt.astype(BF16), ds.astype(BF16)
            dv_acc[...] += _dot(ptb, do)
            dka_acc[...] += _dot(dsb, qa_i)
            dqt_acc[:, cols] += _dot(kt_ref[0], dsb)
            return carry

        def diagonal(i):
            cols = [pl.ds(pl.multiple_of(i * bq + kb * CHUNK, CHUNK), bq - kb * CHUNK) for kb in range(kg)]
            rows = [slice(kb * 2 * CHUNK, (kb + 1) * 2 * CHUNK) for kb in range(kg)]
            s = [_dot(ka_ref[0, rows[kb], :], qt_ref[0, :, cols[kb]]) for kb in range(kg)]
            dp = [_dot(va_ref[0, rows[kb], :], dot_ref[0, :, cols[kb]]) for kb in range(kg)]
            ptb, dsb = [], []
            for kb in range(kg):
                n = bq - kb * CHUNK
                lse2 = jnp.concatenate([jnp.broadcast_to(lse_ref[0, 0:1, cols[kb]], (CHUNK, n)),
                                        jnp.broadcast_to(lse_ref[0, 1:2, cols[kb]], (CHUNK, n))], axis=0)
                pt = jnp.exp(s[kb] - lse2)
                pt = jnp.where(_visible(pt.shape, (jg * kg + kb) * CHUNK, i * bq + kb * CHUNK), pt, 0.0)
                ptb.append(pt.astype(BF16))
                dsb.append((pt * dp[kb]).astype(BF16))
            for kb in range(kg):
                dv_acc[rows[kb], :] += _dot(ptb[kb], do_ref[cols[kb], :])
                dka_acc[rows[kb], :] += _dot(dsb[kb], qa_ref[0, cols[kb], :])
                dqt_acc[:, cols[kb]] += _dot(kt_ref[0, :, rows[kb]], dsb[kb])

        assert kg * CHUNK == bq
        diagonal(jg)
        lax.fori_loop(jg + 1, nq, step, 0)
        lane = lax.broadcasted_iota(jnp.int32, (CHUNK, LANES), 1)
        for kb in range(kg):
            rows = slice(kb * CHUNK, (kb + 1) * CHUNK)
            ra = slice(kb * 2 * CHUNK, kb * 2 * CHUNK + CHUNK)
            rb = slice(kb * 2 * CHUNK + CHUNK, (kb + 1) * 2 * CHUNK)
            dk_ref[rows, :] = jnp.where(lane < C_HDIM, dka_acc[ra, 0:LANES], dka_acc[rb, 0:LANES]).astype(BF16)
            dv_ref[rows, :] = jnp.where(lane < C_HDIM, dv_acc[ra, :], dv_acc[rb, :]).astype(BF16)
            dck_ref[0, rows, :] = (jnp.where(lane == 2 * p, dka_acc[ra, LANES:], 0.0)
                                   + jnp.where(lane == 2 * p + 1, dka_acc[rb, LANES:], 0.0))

        @pl.when(jg == ng - 1)
        def _():
            for c in range(nq):
                dq_ref[c * bq:(c + 1) * bq, :] = (dqt_acc[0:CHUNK, c * bq:(c + 1) * bq].T * (C_HDIM ** -0.5)).astype(BF16)
            dcq_ref[0] = dqt_acc[CHUNK:rows_t, :]

    per_pair = lambda r, c: pl.BlockSpec((1, r, c), lambda p, j: (p, 0, 0))
    by_rows = pl.BlockSpec((1, kg * 2 * CHUNK, 2 * CHUNK), lambda p, j: (p, j, 0))
    by_cols = pl.BlockSpec((1, rows_t, kg * 2 * CHUNK), lambda p, j: (p, 0, j))
    return _call_carrying(
        carried, body, (ka, va, kt, qt, dot_t, qa, dob, lse),
        name="fox_bwd",
        grid=(C_PAIRS, ng),
        in_specs=[by_rows, by_rows, by_cols, per_pair(2 * CHUNK, seq), per_pair(2 * CHUNK, seq),
                  per_pair(seq, 2 * CHUNK), pl.BlockSpec((seq, LANES), lambda p, j: (0, p)), per_pair(2, seq)],
        out_specs=[pl.BlockSpec((seq, LANES), lambda p, j: (0, p)),
                   pl.BlockSpec((kg * CHUNK, LANES), lambda p, j: (j, p)),
                   pl.BlockSpec((kg * CHUNK, LANES), lambda p, j: (j, p)),
                   pl.BlockSpec((1, kg * CHUNK, LANES), lambda p, j: (p, j, 0)),
                   per_pair(C_TAIL, seq)],
        out_shape=[jax.ShapeDtypeStruct((seq, C_WIDTH), BF16)] * 3
        + [jax.ShapeDtypeStruct((C_PAIRS, seq, LANES), F32), jax.ShapeDtypeStruct((C_PAIRS, C_TAIL, seq), F32)],
        scratch_shapes=[pltpu.VMEM((rows_t, seq), F32), pltpu.VMEM((kg * 2 * CHUNK, LANES), F32),
                        pltpu.VMEM((kg * 2 * CHUNK, 2 * CHUNK), F32)],
    )


def fox_post(dcq, dck, proj, bf_row, dproj):
    seq = proj.shape[0]
    nc = seq // CHUNK
    triu = jnp.asarray(np.triu(np.ones((CHUNK, CHUNK), np.float32)), BF16)

    def body(dq_ref, dk_ref, fl_ref, bf_ref, u_ref, _, dfl_ref, dbf_ref, carry_ref):
        @pl.when(pl.program_id(0) == 0)
        def _():
            carry_ref[...] = jnp.zeros_like(carry_ref)
            dbf_ref[...] = jnp.zeros_like(dbf_ref)

        rows = (dq_ref[0] + dq_ref[1]) + (dq_ref[2] + dq_ref[3])
        dc = jnp.concatenate([rows, jnp.zeros((CHUNK - C_TAIL, CHUNK), F32)], axis=0).T
        dc = dc - ((dk_ref[0] + dk_ref[1]) + (dk_ref[2] + dk_ref[3]))
        g = _exact_times(u_ref[...], dc, 3) + carry_ref[...]
        carry_ref[...] += jnp.sum(dc, axis=0, keepdims=True)
        dfl = g * jax.nn.sigmoid(-(fl_ref[:, :LANES] + bf_ref[...]))
        dbf_ref[...] += jnp.sum(dfl, axis=0, keepdims=True)
        dfl_ref[...] = jnp.concatenate([dfl, jnp.zeros_like(dfl)], axis=1).astype(BF16)

    rev = lambda n: nc - 1 - n
    return pl.pallas_call(
        body,
        name="fox_post",
        grid=(nc,),
        in_specs=[
            pl.BlockSpec((C_PAIRS, C_TAIL, CHUNK), lambda n: (0, 0, rev(n))),
            pl.BlockSpec((C_PAIRS, CHUNK, LANES), lambda n: (0, rev(n), 0)),
            pl.BlockSpec((CHUNK, 256), lambda n: (rev(n), 3)),
            _full((1, LANES)), _full((CHUNK, CHUNK)), _ANY,
        ],
        out_specs=[pl.BlockSpec((CHUNK, 256), lambda n: (rev(n), 3)), _full((1, LANES))],
        out_shape=[jax.ShapeDtypeStruct(dproj.shape, BF16), jax.ShapeDtypeStruct((1, LANES), F32)],
        input_output_aliases={5: 0},
        scratch_shapes=[pltpu.VMEM((1, LANES), F32)],
        compiler_params=_cparams("arbitrary"),
    )(dcq, dck, proj, bf_row, triu, dproj)


N_DEV = 8
MESH = pl.DeviceIdType.MESH
_ANY = pl.BlockSpec(memory_space=pl.ANY)


def _mesh_pos():
    return lax.axis_index("x"), lax.axis_index("y"), lax.axis_index("c")


def _dev_index(px, py, pc):
    return 4 * px + 2 * py + pc


def _row_pieces(ref, rows):
    return [ref.at[idx + (pl.ds(r, rows),)] for idx in np.ndindex(*ref.shape[:-2]) for r in range(0, ref.shape[-2], rows)]


class _Transfer:
    def __init__(self, src, dst, rows, send_sem, recv_sem, to):
        self.src, self.dst, self.rows, self.sems, self.to = src, dst, rows, (send_sem, recv_sem), to

    def _copy(self, src, dst):
        return pltpu.make_async_remote_copy(src_ref=src, dst_ref=dst, send_sem=self.sems[0], recv_sem=self.sems[1],
                                            device_id=self.to, device_id_type=MESH)

    def start(self):
        for s, d in zip(_row_pieces(self.src, self.rows), _row_pieces(self.dst, self.rows), strict=True):
            self._copy(s, d).start()

    def wait_send(self):
        self._copy(self.src, self.dst).wait_send()

    def wait_recv(self):
        self._copy(self.src, self.dst).wait_recv()


def _exchange_call(ex, name):
    n_in, n_out = len(ex.inputs), len(ex.out_shape)

    def body(*refs):
        parts = refs[:n_in], refs[n_in:n_in + n_out], refs[n_in + n_out:]
        ex.start(*parts)
        getattr(ex, "relay", lambda *_: None)(*parts)
        ex.finish(*parts)

    return pl.pallas_call(body, name=name, in_specs=[_ANY] * n_in, out_specs=[_ANY] * n_out, out_shape=ex.out_shape,
                          scratch_shapes=ex.scratch, input_output_aliases=getattr(ex, "aliases", {}))(*ex.inputs)


def _carried_refs(refs, n_in, n_out, ex):
    k_in, k_out, k_sem = (len(ex.inputs), len(ex.out_shape), len(ex.scratch)) if ex else (0, 0, 0)
    a, b, c = n_in + k_in, n_in + k_in + n_out, n_in + k_in + n_out + k_out
    own = refs[:n_in] + refs[a:b] + refs[c:len(refs) - k_sem]
    return own, (refs[n_in:a], refs[b:c], refs[len(refs) - k_sem:])


class AllGatherWeights:
    def __init__(self, blocks):
        n = len(blocks)
        self.inputs = tuple(blocks)
        self.out_shape = [jax.ShapeDtypeStruct((N_DEV,) + b.shape, b.dtype) for b in blocks]
        self.scratch = ([pltpu.SemaphoreType.DMA((n, 7)), pltpu.SemaphoreType.DMA((n, 7)), pltpu.SemaphoreType.DMA((n, 2))]
                        + [pltpu.VMEM(b.shape, b.dtype) for b in blocks])

    def _plan(self, ins, outs, scratch):
        send_sems, recv_sems, local_sems, *staged = scratch
        x, y, c = _mesh_pos()
        me, sibling = (x, y, c), (x, y, 1 - c)
        chips = [(1 - x, y), (x, 1 - y), (1 - x, 1 - y)]
        every = range(len(ins))

        def copy(a, k, block, to, own=False):
            slot = outs[a].at[_dev_index(*block)]
            return _Transfer(ins[a] if own else slot, slot, ins[a].shape[-2], send_sems.at[a, k], recv_sems.at[a, k], to)

        mine = [(pltpu.make_async_copy(ins[a], staged[a], local_sems.at[a, 0]),
                 pltpu.make_async_copy(staged[a], outs[a].at[_dev_index(*me)], local_sems.at[a, 1])) for a in every]
        first = [copy(a, 1 + j, me, (*chip, c), own=True) for j, chip in enumerate(chips) for a in every]
        first += [copy(a, 0, me, sibling, own=True) for a in every]
        passed = [[copy(a, 4 + j, (*chip, c), sibling) for a in every] for j, chip in enumerate(chips)]
        return me, sibling, chips, c, every, copy, mine, first, passed

    def start(self, ins, outs, scratch):
        *_, mine, first, _ = self._plan(ins, outs, scratch)
        for to_vmem, _ in mine:
            to_vmem.start()
        for cp in first:
            cp.start()

    def relay(self, ins, outs, scratch):
        me, sibling, chips, c, every, copy, mine, first, passed = self._plan(ins, outs, scratch)
        for to_vmem, to_slot in mine:
            to_vmem.wait()
            to_slot.start()
        for j, chip in enumerate(chips):
            for a in every:
                copy(a, 1 + j, (*chip, c), me).wait_recv()
            for cp in passed[j]:
                cp.start()

    def finish(self, ins, outs, scratch):
        me, sibling, chips, c, every, copy, mine, first, passed = self._plan(ins, outs, scratch)
        for a in every:
            copy(a, 0, sibling, me).wait_recv()
        for j, chip in enumerate(chips):
            for a in every:
                copy(a, 4 + j, (*chip, 1 - c), me).wait_recv()
        for cp in first + [cp for group in passed for cp in group]:
            cp.wait_send()
        for _, to_slot in mine:
            to_slot.wait()


N_CHIP = 4


class PairExchange:
    def __init__(self, by_core, whole=()):
        self.inputs = tuple(by_core) + tuple(whole)
        self.n_by_core = len(by_core)
        self.out_shape = ([jax.ShapeDtypeStruct(a.shape[1:], a.dtype) for a in by_core]
                          + [jax.ShapeDtypeStruct(a.shape, a.dtype) for a in whole])
        n = len(self.inputs)
        self.scratch = [pltpu.SemaphoreType.DMA((n,)), pltpu.SemaphoreType.DMA((n,))]

    def _copies(self, ins, outs, sems):
        x, y, c = _mesh_pos()
        srcs = [r.at[1 - c] if a < self.n_by_core else r for a, r in enumerate(ins)]
        return [_Transfer(srcs[a], outs[a], outs[a].shape[-2], sems[0].at[a], sems[1].at[a], (x, y, 1 - c))
                for a in range(len(ins))]

    def start(self, ins, outs, sems):
        for cp in self._copies(ins, outs, sems):
            cp.start()

    def finish(self, ins, outs, sems):
        copies = self._copies(ins, outs, sems)
        for cp in copies:
            cp.wait_recv()
        for cp in copies:
            cp.wait_send()


def pair_sum(own, other, dtype, rows, name, core, layer, depth, stacked=None):
    n, n_r, n_c = other.shape
    by_core = own.ndim == 4
    own = own if by_core else own[None]

    def body(core_ref, a_ref, b_ref, *refs):
        refs[-1][0, 0] = (a_ref[0, 0] + b_ref[0].astype(F32)).astype(dtype)

    carried = () if stacked is None else (stacked,)
    grid_spec = pltpu.PrefetchScalarGridSpec(
        num_scalar_prefetch=1,
        grid=(n, n_r // rows),
        in_specs=[pl.BlockSpec((1, 1, rows, n_c), lambda i, r, s: (s[0] if by_core else 0, i, r, 0)),
                  pl.BlockSpec((1, rows, n_c), lambda i, r, s: (i, r, 0))] + [_ANY] * len(carried),
        out_specs=pl.BlockSpec((1, 1, rows, n_c), lambda i, r, s: (i, layer, r, 0)),
    )
    return pl.pallas_call(
        body,
        name=name,
        grid_spec=grid_spec,
        out_shape=jax.ShapeDtypeStruct((n, depth, n_r, n_c), dtype),
        input_output_aliases={3: 0} if carried else {},
        compiler_params=_cparams("parallel", "parallel"),
    )(core, own, other, *carried)


def small_sum(a, b, name):
    def body(a_ref, b_ref, o_ref):
        o_ref[...] = a_ref[...] + b_ref[...]

    return pl.pallas_call(body, name=name, out_shape=jax.ShapeDtypeStruct(a.shape, a.dtype))(a, b)


class ChipExchange:
    def __init__(self, by_chip=(), layers=(), gathered=(), stacked=()):
        stacked = tuple(stacked) or (None,) * len(by_chip)
        kept = [s for s in stacked if s is not None]
        self.inputs = tuple(by_chip) + tuple(gathered) + tuple(kept)
        self.n_by_chip, self.n_gathered = len(by_chip), len(gathered)
        self.items = [(a, l) for a in range(len(by_chip)) for l in layers[a]] + [(self.n_by_chip + g, None) for g in range(len(gathered))]
        self.out_shape = ([jax.ShapeDtypeStruct((N_CHIP - 1,) + a.shape[1:], a.dtype) for a in by_chip]
                          + [jax.ShapeDtypeStruct((N_CHIP,) + a.shape, a.dtype) for a in gathered])
        at = iter(range(self.n_by_chip + self.n_gathered, len(self.inputs)))
        self.aliases = {next(at): a for a, s in enumerate(stacked) if s is not None}
        n = len(self.items)
        self.scratch = [pltpu.SemaphoreType.DMA((n, 3)), pltpu.SemaphoreType.DMA((n, 3)),
                        pltpu.SemaphoreType.DMA((max(self.n_gathered, 1),))]

    def _plan(self, ins, outs, sems):
        x, y, c = _mesh_pos()
        chip = 2 * x + y
        n = len(self.items)

        def copy(i, k, sending):
            a, layer = self.items[i]
            px, py = x ^ ((k >> 1) & 1), y ^ (k & 1)
            if layer is not None:
                src, dst = ins[a].at[2 * px + py, layer], outs[a].at[k - 1, layer]
            else:
                src, dst = ins[a], outs[a].at[chip if sending else 2 * px + py]
            return _Transfer(src, dst, dst.shape[-2], sems[0].at[i, k - 1], sems[1].at[i, k - 1], (px, py, c))

        local = [pltpu.make_async_copy(ins[a], outs[a].at[chip], sems[2].at[a - self.n_by_chip])
                 for a in range(self.n_by_chip, self.n_by_chip + self.n_gathered)]
        return n, copy, local

    def start(self, ins, outs, sems):
        n, copy, local = self._plan(ins, outs, sems)
        for cp in local:
            cp.start()
        for k in range(1, N_CHIP):
            for a in range(n):
                copy(a, k, True).start()

    def finish(self, ins, outs, sems):
        n, copy, local = self._plan(ins, outs, sems)
        for k in range(1, N_CHIP):
            for a in range(n):
                copy(a, k, False).wait_recv()
        for k in range(1, N_CHIP):
            for a in range(n):
                copy(a, k, True).wait_send()
        for cp in local:
            cp.wait()


ADAM_LR = 0.001
ADAM_B1 = 0.9
ADAM_B2 = 0.999
ADAM_EPS = 1e-08
ADAM_WD = 0.01
ADAM_STEP = 10


def adam_reduce(parts, w, m, v, rows, name, own=None, chip=None):
    n_l, n_r, n_c = w.shape
    n_parts = parts.shape[0]

    def body(*refs):
        p_ref, w_ref, m_ref, v_ref, g_ref, d_ref, m2_ref, v2_ref = refs[-8:]
        g = p_ref[0, 0].astype(F32)
        if own is not None:
            g = refs[-9][...].reshape(rows, n_c).astype(F32) + g
        for d in range(1, n_parts):
            g = g + p_ref[d, 0].astype(F32)
        m2 = ADAM_B1 * m_ref[0] + (1.0 - ADAM_B1) * g
        v2 = ADAM_B2 * v_ref[0] + (1.0 - ADAM_B2) * (g * g)
        m_hat = m2 / (1.0 - ADAM_B1 ** ADAM_STEP)
        v_hat = v2 / (1.0 - ADAM_B2 ** ADAM_STEP)
        g_ref[0] = g
        d_ref[0] = -ADAM_LR * (m_hat / (jnp.sqrt(v_hat) + ADAM_EPS) + ADAM_WD * w_ref[0])
        m2_ref[0] = m2
        v2_ref[0] = v2

    blk = lambda: pl.BlockSpec((1, rows, n_c), lambda l, r, *_: (l, r, 0))
    in_specs = [pl.BlockSpec((n_parts, 1, rows, n_c), lambda l, r, *_: (0, l, r, 0)), blk(), blk(), blk()]
    args = (parts, w, m, v)
    if own is not None:
        in_specs = [pl.BlockSpec((1, 1, rows, n_c), lambda l, r, s: (s[0], l, r, 0))] + in_specs
        args = (chip, own) + args
    grid_spec = pltpu.PrefetchScalarGridSpec(
        num_scalar_prefetch=0 if own is None else 1, grid=(n_l, n_r // rows), in_specs=in_specs,
        out_specs=[blk(), blk(), blk(), blk()])
    return pl.pallas_call(
        body,
        name=name,
        grid_spec=grid_spec,
        out_shape=[jax.ShapeDtypeStruct(w.shape, F32)] * 4,
        compiler_params=_cparams("parallel", "parallel"),
    )(*args)


def adam_reduce_columns(parts, w, m, v, name, own, chip):
    n_l, n_r, n_c = w.shape
    n_parts = parts.shape[0]
    view = lambda a: jnp.transpose(a, (2, 0, 1))

    def body(_, own_ref, p_ref, w_ref, m_ref, v_ref, g_ref, d_ref, m2_ref, v2_ref):
        for l in range(n_l):
            g = own_ref[0, l].astype(F32) + p_ref[0, l].astype(F32)
            for d in range(1, n_parts):
                g = g + p_ref[d, l].astype(F32)
            g = g.T
            w_l, m_l, v_l = w_ref[:, l, :], m_ref[:, l, :], v_ref[:, l, :]
            m2 = ADAM_B1 * m_l + (1.0 - ADAM_B1) * g
            v2 = ADAM_B2 * v_l + (1.0 - ADAM_B2) * (g * g)
            m_hat = m2 / (1.0 - ADAM_B1 ** ADAM_STEP)
            v_hat = v2 / (1.0 - ADAM_B2 ** ADAM_STEP)
            g_ref[:, l, :] = g
            d_ref[:, l, :] = -ADAM_LR * (m_hat / (jnp.sqrt(v_hat) + ADAM_EPS) + ADAM_WD * w_l)
            m2_ref[:, l, :] = m2
            v2_ref[:, l, :] = v2

    blk = lambda: pl.BlockSpec((LANES, n_l, n_r), lambda c, s: (c, 0, 0))
    grid_spec = pltpu.PrefetchScalarGridSpec(
        num_scalar_prefetch=1, grid=(pl.cdiv(n_c, LANES),),
        in_specs=[pl.BlockSpec((1, n_l, n_r, LANES), lambda c, s: (s[0], 0, 0, c)),
                  pl.BlockSpec((n_parts, n_l, n_r, LANES), lambda c, s: (0, 0, 0, c)), blk(), blk(), blk()],
        out_specs=[blk(), blk(), blk(), blk()])
    outs = pl.pallas_call(
        body,
        name=name,
        grid_spec=grid_spec,
        out_shape=[jax.ShapeDtypeStruct((n_c, n_l, n_r), F32)] * 4,
        compiler_params=_cparams("parallel"),
    )(chip, own, parts, view(w), view(m), view(v))
    return [jnp.transpose(o, (1, 2, 0)) for o in outs]


_SMALL = (("norm_g", (2, 1024)), ("gmlp_ln_g", (2, 4, 64)), ("gmlp_ln_b", (2, 4, 64)),
          ("gmlp_b_s", (2, 4, 128)), ("hgrn_lb", (2, 256)), ("hgrn_onorm_g", (2, 64)), ("fox_b_f", (2, 8)),
          ("final_norm_g", (1024,)), ("loss", ()))


def _padded(n):
    return -(-n // LANES) * LANES


_SMALL_ROWS = -(-sum(_padded(int(np.prod(s))) for _, s in _SMALL) // LANES // 8) * 8


def _pack_small(vals):
    flat = []
    for (name, shape), a in zip(_SMALL, vals, strict=True):
        n = int(np.prod(shape))
        flat.append(jnp.pad(a.reshape(n).astype(F32), (0, _padded(n) - n)))
    flat = jnp.concatenate(flat)
    return jnp.pad(flat, (0, _SMALL_ROWS * LANES - flat.shape[0])).reshape(_SMALL_ROWS, LANES)


def _unpack_small(slab):
    flat, out, at = slab.reshape(-1), {}, 0
    for name, shape in _SMALL:
        n = int(np.prod(shape))
        out[name] = flat[at:at + n].reshape(shape)
        at += _padded(n)
    return out


def sum_parts(parts, name):
    def body(p_ref, o_ref):
        g = p_ref[0]
        for d in range(1, parts.shape[0]):
            g = g + p_ref[d]
        o_ref[...] = g

    return pl.pallas_call(body, name=name, out_shape=jax.ShapeDtypeStruct(parts.shape[1:], F32))(parts)


def adam_small(gs, ws, ms, vs):
    n = len(gs)

    def body(*refs):
        for k in range(n):
            g, w, m, v = (refs[j * n + k][...] for j in range(4))
            m2 = ADAM_B1 * m + (1.0 - ADAM_B1) * g
            v2 = ADAM_B2 * v + (1.0 - ADAM_B2) * (g * g)
            m_hat = m2 / (1.0 - ADAM_B1 ** ADAM_STEP)
            v_hat = v2 / (1.0 - ADAM_B2 ** ADAM_STEP)
            refs[4 * n + k][...] = -ADAM_LR * (m_hat / (jnp.sqrt(v_hat) + ADAM_EPS) + ADAM_WD * w)
            refs[5 * n + k][...] = m2
            refs[6 * n + k][...] = v2

    outs = pl.pallas_call(body, name="adam_small",
                          out_shape=[jax.ShapeDtypeStruct(w.shape, F32) for _ in range(3) for w in ws])(*gs, *ws, *ms, *vs)
    return outs[:n], outs[n:2 * n], outs[2 * n:]


def kernel(x, norm_g, w_in, w_out, gmlp_ln_g, gmlp_ln_b, gmlp_w_s, gmlp_b_s, hgrn_lb, hgrn_onorm_g, fox_b_f, final_norm_g, loss_target, m_norm_g, m_w_in, m_w_out, m_gmlp_ln_g, m_gmlp_ln_b, m_gmlp_w_s, m_gmlp_b_s, m_hgrn_lb, m_hgrn_onorm_g, m_fox_b_f, m_final_norm_g, v_norm_g, v_w_in, v_w_out, v_gmlp_ln_g, v_gmlp_ln_b, v_gmlp_w_s, v_gmlp_b_s, v_hgrn_lb, v_hgrn_onorm_g, v_fox_b_f, v_final_norm_g):
    depth = w_in.shape[0]
    seq = x.shape[1]
    assert w_in.shape[2] * N_DEV == N_IN
    xs, tgt = x[0], loss_target[0]

    wi_blk, wo_blk = w_in.astype(BF16), w_out.astype(BF16)
    (wi_all,) = _exchange_call(AllGatherWeights([wi_blk[0]]), "allgather_weights_0")

    ln_g = gmlp_ln_g.reshape(depth, 1, A_WIDTH)
    ln_b = gmlp_ln_b.reshape(depth, 1, A_WIDTH)
    bs_t = jnp.pad(jnp.transpose(gmlp_b_s, (0, 2, 1)), ((0, 0), (0, 0), (0, LANES - A_GROUPS)))
    lb0, lb1 = hgrn_lb[0:1], hgrn_lb[1:2]
    onorm = jnp.tile(hgrn_onorm_g, (1, B_HEADS)).reshape(depth, 1, B_WIDTH)
    bf_row = jnp.pad(fox_b_f, ((0, 0), (0, LANES - C_HEADS))).reshape(depth, 1, LANES)

    core = lax.axis_index("c").astype(jnp.int32).reshape(1)
    chip = (2 * lax.axis_index("x") + lax.axis_index("y")).astype(jnp.int32).reshape(1)

    saved = []
    xc = xs
    for l in range(depth):
        wi_int = assemble_w_in(wi_all[:, None])
        proj, h = inproj(xc, norm_g[l:l + 1], wi_int, 0)
        ya = gmlp_fwd(proj, ln_g[l], ln_b[l], gmlp_w_s[l], bs_t[l])
        yb, states = hgrn_fwd(proj, lb0, lb1, onorm[l], l)
        ka, va, vt, kt, qt, qa = fox_prep(proj, bf_row[l])
        ride = ([wo_blk] if l == 0 else []) + ([wi_blk[l + 1]] if l + 1 < depth else [])
        o, lse, *gathered = fox_fwd(qt, ka, vt, AllGatherWeights(ride) if ride else None)
        if l == 0:
            wo_all = gathered.pop(0)
        if gathered:
            (wi_all,) = gathered
        x_in = xc
        if l + 1 < depth:
            xc, yfull = outproj(x_in, ya, yb, o, proj, wo_all, l)
        else:
            dx, yfull, d_final_g, loss_tile = outproj(x_in, ya, yb, o, proj, wo_all, l, (final_norm_g[None], tgt))
        saved.append((x_in, proj, h, states, ka, va, kt, qt, qa, o, lse, yfull, wi_int))

    n_shard = w_in.shape[2]
    g_norm = [None] * depth
    g_ln_g, g_ln_b, g_ws, g_bs, g_on, g_bf = ([None] * depth for _ in range(6))
    g_lb0, g_lb1 = jnp.zeros_like(lb0), jnp.zeros_like(lb1)
    swi = swo = rwi = rwo = None
    for l in reversed(range(depth)):
        x_in, proj, h, states, ka, va, kt, qt, qa, o, lse, yfull, wi_int = saved[l]
        dy, gwo = outproj_bwd(dx, yfull, wo_all, l)
        dproj, g_ln_g[l], g_ln_b[l], g_ws[l], dbs_t = gmlp_bwd(proj, dy, ln_g[l], ln_b[l], gmlp_w_s[l], bs_t[l])
        g_bs[l] = dbs_t[:, :A_GROUPS].T
        if l > 0:
            (qwo,) = _exchange_call(PairExchange([gwo]), f"pair_exchange_w_out_{l}")
        else:
            gws = jnp.stack(g_ws).reshape(-1, LANES)
            qwo, qws = _exchange_call(PairExchange([gwo], [gws]), f"pair_exchange_w_out_{l}")
            sws = small_sum(gws, qws, "pair_sum_w_s")
        swo = pair_sum(gwo, qwo, BF16, gwo.shape[2], "pair_sum_w_out", core, l, depth, swo)
        dproj, d0, d1, don = hgrn_bwd(proj, states, dy, lb0, lb1, onorm[l], l, dproj)
        g_lb0, g_lb1 = g_lb0 + d0, g_lb1 + d1
        g_on[l] = don.reshape(B_HEADS, B_KDIM).sum(0)
        dob, dproj, dot_t = fox_bwd_prep(dy, o, proj, dproj)
        top = l == depth - 1
        ride = ChipExchange([swo] if top else [swi, swo], [(l,)] if top else [(l + 1,), (l,)],
                            [sws] if l == 0 else [], [rwo] if top else [rwi, rwo])
        outs = fox_bwd(ka, va, kt, qt, dot_t, qa, dob, lse, ride)
        dqkv, (dck, dcq), got = outs[:3], outs[3:5], list(outs[5:])
        if not top:
            rwi = got.pop(0)
        rwo = got.pop(0)
        if l == 0:
            (rws,) = got
        dproj, dbf = fox_post(dcq, dck, proj, bf_row[l], dproj)
        g_bf[l] = dbf[0, :C_HEADS]
        gwi, for_sibling = split_w_in_grad(inproj_bwd_w(h, dproj, dqkv), n_shard, core)
        (qwi,) = _exchange_call(PairExchange([], [for_sibling]), f"pair_exchange_w_in_{l}")
        swi = pair_sum(gwi, qwi, BF16, 256, "pair_sum_w_in", core, l, depth, swi)
        ride = ChipExchange([swi], [(l,)], stacked=[rwi]) if l == 0 else None
        outs = inproj_bwd_x(dproj, dqkv, wi_int, x_in, norm_g[l:l + 1], dx, 0, ride)
        dx, g_norm[l] = outs[:2]
        if ride is not None:
            (rwi,) = outs[2:]

    gsm = _pack_small([
        jnp.concatenate(g_norm), jnp.stack(g_ln_g), jnp.stack(g_ln_b), jnp.stack(g_bs),
        jnp.concatenate([g_lb0, g_lb1]), jnp.stack(g_on), jnp.stack(g_bf), d_final_g, loss_tile[0, 0]])
    (qsm,) = _exchange_call(PairExchange([], [gsm]), "pair_exchange_small")
    ssm = small_sum(gsm, qsm, "pair_sum_small")
    (rsm,) = _exchange_call(ChipExchange(gathered=[ssm]), "chip_exchange_small")

    small_w = (norm_g, gmlp_ln_g, gmlp_ln_b, gmlp_b_s, hgrn_lb, hgrn_onorm_g, fox_b_f, final_norm_g)
    small_m = (m_norm_g, m_gmlp_ln_g, m_gmlp_ln_b, m_gmlp_b_s, m_hgrn_lb, m_hgrn_onorm_g, m_fox_b_f, m_final_norm_g)
    small_v = (v_norm_g, v_gmlp_ln_g, v_gmlp_ln_b, v_gmlp_b_s, v_hgrn_lb, v_hgrn_onorm_g, v_fox_b_f, v_final_norm_g)
    res_wi = adam_reduce_columns(rwi, w_in, m_w_in, v_w_in, "adam_w_in", swi, chip)
    res_wo = adam_reduce(rwo, w_out, m_w_out, v_w_out, w_out.shape[1], "adam_w_out", own=swo, chip=chip)
    grads = _unpack_small(sum_parts(rsm, "sum_small"))
    names = [name for name, _ in _SMALL if name != "loss"]
    rows = lambda a: a.reshape(1, -1) if a.ndim == 1 else a
    res_sm = adam_small([rows(grads[k]) for k in names], *([rows(a) for a in wmv] for wmv in (small_w, small_m, small_v)))
    res_sm = [grads] + [{k: a.reshape(grads[k].shape) for k, a in zip(names, r, strict=True)} for r in res_sm]
    as_rows = lambda a: a.reshape(1, -1, LANES)
    res_ws = adam_reduce(rws[:, None], as_rows(gmlp_w_s), as_rows(m_gmlp_w_s), as_rows(v_gmlp_w_s), rws.shape[1], "adam_w_s")
    for s, r in zip(res_sm, res_ws, strict=True):
        s["gmlp_w_s"] = r.reshape(gmlp_w_s.shape)

    def group(i):
        s = res_sm[i]
        return [s["norm_g"], res_wi[i], res_wo[i], s["gmlp_ln_g"], s["gmlp_ln_b"], s["gmlp_w_s"], s["gmlp_b_s"],
                s["hgrn_lb"], s["hgrn_onorm_g"], s["fox_b_f"], s["final_norm_g"]]

    return (res_sm[0]["loss"], dx[None], *group(0), *group(1), *group(2), *group(3))
```

```python
import functools

import jax
import jax.numpy as jnp
import numpy as np
from jax import lax
from jax.experimental import pallas as pl
from jax.experimental.pallas import tpu as pltpu

F32 = jnp.float32
BF16 = jnp.bfloat16

NORM_EPS = 1e-6
F_FLOOR = 1e-30
CHUNK = 128
LANES = 128
VMEM_LIMIT = 56 * 1024 * 1024


def _cparams(*sem):
    return pltpu.CompilerParams(dimension_semantics=sem, vmem_limit_bytes=VMEM_LIMIT)


def _dot(a, b, dims=(((1,), (0,)), ((), ())), precision=None):
    return lax.dot_general(a, b, dims, precision=precision, preferred_element_type=F32)


_NT = (((1,), (1,)), ((), ()))
_TN = (((0,), (0,)), ((), ()))


def _bf16_pieces(x, n):
    out, r = [], x
    for i in range(n):
        out.append(r.astype(BF16))
        if i + 1 < n:
            r = r - out[-1].astype(F32)
    return out


@functools.partial(jax.custom_vjp, nondiff_argnums=(2,))
def _times_exact(x, e, n):
    return functools.reduce(jnp.add, [_dot(p, e) for p in _bf16_pieces(x, n)])


def _times_exact_fwd(x, e, n):
    return _times_exact(x, e, n), e


def _times_exact_bwd(n, e, g):
    dx = functools.reduce(jnp.add, [lax.dot_general(p, e, _NT, preferred_element_type=F32) for p in _bf16_pieces(g, n)])
    return dx, jnp.zeros_like(e)


_times_exact.defvjp(_times_exact_fwd, _times_exact_bwd)


@functools.partial(jax.custom_vjp, nondiff_argnums=(2,))
def _exact_times(e, x, n):
    return functools.reduce(jnp.add, [_dot(e, p) for p in _bf16_pieces(x, n)])


def _exact_times_fwd(e, x, n):
    return _exact_times(e, x, n), e


def _exact_times_bwd(n, e, g):
    dx = functools.reduce(jnp.add, [lax.dot_general(e, p, _TN, preferred_element_type=F32) for p in _bf16_pieces(g, n)])
    return jnp.zeros_like(e), dx


_exact_times.defvjp(_exact_times_fwd, _exact_times_bwd)


def _group_mean_matrix(width, group):
    idx = np.arange(width) // group
    return jnp.asarray((idx[:, None] == idx[None, :]).astype(np.float32) / group, BF16)


def _group_ones_matrix(width, group):
    idx = np.arange(width) // group
    return jnp.asarray((idx[:, None] == idx[None, :]).astype(np.float32), BF16)


A_WIDTH = 256
A_GROUPS = 4
A_GDIM = 64


A_ROWS = 512


def _gmlp_chunk(x3, ln_g, ln_b, w_s, bs_t, mean_m, gind):
    n = x3.shape[0] // CHUNK
    u = jax.nn.gelu(x3[:, :A_WIDTH])
    v = jax.nn.gelu(x3[:, A_WIDTH:2 * A_WIDTH])
    z = x3[:, 2 * A_WIDTH:]
    mu = _times_exact(v, mean_m, 2)
    d = v - mu
    var = _times_exact(d * d, mean_m, 2)
    vn = d * lax.rsqrt(var + NORM_EPS) * ln_g + ln_b
    vnb = vn.astype(BF16)
    wide = jnp.concatenate([vnb[i * CHUNK:(i + 1) * CHUNK] for i in range(n)], axis=1)
    row = lax.broadcasted_iota(jnp.int32, (CHUNK, CHUNK), 0)
    col = lax.broadcasted_iota(jnp.int32, (CHUNK, CHUNK), 1)
    causal = row >= col
    lane_g = lax.shift_right_logical(lax.broadcasted_iota(jnp.int32, (CHUNK, n * A_WIDTH), 1), 6) & (A_GROUPS - 1)
    bias = _times_exact(bs_t, gind, 3)
    mixed = jnp.concatenate([bias] * n, axis=1)
    for g in range(A_GROUPS):
        wc = jnp.where(causal, w_s[g], 0.0).astype(BF16)
        mixed = mixed + jnp.where(lane_g == g, _dot(wc, wide), 0.0)
    mixed = jnp.concatenate([mixed[:, i * A_WIDTH:(i + 1) * A_WIDTH] for i in range(n)], axis=0)
    return u * mixed * jax.nn.silu(z)


def _gmlp_consts():
    gind = np.zeros((LANES, A_WIDTH), np.float32)
    for g in range(A_GROUPS):
        gind[g, g * A_GDIM:(g + 1) * A_GDIM] = 1.0
    return _group_mean_matrix(A_WIDTH, A_GDIM), jnp.asarray(gind, BF16)


def _full(shape):
    return pl.BlockSpec(shape, lambda *_: (0,) * len(shape))


def gmlp_fwd(proj, ln_g, ln_b, w_s, bs_t):
    seq = proj.shape[0]
    rows = min(A_ROWS, seq)
    mean_m, gind = _gmlp_consts()

    def body(x_ref, g_ref, b_ref, w_ref, bs_ref, m_ref, gi_ref, y_ref):
        y = _gmlp_chunk(x_ref[...], g_ref[...], b_ref[...], w_ref[...], bs_ref[...], m_ref[...], gi_ref[...])
        y_ref[...] = y.astype(BF16)

    return pl.pallas_call(
        body,
        name="gmlp_fwd",
        grid=(seq // rows,),
        in_specs=[
            pl.BlockSpec((rows, 3 * A_WIDTH), lambda n: (n, 0)),
            _full((1, A_WIDTH)), _full((1, A_WIDTH)), _full((A_GROUPS, CHUNK, CHUNK)), _full((CHUNK, LANES)),
            _full((A_WIDTH, A_WIDTH)), _full((LANES, A_WIDTH)),
        ],
        out_specs=pl.BlockSpec((rows, A_WIDTH), lambda n: (n, 0)),
        out_shape=jax.ShapeDtypeStruct((seq, A_WIDTH), BF16),
        compiler_params=_cparams("parallel"),
    )(proj, ln_g, ln_b, w_s, bs_t, mean_m, gind)


def gmlp_bwd(proj, dy, ln_g, ln_b, w_s, bs_t):
    seq = proj.shape[0]
    rows = min(A_ROWS, seq)
    mean_m, gind = _gmlp_consts()

    def body(x_ref, dy_ref, g_ref, b_ref, w_ref, bs_ref, m_ref, gi_ref, dx_ref, dg_ref, db_ref, dw_ref, dbs_ref):
        fn = functools.partial(_gmlp_chunk, mean_m=m_ref[...], gind=gi_ref[...])
        _, vjp = jax.vjp(fn, x_ref[...], g_ref[...], b_ref[...], w_ref[...], bs_ref[...])
        dx, dg, db, dw, dbs = vjp(dy_ref[...])
        dx_ref[...] = dx.astype(BF16)

        @pl.when(pl.program_id(0) == 0)
        def _():
            dg_ref[...] = jnp.zeros_like(dg_ref)
            db_ref[...] = jnp.zeros_like(db_ref)
            dw_ref[...] = jnp.zeros_like(dw_ref)
            dbs_ref[...] = jnp.zeros_like(dbs_ref)

        dg_ref[...] += dg
        db_ref[...] += db
        dw_ref[...] += dw
        dbs_ref[...] += dbs

    return pl.pallas_call(
        body,
        name="gmlp_bwd",
        grid=(seq // rows,),
        in_specs=[
            pl.BlockSpec((rows, 3 * A_WIDTH), lambda n: (n, 0)),
            pl.BlockSpec((rows, A_WIDTH), lambda n: (n, 0)),
            _full((1, A_WIDTH)), _full((1, A_WIDTH)), _full((A_GROUPS, CHUNK, CHUNK)), _full((CHUNK, LANES)),
            _full((A_WIDTH, A_WIDTH)), _full((LANES, A_WIDTH)),
        ],
        out_specs=[
            pl.BlockSpec((rows, 3 * A_WIDTH), lambda n: (n, 0)),
            _full((1, A_WIDTH)), _full((1, A_WIDTH)), _full((A_GROUPS, CHUNK, CHUNK)), _full((CHUNK, LANES)),
        ],
        out_shape=[
            jax.ShapeDtypeStruct((seq, D_INT), BF16),
            jax.ShapeDtypeStruct((1, A_WIDTH), F32), jax.ShapeDtypeStruct((1, A_WIDTH), F32),
            jax.ShapeDtypeStruct((A_GROUPS, CHUNK, CHUNK), F32), jax.ShapeDtypeStruct((CHUNK, LANES), F32),
        ],
        compiler_params=_cparams("arbitrary"),
    )(proj, dy, ln_g, ln_b, w_s, bs_t, mean_m, gind)


B_WIDTH = 256
B_HEADS = 4
B_KDIM = 64
B_LEVELS = (64, 32, 16, 8, 4, 2, 1)


def _hgrn_consts():
    t = np.arange(CHUNK)
    u = t[None, :]
    mats = [np.tril(np.ones((CHUNK, CHUNK), np.float32))]
    for m in B_LEVELS:
        p = (t // (2 * m)) * (2 * m) + m - 1
        right = (t % (2 * m)) >= m
        sel = np.where(right[:, None], (u > p[:, None]) & (u <= t[:, None]), (u > t[:, None]) & (u <= p[:, None]))
        mats.append(sel.astype(np.float32))
    return jnp.asarray(np.concatenate(mats, 0), BF16), _group_ones_matrix(B_WIDTH, B_KDIM)


def _hgrn_lower_bound(lb0, lb1, layer):
    mx = jnp.maximum(lb0, lb1)
    e0 = jnp.exp(lb0 - mx)
    e1 = jnp.exp(lb1 - mx)
    p0 = e0 / (e0 + e1)
    p1 = e1 / (e0 + e1)
    cs = p0 if layer == 0 else p0 + p1
    return jnp.clip(cs - p0, 0.0, 1.0 - 1e-6)


def _hgrn_chunk(x4, st, lb0, lb1, onorm, layer, tstack, ones_bd):
    q_raw, fl, v, zg = (x4[:, i * B_WIDTH:(i + 1) * B_WIDTH] for i in range(4))
    lb = _hgrn_lower_bound(lb0, lb1, layer)
    q = jax.nn.silu(q_raw) * (B_KDIM ** -0.5)
    f = lb + (1.0 - lb) * jax.nn.sigmoid(fl)
    logf = jnp.log(jnp.maximum(f, F_FLOOR))
    k = (1.0 - lb) * jax.nn.sigmoid(-fl)
    b = _exact_times(tstack[:CHUNK], logf, 3)
    dall = jnp.concatenate([b, _exact_times(tstack[CHUNK:], logf, 2)], axis=0)
    b_last = jnp.sum(logf, axis=0, keepdims=True)
    vb = v.astype(BF16)

    lane_h = lax.shift_right_logical(lax.broadcasted_iota(jnp.int32, (CHUNK, B_WIDTH), 1), 6)
    row = lax.broadcasted_iota(jnp.int32, (CHUNK, B_WIDTH), 0)
    srow = lax.broadcasted_iota(jnp.int32, (B_HEADS * CHUNK, CHUNK), 0) & (CHUNK - 1)
    scol = lax.broadcasted_iota(jnp.int32, (B_HEADS * CHUNK, CHUNK), 1)

    def heads_on_rows(a):
        return jnp.concatenate([jnp.where(lane_h == h, a, 0.0) for h in range(B_HEADS)], axis=0)

    def heads_from_rows(r):
        out = jnp.where(lane_h == 0, r[:CHUNK], 0.0)
        for h in range(1, B_HEADS):
            out = out + jnp.where(lane_h == h, r[h * CHUNK:(h + 1) * CHUNK], 0.0)
        return out

    o = lax.dot_general((q * jnp.exp(b)).astype(BF16), st.astype(BF16), _NT, preferred_element_type=F32)
    scores = jnp.zeros((B_HEADS * CHUNK, CHUNK), F32)
    for li, m in enumerate(B_LEVELS):
        e = jnp.exp(dall[(li + 1) * CHUNK:(li + 2) * CHUNK])
        right = (row & (2 * m - 1)) >= m
        qt = jnp.where(right, q * e, 0.0)
        kt = jnp.where(right, 0.0, k * e)
        sc = lax.dot_general(heads_on_rows(qt).astype(BF16), kt.astype(BF16), _NT, preferred_element_type=F32)
        sh = int(np.log2(2 * m))
        same = lax.shift_right_logical(srow, sh) == lax.shift_right_logical(scol, sh)
        scores = scores + jnp.where(same, sc, 0.0)
    o = o + heads_from_rows(_dot(scores.astype(BF16), vb))
    o = o + _times_exact(q * k, ones_bd, 2) * v

    kv = lax.dot_general(vb, (k * jnp.exp(b_last - b)).astype(BF16), _TN, preferred_element_type=F32)
    st_new = st * jnp.exp(b_last) + jnp.where(ones_bd > 0.5, kv, 0.0)

    ms = _times_exact(o * o, ones_bd, 2) * (1.0 / B_KDIM)
    y = o * lax.rsqrt(ms + NORM_EPS) * onorm * jax.nn.silu(zg)
    return y, st_new


B_ROWS = 256


def _hgrn_rows(x4, st, lb0, lb1, onorm, layer, tstack, ones_bd):
    ys = []
    for i in range(x4.shape[0] // CHUNK):
        y, st = _hgrn_chunk(x4[i * CHUNK:(i + 1) * CHUNK], st, lb0, lb1, onorm, layer, tstack, ones_bd)
        ys.append(y)
    return jnp.concatenate(ys, axis=0), st


def hgrn_fwd(proj, lb0, lb1, onorm, layer):
    seq = proj.shape[0]
    rows = min(B_ROWS, seq)
    nc = seq // rows
    tstack, ones_bd = _hgrn_consts()

    def body(x_ref, lb0_ref, lb1_ref, on_ref, t_ref, e_ref, y_ref, st_out_ref, st_ref):
        @pl.when(pl.program_id(0) == 0)
        def _():
            st_ref[...] = jnp.zeros_like(st_ref)

        st = st_ref[...]
        st_out_ref[0] = st
        y, st_new = _hgrn_rows(x_ref[...], st, lb0_ref[...], lb1_ref[...], on_ref[...], layer, t_ref[...], e_ref[...])
        y_ref[...] = y.astype(BF16)
        st_ref[...] = st_new

    return pl.pallas_call(
        body,
        name=f"hgrn_fwd_{layer}",
        grid=(nc,),
        in_specs=[
            pl.BlockSpec((rows, 4 * B_WIDTH), lambda n: (n, 1)),
            _full((1, B_WIDTH)), _full((1, B_WIDTH)), _full((1, B_WIDTH)),
            _full(((len(B_LEVELS) + 1) * CHUNK, CHUNK)), _full((B_WIDTH, B_WIDTH)),
        ],
        out_specs=[
            pl.BlockSpec((rows, B_WIDTH), lambda n: (n, 0)),
            pl.BlockSpec((1, B_WIDTH, B_WIDTH), lambda n: (n, 0, 0)),
        ],
        out_shape=[jax.ShapeDtypeStruct((seq, B_WIDTH), BF16), jax.ShapeDtypeStruct((nc, B_WIDTH, B_WIDTH), F32)],
        scratch_shapes=[pltpu.VMEM((B_WIDTH, B_WIDTH), F32)],
        compiler_params=_cparams("arbitrary"),
    )(proj, lb0, lb1, onorm, tstack, ones_bd)


def hgrn_bwd(proj, states, dy, lb0, lb1, onorm, layer, dproj):
    seq = proj.shape[0]
    rows = min(B_ROWS, seq)
    nc = seq // rows
    tstack, ones_bd = _hgrn_consts()

    def body(x_ref, st_in_ref, dy_ref, lb0_ref, lb1_ref, on_ref, t_ref, e_ref, _, dx_ref, d0_ref, d1_ref, don_ref, dst_ref):
        @pl.when(pl.program_id(0) == 0)
        def _():
            dst_ref[...] = jnp.zeros_like(dst_ref)
            d0_ref[...] = jnp.zeros_like(d0_ref)
            d1_ref[...] = jnp.zeros_like(d1_ref)
            don_ref[...] = jnp.zeros_like(don_ref)

        fn = functools.partial(_hgrn_rows, layer=layer, tstack=t_ref[...], ones_bd=e_ref[...])
        _, vjp = jax.vjp(fn, x_ref[...], st_in_ref[0], lb0_ref[...], lb1_ref[...], on_ref[...])
        dx, dst, d0, d1, don = vjp((dy_ref[...], dst_ref[...]))
        dx_ref[...] = dx.astype(BF16)
        dst_ref[...] = dst
        d0_ref[...] += d0
        d1_ref[...] += d1
        don_ref[...] += don

    rev = lambda n: nc - 1 - n
    return pl.pallas_call(
        body,
        name=f"hgrn_bwd_{layer}",
        grid=(nc,),
        in_specs=[
            pl.BlockSpec((rows, 4 * B_WIDTH), lambda n: (rev(n), 1)),
            pl.BlockSpec((1, B_WIDTH, B_WIDTH), lambda n: (rev(n), 0, 0)),
            pl.BlockSpec((rows, B_WIDTH), lambda n: (rev(n), 1)),
            _full((1, B_WIDTH)), _full((1, B_WIDTH)), _full((1, B_WIDTH)),
            _full(((len(B_LEVELS) + 1) * CHUNK, CHUNK)), _full((B_WIDTH, B_WIDTH)), _ANY,
        ],
        out_specs=[
            pl.BlockSpec((rows, 4 * B_WIDTH), lambda n: (rev(n), 1)),
            _full((1, B_WIDTH)), _full((1, B_WIDTH)), _full((1, B_WIDTH)),
        ],
        out_shape=[jax.ShapeDtypeStruct(dproj.shape, BF16)] + [jax.ShapeDtypeStruct((1, B_WIDTH), F32)] * 3,
        input_output_aliases={8: 0},
        scratch_shapes=[pltpu.VMEM((B_WIDTH, B_WIDTH), F32)],
        compiler_params=_cparams("arbitrary"),
    )(proj, states, dy, lb0, lb1, onorm, tstack, ones_bd, dproj)


D_MODEL = 1024
D_INT = 4096


def _rms_stats(xf):
    r = lax.rsqrt(jnp.mean(xf * xf, axis=-1, keepdims=True) + NORM_EPS)
    return r, xf * r


def _rms_bwd(dy, g, r, xh):
    u = dy * g
    return r * (u - xh * jnp.mean(u * xh, axis=-1, keepdims=True))


C_QKV = (2048, 3584)
P_WIDTH = D_INT - (C_QKV[1] - C_QKV[0])
P_Z_BLOCK = C_QKV[0] // 512


def inproj(x, g, w, layer):
    seq = x.shape[0]
    tm = min(seq, 512)

    def body(x_ref, g_ref, w_ref, p_ref, qkv_ref, h_ref):
        _, xh = _rms_stats(x_ref[...])
        h = (xh * g_ref[...]).astype(BF16)
        h_ref[...] = h
        p_ref[:, :C_QKV[0]] = _dot(h, w_ref[0, :, :C_QKV[0]])
        qkv_ref[...] = _dot(h, w_ref[0, :, C_QKV[0]:C_QKV[1]]).astype(BF16)
        p_ref[:, C_QKV[0]:] = _dot(h, w_ref[0, :, C_QKV[1]:])

    rows = lambda n: pl.BlockSpec((tm, n), lambda i: (i, 0))
    return pl.pallas_call(
        body,
        name="inproj",
        grid=(seq // tm,),
        in_specs=[rows(D_MODEL), _full((1, D_MODEL)), pl.BlockSpec((1, D_MODEL, D_INT), lambda i: (layer, 0, 0))],
        out_specs=[rows(P_WIDTH), rows(C_QKV[1] - C_QKV[0]), rows(D_MODEL)],
        out_shape=[jax.ShapeDtypeStruct((seq, P_WIDTH), F32), jax.ShapeDtypeStruct((seq, C_QKV[1] - C_QKV[0]), BF16),
                   jax.ShapeDtypeStruct((seq, D_MODEL), BF16)],
        compiler_params=_cparams("parallel"),
    )(x, g, w)


def outproj(x, ya, yb, o, proj, wo, layer, head=None):
    seq = x.shape[0]
    tm = min(seq, 512)
    blk = wo.shape[2]

    def body(x_ref, ya_ref, yb_ref, o_ref, z_ref, w_ref, *refs):
        yc = (o_ref[...] * jax.nn.silu(z_ref[...])).astype(BF16)
        y = jnp.concatenate([ya_ref[...], yb_ref[...], yc], axis=1)
        w = jnp.concatenate([w_ref[d, 0] for d in range(N_DEV)], axis=0)
        xn = x_ref[...] + _dot(y, w)
        if head is None:
            xn_ref, y_ref = refs
            xn_ref[...] = xn
        else:
            g_ref, t_ref, dx_ref, y_ref, dg_ref, loss_ref = refs

            @pl.when(pl.program_id(0) == 0)
            def _():
                dg_ref[...] = jnp.zeros_like(dg_ref)
                loss_ref[...] = jnp.zeros_like(loss_ref)

            g = g_ref[...]
            r, xh = _rms_stats(xn)
            err = xh * g - t_ref[...]
            sq = jnp.sum(jnp.sum(err * err, axis=1, keepdims=True), axis=0, keepdims=True)
            loss_ref[...] += jnp.broadcast_to(sq * (0.5 / D_MODEL), loss_ref.shape)
            dout = err * (1.0 / D_MODEL)
            dg_ref[...] += jnp.sum(dout * xh, axis=0, keepdims=True)
            dx_ref[...] = _rms_bwd(dout, g, r, xh)
        y_ref[...] = y

    rows = lambda: pl.BlockSpec((tm, D_MODEL), lambda i: (i, 0))
    tail = (() if head is None else (_full((1, D_MODEL)), rows()),
            () if head is None else (_full((1, D_MODEL)), _full((8, LANES))),
            () if head is None else (jax.ShapeDtypeStruct((1, D_MODEL), F32), jax.ShapeDtypeStruct((8, LANES), F32)))
    return pl.pallas_call(
        body,
        name="outproj" if head is None else "outproj_loss",
        grid=(seq // tm,),
        in_specs=[
            rows(),
            pl.BlockSpec((tm, 256), lambda i: (i, 0)),
            pl.BlockSpec((tm, 256), lambda i: (i, 0)),
            pl.BlockSpec((tm, 512), lambda i: (i, 0)),
            pl.BlockSpec((tm, 512), lambda i: (i, P_Z_BLOCK)),
            pl.BlockSpec((N_DEV, 1, blk, D_MODEL), lambda i: (0, layer, 0, 0)),
            *tail[0],
        ],
        out_specs=[rows(), rows(), *tail[1]],
        out_shape=[jax.ShapeDtypeStruct((seq, D_MODEL), F32), jax.ShapeDtypeStruct((seq, D_MODEL), BF16), *tail[2]],
        compiler_params=_cparams("parallel" if head is None else "arbitrary"),
    )(x, ya, yb, o, proj, wo, *(head or ()))


def outproj_bwd(dx, y, wo, layer):
    seq = dx.shape[0]
    ts = min(seq, 512)
    blk = wo.shape[2]

    def body(dx_ref, y_ref, w_ref, dy_ref, dw_ref):
        @pl.when(pl.program_id(0) == 0)
        def _():
            dw_ref[...] = jnp.zeros_like(dw_ref)

        dxb = dx_ref[...].astype(BF16)
        w = jnp.concatenate([w_ref[d, 0] for d in range(N_DEV)], axis=0)
        dy_ref[...] = lax.dot_general(dxb, w, _NT, preferred_element_type=F32)
        dw = lax.dot_general(y_ref[...], dxb, _TN, preferred_element_type=F32)
        for d in range(N_DEV):
            dw_ref[d % 2, d // 2] += dw[d * blk:(d + 1) * blk]

    return pl.pallas_call(
        body,
        name="outproj_bwd",
        grid=(seq // ts,),
        in_specs=[
            pl.BlockSpec((ts, D_MODEL), lambda i: (i, 0)),
            pl.BlockSpec((ts, D_MODEL), lambda i: (i, 0)),
            pl.BlockSpec((N_DEV, 1, blk, D_MODEL), lambda i: (0, layer, 0, 0)),
        ],
        out_specs=[pl.BlockSpec((ts, D_MODEL), lambda i: (i, 0)),
                   pl.BlockSpec((2, N_CHIP, blk, D_MODEL), lambda i: (0, 0, 0, 0))],
        out_shape=[jax.ShapeDtypeStruct((seq, D_MODEL), F32), jax.ShapeDtypeStruct((2, N_CHIP, blk, D_MODEL), F32)],
        compiler_params=_cparams("arbitrary"),
    )(dx, y, wo)


def _dproj_parts(dp_ref, dqkv_refs, rows):
    lo, hi = C_QKV
    step = (hi - lo) // len(dqkv_refs)
    return ([(0, dp_ref.at[rows, 0:lo])] + [(lo + i * step, r.at[rows, :]) for i, r in enumerate(dqkv_refs)]
            + [(hi, dp_ref.at[rows, hi:D_INT])])


def inproj_bwd_x(dproj, dqkv, w, x, g, dx_in, layer, carried=None):
    seq = x.shape[0]
    tm = min(seq, 512)

    def body(dp_ref, dq_ref, dk_ref, dv_ref, w_ref, x_ref, g_ref, dxin_ref, dx_ref, dg_ref):
        @pl.when(pl.program_id(0) == 0)
        def _():
            dg_ref[...] = jnp.zeros_like(dg_ref)

        dh = None
        for at, part in _dproj_parts(dp_ref, (dq_ref, dk_ref, dv_ref), slice(None)):
            term = lax.dot_general(part[...], w_ref[0, :, at:at + part.shape[1]], _NT, preferred_element_type=F32)
            dh = term if dh is None else dh + term
        r, xh = _rms_stats(x_ref[...])
        dg_ref[...] += jnp.sum(dh * xh, axis=0, keepdims=True)
        dx_ref[...] = dxin_ref[...] + _rms_bwd(dh, g_ref[...], r, xh)

    third = lambda: pl.BlockSpec((tm, C_WIDTH), lambda i: (i, 0))
    return _call_carrying(
        carried, body, (dproj, *dqkv, w, x, g, dx_in),
        name="inproj_bwd_x",
        grid=(seq // tm,),
        in_specs=[
            pl.BlockSpec((tm, D_INT), lambda i: (i, 0)), third(), third(), third(),
            pl.BlockSpec((1, D_MODEL, D_INT), lambda i: (layer, 0, 0)),
            pl.BlockSpec((tm, D_MODEL), lambda i: (i, 0)),
            _full((1, D_MODEL)),
            pl.BlockSpec((tm, D_MODEL), lambda i: (i, 0)),
        ],
        out_specs=[pl.BlockSpec((tm, D_MODEL), lambda i: (i, 0)), _full((1, D_MODEL))],
        out_shape=[jax.ShapeDtypeStruct((seq, D_MODEL), F32), jax.ShapeDtypeStruct((1, D_MODEL), F32)],
        scratch_shapes=[], semantics=("arbitrary",),
    )


def inproj_bwd_w(h, dproj, dqkv):
    seq = h.shape[0]
    ts, tn = min(seq, 512), 512

    def body(h_ref, dp_ref, dq_ref, dk_ref, dv_ref, dw_ref):
        @pl.when(pl.program_id(0) == 0)
        def _():
            dw_ref[...] = jnp.zeros_like(dw_ref)

        ht = h_ref[...].T
        for at, part in _dproj_parts(dp_ref, (dq_ref, dk_ref, dv_ref), slice(None)):
            for c in range(0, part.shape[1], tn):
                dw_ref[0, :, at + c:at + c + tn] += _dot(ht, part[:, c:c + tn])

    third = lambda: pl.BlockSpec((ts, C_WIDTH), lambda s: (s, 0))
    return pl.pallas_call(
        body,
        name="inproj_bwd_w",
        grid=(seq // ts,),
        in_specs=[pl.BlockSpec((ts, D_MODEL), lambda s: (s, 0)), pl.BlockSpec((ts, D_INT), lambda s: (s, 0)),
                  third(), third(), third()],
        out_specs=_full((1, D_MODEL, D_INT)),
        out_shape=jax.ShapeDtypeStruct((1, D_MODEL, D_INT), F32),
        compiler_params=_cparams("arbitrary"),
    )(h, dproj, *dqkv)


N_IN = 3848


def _internal_of(col):
    return col if col < 768 else (col + 256 if col < 3840 else 768 + col - 3840)


def _column_runs(n_shard):
    runs = []
    for d in range(N_IN // n_shard):
        mine = []
        for j in range(n_shard):
            ci = _internal_of(d * n_shard + j)
            if mine and mine[-1][0] + mine[-1][1] == ci:
                mine[-1][1] += 1
            else:
                mine.append([ci, 1, j])
        runs.append(mine)
    return runs


def assemble_w_in(wi_all):
    n_dev, depth, _, n_shard = wi_all.shape
    tr = 256
    pieces = [[] for _ in range(D_INT // LANES)]
    for d, mine in enumerate(_column_runs(n_shard)):
        for ci, ln, off in mine:
            while ln > 0:
                blk, at = divmod(ci, LANES)
                take = min(ln, LANES - at)
                pieces[blk].append((at, take, d, off))
                ci, ln, off = ci + take, ln - take, off + take

    def body(x_ref, o_ref):
        for blk, parts in enumerate(pieces):
            vals, at = [], 0
            for start, ln, d, off in sorted(parts):
                if start > at:
                    vals.append(jnp.zeros((tr, start - at), BF16))
                vals.append(x_ref[d, 0, :, off:off + ln])
                at = start + ln
            if at < LANES:
                vals.append(jnp.zeros((tr, LANES - at), BF16))
            o_ref[0, :, blk * LANES:(blk + 1) * LANES] = vals[0] if len(vals) == 1 else jnp.concatenate(vals, axis=1)

    return pl.pallas_call(
        body,
        name="assemble_w_in",
        grid=(depth, D_MODEL // tr),
        in_specs=[pl.BlockSpec((n_dev, 1, tr, n_shard), lambda l, r: (0, l, r, 0))],
        out_specs=pl.BlockSpec((1, tr, D_INT), lambda l, r: (l, r, 0)),
        out_shape=jax.ShapeDtypeStruct((depth, D_MODEL, D_INT), BF16),
        compiler_params=_cparams("parallel", "parallel"),
    )(wi_all)


def split_w_in_grad(dwi, n_shard, core):
    tr = 256
    runs = _column_runs(n_shard)

    def body(core_ref, x_ref, keep_ref, send_ref):
        for d, mine in enumerate(runs):
            @pl.when(core_ref[0] == d % 2)
            def _():
                for ci, ln, off in mine:
                    keep_ref[d // 2, :, off:off + ln] = x_ref[0, :, ci:ci + ln]

            @pl.when(core_ref[0] != d % 2)
            def _():
                for ci, ln, off in mine:
                    send_ref[d // 2, :, off:off + ln] = x_ref[0, :, ci:ci + ln].astype(BF16)

    shards = lambda: pl.BlockSpec((N_CHIP, tr, n_shard), lambda r, s: (0, r, 0))
    grid_spec = pltpu.PrefetchScalarGridSpec(
        num_scalar_prefetch=1, grid=(D_MODEL // tr,),
        in_specs=[pl.BlockSpec((1, tr, D_INT), lambda r, s: (0, r, 0))], out_specs=[shards(), shards()])
    return pl.pallas_call(
        body,
        name="split_w_in_grad",
        grid_spec=grid_spec,
        out_shape=[jax.ShapeDtypeStruct((N_CHIP, D_MODEL, n_shard), F32), jax.ShapeDtypeStruct((N_CHIP, D_MODEL, n_shard), BF16)],
        compiler_params=_cparams("parallel"),
    )(core, dwi)


C_WIDTH = 512
C_HEADS = 8
C_HDIM = 64
C_PAIRS = C_HEADS // 2
C_BQ = 512
C_TAIL = 16
C_KG = 4


def _split3(x):
    hi = x.astype(BF16)
    r = x - hi.astype(F32)
    mid = r.astype(BF16)
    return hi, mid, (r - mid.astype(F32)).astype(BF16)


def _piece_selectors():
    sel = np.zeros((C_HEADS, 3 * LANES, LANES), np.float32)
    for p in range(C_PAIRS):
        for e in range(2):
            for t in range(3):
                sel[2 * p + e, t * LANES + 2 * p + e, 3 * e + t] = -1.0
    return sel


def fox_prep(proj, qkv, bf_row):
    seq = proj.shape[0]
    nblk = seq // CHUNK
    tril = jnp.asarray(np.tril(np.ones((CHUNK, CHUNK), np.float32)), BF16)
    sel = jnp.asarray(_piece_selectors(), BF16)
    rows_t = CHUNK + C_TAIL

    def body(fl_ref, q_ref, k_ref, v_ref, bf_ref, l_ref, sel_ref, ka_ref, va_ref, vt_ref, kt_ref, qt_ref, qa_ref, carry_ref):
        @pl.when(pl.program_id(0) == 0)
        def _():
            carry_ref[...] = jnp.zeros_like(carry_ref)

        lf = jax.nn.log_sigmoid(fl_ref[:, :LANES] + bf_ref[...])
        c = _exact_times(l_ref[...], lf, 3) + carry_ref[...]
        carry_ref[...] += jnp.sum(lf, axis=0, keepdims=True)
        c3 = jnp.concatenate(_split3(c), axis=1)
        lane = lax.broadcasted_iota(jnp.int32, (CHUNK, LANES), 1)
        row = lax.broadcasted_iota(jnp.int32, (CHUNK, LANES), 0)
        r16 = lax.broadcasted_iota(jnp.int32, (C_TAIL, 2 * CHUNK), 0)
        l16 = lax.broadcasted_iota(jnp.int32, (C_TAIL, 2 * CHUNK), 1)
        zero = jnp.zeros((CHUNK, LANES), BF16)
        one = jnp.ones((CHUNK, LANES), BF16)

        def by_keys(x, right_a, right_b):
            xb = x.astype(BF16)
            top = jnp.concatenate([jnp.where(lane < C_HDIM, xb, zero), right_a], axis=1)
            return jnp.concatenate([top, jnp.concatenate([jnp.where(lane < C_HDIM, zero, xb), right_b], axis=1)], axis=0)

        def by_lanes(x, tail):
            xt = x.T.astype(BF16)
            main = jnp.concatenate([jnp.where(row < C_HDIM, xt, zero), jnp.where(row < C_HDIM, zero, xt)], axis=1)
            return jnp.concatenate([main, tail], axis=0)

        for p in range(C_PAIRS):
            cols = slice(p * LANES, (p + 1) * LANES)
            q2, k2, v2 = (r[:, cols].astype(F32) for r in (q_ref, k_ref, v_ref))
            q2 = q2 * (C_HDIM ** -0.5)
            negc = [_dot(c3, sel_ref[2 * p + e]).astype(BF16) for e in range(2)]
            ones3 = [jnp.where((lane >= 3 * e) & (lane < 3 * e + 3), one, zero) for e in range(2)]
            tail = jnp.where(((r16 == 2 * p) & (l16 < CHUNK)) | ((r16 == 2 * p + 1) & (l16 >= CHUNK)), 1.0, 0.0).astype(BF16)
            ka_ref[p] = by_keys(k2, negc[0], negc[1])
            va_ref[p] = by_keys(v2, ones3[0], ones3[1])
            kt_ref[p] = by_lanes(k2, tail)
            vt_ref[p] = by_lanes(v2, tail)
            qt_ref[p] = jnp.concatenate([q2.T.astype(BF16), jnp.where(row < 6, one, zero)], axis=0)
            qa_ref[p] = jnp.concatenate([q2.astype(BF16), jnp.where((lane == 2 * p) | (lane == 2 * p + 1), one, zero)], axis=1)

    wide = lambda j: pl.BlockSpec((CHUNK, C_WIDTH), lambda n: (n, j))
    by_rows = pl.BlockSpec((C_PAIRS, 2 * CHUNK, 2 * CHUNK), lambda n: (0, n, 0))
    by_cols = pl.BlockSpec((C_PAIRS, rows_t, 2 * CHUNK), lambda n: (0, 0, n))
    return pl.pallas_call(
        body,
        name="fox_prep",
        grid=(nblk,),
        in_specs=[pl.BlockSpec((CHUNK, 256), lambda n: (n, 3)), wide(0), wide(1), wide(2), _full((1, LANES)),
                  _full((CHUNK, CHUNK)), _full((C_HEADS, 3 * LANES, LANES))],
        out_specs=[by_rows, by_rows, by_cols, by_cols,
                   pl.BlockSpec((C_PAIRS, 2 * CHUNK, CHUNK), lambda n: (0, 0, n)),
                   pl.BlockSpec((C_PAIRS, CHUNK, 2 * CHUNK), lambda n: (0, n, 0))],
        out_shape=[jax.ShapeDtypeStruct((C_PAIRS, 2 * seq, 2 * CHUNK), BF16)] * 2
        + [jax.ShapeDtypeStruct((C_PAIRS, rows_t, 2 * seq), BF16)] * 2
        + [jax.ShapeDtypeStruct((C_PAIRS, 2 * CHUNK, seq), BF16), jax.ShapeDtypeStruct((C_PAIRS, seq, 2 * CHUNK), BF16)],
        scratch_shapes=[pltpu.VMEM((1, LANES), F32)],
        compiler_params=_cparams("arbitrary"),
    )(proj, qkv, qkv, qkv, bf_row, tril, sel)


def _visible(shape, key0, query0):
    row = lax.broadcasted_iota(jnp.int32, shape, 0)
    key = key0 + lax.shift_left(lax.shift_right_logical(row, 8), 7) + (row & (CHUNK - 1))
    return key <= query0 + lax.broadcasted_iota(jnp.int32, shape, 1)


def _rows_ab(a, b, n):
    return jnp.concatenate([jnp.broadcast_to(a, (C_HDIM, n)), jnp.broadcast_to(b, (C_HDIM, n))], axis=0)


def _call_carrying(ex, body, operands, *, name, grid, in_specs, out_specs, out_shape, scratch_shapes, semantics=None):
    if ex is None:
        semantics = semantics or ("parallel", *["arbitrary"] * (len(grid) - 1))
        return pl.pallas_call(body, name=name, grid=grid, in_specs=in_specs, out_specs=out_specs, out_shape=out_shape,
                              scratch_shapes=scratch_shapes, compiler_params=_cparams(*semantics))(*operands)
    n_in, n_out = len(in_specs), len(out_specs)

    def wrapped(*refs):
        own, parts = _carried_refs(refs, n_in, n_out, ex)
        ids = [pl.program_id(a) for a in range(len(grid))]
        pl.when(functools.reduce(jnp.logical_and, [i == 0 for i in ids]))(lambda: ex.start(*parts))
        if hasattr(ex, "relay"):
            linear = functools.reduce(lambda at, ig: at * ig[1] + ig[0], zip(ids, grid), 0)
            pl.when(linear == int(np.prod(grid)) // 2)(lambda: ex.relay(*parts))
        body(*own)
        pl.when(functools.reduce(jnp.logical_and, [i == g - 1 for i, g in zip(ids, grid)]))(lambda: ex.finish(*parts))

    return pl.pallas_call(
        wrapped, name=name, grid=grid,
        in_specs=list(in_specs) + [_ANY] * len(ex.inputs), out_specs=list(out_specs) + [_ANY] * len(ex.out_shape),
        out_shape=list(out_shape) + list(ex.out_shape), scratch_shapes=list(scratch_shapes) + list(ex.scratch),
        input_output_aliases={n_in + i: n_out + o for i, o in getattr(ex, "aliases", {}).items()},
        compiler_params=_cparams(*["arbitrary"] * len(grid)),
    )(*operands, *ex.inputs)


def fox_fwd(qt, ka, vt, carried=None):
    seq = qt.shape[2]
    nblk = seq // CHUNK
    bq = min(C_BQ, seq)
    grp = bq // CHUNK
    rows_t = CHUNK + C_TAIL

    def body(qt_ref, ka_ref, vt_ref, o_ref, lse_ref, acc_ref, s_ref):
        p, i = pl.program_id(0), pl.program_id(1)
        qtile = qt_ref[0]
        r16 = lax.broadcasted_iota(jnp.int32, (C_TAIL, bq), 0)

        def scores(t):
            at = pl.multiple_of(t * grp * 2 * CHUNK, 2 * CHUNK)
            return _dot(ka_ref[0, pl.ds(at, grp * 2 * CHUNK), :], qtile)

        def rescale(al_a, al_b):
            tail = jnp.where(r16 == 2 * p, al_a, jnp.where(r16 == 2 * p + 1, al_b, 1.0))
            return jnp.concatenate([_rows_ab(al_a, al_b, bq), tail], axis=0)

        def diagonal(m):
            ma, mb = m
            na, nb = ma, mb
            blocks = []
            for g in range(grp):
                s = s_ref[g * 2 * CHUNK:(g + 1) * 2 * CHUNK, g * CHUNK:]
                s = jnp.where(_visible(s.shape, i * bq + g * CHUNK, i * bq + g * CHUNK), s, -jnp.inf)
                blocks.append(s)
                unseen = [jnp.full((1, g * CHUNK), -jnp.inf, F32)] if g else []
                na = jnp.maximum(na, jnp.concatenate(unseen + [jnp.max(s[:CHUNK], axis=0, keepdims=True)], axis=1))
                nb = jnp.maximum(nb, jnp.concatenate(unseen + [jnp.max(s[CHUNK:], axis=0, keepdims=True)], axis=1))
            acc_ref[...] = acc_ref[...] * rescale(jnp.exp(ma - na), jnp.exp(mb - nb))
            for g in range(grp):
                n = bq - g * CHUNK
                n2 = jnp.concatenate([jnp.broadcast_to(na[:, g * CHUNK:], (CHUNK, n)),
                                      jnp.broadcast_to(nb[:, g * CHUNK:], (CHUNK, n))], axis=0)
                at = pl.multiple_of((i * grp + g) * 2 * CHUNK, 2 * CHUNK)
                pt = jnp.exp(blocks[g] - n2).astype(BF16)
                acc_ref[:, g * CHUNK:] += _dot(vt_ref[0, :, pl.ds(at, 2 * CHUNK)], pt)
            return na, nb

        def group(t, m):
            ma, mb = m
            at = pl.multiple_of(t * grp * 2 * CHUNK, 2 * CHUNK)
            s = s_ref[...]
            sa = [s[g * 2 * CHUNK:g * 2 * CHUNK + CHUNK] for g in range(grp)]
            sb = [s[g * 2 * CHUNK + CHUNK:(g + 1) * 2 * CHUNK] for g in range(grp)]
            na, nb = ma, mb
            for g in range(grp):
                na = jnp.maximum(na, jnp.max(sa[g], axis=0, keepdims=True))
                nb = jnp.maximum(nb, jnp.max(sb[g], axis=0, keepdims=True))
            al_a, al_b = jnp.exp(ma - na), jnp.exp(mb - nb)
            pt = jnp.concatenate([jnp.exp(x - n) for g in range(grp) for x, n in ((sa[g], na), (sb[g], nb))], axis=0)
            pv = _dot(vt_ref[0, :, pl.ds(at, grp * 2 * CHUNK)], pt.astype(BF16))
            acc_ref[...] = acc_ref[...] * rescale(al_a, al_b) + pv
            return na, nb

        def step(t, m):
            s_next = scores(t + 1)
            m = group(t, m)
            s_ref[...] = s_next
            return m

        acc_ref[...] = jnp.zeros_like(acc_ref)
        s_ref[...] = scores(0)
        m = (jnp.full((1, bq), -jnp.inf, F32), jnp.full((1, bq), -jnp.inf, F32))
        m = lax.fori_loop(0, i, step, m)
        ma, mb = diagonal(m)
        tailv = acc_ref[CHUNK:rows_t, :]
        la = jnp.sum(jnp.where(r16 == 2 * p, tailv, 0.0), axis=0, keepdims=True)
        lb = jnp.sum(jnp.where(r16 == 2 * p + 1, tailv, 0.0), axis=0, keepdims=True)
        o_ref[...] = (acc_ref[0:CHUNK, :] * _rows_ab(1.0 / la, 1.0 / lb, bq)).T
        lse_ref[0, 0:1, :] = ma + jnp.log(la)
        lse_ref[0, 1:2, :] = mb + jnp.log(lb)

    return _call_carrying(
        carried, body, (qt, ka, vt),
        name="fox_fwd",
        grid=(C_PAIRS, seq // bq),
        in_specs=[
            pl.BlockSpec((1, 2 * CHUNK, bq), lambda p, i: (p, 0, i)),
            pl.BlockSpec((1, 2 * seq, 2 * CHUNK), lambda p, i: (p, 0, 0)),
            pl.BlockSpec((1, rows_t, 2 * seq), lambda p, i: (p, 0, 0)),
        ],
        out_specs=[pl.BlockSpec((bq, LANES), lambda p, i: (i, p)), pl.BlockSpec((1, 2, bq), lambda p, i: (p, 0, i))],
        out_shape=[jax.ShapeDtypeStruct((seq, C_WIDTH), F32), jax.ShapeDtypeStruct((C_PAIRS, 2, seq), F32)],
        scratch_shapes=[pltpu.VMEM((rows_t, bq), F32), pltpu.VMEM((grp * 2 * CHUNK, bq), F32)],
    )


def fox_bwd_prep(dy, o, proj, dproj):
    seq = o.shape[0]
    ind = np.zeros((C_WIDTH, LANES), np.float32)
    for h in range(C_HEADS):
        ind[h * C_HDIM:(h + 1) * C_HDIM, h] = 1.0
    ind = jnp.asarray(ind, BF16)
    sel = _piece_selectors()
    sel = jnp.asarray(np.stack([sel[2 * p].T + sel[2 * p + 1].T for p in range(C_PAIRS)]), BF16)

    def body(dy_ref, o_ref, z_ref, ind_ref, sel_ref, _, do_ref, dz_ref, dot_ref):
        dy_c, o_v, z = dy_ref[...], o_ref[...], z_ref[...]
        sg = jax.nn.sigmoid(z)
        do = dy_c * (z * sg)
        do_ref[...] = do.astype(BF16)
        dz_ref[...] = (dy_c * o_v * (sg * (1.0 + z * (1.0 - sg)))).astype(BF16)
        prod = do * o_v
        hi = prod.astype(BF16)
        lo = (prod - hi.astype(F32)).astype(BF16)
        delta = _dot(hi, ind_ref[...]) + _dot(lo, ind_ref[...])
        d3 = jnp.concatenate(_split3(delta.T), axis=0)
        for p in range(C_PAIRS):
            tail = _dot(sel_ref[p], d3).astype(BF16)
            dot_ref[p] = jnp.concatenate([do[:, p * LANES:(p + 1) * LANES].T.astype(BF16), tail], axis=0)

    return pl.pallas_call(
        body,
        name="fox_bwd_prep",
        grid=(seq // CHUNK,),
        in_specs=[
            pl.BlockSpec((CHUNK, C_WIDTH), lambda i: (i, 1)),
            pl.BlockSpec((CHUNK, C_WIDTH), lambda i: (i, 0)),
            pl.BlockSpec((CHUNK, C_WIDTH), lambda i: (i, P_Z_BLOCK)),
            _full((C_WIDTH, LANES)), _full((C_PAIRS, LANES, 3 * LANES)), _ANY,
        ],
        out_specs=[
            pl.BlockSpec((CHUNK, C_WIDTH), lambda i: (i, 0)),
            pl.BlockSpec((CHUNK, C_WIDTH), lambda i: (i, 7)),
            pl.BlockSpec((C_PAIRS, 2 * CHUNK, CHUNK), lambda i: (0, 0, i)),
        ],
        out_shape=[jax.ShapeDtypeStruct((seq, C_WIDTH), BF16), jax.ShapeDtypeStruct(dproj.shape, BF16),
                   jax.ShapeDtypeStruct((C_PAIRS, 2 * CHUNK, seq), BF16)],
        input_output_aliases={5: 1},
        compiler_params=_cparams("parallel"),
    )(dy, o, proj, ind, sel, dproj)


def fox_bwd(ka, va, kt, qt, dot_t, qa, dob, lse, carried=None):
    seq = qt.shape[2]
    nblk = seq // CHUNK
    bq = min(C_BQ, seq)
    nq = seq // bq
    kg = min(C_KG, nblk)
    ng = nblk // kg
    rows_t = CHUNK + C_TAIL

    def body(ka_ref, va_ref, kt_ref, qt_ref, dot_ref, qa_ref, do_ref, lse_ref,
             dq_ref, dk_ref, dv_ref, dck_ref, dcq_ref, dqt_acc, dv_acc, dka_acc):
        p, jg = pl.program_id(0), pl.program_id(1)

        @pl.when(jg == 0)
        def _():
            dqt_acc[...] = jnp.zeros_like(dqt_acc)

        dv_acc[...] = jnp.zeros_like(dv_acc)
        dka_acc[...] = jnp.zeros_like(dka_acc)

        def step(i, carry):
            cols = pl.ds(pl.multiple_of(i * bq, bq), bq)
            qtile, dotile = qt_ref[0, :, cols], dot_ref[0, :, cols]
            do, qa_i = do_ref[cols, :], qa_ref[0, cols, :]
            lse2 = jnp.concatenate([jnp.broadcast_to(lse_ref[0, 0:1, cols], (CHUNK, bq)),
                                    jnp.broadcast_to(lse_ref[0, 1:2, cols], (CHUNK, bq))] * kg, axis=0)
            pt = jnp.exp(_dot(ka_ref[0], qtile) - lse2)
            ds = pt * _dot(va_ref[0], dotile)
            ptb, dsb = pt.astype(BF16), ds.astype(BF16)
            dv_acc[...] += _dot(ptb, do)
            dka_acc[...] += _dot(dsb, qa_i)
            dqt_acc[:, cols] += _dot(kt_ref[0], dsb)
            return carry

        def diagonal(i):
            cols = [pl.ds(pl.multiple_of(i * bq + kb * CHUNK, CHUNK), bq - kb * CHUNK) for kb in range(kg)]
            rows = [slice(kb * 2 * CHUNK, (kb + 1) * 2 * CHUNK) for kb in range(kg)]
            s = [_dot(ka_ref[0, rows[kb], :], qt_ref[0, :, cols[kb]]) for kb in range(kg)]
            dp = [_dot(va_ref[0, rows[kb], :], dot_ref[0, :, cols[kb]]) for kb in range(kg)]
            ptb, dsb = [], []
            for kb in range(kg):
                n = bq - kb * CHUNK
                lse2 = jnp.concatenate([jnp.broadcast_to(lse_ref[0, 0:1, cols[kb]], (CHUNK, n)),
                                        jnp.broadcast_to(lse_ref[0, 1:2, cols[kb]], (CHUNK, n))], axis=0)
                pt = jnp.exp(s[kb] - lse2)
                pt = jnp.where(_visible(pt.shape, (jg * kg + kb) * CHUNK, i * bq + kb * CHUNK), pt, 0.0)
                ptb.append(pt.astype(BF16))
                dsb.append((pt * dp[kb]).astype(BF16))
            for kb in range(kg):
                dv_acc[rows[kb], :] += _dot(ptb[kb], do_ref[cols[kb], :])
                dka_acc[rows[kb], :] += _dot(dsb[kb], qa_ref[0, cols[kb], :])
                dqt_acc[:, cols[kb]] += _dot(kt_ref[0, :, rows[kb]], dsb[kb])

        assert kg * CHUNK == bq
        diagonal(jg)
        lax.fori_loop(jg + 1, nq, step, 0)
        lane = lax.broadcasted_iota(jnp.int32, (CHUNK, LANES), 1)
        for kb in range(kg):
            rows = slice(kb * CHUNK, (kb + 1) * CHUNK)
            ra = slice(kb * 2 * CHUNK, kb * 2 * CHUNK + CHUNK)
            rb = slice(kb * 2 * CHUNK + CHUNK, (kb + 1) * 2 * CHUNK)
            dk_ref[rows, :] = jnp.where(lane < C_HDIM, dka_acc[ra, 0:LANES], dka_acc[rb, 0:LANES]).astype(BF16)
            dv_ref[rows, :] = jnp.where(lane < C_HDIM, dv_acc[ra, :], dv_acc[rb, :]).astype(BF16)
            dck_ref[0, rows, :] = (jnp.where(lane == 2 * p, dka_acc[ra, LANES:], 0.0)
                                   + jnp.where(lane == 2 * p + 1, dka_acc[rb, LANES:], 0.0))

        @pl.when(jg == ng - 1)
        def _():
            for c in range(nq):
                dq_ref[c * bq:(c + 1) * bq, :] = (dqt_acc[0:CHUNK, c * bq:(c + 1) * bq].T * (C_HDIM ** -0.5)).astype(BF16)
            dcq_ref[0] = dqt_acc[CHUNK:rows_t, :]

    per_pair = lambda r, c: pl.BlockSpec((1, r, c), lambda p, j: (p, 0, 0))
    by_rows = pl.BlockSpec((1, kg * 2 * CHUNK, 2 * CHUNK), lambda p, j: (p, j, 0))
    by_cols = pl.BlockSpec((1, rows_t, kg * 2 * CHUNK), lambda p, j: (p, 0, j))
    return _call_carrying(
        carried, body, (ka, va, kt, qt, dot_t, qa, dob, lse),
        name="fox_bwd",
        grid=(C_PAIRS, ng),
        in_specs=[by_rows, by_rows, by_cols, per_pair(2 * CHUNK, seq), per_pair(2 * CHUNK, seq),
                  per_pair(seq, 2 * CHUNK), pl.BlockSpec((seq, LANES), lambda p, j: (0, p)), per_pair(2, seq)],
        out_specs=[pl.BlockSpec((seq, LANES), lambda p, j: (0, p)),
                   pl.BlockSpec((kg * CHUNK, LANES), lambda p, j: (j, p)),
                   pl.BlockSpec((kg * CHUNK, LANES), lambda p, j: (j, p)),
                   pl.BlockSpec((1, kg * CHUNK, LANES), lambda p, j: (p, j, 0)),
                   per_pair(C_TAIL, seq)],
        out_shape=[jax.ShapeDtypeStruct((seq, C_WIDTH), BF16)] * 3
        + [jax.ShapeDtypeStruct((C_PAIRS, seq, LANES), F32), jax.ShapeDtypeStruct((C_PAIRS, C_TAIL, seq), F32)],
        scratch_shapes=[pltpu.VMEM((rows_t, seq), F32), pltpu.VMEM((kg * 2 * CHUNK, LANES), F32),
                        pltpu.VMEM((kg * 2 * CHUNK, 2 * CHUNK), F32)],
    )


def fox_post(dcq, dck, proj, bf_row, dproj):
    seq = proj.shape[0]
    nc = seq // CHUNK
    triu = jnp.asarray(np.triu(np.ones((CHUNK, CHUNK), np.float32)), BF16)

    def body(dq_ref, dk_ref, fl_ref, bf_ref, u_ref, _, dfl_ref, dbf_ref, carry_ref):
        @pl.when(pl.program_id(0) == 0)
        def _():
            carry_ref[...] = jnp.zeros_like(carry_ref)
            dbf_ref[...] = jnp.zeros_like(dbf_ref)

        rows = (dq_ref[0] + dq_ref[1]) + (dq_ref[2] + dq_ref[3])
        dc = jnp.concatenate([rows, jnp.zeros((CHUNK - C_TAIL, CHUNK), F32)], axis=0).T
        dc = dc - ((dk_ref[0] + dk_ref[1]) + (dk_ref[2] + dk_ref[3]))
        g = _exact_times(u_ref[...], dc, 3) + carry_ref[...]
        carry_ref[...] += jnp.sum(dc, axis=0, keepdims=True)
        dfl = g * jax.nn.sigmoid(-(fl_ref[:, :LANES] + bf_ref[...]))
        dbf_ref[...] += jnp.sum(dfl, axis=0, keepdims=True)
        dfl_ref[...] = jnp.concatenate([dfl, jnp.zeros_like(dfl)], axis=1).astype(BF16)

    rev = lambda n: nc - 1 - n
    return pl.pallas_call(
        body,
        name="fox_post",
        grid=(nc,),
        in_specs=[
            pl.BlockSpec((C_PAIRS, C_TAIL, CHUNK), lambda n: (0, 0, rev(n))),
            pl.BlockSpec((C_PAIRS, CHUNK, LANES), lambda n: (0, rev(n), 0)),
            pl.BlockSpec((CHUNK, 256), lambda n: (rev(n), 3)),
            _full((1, LANES)), _full((CHUNK, CHUNK)), _ANY,
        ],
        out_specs=[pl.BlockSpec((CHUNK, 256), lambda n: (rev(n), 3)), _full((1, LANES))],
        out_shape=[jax.ShapeDtypeStruct(dproj.shape, BF16), jax.ShapeDtypeStruct((1, LANES), F32)],
        input_output_aliases={5: 0},
        scratch_shapes=[pltpu.VMEM((1, LANES), F32)],
        compiler_params=_cparams("arbitrary"),
    )(dcq, dck, proj, bf_row, triu, dproj)


N_DEV = 8
MESH = pl.DeviceIdType.MESH
_ANY = pl.BlockSpec(memory_space=pl.ANY)


def _mesh_pos():
    return lax.axis_index("x"), lax.axis_index("y"), lax.axis_index("c")


def _dev_index(px, py, pc):
    return 4 * px + 2 * py + pc


def _row_pieces(ref, rows):
    return [ref.at[idx + (pl.ds(r, rows),)] for idx in np.ndindex(*ref.shape[:-2]) for r in range(0, ref.shape[-2], rows)]


class _Transfer:
    def __init__(self, src, dst, rows, send_sem, recv_sem, to):
        self.src, self.dst, self.rows, self.sems, self.to = src, dst, rows, (send_sem, recv_sem), to

    def _copy(self, src, dst):
        return pltpu.make_async_remote_copy(src_ref=src, dst_ref=dst, send_sem=self.sems[0], recv_sem=self.sems[1],
                                            device_id=self.to, device_id_type=MESH)

    def start(self):
        for s, d in zip(_row_pieces(self.src, self.rows), _row_pieces(self.dst, self.rows), strict=True):
            self._copy(s, d).start()

    def wait_send(self):
        self._copy(self.src, self.dst).wait_send()

    def wait_recv(self):
        self._copy(self.src, self.dst).wait_recv()


def _exchange_call(ex, name):
    n_in, n_out = len(ex.inputs), len(ex.out_shape)

    def body(*refs):
        parts = refs[:n_in], refs[n_in:n_in + n_out], refs[n_in + n_out:]
        ex.start(*parts)
        getattr(ex, "relay", lambda *_: None)(*parts)
        ex.finish(*parts)

    return pl.pallas_call(body, name=name, in_specs=[_ANY] * n_in, out_specs=[_ANY] * n_out, out_shape=ex.out_shape,
                          scratch_shapes=ex.scratch, input_output_aliases=getattr(ex, "aliases", {}))(*ex.inputs)


def _carried_refs(refs, n_in, n_out, ex):
    k_in, k_out, k_sem = (len(ex.inputs), len(ex.out_shape), len(ex.scratch)) if ex else (0, 0, 0)
    a, b, c = n_in + k_in, n_in + k_in + n_out, n_in + k_in + n_out + k_out
    own = refs[:n_in] + refs[a:b] + refs[c:len(refs) - k_sem]
    return own, (refs[n_in:a], refs[b:c], refs[len(refs) - k_sem:])


class AllGatherWeights:
    def __init__(self, blocks):
        n = len(blocks)
        self.inputs = tuple(blocks)
        self.out_shape = [jax.ShapeDtypeStruct((N_DEV,) + b.shape, b.dtype) for b in blocks]
        self.scratch = ([pltpu.SemaphoreType.DMA((n, 7)), pltpu.SemaphoreType.DMA((n, 7)), pltpu.SemaphoreType.DMA((n, 2))]
                        + [pltpu.VMEM(b.shape, b.dtype) for b in blocks])

    def _plan(self, ins, outs, scratch):
        send_sems, recv_sems, local_sems, *staged = scratch
        x, y, c = _mesh_pos()
        me, sibling = (x, y, c), (x, y, 1 - c)
        chips = [(1 - x, y), (x, 1 - y), (1 - x, 1 - y)]
        every = range(len(ins))

        def copy(a, k, block, to, own=False):
            slot = outs[a].at[_dev_index(*block)]
            return _Transfer(ins[a] if own else slot, slot, ins[a].shape[-2], send_sems.at[a, k], recv_sems.at[a, k], to)

        mine = [(pltpu.make_async_copy(ins[a], staged[a], local_sems.at[a, 0]),
                 pltpu.make_async_copy(staged[a], outs[a].at[_dev_index(*me)], local_sems.at[a, 1])) for a in every]
        first = [copy(a, 1 + j, me, (*chip, c), own=True) for j, chip in enumerate(chips) for a in every]
        first += [copy(a, 0, me, sibling, own=True) for a in every]
        passed = [[copy(a, 4 + j, (*chip, c), sibling) for a in every] for j, chip in enumerate(chips)]
        return me, sibling, chips, c, every, copy, mine, first, passed

    def start(self, ins, outs, scratch):
        *_, mine, first, _ = self._plan(ins, outs, scratch)
        for to_vmem, _ in mine:
            to_vmem.start()
        for cp in first:
            cp.start()

    def relay(self, ins, outs, scratch):
        me, sibling, chips, c, every, copy, mine, first, passed = self._plan(ins, outs, scratch)
        for to_vmem, to_slot in mine:
            to_vmem.wait()
            to_slot.start()
        for j, chip in enumerate(chips):
            for a in every:
                copy(a, 1 + j, (*chip, c), me).wait_recv()
            for cp in passed[j]:
                cp.start()

    def finish(self, ins, outs, scratch):
        me, sibling, chips, c, every, copy, mine, first, passed = self._plan(ins, outs, scratch)
        for a in every:
            copy(a, 0, sibling, me).wait_recv()
        for j, chip in enumerate(chips):
            for a in every:
                copy(a, 4 + j, (*chip, 1 - c), me).wait_recv()
        for cp in first + [cp for group in passed for cp in group]:
            cp.wait_send()
        for _, to_slot in mine:
            to_slot.wait()


N_CHIP = 4


class PairExchange:
    def __init__(self, by_core, whole=()):
        self.inputs = tuple(by_core) + tuple(whole)
        self.n_by_core = len(by_core)
        self.out_shape = ([jax.ShapeDtypeStruct(a.shape[1:], a.dtype) for a in by_core]
                          + [jax.ShapeDtypeStruct(a.shape, a.dtype) for a in whole])
        n = len(self.inputs)
        self.scratch = [pltpu.SemaphoreType.DMA((n,)), pltpu.SemaphoreType.DMA((n,))]

    def _copies(self, ins, outs, sems):
        x, y, c = _mesh_pos()
        srcs = [r.at[1 - c] if a < self.n_by_core else r for a, r in enumerate(ins)]
        return [_Transfer(srcs[a], outs[a], outs[a].shape[-2], sems[0].at[a], sems[1].at[a], (x, y, 1 - c))
                for a in range(len(ins))]

    def start(self, ins, outs, sems):
        for cp in self._copies(ins, outs, sems):
            cp.start()

    def finish(self, ins, outs, sems):
        copies = self._copies(ins, outs, sems)
        for cp in copies:
            cp.wait_recv()
        for cp in copies:
            cp.wait_send()


def pair_sum(own, other, dtype, rows, name, core, layer, depth, stacked=None):
    n, n_r, n_c = other.shape
    by_core = own.ndim == 4
    own = own if by_core else own[None]

    def body(core_ref, a_ref, b_ref, *refs):
        refs[-1][0, 0] = (a_ref[0, 0] + b_ref[0].astype(F32)).astype(dtype)

    carried = () if stacked is None else (stacked,)
    grid_spec = pltpu.PrefetchScalarGridSpec(
        num_scalar_prefetch=1,
        grid=(n, n_r // rows),
        in_specs=[pl.BlockSpec((1, 1, rows, n_c), lambda i, r, s: (s[0] if by_core else 0, i, r, 0)),
                  pl.BlockSpec((1, rows, n_c), lambda i, r, s: (i, r, 0))] + [_ANY] * len(carried),
        out_specs=pl.BlockSpec((1, 1, rows, n_c), lambda i, r, s: (i, layer, r, 0)),
    )
    return pl.pallas_call(
        body,
        name=name,
        grid_spec=grid_spec,
        out_shape=jax.ShapeDtypeStruct((n, depth, n_r, n_c), dtype),
        input_output_aliases={3: 0} if carried else {},
        compiler_params=_cparams("parallel", "parallel"),
    )(core, own, other, *carried)


def small_sum(a, b, name):
    def body(a_ref, b_ref, o_ref):
        o_ref[...] = a_ref[...] + b_ref[...]

    return pl.pallas_call(body, name=name, out_shape=jax.ShapeDtypeStruct(a.shape, a.dtype))(a, b)


class ChipExchange:
    def __init__(self, by_chip=(), layers=(), gathered=(), stacked=()):
        stacked = tuple(stacked) or (None,) * len(by_chip)
        kept = [s for s in stacked if s is not None]
        self.inputs = tuple(by_chip) + tuple(gathered) + tuple(kept)
        self.n_by_chip, self.n_gathered = len(by_chip), len(gathered)
        self.items = [(a, l) for a in range(len(by_chip)) for l in layers[a]] + [(self.n_by_chip + g, None) for g in range(len(gathered))]
        self.out_shape = ([jax.ShapeDtypeStruct((N_CHIP - 1,) + a.shape[1:], a.dtype) for a in by_chip]
                          + [jax.ShapeDtypeStruct((N_CHIP,) + a.shape, a.dtype) for a in gathered])
        at = iter(range(self.n_by_chip + self.n_gathered, len(self.inputs)))
        self.aliases = {next(at): a for a, s in enumerate(stacked) if s is not None}
        n = len(self.items)
        self.scratch = [pltpu.SemaphoreType.DMA((n, 3)), pltpu.SemaphoreType.DMA((n, 3)),
                        pltpu.SemaphoreType.DMA((max(self.n_gathered, 1),))]

    def _plan(self, ins, outs, sems):
        x, y, c = _mesh_pos()
        chip = 2 * x + y
        n = len(self.items)

        def copy(i, k, sending):
            a, layer = self.items[i]
            px, py = x ^ ((k >> 1) & 1), y ^ (k & 1)
            if layer is not None:
                src, dst = ins[a].at[2 * px + py, layer], outs[a].at[k - 1, layer]
            else:
                src, dst = ins[a], outs[a].at[chip if sending else 2 * px + py]
            return _Transfer(src, dst, dst.shape[-2], sems[0].at[i, k - 1], sems[1].at[i, k - 1], (px, py, c))

        local = [pltpu.make_async_copy(ins[a], outs[a].at[chip], sems[2].at[a - self.n_by_chip])
                 for a in range(self.n_by_chip, self.n_by_chip + self.n_gathered)]
        return n, copy, local

    def start(self, ins, outs, sems):
        n, copy, local = self._plan(ins, outs, sems)
        for cp in local:
            cp.start()
        for k in range(1, N_CHIP):
            for a in range(n):
                copy(a, k, True).start()

    def finish(self, ins, outs, sems):
        n, copy, local = self._plan(ins, outs, sems)
        for k in range(1, N_CHIP):
            for a in range(n):
                copy(a, k, False).wait_recv()
        for k in range(1, N_CHIP):
            for a in range(n):
                copy(a, k, True).wait_send()
        for cp in local:
            cp.wait()


ADAM_LR = 0.001
ADAM_B1 = 0.9
ADAM_B2 = 0.999
ADAM_EPS = 1e-08
ADAM_WD = 0.01
ADAM_STEP = 10


def adam_reduce(parts, w, m, v, rows, name, own=None, chip=None):
    n_l, n_r, n_c = w.shape
    n_parts = parts.shape[0]

    def body(*refs):
        p_ref, w_ref, m_ref, v_ref, g_ref, d_ref, m2_ref, v2_ref = refs[-8:]
        g = p_ref[0, 0].astype(F32)
        if own is not None:
            g = refs[-9][...].reshape(rows, n_c).astype(F32) + g
        for d in range(1, n_parts):
            g = g + p_ref[d, 0].astype(F32)
        m2 = ADAM_B1 * m_ref[0] + (1.0 - ADAM_B1) * g
        v2 = ADAM_B2 * v_ref[0] + (1.0 - ADAM_B2) * (g * g)
        m_hat = m2 / (1.0 - ADAM_B1 ** ADAM_STEP)
        v_hat = v2 / (1.0 - ADAM_B2 ** ADAM_STEP)
        g_ref[0] = g
        d_ref[0] = -ADAM_LR * (m_hat / (jnp.sqrt(v_hat) + ADAM_EPS) + ADAM_WD * w_ref[0])
        m2_ref[0] = m2
        v2_ref[0] = v2

    blk = lambda: pl.BlockSpec((1, rows, n_c), lambda l, r, *_: (l, r, 0))
    in_specs = [pl.BlockSpec((n_parts, 1, rows, n_c), lambda l, r, *_: (0, l, r, 0)), blk(), blk(), blk()]
    args = (parts, w, m, v)
    if own is not None:
        in_specs = [pl.BlockSpec((1, 1, rows, n_c), lambda l, r, s: (s[0], l, r, 0))] + in_specs
        args = (chip, own) + args
    grid_spec = pltpu.PrefetchScalarGridSpec(
        num_scalar_prefetch=0 if own is None else 1, grid=(n_l, n_r // rows), in_specs=in_specs,
        out_specs=[blk(), blk(), blk(), blk()])
    return pl.pallas_call(
        body,
        name=name,
        grid_spec=grid_spec,
        out_shape=[jax.ShapeDtypeStruct(w.shape, F32)] * 4,
        compiler_params=_cparams("parallel", "parallel"),
    )(*args)


def adam_reduce_columns(parts, w, m, v, name, own, chip):
    n_l, n_r, n_c = w.shape
    n_parts = parts.shape[0]
    view = lambda a: jnp.transpose(a, (2, 0, 1))

    def body(_, own_ref, p_ref, w_ref, m_ref, v_ref, g_ref, d_ref, m2_ref, v2_ref):
        for l in range(n_l):
            g = own_ref[0, l].astype(F32) + p_ref[0, l].astype(F32)
            for d in range(1, n_parts):
                g = g + p_ref[d, l].astype(F32)
            g = g.T
            w_l, m_l, v_l = w_ref[:, l, :], m_ref[:, l, :], v_ref[:, l, :]
            m2 = ADAM_B1 * m_l + (1.0 - ADAM_B1) * g
            v2 = ADAM_B2 * v_l + (1.0 - ADAM_B2) * (g * g)
            m_hat = m2 / (1.0 - ADAM_B1 ** ADAM_STEP)
            v_hat = v2 / (1.0 - ADAM_B2 ** ADAM_STEP)
            g_ref[:, l, :] = g
            d_ref[:, l, :] = -ADAM_LR * (m_hat / (jnp.sqrt(v_hat) + ADAM_EPS) + ADAM_WD * w_l)
            m2_ref[:, l, :] = m2
            v2_ref[:, l, :] = v2

    blk = lambda: pl.BlockSpec((LANES, n_l, n_r), lambda c, s: (c, 0, 0))
    grid_spec = pltpu.PrefetchScalarGridSpec(
        num_scalar_prefetch=1, grid=(pl.cdiv(n_c, LANES),),
        in_specs=[pl.BlockSpec((1, n_l, n_r, LANES), lambda c, s: (s[0], 0, 0, c)),
                  pl.BlockSpec((n_parts, n_l, n_r, LANES), lambda c, s: (0, 0, 0, c)), blk(), blk(), blk()],
        out_specs=[blk(), blk(), blk(), blk()])
    outs = pl.pallas_call(
        body,
        name=name,
        grid_spec=grid_spec,
        out_shape=[jax.ShapeDtypeStruct((n_c, n_l, n_r), F32)] * 4,
        compiler_params=_cparams("parallel"),
    )(chip, own, parts, view(w), view(m), view(v))
    return [jnp.transpose(o, (1, 2, 0)) for o in outs]


_SMALL = (("norm_g", (2, 1024)), ("gmlp_ln_g", (2, 4, 64)), ("gmlp_ln_b", (2, 4, 64)),
          ("gmlp_b_s", (2, 4, 128)), ("hgrn_lb", (2, 256)), ("hgrn_onorm_g", (2, 64)), ("fox_b_f", (2, 8)),
          ("final_norm_g", (1024,)), ("loss", ()))


def _padded(n):
    return -(-n // LANES) * LANES


_SMALL_ROWS = -(-sum(_padded(int(np.prod(s))) for _, s in _SMALL) // LANES // 8) * 8


def _pack_small(vals):
    flat = []
    for (name, shape), a in zip(_SMALL, vals, strict=True):
        n = int(np.prod(shape))
        flat.append(jnp.pad(a.reshape(n).astype(F32), (0, _padded(n) - n)))
    flat = jnp.concatenate(flat)
    return jnp.pad(flat, (0, _SMALL_ROWS * LANES - flat.shape[0])).reshape(_SMALL_ROWS, LANES)


def _unpack_small(slab):
    flat, out, at = slab.reshape(-1), {}, 0
    for name, shape in _SMALL:
        n = int(np.prod(shape))
        out[name] = flat[at:at + n].reshape(shape)
        at += _padded(n)
    return out


def sum_parts(parts, name):
    def body(p_ref, o_ref):
        g = p_ref[0]
        for d in range(1, parts.shape[0]):
            g = g + p_ref[d]
        o_ref[...] = g

    return pl.pallas_call(body, name=name, out_shape=jax.ShapeDtypeStruct(parts.shape[1:], F32))(parts)


def adam_small(gs, ws, ms, vs):
    n = len(gs)

    def body(*refs):
        for k in range(n):
            g, w, m, v = (refs[j * n + k][...] for j in range(4))
            m2 = ADAM_B1 * m + (1.0 - ADAM_B1) * g
            v2 = ADAM_B2 * v + (1.0 - ADAM_B2) * (g * g)
            m_hat = m2 / (1.0 - ADAM_B1 ** ADAM_STEP)
            v_hat = v2 / (1.0 - ADAM_B2 ** ADAM_STEP)
            refs[4 * n + k][...] = -ADAM_LR * (m_hat / (jnp.sqrt(v_hat) + ADAM_EPS) + ADAM_WD * w)
            refs[5 * n + k][...] = m2
            refs[6 * n + k][...] = v2

    outs = pl.pallas_call(body, name="adam_small",
                          out_shape=[jax.ShapeDtypeStruct(w.shape, F32) for _ in range(3) for w in ws])(*gs, *ws, *ms, *vs)
    return outs[:n], outs[n:2 * n], outs[2 * n:]


def kernel(x, norm_g, w_in, w_out, gmlp_ln_g, gmlp_ln_b, gmlp_w_s, gmlp_b_s, hgrn_lb, hgrn_onorm_g, fox_b_f, final_norm_g, loss_target, m_norm_g, m_w_in, m_w_out, m_gmlp_ln_g, m_gmlp_ln_b, m_gmlp_w_s, m_gmlp_b_s, m_hgrn_lb, m_hgrn_onorm_g, m_fox_b_f, m_final_norm_g, v_norm_g, v_w_in, v_w_out, v_gmlp_ln_g, v_gmlp_ln_b, v_gmlp_w_s, v_gmlp_b_s, v_hgrn_lb, v_hgrn_onorm_g, v_fox_b_f, v_final_norm_g):
    depth = w_in.shape[0]
    seq = x.shape[1]
    assert w_in.shape[2] * N_DEV == N_IN
    xs, tgt = x[0], loss_target[0]

    wi_blk, wo_blk = w_in.astype(BF16), w_out.astype(BF16)
    (wi_all,) = _exchange_call(AllGatherWeights([wi_blk[0]]), "allgather_weights_0")

    ln_g = gmlp_ln_g.reshape(depth, 1, A_WIDTH)
    ln_b = gmlp_ln_b.reshape(depth, 1, A_WIDTH)
    bs_t = jnp.pad(jnp.transpose(gmlp_b_s, (0, 2, 1)), ((0, 0), (0, 0), (0, LANES - A_GROUPS)))
    lb0, lb1 = hgrn_lb[0:1], hgrn_lb[1:2]
    onorm = jnp.tile(hgrn_onorm_g, (1, B_HEADS)).reshape(depth, 1, B_WIDTH)
    bf_row = jnp.pad(fox_b_f, ((0, 0), (0, LANES - C_HEADS))).reshape(depth, 1, LANES)

    core = lax.axis_index("c").astype(jnp.int32).reshape(1)
    chip = (2 * lax.axis_index("x") + lax.axis_index("y")).astype(jnp.int32).reshape(1)

    saved = []
    xc = xs
    for l in range(depth):
        wi_int = assemble_w_in(wi_all[:, None])
        proj, qkv, h = inproj(xc, norm_g[l:l + 1], wi_int, 0)
        ya = gmlp_fwd(proj, ln_g[l], ln_b[l], gmlp_w_s[l], bs_t[l])
        yb, states = hgrn_fwd(proj, lb0, lb1, onorm[l], l)
        ka, va, vt, kt, qt, qa = fox_prep(proj, qkv, bf_row[l])
        ride = ([wo_blk] if l == 0 else []) + ([wi_blk[l + 1]] if l + 1 < depth else [])
        o, lse, *gathered = fox_fwd(qt, ka, vt, AllGatherWeights(ride) if ride else None)
        if l == 0:
            wo_all = gathered.pop(0)
        if gathered:
            (wi_all,) = gathered
        x_in = xc
        if l + 1 < depth:
            xc, yfull = outproj(x_in, ya, yb, o, proj, wo_all, l)
        else:
            dx, yfull, d_final_g, loss_tile = outproj(x_in, ya, yb, o, proj, wo_all, l, (final_norm_g[None], tgt))
        saved.append((x_in, proj, h, states, ka, va, kt, qt, qa, o, lse, yfull, wi_int))

    n_shard = w_in.shape[2]
    g_norm = [None] * depth
    g_ln_g, g_ln_b, g_ws, g_bs, g_on, g_bf = ([None] * depth for _ in range(6))
    g_lb0, g_lb1 = jnp.zeros_like(lb0), jnp.zeros_like(lb1)
    swi = swo = rwi = rwo = None
    for l in reversed(range(depth)):
        x_in, proj, h, states, ka, va, kt, qt, qa, o, lse, yfull, wi_int = saved[l]
        dy, gwo = outproj_bwd(dx, yfull, wo_all, l)
        dproj, g_ln_g[l], g_ln_b[l], g_ws[l], dbs_t = gmlp_bwd(proj, dy, ln_g[l], ln_b[l], gmlp_w_s[l], bs_t[l])
        g_bs[l] = dbs_t[:, :A_GROUPS].T
        if l > 0:
            (qwo,) = _exchange_call(PairExchange([gwo]), f"pair_exchange_w_out_{l}")
        else:
            gws = jnp.stack(g_ws).reshape(-1, LANES)
            qwo, qws = _exchange_call(PairExchange([gwo], [gws]), f"pair_exchange_w_out_{l}")
            sws = small_sum(gws, qws, "pair_sum_w_s")
        swo = pair_sum(gwo, qwo, BF16, gwo.shape[2], "pair_sum_w_out", core, l, depth, swo)
        dproj, d0, d1, don = hgrn_bwd(proj, states, dy, lb0, lb1, onorm[l], l, dproj)
        g_lb0, g_lb1 = g_lb0 + d0, g_lb1 + d1
        g_on[l] = don.reshape(B_HEADS, B_KDIM).sum(0)
        dob, dproj, dot_t = fox_bwd_prep(dy, o, proj, dproj)
        top = l == depth - 1
        ride = ChipExchange([swo] if top else [swi, swo], [(l,)] if top else [(l + 1,), (l,)],
                            [sws] if l == 0 else [], [rwo] if top else [rwi, rwo])
        outs = fox_bwd(ka, va, kt, qt, dot_t, qa, dob, lse, ride)
        dqkv, (dck, dcq), got = outs[:3], outs[3:5], list(outs[5:])
        if not top:
            rwi = got.pop(0)
        rwo = got.pop(0)
        if l == 0:
            (rws,) = got
        dproj, dbf = fox_post(dcq, dck, proj, bf_row[l], dproj)
        g_bf[l] = dbf[0, :C_HEADS]
        gwi, for_sibling = split_w_in_grad(inproj_bwd_w(h, dproj, dqkv), n_shard, core)
        (qwi,) = _exchange_call(PairExchange([], [for_sibling]), f"pair_exchange_w_in_{l}")
        swi = pair_sum(gwi, qwi, BF16, 256, "pair_sum_w_in", core, l, depth, swi)
        ride = ChipExchange([swi], [(l,)], stacked=[rwi]) if l == 0 else None
        outs = inproj_bwd_x(dproj, dqkv, wi_int, x_in, norm_g[l:l + 1], dx, 0, ride)
        dx, g_norm[l] = outs[:2]
        if ride is not None:
            (rwi,) = outs[2:]

    gsm = _pack_small([
        jnp.concatenate(g_norm), jnp.stack(g_ln_g), jnp.stack(g_ln_b), jnp.stack(g_bs),
        jnp.concatenate([g_lb0, g_lb1]), jnp.stack(g_on), jnp.stack(g_bf), d_final_g, loss_tile[0, 0]])
    (qsm,) = _exchange_call(PairExchange([], [gsm]), "pair_exchange_small")
    ssm = small_sum(gsm, qsm, "pair_sum_small")
    (rsm,) = _exchange_call(ChipExchange(gathered=[ssm]), "chip_exchange_small")

    small_w = (norm_g, gmlp_ln_g, gmlp_ln_b, gmlp_b_s, hgrn_lb, hgrn_onorm_g, fox_b_f, final_norm_g)
    small_m = (m_norm_g, m_gmlp_ln_g, m_gmlp_ln_b, m_gmlp_b_s, m_hgrn_lb, m_hgrn_onorm_g, m_fox_b_f, m_final_norm_g)
    small_v = (v_norm_g, v_gmlp_ln_g, v_gmlp_ln_b, v_gmlp_b_s, v_hgrn_lb, v_hgrn_onorm_g, v_fox_b_f, v_final_norm_g)
    res_wi = adam_reduce_columns(rwi, w_in, m_w_in, v_w_in, "adam_w_in", swi, chip)
    res_wo = adam_reduce(rwo, w_out, m_w_out, v_w_out, w_out.shape[1], "adam_w_out", own=swo, chip=chip)
    grads = _unpack_small(sum_parts(rsm, "sum_small"))
    names = [name for name, _ in _SMALL if name != "loss"]
    rows = lambda a: a.reshape(1, -1) if a.ndim == 1 else a
    res_sm = adam_small([rows(grads[k]) for k in names], *([rows(a) for a in wmv] for wmv in (small_w, small_m, small_v)))
    res_sm = [grads] + [{k: a.reshape(grads[k].shape) for k, a in zip(names, r, strict=True)} for r in res_sm]
    as_rows = lambda a: a.reshape(1, -1, LANES)
    res_ws = adam_reduce(rws[:, None], as_rows(gmlp_w_s), as_rows(m_gmlp_w_s), as_rows(v_gmlp_w_s), rws.shape[1], "adam_w_s")
    for s, r in zip(res_sm, res_ws, strict=True):
        s["gmlp_w_s"] = r.reshape(gmlp_w_s.shape)

    def group(i):
        s = res_sm[i]
        return [s["norm_g"], res_wi[i], res_wo[i], s["gmlp_ln_g"], s["gmlp_ln_b"], s["gmlp_w_s"], s["gmlp_b_s"],
                s["hgrn_lb"], s["hgrn_onorm_g"], s["fox_b_f"], s["final_norm_g"]]

    return (res_sm[0]["loss"], dx[None], *group(0), *group(1), *group(2), *group(3))
```

```python
import functools

import jax
import jax.numpy as jnp
import numpy as np
from jax import lax
from jax.experimental import pallas as pl
from jax.experimental.pallas import tpu as pltpu

F32 = jnp.float32
BF16 = jnp.bfloat16

NORM_EPS = 1e-6
F_FLOOR = 1e-30
CHUNK = 128
LANES = 128
VMEM_LIMIT = 56 * 1024 * 1024


def _cparams(*sem):
    return pltpu.CompilerParams(dimension_semantics=sem, vmem_limit_bytes=VMEM_LIMIT)


def _dot(a, b, dims=(((1,), (0,)), ((), ())), precision=None):
    return lax.dot_general(a, b, dims, precision=precision, preferred_element_type=F32)


_NT = (((1,), (1,)), ((), ()))
_TN = (((0,), (0,)), ((), ()))


def _bf16_pieces(x, n):
    out, r = [], x
    for i in range(n):
        out.append(r.astype(BF16))
        if i + 1 < n:
            r = r - out[-1].astype(F32)
    return out


@functools.partial(jax.custom_vjp, nondiff_argnums=(2,))
def _times_exact(x, e, n):
    return functools.reduce(jnp.add, [_dot(p, e) for p in _bf16_pieces(x, n)])


def _times_exact_fwd(x, e, n):
    return _times_exact(x, e, n), e


def _times_exact_bwd(n, e, g):
    dx = functools.reduce(jnp.add, [lax.dot_general(p, e, _NT, preferred_element_type=F32) for p in _bf16_pieces(g, n)])
    return dx, jnp.zeros_like(e)


_times_exact.defvjp(_times_exact_fwd, _times_exact_bwd)


@functools.partial(jax.custom_vjp, nondiff_argnums=(2,))
def _exact_times(e, x, n):
    return functools.reduce(jnp.add, [_dot(e, p) for p in _bf16_pieces(x, n)])


def _exact_times_fwd(e, x, n):
    return _exact_times(e, x, n), e


def _exact_times_bwd(n, e, g):
    dx = functools.reduce(jnp.add, [lax.dot_general(e, p, _TN, preferred_element_type=F32) for p in _bf16_pieces(g, n)])
    return jnp.zeros_like(e), dx


_exact_times.defvjp(_exact_times_fwd, _exact_times_bwd)


def _group_mean_matrix(width, group):
    idx = np.arange(width) // group
    return jnp.asarray((idx[:, None] == idx[None, :]).astype(np.float32) / group, BF16)


def _group_ones_matrix(width, group):
    idx = np.arange(width) // group
    return jnp.asarray((idx[:, None] == idx[None, :]).astype(np.float32), BF16)


A_WIDTH = 256
A_GROUPS = 4
A_GDIM = 64


A_ROWS = 512


def _gmlp_chunk(x3, ln_g, ln_b, w_s, bs_t, mean_m, gind):
    n = x3.shape[0] // CHUNK
    u = jax.nn.gelu(x3[:, :A_WIDTH])
    v = jax.nn.gelu(x3[:, A_WIDTH:2 * A_WIDTH])
    z = x3[:, 2 * A_WIDTH:]
    mu = _times_exact(v, mean_m, 2)
    d = v - mu
    var = _times_exact(d * d, mean_m, 2)
    vn = d * lax.rsqrt(var + NORM_EPS) * ln_g + ln_b
    vnb = vn.astype(BF16)
    wide = jnp.concatenate([vnb[i * CHUNK:(i + 1) * CHUNK] for i in range(n)], axis=1)
    row = lax.broadcasted_iota(jnp.int32, (CHUNK, CHUNK), 0)
    col = lax.broadcasted_iota(jnp.int32, (CHUNK, CHUNK), 1)
    causal = row >= col
    lane_g = lax.shift_right_logical(lax.broadcasted_iota(jnp.int32, (CHUNK, n * A_WIDTH), 1), 6) & (A_GROUPS - 1)
    bias = _times_exact(bs_t, gind, 3)
    mixed = jnp.concatenate([bias] * n, axis=1)
    for g in range(A_GROUPS):
        wc = jnp.where(causal, w_s[g], 0.0).astype(BF16)
        mixed = mixed + jnp.where(lane_g == g, _dot(wc, wide), 0.0)
    mixed = jnp.concatenate([mixed[:, i * A_WIDTH:(i + 1) * A_WIDTH] for i in range(n)], axis=0)
    return u * mixed * jax.nn.silu(z)


def _gmlp_consts():
    gind = np.zeros((LANES, A_WIDTH), np.float32)
    for g in range(A_GROUPS):
        gind[g, g * A_GDIM:(g + 1) * A_GDIM] = 1.0
    return _group_mean_matrix(A_WIDTH, A_GDIM), jnp.asarray(gind, BF16)


def _full(shape):
    return pl.BlockSpec(shape, lambda *_: (0,) * len(shape))


def gmlp_fwd(proj, ln_g, ln_b, w_s, bs_t):
    seq = proj.shape[0]
    rows = min(A_ROWS, seq)
    mean_m, gind = _gmlp_consts()

    def body(x_ref, g_ref, b_ref, w_ref, bs_ref, m_ref, gi_ref, y_ref):
        y = _gmlp_chunk(x_ref[...], g_ref[...], b_ref[...], w_ref[...], bs_ref[...], m_ref[...], gi_ref[...])
        y_ref[...] = y.astype(BF16)

    return pl.pallas_call(
        body,
        name="gmlp_fwd",
        grid=(seq // rows,),
        in_specs=[
            pl.BlockSpec((rows, 3 * A_WIDTH), lambda n: (n, 0)),
            _full((1, A_WIDTH)), _full((1, A_WIDTH)), _full((A_GROUPS, CHUNK, CHUNK)), _full((CHUNK, LANES)),
            _full((A_WIDTH, A_WIDTH)), _full((LANES, A_WIDTH)),
        ],
        out_specs=pl.BlockSpec((rows, A_WIDTH), lambda n: (n, 0)),
        out_shape=jax.ShapeDtypeStruct((seq, A_WIDTH), BF16),
        compiler_params=_cparams("parallel"),
    )(proj, ln_g, ln_b, w_s, bs_t, mean_m, gind)


def gmlp_bwd(proj, dy, ln_g, ln_b, w_s, bs_t):
    seq = proj.shape[0]
    rows = min(A_ROWS, seq)
    mean_m, gind = _gmlp_consts()

    def body(x_ref, dy_ref, g_ref, b_ref, w_ref, bs_ref, m_ref, gi_ref, dx_ref, dg_ref, db_ref, dw_ref, dbs_ref):
        fn = functools.partial(_gmlp_chunk, mean_m=m_ref[...], gind=gi_ref[...])
        _, vjp = jax.vjp(fn, x_ref[...], g_ref[...], b_ref[...], w_ref[...], bs_ref[...])
        dx, dg, db, dw, dbs = vjp(dy_ref[...])
        dx_ref[...] = dx.astype(BF16)

        @pl.when(pl.program_id(0) == 0)
        def _():
            dg_ref[...] = jnp.zeros_like(dg_ref)
            db_ref[...] = jnp.zeros_like(db_ref)
            dw_ref[...] = jnp.zeros_like(dw_ref)
            dbs_ref[...] = jnp.zeros_like(dbs_ref)

        dg_ref[...] += dg
        db_ref[...] += db
        dw_ref[...] += dw
        dbs_ref[...] += dbs

    return pl.pallas_call(
        body,
        name="gmlp_bwd",
        grid=(seq // rows,),
        in_specs=[
            pl.BlockSpec((rows, 3 * A_WIDTH), lambda n: (n, 0)),
            pl.BlockSpec((rows, A_WIDTH), lambda n: (n, 0)),
            _full((1, A_WIDTH)), _full((1, A_WIDTH)), _full((A_GROUPS, CHUNK, CHUNK)), _full((CHUNK, LANES)),
            _full((A_WIDTH, A_WIDTH)), _full((LANES, A_WIDTH)),
        ],
        out_specs=[
            pl.BlockSpec((rows, 3 * A_WIDTH), lambda n: (n, 0)),
            _full((1, A_WIDTH)), _full((1, A_WIDTH)), _full((A_GROUPS, CHUNK, CHUNK)), _full((CHUNK, LANES)),
        ],
        out_shape=[
            jax.ShapeDtypeStruct((seq, D_INT), BF16),
            jax.ShapeDtypeStruct((1, A_WIDTH), F32), jax.ShapeDtypeStruct((1, A_WIDTH), F32),
            jax.ShapeDtypeStruct((A_GROUPS, CHUNK, CHUNK), F32), jax.ShapeDtypeStruct((CHUNK, LANES), F32),
        ],
        compiler_params=_cparams("arbitrary"),
    )(proj, dy, ln_g, ln_b, w_s, bs_t, mean_m, gind)


B_WIDTH = 256
B_HEADS = 4
B_KDIM = 64
B_LEVELS = (64, 32, 16, 8, 4, 2, 1)


def _hgrn_consts():
    t = np.arange(CHUNK)
    u = t[None, :]
    mats = [np.tril(np.ones((CHUNK, CHUNK), np.float32))]
    for m in B_LEVELS:
        p = (t // (2 * m)) * (2 * m) + m - 1
        right = (t % (2 * m)) >= m
        sel = np.where(right[:, None], (u > p[:, None]) & (u <= t[:, None]), (u > t[:, None]) & (u <= p[:, None]))
        mats.append(sel.astype(np.float32))
    return jnp.asarray(np.concatenate(mats, 0), BF16), _group_ones_matrix(B_WIDTH, B_KDIM)


def _hgrn_lower_bound(lb0, lb1, layer):
    mx = jnp.maximum(lb0, lb1)
    e0 = jnp.exp(lb0 - mx)
    e1 = jnp.exp(lb1 - mx)
    p0 = e0 / (e0 + e1)
    p1 = e1 / (e0 + e1)
    cs = p0 if layer == 0 else p0 + p1
    return jnp.clip(cs - p0, 0.0, 1.0 - 1e-6)


def _hgrn_chunk(x4, st, lb0, lb1, onorm, layer, tstack, ones_bd):
    q_raw, fl, v, zg = (x4[:, i * B_WIDTH:(i + 1) * B_WIDTH] for i in range(4))
    lb = _hgrn_lower_bound(lb0, lb1, layer)
    q = jax.nn.silu(q_raw) * (B_KDIM ** -0.5)
    f = lb + (1.0 - lb) * jax.nn.sigmoid(fl)
    logf = jnp.log(jnp.maximum(f, F_FLOOR))
    k = (1.0 - lb) * jax.nn.sigmoid(-fl)
    b = _exact_times(tstack[:CHUNK], logf, 3)
    dall = jnp.concatenate([b, _exact_times(tstack[CHUNK:], logf, 2)], axis=0)
    b_last = jnp.sum(logf, axis=0, keepdims=True)
    vb = v.astype(BF16)

    lane_h = lax.shift_right_logical(lax.broadcasted_iota(jnp.int32, (CHUNK, B_WIDTH), 1), 6)
    row = lax.broadcasted_iota(jnp.int32, (CHUNK, B_WIDTH), 0)
    srow = lax.broadcasted_iota(jnp.int32, (B_HEADS * CHUNK, CHUNK), 0) & (CHUNK - 1)
    scol = lax.broadcasted_iota(jnp.int32, (B_HEADS * CHUNK, CHUNK), 1)

    def heads_on_rows(a):
        return jnp.concatenate([jnp.where(lane_h == h, a, 0.0) for h in range(B_HEADS)], axis=0)

    def heads_from_rows(r):
        out = jnp.where(lane_h == 0, r[:CHUNK], 0.0)
        for h in range(1, B_HEADS):
            out = out + jnp.where(lane_h == h, r[h * CHUNK:(h + 1) * CHUNK], 0.0)
        return out

    o = lax.dot_general((q * jnp.exp(b)).astype(BF16), st.astype(BF16), _NT, preferred_element_type=F32)
    scores = jnp.zeros((B_HEADS * CHUNK, CHUNK), F32)
    for li, m in enumerate(B_LEVELS):
        e = jnp.exp(dall[(li + 1) * CHUNK:(li + 2) * CHUNK])
        right = (row & (2 * m - 1)) >= m
        qt = jnp.where(right, q * e, 0.0)
        kt = jnp.where(right, 0.0, k * e)
        sc = lax.dot_general(heads_on_rows(qt).astype(BF16), kt.astype(BF16), _NT, preferred_element_type=F32)
        sh = int(np.log2(2 * m))
        same = lax.shift_right_logical(srow, sh) == lax.shift_right_logical(scol, sh)
        scores = scores + jnp.where(same, sc, 0.0)
    o = o + heads_from_rows(_dot(scores.astype(BF16), vb))
    o = o + _times_exact(q * k, ones_bd, 2) * v

    kv = lax.dot_general(vb, (k * jnp.exp(b_last - b)).astype(BF16), _TN, preferred_element_type=F32)
    st_new = st * jnp.exp(b_last) + jnp.where(ones_bd > 0.5, kv, 0.0)

    ms = _times_exact(o * o, ones_bd, 2) * (1.0 / B_KDIM)
    y = o * lax.rsqrt(ms + NORM_EPS) * onorm * jax.nn.silu(zg)
    return y, st_new


B_ROWS = 256


def _hgrn_rows(x4, st, lb0, lb1, onorm, layer, tstack, ones_bd):
    ys = []
    for i in range(x4.shape[0] // CHUNK):
        y, st = _hgrn_chunk(x4[i * CHUNK:(i + 1) * CHUNK], st, lb0, lb1, onorm, layer, tstack, ones_bd)
        ys.append(y)
    return jnp.concatenate(ys, axis=0), st


def hgrn_fwd(proj, lb0, lb1, onorm, layer):
    seq = proj.shape[0]
    rows = min(B_ROWS, seq)
    nc = seq // rows
    tstack, ones_bd = _hgrn_consts()

    def body(x_ref, lb0_ref, lb1_ref, on_ref, t_ref, e_ref, y_ref, st_out_ref, st_ref):
        @pl.when(pl.program_id(0) == 0)
        def _():
            st_ref[...] = jnp.zeros_like(st_ref)

        st = st_ref[...]
        st_out_ref[0] = st
        y, st_new = _hgrn_rows(x_ref[...], st, lb0_ref[...], lb1_ref[...], on_ref[...], layer, t_ref[...], e_ref[...])
        y_ref[...] = y.astype(BF16)
        st_ref[...] = st_new

    return pl.pallas_call(
        body,
        name=f"hgrn_fwd_{layer}",
        grid=(nc,),
        in_specs=[
            pl.BlockSpec((rows, 4 * B_WIDTH), lambda n: (n, 1)),
            _full((1, B_WIDTH)), _full((1, B_WIDTH)), _full((1, B_WIDTH)),
            _full(((len(B_LEVELS) + 1) * CHUNK, CHUNK)), _full((B_WIDTH, B_WIDTH)),
        ],
        out_specs=[
            pl.BlockSpec((rows, B_WIDTH), lambda n: (n, 0)),
            pl.BlockSpec((1, B_WIDTH, B_WIDTH), lambda n: (n, 0, 0)),
        ],
        out_shape=[jax.ShapeDtypeStruct((seq, B_WIDTH), BF16), jax.ShapeDtypeStruct((nc, B_WIDTH, B_WIDTH), F32)],
        scratch_shapes=[pltpu.VMEM((B_WIDTH, B_WIDTH), F32)],
        compiler_params=_cparams("arbitrary"),
    )(proj, lb0, lb1, onorm, tstack, ones_bd)


def hgrn_bwd(proj, states, dy, lb0, lb1, onorm, layer, dproj):
    seq = proj.shape[0]
    rows = min(B_ROWS, seq)
    nc = seq // rows
    tstack, ones_bd = _hgrn_consts()

    def body(x_ref, st_in_ref, dy_ref, lb0_ref, lb1_ref, on_ref, t_ref, e_ref, _, dx_ref, d0_ref, d1_ref, don_ref, dst_ref):
        @pl.when(pl.program_id(0) == 0)
        def _():
            dst_ref[...] = jnp.zeros_like(dst_ref)
            d0_ref[...] = jnp.zeros_like(d0_ref)
            d1_ref[...] = jnp.zeros_like(d1_ref)
            don_ref[...] = jnp.zeros_like(don_ref)

        fn = functools.partial(_hgrn_rows, layer=layer, tstack=t_ref[...], ones_bd=e_ref[...])
        _, vjp = jax.vjp(fn, x_ref[...], st_in_ref[0], lb0_ref[...], lb1_ref[...], on_ref[...])
        dx, dst, d0, d1, don = vjp((dy_ref[...], dst_ref[...]))
        dx_ref[...] = dx.astype(BF16)
        dst_ref[...] = dst
        d0_ref[...] += d0
        d1_ref[...] += d1
        don_ref[...] += don

    rev = lambda n: nc - 1 - n
    return pl.pallas_call(
        body,
        name=f"hgrn_bwd_{layer}",
        grid=(nc,),
        in_specs=[
            pl.BlockSpec((rows, 4 * B_WIDTH), lambda n: (rev(n), 1)),
            pl.BlockSpec((1, B_WIDTH, B_WIDTH), lambda n: (rev(n), 0, 0)),
            pl.BlockSpec((rows, B_WIDTH), lambda n: (rev(n), 1)),
            _full((1, B_WIDTH)), _full((1, B_WIDTH)), _full((1, B_WIDTH)),
            _full(((len(B_LEVELS) + 1) * CHUNK, CHUNK)), _full((B_WIDTH, B_WIDTH)), _ANY,
        ],
        out_specs=[
            pl.BlockSpec((rows, 4 * B_WIDTH), lambda n: (rev(n), 1)),
            _full((1, B_WIDTH)), _full((1, B_WIDTH)), _full((1, B_WIDTH)),
        ],
        out_shape=[jax.ShapeDtypeStruct(dproj.shape, BF16)] + [jax.ShapeDtypeStruct((1, B_WIDTH), F32)] * 3,
        input_output_aliases={8: 0},
        scratch_shapes=[pltpu.VMEM((B_WIDTH, B_WIDTH), F32)],
        compiler_params=_cparams("arbitrary"),
    )(proj, states, dy, lb0, lb1, onorm, tstack, ones_bd, dproj)


D_MODEL = 1024
D_INT = 4096


def _rms_stats(xf):
    r = lax.rsqrt(jnp.mean(xf * xf, axis=-1, keepdims=True) + NORM_EPS)
    return r, xf * r


def _rms_bwd(dy, g, r, xh):
    u = dy * g
    return r * (u - xh * jnp.mean(u * xh, axis=-1, keepdims=True))


C_QKV = (2048, 3584)
P_WIDTH = D_INT - (C_QKV[1] - C_QKV[0])
P_Z_BLOCK = C_QKV[0] // 512


def inproj(x, g, w, layer):
    seq = x.shape[0]
    tm = min(seq, 512)

    def body(x_ref, g_ref, w_ref, p_ref, qkv_ref, h_ref):
        _, xh = _rms_stats(x_ref[...])
        h = (xh * g_ref[...]).astype(BF16)
        h_ref[...] = h
        p_ref[:, :C_QKV[0]] = _dot(h, w_ref[0, :, :C_QKV[0]])
        qkv_ref[...] = _dot(h, w_ref[0, :, C_QKV[0]:C_QKV[1]]).astype(BF16)
        p_ref[:, C_QKV[0]:] = _dot(h, w_ref[0, :, C_QKV[1]:])

    rows = lambda n: pl.BlockSpec((tm, n), lambda i: (i, 0))
    return pl.pallas_call(
        body,
        name="inproj",
        grid=(seq // tm,),
        in_specs=[rows(D_MODEL), _full((1, D_MODEL)), pl.BlockSpec((1, D_MODEL, D_INT), lambda i: (layer, 0, 0))],
        out_specs=[rows(P_WIDTH), rows(C_QKV[1] - C_QKV[0]), rows(D_MODEL)],
        out_shape=[jax.ShapeDtypeStruct((seq, P_WIDTH), F32), jax.ShapeDtypeStruct((seq, C_QKV[1] - C_QKV[0]), BF16),
                   jax.ShapeDtypeStruct((seq, D_MODEL), BF16)],
        compiler_params=_cparams("parallel"),
    )(x, g, w)


def outproj(x, ya, yb, o, proj, wo, layer, head=None):
    seq = x.shape[0]
    tm = min(seq, 512)
    blk = wo.shape[2]

    def body(x_ref, ya_ref, yb_ref, o_ref, z_ref, w_ref, *refs):
        yc = (o_ref[...] * jax.nn.silu(z_ref[...])).astype(BF16)
        y = jnp.concatenate([ya_ref[...], yb_ref[...], yc], axis=1)
        w = jnp.concatenate([w_ref[d, 0] for d in range(N_DEV)], axis=0)
        xn = x_ref[...] + _dot(y, w)
        if head is None:
            xn_ref, y_ref = refs
            xn_ref[...] = xn
        else:
            g_ref, t_ref, dx_ref, y_ref, dg_ref, loss_ref = refs

            @pl.when(pl.program_id(0) == 0)
            def _():
                dg_ref[...] = jnp.zeros_like(dg_ref)
                loss_ref[...] = jnp.zeros_like(loss_ref)

            g = g_ref[...]
            r, xh = _rms_stats(xn)
            err = xh * g - t_ref[...]
            sq = jnp.sum(jnp.sum(err * err, axis=1, keepdims=True), axis=0, keepdims=True)
            loss_ref[...] += jnp.broadcast_to(sq * (0.5 / D_MODEL), loss_ref.shape)
            dout = err * (1.0 / D_MODEL)
            dg_ref[...] += jnp.sum(dout * xh, axis=0, keepdims=True)
            dx_ref[...] = _rms_bwd(dout, g, r, xh)
        y_ref[...] = y

    rows = lambda: pl.BlockSpec((tm, D_MODEL), lambda i: (i, 0))
    tail = (() if head is None else (_full((1, D_MODEL)), rows()),
            () if head is None else (_full((1, D_MODEL)), _full((8, LANES))),
            () if head is None else (jax.ShapeDtypeStruct((1, D_MODEL), F32), jax.ShapeDtypeStruct((8, LANES), F32)))
    return pl.pallas_call(
        body,
        name="outproj" if head is None else "outproj_loss",
        grid=(seq // tm,),
        in_specs=[
            rows(),
            pl.BlockSpec((tm, 256), lambda i: (i, 0)),
            pl.BlockSpec((tm, 256), lambda i: (i, 0)),
            pl.BlockSpec((tm, 512), lambda i: (i, 0)),
            pl.BlockSpec((tm, 512), lambda i: (i, P_Z_BLOCK)),
            pl.BlockSpec((N_DEV, 1, blk, D_MODEL), lambda i: (0, layer, 0, 0)),
            *tail[0],
        ],
        out_specs=[rows(), rows(), *tail[1]],
        out_shape=[jax.ShapeDtypeStruct((seq, D_MODEL), F32), jax.ShapeDtypeStruct((seq, D_MODEL), BF16), *tail[2]],
        compiler_params=_cparams("parallel" if head is None else "arbitrary"),
    )(x, ya, yb, o, proj, wo, *(head or ()))


def outproj_bwd(dx, y, wo, layer):
    seq = dx.shape[0]
    ts = min(seq, 512)
    blk = wo.shape[2]

    def body(dx_ref, y_ref, w_ref, dy_ref, dw_ref):
        @pl.when(pl.program_id(0) == 0)
        def _():
            dw_ref[...] = jnp.zeros_like(dw_ref)

        dxb = dx_ref[...].astype(BF16)
        w = jnp.concatenate([w_ref[d, 0] for d in range(N_DEV)], axis=0)
        dy_ref[...] = lax.dot_general(dxb, w, _NT, preferred_element_type=F32)
        dw = lax.dot_general(y_ref[...], dxb, _TN, preferred_element_type=F32)
        for d in range(N_DEV):
            dw_ref[d % 2, d // 2] += dw[d * blk:(d + 1) * blk]

    return pl.pallas_call(
        body,
        name="outproj_bwd",
        grid=(seq // ts,),
        in_specs=[
            pl.BlockSpec((ts, D_MODEL), lambda i: (i, 0)),
            pl.BlockSpec((ts, D_MODEL), lambda i: (i, 0)),
            pl.BlockSpec((N_DEV, 1, blk, D_MODEL), lambda i: (0, layer, 0, 0)),
        ],
        out_specs=[pl.BlockSpec((ts, D_MODEL), lambda i: (i, 0)),
                   pl.BlockSpec((2, N_CHIP, blk, D_MODEL), lambda i: (0, 0, 0, 0))],
        out_shape=[jax.ShapeDtypeStruct((seq, D_MODEL), F32), jax.ShapeDtypeStruct((2, N_CHIP, blk, D_MODEL), F32)],
        compiler_params=_cparams("arbitrary"),
    )(dx, y, wo)


def _dproj_parts(dp_ref, dqkv_refs, rows):
    lo, hi = C_QKV
    step = (hi - lo) // len(dqkv_refs)
    return ([(0, dp_ref.at[rows, 0:lo])] + [(lo + i * step, r.at[rows, :]) for i, r in enumerate(dqkv_refs)]
            + [(hi, dp_ref.at[rows, hi:D_INT])])


def inproj_bwd_x(dproj, dqkv, w, x, g, dx_in, layer, carried=None):
    seq = x.shape[0]
    tm = min(seq, 512)

    def body(dp_ref, dq_ref, dk_ref, dv_ref, w_ref, x_ref, g_ref, dxin_ref, dx_ref, dg_ref):
        @pl.when(pl.program_id(0) == 0)
        def _():
            dg_ref[...] = jnp.zeros_like(dg_ref)

        dh = None
        for at, part in _dproj_parts(dp_ref, (dq_ref, dk_ref, dv_ref), slice(None)):
            term = lax.dot_general(part[...], w_ref[0, :, at:at + part.shape[1]], _NT, preferred_element_type=F32)
            dh = term if dh is None else dh + term
        r, xh = _rms_stats(x_ref[...])
        dg_ref[...] += jnp.sum(dh * xh, axis=0, keepdims=True)
        dx_ref[...] = dxin_ref[...] + _rms_bwd(dh, g_ref[...], r, xh)

    third = lambda: pl.BlockSpec((tm, C_WIDTH), lambda i: (i, 0))
    return _call_carrying(
        carried, body, (dproj, *dqkv, w, x, g, dx_in),
        name="inproj_bwd_x",
        grid=(seq // tm,),
        in_specs=[
            pl.BlockSpec((tm, D_INT), lambda i: (i, 0)), third(), third(), third(),
            pl.BlockSpec((1, D_MODEL, D_INT), lambda i: (layer, 0, 0)),
            pl.BlockSpec((tm, D_MODEL), lambda i: (i, 0)),
            _full((1, D_MODEL)),
            pl.BlockSpec((tm, D_MODEL), lambda i: (i, 0)),
        ],
        out_specs=[pl.BlockSpec((tm, D_MODEL), lambda i: (i, 0)), _full((1, D_MODEL))],
        out_shape=[jax.ShapeDtypeStruct((seq, D_MODEL), F32), jax.ShapeDtypeStruct((1, D_MODEL), F32)],
        scratch_shapes=[], semantics=("arbitrary",),
    )


def inproj_bwd_w(h, dproj, dqkv):
    seq = h.shape[0]
    ts, tn = min(seq, 512), 512

    def body(h_ref, dp_ref, dq_ref, dk_ref, dv_ref, dw_ref):
        @pl.when(pl.program_id(0) == 0)
        def _():
            dw_ref[...] = jnp.zeros_like(dw_ref)

        ht = h_ref[...].T
        for at, part in _dproj_parts(dp_ref, (dq_ref, dk_ref, dv_ref), slice(None)):
            for c in range(0, part.shape[1], tn):
                dw_ref[0, :, at + c:at + c + tn] += _dot(ht, part[:, c:c + tn])

    third = lambda: pl.BlockSpec((ts, C_WIDTH), lambda s: (s, 0))
    return pl.pallas_call(
        body,
        name="inproj_bwd_w",
        grid=(seq // ts,),
        in_specs=[pl.BlockSpec((ts, D_MODEL), lambda s: (s, 0)), pl.BlockSpec((ts, D_INT), lambda s: (s, 0)),
                  third(), third(), third()],
        out_specs=_full((1, D_MODEL, D_INT)),
        out_shape=jax.ShapeDtypeStruct((1, D_MODEL, D_INT), F32),
        compiler_params=_cparams("arbitrary"),
    )(h, dproj, *dqkv)


N_IN = 3848


def _internal_of(col):
    return col if col < 768 else (col + 256 if col < 3840 else 768 + col - 3840)


def _column_runs(n_shard):
    runs = []
    for d in range(N_IN // n_shard):
        mine = []
        for j in range(n_shard):
            ci = _internal_of(d * n_shard + j)
            if mine and mine[-1][0] + mine[-1][1] == ci:
                mine[-1][1] += 1
            else:
                mine.append([ci, 1, j])
        runs.append(mine)
    return runs


def assemble_w_in(wi_all):
    n_dev, depth, _, n_shard = wi_all.shape
    tr = 256
    pieces = [[] for _ in range(D_INT // LANES)]
    for d, mine in enumerate(_column_runs(n_shard)):
        for ci, ln, off in mine:
            while ln > 0:
                blk, at = divmod(ci, LANES)
                take = min(ln, LANES - at)
                pieces[blk].append((at, take, d, off))
                ci, ln, off = ci + take, ln - take, off + take

    def body(x_ref, o_ref):
        for blk, parts in enumerate(pieces):
            vals, at = [], 0
            for start, ln, d, off in sorted(parts):
                if start > at:
                    vals.append(jnp.zeros((tr, start - at), BF16))
                vals.append(x_ref[d, 0, :, off:off + ln])
                at = start + ln
            if at < LANES:
                vals.append(jnp.zeros((tr, LANES - at), BF16))
            o_ref[0, :, blk * LANES:(blk + 1) * LANES] = vals[0] if len(vals) == 1 else jnp.concatenate(vals, axis=1)

    return pl.pallas_call(
        body,
        name="assemble_w_in",
        grid=(depth, D_MODEL // tr),
        in_specs=[pl.BlockSpec((n_dev, 1, tr, n_shard), lambda l, r: (0, l, r, 0))],
        out_specs=pl.BlockSpec((1, tr, D_INT), lambda l, r: (l, r, 0)),
        out_shape=jax.ShapeDtypeStruct((depth, D_MODEL, D_INT), BF16),
        compiler_params=_cparams("parallel", "parallel"),
    )(wi_all)


def split_w_in_grad(dwi, n_shard, core):
    tr = 256
    runs = _column_runs(n_shard)

    def body(core_ref, x_ref, keep_ref, send_ref):
        for d, mine in enumerate(runs):
            @pl.when(core_ref[0] == d % 2)
            def _():
                for ci, ln, off in mine:
                    keep_ref[d // 2, :, off:off + ln] = x_ref[0, :, ci:ci + ln]

            @pl.when(core_ref[0] != d % 2)
            def _():
                for ci, ln, off in mine:
                    send_ref[d // 2, :, off:off + ln] = x_ref[0, :, ci:ci + ln].astype(BF16)

    shards = lambda: pl.BlockSpec((N_CHIP, tr, n_shard), lambda r, s: (0, r, 0))
    grid_spec = pltpu.PrefetchScalarGridSpec(
        num_scalar_prefetch=1, grid=(D_MODEL // tr,),
        in_specs=[pl.BlockSpec((1, tr, D_INT), lambda r, s: (0, r, 0))], out_specs=[shards(), shards()])
    return pl.pallas_call(
        body,
        name="split_w_in_grad",
        grid_spec=grid_spec,
        out_shape=[jax.ShapeDtypeStruct((N_CHIP, D_MODEL, n_shard), F32), jax.ShapeDtypeStruct((N_CHIP, D_MODEL, n_shard), BF16)],
        compiler_params=_cparams("parallel"),
    )(core, dwi)


C_WIDTH = 512
C_HEADS = 8
C_HDIM = 64
C_PAIRS = C_HEADS // 2
C_BQ = 512
C_TAIL = 16
C_KG = 4


def _split3(x):
    hi = x.astype(BF16)
    r = x - hi.astype(F32)
    mid = r.astype(BF16)
    return hi, mid, (r - mid.astype(F32)).astype(BF16)


def _piece_selectors():
    sel = np.zeros((C_HEADS, 3 * LANES, LANES), np.float32)
    for p in range(C_PAIRS):
        for e in range(2):
            for t in range(3):
                sel[2 * p + e, t * LANES + 2 * p + e, 3 * e + t] = -1.0
    return sel


def fox_prep(proj, qkv, bf_row):
    seq = proj.shape[0]
    nblk = seq // CHUNK
    tril = jnp.asarray(np.tril(np.ones((CHUNK, CHUNK), np.float32)), BF16)
    sel = jnp.asarray(_piece_selectors(), BF16)
    rows_t = CHUNK + C_TAIL

    def body(fl_ref, q_ref, k_ref, v_ref, bf_ref, l_ref, sel_ref, ka_ref, va_ref, vt_ref, kt_ref, qt_ref, qa_ref, carry_ref):
        @pl.when(pl.program_id(0) == 0)
        def _():
            carry_ref[...] = jnp.zeros_like(carry_ref)

        lf = jax.nn.log_sigmoid(fl_ref[:, :LANES] + bf_ref[...])
        c = _exact_times(l_ref[...], lf, 3) + carry_ref[...]
        carry_ref[...] += jnp.sum(lf, axis=0, keepdims=True)
        c3 = jnp.concatenate(_split3(c), axis=1)
        lane = lax.broadcasted_iota(jnp.int32, (CHUNK, LANES), 1)
        row = lax.broadcasted_iota(jnp.int32, (CHUNK, LANES), 0)
        r16 = lax.broadcasted_iota(jnp.int32, (C_TAIL, 2 * CHUNK), 0)
        l16 = lax.broadcasted_iota(jnp.int32, (C_TAIL, 2 * CHUNK), 1)
        zero = jnp.zeros((CHUNK, LANES), BF16)
        one = jnp.ones((CHUNK, LANES), BF16)

        def by_keys(x, right_a, right_b):
            xb = x.astype(BF16)
            top = jnp.concatenate([jnp.where(lane < C_HDIM, xb, zero), right_a], axis=1)
            return jnp.concatenate([top, jnp.concatenate([jnp.where(lane < C_HDIM, zero, xb), right_b], axis=1)], axis=0)

        def by_lanes(x, tail):
            xt = x.T.astype(BF16)
            main = jnp.concatenate([jnp.where(row < C_HDIM, xt, zero), jnp.where(row < C_HDIM, zero, xt)], axis=1)
            return jnp.concatenate([main, tail], axis=0)

        for p in range(C_PAIRS):
            cols = slice(p * LANES, (p + 1) * LANES)
            q2, k2, v2 = (r[:, cols].astype(F32) for r in (q_ref, k_ref, v_ref))
            q2 = q2 * (C_HDIM ** -0.5)
            negc = [_dot(c3, sel_ref[2 * p + e]).astype(BF16) for e in range(2)]
            ones3 = [jnp.where((lane >= 3 * e) & (lane < 3 * e + 3), one, zero) for e in range(2)]
            tail = jnp.where(((r16 == 2 * p) & (l16 < CHUNK)) | ((r16 == 2 * p + 1) & (l16 >= CHUNK)), 1.0, 0.0).astype(BF16)
            ka_ref[p] = by_keys(k2, negc[0], negc[1])
            va_ref[p] = by_keys(v2, ones3[0], ones3[1])
            kt_ref[p] = by_lanes(k2, tail)
            vt_ref[p] = by_lanes(v2, tail)
            qt_ref[p] = jnp.concatenate([q2.T.astype(BF16), jnp.where(row < 6, one, zero)], axis=0)
            qa_ref[p] = jnp.concatenate([q2.astype(BF16), jnp.where((lane == 2 * p) | (lane == 2 * p + 1), one, zero)], axis=1)

    wide = lambda j: pl.BlockSpec((CHUNK, C_WIDTH), lambda n: (n, j))
    by_rows = pl.BlockSpec((C_PAIRS, 2 * CHUNK, 2 * CHUNK), lambda n: (0, n, 0))
    by_cols = pl.BlockSpec((C_PAIRS, rows_t, 2 * CHUNK), lambda n: (0, 0, n))
    return pl.pallas_call(
        body,
        name="fox_prep",
        grid=(nblk,),
        in_specs=[pl.BlockSpec((CHUNK, 256), lambda n: (n, 3)), wide(0), wide(1), wide(2), _full((1, LANES)),
                  _full((CHUNK, CHUNK)), _full((C_HEADS, 3 * LANES, LANES))],
        out_specs=[by_rows, by_rows, by_cols, by_cols,
                   pl.BlockSpec((C_PAIRS, 2 * CHUNK, CHUNK), lambda n: (0, 0, n)),
                   pl.BlockSpec((C_PAIRS, CHUNK, 2 * CHUNK), lambda n: (0, n, 0))],
        out_shape=[jax.ShapeDtypeStruct((C_PAIRS, 2 * seq, 2 * CHUNK), BF16)] * 2
        + [jax.ShapeDtypeStruct((C_PAIRS, rows_t, 2 * seq), BF16)] * 2
        + [jax.ShapeDtypeStruct((C_PAIRS, 2 * CHUNK, seq), BF16), jax.ShapeDtypeStruct((C_PAIRS, seq, 2 * CHUNK), BF16)],
        scratch_shapes=[pltpu.VMEM((1, LANES), F32)],
        compiler_params=_cparams("arbitrary"),
    )(proj, qkv, qkv, qkv, bf_row, tril, sel)


def _visible(shape, key0, query0):
    row = lax.broadcasted_iota(jnp.int32, shape, 0)
    key = key0 + lax.shift_left(lax.shift_right_logical(row, 8), 7) + (row & (CHUNK - 1))
    return key <= query0 + lax.broadcasted_iota(jnp.int32, shape, 1)


def _rows_ab(a, b, n):
    return jnp.concatenate([jnp.broadcast_to(a, (C_HDIM, n)), jnp.broadcast_to(b, (C_HDIM, n))], axis=0)


def _call_carrying(ex, body, operands, *, name, grid, in_specs, out_specs, out_shape, scratch_shapes, semantics=None):
    if ex is None:
        semantics = semantics or ("parallel", *["arbitrary"] * (len(grid) - 1))
        return pl.pallas_call(body, name=name, grid=grid, in_specs=in_specs, out_specs=out_specs, out_shape=out_shape,
                              scratch_shapes=scratch_shapes, compiler_params=_cparams(*semantics))(*operands)
    n_in, n_out = len(in_specs), len(out_specs)

    def wrapped(*refs):
        own, parts = _carried_refs(refs, n_in, n_out, ex)
        ids = [pl.program_id(a) for a in range(len(grid))]
        pl.when(functools.reduce(jnp.logical_and, [i == 0 for i in ids]))(lambda: ex.start(*parts))
        if hasattr(ex, "relay"):
            linear = functools.reduce(lambda at, ig: at * ig[1] + ig[0], zip(ids, grid), 0)
            pl.when(linear == int(np.prod(grid)) // 2)(lambda: ex.relay(*parts))
        body(*own)
        pl.when(functools.reduce(jnp.logical_and, [i == g - 1 for i, g in zip(ids, grid)]))(lambda: ex.finish(*parts))

    return pl.pallas_call(
        wrapped, name=name, grid=grid,
        in_specs=list(in_specs) + [_ANY] * len(ex.inputs), out_specs=list(out_specs) + [_ANY] * len(ex.out_shape),
        out_shape=list(out_shape) + list(ex.out_shape), scratch_shapes=list(scratch_shapes) + list(ex.scratch),
        input_output_aliases={n_in + i: n_out + o for i, o in getattr(ex, "aliases", {}).items()},
        compiler_params=_cparams(*["arbitrary"] * len(grid)),
    )(*operands, *ex.inputs)


def fox_fwd(qt, ka, vt, carried=None):
    seq = qt.shape[2]
    nblk = seq // CHUNK
    bq = min(C_BQ, seq)
    grp = bq // CHUNK
    rows_t = CHUNK + C_TAIL

    def body(qt_ref, ka_ref, vt_ref, o_ref, lse_ref, acc_ref, s_ref):
        p, i = pl.program_id(0), pl.program_id(1)
        qtile = qt_ref[0]
        r16 = lax.broadcasted_iota(jnp.int32, (C_TAIL, bq), 0)

        def scores(t):
            at = pl.multiple_of(t * grp * 2 * CHUNK, 2 * CHUNK)
            return _dot(ka_ref[0, pl.ds(at, grp * 2 * CHUNK), :], qtile)

        def rescale(al_a, al_b):
            tail = jnp.where(r16 == 2 * p, al_a, jnp.where(r16 == 2 * p + 1, al_b, 1.0))
            return jnp.concatenate([_rows_ab(al_a, al_b, bq), tail], axis=0)

        def diagonal(m):
            ma, mb = m
            na, nb = ma, mb
            blocks = []
            for g in range(grp):
                s = s_ref[g * 2 * CHUNK:(g + 1) * 2 * CHUNK, g * CHUNK:]
                s = jnp.where(_visible(s.shape, i * bq + g * CHUNK, i * bq + g * CHUNK), s, -jnp.inf)
                blocks.append(s)
                unseen = [jnp.full((1, g * CHUNK), -jnp.inf, F32)] if g else []
                na = jnp.maximum(na, jnp.concatenate(unseen + [jnp.max(s[:CHUNK], axis=0, keepdims=True)], axis=1))
                nb = jnp.maximum(nb, jnp.concatenate(unseen + [jnp.max(s[CHUNK:], axis=0, keepdims=True)], axis=1))
            acc_ref[...] = acc_ref[...] * rescale(jnp.exp(ma - na), jnp.exp(mb - nb))
            for g in range(grp):
                n = bq - g * CHUNK
                n2 = jnp.concatenate([jnp.broadcast_to(na[:, g * CHUNK:], (CHUNK, n)),
                                      jnp.broadcast_to(nb[:, g * CHUNK:], (CHUNK, n))], axis=0)
                at = pl.multiple_of((i * grp + g) * 2 * CHUNK, 2 * CHUNK)
                pt = jnp.exp(blocks[g] - n2).astype(BF16)
                acc_ref[:, g * CHUNK:] += _dot(vt_ref[0, :, pl.ds(at, 2 * CHUNK)], pt)
            return na, nb

        def group(t, m):
            ma, mb = m
            at = pl.multiple_of(t * grp * 2 * CHUNK, 2 * CHUNK)
            s = s_ref[...]
            sa = [s[g * 2 * CHUNK:g * 2 * CHUNK + CHUNK] for g in range(grp)]
            sb = [s[g * 2 * CHUNK + CHUNK:(g + 1) * 2 * CHUNK] for g in range(grp)]
            na, nb = ma, mb
            for g in range(grp):
                na = jnp.maximum(na, jnp.max(sa[g], axis=0, keepdims=True))
                nb = jnp.maximum(nb, jnp.max(sb[g], axis=0, keepdims=True))
            al_a, al_b = jnp.exp(ma - na), jnp.exp(mb - nb)
            pt = jnp.concatenate([jnp.exp(x - n) for g in range(grp) for x, n in ((sa[g], na), (sb[g], nb))], axis=0)
            pv = _dot(vt_ref[0, :, pl.ds(at, grp * 2 * CHUNK)], pt.astype(BF16))
            acc_ref[...] = acc_ref[...] * rescale(al_a, al_b) + pv
            return na, nb

        def step(t, m):
            s_next = scores(t + 1)
            m = group(t, m)
            s_ref[...] = s_next
            return m

        acc_ref[...] = jnp.zeros_like(acc_ref)
        s_ref[...] = scores(0)
        m = (jnp.full((1, bq), -jnp.inf, F32), jnp.full((1, bq), -jnp.inf, F32))
        m = lax.fori_loop(0, i, step, m)
        ma, mb = diagonal(m)
        tailv = acc_ref[CHUNK:rows_t, :]
        la = jnp.sum(jnp.where(r16 == 2 * p, tailv, 0.0), axis=0, keepdims=True)
        lb = jnp.sum(jnp.where(r16 == 2 * p + 1, tailv, 0.0), axis=0, keepdims=True)
        o_ref[...] = (acc_ref[0:CHUNK, :] * _rows_ab(1.0 / la, 1.0 / lb, bq)).T
        lse_ref[0, 0:1, :] = ma + jnp.log(la)
        lse_ref[0, 1:2, :] = mb + jnp.log(lb)

    return _call_carrying(
        carried, body, (qt, ka, vt),
        name="fox_fwd",
        grid=(C_PAIRS, seq // bq),
        in_specs=[
            pl.BlockSpec((1, 2 * CHUNK, bq), lambda p, i: (p, 0, i)),
            pl.BlockSpec((1, 2 * seq, 2 * CHUNK), lambda p, i: (p, 0, 0)),
            pl.BlockSpec((1, rows_t, 2 * seq), lambda p, i: (p, 0, 0)),
        ],
        out_specs=[pl.BlockSpec((bq, LANES), lambda p, i: (i, p)), pl.BlockSpec((1, 2, bq), lambda p, i: (p, 0, i))],
        out_shape=[jax.ShapeDtypeStruct((seq, C_WIDTH), F32), jax.ShapeDtypeStruct((C_PAIRS, 2, seq), F32)],
        scratch_shapes=[pltpu.VMEM((rows_t, bq), F32), pltpu.VMEM((grp * 2 * CHUNK, bq), F32)],
    )


def fox_bwd_prep(dy, o, proj, dproj):
    seq = o.shape[0]
    rows = min(seq, 512)
    ind = np.zeros((C_WIDTH, LANES), np.float32)
    for h in range(C_HEADS):
        ind[h * C_HDIM:(h + 1) * C_HDIM, h] = 1.0
    ind = jnp.asarray(ind, BF16)
    sel = _piece_selectors()
    sel = jnp.asarray(np.stack([sel[2 * p].T + sel[2 * p + 1].T for p in range(C_PAIRS)]), BF16)

    def body(dy_ref, o_ref, z_ref, ind_ref, sel_ref, _, do_ref, dz_ref, dot_ref):
        dy_c, o_v, z = dy_ref[...], o_ref[...], z_ref[...]
        sg = jax.nn.sigmoid(z)
        do = dy_c * (z * sg)
        do_ref[...] = do.astype(BF16)
        dz_ref[...] = (dy_c * o_v * (sg * (1.0 + z * (1.0 - sg)))).astype(BF16)
        prod = do * o_v
        hi = prod.astype(BF16)
        lo = (prod - hi.astype(F32)).astype(BF16)
        delta = _dot(hi, ind_ref[...]) + _dot(lo, ind_ref[...])
        d3 = jnp.concatenate(_split3(delta.T), axis=0)
        for p in range(C_PAIRS):
            tail = _dot(sel_ref[p], d3).astype(BF16)
            dot_ref[p] = jnp.concatenate([do[:, p * LANES:(p + 1) * LANES].T.astype(BF16), tail], axis=0)

    return pl.pallas_call(
        body,
        name="fox_bwd_prep",
        grid=(seq // rows,),
        in_specs=[
            pl.BlockSpec((rows, C_WIDTH), lambda i: (i, 1)),
            pl.BlockSpec((rows, C_WIDTH), lambda i: (i, 0)),
            pl.BlockSpec((rows, C_WIDTH), lambda i: (i, P_Z_BLOCK)),
            _full((C_WIDTH, LANES)), _full((C_PAIRS, LANES, 3 * LANES)), _ANY,
        ],
        out_specs=[
            pl.BlockSpec((rows, C_WIDTH), lambda i: (i, 0)),
            pl.BlockSpec((rows, C_WIDTH), lambda i: (i, 7)),
            pl.BlockSpec((C_PAIRS, 2 * CHUNK, rows), lambda i: (0, 0, i)),
        ],
        out_shape=[jax.ShapeDtypeStruct((seq, C_WIDTH), BF16), jax.ShapeDtypeStruct(dproj.shape, BF16),
                   jax.ShapeDtypeStruct((C_PAIRS, 2 * CHUNK, seq), BF16)],
        input_output_aliases={5: 1},
        compiler_params=_cparams("parallel"),
    )(dy, o, proj, ind, sel, dproj)


def fox_bwd(ka, va, kt, qt, dot_t, qa, dob, lse, carried=None):
    seq = qt.shape[2]
    nblk = seq // CHUNK
    bq = min(C_BQ, seq)
    nq = seq // bq
    kg = min(C_KG, nblk)
    ng = nblk // kg
    rows_t = CHUNK + C_TAIL

    def body(ka_ref, va_ref, kt_ref, qt_ref, dot_ref, qa_ref, do_ref, lse_ref,
             dq_ref, dk_ref, dv_ref, dck_ref, dcq_ref, dqt_acc, dv_acc, dka_acc):
        p, jg = pl.program_id(0), pl.program_id(1)

        @pl.when(jg == 0)
        def _():
            dqt_acc[...] = jnp.zeros_like(dqt_acc)

        dv_acc[...] = jnp.zeros_like(dv_acc)
        dka_acc[...] = jnp.zeros_like(dka_acc)

        def step(i, carry):
            cols = pl.ds(pl.multiple_of(i * bq, bq), bq)
            qtile, dotile = qt_ref[0, :, cols], dot_ref[0, :, cols]
            do, qa_i = do_ref[cols, :], qa_ref[0, cols, :]
            lse2 = jnp.concatenate([jnp.broadcast_to(lse_ref[0, 0:1, cols], (CHUNK, bq)),
                                    jnp.broadcast_to(lse_ref[0, 1:2, cols], (CHUNK, bq))] * kg, axis=0)
            pt = jnp.exp(_dot(ka_ref[0], qtile) - lse2)
            ds = pt * _dot(va_ref[0], dotile)
            ptb, dsb = pt.astype(BF16), ds.astype(BF16)
            dv_acc[...] += _dot(ptb, do)
            dka_acc[...] += _dot(dsb, qa_i)
            dqt_acc[:, cols] += _dot(kt_ref[0], dsb)
            return carry

        def diagonal(i):
            cols = [pl.ds(pl.multiple_of(i * bq + kb * CHUNK, CHUNK), bq - kb * CHUNK) for kb in range(kg)]
            rows = [slice(kb * 2 * CHUNK, (kb + 1) * 2 * CHUNK) for kb in range(kg)]
            s = [_dot(ka_ref[0, rows[kb], :], qt_ref[0, :, cols[kb]]) for kb in range(kg)]
            dp = [_dot(va_ref[0, rows[kb], :], dot_ref[0, :, cols[kb]]) for kb in range(kg)]
            ptb, dsb = [], []
            for kb in range(kg):
                n = bq - kb * CHUNK
                lse2 = jnp.concatenate([jnp.broadcast_to(lse_ref[0, 0:1, cols[kb]], (CHUNK, n)),
                                        jnp.broadcast_to(lse_ref[0, 1:2, cols[kb]], (CHUNK, n))], axis=0)
                pt = jnp.exp(s[kb] - lse2)
                pt = jnp.where(_visible(pt.shape, (jg * kg + kb) * CHUNK, i * bq + kb * CHUNK), pt, 0.0)
                ptb.append(pt.astype(BF16))
                dsb.append((pt * dp[kb]).astype(BF16))
            for kb in range(kg):
                dv_acc[rows[kb], :] += _dot(ptb[kb], do_ref[cols[kb], :])
                dka_acc[rows[kb], :] += _dot(dsb[kb], qa_ref[0, cols[kb], :])
                dqt_acc[:, cols[kb]] += _dot(kt_ref[0, :, rows[kb]], dsb[kb])

        assert kg * CHUNK == bq
        diagonal(jg)
        lax.fori_loop(jg + 1, nq, step, 0)
        lane = lax.broadcasted_iota(jnp.int32, (CHUNK, LANES), 1)
        for kb in range(kg):
            rows = slice(kb * CHUNK, (kb + 1) * CHUNK)
            ra = slice(kb * 2 * CHUNK, kb * 2 * CHUNK + CHUNK)
            rb = slice(kb * 2 * CHUNK + CHUNK, (kb + 1) * 2 * CHUNK)
            dk_ref[rows, :] = jnp.where(lane < C_HDIM, dka_acc[ra, 0:LANES], dka_acc[rb, 0:LANES]).astype(BF16)
            dv_ref[rows, :] = jnp.where(lane < C_HDIM, dv_acc[ra, :], dv_acc[rb, :]).astype(BF16)
            dck_ref[0, rows, :] = (jnp.where(lane == 2 * p, dka_acc[ra, LANES:], 0.0)
                                   + jnp.where(lane == 2 * p + 1, dka_acc[rb, LANES:], 0.0))

        @pl.when(jg == ng - 1)
        def _():
            for c in range(nq):
                dq_ref[c * bq:(c + 1) * bq, :] = (dqt_acc[0:CHUNK, c * bq:(c + 1) * bq].T * (C_HDIM ** -0.5)).astype(BF16)
            dcq_ref[0] = dqt_acc[CHUNK:rows_t, :]

    per_pair = lambda r, c: pl.BlockSpec((1, r, c), lambda p, j: (p, 0, 0))
    by_rows = pl.BlockSpec((1, kg * 2 * CHUNK, 2 * CHUNK), lambda p, j: (p, j, 0))
    by_cols = pl.BlockSpec((1, rows_t, kg * 2 * CHUNK), lambda p, j: (p, 0, j))
    return _call_carrying(
        carried, body, (ka, va, kt, qt, dot_t, qa, dob, lse),
        name="fox_bwd",
        grid=(C_PAIRS, ng),
        in_specs=[by_rows, by_rows, by_cols, per_pair(2 * CHUNK, seq), per_pair(2 * CHUNK, seq),
                  per_pair(seq, 2 * CHUNK), pl.BlockSpec((seq, LANES), lambda p, j: (0, p)), per_pair(2, seq)],
        out_specs=[pl.BlockSpec((seq, LANES), lambda p, j: (0, p)),
                   pl.BlockSpec((kg * CHUNK, LANES), lambda p, j: (j, p)),
                   pl.BlockSpec((kg * CHUNK, LANES), lambda p, j: (j, p)),
                   pl.BlockSpec((1, kg * CHUNK, LANES), lambda p, j: (p, j, 0)),
                   per_pair(C_TAIL, seq)],
        out_shape=[jax.ShapeDtypeStruct((seq, C_WIDTH), BF16)] * 3
        + [jax.ShapeDtypeStruct((C_PAIRS, seq, LANES), F32), jax.ShapeDtypeStruct((C_PAIRS, C_TAIL, seq), F32)],
        scratch_shapes=[pltpu.VMEM((rows_t, seq), F32), pltpu.VMEM((kg * 2 * CHUNK, LANES), F32),
                        pltpu.VMEM((kg * 2 * CHUNK, 2 * CHUNK), F32)],
    )


def fox_post(dcq, dck, proj, bf_row, dproj):
    seq = proj.shape[0]
    nc = seq // CHUNK
    triu = jnp.asarray(np.triu(np.ones((CHUNK, CHUNK), np.float32)), BF16)

    def body(dq_ref, dk_ref, fl_ref, bf_ref, u_ref, _, dfl_ref, dbf_ref, carry_ref):
        @pl.when(pl.program_id(0) == 0)
        def _():
            carry_ref[...] = jnp.zeros_like(carry_ref)
            dbf_ref[...] = jnp.zeros_like(dbf_ref)

        rows = (dq_ref[0] + dq_ref[1]) + (dq_ref[2] + dq_ref[3])
        dc = jnp.concatenate([rows, jnp.zeros((CHUNK - C_TAIL, CHUNK), F32)], axis=0).T
        dc = dc - ((dk_ref[0] + dk_ref[1]) + (dk_ref[2] + dk_ref[3]))
        g = _exact_times(u_ref[...], dc, 3) + carry_ref[...]
        carry_ref[...] += jnp.sum(dc, axis=0, keepdims=True)
        dfl = g * jax.nn.sigmoid(-(fl_ref[:, :LANES] + bf_ref[...]))
        dbf_ref[...] += jnp.sum(dfl, axis=0, keepdims=True)
        dfl_ref[...] = jnp.concatenate([dfl, jnp.zeros_like(dfl)], axis=1).astype(BF16)

    rev = lambda n: nc - 1 - n
    return pl.pallas_call(
        body,
        name="fox_post",
        grid=(nc,),
        in_specs=[
            pl.BlockSpec((C_PAIRS, C_TAIL, CHUNK), lambda n: (0, 0, rev(n))),
            pl.BlockSpec((C_PAIRS, CHUNK, LANES), lambda n: (0, rev(n), 0)),
            pl.BlockSpec((CHUNK, 256), lambda n: (rev(n), 3)),
            _full((1, LANES)), _full((CHUNK, CHUNK)), _ANY,
        ],
        out_specs=[pl.BlockSpec((CHUNK, 256), lambda n: (rev(n), 3)), _full((1, LANES))],
        out_shape=[jax.ShapeDtypeStruct(dproj.shape, BF16), jax.ShapeDtypeStruct((1, LANES), F32)],
        input_output_aliases={5: 0},
        scratch_shapes=[pltpu.VMEM((1, LANES), F32)],
        compiler_params=_cparams("arbitrary"),
    )(dcq, dck, proj, bf_row, triu, dproj)


N_DEV = 8
MESH = pl.DeviceIdType.MESH
_ANY = pl.BlockSpec(memory_space=pl.ANY)


def _mesh_pos():
    return lax.axis_index("x"), lax.axis_index("y"), lax.axis_index("c")


def _dev_index(px, py, pc):
    return 4 * px + 2 * py + pc


def _row_pieces(ref, rows):
    return [ref.at[idx + (pl.ds(r, rows),)] for idx in np.ndindex(*ref.shape[:-2]) for r in range(0, ref.shape[-2], rows)]


class _Transfer:
    def __init__(self, src, dst, rows, send_sem, recv_sem, to):
        self.src, self.dst, self.rows, self.sems, self.to = src, dst, rows, (send_sem, recv_sem), to

    def _copy(self, src, dst):
        return pltpu.make_async_remote_copy(src_ref=src, dst_ref=dst, send_sem=self.sems[0], recv_sem=self.sems[1],
                                            device_id=self.to, device_id_type=MESH)

    def start(self):
        for s, d in zip(_row_pieces(self.src, self.rows), _row_pieces(self.dst, self.rows), strict=True):
            self._copy(s, d).start()

    def wait_send(self):
        self._copy(self.src, self.dst).wait_send()

    def wait_recv(self):
        self._copy(self.src, self.dst).wait_recv()


def _exchange_call(ex, name):
    n_in, n_out = len(ex.inputs), len(ex.out_shape)

    def body(*refs):
        parts = refs[:n_in], refs[n_in:n_in + n_out], refs[n_in + n_out:]
        ex.start(*parts)
        getattr(ex, "relay", lambda *_: None)(*parts)
        ex.finish(*parts)

    return pl.pallas_call(body, name=name, in_specs=[_ANY] * n_in, out_specs=[_ANY] * n_out, out_shape=ex.out_shape,
                          scratch_shapes=ex.scratch, input_output_aliases=getattr(ex, "aliases", {}))(*ex.inputs)


def _carried_refs(refs, n_in, n_out, ex):
    k_in, k_out, k_sem = (len(ex.inputs), len(ex.out_shape), len(ex.scratch)) if ex else (0, 0, 0)
    a, b, c = n_in + k_in, n_in + k_in + n_out, n_in + k_in + n_out + k_out
    own = refs[:n_in] + refs[a:b] + refs[c:len(refs) - k_sem]
    return own, (refs[n_in:a], refs[b:c], refs[len(refs) - k_sem:])


class AllGatherWeights:
    def __init__(self, blocks):
        n = len(blocks)
        self.inputs = tuple(blocks)
        self.out_shape = [jax.ShapeDtypeStruct((N_DEV,) + b.shape, b.dtype) for b in blocks]
        self.scratch = ([pltpu.SemaphoreType.DMA((n, 7)), pltpu.SemaphoreType.DMA((n, 7)), pltpu.SemaphoreType.DMA((n, 2))]
                        + [pltpu.VMEM(b.shape, b.dtype) for b in blocks])

    def _plan(self, ins, outs, scratch):
        send_sems, recv_sems, local_sems, *staged = scratch
        x, y, c = _mesh_pos()
        me, sibling = (x, y, c), (x, y, 1 - c)
        chips = [(1 - x, y), (x, 1 - y), (1 - x, 1 - y)]
        every = range(len(ins))

        def copy(a, k, block, to, own=False):
            slot = outs[a].at[_dev_index(*block)]
            return _Transfer(ins[a] if own else slot, slot, ins[a].shape[-2], send_sems.at[a, k], recv_sems.at[a, k], to)

        mine = [(pltpu.make_async_copy(ins[a], staged[a], local_sems.at[a, 0]),
                 pltpu.make_async_copy(staged[a], outs[a].at[_dev_index(*me)], local_sems.at[a, 1])) for a in every]
        first = [copy(a, 1 + j, me, (*chip, c), own=True) for j, chip in enumerate(chips) for a in every]
        first += [copy(a, 0, me, sibling, own=True) for a in every]
        passed = [[copy(a, 4 + j, (*chip, c), sibling) for a in every] for j, chip in enumerate(chips)]
        return me, sibling, chips, c, every, copy, mine, first, passed

    def start(self, ins, outs, scratch):
        *_, mine, first, _ = self._plan(ins, outs, scratch)
        for to_vmem, _ in mine:
            to_vmem.start()
        for cp in first:
            cp.start()

    def relay(self, ins, outs, scratch):
        me, sibling, chips, c, every, copy, mine, first, passed = self._plan(ins, outs, scratch)
        for to_vmem, to_slot in mine:
            to_vmem.wait()
            to_slot.start()
        for j, chip in enumerate(chips):
            for a in every:
                copy(a, 1 + j, (*chip, c), me).wait_recv()
            for cp in passed[j]:
                cp.start()

    def finish(self, ins, outs, scratch):
        me, sibling, chips, c, every, copy, mine, first, passed = self._plan(ins, outs, scratch)
        for a in every:
            copy(a, 0, sibling, me).wait_recv()
        for j, chip in enumerate(chips):
            for a in every:
                copy(a, 4 + j, (*chip, 1 - c), me).wait_recv()
        for cp in first + [cp for group in passed for cp in group]:
            cp.wait_send()
        for _, to_slot in mine:
            to_slot.wait()


N_CHIP = 4


class PairExchange:
    def __init__(self, by_core, whole=()):
        self.inputs = tuple(by_core) + tuple(whole)
        self.n_by_core = len(by_core)
        self.out_shape = ([jax.ShapeDtypeStruct(a.shape[1:], a.dtype) for a in by_core]
                          + [jax.ShapeDtypeStruct(a.shape, a.dtype) for a in whole])
        n = len(self.inputs)
        self.scratch = [pltpu.SemaphoreType.DMA((n,)), pltpu.SemaphoreType.DMA((n,))]

    def _copies(self, ins, outs, sems):
        x, y, c = _mesh_pos()
        srcs = [r.at[1 - c] if a < self.n_by_core else r for a, r in enumerate(ins)]
        return [_Transfer(srcs[a], outs[a], outs[a].shape[-2], sems[0].at[a], sems[1].at[a], (x, y, 1 - c))
                for a in range(len(ins))]

    def start(self, ins, outs, sems):
        for cp in self._copies(ins, outs, sems):
            cp.start()

    def finish(self, ins, outs, sems):
        copies = self._copies(ins, outs, sems)
        for cp in copies:
            cp.wait_recv()
        for cp in copies:
            cp.wait_send()


def pair_sum(own, other, dtype, rows, name, core, layer, depth, stacked=None):
    n, n_r, n_c = other.shape
    by_core = own.ndim == 4
    own = own if by_core else own[None]

    def body(core_ref, a_ref, b_ref, *refs):
        refs[-1][0, 0] = (a_ref[0, 0] + b_ref[0].astype(F32)).astype(dtype)

    carried = () if stacked is None else (stacked,)
    grid_spec = pltpu.PrefetchScalarGridSpec(
        num_scalar_prefetch=1,
        grid=(n, n_r // rows),
        in_specs=[pl.BlockSpec((1, 1, rows, n_c), lambda i, r, s: (s[0] if by_core else 0, i, r, 0)),
                  pl.BlockSpec((1, rows, n_c), lambda i, r, s: (i, r, 0))] + [_ANY] * len(carried),
        out_specs=pl.BlockSpec((1, 1, rows, n_c), lambda i, r, s: (i, layer, r, 0)),
    )
    return pl.pallas_call(
        body,
        name=name,
        grid_spec=grid_spec,
        out_shape=jax.ShapeDtypeStruct((n, depth, n_r, n_c), dtype),
        input_output_aliases={3: 0} if carried else {},
        compiler_params=_cparams("parallel", "parallel"),
    )(core, own, other, *carried)


def small_sum(a, b, name):
    def body(a_ref, b_ref, o_ref):
        o_ref[...] = a_ref[...] + b_ref[...]

    return pl.pallas_call(body, name=name, out_shape=jax.ShapeDtypeStruct(a.shape, a.dtype))(a, b)


class ChipExchange:
    def __init__(self, by_chip=(), layers=(), gathered=(), stacked=()):
        stacked = tuple(stacked) or (None,) * len(by_chip)
        kept = [s for s in stacked if s is not None]
        self.inputs = tuple(by_chip) + tuple(gathered) + tuple(kept)
        self.n_by_chip, self.n_gathered = len(by_chip), len(gathered)
        self.items = [(a, l) for a in range(len(by_chip)) for l in layers[a]] + [(self.n_by_chip + g, None) for g in range(len(gathered))]
        self.out_shape = ([jax.ShapeDtypeStruct((N_CHIP - 1,) + a.shape[1:], a.dtype) for a in by_chip]
                          + [jax.ShapeDtypeStruct((N_CHIP,) + a.shape, a.dtype) for a in gathered])
        at = iter(range(self.n_by_chip + self.n_gathered, len(self.inputs)))
        self.aliases = {next(at): a for a, s in enumerate(stacked) if s is not None}
        n = len(self.items)
        self.scratch = [pltpu.SemaphoreType.DMA((n, 3)), pltpu.SemaphoreType.DMA((n, 3)),
                        pltpu.SemaphoreType.DMA((max(self.n_gathered, 1),))]

    def _plan(self, ins, outs, sems):
        x, y, c = _mesh_pos()
        chip = 2 * x + y
        n = len(self.items)

        def copy(i, k, sending):
            a, layer = self.items[i]
            px, py = x ^ ((k >> 1) & 1), y ^ (k & 1)
            if layer is not None:
                src, dst = ins[a].at[2 * px + py, layer], outs[a].at[k - 1, layer]
            else:
                src, dst = ins[a], outs[a].at[chip if sending else 2 * px + py]
            return _Transfer(src, dst, dst.shape[-2], sems[0].at[i, k - 1], sems[1].at[i, k - 1], (px, py, c))

        local = [pltpu.make_async_copy(ins[a], outs[a].at[chip], sems[2].at[a - self.n_by_chip])
                 for a in range(self.n_by_chip, self.n_by_chip + self.n_gathered)]
        return n, copy, local

    def start(self, ins, outs, sems):
        n, copy, local = self._plan(ins, outs, sems)
        for cp in local:
            cp.start()
        for k in range(1, N_CHIP):
            for a in range(n):
                copy(a, k, True).start()

    def finish(self, ins, outs, sems):
        n, copy, local = self._plan(ins, outs, sems)
        for k in range(1, N_CHIP):
            for a in range(n):
                copy(a, k, False).wait_recv()
        for k in range(1, N_CHIP):
            for a in range(n):
                copy(a, k, True).wait_send()
        for cp in local:
            cp.wait()


ADAM_LR = 0.001
ADAM_B1 = 0.9
ADAM_B2 = 0.999
ADAM_EPS = 1e-08
ADAM_WD = 0.01
ADAM_STEP = 10


def adam_reduce(parts, w, m, v, rows, name, own=None, chip=None):
    n_l, n_r, n_c = w.shape
    n_parts = parts.shape[0]

    def body(*refs):
        p_ref, w_ref, m_ref, v_ref, g_ref, d_ref, m2_ref, v2_ref = refs[-8:]
        g = p_ref[0, 0].astype(F32)
        if own is not None:
            g = refs[-9][...].reshape(rows, n_c).astype(F32) + g
        for d in range(1, n_parts):
            g = g + p_ref[d, 0].astype(F32)
        m2 = ADAM_B1 * m_ref[0] + (1.0 - ADAM_B1) * g
        v2 = ADAM_B2 * v_ref[0] + (1.0 - ADAM_B2) * (g * g)
        m_hat = m2 / (1.0 - ADAM_B1 ** ADAM_STEP)
        v_hat = v2 / (1.0 - ADAM_B2 ** ADAM_STEP)
        g_ref[0] = g
        d_ref[0] = -ADAM_LR * (m_hat / (jnp.sqrt(v_hat) + ADAM_EPS) + ADAM_WD * w_ref[0])
        m2_ref[0] = m2
        v2_ref[0] = v2

    blk = lambda: pl.BlockSpec((1, rows, n_c), lambda l, r, *_: (l, r, 0))
    in_specs = [pl.BlockSpec((n_parts, 1, rows, n_c), lambda l, r, *_: (0, l, r, 0)), blk(), blk(), blk()]
    args = (parts, w, m, v)
    if own is not None:
        in_specs = [pl.BlockSpec((1, 1, rows, n_c), lambda l, r, s: (s[0], l, r, 0))] + in_specs
        args = (chip, own) + args
    grid_spec = pltpu.PrefetchScalarGridSpec(
        num_scalar_prefetch=0 if own is None else 1, grid=(n_l, n_r // rows), in_specs=in_specs,
        out_specs=[blk(), blk(), blk(), blk()])
    return pl.pallas_call(
        body,
        name=name,
        grid_spec=grid_spec,
        out_shape=[jax.ShapeDtypeStruct(w.shape, F32)] * 4,
        compiler_params=_cparams("parallel", "parallel"),
    )(*args)


def adam_reduce_columns(parts, w, m, v, name, own, chip):
    n_l, n_r, n_c = w.shape
    n_parts = parts.shape[0]
    view = lambda a: jnp.transpose(a, (2, 0, 1))

    def body(_, own_ref, p_ref, w_ref, m_ref, v_ref, g_ref, d_ref, m2_ref, v2_ref):
        for l in range(n_l):
            g = own_ref[0, l].astype(F32) + p_ref[0, l].astype(F32)
            for d in range(1, n_parts):
                g = g + p_ref[d, l].astype(F32)
            g = g.T
            w_l, m_l, v_l = w_ref[:, l, :], m_ref[:, l, :], v_ref[:, l, :]
            m2 = ADAM_B1 * m_l + (1.0 - ADAM_B1) * g
            v2 = ADAM_B2 * v_l + (1.0 - ADAM_B2) * (g * g)
            m_hat = m2 / (1.0 - ADAM_B1 ** ADAM_STEP)
            v_hat = v2 / (1.0 - ADAM_B2 ** ADAM_STEP)
            g_ref[:, l, :] = g
            d_ref[:, l, :] = -ADAM_LR * (m_hat / (jnp.sqrt(v_hat) + ADAM_EPS) + ADAM_WD * w_l)
            m2_ref[:, l, :] = m2
            v2_ref[:, l, :] = v2

    blk = lambda: pl.BlockSpec((LANES, n_l, n_r), lambda c, s: (c, 0, 0))
    grid_spec = pltpu.PrefetchScalarGridSpec(
        num_scalar_prefetch=1, grid=(pl.cdiv(n_c, LANES),),
        in_specs=[pl.BlockSpec((1, n_l, n_r, LANES), lambda c, s: (s[0], 0, 0, c)),
                  pl.BlockSpec((n_parts, n_l, n_r, LANES), lambda c, s: (0, 0, 0, c)), blk(), blk(), blk()],
        out_specs=[blk(), blk(), blk(), blk()])
    outs = pl.pallas_call(
        body,
        name=name,
        grid_spec=grid_spec,
        out_shape=[jax.ShapeDtypeStruct((n_c, n_l, n_r), F32)] * 4,
        compiler_params=_cparams("parallel"),
    )(chip, own, parts, view(w), view(m), view(v))
    return [jnp.transpose(o, (1, 2, 0)) for o in outs]


_SMALL = (("norm_g", (2, 1024)), ("gmlp_ln_g", (2, 4, 64)), ("gmlp_ln_b", (2, 4, 64)),
          ("gmlp_b_s", (2, 4, 128)), ("hgrn_lb", (2, 256)), ("hgrn_onorm_g", (2, 64)), ("fox_b_f", (2, 8)),
          ("final_norm_g", (1024,)), ("loss", ()))


def _padded(n):
    return -(-n // LANES) * LANES


_SMALL_ROWS = -(-sum(_padded(int(np.prod(s))) for _, s in _SMALL) // LANES // 8) * 8


def _pack_small(vals):
    flat = []
    for (name, shape), a in zip(_SMALL, vals, strict=True):
        n = int(np.prod(shape))
        flat.append(jnp.pad(a.reshape(n).astype(F32), (0, _padded(n) - n)))
    flat = jnp.concatenate(flat)
    return jnp.pad(flat, (0, _SMALL_ROWS * LANES - flat.shape[0])).reshape(_SMALL_ROWS, LANES)


def _unpack_small(slab):
    flat, out, at = slab.reshape(-1), {}, 0
    for name, shape in _SMALL:
        n = int(np.prod(shape))
        out[name] = flat[at:at + n].reshape(shape)
        at += _padded(n)
    return out


def sum_parts(parts, name):
    def body(p_ref, o_ref):
        g = p_ref[0]
        for d in range(1, parts.shape[0]):
            g = g + p_ref[d]
        o_ref[...] = g

    return pl.pallas_call(body, name=name, out_shape=jax.ShapeDtypeStruct(parts.shape[1:], F32))(parts)


def adam_small(gs, ws, ms, vs):
    n = len(gs)

    def body(*refs):
        for k in range(n):
            g, w, m, v = (refs[j * n + k][...] for j in range(4))
            m2 = ADAM_B1 * m + (1.0 - ADAM_B1) * g
            v2 = ADAM_B2 * v + (1.0 - ADAM_B2) * (g * g)
            m_hat = m2 / (1.0 - ADAM_B1 ** ADAM_STEP)
            v_hat = v2 / (1.0 - ADAM_B2 ** ADAM_STEP)
            refs[4 * n + k][...] = -ADAM_LR * (m_hat / (jnp.sqrt(v_hat) + ADAM_EPS) + ADAM_WD * w)
            refs[5 * n + k][...] = m2
            refs[6 * n + k][...] = v2

    outs = pl.pallas_call(body, name="adam_small",
                          out_shape=[jax.ShapeDtypeStruct(w.shape, F32) for _ in range(3) for w in ws])(*gs, *ws, *ms, *vs)
    return outs[:n], outs[n:2 * n], outs[2 * n:]


def kernel(x, norm_g, w_in, w_out, gmlp_ln_g, gmlp_ln_b, gmlp_w_s, gmlp_b_s, hgrn_lb, hgrn_onorm_g, fox_b_f, final_norm_g, loss_target, m_norm_g, m_w_in, m_w_out, m_gmlp_ln_g, m_gmlp_ln_b, m_gmlp_w_s, m_gmlp_b_s, m_hgrn_lb, m_hgrn_onorm_g, m_fox_b_f, m_final_norm_g, v_norm_g, v_w_in, v_w_out, v_gmlp_ln_g, v_gmlp_ln_b, v_gmlp_w_s, v_gmlp_b_s, v_hgrn_lb, v_hgrn_onorm_g, v_fox_b_f, v_final_norm_g):
    depth = w_in.shape[0]
    seq = x.shape[1]
    assert w_in.shape[2] * N_DEV == N_IN
    xs, tgt = x[0], loss_target[0]

    wi_blk, wo_blk = w_in.astype(BF16), w_out.astype(BF16)
    (wi_all,) = _exchange_call(AllGatherWeights([wi_blk[0]]), "allgather_weights_0")

    ln_g = gmlp_ln_g.reshape(depth, 1, A_WIDTH)
    ln_b = gmlp_ln_b.reshape(depth, 1, A_WIDTH)
    bs_t = jnp.pad(jnp.transpose(gmlp_b_s, (0, 2, 1)), ((0, 0), (0, 0), (0, LANES - A_GROUPS)))
    lb0, lb1 = hgrn_lb[0:1], hgrn_lb[1:2]
    onorm = jnp.tile(hgrn_onorm_g, (1, B_HEADS)).reshape(depth, 1, B_WIDTH)
    bf_row = jnp.pad(fox_b_f, ((0, 0), (0, LANES - C_HEADS))).reshape(depth, 1, LANES)

    core = lax.axis_index("c").astype(jnp.int32).reshape(1)
    chip = (2 * lax.axis_index("x") + lax.axis_index("y")).astype(jnp.int32).reshape(1)

    saved = []
    xc = xs
    for l in range(depth):
        wi_int = assemble_w_in(wi_all[:, None])
        proj, qkv, h = inproj(xc, norm_g[l:l + 1], wi_int, 0)
        ya = gmlp_fwd(proj, ln_g[l], ln_b[l], gmlp_w_s[l], bs_t[l])
        yb, states = hgrn_fwd(proj, lb0, lb1, onorm[l], l)
        ka, va, vt, kt, qt, qa = fox_prep(proj, qkv, bf_row[l])
        ride = ([wo_blk] if l == 0 else []) + ([wi_blk[l + 1]] if l + 1 < depth else [])
        o, lse, *gathered = fox_fwd(qt, ka, vt, AllGatherWeights(ride) if ride else None)
        if l == 0:
            wo_all = gathered.pop(0)
        if gathered:
            (wi_all,) = gathered
        x_in = xc
        if l + 1 < depth:
            xc, yfull = outproj(x_in, ya, yb, o, proj, wo_all, l)
        else:
            dx, yfull, d_final_g, loss_tile = outproj(x_in, ya, yb, o, proj, wo_all, l, (final_norm_g[None], tgt))
        saved.append((x_in, proj, h, states, ka, va, kt, qt, qa, o, lse, yfull, wi_int))

    n_shard = w_in.shape[2]
    g_norm = [None] * depth
    g_ln_g, g_ln_b, g_ws, g_bs, g_on, g_bf = ([None] * depth for _ in range(6))
    g_lb0, g_lb1 = jnp.zeros_like(lb0), jnp.zeros_like(lb1)
    swi = swo = rwi = rwo = None
    for l in reversed(range(depth)):
        x_in, proj, h, states, ka, va, kt, qt, qa, o, lse, yfull, wi_int = saved[l]
        dy, gwo = outproj_bwd(dx, yfull, wo_all, l)
        dproj, g_ln_g[l], g_ln_b[l], g_ws[l], dbs_t = gmlp_bwd(proj, dy, ln_g[l], ln_b[l], gmlp_w_s[l], bs_t[l])
        g_bs[l] = dbs_t[:, :A_GROUPS].T
        if l > 0:
            (qwo,) = _exchange_call(PairExchange([gwo]), f"pair_exchange_w_out_{l}")
        else:
            gws = jnp.stack(g_ws).reshape(-1, LANES)
            qwo, qws = _exchange_call(PairExchange([gwo], [gws]), f"pair_exchange_w_out_{l}")
            sws = small_sum(gws, qws, "pair_sum_w_s")
        swo = pair_sum(gwo, qwo, BF16, gwo.shape[2], "pair_sum_w_out", core, l, depth, swo)
        dproj, d0, d1, don = hgrn_bwd(proj, states, dy, lb0, lb1, onorm[l], l, dproj)
        g_lb0, g_lb1 = g_lb0 + d0, g_lb1 + d1
        g_on[l] = don.reshape(B_HEADS, B_KDIM).sum(0)
        dob, dproj, dot_t = fox_bwd_prep(dy, o, proj, dproj)
        top = l == depth - 1
        ride = ChipExchange([swo] if top else [swi, swo], [(l,)] if top else [(l + 1,), (l,)],
                            [sws] if l == 0 else [], [rwo] if top else [rwi, rwo])
        outs = fox_bwd(ka, va, kt, qt, dot_t, qa, dob, lse, ride)
        dqkv, (dck, dcq), got = outs[:3], outs[3:5], list(outs[5:])
        if not top:
            rwi = got.pop(0)
        rwo = got.pop(0)
        if l == 0:
            (rws,) = got
        dproj, dbf = fox_post(dcq, dck, proj, bf_row[l], dproj)
        g_bf[l] = dbf[0, :C_HEADS]
        gwi, for_sibling = split_w_in_grad(inproj_bwd_w(h, dproj, dqkv), n_shard, core)
        (qwi,) = _exchange_call(PairExchange([], [for_sibling]), f"pair_exchange_w_in_{l}")
        swi = pair_sum(gwi, qwi, BF16, 256, "pair_sum_w_in", core, l, depth, swi)
        ride = ChipExchange([swi], [(l,)], stacked=[rwi]) if l == 0 else None
        outs = inproj_bwd_x(dproj, dqkv, wi_int, x_in, norm_g[l:l + 1], dx, 0, ride)
        dx, g_norm[l] = outs[:2]
        if ride is not None:
            (rwi,) = outs[2:]

    gsm = _pack_small([
        jnp.concatenate(g_norm), jnp.stack(g_ln_g), jnp.stack(g_ln_b), jnp.stack(g_bs),
        jnp.concatenate([g_lb0, g_lb1]), jnp.stack(g_on), jnp.stack(g_bf), d_final_g, loss_tile[0, 0]])
    (qsm,) = _exchange_call(PairExchange([], [gsm]), "pair_exchange_small")
    ssm = small_sum(gsm, qsm, "pair_sum_small")
    (rsm,) = _exchange_call(ChipExchange(gathered=[ssm]), "chip_exchange_small")

    small_w = (norm_g, gmlp_ln_g, gmlp_ln_b, gmlp_b_s, hgrn_lb, hgrn_onorm_g, fox_b_f, final_norm_g)
    small_m = (m_norm_g, m_gmlp_ln_g, m_gmlp_ln_b, m_gmlp_b_s, m_hgrn_lb, m_hgrn_onorm_g, m_fox_b_f, m_final_norm_g)
    small_v = (v_norm_g, v_gmlp_ln_g, v_gmlp_ln_b, v_gmlp_b_s, v_hgrn_lb, v_hgrn_onorm_g, v_fox_b_f, v_final_norm_g)
    res_wi = adam_reduce_columns(rwi, w_in, m_w_in, v_w_in, "adam_w_in", swi, chip)
    res_wo = adam_reduce(rwo, w_out, m_w_out, v_w_out, w_out.shape[1], "adam_w_out", own=swo, chip=chip)
    grads = _unpack_small(sum_parts(rsm, "sum_small"))
    names = [name for name, _ in _SMALL if name != "loss"]
    rows = lambda a: a.reshape(1, -1) if a.ndim == 1 else a
    res_sm = adam_small([rows(grads[k]) for k in names], *([rows(a) for a in wmv] for wmv in (small_w, small_m, small_v)))
    res_sm = [grads] + [{k: a.reshape(grads[k].shape) for k, a in zip(names, r, strict=True)} for r in res_sm]
    as_rows = lambda a: a.reshape(1, -1, LANES)
    res_ws = adam_reduce(rws[:, None], as_rows(gmlp_w_s), as_rows(m_gmlp_w_s), as_rows(v_gmlp_w_s), rws.shape[1], "adam_w_s")
    for s, r in zip(res_sm, res_ws, strict=True):
        s["gmlp_w_s"] = r.reshape(gmlp_w_s.shape)

    def group(i):
        s = res_sm[i]
        return [s["norm_g"], res_wi[i], res_wo[i], s["gmlp_ln_g"], s["gmlp_ln_b"], s["gmlp_w_s"], s["gmlp_b_s"],
                s["hgrn_lb"], s["hgrn_onorm_g"], s["fox_b_f"], s["final_norm_g"]]

    return (res_sm[0]["loss"], dx[None], *group(0), *group(1), *group(2), *group(3))
```

```python
import functools

import jax
import jax.numpy as jnp
import numpy as np
from jax import lax
from jax.experimental import pallas as pl
from jax.experimental.pallas import tpu as pltpu

F32 = jnp.float32
BF16 = jnp.bfloat16

NORM_EPS = 1e-6
F_FLOOR = 1e-30
CHUNK = 128
LANES = 128
VMEM_LIMIT = 56 * 1024 * 1024


def _cparams(*sem):
    return pltpu.CompilerParams(dimension_semantics=sem, vmem_limit_bytes=VMEM_LIMIT)


def _dot(a, b, dims=(((1,), (0,)), ((), ())), precision=None):
    return lax.dot_general(a, b, dims, precision=precision, preferred_element_type=F32)


_NT = (((1,), (1,)), ((), ()))
_TN = (((0,), (0,)), ((), ()))


def _bf16_pieces(x, n):
    out, r = [], x
    for i in range(n):
        out.append(r.astype(BF16))
        if i + 1 < n:
            r = r - out[-1].astype(F32)
    return out


@functools.partial(jax.custom_vjp, nondiff_argnums=(2,))
def _times_exact(x, e, n):
    return functools.reduce(jnp.add, [_dot(p, e) for p in _bf16_pieces(x, n)])


def _times_exact_fwd(x, e, n):
    return _times_exact(x, e, n), e


def _times_exact_bwd(n, e, g):
    dx = functools.reduce(jnp.add, [lax.dot_general(p, e, _NT, preferred_element_type=F32) for p in _bf16_pieces(g, n)])
    return dx, jnp.zeros_like(e)


_times_exact.defvjp(_times_exact_fwd, _times_exact_bwd)


@functools.partial(jax.custom_vjp, nondiff_argnums=(2,))
def _exact_times(e, x, n):
    return functools.reduce(jnp.add, [_dot(e, p) for p in _bf16_pieces(x, n)])


def _exact_times_fwd(e, x, n):
    return _exact_times(e, x, n), e


def _exact_times_bwd(n, e, g):
    dx = functools.reduce(jnp.add, [lax.dot_general(e, p, _TN, preferred_element_type=F32) for p in _bf16_pieces(g, n)])
    return jnp.zeros_like(e), dx


_exact_times.defvjp(_exact_times_fwd, _exact_times_bwd)


def _group_mean_matrix(width, group):
    idx = np.arange(width) // group
    return jnp.asarray((idx[:, None] == idx[None, :]).astype(np.float32) / group, BF16)


def _group_ones_matrix(width, group):
    idx = np.arange(width) // group
    return jnp.asarray((idx[:, None] == idx[None, :]).astype(np.float32), BF16)


A_WIDTH = 256
A_GROUPS = 4
A_GDIM = 64


A_ROWS = 512


def _gmlp_chunk(x3, ln_g, ln_b, w_s, bs_t, mean_m, gind):
    n = x3.shape[0] // CHUNK
    u = jax.nn.gelu(x3[:, :A_WIDTH])
    v = jax.nn.gelu(x3[:, A_WIDTH:2 * A_WIDTH])
    z = x3[:, 2 * A_WIDTH:]
    mu = _times_exact(v, mean_m, 2)
    d = v - mu
    var = _times_exact(d * d, mean_m, 2)
    vn = d * lax.rsqrt(var + NORM_EPS) * ln_g + ln_b
    vnb = vn.astype(BF16)
    wide = jnp.concatenate([vnb[i * CHUNK:(i + 1) * CHUNK] for i in range(n)], axis=1)
    row = lax.broadcasted_iota(jnp.int32, (CHUNK, CHUNK), 0)
    col = lax.broadcasted_iota(jnp.int32, (CHUNK, CHUNK), 1)
    causal = row >= col
    lane_g = lax.shift_right_logical(lax.broadcasted_iota(jnp.int32, (CHUNK, n * A_WIDTH), 1), 6) & (A_GROUPS - 1)
    bias = _times_exact(bs_t, gind, 3)
    mixed = jnp.concatenate([bias] * n, axis=1)
    for g in range(A_GROUPS):
        wc = jnp.where(causal, w_s[g], 0.0).astype(BF16)
        mixed = mixed + jnp.where(lane_g == g, _dot(wc, wide), 0.0)
    mixed = jnp.concatenate([mixed[:, i * A_WIDTH:(i + 1) * A_WIDTH] for i in range(n)], axis=0)
    return u * mixed * jax.nn.silu(z)


def _gmlp_consts():
    gind = np.zeros((LANES, A_WIDTH), np.float32)
    for g in range(A_GROUPS):
        gind[g, g * A_GDIM:(g + 1) * A_GDIM] = 1.0
    return _group_mean_matrix(A_WIDTH, A_GDIM), jnp.asarray(gind, BF16)


def _full(shape):
    return pl.BlockSpec(shape, lambda *_: (0,) * len(shape))


def gmlp_fwd(proj, ln_g, ln_b, w_s, bs_t):
    seq = proj.shape[0]
    rows = min(A_ROWS, seq)
    mean_m, gind = _gmlp_consts()

    def body(x_ref, g_ref, b_ref, w_ref, bs_ref, m_ref, gi_ref, y_ref):
        y = _gmlp_chunk(x_ref[...], g_ref[...], b_ref[...], w_ref[...], bs_ref[...], m_ref[...], gi_ref[...])
        y_ref[...] = y.astype(BF16)

    return pl.pallas_call(
        body,
        name="gmlp_fwd",
        grid=(seq // rows,),
        in_specs=[
            pl.BlockSpec((rows, 3 * A_WIDTH), lambda n: (n, 0)),
            _full((1, A_WIDTH)), _full((1, A_WIDTH)), _full((A_GROUPS, CHUNK, CHUNK)), _full((CHUNK, LANES)),
            _full((A_WIDTH, A_WIDTH)), _full((LANES, A_WIDTH)),
        ],
        out_specs=pl.BlockSpec((rows, A_WIDTH), lambda n: (n, 0)),
        out_shape=jax.ShapeDtypeStruct((seq, A_WIDTH), BF16),
        compiler_params=_cparams("parallel"),
    )(proj, ln_g, ln_b, w_s, bs_t, mean_m, gind)


def gmlp_bwd(proj, dy, ln_g, ln_b, w_s, bs_t):
    seq = proj.shape[0]
    rows = min(A_ROWS, seq)
    mean_m, gind = _gmlp_consts()

    def body(x_ref, dy_ref, g_ref, b_ref, w_ref, bs_ref, m_ref, gi_ref, dx_ref, dg_ref, db_ref, dw_ref, dbs_ref):
        fn = functools.partial(_gmlp_chunk, mean_m=m_ref[...], gind=gi_ref[...])
        _, vjp = jax.vjp(fn, x_ref[...], g_ref[...], b_ref[...], w_ref[...], bs_ref[...])
        dx, dg, db, dw, dbs = vjp(dy_ref[...])
        dx_ref[...] = dx.astype(BF16)

        @pl.when(pl.program_id(0) == 0)
        def _():
            dg_ref[...] = jnp.zeros_like(dg_ref)
            db_ref[...] = jnp.zeros_like(db_ref)
            dw_ref[...] = jnp.zeros_like(dw_ref)
            dbs_ref[...] = jnp.zeros_like(dbs_ref)

        dg_ref[...] += dg
        db_ref[...] += db
        dw_ref[...] += dw
        dbs_ref[...] += dbs

    return pl.pallas_call(
        body,
        name="gmlp_bwd",
        grid=(seq // rows,),
        in_specs=[
            pl.BlockSpec((rows, 3 * A_WIDTH), lambda n: (n, 0)),
            pl.BlockSpec((rows, A_WIDTH), lambda n: (n, 0)),
            _full((1, A_WIDTH)), _full((1, A_WIDTH)), _full((A_GROUPS, CHUNK, CHUNK)), _full((CHUNK, LANES)),
            _full((A_WIDTH, A_WIDTH)), _full((LANES, A_WIDTH)),
        ],
        out_specs=[
            pl.BlockSpec((rows, 3 * A_WIDTH), lambda n: (n, 0)),
            _full((1, A_WIDTH)), _full((1, A_WIDTH)), _full((A_GROUPS, CHUNK, CHUNK)), _full((CHUNK, LANES)),
        ],
        out_shape=[
            jax.ShapeDtypeStruct((seq, D_INT), BF16),
            jax.ShapeDtypeStruct((1, A_WIDTH), F32), jax.ShapeDtypeStruct((1, A_WIDTH), F32),
            jax.ShapeDtypeStruct((A_GROUPS, CHUNK, CHUNK), F32), jax.ShapeDtypeStruct((CHUNK, LANES), F32),
        ],
        compiler_params=_cparams("arbitrary"),
    )(proj, dy, ln_g, ln_b, w_s, bs_t, mean_m, gind)


B_WIDTH = 256
B_HEADS = 4
B_KDIM = 64
B_LEVELS = (64, 32, 16, 8, 4, 2, 1)


def _hgrn_consts():
    t = np.arange(CHUNK)
    u = t[None, :]
    mats = [np.tril(np.ones((CHUNK, CHUNK), np.float32))]
    for m in B_LEVELS:
        p = (t // (2 * m)) * (2 * m) + m - 1
        right = (t % (2 * m)) >= m
        sel = np.where(right[:, None], (u > p[:, None]) & (u <= t[:, None]), (u > t[:, None]) & (u <= p[:, None]))
        mats.append(sel.astype(np.float32))
    return jnp.asarray(np.concatenate(mats, 0), BF16), _group_ones_matrix(B_WIDTH, B_KDIM)


def _hgrn_lower_bound(lb0, lb1, layer):
    mx = jnp.maximum(lb0, lb1)
    e0 = jnp.exp(lb0 - mx)
    e1 = jnp.exp(lb1 - mx)
    p0 = e0 / (e0 + e1)
    p1 = e1 / (e0 + e1)
    cs = p0 if layer == 0 else p0 + p1
    return jnp.clip(cs - p0, 0.0, 1.0 - 1e-6)


def _hgrn_chunk(x4, st, lb0, lb1, onorm, layer, tstack, ones_bd):
    q_raw, fl, v, zg = (x4[:, i * B_WIDTH:(i + 1) * B_WIDTH] for i in range(4))
    lb = _hgrn_lower_bound(lb0, lb1, layer)
    q = jax.nn.silu(q_raw) * (B_KDIM ** -0.5)
    f = lb + (1.0 - lb) * jax.nn.sigmoid(fl)
    logf = jnp.log(jnp.maximum(f, F_FLOOR))
    k = (1.0 - lb) * jax.nn.sigmoid(-fl)
    b = _exact_times(tstack[:CHUNK], logf, 3)
    dall = jnp.concatenate([b, _exact_times(tstack[CHUNK:], logf, 2)], axis=0)
    b_last = jnp.sum(logf, axis=0, keepdims=True)
    vb = v.astype(BF16)

    lane_h = lax.shift_right_logical(lax.broadcasted_iota(jnp.int32, (CHUNK, B_WIDTH), 1), 6)
    row = lax.broadcasted_iota(jnp.int32, (CHUNK, B_WIDTH), 0)
    srow = lax.broadcasted_iota(jnp.int32, (B_HEADS * CHUNK, CHUNK), 0) & (CHUNK - 1)
    scol = lax.broadcasted_iota(jnp.int32, (B_HEADS * CHUNK, CHUNK), 1)

    def heads_on_rows(a):
        return jnp.concatenate([jnp.where(lane_h == h, a, 0.0) for h in range(B_HEADS)], axis=0)

    def heads_from_rows(r):
        out = jnp.where(lane_h == 0, r[:CHUNK], 0.0)
        for h in range(1, B_HEADS):
            out = out + jnp.where(lane_h == h, r[h * CHUNK:(h + 1) * CHUNK], 0.0)
        return out

    o = lax.dot_general((q * jnp.exp(b)).astype(BF16), st.astype(BF16), _NT, preferred_element_type=F32)
    scores = jnp.zeros((B_HEADS * CHUNK, CHUNK), F32)
    for li, m in enumerate(B_LEVELS):
        e = jnp.exp(dall[(li + 1) * CHUNK:(li + 2) * CHUNK])
        right = (row & (2 * m - 1)) >= m
        qt = jnp.where(right, q * e, 0.0)
        kt = jnp.where(right, 0.0, k * e)
        sc = lax.dot_general(heads_on_rows(qt).astype(BF16), kt.astype(BF16), _NT, preferred_element_type=F32)
        sh = int(np.log2(2 * m))
        same = lax.shift_right_logical(srow, sh) == lax.shift_right_logical(scol, sh)
        scores = scores + jnp.where(same, sc, 0.0)
    o = o + heads_from_rows(_dot(scores.astype(BF16), vb))
    o = o + _times_exact(q * k, ones_bd, 2) * v

    kv = lax.dot_general(vb, (k * jnp.exp(b_last - b)).astype(BF16), _TN, preferred_element_type=F32)
    st_new = st * jnp.exp(b_last) + jnp.where(ones_bd > 0.5, kv, 0.0)

    ms = _times_exact(o * o, ones_bd, 2) * (1.0 / B_KDIM)
    y = o * lax.rsqrt(ms + NORM_EPS) * onorm * jax.nn.silu(zg)
    return y, st_new


B_ROWS = 256


def _hgrn_rows(x4, st, lb0, lb1, onorm, layer, tstack, ones_bd):
    ys = []
    for i in range(x4.shape[0] // CHUNK):
        y, st = _hgrn_chunk(x4[i * CHUNK:(i + 1) * CHUNK], st, lb0, lb1, onorm, layer, tstack, ones_bd)
        ys.append(y)
    return jnp.concatenate(ys, axis=0), st


def hgrn_fwd(proj, lb0, lb1, onorm, layer):
    seq = proj.shape[0]
    rows = min(B_ROWS, seq)
    nc = seq // rows
    tstack, ones_bd = _hgrn_consts()

    def body(x_ref, lb0_ref, lb1_ref, on_ref, t_ref, e_ref, y_ref, st_out_ref, st_ref):
        @pl.when(pl.program_id(0) == 0)
        def _():
            st_ref[...] = jnp.zeros_like(st_ref)

        st = st_ref[...]
        st_out_ref[0] = st
        y, st_new = _hgrn_rows(x_ref[...], st, lb0_ref[...], lb1_ref[...], on_ref[...], layer, t_ref[...], e_ref[...])
        y_ref[...] = y.astype(BF16)
        st_ref[...] = st_new

    return pl.pallas_call(
        body,
        name=f"hgrn_fwd_{layer}",
        grid=(nc,),
        in_specs=[
            pl.BlockSpec((rows, 4 * B_WIDTH), lambda n: (n, 1)),
            _full((1, B_WIDTH)), _full((1, B_WIDTH)), _full((1, B_WIDTH)),
            _full(((len(B_LEVELS) + 1) * CHUNK, CHUNK)), _full((B_WIDTH, B_WIDTH)),
        ],
        out_specs=[
            pl.BlockSpec((rows, B_WIDTH), lambda n: (n, 0)),
            pl.BlockSpec((1, B_WIDTH, B_WIDTH), lambda n: (n, 0, 0)),
        ],
        out_shape=[jax.ShapeDtypeStruct((seq, B_WIDTH), BF16), jax.ShapeDtypeStruct((nc, B_WIDTH, B_WIDTH), F32)],
        scratch_shapes=[pltpu.VMEM((B_WIDTH, B_WIDTH), F32)],
        compiler_params=_cparams("arbitrary"),
    )(proj, lb0, lb1, onorm, tstack, ones_bd)


def hgrn_bwd(proj, states, dy, lb0, lb1, onorm, layer, dproj):
    seq = proj.shape[0]
    rows = min(B_ROWS, seq)
    nc = seq // rows
    tstack, ones_bd = _hgrn_consts()

    def body(x_ref, st_in_ref, dy_ref, lb0_ref, lb1_ref, on_ref, t_ref, e_ref, _, dx_ref, d0_ref, d1_ref, don_ref, dst_ref):
        @pl.when(pl.program_id(0) == 0)
        def _():
            dst_ref[...] = jnp.zeros_like(dst_ref)
            d0_ref[...] = jnp.zeros_like(d0_ref)
            d1_ref[...] = jnp.zeros_like(d1_ref)
            don_ref[...] = jnp.zeros_like(don_ref)

        fn = functools.partial(_hgrn_rows, layer=layer, tstack=t_ref[...], ones_bd=e_ref[...])
        _, vjp = jax.vjp(fn, x_ref[...], st_in_ref[0], lb0_ref[...], lb1_ref[...], on_ref[...])
        dx, dst, d0, d1, don = vjp((dy_ref[...], dst_ref[...]))
        dx_ref[...] = dx.astype(BF16)
        dst_ref[...] = dst
        d0_ref[...] += d0
        d1_ref[...] += d1
        don_ref[...] += don

    rev = lambda n: nc - 1 - n
    return pl.pallas_call(
        body,
        name=f"hgrn_bwd_{layer}",
        grid=(nc,),
        in_specs=[
            pl.BlockSpec((rows, 4 * B_WIDTH), lambda n: (rev(n), 1)),
            pl.BlockSpec((1, B_WIDTH, B_WIDTH), lambda n: (rev(n), 0, 0)),
            pl.BlockSpec((rows, B_WIDTH), lambda n: (rev(n), 1)),
            _full((1, B_WIDTH)), _full((1, B_WIDTH)), _full((1, B_WIDTH)),
            _full(((len(B_LEVELS) + 1) * CHUNK, CHUNK)), _full((B_WIDTH, B_WIDTH)), _ANY,
        ],
        out_specs=[
            pl.BlockSpec((rows, 4 * B_WIDTH), lambda n: (rev(n), 1)),
            _full((1, B_WIDTH)), _full((1, B_WIDTH)), _full((1, B_WIDTH)),
        ],
        out_shape=[jax.ShapeDtypeStruct(dproj.shape, BF16)] + [jax.ShapeDtypeStruct((1, B_WIDTH), F32)] * 3,
        input_output_aliases={8: 0},
        scratch_shapes=[pltpu.VMEM((B_WIDTH, B_WIDTH), F32)],
        compiler_params=_cparams("arbitrary"),
    )(proj, states, dy, lb0, lb1, onorm, tstack, ones_bd, dproj)


D_MODEL = 1024
D_INT = 4096


def _rms_stats(xf):
    r = lax.rsqrt(jnp.mean(xf * xf, axis=-1, keepdims=True) + NORM_EPS)
    return r, xf * r


def _rms_bwd(dy, g, r, xh):
    u = dy * g
    return r * (u - xh * jnp.mean(u * xh, axis=-1, keepdims=True))


C_QKV = (2048, 3584)
P_WIDTH = D_INT - (C_QKV[1] - C_QKV[0])
P_Z_BLOCK = C_QKV[0] // 512


def inproj(x, g, w, layer):
    seq = x.shape[0]
    tm = min(seq, 512)

    def body(x_ref, g_ref, w_ref, p_ref, qkv_ref, h_ref):
        _, xh = _rms_stats(x_ref[...])
        h = (xh * g_ref[...]).astype(BF16)
        h_ref[...] = h
        p_ref[:, :C_QKV[0]] = _dot(h, w_ref[0, :, :C_QKV[0]])
        qkv_ref[...] = _dot(h, w_ref[0, :, C_QKV[0]:C_QKV[1]]).astype(BF16)
        p_ref[:, C_QKV[0]:] = _dot(h, w_ref[0, :, C_QKV[1]:])

    rows = lambda n: pl.BlockSpec((tm, n), lambda i: (i, 0))
    return pl.pallas_call(
        body,
        name="inproj",
        grid=(seq // tm,),
        in_specs=[rows(D_MODEL), _full((1, D_MODEL)), pl.BlockSpec((1, D_MODEL, D_INT), lambda i: (layer, 0, 0))],
        out_specs=[rows(P_WIDTH), rows(C_QKV[1] - C_QKV[0]), rows(D_MODEL)],
        out_shape=[jax.ShapeDtypeStruct((seq, P_WIDTH), F32), jax.ShapeDtypeStruct((seq, C_QKV[1] - C_QKV[0]), BF16),
                   jax.ShapeDtypeStruct((seq, D_MODEL), BF16)],
        compiler_params=_cparams("parallel"),
    )(x, g, w)


def outproj(x, ya, yb, o, proj, wo, layer, head=None):
    seq = x.shape[0]
    tm = min(seq, 512)
    blk = wo.shape[2]

    def body(x_ref, ya_ref, yb_ref, o_ref, z_ref, w_ref, *refs):
        yc = (o_ref[...] * jax.nn.silu(z_ref[...])).astype(BF16)
        y = jnp.concatenate([ya_ref[...], yb_ref[...], yc], axis=1)
        w = jnp.concatenate([w_ref[d, 0] for d in range(N_DEV)], axis=0)
        xn = x_ref[...] + _dot(y, w)
        if head is None:
            xn_ref, y_ref = refs
            xn_ref[...] = xn
        else:
            g_ref, t_ref, dx_ref, y_ref, dg_ref, loss_ref = refs

            @pl.when(pl.program_id(0) == 0)
            def _():
                dg_ref[...] = jnp.zeros_like(dg_ref)
                loss_ref[...] = jnp.zeros_like(loss_ref)

            g = g_ref[...]
            r, xh = _rms_stats(xn)
            err = xh * g - t_ref[...]
            sq = jnp.sum(jnp.sum(err * err, axis=1, keepdims=True), axis=0, keepdims=True)
            loss_ref[...] += jnp.broadcast_to(sq * (0.5 / D_MODEL), loss_ref.shape)
            dout = err * (1.0 / D_MODEL)
            dg_ref[...] += jnp.sum(dout * xh, axis=0, keepdims=True)
            dx_ref[...] = _rms_bwd(dout, g, r, xh)
        y_ref[...] = y

    rows = lambda: pl.BlockSpec((tm, D_MODEL), lambda i: (i, 0))
    tail = (() if head is None else (_full((1, D_MODEL)), rows()),
            () if head is None else (_full((1, D_MODEL)), _full((8, LANES))),
            () if head is None else (jax.ShapeDtypeStruct((1, D_MODEL), F32), jax.ShapeDtypeStruct((8, LANES), F32)))
    return pl.pallas_call(
        body,
        name="outproj" if head is None else "outproj_loss",
        grid=(seq // tm,),
        in_specs=[
            rows(),
            pl.BlockSpec((tm, 256), lambda i: (i, 0)),
            pl.BlockSpec((tm, 256), lambda i: (i, 0)),
            pl.BlockSpec((tm, 512), lambda i: (i, 0)),
            pl.BlockSpec((tm, 512), lambda i: (i, P_Z_BLOCK)),
            pl.BlockSpec((N_DEV, 1, blk, D_MODEL), lambda i: (0, layer, 0, 0)),
            *tail[0],
        ],
        out_specs=[rows(), rows(), *tail[1]],
        out_shape=[jax.ShapeDtypeStruct((seq, D_MODEL), F32), jax.ShapeDtypeStruct((seq, D_MODEL), BF16), *tail[2]],
        compiler_params=_cparams("parallel" if head is None else "arbitrary"),
    )(x, ya, yb, o, proj, wo, *(head or ()))


def outproj_bwd(dx, y, wo, layer):
    seq = dx.shape[0]
    ts = min(seq, 512)
    blk = wo.shape[2]

    def body(dx_ref, y_ref, w_ref, dy_ref, dw_ref):
        @pl.when(pl.program_id(0) == 0)
        def _():
            dw_ref[...] = jnp.zeros_like(dw_ref)

        dxb = dx_ref[...].astype(BF16)
        w = jnp.concatenate([w_ref[d, 0] for d in range(N_DEV)], axis=0)
        dy_ref[...] = lax.dot_general(dxb, w, _NT, preferred_element_type=F32)
        dw = lax.dot_general(y_ref[...], dxb, _TN, preferred_element_type=F32)
        for d in range(N_DEV):
            dw_ref[d % 2, d // 2] += dw[d * blk:(d + 1) * blk]

    return pl.pallas_call(
        body,
        name="outproj_bwd",
        grid=(seq // ts,),
        in_specs=[
            pl.BlockSpec((ts, D_MODEL), lambda i: (i, 0)),
            pl.BlockSpec((ts, D_MODEL), lambda i: (i, 0)),
            pl.BlockSpec((N_DEV, 1, blk, D_MODEL), lambda i: (0, layer, 0, 0)),
        ],
        out_specs=[pl.BlockSpec((ts, D_MODEL), lambda i: (i, 0)),
                   pl.BlockSpec((2, N_CHIP, blk, D_MODEL), lambda i: (0, 0, 0, 0))],
        out_shape=[jax.ShapeDtypeStruct((seq, D_MODEL), F32), jax.ShapeDtypeStruct((2, N_CHIP, blk, D_MODEL), F32)],
        compiler_params=_cparams("arbitrary"),
    )(dx, y, wo)


def _dproj_parts(dp_ref, dqkv_refs, rows):
    lo, hi = C_QKV
    step = (hi - lo) // len(dqkv_refs)
    return ([(0, dp_ref.at[rows, 0:lo])] + [(lo + i * step, r.at[rows, :]) for i, r in enumerate(dqkv_refs)]
            + [(hi, dp_ref.at[rows, hi:D_INT])])


def inproj_bwd_x(dproj, dqkv, w, x, g, dx_in, layer, carried=None):
    seq = x.shape[0]
    tm = min(seq, 512)

    def body(dp_ref, dq_ref, dk_ref, dv_ref, w_ref, x_ref, g_ref, dxin_ref, dx_ref, dg_ref):
        @pl.when(pl.program_id(0) == 0)
        def _():
            dg_ref[...] = jnp.zeros_like(dg_ref)

        dh = None
        for at, part in _dproj_parts(dp_ref, (dq_ref, dk_ref, dv_ref), slice(None)):
            term = lax.dot_general(part[...], w_ref[0, :, at:at + part.shape[1]], _NT, preferred_element_type=F32)
            dh = term if dh is None else dh + term
        r, xh = _rms_stats(x_ref[...])
        dg_ref[...] += jnp.sum(dh * xh, axis=0, keepdims=True)
        dx_ref[...] = dxin_ref[...] + _rms_bwd(dh, g_ref[...], r, xh)

    third = lambda: pl.BlockSpec((tm, C_WIDTH), lambda i: (i, 0))
    return _call_carrying(
        carried, body, (dproj, *dqkv, w, x, g, dx_in),
        name="inproj_bwd_x",
        grid=(seq // tm,),
        in_specs=[
            pl.BlockSpec((tm, D_INT), lambda i: (i, 0)), third(), third(), third(),
            pl.BlockSpec((1, D_MODEL, D_INT), lambda i: (layer, 0, 0)),
            pl.BlockSpec((tm, D_MODEL), lambda i: (i, 0)),
            _full((1, D_MODEL)),
            pl.BlockSpec((tm, D_MODEL), lambda i: (i, 0)),
        ],
        out_specs=[pl.BlockSpec((tm, D_MODEL), lambda i: (i, 0)), _full((1, D_MODEL))],
        out_shape=[jax.ShapeDtypeStruct((seq, D_MODEL), F32), jax.ShapeDtypeStruct((1, D_MODEL), F32)],
        scratch_shapes=[], semantics=("arbitrary",),
    )


def inproj_bwd_w(h, dproj, dqkv):
    seq = h.shape[0]
    ts, tn = min(seq, 512), 512

    def body(h_ref, dp_ref, dq_ref, dk_ref, dv_ref, dw_ref):
        @pl.when(pl.program_id(0) == 0)
        def _():
            dw_ref[...] = jnp.zeros_like(dw_ref)

        ht = h_ref[...].T
        for at, part in _dproj_parts(dp_ref, (dq_ref, dk_ref, dv_ref), slice(None)):
            for c in range(0, part.shape[1], tn):
                dw_ref[0, :, at + c:at + c + tn] += _dot(ht, part[:, c:c + tn])

    third = lambda: pl.BlockSpec((ts, C_WIDTH), lambda s: (s, 0))
    return pl.pallas_call(
        body,
        name="inproj_bwd_w",
        grid=(seq // ts,),
        in_specs=[pl.BlockSpec((ts, D_MODEL), lambda s: (s, 0)), pl.BlockSpec((ts, D_INT), lambda s: (s, 0)),
                  third(), third(), third()],
        out_specs=_full((1, D_MODEL, D_INT)),
        out_shape=jax.ShapeDtypeStruct((1, D_MODEL, D_INT), F32),
        compiler_params=_cparams("arbitrary"),
    )(h, dproj, *dqkv)


N_IN = 3848


def _internal_of(col):
    return col if col < 768 else (col + 256 if col < 3840 else 768 + col - 3840)


def _column_runs(n_shard):
    runs = []
    for d in range(N_IN // n_shard):
        mine = []
        for j in range(n_shard):
            ci = _internal_of(d * n_shard + j)
            if mine and mine[-1][0] + mine[-1][1] == ci:
                mine[-1][1] += 1
            else:
                mine.append([ci, 1, j])
        runs.append(mine)
    return runs


def assemble_w_in(wi_all):
    n_dev, depth, _, n_shard = wi_all.shape
    tr = 256
    pieces = [[] for _ in range(D_INT // LANES)]
    for d, mine in enumerate(_column_runs(n_shard)):
        for ci, ln, off in mine:
            while ln > 0:
                blk, at = divmod(ci, LANES)
                take = min(ln, LANES - at)
                pieces[blk].append((at, take, d, off))
                ci, ln, off = ci + take, ln - take, off + take

    def body(x_ref, o_ref):
        for blk, parts in enumerate(pieces):
            vals, at = [], 0
            for start, ln, d, off in sorted(parts):
                if start > at:
                    vals.append(jnp.zeros((tr, start - at), BF16))
                vals.append(x_ref[d, 0, :, off:off + ln])
                at = start + ln
            if at < LANES:
                vals.append(jnp.zeros((tr, LANES - at), BF16))
            o_ref[0, :, blk * LANES:(blk + 1) * LANES] = vals[0] if len(vals) == 1 else jnp.concatenate(vals, axis=1)

    return pl.pallas_call(
        body,
        name="assemble_w_in",
        grid=(depth, D_MODEL // tr),
        in_specs=[pl.BlockSpec((n_dev, 1, tr, n_shard), lambda l, r: (0, l, r, 0))],
        out_specs=pl.BlockSpec((1, tr, D_INT), lambda l, r: (l, r, 0)),
        out_shape=jax.ShapeDtypeStruct((depth, D_MODEL, D_INT), BF16),
        compiler_params=_cparams("parallel", "parallel"),
    )(wi_all)


def split_w_in_grad(dwi, n_shard, core):
    tr = 256
    runs = _column_runs(n_shard)

    def body(core_ref, x_ref, keep_ref, send_ref):
        for d, mine in enumerate(runs):
            @pl.when(core_ref[0] == d % 2)
            def _():
                for ci, ln, off in mine:
                    keep_ref[d // 2, :, off:off + ln] = x_ref[0, :, ci:ci + ln]

            @pl.when(core_ref[0] != d % 2)
            def _():
                for ci, ln, off in mine:
                    send_ref[d // 2, :, off:off + ln] = x_ref[0, :, ci:ci + ln].astype(BF16)

    shards = lambda: pl.BlockSpec((N_CHIP, tr, n_shard), lambda r, s: (0, r, 0))
    grid_spec = pltpu.PrefetchScalarGridSpec(
        num_scalar_prefetch=1, grid=(D_MODEL // tr,),
        in_specs=[pl.BlockSpec((1, tr, D_INT), lambda r, s: (0, r, 0))], out_specs=[shards(), shards()])
    return pl.pallas_call(
        body,
        name="split_w_in_grad",
        grid_spec=grid_spec,
        out_shape=[jax.ShapeDtypeStruct((N_CHIP, D_MODEL, n_shard), F32), jax.ShapeDtypeStruct((N_CHIP, D_MODEL, n_shard), BF16)],
        compiler_params=_cparams("parallel"),
    )(core, dwi)


C_WIDTH = 512
C_HEADS = 8
C_HDIM = 64
C_PAIRS = C_HEADS // 2
C_BQ = 512
C_TAIL = 16
C_KG = 4


def _split3(x):
    hi = x.astype(BF16)
    r = x - hi.astype(F32)
    mid = r.astype(BF16)
    return hi, mid, (r - mid.astype(F32)).astype(BF16)


def _piece_selectors():
    sel = np.zeros((C_HEADS, 3 * LANES, LANES), np.float32)
    for p in range(C_PAIRS):
        for e in range(2):
            for t in range(3):
                sel[2 * p + e, t * LANES + 2 * p + e, 3 * e + t] = -1.0
    return sel


def fox_prep(proj, qkv, bf_row):
    seq = proj.shape[0]
    nblk = seq // CHUNK
    tril = jnp.asarray(np.tril(np.ones((CHUNK, CHUNK), np.float32)), BF16)
    sel = jnp.asarray(_piece_selectors(), BF16)
    rows_t = CHUNK + C_TAIL

    def body(fl_ref, q_ref, k_ref, v_ref, bf_ref, l_ref, sel_ref, ka_ref, va_ref, vt_ref, kt_ref, qt_ref, qa_ref, carry_ref):
        @pl.when(pl.program_id(0) == 0)
        def _():
            carry_ref[...] = jnp.zeros_like(carry_ref)

        lf = jax.nn.log_sigmoid(fl_ref[:, :LANES] + bf_ref[...])
        c = _exact_times(l_ref[...], lf, 3) + carry_ref[...]
        carry_ref[...] += jnp.sum(lf, axis=0, keepdims=True)
        c3 = jnp.concatenate(_split3(c), axis=1)
        lane = lax.broadcasted_iota(jnp.int32, (CHUNK, LANES), 1)
        row = lax.broadcasted_iota(jnp.int32, (CHUNK, LANES), 0)
        r16 = lax.broadcasted_iota(jnp.int32, (C_TAIL, 2 * CHUNK), 0)
        l16 = lax.broadcasted_iota(jnp.int32, (C_TAIL, 2 * CHUNK), 1)
        zero = jnp.zeros((CHUNK, LANES), BF16)
        one = jnp.ones((CHUNK, LANES), BF16)

        def by_keys(x, right_a, right_b):
            xb = x.astype(BF16)
            top = jnp.concatenate([jnp.where(lane < C_HDIM, xb, zero), right_a], axis=1)
            return jnp.concatenate([top, jnp.concatenate([jnp.where(lane < C_HDIM, zero, xb), right_b], axis=1)], axis=0)

        def by_lanes(x, tail):
            xt = x.T.astype(BF16)
            main = jnp.concatenate([jnp.where(row < C_HDIM, xt, zero), jnp.where(row < C_HDIM, zero, xt)], axis=1)
            return jnp.concatenate([main, tail], axis=0)

        for p in range(C_PAIRS):
            cols = slice(p * LANES, (p + 1) * LANES)
            q2, k2, v2 = (r[:, cols].astype(F32) for r in (q_ref, k_ref, v_ref))
            q2 = q2 * (C_HDIM ** -0.5)
            negc = [_dot(c3, sel_ref[2 * p + e]).astype(BF16) for e in range(2)]
            ones3 = [jnp.where((lane >= 3 * e) & (lane < 3 * e + 3), one, zero) for e in range(2)]
            tail = jnp.where(((r16 == 2 * p) & (l16 < CHUNK)) | ((r16 == 2 * p + 1) & (l16 >= CHUNK)), 1.0, 0.0).astype(BF16)
            ka_ref[p] = by_keys(k2, negc[0], negc[1])
            va_ref[p] = by_keys(v2, ones3[0], ones3[1])
            kt_ref[p] = by_lanes(k2, tail)
            vt_ref[p] = by_lanes(v2, tail)
            qt_ref[p] = jnp.concatenate([q2.T.astype(BF16), jnp.where(row < 6, one, zero)], axis=0)
            qa_ref[p] = jnp.concatenate([q2.astype(BF16), jnp.where((lane == 2 * p) | (lane == 2 * p + 1), one, zero)], axis=1)

    wide = lambda j: pl.BlockSpec((CHUNK, C_WIDTH), lambda n: (n, j))
    by_rows = pl.BlockSpec((C_PAIRS, 2 * CHUNK, 2 * CHUNK), lambda n: (0, n, 0))
    by_cols = pl.BlockSpec((C_PAIRS, rows_t, 2 * CHUNK), lambda n: (0, 0, n))
    return pl.pallas_call(
        body,
        name="fox_prep",
        grid=(nblk,),
        in_specs=[pl.BlockSpec((CHUNK, 256), lambda n: (n, 3)), wide(0), wide(1), wide(2), _full((1, LANES)),
                  _full((CHUNK, CHUNK)), _full((C_HEADS, 3 * LANES, LANES))],
        out_specs=[by_rows, by_rows, by_cols, by_cols,
                   pl.BlockSpec((C_PAIRS, 2 * CHUNK, CHUNK), lambda n: (0, 0, n)),
                   pl.BlockSpec((C_PAIRS, CHUNK, 2 * CHUNK), lambda n: (0, n, 0))],
        out_shape=[jax.ShapeDtypeStruct((C_PAIRS, 2 * seq, 2 * CHUNK), BF16)] * 2
        + [jax.ShapeDtypeStruct((C_PAIRS, rows_t, 2 * seq), BF16)] * 2
        + [jax.ShapeDtypeStruct((C_PAIRS, 2 * CHUNK, seq), BF16), jax.ShapeDtypeStruct((C_PAIRS, seq, 2 * CHUNK), BF16)],
        scratch_shapes=[pltpu.VMEM((1, LANES), F32)],
        compiler_params=_cparams("arbitrary"),
    )(proj, qkv, qkv, qkv, bf_row, tril, sel)


def _visible(shape, key0, query0):
    row = lax.broadcasted_iota(jnp.int32, shape, 0)
    key = key0 + lax.shift_left(lax.shift_right_logical(row, 8), 7) + (row & (CHUNK - 1))
    return key <= query0 + lax.broadcasted_iota(jnp.int32, shape, 1)


def _rows_ab(a, b, n):
    return jnp.concatenate([jnp.broadcast_to(a, (C_HDIM, n)), jnp.broadcast_to(b, (C_HDIM, n))], axis=0)


def _call_carrying(ex, body, operands, *, name, grid, in_specs, out_specs, out_shape, scratch_shapes, semantics=None):
    if ex is None:
        semantics = semantics or ("parallel", *["arbitrary"] * (len(grid) - 1))
        return pl.pallas_call(body, name=name, grid=grid, in_specs=in_specs, out_specs=out_specs, out_shape=out_shape,
                              scratch_shapes=scratch_shapes, compiler_params=_cparams(*semantics))(*operands)
    n_in, n_out = len(in_specs), len(out_specs)

    def wrapped(*refs):
        own, parts = _carried_refs(refs, n_in, n_out, ex)
        ids = [pl.program_id(a) for a in range(len(grid))]
        pl.when(functools.reduce(jnp.logical_and, [i == 0 for i in ids]))(lambda: ex.start(*parts))
        if hasattr(ex, "relay"):
            linear = functools.reduce(lambda at, ig: at * ig[1] + ig[0], zip(ids, grid), 0)
            pl.when(linear == int(np.prod(grid)) // 2)(lambda: ex.relay(*parts))
        body(*own)
        pl.when(functools.reduce(jnp.logical_and, [i == g - 1 for i, g in zip(ids, grid)]))(lambda: ex.finish(*parts))

    return pl.pallas_call(
        wrapped, name=name, grid=grid,
        in_specs=list(in_specs) + [_ANY] * len(ex.inputs), out_specs=list(out_specs) + [_ANY] * len(ex.out_shape),
        out_shape=list(out_shape) + list(ex.out_shape), scratch_shapes=list(scratch_shapes) + list(ex.scratch),
        input_output_aliases={n_in + i: n_out + o for i, o in getattr(ex, "aliases", {}).items()},
        compiler_params=_cparams(*["arbitrary"] * len(grid)),
    )(*operands, *ex.inputs)


def fox_fwd(qt, ka, vt, carried=None):
    seq = qt.shape[2]
    nblk = seq // CHUNK
    bq = min(C_BQ, seq)
    grp = bq // CHUNK
    rows_t = CHUNK + C_TAIL

    def body(qt_ref, ka_ref, vt_ref, o_ref, lse_ref, acc_ref, s_ref):
        p, i = pl.program_id(0), pl.program_id(1)
        qtile = qt_ref[0]
        r16 = lax.broadcasted_iota(jnp.int32, (C_TAIL, bq), 0)

        def scores(t):
            at = pl.multiple_of(t * grp * 2 * CHUNK, 2 * CHUNK)
            return _dot(ka_ref[0, pl.ds(at, grp * 2 * CHUNK), :], qtile)

        def rescale(al_a, al_b):
            tail = jnp.where(r16 == 2 * p, al_a, jnp.where(r16 == 2 * p + 1, al_b, 1.0))
            return jnp.concatenate([_rows_ab(al_a, al_b, bq), tail], axis=0)

        def diagonal(m):
            ma, mb = m
            na, nb = ma, mb
            blocks = []
            for g in range(grp):
                s = s_ref[g * 2 * CHUNK:(g + 1) * 2 * CHUNK, g * CHUNK:]
                s = jnp.where(_visible(s.shape, i * bq + g * CHUNK, i * bq + g * CHUNK), s, -jnp.inf)
                blocks.append(s)
                unseen = [jnp.full((1, g * CHUNK), -jnp.inf, F32)] if g else []
                na = jnp.maximum(na, jnp.concatenate(unseen + [jnp.max(s[:CHUNK], axis=0, keepdims=True)], axis=1))
                nb = jnp.maximum(nb, jnp.concatenate(unseen + [jnp.max(s[CHUNK:], axis=0, keepdims=True)], axis=1))
            acc_ref[...] = acc_ref[...] * rescale(jnp.exp(ma - na), jnp.exp(mb - nb))
            for g in range(grp):
                n = bq - g * CHUNK
                n2 = jnp.concatenate([jnp.broadcast_to(na[:, g * CHUNK:], (CHUNK, n)),
                                      jnp.broadcast_to(nb[:, g * CHUNK:], (CHUNK, n))], axis=0)
                at = pl.multiple_of((i * grp + g) * 2 * CHUNK, 2 * CHUNK)
                pt = jnp.exp(blocks[g] - n2).astype(BF16)
                acc_ref[:, g * CHUNK:] += _dot(vt_ref[0, :, pl.ds(at, 2 * CHUNK)], pt)
            return na, nb

        def group(t, m):
            ma, mb = m
            at = pl.multiple_of(t * grp * 2 * CHUNK, 2 * CHUNK)
            s = s_ref[...]
            sa = [s[g * 2 * CHUNK:g * 2 * CHUNK + CHUNK] for g in range(grp)]
            sb = [s[g * 2 * CHUNK + CHUNK:(g + 1) * 2 * CHUNK] for g in range(grp)]
            na, nb = ma, mb
            for g in range(grp):
                na = jnp.maximum(na, jnp.max(sa[g], axis=0, keepdims=True))
                nb = jnp.maximum(nb, jnp.max(sb[g], axis=0, keepdims=True))
            al_a, al_b = jnp.exp(ma - na), jnp.exp(mb - nb)
            pt = jnp.concatenate([jnp.exp(x - n) for g in range(grp) for x, n in ((sa[g], na), (sb[g], nb))], axis=0)
            pv = _dot(vt_ref[0, :, pl.ds(at, grp * 2 * CHUNK)], pt.astype(BF16))
            acc_ref[...] = acc_ref[...] * rescale(al_a, al_b) + pv
            return na, nb

        def step(t, m):
            s_next = scores(t + 1)
            m = group(t, m)
            s_ref[...] = s_next
            return m

        acc_ref[...] = jnp.zeros_like(acc_ref)
        s_ref[...] = scores(0)
        m = (jnp.full((1, bq), -jnp.inf, F32), jnp.full((1, bq), -jnp.inf, F32))
        m = lax.fori_loop(0, i, step, m)
        ma, mb = diagonal(m)
        tailv = acc_ref[CHUNK:rows_t, :]
        la = jnp.sum(jnp.where(r16 == 2 * p, tailv, 0.0), axis=0, keepdims=True)
        lb = jnp.sum(jnp.where(r16 == 2 * p + 1, tailv, 0.0), axis=0, keepdims=True)
        o_ref[...] = (acc_ref[0:CHUNK, :] * _rows_ab(1.0 / la, 1.0 / lb, bq)).T
        lse_ref[0, 0:1, :] = ma + jnp.log(la)
        lse_ref[0, 1:2, :] = mb + jnp.log(lb)

    return _call_carrying(
        carried, body, (qt, ka, vt),
        name="fox_fwd",
        grid=(C_PAIRS, seq // bq),
        in_specs=[
            pl.BlockSpec((1, 2 * CHUNK, bq), lambda p, i: (p, 0, i)),
            pl.BlockSpec((1, 2 * seq, 2 * CHUNK), lambda p, i: (p, 0, 0)),
            pl.BlockSpec((1, rows_t, 2 * seq), lambda p, i: (p, 0, 0)),
        ],
        out_specs=[pl.BlockSpec((bq, LANES), lambda p, i: (i, p)), pl.BlockSpec((1, 2, bq), lambda p, i: (p, 0, i))],
        out_shape=[jax.ShapeDtypeStruct((seq, C_WIDTH), F32), jax.ShapeDtypeStruct((C_PAIRS, 2, seq), F32)],
        scratch_shapes=[pltpu.VMEM((rows_t, bq), F32), pltpu.VMEM((grp * 2 * CHUNK, bq), F32)],
    )


def fox_bwd_prep(dy, o, proj, dproj):
    seq = o.shape[0]
    rows = min(seq, 512)
    ind = np.zeros((C_WIDTH, LANES), np.float32)
    for h in range(C_HEADS):
        ind[h * C_HDIM:(h + 1) * C_HDIM, h] = 1.0
    ind = jnp.asarray(ind, BF16)
    sel = _piece_selectors()
    sel = jnp.asarray(np.stack([sel[2 * p].T + sel[2 * p + 1].T for p in range(C_PAIRS)]), BF16)

    def body(dy_ref, o_ref, z_ref, ind_ref, sel_ref, _, do_ref, dz_ref, dot_ref):
        dy_c, o_v, z = dy_ref[...], o_ref[...], z_ref[...]
        sg = jax.nn.sigmoid(z)
        do = dy_c * (z * sg)
        do_ref[...] = do.astype(BF16)
        dz_ref[...] = (dy_c * o_v * (sg * (1.0 + z * (1.0 - sg)))).astype(BF16)
        prod = do * o_v
        hi = prod.astype(BF16)
        lo = (prod - hi.astype(F32)).astype(BF16)
        delta = _dot(hi, ind_ref[...]) + _dot(lo, ind_ref[...])
        d3 = jnp.concatenate(_split3(delta.T), axis=0)
        for p in range(C_PAIRS):
            tail = _dot(sel_ref[p], d3).astype(BF16)
            dot_ref[p] = jnp.concatenate([do[:, p * LANES:(p + 1) * LANES].T.astype(BF16), tail], axis=0)

    return pl.pallas_call(
        body,
        name="fox_bwd_prep",
        grid=(seq // rows,),
        in_specs=[
            pl.BlockSpec((rows, C_WIDTH), lambda i: (i, 1)),
            pl.BlockSpec((rows, C_WIDTH), lambda i: (i, 0)),
            pl.BlockSpec((rows, C_WIDTH), lambda i: (i, P_Z_BLOCK)),
            _full((C_WIDTH, LANES)), _full((C_PAIRS, LANES, 3 * LANES)), _ANY,
        ],
        out_specs=[
            pl.BlockSpec((rows, C_WIDTH), lambda i: (i, 0)),
            pl.BlockSpec((rows, C_WIDTH), lambda i: (i, 7)),
            pl.BlockSpec((C_PAIRS, 2 * CHUNK, rows), lambda i: (0, 0, i)),
        ],
        out_shape=[jax.ShapeDtypeStruct((seq, C_WIDTH), BF16), jax.ShapeDtypeStruct(dproj.shape, BF16),
                   jax.ShapeDtypeStruct((C_PAIRS, 2 * CHUNK, seq), BF16)],
        input_output_aliases={5: 1},
        compiler_params=_cparams("parallel"),
    )(dy, o, proj, ind, sel, dproj)


def fox_bwd(ka, va, kt, qt, dot_t, qa, dob, lse, carried=None):
    seq = qt.shape[2]
    nblk = seq // CHUNK
    bq = min(C_BQ, seq)
    nq = seq // bq
    kg = min(C_KG, nblk)
    ng = nblk // kg
    rows_t = CHUNK + C_TAIL

    def body(ka_ref, va_ref, kt_ref, qt_ref, dot_ref, qa_ref, do_ref, lse_ref,
             dq_ref, dk_ref, dv_ref, dck_ref, dcq_ref, dqt_acc, dv_acc, dka_acc):
        p, jg = pl.program_id(0), pl.program_id(1)

        @pl.when(jg == 0)
        def _():
            dqt_acc[...] = jnp.zeros_like(dqt_acc)

        dv_acc[...] = jnp.zeros_like(dv_acc)
        dka_acc[...] = jnp.zeros_like(dka_acc)

        def step(i, carry):
            cols = pl.ds(pl.multiple_of(i * bq, bq), bq)
            qtile, dotile = qt_ref[0, :, cols], dot_ref[0, :, cols]
            do, qa_i = do_ref[cols, :], qa_ref[0, cols, :]
            lse2 = jnp.concatenate([jnp.broadcast_to(lse_ref[0, 0:1, cols], (CHUNK, bq)),
                                    jnp.broadcast_to(lse_ref[0, 1:2, cols], (CHUNK, bq))] * kg, axis=0)
            pt = jnp.exp(_dot(ka_ref[0], qtile) - lse2)
            ds = pt * _dot(va_ref[0], dotile)
            ptb, dsb = pt.astype(BF16), ds.astype(BF16)
            dv_acc[...] += _dot(ptb, do)
            dka_acc[...] += _dot(dsb, qa_i)
            dqt_acc[:, cols] += _dot(kt_ref[0], dsb)
            return carry

        def diagonal(i):
            cols = [pl.ds(pl.multiple_of(i * bq + kb * CHUNK, CHUNK), bq - kb * CHUNK) for kb in range(kg)]
            rows = [slice(kb * 2 * CHUNK, (kb + 1) * 2 * CHUNK) for kb in range(kg)]
            s = [_dot(ka_ref[0, rows[kb], :], qt_ref[0, :, cols[kb]]) for kb in range(kg)]
            dp = [_dot(va_ref[0, rows[kb], :], dot_ref[0, :, cols[kb]]) for kb in range(kg)]
            ptb, dsb = [], []
            for kb in range(kg):
                n = bq - kb * CHUNK
                lse2 = jnp.concatenate([jnp.broadcast_to(lse_ref[0, 0:1, cols[kb]], (CHUNK, n)),
                                        jnp.broadcast_to(lse_ref[0, 1:2, cols[kb]], (CHUNK, n))], axis=0)
                pt = jnp.exp(s[kb] - lse2)
                pt = jnp.where(_visible(pt.shape, (jg * kg + kb) * CHUNK, i * bq + kb * CHUNK), pt, 0.0)
                ptb.append(pt.astype(BF16))
                dsb.append((pt * dp[kb]).astype(BF16))
            for kb in range(kg):
                dv_acc[rows[kb], :] += _dot(ptb[kb], do_ref[cols[kb], :])
                dka_acc[rows[kb], :] += _dot(dsb[kb], qa_ref[0, cols[kb], :])
                dqt_acc[:, cols[kb]] += _dot(kt_ref[0, :, rows[kb]], dsb[kb])

        assert kg * CHUNK == bq
        diagonal(jg)
        lax.fori_loop(jg + 1, nq, step, 0)
        lane = lax.broadcasted_iota(jnp.int32, (CHUNK, LANES), 1)
        for kb in range(kg):
            rows = slice(kb * CHUNK, (kb + 1) * CHUNK)
            ra = slice(kb * 2 * CHUNK, kb * 2 * CHUNK + CHUNK)
            rb = slice(kb * 2 * CHUNK + CHUNK, (kb + 1) * 2 * CHUNK)
            dk_ref[rows, :] = jnp.where(lane < C_HDIM, dka_acc[ra, 0:LANES], dka_acc[rb, 0:LANES]).astype(BF16)
            dv_ref[rows, :] = jnp.where(lane < C_HDIM, dv_acc[ra, :], dv_acc[rb, :]).astype(BF16)
            dck_ref[0, rows, :] = (jnp.where(lane == 2 * p, dka_acc[ra, LANES:], 0.0)
                                   + jnp.where(lane == 2 * p + 1, dka_acc[rb, LANES:], 0.0))

        @pl.when(jg == ng - 1)
        def _():
            for c in range(nq):
                dq_ref[c * bq:(c + 1) * bq, :] = (dqt_acc[0:CHUNK, c * bq:(c + 1) * bq].T * (C_HDIM ** -0.5)).astype(BF16)
            dcq_ref[0] = dqt_acc[CHUNK:rows_t, :]

    per_pair = lambda r, c: pl.BlockSpec((1, r, c), lambda p, j: (p, 0, 0))
    by_rows = pl.BlockSpec((1, kg * 2 * CHUNK, 2 * CHUNK), lambda p, j: (p, j, 0))
    by_cols = pl.BlockSpec((1, rows_t, kg * 2 * CHUNK), lambda p, j: (p, 0, j))
    return _call_carrying(
        carried, body, (ka, va, kt, qt, dot_t, qa, dob, lse),
        name="fox_bwd",
        grid=(C_PAIRS, ng),
        in_specs=[by_rows, by_rows, by_cols, per_pair(2 * CHUNK, seq), per_pair(2 * CHUNK, seq),
                  per_pair(seq, 2 * CHUNK), pl.BlockSpec((seq, LANES), lambda p, j: (0, p)), per_pair(2, seq)],
        out_specs=[pl.BlockSpec((seq, LANES), lambda p, j: (0, p)),
                   pl.BlockSpec((kg * CHUNK, LANES), lambda p, j: (j, p)),
                   pl.BlockSpec((kg * CHUNK, LANES), lambda p, j: (j, p)),
                   pl.BlockSpec((1, kg * CHUNK, LANES), lambda p, j: (p, j, 0)),
                   per_pair(C_TAIL, seq)],
        out_shape=[jax.ShapeDtypeStruct((seq, C_WIDTH), BF16)] * 3
        + [jax.ShapeDtypeStruct((C_PAIRS, seq, LANES), F32), jax.ShapeDtypeStruct((C_PAIRS, C_TAIL, seq), F32)],
        scratch_shapes=[pltpu.VMEM((rows_t, seq), F32), pltpu.VMEM((kg * 2 * CHUNK, LANES), F32),
                        pltpu.VMEM((kg * 2 * CHUNK, 2 * CHUNK), F32)],
    )


def fox_post(dcq, dck, proj, bf_row, dproj):
    seq = proj.shape[0]
    rows = min(seq, 512)
    nc = seq // rows
    triu = jnp.asarray(np.triu(np.ones((CHUNK, CHUNK), np.float32)), BF16)

    def body(dq_ref, dk_ref, fl_ref, bf_ref, u_ref, _, dfl_ref, dbf_ref, carry_ref):
        @pl.when(pl.program_id(0) == 0)
        def _():
            carry_ref[...] = jnp.zeros_like(carry_ref)
            dbf_ref[...] = jnp.zeros_like(dbf_ref)

        for j in reversed(range(rows // CHUNK)):
            at = slice(j * CHUNK, (j + 1) * CHUNK)
            heads = (dq_ref[0, :, at] + dq_ref[1, :, at]) + (dq_ref[2, :, at] + dq_ref[3, :, at])
            dc = jnp.concatenate([heads, jnp.zeros((CHUNK - C_TAIL, CHUNK), F32)], axis=0).T
            dc = dc - ((dk_ref[0, at] + dk_ref[1, at]) + (dk_ref[2, at] + dk_ref[3, at]))
            g = _exact_times(u_ref[...], dc, 3) + carry_ref[...]
            carry_ref[...] += jnp.sum(dc, axis=0, keepdims=True)
            dfl = g * jax.nn.sigmoid(-(fl_ref[at, :LANES] + bf_ref[...]))
            dbf_ref[...] += jnp.sum(dfl, axis=0, keepdims=True)
            dfl_ref[at, :] = jnp.concatenate([dfl, jnp.zeros_like(dfl)], axis=1).astype(BF16)

    rev = lambda n: nc - 1 - n
    return pl.pallas_call(
        body,
        name="fox_post",
        grid=(nc,),
        in_specs=[
            pl.BlockSpec((C_PAIRS, C_TAIL, rows), lambda n: (0, 0, rev(n))),
            pl.BlockSpec((C_PAIRS, rows, LANES), lambda n: (0, rev(n), 0)),
            pl.BlockSpec((rows, 256), lambda n: (rev(n), 3)),
            _full((1, LANES)), _full((CHUNK, CHUNK)), _ANY,
        ],
        out_specs=[pl.BlockSpec((rows, 256), lambda n: (rev(n), 3)), _full((1, LANES))],
        out_shape=[jax.ShapeDtypeStruct(dproj.shape, BF16), jax.ShapeDtypeStruct((1, LANES), F32)],
        input_output_aliases={5: 0},
        scratch_shapes=[pltpu.VMEM((1, LANES), F32)],
        compiler_params=_cparams("arbitrary"),
    )(dcq, dck, proj, bf_row, triu, dproj)


N_DEV = 8
MESH = pl.DeviceIdType.MESH
_ANY = pl.BlockSpec(memory_space=pl.ANY)


def _mesh_pos():
    return lax.axis_index("x"), lax.axis_index("y"), lax.axis_index("c")


def _dev_index(px, py, pc):
    return 4 * px + 2 * py + pc


def _row_pieces(ref, rows):
    return [ref.at[idx + (pl.ds(r, rows),)] for idx in np.ndindex(*ref.shape[:-2]) for r in range(0, ref.shape[-2], rows)]


class _Transfer:
    def __init__(self, src, dst, rows, send_sem, recv_sem, to):
        self.src, self.dst, self.rows, self.sems, self.to = src, dst, rows, (send_sem, recv_sem), to

    def _copy(self, src, dst):
        return pltpu.make_async_remote_copy(src_ref=src, dst_ref=dst, send_sem=self.sems[0], recv_sem=self.sems[1],
                                            device_id=self.to, device_id_type=MESH)

    def start(self):
        for s, d in zip(_row_pieces(self.src, self.rows), _row_pieces(self.dst, self.rows), strict=True):
            self._copy(s, d).start()

    def wait_send(self):
        self._copy(self.src, self.dst).wait_send()

    def wait_recv(self):
        self._copy(self.src, self.dst).wait_recv()


def _exchange_call(ex, name):
    n_in, n_out = len(ex.inputs), len(ex.out_shape)

    def body(*refs):
        parts = refs[:n_in], refs[n_in:n_in + n_out], refs[n_in + n_out:]
        ex.start(*parts)
        getattr(ex, "relay", lambda *_: None)(*parts)
        ex.finish(*parts)

    return pl.pallas_call(body, name=name, in_specs=[_ANY] * n_in, out_specs=[_ANY] * n_out, out_shape=ex.out_shape,
                          scratch_shapes=ex.scratch, input_output_aliases=getattr(ex, "aliases", {}))(*ex.inputs)


def _carried_refs(refs, n_in, n_out, ex):
    k_in, k_out, k_sem = (len(ex.inputs), len(ex.out_shape), len(ex.scratch)) if ex else (0, 0, 0)
    a, b, c = n_in + k_in, n_in + k_in + n_out, n_in + k_in + n_out + k_out
    own = refs[:n_in] + refs[a:b] + refs[c:len(refs) - k_sem]
    return own, (refs[n_in:a], refs[b:c], refs[len(refs) - k_sem:])


class AllGatherWeights:
    def __init__(self, blocks):
        n = len(blocks)
        self.inputs = tuple(blocks)
        self.out_shape = [jax.ShapeDtypeStruct((N_DEV,) + b.shape, b.dtype) for b in blocks]
        self.scratch = ([pltpu.SemaphoreType.DMA((n, 7)), pltpu.SemaphoreType.DMA((n, 7)), pltpu.SemaphoreType.DMA((n, 2))]
                        + [pltpu.VMEM(b.shape, b.dtype) for b in blocks])

    def _plan(self, ins, outs, scratch):
        send_sems, recv_sems, local_sems, *staged = scratch
        x, y, c = _mesh_pos()
        me, sibling = (x, y, c), (x, y, 1 - c)
        chips = [(1 - x, y), (x, 1 - y), (1 - x, 1 - y)]
        every = range(len(ins))

        def copy(a, k, block, to, own=False):
            slot = outs[a].at[_dev_index(*block)]
            return _Transfer(ins[a] if own else slot, slot, ins[a].shape[-2], send_sems.at[a, k], recv_sems.at[a, k], to)

        mine = [(pltpu.make_async_copy(ins[a], staged[a], local_sems.at[a, 0]),
                 pltpu.make_async_copy(staged[a], outs[a].at[_dev_index(*me)], local_sems.at[a, 1])) for a in every]
        first = [copy(a, 1 + j, me, (*chip, c), own=True) for j, chip in enumerate(chips) for a in every]
        first += [copy(a, 0, me, sibling, own=True) for a in every]
        passed = [[copy(a, 4 + j, (*chip, c), sibling) for a in every] for j, chip in enumerate(chips)]
        return me, sibling, chips, c, every, copy, mine, first, passed

    def start(self, ins, outs, scratch):
        *_, mine, first, _ = self._plan(ins, outs, scratch)
        for to_vmem, _ in mine:
            to_vmem.start()
        for cp in first:
            cp.start()

    def relay(self, ins, outs, scratch):
        me, sibling, chips, c, every, copy, mine, first, passed = self._plan(ins, outs, scratch)
        for to_vmem, to_slot in mine:
            to_vmem.wait()
            to_slot.start()
        for j, chip in enumerate(chips):
            for a in every:
                copy(a, 1 + j, (*chip, c), me).wait_recv()
            for cp in passed[j]:
                cp.start()

    def finish(self, ins, outs, scratch):
        me, sibling, chips, c, every, copy, mine, first, passed = self._plan(ins, outs, scratch)
        for a in every:
            copy(a, 0, sibling, me).wait_recv()
        for j, chip in enumerate(chips):
            for a in every:
                copy(a, 4 + j, (*chip, 1 - c), me).wait_recv()
        for cp in first + [cp for group in passed for cp in group]:
            cp.wait_send()
        for _, to_slot in mine:
            to_slot.wait()


N_CHIP = 4


class PairExchange:
    def __init__(self, by_core, whole=()):
        self.inputs = tuple(by_core) + tuple(whole)
        self.n_by_core = len(by_core)
        self.out_shape = ([jax.ShapeDtypeStruct(a.shape[1:], a.dtype) for a in by_core]
                          + [jax.ShapeDtypeStruct(a.shape, a.dtype) for a in whole])
        n = len(self.inputs)
        self.scratch = [pltpu.SemaphoreType.DMA((n,)), pltpu.SemaphoreType.DMA((n,))]

    def _copies(self, ins, outs, sems):
        x, y, c = _mesh_pos()
        srcs = [r.at[1 - c] if a < self.n_by_core else r for a, r in enumerate(ins)]
        return [_Transfer(srcs[a], outs[a], outs[a].shape[-2], sems[0].at[a], sems[1].at[a], (x, y, 1 - c))
                for a in range(len(ins))]

    def start(self, ins, outs, sems):
        for cp in self._copies(ins, outs, sems):
            cp.start()

    def finish(self, ins, outs, sems):
        copies = self._copies(ins, outs, sems)
        for cp in copies:
            cp.wait_recv()
        for cp in copies:
            cp.wait_send()


def pair_sum(own, other, dtype, rows, name, core, layer, depth, stacked=None):
    n, n_r, n_c = other.shape
    by_core = own.ndim == 4
    own = own if by_core else own[None]

    def body(core_ref, a_ref, b_ref, *refs):
        refs[-1][0, 0] = (a_ref[0, 0] + b_ref[0].astype(F32)).astype(dtype)

    carried = () if stacked is None else (stacked,)
    grid_spec = pltpu.PrefetchScalarGridSpec(
        num_scalar_prefetch=1,
        grid=(n, n_r // rows),
        in_specs=[pl.BlockSpec((1, 1, rows, n_c), lambda i, r, s: (s[0] if by_core else 0, i, r, 0)),
                  pl.BlockSpec((1, rows, n_c), lambda i, r, s: (i, r, 0))] + [_ANY] * len(carried),
        out_specs=pl.BlockSpec((1, 1, rows, n_c), lambda i, r, s: (i, layer, r, 0)),
    )
    return pl.pallas_call(
        body,
        name=name,
        grid_spec=grid_spec,
        out_shape=jax.ShapeDtypeStruct((n, depth, n_r, n_c), dtype),
        input_output_aliases={3: 0} if carried else {},
        compiler_params=_cparams("parallel", "parallel"),
    )(core, own, other, *carried)


def small_sum(a, b, name):
    def body(a_ref, b_ref, o_ref):
        o_ref[...] = a_ref[...] + b_ref[...]

    return pl.pallas_call(body, name=name, out_shape=jax.ShapeDtypeStruct(a.shape, a.dtype))(a, b)


class ChipExchange:
    def __init__(self, by_chip=(), layers=(), gathered=(), stacked=()):
        stacked = tuple(stacked) or (None,) * len(by_chip)
        kept = [s for s in stacked if s is not None]
        self.inputs = tuple(by_chip) + tuple(gathered) + tuple(kept)
        self.n_by_chip, self.n_gathered = len(by_chip), len(gathered)
        self.items = [(a, l) for a in range(len(by_chip)) for l in layers[a]] + [(self.n_by_chip + g, None) for g in range(len(gathered))]
        self.out_shape = ([jax.ShapeDtypeStruct((N_CHIP - 1,) + a.shape[1:], a.dtype) for a in by_chip]
                          + [jax.ShapeDtypeStruct((N_CHIP,) + a.shape, a.dtype) for a in gathered])
        at = iter(range(self.n_by_chip + self.n_gathered, len(self.inputs)))
        self.aliases = {next(at): a for a, s in enumerate(stacked) if s is not None}
        n = len(self.items)
        self.scratch = [pltpu.SemaphoreType.DMA((n, 3)), pltpu.SemaphoreType.DMA((n, 3)),
                        pltpu.SemaphoreType.DMA((max(self.n_gathered, 1),))]

    def _plan(self, ins, outs, sems):
        x, y, c = _mesh_pos()
        chip = 2 * x + y
        n = len(self.items)

        def copy(i, k, sending):
            a, layer = self.items[i]
            px, py = x ^ ((k >> 1) & 1), y ^ (k & 1)
            if layer is not None:
                src, dst = ins[a].at[2 * px + py, layer], outs[a].at[k - 1, layer]
            else:
                src, dst = ins[a], outs[a].at[chip if sending else 2 * px + py]
            return _Transfer(src, dst, dst.shape[-2], sems[0].at[i, k - 1], sems[1].at[i, k - 1], (px, py, c))

        local = [pltpu.make_async_copy(ins[a], outs[a].at[chip], sems[2].at[a - self.n_by_chip])
                 for a in range(self.n_by_chip, self.n_by_chip + self.n_gathered)]
        return n, copy, local

    def start(self, ins, outs, sems):
        n, copy, local = self._plan(ins, outs, sems)
        for cp in local:
            cp.start()
        for k in range(1, N_CHIP):
            for a in range(n):
                copy(a, k, True).start()

    def finish(self, ins, outs, sems):
        n, copy, local = self._plan(ins, outs, sems)
        for k in range(1, N_CHIP):
            for a in range(n):
                copy(a, k, False).wait_recv()
        for k in range(1, N_CHIP):
            for a in range(n):
                copy(a, k, True).wait_send()
        for cp in local:
            cp.wait()


ADAM_LR = 0.001
ADAM_B1 = 0.9
ADAM_B2 = 0.999
ADAM_EPS = 1e-08
ADAM_WD = 0.01
ADAM_STEP = 10


def adam_reduce(parts, w, m, v, rows, name, own=None, chip=None):
    n_l, n_r, n_c = w.shape
    n_parts = parts.shape[0]

    def body(*refs):
        p_ref, w_ref, m_ref, v_ref, g_ref, d_ref, m2_ref, v2_ref = refs[-8:]
        g = p_ref[0, 0].astype(F32)
        if own is not None:
            g = refs[-9][...].reshape(rows, n_c).astype(F32) + g
        for d in range(1, n_parts):
            g = g + p_ref[d, 0].astype(F32)
        m2 = ADAM_B1 * m_ref[0] + (1.0 - ADAM_B1) * g
        v2 = ADAM_B2 * v_ref[0] + (1.0 - ADAM_B2) * (g * g)
        m_hat = m2 / (1.0 - ADAM_B1 ** ADAM_STEP)
        v_hat = v2 / (1.0 - ADAM_B2 ** ADAM_STEP)
        g_ref[0] = g
        d_ref[0] = -ADAM_LR * (m_hat / (jnp.sqrt(v_hat) + ADAM_EPS) + ADAM_WD * w_ref[0])
        m2_ref[0] = m2
        v2_ref[0] = v2

    blk = lambda: pl.BlockSpec((1, rows, n_c), lambda l, r, *_: (l, r, 0))
    in_specs = [pl.BlockSpec((n_parts, 1, rows, n_c), lambda l, r, *_: (0, l, r, 0)), blk(), blk(), blk()]
    args = (parts, w, m, v)
    if own is not None:
        in_specs = [pl.BlockSpec((1, 1, rows, n_c), lambda l, r, s: (s[0], l, r, 0))] + in_specs
        args = (chip, own) + args
    grid_spec = pltpu.PrefetchScalarGridSpec(
        num_scalar_prefetch=0 if own is None else 1, grid=(n_l, n_r // rows), in_specs=in_specs,
        out_specs=[blk(), blk(), blk(), blk()])
    return pl.pallas_call(
        body,
        name=name,
        grid_spec=grid_spec,
        out_shape=[jax.ShapeDtypeStruct(w.shape, F32)] * 4,
        compiler_params=_cparams("parallel", "parallel"),
    )(*args)


def adam_reduce_columns(parts, w, m, v, name, own, chip):
    n_l, n_r, n_c = w.shape
    n_parts = parts.shape[0]
    view = lambda a: jnp.transpose(a, (2, 0, 1))

    def body(_, own_ref, p_ref, w_ref, m_ref, v_ref, g_ref, d_ref, m2_ref, v2_ref):
        for l in range(n_l):
            g = own_ref[0, l].astype(F32) + p_ref[0, l].astype(F32)
            for d in range(1, n_parts):
                g = g + p_ref[d, l].astype(F32)
            g = g.T
            w_l, m_l, v_l = w_ref[:, l, :], m_ref[:, l, :], v_ref[:, l, :]
            m2 = ADAM_B1 * m_l + (1.0 - ADAM_B1) * g
            v2 = ADAM_B2 * v_l + (1.0 - ADAM_B2) * (g * g)
            m_hat = m2 / (1.0 - ADAM_B1 ** ADAM_STEP)
            v_hat = v2 / (1.0 - ADAM_B2 ** ADAM_STEP)
            g_ref[:, l, :] = g
            d_ref[:, l, :] = -ADAM_LR * (m_hat / (jnp.sqrt(v_hat) + ADAM_EPS) + ADAM_WD * w_l)
            m2_ref[:, l, :] = m2
            v2_ref[:, l, :] = v2

    blk = lambda: pl.BlockSpec((LANES, n_l, n_r), lambda c, s: (c, 0, 0))
    grid_spec = pltpu.PrefetchScalarGridSpec(
        num_scalar_prefetch=1, grid=(pl.cdiv(n_c, LANES),),
        in_specs=[pl.BlockSpec((1, n_l, n_r, LANES), lambda c, s: (s[0], 0, 0, c)),
                  pl.BlockSpec((n_parts, n_l, n_r, LANES), lambda c, s: (0, 0, 0, c)), blk(), blk(), blk()],
        out_specs=[blk(), blk(), blk(), blk()])
    outs = pl.pallas_call(
        body,
        name=name,
        grid_spec=grid_spec,
        out_shape=[jax.ShapeDtypeStruct((n_c, n_l, n_r), F32)] * 4,
        compiler_params=_cparams("parallel"),
    )(chip, own, parts, view(w), view(m), view(v))
    return [jnp.transpose(o, (1, 2, 0)) for o in outs]


_SMALL = (("norm_g", (2, 1024)), ("gmlp_ln_g", (2, 4, 64)), ("gmlp_ln_b", (2, 4, 64)),
          ("gmlp_b_s", (2, 4, 128)), ("hgrn_lb", (2, 256)), ("hgrn_onorm_g", (2, 64)), ("fox_b_f", (2, 8)),
          ("final_norm_g", (1024,)), ("loss", ()))


def _padded(n):
    return -(-n // LANES) * LANES


_SMALL_ROWS = -(-sum(_padded(int(np.prod(s))) for _, s in _SMALL) // LANES // 8) * 8


def _pack_small(vals):
    flat = []
    for (name, shape), a in zip(_SMALL, vals, strict=True):
        n = int(np.prod(shape))
        flat.append(jnp.pad(a.reshape(n).astype(F32), (0, _padded(n) - n)))
    flat = jnp.concatenate(flat)
    return jnp.pad(flat, (0, _SMALL_ROWS * LANES - flat.shape[0])).reshape(_SMALL_ROWS, LANES)


def _unpack_small(slab):
    flat, out, at = slab.reshape(-1), {}, 0
    for name, shape in _SMALL:
        n = int(np.prod(shape))
        out[name] = flat[at:at + n].reshape(shape)
        at += _padded(n)
    return out


def sum_parts(parts, name):
    def body(p_ref, o_ref):
        g = p_ref[0]
        for d in range(1, parts.shape[0]):
            g = g + p_ref[d]
        o_ref[...] = g

    return pl.pallas_call(body, name=name, out_shape=jax.ShapeDtypeStruct(parts.shape[1:], F32))(parts)


def adam_small(gs, ws, ms, vs):
    n = len(gs)

    def body(*refs):
        for k in range(n):
            g, w, m, v = (refs[j * n + k][...] for j in range(4))
            m2 = ADAM_B1 * m + (1.0 - ADAM_B1) * g
            v2 = ADAM_B2 * v + (1.0 - ADAM_B2) * (g * g)
            m_hat = m2 / (1.0 - ADAM_B1 ** ADAM_STEP)
            v_hat = v2 / (1.0 - ADAM_B2 ** ADAM_STEP)
            refs[4 * n + k][...] = -ADAM_LR * (m_hat / (jnp.sqrt(v_hat) + ADAM_EPS) + ADAM_WD * w)
            refs[5 * n + k][...] = m2
            refs[6 * n + k][...] = v2

    outs = pl.pallas_call(body, name="adam_small",
                          out_shape=[jax.ShapeDtypeStruct(w.shape, F32) for _ in range(3) for w in ws])(*gs, *ws, *ms, *vs)
    return outs[:n], outs[n:2 * n], outs[2 * n:]


def kernel(x, norm_g, w_in, w_out, gmlp_ln_g, gmlp_ln_b, gmlp_w_s, gmlp_b_s, hgrn_lb, hgrn_onorm_g, fox_b_f, final_norm_g, loss_target, m_norm_g, m_w_in, m_w_out, m_gmlp_ln_g, m_gmlp_ln_b, m_gmlp_w_s, m_gmlp_b_s, m_hgrn_lb, m_hgrn_onorm_g, m_fox_b_f, m_final_norm_g, v_norm_g, v_w_in, v_w_out, v_gmlp_ln_g, v_gmlp_ln_b, v_gmlp_w_s, v_gmlp_b_s, v_hgrn_lb, v_hgrn_onorm_g, v_fox_b_f, v_final_norm_g):
    depth = w_in.shape[0]
    seq = x.shape[1]
    assert w_in.shape[2] * N_DEV == N_IN
    xs, tgt = x[0], loss_target[0]

    wi_blk, wo_blk = w_in.astype(BF16), w_out.astype(BF16)
    (wi_all,) = _exchange_call(AllGatherWeights([wi_blk[0]]), "allgather_weights_0")

    ln_g = gmlp_ln_g.reshape(depth, 1, A_WIDTH)
    ln_b = gmlp_ln_b.reshape(depth, 1, A_WIDTH)
    bs_t = jnp.pad(jnp.transpose(gmlp_b_s, (0, 2, 1)), ((0, 0), (0, 0), (0, LANES - A_GROUPS)))
    lb0, lb1 = hgrn_lb[0:1], hgrn_lb[1:2]
    onorm = jnp.tile(hgrn_onorm_g, (1, B_HEADS)).reshape(depth, 1, B_WIDTH)
    bf_row = jnp.pad(fox_b_f, ((0, 0), (0, LANES - C_HEADS))).reshape(depth, 1, LANES)

    core = lax.axis_index("c").astype(jnp.int32).reshape(1)
    chip = (2 * lax.axis_index("x") + lax.axis_index("y")).astype(jnp.int32).reshape(1)

    saved = []
    xc = xs
    for l in range(depth):
        wi_int = assemble_w_in(wi_all[:, None])
        proj, qkv, h = inproj(xc, norm_g[l:l + 1], wi_int, 0)
        ya = gmlp_fwd(proj, ln_g[l], ln_b[l], gmlp_w_s[l], bs_t[l])
        yb, states = hgrn_fwd(proj, lb0, lb1, onorm[l], l)
        ka, va, vt, kt, qt, qa = fox_prep(proj, qkv, bf_row[l])
        ride = ([wo_blk] if l == 0 else []) + ([wi_blk[l + 1]] if l + 1 < depth else [])
        o, lse, *gathered = fox_fwd(qt, ka, vt, AllGatherWeights(ride) if ride else None)
        if l == 0:
            wo_all = gathered.pop(0)
        if gathered:
            (wi_all,) = gathered
        x_in = xc
        if l + 1 < depth:
            xc, yfull = outproj(x_in, ya, yb, o, proj, wo_all, l)
        else:
            dx, yfull, d_final_g, loss_tile = outproj(x_in, ya, yb, o, proj, wo_all, l, (final_norm_g[None], tgt))
        saved.append((x_in, proj, h, states, ka, va, kt, qt, qa, o, lse, yfull, wi_int))

    n_shard = w_in.shape[2]
    g_norm = [None] * depth
    g_ln_g, g_ln_b, g_ws, g_bs, g_on, g_bf = ([None] * depth for _ in range(6))
    g_lb0, g_lb1 = jnp.zeros_like(lb0), jnp.zeros_like(lb1)
    swi = swo = rwi = rwo = None
    for l in reversed(range(depth)):
        x_in, proj, h, states, ka, va, kt, qt, qa, o, lse, yfull, wi_int = saved[l]
        dy, gwo = outproj_bwd(dx, yfull, wo_all, l)
        dproj, g_ln_g[l], g_ln_b[l], g_ws[l], dbs_t = gmlp_bwd(proj, dy, ln_g[l], ln_b[l], gmlp_w_s[l], bs_t[l])
        g_bs[l] = dbs_t[:, :A_GROUPS].T
        if l > 0:
            (qwo,) = _exchange_call(PairExchange([gwo]), f"pair_exchange_w_out_{l}")
        else:
            gws = jnp.stack(g_ws).reshape(-1, LANES)
            qwo, qws = _exchange_call(PairExchange([gwo], [gws]), f"pair_exchange_w_out_{l}")
            sws = small_sum(gws, qws, "pair_sum_w_s")
        swo = pair_sum(gwo, qwo, BF16, gwo.shape[2], "pair_sum_w_out", core, l, depth, swo)
        dproj, d0, d1, don = hgrn_bwd(proj, states, dy, lb0, lb1, onorm[l], l, dproj)
        g_lb0, g_lb1 = g_lb0 + d0, g_lb1 + d1
        g_on[l] = don.reshape(B_HEADS, B_KDIM).sum(0)
        dob, dproj, dot_t = fox_bwd_prep(dy, o, proj, dproj)
        top = l == depth - 1
        ride = ChipExchange([swo] if top else [swi, swo], [(l,)] if top else [(l + 1,), (l,)],
                            [sws] if l == 0 else [], [rwo] if top else [rwi, rwo])
        outs = fox_bwd(ka, va, kt, qt, dot_t, qa, dob, lse, ride)
        dqkv, (dck, dcq), got = outs[:3], outs[3:5], list(outs[5:])
        if not top:
            rwi = got.pop(0)
        rwo = got.pop(0)
        if l == 0:
            (rws,) = got
        dproj, dbf = fox_post(dcq, dck, proj, bf_row[l], dproj)
        g_bf[l] = dbf[0, :C_HEADS]
        gwi, for_sibling = split_w_in_grad(inproj_bwd_w(h, dproj, dqkv), n_shard, core)
        (qwi,) = _exchange_call(PairExchange([], [for_sibling]), f"pair_exchange_w_in_{l}")
        swi = pair_sum(gwi, qwi, BF16, 256, "pair_sum_w_in", core, l, depth, swi)
        ride = ChipExchange([swi], [(l,)], stacked=[rwi]) if l == 0 else None
        outs = inproj_bwd_x(dproj, dqkv, wi_int, x_in, norm_g[l:l + 1], dx, 0, ride)
        dx, g_norm[l] = outs[:2]
        if ride is not None:
            (rwi,) = outs[2:]

    gsm = _pack_small([
        jnp.concatenate(g_norm), jnp.stack(g_ln_g), jnp.stack(g_ln_b), jnp.stack(g_bs),
        jnp.concatenate([g_lb0, g_lb1]), jnp.stack(g_on), jnp.stack(g_bf), d_final_g, loss_tile[0, 0]])
    (qsm,) = _exchange_call(PairExchange([], [gsm]), "pair_exchange_small")
    ssm = small_sum(gsm, qsm, "pair_sum_small")
    (rsm,) = _exchange_call(ChipExchange(gathered=[ssm]), "chip_exchange_small")

    small_w = (norm_g, gmlp_ln_g, gmlp_ln_b, gmlp_b_s, hgrn_lb, hgrn_onorm_g, fox_b_f, final_norm_g)
    small_m = (m_norm_g, m_gmlp_ln_g, m_gmlp_ln_b, m_gmlp_b_s, m_hgrn_lb, m_hgrn_onorm_g, m_fox_b_f, m_final_norm_g)
    small_v = (v_norm_g, v_gmlp_ln_g, v_gmlp_ln_b, v_gmlp_b_s, v_hgrn_lb, v_hgrn_onorm_g, v_fox_b_f, v_final_norm_g)
    res_wi = adam_reduce_columns(rwi, w_in, m_w_in, v_w_in, "adam_w_in", swi, chip)
    res_wo = adam_reduce(rwo, w_out, m_w_out, v_w_out, w_out.shape[1], "adam_w_out", own=swo, chip=chip)
    grads = _unpack_small(sum_parts(rsm, "sum_small"))
    names = [name for name, _ in _SMALL if name != "loss"]
    rows = lambda a: a.reshape(1, -1) if a.ndim == 1 else a
    res_sm = adam_small([rows(grads[k]) for k in names], *([rows(a) for a in wmv] for wmv in (small_w, small_m, small_v)))
    res_sm = [grads] + [{k: a.reshape(grads[k].shape) for k, a in zip(names, r, strict=True)} for r in res_sm]
    as_rows = lambda a: a.reshape(1, -1, LANES)
    res_ws = adam_reduce(rws[:, None], as_rows(gmlp_w_s), as_rows(m_gmlp_w_s), as_rows(v_gmlp_w_s), rws.shape[1], "adam_w_s")
    for s, r in zip(res_sm, res_ws, strict=True):
        s["gmlp_w_s"] = r.reshape(gmlp_w_s.shape)

    def group(i):
        s = res_sm[i]
        return [s["norm_g"], res_wi[i], res_wo[i], s["gmlp_ln_g"], s["gmlp_ln_b"], s["gmlp_w_s"], s["gmlp_b_s"],
                s["hgrn_lb"], s["hgrn_onorm_g"], s["fox_b_f"], s["final_norm_g"]]

    return (res_sm[0]["loss"], dx[None], *group(0), *group(1), *group(2), *group(3))
```

```python
import functools

import jax
import jax.numpy as jnp
import numpy as np
from jax import lax
from jax.experimental import pallas as pl
from jax.experimental.pallas import tpu as pltpu

F32 = jnp.float32
BF16 = jnp.bfloat16

NORM_EPS = 1e-6
F_FLOOR = 1e-30
CHUNK = 128
LANES = 128
VMEM_LIMIT = 56 * 1024 * 1024


def _cparams(*sem):
    return pltpu.CompilerParams(dimension_semantics=sem, vmem_limit_bytes=VMEM_LIMIT)


def _dot(a, b, dims=(((1,), (0,)), ((), ())), precision=None):
    return lax.dot_general(a, b, dims, precision=precision, preferred_element_type=F32)


_NT = (((1,), (1,)), ((), ()))
_TN = (((0,), (0,)), ((), ()))


def _bf16_pieces(x, n):
    out, r = [], x
    for i in range(n):
        out.append(r.astype(BF16))
        if i + 1 < n:
            r = r - out[-1].astype(F32)
    return out


@functools.partial(jax.custom_vjp, nondiff_argnums=(2,))
def _times_exact(x, e, n):
    return functools.reduce(jnp.add, [_dot(p, e) for p in _bf16_pieces(x, n)])


def _times_exact_fwd(x, e, n):
    return _times_exact(x, e, n), e


def _times_exact_bwd(n, e, g):
    dx = functools.reduce(jnp.add, [lax.dot_general(p, e, _NT, preferred_element_type=F32) for p in _bf16_pieces(g, n)])
    return dx, jnp.zeros_like(e)


_times_exact.defvjp(_times_exact_fwd, _times_exact_bwd)


@functools.partial(jax.custom_vjp, nondiff_argnums=(2,))
def _exact_times(e, x, n):
    return functools.reduce(jnp.add, [_dot(e, p) for p in _bf16_pieces(x, n)])


def _exact_times_fwd(e, x, n):
    return _exact_times(e, x, n), e


def _exact_times_bwd(n, e, g):
    dx = functools.reduce(jnp.add, [lax.dot_general(e, p, _TN, preferred_element_type=F32) for p in _bf16_pieces(g, n)])
    return jnp.zeros_like(e), dx


_exact_times.defvjp(_exact_times_fwd, _exact_times_bwd)


def _group_mean_matrix(width, group):
    idx = np.arange(width) // group
    return jnp.asarray((idx[:, None] == idx[None, :]).astype(np.float32) / group, BF16)


def _group_ones_matrix(width, group):
    idx = np.arange(width) // group
    return jnp.asarray((idx[:, None] == idx[None, :]).astype(np.float32), BF16)


A_WIDTH = 256
A_GROUPS = 4
A_GDIM = 64


A_ROWS = 512


def _gmlp_chunk(x3, ln_g, ln_b, w_s, bs_t, mean_m, gind):
    n = x3.shape[0] // CHUNK
    u = jax.nn.gelu(x3[:, :A_WIDTH])
    v = jax.nn.gelu(x3[:, A_WIDTH:2 * A_WIDTH])
    z = x3[:, 2 * A_WIDTH:]
    mu = _times_exact(v, mean_m, 2)
    d = v - mu
    var = _times_exact(d * d, mean_m, 2)
    vn = d * lax.rsqrt(var + NORM_EPS) * ln_g + ln_b
    vnb = vn.astype(BF16)
    wide = jnp.concatenate([vnb[i * CHUNK:(i + 1) * CHUNK] for i in range(n)], axis=1)
    row = lax.broadcasted_iota(jnp.int32, (CHUNK, CHUNK), 0)
    col = lax.broadcasted_iota(jnp.int32, (CHUNK, CHUNK), 1)
    causal = row >= col
    lane_g = lax.shift_right_logical(lax.broadcasted_iota(jnp.int32, (CHUNK, n * A_WIDTH), 1), 6) & (A_GROUPS - 1)
    bias = _times_exact(bs_t, gind, 3)
    mixed = jnp.concatenate([bias] * n, axis=1)
    for g in range(A_GROUPS):
        wc = jnp.where(causal, w_s[g], 0.0).astype(BF16)
        mixed = mixed + jnp.where(lane_g == g, _dot(wc, wide), 0.0)
    mixed = jnp.concatenate([mixed[:, i * A_WIDTH:(i + 1) * A_WIDTH] for i in range(n)], axis=0)
    return u * mixed * jax.nn.silu(z)


def _gmlp_consts():
    gind = np.zeros((LANES, A_WIDTH), np.float32)
    for g in range(A_GROUPS):
        gind[g, g * A_GDIM:(g + 1) * A_GDIM] = 1.0
    return _group_mean_matrix(A_WIDTH, A_GDIM), jnp.asarray(gind, BF16)


def _full(shape):
    return pl.BlockSpec(shape, lambda *_: (0,) * len(shape))


def gmlp_fwd(proj, ln_g, ln_b, w_s, bs_t):
    seq = proj.shape[0]
    rows = min(A_ROWS, seq)
    mean_m, gind = _gmlp_consts()

    def body(x_ref, g_ref, b_ref, w_ref, bs_ref, m_ref, gi_ref, y_ref):
        y = _gmlp_chunk(x_ref[...], g_ref[...], b_ref[...], w_ref[...], bs_ref[...], m_ref[...], gi_ref[...])
        y_ref[...] = y.astype(BF16)

    return pl.pallas_call(
        body,
        name="gmlp_fwd",
        grid=(seq // rows,),
        in_specs=[
            pl.BlockSpec((rows, 3 * A_WIDTH), lambda n: (n, 0)),
            _full((1, A_WIDTH)), _full((1, A_WIDTH)), _full((A_GROUPS, CHUNK, CHUNK)), _full((CHUNK, LANES)),
            _full((A_WIDTH, A_WIDTH)), _full((LANES, A_WIDTH)),
        ],
        out_specs=pl.BlockSpec((rows, A_WIDTH), lambda n: (n, 0)),
        out_shape=jax.ShapeDtypeStruct((seq, A_WIDTH), BF16),
        compiler_params=_cparams("parallel"),
    )(proj, ln_g, ln_b, w_s, bs_t, mean_m, gind)


def gmlp_bwd(proj, dy, ln_g, ln_b, w_s, bs_t):
    seq = proj.shape[0]
    rows = min(A_ROWS, seq)
    mean_m, gind = _gmlp_consts()

    def body(x_ref, dy_ref, g_ref, b_ref, w_ref, bs_ref, m_ref, gi_ref, dx_ref, dg_ref, db_ref, dw_ref, dbs_ref):
        fn = functools.partial(_gmlp_chunk, mean_m=m_ref[...], gind=gi_ref[...])
        _, vjp = jax.vjp(fn, x_ref[...], g_ref[...], b_ref[...], w_ref[...], bs_ref[...])
        dx, dg, db, dw, dbs = vjp(dy_ref[...])
        dx_ref[...] = dx.astype(BF16)

        @pl.when(pl.program_id(0) == 0)
        def _():
            dg_ref[...] = jnp.zeros_like(dg_ref)
            db_ref[...] = jnp.zeros_like(db_ref)
            dw_ref[...] = jnp.zeros_like(dw_ref)
            dbs_ref[...] = jnp.zeros_like(dbs_ref)

        dg_ref[...] += dg
        db_ref[...] += db
        dw_ref[...] += dw
        dbs_ref[...] += dbs

    return pl.pallas_call(
        body,
        name="gmlp_bwd",
        grid=(seq // rows,),
        in_specs=[
            pl.BlockSpec((rows, 3 * A_WIDTH), lambda n: (n, 0)),
            pl.BlockSpec((rows, A_WIDTH), lambda n: (n, 0)),
            _full((1, A_WIDTH)), _full((1, A_WIDTH)), _full((A_GROUPS, CHUNK, CHUNK)), _full((CHUNK, LANES)),
            _full((A_WIDTH, A_WIDTH)), _full((LANES, A_WIDTH)),
        ],
        out_specs=[
            pl.BlockSpec((rows, 3 * A_WIDTH), lambda n: (n, 0)),
            _full((1, A_WIDTH)), _full((1, A_WIDTH)), _full((A_GROUPS, CHUNK, CHUNK)), _full((CHUNK, LANES)),
        ],
        out_shape=[
            jax.ShapeDtypeStruct((seq, D_INT), BF16),
            jax.ShapeDtypeStruct((1, A_WIDTH), F32), jax.ShapeDtypeStruct((1, A_WIDTH), F32),
            jax.ShapeDtypeStruct((A_GROUPS, CHUNK, CHUNK), F32), jax.ShapeDtypeStruct((CHUNK, LANES), F32),
        ],
        compiler_params=_cparams("arbitrary"),
    )(proj, dy, ln_g, ln_b, w_s, bs_t, mean_m, gind)


B_WIDTH = 256
B_HEADS = 4
B_KDIM = 64
B_LEVELS = (64, 32, 16, 8, 4, 2, 1)


def _hgrn_consts():
    t = np.arange(CHUNK)
    u = t[None, :]
    mats = [np.tril(np.ones((CHUNK, CHUNK), np.float32))]
    for m in B_LEVELS:
        p = (t // (2 * m)) * (2 * m) + m - 1
        right = (t % (2 * m)) >= m
        sel = np.where(right[:, None], (u > p[:, None]) & (u <= t[:, None]), (u > t[:, None]) & (u <= p[:, None]))
        mats.append(sel.astype(np.float32))
    return jnp.asarray(np.concatenate(mats, 0), BF16), _group_ones_matrix(B_WIDTH, B_KDIM)


def _hgrn_lower_bound(lb0, lb1, layer):
    mx = jnp.maximum(lb0, lb1)
    e0 = jnp.exp(lb0 - mx)
    e1 = jnp.exp(lb1 - mx)
    p0 = e0 / (e0 + e1)
    p1 = e1 / (e0 + e1)
    cs = p0 if layer == 0 else p0 + p1
    return jnp.clip(cs - p0, 0.0, 1.0 - 1e-6)


def _hgrn_chunk(x4, st, lb0, lb1, onorm, layer, tstack, ones_bd):
    q_raw, fl, v, zg = (x4[:, i * B_WIDTH:(i + 1) * B_WIDTH] for i in range(4))
    lb = _hgrn_lower_bound(lb0, lb1, layer)
    q = jax.nn.silu(q_raw) * (B_KDIM ** -0.5)
    f = lb + (1.0 - lb) * jax.nn.sigmoid(fl)
    logf = jnp.log(jnp.maximum(f, F_FLOOR))
    k = (1.0 - lb) * jax.nn.sigmoid(-fl)
    b = _exact_times(tstack[:CHUNK], logf, 3)
    dall = jnp.concatenate([b, _exact_times(tstack[CHUNK:], logf, 2)], axis=0)
    b_last = jnp.sum(logf, axis=0, keepdims=True)
    vb = v.astype(BF16)

    lane_h = lax.shift_right_logical(lax.broadcasted_iota(jnp.int32, (CHUNK, B_WIDTH), 1), 6)
    row = lax.broadcasted_iota(jnp.int32, (CHUNK, B_WIDTH), 0)
    srow = lax.broadcasted_iota(jnp.int32, (B_HEADS * CHUNK, CHUNK), 0) & (CHUNK - 1)
    scol = lax.broadcasted_iota(jnp.int32, (B_HEADS * CHUNK, CHUNK), 1)

    def heads_on_rows(a):
        return jnp.concatenate([jnp.where(lane_h == h, a, 0.0) for h in range(B_HEADS)], axis=0)

    def heads_from_rows(r):
        out = jnp.where(lane_h == 0, r[:CHUNK], 0.0)
        for h in range(1, B_HEADS):
            out = out + jnp.where(lane_h == h, r[h * CHUNK:(h + 1) * CHUNK], 0.0)
        return out

    o = lax.dot_general((q * jnp.exp(b)).astype(BF16), st.astype(BF16), _NT, preferred_element_type=F32)
    scores = jnp.zeros((B_HEADS * CHUNK, CHUNK), F32)
    for li, m in enumerate(B_LEVELS):
        e = jnp.exp(dall[(li + 1) * CHUNK:(li + 2) * CHUNK])
        right = (row & (2 * m - 1)) >= m
        qt = jnp.where(right, q * e, 0.0)
        kt = jnp.where(right, 0.0, k * e)
        sc = lax.dot_general(heads_on_rows(qt).astype(BF16), kt.astype(BF16), _NT, preferred_element_type=F32)
        sh = int(np.log2(2 * m))
        same = lax.shift_right_logical(srow, sh) == lax.shift_right_logical(scol, sh)
        scores = scores + jnp.where(same, sc, 0.0)
    o = o + heads_from_rows(_dot(scores.astype(BF16), vb))
    o = o + _times_exact(q * k, ones_bd, 2) * v

    kv = lax.dot_general(vb, (k * jnp.exp(b_last - b)).astype(BF16), _TN, preferred_element_type=F32)
    st_new = st * jnp.exp(b_last) + jnp.where(ones_bd > 0.5, kv, 0.0)

    ms = _times_exact(o * o, ones_bd, 2) * (1.0 / B_KDIM)
    y = o * lax.rsqrt(ms + NORM_EPS) * onorm * jax.nn.silu(zg)
    return y, st_new


B_ROWS = 256


def _hgrn_rows(x4, st, lb0, lb1, onorm, layer, tstack, ones_bd):
    ys = []
    for i in range(x4.shape[0] // CHUNK):
        y, st = _hgrn_chunk(x4[i * CHUNK:(i + 1) * CHUNK], st, lb0, lb1, onorm, layer, tstack, ones_bd)
        ys.append(y)
    return jnp.concatenate(ys, axis=0), st


def hgrn_fwd(proj, lb0, lb1, onorm, layer):
    seq = proj.shape[0]
    rows = min(B_ROWS, seq)
    nc = seq // rows
    tstack, ones_bd = _hgrn_consts()

    def body(x_ref, lb0_ref, lb1_ref, on_ref, t_ref, e_ref, y_ref, st_out_ref, st_ref):
        @pl.when(pl.program_id(0) == 0)
        def _():
            st_ref[...] = jnp.zeros_like(st_ref)

        st = st_ref[...]
        st_out_ref[0] = st
        y, st_new = _hgrn_rows(x_ref[...], st, lb0_ref[...], lb1_ref[...], on_ref[...], layer, t_ref[...], e_ref[...])
        y_ref[...] = y.astype(BF16)
        st_ref[...] = st_new

    return pl.pallas_call(
        body,
        name=f"hgrn_fwd_{layer}",
        grid=(nc,),
        in_specs=[
            pl.BlockSpec((rows, 4 * B_WIDTH), lambda n: (n, 1)),
            _full((1, B_WIDTH)), _full((1, B_WIDTH)), _full((1, B_WIDTH)),
            _full(((len(B_LEVELS) + 1) * CHUNK, CHUNK)), _full((B_WIDTH, B_WIDTH)),
        ],
        out_specs=[
            pl.BlockSpec((rows, B_WIDTH), lambda n: (n, 0)),
            pl.BlockSpec((1, B_WIDTH, B_WIDTH), lambda n: (n, 0, 0)),
        ],
        out_shape=[jax.ShapeDtypeStruct((seq, B_WIDTH), BF16), jax.ShapeDtypeStruct((nc, B_WIDTH, B_WIDTH), F32)],
        scratch_shapes=[pltpu.VMEM((B_WIDTH, B_WIDTH), F32)],
        compiler_params=_cparams("arbitrary"),
    )(proj, lb0, lb1, onorm, tstack, ones_bd)


def hgrn_bwd(proj, states, dy, lb0, lb1, onorm, layer, dproj):
    seq = proj.shape[0]
    rows = min(B_ROWS, seq)
    nc = seq // rows
    tstack, ones_bd = _hgrn_consts()

    def body(x_ref, st_in_ref, dy_ref, lb0_ref, lb1_ref, on_ref, t_ref, e_ref, _, dx_ref, d0_ref, d1_ref, don_ref, dst_ref):
        @pl.when(pl.program_id(0) == 0)
        def _():
            dst_ref[...] = jnp.zeros_like(dst_ref)
            d0_ref[...] = jnp.zeros_like(d0_ref)
            d1_ref[...] = jnp.zeros_like(d1_ref)
            don_ref[...] = jnp.zeros_like(don_ref)

        fn = functools.partial(_hgrn_rows, layer=layer, tstack=t_ref[...], ones_bd=e_ref[...])
        _, vjp = jax.vjp(fn, x_ref[...], st_in_ref[0], lb0_ref[...], lb1_ref[...], on_ref[...])
        dx, dst, d0, d1, don = vjp((dy_ref[...], dst_ref[...]))
        dx_ref[...] = dx.astype(BF16)
        dst_ref[...] = dst
        d0_ref[...] += d0
        d1_ref[...] += d1
        don_ref[...] += don

    rev = lambda n: nc - 1 - n
    return pl.pallas_call(
        body,
        name=f"hgrn_bwd_{layer}",
        grid=(nc,),
        in_specs=[
            pl.BlockSpec((rows, 4 * B_WIDTH), lambda n: (rev(n), 1)),
            pl.BlockSpec((1, B_WIDTH, B_WIDTH), lambda n: (rev(n), 0, 0)),
            pl.BlockSpec((rows, B_WIDTH), lambda n: (rev(n), 1)),
            _full((1, B_WIDTH)), _full((1, B_WIDTH)), _full((1, B_WIDTH)),
            _full(((len(B_LEVELS) + 1) * CHUNK, CHUNK)), _full((B_WIDTH, B_WIDTH)), _ANY,
        ],
        out_specs=[
            pl.BlockSpec((rows, 4 * B_WIDTH), lambda n: (rev(n), 1)),
            _full((1, B_WIDTH)), _full((1, B_WIDTH)), _full((1, B_WIDTH)),
        ],
        out_shape=[jax.ShapeDtypeStruct(dproj.shape, BF16)] + [jax.ShapeDtypeStruct((1, B_WIDTH), F32)] * 3,
        input_output_aliases={8: 0},
        scratch_shapes=[pltpu.VMEM((B_WIDTH, B_WIDTH), F32)],
        compiler_params=_cparams("arbitrary"),
    )(proj, states, dy, lb0, lb1, onorm, tstack, ones_bd, dproj)


D_MODEL = 1024
D_INT = 4096


def _rms_stats(xf):
    r = lax.rsqrt(jnp.mean(xf * xf, axis=-1, keepdims=True) + NORM_EPS)
    return r, xf * r


def _rms_bwd(dy, g, r, xh):
    u = dy * g
    return r * (u - xh * jnp.mean(u * xh, axis=-1, keepdims=True))


C_QKV = (2048, 3584)
P_WIDTH = D_INT - (C_QKV[1] - C_QKV[0])
P_Z_BLOCK = C_QKV[0] // 512


def inproj(x, g, w, layer):
    seq = x.shape[0]
    tm = min(seq, 512)

    def body(x_ref, g_ref, w_ref, p_ref, qkv_ref, h_ref):
        _, xh = _rms_stats(x_ref[...])
        h = (xh * g_ref[...]).astype(BF16)
        h_ref[...] = h
        p_ref[:, :C_QKV[0]] = _dot(h, w_ref[0, :, :C_QKV[0]])
        qkv_ref[...] = _dot(h, w_ref[0, :, C_QKV[0]:C_QKV[1]]).astype(BF16)
        p_ref[:, C_QKV[0]:] = _dot(h, w_ref[0, :, C_QKV[1]:])

    rows = lambda n: pl.BlockSpec((tm, n), lambda i: (i, 0))
    return pl.pallas_call(
        body,
        name="inproj",
        grid=(seq // tm,),
        in_specs=[rows(D_MODEL), _full((1, D_MODEL)), pl.BlockSpec((1, D_MODEL, D_INT), lambda i: (layer, 0, 0))],
        out_specs=[rows(P_WIDTH), rows(C_QKV[1] - C_QKV[0]), rows(D_MODEL)],
        out_shape=[jax.ShapeDtypeStruct((seq, P_WIDTH), F32), jax.ShapeDtypeStruct((seq, C_QKV[1] - C_QKV[0]), BF16),
                   jax.ShapeDtypeStruct((seq, D_MODEL), BF16)],
        compiler_params=_cparams("parallel"),
    )(x, g, w)


def outproj(x, ya, yb, o, proj, wo, layer, head=None):
    seq = x.shape[0]
    tm = min(seq, 512)
    blk = wo.shape[2]

    def body(x_ref, ya_ref, yb_ref, o_ref, z_ref, w_ref, *refs):
        yc = (o_ref[...] * jax.nn.silu(z_ref[...])).astype(BF16)
        y = jnp.concatenate([ya_ref[...], yb_ref[...], yc], axis=1)
        w = jnp.concatenate([w_ref[d, 0] for d in range(N_DEV)], axis=0)
        xn = x_ref[...] + _dot(y, w)
        if head is None:
            xn_ref, y_ref = refs
            xn_ref[...] = xn
        else:
            g_ref, t_ref, dx_ref, y_ref, dg_ref, loss_ref = refs

            @pl.when(pl.program_id(0) == 0)
            def _():
                dg_ref[...] = jnp.zeros_like(dg_ref)
                loss_ref[...] = jnp.zeros_like(loss_ref)

            g = g_ref[...]
            r, xh = _rms_stats(xn)
            err = xh * g - t_ref[...]
            sq = jnp.sum(jnp.sum(err * err, axis=1, keepdims=True), axis=0, keepdims=True)
            loss_ref[...] += jnp.broadcast_to(sq * (0.5 / D_MODEL), loss_ref.shape)
            dout = err * (1.0 / D_MODEL)
            dg_ref[...] += jnp.sum(dout * xh, axis=0, keepdims=True)
            dx_ref[...] = _rms_bwd(dout, g, r, xh)
        y_ref[...] = y

    rows = lambda: pl.BlockSpec((tm, D_MODEL), lambda i: (i, 0))
    tail = (() if head is None else (_full((1, D_MODEL)), rows()),
            () if head is None else (_full((1, D_MODEL)), _full((8, LANES))),
            () if head is None else (jax.ShapeDtypeStruct((1, D_MODEL), F32), jax.ShapeDtypeStruct((8, LANES), F32)))
    return pl.pallas_call(
        body,
        name="outproj" if head is None else "outproj_loss",
        grid=(seq // tm,),
        in_specs=[
            rows(),
            pl.BlockSpec((tm, 256), lambda i: (i, 0)),
            pl.BlockSpec((tm, 256), lambda i: (i, 0)),
            pl.BlockSpec((tm, 512), lambda i: (i, 0)),
            pl.BlockSpec((tm, 512), lambda i: (i, P_Z_BLOCK)),
            pl.BlockSpec((N_DEV, 1, blk, D_MODEL), lambda i: (0, layer, 0, 0)),
            *tail[0],
        ],
        out_specs=[rows(), rows(), *tail[1]],
        out_shape=[jax.ShapeDtypeStruct((seq, D_MODEL), F32), jax.ShapeDtypeStruct((seq, D_MODEL), BF16), *tail[2]],
        compiler_params=_cparams("parallel" if head is None else "arbitrary"),
    )(x, ya, yb, o, proj, wo, *(head or ()))


def outproj_bwd(dx, y, wo, layer):
    seq = dx.shape[0]
    ts = min(seq, 512)
    blk = wo.shape[2]

    def body(dx_ref, y_ref, w_ref, dy_ref, dw_ref):
        @pl.when(pl.program_id(0) == 0)
        def _():
            dw_ref[...] = jnp.zeros_like(dw_ref)

        dxb = dx_ref[...].astype(BF16)
        w = jnp.concatenate([w_ref[d, 0] for d in range(N_DEV)], axis=0)
        dy_ref[...] = lax.dot_general(dxb, w, _NT, preferred_element_type=F32)
        dw = lax.dot_general(y_ref[...], dxb, _TN, preferred_element_type=F32)
        for d in range(N_DEV):
            dw_ref[d % 2, d // 2] += dw[d * blk:(d + 1) * blk]

    return pl.pallas_call(
        body,
        name="outproj_bwd",
        grid=(seq // ts,),
        in_specs=[
            pl.BlockSpec((ts, D_MODEL), lambda i: (i, 0)),
            pl.BlockSpec((ts, D_MODEL), lambda i: (i, 0)),
            pl.BlockSpec((N_DEV, 1, blk, D_MODEL), lambda i: (0, layer, 0, 0)),
        ],
        out_specs=[pl.BlockSpec((ts, D_MODEL), lambda i: (i, 0)),
                   pl.BlockSpec((2, N_CHIP, blk, D_MODEL), lambda i: (0, 0, 0, 0))],
        out_shape=[jax.ShapeDtypeStruct((seq, D_MODEL), F32), jax.ShapeDtypeStruct((2, N_CHIP, blk, D_MODEL), F32)],
        compiler_params=_cparams("arbitrary"),
    )(dx, y, wo)


def _dproj_parts(dp_ref, dqkv_refs, rows):
    lo, hi = C_QKV
    step = (hi - lo) // len(dqkv_refs)
    return ([(0, dp_ref.at[rows, 0:lo])] + [(lo + i * step, r.at[rows, :]) for i, r in enumerate(dqkv_refs)]
            + [(hi, dp_ref.at[rows, hi:D_INT])])


def inproj_bwd_x(dproj, dqkv, w, x, g, dx_in, layer, carried=None):
    seq = x.shape[0]
    tm = min(seq, 512)

    def body(dp_ref, dq_ref, dk_ref, dv_ref, w_ref, x_ref, g_ref, dxin_ref, dx_ref, dg_ref):
        @pl.when(pl.program_id(0) == 0)
        def _():
            dg_ref[...] = jnp.zeros_like(dg_ref)

        dh = None
        for at, part in _dproj_parts(dp_ref, (dq_ref, dk_ref, dv_ref), slice(None)):
            term = lax.dot_general(part[...], w_ref[0, :, at:at + part.shape[1]], _NT, preferred_element_type=F32)
            dh = term if dh is None else dh + term
        r, xh = _rms_stats(x_ref[...])
        dg_ref[...] += jnp.sum(dh * xh, axis=0, keepdims=True)
        dx_ref[...] = dxin_ref[...] + _rms_bwd(dh, g_ref[...], r, xh)

    third = lambda: pl.BlockSpec((tm, C_WIDTH), lambda i: (i, 0))
    return _call_carrying(
        carried, body, (dproj, *dqkv, w, x, g, dx_in),
        name="inproj_bwd_x",
        grid=(seq // tm,),
        in_specs=[
            pl.BlockSpec((tm, D_INT), lambda i: (i, 0)), third(), third(), third(),
            pl.BlockSpec((1, D_MODEL, D_INT), lambda i: (layer, 0, 0)),
            pl.BlockSpec((tm, D_MODEL), lambda i: (i, 0)),
            _full((1, D_MODEL)),
            pl.BlockSpec((tm, D_MODEL), lambda i: (i, 0)),
        ],
        out_specs=[pl.BlockSpec((tm, D_MODEL), lambda i: (i, 0)), _full((1, D_MODEL))],
        out_shape=[jax.ShapeDtypeStruct((seq, D_MODEL), F32), jax.ShapeDtypeStruct((1, D_MODEL), F32)],
        scratch_shapes=[], semantics=("arbitrary",),
    )


def inproj_bwd_w(h, dproj, dqkv):
    seq = h.shape[0]
    ts, tn = min(seq, 512), 512

    def body(h_ref, dp_ref, dq_ref, dk_ref, dv_ref, dw_ref):
        @pl.when(pl.program_id(0) == 0)
        def _():
            dw_ref[...] = jnp.zeros_like(dw_ref)

        ht = h_ref[...].T
        for at, part in _dproj_parts(dp_ref, (dq_ref, dk_ref, dv_ref), slice(None)):
            for c in range(0, part.shape[1], tn):
                dw_ref[0, :, at + c:at + c + tn] += _dot(ht, part[:, c:c + tn])

    third = lambda: pl.BlockSpec((ts, C_WIDTH), lambda s: (s, 0))
    return pl.pallas_call(
        body,
        name="inproj_bwd_w",
        grid=(seq // ts,),
        in_specs=[pl.BlockSpec((ts, D_MODEL), lambda s: (s, 0)), pl.BlockSpec((ts, D_INT), lambda s: (s, 0)),
                  third(), third(), third()],
        out_specs=_full((1, D_MODEL, D_INT)),
        out_shape=jax.ShapeDtypeStruct((1, D_MODEL, D_INT), F32),
        compiler_params=_cparams("arbitrary"),
    )(h, dproj, *dqkv)


N_IN = 3848


def _internal_of(col):
    return col if col < 768 else (col + 256 if col < 3840 else 768 + col - 3840)


def _column_runs(n_shard):
    runs = []
    for d in range(N_IN // n_shard):
        mine = []
        for j in range(n_shard):
            ci = _internal_of(d * n_shard + j)
            if mine and mine[-1][0] + mine[-1][1] == ci:
                mine[-1][1] += 1
            else:
                mine.append([ci, 1, j])
        runs.append(mine)
    return runs


def assemble_w_in(wi_all):
    n_dev, depth, _, n_shard = wi_all.shape
    tr = 256
    pieces = [[] for _ in range(D_INT // LANES)]
    for d, mine in enumerate(_column_runs(n_shard)):
        for ci, ln, off in mine:
            while ln > 0:
                blk, at = divmod(ci, LANES)
                take = min(ln, LANES - at)
                pieces[blk].append((at, take, d, off))
                ci, ln, off = ci + take, ln - take, off + take

    def body(x_ref, o_ref):
        for blk, parts in enumerate(pieces):
            vals, at = [], 0
            for start, ln, d, off in sorted(parts):
                if start > at:
                    vals.append(jnp.zeros((tr, start - at), BF16))
                vals.append(x_ref[d, 0, :, off:off + ln])
                at = start + ln
            if at < LANES:
                vals.append(jnp.zeros((tr, LANES - at), BF16))
            o_ref[0, :, blk * LANES:(blk + 1) * LANES] = vals[0] if len(vals) == 1 else jnp.concatenate(vals, axis=1)

    return pl.pallas_call(
        body,
        name="assemble_w_in",
        grid=(depth, D_MODEL // tr),
        in_specs=[pl.BlockSpec((n_dev, 1, tr, n_shard), lambda l, r: (0, l, r, 0))],
        out_specs=pl.BlockSpec((1, tr, D_INT), lambda l, r: (l, r, 0)),
        out_shape=jax.ShapeDtypeStruct((depth, D_MODEL, D_INT), BF16),
        compiler_params=_cparams("parallel", "parallel"),
    )(wi_all)


def split_w_in_grad(dwi, n_shard, core):
    tr = 256
    runs = _column_runs(n_shard)

    def body(core_ref, x_ref, keep_ref, send_ref):
        for d, mine in enumerate(runs):
            @pl.when(core_ref[0] == d % 2)
            def _():
                for ci, ln, off in mine:
                    keep_ref[d // 2, :, off:off + ln] = x_ref[0, :, ci:ci + ln]

            @pl.when(core_ref[0] != d % 2)
            def _():
                for ci, ln, off in mine:
                    send_ref[d // 2, :, off:off + ln] = x_ref[0, :, ci:ci + ln].astype(BF16)

    shards = lambda: pl.BlockSpec((N_CHIP, tr, n_shard), lambda r, s: (0, r, 0))
    grid_spec = pltpu.PrefetchScalarGridSpec(
        num_scalar_prefetch=1, grid=(D_MODEL // tr,),
        in_specs=[pl.BlockSpec((1, tr, D_INT), lambda r, s: (0, r, 0))], out_specs=[shards(), shards()])
    return pl.pallas_call(
        body,
        name="split_w_in_grad",
        grid_spec=grid_spec,
        out_shape=[jax.ShapeDtypeStruct((N_CHIP, D_MODEL, n_shard), F32), jax.ShapeDtypeStruct((N_CHIP, D_MODEL, n_shard), BF16)],
        compiler_params=_cparams("parallel"),
    )(core, dwi)


C_WIDTH = 512
C_HEADS = 8
C_HDIM = 64
C_PAIRS = C_HEADS // 2
C_BQ = 512
C_TAIL = 16
C_KG = 4


def _split3(x):
    hi = x.astype(BF16)
    r = x - hi.astype(F32)
    mid = r.astype(BF16)
    return hi, mid, (r - mid.astype(F32)).astype(BF16)


def _piece_selectors():
    sel = np.zeros((C_HEADS, 3 * LANES, LANES), np.float32)
    for p in range(C_PAIRS):
        for e in range(2):
            for t in range(3):
                sel[2 * p + e, t * LANES + 2 * p + e, 3 * e + t] = -1.0
    return sel


def fox_prep(proj, qkv, bf_row):
    seq = proj.shape[0]
    nblk = seq // CHUNK
    nb = min(nblk, 4)
    tril = jnp.asarray(np.tril(np.ones((CHUNK, CHUNK), np.float32)), BF16)
    sel = jnp.asarray(_piece_selectors(), BF16)
    rows_t = CHUNK + C_TAIL

    def body(fl_ref, q_ref, k_ref, v_ref, bf_ref, l_ref, sel_ref, ka_ref, va_ref, vt_ref, kt_ref, qt_ref, qa_ref, carry_ref):
        @pl.when(pl.program_id(0) == 0)
        def _():
            carry_ref[...] = jnp.zeros_like(carry_ref)

        lane = lax.broadcasted_iota(jnp.int32, (CHUNK, LANES), 1)
        row = lax.broadcasted_iota(jnp.int32, (CHUNK, LANES), 0)
        r16 = lax.broadcasted_iota(jnp.int32, (C_TAIL, 2 * CHUNK), 0)
        l16 = lax.broadcasted_iota(jnp.int32, (C_TAIL, 2 * CHUNK), 1)
        zero = jnp.zeros((CHUNK, LANES), BF16)
        one = jnp.ones((CHUNK, LANES), BF16)

        def by_keys(x, right_a, right_b):
            xb = x.astype(BF16)
            top = jnp.concatenate([jnp.where(lane < C_HDIM, xb, zero), right_a], axis=1)
            return jnp.concatenate([top, jnp.concatenate([jnp.where(lane < C_HDIM, zero, xb), right_b], axis=1)], axis=0)

        def by_lanes(x, tail):
            xt = x.T.astype(BF16)
            main = jnp.concatenate([jnp.where(row < C_HDIM, xt, zero), jnp.where(row < C_HDIM, zero, xt)], axis=1)
            return jnp.concatenate([main, tail], axis=0)

        for j in range(nb):
            tok, wide2 = slice(j * CHUNK, (j + 1) * CHUNK), slice(j * 2 * CHUNK, (j + 1) * 2 * CHUNK)
            lf = jax.nn.log_sigmoid(fl_ref[tok, :LANES] + bf_ref[...])
            c = _exact_times(l_ref[...], lf, 3) + carry_ref[...]
            carry_ref[...] += jnp.sum(lf, axis=0, keepdims=True)
            c3 = jnp.concatenate(_split3(c), axis=1)
            for p in range(C_PAIRS):
                cols = slice(p * LANES, (p + 1) * LANES)
                q2, k2, v2 = (r[tok, cols].astype(F32) for r in (q_ref, k_ref, v_ref))
                q2 = q2 * (C_HDIM ** -0.5)
                negc = [_dot(c3, sel_ref[2 * p + e]).astype(BF16) for e in range(2)]
                ones3 = [jnp.where((lane >= 3 * e) & (lane < 3 * e + 3), one, zero) for e in range(2)]
                tail = jnp.where(((r16 == 2 * p) & (l16 < CHUNK)) | ((r16 == 2 * p + 1) & (l16 >= CHUNK)), 1.0, 0.0).astype(BF16)
                ka_ref[p, wide2] = by_keys(k2, negc[0], negc[1])
                va_ref[p, wide2] = by_keys(v2, ones3[0], ones3[1])
                kt_ref[p, :, wide2] = by_lanes(k2, tail)
                vt_ref[p, :, wide2] = by_lanes(v2, tail)
                qt_ref[p, :, tok] = jnp.concatenate([q2.T.astype(BF16), jnp.where(row < 6, one, zero)], axis=0)
                qa_ref[p, tok] = jnp.concatenate([q2.astype(BF16), jnp.where((lane == 2 * p) | (lane == 2 * p + 1), one, zero)], axis=1)

    wide = lambda j: pl.BlockSpec((nb * CHUNK, C_WIDTH), lambda n: (n, j))
    by_rows = pl.BlockSpec((C_PAIRS, nb * 2 * CHUNK, 2 * CHUNK), lambda n: (0, n, 0))
    by_cols = pl.BlockSpec((C_PAIRS, rows_t, nb * 2 * CHUNK), lambda n: (0, 0, n))
    return pl.pallas_call(
        body,
        name="fox_prep",
        grid=(nblk // nb,),
        in_specs=[pl.BlockSpec((nb * CHUNK, 256), lambda n: (n, 3)), wide(0), wide(1), wide(2), _full((1, LANES)),
                  _full((CHUNK, CHUNK)), _full((C_HEADS, 3 * LANES, LANES))],
        out_specs=[by_rows, by_rows, by_cols, by_cols,
                   pl.BlockSpec((C_PAIRS, 2 * CHUNK, nb * CHUNK), lambda n: (0, 0, n)),
                   pl.BlockSpec((C_PAIRS, nb * CHUNK, 2 * CHUNK), lambda n: (0, n, 0))],
        out_shape=[jax.ShapeDtypeStruct((C_PAIRS, 2 * seq, 2 * CHUNK), BF16)] * 2
        + [jax.ShapeDtypeStruct((C_PAIRS, rows_t, 2 * seq), BF16)] * 2
        + [jax.ShapeDtypeStruct((C_PAIRS, 2 * CHUNK, seq), BF16), jax.ShapeDtypeStruct((C_PAIRS, seq, 2 * CHUNK), BF16)],
        scratch_shapes=[pltpu.VMEM((1, LANES), F32)],
        compiler_params=_cparams("arbitrary"),
    )(proj, qkv, qkv, qkv, bf_row, tril, sel)


def _visible(shape, key0, query0):
    row = lax.broadcasted_iota(jnp.int32, shape, 0)
    key = key0 + lax.shift_left(lax.shift_right_logical(row, 8), 7) + (row & (CHUNK - 1))
    return key <= query0 + lax.broadcasted_iota(jnp.int32, shape, 1)


def _rows_ab(a, b, n):
    return jnp.concatenate([jnp.broadcast_to(a, (C_HDIM, n)), jnp.broadcast_to(b, (C_HDIM, n))], axis=0)


def _call_carrying(ex, body, operands, *, name, grid, in_specs, out_specs, out_shape, scratch_shapes, semantics=None):
    if ex is None:
        semantics = semantics or ("parallel", *["arbitrary"] * (len(grid) - 1))
        return pl.pallas_call(body, name=name, grid=grid, in_specs=in_specs, out_specs=out_specs, out_shape=out_shape,
                              scratch_shapes=scratch_shapes, compiler_params=_cparams(*semantics))(*operands)
    n_in, n_out = len(in_specs), len(out_specs)

    def wrapped(*refs):
        own, parts = _carried_refs(refs, n_in, n_out, ex)
        ids = [pl.program_id(a) for a in range(len(grid))]
        pl.when(functools.reduce(jnp.logical_and, [i == 0 for i in ids]))(lambda: ex.start(*parts))
        if hasattr(ex, "relay"):
            linear = functools.reduce(lambda at, ig: at * ig[1] + ig[0], zip(ids, grid), 0)
            pl.when(linear == int(np.prod(grid)) // 2)(lambda: ex.relay(*parts))
        body(*own)
        pl.when(functools.reduce(jnp.logical_and, [i == g - 1 for i, g in zip(ids, grid)]))(lambda: ex.finish(*parts))

    return pl.pallas_call(
        wrapped, name=name, grid=grid,
        in_specs=list(in_specs) + [_ANY] * len(ex.inputs), out_specs=list(out_specs) + [_ANY] * len(ex.out_shape),
        out_shape=list(out_shape) + list(ex.out_shape), scratch_shapes=list(scratch_shapes) + list(ex.scratch),
        input_output_aliases={n_in + i: n_out + o for i, o in getattr(ex, "aliases", {}).items()},
        compiler_params=_cparams(*["arbitrary"] * len(grid)),
    )(*operands, *ex.inputs)


def fox_fwd(qt, ka, vt, carried=None):
    seq = qt.shape[2]
    nblk = seq // CHUNK
    bq = min(C_BQ, seq)
    grp = bq // CHUNK
    rows_t = CHUNK + C_TAIL

    def body(qt_ref, ka_ref, vt_ref, o_ref, lse_ref, acc_ref, s_ref):
        p, i = pl.program_id(0), pl.program_id(1)
        qtile = qt_ref[0]
        r16 = lax.broadcasted_iota(jnp.int32, (C_TAIL, bq), 0)

        def scores(t):
            at = pl.multiple_of(t * grp * 2 * CHUNK, 2 * CHUNK)
            return _dot(ka_ref[0, pl.ds(at, grp * 2 * CHUNK), :], qtile)

        def rescale(al_a, al_b):
            tail = jnp.where(r16 == 2 * p, al_a, jnp.where(r16 == 2 * p + 1, al_b, 1.0))
            return jnp.concatenate([_rows_ab(al_a, al_b, bq), tail], axis=0)

        def diagonal(m):
            ma, mb = m
            na, nb = ma, mb
            blocks = []
            for g in range(grp):
                s = s_ref[g * 2 * CHUNK:(g + 1) * 2 * CHUNK, g * CHUNK:]
                s = jnp.where(_visible(s.shape, i * bq + g * CHUNK, i * bq + g * CHUNK), s, -jnp.inf)
                blocks.append(s)
                unseen = [jnp.full((1, g * CHUNK), -jnp.inf, F32)] if g else []
                na = jnp.maximum(na, jnp.concatenate(unseen + [jnp.max(s[:CHUNK], axis=0, keepdims=True)], axis=1))
                nb = jnp.maximum(nb, jnp.concatenate(unseen + [jnp.max(s[CHUNK:], axis=0, keepdims=True)], axis=1))
            acc_ref[...] = acc_ref[...] * rescale(jnp.exp(ma - na), jnp.exp(mb - nb))
            for g in range(grp):
                n = bq - g * CHUNK
                n2 = jnp.concatenate([jnp.broadcast_to(na[:, g * CHUNK:], (CHUNK, n)),
                                      jnp.broadcast_to(nb[:, g * CHUNK:], (CHUNK, n))], axis=0)
                at = pl.multiple_of((i * grp + g) * 2 * CHUNK, 2 * CHUNK)
                pt = jnp.exp(blocks[g] - n2).astype(BF16)
                acc_ref[:, g * CHUNK:] += _dot(vt_ref[0, :, pl.ds(at, 2 * CHUNK)], pt)
            return na, nb

        def group(t, m):
            ma, mb = m
            at = pl.multiple_of(t * grp * 2 * CHUNK, 2 * CHUNK)
            s = s_ref[...]
            sa = [s[g * 2 * CHUNK:g * 2 * CHUNK + CHUNK] for g in range(grp)]
            sb = [s[g * 2 * CHUNK + CHUNK:(g + 1) * 2 * CHUNK] for g in range(grp)]
            na, nb = ma, mb
            for g in range(grp):
                na = jnp.maximum(na, jnp.max(sa[g], axis=0, keepdims=True))
                nb = jnp.maximum(nb, jnp.max(sb[g], axis=0, keepdims=True))
            al_a, al_b = jnp.exp(ma - na), jnp.exp(mb - nb)
            pt = jnp.concatenate([jnp.exp(x - n) for g in range(grp) for x, n in ((sa[g], na), (sb[g], nb))], axis=0)
            pv = _dot(vt_ref[0, :, pl.ds(at, grp * 2 * CHUNK)], pt.astype(BF16))
            acc_ref[...] = acc_ref[...] * rescale(al_a, al_b) + pv
            return na, nb

        def step(t, m):
            s_next = scores(t + 1)
            m = group(t, m)
            s_ref[...] = s_next
            return m

        acc_ref[...] = jnp.zeros_like(acc_ref)
        s_ref[...] = scores(0)
        m = (jnp.full((1, bq), -jnp.inf, F32), jnp.full((1, bq), -jnp.inf, F32))
        m = lax.fori_loop(0, i, step, m)
        ma, mb = diagonal(m)
        tailv = acc_ref[CHUNK:rows_t, :]
        la = jnp.sum(jnp.where(r16 == 2 * p, tailv, 0.0), axis=0, keepdims=True)
        lb = jnp.sum(jnp.where(r16 == 2 * p + 1, tailv, 0.0), axis=0, keepdims=True)
        o_ref[...] = (acc_ref[0:CHUNK, :] * _rows_ab(1.0 / la, 1.0 / lb, bq)).T
        lse_ref[0, 0:1, :] = ma + jnp.log(la)
        lse_ref[0, 1:2, :] = mb + jnp.log(lb)

    return _call_carrying(
        carried, body, (qt, ka, vt),
        name="fox_fwd",
        grid=(C_PAIRS, seq // bq),
        in_specs=[
            pl.BlockSpec((1, 2 * CHUNK, bq), lambda p, i: (p, 0, i)),
            pl.BlockSpec((1, 2 * seq, 2 * CHUNK), lambda p, i: (p, 0, 0)),
            pl.BlockSpec((1, rows_t, 2 * seq), lambda p, i: (p, 0, 0)),
        ],
        out_specs=[pl.BlockSpec((bq, LANES), lambda p, i: (i, p)), pl.BlockSpec((1, 2, bq), lambda p, i: (p, 0, i))],
        out_shape=[jax.ShapeDtypeStruct((seq, C_WIDTH), F32), jax.ShapeDtypeStruct((C_PAIRS, 2, seq), F32)],
        scratch_shapes=[pltpu.VMEM((rows_t, bq), F32), pltpu.VMEM((grp * 2 * CHUNK, bq), F32)],
    )


def fox_bwd_prep(dy, o, proj, dproj):
    seq = o.shape[0]
    rows = min(seq, 512)
    ind = np.zeros((C_WIDTH, LANES), np.float32)
    for h in range(C_HEADS):
        ind[h * C_HDIM:(h + 1) * C_HDIM, h] = 1.0
    ind = jnp.asarray(ind, BF16)
    sel = _piece_selectors()
    sel = jnp.asarray(np.stack([sel[2 * p].T + sel[2 * p + 1].T for p in range(C_PAIRS)]), BF16)

    def body(dy_ref, o_ref, z_ref, ind_ref, sel_ref, _, do_ref, dz_ref, dot_ref):
        dy_c, o_v, z = dy_ref[...], o_ref[...], z_ref[...]
        sg = jax.nn.sigmoid(z)
        do = dy_c * (z * sg)
        do_ref[...] = do.astype(BF16)
        dz_ref[...] = (dy_c * o_v * (sg * (1.0 + z * (1.0 - sg)))).astype(BF16)
        prod = do * o_v
        hi = prod.astype(BF16)
        lo = (prod - hi.astype(F32)).astype(BF16)
        delta = _dot(hi, ind_ref[...]) + _dot(lo, ind_ref[...])
        d3 = jnp.concatenate(_split3(delta.T), axis=0)
        for p in range(C_PAIRS):
            tail = _dot(sel_ref[p], d3).astype(BF16)
            dot_ref[p] = jnp.concatenate([do[:, p * LANES:(p + 1) * LANES].T.astype(BF16), tail], axis=0)

    return pl.pallas_call(
        body,
        name="fox_bwd_prep",
        grid=(seq // rows,),
        in_specs=[
            pl.BlockSpec((rows, C_WIDTH), lambda i: (i, 1)),
            pl.BlockSpec((rows, C_WIDTH), lambda i: (i, 0)),
            pl.BlockSpec((rows, C_WIDTH), lambda i: (i, P_Z_BLOCK)),
            _full((C_WIDTH, LANES)), _full((C_PAIRS, LANES, 3 * LANES)), _ANY,
        ],
        out_specs=[
            pl.BlockSpec((rows, C_WIDTH), lambda i: (i, 0)),
            pl.BlockSpec((rows, C_WIDTH), lambda i: (i, 7)),
            pl.BlockSpec((C_PAIRS, 2 * CHUNK, rows), lambda i: (0, 0, i)),
        ],
        out_shape=[jax.ShapeDtypeStruct((seq, C_WIDTH), BF16), jax.ShapeDtypeStruct(dproj.shape, BF16),
                   jax.ShapeDtypeStruct((C_PAIRS, 2 * CHUNK, seq), BF16)],
        input_output_aliases={5: 1},
        compiler_params=_cparams("parallel"),
    )(dy, o, proj, ind, sel, dproj)


def fox_bwd(ka, va, kt, qt, dot_t, qa, dob, lse, carried=None):
    seq = qt.shape[2]
    nblk = seq // CHUNK
    bq = min(C_BQ, seq)
    nq = seq // bq
    kg = min(C_KG, nblk)
    ng = nblk // kg
    rows_t = CHUNK + C_TAIL

    def body(ka_ref, va_ref, kt_ref, qt_ref, dot_ref, qa_ref, do_ref, lse_ref,
             dq_ref, dk_ref, dv_ref, dck_ref, dcq_ref, dqt_acc, dv_acc, dka_acc):
        p, jg = pl.program_id(0), pl.program_id(1)

        @pl.when(jg == 0)
        def _():
            dqt_acc[...] = jnp.zeros_like(dqt_acc)

        dv_acc[...] = jnp.zeros_like(dv_acc)
        dka_acc[...] = jnp.zeros_like(dka_acc)

        def step(i, carry):
            cols = pl.ds(pl.multiple_of(i * bq, bq), bq)
            qtile, dotile = qt_ref[0, :, cols], dot_ref[0, :, cols]
            do, qa_i = do_ref[cols, :], qa_ref[0, cols, :]
            lse2 = jnp.concatenate([jnp.broadcast_to(lse_ref[0, 0:1, cols], (CHUNK, bq)),
                                    jnp.broadcast_to(lse_ref[0, 1:2, cols], (CHUNK, bq))] * kg, axis=0)
            pt = jnp.exp(_dot(ka_ref[0], qtile) - lse2)
            ds = pt * _dot(va_ref[0], dotile)
            ptb, dsb = pt.astype(BF16), ds.astype(BF16)
            dv_acc[...] += _dot(ptb, do)
            dka_acc[...] += _dot(dsb, qa_i)
            dqt_acc[:, cols] += _dot(kt_ref[0], dsb)
            return carry

        def diagonal(i):
            cols = [pl.ds(pl.multiple_of(i * bq + kb * CHUNK, CHUNK), bq - kb * CHUNK) for kb in range(kg)]
            rows = [slice(kb * 2 * CHUNK, (kb + 1) * 2 * CHUNK) for kb in range(kg)]
            s = [_dot(ka_ref[0, rows[kb], :], qt_ref[0, :, cols[kb]]) for kb in range(kg)]
            dp = [_dot(va_ref[0, rows[kb], :], dot_ref[0, :, cols[kb]]) for kb in range(kg)]
            ptb, dsb = [], []
            for kb in range(kg):
                n = bq - kb * CHUNK
                lse2 = jnp.concatenate([jnp.broadcast_to(lse_ref[0, 0:1, cols[kb]], (CHUNK, n)),
                                        jnp.broadcast_to(lse_ref[0, 1:2, cols[kb]], (CHUNK, n))], axis=0)
                pt = jnp.exp(s[kb] - lse2)
                pt = jnp.where(_visible(pt.shape, (jg * kg + kb) * CHUNK, i * bq + kb * CHUNK), pt, 0.0)
                ptb.append(pt.astype(BF16))
                dsb.append((pt * dp[kb]).astype(BF16))
            for kb in range(kg):
                dv_acc[rows[kb], :] += _dot(ptb[kb], do_ref[cols[kb], :])
                dka_acc[rows[kb], :] += _dot(dsb[kb], qa_ref[0, cols[kb], :])
                dqt_acc[:, cols[kb]] += _dot(kt_ref[0, :, rows[kb]], dsb[kb])

        assert kg * CHUNK == bq
        diagonal(jg)
        lax.fori_loop(jg + 1, nq, step, 0)
        lane = lax.broadcasted_iota(jnp.int32, (CHUNK, LANES), 1)
        for kb in range(kg):
            rows = slice(kb * CHUNK, (kb + 1) * CHUNK)
            ra = slice(kb * 2 * CHUNK, kb * 2 * CHUNK + CHUNK)
            rb = slice(kb * 2 * CHUNK + CHUNK, (kb + 1) * 2 * CHUNK)
            dk_ref[rows, :] = jnp.where(lane < C_HDIM, dka_acc[ra, 0:LANES], dka_acc[rb, 0:LANES]).astype(BF16)
            dv_ref[rows, :] = jnp.where(lane < C_HDIM, dv_acc[ra, :], dv_acc[rb, :]).astype(BF16)
            dck_ref[0, rows, :] = (jnp.where(lane == 2 * p, dka_acc[ra, LANES:], 0.0)
                                   + jnp.where(lane == 2 * p + 1, dka_acc[rb, LANES:], 0.0))

        @pl.when(jg == ng - 1)
        def _():
            for c in range(nq):
                dq_ref[c * bq:(c + 1) * bq, :] = (dqt_acc[0:CHUNK, c * bq:(c + 1) * bq].T * (C_HDIM ** -0.5)).astype(BF16)
            dcq_ref[0] = dqt_acc[CHUNK:rows_t, :]

    per_pair = lambda r, c: pl.BlockSpec((1, r, c), lambda p, j: (p, 0, 0))
    by_rows = pl.BlockSpec((1, kg * 2 * CHUNK, 2 * CHUNK), lambda p, j: (p, j, 0))
    by_cols = pl.BlockSpec((1, rows_t, kg * 2 * CHUNK), lambda p, j: (p, 0, j))
    return _call_carrying(
        carried, body, (ka, va, kt, qt, dot_t, qa, dob, lse),
        name="fox_bwd",
        grid=(C_PAIRS, ng),
        in_specs=[by_rows, by_rows, by_cols, per_pair(2 * CHUNK, seq), per_pair(2 * CHUNK, seq),
                  per_pair(seq, 2 * CHUNK), pl.BlockSpec((seq, LANES), lambda p, j: (0, p)), per_pair(2, seq)],
        out_specs=[pl.BlockSpec((seq, LANES), lambda p, j: (0, p)),
                   pl.BlockSpec((kg * CHUNK, LANES), lambda p, j: (j, p)),
                   pl.BlockSpec((kg * CHUNK, LANES), lambda p, j: (j, p)),
                   pl.BlockSpec((1, kg * CHUNK, LANES), lambda p, j: (p, j, 0)),
                   per_pair(C_TAIL, seq)],
        out_shape=[jax.ShapeDtypeStruct((seq, C_WIDTH), BF16)] * 3
        + [jax.ShapeDtypeStruct((C_PAIRS, seq, LANES), F32), jax.ShapeDtypeStruct((C_PAIRS, C_TAIL, seq), F32)],
        scratch_shapes=[pltpu.VMEM((rows_t, seq), F32), pltpu.VMEM((kg * 2 * CHUNK, LANES), F32),
                        pltpu.VMEM((kg * 2 * CHUNK, 2 * CHUNK), F32)],
    )


def fox_post(dcq, dck, proj, bf_row, dproj):
    seq = proj.shape[0]
    rows = min(seq, 512)
    nc = seq // rows
    triu = jnp.asarray(np.triu(np.ones((CHUNK, CHUNK), np.float32)), BF16)

    def body(dq_ref, dk_ref, fl_ref, bf_ref, u_ref, _, dfl_ref, dbf_ref, carry_ref):
        @pl.when(pl.program_id(0) == 0)
        def _():
            carry_ref[...] = jnp.zeros_like(carry_ref)
            dbf_ref[...] = jnp.zeros_like(dbf_ref)

        for j in reversed(range(rows // CHUNK)):
            at = slice(j * CHUNK, (j + 1) * CHUNK)
            heads = (dq_ref[0, :, at] + dq_ref[1, :, at]) + (dq_ref[2, :, at] + dq_ref[3, :, at])
            dc = jnp.concatenate([heads, jnp.zeros((CHUNK - C_TAIL, CHUNK), F32)], axis=0).T
            dc = dc - ((dk_ref[0, at] + dk_ref[1, at]) + (dk_ref[2, at] + dk_ref[3, at]))
            g = _exact_times(u_ref[...], dc, 3) + carry_ref[...]
            carry_ref[...] += jnp.sum(dc, axis=0, keepdims=True)
            dfl = g * jax.nn.sigmoid(-(fl_ref[at, :LANES] + bf_ref[...]))
            dbf_ref[...] += jnp.sum(dfl, axis=0, keepdims=True)
            dfl_ref[at, :] = jnp.concatenate([dfl, jnp.zeros_like(dfl)], axis=1).astype(BF16)

    rev = lambda n: nc - 1 - n
    return pl.pallas_call(
        body,
        name="fox_post",
        grid=(nc,),
        in_specs=[
            pl.BlockSpec((C_PAIRS, C_TAIL, rows), lambda n: (0, 0, rev(n))),
            pl.BlockSpec((C_PAIRS, rows, LANES), lambda n: (0, rev(n), 0)),
            pl.BlockSpec((rows, 256), lambda n: (rev(n), 3)),
            _full((1, LANES)), _full((CHUNK, CHUNK)), _ANY,
        ],
        out_specs=[pl.BlockSpec((rows, 256), lambda n: (rev(n), 3)), _full((1, LANES))],
        out_shape=[jax.ShapeDtypeStruct(dproj.shape, BF16), jax.ShapeDtypeStruct((1, LANES), F32)],
        input_output_aliases={5: 0},
        scratch_shapes=[pltpu.VMEM((1, LANES), F32)],
        compiler_params=_cparams("arbitrary"),
    )(dcq, dck, proj, bf_row, triu, dproj)


N_DEV = 8
MESH = pl.DeviceIdType.MESH
_ANY = pl.BlockSpec(memory_space=pl.ANY)


def _mesh_pos():
    return lax.axis_index("x"), lax.axis_index("y"), lax.axis_index("c")


def _dev_index(px, py, pc):
    return 4 * px + 2 * py + pc


def _row_pieces(ref, rows):
    return [ref.at[idx + (pl.ds(r, rows),)] for idx in np.ndindex(*ref.shape[:-2]) for r in range(0, ref.shape[-2], rows)]


class _Transfer:
    def __init__(self, src, dst, rows, send_sem, recv_sem, to):
        self.src, self.dst, self.rows, self.sems, self.to = src, dst, rows, (send_sem, recv_sem), to

    def _copy(self, src, dst):
        return pltpu.make_async_remote_copy(src_ref=src, dst_ref=dst, send_sem=self.sems[0], recv_sem=self.sems[1],
                                            device_id=self.to, device_id_type=MESH)

    def start(self):
        for s, d in zip(_row_pieces(self.src, self.rows), _row_pieces(self.dst, self.rows), strict=True):
            self._copy(s, d).start()

    def wait_send(self):
        self._copy(self.src, self.dst).wait_send()

    def wait_recv(self):
        self._copy(self.src, self.dst).wait_recv()


def _exchange_call(ex, name):
    n_in, n_out = len(ex.inputs), len(ex.out_shape)

    def body(*refs):
        parts = refs[:n_in], refs[n_in:n_in + n_out], refs[n_in + n_out:]
        ex.start(*parts)
        getattr(ex, "relay", lambda *_: None)(*parts)
        ex.finish(*parts)

    return pl.pallas_call(body, name=name, in_specs=[_ANY] * n_in, out_specs=[_ANY] * n_out, out_shape=ex.out_shape,
                          scratch_shapes=ex.scratch, input_output_aliases=getattr(ex, "aliases", {}))(*ex.inputs)


def _carried_refs(refs, n_in, n_out, ex):
    k_in, k_out, k_sem = (len(ex.inputs), len(ex.out_shape), len(ex.scratch)) if ex else (0, 0, 0)
    a, b, c = n_in + k_in, n_in + k_in + n_out, n_in + k_in + n_out + k_out
    own = refs[:n_in] + refs[a:b] + refs[c:len(refs) - k_sem]
    return own, (refs[n_in:a], refs[b:c], refs[len(refs) - k_sem:])


class AllGatherWeights:
    def __init__(self, blocks):
        n = len(blocks)
        self.inputs = tuple(blocks)
        self.out_shape = [jax.ShapeDtypeStruct((N_DEV,) + b.shape, b.dtype) for b in blocks]
        self.scratch = ([pltpu.SemaphoreType.DMA((n, 7)), pltpu.SemaphoreType.DMA((n, 7)), pltpu.SemaphoreType.DMA((n, 2))]
                        + [pltpu.VMEM(b.shape, b.dtype) for b in blocks])

    def _plan(self, ins, outs, scratch):
        send_sems, recv_sems, local_sems, *staged = scratch
        x, y, c = _mesh_pos()
        me, sibling = (x, y, c), (x, y, 1 - c)
        chips = [(1 - x, y), (x, 1 - y), (1 - x, 1 - y)]
        every = range(len(ins))

        def copy(a, k, block, to, own=False):
            slot = outs[a].at[_dev_index(*block)]
            return _Transfer(ins[a] if own else slot, slot, ins[a].shape[-2], send_sems.at[a, k], recv_sems.at[a, k], to)

        mine = [(pltpu.make_async_copy(ins[a], staged[a], local_sems.at[a, 0]),
                 pltpu.make_async_copy(staged[a], outs[a].at[_dev_index(*me)], local_sems.at[a, 1])) for a in every]
        first = [copy(a, 1 + j, me, (*chip, c), own=True) for j, chip in enumerate(chips) for a in every]
        first += [copy(a, 0, me, sibling, own=True) for a in every]
        passed = [[copy(a, 4 + j, (*chip, c), sibling) for a in every] for j, chip in enumerate(chips)]
        return me, sibling, chips, c, every, copy, mine, first, passed

    def start(self, ins, outs, scratch):
        *_, mine, first, _ = self._plan(ins, outs, scratch)
        for to_vmem, _ in mine:
            to_vmem.start()
        for cp in first:
            cp.start()

    def relay(self, ins, outs, scratch):
        me, sibling, chips, c, every, copy, mine, first, passed = self._plan(ins, outs, scratch)
        for to_vmem, to_slot in mine:
            to_vmem.wait()
            to_slot.start()
        for j, chip in enumerate(chips):
            for a in every:
                copy(a, 1 + j, (*chip, c), me).wait_recv()
            for cp in passed[j]:
                cp.start()

    def finish(self, ins, outs, scratch):
        me, sibling, chips, c, every, copy, mine, first, passed = self._plan(ins, outs, scratch)
        for a in every:
            copy(a, 0, sibling, me).wait_recv()
        for j, chip in enumerate(chips):
            for a in every:
                copy(a, 4 + j, (*chip, 1 - c), me).wait_recv()
        for cp in first + [cp for group in passed for cp in group]:
            cp.wait_send()
        for _, to_slot in mine:
            to_slot.wait()


N_CHIP = 4


class PairExchange:
    def __init__(self, by_core, whole=()):
        self.inputs = tuple(by_core) + tuple(whole)
        self.n_by_core = len(by_core)
        self.out_shape = ([jax.ShapeDtypeStruct(a.shape[1:], a.dtype) for a in by_core]
                          + [jax.ShapeDtypeStruct(a.shape, a.dtype) for a in whole])
        n = len(self.inputs)
        self.scratch = [pltpu.SemaphoreType.DMA((n,)), pltpu.SemaphoreType.DMA((n,))]

    def _copies(self, ins, outs, sems):
        x, y, c = _mesh_pos()
        srcs = [r.at[1 - c] if a < self.n_by_core else r for a, r in enumerate(ins)]
        return [_Transfer(srcs[a], outs[a], outs[a].shape[-2], sems[0].at[a], sems[1].at[a], (x, y, 1 - c))
                for a in range(len(ins))]

    def start(self, ins, outs, sems):
        for cp in self._copies(ins, outs, sems):
            cp.start()

    def finish(self, ins, outs, sems):
        copies = self._copies(ins, outs, sems)
        for cp in copies:
            cp.wait_recv()
        for cp in copies:
            cp.wait_send()


def pair_sum(own, other, dtype, rows, name, core, layer, depth, stacked=None):
    n, n_r, n_c = other.shape
    by_core = own.ndim == 4
    own = own if by_core else own[None]

    def body(core_ref, a_ref, b_ref, *refs):
        refs[-1][0, 0] = (a_ref[0, 0] + b_ref[0].astype(F32)).astype(dtype)

    carried = () if stacked is None else (stacked,)
    grid_spec = pltpu.PrefetchScalarGridSpec(
        num_scalar_prefetch=1,
        grid=(n, n_r // rows),
        in_specs=[pl.BlockSpec((1, 1, rows, n_c), lambda i, r, s: (s[0] if by_core else 0, i, r, 0)),
                  pl.BlockSpec((1, rows, n_c), lambda i, r, s: (i, r, 0))] + [_ANY] * len(carried),
        out_specs=pl.BlockSpec((1, 1, rows, n_c), lambda i, r, s: (i, layer, r, 0)),
    )
    return pl.pallas_call(
        body,
        name=name,
        grid_spec=grid_spec,
        out_shape=jax.ShapeDtypeStruct((n, depth, n_r, n_c), dtype),
        input_output_aliases={3: 0} if carried else {},
        compiler_params=_cparams("parallel", "parallel"),
    )(core, own, other, *carried)


def small_sum(a, b, name):
    def body(a_ref, b_ref, o_ref):
        o_ref[...] = a_ref[...] + b_ref[...]

    return pl.pallas_call(body, name=name, out_shape=jax.ShapeDtypeStruct(a.shape, a.dtype))(a, b)


class ChipExchange:
    def __init__(self, by_chip=(), layers=(), gathered=(), stacked=()):
        stacked = tuple(stacked) or (None,) * len(by_chip)
        kept = [s for s in stacked if s is not None]
        self.inputs = tuple(by_chip) + tuple(gathered) + tuple(kept)
        self.n_by_chip, self.n_gathered = len(by_chip), len(gathered)
        self.items = [(a, l) for a in range(len(by_chip)) for l in layers[a]] + [(self.n_by_chip + g, None) for g in range(len(gathered))]
        self.out_shape = ([jax.ShapeDtypeStruct((N_CHIP - 1,) + a.shape[1:], a.dtype) for a in by_chip]
                          + [jax.ShapeDtypeStruct((N_CHIP,) + a.shape, a.dtype) for a in gathered])
        at = iter(range(self.n_by_chip + self.n_gathered, len(self.inputs)))
        self.aliases = {next(at): a for a, s in enumerate(stacked) if s is not None}
        n = len(self.items)
        self.scratch = [pltpu.SemaphoreType.DMA((n, 3)), pltpu.SemaphoreType.DMA((n, 3)),
                        pltpu.SemaphoreType.DMA((max(self.n_gathered, 1),))]

    def _plan(self, ins, outs, sems):
        x, y, c = _mesh_pos()
        chip = 2 * x + y
        n = len(self.items)

        def copy(i, k, sending):
            a, layer = self.items[i]
            px, py = x ^ ((k >> 1) & 1), y ^ (k & 1)
            if layer is not None:
                src, dst = ins[a].at[2 * px + py, layer], outs[a].at[k - 1, layer]
            else:
                src, dst = ins[a], outs[a].at[chip if sending else 2 * px + py]
            return _Transfer(src, dst, dst.shape[-2], sems[0].at[i, k - 1], sems[1].at[i, k - 1], (px, py, c))

        local = [pltpu.make_async_copy(ins[a], outs[a].at[chip], sems[2].at[a - self.n_by_chip])
                 for a in range(self.n_by_chip, self.n_by_chip + self.n_gathered)]
        return n, copy, local

    def start(self, ins, outs, sems):
        n, copy, local = self._plan(ins, outs, sems)
        for cp in local:
            cp.start()
        for k in range(1, N_CHIP):
            for a in range(n):
                copy(a, k, True).start()

    def finish(self, ins, outs, sems):
        n, copy, local = self._plan(ins, outs, sems)
        for k in range(1, N_CHIP):
            for a in range(n):
                copy(a, k, False).wait_recv()
        for k in range(1, N_CHIP):
            for a in range(n):
                copy(a, k, True).wait_send()
        for cp in local:
            cp.wait()


ADAM_LR = 0.001
ADAM_B1 = 0.9
ADAM_B2 = 0.999
ADAM_EPS = 1e-08
ADAM_WD = 0.01
ADAM_STEP = 10


def adam_reduce(parts, w, m, v, rows, name, own=None, chip=None):
    n_l, n_r, n_c = w.shape
    n_parts = parts.shape[0]

    def body(*refs):
        p_ref, w_ref, m_ref, v_ref, g_ref, d_ref, m2_ref, v2_ref = refs[-8:]
        g = p_ref[0, 0].astype(F32)
        if own is not None:
            g = refs[-9][...].reshape(rows, n_c).astype(F32) + g
        for d in range(1, n_parts):
            g = g + p_ref[d, 0].astype(F32)
        m2 = ADAM_B1 * m_ref[0] + (1.0 - ADAM_B1) * g
        v2 = ADAM_B2 * v_ref[0] + (1.0 - ADAM_B2) * (g * g)
        m_hat = m2 / (1.0 - ADAM_B1 ** ADAM_STEP)
        v_hat = v2 / (1.0 - ADAM_B2 ** ADAM_STEP)
        g_ref[0] = g
        d_ref[0] = -ADAM_LR * (m_hat / (jnp.sqrt(v_hat) + ADAM_EPS) + ADAM_WD * w_ref[0])
        m2_ref[0] = m2
        v2_ref[0] = v2

    blk = lambda: pl.BlockSpec((1, rows, n_c), lambda l, r, *_: (l, r, 0))
    in_specs = [pl.BlockSpec((n_parts, 1, rows, n_c), lambda l, r, *_: (0, l, r, 0)), blk(), blk(), blk()]
    args = (parts, w, m, v)
    if own is not None:
        in_specs = [pl.BlockSpec((1, 1, rows, n_c), lambda l, r, s: (s[0], l, r, 0))] + in_specs
        args = (chip, own) + args
    grid_spec = pltpu.PrefetchScalarGridSpec(
        num_scalar_prefetch=0 if own is None else 1, grid=(n_l, n_r // rows), in_specs=in_specs,
        out_specs=[blk(), blk(), blk(), blk()])
    return pl.pallas_call(
        body,
        name=name,
        grid_spec=grid_spec,
        out_shape=[jax.ShapeDtypeStruct(w.shape, F32)] * 4,
        compiler_params=_cparams("parallel", "parallel"),
    )(*args)


def adam_reduce_columns(parts, w, m, v, name, own, chip):
    n_l, n_r, n_c = w.shape
    n_parts = parts.shape[0]
    view = lambda a: jnp.transpose(a, (2, 0, 1))

    def body(_, own_ref, p_ref, w_ref, m_ref, v_ref, g_ref, d_ref, m2_ref, v2_ref):
        for l in range(n_l):
            g = own_ref[0, l].astype(F32) + p_ref[0, l].astype(F32)
            for d in range(1, n_parts):
                g = g + p_ref[d, l].astype(F32)
            g = g.T
            w_l, m_l, v_l = w_ref[:, l, :], m_ref[:, l, :], v_ref[:, l, :]
            m2 = ADAM_B1 * m_l + (1.0 - ADAM_B1) * g
            v2 = ADAM_B2 * v_l + (1.0 - ADAM_B2) * (g * g)
            m_hat = m2 / (1.0 - ADAM_B1 ** ADAM_STEP)
            v_hat = v2 / (1.0 - ADAM_B2 ** ADAM_STEP)
            g_ref[:, l, :] = g
            d_ref[:, l, :] = -ADAM_LR * (m_hat / (jnp.sqrt(v_hat) + ADAM_EPS) + ADAM_WD * w_l)
            m2_ref[:, l, :] = m2
            v2_ref[:, l, :] = v2

    blk = lambda: pl.BlockSpec((LANES, n_l, n_r), lambda c, s: (c, 0, 0))
    grid_spec = pltpu.PrefetchScalarGridSpec(
        num_scalar_prefetch=1, grid=(pl.cdiv(n_c, LANES),),
        in_specs=[pl.BlockSpec((1, n_l, n_r, LANES), lambda c, s: (s[0], 0, 0, c)),
                  pl.BlockSpec((n_parts, n_l, n_r, LANES), lambda c, s: (0, 0, 0, c)), blk(), blk(), blk()],
        out_specs=[blk(), blk(), blk(), blk()])
    outs = pl.pallas_call(
        body,
        name=name,
        grid_spec=grid_spec,
        out_shape=[jax.ShapeDtypeStruct((n_c, n_l, n_r), F32)] * 4,
        compiler_params=_cparams("parallel"),
    )(chip, own, parts, view(w), view(m), view(v))
    return [jnp.transpose(o, (1, 2, 0)) for o in outs]


_SMALL = (("norm_g", (2, 1024)), ("gmlp_ln_g", (2, 4, 64)), ("gmlp_ln_b", (2, 4, 64)),
          ("gmlp_b_s", (2, 4, 128)), ("hgrn_lb", (2, 256)), ("hgrn_onorm_g", (2, 64)), ("fox_b_f", (2, 8)),
          ("final_norm_g", (1024,)), ("loss", ()))


def _padded(n):
    return -(-n // LANES) * LANES


_SMALL_ROWS = -(-sum(_padded(int(np.prod(s))) for _, s in _SMALL) // LANES // 8) * 8


def _pack_small(vals):
    flat = []
    for (name, shape), a in zip(_SMALL, vals, strict=True):
        n = int(np.prod(shape))
        flat.append(jnp.pad(a.reshape(n).astype(F32), (0, _padded(n) - n)))
    flat = jnp.concatenate(flat)
    return jnp.pad(flat, (0, _SMALL_ROWS * LANES - flat.shape[0])).reshape(_SMALL_ROWS, LANES)


def _unpack_small(slab):
    flat, out, at = slab.reshape(-1), {}, 0
    for name, shape in _SMALL:
        n = int(np.prod(shape))
        out[name] = flat[at:at + n].reshape(shape)
        at += _padded(n)
    return out


def sum_parts(parts, name):
    def body(p_ref, o_ref):
        g = p_ref[0]
        for d in range(1, parts.shape[0]):
            g = g + p_ref[d]
        o_ref[...] = g

    return pl.pallas_call(body, name=name, out_shape=jax.ShapeDtypeStruct(parts.shape[1:], F32))(parts)


def adam_small(gs, ws, ms, vs):
    n = len(gs)

    def body(*refs):
        for k in range(n):
            g, w, m, v = (refs[j * n + k][...] for j in range(4))
            m2 = ADAM_B1 * m + (1.0 - ADAM_B1) * g
            v2 = ADAM_B2 * v + (1.0 - ADAM_B2) * (g * g)
            m_hat = m2 / (1.0 - ADAM_B1 ** ADAM_STEP)
            v_hat = v2 / (1.0 - ADAM_B2 ** ADAM_STEP)
            refs[4 * n + k][...] = -ADAM_LR * (m_hat / (jnp.sqrt(v_hat) + ADAM_EPS) + ADAM_WD * w)
            refs[5 * n + k][...] = m2
            refs[6 * n + k][...] = v2

    outs = pl.pallas_call(body, name="adam_small",
                          out_shape=[jax.ShapeDtypeStruct(w.shape, F32) for _ in range(3) for w in ws])(*gs, *ws, *ms, *vs)
    return outs[:n], outs[n:2 * n], outs[2 * n:]


def kernel(x, norm_g, w_in, w_out, gmlp_ln_g, gmlp_ln_b, gmlp_w_s, gmlp_b_s, hgrn_lb, hgrn_onorm_g, fox_b_f, final_norm_g, loss_target, m_norm_g, m_w_in, m_w_out, m_gmlp_ln_g, m_gmlp_ln_b, m_gmlp_w_s, m_gmlp_b_s, m_hgrn_lb, m_hgrn_onorm_g, m_fox_b_f, m_final_norm_g, v_norm_g, v_w_in, v_w_out, v_gmlp_ln_g, v_gmlp_ln_b, v_gmlp_w_s, v_gmlp_b_s, v_hgrn_lb, v_hgrn_onorm_g, v_fox_b_f, v_final_norm_g):
    depth = w_in.shape[0]
    seq = x.shape[1]
    assert w_in.shape[2] * N_DEV == N_IN
    xs, tgt = x[0], loss_target[0]

    wi_blk, wo_blk = w_in.astype(BF16), w_out.astype(BF16)
    (wi_all,) = _exchange_call(AllGatherWeights([wi_blk[0]]), "allgather_weights_0")

    ln_g = gmlp_ln_g.reshape(depth, 1, A_WIDTH)
    ln_b = gmlp_ln_b.reshape(depth, 1, A_WIDTH)
    bs_t = jnp.pad(jnp.transpose(gmlp_b_s, (0, 2, 1)), ((0, 0), (0, 0), (0, LANES - A_GROUPS)))
    lb0, lb1 = hgrn_lb[0:1], hgrn_lb[1:2]
    onorm = jnp.tile(hgrn_onorm_g, (1, B_HEADS)).reshape(depth, 1, B_WIDTH)
    bf_row = jnp.pad(fox_b_f, ((0, 0), (0, LANES - C_HEADS))).reshape(depth, 1, LANES)

    core = lax.axis_index("c").astype(jnp.int32).reshape(1)
    chip = (2 * lax.axis_index("x") + lax.axis_index("y")).astype(jnp.int32).reshape(1)

    saved = []
    xc = xs
    for l in range(depth):
        wi_int = assemble_w_in(wi_all[:, None])
        proj, qkv, h = inproj(xc, norm_g[l:l + 1], wi_int, 0)
        ya = gmlp_fwd(proj, ln_g[l], ln_b[l], gmlp_w_s[l], bs_t[l])
        yb, states = hgrn_fwd(proj, lb0, lb1, onorm[l], l)
        ka, va, vt, kt, qt, qa = fox_prep(proj, qkv, bf_row[l])
        ride = ([wo_blk] if l == 0 else []) + ([wi_blk[l + 1]] if l + 1 < depth else [])
        o, lse, *gathered = fox_fwd(qt, ka, vt, AllGatherWeights(ride) if ride else None)
        if l == 0:
            wo_all = gathered.pop(0)
        if gathered:
            (wi_all,) = gathered
        x_in = xc
        if l + 1 < depth:
            xc, yfull = outproj(x_in, ya, yb, o, proj, wo_all, l)
        else:
            dx, yfull, d_final_g, loss_tile = outproj(x_in, ya, yb, o, proj, wo_all, l, (final_norm_g[None], tgt))
        saved.append((x_in, proj, h, states, ka, va, kt, qt, qa, o, lse, yfull, wi_int))

    n_shard = w_in.shape[2]
    g_norm = [None] * depth
    g_ln_g, g_ln_b, g_ws, g_bs, g_on, g_bf = ([None] * depth for _ in range(6))
    g_lb0, g_lb1 = jnp.zeros_like(lb0), jnp.zeros_like(lb1)
    swi = swo = rwi = rwo = None
    for l in reversed(range(depth)):
        x_in, proj, h, states, ka, va, kt, qt, qa, o, lse, yfull, wi_int = saved[l]
        dy, gwo = outproj_bwd(dx, yfull, wo_all, l)
        dproj, g_ln_g[l], g_ln_b[l], g_ws[l], dbs_t = gmlp_bwd(proj, dy, ln_g[l], ln_b[l], gmlp_w_s[l], bs_t[l])
        g_bs[l] = dbs_t[:, :A_GROUPS].T
        if l > 0:
            (qwo,) = _exchange_call(PairExchange([gwo]), f"pair_exchange_w_out_{l}")
        else:
            gws = jnp.stack(g_ws).reshape(-1, LANES)
            qwo, qws = _exchange_call(PairExchange([gwo], [gws]), f"pair_exchange_w_out_{l}")
            sws = small_sum(gws, qws, "pair_sum_w_s")
        swo = pair_sum(gwo, qwo, BF16, gwo.shape[2], "pair_sum_w_out", core, l, depth, swo)
        dproj, d0, d1, don = hgrn_bwd(proj, states, dy, lb0, lb1, onorm[l], l, dproj)
        g_lb0, g_lb1 = g_lb0 + d0, g_lb1 + d1
        g_on[l] = don.reshape(B_HEADS, B_KDIM).sum(0)
        dob, dproj, dot_t = fox_bwd_prep(dy, o, proj, dproj)
        top = l == depth - 1
        ride = ChipExchange([swo] if top else [swi, swo], [(l,)] if top else [(l + 1,), (l,)],
                            [sws] if l == 0 else [], [rwo] if top else [rwi, rwo])
        outs = fox_bwd(ka, va, kt, qt, dot_t, qa, dob, lse, ride)
        dqkv, (dck, dcq), got = outs[:3], outs[3:5], list(outs[5:])
        if not top:
            rwi = got.pop(0)
        rwo = got.pop(0)
        if l == 0:
            (rws,) = got
        dproj, dbf = fox_post(dcq, dck, proj, bf_row[l], dproj)
        g_bf[l] = dbf[0, :C_HEADS]
        gwi, for_sibling = split_w_in_grad(inproj_bwd_w(h, dproj, dqkv), n_shard, core)
        (qwi,) = _exchange_call(PairExchange([], [for_sibling]), f"pair_exchange_w_in_{l}")
        swi = pair_sum(gwi, qwi, BF16, 256, "pair_sum_w_in", core, l, depth, swi)
        ride = ChipExchange([swi], [(l,)], stacked=[rwi]) if l == 0 else None
        outs = inproj_bwd_x(dproj, dqkv, wi_int, x_in, norm_g[l:l + 1], dx, 0, ride)
        dx, g_norm[l] = outs[:2]
        if ride is not None:
            (rwi,) = outs[2:]

    gsm = _pack_small([
        jnp.concatenate(g_norm), jnp.stack(g_ln_g), jnp.stack(g_ln_b), jnp.stack(g_bs),
        jnp.concatenate([g_lb0, g_lb1]), jnp.stack(g_on), jnp.stack(g_bf), d_final_g, loss_tile[0, 0]])
    (qsm,) = _exchange_call(PairExchange([], [gsm]), "pair_exchange_small")
    ssm = small_sum(gsm, qsm, "pair_sum_small")
    (rsm,) = _exchange_call(ChipExchange(gathered=[ssm]), "chip_exchange_small")

    small_w = (norm_g, gmlp_ln_g, gmlp_ln_b, gmlp_b_s, hgrn_lb, hgrn_onorm_g, fox_b_f, final_norm_g)
    small_m = (m_norm_g, m_gmlp_ln_g, m_gmlp_ln_b, m_gmlp_b_s, m_hgrn_lb, m_hgrn_onorm_g, m_fox_b_f, m_final_norm_g)
    small_v = (v_norm_g, v_gmlp_ln_g, v_gmlp_ln_b, v_gmlp_b_s, v_hgrn_lb, v_hgrn_onorm_g, v_fox_b_f, v_final_norm_g)
    res_wi = adam_reduce_columns(rwi, w_in, m_w_in, v_w_in, "adam_w_in", swi, chip)
    res_wo = adam_reduce(rwo, w_out, m_w_out, v_w_out, w_out.shape[1], "adam_w_out", own=swo, chip=chip)
    grads = _unpack_small(sum_parts(rsm, "sum_small"))
    names = [name for name, _ in _SMALL if name != "loss"]
    rows = lambda a: a.reshape(1, -1) if a.ndim == 1 else a
    res_sm = adam_small([rows(grads[k]) for k in names], *([rows(a) for a in wmv] for wmv in (small_w, small_m, small_v)))
    res_sm = [grads] + [{k: a.reshape(grads[k].shape) for k, a in zip(names, r, strict=True)} for r in res_sm]
    as_rows = lambda a: a.reshape(1, -1, LANES)
    res_ws = adam_reduce(rws[:, None], as_rows(gmlp_w_s), as_rows(m_gmlp_w_s), as_rows(v_gmlp_w_s), rws.shape[1], "adam_w_s")
    for s, r in zip(res_sm, res_ws, strict=True):
        s["gmlp_w_s"] = r.reshape(gmlp_w_s.shape)

    def group(i):
        s = res_sm[i]
        return [s["norm_g"], res_wi[i], res_wo[i], s["gmlp_ln_g"], s["gmlp_ln_b"], s["gmlp_w_s"], s["gmlp_b_s"],
                s["hgrn_lb"], s["hgrn_onorm_g"], s["fox_b_f"], s["final_norm_g"]]

    return (res_sm[0]["loss"], dx[None], *group(0), *group(1), *group(2), *group(3))
```

```python
import functools

import jax
import jax.numpy as jnp
import numpy as np
from jax import lax
from jax.experimental import pallas as pl
from jax.experimental.pallas import tpu as pltpu

F32 = jnp.float32
BF16 = jnp.bfloat16

NORM_EPS = 1e-6
F_FLOOR = 1e-30
CHUNK = 128
LANES = 128
VMEM_LIMIT = 56 * 1024 * 1024


def _cparams(*sem):
    return pltpu.CompilerParams(dimension_semantics=sem, vmem_limit_bytes=VMEM_LIMIT)


def _dot(a, b, dims=(((1,), (0,)), ((), ())), precision=None):
    return lax.dot_general(a, b, dims, precision=precision, preferred_element_type=F32)


_NT = (((1,), (1,)), ((), ()))
_TN = (((0,), (0,)), ((), ()))


def _bf16_pieces(x, n):
    out, r = [], x
    for i in range(n):
        out.append(r.astype(BF16))
        if i + 1 < n:
            r = r - out[-1].astype(F32)
    return out


@functools.partial(jax.custom_vjp, nondiff_argnums=(2,))
def _times_exact(x, e, n):
    return functools.reduce(jnp.add, [_dot(p, e) for p in _bf16_pieces(x, n)])


def _times_exact_fwd(x, e, n):
    return _times_exact(x, e, n), e


def _times_exact_bwd(n, e, g):
    dx = functools.reduce(jnp.add, [lax.dot_general(p, e, _NT, preferred_element_type=F32) for p in _bf16_pieces(g, n)])
    return dx, jnp.zeros_like(e)


_times_exact.defvjp(_times_exact_fwd, _times_exact_bwd)


@functools.partial(jax.custom_vjp, nondiff_argnums=(2,))
def _exact_times(e, x, n):
    return functools.reduce(jnp.add, [_dot(e, p) for p in _bf16_pieces(x, n)])


def _exact_times_fwd(e, x, n):
    return _exact_times(e, x, n), e


def _exact_times_bwd(n, e, g):
    dx = functools.reduce(jnp.add, [lax.dot_general(e, p, _TN, preferred_element_type=F32) for p in _bf16_pieces(g, n)])
    return jnp.zeros_like(e), dx


_exact_times.defvjp(_exact_times_fwd, _exact_times_bwd)


def _group_mean_matrix(width, group):
    idx = np.arange(width) // group
    return jnp.asarray((idx[:, None] == idx[None, :]).astype(np.float32) / group, BF16)


def _group_ones_matrix(width, group):
    idx = np.arange(width) // group
    return jnp.asarray((idx[:, None] == idx[None, :]).astype(np.float32), BF16)


A_WIDTH = 256
A_GROUPS = 4
A_GDIM = 64


A_ROWS = 512


def _gmlp_chunk(x3, ln_g, ln_b, w_s, bs_t, mean_m, gind):
    n = x3.shape[0] // CHUNK
    u = jax.nn.gelu(x3[:, :A_WIDTH])
    v = jax.nn.gelu(x3[:, A_WIDTH:2 * A_WIDTH])
    z = x3[:, 2 * A_WIDTH:]
    mu = _times_exact(v, mean_m, 2)
    d = v - mu
    var = _times_exact(d * d, mean_m, 2)
    vn = d * lax.rsqrt(var + NORM_EPS) * ln_g + ln_b
    vnb = vn.astype(BF16)
    wide = jnp.concatenate([vnb[i * CHUNK:(i + 1) * CHUNK] for i in range(n)], axis=1)
    row = lax.broadcasted_iota(jnp.int32, (CHUNK, CHUNK), 0)
    col = lax.broadcasted_iota(jnp.int32, (CHUNK, CHUNK), 1)
    causal = row >= col
    lane_g = lax.shift_right_logical(lax.broadcasted_iota(jnp.int32, (CHUNK, n * A_WIDTH), 1), 6) & (A_GROUPS - 1)
    bias = _times_exact(bs_t, gind, 3)
    mixed = jnp.concatenate([bias] * n, axis=1)
    for g in range(A_GROUPS):
        wc = jnp.where(causal, w_s[g], 0.0).astype(BF16)
        mixed = mixed + jnp.where(lane_g == g, _dot(wc, wide), 0.0)
    mixed = jnp.concatenate([mixed[:, i * A_WIDTH:(i + 1) * A_WIDTH] for i in range(n)], axis=0)
    return u * mixed * jax.nn.silu(z)


def _gmlp_consts():
    gind = np.zeros((LANES, A_WIDTH), np.float32)
    for g in range(A_GROUPS):
        gind[g, g * A_GDIM:(g + 1) * A_GDIM] = 1.0
    return _group_mean_matrix(A_WIDTH, A_GDIM), jnp.asarray(gind, BF16)


def _full(shape):
    return pl.BlockSpec(shape, lambda *_: (0,) * len(shape))


def gmlp_fwd(proj, ln_g, ln_b, w_s, bs_t):
    seq = proj.shape[0]
    rows = min(A_ROWS, seq)
    mean_m, gind = _gmlp_consts()

    def body(x_ref, g_ref, b_ref, w_ref, bs_ref, m_ref, gi_ref, y_ref):
        y = _gmlp_chunk(x_ref[...], g_ref[...], b_ref[...], w_ref[...], bs_ref[...], m_ref[...], gi_ref[...])
        y_ref[...] = y.astype(BF16)

    return pl.pallas_call(
        body,
        name="gmlp_fwd",
        grid=(seq // rows,),
        in_specs=[
            pl.BlockSpec((rows, 3 * A_WIDTH), lambda n: (n, 0)),
            _full((1, A_WIDTH)), _full((1, A_WIDTH)), _full((A_GROUPS, CHUNK, CHUNK)), _full((CHUNK, LANES)),
            _full((A_WIDTH, A_WIDTH)), _full((LANES, A_WIDTH)),
        ],
        out_specs=pl.BlockSpec((rows, A_WIDTH), lambda n: (n, 0)),
        out_shape=jax.ShapeDtypeStruct((seq, A_WIDTH), BF16),
        compiler_params=_cparams("parallel"),
    )(proj, ln_g, ln_b, w_s, bs_t, mean_m, gind)


def gmlp_bwd(proj, dy, ln_g, ln_b, w_s, bs_t):
    seq = proj.shape[0]
    rows = min(A_ROWS, seq)
    mean_m, gind = _gmlp_consts()

    def body(x_ref, dy_ref, g_ref, b_ref, w_ref, bs_ref, m_ref, gi_ref, dx_ref, dg_ref, db_ref, dw_ref, dbs_ref):
        fn = functools.partial(_gmlp_chunk, mean_m=m_ref[...], gind=gi_ref[...])
        _, vjp = jax.vjp(fn, x_ref[...], g_ref[...], b_ref[...], w_ref[...], bs_ref[...])
        dx, dg, db, dw, dbs = vjp(dy_ref[...])
        dx_ref[...] = dx.astype(BF16)

        @pl.when(pl.program_id(0) == 0)
        def _():
            dg_ref[...] = jnp.zeros_like(dg_ref)
            db_ref[...] = jnp.zeros_like(db_ref)
            dw_ref[...] = jnp.zeros_like(dw_ref)
            dbs_ref[...] = jnp.zeros_like(dbs_ref)

        dg_ref[...] += dg
        db_ref[...] += db
        dw_ref[...] += dw
        dbs_ref[...] += dbs

    return pl.pallas_call(
        body,
        name="gmlp_bwd",
        grid=(seq // rows,),
        in_specs=[
            pl.BlockSpec((rows, 3 * A_WIDTH), lambda n: (n, 0)),
            pl.BlockSpec((rows, A_WIDTH), lambda n: (n, 0)),
            _full((1, A_WIDTH)), _full((1, A_WIDTH)), _full((A_GROUPS, CHUNK, CHUNK)), _full((CHUNK, LANES)),
            _full((A_WIDTH, A_WIDTH)), _full((LANES, A_WIDTH)),
        ],
        out_specs=[
            pl.BlockSpec((rows, 3 * A_WIDTH), lambda n: (n, 0)),
            _full((1, A_WIDTH)), _full((1, A_WIDTH)), _full((A_GROUPS, CHUNK, CHUNK)), _full((CHUNK, LANES)),
        ],
        out_shape=[
            jax.ShapeDtypeStruct((seq, D_INT), BF16),
            jax.ShapeDtypeStruct((1, A_WIDTH), F32), jax.ShapeDtypeStruct((1, A_WIDTH), F32),
            jax.ShapeDtypeStruct((A_GROUPS, CHUNK, CHUNK), F32), jax.ShapeDtypeStruct((CHUNK, LANES), F32),
        ],
        compiler_params=_cparams("arbitrary"),
    )(proj, dy, ln_g, ln_b, w_s, bs_t, mean_m, gind)


B_WIDTH = 256
B_HEADS = 4
B_KDIM = 64
B_LEVELS = (64, 32, 16, 8, 4, 2, 1)


def _hgrn_consts():
    t = np.arange(CHUNK)
    u = t[None, :]
    mats = [np.tril(np.ones((CHUNK, CHUNK), np.float32))]
    for m in B_LEVELS:
        p = (t // (2 * m)) * (2 * m) + m - 1
        right = (t % (2 * m)) >= m
        sel = np.where(right[:, None], (u > p[:, None]) & (u <= t[:, None]), (u > t[:, None]) & (u <= p[:, None]))
        mats.append(sel.astype(np.float32))
    return jnp.asarray(np.concatenate(mats, 0), BF16), _group_ones_matrix(B_WIDTH, B_KDIM)


def _hgrn_lower_bound(lb0, lb1, layer):
    mx = jnp.maximum(lb0, lb1)
    e0 = jnp.exp(lb0 - mx)
    e1 = jnp.exp(lb1 - mx)
    p0 = e0 / (e0 + e1)
    p1 = e1 / (e0 + e1)
    cs = p0 if layer == 0 else p0 + p1
    return jnp.clip(cs - p0, 0.0, 1.0 - 1e-6)


def _hgrn_chunk(x4, st, lb0, lb1, onorm, layer, tstack, ones_bd):
    q_raw, fl, v, zg = (x4[:, i * B_WIDTH:(i + 1) * B_WIDTH] for i in range(4))
    lb = _hgrn_lower_bound(lb0, lb1, layer)
    q = jax.nn.silu(q_raw) * (B_KDIM ** -0.5)
    f = lb + (1.0 - lb) * jax.nn.sigmoid(fl)
    logf = jnp.log(jnp.maximum(f, F_FLOOR))
    k = (1.0 - lb) * jax.nn.sigmoid(-fl)
    b = _exact_times(tstack[:CHUNK], logf, 3)
    dall = jnp.concatenate([b, _exact_times(tstack[CHUNK:], logf, 2)], axis=0)
    b_last = jnp.sum(logf, axis=0, keepdims=True)
    vb = v.astype(BF16)

    lane_h = lax.shift_right_logical(lax.broadcasted_iota(jnp.int32, (CHUNK, B_WIDTH), 1), 6)
    row = lax.broadcasted_iota(jnp.int32, (CHUNK, B_WIDTH), 0)
    srow = lax.broadcasted_iota(jnp.int32, (B_HEADS * CHUNK, CHUNK), 0) & (CHUNK - 1)
    scol = lax.broadcasted_iota(jnp.int32, (B_HEADS * CHUNK, CHUNK), 1)

    def heads_on_rows(a):
        return jnp.concatenate([jnp.where(lane_h == h, a, 0.0) for h in range(B_HEADS)], axis=0)

    def heads_from_rows(r):
        out = jnp.where(lane_h == 0, r[:CHUNK], 0.0)
        for h in range(1, B_HEADS):
            out = out + jnp.where(lane_h == h, r[h * CHUNK:(h + 1) * CHUNK], 0.0)
        return out

    o = lax.dot_general((q * jnp.exp(b)).astype(BF16), st.astype(BF16), _NT, preferred_element_type=F32)
    scores = jnp.zeros((B_HEADS * CHUNK, CHUNK), F32)
    for li, m in enumerate(B_LEVELS):
        e = jnp.exp(dall[(li + 1) * CHUNK:(li + 2) * CHUNK])
        right = (row & (2 * m - 1)) >= m
        qt = jnp.where(right, q * e, 0.0)
        kt = jnp.where(right, 0.0, k * e)
        sc = lax.dot_general(heads_on_rows(qt).astype(BF16), kt.astype(BF16), _NT, preferred_element_type=F32)
        sh = int(np.log2(2 * m))
        same = lax.shift_right_logical(srow, sh) == lax.shift_right_logical(scol, sh)
        scores = scores + jnp.where(same, sc, 0.0)
    o = o + heads_from_rows(_dot(scores.astype(BF16), vb))
    o = o + _times_exact(q * k, ones_bd, 2) * v

    kv = lax.dot_general(vb, (k * jnp.exp(b_last - b)).astype(BF16), _TN, preferred_element_type=F32)
    st_new = st * jnp.exp(b_last) + jnp.where(ones_bd > 0.5, kv, 0.0)

    ms = _times_exact(o * o, ones_bd, 2) * (1.0 / B_KDIM)
    y = o * lax.rsqrt(ms + NORM_EPS) * onorm * jax.nn.silu(zg)
    return y, st_new


B_ROWS = 256


def _hgrn_rows(x4, st, lb0, lb1, onorm, layer, tstack, ones_bd):
    ys = []
    for i in range(x4.shape[0] // CHUNK):
        y, st = _hgrn_chunk(x4[i * CHUNK:(i + 1) * CHUNK], st, lb0, lb1, onorm, layer, tstack, ones_bd)
        ys.append(y)
    return jnp.concatenate(ys, axis=0), st


def hgrn_fwd(proj, lb0, lb1, onorm, layer):
    seq = proj.shape[0]
    rows = min(B_ROWS, seq)
    nc = seq // rows
    tstack, ones_bd = _hgrn_consts()

    def body(x_ref, lb0_ref, lb1_ref, on_ref, t_ref, e_ref, y_ref, st_out_ref, st_ref):
        @pl.when(pl.program_id(0) == 0)
        def _():
            st_ref[...] = jnp.zeros_like(st_ref)

        st = st_ref[...]
        st_out_ref[0] = st
        y, st_new = _hgrn_rows(x_ref[...], st, lb0_ref[...], lb1_ref[...], on_ref[...], layer, t_ref[...], e_ref[...])
        y_ref[...] = y.astype(BF16)
        st_ref[...] = st_new

    return pl.pallas_call(
        body,
        name=f"hgrn_fwd_{layer}",
        grid=(nc,),
        in_specs=[
            pl.BlockSpec((rows, 4 * B_WIDTH), lambda n: (n, 1)),
            _full((1, B_WIDTH)), _full((1, B_WIDTH)), _full((1, B_WIDTH)),
            _full(((len(B_LEVELS) + 1) * CHUNK, CHUNK)), _full((B_WIDTH, B_WIDTH)),
        ],
        out_specs=[
            pl.BlockSpec((rows, B_WIDTH), lambda n: (n, 0)),
            pl.BlockSpec((1, B_WIDTH, B_WIDTH), lambda n: (n, 0, 0)),
        ],
        out_shape=[jax.ShapeDtypeStruct((seq, B_WIDTH), BF16), jax.ShapeDtypeStruct((nc, B_WIDTH, B_WIDTH), F32)],
        scratch_shapes=[pltpu.VMEM((B_WIDTH, B_WIDTH), F32)],
        compiler_params=_cparams("arbitrary"),
    )(proj, lb0, lb1, onorm, tstack, ones_bd)


def hgrn_bwd(proj, states, dy, lb0, lb1, onorm, layer, dproj):
    seq = proj.shape[0]
    rows = min(B_ROWS, seq)
    nc = seq // rows
    tstack, ones_bd = _hgrn_consts()

    def body(x_ref, st_in_ref, dy_ref, lb0_ref, lb1_ref, on_ref, t_ref, e_ref, _, dx_ref, d0_ref, d1_ref, don_ref, dst_ref):
        @pl.when(pl.program_id(0) == 0)
        def _():
            dst_ref[...] = jnp.zeros_like(dst_ref)
            d0_ref[...] = jnp.zeros_like(d0_ref)
            d1_ref[...] = jnp.zeros_like(d1_ref)
            don_ref[...] = jnp.zeros_like(don_ref)

        fn = functools.partial(_hgrn_rows, layer=layer, tstack=t_ref[...], ones_bd=e_ref[...])
        _, vjp = jax.vjp(fn, x_ref[...], st_in_ref[0], lb0_ref[...], lb1_ref[...], on_ref[...])
        dx, dst, d0, d1, don = vjp((dy_ref[...], dst_ref[...]))
        dx_ref[...] = dx.astype(BF16)
        dst_ref[...] = dst
        d0_ref[...] += d0
        d1_ref[...] += d1
        don_ref[...] += don

    rev = lambda n: nc - 1 - n
    return pl.pallas_call(
        body,
        name=f"hgrn_bwd_{layer}",
        grid=(nc,),
        in_specs=[
            pl.BlockSpec((rows, 4 * B_WIDTH), lambda n: (rev(n), 1)),
            pl.BlockSpec((1, B_WIDTH, B_WIDTH), lambda n: (rev(n), 0, 0)),
            pl.BlockSpec((rows, B_WIDTH), lambda n: (rev(n), 1)),
            _full((1, B_WIDTH)), _full((1, B_WIDTH)), _full((1, B_WIDTH)),
            _full(((len(B_LEVELS) + 1) * CHUNK, CHUNK)), _full((B_WIDTH, B_WIDTH)), _ANY,
        ],
        out_specs=[
            pl.BlockSpec((rows, 4 * B_WIDTH), lambda n: (rev(n), 1)),
            _full((1, B_WIDTH)), _full((1, B_WIDTH)), _full((1, B_WIDTH)),
        ],
        out_shape=[jax.ShapeDtypeStruct(dproj.shape, BF16)] + [jax.ShapeDtypeStruct((1, B_WIDTH), F32)] * 3,
        input_output_aliases={8: 0},
        scratch_shapes=[pltpu.VMEM((B_WIDTH, B_WIDTH), F32)],
        compiler_params=_cparams("arbitrary"),
    )(proj, states, dy, lb0, lb1, onorm, tstack, ones_bd, dproj)


D_MODEL = 1024
D_INT = 4096


def _rms_stats(xf):
    r = lax.rsqrt(jnp.mean(xf * xf, axis=-1, keepdims=True) + NORM_EPS)
    return r, xf * r


def _rms_bwd(dy, g, r, xh):
    u = dy * g
    return r * (u - xh * jnp.mean(u * xh, axis=-1, keepdims=True))


C_QKV = (2048, 3584)
P_WIDTH = D_INT - (C_QKV[1] - C_QKV[0])
P_Z_BLOCK = C_QKV[0] // 512


def inproj(x, g, w, layer):
    seq = x.shape[0]
    tm = min(seq, 512)

    def body(x_ref, g_ref, w_ref, p_ref, qkv_ref, h_ref):
        _, xh = _rms_stats(x_ref[...])
        h = (xh * g_ref[...]).astype(BF16)
        h_ref[...] = h
        p_ref[:, :C_QKV[0]] = _dot(h, w_ref[0, :, :C_QKV[0]])
        qkv_ref[...] = _dot(h, w_ref[0, :, C_QKV[0]:C_QKV[1]]).astype(BF16)
        p_ref[:, C_QKV[0]:] = _dot(h, w_ref[0, :, C_QKV[1]:])

    rows = lambda n: pl.BlockSpec((tm, n), lambda i: (i, 0))
    return pl.pallas_call(
        body,
        name="inproj",
        grid=(seq // tm,),
        in_specs=[rows(D_MODEL), _full((1, D_MODEL)), pl.BlockSpec((1, D_MODEL, D_INT), lambda i: (layer, 0, 0))],
        out_specs=[rows(P_WIDTH), rows(C_QKV[1] - C_QKV[0]), rows(D_MODEL)],
        out_shape=[jax.ShapeDtypeStruct((seq, P_WIDTH), F32), jax.ShapeDtypeStruct((seq, C_QKV[1] - C_QKV[0]), BF16),
                   jax.ShapeDtypeStruct((seq, D_MODEL), BF16)],
        compiler_params=_cparams("parallel"),
    )(x, g, w)


def outproj(x, ya, yb, o, proj, wo, layer, head=None):
    seq = x.shape[0]
    tm = min(seq, 512)
    blk = wo.shape[2]

    def body(x_ref, ya_ref, yb_ref, o_ref, z_ref, w_ref, *refs):
        yc = (o_ref[...] * jax.nn.silu(z_ref[...])).astype(BF16)
        y = jnp.concatenate([ya_ref[...], yb_ref[...], yc], axis=1)
        w = jnp.concatenate([w_ref[d, 0] for d in range(N_DEV)], axis=0)
        xn = x_ref[...] + _dot(y, w)
        if head is None:
            xn_ref, y_ref = refs
            xn_ref[...] = xn
        else:
            g_ref, t_ref, dx_ref, y_ref, dg_ref, loss_ref = refs

            @pl.when(pl.program_id(0) == 0)
            def _():
                dg_ref[...] = jnp.zeros_like(dg_ref)
                loss_ref[...] = jnp.zeros_like(loss_ref)

            g = g_ref[...]
            r, xh = _rms_stats(xn)
            err = xh * g - t_ref[...]
            sq = jnp.sum(jnp.sum(err * err, axis=1, keepdims=True), axis=0, keepdims=True)
            loss_ref[...] += jnp.broadcast_to(sq * (0.5 / D_MODEL), loss_ref.shape)
            dout = err * (1.0 / D_MODEL)
            dg_ref[...] += jnp.sum(dout * xh, axis=0, keepdims=True)
            dx_ref[...] = _rms_bwd(dout, g, r, xh)
        y_ref[...] = y

    rows = lambda: pl.BlockSpec((tm, D_MODEL), lambda i: (i, 0))
    tail = (() if head is None else (_full((1, D_MODEL)), rows()),
            () if head is None else (_full((1, D_MODEL)), _full((8, LANES))),
            () if head is None else (jax.ShapeDtypeStruct((1, D_MODEL), F32), jax.ShapeDtypeStruct((8, LANES), F32)))
    return pl.pallas_call(
        body,
        name="outproj" if head is None else "outproj_loss",
        grid=(seq // tm,),
        in_specs=[
            rows(),
            pl.BlockSpec((tm, 256), lambda i: (i, 0)),
            pl.BlockSpec((tm, 256), lambda i: (i, 0)),
            pl.BlockSpec((tm, 512), lambda i: (i, 0)),
            pl.BlockSpec((tm, 512), lambda i: (i, P_Z_BLOCK)),
            pl.BlockSpec((N_DEV, 1, blk, D_MODEL), lambda i: (0, layer, 0, 0)),
            *tail[0],
        ],
        out_specs=[rows(), rows(), *tail[1]],
        out_shape=[jax.ShapeDtypeStruct((seq, D_MODEL), F32), jax.ShapeDtypeStruct((seq, D_MODEL), BF16), *tail[2]],
        compiler_params=_cparams("parallel" if head is None else "arbitrary"),
    )(x, ya, yb, o, proj, wo, *(head or ()))


def outproj_bwd(dx, y, wo, layer):
    seq = dx.shape[0]
    ts = min(seq, 512)
    blk = wo.shape[2]

    def body(dx_ref, y_ref, w_ref, dy_ref, dw_ref):
        @pl.when(pl.program_id(0) == 0)
        def _():
            dw_ref[...] = jnp.zeros_like(dw_ref)

        dxb = dx_ref[...].astype(BF16)
        w = jnp.concatenate([w_ref[d, 0] for d in range(N_DEV)], axis=0)
        dy_ref[...] = lax.dot_general(dxb, w, _NT, preferred_element_type=F32)
        dw = lax.dot_general(y_ref[...], dxb, _TN, preferred_element_type=F32)
        for d in range(N_DEV):
            dw_ref[d % 2, d // 2] += dw[d * blk:(d + 1) * blk]

    return pl.pallas_call(
        body,
        name="outproj_bwd",
        grid=(seq // ts,),
        in_specs=[
            pl.BlockSpec((ts, D_MODEL), lambda i: (i, 0)),
            pl.BlockSpec((ts, D_MODEL), lambda i: (i, 0)),
            pl.BlockSpec((N_DEV, 1, blk, D_MODEL), lambda i: (0, layer, 0, 0)),
        ],
        out_specs=[pl.BlockSpec((ts, D_MODEL), lambda i: (i, 0)),
                   pl.BlockSpec((2, N_CHIP, blk, D_MODEL), lambda i: (0, 0, 0, 0))],
        out_shape=[jax.ShapeDtypeStruct((seq, D_MODEL), F32), jax.ShapeDtypeStruct((2, N_CHIP, blk, D_MODEL), F32)],
        compiler_params=_cparams("arbitrary"),
    )(dx, y, wo)


def _dproj_parts(dp_ref, dqkv_refs, rows):
    lo, hi = C_QKV
    step = (hi - lo) // len(dqkv_refs)
    return ([(0, dp_ref.at[rows, 0:lo])] + [(lo + i * step, r.at[rows, :]) for i, r in enumerate(dqkv_refs)]
            + [(hi, dp_ref.at[rows, hi:D_INT])])


def inproj_bwd_x(dproj, dqkv, w, x, g, dx_in, layer, carried=None):
    seq = x.shape[0]
    tm = min(seq, 512)

    def body(dp_ref, dq_ref, dk_ref, dv_ref, w_ref, x_ref, g_ref, dxin_ref, dx_ref, dg_ref):
        @pl.when(pl.program_id(0) == 0)
        def _():
            dg_ref[...] = jnp.zeros_like(dg_ref)

        dh = None
        for at, part in _dproj_parts(dp_ref, (dq_ref, dk_ref, dv_ref), slice(None)):
            term = lax.dot_general(part[...], w_ref[0, :, at:at + part.shape[1]], _NT, preferred_element_type=F32)
            dh = term if dh is None else dh + term
        r, xh = _rms_stats(x_ref[...])
        dg_ref[...] += jnp.sum(dh * xh, axis=0, keepdims=True)
        dx_ref[...] = dxin_ref[...] + _rms_bwd(dh, g_ref[...], r, xh)

    third = lambda: pl.BlockSpec((tm, C_WIDTH), lambda i: (i, 0))
    return _call_carrying(
        carried, body, (dproj, *dqkv, w, x, g, dx_in),
        name="inproj_bwd_x",
        grid=(seq // tm,),
        in_specs=[
            pl.BlockSpec((tm, D_INT), lambda i: (i, 0)), third(), third(), third(),
            pl.BlockSpec((1, D_MODEL, D_INT), lambda i: (layer, 0, 0)),
            pl.BlockSpec((tm, D_MODEL), lambda i: (i, 0)),
            _full((1, D_MODEL)),
            pl.BlockSpec((tm, D_MODEL), lambda i: (i, 0)),
        ],
        out_specs=[pl.BlockSpec((tm, D_MODEL), lambda i: (i, 0)), _full((1, D_MODEL))],
        out_shape=[jax.ShapeDtypeStruct((seq, D_MODEL), F32), jax.ShapeDtypeStruct((1, D_MODEL), F32)],
        scratch_shapes=[], semantics=("arbitrary",),
    )


def inproj_bwd_w(h, dproj, dqkv):
    seq = h.shape[0]
    ts, tn = min(seq, 512), 512

    def body(h_ref, dp_ref, dq_ref, dk_ref, dv_ref, dw_ref):
        @pl.when(pl.program_id(0) == 0)
        def _():
            dw_ref[...] = jnp.zeros_like(dw_ref)

        ht = h_ref[...].T
        for at, part in _dproj_parts(dp_ref, (dq_ref, dk_ref, dv_ref), slice(None)):
            for c in range(0, part.shape[1], tn):
                dw_ref[0, :, at + c:at + c + tn] += _dot(ht, part[:, c:c + tn])

    third = lambda: pl.BlockSpec((ts, C_WIDTH), lambda s: (s, 0))
    return pl.pallas_call(
        body,
        name="inproj_bwd_w",
        grid=(seq // ts,),
        in_specs=[pl.BlockSpec((ts, D_MODEL), lambda s: (s, 0)), pl.BlockSpec((ts, D_INT), lambda s: (s, 0)),
                  third(), third(), third()],
        out_specs=_full((1, D_MODEL, D_INT)),
        out_shape=jax.ShapeDtypeStruct((1, D_MODEL, D_INT), F32),
        compiler_params=_cparams("arbitrary"),
    )(h, dproj, *dqkv)


N_IN = 3848


def _internal_of(col):
    return col if col < 768 else (col + 256 if col < 3840 else 768 + col - 3840)


def _column_runs(n_shard):
    runs = []
    for d in range(N_IN // n_shard):
        mine = []
        for j in range(n_shard):
            ci = _internal_of(d * n_shard + j)
            if mine and mine[-1][0] + mine[-1][1] == ci:
                mine[-1][1] += 1
            else:
                mine.append([ci, 1, j])
        runs.append(mine)
    return runs


def assemble_w_in(wi_all):
    n_dev, depth, _, n_shard = wi_all.shape
    tr = 256
    pieces = [[] for _ in range(D_INT // LANES)]
    for d, mine in enumerate(_column_runs(n_shard)):
        for ci, ln, off in mine:
            while ln > 0:
                blk, at = divmod(ci, LANES)
                take = min(ln, LANES - at)
                pieces[blk].append((at, take, d, off))
                ci, ln, off = ci + take, ln - take, off + take

    def body(x_ref, o_ref):
        for blk, parts in enumerate(pieces):
            vals, at = [], 0
            for start, ln, d, off in sorted(parts):
                if start > at:
                    vals.append(jnp.zeros((tr, start - at), BF16))
                vals.append(x_ref[d, 0, :, off:off + ln])
                at = start + ln
            if at < LANES:
                vals.append(jnp.zeros((tr, LANES - at), BF16))
            o_ref[0, :, blk * LANES:(blk + 1) * LANES] = vals[0] if len(vals) == 1 else jnp.concatenate(vals, axis=1)

    return pl.pallas_call(
        body,
        name="assemble_w_in",
        grid=(depth, D_MODEL // tr),
        in_specs=[pl.BlockSpec((n_dev, 1, tr, n_shard), lambda l, r: (0, l, r, 0))],
        out_specs=pl.BlockSpec((1, tr, D_INT), lambda l, r: (l, r, 0)),
        out_shape=jax.ShapeDtypeStruct((depth, D_MODEL, D_INT), BF16),
        compiler_params=_cparams("parallel", "parallel"),
    )(wi_all)


def split_w_in_grad(dwi, n_shard, core):
    tr = 256
    runs = _column_runs(n_shard)

    def body(core_ref, x_ref, keep_ref, send_ref):
        for d, mine in enumerate(runs):
            @pl.when(core_ref[0] == d % 2)
            def _():
                for ci, ln, off in mine:
                    keep_ref[d // 2, :, off:off + ln] = x_ref[0, :, ci:ci + ln]

            @pl.when(core_ref[0] != d % 2)
            def _():
                for ci, ln, off in mine:
                    send_ref[d // 2, :, off:off + ln] = x_ref[0, :, ci:ci + ln].astype(BF16)

    shards = lambda: pl.BlockSpec((N_CHIP, tr, n_shard), lambda r, s: (0, r, 0))
    grid_spec = pltpu.PrefetchScalarGridSpec(
        num_scalar_prefetch=1, grid=(D_MODEL // tr,),
        in_specs=[pl.BlockSpec((1, tr, D_INT), lambda r, s: (0, r, 0))], out_specs=[shards(), shards()])
    return pl.pallas_call(
        body,
        name="split_w_in_grad",
        grid_spec=grid_spec,
        out_shape=[jax.ShapeDtypeStruct((N_CHIP, D_MODEL, n_shard), F32), jax.ShapeDtypeStruct((N_CHIP, D_MODEL, n_shard), BF16)],
        compiler_params=_cparams("parallel"),
    )(core, dwi)


C_WIDTH = 512
C_HEADS = 8
C_HDIM = 64
C_PAIRS = C_HEADS // 2
C_BQ = 512
C_TAIL = 16
C_KG = 4


def _split3(x):
    hi = x.astype(BF16)
    r = x - hi.astype(F32)
    mid = r.astype(BF16)
    return hi, mid, (r - mid.astype(F32)).astype(BF16)


def _piece_selectors():
    sel = np.zeros((C_HEADS, 3 * LANES, LANES), np.float32)
    for p in range(C_PAIRS):
        for e in range(2):
            for t in range(3):
                sel[2 * p + e, t * LANES + 2 * p + e, 3 * e + t] = -1.0
    return sel


def fox_prep(proj, qkv, bf_row):
    seq = proj.shape[0]
    nblk = seq // CHUNK
    nb = min(nblk, 4)
    tril = jnp.asarray(np.tril(np.ones((CHUNK, CHUNK), np.float32)), BF16)
    sel = jnp.asarray(_piece_selectors(), BF16)
    rows_t = CHUNK + C_TAIL

    def body(fl_ref, q_ref, k_ref, v_ref, bf_ref, l_ref, sel_ref, ka_ref, va_ref, vt_ref, kt_ref, qt_ref, qa_ref, carry_ref):
        @pl.when(pl.program_id(0) == 0)
        def _():
            carry_ref[...] = jnp.zeros_like(carry_ref)

        lane = lax.broadcasted_iota(jnp.int32, (CHUNK, LANES), 1)
        row = lax.broadcasted_iota(jnp.int32, (CHUNK, LANES), 0)
        r16 = lax.broadcasted_iota(jnp.int32, (C_TAIL, 2 * CHUNK), 0)
        l16 = lax.broadcasted_iota(jnp.int32, (C_TAIL, 2 * CHUNK), 1)
        zero = jnp.zeros((CHUNK, LANES), BF16)
        one = jnp.ones((CHUNK, LANES), BF16)

        def by_keys(x, right_a, right_b):
            xb = x.astype(BF16)
            top = jnp.concatenate([jnp.where(lane < C_HDIM, xb, zero), right_a], axis=1)
            return jnp.concatenate([top, jnp.concatenate([jnp.where(lane < C_HDIM, zero, xb), right_b], axis=1)], axis=0)

        def by_lanes(x, tail):
            xt = x.T.astype(BF16)
            main = jnp.concatenate([jnp.where(row < C_HDIM, xt, zero), jnp.where(row < C_HDIM, zero, xt)], axis=1)
            return jnp.concatenate([main, tail], axis=0)

        for j in range(nb):
            tok, wide2 = slice(j * CHUNK, (j + 1) * CHUNK), slice(j * 2 * CHUNK, (j + 1) * 2 * CHUNK)
            lf = jax.nn.log_sigmoid(fl_ref[tok, :LANES] + bf_ref[...])
            c = _exact_times(l_ref[...], lf, 3) + carry_ref[...]
            carry_ref[...] += jnp.sum(lf, axis=0, keepdims=True)
            c3 = jnp.concatenate(_split3(c), axis=1)
            for p in range(C_PAIRS):
                cols = slice(p * LANES, (p + 1) * LANES)
                q2, k2, v2 = (r[tok, cols].astype(F32) for r in (q_ref, k_ref, v_ref))
                q2 = q2 * (C_HDIM ** -0.5)
                negc = [_dot(c3, sel_ref[2 * p + e]).astype(BF16) for e in range(2)]
                ones3 = [jnp.where((lane >= 3 * e) & (lane < 3 * e + 3), one, zero) for e in range(2)]
                tail = jnp.where(((r16 == 2 * p) & (l16 < CHUNK)) | ((r16 == 2 * p + 1) & (l16 >= CHUNK)), 1.0, 0.0).astype(BF16)
                ka_ref[p, wide2] = by_keys(k2, negc[0], negc[1])
                va_ref[p, wide2] = by_keys(v2, ones3[0], ones3[1])
                kt_ref[p, :, wide2] = by_lanes(k2, tail)
                vt_ref[p, :, wide2] = by_lanes(v2, tail)
                qt_ref[p, :, tok] = jnp.concatenate([q2.T.astype(BF16), jnp.where(row < 6, one, zero)], axis=0)
                qa_ref[p, tok] = jnp.concatenate([q2.astype(BF16), jnp.where((lane == 2 * p) | (lane == 2 * p + 1), one, zero)], axis=1)

    wide = lambda j: pl.BlockSpec((nb * CHUNK, C_WIDTH), lambda n: (n, j))
    by_rows = pl.BlockSpec((C_PAIRS, nb * 2 * CHUNK, 2 * CHUNK), lambda n: (0, n, 0))
    by_cols = pl.BlockSpec((C_PAIRS, rows_t, nb * 2 * CHUNK), lambda n: (0, 0, n))
    return pl.pallas_call(
        body,
        name="fox_prep",
        grid=(nblk // nb,),
        in_specs=[pl.BlockSpec((nb * CHUNK, 256), lambda n: (n, 3)), wide(0), wide(1), wide(2), _full((1, LANES)),
                  _full((CHUNK, CHUNK)), _full((C_HEADS, 3 * LANES, LANES))],
        out_specs=[by_rows, by_rows, by_cols, by_cols,
                   pl.BlockSpec((C_PAIRS, 2 * CHUNK, nb * CHUNK), lambda n: (0, 0, n)),
                   pl.BlockSpec((C_PAIRS, nb * CHUNK, 2 * CHUNK), lambda n: (0, n, 0))],
        out_shape=[jax.ShapeDtypeStruct((C_PAIRS, 2 * seq, 2 * CHUNK), BF16)] * 2
        + [jax.ShapeDtypeStruct((C_PAIRS, rows_t, 2 * seq), BF16)] * 2
        + [jax.ShapeDtypeStruct((C_PAIRS, 2 * CHUNK, seq), BF16), jax.ShapeDtypeStruct((C_PAIRS, seq, 2 * CHUNK), BF16)],
        scratch_shapes=[pltpu.VMEM((1, LANES), F32)],
        compiler_params=_cparams("arbitrary"),
    )(proj, qkv, qkv, qkv, bf_row, tril, sel)


def _visible(shape, key0, query0):
    row = lax.broadcasted_iota(jnp.int32, shape, 0)
    key = key0 + lax.shift_left(lax.shift_right_logical(row, 8), 7) + (row & (CHUNK - 1))
    return key <= query0 + lax.broadcasted_iota(jnp.int32, shape, 1)


def _rows_ab(a, b, n):
    return jnp.concatenate([jnp.broadcast_to(a, (C_HDIM, n)), jnp.broadcast_to(b, (C_HDIM, n))], axis=0)


def _call_carrying(ex, body, operands, *, name, grid, in_specs, out_specs, out_shape, scratch_shapes, semantics=None):
    if ex is None:
        semantics = semantics or ("parallel", *["arbitrary"] * (len(grid) - 1))
        return pl.pallas_call(body, name=name, grid=grid, in_specs=in_specs, out_specs=out_specs, out_shape=out_shape,
                              scratch_shapes=scratch_shapes, compiler_params=_cparams(*semantics))(*operands)
    n_in, n_out = len(in_specs), len(out_specs)

    def wrapped(*refs):
        own, parts = _carried_refs(refs, n_in, n_out, ex)
        ids = [pl.program_id(a) for a in range(len(grid))]
        pl.when(functools.reduce(jnp.logical_and, [i == 0 for i in ids]))(lambda: ex.start(*parts))
        if hasattr(ex, "relay"):
            linear = functools.reduce(lambda at, ig: at * ig[1] + ig[0], zip(ids, grid), 0)
            pl.when(linear == int(np.prod(grid)) // 2)(lambda: ex.relay(*parts))
        body(*own)
        pl.when(functools.reduce(jnp.logical_and, [i == g - 1 for i, g in zip(ids, grid)]))(lambda: ex.finish(*parts))

    return pl.pallas_call(
        wrapped, name=name, grid=grid,
        in_specs=list(in_specs) + [_ANY] * len(ex.inputs), out_specs=list(out_specs) + [_ANY] * len(ex.out_shape),
        out_shape=list(out_shape) + list(ex.out_shape), scratch_shapes=list(scratch_shapes) + list(ex.scratch),
        input_output_aliases={n_in + i: n_out + o for i, o in getattr(ex, "aliases", {}).items()},
        compiler_params=_cparams(*["arbitrary"] * len(grid)),
    )(*operands, *ex.inputs)


def fox_fwd(qt, ka, vt, carried=None):
    seq = qt.shape[2]
    nblk = seq // CHUNK
    bq = min(C_BQ, seq)
    grp = bq // CHUNK
    rows_t = CHUNK + C_TAIL

    def body(qt_ref, ka_ref, vt_ref, o_ref, lse_ref, acc_ref, s_ref):
        p, i = pl.program_id(0), pl.program_id(1)
        qtile = qt_ref[0]
        r16 = lax.broadcasted_iota(jnp.int32, (C_TAIL, bq), 0)

        def scores(t):
            at = pl.multiple_of(t * grp * 2 * CHUNK, 2 * CHUNK)
            return _dot(ka_ref[0, pl.ds(at, grp * 2 * CHUNK), :], qtile)

        def rescale(al_a, al_b):
            tail = jnp.where(r16 == 2 * p, al_a, jnp.where(r16 == 2 * p + 1, al_b, 1.0))
            return jnp.concatenate([_rows_ab(al_a, al_b, bq), tail], axis=0)

        def diagonal(m):
            ma, mb = m
            na, nb = ma, mb
            blocks = []
            for g in range(grp):
                s = s_ref[g * 2 * CHUNK:(g + 1) * 2 * CHUNK, g * CHUNK:]
                s = jnp.where(_visible(s.shape, i * bq + g * CHUNK, i * bq + g * CHUNK), s, -jnp.inf)
                blocks.append(s)
                unseen = [jnp.full((1, g * CHUNK), -jnp.inf, F32)] if g else []
                na = jnp.maximum(na, jnp.concatenate(unseen + [jnp.max(s[:CHUNK], axis=0, keepdims=True)], axis=1))
                nb = jnp.maximum(nb, jnp.concatenate(unseen + [jnp.max(s[CHUNK:], axis=0, keepdims=True)], axis=1))
            acc_ref[...] = acc_ref[...] * rescale(jnp.exp(ma - na), jnp.exp(mb - nb))
            for g in range(grp):
                n = bq - g * CHUNK
                n2 = jnp.concatenate([jnp.broadcast_to(na[:, g * CHUNK:], (CHUNK, n)),
                                      jnp.broadcast_to(nb[:, g * CHUNK:], (CHUNK, n))], axis=0)
                at = pl.multiple_of((i * grp + g) * 2 * CHUNK, 2 * CHUNK)
                pt = jnp.exp(blocks[g] - n2).astype(BF16)
                acc_ref[:, g * CHUNK:] += _dot(vt_ref[0, :, pl.ds(at, 2 * CHUNK)], pt)
            return na, nb

        def group(t, m):
            ma, mb = m
            at = pl.multiple_of(t * grp * 2 * CHUNK, 2 * CHUNK)
            s = s_ref[...]
            sa = [s[g * 2 * CHUNK:g * 2 * CHUNK + CHUNK] for g in range(grp)]
            sb = [s[g * 2 * CHUNK + CHUNK:(g + 1) * 2 * CHUNK] for g in range(grp)]
            na, nb = ma, mb
            for g in range(grp):
                na = jnp.maximum(na, jnp.max(sa[g], axis=0, keepdims=True))
                nb = jnp.maximum(nb, jnp.max(sb[g], axis=0, keepdims=True))
            al_a, al_b = jnp.exp(ma - na), jnp.exp(mb - nb)
            pt = jnp.concatenate([jnp.exp(x - n) for g in range(grp) for x, n in ((sa[g], na), (sb[g], nb))], axis=0)
            pv = _dot(vt_ref[0, :, pl.ds(at, grp * 2 * CHUNK)], pt.astype(BF16))
            acc_ref[...] = acc_ref[...] * rescale(al_a, al_b) + pv
            return na, nb

        def step(t, m):
            s_next = scores(t + 1)
            m = group(t, m)
            s_ref[...] = s_next
            return m

        acc_ref[...] = jnp.zeros_like(acc_ref)
        s_ref[...] = scores(0)
        m = (jnp.full((1, bq), -jnp.inf, F32), jnp.full((1, bq), -jnp.inf, F32))
        m = lax.fori_loop(0, i, step, m)
        ma, mb = diagonal(m)
        tailv = acc_ref[CHUNK:rows_t, :]
        la = jnp.sum(jnp.where(r16 == 2 * p, tailv, 0.0), axis=0, keepdims=True)
        lb = jnp.sum(jnp.where(r16 == 2 * p + 1, tailv, 0.0), axis=0, keepdims=True)
        o_ref[...] = (acc_ref[0:CHUNK, :] * _rows_ab(1.0 / la, 1.0 / lb, bq)).T
        lse_ref[0, 0:1, :] = ma + jnp.log(la)
        lse_ref[0, 1:2, :] = mb + jnp.log(lb)

    return _call_carrying(
        carried, body, (qt, ka, vt),
        name="fox_fwd",
        grid=(C_PAIRS, seq // bq),
        in_specs=[
            pl.BlockSpec((1, 2 * CHUNK, bq), lambda p, i: (p, 0, i)),
            pl.BlockSpec((1, 2 * seq, 2 * CHUNK), lambda p, i: (p, 0, 0)),
            pl.BlockSpec((1, rows_t, 2 * seq), lambda p, i: (p, 0, 0)),
        ],
        out_specs=[pl.BlockSpec((bq, LANES), lambda p, i: (i, p)), pl.BlockSpec((1, 2, bq), lambda p, i: (p, 0, i))],
        out_shape=[jax.ShapeDtypeStruct((seq, C_WIDTH), F32), jax.ShapeDtypeStruct((C_PAIRS, 2, seq), F32)],
        scratch_shapes=[pltpu.VMEM((rows_t, bq), F32), pltpu.VMEM((grp * 2 * CHUNK, bq), F32)],
    )


def fox_bwd_prep(dy, o, proj, dproj):
    seq = o.shape[0]
    rows = min(seq, 512)
    ind = np.zeros((C_WIDTH, LANES), np.float32)
    for h in range(C_HEADS):
        ind[h * C_HDIM:(h + 1) * C_HDIM, h] = 1.0
    ind = jnp.asarray(ind, BF16)
    sel = _piece_selectors()
    sel = jnp.asarray(np.stack([sel[2 * p].T + sel[2 * p + 1].T for p in range(C_PAIRS)]), BF16)

    def body(dy_ref, o_ref, z_ref, ind_ref, sel_ref, _, do_ref, dz_ref, dot_ref):
        dy_c, o_v, z = dy_ref[...], o_ref[...], z_ref[...]
        sg = jax.nn.sigmoid(z)
        do = dy_c * (z * sg)
        do_ref[...] = do.astype(BF16)
        dz_ref[...] = (dy_c * o_v * (sg * (1.0 + z * (1.0 - sg)))).astype(BF16)
        prod = do * o_v
        hi = prod.astype(BF16)
        lo = (prod - hi.astype(F32)).astype(BF16)
        delta = _dot(hi, ind_ref[...]) + _dot(lo, ind_ref[...])
        d3 = jnp.concatenate(_split3(delta.T), axis=0)
        for p in range(C_PAIRS):
            tail = _dot(sel_ref[p], d3).astype(BF16)
            dot_ref[p] = jnp.concatenate([do[:, p * LANES:(p + 1) * LANES].T.astype(BF16), tail], axis=0)

    return pl.pallas_call(
        body,
        name="fox_bwd_prep",
        grid=(seq // rows,),
        in_specs=[
            pl.BlockSpec((rows, C_WIDTH), lambda i: (i, 1)),
            pl.BlockSpec((rows, C_WIDTH), lambda i: (i, 0)),
            pl.BlockSpec((rows, C_WIDTH), lambda i: (i, P_Z_BLOCK)),
            _full((C_WIDTH, LANES)), _full((C_PAIRS, LANES, 3 * LANES)), _ANY,
        ],
        out_specs=[
            pl.BlockSpec((rows, C_WIDTH), lambda i: (i, 0)),
            pl.BlockSpec((rows, C_WIDTH), lambda i: (i, 7)),
            pl.BlockSpec((C_PAIRS, 2 * CHUNK, rows), lambda i: (0, 0, i)),
        ],
        out_shape=[jax.ShapeDtypeStruct((seq, C_WIDTH), BF16), jax.ShapeDtypeStruct(dproj.shape, BF16),
                   jax.ShapeDtypeStruct((C_PAIRS, 2 * CHUNK, seq), BF16)],
        input_output_aliases={5: 1},
        compiler_params=_cparams("parallel"),
    )(dy, o, proj, ind, sel, dproj)


def fox_bwd(ka, va, kt, qt, dot_t, qa, dob, lse, carried=None):
    seq = qt.shape[2]
    nblk = seq // CHUNK
    bq = min(C_BQ, seq)
    nq = seq // bq
    kg = min(C_KG, nblk)
    ng = nblk // kg
    rows_t = CHUNK + C_TAIL

    def body(ka_ref, va_ref, kt_ref, qt_ref, dot_ref, qa_ref, do_ref, lse_ref,
             dq_ref, dk_ref, dv_ref, dck_ref, dcq_ref, dqt_acc, dv_acc, dka_acc):
        p, jg = pl.program_id(0), pl.program_id(1)

        @pl.when(jg == 0)
        def _():
            dqt_acc[...] = jnp.zeros_like(dqt_acc)

        dv_acc[...] = jnp.zeros_like(dv_acc)
        dka_acc[...] = jnp.zeros_like(dka_acc)

        def step(i, carry):
            cols = pl.ds(pl.multiple_of(i * bq, bq), bq)
            qtile, dotile = qt_ref[0, :, cols], dot_ref[0, :, cols]
            do, qa_i = do_ref[cols, :], qa_ref[0, cols, :]
            lse2 = jnp.concatenate([jnp.broadcast_to(lse_ref[0, 0:1, cols], (CHUNK, bq)),
                                    jnp.broadcast_to(lse_ref[0, 1:2, cols], (CHUNK, bq))] * kg, axis=0)
            pt = jnp.exp(_dot(ka_ref[0], qtile) - lse2)
            ds = pt * _dot(va_ref[0], dotile)
            ptb, dsb = pt.astype(BF16), ds.astype(BF16)
            dv_acc[...] += _dot(ptb, do)
            dka_acc[...] += _dot(dsb, qa_i)
            dqt_acc[:, cols] += _dot(kt_ref[0], dsb)
            return carry

        def diagonal(i):
            cols = [pl.ds(pl.multiple_of(i * bq + kb * CHUNK, CHUNK), bq - kb * CHUNK) for kb in range(kg)]
            rows = [slice(kb * 2 * CHUNK, (kb + 1) * 2 * CHUNK) for kb in range(kg)]
            s = [_dot(ka_ref[0, rows[kb], :], qt_ref[0, :, cols[kb]]) for kb in range(kg)]
            dp = [_dot(va_ref[0, rows[kb], :], dot_ref[0, :, cols[kb]]) for kb in range(kg)]
            ptb, dsb = [], []
            for kb in range(kg):
                n = bq - kb * CHUNK
                lse2 = jnp.concatenate([jnp.broadcast_to(lse_ref[0, 0:1, cols[kb]], (CHUNK, n)),
                                        jnp.broadcast_to(lse_ref[0, 1:2, cols[kb]], (CHUNK, n))], axis=0)
                pt = jnp.exp(s[kb] - lse2)
                pt = jnp.where(_visible(pt.shape, (jg * kg + kb) * CHUNK, i * bq + kb * CHUNK), pt, 0.0)
                ptb.append(pt.astype(BF16))
                dsb.append((pt * dp[kb]).astype(BF16))
            for kb in range(kg):
                dv_acc[rows[kb], :] += _dot(ptb[kb], do_ref[cols[kb], :])
                dka_acc[rows[kb], :] += _dot(dsb[kb], qa_ref[0, cols[kb], :])
                dqt_acc[:, cols[kb]] += _dot(kt_ref[0, :, rows[kb]], dsb[kb])

        assert kg * CHUNK == bq
        diagonal(jg)
        lax.fori_loop(jg + 1, nq, step, 0)
        lane = lax.broadcasted_iota(jnp.int32, (CHUNK, LANES), 1)
        for kb in range(kg):
            rows = slice(kb * CHUNK, (kb + 1) * CHUNK)
            ra = slice(kb * 2 * CHUNK, kb * 2 * CHUNK + CHUNK)
            rb = slice(kb * 2 * CHUNK + CHUNK, (kb + 1) * 2 * CHUNK)
            dk_ref[rows, :] = jnp.where(lane < C_HDIM, dka_acc[ra, 0:LANES], dka_acc[rb, 0:LANES]).astype(BF16)
            dv_ref[rows, :] = jnp.where(lane < C_HDIM, dv_acc[ra, :], dv_acc[rb, :]).astype(BF16)
            dck_ref[0, rows, :] = (jnp.where(lane == 2 * p, dka_acc[ra, LANES:], 0.0)
                                   + jnp.where(lane == 2 * p + 1, dka_acc[rb, LANES:], 0.0))

        @pl.when(jg == ng - 1)
        def _():
            for c in range(nq):
                dq_ref[c * bq:(c + 1) * bq, :] = (dqt_acc[0:CHUNK, c * bq:(c + 1) * bq].T * (C_HDIM ** -0.5)).astype(BF16)
            dcq_ref[0] = dqt_acc[CHUNK:rows_t, :]

    per_pair = lambda r, c: pl.BlockSpec((1, r, c), lambda p, j: (p, 0, 0))
    by_rows = pl.BlockSpec((1, kg * 2 * CHUNK, 2 * CHUNK), lambda p, j: (p, j, 0))
    by_cols = pl.BlockSpec((1, rows_t, kg * 2 * CHUNK), lambda p, j: (p, 0, j))
    return _call_carrying(
        carried, body, (ka, va, kt, qt, dot_t, qa, dob, lse),
        name="fox_bwd",
        grid=(C_PAIRS, ng),
        in_specs=[by_rows, by_rows, by_cols, per_pair(2 * CHUNK, seq), per_pair(2 * CHUNK, seq),
                  per_pair(seq, 2 * CHUNK), pl.BlockSpec((seq, LANES), lambda p, j: (0, p)), per_pair(2, seq)],
        out_specs=[pl.BlockSpec((seq, LANES), lambda p, j: (0, p)),
                   pl.BlockSpec((kg * CHUNK, LANES), lambda p, j: (j, p)),
                   pl.BlockSpec((kg * CHUNK, LANES), lambda p, j: (j, p)),
                   pl.BlockSpec((1, kg * CHUNK, LANES), lambda p, j: (p, j, 0)),
                   per_pair(C_TAIL, seq)],
        out_shape=[jax.ShapeDtypeStruct((seq, C_WIDTH), BF16)] * 3
        + [jax.ShapeDtypeStruct((C_PAIRS, seq, LANES), F32), jax.ShapeDtypeStruct((C_PAIRS, C_TAIL, seq), F32)],
        scratch_shapes=[pltpu.VMEM((rows_t, seq), F32), pltpu.VMEM((kg * 2 * CHUNK, LANES), F32),
                        pltpu.VMEM((kg * 2 * CHUNK, 2 * CHUNK), F32)],
    )


def fox_post(dcq, dck, proj, bf_row, dproj):
    seq = proj.shape[0]
    rows = min(seq, 512)
    nc = seq // rows
    triu = jnp.asarray(np.triu(np.ones((CHUNK, CHUNK), np.float32)), BF16)

    def body(dq_ref, dk_ref, fl_ref, bf_ref, u_ref, _, dfl_ref, dbf_ref, carry_ref):
        @pl.when(pl.program_id(0) == 0)
        def _():
            carry_ref[...] = jnp.zeros_like(carry_ref)
            dbf_ref[...] = jnp.zeros_like(dbf_ref)

        for j in reversed(range(rows // CHUNK)):
            at = slice(j * CHUNK, (j + 1) * CHUNK)
            heads = (dq_ref[0, :, at] + dq_ref[1, :, at]) + (dq_ref[2, :, at] + dq_ref[3, :, at])
            dc = jnp.concatenate([heads, jnp.zeros((CHUNK - C_TAIL, CHUNK), F32)], axis=0).T
            dc = dc - ((dk_ref[0, at] + dk_ref[1, at]) + (dk_ref[2, at] + dk_ref[3, at]))
            g = _exact_times(u_ref[...], dc, 3) + carry_ref[...]
            carry_ref[...] += jnp.sum(dc, axis=0, keepdims=True)
            dfl = g * jax.nn.sigmoid(-(fl_ref[at, :LANES] + bf_ref[...]))
            dbf_ref[...] += jnp.sum(dfl, axis=0, keepdims=True)
            dfl_ref[at, :] = jnp.concatenate([dfl, jnp.zeros_like(dfl)], axis=1).astype(BF16)

    rev = lambda n: nc - 1 - n
    return pl.pallas_call(
        body,
        name="fox_post",
        grid=(nc,),
        in_specs=[
            pl.BlockSpec((C_PAIRS, C_TAIL, rows), lambda n: (0, 0, rev(n))),
            pl.BlockSpec((C_PAIRS, rows, LANES), lambda n: (0, rev(n), 0)),
            pl.BlockSpec((rows, 256), lambda n: (rev(n), 3)),
            _full((1, LANES)), _full((CHUNK, CHUNK)), _ANY,
        ],
        out_specs=[pl.BlockSpec((rows, 256), lambda n: (rev(n), 3)), _full((1, LANES))],
        out_shape=[jax.ShapeDtypeStruct(dproj.shape, BF16), jax.ShapeDtypeStruct((1, LANES), F32)],
        input_output_aliases={5: 0},
        scratch_shapes=[pltpu.VMEM((1, LANES), F32)],
        compiler_params=_cparams("arbitrary"),
    )(dcq, dck, proj, bf_row, triu, dproj)


N_DEV = 8
MESH = pl.DeviceIdType.MESH
_ANY = pl.BlockSpec(memory_space=pl.ANY)


def _mesh_pos():
    return lax.axis_index("x"), lax.axis_index("y"), lax.axis_index("c")


def _dev_index(px, py, pc):
    return 4 * px + 2 * py + pc


def _row_pieces(ref, rows):
    return [ref.at[idx + (pl.ds(r, rows),)] for idx in np.ndindex(*ref.shape[:-2]) for r in range(0, ref.shape[-2], rows)]


class _Transfer:
    def __init__(self, src, dst, rows, send_sem, recv_sem, to):
        self.src, self.dst, self.rows, self.sems, self.to = src, dst, rows, (send_sem, recv_sem), to

    def _copy(self, src, dst):
        return pltpu.make_async_remote_copy(src_ref=src, dst_ref=dst, send_sem=self.sems[0], recv_sem=self.sems[1],
                                            device_id=self.to, device_id_type=MESH)

    def start(self):
        for s, d in zip(_row_pieces(self.src, self.rows), _row_pieces(self.dst, self.rows), strict=True):
            self._copy(s, d).start()

    def wait_send(self):
        self._copy(self.src, self.dst).wait_send()

    def wait_recv(self):
        self._copy(self.src, self.dst).wait_recv()


def _exchange_call(ex, name):
    n_in, n_out = len(ex.inputs), len(ex.out_shape)

    def body(*refs):
        parts = refs[:n_in], refs[n_in:n_in + n_out], refs[n_in + n_out:]
        ex.start(*parts)
        getattr(ex, "relay", lambda *_: None)(*parts)
        ex.finish(*parts)

    return pl.pallas_call(body, name=name, in_specs=[_ANY] * n_in, out_specs=[_ANY] * n_out, out_shape=ex.out_shape,
                          scratch_shapes=ex.scratch, input_output_aliases=getattr(ex, "aliases", {}))(*ex.inputs)


def _carried_refs(refs, n_in, n_out, ex):
    k_in, k_out, k_sem = (len(ex.inputs), len(ex.out_shape), len(ex.scratch)) if ex else (0, 0, 0)
    a, b, c = n_in + k_in, n_in + k_in + n_out, n_in + k_in + n_out + k_out
    own = refs[:n_in] + refs[a:b] + refs[c:len(refs) - k_sem]
    return own, (refs[n_in:a], refs[b:c], refs[len(refs) - k_sem:])


class AllGatherWeights:
    def __init__(self, blocks):
        n = len(blocks)
        self.inputs = tuple(blocks)
        self.out_shape = [jax.ShapeDtypeStruct((N_DEV,) + b.shape, b.dtype) for b in blocks]
        self.scratch = ([pltpu.SemaphoreType.DMA((n, 7)), pltpu.SemaphoreType.DMA((n, 7)), pltpu.SemaphoreType.DMA((n, 2))]
                        + [pltpu.VMEM(b.shape, b.dtype) for b in blocks])

    def _plan(self, ins, outs, scratch):
        send_sems, recv_sems, local_sems, *staged = scratch
        x, y, c = _mesh_pos()
        me, sibling = (x, y, c), (x, y, 1 - c)
        chips = [(1 - x, y), (x, 1 - y), (1 - x, 1 - y)]
        every = range(len(ins))

        def copy(a, k, block, to, own=False):
            slot = outs[a].at[_dev_index(*block)]
            return _Transfer(ins[a] if own else slot, slot, ins[a].shape[-2], send_sems.at[a, k], recv_sems.at[a, k], to)

        mine = [(pltpu.make_async_copy(ins[a], staged[a], local_sems.at[a, 0]),
                 pltpu.make_async_copy(staged[a], outs[a].at[_dev_index(*me)], local_sems.at[a, 1])) for a in every]
        first = [copy(a, 1 + j, me, (*chip, c), own=True) for j, chip in enumerate(chips) for a in every]
        first += [copy(a, 0, me, sibling, own=True) for a in every]
        passed = [[copy(a, 4 + j, (*chip, c), sibling) for a in every] for j, chip in enumerate(chips)]
        return me, sibling, chips, c, every, copy, mine, first, passed

    def start(self, ins, outs, scratch):
        *_, mine, first, _ = self._plan(ins, outs, scratch)
        for to_vmem, _ in mine:
            to_vmem.start()
        for cp in first:
            cp.start()

    def relay(self, ins, outs, scratch):
        me, sibling, chips, c, every, copy, mine, first, passed = self._plan(ins, outs, scratch)
        for to_vmem, to_slot in mine:
            to_vmem.wait()
            to_slot.start()
        for j, chip in enumerate(chips):
            for a in every:
                copy(a, 1 + j, (*chip, c), me).wait_recv()
            for cp in passed[j]:
                cp.start()

    def finish(self, ins, outs, scratch):
        me, sibling, chips, c, every, copy, mine, first, passed = self._plan(ins, outs, scratch)
        for a in every:
            copy(a, 0, sibling, me).wait_recv()
        for j, chip in enumerate(chips):
            for a in every:
                copy(a, 4 + j, (*chip, 1 - c), me).wait_recv()
        for cp in first + [cp for group in passed for cp in group]:
            cp.wait_send()
        for _, to_slot in mine:
            to_slot.wait()


N_CHIP = 4


class PairExchange:
    def __init__(self, by_core, whole=()):
        self.inputs = tuple(by_core) + tuple(whole)
        self.n_by_core = len(by_core)
        self.out_shape = ([jax.ShapeDtypeStruct(a.shape[1:], a.dtype) for a in by_core]
                          + [jax.ShapeDtypeStruct(a.shape, a.dtype) for a in whole])
        n = len(self.inputs)
        self.scratch = [pltpu.SemaphoreType.DMA((n,)), pltpu.SemaphoreType.DMA((n,))]

    def _copies(self, ins, outs, sems):
        x, y, c = _mesh_pos()
        srcs = [r.at[1 - c] if a < self.n_by_core else r for a, r in enumerate(ins)]
        return [_Transfer(srcs[a], outs[a], outs[a].shape[-2], sems[0].at[a], sems[1].at[a], (x, y, 1 - c))
                for a in range(len(ins))]

    def start(self, ins, outs, sems):
        for cp in self._copies(ins, outs, sems):
            cp.start()

    def finish(self, ins, outs, sems):
        copies = self._copies(ins, outs, sems)
        for cp in copies:
            cp.wait_recv()
        for cp in copies:
            cp.wait_send()


def pair_sum(own, other, dtype, rows, name, core, layer, depth, stacked=None):
    n, n_r, n_c = other.shape
    by_core = own.ndim == 4
    own = own if by_core else own[None]

    def body(core_ref, a_ref, b_ref, *refs):
        refs[-1][0, 0] = (a_ref[0, 0] + b_ref[0].astype(F32)).astype(dtype)

    carried = () if stacked is None else (stacked,)
    grid_spec = pltpu.PrefetchScalarGridSpec(
        num_scalar_prefetch=1,
        grid=(n, n_r // rows),
        in_specs=[pl.BlockSpec((1, 1, rows, n_c), lambda i, r, s: (s[0] if by_core else 0, i, r, 0)),
                  pl.BlockSpec((1, rows, n_c), lambda i, r, s: (i, r, 0))] + [_ANY] * len(carried),
        out_specs=pl.BlockSpec((1, 1, rows, n_c), lambda i, r, s: (i, layer, r, 0)),
    )
    return pl.pallas_call(
        body,
        name=name,
        grid_spec=grid_spec,
        out_shape=jax.ShapeDtypeStruct((n, depth, n_r, n_c), dtype),
        input_output_aliases={3: 0} if carried else {},
        compiler_params=_cparams("parallel", "parallel"),
    )(core, own, other, *carried)


def small_sum(a, b, name):
    def body(a_ref, b_ref, o_ref):
        o_ref[...] = a_ref[...] + b_ref[...]

    return pl.pallas_call(body, name=name, out_shape=jax.ShapeDtypeStruct(a.shape, a.dtype))(a, b)


class ChipExchange:
    def __init__(self, by_chip=(), layers=(), gathered=(), stacked=()):
        stacked = tuple(stacked) or (None,) * len(by_chip)
        kept = [s for s in stacked if s is not None]
        self.inputs = tuple(by_chip) + tuple(gathered) + tuple(kept)
        self.n_by_chip, self.n_gathered = len(by_chip), len(gathered)
        self.items = [(a, l) for a in range(len(by_chip)) for l in layers[a]] + [(self.n_by_chip + g, None) for g in range(len(gathered))]
        self.out_shape = ([jax.ShapeDtypeStruct((N_CHIP - 1,) + a.shape[1:], a.dtype) for a in by_chip]
                          + [jax.ShapeDtypeStruct((N_CHIP,) + a.shape, a.dtype) for a in gathered])
        at = iter(range(self.n_by_chip + self.n_gathered, len(self.inputs)))
        self.aliases = {next(at): a for a, s in enumerate(stacked) if s is not None}
        n = len(self.items)
        self.scratch = [pltpu.SemaphoreType.DMA((n, 3)), pltpu.SemaphoreType.DMA((n, 3)),
                        pltpu.SemaphoreType.DMA((max(self.n_gathered, 1),))]

    def _plan(self, ins, outs, sems):
        x, y, c = _mesh_pos()
        chip = 2 * x + y
        n = len(self.items)

        def copy(i, k, sending):
            a, layer = self.items[i]
            px, py = x ^ ((k >> 1) & 1), y ^ (k & 1)
            if layer is not None:
                src, dst = ins[a].at[2 * px + py, layer], outs[a].at[k - 1, layer]
            else:
                src, dst = ins[a], outs[a].at[chip if sending else 2 * px + py]
            return _Transfer(src, dst, dst.shape[-2], sems[0].at[i, k - 1], sems[1].at[i, k - 1], (px, py, c))

        local = [pltpu.make_async_copy(ins[a], outs[a].at[chip], sems[2].at[a - self.n_by_chip])
                 for a in range(self.n_by_chip, self.n_by_chip + self.n_gathered)]
        return n, copy, local

    def start(self, ins, outs, sems):
        n, copy, local = self._plan(ins, outs, sems)
        for cp in local:
            cp.start()
        for k in range(1, N_CHIP):
            for a in range(n):
                copy(a, k, True).start()

    def finish(self, ins, outs, sems):
        n, copy, local = self._plan(ins, outs, sems)
        for k in range(1, N_CHIP):
            for a in range(n):
                copy(a, k, False).wait_recv()
        for k in range(1, N_CHIP):
            for a in range(n):
                copy(a, k, True).wait_send()
        for cp in local:
            cp.wait()


ADAM_LR = 0.001
ADAM_B1 = 0.9
ADAM_B2 = 0.999
ADAM_EPS = 1e-08
ADAM_WD = 0.01
ADAM_STEP = 10


def adam_reduce(parts, w, m, v, rows, name, own=None, chip=None):
    n_l, n_r, n_c = w.shape
    n_parts = parts.shape[0]

    def body(*refs):
        p_ref, w_ref, m_ref, v_ref, g_ref, d_ref, m2_ref, v2_ref = refs[-8:]
        g = p_ref[0, 0].astype(F32)
        if own is not None:
            g = refs[-9][...].reshape(rows, n_c).astype(F32) + g
        for d in range(1, n_parts):
            g = g + p_ref[d, 0].astype(F32)
        m2 = ADAM_B1 * m_ref[0] + (1.0 - ADAM_B1) * g
        v2 = ADAM_B2 * v_ref[0] + (1.0 - ADAM_B2) * (g * g)
        m_hat = m2 / (1.0 - ADAM_B1 ** ADAM_STEP)
        v_hat = v2 / (1.0 - ADAM_B2 ** ADAM_STEP)
        g_ref[0] = g
        d_ref[0] = -ADAM_LR * (m_hat / (jnp.sqrt(v_hat) + ADAM_EPS) + ADAM_WD * w_ref[0])
        m2_ref[0] = m2
        v2_ref[0] = v2

    blk = lambda: pl.BlockSpec((1, rows, n_c), lambda l, r, *_: (l, r, 0))
    in_specs = [pl.BlockSpec((n_parts, 1, rows, n_c), lambda l, r, *_: (0, l, r, 0)), blk(), blk(), blk()]
    args = (parts, w, m, v)
    if own is not None:
        in_specs = [pl.BlockSpec((1, 1, rows, n_c), lambda l, r, s: (s[0], l, r, 0))] + in_specs
        args = (chip, own) + args
    grid_spec = pltpu.PrefetchScalarGridSpec(
        num_scalar_prefetch=0 if own is None else 1, grid=(n_l, n_r // rows), in_specs=in_specs,
        out_specs=[blk(), blk(), blk(), blk()])
    return pl.pallas_call(
        body,
        name=name,
        grid_spec=grid_spec,
        out_shape=[jax.ShapeDtypeStruct(w.shape, F32)] * 4,
        compiler_params=_cparams("parallel", "parallel"),
    )(*args)


def adam_reduce_columns(parts, w, m, v, name, own, chip):
    n_l, n_r, n_c = w.shape
    n_parts = parts.shape[0]
    view = lambda a: jnp.transpose(a, (2, 0, 1))

    def body(_, own_ref, p_ref, w_ref, m_ref, v_ref, g_ref, d_ref, m2_ref, v2_ref):
        for l in range(n_l):
            g = own_ref[0, l].astype(F32) + p_ref[0, l].astype(F32)
            for d in range(1, n_parts):
                g = g + p_ref[d, l].astype(F32)
            g = g.T
            w_l, m_l, v_l = w_ref[:, l, :], m_ref[:, l, :], v_ref[:, l, :]
            m2 = ADAM_B1 * m_l + (1.0 - ADAM_B1) * g
            v2 = ADAM_B2 * v_l + (1.0 - ADAM_B2) * (g * g)
            m_hat = m2 / (1.0 - ADAM_B1 ** ADAM_STEP)
            v_hat = v2 / (1.0 - ADAM_B2 ** ADAM_STEP)
            g_ref[:, l, :] = g
            d_ref[:, l, :] = -ADAM_LR * (m_hat / (jnp.sqrt(v_hat) + ADAM_EPS) + ADAM_WD * w_l)
            m2_ref[:, l, :] = m2
            v2_ref[:, l, :] = v2

    blk = lambda: pl.BlockSpec((LANES, n_l, n_r), lambda c, s: (c, 0, 0))
    grid_spec = pltpu.PrefetchScalarGridSpec(
        num_scalar_prefetch=1, grid=(pl.cdiv(n_c, LANES),),
        in_specs=[pl.BlockSpec((1, n_l, n_r, LANES), lambda c, s: (s[0], 0, 0, c)),
                  pl.BlockSpec((n_parts, n_l, n_r, LANES), lambda c, s: (0, 0, 0, c)), blk(), blk(), blk()],
        out_specs=[blk(), blk(), blk(), blk()])
    outs = pl.pallas_call(
        body,
        name=name,
        grid_spec=grid_spec,
        out_shape=[jax.ShapeDtypeStruct((n_c, n_l, n_r), F32)] * 4,
        compiler_params=_cparams("parallel"),
    )(chip, own, parts, view(w), view(m), view(v))
    return [jnp.transpose(o, (1, 2, 0)) for o in outs]


_SMALL = (("norm_g", (2, 1024)), ("gmlp_ln_g", (2, 4, 64)), ("gmlp_ln_b", (2, 4, 64)),
          ("gmlp_b_s", (2, 4, 128)), ("hgrn_lb", (2, 256)), ("hgrn_onorm_g", (2, 64)), ("fox_b_f", (2, 8)),
          ("final_norm_g", (1024,)), ("loss", ()))


def _padded(n):
    return -(-n // LANES) * LANES


_SMALL_ROWS = -(-sum(_padded(int(np.prod(s))) for _, s in _SMALL) // LANES // 8) * 8


def _pack_small(vals):
    flat = []
    for (name, shape), a in zip(_SMALL, vals, strict=True):
        n = int(np.prod(shape))
        flat.append(jnp.pad(a.reshape(n).astype(F32), (0, _padded(n) - n)))
    flat = jnp.concatenate(flat)
    return jnp.pad(flat, (0, _SMALL_ROWS * LANES - flat.shape[0])).reshape(_SMALL_ROWS, LANES)


def _unpack_small(slab):
    flat, out, at = slab.reshape(-1), {}, 0
    for name, shape in _SMALL:
        n = int(np.prod(shape))
        out[name] = flat[at:at + n].reshape(shape)
        at += _padded(n)
    return out


def sum_parts(parts, name):
    def body(p_ref, o_ref):
        g = p_ref[0]
        for d in range(1, parts.shape[0]):
            g = g + p_ref[d]
        o_ref[...] = g

    return pl.pallas_call(body, name=name, out_shape=jax.ShapeDtypeStruct(parts.shape[1:], F32))(parts)


def adam_small(gs, ws, ms, vs):
    n = len(gs)

    def body(*refs):
        for k in range(n):
            g, w, m, v = (refs[j * n + k][...] for j in range(4))
            m2 = ADAM_B1 * m + (1.0 - ADAM_B1) * g
            v2 = ADAM_B2 * v + (1.0 - ADAM_B2) * (g * g)
            m_hat = m2 / (1.0 - ADAM_B1 ** ADAM_STEP)
            v_hat = v2 / (1.0 - ADAM_B2 ** ADAM_STEP)
            refs[4 * n + k][...] = -ADAM_LR * (m_hat / (jnp.sqrt(v_hat) + ADAM_EPS) + ADAM_WD * w)
            refs[5 * n + k][...] = m2
            refs[6 * n + k][...] = v2

    outs = pl.pallas_call(body, name="adam_small",
                          out_shape=[jax.ShapeDtypeStruct(w.shape, F32) for _ in range(3) for w in ws])(*gs, *ws, *ms, *vs)
    return outs[:n], outs[n:2 * n], outs[2 * n:]


def kernel(x, norm_g, w_in, w_out, gmlp_ln_g, gmlp_ln_b, gmlp_w_s, gmlp_b_s, hgrn_lb, hgrn_onorm_g, fox_b_f, final_norm_g, loss_target, m_norm_g, m_w_in, m_w_out, m_gmlp_ln_g, m_gmlp_ln_b, m_gmlp_w_s, m_gmlp_b_s, m_hgrn_lb, m_hgrn_onorm_g, m_fox_b_f, m_final_norm_g, v_norm_g, v_w_in, v_w_out, v_gmlp_ln_g, v_gmlp_ln_b, v_gmlp_w_s, v_gmlp_b_s, v_hgrn_lb, v_hgrn_onorm_g, v_fox_b_f, v_final_norm_g):
    depth = w_in.shape[0]
    seq = x.shape[1]
    assert w_in.shape[2] * N_DEV == N_IN
    xs, tgt = x[0], loss_target[0]

    wi_blk, wo_blk = w_in.astype(BF16), w_out.astype(BF16)
    (wi_all,) = _exchange_call(AllGatherWeights([wi_blk[0]]), "allgather_weights_0")

    ln_g = gmlp_ln_g.reshape(depth, 1, A_WIDTH)
    ln_b = gmlp_ln_b.reshape(depth, 1, A_WIDTH)
    bs_t = jnp.pad(jnp.transpose(gmlp_b_s, (0, 2, 1)), ((0, 0), (0, 0), (0, LANES - A_GROUPS)))
    lb0, lb1 = hgrn_lb[0:1], hgrn_lb[1:2]
    onorm = jnp.tile(hgrn_onorm_g, (1, B_HEADS)).reshape(depth, 1, B_WIDTH)
    bf_row = jnp.pad(fox_b_f, ((0, 0), (0, LANES - C_HEADS))).reshape(depth, 1, LANES)

    core = lax.axis_index("c").astype(jnp.int32).reshape(1)
    chip = (2 * lax.axis_index("x") + lax.axis_index("y")).astype(jnp.int32).reshape(1)

    saved = []
    xc = xs
    for l in range(depth):
        wi_int = assemble_w_in(wi_all[:, None])
        proj, qkv, h = inproj(xc, norm_g[l:l + 1], wi_int, 0)
        ya = gmlp_fwd(proj, ln_g[l], ln_b[l], gmlp_w_s[l], bs_t[l])
        yb, states = hgrn_fwd(proj, lb0, lb1, onorm[l], l)
        ka, va, vt, kt, qt, qa = fox_prep(proj, qkv, bf_row[l])
        ride = ([wo_blk] if l == 0 else []) + ([wi_blk[l + 1]] if l + 1 < depth else [])
        o, lse, *gathered = fox_fwd(qt, ka, vt, AllGatherWeights(ride) if ride else None)
        if l == 0:
            wo_all = gathered.pop(0)
        if gathered:
            (wi_all,) = gathered
        x_in = xc
        if l + 1 < depth:
            xc, yfull = outproj(x_in, ya, yb, o, proj, wo_all, l)
        else:
            dx, yfull, d_final_g, loss_tile = outproj(x_in, ya, yb, o, proj, wo_all, l, (final_norm_g[None], tgt))
        saved.append((x_in, proj, h, states, ka, va, kt, qt, qa, o, lse, yfull, wi_int))

    n_shard = w_in.shape[2]
    g_norm = [None] * depth
    g_ln_g, g_ln_b, g_ws, g_bs, g_on, g_bf = ([None] * depth for _ in range(6))
    g_lb0, g_lb1 = jnp.zeros_like(lb0), jnp.zeros_like(lb1)
    swi = swo = rwi = rwo = None
    for l in reversed(range(depth)):
        x_in, proj, h, states, ka, va, kt, qt, qa, o, lse, yfull, wi_int = saved[l]
        dy, gwo = outproj_bwd(dx, yfull, wo_all, l)
        dproj, g_ln_g[l], g_ln_b[l], g_ws[l], dbs_t = gmlp_bwd(proj, dy, ln_g[l], ln_b[l], gmlp_w_s[l], bs_t[l])
        g_bs[l] = dbs_t[:, :A_GROUPS].T
        if l > 0:
            (qwo,) = _exchange_call(PairExchange([gwo]), f"pair_exchange_w_out_{l}")
        else:
            gws = jnp.stack(g_ws).reshape(-1, LANES)
            qwo, qws = _exchange_call(PairExchange([gwo], [gws]), f"pair_exchange_w_out_{l}")
            sws = small_sum(gws, qws, "pair_sum_w_s")
        swo = pair_sum(gwo, qwo, BF16, gwo.shape[2], "pair_sum_w_out", core, l, depth, swo)
        dproj, d0, d1, don = hgrn_bwd(proj, states, dy, lb0, lb1, onorm[l], l, dproj)
        g_lb0, g_lb1 = g_lb0 + d0, g_lb1 + d1
        g_on[l] = don.reshape(B_HEADS, B_KDIM).sum(0)
        dob, dproj, dot_t = fox_bwd_prep(dy, o, proj, dproj)
        top = l == depth - 1
        ride = ChipExchange([swo] if top else [swi, swo], [(l,)] if top else [(l + 1,), (l,)],
                            [sws] if l == 0 else [], [rwo] if top else [rwi, rwo])
        outs = fox_bwd(ka, va, kt, qt, dot_t, qa, dob, lse, ride)
        dqkv, (dck, dcq), got = outs[:3], outs[3:5], list(outs[5:])
        if not top:
            rwi = got.pop(0)
        rwo = got.pop(0)
        if l == 0:
            (rws,) = got
        dproj, dbf = fox_post(dcq, dck, proj, bf_row[l], dproj)
        g_bf[l] = dbf[0, :C_HEADS]
        gwi, for_sibling = split_w_in_grad(inproj_bwd_w(h, dproj, dqkv), n_shard, core)
        (qwi,) = _exchange_call(PairExchange([], [for_sibling]), f"pair_exchange_w_in_{l}")
        swi = pair_sum(gwi, qwi, BF16, gwi.shape[1], "pair_sum_w_in", core, l, depth, swi)
        ride = ChipExchange([swi], [(l,)], stacked=[rwi]) if l == 0 else None
        outs = inproj_bwd_x(dproj, dqkv, wi_int, x_in, norm_g[l:l + 1], dx, 0, ride)
        dx, g_norm[l] = outs[:2]
        if ride is not None:
            (rwi,) = outs[2:]

    gsm = _pack_small([
        jnp.concatenate(g_norm), jnp.stack(g_ln_g), jnp.stack(g_ln_b), jnp.stack(g_bs),
        jnp.concatenate([g_lb0, g_lb1]), jnp.stack(g_on), jnp.stack(g_bf), d_final_g, loss_tile[0, 0]])
    (qsm,) = _exchange_call(PairExchange([], [gsm]), "pair_exchange_small")
    ssm = small_sum(gsm, qsm, "pair_sum_small")
    (rsm,) = _exchange_call(ChipExchange(gathered=[ssm]), "chip_exchange_small")

    small_w = (norm_g, gmlp_ln_g, gmlp_ln_b, gmlp_b_s, hgrn_lb, hgrn_onorm_g, fox_b_f, final_norm_g)
    small_m = (m_norm_g, m_gmlp_ln_g, m_gmlp_ln_b, m_gmlp_b_s, m_hgrn_lb, m_hgrn_onorm_g, m_fox_b_f, m_final_norm_g)
    small_v = (v_norm_g, v_gmlp_ln_g, v_gmlp_ln_b, v_gmlp_b_s, v_hgrn_lb, v_hgrn_onorm_g, v_fox_b_f, v_final_norm_g)
    res_wi = adam_reduce_columns(rwi, w_in, m_w_in, v_w_in, "adam_w_in", swi, chip)
    res_wo = adam_reduce(rwo, w_out, m_w_out, v_w_out, w_out.shape[1], "adam_w_out", own=swo, chip=chip)
    grads = _unpack_small(sum_parts(rsm, "sum_small"))
    names = [name for name, _ in _SMALL if name != "loss"]
    rows = lambda a: a.reshape(1, -1) if a.ndim == 1 else a
    res_sm = adam_small([rows(grads[k]) for k in names], *([rows(a) for a in wmv] for wmv in (small_w, small_m, small_v)))
    res_sm = [grads] + [{k: a.reshape(grads[k].shape) for k, a in zip(names, r, strict=True)} for r in res_sm]
    as_rows = lambda a: a.reshape(1, -1, LANES)
    res_ws = adam_reduce(rws[:, None], as_rows(gmlp_w_s), as_rows(m_gmlp_w_s), as_rows(v_gmlp_w_s), rws.shape[1], "adam_w_s")
    for s, r in zip(res_sm, res_ws, strict=True):
        s["gmlp_w_s"] = r.reshape(gmlp_w_s.shape)

    def group(i):
        s = res_sm[i]
        return [s["norm_g"], res_wi[i], res_wo[i], s["gmlp_ln_g"], s["gmlp_ln_b"], s["gmlp_w_s"], s["gmlp_b_s"],
                s["hgrn_lb"], s["hgrn_onorm_g"], s["fox_b_f"], s["final_norm_g"]]

    return (res_sm[0]["loss"], dx[None], *group(0), *group(1), *group(2), *group(3))
```

```python
import functools

import jax
import jax.numpy as jnp
import numpy as np
from jax import lax
from jax.experimental import pallas as pl
from jax.experimental.pallas import tpu as pltpu

F32 = jnp.float32
BF16 = jnp.bfloat16

NORM_EPS = 1e-6
F_FLOOR = 1e-30
CHUNK = 128
LANES = 128
VMEM_LIMIT = 56 * 1024 * 1024


def _cparams(*sem):
    return pltpu.CompilerParams(dimension_semantics=sem, vmem_limit_bytes=VMEM_LIMIT)


def _dot(a, b, dims=(((1,), (0,)), ((), ())), precision=None):
    return lax.dot_general(a, b, dims, precision=precision, preferred_element_type=F32)


_NT = (((1,), (1,)), ((), ()))
_TN = (((0,), (0,)), ((), ()))


def _bf16_pieces(x, n):
    out, r = [], x
    for i in range(n):
        out.append(r.astype(BF16))
        if i + 1 < n:
            r = r - out[-1].astype(F32)
    return out


@functools.partial(jax.custom_vjp, nondiff_argnums=(2,))
def _times_exact(x, e, n):
    return functools.reduce(jnp.add, [_dot(p, e) for p in _bf16_pieces(x, n)])


def _times_exact_fwd(x, e, n):
    return _times_exact(x, e, n), e


def _times_exact_bwd(n, e, g):
    dx = functools.reduce(jnp.add, [lax.dot_general(p, e, _NT, preferred_element_type=F32) for p in _bf16_pieces(g, n)])
    return dx, jnp.zeros_like(e)


_times_exact.defvjp(_times_exact_fwd, _times_exact_bwd)


@functools.partial(jax.custom_vjp, nondiff_argnums=(2,))
def _exact_times(e, x, n):
    return functools.reduce(jnp.add, [_dot(e, p) for p in _bf16_pieces(x, n)])


def _exact_times_fwd(e, x, n):
    return _exact_times(e, x, n), e


def _exact_times_bwd(n, e, g):
    dx = functools.reduce(jnp.add, [lax.dot_general(e, p, _TN, preferred_element_type=F32) for p in _bf16_pieces(g, n)])
    return jnp.zeros_like(e), dx


_exact_times.defvjp(_exact_times_fwd, _exact_times_bwd)


def _group_mean_matrix(width, group):
    idx = np.arange(width) // group
    return jnp.asarray((idx[:, None] == idx[None, :]).astype(np.float32) / group, BF16)


def _group_ones_matrix(width, group):
    idx = np.arange(width) // group
    return jnp.asarray((idx[:, None] == idx[None, :]).astype(np.float32), BF16)


A_WIDTH = 256
A_GROUPS = 4
A_GDIM = 64


A_ROWS = 512


def _gmlp_chunk(x3, ln_g, ln_b, w_s, bs_t, mean_m, gind):
    n = x3.shape[0] // CHUNK
    u = jax.nn.gelu(x3[:, :A_WIDTH])
    v = jax.nn.gelu(x3[:, A_WIDTH:2 * A_WIDTH])
    z = x3[:, 2 * A_WIDTH:]
    mu = _times_exact(v, mean_m, 2)
    d = v - mu
    var = _times_exact(d * d, mean_m, 2)
    vn = d * lax.rsqrt(var + NORM_EPS) * ln_g + ln_b
    vnb = vn.astype(BF16)
    wide = jnp.concatenate([vnb[i * CHUNK:(i + 1) * CHUNK] for i in range(n)], axis=1)
    row = lax.broadcasted_iota(jnp.int32, (CHUNK, CHUNK), 0)
    col = lax.broadcasted_iota(jnp.int32, (CHUNK, CHUNK), 1)
    causal = row >= col
    lane_g = lax.shift_right_logical(lax.broadcasted_iota(jnp.int32, (CHUNK, n * A_WIDTH), 1), 6) & (A_GROUPS - 1)
    bias = _times_exact(bs_t, gind, 3)
    mixed = jnp.concatenate([bias] * n, axis=1)
    for g in range(A_GROUPS):
        wc = jnp.where(causal, w_s[g], 0.0).astype(BF16)
        mixed = mixed + jnp.where(lane_g == g, _dot(wc, wide), 0.0)
    mixed = jnp.concatenate([mixed[:, i * A_WIDTH:(i + 1) * A_WIDTH] for i in range(n)], axis=0)
    return u * mixed * jax.nn.silu(z)


def _gmlp_consts():
    gind = np.zeros((LANES, A_WIDTH), np.float32)
    for g in range(A_GROUPS):
        gind[g, g * A_GDIM:(g + 1) * A_GDIM] = 1.0
    return _group_mean_matrix(A_WIDTH, A_GDIM), jnp.asarray(gind, BF16)


def _full(shape):
    return pl.BlockSpec(shape, lambda *_: (0,) * len(shape))


def gmlp_fwd(proj, ln_g, ln_b, w_s, bs_t):
    seq = proj.shape[0]
    rows = min(A_ROWS, seq)
    mean_m, gind = _gmlp_consts()

    def body(x_ref, g_ref, b_ref, w_ref, bs_ref, m_ref, gi_ref, y_ref):
        y = _gmlp_chunk(x_ref[...], g_ref[...], b_ref[...], w_ref[...], bs_ref[...], m_ref[...], gi_ref[...])
        y_ref[...] = y.astype(BF16)

    return pl.pallas_call(
        body,
        name="gmlp_fwd",
        grid=(seq // rows,),
        in_specs=[
            pl.BlockSpec((rows, 3 * A_WIDTH), lambda n: (n, 0)),
            _full((1, A_WIDTH)), _full((1, A_WIDTH)), _full((A_GROUPS, CHUNK, CHUNK)), _full((CHUNK, LANES)),
            _full((A_WIDTH, A_WIDTH)), _full((LANES, A_WIDTH)),
        ],
        out_specs=pl.BlockSpec((rows, A_WIDTH), lambda n: (n, 0)),
        out_shape=jax.ShapeDtypeStruct((seq, A_WIDTH), BF16),
        compiler_params=_cparams("parallel"),
    )(proj, ln_g, ln_b, w_s, bs_t, mean_m, gind)


def gmlp_bwd(proj, dy, ln_g, ln_b, w_s, bs_t):
    seq = proj.shape[0]
    rows = min(A_ROWS, seq)
    mean_m, gind = _gmlp_consts()

    def body(x_ref, dy_ref, g_ref, b_ref, w_ref, bs_ref, m_ref, gi_ref, dx_ref, dg_ref, db_ref, dw_ref, dbs_ref):
        fn = functools.partial(_gmlp_chunk, mean_m=m_ref[...], gind=gi_ref[...])
        _, vjp = jax.vjp(fn, x_ref[...], g_ref[...], b_ref[...], w_ref[...], bs_ref[...])
        dx, dg, db, dw, dbs = vjp(dy_ref[...])
        dx_ref[...] = dx.astype(BF16)

        @pl.when(pl.program_id(0) == 0)
        def _():
            dg_ref[...] = jnp.zeros_like(dg_ref)
            db_ref[...] = jnp.zeros_like(db_ref)
            dw_ref[...] = jnp.zeros_like(dw_ref)
            dbs_ref[...] = jnp.zeros_like(dbs_ref)

        dg_ref[...] += dg
        db_ref[...] += db
        dw_ref[...] += dw
        dbs_ref[...] += dbs

    return pl.pallas_call(
        body,
        name="gmlp_bwd",
        grid=(seq // rows,),
        in_specs=[
            pl.BlockSpec((rows, 3 * A_WIDTH), lambda n: (n, 0)),
            pl.BlockSpec((rows, A_WIDTH), lambda n: (n, 0)),
            _full((1, A_WIDTH)), _full((1, A_WIDTH)), _full((A_GROUPS, CHUNK, CHUNK)), _full((CHUNK, LANES)),
            _full((A_WIDTH, A_WIDTH)), _full((LANES, A_WIDTH)),
        ],
        out_specs=[
            pl.BlockSpec((rows, 3 * A_WIDTH), lambda n: (n, 0)),
            _full((1, A_WIDTH)), _full((1, A_WIDTH)), _full((A_GROUPS, CHUNK, CHUNK)), _full((CHUNK, LANES)),
        ],
        out_shape=[
            jax.ShapeDtypeStruct((seq, D_INT), BF16),
            jax.ShapeDtypeStruct((1, A_WIDTH), F32), jax.ShapeDtypeStruct((1, A_WIDTH), F32),
            jax.ShapeDtypeStruct((A_GROUPS, CHUNK, CHUNK), F32), jax.ShapeDtypeStruct((CHUNK, LANES), F32),
        ],
        compiler_params=_cparams("arbitrary"),
    )(proj, dy, ln_g, ln_b, w_s, bs_t, mean_m, gind)


B_WIDTH = 256
B_HEADS = 4
B_KDIM = 64
B_LEVELS = (64, 32, 16, 8, 4, 2, 1)


def _hgrn_consts():
    t = np.arange(CHUNK)
    u = t[None, :]
    mats = [np.tril(np.ones((CHUNK, CHUNK), np.float32))]
    for m in B_LEVELS:
        p = (t // (2 * m)) * (2 * m) + m - 1
        right = (t % (2 * m)) >= m
        sel = np.where(right[:, None], (u > p[:, None]) & (u <= t[:, None]), (u > t[:, None]) & (u <= p[:, None]))
        mats.append(sel.astype(np.float32))
    return jnp.asarray(np.concatenate(mats, 0), BF16), _group_ones_matrix(B_WIDTH, B_KDIM)


def _hgrn_lower_bound(lb0, lb1, layer):
    mx = jnp.maximum(lb0, lb1)
    e0 = jnp.exp(lb0 - mx)
    e1 = jnp.exp(lb1 - mx)
    p0 = e0 / (e0 + e1)
    p1 = e1 / (e0 + e1)
    cs = p0 if layer == 0 else p0 + p1
    return jnp.clip(cs - p0, 0.0, 1.0 - 1e-6)


def _hgrn_chunk(x4, st, lb0, lb1, onorm, layer, tstack, ones_bd):
    q_raw, fl, v, zg = (x4[:, i * B_WIDTH:(i + 1) * B_WIDTH] for i in range(4))
    lb = _hgrn_lower_bound(lb0, lb1, layer)
    q = jax.nn.silu(q_raw) * (B_KDIM ** -0.5)
    f = lb + (1.0 - lb) * jax.nn.sigmoid(fl)
    logf = jnp.log(jnp.maximum(f, F_FLOOR))
    k = (1.0 - lb) * jax.nn.sigmoid(-fl)
    b = _exact_times(tstack[:CHUNK], logf, 3)
    dall = jnp.concatenate([b, _exact_times(tstack[CHUNK:], logf, 2)], axis=0)
    b_last = jnp.sum(logf, axis=0, keepdims=True)
    vb = v.astype(BF16)

    lane_h = lax.shift_right_logical(lax.broadcasted_iota(jnp.int32, (CHUNK, B_WIDTH), 1), 6)
    row = lax.broadcasted_iota(jnp.int32, (CHUNK, B_WIDTH), 0)
    srow = lax.broadcasted_iota(jnp.int32, (B_HEADS * CHUNK, CHUNK), 0) & (CHUNK - 1)
    scol = lax.broadcasted_iota(jnp.int32, (B_HEADS * CHUNK, CHUNK), 1)

    def heads_on_rows(a):
        return jnp.concatenate([jnp.where(lane_h == h, a, 0.0) for h in range(B_HEADS)], axis=0)

    def heads_from_rows(r):
        out = jnp.where(lane_h == 0, r[:CHUNK], 0.0)
        for h in range(1, B_HEADS):
            out = out + jnp.where(lane_h == h, r[h * CHUNK:(h + 1) * CHUNK], 0.0)
        return out

    o = lax.dot_general((q * jnp.exp(b)).astype(BF16), st.astype(BF16), _NT, preferred_element_type=F32)
    scores = jnp.zeros((B_HEADS * CHUNK, CHUNK), F32)
    for li, m in enumerate(B_LEVELS):
        e = jnp.exp(dall[(li + 1) * CHUNK:(li + 2) * CHUNK])
        right = (row & (2 * m - 1)) >= m
        qt = jnp.where(right, q * e, 0.0)
        kt = jnp.where(right, 0.0, k * e)
        sc = lax.dot_general(heads_on_rows(qt).astype(BF16), kt.astype(BF16), _NT, preferred_element_type=F32)
        sh = int(np.log2(2 * m))
        same = lax.shift_right_logical(srow, sh) == lax.shift_right_logical(scol, sh)
        scores = scores + jnp.where(same, sc, 0.0)
    o = o + heads_from_rows(_dot(scores.astype(BF16), vb))
    o = o + _times_exact(q * k, ones_bd, 2) * v

    kv = lax.dot_general(vb, (k * jnp.exp(b_last - b)).astype(BF16), _TN, preferred_element_type=F32)
    st_new = st * jnp.exp(b_last) + jnp.where(ones_bd > 0.5, kv, 0.0)

    ms = _times_exact(o * o, ones_bd, 2) * (1.0 / B_KDIM)
    y = o * lax.rsqrt(ms + NORM_EPS) * onorm * jax.nn.silu(zg)
    return y, st_new


B_ROWS = 256


def _hgrn_rows(x4, st, lb0, lb1, onorm, layer, tstack, ones_bd):
    ys = []
    for i in range(x4.shape[0] // CHUNK):
        y, st = _hgrn_chunk(x4[i * CHUNK:(i + 1) * CHUNK], st, lb0, lb1, onorm, layer, tstack, ones_bd)
        ys.append(y)
    return jnp.concatenate(ys, axis=0), st


def hgrn_fwd(proj, lb0, lb1, onorm, layer):
    seq = proj.shape[0]
    rows = min(B_ROWS, seq)
    nc = seq // rows
    tstack, ones_bd = _hgrn_consts()

    def body(x_ref, lb0_ref, lb1_ref, on_ref, t_ref, e_ref, y_ref, st_out_ref, st_ref):
        @pl.when(pl.program_id(0) == 0)
        def _():
            st_ref[...] = jnp.zeros_like(st_ref)

        st = st_ref[...]
        st_out_ref[0] = st
        y, st_new = _hgrn_rows(x_ref[...], st, lb0_ref[...], lb1_ref[...], on_ref[...], layer, t_ref[...], e_ref[...])
        y_ref[...] = y.astype(BF16)
        st_ref[...] = st_new

    return pl.pallas_call(
        body,
        name=f"hgrn_fwd_{layer}",
        grid=(nc,),
        in_specs=[
            pl.BlockSpec((rows, 4 * B_WIDTH), lambda n: (n, 1)),
            _full((1, B_WIDTH)), _full((1, B_WIDTH)), _full((1, B_WIDTH)),
            _full(((len(B_LEVELS) + 1) * CHUNK, CHUNK)), _full((B_WIDTH, B_WIDTH)),
        ],
        out_specs=[
            pl.BlockSpec((rows, B_WIDTH), lambda n: (n, 0)),
            pl.BlockSpec((1, B_WIDTH, B_WIDTH), lambda n: (n, 0, 0)),
        ],
        out_shape=[jax.ShapeDtypeStruct((seq, B_WIDTH), BF16), jax.ShapeDtypeStruct((nc, B_WIDTH, B_WIDTH), F32)],
        scratch_shapes=[pltpu.VMEM((B_WIDTH, B_WIDTH), F32)],
        compiler_params=_cparams("arbitrary"),
    )(proj, lb0, lb1, onorm, tstack, ones_bd)


def hgrn_bwd(proj, states, dy, lb0, lb1, onorm, layer, dproj):
    seq = proj.shape[0]
    rows = min(B_ROWS, seq)
    nc = seq // rows
    tstack, ones_bd = _hgrn_consts()

    def body(x_ref, st_in_ref, dy_ref, lb0_ref, lb1_ref, on_ref, t_ref, e_ref, _, dx_ref, d0_ref, d1_ref, don_ref, dst_ref):
        @pl.when(pl.program_id(0) == 0)
        def _():
            dst_ref[...] = jnp.zeros_like(dst_ref)
            d0_ref[...] = jnp.zeros_like(d0_ref)
            d1_ref[...] = jnp.zeros_like(d1_ref)
            don_ref[...] = jnp.zeros_like(don_ref)

        fn = functools.partial(_hgrn_rows, layer=layer, tstack=t_ref[...], ones_bd=e_ref[...])
        _, vjp = jax.vjp(fn, x_ref[...], st_in_ref[0], lb0_ref[...], lb1_ref[...], on_ref[...])
        dx, dst, d0, d1, don = vjp((dy_ref[...], dst_ref[...]))
        dx_ref[...] = dx.astype(BF16)
        dst_ref[...] = dst
        d0_ref[...] += d0
        d1_ref[...] += d1
        don_ref[...] += don

    rev = lambda n: nc - 1 - n
    return pl.pallas_call(
        body,
        name=f"hgrn_bwd_{layer}",
        grid=(nc,),
        in_specs=[
            pl.BlockSpec((rows, 4 * B_WIDTH), lambda n: (rev(n), 1)),
            pl.BlockSpec((1, B_WIDTH, B_WIDTH), lambda n: (rev(n), 0, 0)),
            pl.BlockSpec((rows, B_WIDTH), lambda n: (rev(n), 1)),
            _full((1, B_WIDTH)), _full((1, B_WIDTH)), _full((1, B_WIDTH)),
            _full(((len(B_LEVELS) + 1) * CHUNK, CHUNK)), _full((B_WIDTH, B_WIDTH)), _ANY,
        ],
        out_specs=[
            pl.BlockSpec((rows, 4 * B_WIDTH), lambda n: (rev(n), 1)),
            _full((1, B_WIDTH)), _full((1, B_WIDTH)), _full((1, B_WIDTH)),
        ],
        out_shape=[jax.ShapeDtypeStruct(dproj.shape, BF16)] + [jax.ShapeDtypeStruct((1, B_WIDTH), F32)] * 3,
        input_output_aliases={8: 0},
        scratch_shapes=[pltpu.VMEM((B_WIDTH, B_WIDTH), F32)],
        compiler_params=_cparams("arbitrary"),
    )(proj, states, dy, lb0, lb1, onorm, tstack, ones_bd, dproj)


D_MODEL = 1024
D_INT = 4096


def _rms_stats(xf):
    r = lax.rsqrt(jnp.mean(xf * xf, axis=-1, keepdims=True) + NORM_EPS)
    return r, xf * r


def _rms_bwd(dy, g, r, xh):
    u = dy * g
    return r * (u - xh * jnp.mean(u * xh, axis=-1, keepdims=True))


C_QKV = (2048, 3584)
P_WIDTH = D_INT - (C_QKV[1] - C_QKV[0])
P_Z_BLOCK = C_QKV[0] // 512


def inproj(x, g, w, layer):
    seq = x.shape[0]
    tm = min(seq, 512)

    def body(x_ref, g_ref, w_ref, p_ref, qkv_ref, h_ref):
        _, xh = _rms_stats(x_ref[...])
        h = (xh * g_ref[...]).astype(BF16)
        h_ref[...] = h
        p_ref[:, :C_QKV[0]] = _dot(h, w_ref[0, :, :C_QKV[0]])
        qkv_ref[...] = _dot(h, w_ref[0, :, C_QKV[0]:C_QKV[1]]).astype(BF16)
        p_ref[:, C_QKV[0]:] = _dot(h, w_ref[0, :, C_QKV[1]:])

    rows = lambda n: pl.BlockSpec((tm, n), lambda i: (i, 0))
    return pl.pallas_call(
        body,
        name="inproj",
        grid=(seq // tm,),
        in_specs=[rows(D_MODEL), _full((1, D_MODEL)), pl.BlockSpec((1, D_MODEL, D_INT), lambda i: (layer, 0, 0))],
        out_specs=[rows(P_WIDTH), rows(C_QKV[1] - C_QKV[0]), rows(D_MODEL)],
        out_shape=[jax.ShapeDtypeStruct((seq, P_WIDTH), F32), jax.ShapeDtypeStruct((seq, C_QKV[1] - C_QKV[0]), BF16),
                   jax.ShapeDtypeStruct((seq, D_MODEL), BF16)],
        compiler_params=_cparams("parallel"),
    )(x, g, w)


def outproj(x, ya, yb, o, proj, wo, layer, head=None):
    seq = x.shape[0]
    tm = min(seq, 512)
    blk = wo.shape[2]

    def body(x_ref, ya_ref, yb_ref, o_ref, z_ref, w_ref, *refs):
        yc = (o_ref[...] * jax.nn.silu(z_ref[...])).astype(BF16)
        y = jnp.concatenate([ya_ref[...], yb_ref[...], yc], axis=1)
        w = jnp.concatenate([w_ref[d, 0] for d in range(N_DEV)], axis=0)
        xn = x_ref[...] + _dot(y, w)
        if head is None:
            xn_ref, y_ref = refs
            xn_ref[...] = xn
        else:
            g_ref, t_ref, dx_ref, y_ref, dg_ref, loss_ref = refs

            @pl.when(pl.program_id(0) == 0)
            def _():
                dg_ref[...] = jnp.zeros_like(dg_ref)
                loss_ref[...] = jnp.zeros_like(loss_ref)

            g = g_ref[...]
            r, xh = _rms_stats(xn)
            err = xh * g - t_ref[...]
            sq = jnp.sum(jnp.sum(err * err, axis=1, keepdims=True), axis=0, keepdims=True)
            loss_ref[...] += jnp.broadcast_to(sq * (0.5 / D_MODEL), loss_ref.shape)
            dout = err * (1.0 / D_MODEL)
            dg_ref[...] += jnp.sum(dout * xh, axis=0, keepdims=True)
            dx_ref[...] = _rms_bwd(dout, g, r, xh)
        y_ref[...] = y

    rows = lambda: pl.BlockSpec((tm, D_MODEL), lambda i: (i, 0))
    tail = (() if head is None else (_full((1, D_MODEL)), rows()),
            () if head is None else (_full((1, D_MODEL)), _full((8, LANES))),
            () if head is None else (jax.ShapeDtypeStruct((1, D_MODEL), F32), jax.ShapeDtypeStruct((8, LANES), F32)))
    return pl.pallas_call(
        body,
        name="outproj" if head is None else "outproj_loss",
        grid=(seq // tm,),
        in_specs=[
            rows(),
            pl.BlockSpec((tm, 256), lambda i: (i, 0)),
            pl.BlockSpec((tm, 256), lambda i: (i, 0)),
            pl.BlockSpec((tm, 512), lambda i: (i, 0)),
            pl.BlockSpec((tm, 512), lambda i: (i, P_Z_BLOCK)),
            pl.BlockSpec((N_DEV, 1, blk, D_MODEL), lambda i: (0, layer, 0, 0)),
            *tail[0],
        ],
        out_specs=[rows(), rows(), *tail[1]],
        out_shape=[jax.ShapeDtypeStruct((seq, D_MODEL), F32), jax.ShapeDtypeStruct((seq, D_MODEL), BF16), *tail[2]],
        compiler_params=_cparams("parallel" if head is None else "arbitrary"),
    )(x, ya, yb, o, proj, wo, *(head or ()))


def outproj_bwd(dx, y, wo, layer):
    seq = dx.shape[0]
    ts = min(seq, 512)
    blk = wo.shape[2]

    def body(dx_ref, y_ref, w_ref, dy_ref, dw_ref):
        @pl.when(pl.program_id(0) == 0)
        def _():
            dw_ref[...] = jnp.zeros_like(dw_ref)

        dxb = dx_ref[...].astype(BF16)
        w = jnp.concatenate([w_ref[d, 0] for d in range(N_DEV)], axis=0)
        dy_ref[...] = lax.dot_general(dxb, w, _NT, preferred_element_type=F32)
        dw = lax.dot_general(y_ref[...], dxb, _TN, preferred_element_type=F32)
        for d in range(N_DEV):
            dw_ref[d % 2, d // 2] += dw[d * blk:(d + 1) * blk]

    return pl.pallas_call(
        body,
        name="outproj_bwd",
        grid=(seq // ts,),
        in_specs=[
            pl.BlockSpec((ts, D_MODEL), lambda i: (i, 0)),
            pl.BlockSpec((ts, D_MODEL), lambda i: (i, 0)),
            pl.BlockSpec((N_DEV, 1, blk, D_MODEL), lambda i: (0, layer, 0, 0)),
        ],
        out_specs=[pl.BlockSpec((ts, D_MODEL), lambda i: (i, 0)),
                   pl.BlockSpec((2, N_CHIP, blk, D_MODEL), lambda i: (0, 0, 0, 0))],
        out_shape=[jax.ShapeDtypeStruct((seq, D_MODEL), F32), jax.ShapeDtypeStruct((2, N_CHIP, blk, D_MODEL), F32)],
        compiler_params=_cparams("arbitrary"),
    )(dx, y, wo)


def _dproj_parts(dp_ref, dqkv_refs, rows):
    lo, hi = C_QKV
    step = (hi - lo) // len(dqkv_refs)
    return ([(0, dp_ref.at[rows, 0:lo])] + [(lo + i * step, r.at[rows, :]) for i, r in enumerate(dqkv_refs)]
            + [(hi, dp_ref.at[rows, hi:D_INT])])


def inproj_bwd_x(dproj, dqkv, w, x, g, dx_in, layer, carried=None):
    seq = x.shape[0]
    tm = min(seq, 512)

    def body(dp_ref, dq_ref, dk_ref, dv_ref, w_ref, x_ref, g_ref, dxin_ref, dx_ref, dg_ref):
        @pl.when(pl.program_id(0) == 0)
        def _():
            dg_ref[...] = jnp.zeros_like(dg_ref)

        dh = None
        for at, part in _dproj_parts(dp_ref, (dq_ref, dk_ref, dv_ref), slice(None)):
            term = lax.dot_general(part[...], w_ref[0, :, at:at + part.shape[1]], _NT, preferred_element_type=F32)
            dh = term if dh is None else dh + term
        r, xh = _rms_stats(x_ref[...])
        dg_ref[...] += jnp.sum(dh * xh, axis=0, keepdims=True)
        dx_ref[...] = dxin_ref[...] + _rms_bwd(dh, g_ref[...], r, xh)

    third = lambda: pl.BlockSpec((tm, C_WIDTH), lambda i: (i, 0))
    return _call_carrying(
        carried, body, (dproj, *dqkv, w, x, g, dx_in),
        name="inproj_bwd_x",
        grid=(seq // tm,),
        in_specs=[
            pl.BlockSpec((tm, D_INT), lambda i: (i, 0)), third(), third(), third(),
            pl.BlockSpec((1, D_MODEL, D_INT), lambda i: (layer, 0, 0)),
            pl.BlockSpec((tm, D_MODEL), lambda i: (i, 0)),
            _full((1, D_MODEL)),
            pl.BlockSpec((tm, D_MODEL), lambda i: (i, 0)),
        ],
        out_specs=[pl.BlockSpec((tm, D_MODEL), lambda i: (i, 0)), _full((1, D_MODEL))],
        out_shape=[jax.ShapeDtypeStruct((seq, D_MODEL), F32), jax.ShapeDtypeStruct((1, D_MODEL), F32)],
        scratch_shapes=[], semantics=("arbitrary",),
    )


def inproj_bwd_w(h, dproj, dqkv):
    seq = h.shape[0]
    ts, tn = min(seq, 512), 512

    def body(h_ref, dp_ref, dq_ref, dk_ref, dv_ref, dw_ref):
        @pl.when(pl.program_id(0) == 0)
        def _():
            dw_ref[...] = jnp.zeros_like(dw_ref)

        ht = h_ref[...].T
        for at, part in _dproj_parts(dp_ref, (dq_ref, dk_ref, dv_ref), slice(None)):
            for c in range(0, part.shape[1], tn):
                dw_ref[0, :, at + c:at + c + tn] += _dot(ht, part[:, c:c + tn])

    third = lambda: pl.BlockSpec((ts, C_WIDTH), lambda s: (s, 0))
    return pl.pallas_call(
        body,
        name="inproj_bwd_w",
        grid=(seq // ts,),
        in_specs=[pl.BlockSpec((ts, D_MODEL), lambda s: (s, 0)), pl.BlockSpec((ts, D_INT), lambda s: (s, 0)),
                  third(), third(), third()],
        out_specs=_full((1, D_MODEL, D_INT)),
        out_shape=jax.ShapeDtypeStruct((1, D_MODEL, D_INT), F32),
        compiler_params=_cparams("arbitrary"),
    )(h, dproj, *dqkv)


N_IN = 3848


def _internal_of(col):
    return col if col < 768 else (col + 256 if col < 3840 else 768 + col - 3840)


def _column_runs(n_shard):
    runs = []
    for d in range(N_IN // n_shard):
        mine = []
        for j in range(n_shard):
            ci = _internal_of(d * n_shard + j)
            if mine and mine[-1][0] + mine[-1][1] == ci:
                mine[-1][1] += 1
            else:
                mine.append([ci, 1, j])
        runs.append(mine)
    return runs


def assemble_w_in(wi_all):
    n_dev, depth, _, n_shard = wi_all.shape
    tr = 256
    pieces = [[] for _ in range(D_INT // LANES)]
    for d, mine in enumerate(_column_runs(n_shard)):
        for ci, ln, off in mine:
            while ln > 0:
                blk, at = divmod(ci, LANES)
                take = min(ln, LANES - at)
                pieces[blk].append((at, take, d, off))
                ci, ln, off = ci + take, ln - take, off + take

    def body(x_ref, o_ref):
        for blk, parts in enumerate(pieces):
            vals, at = [], 0
            for start, ln, d, off in sorted(parts):
                if start > at:
                    vals.append(jnp.zeros((tr, start - at), BF16))
                vals.append(x_ref[d, 0, :, off:off + ln])
                at = start + ln
            if at < LANES:
                vals.append(jnp.zeros((tr, LANES - at), BF16))
            o_ref[0, :, blk * LANES:(blk + 1) * LANES] = vals[0] if len(vals) == 1 else jnp.concatenate(vals, axis=1)

    return pl.pallas_call(
        body,
        name="assemble_w_in",
        grid=(depth, D_MODEL // tr),
        in_specs=[pl.BlockSpec((n_dev, 1, tr, n_shard), lambda l, r: (0, l, r, 0))],
        out_specs=pl.BlockSpec((1, tr, D_INT), lambda l, r: (l, r, 0)),
        out_shape=jax.ShapeDtypeStruct((depth, D_MODEL, D_INT), BF16),
        compiler_params=_cparams("parallel", "parallel"),
    )(wi_all)


def split_w_in_grad(dwi, n_shard, core):
    tr = 256
    runs = _column_runs(n_shard)

    def body(core_ref, x_ref, keep_ref, send_ref):
        for d, mine in enumerate(runs):
            @pl.when(core_ref[0] == d % 2)
            def _():
                for ci, ln, off in mine:
                    keep_ref[d // 2, :, off:off + ln] = x_ref[0, :, ci:ci + ln]

            @pl.when(core_ref[0] != d % 2)
            def _():
                for ci, ln, off in mine:
                    send_ref[d // 2, :, off:off + ln] = x_ref[0, :, ci:ci + ln].astype(BF16)

    shards = lambda: pl.BlockSpec((N_CHIP, tr, n_shard), lambda r, s: (0, r, 0))
    grid_spec = pltpu.PrefetchScalarGridSpec(
        num_scalar_prefetch=1, grid=(D_MODEL // tr,),
        in_specs=[pl.BlockSpec((1, tr, D_INT), lambda r, s: (0, r, 0))], out_specs=[shards(), shards()])
    return pl.pallas_call(
        body,
        name="split_w_in_grad",
        grid_spec=grid_spec,
        out_shape=[jax.ShapeDtypeStruct((N_CHIP, D_MODEL, n_shard), F32), jax.ShapeDtypeStruct((N_CHIP, D_MODEL, n_shard), BF16)],
        compiler_params=_cparams("parallel"),
    )(core, dwi)


C_WIDTH = 512
C_HEADS = 8
C_HDIM = 64
C_PAIRS = C_HEADS // 2
C_BQ = 512
C_TAIL = 16
C_KG = 4


def _split3(x):
    hi = x.astype(BF16)
    r = x - hi.astype(F32)
    mid = r.astype(BF16)
    return hi, mid, (r - mid.astype(F32)).astype(BF16)


def _piece_selectors():
    sel = np.zeros((C_HEADS, 3 * LANES, LANES), np.float32)
    for p in range(C_PAIRS):
        for e in range(2):
            for t in range(3):
                sel[2 * p + e, t * LANES + 2 * p + e, 3 * e + t] = -1.0
    return sel


def fox_prep(proj, qkv, bf_row):
    seq = proj.shape[0]
    nblk = seq // CHUNK
    nb = min(nblk, 4)
    tril = jnp.asarray(np.tril(np.ones((CHUNK, CHUNK), np.float32)), BF16)
    sel = jnp.asarray(_piece_selectors(), BF16)
    rows_t = CHUNK + C_TAIL

    def body(fl_ref, q_ref, k_ref, v_ref, bf_ref, l_ref, sel_ref, ka_ref, va_ref, vt_ref, kt_ref, qt_ref, qa_ref, carry_ref):
        @pl.when(pl.program_id(0) == 0)
        def _():
            carry_ref[...] = jnp.zeros_like(carry_ref)

        lane = lax.broadcasted_iota(jnp.int32, (CHUNK, LANES), 1)
        row = lax.broadcasted_iota(jnp.int32, (CHUNK, LANES), 0)
        r16 = lax.broadcasted_iota(jnp.int32, (C_TAIL, 2 * CHUNK), 0)
        l16 = lax.broadcasted_iota(jnp.int32, (C_TAIL, 2 * CHUNK), 1)
        zero = jnp.zeros((CHUNK, LANES), BF16)
        one = jnp.ones((CHUNK, LANES), BF16)

        def by_keys(x, right_a, right_b):
            xb = x.astype(BF16)
            top = jnp.concatenate([jnp.where(lane < C_HDIM, xb, zero), right_a], axis=1)
            return jnp.concatenate([top, jnp.concatenate([jnp.where(lane < C_HDIM, zero, xb), right_b], axis=1)], axis=0)

        def by_lanes(x, tail):
            xt = x.T.astype(BF16)
            main = jnp.concatenate([jnp.where(row < C_HDIM, xt, zero), jnp.where(row < C_HDIM, zero, xt)], axis=1)
            return jnp.concatenate([main, tail], axis=0)

        for j in range(nb):
            tok, wide2 = slice(j * CHUNK, (j + 1) * CHUNK), slice(j * 2 * CHUNK, (j + 1) * 2 * CHUNK)
            lf = jax.nn.log_sigmoid(fl_ref[tok, :LANES] + bf_ref[...])
            c = _exact_times(l_ref[...], lf, 3) + carry_ref[...]
            carry_ref[...] += jnp.sum(lf, axis=0, keepdims=True)
            c3 = jnp.concatenate(_split3(c), axis=1)
            for p in range(C_PAIRS):
                cols = slice(p * LANES, (p + 1) * LANES)
                q2, k2, v2 = (r[tok, cols].astype(F32) for r in (q_ref, k_ref, v_ref))
                q2 = q2 * (C_HDIM ** -0.5)
                negc = [_dot(c3, sel_ref[2 * p + e]).astype(BF16) for e in range(2)]
                ones3 = [jnp.where((lane >= 3 * e) & (lane < 3 * e + 3), one, zero) for e in range(2)]
                tail = jnp.where(((r16 == 2 * p) & (l16 < CHUNK)) | ((r16 == 2 * p + 1) & (l16 >= CHUNK)), 1.0, 0.0).astype(BF16)
                ka_ref[p, wide2] = by_keys(k2, negc[0], negc[1])
                va_ref[p, wide2] = by_keys(v2, ones3[0], ones3[1])
                kt_ref[p, :, wide2] = by_lanes(k2, tail)
                vt_ref[p, :, wide2] = by_lanes(v2, tail)
                qt_ref[p, :, tok] = jnp.concatenate([q2.T.astype(BF16), jnp.where(row < 6, one, zero)], axis=0)
                qa_ref[p, tok] = jnp.concatenate([q2.astype(BF16), jnp.where((lane == 2 * p) | (lane == 2 * p + 1), one, zero)], axis=1)

    wide = lambda j: pl.BlockSpec((nb * CHUNK, C_WIDTH), lambda n: (n, j))
    by_rows = pl.BlockSpec((C_PAIRS, nb * 2 * CHUNK, 2 * CHUNK), lambda n: (0, n, 0))
    by_cols = pl.BlockSpec((C_PAIRS, rows_t, nb * 2 * CHUNK), lambda n: (0, 0, n))
    return pl.pallas_call(
        body,
        name="fox_prep",
        grid=(nblk // nb,),
        in_specs=[pl.BlockSpec((nb * CHUNK, 256), lambda n: (n, 3)), wide(0), wide(1), wide(2), _full((1, LANES)),
                  _full((CHUNK, CHUNK)), _full((C_HEADS, 3 * LANES, LANES))],
        out_specs=[by_rows, by_rows, by_cols, by_cols,
                   pl.BlockSpec((C_PAIRS, 2 * CHUNK, nb * CHUNK), lambda n: (0, 0, n)),
                   pl.BlockSpec((C_PAIRS, nb * CHUNK, 2 * CHUNK), lambda n: (0, n, 0))],
        out_shape=[jax.ShapeDtypeStruct((C_PAIRS, 2 * seq, 2 * CHUNK), BF16)] * 2
        + [jax.ShapeDtypeStruct((C_PAIRS, rows_t, 2 * seq), BF16)] * 2
        + [jax.ShapeDtypeStruct((C_PAIRS, 2 * CHUNK, seq), BF16), jax.ShapeDtypeStruct((C_PAIRS, seq, 2 * CHUNK), BF16)],
        scratch_shapes=[pltpu.VMEM((1, LANES), F32)],
        compiler_params=_cparams("arbitrary"),
    )(proj, qkv, qkv, qkv, bf_row, tril, sel)


def _visible(shape, key0, query0):
    row = lax.broadcasted_iota(jnp.int32, shape, 0)
    key = key0 + lax.shift_left(lax.shift_right_logical(row, 8), 7) + (row & (CHUNK - 1))
    return key <= query0 + lax.broadcasted_iota(jnp.int32, shape, 1)


def _rows_ab(a, b, n):
    return jnp.concatenate([jnp.broadcast_to(a, (C_HDIM, n)), jnp.broadcast_to(b, (C_HDIM, n))], axis=0)


def _call_carrying(ex, body, operands, *, name, grid, in_specs, out_specs, out_shape, scratch_shapes, semantics=None):
    if ex is None:
        semantics = semantics or ("parallel", *["arbitrary"] * (len(grid) - 1))
        return pl.pallas_call(body, name=name, grid=grid, in_specs=in_specs, out_specs=out_specs, out_shape=out_shape,
                              scratch_shapes=scratch_shapes, compiler_params=_cparams(*semantics))(*operands)
    n_in, n_out = len(in_specs), len(out_specs)

    def wrapped(*refs):
        own, parts = _carried_refs(refs, n_in, n_out, ex)
        ids = [pl.program_id(a) for a in range(len(grid))]
        pl.when(functools.reduce(jnp.logical_and, [i == 0 for i in ids]))(lambda: ex.start(*parts))
        if hasattr(ex, "relay"):
            linear = functools.reduce(lambda at, ig: at * ig[1] + ig[0], zip(ids, grid), 0)
            pl.when(linear == int(np.prod(grid)) // 2)(lambda: ex.relay(*parts))
        body(*own)
        pl.when(functools.reduce(jnp.logical_and, [i == g - 1 for i, g in zip(ids, grid)]))(lambda: ex.finish(*parts))

    return pl.pallas_call(
        wrapped, name=name, grid=grid,
        in_specs=list(in_specs) + [_ANY] * len(ex.inputs), out_specs=list(out_specs) + [_ANY] * len(ex.out_shape),
        out_shape=list(out_shape) + list(ex.out_shape), scratch_shapes=list(scratch_shapes) + list(ex.scratch),
        input_output_aliases={n_in + i: n_out + o for i, o in getattr(ex, "aliases", {}).items()},
        compiler_params=_cparams(*["arbitrary"] * len(grid)),
    )(*operands, *ex.inputs)


def fox_fwd(qt, ka, vt, carried=None):
    seq = qt.shape[2]
    nblk = seq // CHUNK
    bq = min(C_BQ, seq)
    grp = bq // CHUNK
    rows_t = CHUNK + C_TAIL

    def body(qt_ref, ka_ref, vt_ref, o_ref, lse_ref, acc_ref, s_ref):
        p, i = pl.program_id(0), pl.program_id(1)
        qtile = qt_ref[0]
        r16 = lax.broadcasted_iota(jnp.int32, (C_TAIL, bq), 0)

        def scores(t):
            at = pl.multiple_of(t * grp * 2 * CHUNK, 2 * CHUNK)
            return _dot(ka_ref[0, pl.ds(at, grp * 2 * CHUNK), :], qtile)

        def rescale(al_a, al_b):
            tail = jnp.where(r16 == 2 * p, al_a, jnp.where(r16 == 2 * p + 1, al_b, 1.0))
            return jnp.concatenate([_rows_ab(al_a, al_b, bq), tail], axis=0)

        def diagonal(m):
            ma, mb = m
            na, nb = ma, mb
            blocks = []
            for g in range(grp):
                s = s_ref[g * 2 * CHUNK:(g + 1) * 2 * CHUNK, g * CHUNK:]
                s = jnp.where(_visible(s.shape, i * bq + g * CHUNK, i * bq + g * CHUNK), s, -jnp.inf)
                blocks.append(s)
                unseen = [jnp.full((1, g * CHUNK), -jnp.inf, F32)] if g else []
                na = jnp.maximum(na, jnp.concatenate(unseen + [jnp.max(s[:CHUNK], axis=0, keepdims=True)], axis=1))
                nb = jnp.maximum(nb, jnp.concatenate(unseen + [jnp.max(s[CHUNK:], axis=0, keepdims=True)], axis=1))
            acc_ref[...] = acc_ref[...] * rescale(jnp.exp(ma - na), jnp.exp(mb - nb))
            for g in range(grp):
                n = bq - g * CHUNK
                n2 = jnp.concatenate([jnp.broadcast_to(na[:, g * CHUNK:], (CHUNK, n)),
                                      jnp.broadcast_to(nb[:, g * CHUNK:], (CHUNK, n))], axis=0)
                at = pl.multiple_of((i * grp + g) * 2 * CHUNK, 2 * CHUNK)
                pt = jnp.exp(blocks[g] - n2).astype(BF16)
                acc_ref[:, g * CHUNK:] += _dot(vt_ref[0, :, pl.ds(at, 2 * CHUNK)], pt)
            return na, nb

        def group(t, m):
            ma, mb = m
            at = pl.multiple_of(t * grp * 2 * CHUNK, 2 * CHUNK)
            s = s_ref[...]
            sa = [s[g * 2 * CHUNK:g * 2 * CHUNK + CHUNK] for g in range(grp)]
            sb = [s[g * 2 * CHUNK + CHUNK:(g + 1) * 2 * CHUNK] for g in range(grp)]
            na, nb = ma, mb
            for g in range(grp):
                na = jnp.maximum(na, jnp.max(sa[g], axis=0, keepdims=True))
                nb = jnp.maximum(nb, jnp.max(sb[g], axis=0, keepdims=True))
            al_a, al_b = jnp.exp(ma - na), jnp.exp(mb - nb)
            pt = jnp.concatenate([jnp.exp(x - n) for g in range(grp) for x, n in ((sa[g], na), (sb[g], nb))], axis=0)
            pv = _dot(vt_ref[0, :, pl.ds(at, grp * 2 * CHUNK)], pt.astype(BF16))
            acc_ref[...] = acc_ref[...] * rescale(al_a, al_b) + pv
            return na, nb

        def step(t, m):
            s_next = scores(t + 1)
            m = group(t, m)
            s_ref[...] = s_next
            return m

        acc_ref[...] = jnp.zeros_like(acc_ref)
        s_ref[...] = scores(0)
        m = (jnp.full((1, bq), -jnp.inf, F32), jnp.full((1, bq), -jnp.inf, F32))
        m = lax.fori_loop(0, i, step, m)
        ma, mb = diagonal(m)
        tailv = acc_ref[CHUNK:rows_t, :]
        la = jnp.sum(jnp.where(r16 == 2 * p, tailv, 0.0), axis=0, keepdims=True)
        lb = jnp.sum(jnp.where(r16 == 2 * p + 1, tailv, 0.0), axis=0, keepdims=True)
        o_ref[...] = (acc_ref[0:CHUNK, :] * _rows_ab(1.0 / la, 1.0 / lb, bq)).T
        lse_ref[0, 0:1, :] = ma + jnp.log(la)
        lse_ref[0, 1:2, :] = mb + jnp.log(lb)

    return _call_carrying(
        carried, body, (qt, ka, vt),
        name="fox_fwd",
        grid=(C_PAIRS, seq // bq),
        in_specs=[
            pl.BlockSpec((1, 2 * CHUNK, bq), lambda p, i: (p, 0, i)),
            pl.BlockSpec((1, 2 * seq, 2 * CHUNK), lambda p, i: (p, 0, 0)),
            pl.BlockSpec((1, rows_t, 2 * seq), lambda p, i: (p, 0, 0)),
        ],
        out_specs=[pl.BlockSpec((bq, LANES), lambda p, i: (i, p)), pl.BlockSpec((1, 2, bq), lambda p, i: (p, 0, i))],
        out_shape=[jax.ShapeDtypeStruct((seq, C_WIDTH), F32), jax.ShapeDtypeStruct((C_PAIRS, 2, seq), F32)],
        scratch_shapes=[pltpu.VMEM((rows_t, bq), F32), pltpu.VMEM((grp * 2 * CHUNK, bq), F32)],
    )


def fox_bwd_prep(dy, o, proj, dproj):
    seq = o.shape[0]
    rows = min(seq, 512)
    ind = np.zeros((C_WIDTH, LANES), np.float32)
    for h in range(C_HEADS):
        ind[h * C_HDIM:(h + 1) * C_HDIM, h] = 1.0
    ind = jnp.asarray(ind, BF16)
    sel = _piece_selectors()
    sel = jnp.asarray(np.stack([sel[2 * p].T + sel[2 * p + 1].T for p in range(C_PAIRS)]), BF16)

    def body(dy_ref, o_ref, z_ref, ind_ref, sel_ref, _, do_ref, dz_ref, dot_ref):
        dy_c, o_v, z = dy_ref[...], o_ref[...], z_ref[...]
        sg = jax.nn.sigmoid(z)
        do = dy_c * (z * sg)
        do_ref[...] = do.astype(BF16)
        dz_ref[...] = (dy_c * o_v * (sg * (1.0 + z * (1.0 - sg)))).astype(BF16)
        prod = do * o_v
        hi = prod.astype(BF16)
        lo = (prod - hi.astype(F32)).astype(BF16)
        delta = _dot(hi, ind_ref[...]) + _dot(lo, ind_ref[...])
        d3 = jnp.concatenate(_split3(delta.T), axis=0)
        for p in range(C_PAIRS):
            tail = _dot(sel_ref[p], d3).astype(BF16)
            dot_ref[p] = jnp.concatenate([do[:, p * LANES:(p + 1) * LANES].T.astype(BF16), tail], axis=0)

    return pl.pallas_call(
        body,
        name="fox_bwd_prep",
        grid=(seq // rows,),
        in_specs=[
            pl.BlockSpec((rows, C_WIDTH), lambda i: (i, 1)),
            pl.BlockSpec((rows, C_WIDTH), lambda i: (i, 0)),
            pl.BlockSpec((rows, C_WIDTH), lambda i: (i, P_Z_BLOCK)),
            _full((C_WIDTH, LANES)), _full((C_PAIRS, LANES, 3 * LANES)), _ANY,
        ],
        out_specs=[
            pl.BlockSpec((rows, C_WIDTH), lambda i: (i, 0)),
            pl.BlockSpec((rows, C_WIDTH), lambda i: (i, 7)),
            pl.BlockSpec((C_PAIRS, 2 * CHUNK, rows), lambda i: (0, 0, i)),
        ],
        out_shape=[jax.ShapeDtypeStruct((seq, C_WIDTH), BF16), jax.ShapeDtypeStruct(dproj.shape, BF16),
                   jax.ShapeDtypeStruct((C_PAIRS, 2 * CHUNK, seq), BF16)],
        input_output_aliases={5: 1},
        compiler_params=_cparams("parallel"),
    )(dy, o, proj, ind, sel, dproj)


def fox_bwd(ka, va, kt, qt, dot_t, qa, dob, lse, carried=None):
    seq = qt.shape[2]
    nblk = seq // CHUNK
    bq = min(C_BQ, seq)
    nq = seq // bq
    kg = min(C_KG, nblk)
    ng = nblk // kg
    rows_t = CHUNK + C_TAIL

    def body(ka_ref, va_ref, kt_ref, qt_ref, dot_ref, qa_ref, do_ref, lse_ref,
             dq_ref, dk_ref, dv_ref, dck_ref, dcq_ref, dqt_acc, dv_acc, dka_acc):
        p, jg = pl.program_id(0), pl.program_id(1)

        @pl.when(jg == 0)
        def _():
            dqt_acc[...] = jnp.zeros_like(dqt_acc)

        dv_acc[...] = jnp.zeros_like(dv_acc)
        dka_acc[...] = jnp.zeros_like(dka_acc)

        def step(i, carry):
            cols = pl.ds(pl.multiple_of(i * bq, bq), bq)
            qtile, dotile = qt_ref[0, :, cols], dot_ref[0, :, cols]
            do, qa_i = do_ref[cols, :], qa_ref[0, cols, :]
            lse2 = jnp.concatenate([jnp.broadcast_to(lse_ref[0, 0:1, cols], (CHUNK, bq)),
                                    jnp.broadcast_to(lse_ref[0, 1:2, cols], (CHUNK, bq))] * kg, axis=0)
            pt = jnp.exp(_dot(ka_ref[0], qtile) - lse2)
            ds = pt * _dot(va_ref[0], dotile)
            ptb, dsb = pt.astype(BF16), ds.astype(BF16)
            dv_acc[...] += _dot(ptb, do)
            dka_acc[...] += _dot(dsb, qa_i)
            dqt_acc[:, cols] += _dot(kt_ref[0], dsb)
            return carry

        def diagonal(i):
            cols = [pl.ds(pl.multiple_of(i * bq + kb * CHUNK, CHUNK), bq - kb * CHUNK) for kb in range(kg)]
            rows = [slice(kb * 2 * CHUNK, (kb + 1) * 2 * CHUNK) for kb in range(kg)]
            s = [_dot(ka_ref[0, rows[kb], :], qt_ref[0, :, cols[kb]]) for kb in range(kg)]
            dp = [_dot(va_ref[0, rows[kb], :], dot_ref[0, :, cols[kb]]) for kb in range(kg)]
            ptb, dsb = [], []
            for kb in range(kg):
                n = bq - kb * CHUNK
                lse2 = jnp.concatenate([jnp.broadcast_to(lse_ref[0, 0:1, cols[kb]], (CHUNK, n)),
                                        jnp.broadcast_to(lse_ref[0, 1:2, cols[kb]], (CHUNK, n))], axis=0)
                pt = jnp.exp(s[kb] - lse2)
                pt = jnp.where(_visible(pt.shape, (jg * kg + kb) * CHUNK, i * bq + kb * CHUNK), pt, 0.0)
                ptb.append(pt.astype(BF16))
                dsb.append((pt * dp[kb]).astype(BF16))
            for kb in range(kg):
                dv_acc[rows[kb], :] += _dot(ptb[kb], do_ref[cols[kb], :])
                dka_acc[rows[kb], :] += _dot(dsb[kb], qa_ref[0, cols[kb], :])
                dqt_acc[:, cols[kb]] += _dot(kt_ref[0, :, rows[kb]], dsb[kb])

        assert kg * CHUNK == bq
        diagonal(jg)
        lax.fori_loop(jg + 1, nq, step, 0)
        lane = lax.broadcasted_iota(jnp.int32, (CHUNK, LANES), 1)
        for kb in range(kg):
            rows = slice(kb * CHUNK, (kb + 1) * CHUNK)
            ra = slice(kb * 2 * CHUNK, kb * 2 * CHUNK + CHUNK)
            rb = slice(kb * 2 * CHUNK + CHUNK, (kb + 1) * 2 * CHUNK)
            dk_ref[rows, :] = jnp.where(lane < C_HDIM, dka_acc[ra, 0:LANES], dka_acc[rb, 0:LANES]).astype(BF16)
            dv_ref[rows, :] = jnp.where(lane < C_HDIM, dv_acc[ra, :], dv_acc[rb, :]).astype(BF16)
            dck_ref[0, rows, :] = (jnp.where(lane == 2 * p, dka_acc[ra, LANES:], 0.0)
                                   + jnp.where(lane == 2 * p + 1, dka_acc[rb, LANES:], 0.0))

        @pl.when(jg == ng - 1)
        def _():
            for c in range(nq):
                dq_ref[c * bq:(c + 1) * bq, :] = (dqt_acc[0:CHUNK, c * bq:(c + 1) * bq].T * (C_HDIM ** -0.5)).astype(BF16)
            dcq_ref[0] = dqt_acc[CHUNK:rows_t, :]

    per_pair = lambda r, c: pl.BlockSpec((1, r, c), lambda p, j: (p, 0, 0))
    by_rows = pl.BlockSpec((1, kg * 2 * CHUNK, 2 * CHUNK), lambda p, j: (p, j, 0))
    by_cols = pl.BlockSpec((1, rows_t, kg * 2 * CHUNK), lambda p, j: (p, 0, j))
    return _call_carrying(
        carried, body, (ka, va, kt, qt, dot_t, qa, dob, lse),
        name="fox_bwd",
        grid=(C_PAIRS, ng),
        in_specs=[by_rows, by_rows, by_cols, per_pair(2 * CHUNK, seq), per_pair(2 * CHUNK, seq),
                  per_pair(seq, 2 * CHUNK), pl.BlockSpec((seq, LANES), lambda p, j: (0, p)), per_pair(2, seq)],
        out_specs=[pl.BlockSpec((seq, LANES), lambda p, j: (0, p)),
                   pl.BlockSpec((kg * CHUNK, LANES), lambda p, j: (j, p)),
                   pl.BlockSpec((kg * CHUNK, LANES), lambda p, j: (j, p)),
                   pl.BlockSpec((1, kg * CHUNK, LANES), lambda p, j: (p, j, 0)),
                   per_pair(C_TAIL, seq)],
        out_shape=[jax.ShapeDtypeStruct((seq, C_WIDTH), BF16)] * 3
        + [jax.ShapeDtypeStruct((C_PAIRS, seq, LANES), F32), jax.ShapeDtypeStruct((C_PAIRS, C_TAIL, seq), F32)],
        scratch_shapes=[pltpu.VMEM((rows_t, seq), F32), pltpu.VMEM((kg * 2 * CHUNK, LANES), F32),
                        pltpu.VMEM((kg * 2 * CHUNK, 2 * CHUNK), F32)],
    )


def fox_post(dcq, dck, proj, bf_row, dproj):
    seq = proj.shape[0]
    rows = min(seq, 512)
    nc = seq // rows
    triu = jnp.asarray(np.triu(np.ones((CHUNK, CHUNK), np.float32)), BF16)

    def body(dq_ref, dk_ref, fl_ref, bf_ref, u_ref, _, dfl_ref, dbf_ref, carry_ref):
        @pl.when(pl.program_id(0) == 0)
        def _():
            carry_ref[...] = jnp.zeros_like(carry_ref)
            dbf_ref[...] = jnp.zeros_like(dbf_ref)

        for j in reversed(range(rows // CHUNK)):
            at = slice(j * CHUNK, (j + 1) * CHUNK)
            heads = (dq_ref[0, :, at] + dq_ref[1, :, at]) + (dq_ref[2, :, at] + dq_ref[3, :, at])
            dc = jnp.concatenate([heads, jnp.zeros((CHUNK - C_TAIL, CHUNK), F32)], axis=0).T
            dc = dc - ((dk_ref[0, at] + dk_ref[1, at]) + (dk_ref[2, at] + dk_ref[3, at]))
            g = _exact_times(u_ref[...], dc, 3) + carry_ref[...]
            carry_ref[...] += jnp.sum(dc, axis=0, keepdims=True)
            dfl = g * jax.nn.sigmoid(-(fl_ref[at, :LANES] + bf_ref[...]))
            dbf_ref[...] += jnp.sum(dfl, axis=0, keepdims=True)
            dfl_ref[at, :] = jnp.concatenate([dfl, jnp.zeros_like(dfl)], axis=1).astype(BF16)

    rev = lambda n: nc - 1 - n
    return pl.pallas_call(
        body,
        name="fox_post",
        grid=(nc,),
        in_specs=[
            pl.BlockSpec((C_PAIRS, C_TAIL, rows), lambda n: (0, 0, rev(n))),
            pl.BlockSpec((C_PAIRS, rows, LANES), lambda n: (0, rev(n), 0)),
            pl.BlockSpec((rows, 256), lambda n: (rev(n), 3)),
            _full((1, LANES)), _full((CHUNK, CHUNK)), _ANY,
        ],
        out_specs=[pl.BlockSpec((rows, 256), lambda n: (rev(n), 3)), _full((1, LANES))],
        out_shape=[jax.ShapeDtypeStruct(dproj.shape, BF16), jax.ShapeDtypeStruct((1, LANES), F32)],
        input_output_aliases={5: 0},
        scratch_shapes=[pltpu.VMEM((1, LANES), F32)],
        compiler_params=_cparams("arbitrary"),
    )(dcq, dck, proj, bf_row, triu, dproj)


N_DEV = 8
MESH = pl.DeviceIdType.MESH
_ANY = pl.BlockSpec(memory_space=pl.ANY)


def _mesh_pos():
    return lax.axis_index("x"), lax.axis_index("y"), lax.axis_index("c")


def _dev_index(px, py, pc):
    return 4 * px + 2 * py + pc


def _row_pieces(ref, rows):
    return [ref.at[idx + (pl.ds(r, rows),)] for idx in np.ndindex(*ref.shape[:-2]) for r in range(0, ref.shape[-2], rows)]


class _Transfer:
    def __init__(self, src, dst, rows, send_sem, recv_sem, to):
        self.src, self.dst, self.rows, self.sems, self.to = src, dst, rows, (send_sem, recv_sem), to

    def _copy(self, src, dst):
        return pltpu.make_async_remote_copy(src_ref=src, dst_ref=dst, send_sem=self.sems[0], recv_sem=self.sems[1],
                                            device_id=self.to, device_id_type=MESH)

    def start(self):
        for s, d in zip(_row_pieces(self.src, self.rows), _row_pieces(self.dst, self.rows), strict=True):
            self._copy(s, d).start()

    def wait_send(self):
        self._copy(self.src, self.dst).wait_send()

    def wait_recv(self):
        self._copy(self.src, self.dst).wait_recv()


def _exchange_call(ex, name):
    n_in, n_out = len(ex.inputs), len(ex.out_shape)

    def body(*refs):
        parts = refs[:n_in], refs[n_in:n_in + n_out], refs[n_in + n_out:]
        ex.start(*parts)
        getattr(ex, "relay", lambda *_: None)(*parts)
        ex.finish(*parts)

    return pl.pallas_call(body, name=name, in_specs=[_ANY] * n_in, out_specs=[_ANY] * n_out, out_shape=ex.out_shape,
                          scratch_shapes=ex.scratch, input_output_aliases=getattr(ex, "aliases", {}))(*ex.inputs)


def _carried_refs(refs, n_in, n_out, ex):
    k_in, k_out, k_sem = (len(ex.inputs), len(ex.out_shape), len(ex.scratch)) if ex else (0, 0, 0)
    a, b, c = n_in + k_in, n_in + k_in + n_out, n_in + k_in + n_out + k_out
    own = refs[:n_in] + refs[a:b] + refs[c:len(refs) - k_sem]
    return own, (refs[n_in:a], refs[b:c], refs[len(refs) - k_sem:])


class AllGatherWeights:
    def __init__(self, blocks):
        n = len(blocks)
        self.inputs = tuple(blocks)
        self.out_shape = [jax.ShapeDtypeStruct((N_DEV,) + b.shape, b.dtype) for b in blocks]
        self.scratch = ([pltpu.SemaphoreType.DMA((n, 7)), pltpu.SemaphoreType.DMA((n, 7)), pltpu.SemaphoreType.DMA((n, 2))]
                        + [pltpu.VMEM(b.shape, b.dtype) for b in blocks])

    def _plan(self, ins, outs, scratch):
        send_sems, recv_sems, local_sems, *staged = scratch
        x, y, c = _mesh_pos()
        me, sibling = (x, y, c), (x, y, 1 - c)
        chips = [(1 - x, y), (x, 1 - y), (1 - x, 1 - y)]
        every = range(len(ins))

        def copy(a, k, block, to, own=False):
            slot = outs[a].at[_dev_index(*block)]
            return _Transfer(ins[a] if own else slot, slot, ins[a].shape[-2], send_sems.at[a, k], recv_sems.at[a, k], to)

        mine = [(pltpu.make_async_copy(ins[a], staged[a], local_sems.at[a, 0]),
                 pltpu.make_async_copy(staged[a], outs[a].at[_dev_index(*me)], local_sems.at[a, 1])) for a in every]
        first = [copy(a, 1 + j, me, (*chip, c), own=True) for j, chip in enumerate(chips) for a in every]
        first += [copy(a, 0, me, sibling, own=True) for a in every]
        passed = [[copy(a, 4 + j, (*chip, c), sibling) for a in every] for j, chip in enumerate(chips)]
        return me, sibling, chips, c, every, copy, mine, first, passed

    def start(self, ins, outs, scratch):
        *_, mine, first, _ = self._plan(ins, outs, scratch)
        for to_vmem, _ in mine:
            to_vmem.start()
        for cp in first:
            cp.start()

    def relay(self, ins, outs, scratch):
        me, sibling, chips, c, every, copy, mine, first, passed = self._plan(ins, outs, scratch)
        for to_vmem, to_slot in mine:
            to_vmem.wait()
            to_slot.start()
        for j, chip in enumerate(chips):
            for a in every:
                copy(a, 1 + j, (*chip, c), me).wait_recv()
            for cp in passed[j]:
                cp.start()

    def finish(self, ins, outs, scratch):
        me, sibling, chips, c, every, copy, mine, first, passed = self._plan(ins, outs, scratch)
        for a in every:
            copy(a, 0, sibling, me).wait_recv()
        for j, chip in enumerate(chips):
            for a in every:
                copy(a, 4 + j, (*chip, 1 - c), me).wait_recv()
        for cp in first + [cp for group in passed for cp in group]:
            cp.wait_send()
        for _, to_slot in mine:
            to_slot.wait()


N_CHIP = 4


class PairExchange:
    def __init__(self, by_core, whole=()):
        self.inputs = tuple(by_core) + tuple(whole)
        self.n_by_core = len(by_core)
        self.out_shape = ([jax.ShapeDtypeStruct(a.shape[1:], a.dtype) for a in by_core]
                          + [jax.ShapeDtypeStruct(a.shape, a.dtype) for a in whole])
        n = len(self.inputs)
        self.scratch = [pltpu.SemaphoreType.DMA((n,)), pltpu.SemaphoreType.DMA((n,))]

    def _copies(self, ins, outs, sems):
        x, y, c = _mesh_pos()
        srcs = [r.at[1 - c] if a < self.n_by_core else r for a, r in enumerate(ins)]
        return [_Transfer(srcs[a], outs[a], outs[a].shape[-2], sems[0].at[a], sems[1].at[a], (x, y, 1 - c))
                for a in range(len(ins))]

    def start(self, ins, outs, sems):
        for cp in self._copies(ins, outs, sems):
            cp.start()

    def finish(self, ins, outs, sems):
        copies = self._copies(ins, outs, sems)
        for cp in copies:
            cp.wait_recv()
        for cp in copies:
            cp.wait_send()


def pair_sum(own, other, dtype, rows, name, core, layer, depth, stacked=None):
    n, n_r, n_c = other.shape
    by_core = own.ndim == 4
    own = own if by_core else own[None]

    def body(core_ref, a_ref, b_ref, *refs):
        refs[-1][0, 0] = (a_ref[0, 0] + b_ref[0].astype(F32)).astype(dtype)

    carried = () if stacked is None else (stacked,)
    grid_spec = pltpu.PrefetchScalarGridSpec(
        num_scalar_prefetch=1,
        grid=(n, n_r // rows),
        in_specs=[pl.BlockSpec((1, 1, rows, n_c), lambda i, r, s: (s[0] if by_core else 0, i, r, 0)),
                  pl.BlockSpec((1, rows, n_c), lambda i, r, s: (i, r, 0))] + [_ANY] * len(carried),
        out_specs=pl.BlockSpec((1, 1, rows, n_c), lambda i, r, s: (i, layer, r, 0)),
    )
    return pl.pallas_call(
        body,
        name=name,
        grid_spec=grid_spec,
        out_shape=jax.ShapeDtypeStruct((n, depth, n_r, n_c), dtype),
        input_output_aliases={3: 0} if carried else {},
        compiler_params=_cparams("parallel", "parallel"),
    )(core, own, other, *carried)


def small_sum(a, b, name):
    def body(a_ref, b_ref, o_ref):
        o_ref[...] = a_ref[...] + b_ref[...]

    return pl.pallas_call(body, name=name, out_shape=jax.ShapeDtypeStruct(a.shape, a.dtype))(a, b)


class ChipExchange:
    def __init__(self, by_chip=(), layers=(), gathered=(), stacked=()):
        stacked = tuple(stacked) or (None,) * len(by_chip)
        kept = [s for s in stacked if s is not None]
        self.inputs = tuple(by_chip) + tuple(gathered) + tuple(kept)
        self.n_by_chip, self.n_gathered = len(by_chip), len(gathered)
        self.items = [(a, l) for a in range(len(by_chip)) for l in layers[a]] + [(self.n_by_chip + g, None) for g in range(len(gathered))]
        self.out_shape = ([jax.ShapeDtypeStruct((N_CHIP - 1,) + a.shape[1:], a.dtype) for a in by_chip]
                          + [jax.ShapeDtypeStruct((N_CHIP,) + a.shape, a.dtype) for a in gathered])
        at = iter(range(self.n_by_chip + self.n_gathered, len(self.inputs)))
        self.aliases = {next(at): a for a, s in enumerate(stacked) if s is not None}
        n = len(self.items)
        self.scratch = [pltpu.SemaphoreType.DMA((n, 3)), pltpu.SemaphoreType.DMA((n, 3)),
                        pltpu.SemaphoreType.DMA((max(self.n_gathered, 1),))]

    def _plan(self, ins, outs, sems):
        x, y, c = _mesh_pos()
        chip = 2 * x + y
        n = len(self.items)

        def copy(i, k, sending):
            a, layer = self.items[i]
            px, py = x ^ ((k >> 1) & 1), y ^ (k & 1)
            if layer is not None:
                src, dst = ins[a].at[2 * px + py, layer], outs[a].at[k - 1, layer]
            else:
                src, dst = ins[a], outs[a].at[chip if sending else 2 * px + py]
            return _Transfer(src, dst, dst.shape[-2], sems[0].at[i, k - 1], sems[1].at[i, k - 1], (px, py, c))

        local = [pltpu.make_async_copy(ins[a], outs[a].at[chip], sems[2].at[a - self.n_by_chip])
                 for a in range(self.n_by_chip, self.n_by_chip + self.n_gathered)]
        return n, copy, local

    def start(self, ins, outs, sems):
        n, copy, local = self._plan(ins, outs, sems)
        for cp in local:
            cp.start()
        for k in range(1, N_CHIP):
            for a in range(n):
                copy(a, k, True).start()

    def finish(self, ins, outs, sems):
        n, copy, local = self._plan(ins, outs, sems)
        for k in range(1, N_CHIP):
            for a in range(n):
                copy(a, k, False).wait_recv()
        for k in range(1, N_CHIP):
            for a in range(n):
                copy(a, k, True).wait_send()
        for cp in local:
            cp.wait()


ADAM_LR = 0.001
ADAM_B1 = 0.9
ADAM_B2 = 0.999
ADAM_EPS = 1e-08
ADAM_WD = 0.01
ADAM_STEP = 10


def adam_reduce(parts, w, m, v, rows, name, own=None, chip=None):
    n_l, n_r, n_c = w.shape
    n_parts = parts.shape[0]

    def body(*refs):
        p_ref, w_ref, m_ref, v_ref, g_ref, d_ref, m2_ref, v2_ref = refs[-8:]
        g = p_ref[0, 0].astype(F32)
        if own is not None:
            g = refs[-9][...].reshape(rows, n_c).astype(F32) + g
        for d in range(1, n_parts):
            g = g + p_ref[d, 0].astype(F32)
        m2 = ADAM_B1 * m_ref[0] + (1.0 - ADAM_B1) * g
        v2 = ADAM_B2 * v_ref[0] + (1.0 - ADAM_B2) * (g * g)
        m_hat = m2 / (1.0 - ADAM_B1 ** ADAM_STEP)
        v_hat = v2 / (1.0 - ADAM_B2 ** ADAM_STEP)
        g_ref[0] = g
        d_ref[0] = -ADAM_LR * (m_hat / (jnp.sqrt(v_hat) + ADAM_EPS) + ADAM_WD * w_ref[0])
        m2_ref[0] = m2
        v2_ref[0] = v2

    blk = lambda: pl.BlockSpec((1, rows, n_c), lambda l, r, *_: (l, r, 0))
    in_specs = [pl.BlockSpec((n_parts, 1, rows, n_c), lambda l, r, *_: (0, l, r, 0)), blk(), blk(), blk()]
    args = (parts, w, m, v)
    if own is not None:
        in_specs = [pl.BlockSpec((1, 1, rows, n_c), lambda l, r, s: (s[0], l, r, 0))] + in_specs
        args = (chip, own) + args
    grid_spec = pltpu.PrefetchScalarGridSpec(
        num_scalar_prefetch=0 if own is None else 1, grid=(n_l, n_r // rows), in_specs=in_specs,
        out_specs=[blk(), blk(), blk(), blk()])
    return pl.pallas_call(
        body,
        name=name,
        grid_spec=grid_spec,
        out_shape=[jax.ShapeDtypeStruct(w.shape, F32)] * 4,
        compiler_params=_cparams("parallel", "parallel"),
    )(*args)


def adam_reduce_columns(parts, w, m, v, name, own, chip):
    n_l, n_r, n_c = w.shape
    n_parts = parts.shape[0]
    view = lambda a: jnp.transpose(a, (2, 0, 1))

    def body(_, own_ref, p_ref, w_ref, m_ref, v_ref, g_ref, d_ref, m2_ref, v2_ref):
        for l in range(n_l):
            g = own_ref[0, l].astype(F32) + p_ref[0, l].astype(F32)
            for d in range(1, n_parts):
                g = g + p_ref[d, l].astype(F32)
            g = g.T
            w_l, m_l, v_l = w_ref[:, l, :], m_ref[:, l, :], v_ref[:, l, :]
            m2 = ADAM_B1 * m_l + (1.0 - ADAM_B1) * g
            v2 = ADAM_B2 * v_l + (1.0 - ADAM_B2) * (g * g)
            m_hat = m2 / (1.0 - ADAM_B1 ** ADAM_STEP)
            v_hat = v2 / (1.0 - ADAM_B2 ** ADAM_STEP)
            g_ref[:, l, :] = g
            d_ref[:, l, :] = -ADAM_LR * (m_hat / (jnp.sqrt(v_hat) + ADAM_EPS) + ADAM_WD * w_l)
            m2_ref[:, l, :] = m2
            v2_ref[:, l, :] = v2

    blk = lambda: pl.BlockSpec((LANES, n_l, n_r), lambda c, s: (c, 0, 0))
    grid_spec = pltpu.PrefetchScalarGridSpec(
        num_scalar_prefetch=1, grid=(pl.cdiv(n_c, LANES),),
        in_specs=[pl.BlockSpec((1, n_l, n_r, LANES), lambda c, s: (s[0], 0, 0, c)),
                  pl.BlockSpec((n_parts, n_l, n_r, LANES), lambda c, s: (0, 0, 0, c)), blk(), blk(), blk()],
        out_specs=[blk(), blk(), blk(), blk()])
    outs = pl.pallas_call(
        body,
        name=name,
        grid_spec=grid_spec,
        out_shape=[jax.ShapeDtypeStruct((n_c, n_l, n_r), F32)] * 4,
        compiler_params=_cparams("parallel"),
    )(chip, own, parts, view(w), view(m), view(v))
    return [jnp.transpose(o, (1, 2, 0)) for o in outs]


_SMALL = (("norm_g", (2, 1024)), ("gmlp_ln_g", (2, 4, 64)), ("gmlp_ln_b", (2, 4, 64)),
          ("gmlp_b_s", (2, 4, 128)), ("hgrn_lb", (2, 256)), ("hgrn_onorm_g", (2, 64)), ("fox_b_f", (2, 8)),
          ("final_norm_g", (1024,)), ("loss", ()))


def _padded(n):
    return -(-n // LANES) * LANES


_SMALL_ROWS = -(-sum(_padded(int(np.prod(s))) for _, s in _SMALL) // LANES // 8) * 8


def _pack_small(vals):
    flat = []
    for (name, shape), a in zip(_SMALL, vals, strict=True):
        n = int(np.prod(shape))
        flat.append(jnp.pad(a.reshape(n).astype(F32), (0, _padded(n) - n)))
    flat = jnp.concatenate(flat)
    return jnp.pad(flat, (0, _SMALL_ROWS * LANES - flat.shape[0])).reshape(_SMALL_ROWS, LANES)


def _unpack_small(slab):
    flat, out, at = slab.reshape(-1), {}, 0
    for name, shape in _SMALL:
        n = int(np.prod(shape))
        out[name] = flat[at:at + n].reshape(shape)
        at += _padded(n)
    return out


def sum_parts(parts, name):
    def body(p_ref, o_ref):
        g = p_ref[0]
        for d in range(1, parts.shape[0]):
            g = g + p_ref[d]
        o_ref[...] = g

    return pl.pallas_call(body, name=name, out_shape=jax.ShapeDtypeStruct(parts.shape[1:], F32))(parts)


def adam_small(gs, ws, ms, vs):
    n = len(gs)

    def body(*refs):
        for k in range(n):
            g, w, m, v = (refs[j * n + k][...] for j in range(4))
            m2 = ADAM_B1 * m + (1.0 - ADAM_B1) * g
            v2 = ADAM_B2 * v + (1.0 - ADAM_B2) * (g * g)
            m_hat = m2 / (1.0 - ADAM_B1 ** ADAM_STEP)
            v_hat = v2 / (1.0 - ADAM_B2 ** ADAM_STEP)
            refs[4 * n + k][...] = -ADAM_LR * (m_hat / (jnp.sqrt(v_hat) + ADAM_EPS) + ADAM_WD * w)
            refs[5 * n + k][...] = m2
            refs[6 * n + k][...] = v2

    outs = pl.pallas_call(body, name="adam_small",
                          out_shape=[jax.ShapeDtypeStruct(w.shape, F32) for _ in range(3) for w in ws])(*gs, *ws, *ms, *vs)
    return outs[:n], outs[n:2 * n], outs[2 * n:]


def kernel(x, norm_g, w_in, w_out, gmlp_ln_g, gmlp_ln_b, gmlp_w_s, gmlp_b_s, hgrn_lb, hgrn_onorm_g, fox_b_f, final_norm_g, loss_target, m_norm_g, m_w_in, m_w_out, m_gmlp_ln_g, m_gmlp_ln_b, m_gmlp_w_s, m_gmlp_b_s, m_hgrn_lb, m_hgrn_onorm_g, m_fox_b_f, m_final_norm_g, v_norm_g, v_w_in, v_w_out, v_gmlp_ln_g, v_gmlp_ln_b, v_gmlp_w_s, v_gmlp_b_s, v_hgrn_lb, v_hgrn_onorm_g, v_fox_b_f, v_final_norm_g):
    depth = w_in.shape[0]
    seq = x.shape[1]
    assert w_in.shape[2] * N_DEV == N_IN
    xs, tgt = x[0], loss_target[0]

    wi_blk, wo_blk = w_in.astype(BF16), w_out.astype(BF16)
    (wi_all,) = _exchange_call(AllGatherWeights([wi_blk[0]]), "allgather_weights_0")

    ln_g = gmlp_ln_g.reshape(depth, 1, A_WIDTH)
    ln_b = gmlp_ln_b.reshape(depth, 1, A_WIDTH)
    bs_t = jnp.pad(jnp.transpose(gmlp_b_s, (0, 2, 1)), ((0, 0), (0, 0), (0, LANES - A_GROUPS)))
    lb0, lb1 = hgrn_lb[0:1], hgrn_lb[1:2]
    onorm = jnp.tile(hgrn_onorm_g, (1, B_HEADS)).reshape(depth, 1, B_WIDTH)
    bf_row = jnp.pad(fox_b_f, ((0, 0), (0, LANES - C_HEADS))).reshape(depth, 1, LANES)

    core = lax.axis_index("c").astype(jnp.int32).reshape(1)
    chip = (2 * lax.axis_index("x") + lax.axis_index("y")).astype(jnp.int32).reshape(1)

    saved = []
    xc = xs
    for l in range(depth):
        wi_int = assemble_w_in(wi_all[:, None])
        proj, qkv, h = inproj(xc, norm_g[l:l + 1], wi_int, 0)
        ya = gmlp_fwd(proj, ln_g[l], ln_b[l], gmlp_w_s[l], bs_t[l])
        yb, states = hgrn_fwd(proj, lb0, lb1, onorm[l], l)
        ka, va, vt, kt, qt, qa = fox_prep(proj, qkv, bf_row[l])
        ride = ([wo_blk] if l == 0 else []) + ([wi_blk[l + 1]] if l + 1 < depth else [])
        o, lse, *gathered = fox_fwd(qt, ka, vt, AllGatherWeights(ride) if ride else None)
        if l == 0:
            wo_all = gathered.pop(0)
        if gathered:
            (wi_all,) = gathered
        x_in = xc
        if l + 1 < depth:
            xc, yfull = outproj(x_in, ya, yb, o, proj, wo_all, l)
        else:
            dx, yfull, d_final_g, loss_tile = outproj(x_in, ya, yb, o, proj, wo_all, l, (final_norm_g[None], tgt))
        saved.append((x_in, proj, h, states, ka, va, kt, qt, qa, o, lse, yfull, wi_int))

    n_shard = w_in.shape[2]
    g_norm = [None] * depth
    g_ln_g, g_ln_b, g_ws, g_bs, g_on, g_bf = ([None] * depth for _ in range(6))
    g_lb0, g_lb1 = jnp.zeros_like(lb0), jnp.zeros_like(lb1)
    swi = swo = rwi = rwo = None
    for l in reversed(range(depth)):
        x_in, proj, h, states, ka, va, kt, qt, qa, o, lse, yfull, wi_int = saved[l]
        dy, gwo = outproj_bwd(dx, yfull, wo_all, l)
        dproj, g_ln_g[l], g_ln_b[l], g_ws[l], dbs_t = gmlp_bwd(proj, dy, ln_g[l], ln_b[l], gmlp_w_s[l], bs_t[l])
        g_bs[l] = dbs_t[:, :A_GROUPS].T
        if l > 0:
            (qwo,) = _exchange_call(PairExchange([gwo]), f"pair_exchange_w_out_{l}")
        else:
            gws = jnp.stack(g_ws).reshape(-1, LANES)
            qwo, qws = _exchange_call(PairExchange([gwo], [gws]), f"pair_exchange_w_out_{l}")
            sws = small_sum(gws, qws, "pair_sum_w_s")
        swo = pair_sum(gwo, qwo, BF16, gwo.shape[2], "pair_sum_w_out", core, l, depth, swo)
        dproj, d0, d1, don = hgrn_bwd(proj, states, dy, lb0, lb1, onorm[l], l, dproj)
        g_lb0, g_lb1 = g_lb0 + d0, g_lb1 + d1
        g_on[l] = don.reshape(B_HEADS, B_KDIM).sum(0)
        dob, dproj, dot_t = fox_bwd_prep(dy, o, proj, dproj)
        top = l == depth - 1
        ride = ChipExchange([swo] if top else [swi, swo], [(l,)] if top else [(l + 1,), (l,)],
                            [sws] if l == 0 else [], [rwo] if top else [rwi, rwo])
        outs = fox_bwd(ka, va, kt, qt, dot_t, qa, dob, lse, ride)
        dqkv, (dck, dcq), got = outs[:3], outs[3:5], list(outs[5:])
        if not top:
            rwi = got.pop(0)
        rwo = got.pop(0)
        if l == 0:
            (rws,) = got
        dproj, dbf = fox_post(dcq, dck, proj, bf_row[l], dproj)
        g_bf[l] = dbf[0, :C_HEADS]
        gwi, for_sibling = split_w_in_grad(inproj_bwd_w(h, dproj, dqkv), n_shard, core)
        if l > 0:
            dx, g_norm[l], qwi = inproj_bwd_x(dproj, dqkv, wi_int, x_in, norm_g[l:l + 1], dx, 0, PairExchange([], [for_sibling]))
            swi = pair_sum(gwi, qwi, BF16, gwi.shape[1], "pair_sum_w_in", core, l, depth, swi)
        else:
            (qwi,) = _exchange_call(PairExchange([], [for_sibling]), f"pair_exchange_w_in_{l}")
            swi = pair_sum(gwi, qwi, BF16, gwi.shape[1], "pair_sum_w_in", core, l, depth, swi)
            dx, g_norm[l], rwi = inproj_bwd_x(dproj, dqkv, wi_int, x_in, norm_g[l:l + 1], dx, 0,
                                              ChipExchange([swi], [(l,)], stacked=[rwi]))

    gsm = _pack_small([
        jnp.concatenate(g_norm), jnp.stack(g_ln_g), jnp.stack(g_ln_b), jnp.stack(g_bs),
        jnp.concatenate([g_lb0, g_lb1]), jnp.stack(g_on), jnp.stack(g_bf), d_final_g, loss_tile[0, 0]])
    (qsm,) = _exchange_call(PairExchange([], [gsm]), "pair_exchange_small")
    ssm = small_sum(gsm, qsm, "pair_sum_small")
    (rsm,) = _exchange_call(ChipExchange(gathered=[ssm]), "chip_exchange_small")

    small_w = (norm_g, gmlp_ln_g, gmlp_ln_b, gmlp_b_s, hgrn_lb, hgrn_onorm_g, fox_b_f, final_norm_g)
    small_m = (m_norm_g, m_gmlp_ln_g, m_gmlp_ln_b, m_gmlp_b_s, m_hgrn_lb, m_hgrn_onorm_g, m_fox_b_f, m_final_norm_g)
    small_v = (v_norm_g, v_gmlp_ln_g, v_gmlp_ln_b, v_gmlp_b_s, v_hgrn_lb, v_hgrn_onorm_g, v_fox_b_f, v_final_norm_g)
    res_wi = adam_reduce_columns(rwi, w_in, m_w_in, v_w_in, "adam_w_in", swi, chip)
    res_wo = adam_reduce(rwo, w_out, m_w_out, v_w_out, w_out.shape[1], "adam_w_out", own=swo, chip=chip)
    grads = _unpack_small(sum_parts(rsm, "sum_small"))
    names = [name for name, _ in _SMALL if name != "loss"]
    rows = lambda a: a.reshape(1, -1) if a.ndim == 1 else a
    res_sm = adam_small([rows(grads[k]) for k in names], *([rows(a) for a in wmv] for wmv in (small_w, small_m, small_v)))
    res_sm = [grads] + [{k: a.reshape(grads[k].shape) for k, a in zip(names, r, strict=True)} for r in res_sm]
    as_rows = lambda a: a.reshape(1, -1, LANES)
    res_ws = adam_reduce(rws[:, None], as_rows(gmlp_w_s), as_rows(m_gmlp_w_s), as_rows(v_gmlp_w_s), rws.shape[1], "adam_w_s")
    for s, r in zip(res_sm, res_ws, strict=True):
        s["gmlp_w_s"] = r.reshape(gmlp_w_s.shape)

    def group(i):
        s = res_sm[i]
        return [s["norm_g"], res_wi[i], res_wo[i], s["gmlp_ln_g"], s["gmlp_ln_b"], s["gmlp_w_s"], s["gmlp_b_s"],
                s["hgrn_lb"], s["hgrn_onorm_g"], s["fox_b_f"], s["final_norm_g"]]

    return (res_sm[0]["loss"], dx[None], *group(0), *group(1), *group(2), *group(3))
```
